```python
import jax, jax.numpy as jnp
from jax import lax
import numpy as np

D_MODEL = 1024
BATCH = 16
SEQ = 2048
DEPTH = 1

HEAD_DIM = 64
NA_HEADS = 8
NA_WIDTH = NA_HEADS * HEAD_DIM
GRID_W = 64
NA_ROWS_MAX = 8
NA_COLS = 16
SW_HEADS = 8
SW_KV_HEADS = 2
SW_GROUP = SW_HEADS // SW_KV_HEADS
SW_WIDTH = SW_HEADS * HEAD_DIM
SW_KV_WIDTH = SW_KV_HEADS * HEAD_DIM
SW_WINDOW = 128
SW_BLOCK = 128
MIX_WIDTH = NA_WIDTH + SW_WIDTH
IN_WIDTH = 3 * NA_WIDTH + SW_WIDTH + 2 * SW_KV_WIDTH
D_FF = 2816
CONV_W = 3
ROPE_THETA = 10000.0
EPS = 1e-6
NEG = -1e30

kernel_name = "hybrid_natten_swa_convffn_block"


def rmsnorm(x, g):
    xf = x.astype(jnp.float32)
    y = xf * lax.rsqrt(jnp.mean(xf * xf, axis=-1, keepdims=True) + EPS)
    return (y * g.astype(jnp.float32)).astype(x.dtype)


def rope(t, pos):
    half = HEAD_DIM // 2
    inv = ROPE_THETA ** (-jnp.arange(half, dtype=jnp.float32) / half)
    ang = pos.astype(jnp.float32)[:, None] * inv[None, :]
    cos = jnp.cos(ang)[None, :, None, :].astype(t.dtype)
    sin = jnp.sin(ang)[None, :, None, :].astype(t.dtype)
    t1, t2 = t[..., :half], t[..., half:]
    return jnp.concatenate([t1 * cos - t2 * sin, t2 * cos + t1 * sin], axis=-1)


def neighbourhood_attention(q, k, v, rpb):
    B, S = q.shape[0], q.shape[1]
    rows = S // GRID_W
    wr = min(NA_ROWS_MAX, rows)

    def grid(t):
        return t.reshape(B, rows, GRID_W, NA_HEADS, HEAD_DIM).transpose(0, 3, 1, 2, 4)

    r = jnp.arange(rows)
    rs = jnp.clip(r - wr // 2, 0, rows - wr)
    row_idx = rs[:, None] + jnp.arange(wr)[None, :]
    kb = jnp.take(grid(k), row_idx, axis=2).reshape(B, NA_HEADS, rows, wr * GRID_W, HEAD_DIM)
    vb = jnp.take(grid(v), row_idx, axis=2).reshape(B, NA_HEADS, rows, wr * GRID_W, HEAD_DIM)

    col = jnp.arange(GRID_W)
    cs = jnp.clip(col - NA_COLS // 2, 0, GRID_W - NA_COLS)
    col_ok = (col[None, :] >= cs[:, None]) & (col[None, :] < cs[:, None] + NA_COLS)
    dc = jnp.clip(col[None, :] - col[:, None] + NA_COLS - 1, 0, 2 * NA_COLS - 2)
    dr = row_idx - r[:, None] + NA_ROWS_MAX - 1
    bias = rpb[:, dr[:, None, :, None], dc[None, :, None, :]]
    bias = jnp.where(col_ok[None, None, :, None, :], bias.astype(jnp.float32), NEG)
    bias = bias.reshape(NA_HEADS, rows, GRID_W, wr * GRID_W)

    s = jnp.einsum('bhrqd,bhrkd->bhrqk', grid(q), kb,
                   preferred_element_type=jnp.float32) * (HEAD_DIM ** -0.5) + bias[None]
    p = jax.nn.softmax(s, axis=-1).astype(v.dtype)
    o = jnp.einsum('bhrqk,bhrkd->bhrqd', p, vb)
    return o.transpose(0, 2, 3, 1, 4).reshape(B, S, NA_WIDTH)


def window_sink_attention(q, k, v, sink):
    B, S = q.shape[0], q.shape[1]
    nb = S // SW_BLOCK
    qb = q.reshape(B, nb, SW_BLOCK, SW_KV_HEADS, SW_GROUP, HEAD_DIM)

    def band(t):
        tp = jnp.pad(t, ((0, 0), (SW_BLOCK, SW_BLOCK), (0, 0), (0, 0)))
        tp = tp.reshape(B, nb + 2, SW_BLOCK, SW_KV_HEADS, HEAD_DIM)
        return jnp.concatenate([tp[:, :-2], tp[:, 1:-1], tp[:, 2:]], axis=2)

    kw, vw = band(k), band(v)
    blk = jnp.arange(nb)[:, None] * SW_BLOCK
    qpos = blk + jnp.arange(SW_BLOCK)[None, :]
    kpos = blk - SW_BLOCK + jnp.arange(3 * SW_BLOCK)[None, :]
    ok = (jnp.abs(qpos[:, :, None] - kpos[:, None, :]) <= SW_WINDOW) \
        & ((kpos >= 0) & (kpos < S))[:, None, :]

    s = jnp.einsum('bnqhgd,bnkhd->bhgnqk', qb, kw,
                   preferred_element_type=jnp.float32) * (HEAD_DIM ** -0.5)
    s = jnp.where(ok[None, None, None], s, NEG)
    sk = sink.astype(jnp.float32).reshape(1, SW_KV_HEADS, SW_GROUP, 1, 1)
    m = jnp.maximum(jnp.max(s, axis=-1), sk)
    p = jnp.exp(s - m[..., None])
    den = jnp.sum(p, axis=-1) + jnp.exp(sk - m)
    p = (p / den[..., None]).astype(v.dtype)
    o = jnp.einsum('bhgnqk,bnkhd->bnqhgd', p, vw)
    return o.reshape(B, S, SW_WIDTH)


def _fwd_setup_inputs(seed: int = 0) -> dict:
    key = jax.random.key(seed)
    ks = jax.random.split(key, 20)
    L, D = DEPTH, D_MODEL

    def nrm(k, shape, scale):
        return jax.random.normal(k, shape, jnp.float32) * scale

    return {
        "x": nrm(ks[0], (BATCH, SEQ, D), 1.0),
        "c": nrm(ks[1], (BATCH, D), 1.0),
        "w_ada": nrm(ks[2], (L, D, 6 * D), 0.5 * D ** -0.5),
        "b_ada": nrm(ks[3], (L, 6 * D), 0.02),
        "g_attn": 1.0 + nrm(ks[4], (L, D), 0.02),
        "w_in": nrm(ks[5], (L, D, IN_WIDTH), D ** -0.5),
        "na_rpb": nrm(ks[6], (L, NA_HEADS, 2 * NA_ROWS_MAX - 1, 2 * NA_COLS - 1), 0.1),
        "sw_sink": nrm(ks[7], (L, SW_HEADS), 0.5),
        "g_na_out": 1.0 + nrm(ks[8], (L, NA_WIDTH), 0.02),
        "g_sw_out": 1.0 + nrm(ks[9], (L, SW_WIDTH), 0.02),
        "w_out": nrm(ks[10], (L, MIX_WIDTH, D), MIX_WIDTH ** -0.5),
        "g_ffn": 1.0 + nrm(ks[11], (L, D), 0.02),
        "w_up": nrm(ks[12], (L, D, 2 * D_FF), D ** -0.5),
        "conv_w": nrm(ks[13], (L, CONV_W, D_FF), CONV_W ** -0.5),
        "conv_b": nrm(ks[14], (L, D_FF), 0.02),
        "w_down": nrm(ks[15], (L, D_FF, D), D_FF ** -0.5),
        "g_final": 1.0 + nrm(ks[16], (D,), 0.02),
    }


def _fwd_reference(x, c, w_ada, b_ada, g_attn, w_in, na_rpb, sw_sink, g_na_out, g_sw_out,
              w_out, g_ffn, w_up, conv_w, conv_b, w_down, g_final):
    S = x.shape[1]
    pos = jnp.arange(S)
    splits = [NA_WIDTH, 2 * NA_WIDTH, 3 * NA_WIDTH, 3 * NA_WIDTH + SW_WIDTH,
              3 * NA_WIDTH + SW_WIDTH + SW_KV_WIDTH]
    for l in range(DEPTH):
        mod = jax.nn.silu(c) @ w_ada[l] + b_ada[l]
        shift_a, scale_a, gate_a, shift_f, scale_f, gate_f = [
            m[:, None, :] for m in jnp.split(mod, 6, axis=-1)]

        h = rmsnorm(x, g_attn[l]) * (1.0 + scale_a) + shift_a
        proj = h @ w_in[l]
        qa, ka, va, qb, kb, vb = jnp.split(proj, splits, axis=-1)
        Bn = x.shape[0]
        qa = qa.reshape(Bn, S, NA_HEADS, HEAD_DIM)
        ka = ka.reshape(Bn, S, NA_HEADS, HEAD_DIM)
        va = va.reshape(Bn, S, NA_HEADS, HEAD_DIM)
        qb = rope(qb.reshape(Bn, S, SW_HEADS, HEAD_DIM), pos)
        kb = rope(kb.reshape(Bn, S, SW_KV_HEADS, HEAD_DIM), pos)
        vb = vb.reshape(Bn, S, SW_KV_HEADS, HEAD_DIM)

        o_a = rmsnorm(neighbourhood_attention(qa, ka, va, na_rpb[l]), g_na_out[l])
        o_b = rmsnorm(window_sink_attention(qb, kb, vb, sw_sink[l]), g_sw_out[l])
        mix = jnp.concatenate([o_a, o_b], axis=-1) @ w_out[l]
        x = x + gate_a * mix

        h = rmsnorm(x, g_ffn[l]) * (1.0 + scale_f) + shift_f
        val, gt = jnp.split(h @ w_up[l], 2, axis=-1)
        gp = jnp.pad(gt, ((0, 0), (1, 1), (0, 0)))
        cw = conv_w[l]
        gc = gp[:, :-2] * cw[0] + gp[:, 1:-1] * cw[1] + gp[:, 2:] * cw[2] + conv_b[l]
        x = x + gate_f * ((jax.nn.silu(gc) * val) @ w_down[l])
    return rmsnorm(x, g_final)


import jax as _jax
import jax.numpy as _jnp

TWIN_FORMAT = 'train_step'
FWD_PARAMS = ['x', 'c', 'w_ada', 'b_ada', 'g_attn', 'w_in', 'na_rpb', 'sw_sink', 'g_na_out', 'g_sw_out', 'w_out', 'g_ffn', 'w_up', 'conv_w', 'conv_b', 'w_down', 'g_final']
TWIN_WEIGHTS = ['w_ada', 'b_ada', 'g_attn', 'w_in', 'na_rpb', 'sw_sink', 'g_na_out', 'g_sw_out', 'w_out', 'g_ffn', 'w_up', 'conv_w', 'conv_b', 'w_down', 'g_final']
TWIN_DIFF_INPUT = 'x'
TWIN_INPUTS = ['x', 'c', 'w_ada', 'b_ada', 'g_attn', 'w_in', 'na_rpb', 'sw_sink', 'g_na_out', 'g_sw_out', 'w_out', 'g_ffn', 'w_up', 'conv_w', 'conv_b', 'w_down', 'g_final', 'loss_target', 'm_w_ada', 'm_b_ada', 'm_g_attn', 'm_w_in', 'm_na_rpb', 'm_sw_sink', 'm_g_na_out', 'm_g_sw_out', 'm_w_out', 'm_g_ffn', 'm_w_up', 'm_conv_w', 'm_conv_b', 'm_w_down', 'm_g_final', 'v_w_ada', 'v_b_ada', 'v_g_attn', 'v_w_in', 'v_na_rpb', 'v_sw_sink', 'v_g_na_out', 'v_g_sw_out', 'v_w_out', 'v_g_ffn', 'v_w_up', 'v_conv_w', 'v_conv_b', 'v_w_down', 'v_g_final']
TWIN_OUTPUTS = ['loss', 'grad_x', 'grad_w_ada', 'grad_b_ada', 'grad_g_attn', 'grad_w_in', 'grad_na_rpb', 'grad_sw_sink', 'grad_g_na_out', 'grad_g_sw_out', 'grad_w_out', 'grad_g_ffn', 'grad_w_up', 'grad_conv_w', 'grad_conv_b', 'grad_w_down', 'grad_g_final', 'delta_w_ada', 'delta_b_ada', 'delta_g_attn', 'delta_w_in', 'delta_na_rpb', 'delta_sw_sink', 'delta_g_na_out', 'delta_g_sw_out', 'delta_w_out', 'delta_g_ffn', 'delta_w_up', 'delta_conv_w', 'delta_conv_b', 'delta_w_down', 'delta_g_final', 'new_m_w_ada', 'new_m_b_ada', 'new_m_g_attn', 'new_m_w_in', 'new_m_na_rpb', 'new_m_sw_sink', 'new_m_g_na_out', 'new_m_g_sw_out', 'new_m_w_out', 'new_m_g_ffn', 'new_m_w_up', 'new_m_conv_w', 'new_m_conv_b', 'new_m_w_down', 'new_m_g_final', 'new_v_w_ada', 'new_v_b_ada', 'new_v_g_attn', 'new_v_w_in', 'new_v_na_rpb', 'new_v_sw_sink', 'new_v_g_na_out', 'new_v_g_sw_out', 'new_v_w_out', 'new_v_g_ffn', 'new_v_w_up', 'new_v_conv_w', 'new_v_conv_b', 'new_v_w_down', 'new_v_g_final']
TWIN_LEAF_KINDS = {'loss': 'loss', 'grad_x': 'grad_x', 'grad_w_ada': 'grad_w', 'grad_b_ada': 'grad_w', 'grad_g_attn': 'grad_w', 'grad_w_in': 'grad_w', 'grad_na_rpb': 'grad_w', 'grad_sw_sink': 'grad_w', 'grad_g_na_out': 'grad_w', 'grad_g_sw_out': 'grad_w', 'grad_w_out': 'grad_w', 'grad_g_ffn': 'grad_w', 'grad_w_up': 'grad_w', 'grad_conv_w': 'grad_w', 'grad_conv_b': 'grad_w', 'grad_w_down': 'grad_w', 'grad_g_final': 'grad_w', 'delta_w_ada': 'delta_w', 'delta_b_ada': 'delta_w', 'delta_g_attn': 'delta_w', 'delta_w_in': 'delta_w', 'delta_na_rpb': 'delta_w', 'delta_sw_sink': 'delta_w', 'delta_g_na_out': 'delta_w', 'delta_g_sw_out': 'delta_w', 'delta_w_out': 'delta_w', 'delta_g_ffn': 'delta_w', 'delta_w_up': 'delta_w', 'delta_conv_w': 'delta_w', 'delta_conv_b': 'delta_w', 'delta_w_down': 'delta_w', 'delta_g_final': 'delta_w', 'new_m_w_ada': 'new_m', 'new_m_b_ada': 'new_m', 'new_m_g_attn': 'new_m', 'new_m_w_in': 'new_m', 'new_m_na_rpb': 'new_m', 'new_m_sw_sink': 'new_m', 'new_m_g_na_out': 'new_m', 'new_m_g_sw_out': 'new_m', 'new_m_w_out': 'new_m', 'new_m_g_ffn': 'new_m', 'new_m_w_up': 'new_m', 'new_m_conv_w': 'new_m', 'new_m_conv_b': 'new_m', 'new_m_w_down': 'new_m', 'new_m_g_final': 'new_m', 'new_v_w_ada': 'new_v', 'new_v_b_ada': 'new_v', 'new_v_g_attn': 'new_v', 'new_v_w_in': 'new_v', 'new_v_na_rpb': 'new_v', 'new_v_sw_sink': 'new_v', 'new_v_g_na_out': 'new_v', 'new_v_g_sw_out': 'new_v', 'new_v_w_out': 'new_v', 'new_v_g_ffn': 'new_v', 'new_v_w_up': 'new_v', 'new_v_conv_w': 'new_v', 'new_v_conv_b': 'new_v', 'new_v_w_down': 'new_v', 'new_v_g_final': 'new_v'}


def _forward(args):
    return _fwd_reference(*[args[k] for k in FWD_PARAMS])


def _output_shape():
    out = _jax.eval_shape(lambda: _forward(_fwd_setup_inputs(0)))
    return out.shape, out.dtype

N_MICROBATCH = 1
ADAM_LR = 0.001
ADAM_B1 = 0.9
ADAM_B2 = 0.999
ADAM_EPS = 1e-08
ADAM_WD = 0.01
ADAM_STEP = 10
PER_EXAMPLE_BATCH_AXIS = {'x': 0, 'c': 0, 'loss_target': 0}
SHARED_INPUTS = []
_WEIGHT_DTYPES = {'w_ada': _jnp.float32, 'b_ada': _jnp.float32, 'g_attn': _jnp.float32, 'w_in': _jnp.float32, 'na_rpb': _jnp.float32, 'sw_sink': _jnp.float32, 'g_na_out': _jnp.float32, 'g_sw_out': _jnp.float32, 'w_out': _jnp.float32, 'g_ffn': _jnp.float32, 'w_up': _jnp.float32, 'conv_w': _jnp.float32, 'conv_b': _jnp.float32, 'w_down': _jnp.float32, 'g_final': _jnp.float32}
MOMENT_SCALE = {'w_ada': 8.795312e-02, 'b_ada': 1.478513e-01, 'g_attn': 5.158392e-02, 'w_in': 4.677636e-02, 'na_rpb': 1.046542e-02, 'sw_sink': 1.432682e-03, 'g_na_out': 6.604983e-02, 'g_sw_out': 7.979177e-02, 'w_out': 6.694116e-02, 'g_ffn': 5.415396e-02, 'w_up': 2.343721e-02, 'conv_w': 2.359244e-02, 'conv_b': 2.048152e-02, 'w_down': 3.812768e-02, 'g_final': 3.229616e+01}


def _to_microbatches(a, axis):
    t = _jnp.moveaxis(a, axis, 0)
    t = t.reshape((N_MICROBATCH, t.shape[0] // N_MICROBATCH) + t.shape[1:])
    return _jnp.moveaxis(t, 1, axis + 1)


def setup_inputs(seed: int = 0) -> dict:
    inp = _fwd_setup_inputs(seed)
    key = _jax.random.fold_in(_jax.random.key(seed), 7919)
    shape, _ = _output_shape()
    out = dict(inp)
    out["loss_target"] = _jax.random.normal(_jax.random.fold_in(key, 0), shape, _jnp.float32)
    for i, name in enumerate(TWIN_WEIGHTS):
        w = inp[name].astype(_jnp.float32)
        if MOMENT_SCALE is None:
            s = _jnp.sqrt(_jnp.mean(_jnp.square(w)) + 1e-30)
        else:
            s = MOMENT_SCALE[name]
        km, kv = _jax.random.split(_jax.random.fold_in(key, i + 1))
        out[name] = w
        out["m_" + name] = s * _jax.random.normal(km, w.shape, _jnp.float32)
        out["v_" + name] = (s * s) * _jax.random.uniform(kv, w.shape, _jnp.float32, 0.5, 1.5)
    if N_MICROBATCH > 1:
        for name, axis in PER_EXAMPLE_BATCH_AXIS.items():
            out[name] = _to_microbatches(out[name], axis)
    return {'x': out['x'], 'c': out['c'], 'w_ada': out['w_ada'], 'b_ada': out['b_ada'], 'g_attn': out['g_attn'], 'w_in': out['w_in'], 'na_rpb': out['na_rpb'], 'sw_sink': out['sw_sink'], 'g_na_out': out['g_na_out'], 'g_sw_out': out['g_sw_out'], 'w_out': out['w_out'], 'g_ffn': out['g_ffn'], 'w_up': out['w_up'], 'conv_w': out['conv_w'], 'conv_b': out['conv_b'], 'w_down': out['w_down'], 'g_final': out['g_final'], 'loss_target': out['loss_target'], 'm_w_ada': out['m_w_ada'], 'm_b_ada': out['m_b_ada'], 'm_g_attn': out['m_g_attn'], 'm_w_in': out['m_w_in'], 'm_na_rpb': out['m_na_rpb'], 'm_sw_sink': out['m_sw_sink'], 'm_g_na_out': out['m_g_na_out'], 'm_g_sw_out': out['m_g_sw_out'], 'm_w_out': out['m_w_out'], 'm_g_ffn': out['m_g_ffn'], 'm_w_up': out['m_w_up'], 'm_conv_w': out['m_conv_w'], 'm_conv_b': out['m_conv_b'], 'm_w_down': out['m_w_down'], 'm_g_final': out['m_g_final'], 'v_w_ada': out['v_w_ada'], 'v_b_ada': out['v_b_ada'], 'v_g_attn': out['v_g_attn'], 'v_w_in': out['v_w_in'], 'v_na_rpb': out['v_na_rpb'], 'v_sw_sink': out['v_sw_sink'], 'v_g_na_out': out['v_g_na_out'], 'v_g_sw_out': out['v_g_sw_out'], 'v_w_out': out['v_w_out'], 'v_g_ffn': out['v_g_ffn'], 'v_w_up': out['v_w_up'], 'v_conv_w': out['v_conv_w'], 'v_conv_b': out['v_conv_b'], 'v_w_down': out['v_w_down'], 'v_g_final': out['v_g_final']}


def _loss(weights, diff, rest, loss_target):
    with _jax.named_scope("forward"):
        args = {**rest, TWIN_DIFF_INPUT: diff, **{k: w.astype(_WEIGHT_DTYPES[k]) for k, w in weights.items()}}
        y = _forward(args)
    with _jax.named_scope("loss_head"):
        err = _jnp.square(y.astype(_jnp.float32) - loss_target)
        return 0.5 * _jnp.sum(_jnp.mean(err, axis=-1)) if err.ndim else 0.5 * err


def _adamw(w, g, m, v):
    m = ADAM_B1 * m + (1.0 - ADAM_B1) * g
    v = ADAM_B2 * v + (1.0 - ADAM_B2) * _jnp.square(g)
    m_hat = m / (1.0 - ADAM_B1 ** ADAM_STEP)
    v_hat = v / (1.0 - ADAM_B2 ** ADAM_STEP)
    delta = -ADAM_LR * (m_hat / (_jnp.sqrt(v_hat) + ADAM_EPS) + ADAM_WD * w)
    return delta, m, v


def reference(x, c, w_ada, b_ada, g_attn, w_in, na_rpb, sw_sink, g_na_out, g_sw_out, w_out, g_ffn, w_up, conv_w, conv_b, w_down, g_final, loss_target, m_w_ada, m_b_ada, m_g_attn, m_w_in, m_na_rpb, m_sw_sink, m_g_na_out, m_g_sw_out, m_w_out, m_g_ffn, m_w_up, m_conv_w, m_conv_b, m_w_down, m_g_final, v_w_ada, v_b_ada, v_g_attn, v_w_in, v_na_rpb, v_sw_sink, v_g_na_out, v_g_sw_out, v_w_out, v_g_ffn, v_w_up, v_conv_w, v_conv_b, v_w_down, v_g_final):
    given = dict(x=x, c=c, w_ada=w_ada, b_ada=b_ada, g_attn=g_attn, w_in=w_in, na_rpb=na_rpb, sw_sink=sw_sink, g_na_out=g_na_out, g_sw_out=g_sw_out, w_out=w_out, g_ffn=g_ffn, w_up=w_up, conv_w=conv_w, conv_b=conv_b, w_down=w_down, g_final=g_final, loss_target=loss_target, m_w_ada=m_w_ada, m_b_ada=m_b_ada, m_g_attn=m_g_attn, m_w_in=m_w_in, m_na_rpb=m_na_rpb, m_sw_sink=m_sw_sink, m_g_na_out=m_g_na_out, m_g_sw_out=m_g_sw_out, m_w_out=m_w_out, m_g_ffn=m_g_ffn, m_w_up=m_w_up, m_conv_w=m_conv_w, m_conv_b=m_conv_b, m_w_down=m_w_down, m_g_final=m_g_final, v_w_ada=v_w_ada, v_b_ada=v_b_ada, v_g_attn=v_g_attn, v_w_in=v_w_in, v_na_rpb=v_na_rpb, v_sw_sink=v_sw_sink, v_g_na_out=v_g_na_out, v_g_sw_out=v_g_sw_out, v_w_out=v_w_out, v_g_ffn=v_g_ffn, v_w_up=v_w_up, v_conv_w=v_conv_w, v_conv_b=v_conv_b, v_w_down=v_w_down, v_g_final=v_g_final)
    weights = {n: given[n] for n in TWIN_WEIGHTS}
    shared = {n: given[n] for n in SHARED_INPUTS}
    per_example = {n: given[n] for n in ['x', 'c']}
    grad_fn = _jax.value_and_grad(_loss, argnums=(0, 1))

    def one_microbatch(ex, loss_target):
        ex = dict(ex)
        diff = ex.pop(TWIN_DIFF_INPUT)
        return grad_fn(weights, diff, {**shared, **ex}, loss_target)

    if N_MICROBATCH == 1:
        loss, (grad_w, grad_x) = one_microbatch(per_example, given["loss_target"])
    else:
        def body(carry, xs):
            loss_sum, grad_sum = carry
            l_k, (gw_k, gx_k) = one_microbatch(xs[0], xs[1])
            with _jax.named_scope("update"):
                return (loss_sum + l_k, _jax.tree.map(_jnp.add, grad_sum, gw_k)), gx_k

        init = (_jnp.zeros((), _jnp.float32), _jax.tree.map(_jnp.zeros_like, weights))
        (loss, grad_w), grad_x = _jax.lax.scan(body, init, (per_example, given["loss_target"]))
    with _jax.named_scope("update"):
        delta_w, new_m, new_v = {}, {}, {}
        for n in TWIN_WEIGHTS:
            delta_w[n], new_m[n], new_v[n] = _adamw(weights[n], grad_w[n], given["m_" + n], given["v_" + n])
    return (loss, grad_x, *[grad_w[n] for n in TWIN_WEIGHTS], *[delta_w[n] for n in TWIN_WEIGHTS],
            *[new_m[n] for n in TWIN_WEIGHTS], *[new_v[n] for n in TWIN_WEIGHTS])
```

```python
import functools

import jax
import jax.numpy as jnp
from jax import lax
from jax.experimental import pallas as pl
from jax.experimental.pallas import tpu as pltpu

F32 = jnp.float32
BF16 = jnp.bfloat16
MESH = pl.DeviceIdType.MESH

D_MODEL = 1024
HEAD_DIM = 64
NA_WIDTH = 512
SW_WIDTH = 512
SW_KV_WIDTH = 128
IN_WIDTH = 2304
D_FF = 2816
GRID_W = 64
NA_ROWS = 8
NA_COLS = 16
SW_BLOCK = 128
ROPE_THETA = 10000.0
EPS = 1e-6
NEG = -1e30
QK_SCALE = HEAD_DIM ** -0.5

ADAM_LR = 0.001
ADAM_B1 = 0.9
ADAM_B2 = 0.999
ADAM_EPS = 1e-08
ADAM_WD = 0.01
ADAM_STEP = 10

N_SHARD = 4
N_DEV = 8
LANES = 128
SUBLANES = 8
TOKEN_TILE = 512
FF_TILE = 256
VMEM_BIG = 56 * 1024 * 1024


def _mm(a, b):
    return jnp.dot(a, b, preferred_element_type=F32)


def _mm_nt(a, b):
    return lax.dot_general(a, b, (((1,), (1,)), ((), ())), preferred_element_type=F32)


def _mm_tn(a, b):
    return lax.dot_general(a, b, (((0,), (0,)), ((), ())), preferred_element_type=F32)


def _cparams(sem=None, vmem=None):
    kw = {}
    if sem is not None:
        kw["dimension_semantics"] = sem
    if vmem is not None:
        kw["vmem_limit_bytes"] = vmem
    return pltpu.CompilerParams(**kw)


def _sigmoid(x):
    return 1.0 / (1.0 + jnp.exp(-x))


def _rms_stats(x):
    r = lax.rsqrt(jnp.mean(x * x, axis=-1, keepdims=True) + EPS)
    return r, x * r


def _rms_bwd(dxn, xn, r):
    return r * (dxn - xn * jnp.mean(dxn * xn, axis=-1, keepdims=True))


def _my_pos():
    return lax.axis_index("x"), lax.axis_index("y"), lax.axis_index("c")


def _flip(v, bit):
    return 1 - v if bit else v


def _ada_forward(c8, w_ada, b_ada):
    d = c8.shape[1]
    ncol = w_ada.shape[1]

    def body(c_ref, w_ref, b_ref, mod_ref, sc_ref, m_scr, mod_buf, ssem, rsem, ssem2, rsem2):
        x, y, c = _my_pos()
        me = 4 * x + 2 * y + c
        shard = 2 * x + y
        cv = c_ref[...]
        my_rows = pl.ds(pl.multiple_of(me * SUBLANES, SUBLANES), SUBLANES)
        sc_ref[my_rows, :] = cv * _sigmoid(cv)

        def copy1(k):
            peer = (_flip(x, (k >> 2) & 1), _flip(y, (k >> 1) & 1), _flip(c, k & 1))
            return pltpu.make_async_remote_copy(
                src_ref=sc_ref.at[my_rows, :], dst_ref=sc_ref.at[my_rows, :],
                send_sem=ssem.at[k - 1], recv_sem=rsem.at[k - 1], device_id=peer, device_id_type=MESH)

        sends = [copy1(k) for k in range(1, N_DEV)]
        for cp in sends:
            cp.start()
        for cp in sends:
            cp.wait_recv()
        m_scr[...] = _mm(sc_ref[...].astype(BF16), w_ref[...].astype(BF16))

        def copy2(k):
            px, py = _flip(x, (k >> 1) & 1), _flip(y, k & 1)
            rows = pl.ds(pl.multiple_of((4 * px + 2 * py + c) * SUBLANES, SUBLANES), SUBLANES)
            return pltpu.make_async_remote_copy(
                src_ref=m_scr.at[rows, :], dst_ref=mod_buf.at[shard],
                send_sem=ssem2.at[k - 1], recv_sem=rsem2.at[k - 1], device_id=(px, py, c), device_id_type=MESH)

        sends2 = [copy2(k) for k in range(1, N_SHARD)]
        for cp in sends2:
            cp.start()
        mod_buf[shard] = m_scr[my_rows, :]
        for cp in sends2:
            cp.wait_recv()
        for s in range(N_SHARD):
            mod_ref[:, s * ncol:(s + 1) * ncol] = mod_buf[s] + b_ref[:, s * ncol:(s + 1) * ncol]
        for cp in sends + sends2:
            cp.wait_send()

    vm = pl.BlockSpec(memory_space=pltpu.VMEM)
    return pl.pallas_call(
        body, name="ada_forward",
        out_shape=(jax.ShapeDtypeStruct((SUBLANES, N_SHARD * ncol), F32),
                   jax.ShapeDtypeStruct((N_DEV * SUBLANES, d), F32)),
        in_specs=[vm, vm, vm], out_specs=(vm, vm),
        scratch_shapes=[pltpu.VMEM((N_DEV * SUBLANES, ncol), F32), pltpu.VMEM((N_SHARD, SUBLANES, ncol), F32),
                        pltpu.SemaphoreType.DMA((N_DEV - 1,)), pltpu.SemaphoreType.DMA((N_DEV - 1,)),
                        pltpu.SemaphoreType.DMA((N_SHARD - 1,)), pltpu.SemaphoreType.DMA((N_SHARD - 1,))],
        compiler_params=_cparams(vmem=VMEM_BIG),
    )(c8, w_ada, b_ada)


def _gather_shards(shards):
    n = len(shards)

    def body(*refs):
        src, dst = refs[:n], refs[n:2 * n]
        ssem, rsem, lsem = refs[2 * n:]
        x, y, c = _my_pos()
        shard = 2 * x + y
        local = [pltpu.make_async_copy(src[i], dst[i].at[shard], lsem.at[i]) for i in range(n)]
        for cp in local:
            cp.start()
        remote = []
        for i in range(n):
            for k in range(1, N_SHARD):
                px, py = _flip(x, (k >> 1) & 1), _flip(y, k & 1)
                remote.append(pltpu.make_async_remote_copy(
                    src_ref=src[i], dst_ref=dst[i].at[shard],
                    send_sem=ssem.at[i, k - 1], recv_sem=rsem.at[i, k - 1], device_id=(px, py, c), device_id_type=MESH))
        for cp in remote:
            cp.start()
        for cp in remote:
            cp.wait_recv()
        for cp in remote:
            cp.wait_send()
        for cp in local:
            cp.wait()

    hbm = pl.BlockSpec(memory_space=pl.ANY)
    return pl.pallas_call(
        body, name="gather_weights",
        out_shape=tuple(jax.ShapeDtypeStruct((N_SHARD,) + s.shape, s.dtype) for s in shards),
        in_specs=[hbm] * n, out_specs=tuple([hbm] * n),
        scratch_shapes=[pltpu.SemaphoreType.DMA((n, N_SHARD - 1)), pltpu.SemaphoreType.DMA((n, N_SHARD - 1)),
                        pltpu.SemaphoreType.DMA((n,))],
    )(*shards)


def _scatter_partials(grads):
    n = len(grads)

    def body(*refs):
        src, dst = refs[:n], refs[n:2 * n]
        ssem, rsem, lsem = refs[2 * n:]
        x, y, c = _my_pos()
        shard = 2 * x + y
        local = [pltpu.make_async_copy(src[i].at[shard], dst[i].at[N_SHARD - 1], lsem.at[i]) for i in range(n)]
        for cp in local:
            cp.start()
        remote = []
        for i in range(n):
            for k in range(1, N_SHARD):
                px, py = _flip(x, (k >> 1) & 1), _flip(y, k & 1)
                remote.append(pltpu.make_async_remote_copy(
                    src_ref=src[i].at[2 * px + py], dst_ref=dst[i].at[k - 1],
                    send_sem=ssem.at[i, k - 1], recv_sem=rsem.at[i, k - 1], device_id=(px, py, c), device_id_type=MESH))
        for cp in remote:
            cp.start()
        for cp in remote:
            cp.wait_recv()
        for cp in remote:
            cp.wait_send()
        for cp in local:
            cp.wait()

    hbm = pl.BlockSpec(memory_space=pl.ANY)
    return pl.pallas_call(
        body, name="scatter_grads",
        out_shape=tuple(jax.ShapeDtypeStruct(g.shape, g.dtype) for g in grads),
        in_specs=[hbm] * n, out_specs=tuple([hbm] * n),
        scratch_shapes=[pltpu.SemaphoreType.DMA((n, N_SHARD - 1)), pltpu.SemaphoreType.DMA((n, N_SHARD - 1)),
                        pltpu.SemaphoreType.DMA((n,))],
    )(*grads)


def _swap_with_sibling(parts):
    n = len(parts)

    def body(*refs):
        src, dst = refs[:n], refs[n:2 * n]
        ssem, rsem = refs[2 * n:]
        x, y, c = _my_pos()
        cps = [pltpu.make_async_remote_copy(src_ref=src[i], dst_ref=dst[i], send_sem=ssem.at[i], recv_sem=rsem.at[i],
                                            device_id=(x, y, 1 - c), device_id_type=MESH) for i in range(n)]
        for cp in cps:
            cp.start()
        for cp in cps:
            cp.wait_recv()
        for cp in cps:
            cp.wait_send()

    hbm = pl.BlockSpec(memory_space=pl.ANY)
    return pl.pallas_call(
        body, name="swap_sibling",
        out_shape=tuple(jax.ShapeDtypeStruct(p.shape, p.dtype) for p in parts),
        in_specs=[hbm] * n, out_specs=tuple([hbm] * n),
        scratch_shapes=[pltpu.SemaphoreType.DMA((n,)), pltpu.SemaphoreType.DMA((n,))],
    )(*parts)


def _allreduce_small(packed):
    r = packed.shape[0]

    def body(p_ref, sum_ref, all_ref, ssem, rsem):
        x, y, c = _my_pos()
        me = 4 * x + 2 * y + c
        all_ref[me] = p_ref[...]
        cps = []
        for k in range(1, N_DEV):
            peer = (_flip(x, (k >> 2) & 1), _flip(y, (k >> 1) & 1), _flip(c, k & 1))
            cps.append(pltpu.make_async_remote_copy(
                src_ref=all_ref.at[me], dst_ref=all_ref.at[me], send_sem=ssem.at[k - 1], recv_sem=rsem.at[k - 1],
                device_id=peer, device_id_type=MESH))
        for cp in cps:
            cp.start()
        for cp in cps:
            cp.wait_recv()
        acc = all_ref[0]
        for dev in range(1, N_DEV):
            acc = acc + all_ref[dev]
        sum_ref[...] = acc
        for cp in cps:
            cp.wait_send()

    vm = pl.BlockSpec(memory_space=pltpu.VMEM)
    return pl.pallas_call(
        body, name="allreduce_small",
        out_shape=(jax.ShapeDtypeStruct((r, LANES), F32), jax.ShapeDtypeStruct((N_DEV, r, LANES), F32)),
        in_specs=[vm], out_specs=(vm, vm),
        scratch_shapes=[pltpu.SemaphoreType.DMA((N_DEV - 1,)), pltpu.SemaphoreType.DMA((N_DEV - 1,))],
    )(packed)


def _rope_rot(t):
    w = t.shape[1]
    lane = lax.broadcasted_iota(jnp.int32, t.shape, 1)
    first = (lane % HEAD_DIM) < (HEAD_DIM // 2)
    return jnp.where(first, pltpu.roll(t, w - HEAD_DIM // 2, 1), pltpu.roll(t, HEAD_DIM // 2, 1))


def _in_proj(x, mod3, g_attn, w_in_t, cos_t, sin_t, seq):
    t, d = x.shape
    tm = TOKEN_TILE
    per_seq = seq // tm
    rope_lo, rope_hi = 3 * NA_WIDTH, 3 * NA_WIDTH + SW_WIDTH + SW_KV_WIDTH
    n_rep = (rope_hi - rope_lo) // LANES

    def body(x_ref, mod_ref, g_ref, w_ref, cos_ref, sin_ref, h_ref, p_ref):
        r, xn = _rms_stats(x_ref[...])
        shift, scale = mod_ref[0, :, 0:d], mod_ref[0, :, d:2 * d]
        hb = ((xn * g_ref[...]) * (1.0 + scale) + shift).astype(BF16)
        h_ref[...] = hb
        p_ref[:, :rope_lo] = _mm_nt(hb, w_ref[:rope_lo, :]).astype(BF16)
        pr = _mm_nt(hb, w_ref[rope_lo:rope_hi, :])
        cos = jnp.concatenate([cos_ref[...]] * n_rep, axis=1)
        sin = jnp.concatenate([sin_ref[...]] * n_rep, axis=1)
        p_ref[:, rope_lo:rope_hi] = (pr * cos + _rope_rot(pr) * sin).astype(BF16)
        p_ref[:, rope_hi:] = _mm_nt(hb, w_ref[rope_hi:, :]).astype(BF16)

    return pl.pallas_call(
        body, name="in_proj", grid=(t // tm,),
        out_shape=(jax.ShapeDtypeStruct((t, d), BF16), jax.ShapeDtypeStruct((t, IN_WIDTH), BF16)),
        in_specs=[pl.BlockSpec((tm, d), lambda i: (i, 0)),
                  pl.BlockSpec((1, 1, 6 * d), lambda i: (i // per_seq, 0, 0)),
                  pl.BlockSpec((1, d), lambda i: (0, 0)),
                  pl.BlockSpec((IN_WIDTH, d), lambda i: (0, 0)),
                  pl.BlockSpec((tm, LANES), lambda i: (i % per_seq, 0)),
                  pl.BlockSpec((tm, LANES), lambda i: (i % per_seq, 0))],
        out_specs=(pl.BlockSpec((tm, d), lambda i: (i, 0)), pl.BlockSpec((tm, IN_WIDTH), lambda i: (i, 0))),
        compiler_params=_cparams(("arbitrary",), VMEM_BIG),
    )(x, mod3, g_attn, w_in_t, cos_t, sin_t)


def _na_bias_tiles(rpb2d):
    n_heads = rpb2d.shape[0] // (2 * NA_ROWS - 1)
    n_dr = 2 * NA_ROWS - 1
    n_dc = 2 * NA_COLS - 1

    def body(t_ref, o_ref):
        h = pl.program_id(0)
        q = lax.broadcasted_iota(jnp.int32, (GRID_W, LANES), 0)
        lane = lax.broadcasted_iota(jnp.int32, (GRID_W, LANES), 1)
        k = lane % GRID_W
        cs = jnp.clip(q - NA_COLS // 2, 0, GRID_W - NA_COLS)
        ok = (k >= cs) & (k < cs + NA_COLS)
        diag = k - q + (NA_COLS - 1)
        low = lane < GRID_W
        for dd in range(n_dr - 1):
            def step(dc, acc, dd=dd):
                s0 = t_ref[h * n_dr + dd, dc]
                s1 = t_ref[h * n_dr + dd + 1, dc]
                return acc + jnp.where(diag == dc, jnp.where(low, s0, s1), 0.0)
            tile = lax.fori_loop(0, n_dc, step, jnp.zeros((GRID_W, LANES), F32))
            o_ref[0, dd] = jnp.where(ok, tile, NEG)

    return pl.pallas_call(
        body, name="na_bias_tiles", grid=(n_heads,),
        out_shape=jax.ShapeDtypeStruct((n_heads, n_dr - 1, GRID_W, LANES), F32),
        in_specs=[pl.BlockSpec(memory_space=pltpu.SMEM)],
        out_specs=pl.BlockSpec((1, n_dr - 1, GRID_W, LANES), lambda h: (h, 0, 0, 0)),
        compiler_params=_cparams(("arbitrary",)),
    )(rpb2d)


def _na_prepare(k_ref, v_ref, km, vm):
    lane = lax.broadcasted_iota(jnp.int32, k_ref.shape, 1)
    low = lane < HEAD_DIM
    kv = k_ref[...]
    vv = v_ref[...]
    zero = jnp.zeros_like(kv)
    km[0] = jnp.where(low, kv, zero)
    km[1] = jnp.where(low, zero, kv)
    vm[0] = jnp.where(low, vv, zero)
    vm[1] = jnp.where(low, zero, vv)


def _na_window(r, n_rows):
    rs = jnp.clip(r - NA_ROWS // 2, 0, n_rows - NA_ROWS)
    return rs, r - rs


def _na_probs(q, kw, tp_ref, h, off):
    s = _mm_nt(q, kw) * QK_SCALE
    bias = jnp.concatenate([tp_ref[h, 2 * w - off + (NA_ROWS - 1)] for w in range(NA_ROWS // 2)], axis=1)
    s = s + bias
    m = jnp.max(s, axis=-1, keepdims=True)
    e = jnp.exp(s - m)
    return e / jnp.sum(e, axis=-1, keepdims=True)


def _na_forward(proj, tiles, batch, seq):
    t = proj.shape[0]
    n_rows = seq // GRID_W
    n_pairs = NA_WIDTH // LANES
    win = NA_ROWS * GRID_W

    def body(q_ref, k_ref, v_ref, tp_ref, o_ref, km, vm):
        _na_prepare(k_ref, v_ref, km, vm)

        def row(r, carry):
            rs, off = _na_window(r, n_rows)
            rows = pl.ds(pl.multiple_of(r * GRID_W, GRID_W), GRID_W)
            wrows = pl.ds(pl.multiple_of(rs * GRID_W, GRID_W), win)
            q = q_ref[rows, :]
            acc = jnp.zeros((GRID_W, LANES), F32)
            for h in range(2):
                p = _na_probs(q, km[h, wrows, :], tp_ref, h, off)
                acc = acc + _mm(p.astype(BF16), vm[h, wrows, :])
            o_ref[rows, :] = acc
            return carry

        lax.fori_loop(0, n_rows, row, 0)

    return pl.pallas_call(
        body, name="na_forward", grid=(batch, n_pairs),
        out_shape=jax.ShapeDtypeStruct((t, NA_WIDTH), F32),
        in_specs=[pl.BlockSpec((seq, LANES), lambda b, p: (b, p)),
                  pl.BlockSpec((seq, LANES), lambda b, p: (b, n_pairs + p)),
                  pl.BlockSpec((seq, LANES), lambda b, p: (b, 2 * n_pairs + p)),
                  pl.BlockSpec((2, 2 * NA_ROWS - 2, GRID_W, LANES), lambda b, p: (p, 0, 0, 0))],
        out_specs=pl.BlockSpec((seq, LANES), lambda b, p: (b, p)),
        scratch_shapes=[pltpu.VMEM((2, seq, LANES), BF16), pltpu.VMEM((2, seq, LANES), BF16)],
        compiler_params=_cparams(("arbitrary", "arbitrary")),
    )(proj, proj, proj, tiles)


def _sw_prepare(kv_ref, g, dst_lo, dst_hi, seq):
    lane = lax.broadcasted_iota(jnp.int32, kv_ref.shape, 1)
    mine = (lane // HEAD_DIM) == g
    kg = jnp.where(mine, kv_ref[...].astype(F32), 0.0)
    kr = pltpu.roll(kg, HEAD_DIM, 1)
    first = g == 0
    zero = jnp.zeros((SW_BLOCK, LANES), BF16)
    for dst, val in ((dst_lo, jnp.where(first, kg, kr)), (dst_hi, jnp.where(first, kr, kg))):
        dst[0:SW_BLOCK, :] = zero
        dst[SW_BLOCK:SW_BLOCK + seq, :] = val.astype(BF16)
        dst[SW_BLOCK + seq:, :] = zero


def _sw_mask(n, seq):
    qi = lax.broadcasted_iota(jnp.int32, (SW_BLOCK, 3 * SW_BLOCK), 0)
    kj = lax.broadcasted_iota(jnp.int32, (SW_BLOCK, 3 * SW_BLOCK), 1)
    kpos = n * SW_BLOCK - SW_BLOCK + kj
    return (jnp.abs(qi + SW_BLOCK - kj) <= SW_BLOCK) & (kpos >= 0) & (kpos < seq)


def _sw_probs(qb, kk, ok, sk):
    s = jnp.where(ok, _mm_nt(qb, kk) * QK_SCALE, NEG)
    m = jnp.maximum(jnp.max(s, axis=-1, keepdims=True), sk)
    p = jnp.exp(s - m)
    es = jnp.exp(sk - m)
    den = jnp.sum(p, axis=-1, keepdims=True) + es
    return p / den, es / den


def _sw_forward(proj, sink, batch, seq):
    t = proj.shape[0]
    n_pairs = SW_WIDTH // LANES
    q_blk = 3 * NA_WIDTH // LANES
    k_blk = q_blk + n_pairs
    n_blocks = seq // SW_BLOCK
    pad = seq + 2 * SW_BLOCK

    def body(sink_ref, q_ref, k_ref, v_ref, o_ref, k_lo, k_hi, v_lo, v_hi):
        hp = pl.program_id(1)
        g = hp // 2
        _sw_prepare(k_ref, g, k_lo, k_hi, seq)
        _sw_prepare(v_ref, g, v_lo, v_hi, seq)

        def block(n, carry):
            rows = pl.ds(pl.multiple_of(n * SW_BLOCK, SW_BLOCK), SW_BLOCK)
            wrows = pl.ds(pl.multiple_of(n * SW_BLOCK, SW_BLOCK), 3 * SW_BLOCK)
            qb = q_ref[rows, :]
            ok = _sw_mask(n, seq)
            acc = jnp.zeros((SW_BLOCK, LANES), F32)
            for i, (kr, vr) in enumerate(((k_lo, v_lo), (k_hi, v_hi))):
                p, _ = _sw_probs(qb, kr[wrows, :], ok, sink_ref[2 * hp + i])
                acc = acc + _mm(p.astype(BF16), vr[wrows, :])
            o_ref[rows, :] = acc
            return carry

        lax.fori_loop(0, n_blocks, block, 0)

    return pl.pallas_call(
        body, name="sw_forward", grid=(batch, n_pairs),
        out_shape=jax.ShapeDtypeStruct((t, SW_WIDTH), F32),
        in_specs=[pl.BlockSpec(memory_space=pltpu.SMEM),
                  pl.BlockSpec((seq, LANES), lambda b, p: (b, q_blk + p)),
                  pl.BlockSpec((seq, LANES), lambda b, p: (b, k_blk)),
                  pl.BlockSpec((seq, LANES), lambda b, p: (b, k_blk + 1))],
        out_specs=pl.BlockSpec((seq, LANES), lambda b, p: (b, p)),
        scratch_shapes=[pltpu.VMEM((pad, LANES), BF16)] * 4,
        compiler_params=_cparams(("arbitrary", "arbitrary")),
    )(sink, proj, proj, proj)


def _out_proj(oa, ob, g_na, g_sw, w_out, x, mod3, g_ffn, seq):
    t, d = x.shape
    tm = TOKEN_TILE
    per_seq = seq // tm

    def body(oa_ref, ob_ref, gna_ref, gsw_ref, w_ref, x_ref, mod_ref, gf_ref, oab_ref, mix_ref, x1_ref, h2_ref):
        _, na = _rms_stats(oa_ref[...])
        _, nb = _rms_stats(ob_ref[...])
        oab = jnp.concatenate([na * gna_ref[...], nb * gsw_ref[...]], axis=1).astype(BF16)
        oab_ref[...] = oab
        mix = _mm(oab, w_ref[...])
        mix_ref[...] = mix
        gate_a = mod_ref[0, :, 2 * d:3 * d]
        shift_f, scale_f = mod_ref[0, :, 3 * d:4 * d], mod_ref[0, :, 4 * d:5 * d]
        x1 = x_ref[...] + gate_a * mix
        x1_ref[...] = x1
        _, xn = _rms_stats(x1)
        h2_ref[...] = ((xn * gf_ref[...]) * (1.0 + scale_f) + shift_f).astype(BF16)

    tile = lambda w: pl.BlockSpec((tm, w), lambda i: (i, 0))
    vec = lambda w: pl.BlockSpec((1, w), lambda i: (0, 0))
    return pl.pallas_call(
        body, name="out_proj", grid=(t // tm,),
        out_shape=(jax.ShapeDtypeStruct((t, d), BF16), jax.ShapeDtypeStruct((t, d), F32),
                   jax.ShapeDtypeStruct((t, d), F32), jax.ShapeDtypeStruct((t, d), BF16)),
        in_specs=[tile(NA_WIDTH), tile(SW_WIDTH), vec(NA_WIDTH), vec(SW_WIDTH),
                  pl.BlockSpec((d, d), lambda i: (0, 0)), tile(d),
                  pl.BlockSpec((1, 1, 6 * d), lambda i: (i // per_seq, 0, 0)), vec(d)],
        out_specs=(tile(d), tile(d), tile(d), tile(d)),
        compiler_params=_cparams(("arbitrary",), VMEM_BIG),
    )(oa, ob, g_na, g_sw, w_out, x, mod3, g_ffn)


def _up_proj(h2, w_up):
    t, d = h2.shape
    tm = TOKEN_TILE
    wcol = w_up.shape[2]

    def body(h_ref, w_ref, u_ref):
        u_ref[0] = _mm(h_ref[...], w_ref[0])

    return pl.pallas_call(
        body, name="up_proj", grid=(N_SHARD, t // tm),
        out_shape=jax.ShapeDtypeStruct((2, t, D_FF), F32),
        in_specs=[pl.BlockSpec((tm, d), lambda j, i: (i, 0)), pl.BlockSpec((1, d, wcol), lambda j, i: (j, 0, 0))],
        out_specs=pl.BlockSpec((1, tm, wcol), lambda j, i: (j // 2, i, j % 2)),
        compiler_params=_cparams(("arbitrary", "arbitrary"), VMEM_BIG),
    )(h2, w_up)


def _conv_taps(gt):
    n = gt.shape[0]
    row = lax.broadcasted_iota(jnp.int32, gt.shape, 0)
    prev = jnp.where(row == 0, 0.0, pltpu.roll(gt, 1, 0))
    nxt = jnp.where(row == n - 1, 0.0, pltpu.roll(gt, n - 1, 0))
    return prev, nxt


def _conv_gate(u, conv_w, conv_b, batch, seq):
    t = u.shape[1]
    cw = FF_TILE

    def body(u_ref, w_ref, b_ref, a_ref):
        val, gt = u_ref[0], u_ref[1]
        prev, nxt = _conv_taps(gt)
        gc = prev * w_ref[0:1, :] + gt * w_ref[1:2, :] + nxt * w_ref[2:3, :] + b_ref[...]
        a_ref[...] = ((gc * _sigmoid(gc)) * val).astype(BF16)

    return pl.pallas_call(
        body, name="conv_gate", grid=(batch, D_FF // cw),
        out_shape=jax.ShapeDtypeStruct((t, D_FF), BF16),
        in_specs=[pl.BlockSpec((2, seq, cw), lambda b, j: (0, b, j)),
                  pl.BlockSpec((3, cw), lambda b, j: (0, j)), pl.BlockSpec((1, cw), lambda b, j: (0, j))],
        out_specs=pl.BlockSpec((seq, cw), lambda b, j: (b, j)),
        compiler_params=_cparams(("arbitrary", "arbitrary"), VMEM_BIG),
    )(u, conv_w, conv_b)


def _down_and_loss(a, w_down, x1, mod3, g_final, target, seq):
    t, d = x1.shape
    tm = TOKEN_TILE
    per_seq = seq // tm
    batch = t // seq

    def body(a_ref, w_ref, x1_ref, mod_ref, g_ref, tgt_ref, dx2_ref, dffn_ref, loss_ref, dgate_ref, dg_ref):
        i = pl.program_id(0)
        f = _mm(a_ref[...], w_ref[...])
        gate_f = mod_ref[0, :, 5 * d:6 * d]
        x2 = x1_ref[...] + gate_f * f
        r, xn = _rms_stats(x2)
        err = xn * g_ref[...] - tgt_ref[...]
        part = 0.5 * jnp.sum(jnp.mean(err * err, axis=-1, keepdims=True))
        dy = err / d
        dx2 = _rms_bwd(dy * g_ref[...], xn, r)
        dx2_ref[...] = dx2
        dffn_ref[...] = (dx2 * gate_f).astype(BF16)

        @pl.when(i == 0)
        def _():
            loss_ref[...] = jnp.zeros_like(loss_ref)
            dg_ref[...] = jnp.zeros_like(dg_ref)

        @pl.when(i % per_seq == 0)
        def _():
            dgate_ref[...] = jnp.zeros_like(dgate_ref)

        loss_ref[...] += part
        dg_ref[...] += jnp.sum(dy * xn, axis=0, keepdims=True)
        dgate_ref[0] += jnp.sum(dx2 * f, axis=0, keepdims=True)

    tile = lambda w: pl.BlockSpec((tm, w), lambda i: (i, 0))
    return pl.pallas_call(
        body, name="down_loss", grid=(t // tm,),
        out_shape=(jax.ShapeDtypeStruct((t, d), F32), jax.ShapeDtypeStruct((t, d), BF16),
                   jax.ShapeDtypeStruct((SUBLANES, LANES), F32), jax.ShapeDtypeStruct((batch, 1, d), F32),
                   jax.ShapeDtypeStruct((1, d), F32)),
        in_specs=[tile(D_FF), pl.BlockSpec((D_FF, d), lambda i: (0, 0)), tile(d),
                  pl.BlockSpec((1, 1, 6 * d), lambda i: (i // per_seq, 0, 0)),
                  pl.BlockSpec((1, d), lambda i: (0, 0)), tile(d)],
        out_specs=(tile(d), tile(d), pl.BlockSpec((SUBLANES, LANES), lambda i: (0, 0)),
                   pl.BlockSpec((1, 1, d), lambda i: (i // per_seq, 0, 0)), pl.BlockSpec((1, d), lambda i: (0, 0))),
        compiler_params=_cparams(("arbitrary",), VMEM_BIG),
    )(a, w_down, x1, mod3, g_final, target)


def _ffn_backward(dffn, w_down, u, conv_w, conv_b, batch, seq):
    t, d = dffn.shape
    cw = FF_TILE

    def body(df_ref, wd_ref, u_ref, w_ref, b_ref, du_ref, gwd_ref, gcw_ref, gcb_ref):
        b = pl.program_id(1)
        df = df_ref[...]
        da = _mm_nt(df, wd_ref[...])
        val, gt = u_ref[0], u_ref[1]
        prev, nxt = _conv_taps(gt)
        gc = prev * w_ref[0:1, :] + gt * w_ref[1:2, :] + nxt * w_ref[2:3, :] + b_ref[...]
        sg = _sigmoid(gc)
        sl = gc * sg

        @pl.when(b == 0)
        def _():
            gwd_ref[...] = jnp.zeros_like(gwd_ref)
            gcw_ref[...] = jnp.zeros_like(gcw_ref)
            gcb_ref[...] = jnp.zeros_like(gcb_ref)

        gwd_ref[...] += _mm_tn((sl * val).astype(BF16), df)
        du_ref[0] = (da * sl).astype(BF16)
        dgc = (da * val) * (sg * (1.0 + gc * (1.0 - sg)))
        gcb_ref[...] += jnp.sum(dgc, axis=0, keepdims=True)
        gcw_ref[0:1, :] += jnp.sum(dgc * prev, axis=0, keepdims=True)
        gcw_ref[1:2, :] += jnp.sum(dgc * gt, axis=0, keepdims=True)
        gcw_ref[2:3, :] += jnp.sum(dgc * nxt, axis=0, keepdims=True)
        dprev, dnxt = _conv_taps(dgc)
        du_ref[1] = (dnxt * w_ref[0:1, :] + dgc * w_ref[1:2, :] + dprev * w_ref[2:3, :]).astype(BF16)

    return pl.pallas_call(
        body, name="ffn_backward", grid=(D_FF // cw, batch),
        out_shape=(jax.ShapeDtypeStruct((2, t, D_FF), BF16), jax.ShapeDtypeStruct((D_FF, d), F32),
                   jax.ShapeDtypeStruct((3, D_FF), F32), jax.ShapeDtypeStruct((1, D_FF), F32)),
        in_specs=[pl.BlockSpec((seq, d), lambda j, b: (b, 0)), pl.BlockSpec((cw, d), lambda j, b: (j, 0)),
                  pl.BlockSpec((2, seq, cw), lambda j, b: (0, b, j)),
                  pl.BlockSpec((3, cw), lambda j, b: (0, j)), pl.BlockSpec((1, cw), lambda j, b: (0, j))],
        out_specs=(pl.BlockSpec((2, seq, cw), lambda j, b: (0, b, j)), pl.BlockSpec((cw, d), lambda j, b: (j, 0)),
                   pl.BlockSpec((3, cw), lambda j, b: (0, j)), pl.BlockSpec((1, cw), lambda j, b: (0, j))),
        compiler_params=_cparams(("arbitrary", "arbitrary"), VMEM_BIG),
    )(dffn, w_down, u, conv_w, conv_b)


def _up_backward(du, w_up, x1, mod3, g_ffn, dx2, mix, seq):
    _, t, _ = du.shape
    d = x1.shape[1]
    tm = TOKEN_TILE // 2
    per_seq = seq // tm
    batch = t // seq
    wcol = w_up.shape[2]

    def body(du_ref, w_ref, x1_ref, mod_ref, g_ref, dx2_ref, mix_ref,
             dx1_ref, dmix_ref, dsh_ref, dsc_ref, dga_ref, dg_ref):
        i = pl.program_id(0)
        dh = jnp.zeros((tm, d), F32)
        for j in range(N_SHARD):
            dh = dh + _mm_nt(du_ref[j // 2, :, (j % 2) * wcol:(j % 2 + 1) * wcol], w_ref[j])
        gate_a = mod_ref[0, :, 2 * d:3 * d]
        scale_f = mod_ref[0, :, 4 * d:5 * d]
        r, xn = _rms_stats(x1_ref[...])
        xg = xn * g_ref[...]
        dxg = dh * (1.0 + scale_f)
        dx1 = dx2_ref[...] + _rms_bwd(dxg * g_ref[...], xn, r)
        dx1_ref[...] = dx1
        dmix_ref[...] = (dx1 * gate_a).astype(BF16)

        @pl.when(i == 0)
        def _():
            dg_ref[...] = jnp.zeros_like(dg_ref)

        @pl.when(i % per_seq == 0)
        def _():
            dsh_ref[...] = jnp.zeros_like(dsh_ref)
            dsc_ref[...] = jnp.zeros_like(dsc_ref)
            dga_ref[...] = jnp.zeros_like(dga_ref)

        dg_ref[...] += jnp.sum(dxg * xn, axis=0, keepdims=True)
        dsh_ref[0] += jnp.sum(dh, axis=0, keepdims=True)
        dsc_ref[0] += jnp.sum(dh * xg, axis=0, keepdims=True)
        dga_ref[0] += jnp.sum(dx1 * mix_ref[...], axis=0, keepdims=True)

    tile = lambda w: pl.BlockSpec((tm, w), lambda i: (i, 0))
    per_b = pl.BlockSpec((1, 1, d), lambda i: (i // per_seq, 0, 0))
    small = jax.ShapeDtypeStruct((batch, 1, d), F32)
    return pl.pallas_call(
        body, name="up_backward", grid=(t // tm,),
        out_shape=(jax.ShapeDtypeStruct((t, d), F32), jax.ShapeDtypeStruct((t, d), BF16), small, small, small,
                   jax.ShapeDtypeStruct((1, d), F32)),
        in_specs=[pl.BlockSpec((2, tm, D_FF), lambda i: (0, i, 0)),
                  pl.BlockSpec((N_SHARD, d, wcol), lambda i: (0, 0, 0)), tile(d),
                  pl.BlockSpec((1, 1, 6 * d), lambda i: (i // per_seq, 0, 0)),
                  pl.BlockSpec((1, d), lambda i: (0, 0)), tile(d), tile(d)],
        out_specs=(tile(d), tile(d), per_b, per_b, per_b, pl.BlockSpec((1, d), lambda i: (0, 0))),
        compiler_params=_cparams(("arbitrary",), VMEM_BIG),
    )(du, w_up, x1, mod3, g_ffn, dx2, mix)


def _up_weight_grad(h2, du):
    t, d = h2.shape
    tk = TOKEN_TILE
    wcol = D_FF // 2

    def body(h_ref, du_ref, g_ref):
        @pl.when(pl.program_id(1) == 0)
        def _():
            g_ref[...] = jnp.zeros_like(g_ref)
        g_ref[0] += _mm_tn(h_ref[...], du_ref[0])

    return pl.pallas_call(
        body, name="up_weight_grad", grid=(N_SHARD, t // tk),
        out_shape=jax.ShapeDtypeStruct((N_SHARD, d, wcol), F32),
        in_specs=[pl.BlockSpec((tk, d), lambda j, k: (k, 0)),
                  pl.BlockSpec((1, tk, wcol), lambda j, k: (j // 2, k, j % 2))],
        out_specs=pl.BlockSpec((1, d, wcol), lambda j, k: (j, 0, 0)),
        compiler_params=_cparams(("arbitrary", "arbitrary"), VMEM_BIG),
    )(h2, du)


def _out_backward(dmix, w_out, oab, oa, ob, g_na, g_sw):
    t, d = dmix.shape
    tm = TOKEN_TILE
    hw = NA_WIDTH

    def body(dm_ref, w_ref, oab_ref, oa_ref, ob_ref, gna_ref, gsw_ref, doa_ref, dob_ref, gw_ref, dgna_ref, dgsw_ref):
        @pl.when(pl.program_id(0) == 0)
        def _():
            gw_ref[...] = jnp.zeros_like(gw_ref)
            dgna_ref[...] = jnp.zeros_like(dgna_ref)
            dgsw_ref[...] = jnp.zeros_like(dgsw_ref)

        dm = dm_ref[...]
        gw_ref[...] += _mm_tn(oab_ref[...], dm)
        do = _mm_nt(dm, w_ref[...])
        for raw_ref, g_ref, dst_ref, dg_ref, lo in ((oa_ref, gna_ref, doa_ref, dgna_ref, 0),
                                                     (ob_ref, gsw_ref, dob_ref, dgsw_ref, hw)):
            r, xn = _rms_stats(raw_ref[...])
            dpart = do[:, lo:lo + hw]
            dg_ref[...] += jnp.sum(dpart * xn, axis=0, keepdims=True)
            dst_ref[...] = _rms_bwd(dpart * g_ref[...], xn, r).astype(BF16)

    tile = lambda w: pl.BlockSpec((tm, w), lambda i: (i, 0))
    vec = lambda w: pl.BlockSpec((1, w), lambda i: (0, 0))
    return pl.pallas_call(
        body, name="out_backward", grid=(t // tm,),
        out_shape=(jax.ShapeDtypeStruct((t, hw), BF16), jax.ShapeDtypeStruct((t, hw), BF16),
                   jax.ShapeDtypeStruct((d, d), F32), jax.ShapeDtypeStruct((1, hw), F32),
                   jax.ShapeDtypeStruct((1, hw), F32)),
        in_specs=[tile(d), pl.BlockSpec((d, d), lambda i: (0, 0)), tile(d), tile(hw), tile(hw), vec(hw), vec(hw)],
        out_specs=(tile(hw), tile(hw), pl.BlockSpec((d, d), lambda i: (0, 0)), vec(hw), vec(hw)),
        compiler_params=_cparams(("arbitrary",), VMEM_BIG),
    )(dmix, w_out, oab, oa, ob, g_na, g_sw)


def _na_backward(proj, d_o, tiles, batch, seq):
    t = proj.shape[0]
    n_rows = seq // GRID_W
    n_pairs = NA_WIDTH // LANES
    win = NA_ROWS * GRID_W
    n_tiles = 2 * NA_ROWS - 2

    def body(q_ref, k_ref, v_ref, do_ref, tp_ref, dq_ref, dk_ref, dv_ref, dtp_ref, km, vm, dk_acc, dv_acc):
        @pl.when(pl.program_id(1) == 0)
        def _():
            dtp_ref[...] = jnp.zeros_like(dtp_ref)

        _na_prepare(k_ref, v_ref, km, vm)
        dk_acc[...] = jnp.zeros_like(dk_acc)
        dv_acc[...] = jnp.zeros_like(dv_acc)
        lane = lax.broadcasted_iota(jnp.int32, (GRID_W, LANES), 1)

        def row(r, carry):
            rs, off = _na_window(r, n_rows)
            rows = pl.ds(pl.multiple_of(r * GRID_W, GRID_W), GRID_W)
            wrows = pl.ds(pl.multiple_of(rs * GRID_W, GRID_W), win)
            q = q_ref[rows, :]
            do = do_ref[rows, :]
            dq = jnp.zeros((GRID_W, LANES), F32)
            for h in range(2):
                mine = (lane // HEAD_DIM) == h
                kw = km[h, wrows, :]
                p = _na_probs(q, kw, tp_ref, h, off)
                dp = _mm_nt(do, vm[h, wrows, :])
                ds = p * (dp - jnp.sum(p * dp, axis=-1, keepdims=True))
                for w in range(NA_ROWS // 2):
                    dtp_ref[h, 2 * w - off + (NA_ROWS - 1)] += ds[:, w * LANES:(w + 1) * LANES]
                dsb = (ds * QK_SCALE).astype(BF16)
                dq = dq + _mm(dsb, kw)
                dk_acc[wrows, :] += _mm_tn(dsb, jnp.where(mine, q, jnp.zeros_like(q)))
                dv_acc[wrows, :] += _mm_tn(p.astype(BF16), jnp.where(mine, do, jnp.zeros_like(do)))
            dq_ref[rows, :] = dq.astype(BF16)
            return carry

        lax.fori_loop(0, n_rows, row, 0)
        dk_ref[...] = dk_acc[...].astype(BF16)
        dv_ref[...] = dv_acc[...].astype(BF16)

    blk = lambda off: pl.BlockSpec((seq, LANES), lambda p, b: (b, off + p))
    out = jax.ShapeDtypeStruct((t, NA_WIDTH), BF16)
    return pl.pallas_call(
        body, name="na_backward", grid=(n_pairs, batch),
        out_shape=(out, out, out, jax.ShapeDtypeStruct(tiles.shape, F32)),
        in_specs=[blk(0), blk(n_pairs), blk(2 * n_pairs), blk(0),
                  pl.BlockSpec((2, n_tiles, GRID_W, LANES), lambda p, b: (p, 0, 0, 0))],
        out_specs=(blk(0), blk(0), blk(0), pl.BlockSpec((2, n_tiles, GRID_W, LANES), lambda p, b: (p, 0, 0, 0))),
        scratch_shapes=[pltpu.VMEM((2, seq, LANES), BF16), pltpu.VMEM((2, seq, LANES), BF16),
                        pltpu.VMEM((seq, LANES), F32), pltpu.VMEM((seq, LANES), F32)],
        compiler_params=_cparams(("arbitrary", "arbitrary")),
    )(proj, proj, proj, d_o, tiles)


def _na_bias_grad(dtiles):
    n_heads, n_tiles = dtiles.shape[0], dtiles.shape[1]
    n_dc = 2 * NA_COLS - 1

    def body(t_ref, o_ref):
        q = lax.broadcasted_iota(jnp.int32, (GRID_W, LANES), 0)
        lane = lax.broadcasted_iota(jnp.int32, (GRID_W, LANES), 1)
        diag = (lane % GRID_W) - q + (NA_COLS - 1)
        orow = lax.broadcasted_iota(jnp.int32, (n_dc + 1, LANES), 0)
        for dd in range(n_tiles):
            tile = t_ref[0, dd]

            def step(dc, acc, tile=tile):
                part = jnp.sum(jnp.where(diag == dc, tile, 0.0), axis=0, keepdims=True)
                return jnp.where(orow == dc, part, acc)

            o_ref[0, dd] = lax.fori_loop(0, n_dc, step, jnp.zeros((n_dc + 1, LANES), F32))

    return pl.pallas_call(
        body, name="na_bias_grad", grid=(n_heads,),
        out_shape=jax.ShapeDtypeStruct((n_heads, n_tiles, n_dc + 1, LANES), F32),
        in_specs=[pl.BlockSpec((1, n_tiles, GRID_W, LANES), lambda h: (h, 0, 0, 0))],
        out_specs=pl.BlockSpec((1, n_tiles, n_dc + 1, LANES), lambda h: (h, 0, 0, 0)),
        compiler_params=_cparams(("arbitrary",)),
    )(dtiles)


def _sw_backward(proj, d_o, sink, batch, seq):
    t = proj.shape[0]
    n_pairs = SW_WIDTH // LANES
    q_blk = 3 * NA_WIDTH // LANES
    k_blk = q_blk + n_pairs
    n_blocks = seq // SW_BLOCK
    pad = seq + 2 * SW_BLOCK

    def body(sink_ref, q_ref, k_ref, v_ref, do_ref, dq_ref, dk_ref, dv_ref, dsk_ref,
             k_lo, k_hi, v_lo, v_hi, dk_loc, dv_loc, dk_tot, dv_tot):
        hp = pl.program_id(1)
        g = hp // 2
        _sw_prepare(k_ref, g, k_lo, k_hi, seq)
        _sw_prepare(v_ref, g, v_lo, v_hi, seq)
        dk_loc[...] = jnp.zeros_like(dk_loc)
        dv_loc[...] = jnp.zeros_like(dv_loc)

        @pl.when(hp == 0)
        def _():
            dk_tot[...] = jnp.zeros_like(dk_tot)
            dv_tot[...] = jnp.zeros_like(dv_tot)

        lane = lax.broadcasted_iota(jnp.int32, (SW_BLOCK, LANES), 1)

        def block(n, carry):
            rows = pl.ds(pl.multiple_of(n * SW_BLOCK, SW_BLOCK), SW_BLOCK)
            wrows = pl.ds(pl.multiple_of(n * SW_BLOCK, SW_BLOCK), 3 * SW_BLOCK)
            qb = q_ref[rows, :]
            do = do_ref[rows, :]
            ok = _sw_mask(n, seq)
            dq = jnp.zeros((SW_BLOCK, LANES), F32)
            new = []
            for i, (kr, vr) in enumerate(((k_lo, v_lo), (k_hi, v_hi))):
                mine = (lane // HEAD_DIM) == i
                kk = kr[wrows, :]
                p, ps = _sw_probs(qb, kk, ok, sink_ref[2 * hp + i])
                dp = _mm_nt(do, vr[wrows, :])
                delta = jnp.sum(p * dp, axis=-1, keepdims=True)
                ds = p * (dp - delta)
                new.append(carry[i] - ps * delta)
                dsb = (ds * QK_SCALE).astype(BF16)
                dq = dq + _mm(dsb, kk)
                dk_loc[wrows, :] += _mm_tn(dsb, jnp.where(mine, qb, jnp.zeros_like(qb)))
                dv_loc[wrows, :] += _mm_tn(p.astype(BF16), jnp.where(mine, do, jnp.zeros_like(do)))
            dq_ref[rows, :] = dq
            return tuple(new)

        zero = jnp.zeros((SW_BLOCK, 1), F32)
        s0, s1 = lax.fori_loop(0, n_blocks, block, (zero, zero))
        row = lax.broadcasted_iota(jnp.int32, (SUBLANES, LANES), 0)
        dsk_ref[0, 0] = jnp.where(row == 0, jnp.sum(s0), jnp.where(row == 1, jnp.sum(s1), 0.0))

        lane_s = lax.broadcasted_iota(jnp.int32, (seq, LANES), 1)
        mine_g = (lane_s // HEAD_DIM) == g
        for loc, tot in ((dk_loc, dk_tot), (dv_loc, dv_tot)):
            part = loc[SW_BLOCK:SW_BLOCK + seq, :]
            tot[...] += jnp.where(mine_g, part + pltpu.roll(part, HEAD_DIM, 1), 0.0)

        @pl.when(hp == n_pairs - 1)
        def _():
            dk_ref[...] = dk_tot[...]
            dv_ref[...] = dv_tot[...].astype(BF16)

    return pl.pallas_call(
        body, name="sw_backward", grid=(batch, n_pairs),
        out_shape=(jax.ShapeDtypeStruct((t, SW_WIDTH), F32), jax.ShapeDtypeStruct((t, LANES), F32),
                   jax.ShapeDtypeStruct((t, LANES), BF16), jax.ShapeDtypeStruct((batch, n_pairs, SUBLANES, LANES), F32)),
        in_specs=[pl.BlockSpec(memory_space=pltpu.SMEM),
                  pl.BlockSpec((seq, LANES), lambda b, p: (b, q_blk + p)),
                  pl.BlockSpec((seq, LANES), lambda b, p: (b, k_blk)),
                  pl.BlockSpec((seq, LANES), lambda b, p: (b, k_blk + 1)),
                  pl.BlockSpec((seq, LANES), lambda b, p: (b, p))],
        out_specs=(pl.BlockSpec((seq, LANES), lambda b, p: (b, p)), pl.BlockSpec((seq, LANES), lambda b, p: (b, 0)),
                   pl.BlockSpec((seq, LANES), lambda b, p: (b, 0)),
                   pl.BlockSpec((1, 1, SUBLANES, LANES), lambda b, p: (b, p, 0, 0))),
        scratch_shapes=[pltpu.VMEM((pad, LANES), BF16)] * 4 + [pltpu.VMEM((pad, LANES), F32)] * 2
        + [pltpu.VMEM((seq, LANES), F32)] * 2,
        compiler_params=_cparams(("arbitrary", "arbitrary")),
    )(sink, proj, proj, proj, d_o)


def _in_backward(dqkv_a, dq_b, dk_b, dv_b, w_in_t, h1, x, mod3, g_attn, dx1, cos_t, sin_t, seq):
    t, d = x.shape
    tm = TOKEN_TILE // 2
    per_seq = seq // tm
    batch = t // seq
    dqa, dka, dva = dqkv_a
    n_q = SW_WIDTH // LANES

    def body(dqa_ref, dka_ref, dva_ref, dqb_ref, dkb_ref, dvb_ref, w_ref, h_ref, x_ref, mod_ref, g_ref, dx1_ref,
             cos_ref, sin_ref, dx_ref, gw_ref, dsh_ref, dsc_ref, dg_ref):
        i = pl.program_id(0)

        @pl.when(i == 0)
        def _():
            gw_ref[...] = jnp.zeros_like(gw_ref)
            dg_ref[...] = jnp.zeros_like(dg_ref)

        @pl.when(i % per_seq == 0)
        def _():
            dsh_ref[...] = jnp.zeros_like(dsh_ref)
            dsc_ref[...] = jnp.zeros_like(dsc_ref)

        dr = jnp.concatenate([dqb_ref[...], dkb_ref[...]], axis=1)
        cos = jnp.concatenate([cos_ref[...]] * (n_q + 1), axis=1)
        sin = jnp.concatenate([sin_ref[...]] * (n_q + 1), axis=1)
        dr = dr * cos + _rope_rot(dr * sin)
        dproj = jnp.concatenate([dqa_ref[...], dka_ref[...], dva_ref[...], dr.astype(BF16), dvb_ref[...]], axis=1)
        gw_ref[...] += _mm_tn(dproj, h_ref[...])
        dh = _mm(dproj, w_ref[...])
        scale = mod_ref[0, :, d:2 * d]
        r, xn = _rms_stats(x_ref[...])
        xg = xn * g_ref[...]
        dxg = dh * (1.0 + scale)
        dx_ref[...] = dx1_ref[...] + _rms_bwd(dxg * g_ref[...], xn, r)
        dg_ref[...] += jnp.sum(dxg * xn, axis=0, keepdims=True)
        dsh_ref[0] += jnp.sum(dh, axis=0, keepdims=True)
        dsc_ref[0] += jnp.sum(dh * xg, axis=0, keepdims=True)

    tile = lambda w: pl.BlockSpec((tm, w), lambda i: (i, 0))
    per_b = pl.BlockSpec((1, 1, d), lambda i: (i // per_seq, 0, 0))
    small = jax.ShapeDtypeStruct((batch, 1, d), F32)
    rope = pl.BlockSpec((tm, LANES), lambda i: (i % per_seq, 0))
    return pl.pallas_call(
        body, name="in_backward", grid=(t // tm,),
        out_shape=(jax.ShapeDtypeStruct((t, d), F32), jax.ShapeDtypeStruct((IN_WIDTH, d), F32), small, small,
                   jax.ShapeDtypeStruct((1, d), F32)),
        in_specs=[tile(NA_WIDTH), tile(NA_WIDTH), tile(NA_WIDTH), tile(SW_WIDTH), tile(LANES), tile(LANES),
                  pl.BlockSpec((IN_WIDTH, d), lambda i: (0, 0)), tile(d), tile(d),
                  pl.BlockSpec((1, 1, 6 * d), lambda i: (i // per_seq, 0, 0)),
                  pl.BlockSpec((1, d), lambda i: (0, 0)), tile(d), rope, rope],
        out_specs=(tile(d), pl.BlockSpec((IN_WIDTH, d), lambda i: (0, 0)), per_b, per_b,
                   pl.BlockSpec((1, d), lambda i: (0, 0))),
        compiler_params=_cparams(("arbitrary",), VMEM_BIG),
    )(dqa, dka, dva, dq_b, dk_b, dv_b, w_in_t, h1, x, mod3, g_attn, dx1, cos_t, sin_t)


def _ada_weight_grad(sc_all, dmod_cols):
    d = sc_all.shape[1]
    ncol = dmod_cols.shape[1]

    def body(s_ref, m_ref, o_ref):
        o_ref[...] = _mm_tn(s_ref[...].astype(BF16), m_ref[...].astype(BF16))

    return pl.pallas_call(
        body, name="ada_weight_grad",
        out_shape=jax.ShapeDtypeStruct((d, ncol), F32),
        compiler_params=_cparams(vmem=VMEM_BIG),
    )(sc_all, dmod_cols)


def _row_tile(rows, cols):
    target = max(SUBLANES, (1 << 20) // (4 * cols))
    best = rows
    for cand in range(SUBLANES, rows + 1, SUBLANES):
        if rows % cand == 0 and cand <= target:
            best = cand
    return best if rows % SUBLANES == 0 else rows


def _sum_slots(parts, name):
    _, rows, cols = parts.shape
    tr = _row_tile(rows, cols)

    def body(p_ref, o_ref):
        o_ref[...] = ((p_ref[N_SHARD - 1] + p_ref[0]) + p_ref[1]) + p_ref[2]

    return pl.pallas_call(
        body, name=name, grid=(rows // tr,),
        out_shape=jax.ShapeDtypeStruct((rows, cols), F32),
        in_specs=[pl.BlockSpec((N_SHARD, tr, cols), lambda i: (0, i, 0))],
        out_specs=pl.BlockSpec((tr, cols), lambda i: (i, 0)),
        compiler_params=_cparams(("arbitrary",)),
    )(parts)


def _adamw(w, grads, m, v, name):
    rows, cols = w.shape
    tr = _row_tile(rows, cols)
    ng = len(grads)

    def body(*refs):
        w_ref = refs[0]
        g_refs = refs[1:1 + ng]
        m_ref, v_ref = refs[1 + ng], refs[2 + ng]
        g_out, d_out, m_out, v_out = refs[3 + ng:]
        g = g_refs[0][...]
        for extra in g_refs[1:]:
            g = g + extra[...]
        g_out[...] = g
        m2 = ADAM_B1 * m_ref[...] + (1.0 - ADAM_B1) * g
        v2 = ADAM_B2 * v_ref[...] + (1.0 - ADAM_B2) * (g * g)
        m_out[...] = m2
        v_out[...] = v2
        m_hat = m2 / (1.0 - ADAM_B1 ** ADAM_STEP)
        v_hat = v2 / (1.0 - ADAM_B2 ** ADAM_STEP)
        d_out[...] = -ADAM_LR * (m_hat / (jnp.sqrt(v_hat) + ADAM_EPS) + ADAM_WD * w_ref[...])

    spec = pl.BlockSpec((tr, cols), lambda i: (i, 0))
    out = jax.ShapeDtypeStruct((rows, cols), F32)
    return pl.pallas_call(
        body, name=name, grid=(rows // tr,),
        out_shape=(out, out, out, out),
        in_specs=[spec] * (3 + ng), out_specs=(spec, spec, spec, spec),
        compiler_params=_cparams(("arbitrary",)),
    )(w, *grads, m, v)


def _pack_rows(arrays):
    tile = SUBLANES * LANES
    rows, offsets, at = [], [], 0
    for a in arrays:
        flat = a.reshape(-1).astype(F32)
        n = -(-flat.shape[0] // tile) * tile
        rows.append(jnp.pad(flat, (0, n - flat.shape[0])).reshape(-1, LANES))
        offsets.append(at)
        at += n // LANES
    return jnp.concatenate(rows, axis=0), offsets


def _unpack_rows(packed, offsets, shapes):
    out = []
    for off, shape in zip(offsets, shapes):
        n = 1
        for s in shape:
            n *= s
        nrow = -(-n // LANES)
        out.append(packed[off:off + nrow].reshape(-1)[:n].reshape(shape))
    return out


def _rope_tables(seq):
    half = HEAD_DIM // 2
    inv = ROPE_THETA ** (-jnp.arange(half, dtype=F32) / half)
    ang = jnp.arange(seq).astype(F32)[:, None] * inv[None, :]
    cos, sin = jnp.cos(ang), jnp.sin(ang)
    cos_t = jnp.concatenate([cos, cos, cos, cos], axis=1)
    sin_t = jnp.concatenate([-sin, sin, -sin, sin], axis=1)
    return cos_t, sin_t


def kernel(x, c, w_ada, b_ada, g_attn, w_in, na_rpb, sw_sink, g_na_out, g_sw_out, w_out, g_ffn, w_up, conv_w, conv_b, w_down, g_final, loss_target, m_w_ada, m_b_ada, m_g_attn, m_w_in, m_na_rpb, m_sw_sink, m_g_na_out, m_g_sw_out, m_w_out, m_g_ffn, m_w_up, m_conv_w, m_conv_b, m_w_down, m_g_final, v_w_ada, v_b_ada, v_g_attn, v_w_in, v_na_rpb, v_sw_sink, v_g_na_out, v_g_sw_out, v_w_out, v_g_ffn, v_w_up, v_conv_w, v_conv_b, v_w_down, v_g_final):
    batch, seq, d = x.shape
    t = batch * seq
    assert d == D_MODEL and seq % (NA_ROWS * GRID_W) == 0 and seq % TOKEN_TILE == 0 and batch <= SUBLANES
    shard = 2 * lax.axis_index("x") + lax.axis_index("y")
    xt = x.reshape(t, d)
    tgt = loss_target.reshape(t, d)

    c8 = jnp.pad(c, ((0, SUBLANES - batch), (0, 0)))
    mod8, sc_all = _ada_forward(c8, w_ada[0], b_ada)
    mod3 = mod8[:batch].reshape(batch, 1, 6 * d)
    w_in_t_s = jnp.transpose(w_in[0]).astype(BF16)
    gathered = _gather_shards([w_in_t_s, w_out[0].astype(BF16), w_up[0].astype(BF16), w_down[0].astype(BF16),
                               conv_w[0]])
    w_in_t = gathered[0].reshape(IN_WIDTH, d)
    w_out_f = gathered[1].reshape(d, d)
    w_up_f = gathered[2]
    w_down_f = gathered[3].reshape(D_FF, d)
    conv_w_f = jnp.transpose(gathered[4], (1, 0, 2)).reshape(3, D_FF)

    cos_t, sin_t = _rope_tables(seq)
    h1, proj = _in_proj(xt, mod3, g_attn, w_in_t, cos_t, sin_t, seq)
    n_heads = NA_WIDTH // HEAD_DIM
    tiles = _na_bias_tiles(na_rpb[0].reshape(n_heads * (2 * NA_ROWS - 1), 2 * NA_COLS - 1))
    sink = sw_sink[0]
    oa = _na_forward(proj, tiles, batch, seq)
    ob = _sw_forward(proj, sink, batch, seq)
    oab, mix, x1, h2 = _out_proj(oa, ob, g_na_out, g_sw_out, w_out_f, xt, mod3, g_ffn, seq)
    u = _up_proj(h2, w_up_f)
    a = _conv_gate(u, conv_w_f, conv_b, batch, seq)
    dx2, dffn, loss_part, dgate_f, dg_final = _down_and_loss(a, w_down_f, x1, mod3, g_final.reshape(1, d), tgt, seq)

    du, gw_down, gconv_w, gconv_b = _ffn_backward(dffn, w_down_f, u, conv_w_f, conv_b, batch, seq)
    dx1, dmix, dshift_f, dscale_f, dgate_a, dg_ffn = _up_backward(du, w_up_f, x1, mod3, g_ffn, dx2, mix, seq)
    gw_up = _up_weight_grad(h2, du)
    doa, dob, gw_out, dg_na, dg_sw = _out_backward(dmix, w_out_f, oab, oa, ob, g_na_out, g_sw_out)
    dqa, dka, dva, dtiles = _na_backward(proj, doa, tiles, batch, seq)
    dq_b, dk_b, dv_b, dsink_parts = _sw_backward(proj, dob, sink, batch, seq)
    gx, gw_in_t, dshift_a, dscale_a, dg_attn = _in_backward(
        (dqa, dka, dva), dq_b, dk_b, dv_b, w_in_t, h1, xt, mod3, g_attn, dx1, cos_t, sin_t, seq)

    red = jnp.sum(_na_bias_grad(dtiles)[:, :, :2 * NA_COLS - 1, :].reshape(n_heads, 2 * NA_ROWS - 2, 2 * NA_COLS - 1, 2,
                                                                        GRID_W), axis=-1)
    zero_row = jnp.zeros((n_heads, 1, 2 * NA_COLS - 1), F32)
    g_rpb = jnp.concatenate([red[..., 0], zero_row], axis=1) + jnp.concatenate([zero_row, red[..., 1]], axis=1)
    g_sink = jnp.sum(dsink_parts[:, :, :2, 0], axis=0).reshape(SW_WIDTH // HEAD_DIM)

    dmod = jnp.concatenate([dshift_a, dscale_a, dgate_a, dshift_f, dscale_f, dgate_f], axis=2).reshape(batch, 6 * d)
    dmod8 = jnp.pad(dmod, ((0, SUBLANES - batch), (0, 0)))
    small_parts = [jnp.sum(dmod, axis=0), dg_attn, g_rpb, g_sink, dg_na, dg_sw, dg_ffn, gconv_w, gconv_b, dg_final,
                   loss_part[0, 0:1]]
    packed, offsets = _pack_rows(small_parts + [dmod8])
    summed, every = _allreduce_small(packed)
    small_shapes = [(1, 6 * d), (1, d), na_rpb.shape, sw_sink.shape, (1, NA_WIDTH), (1, SW_WIDTH), (1, d),
                    (3, D_FF), (1, D_FF), (d,), ()]
    (g_b_ada, g_g_attn, g_na_rpb, g_sw_sink, g_g_na, g_g_sw, g_g_ffn, g_conv_w_full, g_conv_b, g_g_final,
     loss) = _unpack_rows(summed, offsets[:-1], small_shapes)
    dmod_rows = every[:, offsets[-1]:offsets[-1] + SUBLANES * 6 * d // LANES, :].reshape(N_DEV * SUBLANES, 6 * d)
    ncol = w_ada.shape[2]
    g_w_ada = _ada_weight_grad(sc_all, lax.dynamic_slice(dmod_rows, (0, shard * ncol), (N_DEV * SUBLANES, ncol)))
    cshard = conv_w.shape[2]
    g_conv_w = lax.dynamic_slice(g_conv_w_full, (0, shard * cshard), (3, cshard)).reshape(conv_w.shape)

    n_in = IN_WIDTH // N_SHARD
    parts = _scatter_partials([gw_in_t.reshape(N_SHARD, n_in, d), gw_out.reshape(N_SHARD, d // N_SHARD, d), gw_up,
                               gw_down.reshape(N_SHARD, D_FF // N_SHARD, d)])
    mine = [_sum_slots(p, name) for p, name in zip(parts, ("sum_w_in", "sum_w_out", "sum_w_up", "sum_w_down"))]
    theirs = _swap_with_sibling(mine)

    def big(w, m, v, g_parts, name):
        shape = w.shape
        outs = _adamw(w[0], g_parts, m[0], v[0], name)
        return [o.reshape(shape) for o in outs]

    r_w_ada = big(w_ada, m_w_ada, v_w_ada, [g_w_ada], "adamw_w_ada")
    r_w_in = big(w_in, m_w_in, v_w_in, [jnp.transpose(mine[0]), jnp.transpose(theirs[0])], "adamw_w_in")
    r_w_out = big(w_out, m_w_out, v_w_out, [mine[1], theirs[1]], "adamw_w_out")
    r_w_up = big(w_up, m_w_up, v_w_up, [mine[2], theirs[2]], "adamw_w_up")
    r_w_down = big(w_down, m_w_down, v_w_down, [mine[3], theirs[3]], "adamw_w_down")

    small_w = [b_ada, g_attn, na_rpb, sw_sink, g_na_out, g_sw_out, g_ffn, conv_w, conv_b, g_final]
    small_m = [m_b_ada, m_g_attn, m_na_rpb, m_sw_sink, m_g_na_out, m_g_sw_out, m_g_ffn, m_conv_w, m_conv_b, m_g_final]
    small_v = [v_b_ada, v_g_attn, v_na_rpb, v_sw_sink, v_g_na_out, v_g_sw_out, v_g_ffn, v_conv_w, v_conv_b, v_g_final]
    small_g = [g_b_ada, g_g_attn, g_na_rpb, g_sw_sink, g_g_na, g_g_sw, g_g_ffn, g_conv_w, g_conv_b, g_g_final]
    pw, offs = _pack_rows(small_w)
    pg, _ = _pack_rows(small_g)
    pm, _ = _pack_rows(small_m)
    pv, _ = _pack_rows(small_v)
    shapes = [w.shape for w in small_w]
    r_small = [_unpack_rows(o, offs, shapes) for o in _adamw(pw, [pg], pm, pv, "adamw_small")]

    def pick(k):
        b_, ga_, rpb_, sk_, gna_, gsw_, gf_, cw_, cb_, gfin_ = r_small[k]
        return [r_w_ada[k], b_, ga_, r_w_in[k], rpb_, sk_, gna_, gsw_, r_w_out[k], gf_, r_w_up[k], cw_, cb_,
                r_w_down[k], gfin_]

    return (loss, gx.reshape(batch, seq, d), *pick(0), *pick(1), *pick(2), *pick(3))
```

```python
import functools

import jax
import jax.numpy as jnp
from jax import lax
from jax.experimental import pallas as pl
from jax.experimental.pallas import tpu as pltpu

F32 = jnp.float32
BF16 = jnp.bfloat16
MESH = pl.DeviceIdType.MESH

D_MODEL = 1024
HEAD_DIM = 64
NA_WIDTH = 512
SW_WIDTH = 512
SW_KV_WIDTH = 128
IN_WIDTH = 2304
D_FF = 2816
GRID_W = 64
NA_ROWS = 8
NA_COLS = 16
SW_BLOCK = 128
ROPE_THETA = 10000.0
EPS = 1e-6
NEG = -1e30
QK_SCALE = HEAD_DIM ** -0.5

ADAM_LR = 0.001
ADAM_B1 = 0.9
ADAM_B2 = 0.999
ADAM_EPS = 1e-08
ADAM_WD = 0.01
ADAM_STEP = 10

N_SHARD = 4
N_DEV = 8
LANES = 128
SUBLANES = 8
TOKEN_TILE = 512
FF_TILE = 256
CONV_CHUNK = 64
VMEM_BIG = 56 * 1024 * 1024


def _mm(a, b):
    return jnp.dot(a, b, preferred_element_type=F32)


def _mm_nt(a, b):
    return lax.dot_general(a, b, (((1,), (1,)), ((), ())), preferred_element_type=F32)


def _mm_tn(a, b):
    return lax.dot_general(a, b, (((0,), (0,)), ((), ())), preferred_element_type=F32)


def _cparams(sem=None, vmem=None):
    kw = {}
    if sem is not None:
        kw["dimension_semantics"] = sem
    if vmem is not None:
        kw["vmem_limit_bytes"] = vmem
    return pltpu.CompilerParams(**kw)


def _sigmoid(x):
    return 1.0 / (1.0 + jnp.exp(-x))


def _rms_stats(x):
    r = lax.rsqrt(jnp.mean(x * x, axis=-1, keepdims=True) + EPS)
    return r, x * r


def _rms_bwd(dxn, xn, r):
    return r * (dxn - xn * jnp.mean(dxn * xn, axis=-1, keepdims=True))


def _my_pos():
    return lax.axis_index("x"), lax.axis_index("y"), lax.axis_index("c")


def _flip(v, bit):
    return 1 - v if bit else v


def _ada_forward(c8, w_ada, b_ada, rider):
    d = c8.shape[1]
    ncol = w_ada.shape[1]

    def body(c_ref, w_ref, b_ref, mod_ref, sc_ref, m_scr, mod_buf, ssem, rsem, ssem2, rsem2):
        x, y, c = _my_pos()
        me = 4 * x + 2 * y + c
        shard = 2 * x + y
        cv = c_ref[...]
        my_rows = pl.ds(pl.multiple_of(me * SUBLANES, SUBLANES), SUBLANES)
        sc_ref[my_rows, :] = cv * _sigmoid(cv)

        def copy1(k):
            peer = (_flip(x, (k >> 2) & 1), _flip(y, (k >> 1) & 1), _flip(c, k & 1))
            return pltpu.make_async_remote_copy(
                src_ref=sc_ref.at[my_rows, :], dst_ref=sc_ref.at[my_rows, :],
                send_sem=ssem.at[k - 1], recv_sem=rsem.at[k - 1], device_id=peer, device_id_type=MESH)

        sends = [copy1(k) for k in range(1, N_DEV)]
        for cp in sends:
            cp.start()
        for cp in sends:
            cp.wait_recv()
        m_scr[...] = _mm(sc_ref[...].astype(BF16), w_ref[...].astype(BF16))

        def copy2(k):
            px, py = _flip(x, (k >> 1) & 1), _flip(y, k & 1)
            rows = pl.ds(pl.multiple_of((4 * px + 2 * py + c) * SUBLANES, SUBLANES), SUBLANES)
            return pltpu.make_async_remote_copy(
                src_ref=m_scr.at[rows, :], dst_ref=mod_buf.at[shard],
                send_sem=ssem2.at[k - 1], recv_sem=rsem2.at[k - 1], device_id=(px, py, c), device_id_type=MESH)

        sends2 = [copy2(k) for k in range(1, N_SHARD)]
        for cp in sends2:
            cp.start()
        mod_buf[shard] = m_scr[my_rows, :]
        for cp in sends2:
            cp.wait_recv()
        for s in range(N_SHARD):
            mod_ref[:, s * ncol:(s + 1) * ncol] = mod_buf[s] + b_ref[:, s * ncol:(s + 1) * ncol]
        for cp in sends + sends2:
            cp.wait_send()

    vm = pl.BlockSpec(memory_space=pltpu.VMEM)
    return _hosted(
        body, rider, name="ada_forward", grid=(),
        out_shape=(jax.ShapeDtypeStruct((SUBLANES, N_SHARD * ncol), F32),
                   jax.ShapeDtypeStruct((N_DEV * SUBLANES, d), F32)),
        in_specs=[vm, vm, vm], out_specs=(vm, vm),
        scratch_shapes=[pltpu.VMEM((N_DEV * SUBLANES, ncol), F32), pltpu.VMEM((N_SHARD, SUBLANES, ncol), F32),
                        pltpu.SemaphoreType.DMA((N_DEV - 1,)), pltpu.SemaphoreType.DMA((N_DEV - 1,)),
                        pltpu.SemaphoreType.DMA((N_SHARD - 1,)), pltpu.SemaphoreType.DMA((N_SHARD - 1,))],
        compiler_params=_cparams(vmem=VMEM_BIG), args=[c8, w_ada, b_ada])


class _Rider:
    def __init__(self, kind, srcs, owns=()):
        self.kind, self.srcs, self.owns = kind, list(srcs), list(owns)
        n = len(self.srcs)
        sds = jax.ShapeDtypeStruct
        dma = pltpu.SemaphoreType.DMA
        if kind == "gather":
            self.out_shapes = [sds((N_SHARD,) + s.shape, s.dtype) for s in self.srcs]
            self.sems = [dma((n, N_SHARD - 1)), dma((n, N_SHARD - 1)), dma((n,))]
        elif kind == "scatter":
            self.out_shapes = ([sds((N_SHARD - 1,) + s.shape[1:], s.dtype) for s in self.srcs]
                               + [sds(o.shape[1:], o.dtype) for o in self.owns])
            self.sems = [dma((n, N_SHARD - 1)), dma((n, N_SHARD - 1)), dma((max(len(self.owns), 1),))]
        else:
            self.out_shapes = [sds(s.shape, s.dtype) for s in self.srcs]
            self.sems = [dma((n,)), dma((n,))]

    @property
    def inputs(self):
        return self.srcs + self.owns

    def copies(self, ins, outs, sems):
        n = len(self.srcs)
        x, y, c = _my_pos()
        shard = 2 * x + y
        local, remote = [], []
        if self.kind == "swap":
            ssem, rsem = sems
            for i in range(n):
                remote.append(pltpu.make_async_remote_copy(
                    src_ref=ins[i], dst_ref=outs[i], send_sem=ssem.at[i], recv_sem=rsem.at[i],
                    device_id=(x, y, 1 - c), device_id_type=MESH))
            return local, remote
        ssem, rsem, lsem = sems
        for i in range(n):
            if self.kind == "gather":
                local.append(pltpu.make_async_copy(ins[i], outs[i].at[shard], lsem.at[i]))
            for k in range(1, N_SHARD):
                px, py = _flip(x, (k >> 1) & 1), _flip(y, k & 1)
                if self.kind == "gather":
                    src, dst = ins[i], outs[i].at[shard]
                else:
                    src, dst = ins[i].at[2 * px + py], outs[i].at[k - 1]
                remote.append(pltpu.make_async_remote_copy(
                    src_ref=src, dst_ref=dst, send_sem=ssem.at[i, k - 1], recv_sem=rsem.at[i, k - 1],
                    device_id=(px, py, c), device_id_type=MESH))
        if self.kind == "scatter":
            for i in range(len(self.owns)):
                local.append(pltpu.make_async_copy(ins[n + i].at[shard], outs[n + i], lsem.at[i]))
        return local, remote

    def start(self, ins, outs, sems):
        local, remote = self.copies(ins, outs, sems)
        for cp in local + remote:
            cp.start()

    def wait(self, ins, outs, sems):
        local, remote = self.copies(ins, outs, sems)
        for cp in remote:
            cp.wait_recv()
        for cp in remote:
            cp.wait_send()
        for cp in local:
            cp.wait()


def _hosted(body, rider, *, name, grid, out_shape, in_specs, out_specs, scratch_shapes, compiler_params, args):
    out_shape, out_specs = list(out_shape), list(out_specs)
    if rider is None:
        outs = pl.pallas_call(body, name=name, grid=grid, out_shape=tuple(out_shape), in_specs=list(in_specs),
                              out_specs=tuple(out_specs), scratch_shapes=list(scratch_shapes),
                              compiler_params=compiler_params)(*args)
        return list(outs), []
    n_in, n_out, n_scr = len(in_specs), len(out_shape), len(scratch_shapes)
    nr_in, nr_out = len(rider.inputs), len(rider.out_shapes)

    def full(*refs):
        ins, refs = refs[:n_in], refs[n_in:]
        r_in, refs = refs[:nr_in], refs[nr_in:]
        outs, refs = refs[:n_out], refs[n_out:]
        r_out, refs = refs[:nr_out], refs[nr_out:]
        scr, sems = refs[:n_scr], refs[n_scr:]
        if grid:
            first = last = None
            for ax, size in enumerate(grid):
                f, l = pl.program_id(ax) == 0, pl.program_id(ax) == size - 1
                first = f if first is None else jnp.logical_and(first, f)
                last = l if last is None else jnp.logical_and(last, l)
            pl.when(first)(lambda: rider.start(r_in, r_out, sems))
            body(*ins, *outs, *scr)
            pl.when(last)(lambda: rider.wait(r_in, r_out, sems))
        else:
            rider.start(r_in, r_out, sems)
            body(*ins, *outs, *scr)
            rider.wait(r_in, r_out, sems)

    hbm = pl.BlockSpec(memory_space=pl.ANY)
    res = pl.pallas_call(
        full, name=name, grid=grid, out_shape=tuple(out_shape + rider.out_shapes),
        in_specs=list(in_specs) + [hbm] * nr_in, out_specs=tuple(out_specs + [hbm] * nr_out),
        scratch_shapes=list(scratch_shapes) + rider.sems, compiler_params=compiler_params,
    )(*args, *rider.inputs)
    return list(res[:n_out]), list(res[n_out:])


def _ride_alone(rider, name):
    return _hosted(lambda: None, rider, name=name, grid=(), out_shape=[], in_specs=[], out_specs=[], scratch_shapes=[],
                   compiler_params=_cparams(), args=[])[1]


def _allreduce_small(packed):
    r = packed.shape[0]

    def body(p_ref, sum_ref, all_ref, ssem, rsem):
        x, y, c = _my_pos()
        me = 4 * x + 2 * y + c
        all_ref[me] = p_ref[...]
        cps = []
        for k in range(1, N_DEV):
            peer = (_flip(x, (k >> 2) & 1), _flip(y, (k >> 1) & 1), _flip(c, k & 1))
            cps.append(pltpu.make_async_remote_copy(
                src_ref=all_ref.at[me], dst_ref=all_ref.at[me], send_sem=ssem.at[k - 1], recv_sem=rsem.at[k - 1],
                device_id=peer, device_id_type=MESH))
        for cp in cps:
            cp.start()
        for cp in cps:
            cp.wait_recv()
        acc = all_ref[0]
        for dev in range(1, N_DEV):
            acc = acc + all_ref[dev]
        sum_ref[...] = acc
        for cp in cps:
            cp.wait_send()

    vm = pl.BlockSpec(memory_space=pltpu.VMEM)
    return pl.pallas_call(
        body, name="allreduce_small",
        out_shape=(jax.ShapeDtypeStruct((r, LANES), F32), jax.ShapeDtypeStruct((N_DEV, r, LANES), F32)),
        in_specs=[vm], out_specs=(vm, vm),
        scratch_shapes=[pltpu.SemaphoreType.DMA((N_DEV - 1,)), pltpu.SemaphoreType.DMA((N_DEV - 1,))],
    )(packed)


def _rope_rot(t):
    w = t.shape[1]
    lane = lax.broadcasted_iota(jnp.int32, t.shape, 1)
    first = (lane % HEAD_DIM) < (HEAD_DIM // 2)
    return jnp.where(first, pltpu.roll(t, w - HEAD_DIM // 2, 1), pltpu.roll(t, HEAD_DIM // 2, 1))


def _in_proj(x, mod3, g_attn, w_in_t, cos_t, sin_t, seq):
    t, d = x.shape
    tm = TOKEN_TILE
    per_seq = seq // tm
    rope_lo, rope_hi = 3 * NA_WIDTH, 3 * NA_WIDTH + SW_WIDTH + SW_KV_WIDTH
    n_rep = (rope_hi - rope_lo) // LANES

    def body(x_ref, mod_ref, g_ref, w_ref, cos_ref, sin_ref, h_ref, p_ref):
        r, xn = _rms_stats(x_ref[...])
        shift, scale = mod_ref[0, :, 0:d], mod_ref[0, :, d:2 * d]
        hb = ((xn * g_ref[...]) * (1.0 + scale) + shift).astype(BF16)
        h_ref[...] = hb
        p_ref[:, :rope_lo] = _mm_nt(hb, w_ref[:rope_lo, :]).astype(BF16)
        pr = _mm_nt(hb, w_ref[rope_lo:rope_hi, :])
        cos = jnp.concatenate([cos_ref[...]] * n_rep, axis=1)
        sin = jnp.concatenate([sin_ref[...]] * n_rep, axis=1)
        p_ref[:, rope_lo:rope_hi] = (pr * cos + _rope_rot(pr) * sin).astype(BF16)
        p_ref[:, rope_hi:] = _mm_nt(hb, w_ref[rope_hi:, :]).astype(BF16)

    return pl.pallas_call(
        body, name="in_proj", grid=(t // tm,),
        out_shape=(jax.ShapeDtypeStruct((t, d), BF16), jax.ShapeDtypeStruct((t, IN_WIDTH), BF16)),
        in_specs=[pl.BlockSpec((tm, d), lambda i: (i, 0)),
                  pl.BlockSpec((1, 1, 6 * d), lambda i: (i // per_seq, 0, 0)),
                  pl.BlockSpec((1, d), lambda i: (0, 0)),
                  pl.BlockSpec((IN_WIDTH, d), lambda i: (0, 0)),
                  pl.BlockSpec((tm, LANES), lambda i: (i % per_seq, 0)),
                  pl.BlockSpec((tm, LANES), lambda i: (i % per_seq, 0))],
        out_specs=(pl.BlockSpec((tm, d), lambda i: (i, 0)), pl.BlockSpec((tm, IN_WIDTH), lambda i: (i, 0))),
        compiler_params=_cparams(("arbitrary",), VMEM_BIG),
    )(x, mod3, g_attn, w_in_t, cos_t, sin_t)


def _na_bias_pattern():
    n_dc = 2 * NA_COLS - 1
    j = lax.broadcasted_iota(jnp.int32, (GRID_W, GRID_W * LANES), 0)
    m = lax.broadcasted_iota(jnp.int32, (GRID_W, GRID_W * LANES), 1)
    q, lane = m // LANES, m % LANES
    k = lane % GRID_W
    cs = jnp.clip(q - NA_COLS // 2, 0, GRID_W - NA_COLS)
    ok = (k >= cs) & (k < cs + NA_COLS)
    hit = ok & (j < 2 * n_dc) & (lane // GRID_W == j // n_dc) & (k - q + (NA_COLS - 1) == j % n_dc)
    return hit.astype(F32), jnp.where(ok[0:1], 0.0, NEG).astype(F32)


def _na_bias_tiles(rows2, expand, mask):
    n, width = rows2.shape[0], expand.shape[1]
    step = 2048

    def body(r_ref, e_ref, m_ref, o_ref):
        o_ref[...] = jnp.dot(r_ref[...], e_ref[...], precision=lax.Precision.HIGHEST,
                             preferred_element_type=F32) + m_ref[...]

    return pl.pallas_call(
        body, name="na_bias_tiles", grid=(width // step,),
        out_shape=jax.ShapeDtypeStruct((n, width), F32),
        in_specs=[pl.BlockSpec(rows2.shape, lambda i: (0, 0)), pl.BlockSpec((expand.shape[0], step), lambda i: (0, i)),
                  pl.BlockSpec((1, step), lambda i: (0, i))],
        out_specs=pl.BlockSpec((n, step), lambda i: (0, i)),
        compiler_params=_cparams(("arbitrary",)),
    )(rows2, expand, mask)


def _na_prepare(k_ref, v_ref, km, vm):
    lane = lax.broadcasted_iota(jnp.int32, k_ref.shape, 1)
    low = lane < HEAD_DIM
    kv = k_ref[...]
    vv = v_ref[...]
    zero = jnp.zeros_like(kv)
    km[0] = jnp.where(low, kv, zero)
    km[1] = jnp.where(low, zero, kv)
    vm[0] = jnp.where(low, vv, zero)
    vm[1] = jnp.where(low, zero, vv)


def _na_window(r, n_rows):
    rs = jnp.clip(r - NA_ROWS // 2, 0, n_rows - NA_ROWS)
    return rs, r - rs


def _na_probs(q, kw, tp_ref, h, off):
    s = _mm_nt(q, kw) * QK_SCALE
    bias = jnp.concatenate([tp_ref[h, 2 * w - off + (NA_ROWS - 1)] for w in range(NA_ROWS // 2)], axis=1)
    s = s + bias
    m = jnp.max(s, axis=-1, keepdims=True)
    e = jnp.exp(s - m)
    return e / jnp.sum(e, axis=-1, keepdims=True)


def _na_forward(proj, tiles, batch, seq, rider=None):
    t = proj.shape[0]
    n_rows = seq // GRID_W
    n_pairs = NA_WIDTH // LANES
    win = NA_ROWS * GRID_W

    def body(q_ref, k_ref, v_ref, tp_ref, o_ref, km, vm):
        _na_prepare(k_ref, v_ref, km, vm)

        def row(r, carry):
            rs, off = _na_window(r, n_rows)
            rows = pl.ds(pl.multiple_of(r * GRID_W, GRID_W), GRID_W)
            wrows = pl.ds(pl.multiple_of(rs * GRID_W, GRID_W), win)
            q = q_ref[rows, :]
            acc = jnp.zeros((GRID_W, LANES), F32)
            for h in range(2):
                p = _na_probs(q, km[h, wrows, :], tp_ref, h, off)
                acc = acc + _mm(p.astype(BF16), vm[h, wrows, :])
            o_ref[rows, :] = acc
            return carry

        lax.fori_loop(0, n_rows, row, 0, unroll=2)

    return _hosted(
        body, rider, name="na_forward", grid=(batch, n_pairs),
        out_shape=[jax.ShapeDtypeStruct((t, NA_WIDTH), F32)],
        in_specs=[pl.BlockSpec((seq, LANES), lambda b, p: (b, p)),
                  pl.BlockSpec((seq, LANES), lambda b, p: (b, n_pairs + p)),
                  pl.BlockSpec((seq, LANES), lambda b, p: (b, 2 * n_pairs + p)),
                  pl.BlockSpec((2, 2 * NA_ROWS - 2, GRID_W, LANES), lambda b, p: (p, 0, 0, 0))],
        out_specs=[pl.BlockSpec((seq, LANES), lambda b, p: (b, p))],
        scratch_shapes=[pltpu.VMEM((2, seq, LANES), BF16), pltpu.VMEM((2, seq, LANES), BF16)],
        compiler_params=_cparams(("arbitrary", "arbitrary")), args=[proj, proj, proj, tiles])


def _sw_prepare(kv_ref, g, dst_lo, dst_hi, seq):
    lane = lax.broadcasted_iota(jnp.int32, kv_ref.shape, 1)
    mine = (lane // HEAD_DIM) == g
    kg = jnp.where(mine, kv_ref[...].astype(F32), 0.0)
    kr = pltpu.roll(kg, HEAD_DIM, 1)
    first = g == 0
    zero = jnp.zeros((SW_BLOCK, LANES), BF16)
    for dst, val in ((dst_lo, jnp.where(first, kg, kr)), (dst_hi, jnp.where(first, kr, kg))):
        dst[0:SW_BLOCK, :] = zero
        dst[SW_BLOCK:SW_BLOCK + seq, :] = val.astype(BF16)
        dst[SW_BLOCK + seq:, :] = zero


def _sw_mask(n, seq):
    qi = lax.broadcasted_iota(jnp.int32, (SW_BLOCK, 3 * SW_BLOCK), 0)
    kj = lax.broadcasted_iota(jnp.int32, (SW_BLOCK, 3 * SW_BLOCK), 1)
    kpos = n * SW_BLOCK - SW_BLOCK + kj
    return (jnp.abs(qi + SW_BLOCK - kj) <= SW_BLOCK) & (kpos >= 0) & (kpos < seq)


def _sw_probs(qb, kk, ok, sk):
    s = jnp.where(ok, _mm_nt(qb, kk) * QK_SCALE, NEG)
    m = jnp.maximum(jnp.max(s, axis=-1, keepdims=True), sk)
    p = jnp.exp(s - m)
    es = jnp.exp(sk - m)
    den = jnp.sum(p, axis=-1, keepdims=True) + es
    return p / den, es / den


def _sw_forward(proj, sink, batch, seq, rider=None):
    t = proj.shape[0]
    n_pairs = SW_WIDTH // LANES
    q_blk = 3 * NA_WIDTH // LANES
    k_blk = q_blk + n_pairs
    n_blocks = seq // SW_BLOCK
    pad = seq + 2 * SW_BLOCK

    def body(sink_ref, q_ref, k_ref, v_ref, o_ref, k_lo, k_hi, v_lo, v_hi):
        hp = pl.program_id(1)
        g = hp // 2
        _sw_prepare(k_ref, g, k_lo, k_hi, seq)
        _sw_prepare(v_ref, g, v_lo, v_hi, seq)

        def block(n, carry):
            rows = pl.ds(pl.multiple_of(n * SW_BLOCK, SW_BLOCK), SW_BLOCK)
            wrows = pl.ds(pl.multiple_of(n * SW_BLOCK, SW_BLOCK), 3 * SW_BLOCK)
            qb = q_ref[rows, :]
            ok = _sw_mask(n, seq)
            acc = jnp.zeros((SW_BLOCK, LANES), F32)
            for i, (kr, vr) in enumerate(((k_lo, v_lo), (k_hi, v_hi))):
                p, _ = _sw_probs(qb, kr[wrows, :], ok, sink_ref[2 * hp + i])
                acc = acc + _mm(p.astype(BF16), vr[wrows, :])
            o_ref[rows, :] = acc
            return carry

        lax.fori_loop(0, n_blocks, block, 0, unroll=2)

    return _hosted(
        body, rider, name="sw_forward", grid=(batch, n_pairs),
        out_shape=[jax.ShapeDtypeStruct((t, SW_WIDTH), F32)],
        in_specs=[pl.BlockSpec(memory_space=pltpu.SMEM),
                  pl.BlockSpec((seq, LANES), lambda b, p: (b, q_blk + p)),
                  pl.BlockSpec((seq, LANES), lambda b, p: (b, k_blk)),
                  pl.BlockSpec((seq, LANES), lambda b, p: (b, k_blk + 1))],
        out_specs=[pl.BlockSpec((seq, LANES), lambda b, p: (b, p))],
        scratch_shapes=[pltpu.VMEM((pad, LANES), BF16)] * 4,
        compiler_params=_cparams(("arbitrary", "arbitrary")), args=[sink, proj, proj, proj])


def _out_proj(oa, ob, g_na, g_sw, w_out, x, mod3, g_ffn, seq):
    t, d = x.shape
    tm = TOKEN_TILE
    per_seq = seq // tm

    def body(oa_ref, ob_ref, gna_ref, gsw_ref, w_ref, x_ref, mod_ref, gf_ref, oab_ref, mix_ref, x1_ref, h2_ref):
        _, na = _rms_stats(oa_ref[...])
        _, nb = _rms_stats(ob_ref[...])
        oab = jnp.concatenate([na * gna_ref[...], nb * gsw_ref[...]], axis=1).astype(BF16)
        oab_ref[...] = oab
        mix = _mm(oab, w_ref[...])
        mix_ref[...] = mix
        gate_a = mod_ref[0, :, 2 * d:3 * d]
        shift_f, scale_f = mod_ref[0, :, 3 * d:4 * d], mod_ref[0, :, 4 * d:5 * d]
        x1 = x_ref[...] + gate_a * mix
        x1_ref[...] = x1
        _, xn = _rms_stats(x1)
        h2_ref[...] = ((xn * gf_ref[...]) * (1.0 + scale_f) + shift_f).astype(BF16)

    tile = lambda w: pl.BlockSpec((tm, w), lambda i: (i, 0))
    vec = lambda w: pl.BlockSpec((1, w), lambda i: (0, 0))
    return pl.pallas_call(
        body, name="out_proj", grid=(t // tm,),
        out_shape=(jax.ShapeDtypeStruct((t, d), BF16), jax.ShapeDtypeStruct((t, d), F32),
                   jax.ShapeDtypeStruct((t, d), F32), jax.ShapeDtypeStruct((t, d), BF16)),
        in_specs=[tile(NA_WIDTH), tile(SW_WIDTH), vec(NA_WIDTH), vec(SW_WIDTH),
                  pl.BlockSpec((d, d), lambda i: (0, 0)), tile(d),
                  pl.BlockSpec((1, 1, 6 * d), lambda i: (i // per_seq, 0, 0)), vec(d)],
        out_specs=(tile(d), tile(d), tile(d), tile(d)),
        compiler_params=_cparams(("arbitrary",), VMEM_BIG),
    )(oa, ob, g_na, g_sw, w_out, x, mod3, g_ffn)


def _up_proj(h2, w_up):
    t, d = h2.shape
    tm = TOKEN_TILE
    wcol = w_up.shape[2]

    def body(h_ref, w_ref, u_ref):
        u_ref[0] = _mm(h_ref[...], w_ref[0])

    return pl.pallas_call(
        body, name="up_proj", grid=(N_SHARD, t // tm),
        out_shape=jax.ShapeDtypeStruct((2, t, D_FF), F32),
        in_specs=[pl.BlockSpec((tm, d), lambda j, i: (i, 0)), pl.BlockSpec((1, d, wcol), lambda j, i: (j, 0, 0))],
        out_specs=pl.BlockSpec((1, tm, wcol), lambda j, i: (j // 2, i, j % 2)),
        compiler_params=_cparams(("arbitrary", "arbitrary"), VMEM_BIG),
    )(h2, w_up)


def _taps_chunk(load, s, rows, seq):
    cur = load(s, rows)
    above = load(pl.multiple_of(jnp.maximum(s - SUBLANES, 0), SUBLANES), SUBLANES)
    below = load(pl.multiple_of(jnp.minimum(s + rows, seq - SUBLANES), SUBLANES), SUBLANES)
    up = jnp.where(s > 0, above[SUBLANES - 1:SUBLANES, :], 0.0)
    dn = jnp.where(s + rows < seq, below[0:1, :], 0.0)
    row = lax.broadcasted_iota(jnp.int32, cur.shape, 0)
    prev = jnp.where(row == 0, up, pltpu.roll(cur, 1, 0))
    nxt = jnp.where(row == rows - 1, dn, pltpu.roll(cur, rows - 1, 0))
    return cur, prev, nxt


def _conv_gate(u, conv_w, conv_b, batch, seq):
    t = u.shape[1]
    cw = FF_TILE
    rows = CONV_CHUNK

    def body(u_ref, w_ref, b_ref, a_ref):
        def chunk(i, carry):
            s = pl.multiple_of(i * rows, rows)
            gt, prev, nxt = _taps_chunk(lambda at, n: u_ref[1, pl.ds(at, n), :], s, rows, seq)
            gc = prev * w_ref[0:1, :] + gt * w_ref[1:2, :] + nxt * w_ref[2:3, :] + b_ref[...]
            a_ref[pl.ds(s, rows), :] = ((gc * _sigmoid(gc)) * u_ref[0, pl.ds(s, rows), :]).astype(BF16)
            return carry

        lax.fori_loop(0, seq // rows, chunk, 0)

    return pl.pallas_call(
        body, name="conv_gate", grid=(batch, D_FF // cw),
        out_shape=jax.ShapeDtypeStruct((t, D_FF), BF16),
        in_specs=[pl.BlockSpec((2, seq, cw), lambda b, j: (0, b, j)),
                  pl.BlockSpec((3, cw), lambda b, j: (0, j)), pl.BlockSpec((1, cw), lambda b, j: (0, j))],
        out_specs=pl.BlockSpec((seq, cw), lambda b, j: (b, j)),
        compiler_params=_cparams(("arbitrary", "arbitrary"), VMEM_BIG),
    )(u, conv_w, conv_b)


def _down_and_loss(a, w_down, x1, mod3, g_final, target, seq):
    t, d = x1.shape
    tm = TOKEN_TILE
    per_seq = seq // tm
    batch = t // seq

    def body(a_ref, w_ref, x1_ref, mod_ref, g_ref, tgt_ref, dx2_ref, dffn_ref, loss_ref, dgate_ref, dg_ref):
        i = pl.program_id(0)
        f = _mm(a_ref[...], w_ref[...])
        gate_f = mod_ref[0, :, 5 * d:6 * d]
        x2 = x1_ref[...] + gate_f * f
        r, xn = _rms_stats(x2)
        err = xn * g_ref[...] - tgt_ref[...]
        part = 0.5 * jnp.sum(jnp.mean(err * err, axis=-1, keepdims=True))
        dy = err / d
        dx2 = _rms_bwd(dy * g_ref[...], xn, r)
        dx2_ref[...] = dx2
        dffn_ref[...] = (dx2 * gate_f).astype(BF16)

        @pl.when(i == 0)
        def _():
            loss_ref[...] = jnp.zeros_like(loss_ref)
            dg_ref[...] = jnp.zeros_like(dg_ref)

        @pl.when(i % per_seq == 0)
        def _():
            dgate_ref[...] = jnp.zeros_like(dgate_ref)

        loss_ref[...] += part
        dg_ref[...] += jnp.sum(dy * xn, axis=0, keepdims=True)
        dgate_ref[0] += jnp.sum(dx2 * f, axis=0, keepdims=True)

    tile = lambda w: pl.BlockSpec((tm, w), lambda i: (i, 0))
    return pl.pallas_call(
        body, name="down_loss", grid=(t // tm,),
        out_shape=(jax.ShapeDtypeStruct((t, d), F32), jax.ShapeDtypeStruct((t, d), BF16),
                   jax.ShapeDtypeStruct((SUBLANES, LANES), F32), jax.ShapeDtypeStruct((batch, 1, d), F32),
                   jax.ShapeDtypeStruct((1, d), F32)),
        in_specs=[tile(D_FF), pl.BlockSpec((D_FF, d), lambda i: (0, 0)), tile(d),
                  pl.BlockSpec((1, 1, 6 * d), lambda i: (i // per_seq, 0, 0)),
                  pl.BlockSpec((1, d), lambda i: (0, 0)), tile(d)],
        out_specs=(tile(d), tile(d), pl.BlockSpec((SUBLANES, LANES), lambda i: (0, 0)),
                   pl.BlockSpec((1, 1, d), lambda i: (i // per_seq, 0, 0)), pl.BlockSpec((1, d), lambda i: (0, 0))),
        compiler_params=_cparams(("arbitrary",), VMEM_BIG),
    )(a, w_down, x1, mod3, g_final, target)


def _ffn_backward(dffn, w_down, u, conv_w, conv_b, batch, seq):
    t, d = dffn.shape
    cw = FF_TILE

    rows = CONV_CHUNK

    def body(df_ref, wd_ref, u_ref, w_ref, b_ref, du_ref, gwd_ref, gwdb_ref, gcw_ref, gcb_ref, da_scr, a_scr, dgc_scr):
        b = pl.program_id(1)
        df = df_ref[...]
        da_scr[...] = _mm_nt(df, wd_ref[...])

        @pl.when(b == 0)
        def _():
            gwd_ref[...] = jnp.zeros_like(gwd_ref)
            gcw_ref[...] = jnp.zeros_like(gcw_ref)
            gcb_ref[...] = jnp.zeros_like(gcb_ref)

        def fold(v):
            return jnp.sum(v.reshape(rows // SUBLANES, SUBLANES, cw), axis=0)

        def chunk(i, carry):
            s = pl.multiple_of(i * rows, rows)
            here = pl.ds(s, rows)
            gt, prev, nxt = _taps_chunk(lambda at, n: u_ref[1, pl.ds(at, n), :], s, rows, seq)
            val, da = u_ref[0, here, :], da_scr[here, :]
            gc = prev * w_ref[0:1, :] + gt * w_ref[1:2, :] + nxt * w_ref[2:3, :] + b_ref[...]
            sg = _sigmoid(gc)
            sl = gc * sg
            a_scr[here, :] = (sl * val).astype(BF16)
            du_ref[0, here, :] = (da * sl).astype(BF16)
            dgc = (da * val) * (sg * (1.0 + gc * (1.0 - sg)))
            dgc_scr[here, :] = dgc
            cb, c0, c1, c2 = carry
            return cb + fold(dgc), c0 + fold(dgc * prev), c1 + fold(dgc * gt), c2 + fold(dgc * nxt)

        zero = jnp.zeros((SUBLANES, cw), F32)
        cb, c0, c1, c2 = lax.fori_loop(0, seq // rows, chunk, (zero, zero, zero, zero))
        gcb_ref[...] += jnp.sum(cb, axis=0, keepdims=True)
        gcw_ref[0:1, :] += jnp.sum(c0, axis=0, keepdims=True)
        gcw_ref[1:2, :] += jnp.sum(c1, axis=0, keepdims=True)
        gcw_ref[2:3, :] += jnp.sum(c2, axis=0, keepdims=True)
        gwd_ref[...] += _mm_tn(a_scr[...], df)

        @pl.when(b == batch - 1)
        def _():
            gwdb_ref[...] = gwd_ref[...].astype(BF16)

        def chunk2(i, carry):
            s = pl.multiple_of(i * rows, rows)
            dgc, dprev, dnxt = _taps_chunk(lambda at, n: dgc_scr[pl.ds(at, n), :], s, rows, seq)
            du_ref[1, pl.ds(s, rows), :] = (dnxt * w_ref[0:1, :] + dgc * w_ref[1:2, :]
                                            + dprev * w_ref[2:3, :]).astype(BF16)
            return carry

        lax.fori_loop(0, seq // rows, chunk2, 0)

    wd_spec = pl.BlockSpec((cw, d), lambda j, b: (j, 0))
    return pl.pallas_call(
        body, name="ffn_backward", grid=(D_FF // cw, batch),
        out_shape=(jax.ShapeDtypeStruct((2, t, D_FF), BF16), jax.ShapeDtypeStruct((D_FF, d), F32),
                   jax.ShapeDtypeStruct((D_FF, d), BF16),
                   jax.ShapeDtypeStruct((3, D_FF), F32), jax.ShapeDtypeStruct((1, D_FF), F32)),
        in_specs=[pl.BlockSpec((seq, d), lambda j, b: (b, 0)), wd_spec,
                  pl.BlockSpec((2, seq, cw), lambda j, b: (0, b, j)),
                  pl.BlockSpec((3, cw), lambda j, b: (0, j)), pl.BlockSpec((1, cw), lambda j, b: (0, j))],
        out_specs=(pl.BlockSpec((2, seq, cw), lambda j, b: (0, b, j)), wd_spec, wd_spec,
                   pl.BlockSpec((3, cw), lambda j, b: (0, j)), pl.BlockSpec((1, cw), lambda j, b: (0, j))),
        scratch_shapes=[pltpu.VMEM((seq, cw), F32), pltpu.VMEM((seq, cw), BF16), pltpu.VMEM((seq, cw), F32)],
        compiler_params=_cparams(("arbitrary", "arbitrary"), VMEM_BIG),
    )(dffn, w_down, u, conv_w, conv_b)


def _up_backward(du, w_up, x1, mod3, g_ffn, dx2, mix, seq):
    _, t, _ = du.shape
    d = x1.shape[1]
    tm = TOKEN_TILE // 2
    per_seq = seq // tm
    batch = t // seq
    wcol = w_up.shape[2]

    def body(du_ref, w_ref, x1_ref, mod_ref, g_ref, dx2_ref, mix_ref,
             dx1_ref, dmix_ref, dsh_ref, dsc_ref, dga_ref, dg_ref):
        i = pl.program_id(0)
        dh = jnp.zeros((tm, d), F32)
        for j in range(N_SHARD):
            dh = dh + _mm_nt(du_ref[j // 2, :, (j % 2) * wcol:(j % 2 + 1) * wcol], w_ref[j])
        gate_a = mod_ref[0, :, 2 * d:3 * d]
        scale_f = mod_ref[0, :, 4 * d:5 * d]
        r, xn = _rms_stats(x1_ref[...])
        xg = xn * g_ref[...]
        dxg = dh * (1.0 + scale_f)
        dx1 = dx2_ref[...] + _rms_bwd(dxg * g_ref[...], xn, r)
        dx1_ref[...] = dx1
        dmix_ref[...] = (dx1 * gate_a).astype(BF16)

        @pl.when(i == 0)
        def _():
            dg_ref[...] = jnp.zeros_like(dg_ref)

        @pl.when(i % per_seq == 0)
        def _():
            dsh_ref[...] = jnp.zeros_like(dsh_ref)
            dsc_ref[...] = jnp.zeros_like(dsc_ref)
            dga_ref[...] = jnp.zeros_like(dga_ref)

        dg_ref[...] += jnp.sum(dxg * xn, axis=0, keepdims=True)
        dsh_ref[0] += jnp.sum(dh, axis=0, keepdims=True)
        dsc_ref[0] += jnp.sum(dh * xg, axis=0, keepdims=True)
        dga_ref[0] += jnp.sum(dx1 * mix_ref[...], axis=0, keepdims=True)

    tile = lambda w: pl.BlockSpec((tm, w), lambda i: (i, 0))
    per_b = pl.BlockSpec((1, 1, d), lambda i: (i // per_seq, 0, 0))
    small = jax.ShapeDtypeStruct((batch, 1, d), F32)
    return pl.pallas_call(
        body, name="up_backward", grid=(t // tm,),
        out_shape=(jax.ShapeDtypeStruct((t, d), F32), jax.ShapeDtypeStruct((t, d), BF16), small, small, small,
                   jax.ShapeDtypeStruct((1, d), F32)),
        in_specs=[pl.BlockSpec((2, tm, D_FF), lambda i: (0, i, 0)),
                  pl.BlockSpec((N_SHARD, d, wcol), lambda i: (0, 0, 0)), tile(d),
                  pl.BlockSpec((1, 1, 6 * d), lambda i: (i // per_seq, 0, 0)),
                  pl.BlockSpec((1, d), lambda i: (0, 0)), tile(d), tile(d)],
        out_specs=(tile(d), tile(d), per_b, per_b, per_b, pl.BlockSpec((1, d), lambda i: (0, 0))),
        compiler_params=_cparams(("arbitrary",), VMEM_BIG),
    )(du, w_up, x1, mod3, g_ffn, dx2, mix)


def _up_weight_grad(h2, du, rider=None):
    t, d = h2.shape
    tk = TOKEN_TILE
    wcol = D_FF // 2
    n_k = t // tk

    def body(h_ref, du_ref, g_ref, gb_ref):
        k = pl.program_id(1)

        @pl.when(k == 0)
        def _():
            g_ref[...] = jnp.zeros_like(g_ref)

        g_ref[0] += _mm_tn(h_ref[...], du_ref[0])

        @pl.when(k == n_k - 1)
        def _():
            gb_ref[...] = g_ref[...].astype(BF16)

    g_spec = pl.BlockSpec((1, d, wcol), lambda j, k: (j, 0, 0))
    return _hosted(
        body, rider, name="up_weight_grad", grid=(N_SHARD, n_k),
        out_shape=[jax.ShapeDtypeStruct((N_SHARD, d, wcol), F32), jax.ShapeDtypeStruct((N_SHARD, d, wcol), BF16)],
        in_specs=[pl.BlockSpec((tk, d), lambda j, k: (k, 0)),
                  pl.BlockSpec((1, tk, wcol), lambda j, k: (j // 2, k, j % 2))],
        out_specs=[g_spec, g_spec], scratch_shapes=[],
        compiler_params=_cparams(("arbitrary", "arbitrary"), VMEM_BIG), args=[h2, du])


def _out_backward(dmix, w_out, oab, oa, ob, g_na, g_sw):
    t, d = dmix.shape
    tm = TOKEN_TILE
    hw = NA_WIDTH

    def body(dm_ref, w_ref, oab_ref, oa_ref, ob_ref, gna_ref, gsw_ref,
             doa_ref, dob_ref, gw_ref, gwb_ref, dgna_ref, dgsw_ref):
        @pl.when(pl.program_id(0) == 0)
        def _():
            gw_ref[...] = jnp.zeros_like(gw_ref)
            dgna_ref[...] = jnp.zeros_like(dgna_ref)
            dgsw_ref[...] = jnp.zeros_like(dgsw_ref)

        dm = dm_ref[...]
        gw_ref[...] += _mm_tn(oab_ref[...], dm)

        @pl.when(pl.program_id(0) == t // tm - 1)
        def _():
            gwb_ref[...] = gw_ref[...].astype(BF16)

        do = _mm_nt(dm, w_ref[...])
        for raw_ref, g_ref, dst_ref, dg_ref, lo in ((oa_ref, gna_ref, doa_ref, dgna_ref, 0),
                                                     (ob_ref, gsw_ref, dob_ref, dgsw_ref, hw)):
            r, xn = _rms_stats(raw_ref[...])
            dpart = do[:, lo:lo + hw]
            dg_ref[...] += jnp.sum(dpart * xn, axis=0, keepdims=True)
            dst_ref[...] = _rms_bwd(dpart * g_ref[...], xn, r).astype(BF16)

    tile = lambda w: pl.BlockSpec((tm, w), lambda i: (i, 0))
    vec = lambda w: pl.BlockSpec((1, w), lambda i: (0, 0))
    return pl.pallas_call(
        body, name="out_backward", grid=(t // tm,),
        out_shape=(jax.ShapeDtypeStruct((t, hw), BF16), jax.ShapeDtypeStruct((t, hw), BF16),
                   jax.ShapeDtypeStruct((d, d), F32), jax.ShapeDtypeStruct((d, d), BF16),
                   jax.ShapeDtypeStruct((1, hw), F32), jax.ShapeDtypeStruct((1, hw), F32)),
        in_specs=[tile(d), pl.BlockSpec((d, d), lambda i: (0, 0)), tile(d), tile(hw), tile(hw), vec(hw), vec(hw)],
        out_specs=(tile(hw), tile(hw), pl.BlockSpec((d, d), lambda i: (0, 0)), pl.BlockSpec((d, d), lambda i: (0, 0)),
                   vec(hw), vec(hw)),
        compiler_params=_cparams(("arbitrary",), VMEM_BIG),
    )(dmix, w_out, oab, oa, ob, g_na, g_sw)


def _na_backward(proj, d_o, tiles, batch, seq, rider=None):
    t = proj.shape[0]
    n_rows = seq // GRID_W
    n_pairs = NA_WIDTH // LANES
    win = NA_ROWS * GRID_W
    n_tiles = 2 * NA_ROWS - 2

    def body(q_ref, k_ref, v_ref, do_ref, tp_ref, dq_ref, dk_ref, dv_ref, dtp_ref, km, vm, dk_acc, dv_acc):
        @pl.when(pl.program_id(1) == 0)
        def _():
            dtp_ref[...] = jnp.zeros_like(dtp_ref)

        _na_prepare(k_ref, v_ref, km, vm)
        dk_acc[...] = jnp.zeros_like(dk_acc)
        dv_acc[...] = jnp.zeros_like(dv_acc)
        lane = lax.broadcasted_iota(jnp.int32, (GRID_W, LANES), 1)

        def row(r, carry):
            rs, off = _na_window(r, n_rows)
            rows = pl.ds(pl.multiple_of(r * GRID_W, GRID_W), GRID_W)
            wrows = pl.ds(pl.multiple_of(rs * GRID_W, GRID_W), win)
            q = q_ref[rows, :]
            do = do_ref[rows, :]
            dq = jnp.zeros((GRID_W, LANES), F32)
            for h in range(2):
                mine = (lane // HEAD_DIM) == h
                kw = km[h, wrows, :]
                p = _na_probs(q, kw, tp_ref, h, off)
                dp = _mm_nt(do, vm[h, wrows, :])
                ds = p * (dp - jnp.sum(p * dp, axis=-1, keepdims=True))
                for w in range(NA_ROWS // 2):
                    dtp_ref[h, 2 * w - off + (NA_ROWS - 1)] += ds[:, w * LANES:(w + 1) * LANES]
                dsb = (ds * QK_SCALE).astype(BF16)
                dq = dq + _mm(dsb, kw)
                dk_acc[wrows, :] += _mm_tn(dsb, jnp.where(mine, q, jnp.zeros_like(q)))
                dv_acc[wrows, :] += _mm_tn(p.astype(BF16), jnp.where(mine, do, jnp.zeros_like(do)))
            dq_ref[rows, :] = dq.astype(BF16)
            return carry

        lax.fori_loop(0, n_rows, row, 0, unroll=2)
        dk_ref[...] = dk_acc[...].astype(BF16)
        dv_ref[...] = dv_acc[...].astype(BF16)

    blk = lambda off: pl.BlockSpec((seq, LANES), lambda p, b: (b, off + p))
    out = jax.ShapeDtypeStruct((t, NA_WIDTH), BF16)
    return _hosted(
        body, rider, name="na_backward", grid=(n_pairs, batch),
        out_shape=[out, out, out, jax.ShapeDtypeStruct(tiles.shape, F32)],
        in_specs=[blk(0), blk(n_pairs), blk(2 * n_pairs), blk(0),
                  pl.BlockSpec((2, n_tiles, GRID_W, LANES), lambda p, b: (p, 0, 0, 0))],
        out_specs=[blk(0), blk(0), blk(0), pl.BlockSpec((2, n_tiles, GRID_W, LANES), lambda p, b: (p, 0, 0, 0))],
        scratch_shapes=[pltpu.VMEM((2, seq, LANES), BF16), pltpu.VMEM((2, seq, LANES), BF16),
                        pltpu.VMEM((seq, LANES), F32), pltpu.VMEM((seq, LANES), F32)],
        compiler_params=_cparams(("arbitrary", "arbitrary")), args=[proj, proj, proj, d_o, tiles])


def _na_bias_grad(dtiles_flat, expand):
    n = dtiles_flat.shape[0]

    def body(t_ref, e_ref, o_ref):
        o_ref[...] = lax.dot_general(t_ref[...], e_ref[...], (((1,), (1,)), ((), ())),
                                     precision=lax.Precision.HIGHEST, preferred_element_type=F32)

    return pl.pallas_call(
        body, name="na_bias_grad",
        out_shape=jax.ShapeDtypeStruct((n, expand.shape[0]), F32),
        compiler_params=_cparams(vmem=VMEM_BIG),
    )(dtiles_flat, expand)


def _sw_backward(proj, d_o, sink, batch, seq, rider=None):
    t = proj.shape[0]
    n_pairs = SW_WIDTH // LANES
    q_blk = 3 * NA_WIDTH // LANES
    k_blk = q_blk + n_pairs
    n_blocks = seq // SW_BLOCK
    pad = seq + 2 * SW_BLOCK

    def body(sink_ref, q_ref, k_ref, v_ref, do_ref, dq_ref, dk_ref, dv_ref, dsk_ref,
             k_lo, k_hi, v_lo, v_hi, dk_loc, dv_loc, dk_tot, dv_tot):
        hp = pl.program_id(1)
        g = hp // 2
        _sw_prepare(k_ref, g, k_lo, k_hi, seq)
        _sw_prepare(v_ref, g, v_lo, v_hi, seq)
        dk_loc[...] = jnp.zeros_like(dk_loc)
        dv_loc[...] = jnp.zeros_like(dv_loc)

        @pl.when(hp == 0)
        def _():
            dk_tot[...] = jnp.zeros_like(dk_tot)
            dv_tot[...] = jnp.zeros_like(dv_tot)

        lane = lax.broadcasted_iota(jnp.int32, (SW_BLOCK, LANES), 1)

        def block(n, carry):
            rows = pl.ds(pl.multiple_of(n * SW_BLOCK, SW_BLOCK), SW_BLOCK)
            wrows = pl.ds(pl.multiple_of(n * SW_BLOCK, SW_BLOCK), 3 * SW_BLOCK)
            qb = q_ref[rows, :]
            do = do_ref[rows, :]
            ok = _sw_mask(n, seq)
            dq = jnp.zeros((SW_BLOCK, LANES), F32)
            new = []
            for i, (kr, vr) in enumerate(((k_lo, v_lo), (k_hi, v_hi))):
                mine = (lane // HEAD_DIM) == i
                kk = kr[wrows, :]
                p, ps = _sw_probs(qb, kk, ok, sink_ref[2 * hp + i])
                dp = _mm_nt(do, vr[wrows, :])
                delta = jnp.sum(p * dp, axis=-1, keepdims=True)
                ds = p * (dp - delta)
                new.append(carry[i] - ps * delta)
                dsb = (ds * QK_SCALE).astype(BF16)
                dq = dq + _mm(dsb, kk)
                dk_loc[wrows, :] += _mm_tn(dsb, jnp.where(mine, qb, jnp.zeros_like(qb)))
                dv_loc[wrows, :] += _mm_tn(p.astype(BF16), jnp.where(mine, do, jnp.zeros_like(do)))
            dq_ref[rows, :] = dq
            return tuple(new)

        zero = jnp.zeros((SW_BLOCK, 1), F32)
        s0, s1 = lax.fori_loop(0, n_blocks, block, (zero, zero), unroll=2)
        row = lax.broadcasted_iota(jnp.int32, (SUBLANES, LANES), 0)
        dsk_ref[0, 0] = jnp.where(row == 0, jnp.sum(s0), jnp.where(row == 1, jnp.sum(s1), 0.0))

        lane_s = lax.broadcasted_iota(jnp.int32, (seq, LANES), 1)
        mine_g = (lane_s // HEAD_DIM) == g
        for loc, tot in ((dk_loc, dk_tot), (dv_loc, dv_tot)):
            part = loc[SW_BLOCK:SW_BLOCK + seq, :]
            tot[...] += jnp.where(mine_g, part + pltpu.roll(part, HEAD_DIM, 1), 0.0)

        @pl.when(hp == n_pairs - 1)
        def _():
            dk_ref[...] = dk_tot[...]
            dv_ref[...] = dv_tot[...].astype(BF16)

    return _hosted(
        body, rider, name="sw_backward", grid=(batch, n_pairs),
        out_shape=[jax.ShapeDtypeStruct((t, SW_WIDTH), F32), jax.ShapeDtypeStruct((t, LANES), F32),
                   jax.ShapeDtypeStruct((t, LANES), BF16), jax.ShapeDtypeStruct((batch, n_pairs, SUBLANES, LANES), F32)],
        in_specs=[pl.BlockSpec(memory_space=pltpu.SMEM),
                  pl.BlockSpec((seq, LANES), lambda b, p: (b, q_blk + p)),
                  pl.BlockSpec((seq, LANES), lambda b, p: (b, k_blk)),
                  pl.BlockSpec((seq, LANES), lambda b, p: (b, k_blk + 1)),
                  pl.BlockSpec((seq, LANES), lambda b, p: (b, p))],
        out_specs=[pl.BlockSpec((seq, LANES), lambda b, p: (b, p)), pl.BlockSpec((seq, LANES), lambda b, p: (b, 0)),
                   pl.BlockSpec((seq, LANES), lambda b, p: (b, 0)),
                   pl.BlockSpec((1, 1, SUBLANES, LANES), lambda b, p: (b, p, 0, 0))],
        scratch_shapes=[pltpu.VMEM((pad, LANES), BF16)] * 4 + [pltpu.VMEM((pad, LANES), F32)] * 2
        + [pltpu.VMEM((seq, LANES), F32)] * 2,
        compiler_params=_cparams(("arbitrary", "arbitrary")), args=[sink, proj, proj, proj, d_o])


def _in_backward(dqkv_a, dq_b, dk_b, dv_b, w_in_t, h1, x, mod3, g_attn, dx1, cos_t, sin_t, seq):
    t, d = x.shape
    tm = TOKEN_TILE // 2
    per_seq = seq // tm
    batch = t // seq
    dqa, dka, dva = dqkv_a
    n_q = SW_WIDTH // LANES

    def body(dqa_ref, dka_ref, dva_ref, dqb_ref, dkb_ref, dvb_ref, w_ref, h_ref, x_ref, mod_ref, g_ref, dx1_ref,
             cos_ref, sin_ref, dx_ref, gw_ref, gwb_ref, dsh_ref, dsc_ref, dg_ref):
        i = pl.program_id(0)

        @pl.when(i == 0)
        def _():
            gw_ref[...] = jnp.zeros_like(gw_ref)
            dg_ref[...] = jnp.zeros_like(dg_ref)

        @pl.when(i % per_seq == 0)
        def _():
            dsh_ref[...] = jnp.zeros_like(dsh_ref)
            dsc_ref[...] = jnp.zeros_like(dsc_ref)

        dr = jnp.concatenate([dqb_ref[...], dkb_ref[...]], axis=1)
        cos = jnp.concatenate([cos_ref[...]] * (n_q + 1), axis=1)
        sin = jnp.concatenate([sin_ref[...]] * (n_q + 1), axis=1)
        dr = dr * cos + _rope_rot(dr * sin)
        dproj = jnp.concatenate([dqa_ref[...], dka_ref[...], dva_ref[...], dr.astype(BF16), dvb_ref[...]], axis=1)
        gw_ref[...] += _mm_tn(dproj, h_ref[...])

        @pl.when(i == t // tm - 1)
        def _():
            gwb_ref[...] = gw_ref[...].astype(BF16)

        dh = _mm(dproj, w_ref[...])
        scale = mod_ref[0, :, d:2 * d]
        r, xn = _rms_stats(x_ref[...])
        xg = xn * g_ref[...]
        dxg = dh * (1.0 + scale)
        dx_ref[...] = dx1_ref[...] + _rms_bwd(dxg * g_ref[...], xn, r)
        dg_ref[...] += jnp.sum(dxg * xn, axis=0, keepdims=True)
        dsh_ref[0] += jnp.sum(dh, axis=0, keepdims=True)
        dsc_ref[0] += jnp.sum(dh * xg, axis=0, keepdims=True)

    tile = lambda w: pl.BlockSpec((tm, w), lambda i: (i, 0))
    per_b = pl.BlockSpec((1, 1, d), lambda i: (i // per_seq, 0, 0))
    small = jax.ShapeDtypeStruct((batch, 1, d), F32)
    rope = pl.BlockSpec((tm, LANES), lambda i: (i % per_seq, 0))
    return pl.pallas_call(
        body, name="in_backward", grid=(t // tm,),
        out_shape=(jax.ShapeDtypeStruct((t, d), F32), jax.ShapeDtypeStruct((IN_WIDTH, d), F32),
                   jax.ShapeDtypeStruct((IN_WIDTH, d), BF16), small, small, jax.ShapeDtypeStruct((1, d), F32)),
        in_specs=[tile(NA_WIDTH), tile(NA_WIDTH), tile(NA_WIDTH), tile(SW_WIDTH), tile(LANES), tile(LANES),
                  pl.BlockSpec((IN_WIDTH, d), lambda i: (0, 0)), tile(d), tile(d),
                  pl.BlockSpec((1, 1, 6 * d), lambda i: (i // per_seq, 0, 0)),
                  pl.BlockSpec((1, d), lambda i: (0, 0)), tile(d), rope, rope],
        out_specs=(tile(d), pl.BlockSpec((IN_WIDTH, d), lambda i: (0, 0)), pl.BlockSpec((IN_WIDTH, d), lambda i: (0, 0)),
                   per_b, per_b, pl.BlockSpec((1, d), lambda i: (0, 0))),
        compiler_params=_cparams(("arbitrary",), VMEM_BIG),
    )(dqa, dka, dva, dq_b, dk_b, dv_b, w_in_t, h1, x, mod3, g_attn, dx1, cos_t, sin_t)


def _ada_weight_grad(sc_all, dmod_cols):
    d = sc_all.shape[1]
    ncol = dmod_cols.shape[1]

    def body(s_ref, m_ref, o_ref):
        o_ref[...] = _mm_tn(s_ref[...].astype(BF16), m_ref[...].astype(BF16))

    return pl.pallas_call(
        body, name="ada_weight_grad",
        out_shape=jax.ShapeDtypeStruct((d, ncol), F32),
        compiler_params=_cparams(vmem=VMEM_BIG),
    )(sc_all, dmod_cols)


def _row_tile(rows, cols):
    target = max(SUBLANES, (1 << 20) // (4 * cols))
    best = rows
    for cand in range(SUBLANES, rows + 1, SUBLANES):
        if rows % cand == 0 and cand <= target:
            best = cand
    return best if rows % SUBLANES == 0 else rows


def _sum_slots(recv, own, name):
    _, rows, cols = recv.shape
    tr = _row_tile(rows, cols)

    def body(p_ref, own_ref, o_ref):
        o_ref[...] = ((own_ref[...] + p_ref[0].astype(F32)) + p_ref[1].astype(F32)) + p_ref[2].astype(F32)

    return pl.pallas_call(
        body, name=name, grid=(rows // tr,),
        out_shape=jax.ShapeDtypeStruct((rows, cols), F32),
        in_specs=[pl.BlockSpec((N_SHARD - 1, tr, cols), lambda i: (0, i, 0)), pl.BlockSpec((tr, cols), lambda i: (i, 0))],
        out_specs=pl.BlockSpec((tr, cols), lambda i: (i, 0)),
        compiler_params=_cparams(("arbitrary",)),
    )(recv, own)


def _adamw(w, grads, m, v, name):
    rows, cols = w.shape
    tr = _row_tile(rows, cols)
    ng = len(grads)

    def body(*refs):
        w_ref = refs[0]
        g_refs = refs[1:1 + ng]
        m_ref, v_ref = refs[1 + ng], refs[2 + ng]
        g_out, d_out, m_out, v_out = refs[3 + ng:]
        g = g_refs[0][...]
        for extra in g_refs[1:]:
            g = g + extra[...]
        g_out[...] = g
        m2 = ADAM_B1 * m_ref[...] + (1.0 - ADAM_B1) * g
        v2 = ADAM_B2 * v_ref[...] + (1.0 - ADAM_B2) * (g * g)
        m_out[...] = m2
        v_out[...] = v2
        m_hat = m2 / (1.0 - ADAM_B1 ** ADAM_STEP)
        v_hat = v2 / (1.0 - ADAM_B2 ** ADAM_STEP)
        d_out[...] = -ADAM_LR * (m_hat / (jnp.sqrt(v_hat) + ADAM_EPS) + ADAM_WD * w_ref[...])

    spec = pl.BlockSpec((tr, cols), lambda i: (i, 0))
    out = jax.ShapeDtypeStruct((rows, cols), F32)
    return pl.pallas_call(
        body, name=name, grid=(rows // tr,),
        out_shape=(out, out, out, out),
        in_specs=[spec] * (3 + ng), out_specs=(spec, spec, spec, spec),
        compiler_params=_cparams(("arbitrary",)),
    )(w, *grads, m, v)


def _pack_rows(arrays):
    tile = SUBLANES * LANES
    rows, offsets, at = [], [], 0
    for a in arrays:
        flat = a.reshape(-1).astype(F32)
        n = -(-flat.shape[0] // tile) * tile
        rows.append(jnp.pad(flat, (0, n - flat.shape[0])).reshape(-1, LANES))
        offsets.append(at)
        at += n // LANES
    return jnp.concatenate(rows, axis=0), offsets


def _unpack_rows(packed, offsets, shapes):
    out = []
    for off, shape in zip(offsets, shapes):
        n = 1
        for s in shape:
            n *= s
        nrow = -(-n // LANES)
        out.append(packed[off:off + nrow].reshape(-1)[:n].reshape(shape))
    return out


def _rope_tables(seq):
    half = HEAD_DIM // 2
    inv = ROPE_THETA ** (-jnp.arange(half, dtype=F32) / half)
    ang = jnp.arange(seq).astype(F32)[:, None] * inv[None, :]
    cos, sin = jnp.cos(ang), jnp.sin(ang)
    cos_t = jnp.concatenate([cos, cos, cos, cos], axis=1)
    sin_t = jnp.concatenate([-sin, sin, -sin, sin], axis=1)
    return cos_t, sin_t


def kernel(x, c, w_ada, b_ada, g_attn, w_in, na_rpb, sw_sink, g_na_out, g_sw_out, w_out, g_ffn, w_up, conv_w, conv_b, w_down, g_final, loss_target, m_w_ada, m_b_ada, m_g_attn, m_w_in, m_na_rpb, m_sw_sink, m_g_na_out, m_g_sw_out, m_w_out, m_g_ffn, m_w_up, m_conv_w, m_conv_b, m_w_down, m_g_final, v_w_ada, v_b_ada, v_g_attn, v_w_in, v_na_rpb, v_sw_sink, v_g_na_out, v_g_sw_out, v_w_out, v_g_ffn, v_w_up, v_conv_w, v_conv_b, v_w_down, v_g_final):
    batch, seq, d = x.shape
    t = batch * seq
    assert d == D_MODEL and seq % (NA_ROWS * GRID_W) == 0 and seq % TOKEN_TILE == 0 and batch <= SUBLANES
    shard = 2 * lax.axis_index("x") + lax.axis_index("y")
    xt = x.reshape(t, d)
    tgt = loss_target.reshape(t, d)

    c8 = jnp.pad(c, ((0, SUBLANES - batch), (0, 0)))
    w_in_t_s = jnp.transpose(w_in[0]).astype(BF16)
    (mod8, sc_all), (w_in_g,) = _ada_forward(c8, w_ada[0], b_ada, _Rider("gather", [w_in_t_s]))
    mod3 = mod8[:batch].reshape(batch, 1, 6 * d)
    w_in_t = w_in_g.reshape(IN_WIDTH, d)

    cos_t, sin_t = _rope_tables(seq)
    h1, proj = _in_proj(xt, mod3, g_attn, w_in_t, cos_t, sin_t, seq)
    n_heads = NA_WIDTH // HEAD_DIM
    n_tiles, n_dc = 2 * NA_ROWS - 2, 2 * NA_COLS - 1
    expand, neg_mask = _na_bias_pattern()
    rpb = na_rpb[0]
    rows2 = jnp.concatenate([rpb[:, :-1, :], rpb[:, 1:, :]], axis=2).reshape(n_heads * n_tiles, 2 * n_dc)
    rows2 = jnp.pad(rows2, ((0, 0), (0, GRID_W - 2 * n_dc)))
    tiles = _na_bias_tiles(rows2, expand, neg_mask).reshape(n_heads, n_tiles, GRID_W, LANES)
    sink = sw_sink[0]
    (oa,), (w_out_g, w_up_f) = _na_forward(proj, tiles, batch, seq,
                                            _Rider("gather", [w_out[0].astype(BF16), w_up[0].astype(BF16)]))
    (ob,), (w_down_g, conv_w_g) = _sw_forward(proj, sink, batch, seq,
                                              _Rider("gather", [w_down[0].astype(BF16), conv_w[0]]))
    w_out_f = w_out_g.reshape(d, d)
    w_down_f = w_down_g.reshape(D_FF, d)
    conv_w_f = jnp.transpose(conv_w_g, (1, 0, 2)).reshape(3, D_FF)
    oab, mix, x1, h2 = _out_proj(oa, ob, g_na_out, g_sw_out, w_out_f, xt, mod3, g_ffn, seq)
    u = _up_proj(h2, w_up_f)
    a = _conv_gate(u, conv_w_f, conv_b, batch, seq)
    dx2, dffn, loss_part, dgate_f, dg_final = _down_and_loss(a, w_down_f, x1, mod3, g_final.reshape(1, d), tgt, seq)

    du, gw_down, gw_down_b, gconv_w, gconv_b = _ffn_backward(dffn, w_down_f, u, conv_w_f, conv_b, batch, seq)
    blocks = lambda g, rows: g.reshape(N_SHARD, rows // N_SHARD, d)
    (gw_up, gw_up_b), (recv_down, own_down) = _up_weight_grad(
        h2, du, _Rider("scatter", [blocks(gw_down_b, D_FF)], [blocks(gw_down, D_FF)]))
    dx1, dmix, dshift_f, dscale_f, dgate_a, dg_ffn = _up_backward(du, w_up_f, x1, mod3, g_ffn, dx2, mix, seq)
    doa, dob, gw_out, gw_out_b, dg_na, dg_sw = _out_backward(dmix, w_out_f, oab, oa, ob, g_na_out, g_sw_out)
    (dqa, dka, dva, dtiles), (recv_up, own_up) = _na_backward(
        proj, doa, tiles, batch, seq, _Rider("scatter", [gw_up_b], [gw_up]))
    (dq_b, dk_b, dv_b, dsink_parts), (recv_out, own_out) = _sw_backward(
        proj, dob, sink, batch, seq, _Rider("scatter", [blocks(gw_out_b, d)], [blocks(gw_out, d)]))
    gx, gw_in_t, gw_in_b, dshift_a, dscale_a, dg_attn = _in_backward(
        (dqa, dka, dva), dq_b, dk_b, dv_b, w_in_t, h1, xt, mod3, g_attn, dx1, cos_t, sin_t, seq)
    recv_in, own_in = _ride_alone(_Rider("scatter", [blocks(gw_in_b, IN_WIDTH)], [blocks(gw_in_t, IN_WIDTH)]),
                                  "scatter_w_in")
    mine = [_sum_slots(r, o, name) for r, o, name in ((recv_in, own_in, "sum_w_in"), (recv_out, own_out, "sum_w_out"),
                                                      (recv_up, own_up, "sum_w_up"), (recv_down, own_down, "sum_w_down"))]
    theirs = _ride_alone(_Rider("swap", mine), "swap_sibling")

    red = _na_bias_grad(dtiles.reshape(n_heads * n_tiles, GRID_W * LANES), expand)[:, :2 * n_dc]
    red = red.reshape(n_heads, n_tiles, 2, n_dc)
    zero_row = jnp.zeros((n_heads, 1, n_dc), F32)
    g_rpb = (jnp.concatenate([red[:, :, 0, :], zero_row], axis=1)
             + jnp.concatenate([zero_row, red[:, :, 1, :]], axis=1))
    g_sink = jnp.sum(dsink_parts[:, :, :2, 0], axis=0).reshape(SW_WIDTH // HEAD_DIM)

    dmod = jnp.concatenate([dshift_a, dscale_a, dgate_a, dshift_f, dscale_f, dgate_f], axis=2).reshape(batch, 6 * d)
    dmod8 = jnp.pad(dmod, ((0, SUBLANES - batch), (0, 0)))
    small_parts = [jnp.sum(dmod, axis=0), dg_attn, g_rpb, g_sink, dg_na, dg_sw, dg_ffn, gconv_w, gconv_b, dg_final,
                   loss_part[0, 0:1]]
    packed, offsets = _pack_rows(small_parts + [dmod8])
    summed, every = _allreduce_small(packed)
    small_shapes = [(1, 6 * d), (1, d), na_rpb.shape, sw_sink.shape, (1, NA_WIDTH), (1, SW_WIDTH), (1, d),
                    (3, D_FF), (1, D_FF), (d,), ()]
    (g_b_ada, g_g_attn, g_na_rpb, g_sw_sink, g_g_na, g_g_sw, g_g_ffn, g_conv_w_full, g_conv_b, g_g_final,
     loss) = _unpack_rows(summed, offsets[:-1], small_shapes)
    dmod_rows = every[:, offsets[-1]:offsets[-1] + SUBLANES * 6 * d // LANES, :].reshape(N_DEV * SUBLANES, 6 * d)
    ncol = w_ada.shape[2]
    g_w_ada = _ada_weight_grad(sc_all, lax.dynamic_slice(dmod_rows, (0, shard * ncol), (N_DEV * SUBLANES, ncol)))
    cshard = conv_w.shape[2]
    g_conv_w = lax.dynamic_slice(g_conv_w_full, (0, shard * cshard), (3, cshard)).reshape(conv_w.shape)

    def big(w, m, v, g_parts, name):
        shape = w.shape
        outs = _adamw(w[0], g_parts, m[0], v[0], name)
        return [o.reshape(shape) for o in outs]

    r_w_ada = big(w_ada, m_w_ada, v_w_ada, [g_w_ada], "adamw_w_ada")
    r_w_in = big(w_in, m_w_in, v_w_in, [jnp.transpose(mine[0]), jnp.transpose(theirs[0])], "adamw_w_in")
    r_w_out = big(w_out, m_w_out, v_w_out, [mine[1], theirs[1]], "adamw_w_out")
    r_w_up = big(w_up, m_w_up, v_w_up, [mine[2], theirs[2]], "adamw_w_up")
    r_w_down = big(w_down, m_w_down, v_w_down, [mine[3], theirs[3]], "adamw_w_down")

    small_w = [b_ada, g_attn, na_rpb, sw_sink, g_na_out, g_sw_out, g_ffn, conv_w, conv_b, g_final]
    small_m = [m_b_ada, m_g_attn, m_na_rpb, m_sw_sink, m_g_na_out, m_g_sw_out, m_g_ffn, m_conv_w, m_conv_b, m_g_final]
    small_v = [v_b_ada, v_g_attn, v_na_rpb, v_sw_sink, v_g_na_out, v_g_sw_out, v_g_ffn, v_conv_w, v_conv_b, v_g_final]
    small_g = [g_b_ada, g_g_attn, g_na_rpb, g_sw_sink, g_g_na, g_g_sw, g_g_ffn, g_conv_w, g_conv_b, g_g_final]
    pw, offs = _pack_rows(small_w)
    pg, _ = _pack_rows(small_g)
    pm, _ = _pack_rows(small_m)
    pv, _ = _pack_rows(small_v)
    shapes = [w.shape for w in small_w]
    r_small = [_unpack_rows(o, offs, shapes) for o in _adamw(pw, [pg], pm, pv, "adamw_small")]

    def pick(k):
        b_, ga_, rpb_, sk_, gna_, gsw_, gf_, cw_, cb_, gfin_ = r_small[k]
        return [r_w_ada[k], b_, ga_, r_w_in[k], rpb_, sk_, gna_, gsw_, r_w_out[k], gf_, r_w_up[k], cw_, cb_,
                r_w_down[k], gfin_]

    return (loss, gx.reshape(batch, seq, d), *pick(0), *pick(1), *pick(2), *pick(3))
```

```python
import functools

import jax
import jax.numpy as jnp
from jax import lax
from jax.experimental import pallas as pl
from jax.experimental.pallas import tpu as pltpu

F32 = jnp.float32
BF16 = jnp.bfloat16
MESH = pl.DeviceIdType.MESH

D_MODEL = 1024
HEAD_DIM = 64
NA_WIDTH = 512
SW_WIDTH = 512
SW_KV_WIDTH = 128
IN_WIDTH = 2304
D_FF = 2816
GRID_W = 64
NA_ROWS = 8
NA_COLS = 16
SW_BLOCK = 128
ROPE_THETA = 10000.0
EPS = 1e-6
NEG = -1e30
QK_SCALE = HEAD_DIM ** -0.5

ADAM_LR = 0.001
ADAM_B1 = 0.9
ADAM_B2 = 0.999
ADAM_EPS = 1e-08
ADAM_WD = 0.01
ADAM_STEP = 10

N_SHARD = 4
N_DEV = 8
LANES = 128
SUBLANES = 8
TOKEN_TILE = 512
FF_TILE = 256
CONV_CHUNK = 64
VMEM_BIG = 56 * 1024 * 1024


def _mm(a, b):
    return jnp.dot(a, b, preferred_element_type=F32)


def _mm_nt(a, b):
    return lax.dot_general(a, b, (((1,), (1,)), ((), ())), preferred_element_type=F32)


def _mm_tn(a, b):
    return lax.dot_general(a, b, (((0,), (0,)), ((), ())), preferred_element_type=F32)


def _cparams(sem=None, vmem=None):
    kw = {}
    if sem is not None:
        kw["dimension_semantics"] = sem
    if vmem is not None:
        kw["vmem_limit_bytes"] = vmem
    return pltpu.CompilerParams(**kw)


def _sigmoid(x):
    return 1.0 / (1.0 + jnp.exp(-x))


def _rms_stats(x):
    r = lax.rsqrt(jnp.mean(x * x, axis=-1, keepdims=True) + EPS)
    return r, x * r


def _rms_bwd(dxn, xn, r):
    return r * (dxn - xn * jnp.mean(dxn * xn, axis=-1, keepdims=True))


def _my_pos():
    return lax.axis_index("x"), lax.axis_index("y"), lax.axis_index("c")


def _flip(v, bit):
    return 1 - v if bit else v


def _ada_forward(c8, w_ada, b_ada, rider):
    d = c8.shape[1]
    ncol = w_ada.shape[1]

    def body(c_ref, w_ref, b_ref, mod_ref, sc_ref, m_scr, mod_buf, ssem, rsem, ssem2, rsem2):
        x, y, c = _my_pos()
        me = 4 * x + 2 * y + c
        shard = 2 * x + y
        cv = c_ref[...]
        my_rows = pl.ds(pl.multiple_of(me * SUBLANES, SUBLANES), SUBLANES)
        sc_ref[my_rows, :] = cv * _sigmoid(cv)

        def copy1(k):
            peer = (_flip(x, (k >> 2) & 1), _flip(y, (k >> 1) & 1), _flip(c, k & 1))
            return pltpu.make_async_remote_copy(
                src_ref=sc_ref.at[my_rows, :], dst_ref=sc_ref.at[my_rows, :],
                send_sem=ssem.at[k - 1], recv_sem=rsem.at[k - 1], device_id=peer, device_id_type=MESH)

        sends = [copy1(k) for k in range(1, N_DEV)]
        for cp in sends:
            cp.start()
        for cp in sends:
            cp.wait_recv()
        m_scr[...] = _mm(sc_ref[...].astype(BF16), w_ref[...].astype(BF16))

        def copy2(k):
            px, py = _flip(x, (k >> 1) & 1), _flip(y, k & 1)
            rows = pl.ds(pl.multiple_of((4 * px + 2 * py + c) * SUBLANES, SUBLANES), SUBLANES)
            return pltpu.make_async_remote_copy(
                src_ref=m_scr.at[rows, :], dst_ref=mod_buf.at[shard],
                send_sem=ssem2.at[k - 1], recv_sem=rsem2.at[k - 1], device_id=(px, py, c), device_id_type=MESH)

        sends2 = [copy2(k) for k in range(1, N_SHARD)]
        for cp in sends2:
            cp.start()
        mod_buf[shard] = m_scr[my_rows, :]
        for cp in sends2:
            cp.wait_recv()
        for s in range(N_SHARD):
            mod_ref[:, s * ncol:(s + 1) * ncol] = mod_buf[s] + b_ref[:, s * ncol:(s + 1) * ncol]
        for cp in sends + sends2:
            cp.wait_send()

    vm = pl.BlockSpec(memory_space=pltpu.VMEM)
    return _hosted(
        body, rider, name="ada_forward", grid=(),
        out_shape=(jax.ShapeDtypeStruct((SUBLANES, N_SHARD * ncol), F32),
                   jax.ShapeDtypeStruct((N_DEV * SUBLANES, d), F32)),
        in_specs=[vm, vm, vm], out_specs=(vm, vm),
        scratch_shapes=[pltpu.VMEM((N_DEV * SUBLANES, ncol), F32), pltpu.VMEM((N_SHARD, SUBLANES, ncol), F32),
                        pltpu.SemaphoreType.DMA((N_DEV - 1,)), pltpu.SemaphoreType.DMA((N_DEV - 1,)),
                        pltpu.SemaphoreType.DMA((N_SHARD - 1,)), pltpu.SemaphoreType.DMA((N_SHARD - 1,))],
        compiler_params=_cparams(vmem=VMEM_BIG), args=[c8, w_ada, b_ada])


class _Rider:
    def __init__(self, kind, srcs, owns=()):
        self.kind, self.srcs, self.owns = kind, list(srcs), list(owns)
        n = len(self.srcs)
        sds = jax.ShapeDtypeStruct
        dma = pltpu.SemaphoreType.DMA
        if kind == "gather":
            self.out_shapes = [sds((N_SHARD,) + s.shape, s.dtype) for s in self.srcs]
            self.sems = [dma((n, N_SHARD - 1)), dma((n, N_SHARD - 1)), dma((n,))]
        elif kind == "scatter":
            self.out_shapes = ([sds((N_SHARD - 1,) + s.shape[1:], s.dtype) for s in self.srcs]
                               + [sds(o.shape[1:], o.dtype) for o in self.owns])
            self.sems = [dma((n, N_SHARD - 1)), dma((n, N_SHARD - 1)), dma((max(len(self.owns), 1),))]
        else:
            self.out_shapes = [sds(s.shape, s.dtype) for s in self.srcs]
            self.sems = [dma((n,)), dma((n,))]

    @property
    def inputs(self):
        return self.srcs + self.owns

    def copies(self, ins, outs, sems):
        n = len(self.srcs)
        x, y, c = _my_pos()
        shard = 2 * x + y
        local, remote = [], []
        if self.kind == "swap":
            ssem, rsem = sems
            for i in range(n):
                remote.append(pltpu.make_async_remote_copy(
                    src_ref=ins[i], dst_ref=outs[i], send_sem=ssem.at[i], recv_sem=rsem.at[i],
                    device_id=(x, y, 1 - c), device_id_type=MESH))
            return local, remote
        ssem, rsem, lsem = sems
        for i in range(n):
            if self.kind == "gather":
                local.append(pltpu.make_async_copy(ins[i], outs[i].at[shard], lsem.at[i]))
            for k in range(1, N_SHARD):
                px, py = _flip(x, (k >> 1) & 1), _flip(y, k & 1)
                if self.kind == "gather":
                    src, dst = ins[i], outs[i].at[shard]
                else:
                    src, dst = ins[i].at[2 * px + py], outs[i].at[k - 1]
                remote.append(pltpu.make_async_remote_copy(
                    src_ref=src, dst_ref=dst, send_sem=ssem.at[i, k - 1], recv_sem=rsem.at[i, k - 1],
                    device_id=(px, py, c), device_id_type=MESH))
        if self.kind == "scatter":
            for i in range(len(self.owns)):
                local.append(pltpu.make_async_copy(ins[n + i].at[shard], outs[n + i], lsem.at[i]))
        return local, remote

    def start(self, ins, outs, sems):
        local, remote = self.copies(ins, outs, sems)
        for cp in local + remote:
            cp.start()

    def wait(self, ins, outs, sems):
        local, remote = self.copies(ins, outs, sems)
        for cp in remote:
            cp.wait_recv()
        for cp in remote:
            cp.wait_send()
        for cp in local:
            cp.wait()


def _hosted(body, rider, *, name, grid, out_shape, in_specs, out_specs, scratch_shapes, compiler_params, args):
    out_shape, out_specs = list(out_shape), list(out_specs)
    if rider is None:
        outs = pl.pallas_call(body, name=name, grid=grid, out_shape=tuple(out_shape), in_specs=list(in_specs),
                              out_specs=tuple(out_specs), scratch_shapes=list(scratch_shapes),
                              compiler_params=compiler_params)(*args)
        return list(outs), []
    n_in, n_out, n_scr = len(in_specs), len(out_shape), len(scratch_shapes)
    nr_in, nr_out = len(rider.inputs), len(rider.out_shapes)

    def full(*refs):
        ins, refs = refs[:n_in], refs[n_in:]
        r_in, refs = refs[:nr_in], refs[nr_in:]
        outs, refs = refs[:n_out], refs[n_out:]
        r_out, refs = refs[:nr_out], refs[nr_out:]
        scr, sems = refs[:n_scr], refs[n_scr:]
        if grid:
            first = last = None
            for ax, size in enumerate(grid):
                f, l = pl.program_id(ax) == 0, pl.program_id(ax) == size - 1
                first = f if first is None else jnp.logical_and(first, f)
                last = l if last is None else jnp.logical_and(last, l)
            pl.when(first)(lambda: rider.start(r_in, r_out, sems))
            body(*ins, *outs, *scr)
            pl.when(last)(lambda: rider.wait(r_in, r_out, sems))
        else:
            rider.start(r_in, r_out, sems)
            body(*ins, *outs, *scr)
            rider.wait(r_in, r_out, sems)

    hbm = pl.BlockSpec(memory_space=pl.ANY)
    res = pl.pallas_call(
        full, name=name, grid=grid, out_shape=tuple(out_shape + rider.out_shapes),
        in_specs=list(in_specs) + [hbm] * nr_in, out_specs=tuple(out_specs + [hbm] * nr_out),
        scratch_shapes=list(scratch_shapes) + rider.sems, compiler_params=compiler_params,
    )(*args, *rider.inputs)
    return list(res[:n_out]), list(res[n_out:])


def _ride_alone(rider, name):
    return _hosted(lambda: None, rider, name=name, grid=(), out_shape=[], in_specs=[], out_specs=[], scratch_shapes=[],
                   compiler_params=_cparams(), args=[])[1]


def _allreduce_small(packed):
    r = packed.shape[0]

    def body(p_ref, sum_ref, all_ref, ssem, rsem):
        x, y, c = _my_pos()
        me = 4 * x + 2 * y + c
        all_ref[me] = p_ref[...]
        cps = []
        for k in range(1, N_DEV):
            peer = (_flip(x, (k >> 2) & 1), _flip(y, (k >> 1) & 1), _flip(c, k & 1))
            cps.append(pltpu.make_async_remote_copy(
                src_ref=all_ref.at[me], dst_ref=all_ref.at[me], send_sem=ssem.at[k - 1], recv_sem=rsem.at[k - 1],
                device_id=peer, device_id_type=MESH))
        for cp in cps:
            cp.start()
        for cp in cps:
            cp.wait_recv()
        acc = all_ref[0]
        for dev in range(1, N_DEV):
            acc = acc + all_ref[dev]
        sum_ref[...] = acc
        for cp in cps:
            cp.wait_send()

    vm = pl.BlockSpec(memory_space=pltpu.VMEM)
    return pl.pallas_call(
        body, name="allreduce_small",
        out_shape=(jax.ShapeDtypeStruct((r, LANES), F32), jax.ShapeDtypeStruct((N_DEV, r, LANES), F32)),
        in_specs=[vm], out_specs=(vm, vm),
        scratch_shapes=[pltpu.SemaphoreType.DMA((N_DEV - 1,)), pltpu.SemaphoreType.DMA((N_DEV - 1,))],
    )(packed)


def _rope_rot(t):
    w = t.shape[1]
    lane = lax.broadcasted_iota(jnp.int32, t.shape, 1)
    first = (lane % HEAD_DIM) < (HEAD_DIM // 2)
    return jnp.where(first, pltpu.roll(t, w - HEAD_DIM // 2, 1), pltpu.roll(t, HEAD_DIM // 2, 1))


def _in_proj(x, mod3, g_attn, w_in_t, cos_t, sin_t, seq):
    t, d = x.shape
    tm = TOKEN_TILE
    per_seq = seq // tm
    rope_lo, rope_hi = 3 * NA_WIDTH, 3 * NA_WIDTH + SW_WIDTH + SW_KV_WIDTH
    n_rep = (rope_hi - rope_lo) // LANES

    def body(x_ref, mod_ref, g_ref, w_ref, cos_ref, sin_ref, h_ref, p_ref):
        r, xn = _rms_stats(x_ref[...])
        shift, scale = mod_ref[0, :, 0:d], mod_ref[0, :, d:2 * d]
        hb = ((xn * g_ref[...]) * (1.0 + scale) + shift).astype(BF16)
        h_ref[...] = hb
        p_ref[:, :rope_lo] = _mm_nt(hb, w_ref[:rope_lo, :]).astype(BF16)
        pr = _mm_nt(hb, w_ref[rope_lo:rope_hi, :])
        cos = jnp.concatenate([cos_ref[...]] * n_rep, axis=1)
        sin = jnp.concatenate([sin_ref[...]] * n_rep, axis=1)
        p_ref[:, rope_lo:rope_hi] = (pr * cos + _rope_rot(pr) * sin).astype(BF16)
        p_ref[:, rope_hi:] = _mm_nt(hb, w_ref[rope_hi:, :]).astype(BF16)

    return pl.pallas_call(
        body, name="in_proj", grid=(t // tm,),
        out_shape=(jax.ShapeDtypeStruct((t, d), BF16), jax.ShapeDtypeStruct((t, IN_WIDTH), BF16)),
        in_specs=[pl.BlockSpec((tm, d), lambda i: (i, 0)),
                  pl.BlockSpec((1, 1, 6 * d), lambda i: (i // per_seq, 0, 0)),
                  pl.BlockSpec((1, d), lambda i: (0, 0)),
                  pl.BlockSpec((IN_WIDTH, d), lambda i: (0, 0)),
                  pl.BlockSpec((tm, LANES), lambda i: (i % per_seq, 0)),
                  pl.BlockSpec((tm, LANES), lambda i: (i % per_seq, 0))],
        out_specs=(pl.BlockSpec((tm, d), lambda i: (i, 0)), pl.BlockSpec((tm, IN_WIDTH), lambda i: (i, 0))),
        compiler_params=_cparams(("arbitrary",), VMEM_BIG),
    )(x, mod3, g_attn, w_in_t, cos_t, sin_t)


def _na_bias_pattern():
    n_dc = 2 * NA_COLS - 1
    j = lax.broadcasted_iota(jnp.int32, (GRID_W, GRID_W * LANES), 0)
    m = lax.broadcasted_iota(jnp.int32, (GRID_W, GRID_W * LANES), 1)
    q, lane = m // LANES, m % LANES
    k = lane % GRID_W
    cs = jnp.clip(q - NA_COLS // 2, 0, GRID_W - NA_COLS)
    ok = (k >= cs) & (k < cs + NA_COLS)
    hit = ok & (j < 2 * n_dc) & (lane // GRID_W == j // n_dc) & (k - q + (NA_COLS - 1) == j % n_dc)
    return hit.astype(F32), jnp.where(ok[0:1], 0.0, NEG).astype(F32)


def _na_bias_tiles(rows2, expand, mask):
    n, width = rows2.shape[0], expand.shape[1]
    step = 2048

    def body(r_ref, e_ref, m_ref, o_ref):
        o_ref[...] = jnp.dot(r_ref[...], e_ref[...], precision=lax.Precision.HIGHEST,
                             preferred_element_type=F32) + m_ref[...]

    return pl.pallas_call(
        body, name="na_bias_tiles", grid=(width // step,),
        out_shape=jax.ShapeDtypeStruct((n, width), F32),
        in_specs=[pl.BlockSpec(rows2.shape, lambda i: (0, 0)), pl.BlockSpec((expand.shape[0], step), lambda i: (0, i)),
                  pl.BlockSpec((1, step), lambda i: (0, i))],
        out_specs=pl.BlockSpec((n, step), lambda i: (0, i)),
        compiler_params=_cparams(("arbitrary",)),
    )(rows2, expand, mask)


def _na_prepare(k_ref, v_ref, km, vm):
    lane = lax.broadcasted_iota(jnp.int32, k_ref.shape, 1)
    low = lane < HEAD_DIM
    kv = k_ref[...]
    vv = v_ref[...]
    zero = jnp.zeros_like(kv)
    km[0] = jnp.where(low, kv, zero)
    km[1] = jnp.where(low, zero, kv)
    vm[0] = jnp.where(low, vv, zero)
    vm[1] = jnp.where(low, zero, vv)


def _na_window(r, n_rows):
    rs = jnp.clip(r - NA_ROWS // 2, 0, n_rows - NA_ROWS)
    return rs, r - rs


def _na_pair_window(ref, wrows):
    return jnp.concatenate([ref[0, wrows, :], ref[1, wrows, :]], axis=0)


def _na_probs(q, k2, tp_ref, off):
    win = k2.shape[0] // 2
    s = _mm_nt(q, k2) * QK_SCALE
    bias = jnp.concatenate([tp_ref[h, 2 * w - off + (NA_ROWS - 1)] for h in range(2) for w in range(NA_ROWS // 2)],
                           axis=1)
    s = s + bias
    halves = []
    for h in range(2):
        sh = s[:, h * win:(h + 1) * win]
        e = jnp.exp(sh - jnp.max(sh, axis=-1, keepdims=True))
        halves.append(e / jnp.sum(e, axis=-1, keepdims=True))
    return jnp.concatenate(halves, axis=1)


def _na_forward(proj, tiles, batch, seq, rider=None):
    t = proj.shape[0]
    n_rows = seq // GRID_W
    n_pairs = NA_WIDTH // LANES
    win = NA_ROWS * GRID_W

    def body(q_ref, k_ref, v_ref, tp_ref, o_ref, km, vm):
        _na_prepare(k_ref, v_ref, km, vm)

        def row(r, carry):
            rs, off = _na_window(r, n_rows)
            rows = pl.ds(pl.multiple_of(r * GRID_W, GRID_W), GRID_W)
            wrows = pl.ds(pl.multiple_of(rs * GRID_W, GRID_W), win)
            p = _na_probs(q_ref[rows, :], _na_pair_window(km, wrows), tp_ref, off)
            o_ref[rows, :] = _mm(p.astype(BF16), _na_pair_window(vm, wrows))
            return carry

        lax.fori_loop(0, n_rows, row, 0, unroll=2)

    return _hosted(
        body, rider, name="na_forward", grid=(batch, n_pairs),
        out_shape=[jax.ShapeDtypeStruct((t, NA_WIDTH), F32)],
        in_specs=[pl.BlockSpec((seq, LANES), lambda b, p: (b, p)),
                  pl.BlockSpec((seq, LANES), lambda b, p: (b, n_pairs + p)),
                  pl.BlockSpec((seq, LANES), lambda b, p: (b, 2 * n_pairs + p)),
                  pl.BlockSpec((2, 2 * NA_ROWS - 2, GRID_W, LANES), lambda b, p: (p, 0, 0, 0))],
        out_specs=[pl.BlockSpec((seq, LANES), lambda b, p: (b, p))],
        scratch_shapes=[pltpu.VMEM((2, seq, LANES), BF16), pltpu.VMEM((2, seq, LANES), BF16)],
        compiler_params=_cparams(("arbitrary", "arbitrary")), args=[proj, proj, proj, tiles])


def _sw_prepare(kv_ref, g, dst_lo, dst_hi, seq):
    lane = lax.broadcasted_iota(jnp.int32, kv_ref.shape, 1)
    mine = (lane // HEAD_DIM) == g
    kg = jnp.where(mine, kv_ref[...].astype(F32), 0.0)
    kr = pltpu.roll(kg, HEAD_DIM, 1)
    first = g == 0
    zero = jnp.zeros((SW_BLOCK, LANES), BF16)
    for dst, val in ((dst_lo, jnp.where(first, kg, kr)), (dst_hi, jnp.where(first, kr, kg))):
        dst[0:SW_BLOCK, :] = zero
        dst[SW_BLOCK:SW_BLOCK + seq, :] = val.astype(BF16)
        dst[SW_BLOCK + seq:, :] = zero


def _sw_mask(n, seq):
    qi = lax.broadcasted_iota(jnp.int32, (SW_BLOCK, 3 * SW_BLOCK), 0)
    kj = lax.broadcasted_iota(jnp.int32, (SW_BLOCK, 3 * SW_BLOCK), 1)
    kpos = n * SW_BLOCK - SW_BLOCK + kj
    return (jnp.abs(qi + SW_BLOCK - kj) <= SW_BLOCK) & (kpos >= 0) & (kpos < seq)


def _sw_probs(qb, k2, ok, sinks):
    band = k2.shape[0] // 2
    s2 = _mm_nt(qb, k2) * QK_SCALE
    halves, sink_p = [], []
    for i in range(2):
        s = jnp.where(ok, s2[:, i * band:(i + 1) * band], NEG)
        m = jnp.maximum(jnp.max(s, axis=-1, keepdims=True), sinks[i])
        p = jnp.exp(s - m)
        es = jnp.exp(sinks[i] - m)
        den = jnp.sum(p, axis=-1, keepdims=True) + es
        halves.append(p / den)
        sink_p.append(es / den)
    return jnp.concatenate(halves, axis=1), sink_p


def _sw_forward(proj, sink, batch, seq, rider=None):
    t = proj.shape[0]
    n_pairs = SW_WIDTH // LANES
    q_blk = 3 * NA_WIDTH // LANES
    k_blk = q_blk + n_pairs
    n_blocks = seq // SW_BLOCK
    pad = seq + 2 * SW_BLOCK

    def body(sink_ref, q_ref, k_ref, v_ref, o_ref, k_lo, k_hi, v_lo, v_hi):
        hp = pl.program_id(1)
        g = hp // 2
        _sw_prepare(k_ref, g, k_lo, k_hi, seq)
        _sw_prepare(v_ref, g, v_lo, v_hi, seq)

        def block(n, carry):
            rows = pl.ds(pl.multiple_of(n * SW_BLOCK, SW_BLOCK), SW_BLOCK)
            wrows = pl.ds(pl.multiple_of(n * SW_BLOCK, SW_BLOCK), 3 * SW_BLOCK)
            k2 = jnp.concatenate([k_lo[wrows, :], k_hi[wrows, :]], axis=0)
            v2 = jnp.concatenate([v_lo[wrows, :], v_hi[wrows, :]], axis=0)
            p, _ = _sw_probs(q_ref[rows, :], k2, _sw_mask(n, seq), (sink_ref[2 * hp], sink_ref[2 * hp + 1]))
            o_ref[rows, :] = _mm(p.astype(BF16), v2)
            return carry

        lax.fori_loop(0, n_blocks, block, 0, unroll=2)

    return _hosted(
        body, rider, name="sw_forward", grid=(batch, n_pairs),
        out_shape=[jax.ShapeDtypeStruct((t, SW_WIDTH), F32)],
        in_specs=[pl.BlockSpec(memory_space=pltpu.SMEM),
                  pl.BlockSpec((seq, LANES), lambda b, p: (b, q_blk + p)),
                  pl.BlockSpec((seq, LANES), lambda b, p: (b, k_blk)),
                  pl.BlockSpec((seq, LANES), lambda b, p: (b, k_blk + 1))],
        out_specs=[pl.BlockSpec((seq, LANES), lambda b, p: (b, p))],
        scratch_shapes=[pltpu.VMEM((pad, LANES), BF16)] * 4,
        compiler_params=_cparams(("arbitrary", "arbitrary")), args=[sink, proj, proj, proj])


def _out_proj(oa, ob, g_na, g_sw, w_out, x, mod3, g_ffn, seq):
    t, d = x.shape
    tm = TOKEN_TILE
    per_seq = seq // tm

    def body(oa_ref, ob_ref, gna_ref, gsw_ref, w_ref, x_ref, mod_ref, gf_ref, oab_ref, mix_ref, x1_ref, h2_ref):
        _, na = _rms_stats(oa_ref[...])
        _, nb = _rms_stats(ob_ref[...])
        oab = jnp.concatenate([na * gna_ref[...], nb * gsw_ref[...]], axis=1).astype(BF16)
        oab_ref[...] = oab
        mix = _mm(oab, w_ref[...])
        mix_ref[...] = mix
        gate_a = mod_ref[0, :, 2 * d:3 * d]
        shift_f, scale_f = mod_ref[0, :, 3 * d:4 * d], mod_ref[0, :, 4 * d:5 * d]
        x1 = x_ref[...] + gate_a * mix
        x1_ref[...] = x1
        _, xn = _rms_stats(x1)
        h2_ref[...] = ((xn * gf_ref[...]) * (1.0 + scale_f) + shift_f).astype(BF16)

    tile = lambda w: pl.BlockSpec((tm, w), lambda i: (i, 0))
    vec = lambda w: pl.BlockSpec((1, w), lambda i: (0, 0))
    return pl.pallas_call(
        body, name="out_proj", grid=(t // tm,),
        out_shape=(jax.ShapeDtypeStruct((t, d), BF16), jax.ShapeDtypeStruct((t, d), F32),
                   jax.ShapeDtypeStruct((t, d), F32), jax.ShapeDtypeStruct((t, d), BF16)),
        in_specs=[tile(NA_WIDTH), tile(SW_WIDTH), vec(NA_WIDTH), vec(SW_WIDTH),
                  pl.BlockSpec((d, d), lambda i: (0, 0)), tile(d),
                  pl.BlockSpec((1, 1, 6 * d), lambda i: (i // per_seq, 0, 0)), vec(d)],
        out_specs=(tile(d), tile(d), tile(d), tile(d)),
        compiler_params=_cparams(("arbitrary",), VMEM_BIG),
    )(oa, ob, g_na, g_sw, w_out, x, mod3, g_ffn)


def _up_proj(h2, w_up):
    t, d = h2.shape
    tm = TOKEN_TILE
    wcol = w_up.shape[2]

    def body(h_ref, w_ref, u_ref):
        u_ref[0] = _mm(h_ref[...], w_ref[0])

    return pl.pallas_call(
        body, name="up_proj", grid=(N_SHARD, t // tm),
        out_shape=jax.ShapeDtypeStruct((2, t, D_FF), F32),
        in_specs=[pl.BlockSpec((tm, d), lambda j, i: (i, 0)), pl.BlockSpec((1, d, wcol), lambda j, i: (j, 0, 0))],
        out_specs=pl.BlockSpec((1, tm, wcol), lambda j, i: (j // 2, i, j % 2)),
        compiler_params=_cparams(("arbitrary", "arbitrary"), VMEM_BIG),
    )(h2, w_up)


def _taps_chunk(load, s, rows, seq):
    cur = load(s, rows)
    above = load(pl.multiple_of(jnp.maximum(s - SUBLANES, 0), SUBLANES), SUBLANES)
    below = load(pl.multiple_of(jnp.minimum(s + rows, seq - SUBLANES), SUBLANES), SUBLANES)
    up = jnp.where(s > 0, above[SUBLANES - 1:SUBLANES, :], 0.0)
    dn = jnp.where(s + rows < seq, below[0:1, :], 0.0)
    row = lax.broadcasted_iota(jnp.int32, cur.shape, 0)
    prev = jnp.where(row == 0, up, pltpu.roll(cur, 1, 0))
    nxt = jnp.where(row == rows - 1, dn, pltpu.roll(cur, rows - 1, 0))
    return cur, prev, nxt


def _conv_gate(u, conv_w, conv_b, batch, seq):
    t = u.shape[1]
    cw = FF_TILE
    rows = CONV_CHUNK

    def body(u_ref, w_ref, b_ref, a_ref):
        def chunk(i, carry):
            s = pl.multiple_of(i * rows, rows)
            gt, prev, nxt = _taps_chunk(lambda at, n: u_ref[1, pl.ds(at, n), :], s, rows, seq)
            gc = prev * w_ref[0:1, :] + gt * w_ref[1:2, :] + nxt * w_ref[2:3, :] + b_ref[...]
            a_ref[pl.ds(s, rows), :] = ((gc * _sigmoid(gc)) * u_ref[0, pl.ds(s, rows), :]).astype(BF16)
            return carry

        lax.fori_loop(0, seq // rows, chunk, 0)

    return pl.pallas_call(
        body, name="conv_gate", grid=(batch, D_FF // cw),
        out_shape=jax.ShapeDtypeStruct((t, D_FF), BF16),
        in_specs=[pl.BlockSpec((2, seq, cw), lambda b, j: (0, b, j)),
                  pl.BlockSpec((3, cw), lambda b, j: (0, j)), pl.BlockSpec((1, cw), lambda b, j: (0, j))],
        out_specs=pl.BlockSpec((seq, cw), lambda b, j: (b, j)),
        compiler_params=_cparams(("arbitrary", "arbitrary"), VMEM_BIG),
    )(u, conv_w, conv_b)


def _down_and_loss(a, w_down, x1, mod3, g_final, target, seq):
    t, d = x1.shape
    tm = TOKEN_TILE
    per_seq = seq // tm
    batch = t // seq

    def body(a_ref, w_ref, x1_ref, mod_ref, g_ref, tgt_ref, dx2_ref, dffn_ref, loss_ref, dgate_ref, dg_ref):
        i = pl.program_id(0)
        f = _mm(a_ref[...], w_ref[...])
        gate_f = mod_ref[0, :, 5 * d:6 * d]
        x2 = x1_ref[...] + gate_f * f
        r, xn = _rms_stats(x2)
        err = xn * g_ref[...] - tgt_ref[...]
        part = 0.5 * jnp.sum(jnp.mean(err * err, axis=-1, keepdims=True))
        dy = err / d
        dx2 = _rms_bwd(dy * g_ref[...], xn, r)
        dx2_ref[...] = dx2
        dffn_ref[...] = (dx2 * gate_f).astype(BF16)

        @pl.when(i == 0)
        def _():
            loss_ref[...] = jnp.zeros_like(loss_ref)
            dg_ref[...] = jnp.zeros_like(dg_ref)

        @pl.when(i % per_seq == 0)
        def _():
            dgate_ref[...] = jnp.zeros_like(dgate_ref)

        loss_ref[...] += part
        dg_ref[...] += jnp.sum(dy * xn, axis=0, keepdims=True)
        dgate_ref[0] += jnp.sum(dx2 * f, axis=0, keepdims=True)

    tile = lambda w: pl.BlockSpec((tm, w), lambda i: (i, 0))
    return pl.pallas_call(
        body, name="down_loss", grid=(t // tm,),
        out_shape=(jax.ShapeDtypeStruct((t, d), F32), jax.ShapeDtypeStruct((t, d), BF16),
                   jax.ShapeDtypeStruct((SUBLANES, LANES), F32), jax.ShapeDtypeStruct((batch, 1, d), F32),
                   jax.ShapeDtypeStruct((1, d), F32)),
        in_specs=[tile(D_FF), pl.BlockSpec((D_FF, d), lambda i: (0, 0)), tile(d),
                  pl.BlockSpec((1, 1, 6 * d), lambda i: (i // per_seq, 0, 0)),
                  pl.BlockSpec((1, d), lambda i: (0, 0)), tile(d)],
        out_specs=(tile(d), tile(d), pl.BlockSpec((SUBLANES, LANES), lambda i: (0, 0)),
                   pl.BlockSpec((1, 1, d), lambda i: (i // per_seq, 0, 0)), pl.BlockSpec((1, d), lambda i: (0, 0))),
        compiler_params=_cparams(("arbitrary",), VMEM_BIG),
    )(a, w_down, x1, mod3, g_final, target)


def _ffn_backward(dffn, w_down, u, conv_w, conv_b, batch, seq):
    t, d = dffn.shape
    cw = FF_TILE

    rows = CONV_CHUNK

    def body(df_ref, wd_ref, u_ref, w_ref, b_ref, du_ref, gwd_ref, gwdb_ref, gcw_ref, gcb_ref, da_scr, a_scr, dgc_scr):
        b = pl.program_id(1)
        df = df_ref[...]
        da_scr[...] = _mm_nt(df, wd_ref[...])

        @pl.when(b == 0)
        def _():
            gwd_ref[...] = jnp.zeros_like(gwd_ref)
            gcw_ref[...] = jnp.zeros_like(gcw_ref)
            gcb_ref[...] = jnp.zeros_like(gcb_ref)

        def fold(v):
            return jnp.sum(v.reshape(rows // SUBLANES, SUBLANES, cw), axis=0)

        def chunk(i, carry):
            s = pl.multiple_of(i * rows, rows)
            here = pl.ds(s, rows)
            gt, prev, nxt = _taps_chunk(lambda at, n: u_ref[1, pl.ds(at, n), :], s, rows, seq)
            val, da = u_ref[0, here, :], da_scr[here, :]
            gc = prev * w_ref[0:1, :] + gt * w_ref[1:2, :] + nxt * w_ref[2:3, :] + b_ref[...]
            sg = _sigmoid(gc)
            sl = gc * sg
            a_scr[here, :] = (sl * val).astype(BF16)
            du_ref[0, here, :] = (da * sl).astype(BF16)
            dgc = (da * val) * (sg * (1.0 + gc * (1.0 - sg)))
            dgc_scr[here, :] = dgc
            cb, c0, c1, c2 = carry
            return cb + fold(dgc), c0 + fold(dgc * prev), c1 + fold(dgc * gt), c2 + fold(dgc * nxt)

        zero = jnp.zeros((SUBLANES, cw), F32)
        cb, c0, c1, c2 = lax.fori_loop(0, seq // rows, chunk, (zero, zero, zero, zero))
        gcb_ref[...] += jnp.sum(cb, axis=0, keepdims=True)
        gcw_ref[0:1, :] += jnp.sum(c0, axis=0, keepdims=True)
        gcw_ref[1:2, :] += jnp.sum(c1, axis=0, keepdims=True)
        gcw_ref[2:3, :] += jnp.sum(c2, axis=0, keepdims=True)
        gwd_ref[...] += _mm_tn(a_scr[...], df)

        @pl.when(b == batch - 1)
        def _():
            gwdb_ref[...] = gwd_ref[...].astype(BF16)

        def chunk2(i, carry):
            s = pl.multiple_of(i * rows, rows)
            dgc, dprev, dnxt = _taps_chunk(lambda at, n: dgc_scr[pl.ds(at, n), :], s, rows, seq)
            du_ref[1, pl.ds(s, rows), :] = (dnxt * w_ref[0:1, :] + dgc * w_ref[1:2, :]
                                            + dprev * w_ref[2:3, :]).astype(BF16)
            return carry

        lax.fori_loop(0, seq // rows, chunk2, 0)

    wd_spec = pl.BlockSpec((cw, d), lambda j, b: (j, 0))
    return pl.pallas_call(
        body, name="ffn_backward", grid=(D_FF // cw, batch),
        out_shape=(jax.ShapeDtypeStruct((2, t, D_FF), BF16), jax.ShapeDtypeStruct((D_FF, d), F32),
                   jax.ShapeDtypeStruct((D_FF, d), BF16),
                   jax.ShapeDtypeStruct((3, D_FF), F32), jax.ShapeDtypeStruct((1, D_FF), F32)),
        in_specs=[pl.BlockSpec((seq, d), lambda j, b: (b, 0)), wd_spec,
                  pl.BlockSpec((2, seq, cw), lambda j, b: (0, b, j)),
                  pl.BlockSpec((3, cw), lambda j, b: (0, j)), pl.BlockSpec((1, cw), lambda j, b: (0, j))],
        out_specs=(pl.BlockSpec((2, seq, cw), lambda j, b: (0, b, j)), wd_spec, wd_spec,
                   pl.BlockSpec((3, cw), lambda j, b: (0, j)), pl.BlockSpec((1, cw), lambda j, b: (0, j))),
        scratch_shapes=[pltpu.VMEM((seq, cw), F32), pltpu.VMEM((seq, cw), BF16), pltpu.VMEM((seq, cw), F32)],
        compiler_params=_cparams(("arbitrary", "arbitrary"), VMEM_BIG),
    )(dffn, w_down, u, conv_w, conv_b)


def _up_backward(du, w_up, x1, mod3, g_ffn, dx2, mix, seq):
    _, t, _ = du.shape
    d = x1.shape[1]
    tm = TOKEN_TILE // 2
    per_seq = seq // tm
    batch = t // seq
    wcol = w_up.shape[2]

    def body(du_ref, w_ref, x1_ref, mod_ref, g_ref, dx2_ref, mix_ref,
             dx1_ref, dmix_ref, dsh_ref, dsc_ref, dga_ref, dg_ref):
        i = pl.program_id(0)
        dh = jnp.zeros((tm, d), F32)
        for j in range(N_SHARD):
            dh = dh + _mm_nt(du_ref[j // 2, :, (j % 2) * wcol:(j % 2 + 1) * wcol], w_ref[j])
        gate_a = mod_ref[0, :, 2 * d:3 * d]
        scale_f = mod_ref[0, :, 4 * d:5 * d]
        r, xn = _rms_stats(x1_ref[...])
        xg = xn * g_ref[...]
        dxg = dh * (1.0 + scale_f)
        dx1 = dx2_ref[...] + _rms_bwd(dxg * g_ref[...], xn, r)
        dx1_ref[...] = dx1
        dmix_ref[...] = (dx1 * gate_a).astype(BF16)

        @pl.when(i == 0)
        def _():
            dg_ref[...] = jnp.zeros_like(dg_ref)

        @pl.when(i % per_seq == 0)
        def _():
            dsh_ref[...] = jnp.zeros_like(dsh_ref)
            dsc_ref[...] = jnp.zeros_like(dsc_ref)
            dga_ref[...] = jnp.zeros_like(dga_ref)

        dg_ref[...] += jnp.sum(dxg * xn, axis=0, keepdims=True)
        dsh_ref[0] += jnp.sum(dh, axis=0, keepdims=True)
        dsc_ref[0] += jnp.sum(dh * xg, axis=0, keepdims=True)
        dga_ref[0] += jnp.sum(dx1 * mix_ref[...], axis=0, keepdims=True)

    tile = lambda w: pl.BlockSpec((tm, w), lambda i: (i, 0))
    per_b = pl.BlockSpec((1, 1, d), lambda i: (i // per_seq, 0, 0))
    small = jax.ShapeDtypeStruct((batch, 1, d), F32)
    return pl.pallas_call(
        body, name="up_backward", grid=(t // tm,),
        out_shape=(jax.ShapeDtypeStruct((t, d), F32), jax.ShapeDtypeStruct((t, d), BF16), small, small, small,
                   jax.ShapeDtypeStruct((1, d), F32)),
        in_specs=[pl.BlockSpec((2, tm, D_FF), lambda i: (0, i, 0)),
                  pl.BlockSpec((N_SHARD, d, wcol), lambda i: (0, 0, 0)), tile(d),
                  pl.BlockSpec((1, 1, 6 * d), lambda i: (i // per_seq, 0, 0)),
                  pl.BlockSpec((1, d), lambda i: (0, 0)), tile(d), tile(d)],
        out_specs=(tile(d), tile(d), per_b, per_b, per_b, pl.BlockSpec((1, d), lambda i: (0, 0))),
        compiler_params=_cparams(("arbitrary",), VMEM_BIG),
    )(du, w_up, x1, mod3, g_ffn, dx2, mix)


def _up_weight_grad(h2, du, rider=None):
    t, d = h2.shape
    tk = TOKEN_TILE
    wcol = D_FF // 2
    n_k = t // tk

    def body(h_ref, du_ref, g_ref, gb_ref):
        k = pl.program_id(1)

        @pl.when(k == 0)
        def _():
            g_ref[...] = jnp.zeros_like(g_ref)

        g_ref[0] += _mm_tn(h_ref[...], du_ref[0])

        @pl.when(k == n_k - 1)
        def _():
            gb_ref[...] = g_ref[...].astype(BF16)

    g_spec = pl.BlockSpec((1, d, wcol), lambda j, k: (j, 0, 0))
    return _hosted(
        body, rider, name="up_weight_grad", grid=(N_SHARD, n_k),
        out_shape=[jax.ShapeDtypeStruct((N_SHARD, d, wcol), F32), jax.ShapeDtypeStruct((N_SHARD, d, wcol), BF16)],
        in_specs=[pl.BlockSpec((tk, d), lambda j, k: (k, 0)),
                  pl.BlockSpec((1, tk, wcol), lambda j, k: (j // 2, k, j % 2))],
        out_specs=[g_spec, g_spec], scratch_shapes=[],
        compiler_params=_cparams(("arbitrary", "arbitrary"), VMEM_BIG), args=[h2, du])


def _out_backward(dmix, w_out, oab, oa, ob, g_na, g_sw):
    t, d = dmix.shape
    tm = TOKEN_TILE
    hw = NA_WIDTH

    def body(dm_ref, w_ref, oab_ref, oa_ref, ob_ref, gna_ref, gsw_ref,
             doa_ref, dob_ref, gw_ref, gwb_ref, dgna_ref, dgsw_ref):
        @pl.when(pl.program_id(0) == 0)
        def _():
            gw_ref[...] = jnp.zeros_like(gw_ref)
            dgna_ref[...] = jnp.zeros_like(dgna_ref)
            dgsw_ref[...] = jnp.zeros_like(dgsw_ref)

        dm = dm_ref[...]
        gw_ref[...] += _mm_tn(oab_ref[...], dm)

        @pl.when(pl.program_id(0) == t // tm - 1)
        def _():
            gwb_ref[...] = gw_ref[...].astype(BF16)

        do = _mm_nt(dm, w_ref[...])
        for raw_ref, g_ref, dst_ref, dg_ref, lo in ((oa_ref, gna_ref, doa_ref, dgna_ref, 0),
                                                     (ob_ref, gsw_ref, dob_ref, dgsw_ref, hw)):
            r, xn = _rms_stats(raw_ref[...])
            dpart = do[:, lo:lo + hw]
            dg_ref[...] += jnp.sum(dpart * xn, axis=0, keepdims=True)
            dst_ref[...] = _rms_bwd(dpart * g_ref[...], xn, r).astype(BF16)

    tile = lambda w: pl.BlockSpec((tm, w), lambda i: (i, 0))
    vec = lambda w: pl.BlockSpec((1, w), lambda i: (0, 0))
    return pl.pallas_call(
        body, name="out_backward", grid=(t // tm,),
        out_shape=(jax.ShapeDtypeStruct((t, hw), BF16), jax.ShapeDtypeStruct((t, hw), BF16),
                   jax.ShapeDtypeStruct((d, d), F32), jax.ShapeDtypeStruct((d, d), BF16),
                   jax.ShapeDtypeStruct((1, hw), F32), jax.ShapeDtypeStruct((1, hw), F32)),
        in_specs=[tile(d), pl.BlockSpec((d, d), lambda i: (0, 0)), tile(d), tile(hw), tile(hw), vec(hw), vec(hw)],
        out_specs=(tile(hw), tile(hw), pl.BlockSpec((d, d), lambda i: (0, 0)), pl.BlockSpec((d, d), lambda i: (0, 0)),
                   vec(hw), vec(hw)),
        compiler_params=_cparams(("arbitrary",), VMEM_BIG),
    )(dmix, w_out, oab, oa, ob, g_na, g_sw)


def _na_backward(proj, d_o, tiles, batch, seq, rider=None):
    t = proj.shape[0]
    n_rows = seq // GRID_W
    n_pairs = NA_WIDTH // LANES
    win = NA_ROWS * GRID_W
    n_tiles = 2 * NA_ROWS - 2

    def body(q_ref, k_ref, v_ref, do_ref, tp_ref, dq_ref, dk_ref, dv_ref, dtp_ref, km, vm, dk_acc, dv_acc):
        @pl.when(pl.program_id(1) == 0)
        def _():
            dtp_ref[...] = jnp.zeros_like(dtp_ref)

        _na_prepare(k_ref, v_ref, km, vm)
        dk_acc[...] = jnp.zeros_like(dk_acc)
        dv_acc[...] = jnp.zeros_like(dv_acc)
        low = lax.broadcasted_iota(jnp.int32, (win, LANES), 1) < HEAD_DIM

        def row(r, carry):
            rs, off = _na_window(r, n_rows)
            rows = pl.ds(pl.multiple_of(r * GRID_W, GRID_W), GRID_W)
            wrows = pl.ds(pl.multiple_of(rs * GRID_W, GRID_W), win)
            q = q_ref[rows, :]
            do = do_ref[rows, :]
            k2 = _na_pair_window(km, wrows)
            p = _na_probs(q, k2, tp_ref, off)
            dp = _mm_nt(do, _na_pair_window(vm, wrows))
            parts = []
            for h in range(2):
                ph, dph = p[:, h * win:(h + 1) * win], dp[:, h * win:(h + 1) * win]
                dsh = ph * (dph - jnp.sum(ph * dph, axis=-1, keepdims=True))
                for w in range(NA_ROWS // 2):
                    dtp_ref[h, 2 * w - off + (NA_ROWS - 1)] += dsh[:, w * LANES:(w + 1) * LANES]
                parts.append(dsh)
            dsb = (jnp.concatenate(parts, axis=1) * QK_SCALE).astype(BF16)
            dq_ref[rows, :] = _mm(dsb, k2).astype(BF16)
            dk2 = _mm_tn(dsb, q)
            dv2 = _mm_tn(p.astype(BF16), do)
            dk_acc[wrows, :] += jnp.where(low, dk2[:win], dk2[win:])
            dv_acc[wrows, :] += jnp.where(low, dv2[:win], dv2[win:])
            return carry

        lax.fori_loop(0, n_rows, row, 0, unroll=2)
        dk_ref[...] = dk_acc[...].astype(BF16)
        dv_ref[...] = dv_acc[...].astype(BF16)

    blk = lambda off: pl.BlockSpec((seq, LANES), lambda p, b: (b, off + p))
    out = jax.ShapeDtypeStruct((t, NA_WIDTH), BF16)
    return _hosted(
        body, rider, name="na_backward", grid=(n_pairs, batch),
        out_shape=[out, out, out, jax.ShapeDtypeStruct(tiles.shape, F32)],
        in_specs=[blk(0), blk(n_pairs), blk(2 * n_pairs), blk(0),
                  pl.BlockSpec((2, n_tiles, GRID_W, LANES), lambda p, b: (p, 0, 0, 0))],
        out_specs=[blk(0), blk(0), blk(0), pl.BlockSpec((2, n_tiles, GRID_W, LANES), lambda p, b: (p, 0, 0, 0))],
        scratch_shapes=[pltpu.VMEM((2, seq, LANES), BF16), pltpu.VMEM((2, seq, LANES), BF16),
                        pltpu.VMEM((seq, LANES), F32), pltpu.VMEM((seq, LANES), F32)],
        compiler_params=_cparams(("arbitrary", "arbitrary")), args=[proj, proj, proj, d_o, tiles])


def _na_bias_grad(dtiles_flat, expand):
    n = dtiles_flat.shape[0]

    def body(t_ref, e_ref, o_ref):
        o_ref[...] = lax.dot_general(t_ref[...], e_ref[...], (((1,), (1,)), ((), ())),
                                     precision=lax.Precision.HIGHEST, preferred_element_type=F32)

    return pl.pallas_call(
        body, name="na_bias_grad",
        out_shape=jax.ShapeDtypeStruct((n, expand.shape[0]), F32),
        compiler_params=_cparams(vmem=VMEM_BIG),
    )(dtiles_flat, expand)


def _sw_backward(proj, d_o, sink, batch, seq, rider=None):
    t = proj.shape[0]
    n_pairs = SW_WIDTH // LANES
    q_blk = 3 * NA_WIDTH // LANES
    k_blk = q_blk + n_pairs
    n_blocks = seq // SW_BLOCK
    pad = seq + 2 * SW_BLOCK

    def body(sink_ref, q_ref, k_ref, v_ref, do_ref, dq_ref, dk_ref, dv_ref, dsk_ref,
             k_lo, k_hi, v_lo, v_hi, dk_loc, dv_loc, dk_tot, dv_tot):
        hp = pl.program_id(1)
        g = hp // 2
        _sw_prepare(k_ref, g, k_lo, k_hi, seq)
        _sw_prepare(v_ref, g, v_lo, v_hi, seq)
        dk_loc[...] = jnp.zeros_like(dk_loc)
        dv_loc[...] = jnp.zeros_like(dv_loc)

        @pl.when(hp == 0)
        def _():
            dk_tot[...] = jnp.zeros_like(dk_tot)
            dv_tot[...] = jnp.zeros_like(dv_tot)

        band = 3 * SW_BLOCK
        low = lax.broadcasted_iota(jnp.int32, (band, LANES), 1) < HEAD_DIM

        def block(n, carry):
            rows = pl.ds(pl.multiple_of(n * SW_BLOCK, SW_BLOCK), SW_BLOCK)
            wrows = pl.ds(pl.multiple_of(n * SW_BLOCK, SW_BLOCK), band)
            qb = q_ref[rows, :]
            do = do_ref[rows, :]
            k2 = jnp.concatenate([k_lo[wrows, :], k_hi[wrows, :]], axis=0)
            v2 = jnp.concatenate([v_lo[wrows, :], v_hi[wrows, :]], axis=0)
            p, ps = _sw_probs(qb, k2, _sw_mask(n, seq), (sink_ref[2 * hp], sink_ref[2 * hp + 1]))
            dp = _mm_nt(do, v2)
            parts, new = [], []
            for i in range(2):
                ph, dph = p[:, i * band:(i + 1) * band], dp[:, i * band:(i + 1) * band]
                delta = jnp.sum(ph * dph, axis=-1, keepdims=True)
                parts.append(ph * (dph - delta))
                new.append(carry[i] - ps[i] * delta)
            dsb = (jnp.concatenate(parts, axis=1) * QK_SCALE).astype(BF16)
            dq_ref[rows, :] = _mm(dsb, k2)
            dk2 = _mm_tn(dsb, qb)
            dv2 = _mm_tn(p.astype(BF16), do)
            dk_loc[wrows, :] += jnp.where(low, dk2[:band], dk2[band:])
            dv_loc[wrows, :] += jnp.where(low, dv2[:band], dv2[band:])
            return tuple(new)

        zero = jnp.zeros((SW_BLOCK, 1), F32)
        s0, s1 = lax.fori_loop(0, n_blocks, block, (zero, zero), unroll=2)
        row = lax.broadcasted_iota(jnp.int32, (SUBLANES, LANES), 0)
        dsk_ref[0, 0] = jnp.where(row == 0, jnp.sum(s0), jnp.where(row == 1, jnp.sum(s1), 0.0))

        lane_s = lax.broadcasted_iota(jnp.int32, (seq, LANES), 1)
        mine_g = (lane_s // HEAD_DIM) == g
        for loc, tot in ((dk_loc, dk_tot), (dv_loc, dv_tot)):
            part = loc[SW_BLOCK:SW_BLOCK + seq, :]
            tot[...] += jnp.where(mine_g, part + pltpu.roll(part, HEAD_DIM, 1), 0.0)

        @pl.when(hp == n_pairs - 1)
        def _():
            dk_ref[...] = dk_tot[...]
            dv_ref[...] = dv_tot[...].astype(BF16)

    return _hosted(
        body, rider, name="sw_backward", grid=(batch, n_pairs),
        out_shape=[jax.ShapeDtypeStruct((t, SW_WIDTH), F32), jax.ShapeDtypeStruct((t, LANES), F32),
                   jax.ShapeDtypeStruct((t, LANES), BF16), jax.ShapeDtypeStruct((batch, n_pairs, SUBLANES, LANES), F32)],
        in_specs=[pl.BlockSpec(memory_space=pltpu.SMEM),
                  pl.BlockSpec((seq, LANES), lambda b, p: (b, q_blk + p)),
                  pl.BlockSpec((seq, LANES), lambda b, p: (b, k_blk)),
                  pl.BlockSpec((seq, LANES), lambda b, p: (b, k_blk + 1)),
                  pl.BlockSpec((seq, LANES), lambda b, p: (b, p))],
        out_specs=[pl.BlockSpec((seq, LANES), lambda b, p: (b, p)), pl.BlockSpec((seq, LANES), lambda b, p: (b, 0)),
                   pl.BlockSpec((seq, LANES), lambda b, p: (b, 0)),
                   pl.BlockSpec((1, 1, SUBLANES, LANES), lambda b, p: (b, p, 0, 0))],
        scratch_shapes=[pltpu.VMEM((pad, LANES), BF16)] * 4 + [pltpu.VMEM((pad, LANES), F32)] * 2
        + [pltpu.VMEM((seq, LANES), F32)] * 2,
        compiler_params=_cparams(("arbitrary", "arbitrary")), args=[sink, proj, proj, proj, d_o])


def _in_backward(dqkv_a, dq_b, dk_b, dv_b, w_in_t, h1, x, mod3, g_attn, dx1, cos_t, sin_t, seq):
    t, d = x.shape
    tm = TOKEN_TILE // 2
    per_seq = seq // tm
    batch = t // seq
    dqa, dka, dva = dqkv_a
    n_q = SW_WIDTH // LANES

    def body(dqa_ref, dka_ref, dva_ref, dqb_ref, dkb_ref, dvb_ref, w_ref, h_ref, x_ref, mod_ref, g_ref, dx1_ref,
             cos_ref, sin_ref, dx_ref, gw_ref, gwb_ref, dsh_ref, dsc_ref, dg_ref):
        i = pl.program_id(0)

        @pl.when(i == 0)
        def _():
            gw_ref[...] = jnp.zeros_like(gw_ref)
            dg_ref[...] = jnp.zeros_like(dg_ref)

        @pl.when(i % per_seq == 0)
        def _():
            dsh_ref[...] = jnp.zeros_like(dsh_ref)
            dsc_ref[...] = jnp.zeros_like(dsc_ref)

        dr = jnp.concatenate([dqb_ref[...], dkb_ref[...]], axis=1)
        cos = jnp.concatenate([cos_ref[...]] * (n_q + 1), axis=1)
        sin = jnp.concatenate([sin_ref[...]] * (n_q + 1), axis=1)
        dr = dr * cos + _rope_rot(dr * sin)
        dproj = jnp.concatenate([dqa_ref[...], dka_ref[...], dva_ref[...], dr.astype(BF16), dvb_ref[...]], axis=1)
        gw_ref[...] += _mm_tn(dproj, h_ref[...])

        @pl.when(i == t // tm - 1)
        def _():
            gwb_ref[...] = gw_ref[...].astype(BF16)

        dh = _mm(dproj, w_ref[...])
        scale = mod_ref[0, :, d:2 * d]
        r, xn = _rms_stats(x_ref[...])
        xg = xn * g_ref[...]
        dxg = dh * (1.0 + scale)
        dx_ref[...] = dx1_ref[...] + _rms_bwd(dxg * g_ref[...], xn, r)
        dg_ref[...] += jnp.sum(dxg * xn, axis=0, keepdims=True)
        dsh_ref[0] += jnp.sum(dh, axis=0, keepdims=True)
        dsc_ref[0] += jnp.sum(dh * xg, axis=0, keepdims=True)

    tile = lambda w: pl.BlockSpec((tm, w), lambda i: (i, 0))
    per_b = pl.BlockSpec((1, 1, d), lambda i: (i // per_seq, 0, 0))
    small = jax.ShapeDtypeStruct((batch, 1, d), F32)
    rope = pl.BlockSpec((tm, LANES), lambda i: (i % per_seq, 0))
    return pl.pallas_call(
        body, name="in_backward", grid=(t // tm,),
        out_shape=(jax.ShapeDtypeStruct((t, d), F32), jax.ShapeDtypeStruct((IN_WIDTH, d), F32),
                   jax.ShapeDtypeStruct((IN_WIDTH, d), BF16), small, small, jax.ShapeDtypeStruct((1, d), F32)),
        in_specs=[tile(NA_WIDTH), tile(NA_WIDTH), tile(NA_WIDTH), tile(SW_WIDTH), tile(LANES), tile(LANES),
                  pl.BlockSpec((IN_WIDTH, d), lambda i: (0, 0)), tile(d), tile(d),
                  pl.BlockSpec((1, 1, 6 * d), lambda i: (i // per_seq, 0, 0)),
                  pl.BlockSpec((1, d), lambda i: (0, 0)), tile(d), rope, rope],
        out_specs=(tile(d), pl.BlockSpec((IN_WIDTH, d), lambda i: (0, 0)), pl.BlockSpec((IN_WIDTH, d), lambda i: (0, 0)),
                   per_b, per_b, pl.BlockSpec((1, d), lambda i: (0, 0))),
        compiler_params=_cparams(("arbitrary",), VMEM_BIG),
    )(dqa, dka, dva, dq_b, dk_b, dv_b, w_in_t, h1, x, mod3, g_attn, dx1, cos_t, sin_t)


def _ada_weight_grad(sc_all, dmod_cols):
    d = sc_all.shape[1]
    ncol = dmod_cols.shape[1]

    def body(s_ref, m_ref, o_ref):
        o_ref[...] = _mm_tn(s_ref[...].astype(BF16), m_ref[...].astype(BF16))

    return pl.pallas_call(
        body, name="ada_weight_grad",
        out_shape=jax.ShapeDtypeStruct((d, ncol), F32),
        compiler_params=_cparams(vmem=VMEM_BIG),
    )(sc_all, dmod_cols)


def _row_tile(rows, cols):
    target = max(SUBLANES, (1 << 20) // (4 * cols))
    best = rows
    for cand in range(SUBLANES, rows + 1, SUBLANES):
        if rows % cand == 0 and cand <= target:
            best = cand
    return best if rows % SUBLANES == 0 else rows


def _sum_slots(recv, own, name):
    _, rows, cols = recv.shape
    tr = _row_tile(rows, cols)

    def body(p_ref, own_ref, o_ref):
        o_ref[...] = ((own_ref[...] + p_ref[0].astype(F32)) + p_ref[1].astype(F32)) + p_ref[2].astype(F32)

    return pl.pallas_call(
        body, name=name, grid=(rows // tr,),
        out_shape=jax.ShapeDtypeStruct((rows, cols), F32),
        in_specs=[pl.BlockSpec((N_SHARD - 1, tr, cols), lambda i: (0, i, 0)), pl.BlockSpec((tr, cols), lambda i: (i, 0))],
        out_specs=pl.BlockSpec((tr, cols), lambda i: (i, 0)),
        compiler_params=_cparams(("arbitrary",)),
    )(recv, own)


def _adamw(w, grads, m, v, name):
    rows, cols = w.shape
    tr = _row_tile(rows, cols)
    ng = len(grads)

    def body(*refs):
        w_ref = refs[0]
        g_refs = refs[1:1 + ng]
        m_ref, v_ref = refs[1 + ng], refs[2 + ng]
        g_out, d_out, m_out, v_out = refs[3 + ng:]
        g = g_refs[0][...]
        for extra in g_refs[1:]:
            g = g + extra[...]
        g_out[...] = g
        m2 = ADAM_B1 * m_ref[...] + (1.0 - ADAM_B1) * g
        v2 = ADAM_B2 * v_ref[...] + (1.0 - ADAM_B2) * (g * g)
        m_out[...] = m2
        v_out[...] = v2
        m_hat = m2 / (1.0 - ADAM_B1 ** ADAM_STEP)
        v_hat = v2 / (1.0 - ADAM_B2 ** ADAM_STEP)
        d_out[...] = -ADAM_LR * (m_hat / (jnp.sqrt(v_hat) + ADAM_EPS) + ADAM_WD * w_ref[...])

    spec = pl.BlockSpec((tr, cols), lambda i: (i, 0))
    out = jax.ShapeDtypeStruct((rows, cols), F32)
    return pl.pallas_call(
        body, name=name, grid=(rows // tr,),
        out_shape=(out, out, out, out),
        in_specs=[spec] * (3 + ng), out_specs=(spec, spec, spec, spec),
        compiler_params=_cparams(("arbitrary",)),
    )(w, *grads, m, v)


def _pack_rows(arrays):
    tile = SUBLANES * LANES
    rows, offsets, at = [], [], 0
    for a in arrays:
        flat = a.reshape(-1).astype(F32)
        n = -(-flat.shape[0] // tile) * tile
        rows.append(jnp.pad(flat, (0, n - flat.shape[0])).reshape(-1, LANES))
        offsets.append(at)
        at += n // LANES
    return jnp.concatenate(rows, axis=0), offsets


def _unpack_rows(packed, offsets, shapes):
    out = []
    for off, shape in zip(offsets, shapes):
        n = 1
        for s in shape:
            n *= s
        nrow = -(-n // LANES)
        out.append(packed[off:off + nrow].reshape(-1)[:n].reshape(shape))
    return out


def _rope_tables(seq):
    half = HEAD_DIM // 2
    inv = ROPE_THETA ** (-jnp.arange(half, dtype=F32) / half)
    ang = jnp.arange(seq).astype(F32)[:, None] * inv[None, :]
    cos, sin = jnp.cos(ang), jnp.sin(ang)
    cos_t = jnp.concatenate([cos, cos, cos, cos], axis=1)
    sin_t = jnp.concatenate([-sin, sin, -sin, sin], axis=1)
    return cos_t, sin_t


def kernel(x, c, w_ada, b_ada, g_attn, w_in, na_rpb, sw_sink, g_na_out, g_sw_out, w_out, g_ffn, w_up, conv_w, conv_b, w_down, g_final, loss_target, m_w_ada, m_b_ada, m_g_attn, m_w_in, m_na_rpb, m_sw_sink, m_g_na_out, m_g_sw_out, m_w_out, m_g_ffn, m_w_up, m_conv_w, m_conv_b, m_w_down, m_g_final, v_w_ada, v_b_ada, v_g_attn, v_w_in, v_na_rpb, v_sw_sink, v_g_na_out, v_g_sw_out, v_w_out, v_g_ffn, v_w_up, v_conv_w, v_conv_b, v_w_down, v_g_final):
    batch, seq, d = x.shape
    t = batch * seq
    assert d == D_MODEL and seq % (NA_ROWS * GRID_W) == 0 and seq % TOKEN_TILE == 0 and batch <= SUBLANES
    shard = 2 * lax.axis_index("x") + lax.axis_index("y")
    xt = x.reshape(t, d)
    tgt = loss_target.reshape(t, d)

    c8 = jnp.pad(c, ((0, SUBLANES - batch), (0, 0)))
    w_in_t_s = jnp.transpose(w_in[0]).astype(BF16)
    (mod8, sc_all), (w_in_g,) = _ada_forward(c8, w_ada[0], b_ada, _Rider("gather", [w_in_t_s]))
    mod3 = mod8[:batch].reshape(batch, 1, 6 * d)
    w_in_t = w_in_g.reshape(IN_WIDTH, d)

    cos_t, sin_t = _rope_tables(seq)
    h1, proj = _in_proj(xt, mod3, g_attn, w_in_t, cos_t, sin_t, seq)
    n_heads = NA_WIDTH // HEAD_DIM
    n_tiles, n_dc = 2 * NA_ROWS - 2, 2 * NA_COLS - 1
    expand, neg_mask = _na_bias_pattern()
    rpb = na_rpb[0]
    rows2 = jnp.concatenate([rpb[:, :-1, :], rpb[:, 1:, :]], axis=2).reshape(n_heads * n_tiles, 2 * n_dc)
    rows2 = jnp.pad(rows2, ((0, 0), (0, GRID_W - 2 * n_dc)))
    tiles = _na_bias_tiles(rows2, expand, neg_mask).reshape(n_heads, n_tiles, GRID_W, LANES)
    sink = sw_sink[0]
    (oa,), (w_out_g, w_up_f) = _na_forward(proj, tiles, batch, seq,
                                            _Rider("gather", [w_out[0].astype(BF16), w_up[0].astype(BF16)]))
    (ob,), (w_down_g, conv_w_g) = _sw_forward(proj, sink, batch, seq,
                                              _Rider("gather", [w_down[0].astype(BF16), conv_w[0]]))
    w_out_f = w_out_g.reshape(d, d)
    w_down_f = w_down_g.reshape(D_FF, d)
    conv_w_f = jnp.transpose(conv_w_g, (1, 0, 2)).reshape(3, D_FF)
    oab, mix, x1, h2 = _out_proj(oa, ob, g_na_out, g_sw_out, w_out_f, xt, mod3, g_ffn, seq)
    u = _up_proj(h2, w_up_f)
    a = _conv_gate(u, conv_w_f, conv_b, batch, seq)
    dx2, dffn, loss_part, dgate_f, dg_final = _down_and_loss(a, w_down_f, x1, mod3, g_final.reshape(1, d), tgt, seq)

    du, gw_down, gw_down_b, gconv_w, gconv_b = _ffn_backward(dffn, w_down_f, u, conv_w_f, conv_b, batch, seq)
    blocks = lambda g, rows: g.reshape(N_SHARD, rows // N_SHARD, d)
    (gw_up, gw_up_b), (recv_down, own_down) = _up_weight_grad(
        h2, du, _Rider("scatter", [blocks(gw_down_b, D_FF)], [blocks(gw_down, D_FF)]))
    dx1, dmix, dshift_f, dscale_f, dgate_a, dg_ffn = _up_backward(du, w_up_f, x1, mod3, g_ffn, dx2, mix, seq)
    doa, dob, gw_out, gw_out_b, dg_na, dg_sw = _out_backward(dmix, w_out_f, oab, oa, ob, g_na_out, g_sw_out)
    (dqa, dka, dva, dtiles), (recv_up, own_up) = _na_backward(
        proj, doa, tiles, batch, seq, _Rider("scatter", [gw_up_b], [gw_up]))
    (dq_b, dk_b, dv_b, dsink_parts), (recv_out, own_out) = _sw_backward(
        proj, dob, sink, batch, seq, _Rider("scatter", [blocks(gw_out_b, d)], [blocks(gw_out, d)]))
    gx, gw_in_t, gw_in_b, dshift_a, dscale_a, dg_attn = _in_backward(
        (dqa, dka, dva), dq_b, dk_b, dv_b, w_in_t, h1, xt, mod3, g_attn, dx1, cos_t, sin_t, seq)
    recv_in, own_in = _ride_alone(_Rider("scatter", [blocks(gw_in_b, IN_WIDTH)], [blocks(gw_in_t, IN_WIDTH)]),
                                  "scatter_w_in")
    mine = [_sum_slots(r, o, name) for r, o, name in ((recv_in, own_in, "sum_w_in"), (recv_out, own_out, "sum_w_out"),
                                                      (recv_up, own_up, "sum_w_up"), (recv_down, own_down, "sum_w_down"))]
    theirs = _ride_alone(_Rider("swap", mine), "swap_sibling")

    red = _na_bias_grad(dtiles.reshape(n_heads * n_tiles, GRID_W * LANES), expand)[:, :2 * n_dc]
    red = red.reshape(n_heads, n_tiles, 2, n_dc)
    zero_row = jnp.zeros((n_heads, 1, n_dc), F32)
    g_rpb = (jnp.concatenate([red[:, :, 0, :], zero_row], axis=1)
             + jnp.concatenate([zero_row, red[:, :, 1, :]], axis=1))
    g_sink = jnp.sum(dsink_parts[:, :, :2, 0], axis=0).reshape(SW_WIDTH // HEAD_DIM)

    dmod = jnp.concatenate([dshift_a, dscale_a, dgate_a, dshift_f, dscale_f, dgate_f], axis=2).reshape(batch, 6 * d)
    dmod8 = jnp.pad(dmod, ((0, SUBLANES - batch), (0, 0)))
    small_parts = [jnp.sum(dmod, axis=0), dg_attn, g_rpb, g_sink, dg_na, dg_sw, dg_ffn, gconv_w, gconv_b, dg_final,
                   loss_part[0, 0:1]]
    packed, offsets = _pack_rows(small_parts + [dmod8])
    summed, every = _allreduce_small(packed)
    small_shapes = [(1, 6 * d), (1, d), na_rpb.shape, sw_sink.shape, (1, NA_WIDTH), (1, SW_WIDTH), (1, d),
                    (3, D_FF), (1, D_FF), (d,), ()]
    (g_b_ada, g_g_attn, g_na_rpb, g_sw_sink, g_g_na, g_g_sw, g_g_ffn, g_conv_w_full, g_conv_b, g_g_final,
     loss) = _unpack_rows(summed, offsets[:-1], small_shapes)
    dmod_rows = every[:, offsets[-1]:offsets[-1] + SUBLANES * 6 * d // LANES, :].reshape(N_DEV * SUBLANES, 6 * d)
    ncol = w_ada.shape[2]
    g_w_ada = _ada_weight_grad(sc_all, lax.dynamic_slice(dmod_rows, (0, shard * ncol), (N_DEV * SUBLANES, ncol)))
    cshard = conv_w.shape[2]
    g_conv_w = lax.dynamic_slice(g_conv_w_full, (0, shard * cshard), (3, cshard)).reshape(conv_w.shape)

    def big(w, m, v, g_parts, name):
        shape = w.shape
        outs = _adamw(w[0], g_parts, m[0], v[0], name)
        return [o.reshape(shape) for o in outs]

    r_w_ada = big(w_ada, m_w_ada, v_w_ada, [g_w_ada], "adamw_w_ada")
    r_w_in = big(w_in, m_w_in, v_w_in, [jnp.transpose(mine[0]), jnp.transpose(theirs[0])], "adamw_w_in")
    r_w_out = big(w_out, m_w_out, v_w_out, [mine[1], theirs[1]], "adamw_w_out")
    r_w_up = big(w_up, m_w_up, v_w_up, [mine[2], theirs[2]], "adamw_w_up")
    r_w_down = big(w_down, m_w_down, v_w_down, [mine[3], theirs[3]], "adamw_w_down")

    small_w = [b_ada, g_attn, na_rpb, sw_sink, g_na_out, g_sw_out, g_ffn, conv_w, conv_b, g_final]
    small_m = [m_b_ada, m_g_attn, m_na_rpb, m_sw_sink, m_g_na_out, m_g_sw_out, m_g_ffn, m_conv_w, m_conv_b, m_g_final]
    small_v = [v_b_ada, v_g_attn, v_na_rpb, v_sw_sink, v_g_na_out, v_g_sw_out, v_g_ffn, v_conv_w, v_conv_b, v_g_final]
    small_g = [g_b_ada, g_g_attn, g_na_rpb, g_sw_sink, g_g_na, g_g_sw, g_g_ffn, g_conv_w, g_conv_b, g_g_final]
    pw, offs = _pack_rows(small_w)
    pg, _ = _pack_rows(small_g)
    pm, _ = _pack_rows(small_m)
    pv, _ = _pack_rows(small_v)
    shapes = [w.shape for w in small_w]
    r_small = [_unpack_rows(o, offs, shapes) for o in _adamw(pw, [pg], pm, pv, "adamw_small")]

    def pick(k):
        b_, ga_, rpb_, sk_, gna_, gsw_, gf_, cw_, cb_, gfin_ = r_small[k]
        return [r_w_ada[k], b_, ga_, r_w_in[k], rpb_, sk_, gna_, gsw_, r_w_out[k], gf_, r_w_up[k], cw_, cb_,
                r_w_down[k], gfin_]

    return (loss, gx.reshape(batch, seq, d), *pick(0), *pick(1), *pick(2), *pick(3))
```

```python
import functools

import jax
import jax.numpy as jnp
from jax import lax
from jax.experimental import pallas as pl
from jax.experimental.pallas import tpu as pltpu

F32 = jnp.float32
BF16 = jnp.bfloat16
MESH = pl.DeviceIdType.MESH

D_MODEL = 1024
HEAD_DIM = 64
NA_WIDTH = 512
SW_WIDTH = 512
SW_KV_WIDTH = 128
IN_WIDTH = 2304
D_FF = 2816
GRID_W = 64
NA_ROWS = 8
NA_COLS = 16
SW_BLOCK = 128
ROPE_THETA = 10000.0
EPS = 1e-6
NEG = -1e30
QK_SCALE = HEAD_DIM ** -0.5

ADAM_LR = 0.001
ADAM_B1 = 0.9
ADAM_B2 = 0.999
ADAM_EPS = 1e-08
ADAM_WD = 0.01
ADAM_STEP = 10

N_SHARD = 4
N_DEV = 8
LANES = 128
SUBLANES = 8
TOKEN_TILE = 512
FF_TILE = 256
CONV_CHUNK = 64
VMEM_BIG = 56 * 1024 * 1024


def _mm(a, b):
    return jnp.dot(a, b, preferred_element_type=F32)


def _mm_nt(a, b):
    return lax.dot_general(a, b, (((1,), (1,)), ((), ())), preferred_element_type=F32)


def _mm_tn(a, b):
    return lax.dot_general(a, b, (((0,), (0,)), ((), ())), preferred_element_type=F32)


def _cparams(sem=None, vmem=None):
    kw = {}
    if sem is not None:
        kw["dimension_semantics"] = sem
    if vmem is not None:
        kw["vmem_limit_bytes"] = vmem
    return pltpu.CompilerParams(**kw)


def _sigmoid(x):
    return 1.0 / (1.0 + jnp.exp(-x))


def _rms_stats(x):
    r = lax.rsqrt(jnp.mean(x * x, axis=-1, keepdims=True) + EPS)
    return r, x * r


def _rms_bwd(dxn, xn, r):
    return r * (dxn - xn * jnp.mean(dxn * xn, axis=-1, keepdims=True))


def _my_pos():
    return lax.axis_index("x"), lax.axis_index("y"), lax.axis_index("c")


def _flip(v, bit):
    return 1 - v if bit else v


def _ada_forward(c8, w_ada, b_ada, rider):
    d = c8.shape[1]
    ncol = w_ada.shape[1]

    def body(c_ref, w_ref, b_ref, mod_ref, sc_ref, m_scr, mod_buf, ssem, rsem, ssem2, rsem2):
        x, y, c = _my_pos()
        me = 4 * x + 2 * y + c
        shard = 2 * x + y
        cv = c_ref[...]
        my_rows = pl.ds(pl.multiple_of(me * SUBLANES, SUBLANES), SUBLANES)
        sc_ref[my_rows, :] = cv * _sigmoid(cv)

        def copy1(k):
            peer = (_flip(x, (k >> 2) & 1), _flip(y, (k >> 1) & 1), _flip(c, k & 1))
            return pltpu.make_async_remote_copy(
                src_ref=sc_ref.at[my_rows, :], dst_ref=sc_ref.at[my_rows, :],
                send_sem=ssem.at[k - 1], recv_sem=rsem.at[k - 1], device_id=peer, device_id_type=MESH)

        sends = [copy1(k) for k in range(1, N_DEV)]
        for cp in sends:
            cp.start()
        for cp in sends:
            cp.wait_recv()
        m_scr[...] = _mm(sc_ref[...].astype(BF16), w_ref[...].astype(BF16))

        def copy2(k):
            px, py = _flip(x, (k >> 1) & 1), _flip(y, k & 1)
            rows = pl.ds(pl.multiple_of((4 * px + 2 * py + c) * SUBLANES, SUBLANES), SUBLANES)
            return pltpu.make_async_remote_copy(
                src_ref=m_scr.at[rows, :], dst_ref=mod_buf.at[shard],
                send_sem=ssem2.at[k - 1], recv_sem=rsem2.at[k - 1], device_id=(px, py, c), device_id_type=MESH)

        sends2 = [copy2(k) for k in range(1, N_SHARD)]
        for cp in sends2:
            cp.start()
        mod_buf[shard] = m_scr[my_rows, :]
        for cp in sends2:
            cp.wait_recv()
        for s in range(N_SHARD):
            mod_ref[:, s * ncol:(s + 1) * ncol] = mod_buf[s] + b_ref[:, s * ncol:(s + 1) * ncol]
        for cp in sends + sends2:
            cp.wait_send()

    vm = pl.BlockSpec(memory_space=pltpu.VMEM)
    return _hosted(
        body, rider, name="ada_forward", grid=(),
        out_shape=(jax.ShapeDtypeStruct((SUBLANES, N_SHARD * ncol), F32),
                   jax.ShapeDtypeStruct((N_DEV * SUBLANES, d), F32)),
        in_specs=[vm, vm, vm], out_specs=(vm, vm),
        scratch_shapes=[pltpu.VMEM((N_DEV * SUBLANES, ncol), F32), pltpu.VMEM((N_SHARD, SUBLANES, ncol), F32),
                        pltpu.SemaphoreType.DMA((N_DEV - 1,)), pltpu.SemaphoreType.DMA((N_DEV - 1,)),
                        pltpu.SemaphoreType.DMA((N_SHARD - 1,)), pltpu.SemaphoreType.DMA((N_SHARD - 1,))],
        compiler_params=_cparams(vmem=VMEM_BIG), args=[c8, w_ada, b_ada])


class _Rider:
    def __init__(self, kind, srcs, owns=()):
        self.kind, self.srcs, self.owns = kind, list(srcs), list(owns)
        n = len(self.srcs)
        sds = jax.ShapeDtypeStruct
        dma = pltpu.SemaphoreType.DMA
        if kind == "gather":
            self.out_shapes = [sds((N_SHARD,) + s.shape, s.dtype) for s in self.srcs]
            self.sems = [dma((n, N_SHARD - 1)), dma((n, N_SHARD - 1)), dma((n,))]
        elif kind == "scatter":
            self.out_shapes = ([sds((N_SHARD - 1,) + s.shape[1:], s.dtype) for s in self.srcs]
                               + [sds(o.shape[1:], o.dtype) for o in self.owns])
            self.sems = [dma((n, N_SHARD - 1)), dma((n, N_SHARD - 1)), dma((max(len(self.owns), 1),))]
        else:
            self.out_shapes = [sds(s.shape, s.dtype) for s in self.srcs]
            self.sems = [dma((n,)), dma((n,))]

    @property
    def inputs(self):
        return self.srcs + self.owns

    def copies(self, ins, outs, sems):
        n = len(self.srcs)
        x, y, c = _my_pos()
        shard = 2 * x + y
        local, remote = [], []
        if self.kind == "swap":
            ssem, rsem = sems
            for i in range(n):
                remote.append(pltpu.make_async_remote_copy(
                    src_ref=ins[i], dst_ref=outs[i], send_sem=ssem.at[i], recv_sem=rsem.at[i],
                    device_id=(x, y, 1 - c), device_id_type=MESH))
            return local, remote
        ssem, rsem, lsem = sems
        for i in range(n):
            if self.kind == "gather":
                local.append(pltpu.make_async_copy(ins[i], outs[i].at[shard], lsem.at[i]))
            for k in range(1, N_SHARD):
                px, py = _flip(x, (k >> 1) & 1), _flip(y, k & 1)
                if self.kind == "gather":
                    src, dst = ins[i], outs[i].at[shard]
                else:
                    src, dst = ins[i].at[2 * px + py], outs[i].at[k - 1]
                remote.append(pltpu.make_async_remote_copy(
                    src_ref=src, dst_ref=dst, send_sem=ssem.at[i, k - 1], recv_sem=rsem.at[i, k - 1],
                    device_id=(px, py, c), device_id_type=MESH))
        if self.kind == "scatter":
            for i in range(len(self.owns)):
                local.append(pltpu.make_async_copy(ins[n + i].at[shard], outs[n + i], lsem.at[i]))
        return local, remote

    def start(self, ins, outs, sems):
        local, remote = self.copies(ins, outs, sems)
        for cp in local + remote:
            cp.start()

    def wait(self, ins, outs, sems):
        local, remote = self.copies(ins, outs, sems)
        for cp in remote:
            cp.wait_recv()
        for cp in remote:
            cp.wait_send()
        for cp in local:
            cp.wait()


def _hosted(body, rider, *, name, grid, out_shape, in_specs, out_specs, scratch_shapes, compiler_params, args):
    out_shape, out_specs = list(out_shape), list(out_specs)
    if rider is None:
        outs = pl.pallas_call(body, name=name, grid=grid, out_shape=tuple(out_shape), in_specs=list(in_specs),
                              out_specs=tuple(out_specs), scratch_shapes=list(scratch_shapes),
                              compiler_params=compiler_params)(*args)
        return list(outs), []
    n_in, n_out, n_scr = len(in_specs), len(out_shape), len(scratch_shapes)
    nr_in, nr_out = len(rider.inputs), len(rider.out_shapes)

    def full(*refs):
        ins, refs = refs[:n_in], refs[n_in:]
        r_in, refs = refs[:nr_in], refs[nr_in:]
        outs, refs = refs[:n_out], refs[n_out:]
        r_out, refs = refs[:nr_out], refs[nr_out:]
        scr, sems = refs[:n_scr], refs[n_scr:]
        if grid:
            first = last = None
            for ax, size in enumerate(grid):
                f, l = pl.program_id(ax) == 0, pl.program_id(ax) == size - 1
                first = f if first is None else jnp.logical_and(first, f)
                last = l if last is None else jnp.logical_and(last, l)
            pl.when(first)(lambda: rider.start(r_in, r_out, sems))
            body(*ins, *outs, *scr)
            pl.when(last)(lambda: rider.wait(r_in, r_out, sems))
        else:
            rider.start(r_in, r_out, sems)
            body(*ins, *outs, *scr)
            rider.wait(r_in, r_out, sems)

    hbm = pl.BlockSpec(memory_space=pl.ANY)
    res = pl.pallas_call(
        full, name=name, grid=grid, out_shape=tuple(out_shape + rider.out_shapes),
        in_specs=list(in_specs) + [hbm] * nr_in, out_specs=tuple(out_specs + [hbm] * nr_out),
        scratch_shapes=list(scratch_shapes) + rider.sems, compiler_params=compiler_params,
    )(*args, *rider.inputs)
    return list(res[:n_out]), list(res[n_out:])


def _ride_alone(rider, name):
    return _hosted(lambda: None, rider, name=name, grid=(), out_shape=[], in_specs=[], out_specs=[], scratch_shapes=[],
                   compiler_params=_cparams(), args=[])[1]


def _allreduce_small(packed, rider=None):
    r = packed.shape[0]

    def body(p_ref, sum_ref, all_ref, ssem, rsem):
        x, y, c = _my_pos()
        me = 4 * x + 2 * y + c
        all_ref[me] = p_ref[...]
        cps = []
        for k in range(1, N_DEV):
            peer = (_flip(x, (k >> 2) & 1), _flip(y, (k >> 1) & 1), _flip(c, k & 1))
            cps.append(pltpu.make_async_remote_copy(
                src_ref=all_ref.at[me], dst_ref=all_ref.at[me], send_sem=ssem.at[k - 1], recv_sem=rsem.at[k - 1],
                device_id=peer, device_id_type=MESH))
        for cp in cps:
            cp.start()
        for cp in cps:
            cp.wait_recv()
        acc = all_ref[0]
        for dev in range(1, N_DEV):
            acc = acc + all_ref[dev]
        sum_ref[...] = acc
        for cp in cps:
            cp.wait_send()

    vm = pl.BlockSpec(memory_space=pltpu.VMEM)
    return _hosted(
        body, rider, name="allreduce_small", grid=(),
        out_shape=[jax.ShapeDtypeStruct((r, LANES), F32), jax.ShapeDtypeStruct((N_DEV, r, LANES), F32)],
        in_specs=[vm], out_specs=[vm, vm],
        scratch_shapes=[pltpu.SemaphoreType.DMA((N_DEV - 1,)), pltpu.SemaphoreType.DMA((N_DEV - 1,))],
        compiler_params=_cparams(), args=[packed])


def _rope_rot(t):
    w = t.shape[1]
    lane = lax.broadcasted_iota(jnp.int32, t.shape, 1)
    first = (lane % HEAD_DIM) < (HEAD_DIM // 2)
    return jnp.where(first, pltpu.roll(t, w - HEAD_DIM // 2, 1), pltpu.roll(t, HEAD_DIM // 2, 1))


def _in_proj(x, mod3, g_attn, w_in_t, cos_t, sin_t, seq, rider=None):
    t, d = x.shape
    tm = TOKEN_TILE
    per_seq = seq // tm
    rope_lo, rope_hi = 3 * NA_WIDTH, 3 * NA_WIDTH + SW_WIDTH + SW_KV_WIDTH
    n_rep = (rope_hi - rope_lo) // LANES

    def body(x_ref, mod_ref, g_ref, w_ref, cos_ref, sin_ref, h_ref, p_ref):
        r, xn = _rms_stats(x_ref[...])
        shift, scale = mod_ref[0, :, 0:d], mod_ref[0, :, d:2 * d]
        hb = ((xn * g_ref[...]) * (1.0 + scale) + shift).astype(BF16)
        h_ref[...] = hb
        p_ref[:, :rope_lo] = _mm_nt(hb, w_ref[:rope_lo, :]).astype(BF16)
        pr = _mm_nt(hb, w_ref[rope_lo:rope_hi, :])
        cos = jnp.concatenate([cos_ref[...]] * n_rep, axis=1)
        sin = jnp.concatenate([sin_ref[...]] * n_rep, axis=1)
        p_ref[:, rope_lo:rope_hi] = (pr * cos + _rope_rot(pr) * sin).astype(BF16)
        p_ref[:, rope_hi:] = _mm_nt(hb, w_ref[rope_hi:, :]).astype(BF16)

    return _hosted(
        body, rider, name="in_proj", grid=(t // tm,),
        out_shape=[jax.ShapeDtypeStruct((t, d), BF16), jax.ShapeDtypeStruct((t, IN_WIDTH), BF16)],
        in_specs=[pl.BlockSpec((tm, d), lambda i: (i, 0)),
                  pl.BlockSpec((1, 1, 6 * d), lambda i: (i // per_seq, 0, 0)),
                  pl.BlockSpec((1, d), lambda i: (0, 0)),
                  pl.BlockSpec((IN_WIDTH, d), lambda i: (0, 0)),
                  pl.BlockSpec((tm, LANES), lambda i: (i % per_seq, 0)),
                  pl.BlockSpec((tm, LANES), lambda i: (i % per_seq, 0))],
        out_specs=[pl.BlockSpec((tm, d), lambda i: (i, 0)), pl.BlockSpec((tm, IN_WIDTH), lambda i: (i, 0))],
        scratch_shapes=[], compiler_params=_cparams(("arbitrary",), VMEM_BIG),
        args=[x, mod3, g_attn, w_in_t, cos_t, sin_t])


def _na_bias_pattern():
    n_dc = 2 * NA_COLS - 1
    j = lax.broadcasted_iota(jnp.int32, (GRID_W, GRID_W * LANES), 0)
    m = lax.broadcasted_iota(jnp.int32, (GRID_W, GRID_W * LANES), 1)
    q, lane = m // LANES, m % LANES
    k = lane % GRID_W
    cs = jnp.clip(q - NA_COLS // 2, 0, GRID_W - NA_COLS)
    ok = (k >= cs) & (k < cs + NA_COLS)
    hit = ok & (j < 2 * n_dc) & (lane // GRID_W == j // n_dc) & (k - q + (NA_COLS - 1) == j % n_dc)
    return hit.astype(F32), jnp.where(ok[0:1], 0.0, NEG).astype(F32)


def _na_bias_tiles(rows2, expand, mask):
    n, width = rows2.shape[0], expand.shape[1]
    step = 2048

    def body(r_ref, e_ref, m_ref, o_ref):
        o_ref[...] = jnp.dot(r_ref[...], e_ref[...], precision=lax.Precision.HIGHEST,
                             preferred_element_type=F32) + m_ref[...]

    return pl.pallas_call(
        body, name="na_bias_tiles", grid=(width // step,),
        out_shape=jax.ShapeDtypeStruct((n, width), F32),
        in_specs=[pl.BlockSpec(rows2.shape, lambda i: (0, 0)), pl.BlockSpec((expand.shape[0], step), lambda i: (0, i)),
                  pl.BlockSpec((1, step), lambda i: (0, i))],
        out_specs=pl.BlockSpec((n, step), lambda i: (0, i)),
        compiler_params=_cparams(("arbitrary",)),
    )(rows2, expand, mask)


def _na_prepare(k_ref, v_ref, km, vm):
    lane = lax.broadcasted_iota(jnp.int32, k_ref.shape, 1)
    low = lane < HEAD_DIM
    kv = k_ref[...]
    vv = v_ref[...]
    zero = jnp.zeros_like(kv)
    km[0] = jnp.where(low, kv, zero)
    km[1] = jnp.where(low, zero, kv)
    vm[0] = jnp.where(low, vv, zero)
    vm[1] = jnp.where(low, zero, vv)


def _na_window(r, n_rows):
    rs = jnp.clip(r - NA_ROWS // 2, 0, n_rows - NA_ROWS)
    return rs, r - rs


def _na_pair_window(ref, wrows):
    return jnp.concatenate([ref[0, wrows, :], ref[1, wrows, :]], axis=0)


def _na_probs(q, k2, tp_ref, off):
    win = k2.shape[0] // 2
    s = _mm_nt(q, k2) * QK_SCALE
    bias = jnp.concatenate([tp_ref[h, 2 * w - off + (NA_ROWS - 1)] for h in range(2) for w in range(NA_ROWS // 2)],
                           axis=1)
    s = s + bias
    halves = []
    for h in range(2):
        sh = s[:, h * win:(h + 1) * win]
        e = jnp.exp(sh - jnp.max(sh, axis=-1, keepdims=True))
        halves.append(e / jnp.sum(e, axis=-1, keepdims=True))
    return jnp.concatenate(halves, axis=1)


def _na_forward(proj, tiles, batch, seq, rider=None):
    t = proj.shape[0]
    n_rows = seq // GRID_W
    n_pairs = NA_WIDTH // LANES
    win = NA_ROWS * GRID_W

    def body(q_ref, k_ref, v_ref, tp_ref, o_ref, km, vm):
        _na_prepare(k_ref, v_ref, km, vm)

        def row(r, carry):
            rs, off = _na_window(r, n_rows)
            rows = pl.ds(pl.multiple_of(r * GRID_W, GRID_W), GRID_W)
            wrows = pl.ds(pl.multiple_of(rs * GRID_W, GRID_W), win)
            p = _na_probs(q_ref[rows, :], _na_pair_window(km, wrows), tp_ref, off)
            o_ref[rows, :] = _mm(p.astype(BF16), _na_pair_window(vm, wrows))
            return carry

        lax.fori_loop(0, n_rows, row, 0, unroll=2)

    return _hosted(
        body, rider, name="na_forward", grid=(batch, n_pairs),
        out_shape=[jax.ShapeDtypeStruct((t, NA_WIDTH), F32)],
        in_specs=[pl.BlockSpec((seq, LANES), lambda b, p: (b, p)),
                  pl.BlockSpec((seq, LANES), lambda b, p: (b, n_pairs + p)),
                  pl.BlockSpec((seq, LANES), lambda b, p: (b, 2 * n_pairs + p)),
                  pl.BlockSpec((2, 2 * NA_ROWS - 2, GRID_W, LANES), lambda b, p: (p, 0, 0, 0))],
        out_specs=[pl.BlockSpec((seq, LANES), lambda b, p: (b, p))],
        scratch_shapes=[pltpu.VMEM((2, seq, LANES), BF16), pltpu.VMEM((2, seq, LANES), BF16)],
        compiler_params=_cparams(("arbitrary", "arbitrary")), args=[proj, proj, proj, tiles])


def _sw_prepare(kv_ref, g, dst_lo, dst_hi, seq):
    lane = lax.broadcasted_iota(jnp.int32, kv_ref.shape, 1)
    mine = (lane // HEAD_DIM) == g
    kg = jnp.where(mine, kv_ref[...].astype(F32), 0.0)
    kr = pltpu.roll(kg, HEAD_DIM, 1)
    first = g == 0
    zero = jnp.zeros((SW_BLOCK, LANES), BF16)
    for dst, val in ((dst_lo, jnp.where(first, kg, kr)), (dst_hi, jnp.where(first, kr, kg))):
        dst[0:SW_BLOCK, :] = zero
        dst[SW_BLOCK:SW_BLOCK + seq, :] = val.astype(BF16)
        dst[SW_BLOCK + seq:, :] = zero


def _sw_mask(n, seq):
    qi = lax.broadcasted_iota(jnp.int32, (SW_BLOCK, 3 * SW_BLOCK), 0)
    kj = lax.broadcasted_iota(jnp.int32, (SW_BLOCK, 3 * SW_BLOCK), 1)
    kpos = n * SW_BLOCK - SW_BLOCK + kj
    return (jnp.abs(qi + SW_BLOCK - kj) <= SW_BLOCK) & (kpos >= 0) & (kpos < seq)


def _sw_probs(qb, k2, ok, sinks):
    band = k2.shape[0] // 2
    s2 = _mm_nt(qb, k2) * QK_SCALE
    halves, sink_p = [], []
    for i in range(2):
        s = jnp.where(ok, s2[:, i * band:(i + 1) * band], NEG)
        m = jnp.maximum(jnp.max(s, axis=-1, keepdims=True), sinks[i])
        p = jnp.exp(s - m)
        es = jnp.exp(sinks[i] - m)
        den = jnp.sum(p, axis=-1, keepdims=True) + es
        halves.append(p / den)
        sink_p.append(es / den)
    return jnp.concatenate(halves, axis=1), sink_p


def _sw_forward(proj, sink, batch, seq, rider=None):
    t = proj.shape[0]
    n_pairs = SW_WIDTH // LANES
    q_blk = 3 * NA_WIDTH // LANES
    k_blk = q_blk + n_pairs
    n_blocks = seq // SW_BLOCK
    pad = seq + 2 * SW_BLOCK

    def body(sink_ref, q_ref, k_ref, v_ref, o_ref, k_lo, k_hi, v_lo, v_hi):
        hp = pl.program_id(1)
        g = hp // 2
        _sw_prepare(k_ref, g, k_lo, k_hi, seq)
        _sw_prepare(v_ref, g, v_lo, v_hi, seq)

        def block(n, carry):
            rows = pl.ds(pl.multiple_of(n * SW_BLOCK, SW_BLOCK), SW_BLOCK)
            wrows = pl.ds(pl.multiple_of(n * SW_BLOCK, SW_BLOCK), 3 * SW_BLOCK)
            k2 = jnp.concatenate([k_lo[wrows, :], k_hi[wrows, :]], axis=0)
            v2 = jnp.concatenate([v_lo[wrows, :], v_hi[wrows, :]], axis=0)
            p, _ = _sw_probs(q_ref[rows, :], k2, _sw_mask(n, seq), (sink_ref[2 * hp], sink_ref[2 * hp + 1]))
            o_ref[rows, :] = _mm(p.astype(BF16), v2)
            return carry

        lax.fori_loop(0, n_blocks, block, 0, unroll=2)

    return _hosted(
        body, rider, name="sw_forward", grid=(batch, n_pairs),
        out_shape=[jax.ShapeDtypeStruct((t, SW_WIDTH), F32)],
        in_specs=[pl.BlockSpec(memory_space=pltpu.SMEM),
                  pl.BlockSpec((seq, LANES), lambda b, p: (b, q_blk + p)),
                  pl.BlockSpec((seq, LANES), lambda b, p: (b, k_blk)),
                  pl.BlockSpec((seq, LANES), lambda b, p: (b, k_blk + 1))],
        out_specs=[pl.BlockSpec((seq, LANES), lambda b, p: (b, p))],
        scratch_shapes=[pltpu.VMEM((pad, LANES), BF16)] * 4,
        compiler_params=_cparams(("arbitrary", "arbitrary")), args=[sink, proj, proj, proj])


def _out_proj(oa, ob, g_na, g_sw, w_out, x, mod3, g_ffn, seq):
    t, d = x.shape
    tm = TOKEN_TILE
    per_seq = seq // tm

    def body(oa_ref, ob_ref, gna_ref, gsw_ref, w_ref, x_ref, mod_ref, gf_ref, oab_ref, mix_ref, x1_ref, h2_ref):
        _, na = _rms_stats(oa_ref[...])
        _, nb = _rms_stats(ob_ref[...])
        oab = jnp.concatenate([na * gna_ref[...], nb * gsw_ref[...]], axis=1).astype(BF16)
        oab_ref[...] = oab
        mix = _mm(oab, w_ref[...])
        mix_ref[...] = mix
        gate_a = mod_ref[0, :, 2 * d:3 * d]
        shift_f, scale_f = mod_ref[0, :, 3 * d:4 * d], mod_ref[0, :, 4 * d:5 * d]
        x1 = x_ref[...] + gate_a * mix
        x1_ref[...] = x1
        _, xn = _rms_stats(x1)
        h2_ref[...] = ((xn * gf_ref[...]) * (1.0 + scale_f) + shift_f).astype(BF16)

    tile = lambda w: pl.BlockSpec((tm, w), lambda i: (i, 0))
    vec = lambda w: pl.BlockSpec((1, w), lambda i: (0, 0))
    return pl.pallas_call(
        body, name="out_proj", grid=(t // tm,),
        out_shape=(jax.ShapeDtypeStruct((t, d), BF16), jax.ShapeDtypeStruct((t, d), F32),
                   jax.ShapeDtypeStruct((t, d), F32), jax.ShapeDtypeStruct((t, d), BF16)),
        in_specs=[tile(NA_WIDTH), tile(SW_WIDTH), vec(NA_WIDTH), vec(SW_WIDTH),
                  pl.BlockSpec((d, d), lambda i: (0, 0)), tile(d),
                  pl.BlockSpec((1, 1, 6 * d), lambda i: (i // per_seq, 0, 0)), vec(d)],
        out_specs=(tile(d), tile(d), tile(d), tile(d)),
        compiler_params=_cparams(("arbitrary",), VMEM_BIG),
    )(oa, ob, g_na, g_sw, w_out, x, mod3, g_ffn)


def _up_proj(h2, w_up, rider=None):
    t, d = h2.shape
    tm = TOKEN_TILE
    wcol = w_up.shape[2]

    def body(h_ref, w_ref, u_ref):
        u_ref[0] = _mm(h_ref[...], w_ref[0])

    return _hosted(
        body, rider, name="up_proj", grid=(N_SHARD, t // tm),
        out_shape=[jax.ShapeDtypeStruct((2, t, D_FF), F32)],
        in_specs=[pl.BlockSpec((tm, d), lambda j, i: (i, 0)), pl.BlockSpec((1, d, wcol), lambda j, i: (j, 0, 0))],
        out_specs=[pl.BlockSpec((1, tm, wcol), lambda j, i: (j // 2, i, j % 2))],
        scratch_shapes=[], compiler_params=_cparams(("arbitrary", "arbitrary"), VMEM_BIG), args=[h2, w_up])


def _taps_chunk(load, s, rows, seq):
    cur = load(s, rows)
    above = load(pl.multiple_of(jnp.maximum(s - SUBLANES, 0), SUBLANES), SUBLANES)
    below = load(pl.multiple_of(jnp.minimum(s + rows, seq - SUBLANES), SUBLANES), SUBLANES)
    up = jnp.where(s > 0, above[SUBLANES - 1:SUBLANES, :], 0.0)
    dn = jnp.where(s + rows < seq, below[0:1, :], 0.0)
    row = lax.broadcasted_iota(jnp.int32, cur.shape, 0)
    prev = jnp.where(row == 0, up, pltpu.roll(cur, 1, 0))
    nxt = jnp.where(row == rows - 1, dn, pltpu.roll(cur, rows - 1, 0))
    return cur, prev, nxt


def _conv_gate(u, conv_w, conv_b, batch, seq):
    t = u.shape[1]
    cw = FF_TILE
    rows = CONV_CHUNK

    def body(u_ref, w_ref, b_ref, a_ref):
        def chunk(i, carry):
            s = pl.multiple_of(i * rows, rows)
            gt, prev, nxt = _taps_chunk(lambda at, n: u_ref[1, pl.ds(at, n), :], s, rows, seq)
            gc = prev * w_ref[0:1, :] + gt * w_ref[1:2, :] + nxt * w_ref[2:3, :] + b_ref[...]
            a_ref[pl.ds(s, rows), :] = ((gc * _sigmoid(gc)) * u_ref[0, pl.ds(s, rows), :]).astype(BF16)
            return carry

        lax.fori_loop(0, seq // rows, chunk, 0)

    return pl.pallas_call(
        body, name="conv_gate", grid=(batch, D_FF // cw),
        out_shape=jax.ShapeDtypeStruct((t, D_FF), BF16),
        in_specs=[pl.BlockSpec((2, seq, cw), lambda b, j: (0, b, j)),
                  pl.BlockSpec((3, cw), lambda b, j: (0, j)), pl.BlockSpec((1, cw), lambda b, j: (0, j))],
        out_specs=pl.BlockSpec((seq, cw), lambda b, j: (b, j)),
        compiler_params=_cparams(("arbitrary", "arbitrary"), VMEM_BIG),
    )(u, conv_w, conv_b)


def _down_and_loss(a, w_down, x1, mod3, g_final, target, seq):
    t, d = x1.shape
    tm = TOKEN_TILE
    per_seq = seq // tm
    batch = t // seq

    def body(a_ref, w_ref, x1_ref, mod_ref, g_ref, tgt_ref, dx2_ref, dffn_ref, loss_ref, dgate_ref, dg_ref):
        i = pl.program_id(0)
        f = _mm(a_ref[...], w_ref[...])
        gate_f = mod_ref[0, :, 5 * d:6 * d]
        x2 = x1_ref[...] + gate_f * f
        r, xn = _rms_stats(x2)
        err = xn * g_ref[...] - tgt_ref[...]
        part = 0.5 * jnp.sum(jnp.mean(err * err, axis=-1, keepdims=True))
        dy = err / d
        dx2 = _rms_bwd(dy * g_ref[...], xn, r)
        dx2_ref[...] = dx2
        dffn_ref[...] = (dx2 * gate_f).astype(BF16)

        @pl.when(i == 0)
        def _():
            loss_ref[...] = jnp.zeros_like(loss_ref)
            dg_ref[...] = jnp.zeros_like(dg_ref)

        @pl.when(i % per_seq == 0)
        def _():
            dgate_ref[...] = jnp.zeros_like(dgate_ref)

        loss_ref[...] += part
        dg_ref[...] += jnp.sum(dy * xn, axis=0, keepdims=True)
        dgate_ref[0] += jnp.sum(dx2 * f, axis=0, keepdims=True)

    tile = lambda w: pl.BlockSpec((tm, w), lambda i: (i, 0))
    return pl.pallas_call(
        body, name="down_loss", grid=(t // tm,),
        out_shape=(jax.ShapeDtypeStruct((t, d), F32), jax.ShapeDtypeStruct((t, d), BF16),
                   jax.ShapeDtypeStruct((SUBLANES, LANES), F32), jax.ShapeDtypeStruct((batch, 1, d), F32),
                   jax.ShapeDtypeStruct((1, d), F32)),
        in_specs=[tile(D_FF), pl.BlockSpec((D_FF, d), lambda i: (0, 0)), tile(d),
                  pl.BlockSpec((1, 1, 6 * d), lambda i: (i // per_seq, 0, 0)),
                  pl.BlockSpec((1, d), lambda i: (0, 0)), tile(d)],
        out_specs=(tile(d), tile(d), pl.BlockSpec((SUBLANES, LANES), lambda i: (0, 0)),
                   pl.BlockSpec((1, 1, d), lambda i: (i // per_seq, 0, 0)), pl.BlockSpec((1, d), lambda i: (0, 0))),
        compiler_params=_cparams(("arbitrary",), VMEM_BIG),
    )(a, w_down, x1, mod3, g_final, target)


def _down_weight_grad(a, dffn):
    t, dff = a.shape
    d = dffn.shape[1]
    tk = TOKEN_TILE
    n_k = t // tk

    def body(a_ref, df_ref, g_ref, gb_ref):
        k = pl.program_id(0)

        @pl.when(k == 0)
        def _():
            g_ref[...] = jnp.zeros_like(g_ref)

        g_ref[...] += _mm_tn(a_ref[...], df_ref[...])

        @pl.when(k == n_k - 1)
        def _():
            gb_ref[...] = g_ref[...].astype(BF16)

    whole = pl.BlockSpec((dff, d), lambda k: (0, 0))
    return pl.pallas_call(
        body, name="down_weight_grad", grid=(n_k,),
        out_shape=(jax.ShapeDtypeStruct((dff, d), F32), jax.ShapeDtypeStruct((dff, d), BF16)),
        in_specs=[pl.BlockSpec((tk, dff), lambda k: (k, 0)), pl.BlockSpec((tk, d), lambda k: (k, 0))],
        out_specs=(whole, whole),
        compiler_params=_cparams(("arbitrary",), VMEM_BIG),
    )(a, dffn)


def _ffn_backward(dffn, w_down, u, conv_w, conv_b, batch, seq, rider=None):
    t, d = dffn.shape
    cw = FF_TILE
    rows = CONV_CHUNK

    def body(df_ref, wd_ref, u_ref, w_ref, b_ref, du_ref, gcw_ref, gcb_ref, da_scr, dgc_scr):
        b = pl.program_id(1)
        da_scr[...] = _mm_nt(df_ref[...], wd_ref[...])

        @pl.when(b == 0)
        def _():
            gcw_ref[...] = jnp.zeros_like(gcw_ref)
            gcb_ref[...] = jnp.zeros_like(gcb_ref)

        def fold(v):
            return jnp.sum(v.reshape(rows // SUBLANES, SUBLANES, cw), axis=0)

        def chunk(i, carry):
            s = pl.multiple_of(i * rows, rows)
            here = pl.ds(s, rows)
            gt, prev, nxt = _taps_chunk(lambda at, n: u_ref[1, pl.ds(at, n), :], s, rows, seq)
            val, da = u_ref[0, here, :], da_scr[here, :]
            gc = prev * w_ref[0:1, :] + gt * w_ref[1:2, :] + nxt * w_ref[2:3, :] + b_ref[...]
            sg = _sigmoid(gc)
            sl = gc * sg
            du_ref[0, here, :] = (da * sl).astype(BF16)
            dgc = (da * val) * (sg * (1.0 + gc * (1.0 - sg)))
            dgc_scr[here, :] = dgc
            cb, c0, c1, c2 = carry
            return cb + fold(dgc), c0 + fold(dgc * prev), c1 + fold(dgc * gt), c2 + fold(dgc * nxt)

        zero = jnp.zeros((SUBLANES, cw), F32)
        cb, c0, c1, c2 = lax.fori_loop(0, seq // rows, chunk, (zero, zero, zero, zero))
        gcb_ref[...] += jnp.sum(cb, axis=0, keepdims=True)
        gcw_ref[0:1, :] += jnp.sum(c0, axis=0, keepdims=True)
        gcw_ref[1:2, :] += jnp.sum(c1, axis=0, keepdims=True)
        gcw_ref[2:3, :] += jnp.sum(c2, axis=0, keepdims=True)

        def chunk2(i, carry):
            s = pl.multiple_of(i * rows, rows)
            dgc, dprev, dnxt = _taps_chunk(lambda at, n: dgc_scr[pl.ds(at, n), :], s, rows, seq)
            du_ref[1, pl.ds(s, rows), :] = (dnxt * w_ref[0:1, :] + dgc * w_ref[1:2, :]
                                            + dprev * w_ref[2:3, :]).astype(BF16)
            return carry

        lax.fori_loop(0, seq // rows, chunk2, 0)

    return _hosted(
        body, rider, name="ffn_backward", grid=(D_FF // cw, batch),
        out_shape=[jax.ShapeDtypeStruct((2, t, D_FF), BF16),
                   jax.ShapeDtypeStruct((3, D_FF), F32), jax.ShapeDtypeStruct((1, D_FF), F32)],
        in_specs=[pl.BlockSpec((seq, d), lambda j, b: (b, 0)), pl.BlockSpec((cw, d), lambda j, b: (j, 0)),
                  pl.BlockSpec((2, seq, cw), lambda j, b: (0, b, j)),
                  pl.BlockSpec((3, cw), lambda j, b: (0, j)), pl.BlockSpec((1, cw), lambda j, b: (0, j))],
        out_specs=[pl.BlockSpec((2, seq, cw), lambda j, b: (0, b, j)),
                   pl.BlockSpec((3, cw), lambda j, b: (0, j)), pl.BlockSpec((1, cw), lambda j, b: (0, j))],
        scratch_shapes=[pltpu.VMEM((seq, cw), F32), pltpu.VMEM((seq, cw), F32)],
        compiler_params=_cparams(("arbitrary", "arbitrary"), VMEM_BIG), args=[dffn, w_down, u, conv_w, conv_b])


def _up_backward(du, w_up, x1, mod3, g_ffn, dx2, mix, seq):
    _, t, _ = du.shape
    d = x1.shape[1]
    tm = TOKEN_TILE // 2
    per_seq = seq // tm
    batch = t // seq
    wcol = w_up.shape[2]

    def body(du_ref, w_ref, x1_ref, mod_ref, g_ref, dx2_ref, mix_ref,
             dx1_ref, dmix_ref, dsh_ref, dsc_ref, dga_ref, dg_ref):
        i = pl.program_id(0)
        dh = jnp.zeros((tm, d), F32)
        for j in range(N_SHARD):
            dh = dh + _mm_nt(du_ref[j // 2, :, (j % 2) * wcol:(j % 2 + 1) * wcol], w_ref[j])
        gate_a = mod_ref[0, :, 2 * d:3 * d]
        scale_f = mod_ref[0, :, 4 * d:5 * d]
        r, xn = _rms_stats(x1_ref[...])
        xg = xn * g_ref[...]
        dxg = dh * (1.0 + scale_f)
        dx1 = dx2_ref[...] + _rms_bwd(dxg * g_ref[...], xn, r)
        dx1_ref[...] = dx1
        dmix_ref[...] = (dx1 * gate_a).astype(BF16)

        @pl.when(i == 0)
        def _():
            dg_ref[...] = jnp.zeros_like(dg_ref)

        @pl.when(i % per_seq == 0)
        def _():
            dsh_ref[...] = jnp.zeros_like(dsh_ref)
            dsc_ref[...] = jnp.zeros_like(dsc_ref)
            dga_ref[...] = jnp.zeros_like(dga_ref)

        dg_ref[...] += jnp.sum(dxg * xn, axis=0, keepdims=True)
        dsh_ref[0] += jnp.sum(dh, axis=0, keepdims=True)
        dsc_ref[0] += jnp.sum(dh * xg, axis=0, keepdims=True)
        dga_ref[0] += jnp.sum(dx1 * mix_ref[...], axis=0, keepdims=True)

    tile = lambda w: pl.BlockSpec((tm, w), lambda i: (i, 0))
    per_b = pl.BlockSpec((1, 1, d), lambda i: (i // per_seq, 0, 0))
    small = jax.ShapeDtypeStruct((batch, 1, d), F32)
    return pl.pallas_call(
        body, name="up_backward", grid=(t // tm,),
        out_shape=(jax.ShapeDtypeStruct((t, d), F32), jax.ShapeDtypeStruct((t, d), BF16), small, small, small,
                   jax.ShapeDtypeStruct((1, d), F32)),
        in_specs=[pl.BlockSpec((2, tm, D_FF), lambda i: (0, i, 0)),
                  pl.BlockSpec((N_SHARD, d, wcol), lambda i: (0, 0, 0)), tile(d),
                  pl.BlockSpec((1, 1, 6 * d), lambda i: (i // per_seq, 0, 0)),
                  pl.BlockSpec((1, d), lambda i: (0, 0)), tile(d), tile(d)],
        out_specs=(tile(d), tile(d), per_b, per_b, per_b, pl.BlockSpec((1, d), lambda i: (0, 0))),
        compiler_params=_cparams(("arbitrary",), VMEM_BIG),
    )(du, w_up, x1, mod3, g_ffn, dx2, mix)


def _up_weight_grad(h2, du, rider=None):
    t, d = h2.shape
    tk = TOKEN_TILE
    wcol = D_FF // 2
    n_k = t // tk

    def body(h_ref, du_ref, g_ref, gb_ref):
        k = pl.program_id(1)

        @pl.when(k == 0)
        def _():
            g_ref[...] = jnp.zeros_like(g_ref)

        g_ref[0] += _mm_tn(h_ref[...], du_ref[0])

        @pl.when(k == n_k - 1)
        def _():
            gb_ref[...] = g_ref[...].astype(BF16)

    g_spec = pl.BlockSpec((1, d, wcol), lambda j, k: (j, 0, 0))
    return _hosted(
        body, rider, name="up_weight_grad", grid=(N_SHARD, n_k),
        out_shape=[jax.ShapeDtypeStruct((N_SHARD, d, wcol), F32), jax.ShapeDtypeStruct((N_SHARD, d, wcol), BF16)],
        in_specs=[pl.BlockSpec((tk, d), lambda j, k: (k, 0)),
                  pl.BlockSpec((1, tk, wcol), lambda j, k: (j // 2, k, j % 2))],
        out_specs=[g_spec, g_spec], scratch_shapes=[],
        compiler_params=_cparams(("arbitrary", "arbitrary"), VMEM_BIG), args=[h2, du])


def _out_backward(dmix, w_out, oab, oa, ob, g_na, g_sw):
    t, d = dmix.shape
    tm = TOKEN_TILE
    hw = NA_WIDTH

    def body(dm_ref, w_ref, oab_ref, oa_ref, ob_ref, gna_ref, gsw_ref,
             doa_ref, dob_ref, gw_ref, gwb_ref, dgna_ref, dgsw_ref):
        @pl.when(pl.program_id(0) == 0)
        def _():
            gw_ref[...] = jnp.zeros_like(gw_ref)
            dgna_ref[...] = jnp.zeros_like(dgna_ref)
            dgsw_ref[...] = jnp.zeros_like(dgsw_ref)

        dm = dm_ref[...]
        gw_ref[...] += _mm_tn(oab_ref[...], dm)

        @pl.when(pl.program_id(0) == t // tm - 1)
        def _():
            gwb_ref[...] = gw_ref[...].astype(BF16)

        do = _mm_nt(dm, w_ref[...])
        for raw_ref, g_ref, dst_ref, dg_ref, lo in ((oa_ref, gna_ref, doa_ref, dgna_ref, 0),
                                                     (ob_ref, gsw_ref, dob_ref, dgsw_ref, hw)):
            r, xn = _rms_stats(raw_ref[...])
            dpart = do[:, lo:lo + hw]
            dg_ref[...] += jnp.sum(dpart * xn, axis=0, keepdims=True)
            dst_ref[...] = _rms_bwd(dpart * g_ref[...], xn, r).astype(BF16)

    tile = lambda w: pl.BlockSpec((tm, w), lambda i: (i, 0))
    vec = lambda w: pl.BlockSpec((1, w), lambda i: (0, 0))
    return pl.pallas_call(
        body, name="out_backward", grid=(t // tm,),
        out_shape=(jax.ShapeDtypeStruct((t, hw), BF16), jax.ShapeDtypeStruct((t, hw), BF16),
                   jax.ShapeDtypeStruct((d, d), F32), jax.ShapeDtypeStruct((d, d), BF16),
                   jax.ShapeDtypeStruct((1, hw), F32), jax.ShapeDtypeStruct((1, hw), F32)),
        in_specs=[tile(d), pl.BlockSpec((d, d), lambda i: (0, 0)), tile(d), tile(hw), tile(hw), vec(hw), vec(hw)],
        out_specs=(tile(hw), tile(hw), pl.BlockSpec((d, d), lambda i: (0, 0)), pl.BlockSpec((d, d), lambda i: (0, 0)),
                   vec(hw), vec(hw)),
        compiler_params=_cparams(("arbitrary",), VMEM_BIG),
    )(dmix, w_out, oab, oa, ob, g_na, g_sw)


def _na_backward(proj, d_o, tiles, batch, seq, rider=None):
    t = proj.shape[0]
    n_rows = seq // GRID_W
    n_pairs = NA_WIDTH // LANES
    win = NA_ROWS * GRID_W
    n_tiles = 2 * NA_ROWS - 2

    def body(q_ref, k_ref, v_ref, do_ref, tp_ref, dq_ref, dk_ref, dv_ref, dtp_ref, km, vm, dk_acc, dv_acc):
        @pl.when(pl.program_id(1) == 0)
        def _():
            dtp_ref[...] = jnp.zeros_like(dtp_ref)

        _na_prepare(k_ref, v_ref, km, vm)
        dk_acc[...] = jnp.zeros_like(dk_acc)
        dv_acc[...] = jnp.zeros_like(dv_acc)
        low = lax.broadcasted_iota(jnp.int32, (win, LANES), 1) < HEAD_DIM

        def row(r, carry):
            rs, off = _na_window(r, n_rows)
            rows = pl.ds(pl.multiple_of(r * GRID_W, GRID_W), GRID_W)
            wrows = pl.ds(pl.multiple_of(rs * GRID_W, GRID_W), win)
            q = q_ref[rows, :]
            do = do_ref[rows, :]
            k2 = _na_pair_window(km, wrows)
            p = _na_probs(q, k2, tp_ref, off)
            dp = _mm_nt(do, _na_pair_window(vm, wrows))
            parts = []
            for h in range(2):
                ph, dph = p[:, h * win:(h + 1) * win], dp[:, h * win:(h + 1) * win]
                dsh = ph * (dph - jnp.sum(ph * dph, axis=-1, keepdims=True))
                for w in range(NA_ROWS // 2):
                    dtp_ref[h, 2 * w - off + (NA_ROWS - 1)] += dsh[:, w * LANES:(w + 1) * LANES]
                parts.append(dsh)
            dsb = (jnp.concatenate(parts, axis=1) * QK_SCALE).astype(BF16)
            dq_ref[rows, :] = _mm(dsb, k2).astype(BF16)
            dk2 = _mm_tn(dsb, q)
            dv2 = _mm_tn(p.astype(BF16), do)
            dk_acc[wrows, :] += jnp.where(low, dk2[:win], dk2[win:])
            dv_acc[wrows, :] += jnp.where(low, dv2[:win], dv2[win:])
            return carry

        lax.fori_loop(0, n_rows, row, 0, unroll=2)
        dk_ref[...] = dk_acc[...].astype(BF16)
        dv_ref[...] = dv_acc[...].astype(BF16)

    blk = lambda off: pl.BlockSpec((seq, LANES), lambda p, b: (b, off + p))
    out = jax.ShapeDtypeStruct((t, NA_WIDTH), BF16)
    return _hosted(
        body, rider, name="na_backward", grid=(n_pairs, batch),
        out_shape=[out, out, out, jax.ShapeDtypeStruct(tiles.shape, F32)],
        in_specs=[blk(0), blk(n_pairs), blk(2 * n_pairs), blk(0),
                  pl.BlockSpec((2, n_tiles, GRID_W, LANES), lambda p, b: (p, 0, 0, 0))],
        out_specs=[blk(0), blk(0), blk(0), pl.BlockSpec((2, n_tiles, GRID_W, LANES), lambda p, b: (p, 0, 0, 0))],
        scratch_shapes=[pltpu.VMEM((2, seq, LANES), BF16), pltpu.VMEM((2, seq, LANES), BF16),
                        pltpu.VMEM((seq, LANES), F32), pltpu.VMEM((seq, LANES), F32)],
        compiler_params=_cparams(("arbitrary", "arbitrary")), args=[proj, proj, proj, d_o, tiles])


def _na_bias_grad(dtiles_flat, expand):
    n = dtiles_flat.shape[0]

    def body(t_ref, e_ref, o_ref):
        o_ref[...] = lax.dot_general(t_ref[...], e_ref[...], (((1,), (1,)), ((), ())),
                                     precision=lax.Precision.HIGHEST, preferred_element_type=F32)

    return pl.pallas_call(
        body, name="na_bias_grad",
        out_shape=jax.ShapeDtypeStruct((n, expand.shape[0]), F32),
        compiler_params=_cparams(vmem=VMEM_BIG),
    )(dtiles_flat, expand)


def _sw_backward(proj, d_o, sink, batch, seq, rider=None):
    t = proj.shape[0]
    n_pairs = SW_WIDTH // LANES
    q_blk = 3 * NA_WIDTH // LANES
    k_blk = q_blk + n_pairs
    n_blocks = seq // SW_BLOCK
    pad = seq + 2 * SW_BLOCK

    def body(sink_ref, q_ref, k_ref, v_ref, do_ref, dq_ref, dk_ref, dv_ref, dsk_ref,
             k_lo, k_hi, v_lo, v_hi, dk_loc, dv_loc, dk_tot, dv_tot):
        hp = pl.program_id(1)
        g = hp // 2
        _sw_prepare(k_ref, g, k_lo, k_hi, seq)
        _sw_prepare(v_ref, g, v_lo, v_hi, seq)
        dk_loc[...] = jnp.zeros_like(dk_loc)
        dv_loc[...] = jnp.zeros_like(dv_loc)

        @pl.when(hp == 0)
        def _():
            dk_tot[...] = jnp.zeros_like(dk_tot)
            dv_tot[...] = jnp.zeros_like(dv_tot)

        band = 3 * SW_BLOCK
        low = lax.broadcasted_iota(jnp.int32, (band, LANES), 1) < HEAD_DIM

        def block(n, carry):
            rows = pl.ds(pl.multiple_of(n * SW_BLOCK, SW_BLOCK), SW_BLOCK)
            wrows = pl.ds(pl.multiple_of(n * SW_BLOCK, SW_BLOCK), band)
            qb = q_ref[rows, :]
            do = do_ref[rows, :]
            k2 = jnp.concatenate([k_lo[wrows, :], k_hi[wrows, :]], axis=0)
            v2 = jnp.concatenate([v_lo[wrows, :], v_hi[wrows, :]], axis=0)
            p, ps = _sw_probs(qb, k2, _sw_mask(n, seq), (sink_ref[2 * hp], sink_ref[2 * hp + 1]))
            dp = _mm_nt(do, v2)
            parts, new = [], []
            for i in range(2):
                ph, dph = p[:, i * band:(i + 1) * band], dp[:, i * band:(i + 1) * band]
                delta = jnp.sum(ph * dph, axis=-1, keepdims=True)
                parts.append(ph * (dph - delta))
                new.append(carry[i] - ps[i] * delta)
            dsb = (jnp.concatenate(parts, axis=1) * QK_SCALE).astype(BF16)
            dq_ref[rows, :] = _mm(dsb, k2)
            dk2 = _mm_tn(dsb, qb)
            dv2 = _mm_tn(p.astype(BF16), do)
            dk_loc[wrows, :] += jnp.where(low, dk2[:band], dk2[band:])
            dv_loc[wrows, :] += jnp.where(low, dv2[:band], dv2[band:])
            return tuple(new)

        zero = jnp.zeros((SW_BLOCK, 1), F32)
        s0, s1 = lax.fori_loop(0, n_blocks, block, (zero, zero), unroll=2)
        row = lax.broadcasted_iota(jnp.int32, (SUBLANES, LANES), 0)
        dsk_ref[0, 0] = jnp.where(row == 0, jnp.sum(s0), jnp.where(row == 1, jnp.sum(s1), 0.0))

        lane_s = lax.broadcasted_iota(jnp.int32, (seq, LANES), 1)
        mine_g = (lane_s // HEAD_DIM) == g
        for loc, tot in ((dk_loc, dk_tot), (dv_loc, dv_tot)):
            part = loc[SW_BLOCK:SW_BLOCK + seq, :]
            tot[...] += jnp.where(mine_g, part + pltpu.roll(part, HEAD_DIM, 1), 0.0)

        @pl.when(hp == n_pairs - 1)
        def _():
            dk_ref[...] = dk_tot[...]
            dv_ref[...] = dv_tot[...].astype(BF16)

    return _hosted(
        body, rider, name="sw_backward", grid=(batch, n_pairs),
        out_shape=[jax.ShapeDtypeStruct((t, SW_WIDTH), F32), jax.ShapeDtypeStruct((t, LANES), F32),
                   jax.ShapeDtypeStruct((t, LANES), BF16), jax.ShapeDtypeStruct((batch, n_pairs, SUBLANES, LANES), F32)],
        in_specs=[pl.BlockSpec(memory_space=pltpu.SMEM),
                  pl.BlockSpec((seq, LANES), lambda b, p: (b, q_blk + p)),
                  pl.BlockSpec((seq, LANES), lambda b, p: (b, k_blk)),
                  pl.BlockSpec((seq, LANES), lambda b, p: (b, k_blk + 1)),
                  pl.BlockSpec((seq, LANES), lambda b, p: (b, p))],
        out_specs=[pl.BlockSpec((seq, LANES), lambda b, p: (b, p)), pl.BlockSpec((seq, LANES), lambda b, p: (b, 0)),
                   pl.BlockSpec((seq, LANES), lambda b, p: (b, 0)),
                   pl.BlockSpec((1, 1, SUBLANES, LANES), lambda b, p: (b, p, 0, 0))],
        scratch_shapes=[pltpu.VMEM((pad, LANES), BF16)] * 4 + [pltpu.VMEM((pad, LANES), F32)] * 2
        + [pltpu.VMEM((seq, LANES), F32)] * 2,
        compiler_params=_cparams(("arbitrary", "arbitrary")), args=[sink, proj, proj, proj, d_o])


def _in_backward(dqkv_a, dq_b, dk_b, dv_b, w_in_t, h1, x, mod3, g_attn, dx1, cos_t, sin_t, seq):
    t, d = x.shape
    tm = TOKEN_TILE // 2
    per_seq = seq // tm
    batch = t // seq
    dqa, dka, dva = dqkv_a
    n_q = SW_WIDTH // LANES

    def body(dqa_ref, dka_ref, dva_ref, dqb_ref, dkb_ref, dvb_ref, w_ref, h_ref, x_ref, mod_ref, g_ref, dx1_ref,
             cos_ref, sin_ref, dx_ref, gw_ref, gwb_ref, dsh_ref, dsc_ref, dg_ref):
        i = pl.program_id(0)

        @pl.when(i == 0)
        def _():
            gw_ref[...] = jnp.zeros_like(gw_ref)
            dg_ref[...] = jnp.zeros_like(dg_ref)

        @pl.when(i % per_seq == 0)
        def _():
            dsh_ref[...] = jnp.zeros_like(dsh_ref)
            dsc_ref[...] = jnp.zeros_like(dsc_ref)

        dr = jnp.concatenate([dqb_ref[...], dkb_ref[...]], axis=1)
        cos = jnp.concatenate([cos_ref[...]] * (n_q + 1), axis=1)
        sin = jnp.concatenate([sin_ref[...]] * (n_q + 1), axis=1)
        dr = dr * cos + _rope_rot(dr * sin)
        dproj = jnp.concatenate([dqa_ref[...], dka_ref[...], dva_ref[...], dr.astype(BF16), dvb_ref[...]], axis=1)
        gw_ref[...] += _mm_tn(dproj, h_ref[...])

        @pl.when(i == t // tm - 1)
        def _():
            gwb_ref[...] = gw_ref[...].astype(BF16)

        dh = _mm(dproj, w_ref[...])
        scale = mod_ref[0, :, d:2 * d]
        r, xn = _rms_stats(x_ref[...])
        xg = xn * g_ref[...]
        dxg = dh * (1.0 + scale)
        dx_ref[...] = dx1_ref[...] + _rms_bwd(dxg * g_ref[...], xn, r)
        dg_ref[...] += jnp.sum(dxg * xn, axis=0, keepdims=True)
        dsh_ref[0] += jnp.sum(dh, axis=0, keepdims=True)
        dsc_ref[0] += jnp.sum(dh * xg, axis=0, keepdims=True)

    tile = lambda w: pl.BlockSpec((tm, w), lambda i: (i, 0))
    per_b = pl.BlockSpec((1, 1, d), lambda i: (i // per_seq, 0, 0))
    small = jax.ShapeDtypeStruct((batch, 1, d), F32)
    rope = pl.BlockSpec((tm, LANES), lambda i: (i % per_seq, 0))
    return pl.pallas_call(
        body, name="in_backward", grid=(t // tm,),
        out_shape=(jax.ShapeDtypeStruct((t, d), F32), jax.ShapeDtypeStruct((IN_WIDTH, d), F32),
                   jax.ShapeDtypeStruct((IN_WIDTH, d), BF16), small, small, jax.ShapeDtypeStruct((1, d), F32)),
        in_specs=[tile(NA_WIDTH), tile(NA_WIDTH), tile(NA_WIDTH), tile(SW_WIDTH), tile(LANES), tile(LANES),
                  pl.BlockSpec((IN_WIDTH, d), lambda i: (0, 0)), tile(d), tile(d),
                  pl.BlockSpec((1, 1, 6 * d), lambda i: (i // per_seq, 0, 0)),
                  pl.BlockSpec((1, d), lambda i: (0, 0)), tile(d), rope, rope],
        out_specs=(tile(d), pl.BlockSpec((IN_WIDTH, d), lambda i: (0, 0)), pl.BlockSpec((IN_WIDTH, d), lambda i: (0, 0)),
                   per_b, per_b, pl.BlockSpec((1, d), lambda i: (0, 0))),
        compiler_params=_cparams(("arbitrary",), VMEM_BIG),
    )(dqa, dka, dva, dq_b, dk_b, dv_b, w_in_t, h1, x, mod3, g_attn, dx1, cos_t, sin_t)


def _ada_weight_grad(sc_all, dmod_cols):
    d = sc_all.shape[1]
    ncol = dmod_cols.shape[1]

    def body(s_ref, m_ref, o_ref):
        o_ref[...] = _mm_tn(s_ref[...].astype(BF16), m_ref[...].astype(BF16))

    return pl.pallas_call(
        body, name="ada_weight_grad",
        out_shape=jax.ShapeDtypeStruct((d, ncol), F32),
        compiler_params=_cparams(vmem=VMEM_BIG),
    )(sc_all, dmod_cols)


def _row_tile(rows, cols):
    target = max(SUBLANES, (1 << 20) // (4 * cols))
    best = rows
    for cand in range(SUBLANES, rows + 1, SUBLANES):
        if rows % cand == 0 and cand <= target:
            best = cand
    return best if rows % SUBLANES == 0 else rows


def _sum_slots(recv, own, name):
    _, rows, cols = recv.shape
    tr = _row_tile(rows, cols)

    def body(p_ref, own_ref, o_ref):
        o_ref[...] = ((own_ref[...] + p_ref[0].astype(F32)) + p_ref[1].astype(F32)) + p_ref[2].astype(F32)

    return pl.pallas_call(
        body, name=name, grid=(rows // tr,),
        out_shape=jax.ShapeDtypeStruct((rows, cols), F32),
        in_specs=[pl.BlockSpec((N_SHARD - 1, tr, cols), lambda i: (0, i, 0)), pl.BlockSpec((tr, cols), lambda i: (i, 0))],
        out_specs=pl.BlockSpec((tr, cols), lambda i: (i, 0)),
        compiler_params=_cparams(("arbitrary",)),
    )(recv, own)


def _adamw(w, grads, m, v, name):
    rows, cols = w.shape
    tr = _row_tile(rows, cols)
    ng = len(grads)

    def body(*refs):
        w_ref = refs[0]
        g_refs = refs[1:1 + ng]
        m_ref, v_ref = refs[1 + ng], refs[2 + ng]
        g_out, d_out, m_out, v_out = refs[3 + ng:]
        g = g_refs[0][...]
        for extra in g_refs[1:]:
            g = g + extra[...]
        g_out[...] = g
        m2 = ADAM_B1 * m_ref[...] + (1.0 - ADAM_B1) * g
        v2 = ADAM_B2 * v_ref[...] + (1.0 - ADAM_B2) * (g * g)
        m_out[...] = m2
        v_out[...] = v2
        m_hat = m2 / (1.0 - ADAM_B1 ** ADAM_STEP)
        v_hat = v2 / (1.0 - ADAM_B2 ** ADAM_STEP)
        d_out[...] = -ADAM_LR * (m_hat / (jnp.sqrt(v_hat) + ADAM_EPS) + ADAM_WD * w_ref[...])

    spec = pl.BlockSpec((tr, cols), lambda i: (i, 0))
    out = jax.ShapeDtypeStruct((rows, cols), F32)
    return pl.pallas_call(
        body, name=name, grid=(rows // tr,),
        out_shape=(out, out, out, out),
        in_specs=[spec] * (3 + ng), out_specs=(spec, spec, spec, spec),
        compiler_params=_cparams(("arbitrary",)),
    )(w, *grads, m, v)


def _pack_rows(arrays):
    tile = SUBLANES * LANES
    rows, offsets, at = [], [], 0
    for a in arrays:
        flat = a.reshape(-1).astype(F32)
        n = -(-flat.shape[0] // tile) * tile
        rows.append(jnp.pad(flat, (0, n - flat.shape[0])).reshape(-1, LANES))
        offsets.append(at)
        at += n // LANES
    return jnp.concatenate(rows, axis=0), offsets


def _unpack_rows(packed, offsets, shapes):
    out = []
    for off, shape in zip(offsets, shapes):
        n = 1
        for s in shape:
            n *= s
        nrow = -(-n // LANES)
        out.append(packed[off:off + nrow].reshape(-1)[:n].reshape(shape))
    return out


def _rope_tables(seq):
    half = HEAD_DIM // 2
    inv = ROPE_THETA ** (-jnp.arange(half, dtype=F32) / half)
    ang = jnp.arange(seq).astype(F32)[:, None] * inv[None, :]
    cos, sin = jnp.cos(ang), jnp.sin(ang)
    cos_t = jnp.concatenate([cos, cos, cos, cos], axis=1)
    sin_t = jnp.concatenate([-sin, sin, -sin, sin], axis=1)
    return cos_t, sin_t


def kernel(x, c, w_ada, b_ada, g_attn, w_in, na_rpb, sw_sink, g_na_out, g_sw_out, w_out, g_ffn, w_up, conv_w, conv_b, w_down, g_final, loss_target, m_w_ada, m_b_ada, m_g_attn, m_w_in, m_na_rpb, m_sw_sink, m_g_na_out, m_g_sw_out, m_w_out, m_g_ffn, m_w_up, m_conv_w, m_conv_b, m_w_down, m_g_final, v_w_ada, v_b_ada, v_g_attn, v_w_in, v_na_rpb, v_sw_sink, v_g_na_out, v_g_sw_out, v_w_out, v_g_ffn, v_w_up, v_conv_w, v_conv_b, v_w_down, v_g_final):
    batch, seq, d = x.shape
    t = batch * seq
    assert d == D_MODEL and seq % (NA_ROWS * GRID_W) == 0 and seq % TOKEN_TILE == 0 and batch <= SUBLANES
    shard = 2 * lax.axis_index("x") + lax.axis_index("y")
    xt = x.reshape(t, d)
    tgt = loss_target.reshape(t, d)

    c8 = jnp.pad(c, ((0, SUBLANES - batch), (0, 0)))
    w_in_t_s = jnp.transpose(w_in[0]).astype(BF16)
    (mod8, sc_all), (w_in_g,) = _ada_forward(c8, w_ada[0], b_ada, _Rider("gather", [w_in_t_s]))
    mod3 = mod8[:batch].reshape(batch, 1, 6 * d)
    w_in_t = w_in_g.reshape(IN_WIDTH, d)

    cos_t, sin_t = _rope_tables(seq)
    (h1, proj), (w_out_g,) = _in_proj(xt, mod3, g_attn, w_in_t, cos_t, sin_t, seq,
                                      _Rider("gather", [w_out[0].astype(BF16)]))
    n_heads = NA_WIDTH // HEAD_DIM
    n_tiles, n_dc = 2 * NA_ROWS - 2, 2 * NA_COLS - 1
    expand, neg_mask = _na_bias_pattern()
    rpb = na_rpb[0]
    rows2 = jnp.concatenate([rpb[:, :-1, :], rpb[:, 1:, :]], axis=2).reshape(n_heads * n_tiles, 2 * n_dc)
    rows2 = jnp.pad(rows2, ((0, 0), (0, GRID_W - 2 * n_dc)))
    tiles = _na_bias_tiles(rows2, expand, neg_mask).reshape(n_heads, n_tiles, GRID_W, LANES)
    sink = sw_sink[0]
    (oa,), (w_up_f,) = _na_forward(proj, tiles, batch, seq, _Rider("gather", [w_up[0].astype(BF16)]))
    (ob,), (conv_w_g,) = _sw_forward(proj, sink, batch, seq, _Rider("gather", [conv_w[0]]))
    w_out_f = w_out_g.reshape(d, d)
    conv_w_f = jnp.transpose(conv_w_g, (1, 0, 2)).reshape(3, D_FF)
    oab, mix, x1, h2 = _out_proj(oa, ob, g_na_out, g_sw_out, w_out_f, xt, mod3, g_ffn, seq)
    (u,), (w_down_g,) = _up_proj(h2, w_up_f, _Rider("gather", [w_down[0].astype(BF16)]))
    w_down_f = w_down_g.reshape(D_FF, d)
    a = _conv_gate(u, conv_w_f, conv_b, batch, seq)
    dx2, dffn, loss_part, dgate_f, dg_final = _down_and_loss(a, w_down_f, x1, mod3, g_final.reshape(1, d), tgt, seq)

    gw_down, gw_down_b = _down_weight_grad(a, dffn)
    blocks = lambda g, rows: g.reshape(N_SHARD, rows // N_SHARD, d)
    (du, gconv_w, gconv_b), (recv_down, own_down) = _ffn_backward(
        dffn, w_down_f, u, conv_w_f, conv_b, batch, seq,
        _Rider("scatter", [blocks(gw_down_b, D_FF)], [blocks(gw_down, D_FF)]))
    (gw_up, gw_up_b), _ = _up_weight_grad(h2, du)
    dx1, dmix, dshift_f, dscale_f, dgate_a, dg_ffn = _up_backward(du, w_up_f, x1, mod3, g_ffn, dx2, mix, seq)
    doa, dob, gw_out, gw_out_b, dg_na, dg_sw = _out_backward(dmix, w_out_f, oab, oa, ob, g_na_out, g_sw_out)
    (dqa, dka, dva, dtiles), (recv_up, own_up) = _na_backward(
        proj, doa, tiles, batch, seq, _Rider("scatter", [gw_up_b], [gw_up]))
    (dq_b, dk_b, dv_b, dsink_parts), (recv_out, own_out) = _sw_backward(
        proj, dob, sink, batch, seq, _Rider("scatter", [blocks(gw_out_b, d)], [blocks(gw_out, d)]))
    gx, gw_in_t, gw_in_b, dshift_a, dscale_a, dg_attn = _in_backward(
        (dqa, dka, dva), dq_b, dk_b, dv_b, w_in_t, h1, xt, mod3, g_attn, dx1, cos_t, sin_t, seq)

    red = _na_bias_grad(dtiles.reshape(n_heads * n_tiles, GRID_W * LANES), expand)[:, :2 * n_dc]
    red = red.reshape(n_heads, n_tiles, 2, n_dc)
    zero_row = jnp.zeros((n_heads, 1, n_dc), F32)
    g_rpb = (jnp.concatenate([red[:, :, 0, :], zero_row], axis=1)
             + jnp.concatenate([zero_row, red[:, :, 1, :]], axis=1))
    g_sink = jnp.sum(dsink_parts[:, :, :2, 0], axis=0).reshape(SW_WIDTH // HEAD_DIM)

    dmod = jnp.concatenate([dshift_a, dscale_a, dgate_a, dshift_f, dscale_f, dgate_f], axis=2).reshape(batch, 6 * d)
    small_parts = [jnp.sum(dmod, axis=0), dg_attn, g_rpb, g_sink, dg_na, dg_sw, dg_ffn, gconv_w, gconv_b, dg_final,
                   loss_part[0, 0:1]]
    packed, offsets = _pack_rows(small_parts + [dmod])
    (summed, every), (recv_in, own_in) = _allreduce_small(
        packed, _Rider("scatter", [blocks(gw_in_b, IN_WIDTH)], [blocks(gw_in_t, IN_WIDTH)]))
    mine = [_sum_slots(r, o, name) for r, o, name in ((recv_in, own_in, "sum_w_in"), (recv_out, own_out, "sum_w_out"),
                                                      (recv_up, own_up, "sum_w_up"), (recv_down, own_down, "sum_w_down"))]
    theirs = _ride_alone(_Rider("swap", mine), "swap_sibling")
    small_shapes = [(1, 6 * d), (1, d), na_rpb.shape, sw_sink.shape, (1, NA_WIDTH), (1, SW_WIDTH), (1, d),
                    (3, D_FF), (1, D_FF), (d,), ()]
    (g_b_ada, g_g_attn, g_na_rpb, g_sw_sink, g_g_na, g_g_sw, g_g_ffn, g_conv_w_full, g_conv_b, g_g_final,
     loss) = _unpack_rows(summed, offsets[:-1], small_shapes)
    dmod_rows = every[:, offsets[-1]:offsets[-1] + batch * 6 * d // LANES, :].reshape(N_DEV, batch, 6 * d)
    dmod_rows = jnp.pad(dmod_rows, ((0, 0), (0, SUBLANES - batch), (0, 0))).reshape(N_DEV * SUBLANES, 6 * d)
    ncol = w_ada.shape[2]
    g_w_ada = _ada_weight_grad(sc_all, lax.dynamic_slice(dmod_rows, (0, shard * ncol), (N_DEV * SUBLANES, ncol)))
    cshard = conv_w.shape[2]
    g_conv_w = lax.dynamic_slice(g_conv_w_full, (0, shard * cshard), (3, cshard)).reshape(conv_w.shape)

    def big(w, m, v, g_parts, name):
        shape = w.shape
        outs = _adamw(w[0], g_parts, m[0], v[0], name)
        return [o.reshape(shape) for o in outs]

    r_w_ada = big(w_ada, m_w_ada, v_w_ada, [g_w_ada], "adamw_w_ada")
    r_w_in = big(w_in, m_w_in, v_w_in, [jnp.transpose(mine[0]), jnp.transpose(theirs[0])], "adamw_w_in")
    r_w_out = big(w_out, m_w_out, v_w_out, [mine[1], theirs[1]], "adamw_w_out")
    r_w_up = big(w_up, m_w_up, v_w_up, [mine[2], theirs[2]], "adamw_w_up")
    r_w_down = big(w_down, m_w_down, v_w_down, [mine[3], theirs[3]], "adamw_w_down")

    small_w = [b_ada, g_attn, na_rpb, sw_sink, g_na_out, g_sw_out, g_ffn, conv_w, conv_b, g_final]
    small_m = [m_b_ada, m_g_attn, m_na_rpb, m_sw_sink, m_g_na_out, m_g_sw_out, m_g_ffn, m_conv_w, m_conv_b, m_g_final]
    small_v = [v_b_ada, v_g_attn, v_na_rpb, v_sw_sink, v_g_na_out, v_g_sw_out, v_g_ffn, v_conv_w, v_conv_b, v_g_final]
    small_g = [g_b_ada, g_g_attn, g_na_rpb, g_sw_sink, g_g_na, g_g_sw, g_g_ffn, g_conv_w, g_conv_b, g_g_final]
    pw, offs = _pack_rows(small_w)
    pg, _ = _pack_rows(small_g)
    pm, _ = _pack_rows(small_m)
    pv, _ = _pack_rows(small_v)
    shapes = [w.shape for w in small_w]
    r_small = [_unpack_rows(o, offs, shapes) for o in _adamw(pw, [pg], pm, pv, "adamw_small")]

    def pick(k):
        b_, ga_, rpb_, sk_, gna_, gsw_, gf_, cw_, cb_, gfin_ = r_small[k]
        return [r_w_ada[k], b_, ga_, r_w_in[k], rpb_, sk_, gna_, gsw_, r_w_out[k], gf_, r_w_up[k], cw_, cb_,
                r_w_down[k], gfin_]

    return (loss, gx.reshape(batch, seq, d), *pick(0), *pick(1), *pick(2), *pick(3))
```

```python
import functools

import jax
import jax.numpy as jnp
from jax import lax
from jax.experimental import pallas as pl
from jax.experimental.pallas import tpu as pltpu

F32 = jnp.float32
BF16 = jnp.bfloat16
MESH = pl.DeviceIdType.MESH

D_MODEL = 1024
HEAD_DIM = 64
NA_WIDTH = 512
SW_WIDTH = 512
SW_KV_WIDTH = 128
IN_WIDTH = 2304
D_FF = 2816
GRID_W = 64
NA_ROWS = 8
NA_COLS = 16
SW_BLOCK = 128
ROPE_THETA = 10000.0
EPS = 1e-6
NEG = -1e30
QK_SCALE = HEAD_DIM ** -0.5

ADAM_LR = 0.001
ADAM_B1 = 0.9
ADAM_B2 = 0.999
ADAM_EPS = 1e-08
ADAM_WD = 0.01
ADAM_STEP = 10

N_SHARD = 4
N_DEV = 8
LANES = 128
SUBLANES = 8
TOKEN_TILE = 512
FF_TILE = 256
CONV_CHUNK = 64
NA_GROUP = 4
SW_GROUP_BLOCKS = 4
VMEM_BIG = 56 * 1024 * 1024


def _mm(a, b):
    return jnp.dot(a, b, preferred_element_type=F32)


def _mm_nt(a, b):
    return lax.dot_general(a, b, (((1,), (1,)), ((), ())), preferred_element_type=F32)


def _mm_tn(a, b):
    return lax.dot_general(a, b, (((0,), (0,)), ((), ())), preferred_element_type=F32)


def _cparams(sem=None, vmem=None):
    kw = {}
    if sem is not None:
        kw["dimension_semantics"] = sem
    if vmem is not None:
        kw["vmem_limit_bytes"] = vmem
    return pltpu.CompilerParams(**kw)


def _sigmoid(x):
    return 1.0 / (1.0 + jnp.exp(-x))


def _rms_stats(x):
    r = lax.rsqrt(jnp.mean(x * x, axis=-1, keepdims=True) + EPS)
    return r, x * r


def _rms_bwd(dxn, xn, r):
    return r * (dxn - xn * jnp.mean(dxn * xn, axis=-1, keepdims=True))


def _my_pos():
    return lax.axis_index("x"), lax.axis_index("y"), lax.axis_index("c")


def _flip(v, bit):
    return 1 - v if bit else v


def _ada_forward(c8, w_ada, b_ada, rider):
    d = c8.shape[1]
    ncol = w_ada.shape[1]

    def body(c_ref, w_ref, b_ref, mod_ref, sc_ref, m_scr, mod_buf, ssem, rsem, ssem2, rsem2):
        x, y, c = _my_pos()
        me = 4 * x + 2 * y + c
        shard = 2 * x + y
        cv = c_ref[...]
        my_rows = pl.ds(pl.multiple_of(me * SUBLANES, SUBLANES), SUBLANES)
        sc_ref[my_rows, :] = cv * _sigmoid(cv)

        def copy1(k):
            peer = (_flip(x, (k >> 2) & 1), _flip(y, (k >> 1) & 1), _flip(c, k & 1))
            return pltpu.make_async_remote_copy(
                src_ref=sc_ref.at[my_rows, :], dst_ref=sc_ref.at[my_rows, :],
                send_sem=ssem.at[k - 1], recv_sem=rsem.at[k - 1], device_id=peer, device_id_type=MESH)

        sends = [copy1(k) for k in range(1, N_DEV)]
        for cp in sends:
            cp.start()
        for cp in sends:
            cp.wait_recv()
        m_scr[...] = _mm(sc_ref[...].astype(BF16), w_ref[...].astype(BF16))

        def copy2(k):
            px, py = _flip(x, (k >> 1) & 1), _flip(y, k & 1)
            rows = pl.ds(pl.multiple_of((4 * px + 2 * py + c) * SUBLANES, SUBLANES), SUBLANES)
            return pltpu.make_async_remote_copy(
                src_ref=m_scr.at[rows, :], dst_ref=mod_buf.at[shard],
                send_sem=ssem2.at[k - 1], recv_sem=rsem2.at[k - 1], device_id=(px, py, c), device_id_type=MESH)

        sends2 = [copy2(k) for k in range(1, N_SHARD)]
        for cp in sends2:
            cp.start()
        mod_buf[shard] = m_scr[my_rows, :]
        for cp in sends2:
            cp.wait_recv()
        for s in range(N_SHARD):
            mod_ref[:, s * ncol:(s + 1) * ncol] = mod_buf[s] + b_ref[:, s * ncol:(s + 1) * ncol]
        for cp in sends + sends2:
            cp.wait_send()

    vm = pl.BlockSpec(memory_space=pltpu.VMEM)
    return _hosted(
        body, rider, name="ada_forward", grid=(),
        out_shape=(jax.ShapeDtypeStruct((SUBLANES, N_SHARD * ncol), F32),
                   jax.ShapeDtypeStruct((N_DEV * SUBLANES, d), F32)),
        in_specs=[vm, vm, vm], out_specs=(vm, vm),
        scratch_shapes=[pltpu.VMEM((N_DEV * SUBLANES, ncol), F32), pltpu.VMEM((N_SHARD, SUBLANES, ncol), F32),
                        pltpu.SemaphoreType.DMA((N_DEV - 1,)), pltpu.SemaphoreType.DMA((N_DEV - 1,)),
                        pltpu.SemaphoreType.DMA((N_SHARD - 1,)), pltpu.SemaphoreType.DMA((N_SHARD - 1,))],
        compiler_params=_cparams(vmem=VMEM_BIG), args=[c8, w_ada, b_ada])


class _Rider:
    def __init__(self, kind, srcs, owns=()):
        self.kind, self.srcs, self.owns = kind, list(srcs), list(owns)
        n = len(self.srcs)
        sds = jax.ShapeDtypeStruct
        dma = pltpu.SemaphoreType.DMA
        if kind == "gather":
            self.out_shapes = [sds((N_SHARD,) + s.shape, s.dtype) for s in self.srcs]
            self.sems = [dma((n, N_SHARD - 1)), dma((n, N_SHARD - 1)), dma((n,))]
        elif kind == "scatter":
            self.out_shapes = ([sds((N_SHARD - 1,) + s.shape[1:], s.dtype) for s in self.srcs]
                               + [sds(o.shape[1:], o.dtype) for o in self.owns])
            self.sems = [dma((n, N_SHARD - 1)), dma((n, N_SHARD - 1)), dma((max(len(self.owns), 1),))]
        else:
            self.out_shapes = [sds(s.shape, s.dtype) for s in self.srcs]
            self.sems = [dma((n,)), dma((n,))]

    @property
    def inputs(self):
        return self.srcs + self.owns

    def copies(self, ins, outs, sems):
        n = len(self.srcs)
        x, y, c = _my_pos()
        shard = 2 * x + y
        local, remote = [], []
        if self.kind == "swap":
            ssem, rsem = sems
            for i in range(n):
                remote.append(pltpu.make_async_remote_copy(
                    src_ref=ins[i], dst_ref=outs[i], send_sem=ssem.at[i], recv_sem=rsem.at[i],
                    device_id=(x, y, 1 - c), device_id_type=MESH))
            return local, remote
        ssem, rsem, lsem = sems
        for i in range(n):
            if self.kind == "gather":
                local.append(pltpu.make_async_copy(ins[i], outs[i].at[shard], lsem.at[i]))
            for k in range(1, N_SHARD):
                px, py = _flip(x, (k >> 1) & 1), _flip(y, k & 1)
                if self.kind == "gather":
                    src, dst = ins[i], outs[i].at[shard]
                else:
                    src, dst = ins[i].at[2 * px + py], outs[i].at[k - 1]
                remote.append(pltpu.make_async_remote_copy(
                    src_ref=src, dst_ref=dst, send_sem=ssem.at[i, k - 1], recv_sem=rsem.at[i, k - 1],
                    device_id=(px, py, c), device_id_type=MESH))
        if self.kind == "scatter":
            for i in range(len(self.owns)):
                local.append(pltpu.make_async_copy(ins[n + i].at[shard], outs[n + i], lsem.at[i]))
        return local, remote

    def start(self, ins, outs, sems):
        local, remote = self.copies(ins, outs, sems)
        for cp in local + remote:
            cp.start()

    def wait(self, ins, outs, sems):
        local, remote = self.copies(ins, outs, sems)
        for cp in remote:
            cp.wait_recv()
        for cp in remote:
            cp.wait_send()
        for cp in local:
            cp.wait()


def _hosted(body, rider, *, name, grid, out_shape, in_specs, out_specs, scratch_shapes, compiler_params, args):
    out_shape, out_specs = list(out_shape), list(out_specs)
    if rider is None:
        outs = pl.pallas_call(body, name=name, grid=grid, out_shape=tuple(out_shape), in_specs=list(in_specs),
                              out_specs=tuple(out_specs), scratch_shapes=list(scratch_shapes),
                              compiler_params=compiler_params)(*args)
        return list(outs), []
    n_in, n_out, n_scr = len(in_specs), len(out_shape), len(scratch_shapes)
    nr_in, nr_out = len(rider.inputs), len(rider.out_shapes)

    def full(*refs):
        ins, refs = refs[:n_in], refs[n_in:]
        r_in, refs = refs[:nr_in], refs[nr_in:]
        outs, refs = refs[:n_out], refs[n_out:]
        r_out, refs = refs[:nr_out], refs[nr_out:]
        scr, sems = refs[:n_scr], refs[n_scr:]
        if grid:
            first = last = None
            for ax, size in enumerate(grid):
                f, l = pl.program_id(ax) == 0, pl.program_id(ax) == size - 1
                first = f if first is None else jnp.logical_and(first, f)
                last = l if last is None else jnp.logical_and(last, l)
            pl.when(first)(lambda: rider.start(r_in, r_out, sems))
            body(*ins, *outs, *scr)
            pl.when(last)(lambda: rider.wait(r_in, r_out, sems))
        else:
            rider.start(r_in, r_out, sems)
            body(*ins, *outs, *scr)
            rider.wait(r_in, r_out, sems)

    hbm = pl.BlockSpec(memory_space=pl.ANY)
    res = pl.pallas_call(
        full, name=name, grid=grid, out_shape=tuple(out_shape + rider.out_shapes),
        in_specs=list(in_specs) + [hbm] * nr_in, out_specs=tuple(out_specs + [hbm] * nr_out),
        scratch_shapes=list(scratch_shapes) + rider.sems, compiler_params=compiler_params,
    )(*args, *rider.inputs)
    return list(res[:n_out]), list(res[n_out:])


def _ride_alone(rider, name):
    return _hosted(lambda: None, rider, name=name, grid=(), out_shape=[], in_specs=[], out_specs=[], scratch_shapes=[],
                   compiler_params=_cparams(), args=[])[1]


def _allreduce_small(packed, rider=None):
    r = packed.shape[0]

    def body(p_ref, sum_ref, all_ref, ssem, rsem):
        x, y, c = _my_pos()
        me = 4 * x + 2 * y + c
        all_ref[me] = p_ref[...]
        cps = []
        for k in range(1, N_DEV):
            peer = (_flip(x, (k >> 2) & 1), _flip(y, (k >> 1) & 1), _flip(c, k & 1))
            cps.append(pltpu.make_async_remote_copy(
                src_ref=all_ref.at[me], dst_ref=all_ref.at[me], send_sem=ssem.at[k - 1], recv_sem=rsem.at[k - 1],
                device_id=peer, device_id_type=MESH))
        for cp in cps:
            cp.start()
        for cp in cps:
            cp.wait_recv()
        acc = all_ref[0]
        for dev in range(1, N_DEV):
            acc = acc + all_ref[dev]
        sum_ref[...] = acc
        for cp in cps:
            cp.wait_send()

    vm = pl.BlockSpec(memory_space=pltpu.VMEM)
    return _hosted(
        body, rider, name="allreduce_small", grid=(),
        out_shape=[jax.ShapeDtypeStruct((r, LANES), F32), jax.ShapeDtypeStruct((N_DEV, r, LANES), F32)],
        in_specs=[vm], out_specs=[vm, vm],
        scratch_shapes=[pltpu.SemaphoreType.DMA((N_DEV - 1,)), pltpu.SemaphoreType.DMA((N_DEV - 1,))],
        compiler_params=_cparams(), args=[packed])


def _rope_rot(t):
    w = t.shape[1]
    lane = lax.broadcasted_iota(jnp.int32, t.shape, 1)
    first = (lane % HEAD_DIM) < (HEAD_DIM // 2)
    return jnp.where(first, pltpu.roll(t, w - HEAD_DIM // 2, 1), pltpu.roll(t, HEAD_DIM // 2, 1))


def _in_proj(x, mod3, g_attn, w_in_t, cos_t, sin_t, seq, rider=None):
    t, d = x.shape
    tm = TOKEN_TILE
    per_seq = seq // tm
    rope_lo, rope_hi = 3 * NA_WIDTH, 3 * NA_WIDTH + SW_WIDTH + SW_KV_WIDTH
    n_rep = (rope_hi - rope_lo) // LANES

    def body(x_ref, mod_ref, g_ref, w_ref, cos_ref, sin_ref, h_ref, p_ref):
        r, xn = _rms_stats(x_ref[...])
        shift, scale = mod_ref[0, :, 0:d], mod_ref[0, :, d:2 * d]
        hb = ((xn * g_ref[...]) * (1.0 + scale) + shift).astype(BF16)
        h_ref[...] = hb
        p_ref[:, :rope_lo] = _mm_nt(hb, w_ref[:rope_lo, :]).astype(BF16)
        pr = _mm_nt(hb, w_ref[rope_lo:rope_hi, :])
        cos = jnp.concatenate([cos_ref[...]] * n_rep, axis=1)
        sin = jnp.concatenate([sin_ref[...]] * n_rep, axis=1)
        p_ref[:, rope_lo:rope_hi] = (pr * cos + _rope_rot(pr) * sin).astype(BF16)
        p_ref[:, rope_hi:] = _mm_nt(hb, w_ref[rope_hi:, :]).astype(BF16)

    return _hosted(
        body, rider, name="in_proj", grid=(t // tm,),
        out_shape=[jax.ShapeDtypeStruct((t, d), BF16), jax.ShapeDtypeStruct((t, IN_WIDTH), BF16)],
        in_specs=[pl.BlockSpec((tm, d), lambda i: (i, 0)),
                  pl.BlockSpec((1, 1, 6 * d), lambda i: (i // per_seq, 0, 0)),
                  pl.BlockSpec((1, d), lambda i: (0, 0)),
                  pl.BlockSpec((IN_WIDTH, d), lambda i: (0, 0)),
                  pl.BlockSpec((tm, LANES), lambda i: (i % per_seq, 0)),
                  pl.BlockSpec((tm, LANES), lambda i: (i % per_seq, 0))],
        out_specs=[pl.BlockSpec((tm, d), lambda i: (i, 0)), pl.BlockSpec((tm, IN_WIDTH), lambda i: (i, 0))],
        scratch_shapes=[], compiler_params=_cparams(("arbitrary",), VMEM_BIG),
        args=[x, mod3, g_attn, w_in_t, cos_t, sin_t])


def _na_bias_pattern():
    n_dc = 2 * NA_COLS - 1
    j = lax.broadcasted_iota(jnp.int32, (GRID_W, GRID_W * LANES), 0)
    m = lax.broadcasted_iota(jnp.int32, (GRID_W, GRID_W * LANES), 1)
    q, lane = m // LANES, m % LANES
    k = lane % GRID_W
    cs = jnp.clip(q - NA_COLS // 2, 0, GRID_W - NA_COLS)
    ok = (k >= cs) & (k < cs + NA_COLS)
    hit = ok & (j < 2 * n_dc) & (lane // GRID_W == j // n_dc) & (k - q + (NA_COLS - 1) == j % n_dc)
    return hit.astype(F32), jnp.where(ok[0:1], 0.0, NEG).astype(F32)


def _na_bias_tiles(rows2, expand, mask):
    n, width = rows2.shape[0], expand.shape[1]
    step = 2048

    def body(r_ref, e_ref, m_ref, o_ref):
        o_ref[...] = jnp.dot(r_ref[...], e_ref[...], precision=lax.Precision.HIGHEST,
                             preferred_element_type=F32) + m_ref[...]

    return pl.pallas_call(
        body, name="na_bias_tiles", grid=(width // step,),
        out_shape=jax.ShapeDtypeStruct((n, width), F32),
        in_specs=[pl.BlockSpec(rows2.shape, lambda i: (0, 0)), pl.BlockSpec((expand.shape[0], step), lambda i: (0, i)),
                  pl.BlockSpec((1, step), lambda i: (0, i))],
        out_specs=pl.BlockSpec((n, step), lambda i: (0, i)),
        compiler_params=_cparams(("arbitrary",)),
    )(rows2, expand, mask)


def _na_prepare(k_ref, v_ref, km, vm):
    lane = lax.broadcasted_iota(jnp.int32, k_ref.shape, 1)
    low = lane < HEAD_DIM
    kv = k_ref[...]
    vv = v_ref[...]
    zero = jnp.zeros_like(kv)
    km[0] = jnp.where(low, kv, zero)
    km[1] = jnp.where(low, zero, kv)
    vm[0] = jnp.where(low, vv, zero)
    vm[1] = jnp.where(low, zero, vv)


def _na_window(r, n_rows):
    rs = jnp.clip(r - NA_ROWS // 2, 0, n_rows - NA_ROWS)
    return rs, r - rs


def _na_pair_window(ref, wrows):
    return jnp.concatenate([ref[0, wrows, :], ref[1, wrows, :]], axis=0)


def _na_scores(q, k2, tp_ref, off):
    bias = jnp.concatenate([tp_ref[h, 2 * w - off + (NA_ROWS - 1)] for h in range(2) for w in range(NA_ROWS // 2)],
                           axis=1)
    return _mm_nt(q, k2) * QK_SCALE + bias


def _pair_softmax(s):
    win = s.shape[1] // 2
    halves = []
    for h in range(2):
        sh = s[:, h * win:(h + 1) * win]
        e = jnp.exp(sh - jnp.max(sh, axis=-1, keepdims=True))
        halves.append(e / jnp.sum(e, axis=-1, keepdims=True))
    return jnp.concatenate(halves, axis=1)


def _na_forward(proj, tiles, batch, seq, rider=None):
    t = proj.shape[0]
    n_rows = seq // GRID_W
    n_pairs = NA_WIDTH // LANES
    win = NA_ROWS * GRID_W

    def body(q_ref, k_ref, v_ref, tp_ref, o_ref, km, vm):
        _na_prepare(k_ref, v_ref, km, vm)

        def scores(r):
            rs, off = _na_window(r, n_rows)
            rows = pl.ds(pl.multiple_of(r * GRID_W, GRID_W), GRID_W)
            wrows = pl.ds(pl.multiple_of(rs * GRID_W, GRID_W), win)
            return rows, wrows, _na_scores(q_ref[rows, :], _na_pair_window(km, wrows), tp_ref, off)

        def finish(rows, wrows, s):
            o_ref[rows, :] = _mm(_pair_softmax(s).astype(BF16), _na_pair_window(vm, wrows))

        def row_group(i, carry):
            for state in [scores(NA_GROUP * i + j) for j in range(NA_GROUP)]:
                finish(*state)
            return carry

        lax.fori_loop(0, n_rows // NA_GROUP, row_group, 0)

    return _hosted(
        body, rider, name="na_forward", grid=(batch, n_pairs),
        out_shape=[jax.ShapeDtypeStruct((t, NA_WIDTH), F32)],
        in_specs=[pl.BlockSpec((seq, LANES), lambda b, p: (b, p)),
                  pl.BlockSpec((seq, LANES), lambda b, p: (b, n_pairs + p)),
                  pl.BlockSpec((seq, LANES), lambda b, p: (b, 2 * n_pairs + p)),
                  pl.BlockSpec((2, 2 * NA_ROWS - 2, GRID_W, LANES), lambda b, p: (p, 0, 0, 0))],
        out_specs=[pl.BlockSpec((seq, LANES), lambda b, p: (b, p))],
        scratch_shapes=[pltpu.VMEM((2, seq, LANES), BF16), pltpu.VMEM((2, seq, LANES), BF16)],
        compiler_params=_cparams(("arbitrary", "arbitrary")), args=[proj, proj, proj, tiles])


def _sw_prepare(kv_ref, g, dst_lo, dst_hi, seq):
    lane = lax.broadcasted_iota(jnp.int32, kv_ref.shape, 1)
    mine = (lane // HEAD_DIM) == g
    kg = jnp.where(mine, kv_ref[...].astype(F32), 0.0)
    kr = pltpu.roll(kg, HEAD_DIM, 1)
    first = g == 0
    zero = jnp.zeros((SW_BLOCK, LANES), BF16)
    for dst, val in ((dst_lo, jnp.where(first, kg, kr)), (dst_hi, jnp.where(first, kr, kg))):
        dst[0:SW_BLOCK, :] = zero
        dst[SW_BLOCK:SW_BLOCK + seq, :] = val.astype(BF16)
        dst[SW_BLOCK + seq:, :] = zero


def _sw_mask(n, seq):
    qi = lax.broadcasted_iota(jnp.int32, (SW_BLOCK, 3 * SW_BLOCK), 0)
    kj = lax.broadcasted_iota(jnp.int32, (SW_BLOCK, 3 * SW_BLOCK), 1)
    kpos = n * SW_BLOCK - SW_BLOCK + kj
    return (jnp.abs(qi + SW_BLOCK - kj) <= SW_BLOCK) & (kpos >= 0) & (kpos < seq)


def _sw_probs(s2, ok, sinks):
    band = s2.shape[1] // 2
    halves, sink_p = [], []
    for i in range(2):
        s = jnp.where(ok, s2[:, i * band:(i + 1) * band], NEG)
        m = jnp.maximum(jnp.max(s, axis=-1, keepdims=True), sinks[i])
        p = jnp.exp(s - m)
        es = jnp.exp(sinks[i] - m)
        den = jnp.sum(p, axis=-1, keepdims=True) + es
        halves.append(p / den)
        sink_p.append(es / den)
    return jnp.concatenate(halves, axis=1), sink_p


def _sw_forward(proj, sink, batch, seq, rider=None):
    t = proj.shape[0]
    n_pairs = SW_WIDTH // LANES
    q_blk = 3 * NA_WIDTH // LANES
    k_blk = q_blk + n_pairs
    n_blocks = seq // SW_BLOCK
    pad = seq + 2 * SW_BLOCK

    def body(sink_ref, q_ref, k_ref, v_ref, o_ref, k_lo, k_hi, v_lo, v_hi):
        hp = pl.program_id(1)
        g = hp // 2
        _sw_prepare(k_ref, g, k_lo, k_hi, seq)
        _sw_prepare(v_ref, g, v_lo, v_hi, seq)

        sinks = (sink_ref[2 * hp], sink_ref[2 * hp + 1])

        def scores(n):
            rows = pl.ds(pl.multiple_of(n * SW_BLOCK, SW_BLOCK), SW_BLOCK)
            wrows = pl.ds(pl.multiple_of(n * SW_BLOCK, SW_BLOCK), 3 * SW_BLOCK)
            k2 = jnp.concatenate([k_lo[wrows, :], k_hi[wrows, :]], axis=0)
            return n, rows, wrows, _mm_nt(q_ref[rows, :], k2) * QK_SCALE

        def finish(n, rows, wrows, s2):
            p, _ = _sw_probs(s2, _sw_mask(n, seq), sinks)
            v2 = jnp.concatenate([v_lo[wrows, :], v_hi[wrows, :]], axis=0)
            o_ref[rows, :] = _mm(p.astype(BF16), v2)

        def block_group(i, carry):
            for state in [scores(SW_GROUP_BLOCKS * i + j) for j in range(SW_GROUP_BLOCKS)]:
                finish(*state)
            return carry

        lax.fori_loop(0, n_blocks // SW_GROUP_BLOCKS, block_group, 0)

    return _hosted(
        body, rider, name="sw_forward", grid=(batch, n_pairs),
        out_shape=[jax.ShapeDtypeStruct((t, SW_WIDTH), F32)],
        in_specs=[pl.BlockSpec(memory_space=pltpu.SMEM),
                  pl.BlockSpec((seq, LANES), lambda b, p: (b, q_blk + p)),
                  pl.BlockSpec((seq, LANES), lambda b, p: (b, k_blk)),
                  pl.BlockSpec((seq, LANES), lambda b, p: (b, k_blk + 1))],
        out_specs=[pl.BlockSpec((seq, LANES), lambda b, p: (b, p))],
        scratch_shapes=[pltpu.VMEM((pad, LANES), BF16)] * 4,
        compiler_params=_cparams(("arbitrary", "arbitrary")), args=[sink, proj, proj, proj])


def _out_proj(oa, ob, g_na, g_sw, w_out, x, mod3, g_ffn, seq):
    t, d = x.shape
    tm = TOKEN_TILE
    per_seq = seq // tm

    def body(oa_ref, ob_ref, gna_ref, gsw_ref, w_ref, x_ref, mod_ref, gf_ref, oab_ref, mix_ref, x1_ref, h2_ref):
        _, na = _rms_stats(oa_ref[...])
        _, nb = _rms_stats(ob_ref[...])
        oab = jnp.concatenate([na * gna_ref[...], nb * gsw_ref[...]], axis=1).astype(BF16)
        oab_ref[...] = oab
        mix = _mm(oab, w_ref[...])
        mix_ref[...] = mix
        gate_a = mod_ref[0, :, 2 * d:3 * d]
        shift_f, scale_f = mod_ref[0, :, 3 * d:4 * d], mod_ref[0, :, 4 * d:5 * d]
        x1 = x_ref[...] + gate_a * mix
        x1_ref[...] = x1
        _, xn = _rms_stats(x1)
        h2_ref[...] = ((xn * gf_ref[...]) * (1.0 + scale_f) + shift_f).astype(BF16)

    tile = lambda w: pl.BlockSpec((tm, w), lambda i: (i, 0))
    vec = lambda w: pl.BlockSpec((1, w), lambda i: (0, 0))
    return pl.pallas_call(
        body, name="out_proj", grid=(t // tm,),
        out_shape=(jax.ShapeDtypeStruct((t, d), BF16), jax.ShapeDtypeStruct((t, d), F32),
                   jax.ShapeDtypeStruct((t, d), F32), jax.ShapeDtypeStruct((t, d), BF16)),
        in_specs=[tile(NA_WIDTH), tile(SW_WIDTH), vec(NA_WIDTH), vec(SW_WIDTH),
                  pl.BlockSpec((d, d), lambda i: (0, 0)), tile(d),
                  pl.BlockSpec((1, 1, 6 * d), lambda i: (i // per_seq, 0, 0)), vec(d)],
        out_specs=(tile(d), tile(d), tile(d), tile(d)),
        compiler_params=_cparams(("arbitrary",), VMEM_BIG),
    )(oa, ob, g_na, g_sw, w_out, x, mod3, g_ffn)


def _up_proj(h2, w_up_halves, rider=None):
    t, d = h2.shape
    tm = TOKEN_TILE
    w_a, w_b = w_up_halves
    half, wcol = w_a.shape[1], w_a.shape[2]

    def body(h_ref, wa_ref, wb_ref, u_ref):
        u_ref[0] = _mm(h_ref[:, :half], wa_ref[0]) + _mm(h_ref[:, half:], wb_ref[0])

    w_spec = pl.BlockSpec((1, half, wcol), lambda j, i: (j, 0, 0))
    return _hosted(
        body, rider, name="up_proj", grid=(N_SHARD, t // tm),
        out_shape=[jax.ShapeDtypeStruct((2, t, D_FF), F32)],
        in_specs=[pl.BlockSpec((tm, d), lambda j, i: (i, 0)), w_spec, w_spec],
        out_specs=[pl.BlockSpec((1, tm, wcol), lambda j, i: (j // 2, i, j % 2))],
        scratch_shapes=[], compiler_params=_cparams(("arbitrary", "arbitrary"), VMEM_BIG), args=[h2, w_a, w_b])


def _taps_chunk(load, s, rows, seq):
    cur = load(s, rows)
    above = load(pl.multiple_of(jnp.maximum(s - SUBLANES, 0), SUBLANES), SUBLANES)
    below = load(pl.multiple_of(jnp.minimum(s + rows, seq - SUBLANES), SUBLANES), SUBLANES)
    up = jnp.where(s > 0, above[SUBLANES - 1:SUBLANES, :], 0.0)
    dn = jnp.where(s + rows < seq, below[0:1, :], 0.0)
    row = lax.broadcasted_iota(jnp.int32, cur.shape, 0)
    prev = jnp.where(row == 0, up, pltpu.roll(cur, 1, 0))
    nxt = jnp.where(row == rows - 1, dn, pltpu.roll(cur, rows - 1, 0))
    return cur, prev, nxt


def _conv_gate(u, conv_w, conv_b, batch, seq):
    t = u.shape[1]
    cw = FF_TILE
    rows = CONV_CHUNK

    def body(u_ref, w_ref, b_ref, a_ref):
        def chunk(i, carry):
            s = pl.multiple_of(i * rows, rows)
            gt, prev, nxt = _taps_chunk(lambda at, n: u_ref[1, pl.ds(at, n), :], s, rows, seq)
            gc = prev * w_ref[0:1, :] + gt * w_ref[1:2, :] + nxt * w_ref[2:3, :] + b_ref[...]
            a_ref[pl.ds(s, rows), :] = ((gc * _sigmoid(gc)) * u_ref[0, pl.ds(s, rows), :]).astype(BF16)
            return carry

        lax.fori_loop(0, seq // rows, chunk, 0)

    return pl.pallas_call(
        body, name="conv_gate", grid=(batch, D_FF // cw),
        out_shape=jax.ShapeDtypeStruct((t, D_FF), BF16),
        in_specs=[pl.BlockSpec((2, seq, cw), lambda b, j: (0, b, j)),
                  pl.BlockSpec((3, cw), lambda b, j: (0, j)), pl.BlockSpec((1, cw), lambda b, j: (0, j))],
        out_specs=pl.BlockSpec((seq, cw), lambda b, j: (b, j)),
        compiler_params=_cparams(("arbitrary", "arbitrary"), VMEM_BIG),
    )(u, conv_w, conv_b)


def _down_and_loss(a, w_down, x1, mod3, g_final, target, seq):
    t, d = x1.shape
    tm = TOKEN_TILE
    per_seq = seq // tm
    batch = t // seq

    def body(a_ref, w_ref, x1_ref, mod_ref, g_ref, tgt_ref, dx2_ref, dffn_ref, loss_ref, dgate_ref, dg_ref):
        i = pl.program_id(0)
        f = _mm(a_ref[...], w_ref[...])
        gate_f = mod_ref[0, :, 5 * d:6 * d]
        x2 = x1_ref[...] + gate_f * f
        r, xn = _rms_stats(x2)
        err = xn * g_ref[...] - tgt_ref[...]
        part = 0.5 * jnp.sum(jnp.mean(err * err, axis=-1, keepdims=True))
        dy = err / d
        dx2 = _rms_bwd(dy * g_ref[...], xn, r)
        dx2_ref[...] = dx2
        dffn_ref[...] = (dx2 * gate_f).astype(BF16)

        @pl.when(i == 0)
        def _():
            loss_ref[...] = jnp.zeros_like(loss_ref)
            dg_ref[...] = jnp.zeros_like(dg_ref)

        @pl.when(i % per_seq == 0)
        def _():
            dgate_ref[...] = jnp.zeros_like(dgate_ref)

        loss_ref[...] += part
        dg_ref[...] += jnp.sum(dy * xn, axis=0, keepdims=True)
        dgate_ref[0] += jnp.sum(dx2 * f, axis=0, keepdims=True)

    tile = lambda w: pl.BlockSpec((tm, w), lambda i: (i, 0))
    return pl.pallas_call(
        body, name="down_loss", grid=(t // tm,),
        out_shape=(jax.ShapeDtypeStruct((t, d), F32), jax.ShapeDtypeStruct((t, d), BF16),
                   jax.ShapeDtypeStruct((SUBLANES, LANES), F32), jax.ShapeDtypeStruct((batch, 1, d), F32),
                   jax.ShapeDtypeStruct((1, d), F32)),
        in_specs=[tile(D_FF), pl.BlockSpec((D_FF, d), lambda i: (0, 0)), tile(d),
                  pl.BlockSpec((1, 1, 6 * d), lambda i: (i // per_seq, 0, 0)),
                  pl.BlockSpec((1, d), lambda i: (0, 0)), tile(d)],
        out_specs=(tile(d), tile(d), pl.BlockSpec((SUBLANES, LANES), lambda i: (0, 0)),
                   pl.BlockSpec((1, 1, d), lambda i: (i // per_seq, 0, 0)), pl.BlockSpec((1, d), lambda i: (0, 0))),
        compiler_params=_cparams(("arbitrary",), VMEM_BIG),
    )(a, w_down, x1, mod3, g_final, target)


def _down_weight_grad(a, dffn):
    t, dff = a.shape
    d = dffn.shape[1]
    tk = TOKEN_TILE
    n_k = t // tk

    def body(a_ref, df_ref, g_ref, gb_ref):
        k = pl.program_id(0)

        @pl.when(k == 0)
        def _():
            g_ref[...] = jnp.zeros_like(g_ref)

        g_ref[...] += _mm_tn(a_ref[...], df_ref[...])

        @pl.when(k == n_k - 1)
        def _():
            gb_ref[...] = g_ref[...].astype(BF16)

    whole = pl.BlockSpec((dff, d), lambda k: (0, 0))
    return pl.pallas_call(
        body, name="down_weight_grad", grid=(n_k,),
        out_shape=(jax.ShapeDtypeStruct((dff, d), F32), jax.ShapeDtypeStruct((dff, d), BF16)),
        in_specs=[pl.BlockSpec((tk, dff), lambda k: (k, 0)), pl.BlockSpec((tk, d), lambda k: (k, 0))],
        out_specs=(whole, whole),
        compiler_params=_cparams(("arbitrary",), VMEM_BIG),
    )(a, dffn)


def _ffn_backward(dffn, w_down, u, conv_w, conv_b, batch, seq, rider=None):
    t, d = dffn.shape
    cw = FF_TILE
    rows = CONV_CHUNK

    def body(df_ref, wd_ref, u_ref, w_ref, b_ref, du_ref, gcw_ref, gcb_ref, da_scr, dgc_scr):
        b = pl.program_id(1)
        da_scr[...] = _mm_nt(df_ref[...], wd_ref[...])

        @pl.when(b == 0)
        def _():
            gcw_ref[...] = jnp.zeros_like(gcw_ref)
            gcb_ref[...] = jnp.zeros_like(gcb_ref)

        def fold(v):
            return jnp.sum(v.reshape(rows // SUBLANES, SUBLANES, cw), axis=0)

        def chunk(i, carry):
            s = pl.multiple_of(i * rows, rows)
            here = pl.ds(s, rows)
            gt, prev, nxt = _taps_chunk(lambda at, n: u_ref[1, pl.ds(at, n), :], s, rows, seq)
            val, da = u_ref[0, here, :], da_scr[here, :]
            gc = prev * w_ref[0:1, :] + gt * w_ref[1:2, :] + nxt * w_ref[2:3, :] + b_ref[...]
            sg = _sigmoid(gc)
            sl = gc * sg
            du_ref[0, here, :] = (da * sl).astype(BF16)
            dgc = (da * val) * (sg * (1.0 + gc * (1.0 - sg)))
            dgc_scr[here, :] = dgc
            cb, c0, c1, c2 = carry
            return cb + fold(dgc), c0 + fold(dgc * prev), c1 + fold(dgc * gt), c2 + fold(dgc * nxt)

        zero = jnp.zeros((SUBLANES, cw), F32)
        cb, c0, c1, c2 = lax.fori_loop(0, seq // rows, chunk, (zero, zero, zero, zero))
        gcb_ref[...] += jnp.sum(cb, axis=0, keepdims=True)
        gcw_ref[0:1, :] += jnp.sum(c0, axis=0, keepdims=True)
        gcw_ref[1:2, :] += jnp.sum(c1, axis=0, keepdims=True)
        gcw_ref[2:3, :] += jnp.sum(c2, axis=0, keepdims=True)

        def chunk2(i, carry):
            s = pl.multiple_of(i * rows, rows)
            dgc, dprev, dnxt = _taps_chunk(lambda at, n: dgc_scr[pl.ds(at, n), :], s, rows, seq)
            du_ref[1, pl.ds(s, rows), :] = (dnxt * w_ref[0:1, :] + dgc * w_ref[1:2, :]
                                            + dprev * w_ref[2:3, :]).astype(BF16)
            return carry

        lax.fori_loop(0, seq // rows, chunk2, 0)

    return _hosted(
        body, rider, name="ffn_backward", grid=(D_FF // cw, batch),
        out_shape=[jax.ShapeDtypeStruct((2, t, D_FF), BF16),
                   jax.ShapeDtypeStruct((3, D_FF), F32), jax.ShapeDtypeStruct((1, D_FF), F32)],
        in_specs=[pl.BlockSpec((seq, d), lambda j, b: (b, 0)), pl.BlockSpec((cw, d), lambda j, b: (j, 0)),
                  pl.BlockSpec((2, seq, cw), lambda j, b: (0, b, j)),
                  pl.BlockSpec((3, cw), lambda j, b: (0, j)), pl.BlockSpec((1, cw), lambda j, b: (0, j))],
        out_specs=[pl.BlockSpec((2, seq, cw), lambda j, b: (0, b, j)),
                   pl.BlockSpec((3, cw), lambda j, b: (0, j)), pl.BlockSpec((1, cw), lambda j, b: (0, j))],
        scratch_shapes=[pltpu.VMEM((seq, cw), F32), pltpu.VMEM((seq, cw), F32)],
        compiler_params=_cparams(("arbitrary", "arbitrary"), VMEM_BIG), args=[dffn, w_down, u, conv_w, conv_b])


def _up_backward(du, w_up, x1, mod3, g_ffn, dx2, mix, seq):
    _, t, _ = du.shape
    d = x1.shape[1]
    tm = TOKEN_TILE // 2
    per_seq = seq // tm
    batch = t // seq
    w_a, w_b = w_up
    half, wcol = w_a.shape[1], w_a.shape[2]

    def body(du_ref, wa_ref, wb_ref, x1_ref, mod_ref, g_ref, dx2_ref, mix_ref,
             dx1_ref, dmix_ref, dsh_ref, dsc_ref, dga_ref, dg_ref):
        i = pl.program_id(0)
        parts = []
        for w_ref in (wa_ref, wb_ref):
            acc = jnp.zeros((tm, half), F32)
            for j in range(N_SHARD):
                acc = acc + _mm_nt(du_ref[j // 2, :, (j % 2) * wcol:(j % 2 + 1) * wcol], w_ref[j])
            parts.append(acc)
        dh = jnp.concatenate(parts, axis=1)
        gate_a = mod_ref[0, :, 2 * d:3 * d]
        scale_f = mod_ref[0, :, 4 * d:5 * d]
        r, xn = _rms_stats(x1_ref[...])
        xg = xn * g_ref[...]
        dxg = dh * (1.0 + scale_f)
        dx1 = dx2_ref[...] + _rms_bwd(dxg * g_ref[...], xn, r)
        dx1_ref[...] = dx1
        dmix_ref[...] = (dx1 * gate_a).astype(BF16)

        @pl.when(i == 0)
        def _():
            dg_ref[...] = jnp.zeros_like(dg_ref)

        @pl.when(i % per_seq == 0)
        def _():
            dsh_ref[...] = jnp.zeros_like(dsh_ref)
            dsc_ref[...] = jnp.zeros_like(dsc_ref)
            dga_ref[...] = jnp.zeros_like(dga_ref)

        dg_ref[...] += jnp.sum(dxg * xn, axis=0, keepdims=True)
        dsh_ref[0] += jnp.sum(dh, axis=0, keepdims=True)
        dsc_ref[0] += jnp.sum(dh * xg, axis=0, keepdims=True)
        dga_ref[0] += jnp.sum(dx1 * mix_ref[...], axis=0, keepdims=True)

    tile = lambda w: pl.BlockSpec((tm, w), lambda i: (i, 0))
    per_b = pl.BlockSpec((1, 1, d), lambda i: (i // per_seq, 0, 0))
    small = jax.ShapeDtypeStruct((batch, 1, d), F32)
    return pl.pallas_call(
        body, name="up_backward", grid=(t // tm,),
        out_shape=(jax.ShapeDtypeStruct((t, d), F32), jax.ShapeDtypeStruct((t, d), BF16), small, small, small,
                   jax.ShapeDtypeStruct((1, d), F32)),
        in_specs=[pl.BlockSpec((2, tm, D_FF), lambda i: (0, i, 0)),
                  pl.BlockSpec((N_SHARD, half, wcol), lambda i: (0, 0, 0)),
                  pl.BlockSpec((N_SHARD, half, wcol), lambda i: (0, 0, 0)), tile(d),
                  pl.BlockSpec((1, 1, 6 * d), lambda i: (i // per_seq, 0, 0)),
                  pl.BlockSpec((1, d), lambda i: (0, 0)), tile(d), tile(d)],
        out_specs=(tile(d), tile(d), per_b, per_b, per_b, pl.BlockSpec((1, d), lambda i: (0, 0))),
        compiler_params=_cparams(("arbitrary",), VMEM_BIG),
    )(du, w_a, w_b, x1, mod3, g_ffn, dx2, mix)


def _up_weight_grad(h2, du, rider=None):
    t, d = h2.shape
    tk = TOKEN_TILE
    wcol = D_FF // 2
    n_k = t // tk

    def body(h_ref, du_ref, g_ref, gb_ref):
        k = pl.program_id(1)

        @pl.when(k == 0)
        def _():
            g_ref[...] = jnp.zeros_like(g_ref)

        g_ref[0] += _mm_tn(h_ref[...], du_ref[0])

        @pl.when(k == n_k - 1)
        def _():
            gb_ref[...] = g_ref[...].astype(BF16)

    g_spec = pl.BlockSpec((1, d, wcol), lambda j, k: (j, 0, 0))
    return _hosted(
        body, rider, name="up_weight_grad", grid=(N_SHARD, n_k),
        out_shape=[jax.ShapeDtypeStruct((N_SHARD, d, wcol), F32), jax.ShapeDtypeStruct((N_SHARD, d, wcol), BF16)],
        in_specs=[pl.BlockSpec((tk, d), lambda j, k: (k, 0)),
                  pl.BlockSpec((1, tk, wcol), lambda j, k: (j // 2, k, j % 2))],
        out_specs=[g_spec, g_spec], scratch_shapes=[],
        compiler_params=_cparams(("arbitrary", "arbitrary"), VMEM_BIG), args=[h2, du])


def _out_backward(dmix, w_out, oab, oa, ob, g_na, g_sw):
    t, d = dmix.shape
    tm = TOKEN_TILE
    hw = NA_WIDTH

    def body(dm_ref, w_ref, oab_ref, oa_ref, ob_ref, gna_ref, gsw_ref,
             doa_ref, dob_ref, gw_ref, gwb_ref, dgna_ref, dgsw_ref):
        @pl.when(pl.program_id(0) == 0)
        def _():
            gw_ref[...] = jnp.zeros_like(gw_ref)
            dgna_ref[...] = jnp.zeros_like(dgna_ref)
            dgsw_ref[...] = jnp.zeros_like(dgsw_ref)

        dm = dm_ref[...]
        gw_ref[...] += _mm_tn(oab_ref[...], dm)

        @pl.when(pl.program_id(0) == t // tm - 1)
        def _():
            gwb_ref[...] = gw_ref[...].astype(BF16)

        do = _mm_nt(dm, w_ref[...])
        for raw_ref, g_ref, dst_ref, dg_ref, lo in ((oa_ref, gna_ref, doa_ref, dgna_ref, 0),
                                                     (ob_ref, gsw_ref, dob_ref, dgsw_ref, hw)):
            r, xn = _rms_stats(raw_ref[...])
            dpart = do[:, lo:lo + hw]
            dg_ref[...] += jnp.sum(dpart * xn, axis=0, keepdims=True)
            dst_ref[...] = _rms_bwd(dpart * g_ref[...], xn, r).astype(BF16)

    tile = lambda w: pl.BlockSpec((tm, w), lambda i: (i, 0))
    vec = lambda w: pl.BlockSpec((1, w), lambda i: (0, 0))
    return pl.pallas_call(
        body, name="out_backward", grid=(t // tm,),
        out_shape=(jax.ShapeDtypeStruct((t, hw), BF16), jax.ShapeDtypeStruct((t, hw), BF16),
                   jax.ShapeDtypeStruct((d, d), F32), jax.ShapeDtypeStruct((d, d), BF16),
                   jax.ShapeDtypeStruct((1, hw), F32), jax.ShapeDtypeStruct((1, hw), F32)),
        in_specs=[tile(d), pl.BlockSpec((d, d), lambda i: (0, 0)), tile(d), tile(hw), tile(hw), vec(hw), vec(hw)],
        out_specs=(tile(hw), tile(hw), pl.BlockSpec((d, d), lambda i: (0, 0)), pl.BlockSpec((d, d), lambda i: (0, 0)),
                   vec(hw), vec(hw)),
        compiler_params=_cparams(("arbitrary",), VMEM_BIG),
    )(dmix, w_out, oab, oa, ob, g_na, g_sw)


def _na_backward(proj, d_o, tiles, batch, seq, rider=None):
    t = proj.shape[0]
    n_rows = seq // GRID_W
    n_pairs = NA_WIDTH // LANES
    win = NA_ROWS * GRID_W
    n_tiles = 2 * NA_ROWS - 2

    def body(q_ref, k_ref, v_ref, do_ref, tp_ref, dq_ref, dk_ref, dv_ref, dtp_ref, km, vm, dk_acc, dv_acc):
        @pl.when(pl.program_id(1) == 0)
        def _():
            dtp_ref[...] = jnp.zeros_like(dtp_ref)

        _na_prepare(k_ref, v_ref, km, vm)
        dk_acc[...] = jnp.zeros_like(dk_acc)
        dv_acc[...] = jnp.zeros_like(dv_acc)
        low = lax.broadcasted_iota(jnp.int32, (win, LANES), 1) < HEAD_DIM

        def scores(r):
            rs, off = _na_window(r, n_rows)
            rows = pl.ds(pl.multiple_of(r * GRID_W, GRID_W), GRID_W)
            wrows = pl.ds(pl.multiple_of(rs * GRID_W, GRID_W), win)
            q, do = q_ref[rows, :], do_ref[rows, :]
            k2 = _na_pair_window(km, wrows)
            s = _na_scores(q, k2, tp_ref, off)
            dp = _mm_nt(do, _na_pair_window(vm, wrows))
            return rows, wrows, off, q, do, k2, s, dp

        def finish(rows, wrows, off, q, do, k2, s, dp):
            p = _pair_softmax(s)
            parts = []
            for h in range(2):
                ph, dph = p[:, h * win:(h + 1) * win], dp[:, h * win:(h + 1) * win]
                dsh = ph * (dph - jnp.sum(ph * dph, axis=-1, keepdims=True))
                for w in range(NA_ROWS // 2):
                    dtp_ref[h, 2 * w - off + (NA_ROWS - 1)] += dsh[:, w * LANES:(w + 1) * LANES]
                parts.append(dsh)
            dsb = (jnp.concatenate(parts, axis=1) * QK_SCALE).astype(BF16)
            dq_ref[rows, :] = _mm(dsb, k2).astype(BF16)
            dk2 = _mm_tn(dsb, q)
            dv2 = _mm_tn(p.astype(BF16), do)
            dk_acc[wrows, :] += jnp.where(low, dk2[:win], dk2[win:])
            dv_acc[wrows, :] += jnp.where(low, dv2[:win], dv2[win:])

        def row_group(i, carry):
            for state in [scores(NA_GROUP * i + j) for j in range(NA_GROUP)]:
                finish(*state)
            return carry

        lax.fori_loop(0, n_rows // NA_GROUP, row_group, 0)
        dk_ref[...] = dk_acc[...].astype(BF16)
        dv_ref[...] = dv_acc[...].astype(BF16)

    blk = lambda off: pl.BlockSpec((seq, LANES), lambda p, b: (b, off + p))
    out = jax.ShapeDtypeStruct((t, NA_WIDTH), BF16)
    return _hosted(
        body, rider, name="na_backward", grid=(n_pairs, batch),
        out_shape=[out, out, out, jax.ShapeDtypeStruct(tiles.shape, F32)],
        in_specs=[blk(0), blk(n_pairs), blk(2 * n_pairs), blk(0),
                  pl.BlockSpec((2, n_tiles, GRID_W, LANES), lambda p, b: (p, 0, 0, 0))],
        out_specs=[blk(0), blk(0), blk(0), pl.BlockSpec((2, n_tiles, GRID_W, LANES), lambda p, b: (p, 0, 0, 0))],
        scratch_shapes=[pltpu.VMEM((2, seq, LANES), BF16), pltpu.VMEM((2, seq, LANES), BF16),
                        pltpu.VMEM((seq, LANES), F32), pltpu.VMEM((seq, LANES), F32)],
        compiler_params=_cparams(("arbitrary", "arbitrary")), args=[proj, proj, proj, d_o, tiles])


def _na_bias_grad(dtiles_flat, expand):
    n = dtiles_flat.shape[0]

    def body(t_ref, e_ref, o_ref):
        o_ref[...] = lax.dot_general(t_ref[...], e_ref[...], (((1,), (1,)), ((), ())),
                                     precision=lax.Precision.HIGHEST, preferred_element_type=F32)

    return pl.pallas_call(
        body, name="na_bias_grad",
        out_shape=jax.ShapeDtypeStruct((n, expand.shape[0]), F32),
        compiler_params=_cparams(vmem=VMEM_BIG),
    )(dtiles_flat, expand)


def _sw_backward(proj, d_o, sink, batch, seq, rider=None):
    t = proj.shape[0]
    n_pairs = SW_WIDTH // LANES
    q_blk = 3 * NA_WIDTH // LANES
    k_blk = q_blk + n_pairs
    n_blocks = seq // SW_BLOCK
    pad = seq + 2 * SW_BLOCK

    def body(sink_ref, q_ref, k_ref, v_ref, do_ref, dq_ref, dk_ref, dv_ref, dsk_ref,
             k_lo, k_hi, v_lo, v_hi, dk_loc, dv_loc, dk_tot, dv_tot):
        hp = pl.program_id(1)
        g = hp // 2
        _sw_prepare(k_ref, g, k_lo, k_hi, seq)
        _sw_prepare(v_ref, g, v_lo, v_hi, seq)
        dk_loc[...] = jnp.zeros_like(dk_loc)
        dv_loc[...] = jnp.zeros_like(dv_loc)

        @pl.when(hp == 0)
        def _():
            dk_tot[...] = jnp.zeros_like(dk_tot)
            dv_tot[...] = jnp.zeros_like(dv_tot)

        band = 3 * SW_BLOCK
        low = lax.broadcasted_iota(jnp.int32, (band, LANES), 1) < HEAD_DIM

        sinks = (sink_ref[2 * hp], sink_ref[2 * hp + 1])

        def scores(n):
            rows = pl.ds(pl.multiple_of(n * SW_BLOCK, SW_BLOCK), SW_BLOCK)
            wrows = pl.ds(pl.multiple_of(n * SW_BLOCK, SW_BLOCK), band)
            qb, do = q_ref[rows, :], do_ref[rows, :]
            k2 = jnp.concatenate([k_lo[wrows, :], k_hi[wrows, :]], axis=0)
            v2 = jnp.concatenate([v_lo[wrows, :], v_hi[wrows, :]], axis=0)
            return n, rows, wrows, qb, do, k2, _mm_nt(qb, k2) * QK_SCALE, _mm_nt(do, v2)

        def finish(sink_acc, n, rows, wrows, qb, do, k2, s2, dp):
            p, ps = _sw_probs(s2, _sw_mask(n, seq), sinks)
            parts, new = [], []
            for i in range(2):
                ph, dph = p[:, i * band:(i + 1) * band], dp[:, i * band:(i + 1) * band]
                delta = jnp.sum(ph * dph, axis=-1, keepdims=True)
                parts.append(ph * (dph - delta))
                new.append(sink_acc[i] - ps[i] * delta)
            dsb = (jnp.concatenate(parts, axis=1) * QK_SCALE).astype(BF16)
            dq_ref[rows, :] = _mm(dsb, k2)
            dk2 = _mm_tn(dsb, qb)
            dv2 = _mm_tn(p.astype(BF16), do)
            dk_loc[wrows, :] += jnp.where(low, dk2[:band], dk2[band:])
            dv_loc[wrows, :] += jnp.where(low, dv2[:band], dv2[band:])
            return tuple(new)

        def block_group(i, carry):
            for state in [scores(SW_GROUP_BLOCKS * i + j) for j in range(SW_GROUP_BLOCKS)]:
                carry = finish(carry, *state)
            return carry

        zero = jnp.zeros((SW_BLOCK, 1), F32)
        s0, s1 = lax.fori_loop(0, n_blocks // SW_GROUP_BLOCKS, block_group, (zero, zero))
        row = lax.broadcasted_iota(jnp.int32, (SUBLANES, LANES), 0)
        dsk_ref[0, 0] = jnp.where(row == 0, jnp.sum(s0), jnp.where(row == 1, jnp.sum(s1), 0.0))

        lane_s = lax.broadcasted_iota(jnp.int32, (seq, LANES), 1)
        mine_g = (lane_s // HEAD_DIM) == g
        for loc, tot in ((dk_loc, dk_tot), (dv_loc, dv_tot)):
            part = loc[SW_BLOCK:SW_BLOCK + seq, :]
            tot[...] += jnp.where(mine_g, part + pltpu.roll(part, HEAD_DIM, 1), 0.0)

        @pl.when(hp == n_pairs - 1)
        def _():
            dk_ref[...] = dk_tot[...]
            dv_ref[...] = dv_tot[...].astype(BF16)

    return _hosted(
        body, rider, name="sw_backward", grid=(batch, n_pairs),
        out_shape=[jax.ShapeDtypeStruct((t, SW_WIDTH), F32), jax.ShapeDtypeStruct((t, LANES), F32),
                   jax.ShapeDtypeStruct((t, LANES), BF16), jax.ShapeDtypeStruct((batch, n_pairs, SUBLANES, LANES), F32)],
        in_specs=[pl.BlockSpec(memory_space=pltpu.SMEM),
                  pl.BlockSpec((seq, LANES), lambda b, p: (b, q_blk + p)),
                  pl.BlockSpec((seq, LANES), lambda b, p: (b, k_blk)),
                  pl.BlockSpec((seq, LANES), lambda b, p: (b, k_blk + 1)),
                  pl.BlockSpec((seq, LANES), lambda b, p: (b, p))],
        out_specs=[pl.BlockSpec((seq, LANES), lambda b, p: (b, p)), pl.BlockSpec((seq, LANES), lambda b, p: (b, 0)),
                   pl.BlockSpec((seq, LANES), lambda b, p: (b, 0)),
                   pl.BlockSpec((1, 1, SUBLANES, LANES), lambda b, p: (b, p, 0, 0))],
        scratch_shapes=[pltpu.VMEM((pad, LANES), BF16)] * 4 + [pltpu.VMEM((pad, LANES), F32)] * 2
        + [pltpu.VMEM((seq, LANES), F32)] * 2,
        compiler_params=_cparams(("arbitrary", "arbitrary")), args=[sink, proj, proj, proj, d_o])


def _in_backward(dqkv_a, dq_b, dk_b, dv_b, w_in_t, h1, x, mod3, g_attn, dx1, cos_t, sin_t, seq):
    t, d = x.shape
    tm = TOKEN_TILE // 2
    per_seq = seq // tm
    batch = t // seq
    dqa, dka, dva = dqkv_a
    n_q = SW_WIDTH // LANES

    def body(dqa_ref, dka_ref, dva_ref, dqb_ref, dkb_ref, dvb_ref, w_ref, h_ref, x_ref, mod_ref, g_ref, dx1_ref,
             cos_ref, sin_ref, dx_ref, gw_ref, gwb_ref, dsh_ref, dsc_ref, dg_ref):
        i = pl.program_id(0)

        @pl.when(i == 0)
        def _():
            gw_ref[...] = jnp.zeros_like(gw_ref)
            dg_ref[...] = jnp.zeros_like(dg_ref)

        @pl.when(i % per_seq == 0)
        def _():
            dsh_ref[...] = jnp.zeros_like(dsh_ref)
            dsc_ref[...] = jnp.zeros_like(dsc_ref)

        dr = jnp.concatenate([dqb_ref[...], dkb_ref[...]], axis=1)
        cos = jnp.concatenate([cos_ref[...]] * (n_q + 1), axis=1)
        sin = jnp.concatenate([sin_ref[...]] * (n_q + 1), axis=1)
        dr = dr * cos + _rope_rot(dr * sin)
        dproj = jnp.concatenate([dqa_ref[...], dka_ref[...], dva_ref[...], dr.astype(BF16), dvb_ref[...]], axis=1)
        gw_ref[...] += _mm_tn(dproj, h_ref[...])

        @pl.when(i == t // tm - 1)
        def _():
            gwb_ref[...] = gw_ref[...].astype(BF16)

        dh = _mm(dproj, w_ref[...])
        scale = mod_ref[0, :, d:2 * d]
        r, xn = _rms_stats(x_ref[...])
        xg = xn * g_ref[...]
        dxg = dh * (1.0 + scale)
        dx_ref[...] = dx1_ref[...] + _rms_bwd(dxg * g_ref[...], xn, r)
        dg_ref[...] += jnp.sum(dxg * xn, axis=0, keepdims=True)
        dsh_ref[0] += jnp.sum(dh, axis=0, keepdims=True)
        dsc_ref[0] += jnp.sum(dh * xg, axis=0, keepdims=True)

    tile = lambda w: pl.BlockSpec((tm, w), lambda i: (i, 0))
    per_b = pl.BlockSpec((1, 1, d), lambda i: (i // per_seq, 0, 0))
    small = jax.ShapeDtypeStruct((batch, 1, d), F32)
    rope = pl.BlockSpec((tm, LANES), lambda i: (i % per_seq, 0))
    return pl.pallas_call(
        body, name="in_backward", grid=(t // tm,),
        out_shape=(jax.ShapeDtypeStruct((t, d), F32), jax.ShapeDtypeStruct((IN_WIDTH, d), F32),
                   jax.ShapeDtypeStruct((IN_WIDTH, d), BF16), small, small, jax.ShapeDtypeStruct((1, d), F32)),
        in_specs=[tile(NA_WIDTH), tile(NA_WIDTH), tile(NA_WIDTH), tile(SW_WIDTH), tile(LANES), tile(LANES),
                  pl.BlockSpec((IN_WIDTH, d), lambda i: (0, 0)), tile(d), tile(d),
                  pl.BlockSpec((1, 1, 6 * d), lambda i: (i // per_seq, 0, 0)),
                  pl.BlockSpec((1, d), lambda i: (0, 0)), tile(d), rope, rope],
        out_specs=(tile(d), pl.BlockSpec((IN_WIDTH, d), lambda i: (0, 0)), pl.BlockSpec((IN_WIDTH, d), lambda i: (0, 0)),
                   per_b, per_b, pl.BlockSpec((1, d), lambda i: (0, 0))),
        compiler_params=_cparams(("arbitrary",), VMEM_BIG),
    )(dqa, dka, dva, dq_b, dk_b, dv_b, w_in_t, h1, x, mod3, g_attn, dx1, cos_t, sin_t)


def _ada_weight_grad(sc_all, dmod_cols):
    d = sc_all.shape[1]
    ncol = dmod_cols.shape[1]

    def body(s_ref, m_ref, o_ref):
        o_ref[...] = _mm_tn(s_ref[...].astype(BF16), m_ref[...].astype(BF16))

    return pl.pallas_call(
        body, name="ada_weight_grad",
        out_shape=jax.ShapeDtypeStruct((d, ncol), F32),
        compiler_params=_cparams(vmem=VMEM_BIG),
    )(sc_all, dmod_cols)


def _row_tile(rows, cols):
    target = max(SUBLANES, (1 << 20) // (4 * cols))
    best = rows
    for cand in range(SUBLANES, rows + 1, SUBLANES):
        if rows % cand == 0 and cand <= target:
            best = cand
    return best if rows % SUBLANES == 0 else rows


def _sum_slots(recv, own, name):
    _, rows, cols = recv.shape
    tr = _row_tile(rows, cols)

    def body(p_ref, own_ref, o_ref):
        o_ref[...] = ((own_ref[...] + p_ref[0].astype(F32)) + p_ref[1].astype(F32)) + p_ref[2].astype(F32)

    return pl.pallas_call(
        body, name=name, grid=(rows // tr,),
        out_shape=jax.ShapeDtypeStruct((rows, cols), F32),
        in_specs=[pl.BlockSpec((N_SHARD - 1, tr, cols), lambda i: (0, i, 0)), pl.BlockSpec((tr, cols), lambda i: (i, 0))],
        out_specs=pl.BlockSpec((tr, cols), lambda i: (i, 0)),
        compiler_params=_cparams(("arbitrary",)),
    )(recv, own)


def _adamw(w, grads, m, v, name):
    rows, cols = w.shape
    tr = _row_tile(rows, cols)
    ng = len(grads)

    def body(*refs):
        w_ref = refs[0]
        g_refs = refs[1:1 + ng]
        m_ref, v_ref = refs[1 + ng], refs[2 + ng]
        g_out, d_out, m_out, v_out = refs[3 + ng:]
        g = g_refs[0][...]
        for extra in g_refs[1:]:
            g = g + extra[...]
        g_out[...] = g
        m2 = ADAM_B1 * m_ref[...] + (1.0 - ADAM_B1) * g
        v2 = ADAM_B2 * v_ref[...] + (1.0 - ADAM_B2) * (g * g)
        m_out[...] = m2
        v_out[...] = v2
        m_hat = m2 / (1.0 - ADAM_B1 ** ADAM_STEP)
        v_hat = v2 / (1.0 - ADAM_B2 ** ADAM_STEP)
        d_out[...] = -ADAM_LR * (m_hat / (jnp.sqrt(v_hat) + ADAM_EPS) + ADAM_WD * w_ref[...])

    spec = pl.BlockSpec((tr, cols), lambda i: (i, 0))
    out = jax.ShapeDtypeStruct((rows, cols), F32)
    return pl.pallas_call(
        body, name=name, grid=(rows // tr,),
        out_shape=(out, out, out, out),
        in_specs=[spec] * (3 + ng), out_specs=(spec, spec, spec, spec),
        compiler_params=_cparams(("arbitrary",)),
    )(w, *grads, m, v)


def _pack_rows(arrays):
    tile = SUBLANES * LANES
    rows, offsets, at = [], [], 0
    for a in arrays:
        flat = a.reshape(-1).astype(F32)
        n = -(-flat.shape[0] // tile) * tile
        rows.append(jnp.pad(flat, (0, n - flat.shape[0])).reshape(-1, LANES))
        offsets.append(at)
        at += n // LANES
    return jnp.concatenate(rows, axis=0), offsets


def _unpack_rows(packed, offsets, shapes):
    out = []
    for off, shape in zip(offsets, shapes):
        n = 1
        for s in shape:
            n *= s
        nrow = -(-n // LANES)
        out.append(packed[off:off + nrow].reshape(-1)[:n].reshape(shape))
    return out


def _rope_tables(seq):
    half = HEAD_DIM // 2
    inv = ROPE_THETA ** (-jnp.arange(half, dtype=F32) / half)
    ang = jnp.arange(seq).astype(F32)[:, None] * inv[None, :]
    cos, sin = jnp.cos(ang), jnp.sin(ang)
    cos_t = jnp.concatenate([cos, cos, cos, cos], axis=1)
    sin_t = jnp.concatenate([-sin, sin, -sin, sin], axis=1)
    return cos_t, sin_t


def kernel(x, c, w_ada, b_ada, g_attn, w_in, na_rpb, sw_sink, g_na_out, g_sw_out, w_out, g_ffn, w_up, conv_w, conv_b, w_down, g_final, loss_target, m_w_ada, m_b_ada, m_g_attn, m_w_in, m_na_rpb, m_sw_sink, m_g_na_out, m_g_sw_out, m_w_out, m_g_ffn, m_w_up, m_conv_w, m_conv_b, m_w_down, m_g_final, v_w_ada, v_b_ada, v_g_attn, v_w_in, v_na_rpb, v_sw_sink, v_g_na_out, v_g_sw_out, v_w_out, v_g_ffn, v_w_up, v_conv_w, v_conv_b, v_w_down, v_g_final):
    batch, seq, d = x.shape
    t = batch * seq
    assert d == D_MODEL and seq % (NA_ROWS * GRID_W) == 0 and seq % TOKEN_TILE == 0 and batch <= SUBLANES
    shard = 2 * lax.axis_index("x") + lax.axis_index("y")
    xt = x.reshape(t, d)
    tgt = loss_target.reshape(t, d)

    c8 = jnp.pad(c, ((0, SUBLANES - batch), (0, 0)))
    w_in_t_s = jnp.transpose(w_in[0]).astype(BF16)
    (mod8, sc_all), (w_in_g,) = _ada_forward(c8, w_ada[0], b_ada, _Rider("gather", [w_in_t_s]))
    mod3 = mod8[:batch].reshape(batch, 1, 6 * d)
    w_in_t = w_in_g.reshape(IN_WIDTH, d)

    cos_t, sin_t = _rope_tables(seq)
    (h1, proj), (w_out_g,) = _in_proj(xt, mod3, g_attn, w_in_t, cos_t, sin_t, seq,
                                      _Rider("gather", [w_out[0].astype(BF16)]))
    n_heads = NA_WIDTH // HEAD_DIM
    n_tiles, n_dc = 2 * NA_ROWS - 2, 2 * NA_COLS - 1
    expand, neg_mask = _na_bias_pattern()
    rpb = na_rpb[0]
    rows2 = jnp.concatenate([rpb[:, :-1, :], rpb[:, 1:, :]], axis=2).reshape(n_heads * n_tiles, 2 * n_dc)
    rows2 = jnp.pad(rows2, ((0, 0), (0, GRID_W - 2 * n_dc)))
    tiles = _na_bias_tiles(rows2, expand, neg_mask).reshape(n_heads, n_tiles, GRID_W, LANES)
    sink = sw_sink[0]
    w_up_b16 = w_up[0].astype(BF16)
    (oa,), (w_up_a,) = _na_forward(proj, tiles, batch, seq, _Rider("gather", [w_up_b16[:d // 2]]))
    (ob,), (w_up_b, conv_w_g) = _sw_forward(proj, sink, batch, seq, _Rider("gather", [w_up_b16[d // 2:], conv_w[0]]))
    w_up_f = (w_up_a, w_up_b)
    w_out_f = w_out_g.reshape(d, d)
    conv_w_f = jnp.transpose(conv_w_g, (1, 0, 2)).reshape(3, D_FF)
    oab, mix, x1, h2 = _out_proj(oa, ob, g_na_out, g_sw_out, w_out_f, xt, mod3, g_ffn, seq)
    (u,), (w_down_g,) = _up_proj(h2, w_up_f, _Rider("gather", [w_down[0].astype(BF16)]))
    w_down_f = w_down_g.reshape(D_FF, d)
    a = _conv_gate(u, conv_w_f, conv_b, batch, seq)
    dx2, dffn, loss_part, dgate_f, dg_final = _down_and_loss(a, w_down_f, x1, mod3, g_final.reshape(1, d), tgt, seq)

    gw_down, gw_down_b = _down_weight_grad(a, dffn)
    blocks = lambda g, rows: g.reshape(N_SHARD, rows // N_SHARD, d)
    (du, gconv_w, gconv_b), (recv_down, own_down) = _ffn_backward(
        dffn, w_down_f, u, conv_w_f, conv_b, batch, seq,
        _Rider("scatter", [blocks(gw_down_b, D_FF)], [blocks(gw_down, D_FF)]))
    (gw_up, gw_up_b), _ = _up_weight_grad(h2, du)
    dx1, dmix, dshift_f, dscale_f, dgate_a, dg_ffn = _up_backward(du, w_up_f, x1, mod3, g_ffn, dx2, mix, seq)
    doa, dob, gw_out, gw_out_b, dg_na, dg_sw = _out_backward(dmix, w_out_f, oab, oa, ob, g_na_out, g_sw_out)
    (dqa, dka, dva, dtiles), (recv_up, own_up) = _na_backward(
        proj, doa, tiles, batch, seq, _Rider("scatter", [gw_up_b], [gw_up]))
    (dq_b, dk_b, dv_b, dsink_parts), (recv_out, own_out) = _sw_backward(
        proj, dob, sink, batch, seq, _Rider("scatter", [blocks(gw_out_b, d)], [blocks(gw_out, d)]))
    gx, gw_in_t, gw_in_b, dshift_a, dscale_a, dg_attn = _in_backward(
        (dqa, dka, dva), dq_b, dk_b, dv_b, w_in_t, h1, xt, mod3, g_attn, dx1, cos_t, sin_t, seq)

    red = _na_bias_grad(dtiles.reshape(n_heads * n_tiles, GRID_W * LANES), expand)[:, :2 * n_dc]
    red = red.reshape(n_heads, n_tiles, 2, n_dc)
    zero_row = jnp.zeros((n_heads, 1, n_dc), F32)
    g_rpb = (jnp.concatenate([red[:, :, 0, :], zero_row], axis=1)
             + jnp.concatenate([zero_row, red[:, :, 1, :]], axis=1))
    g_sink = jnp.sum(dsink_parts[:, :, :2, 0], axis=0).reshape(SW_WIDTH // HEAD_DIM)

    dmod = jnp.concatenate([dshift_a, dscale_a, dgate_a, dshift_f, dscale_f, dgate_f], axis=2).reshape(batch, 6 * d)
    small_parts = [jnp.sum(dmod, axis=0), dg_attn, g_rpb, g_sink, dg_na, dg_sw, dg_ffn, gconv_w, gconv_b, dg_final,
                   loss_part[0, 0:1]]
    packed, offsets = _pack_rows(small_parts + [dmod])
    (summed, every), (recv_in, own_in) = _allreduce_small(
        packed, _Rider("scatter", [blocks(gw_in_b, IN_WIDTH)], [blocks(gw_in_t, IN_WIDTH)]))
    mine = [_sum_slots(r, o, name) for r, o, name in ((recv_in, own_in, "sum_w_in"), (recv_out, own_out, "sum_w_out"),
                                                      (recv_up, own_up, "sum_w_up"), (recv_down, own_down, "sum_w_down"))]
    theirs = _ride_alone(_Rider("swap", mine), "swap_sibling")
    small_shapes = [(1, 6 * d), (1, d), na_rpb.shape, sw_sink.shape, (1, NA_WIDTH), (1, SW_WIDTH), (1, d),
                    (3, D_FF), (1, D_FF), (d,), ()]
    (g_b_ada, g_g_attn, g_na_rpb, g_sw_sink, g_g_na, g_g_sw, g_g_ffn, g_conv_w_full, g_conv_b, g_g_final,
     loss) = _unpack_rows(summed, offsets[:-1], small_shapes)
    dmod_rows = every[:, offsets[-1]:offsets[-1] + batch * 6 * d // LANES, :].reshape(N_DEV, batch, 6 * d)
    dmod_rows = jnp.pad(dmod_rows, ((0, 0), (0, SUBLANES - batch), (0, 0))).reshape(N_DEV * SUBLANES, 6 * d)
    ncol = w_ada.shape[2]
    g_w_ada = _ada_weight_grad(sc_all, lax.dynamic_slice(dmod_rows, (0, shard * ncol), (N_DEV * SUBLANES, ncol)))
    cshard = conv_w.shape[2]
    g_conv_w = lax.dynamic_slice(g_conv_w_full, (0, shard * cshard), (3, cshard)).reshape(conv_w.shape)

    def big(w, m, v, g_parts, name):
        shape = w.shape
        outs = _adamw(w[0], g_parts, m[0], v[0], name)
        return [o.reshape(shape) for o in outs]

    r_w_ada = big(w_ada, m_w_ada, v_w_ada, [g_w_ada], "adamw_w_ada")
    r_w_in = big(w_in, m_w_in, v_w_in, [jnp.transpose(mine[0]), jnp.transpose(theirs[0])], "adamw_w_in")
    r_w_out = big(w_out, m_w_out, v_w_out, [mine[1], theirs[1]], "adamw_w_out")
    r_w_up = big(w_up, m_w_up, v_w_up, [mine[2], theirs[2]], "adamw_w_up")
    r_w_down = big(w_down, m_w_down, v_w_down, [mine[3], theirs[3]], "adamw_w_down")

    small_w = [b_ada, g_attn, na_rpb, sw_sink, g_na_out, g_sw_out, g_ffn, conv_w, conv_b, g_final]
    small_m = [m_b_ada, m_g_attn, m_na_rpb, m_sw_sink, m_g_na_out, m_g_sw_out, m_g_ffn, m_conv_w, m_conv_b, m_g_final]
    small_v = [v_b_ada, v_g_attn, v_na_rpb, v_sw_sink, v_g_na_out, v_g_sw_out, v_g_ffn, v_conv_w, v_conv_b, v_g_final]
    small_g = [g_b_ada, g_g_attn, g_na_rpb, g_sw_sink, g_g_na, g_g_sw, g_g_ffn, g_conv_w, g_conv_b, g_g_final]
    pw, offs = _pack_rows(small_w)
    pg, _ = _pack_rows(small_g)
    pm, _ = _pack_rows(small_m)
    pv, _ = _pack_rows(small_v)
    shapes = [w.shape for w in small_w]
    r_small = [_unpack_rows(o, offs, shapes) for o in _adamw(pw, [pg], pm, pv, "adamw_small")]

    def pick(k):
        b_, ga_, rpb_, sk_, gna_, gsw_, gf_, cw_, cb_, gfin_ = r_small[k]
        return [r_w_ada[k], b_, ga_, r_w_in[k], rpb_, sk_, gna_, gsw_, r_w_out[k], gf_, r_w_up[k], cw_, cb_,
                r_w_down[k], gfin_]

    return (loss, gx.reshape(batch, seq, d), *pick(0), *pick(1), *pick(2), *pick(3))
```

```python
import functools

import jax
import jax.numpy as jnp
import numpy as np
from jax import lax
from jax.experimental import pallas as pl
from jax.experimental.pallas import tpu as pltpu

F32 = jnp.float32
BF16 = jnp.bfloat16
MESH = pl.DeviceIdType.MESH

D_MODEL = 1024
HEAD_DIM = 64
NA_WIDTH = 512
SW_WIDTH = 512
SW_KV_WIDTH = 128
IN_WIDTH = 2304
D_FF = 2816
GRID_W = 64
NA_ROWS = 8
NA_COLS = 16
SW_BLOCK = 128
ROPE_THETA = 10000.0
EPS = 1e-6
NEG = -1e30
QK_SCALE = HEAD_DIM ** -0.5

ADAM_LR = 0.001
ADAM_B1 = 0.9
ADAM_B2 = 0.999
ADAM_EPS = 1e-08
ADAM_WD = 0.01
ADAM_STEP = 10

N_SHARD = 4
N_DEV = 8
LANES = 128
SUBLANES = 8
TOKEN_TILE = 512
FF_TILE = 256
CONV_CHUNK = 64
NA_GROUP = 4
SW_GROUP_BLOCKS = 4
VMEM_BIG = 56 * 1024 * 1024


def _mm(a, b):
    return jnp.dot(a, b, preferred_element_type=F32)


def _mm_nt(a, b):
    return lax.dot_general(a, b, (((1,), (1,)), ((), ())), preferred_element_type=F32)


def _mm_tn(a, b):
    return lax.dot_general(a, b, (((0,), (0,)), ((), ())), preferred_element_type=F32)


def _cparams(sem=None, vmem=None):
    kw = {}
    if sem is not None:
        kw["dimension_semantics"] = sem
    if vmem is not None:
        kw["vmem_limit_bytes"] = vmem
    return pltpu.CompilerParams(**kw)


def _sigmoid(x):
    return 1.0 / (1.0 + jnp.exp(-x))


def _rms_stats(x):
    r = lax.rsqrt(jnp.mean(x * x, axis=-1, keepdims=True) + EPS)
    return r, x * r


def _rms_bwd(dxn, xn, r):
    return r * (dxn - xn * jnp.mean(dxn * xn, axis=-1, keepdims=True))


def _my_pos():
    return lax.axis_index("x"), lax.axis_index("y"), lax.axis_index("c")


def _flip(v, bit):
    return 1 - v if bit else v


def _ada_forward(c8, w_ada, b_ada, rider):
    d = c8.shape[1]
    ncol = w_ada.shape[1]

    def body(c_ref, w_ref, b_ref, mod_ref, sc_ref, m_scr, mod_buf, ssem, rsem, ssem2, rsem2):
        x, y, c = _my_pos()
        me = 4 * x + 2 * y + c
        shard = 2 * x + y
        cv = c_ref[...]
        my_rows = pl.ds(pl.multiple_of(me * SUBLANES, SUBLANES), SUBLANES)
        sc_ref[my_rows, :] = cv * _sigmoid(cv)

        def copy1(k):
            peer = (_flip(x, (k >> 2) & 1), _flip(y, (k >> 1) & 1), _flip(c, k & 1))
            return pltpu.make_async_remote_copy(
                src_ref=sc_ref.at[my_rows, :], dst_ref=sc_ref.at[my_rows, :],
                send_sem=ssem.at[k - 1], recv_sem=rsem.at[k - 1], device_id=peer, device_id_type=MESH)

        sends = [copy1(k) for k in range(1, N_DEV)]
        for cp in sends:
            cp.start()
        for cp in sends:
            cp.wait_recv()
        m_scr[...] = _mm(sc_ref[...].astype(BF16), w_ref[...].astype(BF16))

        def copy2(k):
            px, py = _flip(x, (k >> 1) & 1), _flip(y, k & 1)
            rows = pl.ds(pl.multiple_of((4 * px + 2 * py + c) * SUBLANES, SUBLANES), SUBLANES)
            return pltpu.make_async_remote_copy(
                src_ref=m_scr.at[rows, :], dst_ref=mod_buf.at[shard],
                send_sem=ssem2.at[k - 1], recv_sem=rsem2.at[k - 1], device_id=(px, py, c), device_id_type=MESH)

        sends2 = [copy2(k) for k in range(1, N_SHARD)]
        for cp in sends2:
            cp.start()
        mod_buf[shard] = m_scr[my_rows, :]
        for cp in sends2:
            cp.wait_recv()
        for s in range(N_SHARD):
            mod_ref[:, s * ncol:(s + 1) * ncol] = mod_buf[s] + b_ref[:, s * ncol:(s + 1) * ncol]
        for cp in sends + sends2:
            cp.wait_send()

    vm = pl.BlockSpec(memory_space=pltpu.VMEM)
    return _hosted(
        body, rider, name="ada_forward", grid=(),
        out_shape=(jax.ShapeDtypeStruct((SUBLANES, N_SHARD * ncol), F32),
                   jax.ShapeDtypeStruct((N_DEV * SUBLANES, d), F32)),
        in_specs=[vm, vm, vm], out_specs=(vm, vm),
        scratch_shapes=[pltpu.VMEM((N_DEV * SUBLANES, ncol), F32), pltpu.VMEM((N_SHARD, SUBLANES, ncol), F32),
                        pltpu.SemaphoreType.DMA((N_DEV - 1,)), pltpu.SemaphoreType.DMA((N_DEV - 1,)),
                        pltpu.SemaphoreType.DMA((N_SHARD - 1,)), pltpu.SemaphoreType.DMA((N_SHARD - 1,))],
        compiler_params=_cparams(vmem=VMEM_BIG), args=[c8, w_ada, b_ada])


class _Rider:
    def __init__(self, kind, srcs, owns=()):
        self.kind, self.srcs, self.owns = kind, list(srcs), list(owns)
        n = len(self.srcs)
        sds = jax.ShapeDtypeStruct
        dma = pltpu.SemaphoreType.DMA
        if kind == "gather":
            self.out_shapes = [sds((N_SHARD,) + s.shape, s.dtype) for s in self.srcs]
            self.sems = [dma((n, N_SHARD - 1)), dma((n, N_SHARD - 1)), dma((n,))]
        elif kind == "scatter":
            self.out_shapes = ([sds((N_SHARD - 1,) + s.shape[1:], s.dtype) for s in self.srcs]
                               + [sds(o.shape[1:], o.dtype) for o in self.owns])
            self.sems = [dma((n, N_SHARD - 1)), dma((n, N_SHARD - 1)), dma((max(len(self.owns), 1),))]
        else:
            self.out_shapes = [sds(s.shape, s.dtype) for s in self.srcs]
            self.sems = [dma((n,)), dma((n,))]

    @property
    def inputs(self):
        return self.srcs + self.owns

    def copies(self, ins, outs, sems):
        n = len(self.srcs)
        x, y, c = _my_pos()
        shard = 2 * x + y
        local, remote = [], []
        if self.kind == "swap":
            ssem, rsem = sems
            for i in range(n):
                remote.append(pltpu.make_async_remote_copy(
                    src_ref=ins[i], dst_ref=outs[i], send_sem=ssem.at[i], recv_sem=rsem.at[i],
                    device_id=(x, y, 1 - c), device_id_type=MESH))
            return local, remote
        ssem, rsem, lsem = sems
        for i in range(n):
            if self.kind == "gather":
                local.append(pltpu.make_async_copy(ins[i], outs[i].at[shard], lsem.at[i]))
            for k in range(1, N_SHARD):
                px, py = _flip(x, (k >> 1) & 1), _flip(y, k & 1)
                if self.kind == "gather":
                    src, dst = ins[i], outs[i].at[shard]
                else:
                    src, dst = ins[i].at[2 * px + py], outs[i].at[k - 1]
                remote.append(pltpu.make_async_remote_copy(
                    src_ref=src, dst_ref=dst, send_sem=ssem.at[i, k - 1], recv_sem=rsem.at[i, k - 1],
                    device_id=(px, py, c), device_id_type=MESH))
        if self.kind == "scatter":
            for i in range(len(self.owns)):
                local.append(pltpu.make_async_copy(ins[n + i].at[shard], outs[n + i], lsem.at[i]))
        return local, remote

    def start(self, ins, outs, sems):
        local, remote = self.copies(ins, outs, sems)
        for cp in local + remote:
            cp.start()

    def wait(self, ins, outs, sems):
        local, remote = self.copies(ins, outs, sems)
        for cp in remote:
            cp.wait_recv()
        for cp in remote:
            cp.wait_send()
        for cp in local:
            cp.wait()


def _hosted(body, rider, *, name, grid, out_shape, in_specs, out_specs, scratch_shapes, compiler_params, args):
    out_shape, out_specs = list(out_shape), list(out_specs)
    if rider is None:
        outs = pl.pallas_call(body, name=name, grid=grid, out_shape=tuple(out_shape), in_specs=list(in_specs),
                              out_specs=tuple(out_specs), scratch_shapes=list(scratch_shapes),
                              compiler_params=compiler_params)(*args)
        return list(outs), []
    n_in, n_out, n_scr = len(in_specs), len(out_shape), len(scratch_shapes)
    nr_in, nr_out = len(rider.inputs), len(rider.out_shapes)

    def full(*refs):
        ins, refs = refs[:n_in], refs[n_in:]
        r_in, refs = refs[:nr_in], refs[nr_in:]
        outs, refs = refs[:n_out], refs[n_out:]
        r_out, refs = refs[:nr_out], refs[nr_out:]
        scr, sems = refs[:n_scr], refs[n_scr:]
        if grid:
            first = last = None
            for ax, size in enumerate(grid):
                f, l = pl.program_id(ax) == 0, pl.program_id(ax) == size - 1
                first = f if first is None else jnp.logical_and(first, f)
                last = l if last is None else jnp.logical_and(last, l)
            pl.when(first)(lambda: rider.start(r_in, r_out, sems))
            body(*ins, *outs, *scr)
            pl.when(last)(lambda: rider.wait(r_in, r_out, sems))
        else:
            rider.start(r_in, r_out, sems)
            body(*ins, *outs, *scr)
            rider.wait(r_in, r_out, sems)

    hbm = pl.BlockSpec(memory_space=pl.ANY)
    res = pl.pallas_call(
        full, name=name, grid=grid, out_shape=tuple(out_shape + rider.out_shapes),
        in_specs=list(in_specs) + [hbm] * nr_in, out_specs=tuple(out_specs + [hbm] * nr_out),
        scratch_shapes=list(scratch_shapes) + rider.sems, compiler_params=compiler_params,
    )(*args, *rider.inputs)
    return list(res[:n_out]), list(res[n_out:])


def _ride_alone(rider, name):
    return _hosted(lambda: None, rider, name=name, grid=(), out_shape=[], in_specs=[], out_specs=[], scratch_shapes=[],
                   compiler_params=_cparams(), args=[])[1]


def _allreduce_small(packed, rider=None):
    r = packed.shape[0]

    def body(p_ref, sum_ref, all_ref, ssem, rsem):
        x, y, c = _my_pos()
        me = 4 * x + 2 * y + c
        all_ref[me] = p_ref[...]
        cps = []
        for k in range(1, N_DEV):
            peer = (_flip(x, (k >> 2) & 1), _flip(y, (k >> 1) & 1), _flip(c, k & 1))
            cps.append(pltpu.make_async_remote_copy(
                src_ref=all_ref.at[me], dst_ref=all_ref.at[me], send_sem=ssem.at[k - 1], recv_sem=rsem.at[k - 1],
                device_id=peer, device_id_type=MESH))
        for cp in cps:
            cp.start()
        for cp in cps:
            cp.wait_recv()
        acc = all_ref[0]
        for dev in range(1, N_DEV):
            acc = acc + all_ref[dev]
        sum_ref[...] = acc
        for cp in cps:
            cp.wait_send()

    vm = pl.BlockSpec(memory_space=pltpu.VMEM)
    return _hosted(
        body, rider, name="allreduce_small", grid=(),
        out_shape=[jax.ShapeDtypeStruct((r, LANES), F32), jax.ShapeDtypeStruct((N_DEV, r, LANES), F32)],
        in_specs=[vm], out_specs=[vm, vm],
        scratch_shapes=[pltpu.SemaphoreType.DMA((N_DEV - 1,)), pltpu.SemaphoreType.DMA((N_DEV - 1,))],
        compiler_params=_cparams(), args=[packed])


def _rope_rot(t):
    w = t.shape[1]
    lane = lax.broadcasted_iota(jnp.int32, t.shape, 1)
    first = (lane % HEAD_DIM) < (HEAD_DIM // 2)
    return jnp.where(first, pltpu.roll(t, w - HEAD_DIM // 2, 1), pltpu.roll(t, HEAD_DIM // 2, 1))


def _in_proj(x, mod3, g_attn, w_in_t, cos_t, sin_t, seq, rider=None):
    t, d = x.shape
    tm = TOKEN_TILE
    per_seq = seq // tm
    rope_lo, rope_hi = 3 * NA_WIDTH, 3 * NA_WIDTH + SW_WIDTH + SW_KV_WIDTH
    n_rep = (rope_hi - rope_lo) // LANES

    def body(x_ref, mod_ref, g_ref, w_ref, cos_ref, sin_ref, h_ref, p_ref):
        r, xn = _rms_stats(x_ref[...])
        shift, scale = mod_ref[0, :, 0:d], mod_ref[0, :, d:2 * d]
        hb = ((xn * g_ref[...]) * (1.0 + scale) + shift).astype(BF16)
        h_ref[...] = hb
        p_ref[:, :rope_lo] = _mm_nt(hb, w_ref[:rope_lo, :]).astype(BF16)
        pr = _mm_nt(hb, w_ref[rope_lo:rope_hi, :])
        cos = jnp.concatenate([cos_ref[...]] * n_rep, axis=1)
        sin = jnp.concatenate([sin_ref[...]] * n_rep, axis=1)
        p_ref[:, rope_lo:rope_hi] = (pr * cos + _rope_rot(pr) * sin).astype(BF16)
        p_ref[:, rope_hi:] = _mm_nt(hb, w_ref[rope_hi:, :]).astype(BF16)

    return _hosted(
        body, rider, name="in_proj", grid=(t // tm,),
        out_shape=[jax.ShapeDtypeStruct((t, d), BF16), jax.ShapeDtypeStruct((t, IN_WIDTH), BF16)],
        in_specs=[pl.BlockSpec((tm, d), lambda i: (i, 0)),
                  pl.BlockSpec((1, 1, 6 * d), lambda i: (i // per_seq, 0, 0)),
                  pl.BlockSpec((1, d), lambda i: (0, 0)),
                  pl.BlockSpec((IN_WIDTH, d), lambda i: (0, 0)),
                  pl.BlockSpec((tm, LANES), lambda i: (i % per_seq, 0)),
                  pl.BlockSpec((tm, LANES), lambda i: (i % per_seq, 0))],
        out_specs=[pl.BlockSpec((tm, d), lambda i: (i, 0)), pl.BlockSpec((tm, IN_WIDTH), lambda i: (i, 0))],
        scratch_shapes=[], compiler_params=_cparams(("arbitrary",), VMEM_BIG),
        args=[x, mod3, g_attn, w_in_t, cos_t, sin_t])


def _na_bias_pattern():
    n_dc = 2 * NA_COLS - 1
    j = np.arange(GRID_W)[:, None]
    m = np.arange(GRID_W * LANES)[None, :]
    q, lane = m // LANES, m % LANES
    k = lane % GRID_W
    cs = np.clip(q - NA_COLS // 2, 0, GRID_W - NA_COLS)
    ok = (k >= cs) & (k < cs + NA_COLS)
    hit = ok & (j < 2 * n_dc) & (lane // GRID_W == j // n_dc) & (k - q + (NA_COLS - 1) == j % n_dc)
    return jnp.asarray(hit.astype(np.float32)), jnp.asarray(np.where(ok, 0.0, NEG).astype(np.float32))


def _na_bias_tiles(rows2, expand, mask):
    n, width = rows2.shape[0], expand.shape[1]
    q_step = 16
    step = q_step * LANES

    def body(r_ref, e_ref, m_ref, o_ref):
        flat = jnp.dot(r_ref[...], e_ref[...], precision=lax.Precision.HIGHEST,
                       preferred_element_type=F32) + m_ref[...]
        for qq in range(q_step):
            o_ref[:, qq, :] = flat[:, qq * LANES:(qq + 1) * LANES]

    return pl.pallas_call(
        body, name="na_bias_tiles", grid=(width // step,),
        out_shape=jax.ShapeDtypeStruct((n, GRID_W, LANES), F32),
        in_specs=[pl.BlockSpec(rows2.shape, lambda i: (0, 0)), pl.BlockSpec((expand.shape[0], step), lambda i: (0, i)),
                  pl.BlockSpec((1, step), lambda i: (0, i))],
        out_specs=pl.BlockSpec((n, q_step, LANES), lambda i: (0, i, 0)),
        compiler_params=_cparams(("arbitrary",)),
    )(rows2, expand, mask)


def _na_prepare(k_ref, v_ref, km, vm):
    lane = lax.broadcasted_iota(jnp.int32, k_ref.shape, 1)
    low = lane < HEAD_DIM
    kv = k_ref[...]
    vv = v_ref[...]
    zero = jnp.zeros_like(kv)
    km[0] = jnp.where(low, kv, zero)
    km[1] = jnp.where(low, zero, kv)
    vm[0] = jnp.where(low, vv, zero)
    vm[1] = jnp.where(low, zero, vv)


def _na_window(r, n_rows):
    rs = jnp.clip(r - NA_ROWS // 2, 0, n_rows - NA_ROWS)
    return rs, r - rs


def _na_pair_window(ref, wrows):
    return jnp.concatenate([ref[0, wrows, :], ref[1, wrows, :]], axis=0)


def _na_scores(q, k2, tp_ref, off):
    bias = jnp.concatenate([tp_ref[h, 2 * w - off + (NA_ROWS - 1)] for h in range(2) for w in range(NA_ROWS // 2)],
                           axis=1)
    return _mm_nt(q, k2) * QK_SCALE + bias


def _pair_softmax(s):
    win = s.shape[1] // 2
    halves = []
    for h in range(2):
        sh = s[:, h * win:(h + 1) * win]
        e = jnp.exp(sh - jnp.max(sh, axis=-1, keepdims=True))
        halves.append(e / jnp.sum(e, axis=-1, keepdims=True))
    return jnp.concatenate(halves, axis=1)


def _na_forward(proj, tiles, batch, seq, rider=None):
    t = proj.shape[0]
    n_rows = seq // GRID_W
    n_pairs = NA_WIDTH // LANES
    win = NA_ROWS * GRID_W

    def body(q_ref, k_ref, v_ref, tp_ref, o_ref, km, vm):
        _na_prepare(k_ref, v_ref, km, vm)

        def scores(r):
            rs, off = _na_window(r, n_rows)
            rows = pl.ds(pl.multiple_of(r * GRID_W, GRID_W), GRID_W)
            wrows = pl.ds(pl.multiple_of(rs * GRID_W, GRID_W), win)
            return rows, wrows, _na_scores(q_ref[rows, :], _na_pair_window(km, wrows), tp_ref, off)

        def finish(rows, wrows, s):
            o_ref[rows, :] = _mm(_pair_softmax(s).astype(BF16), _na_pair_window(vm, wrows))

        def row_group(i, carry):
            for state in [scores(NA_GROUP * i + j) for j in range(NA_GROUP)]:
                finish(*state)
            return carry

        lax.fori_loop(0, n_rows // NA_GROUP, row_group, 0)

    return _hosted(
        body, rider, name="na_forward", grid=(batch, n_pairs),
        out_shape=[jax.ShapeDtypeStruct((t, NA_WIDTH), F32)],
        in_specs=[pl.BlockSpec((seq, LANES), lambda b, p: (b, p)),
                  pl.BlockSpec((seq, LANES), lambda b, p: (b, n_pairs + p)),
                  pl.BlockSpec((seq, LANES), lambda b, p: (b, 2 * n_pairs + p)),
                  pl.BlockSpec((2, 2 * NA_ROWS - 2, GRID_W, LANES), lambda b, p: (p, 0, 0, 0))],
        out_specs=[pl.BlockSpec((seq, LANES), lambda b, p: (b, p))],
        scratch_shapes=[pltpu.VMEM((2, seq, LANES), BF16), pltpu.VMEM((2, seq, LANES), BF16)],
        compiler_params=_cparams(("arbitrary", "arbitrary")), args=[proj, proj, proj, tiles])


def _sw_prepare(kv_ref, g, dst_lo, dst_hi, seq):
    lane = lax.broadcasted_iota(jnp.int32, kv_ref.shape, 1)
    mine = (lane // HEAD_DIM) == g
    kg = jnp.where(mine, kv_ref[...].astype(F32), 0.0)
    kr = pltpu.roll(kg, HEAD_DIM, 1)
    first = g == 0
    zero = jnp.zeros((SW_BLOCK, LANES), BF16)
    for dst, val in ((dst_lo, jnp.where(first, kg, kr)), (dst_hi, jnp.where(first, kr, kg))):
        dst[0:SW_BLOCK, :] = zero
        dst[SW_BLOCK:SW_BLOCK + seq, :] = val.astype(BF16)
        dst[SW_BLOCK + seq:, :] = zero


def _sw_mask(n, seq):
    qi = lax.broadcasted_iota(jnp.int32, (SW_BLOCK, 3 * SW_BLOCK), 0)
    kj = lax.broadcasted_iota(jnp.int32, (SW_BLOCK, 3 * SW_BLOCK), 1)
    kpos = n * SW_BLOCK - SW_BLOCK + kj
    return (jnp.abs(qi + SW_BLOCK - kj) <= SW_BLOCK) & (kpos >= 0) & (kpos < seq)


def _sw_probs(s2, ok, sinks):
    band = s2.shape[1] // 2
    halves, sink_p = [], []
    for i in range(2):
        s = jnp.where(ok, s2[:, i * band:(i + 1) * band], NEG)
        m = jnp.maximum(jnp.max(s, axis=-1, keepdims=True), sinks[i])
        p = jnp.exp(s - m)
        es = jnp.exp(sinks[i] - m)
        den = jnp.sum(p, axis=-1, keepdims=True) + es
        halves.append(p / den)
        sink_p.append(es / den)
    return jnp.concatenate(halves, axis=1), sink_p


def _sw_forward(proj, sink, batch, seq, rider=None):
    t = proj.shape[0]
    n_pairs = SW_WIDTH // LANES
    q_blk = 3 * NA_WIDTH // LANES
    k_blk = q_blk + n_pairs
    n_blocks = seq // SW_BLOCK
    pad = seq + 2 * SW_BLOCK

    def body(sink_ref, q_ref, k_ref, v_ref, o_ref, k_lo, k_hi, v_lo, v_hi):
        hp = pl.program_id(1)
        g = hp // 2
        _sw_prepare(k_ref, g, k_lo, k_hi, seq)
        _sw_prepare(v_ref, g, v_lo, v_hi, seq)

        sinks = (sink_ref[2 * hp], sink_ref[2 * hp + 1])

        def scores(n):
            rows = pl.ds(pl.multiple_of(n * SW_BLOCK, SW_BLOCK), SW_BLOCK)
            wrows = pl.ds(pl.multiple_of(n * SW_BLOCK, SW_BLOCK), 3 * SW_BLOCK)
            k2 = jnp.concatenate([k_lo[wrows, :], k_hi[wrows, :]], axis=0)
            return n, rows, wrows, _mm_nt(q_ref[rows, :], k2) * QK_SCALE

        def finish(n, rows, wrows, s2):
            p, _ = _sw_probs(s2, _sw_mask(n, seq), sinks)
            v2 = jnp.concatenate([v_lo[wrows, :], v_hi[wrows, :]], axis=0)
            o_ref[rows, :] = _mm(p.astype(BF16), v2)

        def block_group(i, carry):
            for state in [scores(SW_GROUP_BLOCKS * i + j) for j in range(SW_GROUP_BLOCKS)]:
                finish(*state)
            return carry

        lax.fori_loop(0, n_blocks // SW_GROUP_BLOCKS, block_group, 0)

    return _hosted(
        body, rider, name="sw_forward", grid=(batch, n_pairs),
        out_shape=[jax.ShapeDtypeStruct((t, SW_WIDTH), F32)],
        in_specs=[pl.BlockSpec(memory_space=pltpu.SMEM),
                  pl.BlockSpec((seq, LANES), lambda b, p: (b, q_blk + p)),
                  pl.BlockSpec((seq, LANES), lambda b, p: (b, k_blk)),
                  pl.BlockSpec((seq, LANES), lambda b, p: (b, k_blk + 1))],
        out_specs=[pl.BlockSpec((seq, LANES), lambda b, p: (b, p))],
        scratch_shapes=[pltpu.VMEM((pad, LANES), BF16)] * 4,
        compiler_params=_cparams(("arbitrary", "arbitrary")), args=[sink, proj, proj, proj])


def _out_proj(oa, ob, g_na, g_sw, w_out, x, mod3, g_ffn, seq):
    t, d = x.shape
    tm = TOKEN_TILE
    per_seq = seq // tm

    def body(oa_ref, ob_ref, gna_ref, gsw_ref, w_ref, x_ref, mod_ref, gf_ref, oab_ref, mix_ref, x1_ref, h2_ref):
        _, na = _rms_stats(oa_ref[...])
        _, nb = _rms_stats(ob_ref[...])
        oab = jnp.concatenate([na * gna_ref[...], nb * gsw_ref[...]], axis=1).astype(BF16)
        oab_ref[...] = oab
        mix = _mm(oab, w_ref[...])
        mix_ref[...] = mix
        gate_a = mod_ref[0, :, 2 * d:3 * d]
        shift_f, scale_f = mod_ref[0, :, 3 * d:4 * d], mod_ref[0, :, 4 * d:5 * d]
        x1 = x_ref[...] + gate_a * mix
        x1_ref[...] = x1
        _, xn = _rms_stats(x1)
        h2_ref[...] = ((xn * gf_ref[...]) * (1.0 + scale_f) + shift_f).astype(BF16)

    tile = lambda w: pl.BlockSpec((tm, w), lambda i: (i, 0))
    vec = lambda w: pl.BlockSpec((1, w), lambda i: (0, 0))
    return pl.pallas_call(
        body, name="out_proj", grid=(t // tm,),
        out_shape=(jax.ShapeDtypeStruct((t, d), BF16), jax.ShapeDtypeStruct((t, d), F32),
                   jax.ShapeDtypeStruct((t, d), F32), jax.ShapeDtypeStruct((t, d), BF16)),
        in_specs=[tile(NA_WIDTH), tile(SW_WIDTH), vec(NA_WIDTH), vec(SW_WIDTH),
                  pl.BlockSpec((d, d), lambda i: (0, 0)), tile(d),
                  pl.BlockSpec((1, 1, 6 * d), lambda i: (i // per_seq, 0, 0)), vec(d)],
        out_specs=(tile(d), tile(d), tile(d), tile(d)),
        compiler_params=_cparams(("arbitrary",), VMEM_BIG),
    )(oa, ob, g_na, g_sw, w_out, x, mod3, g_ffn)


def _up_proj(h2, w_up_halves, rider=None):
    t, d = h2.shape
    tm = TOKEN_TILE
    w_a, w_b = w_up_halves
    half, wcol = w_a.shape[1], w_a.shape[2]

    def body(h_ref, wa_ref, wb_ref, u_ref):
        u_ref[0] = _mm(h_ref[:, :half], wa_ref[0]) + _mm(h_ref[:, half:], wb_ref[0])

    w_spec = pl.BlockSpec((1, half, wcol), lambda j, i: (j, 0, 0))
    return _hosted(
        body, rider, name="up_proj", grid=(N_SHARD, t // tm),
        out_shape=[jax.ShapeDtypeStruct((2, t, D_FF), F32)],
        in_specs=[pl.BlockSpec((tm, d), lambda j, i: (i, 0)), w_spec, w_spec],
        out_specs=[pl.BlockSpec((1, tm, wcol), lambda j, i: (j // 2, i, j % 2))],
        scratch_shapes=[], compiler_params=_cparams(("arbitrary", "arbitrary"), VMEM_BIG), args=[h2, w_a, w_b])


def _taps_chunk(load, s, rows, seq):
    cur = load(s, rows)
    above = load(pl.multiple_of(jnp.maximum(s - SUBLANES, 0), SUBLANES), SUBLANES)
    below = load(pl.multiple_of(jnp.minimum(s + rows, seq - SUBLANES), SUBLANES), SUBLANES)
    up = jnp.where(s > 0, above[SUBLANES - 1:SUBLANES, :], 0.0)
    dn = jnp.where(s + rows < seq, below[0:1, :], 0.0)
    row = lax.broadcasted_iota(jnp.int32, cur.shape, 0)
    prev = jnp.where(row == 0, up, pltpu.roll(cur, 1, 0))
    nxt = jnp.where(row == rows - 1, dn, pltpu.roll(cur, rows - 1, 0))
    return cur, prev, nxt


def _conv_gate(u, conv_w, conv_b, batch, seq):
    t = u.shape[1]
    cw = FF_TILE
    rows = CONV_CHUNK

    def body(u_ref, w_ref, b_ref, a_ref):
        def chunk(i, carry):
            s = pl.multiple_of(i * rows, rows)
            gt, prev, nxt = _taps_chunk(lambda at, n: u_ref[1, pl.ds(at, n), :], s, rows, seq)
            gc = prev * w_ref[0:1, :] + gt * w_ref[1:2, :] + nxt * w_ref[2:3, :] + b_ref[...]
            a_ref[pl.ds(s, rows), :] = ((gc * _sigmoid(gc)) * u_ref[0, pl.ds(s, rows), :]).astype(BF16)
            return carry

        lax.fori_loop(0, seq // rows, chunk, 0)

    return pl.pallas_call(
        body, name="conv_gate", grid=(batch, D_FF // cw),
        out_shape=jax.ShapeDtypeStruct((t, D_FF), BF16),
        in_specs=[pl.BlockSpec((2, seq, cw), lambda b, j: (0, b, j)),
                  pl.BlockSpec((3, cw), lambda b, j: (0, j)), pl.BlockSpec((1, cw), lambda b, j: (0, j))],
        out_specs=pl.BlockSpec((seq, cw), lambda b, j: (b, j)),
        compiler_params=_cparams(("arbitrary", "arbitrary"), VMEM_BIG),
    )(u, conv_w, conv_b)


def _down_and_loss(a, w_down, x1, mod3, g_final, target, seq):
    t, d = x1.shape
    tm = TOKEN_TILE
    per_seq = seq // tm
    batch = t // seq

    def body(a_ref, w_ref, x1_ref, mod_ref, g_ref, tgt_ref, dx2_ref, dffn_ref, loss_ref, dgate_ref, dg_ref):
        i = pl.program_id(0)
        f = _mm(a_ref[...], w_ref[...])
        gate_f = mod_ref[0, :, 5 * d:6 * d]
        x2 = x1_ref[...] + gate_f * f
        r, xn = _rms_stats(x2)
        err = xn * g_ref[...] - tgt_ref[...]
        part = 0.5 * jnp.sum(jnp.mean(err * err, axis=-1, keepdims=True))
        dy = err / d
        dx2 = _rms_bwd(dy * g_ref[...], xn, r)
        dx2_ref[...] = dx2
        dffn_ref[...] = (dx2 * gate_f).astype(BF16)

        @pl.when(i == 0)
        def _():
            loss_ref[...] = jnp.zeros_like(loss_ref)
            dg_ref[...] = jnp.zeros_like(dg_ref)

        @pl.when(i % per_seq == 0)
        def _():
            dgate_ref[...] = jnp.zeros_like(dgate_ref)

        loss_ref[...] += part
        dg_ref[...] += jnp.sum(dy * xn, axis=0, keepdims=True)
        dgate_ref[0] += jnp.sum(dx2 * f, axis=0, keepdims=True)

    tile = lambda w: pl.BlockSpec((tm, w), lambda i: (i, 0))
    return pl.pallas_call(
        body, name="down_loss", grid=(t // tm,),
        out_shape=(jax.ShapeDtypeStruct((t, d), F32), jax.ShapeDtypeStruct((t, d), BF16),
                   jax.ShapeDtypeStruct((SUBLANES, LANES), F32), jax.ShapeDtypeStruct((batch, 1, d), F32),
                   jax.ShapeDtypeStruct((1, d), F32)),
        in_specs=[tile(D_FF), pl.BlockSpec((D_FF, d), lambda i: (0, 0)), tile(d),
                  pl.BlockSpec((1, 1, 6 * d), lambda i: (i // per_seq, 0, 0)),
                  pl.BlockSpec((1, d), lambda i: (0, 0)), tile(d)],
        out_specs=(tile(d), tile(d), pl.BlockSpec((SUBLANES, LANES), lambda i: (0, 0)),
                   pl.BlockSpec((1, 1, d), lambda i: (i // per_seq, 0, 0)), pl.BlockSpec((1, d), lambda i: (0, 0))),
        compiler_params=_cparams(("arbitrary",), VMEM_BIG),
    )(a, w_down, x1, mod3, g_final, target)


def _down_weight_grad(a, dffn):
    t, dff = a.shape
    d = dffn.shape[1]
    tk = TOKEN_TILE
    n_k = t // tk

    def body(a_ref, df_ref, g_ref, gb_ref):
        k = pl.program_id(0)

        @pl.when(k == 0)
        def _():
            g_ref[...] = jnp.zeros_like(g_ref)

        g_ref[...] += _mm_tn(a_ref[...], df_ref[...])

        @pl.when(k == n_k - 1)
        def _():
            gb_ref[...] = g_ref[...].astype(BF16)

    whole = pl.BlockSpec((dff, d), lambda k: (0, 0))
    return pl.pallas_call(
        body, name="down_weight_grad", grid=(n_k,),
        out_shape=(jax.ShapeDtypeStruct((dff, d), F32), jax.ShapeDtypeStruct((dff, d), BF16)),
        in_specs=[pl.BlockSpec((tk, dff), lambda k: (k, 0)), pl.BlockSpec((tk, d), lambda k: (k, 0))],
        out_specs=(whole, whole),
        compiler_params=_cparams(("arbitrary",), VMEM_BIG),
    )(a, dffn)


def _ffn_backward(dffn, w_down, u, conv_w, conv_b, batch, seq, rider=None):
    t, d = dffn.shape
    cw = FF_TILE
    rows = CONV_CHUNK

    def body(df_ref, wd_ref, u_ref, w_ref, b_ref, du_ref, gcw_ref, gcb_ref, da_scr, dgc_scr):
        b = pl.program_id(1)
        da_scr[...] = _mm_nt(df_ref[...], wd_ref[...])

        @pl.when(b == 0)
        def _():
            gcw_ref[...] = jnp.zeros_like(gcw_ref)
            gcb_ref[...] = jnp.zeros_like(gcb_ref)

        def fold(v):
            return jnp.sum(v.reshape(rows // SUBLANES, SUBLANES, cw), axis=0)

        def chunk(i, carry):
            s = pl.multiple_of(i * rows, rows)
            here = pl.ds(s, rows)
            gt, prev, nxt = _taps_chunk(lambda at, n: u_ref[1, pl.ds(at, n), :], s, rows, seq)
            val, da = u_ref[0, here, :], da_scr[here, :]
            gc = prev * w_ref[0:1, :] + gt * w_ref[1:2, :] + nxt * w_ref[2:3, :] + b_ref[...]
            sg = _sigmoid(gc)
            sl = gc * sg
            du_ref[0, here, :] = (da * sl).astype(BF16)
            dgc = (da * val) * (sg * (1.0 + gc * (1.0 - sg)))
            dgc_scr[here, :] = dgc
            cb, c0, c1, c2 = carry
            return cb + fold(dgc), c0 + fold(dgc * prev), c1 + fold(dgc * gt), c2 + fold(dgc * nxt)

        zero = jnp.zeros((SUBLANES, cw), F32)
        cb, c0, c1, c2 = lax.fori_loop(0, seq // rows, chunk, (zero, zero, zero, zero))
        gcb_ref[...] += jnp.sum(cb, axis=0, keepdims=True)
        gcw_ref[0:1, :] += jnp.sum(c0, axis=0, keepdims=True)
        gcw_ref[1:2, :] += jnp.sum(c1, axis=0, keepdims=True)
        gcw_ref[2:3, :] += jnp.sum(c2, axis=0, keepdims=True)

        def chunk2(i, carry):
            s = pl.multiple_of(i * rows, rows)
            dgc, dprev, dnxt = _taps_chunk(lambda at, n: dgc_scr[pl.ds(at, n), :], s, rows, seq)
            du_ref[1, pl.ds(s, rows), :] = (dnxt * w_ref[0:1, :] + dgc * w_ref[1:2, :]
                                            + dprev * w_ref[2:3, :]).astype(BF16)
            return carry

        lax.fori_loop(0, seq // rows, chunk2, 0)

    return _hosted(
        body, rider, name="ffn_backward", grid=(D_FF // cw, batch),
        out_shape=[jax.ShapeDtypeStruct((2, t, D_FF), BF16),
                   jax.ShapeDtypeStruct((3, D_FF), F32), jax.ShapeDtypeStruct((1, D_FF), F32)],
        in_specs=[pl.BlockSpec((seq, d), lambda j, b: (b, 0)), pl.BlockSpec((cw, d), lambda j, b: (j, 0)),
                  pl.BlockSpec((2, seq, cw), lambda j, b: (0, b, j)),
                  pl.BlockSpec((3, cw), lambda j, b: (0, j)), pl.BlockSpec((1, cw), lambda j, b: (0, j))],
        out_specs=[pl.BlockSpec((2, seq, cw), lambda j, b: (0, b, j)),
                   pl.BlockSpec((3, cw), lambda j, b: (0, j)), pl.BlockSpec((1, cw), lambda j, b: (0, j))],
        scratch_shapes=[pltpu.VMEM((seq, cw), F32), pltpu.VMEM((seq, cw), F32)],
        compiler_params=_cparams(("arbitrary", "arbitrary"), VMEM_BIG), args=[dffn, w_down, u, conv_w, conv_b])


def _up_backward(du, w_up, x1, mod3, g_ffn, dx2, mix, seq, rider=None):
    _, t, _ = du.shape
    d = x1.shape[1]
    tm = TOKEN_TILE // 2
    per_seq = seq // tm
    batch = t // seq
    w_a, w_b = w_up
    half, wcol = w_a.shape[1], w_a.shape[2]

    def body(du_ref, wa_ref, wb_ref, x1_ref, mod_ref, g_ref, dx2_ref, mix_ref,
             dx1_ref, dmix_ref, dsh_ref, dsc_ref, dga_ref, dg_ref):
        i = pl.program_id(0)
        parts = []
        for w_ref in (wa_ref, wb_ref):
            acc = jnp.zeros((tm, half), F32)
            for j in range(N_SHARD):
                acc = acc + _mm_nt(du_ref[j // 2, :, (j % 2) * wcol:(j % 2 + 1) * wcol], w_ref[j])
            parts.append(acc)
        dh = jnp.concatenate(parts, axis=1)
        gate_a = mod_ref[0, :, 2 * d:3 * d]
        scale_f = mod_ref[0, :, 4 * d:5 * d]
        r, xn = _rms_stats(x1_ref[...])
        xg = xn * g_ref[...]
        dxg = dh * (1.0 + scale_f)
        dx1 = dx2_ref[...] + _rms_bwd(dxg * g_ref[...], xn, r)
        dx1_ref[...] = dx1
        dmix_ref[...] = (dx1 * gate_a).astype(BF16)

        @pl.when(i == 0)
        def _():
            dg_ref[...] = jnp.zeros_like(dg_ref)

        @pl.when(i % per_seq == 0)
        def _():
            dsh_ref[...] = jnp.zeros_like(dsh_ref)
            dsc_ref[...] = jnp.zeros_like(dsc_ref)
            dga_ref[...] = jnp.zeros_like(dga_ref)

        dg_ref[...] += jnp.sum(dxg * xn, axis=0, keepdims=True)
        dsh_ref[0] += jnp.sum(dh, axis=0, keepdims=True)
        dsc_ref[0] += jnp.sum(dh * xg, axis=0, keepdims=True)
        dga_ref[0] += jnp.sum(dx1 * mix_ref[...], axis=0, keepdims=True)

    tile = lambda w: pl.BlockSpec((tm, w), lambda i: (i, 0))
    per_b = pl.BlockSpec((1, 1, d), lambda i: (i // per_seq, 0, 0))
    small = jax.ShapeDtypeStruct((batch, 1, d), F32)
    return _hosted(
        body, rider, name="up_backward", grid=(t // tm,),
        out_shape=[jax.ShapeDtypeStruct((t, d), F32), jax.ShapeDtypeStruct((t, d), BF16), small, small, small,
                   jax.ShapeDtypeStruct((1, d), F32)],
        in_specs=[pl.BlockSpec((2, tm, D_FF), lambda i: (0, i, 0)),
                  pl.BlockSpec((N_SHARD, half, wcol), lambda i: (0, 0, 0)),
                  pl.BlockSpec((N_SHARD, half, wcol), lambda i: (0, 0, 0)), tile(d),
                  pl.BlockSpec((1, 1, 6 * d), lambda i: (i // per_seq, 0, 0)),
                  pl.BlockSpec((1, d), lambda i: (0, 0)), tile(d), tile(d)],
        out_specs=[tile(d), tile(d), per_b, per_b, per_b, pl.BlockSpec((1, d), lambda i: (0, 0))],
        scratch_shapes=[], compiler_params=_cparams(("arbitrary",), VMEM_BIG),
        args=[du, w_a, w_b, x1, mod3, g_ffn, dx2, mix])


def _up_weight_grad(h2, du, rider=None):
    t, d = h2.shape
    tk = TOKEN_TILE
    wcol = D_FF // 2
    half = d // 2
    n_k = t // tk

    def body(h_ref, du_ref, ga_ref, gb_ref, ga16_ref, gb16_ref):
        k = pl.program_id(1)

        @pl.when(k == 0)
        def _():
            ga_ref[...] = jnp.zeros_like(ga_ref)
            gb_ref[...] = jnp.zeros_like(gb_ref)

        du = du_ref[0]
        ga_ref[0] += _mm_tn(h_ref[:, :half], du)
        gb_ref[0] += _mm_tn(h_ref[:, half:], du)

        @pl.when(k == n_k - 1)
        def _():
            ga16_ref[...] = ga_ref[...].astype(BF16)
            gb16_ref[...] = gb_ref[...].astype(BF16)

    g_spec = pl.BlockSpec((1, half, wcol), lambda j, k: (j, 0, 0))
    f32_out = jax.ShapeDtypeStruct((N_SHARD, half, wcol), F32)
    b16_out = jax.ShapeDtypeStruct((N_SHARD, half, wcol), BF16)
    return _hosted(
        body, rider, name="up_weight_grad", grid=(N_SHARD, n_k),
        out_shape=[f32_out, f32_out, b16_out, b16_out],
        in_specs=[pl.BlockSpec((tk, d), lambda j, k: (k, 0)),
                  pl.BlockSpec((1, tk, wcol), lambda j, k: (j // 2, k, j % 2))],
        out_specs=[g_spec, g_spec, g_spec, g_spec], scratch_shapes=[],
        compiler_params=_cparams(("arbitrary", "arbitrary"), VMEM_BIG), args=[h2, du])


def _out_backward(dmix, w_out, oab, oa, ob, g_na, g_sw):
    t, d = dmix.shape
    tm = TOKEN_TILE
    hw = NA_WIDTH

    def body(dm_ref, w_ref, oab_ref, oa_ref, ob_ref, gna_ref, gsw_ref,
             doa_ref, dob_ref, gw_ref, gwb_ref, dgna_ref, dgsw_ref):
        @pl.when(pl.program_id(0) == 0)
        def _():
            gw_ref[...] = jnp.zeros_like(gw_ref)
            dgna_ref[...] = jnp.zeros_like(dgna_ref)
            dgsw_ref[...] = jnp.zeros_like(dgsw_ref)

        dm = dm_ref[...]
        gw_ref[...] += _mm_tn(oab_ref[...], dm)

        @pl.when(pl.program_id(0) == t // tm - 1)
        def _():
            gwb_ref[...] = gw_ref[...].astype(BF16)

        do = _mm_nt(dm, w_ref[...])
        for raw_ref, g_ref, dst_ref, dg_ref, lo in ((oa_ref, gna_ref, doa_ref, dgna_ref, 0),
                                                     (ob_ref, gsw_ref, dob_ref, dgsw_ref, hw)):
            r, xn = _rms_stats(raw_ref[...])
            dpart = do[:, lo:lo + hw]
            dg_ref[...] += jnp.sum(dpart * xn, axis=0, keepdims=True)
            dst_ref[...] = _rms_bwd(dpart * g_ref[...], xn, r).astype(BF16)

    tile = lambda w: pl.BlockSpec((tm, w), lambda i: (i, 0))
    vec = lambda w: pl.BlockSpec((1, w), lambda i: (0, 0))
    return pl.pallas_call(
        body, name="out_backward", grid=(t // tm,),
        out_shape=(jax.ShapeDtypeStruct((t, hw), BF16), jax.ShapeDtypeStruct((t, hw), BF16),
                   jax.ShapeDtypeStruct((d, d), F32), jax.ShapeDtypeStruct((d, d), BF16),
                   jax.ShapeDtypeStruct((1, hw), F32), jax.ShapeDtypeStruct((1, hw), F32)),
        in_specs=[tile(d), pl.BlockSpec((d, d), lambda i: (0, 0)), tile(d), tile(hw), tile(hw), vec(hw), vec(hw)],
        out_specs=(tile(hw), tile(hw), pl.BlockSpec((d, d), lambda i: (0, 0)), pl.BlockSpec((d, d), lambda i: (0, 0)),
                   vec(hw), vec(hw)),
        compiler_params=_cparams(("arbitrary",), VMEM_BIG),
    )(dmix, w_out, oab, oa, ob, g_na, g_sw)


def _na_backward(proj, d_o, tiles, batch, seq, rider=None):
    t = proj.shape[0]
    n_rows = seq // GRID_W
    n_pairs = NA_WIDTH // LANES
    win = NA_ROWS * GRID_W
    n_tiles = 2 * NA_ROWS - 2

    def body(q_ref, k_ref, v_ref, do_ref, tp_ref, dq_ref, dk_ref, dv_ref, dtp_ref, km, vm, dk_acc, dv_acc):
        @pl.when(pl.program_id(1) == 0)
        def _():
            dtp_ref[...] = jnp.zeros_like(dtp_ref)

        _na_prepare(k_ref, v_ref, km, vm)
        dk_acc[...] = jnp.zeros_like(dk_acc)
        dv_acc[...] = jnp.zeros_like(dv_acc)
        low = lax.broadcasted_iota(jnp.int32, (win, LANES), 1) < HEAD_DIM

        def scores(r):
            rs, off = _na_window(r, n_rows)
            rows = pl.ds(pl.multiple_of(r * GRID_W, GRID_W), GRID_W)
            wrows = pl.ds(pl.multiple_of(rs * GRID_W, GRID_W), win)
            q, do = q_ref[rows, :], do_ref[rows, :]
            k2 = _na_pair_window(km, wrows)
            s = _na_scores(q, k2, tp_ref, off)
            dp = _mm_nt(do, _na_pair_window(vm, wrows))
            return rows, wrows, off, q, do, k2, s, dp

        def finish(rows, wrows, off, q, do, k2, s, dp):
            p = _pair_softmax(s)
            parts = []
            for h in range(2):
                ph, dph = p[:, h * win:(h + 1) * win], dp[:, h * win:(h + 1) * win]
                dsh = ph * (dph - jnp.sum(ph * dph, axis=-1, keepdims=True))
                for w in range(NA_ROWS // 2):
                    dtp_ref[h, 2 * w - off + (NA_ROWS - 1)] += dsh[:, w * LANES:(w + 1) * LANES]
                parts.append(dsh)
            dsb = (jnp.concatenate(parts, axis=1) * QK_SCALE).astype(BF16)
            dq_ref[rows, :] = _mm(dsb, k2).astype(BF16)
            dk2 = _mm_tn(dsb, q)
            dv2 = _mm_tn(p.astype(BF16), do)
            dk_acc[wrows, :] += jnp.where(low, dk2[:win], dk2[win:])
            dv_acc[wrows, :] += jnp.where(low, dv2[:win], dv2[win:])

        def row_group(i, carry):
            for state in [scores(NA_GROUP * i + j) for j in range(NA_GROUP)]:
                finish(*state)
            return carry

        lax.fori_loop(0, n_rows // NA_GROUP, row_group, 0)
        dk_ref[...] = dk_acc[...].astype(BF16)
        dv_ref[...] = dv_acc[...].astype(BF16)

    blk = lambda off: pl.BlockSpec((seq, LANES), lambda p, b: (b, off + p))
    out = jax.ShapeDtypeStruct((t, NA_WIDTH), BF16)
    return _hosted(
        body, rider, name="na_backward", grid=(n_pairs, batch),
        out_shape=[out, out, out, jax.ShapeDtypeStruct(tiles.shape, F32)],
        in_specs=[blk(0), blk(n_pairs), blk(2 * n_pairs), blk(0),
                  pl.BlockSpec((2, n_tiles, GRID_W, LANES), lambda p, b: (p, 0, 0, 0))],
        out_specs=[blk(0), blk(0), blk(0), pl.BlockSpec((2, n_tiles, GRID_W, LANES), lambda p, b: (p, 0, 0, 0))],
        scratch_shapes=[pltpu.VMEM((2, seq, LANES), BF16), pltpu.VMEM((2, seq, LANES), BF16),
                        pltpu.VMEM((seq, LANES), F32), pltpu.VMEM((seq, LANES), F32)],
        compiler_params=_cparams(("arbitrary", "arbitrary")), args=[proj, proj, proj, d_o, tiles])


def _na_bias_grad(dtiles, expand):
    n = dtiles.shape[0]

    def body(t_ref, e_ref, o_ref):
        flat = jnp.concatenate([t_ref[:, qq, :] for qq in range(GRID_W)], axis=1)
        o_ref[...] = lax.dot_general(flat, e_ref[...], (((1,), (1,)), ((), ())),
                                     precision=lax.Precision.HIGHEST, preferred_element_type=F32)

    return pl.pallas_call(
        body, name="na_bias_grad",
        out_shape=jax.ShapeDtypeStruct((n, expand.shape[0]), F32),
        compiler_params=_cparams(vmem=VMEM_BIG),
    )(dtiles, expand)


def _sw_backward(proj, d_o, sink, batch, seq, rider=None):
    t = proj.shape[0]
    n_pairs = SW_WIDTH // LANES
    q_blk = 3 * NA_WIDTH // LANES
    k_blk = q_blk + n_pairs
    n_blocks = seq // SW_BLOCK
    pad = seq + 2 * SW_BLOCK

    def body(sink_ref, q_ref, k_ref, v_ref, do_ref, dq_ref, dk_ref, dv_ref, dsk_ref,
             k_lo, k_hi, v_lo, v_hi, dk_loc, dv_loc, dk_tot, dv_tot):
        hp = pl.program_id(1)
        g = hp // 2
        _sw_prepare(k_ref, g, k_lo, k_hi, seq)
        _sw_prepare(v_ref, g, v_lo, v_hi, seq)
        dk_loc[...] = jnp.zeros_like(dk_loc)
        dv_loc[...] = jnp.zeros_like(dv_loc)

        @pl.when(hp == 0)
        def _():
            dk_tot[...] = jnp.zeros_like(dk_tot)
            dv_tot[...] = jnp.zeros_like(dv_tot)

        band = 3 * SW_BLOCK
        low = lax.broadcasted_iota(jnp.int32, (band, LANES), 1) < HEAD_DIM

        sinks = (sink_ref[2 * hp], sink_ref[2 * hp + 1])

        def scores(n):
            rows = pl.ds(pl.multiple_of(n * SW_BLOCK, SW_BLOCK), SW_BLOCK)
            wrows = pl.ds(pl.multiple_of(n * SW_BLOCK, SW_BLOCK), band)
            qb, do = q_ref[rows, :], do_ref[rows, :]
            k2 = jnp.concatenate([k_lo[wrows, :], k_hi[wrows, :]], axis=0)
            v2 = jnp.concatenate([v_lo[wrows, :], v_hi[wrows, :]], axis=0)
            return n, rows, wrows, qb, do, k2, _mm_nt(qb, k2) * QK_SCALE, _mm_nt(do, v2)

        def finish(sink_acc, n, rows, wrows, qb, do, k2, s2, dp):
            p, ps = _sw_probs(s2, _sw_mask(n, seq), sinks)
            parts, new = [], []
            for i in range(2):
                ph, dph = p[:, i * band:(i + 1) * band], dp[:, i * band:(i + 1) * band]
                delta = jnp.sum(ph * dph, axis=-1, keepdims=True)
                parts.append(ph * (dph - delta))
                new.append(sink_acc[i] - ps[i] * delta)
            dsb = (jnp.concatenate(parts, axis=1) * QK_SCALE).astype(BF16)
            dq_ref[rows, :] = _mm(dsb, k2)
            dk2 = _mm_tn(dsb, qb)
            dv2 = _mm_tn(p.astype(BF16), do)
            dk_loc[wrows, :] += jnp.where(low, dk2[:band], dk2[band:])
            dv_loc[wrows, :] += jnp.where(low, dv2[:band], dv2[band:])
            return tuple(new)

        def block_group(i, carry):
            for state in [scores(SW_GROUP_BLOCKS * i + j) for j in range(SW_GROUP_BLOCKS)]:
                carry = finish(carry, *state)
            return carry

        zero = jnp.zeros((SW_BLOCK, 1), F32)
        s0, s1 = lax.fori_loop(0, n_blocks // SW_GROUP_BLOCKS, block_group, (zero, zero))
        row = lax.broadcasted_iota(jnp.int32, (SUBLANES, LANES), 0)
        dsk_ref[0, 0] = jnp.where(row == 0, jnp.sum(s0), jnp.where(row == 1, jnp.sum(s1), 0.0))

        lane_s = lax.broadcasted_iota(jnp.int32, (seq, LANES), 1)
        mine_g = (lane_s // HEAD_DIM) == g
        for loc, tot in ((dk_loc, dk_tot), (dv_loc, dv_tot)):
            part = loc[SW_BLOCK:SW_BLOCK + seq, :]
            tot[...] += jnp.where(mine_g, part + pltpu.roll(part, HEAD_DIM, 1), 0.0)

        @pl.when(hp == n_pairs - 1)
        def _():
            dk_ref[...] = dk_tot[...]
            dv_ref[...] = dv_tot[...].astype(BF16)

    return _hosted(
        body, rider, name="sw_backward", grid=(batch, n_pairs),
        out_shape=[jax.ShapeDtypeStruct((t, SW_WIDTH), F32), jax.ShapeDtypeStruct((t, LANES), F32),
                   jax.ShapeDtypeStruct((t, LANES), BF16), jax.ShapeDtypeStruct((batch, n_pairs, SUBLANES, LANES), F32)],
        in_specs=[pl.BlockSpec(memory_space=pltpu.SMEM),
                  pl.BlockSpec((seq, LANES), lambda b, p: (b, q_blk + p)),
                  pl.BlockSpec((seq, LANES), lambda b, p: (b, k_blk)),
                  pl.BlockSpec((seq, LANES), lambda b, p: (b, k_blk + 1)),
                  pl.BlockSpec((seq, LANES), lambda b, p: (b, p))],
        out_specs=[pl.BlockSpec((seq, LANES), lambda b, p: (b, p)), pl.BlockSpec((seq, LANES), lambda b, p: (b, 0)),
                   pl.BlockSpec((seq, LANES), lambda b, p: (b, 0)),
                   pl.BlockSpec((1, 1, SUBLANES, LANES), lambda b, p: (b, p, 0, 0))],
        scratch_shapes=[pltpu.VMEM((pad, LANES), BF16)] * 4 + [pltpu.VMEM((pad, LANES), F32)] * 2
        + [pltpu.VMEM((seq, LANES), F32)] * 2,
        compiler_params=_cparams(("arbitrary", "arbitrary")), args=[sink, proj, proj, proj, d_o])


def _in_backward(dqkv_a, dq_b, dk_b, dv_b, w_in_t, h1, x, mod3, g_attn, dx1, cos_t, sin_t, seq):
    t, d = x.shape
    tm = TOKEN_TILE // 2
    per_seq = seq // tm
    batch = t // seq
    dqa, dka, dva = dqkv_a
    n_q = SW_WIDTH // LANES

    def body(dqa_ref, dka_ref, dva_ref, dqb_ref, dkb_ref, dvb_ref, w_ref, h_ref, x_ref, mod_ref, g_ref, dx1_ref,
             cos_ref, sin_ref, dx_ref, gw_ref, gwb_ref, dsh_ref, dsc_ref, dg_ref):
        i = pl.program_id(0)

        @pl.when(i == 0)
        def _():
            gw_ref[...] = jnp.zeros_like(gw_ref)
            dg_ref[...] = jnp.zeros_like(dg_ref)

        @pl.when(i % per_seq == 0)
        def _():
            dsh_ref[...] = jnp.zeros_like(dsh_ref)
            dsc_ref[...] = jnp.zeros_like(dsc_ref)

        dr = jnp.concatenate([dqb_ref[...], dkb_ref[...]], axis=1)
        cos = jnp.concatenate([cos_ref[...]] * (n_q + 1), axis=1)
        sin = jnp.concatenate([sin_ref[...]] * (n_q + 1), axis=1)
        dr = dr * cos + _rope_rot(dr * sin)
        dproj = jnp.concatenate([dqa_ref[...], dka_ref[...], dva_ref[...], dr.astype(BF16), dvb_ref[...]], axis=1)
        gw_ref[...] += _mm_tn(dproj, h_ref[...])

        @pl.when(i == t // tm - 1)
        def _():
            gwb_ref[...] = gw_ref[...].astype(BF16)

        dh = _mm(dproj, w_ref[...])
        scale = mod_ref[0, :, d:2 * d]
        r, xn = _rms_stats(x_ref[...])
        xg = xn * g_ref[...]
        dxg = dh * (1.0 + scale)
        dx_ref[...] = dx1_ref[...] + _rms_bwd(dxg * g_ref[...], xn, r)
        dg_ref[...] += jnp.sum(dxg * xn, axis=0, keepdims=True)
        dsh_ref[0] += jnp.sum(dh, axis=0, keepdims=True)
        dsc_ref[0] += jnp.sum(dh * xg, axis=0, keepdims=True)

    tile = lambda w: pl.BlockSpec((tm, w), lambda i: (i, 0))
    per_b = pl.BlockSpec((1, 1, d), lambda i: (i // per_seq, 0, 0))
    small = jax.ShapeDtypeStruct((batch, 1, d), F32)
    rope = pl.BlockSpec((tm, LANES), lambda i: (i % per_seq, 0))
    return pl.pallas_call(
        body, name="in_backward", grid=(t // tm,),
        out_shape=(jax.ShapeDtypeStruct((t, d), F32), jax.ShapeDtypeStruct((IN_WIDTH, d), F32),
                   jax.ShapeDtypeStruct((IN_WIDTH, d), BF16), small, small, jax.ShapeDtypeStruct((1, d), F32)),
        in_specs=[tile(NA_WIDTH), tile(NA_WIDTH), tile(NA_WIDTH), tile(SW_WIDTH), tile(LANES), tile(LANES),
                  pl.BlockSpec((IN_WIDTH, d), lambda i: (0, 0)), tile(d), tile(d),
                  pl.BlockSpec((1, 1, 6 * d), lambda i: (i // per_seq, 0, 0)),
                  pl.BlockSpec((1, d), lambda i: (0, 0)), tile(d), rope, rope],
        out_specs=(tile(d), pl.BlockSpec((IN_WIDTH, d), lambda i: (0, 0)), pl.BlockSpec((IN_WIDTH, d), lambda i: (0, 0)),
                   per_b, per_b, pl.BlockSpec((1, d), lambda i: (0, 0))),
        compiler_params=_cparams(("arbitrary",), VMEM_BIG),
    )(dqa, dka, dva, dq_b, dk_b, dv_b, w_in_t, h1, x, mod3, g_attn, dx1, cos_t, sin_t)


def _ada_weight_grad(sc_all, dmod_cols):
    d = sc_all.shape[1]
    ncol = dmod_cols.shape[1]

    def body(s_ref, m_ref, o_ref):
        o_ref[...] = _mm_tn(s_ref[...].astype(BF16), m_ref[...].astype(BF16))

    return pl.pallas_call(
        body, name="ada_weight_grad",
        out_shape=jax.ShapeDtypeStruct((d, ncol), F32),
        compiler_params=_cparams(vmem=VMEM_BIG),
    )(sc_all, dmod_cols)


def _row_tile(rows, cols):
    target = max(SUBLANES, (1 << 20) // (4 * cols))
    best = rows
    for cand in range(SUBLANES, rows + 1, SUBLANES):
        if rows % cand == 0 and cand <= target:
            best = cand
    return best if rows % SUBLANES == 0 else rows


def _sum_slots(parts, name):
    n = len(parts)
    _, rows, cols = parts[0][0].shape
    tr = _row_tile(rows, cols)
    per = rows // tr

    def body(*refs):
        o_ref = refs[-1]
        for q in range(n):
            @pl.when(pl.program_id(0) == q)
            def _(q=q):
                p_ref, own_ref = refs[2 * q], refs[2 * q + 1]
                o_ref[...] = ((own_ref[...] + p_ref[0].astype(F32)) + p_ref[1].astype(F32)) + p_ref[2].astype(F32)

    in_specs, args = [], []
    for q, (recv, own) in enumerate(parts):
        in_specs.append(pl.BlockSpec((N_SHARD - 1, tr, cols), lambda p, i, q=q: (0, jnp.where(p == q, i, 0), 0)))
        in_specs.append(pl.BlockSpec((tr, cols), lambda p, i, q=q: (jnp.where(p == q, i, 0), 0)))
        args += [recv, own]
    return pl.pallas_call(
        body, name=name, grid=(n, per),
        out_shape=jax.ShapeDtypeStruct((n * rows, cols), F32),
        in_specs=in_specs, out_specs=pl.BlockSpec((tr, cols), lambda p, i: (p * per + i, 0)),
        compiler_params=_cparams(("arbitrary", "arbitrary")),
    )(*args)


def _adamw(w, grads, m, v, name):
    rows, cols = w.shape
    tr = _row_tile(rows, cols)
    ng = len(grads)

    def body(*refs):
        w_ref = refs[0]
        g_refs = refs[1:1 + ng]
        m_ref, v_ref = refs[1 + ng], refs[2 + ng]
        g_out, d_out, m_out, v_out = refs[3 + ng:]
        g = g_refs[0][...]
        for extra in g_refs[1:]:
            g = g + extra[...]
        g_out[...] = g
        m2 = ADAM_B1 * m_ref[...] + (1.0 - ADAM_B1) * g
        v2 = ADAM_B2 * v_ref[...] + (1.0 - ADAM_B2) * (g * g)
        m_out[...] = m2
        v_out[...] = v2
        m_hat = m2 / (1.0 - ADAM_B1 ** ADAM_STEP)
        v_hat = v2 / (1.0 - ADAM_B2 ** ADAM_STEP)
        d_out[...] = -ADAM_LR * (m_hat / (jnp.sqrt(v_hat) + ADAM_EPS) + ADAM_WD * w_ref[...])

    spec = pl.BlockSpec((tr, cols), lambda i: (i, 0))
    out = jax.ShapeDtypeStruct((rows, cols), F32)
    return pl.pallas_call(
        body, name=name, grid=(rows // tr,),
        out_shape=(out, out, out, out),
        in_specs=[spec] * (3 + ng), out_specs=(spec, spec, spec, spec),
        compiler_params=_cparams(("arbitrary",)),
    )(w, *grads, m, v)


def _pack_rows(arrays):
    tile = SUBLANES * LANES
    rows, offsets, at = [], [], 0
    for a in arrays:
        flat = a.reshape(-1).astype(F32)
        n = -(-flat.shape[0] // tile) * tile
        rows.append(jnp.pad(flat, (0, n - flat.shape[0])).reshape(-1, LANES))
        offsets.append(at)
        at += n // LANES
    return jnp.concatenate(rows, axis=0), offsets


def _unpack_rows(packed, offsets, shapes):
    out = []
    for off, shape in zip(offsets, shapes):
        n = 1
        for s in shape:
            n *= s
        nrow = -(-n // LANES)
        out.append(packed[off:off + nrow].reshape(-1)[:n].reshape(shape))
    return out


def _rope_tables(seq):
    half = HEAD_DIM // 2
    inv = np.float32(ROPE_THETA) ** (-np.arange(half, dtype=np.float32) / np.float32(half))
    ang = (np.arange(seq, dtype=np.float32)[:, None] * inv[None, :]).astype(np.float64)
    cos, sin = np.cos(ang).astype(np.float32), np.sin(ang).astype(np.float32)
    cos_t = np.concatenate([cos, cos, cos, cos], axis=1)
    sin_t = np.concatenate([-sin, sin, -sin, sin], axis=1)
    return jnp.asarray(cos_t), jnp.asarray(sin_t)


def kernel(x, c, w_ada, b_ada, g_attn, w_in, na_rpb, sw_sink, g_na_out, g_sw_out, w_out, g_ffn, w_up, conv_w, conv_b, w_down, g_final, loss_target, m_w_ada, m_b_ada, m_g_attn, m_w_in, m_na_rpb, m_sw_sink, m_g_na_out, m_g_sw_out, m_w_out, m_g_ffn, m_w_up, m_conv_w, m_conv_b, m_w_down, m_g_final, v_w_ada, v_b_ada, v_g_attn, v_w_in, v_na_rpb, v_sw_sink, v_g_na_out, v_g_sw_out, v_w_out, v_g_ffn, v_w_up, v_conv_w, v_conv_b, v_w_down, v_g_final):
    batch, seq, d = x.shape
    t = batch * seq
    assert d == D_MODEL and seq % (NA_ROWS * GRID_W) == 0 and seq % TOKEN_TILE == 0 and batch <= SUBLANES
    shard = 2 * lax.axis_index("x") + lax.axis_index("y")
    xt = x.reshape(t, d)
    tgt = loss_target.reshape(t, d)

    c8 = jnp.pad(c, ((0, SUBLANES - batch), (0, 0)))
    w_in_t_s = jnp.transpose(w_in[0]).astype(BF16)
    (mod8, sc_all), (w_in_g,) = _ada_forward(c8, w_ada[0], b_ada, _Rider("gather", [w_in_t_s]))
    mod3 = mod8[:batch].reshape(batch, 1, 6 * d)
    w_in_t = w_in_g.reshape(IN_WIDTH, d)

    cos_t, sin_t = _rope_tables(seq)
    (h1, proj), (w_out_g,) = _in_proj(xt, mod3, g_attn, w_in_t, cos_t, sin_t, seq,
                                      _Rider("gather", [w_out[0].astype(BF16)]))
    n_heads = NA_WIDTH // HEAD_DIM
    n_tiles, n_dc = 2 * NA_ROWS - 2, 2 * NA_COLS - 1
    expand, neg_mask = _na_bias_pattern()
    rpb = na_rpb[0]
    rows2 = jnp.concatenate([rpb[:, :-1, :], rpb[:, 1:, :]], axis=2).reshape(n_heads * n_tiles, 2 * n_dc)
    rows2 = jnp.pad(rows2, ((0, 0), (0, GRID_W - 2 * n_dc)))
    tiles = _na_bias_tiles(rows2, expand, neg_mask).reshape(n_heads, n_tiles, GRID_W, LANES)
    sink = sw_sink[0]
    w_up_b16 = w_up[0].astype(BF16)
    (oa,), (w_up_a,) = _na_forward(proj, tiles, batch, seq, _Rider("gather", [w_up_b16[:d // 2]]))
    (ob,), (w_up_b, conv_w_g) = _sw_forward(proj, sink, batch, seq, _Rider("gather", [w_up_b16[d // 2:], conv_w[0]]))
    w_up_f = (w_up_a, w_up_b)
    w_out_f = w_out_g.reshape(d, d)
    conv_w_f = jnp.transpose(conv_w_g, (1, 0, 2)).reshape(3, D_FF)
    oab, mix, x1, h2 = _out_proj(oa, ob, g_na_out, g_sw_out, w_out_f, xt, mod3, g_ffn, seq)
    (u,), (w_down_g,) = _up_proj(h2, w_up_f, _Rider("gather", [w_down[0].astype(BF16)]))
    w_down_f = w_down_g.reshape(D_FF, d)
    a = _conv_gate(u, conv_w_f, conv_b, batch, seq)
    dx2, dffn, loss_part, dgate_f, dg_final = _down_and_loss(a, w_down_f, x1, mod3, g_final.reshape(1, d), tgt, seq)

    gw_down, gw_down_b = _down_weight_grad(a, dffn)
    blocks = lambda g, rows: g.reshape(N_SHARD, rows // N_SHARD, d)
    (du, gconv_w, gconv_b), (recv_down, own_down) = _ffn_backward(
        dffn, w_down_f, u, conv_w_f, conv_b, batch, seq,
        _Rider("scatter", [blocks(gw_down_b, D_FF)], [blocks(gw_down, D_FF)]))
    (gw_up_top, gw_up_bot, gw_up_top_b, gw_up_bot_b), _ = _up_weight_grad(h2, du)
    (dx1, dmix, dshift_f, dscale_f, dgate_a, dg_ffn), (recv_up_top, own_up_top) = _up_backward(
        du, w_up_f, x1, mod3, g_ffn, dx2, mix, seq, _Rider("scatter", [gw_up_top_b], [gw_up_top]))
    doa, dob, gw_out, gw_out_b, dg_na, dg_sw = _out_backward(dmix, w_out_f, oab, oa, ob, g_na_out, g_sw_out)
    (dqa, dka, dva, dtiles), (recv_up_bot, own_up_bot) = _na_backward(
        proj, doa, tiles, batch, seq, _Rider("scatter", [gw_up_bot_b], [gw_up_bot]))
    (dq_b, dk_b, dv_b, dsink_parts), (recv_out, own_out) = _sw_backward(
        proj, dob, sink, batch, seq, _Rider("scatter", [blocks(gw_out_b, d)], [blocks(gw_out, d)]))
    gx, gw_in_t, gw_in_b, dshift_a, dscale_a, dg_attn = _in_backward(
        (dqa, dka, dva), dq_b, dk_b, dv_b, w_in_t, h1, xt, mod3, g_attn, dx1, cos_t, sin_t, seq)

    red = _na_bias_grad(dtiles.reshape(n_heads * n_tiles, GRID_W, LANES), expand)[:, :2 * n_dc]
    red = red.reshape(n_heads, n_tiles, 2, n_dc)
    zero_row = jnp.zeros((n_heads, 1, n_dc), F32)
    g_rpb = (jnp.concatenate([red[:, :, 0, :], zero_row], axis=1)
             + jnp.concatenate([zero_row, red[:, :, 1, :]], axis=1))
    g_sink = jnp.sum(dsink_parts[:, :, :2, 0], axis=0).reshape(SW_WIDTH // HEAD_DIM)

    dmod = jnp.concatenate([dshift_a, dscale_a, dgate_a, dshift_f, dscale_f, dgate_f], axis=2).reshape(batch, 6 * d)
    small_parts = [jnp.sum(dmod, axis=0), dg_attn, g_rpb, g_sink, dg_na, dg_sw, dg_ffn, gconv_w, gconv_b, dg_final,
                   loss_part[0, 0:1]]
    packed, offsets = _pack_rows(small_parts + [dmod])
    (summed, every), (recv_in, own_in) = _allreduce_small(
        packed, _Rider("scatter", [blocks(gw_in_b, IN_WIDTH)], [blocks(gw_in_t, IN_WIDTH)]))
    mine = [_sum_slots([(recv_in, own_in)], "sum_w_in"), _sum_slots([(recv_out, own_out)], "sum_w_out"),
            _sum_slots([(recv_up_top, own_up_top), (recv_up_bot, own_up_bot)], "sum_w_up"),
            _sum_slots([(recv_down, own_down)], "sum_w_down")]
    theirs = _ride_alone(_Rider("swap", mine), "swap_sibling")
    small_shapes = [(1, 6 * d), (1, d), na_rpb.shape, sw_sink.shape, (1, NA_WIDTH), (1, SW_WIDTH), (1, d),
                    (3, D_FF), (1, D_FF), (d,), ()]
    (g_b_ada, g_g_attn, g_na_rpb, g_sw_sink, g_g_na, g_g_sw, g_g_ffn, g_conv_w_full, g_conv_b, g_g_final,
     loss) = _unpack_rows(summed, offsets[:-1], small_shapes)
    dmod_rows = every[:, offsets[-1]:offsets[-1] + batch * 6 * d // LANES, :].reshape(N_DEV, batch, 6 * d)
    dmod_rows = jnp.pad(dmod_rows, ((0, 0), (0, SUBLANES - batch), (0, 0))).reshape(N_DEV * SUBLANES, 6 * d)
    ncol = w_ada.shape[2]
    g_w_ada = _ada_weight_grad(sc_all, lax.dynamic_slice(dmod_rows, (0, shard * ncol), (N_DEV * SUBLANES, ncol)))
    cshard = conv_w.shape[2]
    g_conv_w = lax.dynamic_slice(g_conv_w_full, (0, shard * cshard), (3, cshard)).reshape(conv_w.shape)

    def big(w, m, v, g_parts, name):
        shape = w.shape
        outs = _adamw(w[0], g_parts, m[0], v[0], name)
        return [o.reshape(shape) for o in outs]

    r_w_ada = big(w_ada, m_w_ada, v_w_ada, [g_w_ada], "adamw_w_ada")
    r_w_in = [jnp.transpose(o).reshape(w_in.shape) for o in
              _adamw(jnp.transpose(w_in[0]), [mine[0], theirs[0]], jnp.transpose(m_w_in[0]), jnp.transpose(v_w_in[0]),
                     "adamw_w_in")]
    r_w_out = big(w_out, m_w_out, v_w_out, [mine[1], theirs[1]], "adamw_w_out")
    r_w_up = big(w_up, m_w_up, v_w_up, [mine[2], theirs[2]], "adamw_w_up")
    r_w_down = big(w_down, m_w_down, v_w_down, [mine[3], theirs[3]], "adamw_w_down")

    small_w = [b_ada, g_attn, na_rpb, sw_sink, g_na_out, g_sw_out, g_ffn, conv_w, conv_b, g_final]
    small_m = [m_b_ada, m_g_attn, m_na_rpb, m_sw_sink, m_g_na_out, m_g_sw_out, m_g_ffn, m_conv_w, m_conv_b, m_g_final]
    small_v = [v_b_ada, v_g_attn, v_na_rpb, v_sw_sink, v_g_na_out, v_g_sw_out, v_g_ffn, v_conv_w, v_conv_b, v_g_final]
    small_g = [g_b_ada, g_g_attn, g_na_rpb, g_sw_sink, g_g_na, g_g_sw, g_g_ffn, g_conv_w, g_conv_b, g_g_final]
    pw, offs = _pack_rows(small_w)
    pg, _ = _pack_rows(small_g)
    pm, _ = _pack_rows(small_m)
    pv, _ = _pack_rows(small_v)
    shapes = [w.shape for w in small_w]
    r_small = [_unpack_rows(o, offs, shapes) for o in _adamw(pw, [pg], pm, pv, "adamw_small")]

    def pick(k):
        b_, ga_, rpb_, sk_, gna_, gsw_, gf_, cw_, cb_, gfin_ = r_small[k]
        return [r_w_ada[k], b_, ga_, r_w_in[k], rpb_, sk_, gna_, gsw_, r_w_out[k], gf_, r_w_up[k], cw_, cb_,
                r_w_down[k], gfin_]

    return (loss, gx.reshape(batch, seq, d), *pick(0), *pick(1), *pick(2), *pick(3))
```

```python
import functools

import jax
import jax.numpy as jnp
import numpy as np
from jax import lax
from jax.experimental import pallas as pl
from jax.experimental.pallas import tpu as pltpu

F32 = jnp.float32
BF16 = jnp.bfloat16
MESH = pl.DeviceIdType.MESH

D_MODEL = 1024
HEAD_DIM = 64
NA_WIDTH = 512
SW_WIDTH = 512
SW_KV_WIDTH = 128
IN_WIDTH = 2304
D_FF = 2816
GRID_W = 64
NA_ROWS = 8
NA_COLS = 16
SW_BLOCK = 128
ROPE_THETA = 10000.0
EPS = 1e-6
NEG = -1e30
QK_SCALE = HEAD_DIM ** -0.5

ADAM_LR = 0.001
ADAM_B1 = 0.9
ADAM_B2 = 0.999
ADAM_EPS = 1e-08
ADAM_WD = 0.01
ADAM_STEP = 10

N_SHARD = 4
N_DEV = 8
LANES = 128
SUBLANES = 8
TOKEN_TILE = 512
FF_TILE = 256
CONV_CHUNK = 64
NA_GROUP = 4
SW_GROUP_BLOCKS = 4
VMEM_BIG = 56 * 1024 * 1024


def _mm(a, b):
    return jnp.dot(a, b, preferred_element_type=F32)


def _mm_nt(a, b):
    return lax.dot_general(a, b, (((1,), (1,)), ((), ())), preferred_element_type=F32)


def _mm_tn(a, b):
    return lax.dot_general(a, b, (((0,), (0,)), ((), ())), preferred_element_type=F32)


def _cparams(sem=None, vmem=None):
    kw = {}
    if sem is not None:
        kw["dimension_semantics"] = sem
    if vmem is not None:
        kw["vmem_limit_bytes"] = vmem
    return pltpu.CompilerParams(**kw)


def _sigmoid(x):
    return 1.0 / (1.0 + jnp.exp(-x))


def _rms_stats(x):
    r = lax.rsqrt(jnp.mean(x * x, axis=-1, keepdims=True) + EPS)
    return r, x * r


def _rms_bwd(dxn, xn, r):
    return r * (dxn - xn * jnp.mean(dxn * xn, axis=-1, keepdims=True))


def _my_pos():
    return lax.axis_index("x"), lax.axis_index("y"), lax.axis_index("c")


def _flip(v, bit):
    return 1 - v if bit else v


def _ada_forward(c8, w_ada, b_ada, rider):
    d = c8.shape[1]
    ncol = w_ada.shape[1]

    def body(c_ref, w_ref, b_ref, mod_ref, sc_ref, m_scr, mod_buf, ssem, rsem, ssem2, rsem2):
        x, y, c = _my_pos()
        me = 4 * x + 2 * y + c
        shard = 2 * x + y
        cv = c_ref[...]
        my_rows = pl.ds(pl.multiple_of(me * SUBLANES, SUBLANES), SUBLANES)
        sc_ref[my_rows, :] = cv * _sigmoid(cv)

        def copy1(k):
            peer = (_flip(x, (k >> 2) & 1), _flip(y, (k >> 1) & 1), _flip(c, k & 1))
            return pltpu.make_async_remote_copy(
                src_ref=sc_ref.at[my_rows, :], dst_ref=sc_ref.at[my_rows, :],
                send_sem=ssem.at[k - 1], recv_sem=rsem.at[k - 1], device_id=peer, device_id_type=MESH)

        sends = [copy1(k) for k in range(1, N_DEV)]
        for cp in sends:
            cp.start()
        for cp in sends:
            cp.wait_recv()
        m_scr[...] = _mm(sc_ref[...].astype(BF16), w_ref[...].astype(BF16))

        def copy2(k):
            px, py = _flip(x, (k >> 1) & 1), _flip(y, k & 1)
            rows = pl.ds(pl.multiple_of((4 * px + 2 * py + c) * SUBLANES, SUBLANES), SUBLANES)
            return pltpu.make_async_remote_copy(
                src_ref=m_scr.at[rows, :], dst_ref=mod_buf.at[shard],
                send_sem=ssem2.at[k - 1], recv_sem=rsem2.at[k - 1], device_id=(px, py, c), device_id_type=MESH)

        sends2 = [copy2(k) for k in range(1, N_SHARD)]
        for cp in sends2:
            cp.start()
        mod_buf[shard] = m_scr[my_rows, :]
        for cp in sends2:
            cp.wait_recv()
        for s in range(N_SHARD):
            mod_ref[:, s * ncol:(s + 1) * ncol] = mod_buf[s] + b_ref[:, s * ncol:(s + 1) * ncol]
        for cp in sends + sends2:
            cp.wait_send()

    vm = pl.BlockSpec(memory_space=pltpu.VMEM)
    return _hosted(
        body, rider, name="ada_forward", grid=(),
        out_shape=(jax.ShapeDtypeStruct((SUBLANES, N_SHARD * ncol), F32),
                   jax.ShapeDtypeStruct((N_DEV * SUBLANES, d), F32)),
        in_specs=[vm, vm, vm], out_specs=(vm, vm),
        scratch_shapes=[pltpu.VMEM((N_DEV * SUBLANES, ncol), F32), pltpu.VMEM((N_SHARD, SUBLANES, ncol), F32),
                        pltpu.SemaphoreType.DMA((N_DEV - 1,)), pltpu.SemaphoreType.DMA((N_DEV - 1,)),
                        pltpu.SemaphoreType.DMA((N_SHARD - 1,)), pltpu.SemaphoreType.DMA((N_SHARD - 1,))],
        compiler_params=_cparams(vmem=VMEM_BIG), args=[c8, w_ada, b_ada])


class _Rider:
    def __init__(self, kind, srcs, owns=()):
        self.kind, self.srcs, self.owns = kind, list(srcs), list(owns)
        n = len(self.srcs)
        sds = jax.ShapeDtypeStruct
        dma = pltpu.SemaphoreType.DMA
        if kind == "gather":
            self.out_shapes = [sds((N_SHARD,) + s.shape, s.dtype) for s in self.srcs]
            self.sems = [dma((n, N_SHARD - 1)), dma((n, N_SHARD - 1)), dma((n,))]
        elif kind == "scatter":
            self.out_shapes = ([sds((N_SHARD - 1,) + s.shape[1:], s.dtype) for s in self.srcs]
                               + [sds(o.shape[1:], o.dtype) for o in self.owns])
            self.sems = [dma((n, N_SHARD - 1)), dma((n, N_SHARD - 1)), dma((max(len(self.owns), 1),))]
        else:
            self.out_shapes = [sds(s.shape, s.dtype) for s in self.srcs]
            self.sems = [dma((n,)), dma((n,))]

    @property
    def inputs(self):
        return self.srcs + self.owns

    def plan(self, n_steps):
        lead = 0 if self.kind != "scatter" else 1
        self.n_chunks = [_row_chunks(a.shape[lead], a.dtype, n_steps) for a in self.srcs + self.owns]
        return max(self.n_chunks)

    def copies(self, ins, outs, sems, chunk=None):
        n = len(self.srcs)
        x, y, c = _my_pos()
        shard = 2 * x + y
        local, remote = [], []

        def rows(ref, i):
            if chunk is None:
                return ref
            size = ref.shape[0] // self.n_chunks[i]
            return ref.at[pl.ds(chunk * size, size)]

        def live(i):
            return chunk is None or chunk < self.n_chunks[i]

        if self.kind == "swap":
            ssem, rsem = sems
            for i in range(n):
                if live(i):
                    remote.append(pltpu.make_async_remote_copy(
                        src_ref=rows(ins[i], i), dst_ref=rows(outs[i], i), send_sem=ssem.at[i], recv_sem=rsem.at[i],
                        device_id=(x, y, 1 - c), device_id_type=MESH))
            return local, remote
        ssem, rsem, lsem = sems
        for i in range(n):
            if not live(i):
                continue
            if self.kind == "gather":
                local.append(pltpu.make_async_copy(rows(ins[i], i), rows(outs[i].at[shard], i), lsem.at[i]))
            for k in range(1, N_SHARD):
                px, py = _flip(x, (k >> 1) & 1), _flip(y, k & 1)
                if self.kind == "gather":
                    src, dst = ins[i], outs[i].at[shard]
                else:
                    src, dst = ins[i].at[2 * px + py], outs[i].at[k - 1]
                remote.append(pltpu.make_async_remote_copy(
                    src_ref=rows(src, i), dst_ref=rows(dst, i), send_sem=ssem.at[i, k - 1], recv_sem=rsem.at[i, k - 1],
                    device_id=(px, py, c), device_id_type=MESH))
        if self.kind == "scatter":
            for i in range(len(self.owns)):
                if live(n + i):
                    local.append(pltpu.make_async_copy(rows(ins[n + i].at[shard], n + i), rows(outs[n + i], n + i),
                                                       lsem.at[i]))
        return local, remote

    def start(self, ins, outs, sems, chunk):
        local, remote = self.copies(ins, outs, sems, chunk)
        for cp in local + remote:
            cp.start()

    def wait(self, ins, outs, sems):
        local, remote = self.copies(ins, outs, sems)
        for cp in remote:
            cp.wait_recv()
        for cp in remote:
            cp.wait_send()
        for cp in local:
            cp.wait()


def _row_chunks(rows, dtype, n_steps):
    tile_rows = SUBLANES * (4 // jnp.dtype(dtype).itemsize)
    for n in range(min(n_steps, rows // tile_rows), 1, -1):
        if rows % n == 0 and (rows // n) % tile_rows == 0:
            return n
    return 1


def _hosted(body, rider, *, name, grid, out_shape, in_specs, out_specs, scratch_shapes, compiler_params, args):
    out_shape, out_specs = list(out_shape), list(out_specs)
    if rider is None:
        outs = pl.pallas_call(body, name=name, grid=grid, out_shape=tuple(out_shape), in_specs=list(in_specs),
                              out_specs=tuple(out_specs), scratch_shapes=list(scratch_shapes),
                              compiler_params=compiler_params)(*args)
        return list(outs), []
    n_in, n_out, n_scr = len(in_specs), len(out_shape), len(scratch_shapes)
    nr_in, nr_out = len(rider.inputs), len(rider.out_shapes)
    n_steps = 1
    for size in grid:
        n_steps *= size

    def full(*refs):
        ins, refs = refs[:n_in], refs[n_in:]
        r_in, refs = refs[:nr_in], refs[nr_in:]
        outs, refs = refs[:n_out], refs[n_out:]
        r_out, refs = refs[:nr_out], refs[nr_out:]
        scr, sems = refs[:n_scr], refs[n_scr:]
        if grid:
            step = 0
            for ax, size in enumerate(grid):
                step = step * size + pl.program_id(ax)
            for chunk in range(rider.plan(n_steps)):
                pl.when(step == chunk)(functools.partial(rider.start, r_in, r_out, sems, chunk))
            body(*ins, *outs, *scr)
            pl.when(step == n_steps - 1)(lambda: rider.wait(r_in, r_out, sems))
        else:
            rider.plan(1)
            rider.start(r_in, r_out, sems, 0)
            body(*ins, *outs, *scr)
            rider.wait(r_in, r_out, sems)

    hbm = pl.BlockSpec(memory_space=pl.ANY)
    res = pl.pallas_call(
        full, name=name, grid=grid, out_shape=tuple(out_shape + rider.out_shapes),
        in_specs=list(in_specs) + [hbm] * nr_in, out_specs=tuple(out_specs + [hbm] * nr_out),
        scratch_shapes=list(scratch_shapes) + rider.sems, compiler_params=compiler_params,
    )(*args, *rider.inputs)
    return list(res[:n_out]), list(res[n_out:])


def _ride_alone(rider, name):
    return _hosted(lambda: None, rider, name=name, grid=(), out_shape=[], in_specs=[], out_specs=[], scratch_shapes=[],
                   compiler_params=_cparams(), args=[])[1]


def _allreduce_small(packed, rider=None):
    r = packed.shape[0]

    def body(p_ref, sum_ref, all_ref, ssem, rsem):
        x, y, c = _my_pos()
        me = 4 * x + 2 * y + c
        all_ref[me] = p_ref[...]
        cps = []
        for k in range(1, N_DEV):
            peer = (_flip(x, (k >> 2) & 1), _flip(y, (k >> 1) & 1), _flip(c, k & 1))
            cps.append(pltpu.make_async_remote_copy(
                src_ref=all_ref.at[me], dst_ref=all_ref.at[me], send_sem=ssem.at[k - 1], recv_sem=rsem.at[k - 1],
                device_id=peer, device_id_type=MESH))
        for cp in cps:
            cp.start()
        for cp in cps:
            cp.wait_recv()
        acc = all_ref[0]
        for dev in range(1, N_DEV):
            acc = acc + all_ref[dev]
        sum_ref[...] = acc
        for cp in cps:
            cp.wait_send()

    vm = pl.BlockSpec(memory_space=pltpu.VMEM)
    return _hosted(
        body, rider, name="allreduce_small", grid=(),
        out_shape=[jax.ShapeDtypeStruct((r, LANES), F32), jax.ShapeDtypeStruct((N_DEV, r, LANES), F32)],
        in_specs=[vm], out_specs=[vm, vm],
        scratch_shapes=[pltpu.SemaphoreType.DMA((N_DEV - 1,)), pltpu.SemaphoreType.DMA((N_DEV - 1,))],
        compiler_params=_cparams(), args=[packed])


def _rope_rot(t):
    w = t.shape[1]
    lane = lax.broadcasted_iota(jnp.int32, t.shape, 1)
    first = (lane % HEAD_DIM) < (HEAD_DIM // 2)
    return jnp.where(first, pltpu.roll(t, w - HEAD_DIM // 2, 1), pltpu.roll(t, HEAD_DIM // 2, 1))


def _in_proj(x, mod3, g_attn, w_in_t, cos_t, sin_t, seq, rider=None):
    t, d = x.shape
    tm = TOKEN_TILE
    per_seq = seq // tm
    rope_lo, rope_hi = 3 * NA_WIDTH, 3 * NA_WIDTH + SW_WIDTH + SW_KV_WIDTH
    n_rep = (rope_hi - rope_lo) // LANES

    def body(x_ref, mod_ref, g_ref, w_ref, cos_ref, sin_ref, h_ref, p_ref):
        r, xn = _rms_stats(x_ref[...])
        shift, scale = mod_ref[0, :, 0:d], mod_ref[0, :, d:2 * d]
        hb = ((xn * g_ref[...]) * (1.0 + scale) + shift).astype(BF16)
        h_ref[...] = hb
        p_ref[:, :rope_lo] = _mm_nt(hb, w_ref[:rope_lo, :]).astype(BF16)
        pr = _mm_nt(hb, w_ref[rope_lo:rope_hi, :])
        cos = jnp.concatenate([cos_ref[...]] * n_rep, axis=1)
        sin = jnp.concatenate([sin_ref[...]] * n_rep, axis=1)
        p_ref[:, rope_lo:rope_hi] = (pr * cos + _rope_rot(pr) * sin).astype(BF16)
        p_ref[:, rope_hi:] = _mm_nt(hb, w_ref[rope_hi:, :]).astype(BF16)

    return _hosted(
        body, rider, name="in_proj", grid=(t // tm,),
        out_shape=[jax.ShapeDtypeStruct((t, d), BF16), jax.ShapeDtypeStruct((t, IN_WIDTH), BF16)],
        in_specs=[pl.BlockSpec((tm, d), lambda i: (i, 0)),
                  pl.BlockSpec((1, 1, 6 * d), lambda i: (i // per_seq, 0, 0)),
                  pl.BlockSpec((1, d), lambda i: (0, 0)),
                  pl.BlockSpec((IN_WIDTH, d), lambda i: (0, 0)),
                  pl.BlockSpec((tm, LANES), lambda i: (i % per_seq, 0)),
                  pl.BlockSpec((tm, LANES), lambda i: (i % per_seq, 0))],
        out_specs=[pl.BlockSpec((tm, d), lambda i: (i, 0)), pl.BlockSpec((tm, IN_WIDTH), lambda i: (i, 0))],
        scratch_shapes=[], compiler_params=_cparams(("arbitrary",), VMEM_BIG),
        args=[x, mod3, g_attn, w_in_t, cos_t, sin_t])


def _na_bias_pattern():
    n_dc = 2 * NA_COLS - 1
    j = np.arange(GRID_W)[:, None]
    m = np.arange(GRID_W * LANES)[None, :]
    q, lane = m // LANES, m % LANES
    k = lane % GRID_W
    cs = np.clip(q - NA_COLS // 2, 0, GRID_W - NA_COLS)
    ok = (k >= cs) & (k < cs + NA_COLS)
    hit = ok & (j < 2 * n_dc) & (lane // GRID_W == j // n_dc) & (k - q + (NA_COLS - 1) == j % n_dc)
    return jnp.asarray(hit.astype(np.float32)), jnp.asarray(np.where(ok, 0.0, NEG).astype(np.float32))


def _na_bias_tiles(rows2, expand, mask):
    n, width = rows2.shape[0], expand.shape[1]
    q_step = 16
    step = q_step * LANES

    def body(r_ref, e_ref, m_ref, o_ref):
        flat = jnp.dot(r_ref[...], e_ref[...], precision=lax.Precision.HIGHEST,
                       preferred_element_type=F32) + m_ref[...]
        for qq in range(q_step):
            o_ref[:, qq, :] = flat[:, qq * LANES:(qq + 1) * LANES]

    return pl.pallas_call(
        body, name="na_bias_tiles", grid=(width // step,),
        out_shape=jax.ShapeDtypeStruct((n, GRID_W, LANES), F32),
        in_specs=[pl.BlockSpec(rows2.shape, lambda i: (0, 0)), pl.BlockSpec((expand.shape[0], step), lambda i: (0, i)),
                  pl.BlockSpec((1, step), lambda i: (0, i))],
        out_specs=pl.BlockSpec((n, q_step, LANES), lambda i: (0, i, 0)),
        compiler_params=_cparams(("arbitrary",)),
    )(rows2, expand, mask)


def _na_prepare(k_ref, v_ref, km, vm):
    lane = lax.broadcasted_iota(jnp.int32, k_ref.shape, 1)
    low = lane < HEAD_DIM
    kv = k_ref[...]
    vv = v_ref[...]
    zero = jnp.zeros_like(kv)
    km[0] = jnp.where(low, kv, zero)
    km[1] = jnp.where(low, zero, kv)
    vm[0] = jnp.where(low, vv, zero)
    vm[1] = jnp.where(low, zero, vv)


def _na_window(r, n_rows):
    rs = jnp.clip(r - NA_ROWS // 2, 0, n_rows - NA_ROWS)
    return rs, r - rs


def _na_pair_window(ref, wrows):
    return jnp.concatenate([ref[0, wrows, :], ref[1, wrows, :]], axis=0)


def _na_scores(q, k2, tp_ref, off):
    bias = jnp.concatenate([tp_ref[h, 2 * w - off + (NA_ROWS - 1)] for h in range(2) for w in range(NA_ROWS // 2)],
                           axis=1)
    return _mm_nt(q, k2) * QK_SCALE + bias


def _pair_softmax(s):
    win = s.shape[1] // 2
    halves = []
    for h in range(2):
        sh = s[:, h * win:(h + 1) * win]
        e = jnp.exp(sh - jnp.max(sh, axis=-1, keepdims=True))
        halves.append(e / jnp.sum(e, axis=-1, keepdims=True))
    return jnp.concatenate(halves, axis=1)


def _na_forward(proj, tiles, batch, seq, rider=None):
    t = proj.shape[0]
    n_rows = seq // GRID_W
    n_pairs = NA_WIDTH // LANES
    win = NA_ROWS * GRID_W

    def body(q_ref, k_ref, v_ref, tp_ref, o_ref, km, vm):
        _na_prepare(k_ref, v_ref, km, vm)

        def scores(r):
            rs, off = _na_window(r, n_rows)
            rows = pl.ds(pl.multiple_of(r * GRID_W, GRID_W), GRID_W)
            wrows = pl.ds(pl.multiple_of(rs * GRID_W, GRID_W), win)
            return rows, wrows, _na_scores(q_ref[rows, :], _na_pair_window(km, wrows), tp_ref, off)

        def finish(rows, wrows, s):
            o_ref[rows, :] = _mm(_pair_softmax(s).astype(BF16), _na_pair_window(vm, wrows))

        def row_group(i, carry):
            for state in [scores(NA_GROUP * i + j) for j in range(NA_GROUP)]:
                finish(*state)
            return carry

        lax.fori_loop(0, n_rows // NA_GROUP, row_group, 0)

    return _hosted(
        body, rider, name="na_forward", grid=(batch, n_pairs),
        out_shape=[jax.ShapeDtypeStruct((t, NA_WIDTH), F32)],
        in_specs=[pl.BlockSpec((seq, LANES), lambda b, p: (b, p)),
                  pl.BlockSpec((seq, LANES), lambda b, p: (b, n_pairs + p)),
                  pl.BlockSpec((seq, LANES), lambda b, p: (b, 2 * n_pairs + p)),
                  pl.BlockSpec((2, 2 * NA_ROWS - 2, GRID_W, LANES), lambda b, p: (p, 0, 0, 0))],
        out_specs=[pl.BlockSpec((seq, LANES), lambda b, p: (b, p))],
        scratch_shapes=[pltpu.VMEM((2, seq, LANES), BF16), pltpu.VMEM((2, seq, LANES), BF16)],
        compiler_params=_cparams(("arbitrary", "arbitrary")), args=[proj, proj, proj, tiles])


def _sw_prepare(kv_ref, g, dst_lo, dst_hi, seq):
    lane = lax.broadcasted_iota(jnp.int32, kv_ref.shape, 1)
    mine = (lane // HEAD_DIM) == g
    kg = jnp.where(mine, kv_ref[...].astype(F32), 0.0)
    kr = pltpu.roll(kg, HEAD_DIM, 1)
    first = g == 0
    zero = jnp.zeros((SW_BLOCK, LANES), BF16)
    for dst, val in ((dst_lo, jnp.where(first, kg, kr)), (dst_hi, jnp.where(first, kr, kg))):
        dst[0:SW_BLOCK, :] = zero
        dst[SW_BLOCK:SW_BLOCK + seq, :] = val.astype(BF16)
        dst[SW_BLOCK + seq:, :] = zero


def _sw_mask(n, seq):
    qi = lax.broadcasted_iota(jnp.int32, (SW_BLOCK, 3 * SW_BLOCK), 0)
    kj = lax.broadcasted_iota(jnp.int32, (SW_BLOCK, 3 * SW_BLOCK), 1)
    kpos = n * SW_BLOCK - SW_BLOCK + kj
    return (jnp.abs(qi + SW_BLOCK - kj) <= SW_BLOCK) & (kpos >= 0) & (kpos < seq)


def _sw_probs(s2, ok, sinks):
    band = s2.shape[1] // 2
    halves, sink_p = [], []
    for i in range(2):
        s = jnp.where(ok, s2[:, i * band:(i + 1) * band], NEG)
        m = jnp.maximum(jnp.max(s, axis=-1, keepdims=True), sinks[i])
        p = jnp.exp(s - m)
        es = jnp.exp(sinks[i] - m)
        den = jnp.sum(p, axis=-1, keepdims=True) + es
        halves.append(p / den)
        sink_p.append(es / den)
    return jnp.concatenate(halves, axis=1), sink_p


def _sw_forward(proj, sink, batch, seq, rider=None):
    t = proj.shape[0]
    n_pairs = SW_WIDTH // LANES
    q_blk = 3 * NA_WIDTH // LANES
    k_blk = q_blk + n_pairs
    n_blocks = seq // SW_BLOCK
    pad = seq + 2 * SW_BLOCK

    def body(sink_ref, q_ref, k_ref, v_ref, o_ref, k_lo, k_hi, v_lo, v_hi):
        hp = pl.program_id(1)
        g = hp // 2
        _sw_prepare(k_ref, g, k_lo, k_hi, seq)
        _sw_prepare(v_ref, g, v_lo, v_hi, seq)

        sinks = (sink_ref[2 * hp], sink_ref[2 * hp + 1])

        def scores(n):
            rows = pl.ds(pl.multiple_of(n * SW_BLOCK, SW_BLOCK), SW_BLOCK)
            wrows = pl.ds(pl.multiple_of(n * SW_BLOCK, SW_BLOCK), 3 * SW_BLOCK)
            k2 = jnp.concatenate([k_lo[wrows, :], k_hi[wrows, :]], axis=0)
            return n, rows, wrows, _mm_nt(q_ref[rows, :], k2) * QK_SCALE

        def finish(n, rows, wrows, s2):
            p, _ = _sw_probs(s2, _sw_mask(n, seq), sinks)
            v2 = jnp.concatenate([v_lo[wrows, :], v_hi[wrows, :]], axis=0)
            o_ref[rows, :] = _mm(p.astype(BF16), v2)

        def block_group(i, carry):
            for state in [scores(SW_GROUP_BLOCKS * i + j) for j in range(SW_GROUP_BLOCKS)]:
                finish(*state)
            return carry

        lax.fori_loop(0, n_blocks // SW_GROUP_BLOCKS, block_group, 0)

    return _hosted(
        body, rider, name="sw_forward", grid=(batch, n_pairs),
        out_shape=[jax.ShapeDtypeStruct((t, SW_WIDTH), F32)],
        in_specs=[pl.BlockSpec(memory_space=pltpu.SMEM),
                  pl.BlockSpec((seq, LANES), lambda b, p: (b, q_blk + p)),
                  pl.BlockSpec((seq, LANES), lambda b, p: (b, k_blk)),
                  pl.BlockSpec((seq, LANES), lambda b, p: (b, k_blk + 1))],
        out_specs=[pl.BlockSpec((seq, LANES), lambda b, p: (b, p))],
        scratch_shapes=[pltpu.VMEM((pad, LANES), BF16)] * 4,
        compiler_params=_cparams(("arbitrary", "arbitrary")), args=[sink, proj, proj, proj])


def _out_proj(oa, ob, g_na, g_sw, w_out, x, mod3, g_ffn, seq):
    t, d = x.shape
    tm = TOKEN_TILE
    per_seq = seq // tm

    def body(oa_ref, ob_ref, gna_ref, gsw_ref, w_ref, x_ref, mod_ref, gf_ref, oab_ref, mix_ref, x1_ref, h2_ref):
        _, na = _rms_stats(oa_ref[...])
        _, nb = _rms_stats(ob_ref[...])
        oab = jnp.concatenate([na * gna_ref[...], nb * gsw_ref[...]], axis=1).astype(BF16)
        oab_ref[...] = oab
        mix = _mm(oab, w_ref[...])
        mix_ref[...] = mix
        gate_a = mod_ref[0, :, 2 * d:3 * d]
        shift_f, scale_f = mod_ref[0, :, 3 * d:4 * d], mod_ref[0, :, 4 * d:5 * d]
        x1 = x_ref[...] + gate_a * mix
        x1_ref[...] = x1
        _, xn = _rms_stats(x1)
        h2_ref[...] = ((xn * gf_ref[...]) * (1.0 + scale_f) + shift_f).astype(BF16)

    tile = lambda w: pl.BlockSpec((tm, w), lambda i: (i, 0))
    vec = lambda w: pl.BlockSpec((1, w), lambda i: (0, 0))
    return pl.pallas_call(
        body, name="out_proj", grid=(t // tm,),
        out_shape=(jax.ShapeDtypeStruct((t, d), BF16), jax.ShapeDtypeStruct((t, d), F32),
                   jax.ShapeDtypeStruct((t, d), F32), jax.ShapeDtypeStruct((t, d), BF16)),
        in_specs=[tile(NA_WIDTH), tile(SW_WIDTH), vec(NA_WIDTH), vec(SW_WIDTH),
                  pl.BlockSpec((d, d), lambda i: (0, 0)), tile(d),
                  pl.BlockSpec((1, 1, 6 * d), lambda i: (i // per_seq, 0, 0)), vec(d)],
        out_specs=(tile(d), tile(d), tile(d), tile(d)),
        compiler_params=_cparams(("arbitrary",), VMEM_BIG),
    )(oa, ob, g_na, g_sw, w_out, x, mod3, g_ffn)


def _up_proj(h2, w_up_halves, rider=None):
    t, d = h2.shape
    tm = TOKEN_TILE
    w_a, w_b = w_up_halves
    half, wcol = w_a.shape[1], w_a.shape[2]

    def body(h_ref, wa_ref, wb_ref, u_ref):
        u_ref[0] = _mm(h_ref[:, :half], wa_ref[0]) + _mm(h_ref[:, half:], wb_ref[0])

    w_spec = pl.BlockSpec((1, half, wcol), lambda j, i: (j, 0, 0))
    return _hosted(
        body, rider, name="up_proj", grid=(N_SHARD, t // tm),
        out_shape=[jax.ShapeDtypeStruct((2, t, D_FF), F32)],
        in_specs=[pl.BlockSpec((tm, d), lambda j, i: (i, 0)), w_spec, w_spec],
        out_specs=[pl.BlockSpec((1, tm, wcol), lambda j, i: (j // 2, i, j % 2))],
        scratch_shapes=[], compiler_params=_cparams(("arbitrary", "arbitrary"), VMEM_BIG), args=[h2, w_a, w_b])


def _taps_chunk(load, s, rows, seq):
    cur = load(s, rows)
    above = load(pl.multiple_of(jnp.maximum(s - SUBLANES, 0), SUBLANES), SUBLANES)
    below = load(pl.multiple_of(jnp.minimum(s + rows, seq - SUBLANES), SUBLANES), SUBLANES)
    up = jnp.where(s > 0, above[SUBLANES - 1:SUBLANES, :], 0.0)
    dn = jnp.where(s + rows < seq, below[0:1, :], 0.0)
    row = lax.broadcasted_iota(jnp.int32, cur.shape, 0)
    prev = jnp.where(row == 0, up, pltpu.roll(cur, 1, 0))
    nxt = jnp.where(row == rows - 1, dn, pltpu.roll(cur, rows - 1, 0))
    return cur, prev, nxt


def _conv_gate(u, conv_w, conv_b, batch, seq):
    t = u.shape[1]
    cw = FF_TILE
    rows = CONV_CHUNK

    def body(u_ref, w_ref, b_ref, a_ref):
        def chunk(i, carry):
            s = pl.multiple_of(i * rows, rows)
            gt, prev, nxt = _taps_chunk(lambda at, n: u_ref[1, pl.ds(at, n), :], s, rows, seq)
            gc = prev * w_ref[0:1, :] + gt * w_ref[1:2, :] + nxt * w_ref[2:3, :] + b_ref[...]
            a_ref[pl.ds(s, rows), :] = ((gc * _sigmoid(gc)) * u_ref[0, pl.ds(s, rows), :]).astype(BF16)
            return carry

        lax.fori_loop(0, seq // rows, chunk, 0)

    return pl.pallas_call(
        body, name="conv_gate", grid=(batch, D_FF // cw),
        out_shape=jax.ShapeDtypeStruct((t, D_FF), BF16),
        in_specs=[pl.BlockSpec((2, seq, cw), lambda b, j: (0, b, j)),
                  pl.BlockSpec((3, cw), lambda b, j: (0, j)), pl.BlockSpec((1, cw), lambda b, j: (0, j))],
        out_specs=pl.BlockSpec((seq, cw), lambda b, j: (b, j)),
        compiler_params=_cparams(("arbitrary", "arbitrary"), VMEM_BIG),
    )(u, conv_w, conv_b)


def _down_and_loss(a, w_down, x1, mod3, g_final, target, seq):
    t, d = x1.shape
    tm = TOKEN_TILE
    per_seq = seq // tm
    batch = t // seq

    def body(a_ref, w_ref, x1_ref, mod_ref, g_ref, tgt_ref, dx2_ref, dffn_ref, loss_ref, dgate_ref, dg_ref):
        i = pl.program_id(0)
        f = _mm(a_ref[...], w_ref[...])
        gate_f = mod_ref[0, :, 5 * d:6 * d]
        x2 = x1_ref[...] + gate_f * f
        r, xn = _rms_stats(x2)
        err = xn * g_ref[...] - tgt_ref[...]
        part = 0.5 * jnp.sum(jnp.mean(err * err, axis=-1, keepdims=True))
        dy = err / d
        dx2 = _rms_bwd(dy * g_ref[...], xn, r)
        dx2_ref[...] = dx2
        dffn_ref[...] = (dx2 * gate_f).astype(BF16)

        @pl.when(i == 0)
        def _():
            loss_ref[...] = jnp.zeros_like(loss_ref)
            dg_ref[...] = jnp.zeros_like(dg_ref)

        @pl.when(i % per_seq == 0)
        def _():
            dgate_ref[...] = jnp.zeros_like(dgate_ref)

        loss_ref[...] += part
        dg_ref[...] += jnp.sum(dy * xn, axis=0, keepdims=True)
        dgate_ref[0] += jnp.sum(dx2 * f, axis=0, keepdims=True)

    tile = lambda w: pl.BlockSpec((tm, w), lambda i: (i, 0))
    return pl.pallas_call(
        body, name="down_loss", grid=(t // tm,),
        out_shape=(jax.ShapeDtypeStruct((t, d), F32), jax.ShapeDtypeStruct((t, d), BF16),
                   jax.ShapeDtypeStruct((SUBLANES, LANES), F32), jax.ShapeDtypeStruct((batch, 1, d), F32),
                   jax.ShapeDtypeStruct((1, d), F32)),
        in_specs=[tile(D_FF), pl.BlockSpec((D_FF, d), lambda i: (0, 0)), tile(d),
                  pl.BlockSpec((1, 1, 6 * d), lambda i: (i // per_seq, 0, 0)),
                  pl.BlockSpec((1, d), lambda i: (0, 0)), tile(d)],
        out_specs=(tile(d), tile(d), pl.BlockSpec((SUBLANES, LANES), lambda i: (0, 0)),
                   pl.BlockSpec((1, 1, d), lambda i: (i // per_seq, 0, 0)), pl.BlockSpec((1, d), lambda i: (0, 0))),
        compiler_params=_cparams(("arbitrary",), VMEM_BIG),
    )(a, w_down, x1, mod3, g_final, target)


def _down_weight_grad(a, dffn):
    t, dff = a.shape
    d = dffn.shape[1]
    tk = TOKEN_TILE
    n_k = t // tk

    def body(a_ref, df_ref, g_ref, gb_ref):
        k = pl.program_id(0)

        @pl.when(k == 0)
        def _():
            g_ref[...] = jnp.zeros_like(g_ref)

        g_ref[...] += _mm_tn(a_ref[...], df_ref[...])

        @pl.when(k == n_k - 1)
        def _():
            gb_ref[...] = g_ref[...].astype(BF16)

    whole = pl.BlockSpec((dff, d), lambda k: (0, 0))
    return pl.pallas_call(
        body, name="down_weight_grad", grid=(n_k,),
        out_shape=(jax.ShapeDtypeStruct((dff, d), F32), jax.ShapeDtypeStruct((dff, d), BF16)),
        in_specs=[pl.BlockSpec((tk, dff), lambda k: (k, 0)), pl.BlockSpec((tk, d), lambda k: (k, 0))],
        out_specs=(whole, whole),
        compiler_params=_cparams(("arbitrary",), VMEM_BIG),
    )(a, dffn)


def _ffn_backward(dffn, w_down, u, conv_w, conv_b, batch, seq, rider=None):
    t, d = dffn.shape
    cw = FF_TILE
    rows = CONV_CHUNK

    def body(df_ref, wd_ref, u_ref, w_ref, b_ref, du_ref, gcw_ref, gcb_ref, da_scr, dgc_scr):
        b = pl.program_id(1)
        da_scr[...] = _mm_nt(df_ref[...], wd_ref[...])

        @pl.when(b == 0)
        def _():
            gcw_ref[...] = jnp.zeros_like(gcw_ref)
            gcb_ref[...] = jnp.zeros_like(gcb_ref)

        def fold(v):
            return jnp.sum(v.reshape(rows // SUBLANES, SUBLANES, cw), axis=0)

        def chunk(i, carry):
            s = pl.multiple_of(i * rows, rows)
            here = pl.ds(s, rows)
            gt, prev, nxt = _taps_chunk(lambda at, n: u_ref[1, pl.ds(at, n), :], s, rows, seq)
            val, da = u_ref[0, here, :], da_scr[here, :]
            gc = prev * w_ref[0:1, :] + gt * w_ref[1:2, :] + nxt * w_ref[2:3, :] + b_ref[...]
            sg = _sigmoid(gc)
            sl = gc * sg
            du_ref[0, here, :] = (da * sl).astype(BF16)
            dgc = (da * val) * (sg * (1.0 + gc * (1.0 - sg)))
            dgc_scr[here, :] = dgc
            cb, c0, c1, c2 = carry
            return cb + fold(dgc), c0 + fold(dgc * prev), c1 + fold(dgc * gt), c2 + fold(dgc * nxt)

        zero = jnp.zeros((SUBLANES, cw), F32)
        cb, c0, c1, c2 = lax.fori_loop(0, seq // rows, chunk, (zero, zero, zero, zero))
        gcb_ref[...] += jnp.sum(cb, axis=0, keepdims=True)
        gcw_ref[0:1, :] += jnp.sum(c0, axis=0, keepdims=True)
        gcw_ref[1:2, :] += jnp.sum(c1, axis=0, keepdims=True)
        gcw_ref[2:3, :] += jnp.sum(c2, axis=0, keepdims=True)

        def chunk2(i, carry):
            s = pl.multiple_of(i * rows, rows)
            dgc, dprev, dnxt = _taps_chunk(lambda at, n: dgc_scr[pl.ds(at, n), :], s, rows, seq)
            du_ref[1, pl.ds(s, rows), :] = (dnxt * w_ref[0:1, :] + dgc * w_ref[1:2, :]
                                            + dprev * w_ref[2:3, :]).astype(BF16)
            return carry

        lax.fori_loop(0, seq // rows, chunk2, 0)

    return _hosted(
        body, rider, name="ffn_backward", grid=(D_FF // cw, batch),
        out_shape=[jax.ShapeDtypeStruct((2, t, D_FF), BF16),
                   jax.ShapeDtypeStruct((3, D_FF), F32), jax.ShapeDtypeStruct((1, D_FF), F32)],
        in_specs=[pl.BlockSpec((seq, d), lambda j, b: (b, 0)), pl.BlockSpec((cw, d), lambda j, b: (j, 0)),
                  pl.BlockSpec((2, seq, cw), lambda j, b: (0, b, j)),
                  pl.BlockSpec((3, cw), lambda j, b: (0, j)), pl.BlockSpec((1, cw), lambda j, b: (0, j))],
        out_specs=[pl.BlockSpec((2, seq, cw), lambda j, b: (0, b, j)),
                   pl.BlockSpec((3, cw), lambda j, b: (0, j)), pl.BlockSpec((1, cw), lambda j, b: (0, j))],
        scratch_shapes=[pltpu.VMEM((seq, cw), F32), pltpu.VMEM((seq, cw), F32)],
        compiler_params=_cparams(("arbitrary", "arbitrary"), VMEM_BIG), args=[dffn, w_down, u, conv_w, conv_b])


def _up_backward(du, w_up, x1, mod3, g_ffn, dx2, mix, seq, rider=None):
    _, t, _ = du.shape
    d = x1.shape[1]
    tm = TOKEN_TILE // 2
    per_seq = seq // tm
    batch = t // seq
    w_a, w_b = w_up
    half, wcol = w_a.shape[1], w_a.shape[2]

    def body(du_ref, wa_ref, wb_ref, x1_ref, mod_ref, g_ref, dx2_ref, mix_ref,
             dx1_ref, dmix_ref, dsh_ref, dsc_ref, dga_ref, dg_ref):
        i = pl.program_id(0)
        parts = []
        for w_ref in (wa_ref, wb_ref):
            acc = jnp.zeros((tm, half), F32)
            for j in range(N_SHARD):
                acc = acc + _mm_nt(du_ref[j // 2, :, (j % 2) * wcol:(j % 2 + 1) * wcol], w_ref[j])
            parts.append(acc)
        dh = jnp.concatenate(parts, axis=1)
        gate_a = mod_ref[0, :, 2 * d:3 * d]
        scale_f = mod_ref[0, :, 4 * d:5 * d]
        r, xn = _rms_stats(x1_ref[...])
        xg = xn * g_ref[...]
        dxg = dh * (1.0 + scale_f)
        dx1 = dx2_ref[...] + _rms_bwd(dxg * g_ref[...], xn, r)
        dx1_ref[...] = dx1
        dmix_ref[...] = (dx1 * gate_a).astype(BF16)

        @pl.when(i == 0)
        def _():
            dg_ref[...] = jnp.zeros_like(dg_ref)

        @pl.when(i % per_seq == 0)
        def _():
            dsh_ref[...] = jnp.zeros_like(dsh_ref)
            dsc_ref[...] = jnp.zeros_like(dsc_ref)
            dga_ref[...] = jnp.zeros_like(dga_ref)

        dg_ref[...] += jnp.sum(dxg * xn, axis=0, keepdims=True)
        dsh_ref[0] += jnp.sum(dh, axis=0, keepdims=True)
        dsc_ref[0] += jnp.sum(dh * xg, axis=0, keepdims=True)
        dga_ref[0] += jnp.sum(dx1 * mix_ref[...], axis=0, keepdims=True)

    tile = lambda w: pl.BlockSpec((tm, w), lambda i: (i, 0))
    per_b = pl.BlockSpec((1, 1, d), lambda i: (i // per_seq, 0, 0))
    small = jax.ShapeDtypeStruct((batch, 1, d), F32)
    return _hosted(
        body, rider, name="up_backward", grid=(t // tm,),
        out_shape=[jax.ShapeDtypeStruct((t, d), F32), jax.ShapeDtypeStruct((t, d), BF16), small, small, small,
                   jax.ShapeDtypeStruct((1, d), F32)],
        in_specs=[pl.BlockSpec((2, tm, D_FF), lambda i: (0, i, 0)),
                  pl.BlockSpec((N_SHARD, half, wcol), lambda i: (0, 0, 0)),
                  pl.BlockSpec((N_SHARD, half, wcol), lambda i: (0, 0, 0)), tile(d),
                  pl.BlockSpec((1, 1, 6 * d), lambda i: (i // per_seq, 0, 0)),
                  pl.BlockSpec((1, d), lambda i: (0, 0)), tile(d), tile(d)],
        out_specs=[tile(d), tile(d), per_b, per_b, per_b, pl.BlockSpec((1, d), lambda i: (0, 0))],
        scratch_shapes=[], compiler_params=_cparams(("arbitrary",), VMEM_BIG),
        args=[du, w_a, w_b, x1, mod3, g_ffn, dx2, mix])


def _up_weight_grad(h2, du, rider=None):
    t, d = h2.shape
    tk = TOKEN_TILE
    wcol = D_FF // 2
    half = d // 2
    n_k = t // tk

    def body(h_ref, du_ref, ga_ref, gb_ref, ga16_ref, gb16_ref):
        k = pl.program_id(1)

        @pl.when(k == 0)
        def _():
            ga_ref[...] = jnp.zeros_like(ga_ref)
            gb_ref[...] = jnp.zeros_like(gb_ref)

        du = du_ref[0]
        ga_ref[0] += _mm_tn(h_ref[:, :half], du)
        gb_ref[0] += _mm_tn(h_ref[:, half:], du)

        @pl.when(k == n_k - 1)
        def _():
            ga16_ref[...] = ga_ref[...].astype(BF16)
            gb16_ref[...] = gb_ref[...].astype(BF16)

    g_spec = pl.BlockSpec((1, half, wcol), lambda j, k: (j, 0, 0))
    f32_out = jax.ShapeDtypeStruct((N_SHARD, half, wcol), F32)
    b16_out = jax.ShapeDtypeStruct((N_SHARD, half, wcol), BF16)
    return _hosted(
        body, rider, name="up_weight_grad", grid=(N_SHARD, n_k),
        out_shape=[f32_out, f32_out, b16_out, b16_out],
        in_specs=[pl.BlockSpec((tk, d), lambda j, k: (k, 0)),
                  pl.BlockSpec((1, tk, wcol), lambda j, k: (j // 2, k, j % 2))],
        out_specs=[g_spec, g_spec, g_spec, g_spec], scratch_shapes=[],
        compiler_params=_cparams(("arbitrary", "arbitrary"), VMEM_BIG), args=[h2, du])


def _out_backward(dmix, w_out, oab, oa, ob, g_na, g_sw):
    t, d = dmix.shape
    tm = TOKEN_TILE
    hw = NA_WIDTH

    def body(dm_ref, w_ref, oab_ref, oa_ref, ob_ref, gna_ref, gsw_ref,
             doa_ref, dob_ref, gw_ref, gwb_ref, dgna_ref, dgsw_ref):
        @pl.when(pl.program_id(0) == 0)
        def _():
            gw_ref[...] = jnp.zeros_like(gw_ref)
            dgna_ref[...] = jnp.zeros_like(dgna_ref)
            dgsw_ref[...] = jnp.zeros_like(dgsw_ref)

        dm = dm_ref[...]
        gw_ref[...] += _mm_tn(oab_ref[...], dm)

        @pl.when(pl.program_id(0) == t // tm - 1)
        def _():
            gwb_ref[...] = gw_ref[...].astype(BF16)

        do = _mm_nt(dm, w_ref[...])
        for raw_ref, g_ref, dst_ref, dg_ref, lo in ((oa_ref, gna_ref, doa_ref, dgna_ref, 0),
                                                     (ob_ref, gsw_ref, dob_ref, dgsw_ref, hw)):
            r, xn = _rms_stats(raw_ref[...])
            dpart = do[:, lo:lo + hw]
            dg_ref[...] += jnp.sum(dpart * xn, axis=0, keepdims=True)
            dst_ref[...] = _rms_bwd(dpart * g_ref[...], xn, r).astype(BF16)

    tile = lambda w: pl.BlockSpec((tm, w), lambda i: (i, 0))
    vec = lambda w: pl.BlockSpec((1, w), lambda i: (0, 0))
    return pl.pallas_call(
        body, name="out_backward", grid=(t // tm,),
        out_shape=(jax.ShapeDtypeStruct((t, hw), BF16), jax.ShapeDtypeStruct((t, hw), BF16),
                   jax.ShapeDtypeStruct((d, d), F32), jax.ShapeDtypeStruct((d, d), BF16),
                   jax.ShapeDtypeStruct((1, hw), F32), jax.ShapeDtypeStruct((1, hw), F32)),
        in_specs=[tile(d), pl.BlockSpec((d, d), lambda i: (0, 0)), tile(d), tile(hw), tile(hw), vec(hw), vec(hw)],
        out_specs=(tile(hw), tile(hw), pl.BlockSpec((d, d), lambda i: (0, 0)), pl.BlockSpec((d, d), lambda i: (0, 0)),
                   vec(hw), vec(hw)),
        compiler_params=_cparams(("arbitrary",), VMEM_BIG),
    )(dmix, w_out, oab, oa, ob, g_na, g_sw)


def _na_backward(proj, d_o, tiles, batch, seq, rider=None):
    t = proj.shape[0]
    n_rows = seq // GRID_W
    n_pairs = NA_WIDTH // LANES
    win = NA_ROWS * GRID_W
    n_tiles = 2 * NA_ROWS - 2

    def body(q_ref, k_ref, v_ref, do_ref, tp_ref, dq_ref, dk_ref, dv_ref, dtp_ref, km, vm, dk_acc, dv_acc):
        @pl.when(pl.program_id(1) == 0)
        def _():
            dtp_ref[...] = jnp.zeros_like(dtp_ref)

        _na_prepare(k_ref, v_ref, km, vm)
        dk_acc[...] = jnp.zeros_like(dk_acc)
        dv_acc[...] = jnp.zeros_like(dv_acc)
        low = lax.broadcasted_iota(jnp.int32, (win, LANES), 1) < HEAD_DIM

        def scores(r):
            rs, off = _na_window(r, n_rows)
            rows = pl.ds(pl.multiple_of(r * GRID_W, GRID_W), GRID_W)
            wrows = pl.ds(pl.multiple_of(rs * GRID_W, GRID_W), win)
            q, do = q_ref[rows, :], do_ref[rows, :]
            k2 = _na_pair_window(km, wrows)
            s = _na_scores(q, k2, tp_ref, off)
            dp = _mm_nt(do, _na_pair_window(vm, wrows))
            return rows, wrows, off, q, do, k2, s, dp

        def finish(rows, wrows, off, q, do, k2, s, dp):
            p = _pair_softmax(s)
            parts = []
            for h in range(2):
                ph, dph = p[:, h * win:(h + 1) * win], dp[:, h * win:(h + 1) * win]
                dsh = ph * (dph - jnp.sum(ph * dph, axis=-1, keepdims=True))
                for w in range(NA_ROWS // 2):
                    dtp_ref[h, 2 * w - off + (NA_ROWS - 1)] += dsh[:, w * LANES:(w + 1) * LANES]
                parts.append(dsh)
            dsb = (jnp.concatenate(parts, axis=1) * QK_SCALE).astype(BF16)
            dq_ref[rows, :] = _mm(dsb, k2).astype(BF16)
            dk2 = _mm_tn(dsb, q)
            dv2 = _mm_tn(p.astype(BF16), do)
            dk_acc[wrows, :] += jnp.where(low, dk2[:win], dk2[win:])
            dv_acc[wrows, :] += jnp.where(low, dv2[:win], dv2[win:])

        def row_group(i, carry):
            for state in [scores(NA_GROUP * i + j) for j in range(NA_GROUP)]:
                finish(*state)
            return carry

        lax.fori_loop(0, n_rows // NA_GROUP, row_group, 0)
        dk_ref[...] = dk_acc[...].astype(BF16)
        dv_ref[...] = dv_acc[...].astype(BF16)

    blk = lambda off: pl.BlockSpec((seq, LANES), lambda p, b: (b, off + p))
    out = jax.ShapeDtypeStruct((t, NA_WIDTH), BF16)
    return _hosted(
        body, rider, name="na_backward", grid=(n_pairs, batch),
        out_shape=[out, out, out, jax.ShapeDtypeStruct(tiles.shape, F32)],
        in_specs=[blk(0), blk(n_pairs), blk(2 * n_pairs), blk(0),
                  pl.BlockSpec((2, n_tiles, GRID_W, LANES), lambda p, b: (p, 0, 0, 0))],
        out_specs=[blk(0), blk(0), blk(0), pl.BlockSpec((2, n_tiles, GRID_W, LANES), lambda p, b: (p, 0, 0, 0))],
        scratch_shapes=[pltpu.VMEM((2, seq, LANES), BF16), pltpu.VMEM((2, seq, LANES), BF16),
                        pltpu.VMEM((seq, LANES), F32), pltpu.VMEM((seq, LANES), F32)],
        compiler_params=_cparams(("arbitrary", "arbitrary")), args=[proj, proj, proj, d_o, tiles])


def _na_bias_grad(dtiles, expand):
    n = dtiles.shape[0]

    def body(t_ref, e_ref, o_ref):
        flat = jnp.concatenate([t_ref[:, qq, :] for qq in range(GRID_W)], axis=1)
        o_ref[...] = lax.dot_general(flat, e_ref[...], (((1,), (1,)), ((), ())),
                                     precision=lax.Precision.HIGHEST, preferred_element_type=F32)

    return pl.pallas_call(
        body, name="na_bias_grad",
        out_shape=jax.ShapeDtypeStruct((n, expand.shape[0]), F32),
        compiler_params=_cparams(vmem=VMEM_BIG),
    )(dtiles, expand)


def _sw_backward(proj, d_o, sink, batch, seq, rider=None):
    t = proj.shape[0]
    n_pairs = SW_WIDTH // LANES
    q_blk = 3 * NA_WIDTH // LANES
    k_blk = q_blk + n_pairs
    n_blocks = seq // SW_BLOCK
    pad = seq + 2 * SW_BLOCK

    def body(sink_ref, q_ref, k_ref, v_ref, do_ref, dq_ref, dk_ref, dv_ref, dsk_ref,
             k_lo, k_hi, v_lo, v_hi, dk_loc, dv_loc, dk_tot, dv_tot):
        hp = pl.program_id(1)
        g = hp // 2
        _sw_prepare(k_ref, g, k_lo, k_hi, seq)
        _sw_prepare(v_ref, g, v_lo, v_hi, seq)
        dk_loc[...] = jnp.zeros_like(dk_loc)
        dv_loc[...] = jnp.zeros_like(dv_loc)

        @pl.when(hp == 0)
        def _():
            dk_tot[...] = jnp.zeros_like(dk_tot)
            dv_tot[...] = jnp.zeros_like(dv_tot)

        band = 3 * SW_BLOCK
        low = lax.broadcasted_iota(jnp.int32, (band, LANES), 1) < HEAD_DIM

        sinks = (sink_ref[2 * hp], sink_ref[2 * hp + 1])

        def scores(n):
            rows = pl.ds(pl.multiple_of(n * SW_BLOCK, SW_BLOCK), SW_BLOCK)
            wrows = pl.ds(pl.multiple_of(n * SW_BLOCK, SW_BLOCK), band)
            qb, do = q_ref[rows, :], do_ref[rows, :]
            k2 = jnp.concatenate([k_lo[wrows, :], k_hi[wrows, :]], axis=0)
            v2 = jnp.concatenate([v_lo[wrows, :], v_hi[wrows, :]], axis=0)
            return n, rows, wrows, qb, do, k2, _mm_nt(qb, k2) * QK_SCALE, _mm_nt(do, v2)

        def finish(sink_acc, n, rows, wrows, qb, do, k2, s2, dp):
            p, ps = _sw_probs(s2, _sw_mask(n, seq), sinks)
            parts, new = [], []
            for i in range(2):
                ph, dph = p[:, i * band:(i + 1) * band], dp[:, i * band:(i + 1) * band]
                delta = jnp.sum(ph * dph, axis=-1, keepdims=True)
                parts.append(ph * (dph - delta))
                new.append(sink_acc[i] - ps[i] * delta)
            dsb = (jnp.concatenate(parts, axis=1) * QK_SCALE).astype(BF16)
            dq_ref[rows, :] = _mm(dsb, k2)
            dk2 = _mm_tn(dsb, qb)
            dv2 = _mm_tn(p.astype(BF16), do)
            dk_loc[wrows, :] += jnp.where(low, dk2[:band], dk2[band:])
            dv_loc[wrows, :] += jnp.where(low, dv2[:band], dv2[band:])
            return tuple(new)

        def block_group(i, carry):
            for state in [scores(SW_GROUP_BLOCKS * i + j) for j in range(SW_GROUP_BLOCKS)]:
                carry = finish(carry, *state)
            return carry

        zero = jnp.zeros((SW_BLOCK, 1), F32)
        s0, s1 = lax.fori_loop(0, n_blocks // SW_GROUP_BLOCKS, block_group, (zero, zero))
        row = lax.broadcasted_iota(jnp.int32, (SUBLANES, LANES), 0)
        dsk_ref[0, 0] = jnp.where(row == 0, jnp.sum(s0), jnp.where(row == 1, jnp.sum(s1), 0.0))

        lane_s = lax.broadcasted_iota(jnp.int32, (seq, LANES), 1)
        mine_g = (lane_s // HEAD_DIM) == g
        for loc, tot in ((dk_loc, dk_tot), (dv_loc, dv_tot)):
            part = loc[SW_BLOCK:SW_BLOCK + seq, :]
            tot[...] += jnp.where(mine_g, part + pltpu.roll(part, HEAD_DIM, 1), 0.0)

        @pl.when(hp == n_pairs - 1)
        def _():
            dk_ref[...] = dk_tot[...]
            dv_ref[...] = dv_tot[...].astype(BF16)

    return _hosted(
        body, rider, name="sw_backward", grid=(batch, n_pairs),
        out_shape=[jax.ShapeDtypeStruct((t, SW_WIDTH), F32), jax.ShapeDtypeStruct((t, LANES), F32),
                   jax.ShapeDtypeStruct((t, LANES), BF16), jax.ShapeDtypeStruct((batch, n_pairs, SUBLANES, LANES), F32)],
        in_specs=[pl.BlockSpec(memory_space=pltpu.SMEM),
                  pl.BlockSpec((seq, LANES), lambda b, p: (b, q_blk + p)),
                  pl.BlockSpec((seq, LANES), lambda b, p: (b, k_blk)),
                  pl.BlockSpec((seq, LANES), lambda b, p: (b, k_blk + 1)),
                  pl.BlockSpec((seq, LANES), lambda b, p: (b, p))],
        out_specs=[pl.BlockSpec((seq, LANES), lambda b, p: (b, p)), pl.BlockSpec((seq, LANES), lambda b, p: (b, 0)),
                   pl.BlockSpec((seq, LANES), lambda b, p: (b, 0)),
                   pl.BlockSpec((1, 1, SUBLANES, LANES), lambda b, p: (b, p, 0, 0))],
        scratch_shapes=[pltpu.VMEM((pad, LANES), BF16)] * 4 + [pltpu.VMEM((pad, LANES), F32)] * 2
        + [pltpu.VMEM((seq, LANES), F32)] * 2,
        compiler_params=_cparams(("arbitrary", "arbitrary")), args=[sink, proj, proj, proj, d_o])


def _in_backward(dqkv_a, dq_b, dk_b, dv_b, w_in_t, h1, x, mod3, g_attn, dx1, cos_t, sin_t, seq):
    t, d = x.shape
    tm = TOKEN_TILE // 2
    per_seq = seq // tm
    batch = t // seq
    dqa, dka, dva = dqkv_a
    n_q = SW_WIDTH // LANES

    def body(dqa_ref, dka_ref, dva_ref, dqb_ref, dkb_ref, dvb_ref, w_ref, h_ref, x_ref, mod_ref, g_ref, dx1_ref,
             cos_ref, sin_ref, dx_ref, gw_ref, gwb_ref, dsh_ref, dsc_ref, dg_ref):
        i = pl.program_id(0)

        @pl.when(i == 0)
        def _():
            gw_ref[...] = jnp.zeros_like(gw_ref)
            dg_ref[...] = jnp.zeros_like(dg_ref)

        @pl.when(i % per_seq == 0)
        def _():
            dsh_ref[...] = jnp.zeros_like(dsh_ref)
            dsc_ref[...] = jnp.zeros_like(dsc_ref)

        dr = jnp.concatenate([dqb_ref[...], dkb_ref[...]], axis=1)
        cos = jnp.concatenate([cos_ref[...]] * (n_q + 1), axis=1)
        sin = jnp.concatenate([sin_ref[...]] * (n_q + 1), axis=1)
        dr = dr * cos + _rope_rot(dr * sin)
        dproj = jnp.concatenate([dqa_ref[...], dka_ref[...], dva_ref[...], dr.astype(BF16), dvb_ref[...]], axis=1)
        gw_ref[...] += _mm_tn(dproj, h_ref[...])

        @pl.when(i == t // tm - 1)
        def _():
            gwb_ref[...] = gw_ref[...].astype(BF16)

        dh = _mm(dproj, w_ref[...])
        scale = mod_ref[0, :, d:2 * d]
        r, xn = _rms_stats(x_ref[...])
        xg = xn * g_ref[...]
        dxg = dh * (1.0 + scale)
        dx_ref[...] = dx1_ref[...] + _rms_bwd(dxg * g_ref[...], xn, r)
        dg_ref[...] += jnp.sum(dxg * xn, axis=0, keepdims=True)
        dsh_ref[0] += jnp.sum(dh, axis=0, keepdims=True)
        dsc_ref[0] += jnp.sum(dh * xg, axis=0, keepdims=True)

    tile = lambda w: pl.BlockSpec((tm, w), lambda i: (i, 0))
    per_b = pl.BlockSpec((1, 1, d), lambda i: (i // per_seq, 0, 0))
    small = jax.ShapeDtypeStruct((batch, 1, d), F32)
    rope = pl.BlockSpec((tm, LANES), lambda i: (i % per_seq, 0))
    return pl.pallas_call(
        body, name="in_backward", grid=(t // tm,),
        out_shape=(jax.ShapeDtypeStruct((t, d), F32), jax.ShapeDtypeStruct((IN_WIDTH, d), F32),
                   jax.ShapeDtypeStruct((IN_WIDTH, d), BF16), small, small, jax.ShapeDtypeStruct((1, d), F32)),
        in_specs=[tile(NA_WIDTH), tile(NA_WIDTH), tile(NA_WIDTH), tile(SW_WIDTH), tile(LANES), tile(LANES),
                  pl.BlockSpec((IN_WIDTH, d), lambda i: (0, 0)), tile(d), tile(d),
                  pl.BlockSpec((1, 1, 6 * d), lambda i: (i // per_seq, 0, 0)),
                  pl.BlockSpec((1, d), lambda i: (0, 0)), tile(d), rope, rope],
        out_specs=(tile(d), pl.BlockSpec((IN_WIDTH, d), lambda i: (0, 0)), pl.BlockSpec((IN_WIDTH, d), lambda i: (0, 0)),
                   per_b, per_b, pl.BlockSpec((1, d), lambda i: (0, 0))),
        compiler_params=_cparams(("arbitrary",), VMEM_BIG),
    )(dqa, dka, dva, dq_b, dk_b, dv_b, w_in_t, h1, x, mod3, g_attn, dx1, cos_t, sin_t)


def _ada_weight_grad(sc_all, dmod_cols):
    d = sc_all.shape[1]
    ncol = dmod_cols.shape[1]

    def body(s_ref, m_ref, o_ref):
        o_ref[...] = _mm_tn(s_ref[...].astype(BF16), m_ref[...].astype(BF16))

    return pl.pallas_call(
        body, name="ada_weight_grad",
        out_shape=jax.ShapeDtypeStruct((d, ncol), F32),
        compiler_params=_cparams(vmem=VMEM_BIG),
    )(sc_all, dmod_cols)


def _row_tile(rows, cols):
    target = max(SUBLANES, (1 << 20) // (4 * cols))
    best = rows
    for cand in range(SUBLANES, rows + 1, SUBLANES):
        if rows % cand == 0 and cand <= target:
            best = cand
    return best if rows % SUBLANES == 0 else rows


def _sum_slots(parts, name):
    n = len(parts)
    _, rows, cols = parts[0][0].shape
    tr = _row_tile(rows, cols)
    per = rows // tr

    def body(*refs):
        o_ref = refs[-1]
        for q in range(n):
            @pl.when(pl.program_id(0) == q)
            def _(q=q):
                p_ref, own_ref = refs[2 * q], refs[2 * q + 1]
                o_ref[...] = ((own_ref[...] + p_ref[0].astype(F32)) + p_ref[1].astype(F32)) + p_ref[2].astype(F32)

    in_specs, args = [], []
    for q, (recv, own) in enumerate(parts):
        in_specs.append(pl.BlockSpec((N_SHARD - 1, tr, cols), lambda p, i, q=q: (0, jnp.where(p == q, i, 0), 0)))
        in_specs.append(pl.BlockSpec((tr, cols), lambda p, i, q=q: (jnp.where(p == q, i, 0), 0)))
        args += [recv, own]
    return pl.pallas_call(
        body, name=name, grid=(n, per),
        out_shape=jax.ShapeDtypeStruct((n * rows, cols), F32),
        in_specs=in_specs, out_specs=pl.BlockSpec((tr, cols), lambda p, i: (p * per + i, 0)),
        compiler_params=_cparams(("arbitrary", "arbitrary")),
    )(*args)


def _adamw(w, grads, m, v, name):
    rows, cols = w.shape
    tr = _row_tile(rows, cols)
    ng = len(grads)

    def body(*refs):
        w_ref = refs[0]
        g_refs = refs[1:1 + ng]
        m_ref, v_ref = refs[1 + ng], refs[2 + ng]
        g_out, d_out, m_out, v_out = refs[3 + ng:]
        g = g_refs[0][...]
        for extra in g_refs[1:]:
            g = g + extra[...]
        g_out[...] = g
        m2 = ADAM_B1 * m_ref[...] + (1.0 - ADAM_B1) * g
        v2 = ADAM_B2 * v_ref[...] + (1.0 - ADAM_B2) * (g * g)
        m_out[...] = m2
        v_out[...] = v2
        m_hat = m2 / (1.0 - ADAM_B1 ** ADAM_STEP)
        v_hat = v2 / (1.0 - ADAM_B2 ** ADAM_STEP)
        d_out[...] = -ADAM_LR * (m_hat / (jnp.sqrt(v_hat) + ADAM_EPS) + ADAM_WD * w_ref[...])

    spec = pl.BlockSpec((tr, cols), lambda i: (i, 0))
    out = jax.ShapeDtypeStruct((rows, cols), F32)
    return pl.pallas_call(
        body, name=name, grid=(rows // tr,),
        out_shape=(out, out, out, out),
        in_specs=[spec] * (3 + ng), out_specs=(spec, spec, spec, spec),
        compiler_params=_cparams(("arbitrary",)),
    )(w, *grads, m, v)


def _pack_rows(arrays):
    tile = SUBLANES * LANES
    rows, offsets, at = [], [], 0
    for a in arrays:
        flat = a.reshape(-1).astype(F32)
        n = -(-flat.shape[0] // tile) * tile
        rows.append(jnp.pad(flat, (0, n - flat.shape[0])).reshape(-1, LANES))
        offsets.append(at)
        at += n // LANES
    return jnp.concatenate(rows, axis=0), offsets


def _unpack_rows(packed, offsets, shapes):
    out = []
    for off, shape in zip(offsets, shapes):
        n = 1
        for s in shape:
            n *= s
        nrow = -(-n // LANES)
        out.append(packed[off:off + nrow].reshape(-1)[:n].reshape(shape))
    return out


def _rope_tables(seq):
    half = HEAD_DIM // 2
    inv = np.float32(ROPE_THETA) ** (-np.arange(half, dtype=np.float32) / np.float32(half))
    ang = (np.arange(seq, dtype=np.float32)[:, None] * inv[None, :]).astype(np.float64)
    cos, sin = np.cos(ang).astype(np.float32), np.sin(ang).astype(np.float32)
    cos_t = np.concatenate([cos, cos, cos, cos], axis=1)
    sin_t = np.concatenate([-sin, sin, -sin, sin], axis=1)
    return jnp.asarray(cos_t), jnp.asarray(sin_t)


def kernel(x, c, w_ada, b_ada, g_attn, w_in, na_rpb, sw_sink, g_na_out, g_sw_out, w_out, g_ffn, w_up, conv_w, conv_b, w_down, g_final, loss_target, m_w_ada, m_b_ada, m_g_attn, m_w_in, m_na_rpb, m_sw_sink, m_g_na_out, m_g_sw_out, m_w_out, m_g_ffn, m_w_up, m_conv_w, m_conv_b, m_w_down, m_g_final, v_w_ada, v_b_ada, v_g_attn, v_w_in, v_na_rpb, v_sw_sink, v_g_na_out, v_g_sw_out, v_w_out, v_g_ffn, v_w_up, v_conv_w, v_conv_b, v_w_down, v_g_final):
    batch, seq, d = x.shape
    t = batch * seq
    assert d == D_MODEL and seq % (NA_ROWS * GRID_W) == 0 and seq % TOKEN_TILE == 0 and batch <= SUBLANES
    shard = 2 * lax.axis_index("x") + lax.axis_index("y")
    xt = x.reshape(t, d)
    tgt = loss_target.reshape(t, d)

    c8 = jnp.pad(c, ((0, SUBLANES - batch), (0, 0)))
    w_in_t_s = jnp.transpose(w_in[0]).astype(BF16)
    (mod8, sc_all), (w_in_g,) = _ada_forward(c8, w_ada[0], b_ada, _Rider("gather", [w_in_t_s]))
    mod3 = mod8[:batch].reshape(batch, 1, 6 * d)
    w_in_t = w_in_g.reshape(IN_WIDTH, d)

    cos_t, sin_t = _rope_tables(seq)
    (h1, proj), (w_out_g,) = _in_proj(xt, mod3, g_attn, w_in_t, cos_t, sin_t, seq,
                                      _Rider("gather", [w_out[0].astype(BF16)]))
    n_heads = NA_WIDTH // HEAD_DIM
    n_tiles, n_dc = 2 * NA_ROWS - 2, 2 * NA_COLS - 1
    expand, neg_mask = _na_bias_pattern()
    rpb = na_rpb[0]
    rows2 = jnp.concatenate([rpb[:, :-1, :], rpb[:, 1:, :]], axis=2).reshape(n_heads * n_tiles, 2 * n_dc)
    rows2 = jnp.pad(rows2, ((0, 0), (0, GRID_W - 2 * n_dc)))
    tiles = _na_bias_tiles(rows2, expand, neg_mask).reshape(n_heads, n_tiles, GRID_W, LANES)
    sink = sw_sink[0]
    w_up_b16 = w_up[0].astype(BF16)
    (oa,), (w_up_a,) = _na_forward(proj, tiles, batch, seq, _Rider("gather", [w_up_b16[:d // 2]]))
    (ob,), (w_up_b, conv_w_g) = _sw_forward(proj, sink, batch, seq, _Rider("gather", [w_up_b16[d // 2:], conv_w[0]]))
    w_up_f = (w_up_a, w_up_b)
    w_out_f = w_out_g.reshape(d, d)
    conv_w_f = jnp.transpose(conv_w_g, (1, 0, 2)).reshape(3, D_FF)
    oab, mix, x1, h2 = _out_proj(oa, ob, g_na_out, g_sw_out, w_out_f, xt, mod3, g_ffn, seq)
    (u,), (w_down_g,) = _up_proj(h2, w_up_f, _Rider("gather", [w_down[0].astype(BF16)]))
    w_down_f = w_down_g.reshape(D_FF, d)
    a = _conv_gate(u, conv_w_f, conv_b, batch, seq)
    dx2, dffn, loss_part, dgate_f, dg_final = _down_and_loss(a, w_down_f, x1, mod3, g_final.reshape(1, d), tgt, seq)

    gw_down, gw_down_b = _down_weight_grad(a, dffn)
    blocks = lambda g, rows: g.reshape(N_SHARD, rows // N_SHARD, d)
    (du, gconv_w, gconv_b), (recv_down, own_down) = _ffn_backward(
        dffn, w_down_f, u, conv_w_f, conv_b, batch, seq,
        _Rider("scatter", [blocks(gw_down_b, D_FF)], [blocks(gw_down, D_FF)]))
    (gw_up_top, gw_up_bot, gw_up_top_b, gw_up_bot_b), _ = _up_weight_grad(h2, du)
    (dx1, dmix, dshift_f, dscale_f, dgate_a, dg_ffn), (recv_up_top, own_up_top) = _up_backward(
        du, w_up_f, x1, mod3, g_ffn, dx2, mix, seq, _Rider("scatter", [gw_up_top_b], [gw_up_top]))
    doa, dob, gw_out, gw_out_b, dg_na, dg_sw = _out_backward(dmix, w_out_f, oab, oa, ob, g_na_out, g_sw_out)
    (dqa, dka, dva, dtiles), (recv_up_bot, own_up_bot) = _na_backward(
        proj, doa, tiles, batch, seq, _Rider("scatter", [gw_up_bot_b], [gw_up_bot]))
    (dq_b, dk_b, dv_b, dsink_parts), (recv_out, own_out) = _sw_backward(
        proj, dob, sink, batch, seq, _Rider("scatter", [blocks(gw_out_b, d)], [blocks(gw_out, d)]))
    gx, gw_in_t, gw_in_b, dshift_a, dscale_a, dg_attn = _in_backward(
        (dqa, dka, dva), dq_b, dk_b, dv_b, w_in_t, h1, xt, mod3, g_attn, dx1, cos_t, sin_t, seq)

    red = _na_bias_grad(dtiles.reshape(n_heads * n_tiles, GRID_W, LANES), expand)[:, :2 * n_dc]
    red = red.reshape(n_heads, n_tiles, 2, n_dc)
    zero_row = jnp.zeros((n_heads, 1, n_dc), F32)
    g_rpb = (jnp.concatenate([red[:, :, 0, :], zero_row], axis=1)
             + jnp.concatenate([zero_row, red[:, :, 1, :]], axis=1))
    g_sink = jnp.sum(dsink_parts[:, :, :2, 0], axis=0).reshape(SW_WIDTH // HEAD_DIM)

    dmod = jnp.concatenate([dshift_a, dscale_a, dgate_a, dshift_f, dscale_f, dgate_f], axis=2).reshape(batch, 6 * d)
    small_parts = [jnp.sum(dmod, axis=0), dg_attn, g_rpb, g_sink, dg_na, dg_sw, dg_ffn, gconv_w, gconv_b, dg_final,
                   loss_part[0, 0:1]]
    packed, offsets = _pack_rows(small_parts + [dmod])
    (summed, every), (recv_in, own_in) = _allreduce_small(
        packed, _Rider("scatter", [blocks(gw_in_b, IN_WIDTH)], [blocks(gw_in_t, IN_WIDTH)]))
    mine = [_sum_slots([(recv_in, own_in)], "sum_w_in"), _sum_slots([(recv_out, own_out)], "sum_w_out"),
            _sum_slots([(recv_up_top, own_up_top), (recv_up_bot, own_up_bot)], "sum_w_up"),
            _sum_slots([(recv_down, own_down)], "sum_w_down")]
    theirs = _ride_alone(_Rider("swap", mine), "swap_sibling")
    small_shapes = [(1, 6 * d), (1, d), na_rpb.shape, sw_sink.shape, (1, NA_WIDTH), (1, SW_WIDTH), (1, d),
                    (3, D_FF), (1, D_FF), (d,), ()]
    (g_b_ada, g_g_attn, g_na_rpb, g_sw_sink, g_g_na, g_g_sw, g_g_ffn, g_conv_w_full, g_conv_b, g_g_final,
     loss) = _unpack_rows(summed, offsets[:-1], small_shapes)
    dmod_rows = every[:, offsets[-1]:offsets[-1] + batch * 6 * d // LANES, :].reshape(N_DEV, batch, 6 * d)
    dmod_rows = jnp.pad(dmod_rows, ((0, 0), (0, SUBLANES - batch), (0, 0))).reshape(N_DEV * SUBLANES, 6 * d)
    ncol = w_ada.shape[2]
    g_w_ada = _ada_weight_grad(sc_all, lax.dynamic_slice(dmod_rows, (0, shard * ncol), (N_DEV * SUBLANES, ncol)))
    cshard = conv_w.shape[2]
    g_conv_w = lax.dynamic_slice(g_conv_w_full, (0, shard * cshard), (3, cshard)).reshape(conv_w.shape)

    def big(w, m, v, g_parts, name):
        shape = w.shape
        outs = _adamw(w[0], g_parts, m[0], v[0], name)
        return [o.reshape(shape) for o in outs]

    r_w_ada = big(w_ada, m_w_ada, v_w_ada, [g_w_ada], "adamw_w_ada")
    r_w_in = [jnp.transpose(o).reshape(w_in.shape) for o in
              _adamw(jnp.transpose(w_in[0]), [mine[0], theirs[0]], jnp.transpose(m_w_in[0]), jnp.transpose(v_w_in[0]),
                     "adamw_w_in")]
    r_w_out = big(w_out, m_w_out, v_w_out, [mine[1], theirs[1]], "adamw_w_out")
    r_w_up = big(w_up, m_w_up, v_w_up, [mine[2], theirs[2]], "adamw_w_up")
    r_w_down = big(w_down, m_w_down, v_w_down, [mine[3], theirs[3]], "adamw_w_down")

    small_w = [b_ada, g_attn, na_rpb, sw_sink, g_na_out, g_sw_out, g_ffn, conv_w, conv_b, g_final]
    small_m = [m_b_ada, m_g_attn, m_na_rpb, m_sw_sink, m_g_na_out, m_g_sw_out, m_g_ffn, m_conv_w, m_conv_b, m_g_final]
    small_v = [v_b_ada, v_g_attn, v_na_rpb, v_sw_sink, v_g_na_out, v_g_sw_out, v_g_ffn, v_conv_w, v_conv_b, v_g_final]
    small_g = [g_b_ada, g_g_attn, g_na_rpb, g_sw_sink, g_g_na, g_g_sw, g_g_ffn, g_conv_w, g_conv_b, g_g_final]
    pw, offs = _pack_rows(small_w)
    pg, _ = _pack_rows(small_g)
    pm, _ = _pack_rows(small_m)
    pv, _ = _pack_rows(small_v)
    shapes = [w.shape for w in small_w]
    r_small = [_unpack_rows(o, offs, shapes) for o in _adamw(pw, [pg], pm, pv, "adamw_small")]

    def pick(k):
        b_, ga_, rpb_, sk_, gna_, gsw_, gf_, cw_, cb_, gfin_ = r_small[k]
        return [r_w_ada[k], b_, ga_, r_w_in[k], rpb_, sk_, gna_, gsw_, r_w_out[k], gf_, r_w_up[k], cw_, cb_,
                r_w_down[k], gfin_]

    return (loss, gx.reshape(batch, seq, d), *pick(0), *pick(1), *pick(2), *pick(3))
```

```python
import functools

import jax
import jax.numpy as jnp
import numpy as np
from jax import lax
from jax.experimental import pallas as pl
from jax.experimental.pallas import tpu as pltpu

F32 = jnp.float32
BF16 = jnp.bfloat16
MESH = pl.DeviceIdType.MESH

D_MODEL = 1024
HEAD_DIM = 64
NA_WIDTH = 512
SW_WIDTH = 512
SW_KV_WIDTH = 128
IN_WIDTH = 2304
D_FF = 2816
GRID_W = 64
NA_ROWS = 8
NA_COLS = 16
SW_BLOCK = 128
ROPE_THETA = 10000.0
EPS = 1e-6
NEG = -1e30
QK_SCALE = HEAD_DIM ** -0.5

ADAM_LR = 0.001
ADAM_B1 = 0.9
ADAM_B2 = 0.999
ADAM_EPS = 1e-08
ADAM_WD = 0.01
ADAM_STEP = 10

N_SHARD = 4
N_DEV = 8
LANES = 128
SUBLANES = 8
TOKEN_TILE = 512
FF_TILE = 256
CONV_CHUNK = 64
NA_GROUP = 4
SW_GROUP_BLOCKS = 4
VMEM_BIG = 56 * 1024 * 1024


def _mm(a, b):
    return jnp.dot(a, b, preferred_element_type=F32)


def _mm_nt(a, b):
    return lax.dot_general(a, b, (((1,), (1,)), ((), ())), preferred_element_type=F32)


def _mm_tn(a, b):
    return lax.dot_general(a, b, (((0,), (0,)), ((), ())), preferred_element_type=F32)


def _cparams(sem=None, vmem=None):
    kw = {}
    if sem is not None:
        kw["dimension_semantics"] = sem
    if vmem is not None:
        kw["vmem_limit_bytes"] = vmem
    return pltpu.CompilerParams(**kw)


def _sigmoid(x):
    return 1.0 / (1.0 + jnp.exp(-x))


def _rms_stats(x):
    r = lax.rsqrt(jnp.mean(x * x, axis=-1, keepdims=True) + EPS)
    return r, x * r


def _rms_bwd(dxn, xn, r):
    return r * (dxn - xn * jnp.mean(dxn * xn, axis=-1, keepdims=True))


def _my_pos():
    return lax.axis_index("x"), lax.axis_index("y"), lax.axis_index("c")


def _flip(v, bit):
    return 1 - v if bit else v


def _ada_forward(c8, w_ada, b_ada, rider):
    d = c8.shape[1]
    ncol = w_ada.shape[1]

    def body(c_ref, w_ref, b_ref, mod_ref, sc_ref, m_scr, mod_buf, ssem, rsem, ssem2, rsem2):
        x, y, c = _my_pos()
        me = 4 * x + 2 * y + c
        shard = 2 * x + y
        cv = c_ref[...]
        my_rows = pl.ds(pl.multiple_of(me * SUBLANES, SUBLANES), SUBLANES)
        sc_ref[my_rows, :] = cv * _sigmoid(cv)

        def copy1(k):
            peer = (_flip(x, (k >> 2) & 1), _flip(y, (k >> 1) & 1), _flip(c, k & 1))
            return pltpu.make_async_remote_copy(
                src_ref=sc_ref.at[my_rows, :], dst_ref=sc_ref.at[my_rows, :],
                send_sem=ssem.at[k - 1], recv_sem=rsem.at[k - 1], device_id=peer, device_id_type=MESH)

        sends = [copy1(k) for k in range(1, N_DEV)]
        for cp in sends:
            cp.start()
        for cp in sends:
            cp.wait_recv()
        m_scr[...] = _mm(sc_ref[...].astype(BF16), w_ref[...].astype(BF16))

        def copy2(k):
            px, py = _flip(x, (k >> 1) & 1), _flip(y, k & 1)
            rows = pl.ds(pl.multiple_of((4 * px + 2 * py + c) * SUBLANES, SUBLANES), SUBLANES)
            return pltpu.make_async_remote_copy(
                src_ref=m_scr.at[rows, :], dst_ref=mod_buf.at[shard],
                send_sem=ssem2.at[k - 1], recv_sem=rsem2.at[k - 1], device_id=(px, py, c), device_id_type=MESH)

        sends2 = [copy2(k) for k in range(1, N_SHARD)]
        for cp in sends2:
            cp.start()
        mod_buf[shard] = m_scr[my_rows, :]
        for cp in sends2:
            cp.wait_recv()
        for s in range(N_SHARD):
            mod_ref[:, s * ncol:(s + 1) * ncol] = mod_buf[s] + b_ref[:, s * ncol:(s + 1) * ncol]
        for cp in sends + sends2:
            cp.wait_send()

    vm = pl.BlockSpec(memory_space=pltpu.VMEM)
    return _hosted(
        body, rider, name="ada_forward", grid=(),
        out_shape=(jax.ShapeDtypeStruct((SUBLANES, N_SHARD * ncol), F32),
                   jax.ShapeDtypeStruct((N_DEV * SUBLANES, d), F32)),
        in_specs=[vm, vm, vm], out_specs=(vm, vm),
        scratch_shapes=[pltpu.VMEM((N_DEV * SUBLANES, ncol), F32), pltpu.VMEM((N_SHARD, SUBLANES, ncol), F32),
                        pltpu.SemaphoreType.DMA((N_DEV - 1,)), pltpu.SemaphoreType.DMA((N_DEV - 1,)),
                        pltpu.SemaphoreType.DMA((N_SHARD - 1,)), pltpu.SemaphoreType.DMA((N_SHARD - 1,))],
        compiler_params=_cparams(vmem=VMEM_BIG), args=[c8, w_ada, b_ada])


class _Rider:
    def __init__(self, kind, srcs, owns=()):
        self.kind, self.srcs, self.owns = kind, list(srcs), list(owns)
        n = len(self.srcs)
        sds = jax.ShapeDtypeStruct
        dma = pltpu.SemaphoreType.DMA
        if kind == "gather":
            self.out_shapes = [sds((N_SHARD,) + s.shape, s.dtype) for s in self.srcs]
            self.sems = [dma((n, N_SHARD - 1)), dma((n, N_SHARD - 1)), dma((n,)),
                         dma((n, N_SHARD - 1)), dma((n, N_SHARD - 1))]
        elif kind == "scatter":
            self.out_shapes = ([sds((N_SHARD - 1,) + s.shape[1:], s.dtype) for s in self.srcs]
                               + [sds(o.shape[1:], o.dtype) for o in self.owns])
            self.sems = [dma((n, N_SHARD - 1)), dma((n, N_SHARD - 1)), dma((max(len(self.owns), 1),))]
        else:
            self.out_shapes = [sds(s.shape, s.dtype) for s in self.srcs]
            self.sems = [dma((n,)), dma((n,))]

    @property
    def inputs(self):
        return self.srcs + self.owns

    def _halved(self, i):
        a = self.srcs[i]
        tile_rows = SUBLANES * (4 // jnp.dtype(a.dtype).itemsize)
        return self.kind == "gather" and a.shape[0] % (2 * tile_rows) == 0

    def copies(self, ins, outs, sems):
        n = len(self.srcs)
        x, y, c = _my_pos()
        shard = 2 * x + y
        local, remote, relay = [], [], []
        if self.kind == "swap":
            ssem, rsem = sems
            for i in range(n):
                remote.append(pltpu.make_async_remote_copy(
                    src_ref=ins[i], dst_ref=outs[i], send_sem=ssem.at[i], recv_sem=rsem.at[i],
                    device_id=(x, y, 1 - c), device_id_type=MESH))
            return local, remote, relay
        if self.kind == "gather":
            ssem, rsem, lsem, ssem2, rsem2 = sems
        else:
            ssem, rsem, lsem = sems
        for i in range(n):
            if self.kind == "gather":
                local.append(pltpu.make_async_copy(ins[i], outs[i].at[shard], lsem.at[i]))
                half = ins[i].shape[0] // 2
                mine = pl.ds(pl.multiple_of(c * half, half), half) if self._halved(i) else None
            for k in range(1, N_SHARD):
                px, py = _flip(x, (k >> 1) & 1), _flip(y, k & 1)
                if self.kind == "gather":
                    src, dst = ins[i], outs[i].at[shard]
                    if mine is not None:
                        src, dst = src.at[mine], dst.at[mine]
                        got = outs[i].at[2 * px + py].at[mine]
                        relay.append(pltpu.make_async_remote_copy(
                            src_ref=got, dst_ref=got, send_sem=ssem2.at[i, k - 1], recv_sem=rsem2.at[i, k - 1],
                            device_id=(x, y, 1 - c), device_id_type=MESH))
                else:
                    src, dst = ins[i].at[2 * px + py], outs[i].at[k - 1]
                remote.append(pltpu.make_async_remote_copy(
                    src_ref=src, dst_ref=dst, send_sem=ssem.at[i, k - 1], recv_sem=rsem.at[i, k - 1],
                    device_id=(px, py, c), device_id_type=MESH))
        if self.kind == "scatter":
            for i in range(len(self.owns)):
                local.append(pltpu.make_async_copy(ins[n + i].at[shard], outs[n + i], lsem.at[i]))
        return local, remote, relay

    def start(self, ins, outs, sems):
        local, remote, _ = self.copies(ins, outs, sems)
        for cp in local + remote:
            cp.start()

    def wait(self, ins, outs, sems):
        local, remote, relay = self.copies(ins, outs, sems)
        for cp in remote:
            cp.wait_recv()
        for cp in relay:
            cp.start()
        for cp in relay:
            cp.wait_recv()
        for cp in remote + relay:
            cp.wait_send()
        for cp in local:
            cp.wait()


def _hosted(body, rider, *, name, grid, out_shape, in_specs, out_specs, scratch_shapes, compiler_params, args):
    out_shape, out_specs = list(out_shape), list(out_specs)
    if rider is None:
        outs = pl.pallas_call(body, name=name, grid=grid, out_shape=tuple(out_shape), in_specs=list(in_specs),
                              out_specs=tuple(out_specs), scratch_shapes=list(scratch_shapes),
                              compiler_params=compiler_params)(*args)
        return list(outs), []
    n_in, n_out, n_scr = len(in_specs), len(out_shape), len(scratch_shapes)
    nr_in, nr_out = len(rider.inputs), len(rider.out_shapes)
    n_steps = 1
    for size in grid:
        n_steps *= size

    def full(*refs):
        ins, refs = refs[:n_in], refs[n_in:]
        r_in, refs = refs[:nr_in], refs[nr_in:]
        outs, refs = refs[:n_out], refs[n_out:]
        r_out, refs = refs[:nr_out], refs[nr_out:]
        scr, sems = refs[:n_scr], refs[n_scr:]
        if grid:
            step = 0
            for ax, size in enumerate(grid):
                step = step * size + pl.program_id(ax)
            pl.when(step == 0)(lambda: rider.start(r_in, r_out, sems))
            body(*ins, *outs, *scr)
            pl.when(step == n_steps - 1)(lambda: rider.wait(r_in, r_out, sems))
        else:
            rider.start(r_in, r_out, sems)
            body(*ins, *outs, *scr)
            rider.wait(r_in, r_out, sems)

    hbm = pl.BlockSpec(memory_space=pl.ANY)
    res = pl.pallas_call(
        full, name=name, grid=grid, out_shape=tuple(out_shape + rider.out_shapes),
        in_specs=list(in_specs) + [hbm] * nr_in, out_specs=tuple(out_specs + [hbm] * nr_out),
        scratch_shapes=list(scratch_shapes) + rider.sems, compiler_params=compiler_params,
    )(*args, *rider.inputs)
    return list(res[:n_out]), list(res[n_out:])


def _ride_alone(rider, name):
    return _hosted(lambda: None, rider, name=name, grid=(), out_shape=[], in_specs=[], out_specs=[], scratch_shapes=[],
                   compiler_params=_cparams(), args=[])[1]


def _allreduce_small(packed, rider=None):
    r = packed.shape[0]

    def body(p_ref, sum_ref, all_ref, ssem, rsem):
        x, y, c = _my_pos()
        me = 4 * x + 2 * y + c
        all_ref[me] = p_ref[...]
        cps = []
        for k in range(1, N_DEV):
            peer = (_flip(x, (k >> 2) & 1), _flip(y, (k >> 1) & 1), _flip(c, k & 1))
            cps.append(pltpu.make_async_remote_copy(
                src_ref=all_ref.at[me], dst_ref=all_ref.at[me], send_sem=ssem.at[k - 1], recv_sem=rsem.at[k - 1],
                device_id=peer, device_id_type=MESH))
        for cp in cps:
            cp.start()
        for cp in cps:
            cp.wait_recv()
        acc = all_ref[0]
        for dev in range(1, N_DEV):
            acc = acc + all_ref[dev]
        sum_ref[...] = acc
        for cp in cps:
            cp.wait_send()

    vm = pl.BlockSpec(memory_space=pltpu.VMEM)
    return _hosted(
        body, rider, name="allreduce_small", grid=(),
        out_shape=[jax.ShapeDtypeStruct((r, LANES), F32), jax.ShapeDtypeStruct((N_DEV, r, LANES), F32)],
        in_specs=[vm], out_specs=[vm, vm],
        scratch_shapes=[pltpu.SemaphoreType.DMA((N_DEV - 1,)), pltpu.SemaphoreType.DMA((N_DEV - 1,))],
        compiler_params=_cparams(), args=[packed])


def _rope_rot(t):
    w = t.shape[1]
    lane = lax.broadcasted_iota(jnp.int32, t.shape, 1)
    first = (lane % HEAD_DIM) < (HEAD_DIM // 2)
    return jnp.where(first, pltpu.roll(t, w - HEAD_DIM // 2, 1), pltpu.roll(t, HEAD_DIM // 2, 1))


def _in_proj(x, mod3, g_attn, w_in_t, cos_t, sin_t, seq, rider=None):
    t, d = x.shape
    tm = TOKEN_TILE
    per_seq = seq // tm
    rope_lo, rope_hi = 3 * NA_WIDTH, 3 * NA_WIDTH + SW_WIDTH + SW_KV_WIDTH
    n_rep = (rope_hi - rope_lo) // LANES

    def body(x_ref, mod_ref, g_ref, w_ref, cos_ref, sin_ref, h_ref, p_ref):
        r, xn = _rms_stats(x_ref[...])
        shift, scale = mod_ref[0, :, 0:d], mod_ref[0, :, d:2 * d]
        hb = ((xn * g_ref[...]) * (1.0 + scale) + shift).astype(BF16)
        h_ref[...] = hb
        p_ref[:, :rope_lo] = _mm_nt(hb, w_ref[:rope_lo, :]).astype(BF16)
        pr = _mm_nt(hb, w_ref[rope_lo:rope_hi, :])
        cos = jnp.concatenate([cos_ref[...]] * n_rep, axis=1)
        sin = jnp.concatenate([sin_ref[...]] * n_rep, axis=1)
        p_ref[:, rope_lo:rope_hi] = (pr * cos + _rope_rot(pr) * sin).astype(BF16)
        p_ref[:, rope_hi:] = _mm_nt(hb, w_ref[rope_hi:, :]).astype(BF16)

    return _hosted(
        body, rider, name="in_proj", grid=(t // tm,),
        out_shape=[jax.ShapeDtypeStruct((t, d), BF16), jax.ShapeDtypeStruct((t, IN_WIDTH), BF16)],
        in_specs=[pl.BlockSpec((tm, d), lambda i: (i, 0)),
                  pl.BlockSpec((1, 1, 6 * d), lambda i: (i // per_seq, 0, 0)),
                  pl.BlockSpec((1, d), lambda i: (0, 0)),
                  pl.BlockSpec((IN_WIDTH, d), lambda i: (0, 0)),
                  pl.BlockSpec((tm, LANES), lambda i: (i % per_seq, 0)),
                  pl.BlockSpec((tm, LANES), lambda i: (i % per_seq, 0))],
        out_specs=[pl.BlockSpec((tm, d), lambda i: (i, 0)), pl.BlockSpec((tm, IN_WIDTH), lambda i: (i, 0))],
        scratch_shapes=[], compiler_params=_cparams(("arbitrary",), VMEM_BIG),
        args=[x, mod3, g_attn, w_in_t, cos_t, sin_t])


def _na_bias_pattern():
    n_dc = 2 * NA_COLS - 1
    j = np.arange(GRID_W)[:, None]
    m = np.arange(GRID_W * LANES)[None, :]
    q, lane = m // LANES, m % LANES
    k = lane % GRID_W
    cs = np.clip(q - NA_COLS // 2, 0, GRID_W - NA_COLS)
    ok = (k >= cs) & (k < cs + NA_COLS)
    hit = ok & (j < 2 * n_dc) & (lane // GRID_W == j // n_dc) & (k - q + (NA_COLS - 1) == j % n_dc)
    return jnp.asarray(hit.astype(np.float32)), jnp.asarray(np.where(ok, 0.0, NEG).astype(np.float32))


def _na_bias_tiles(rows2, expand, mask):
    n, width = rows2.shape[0], expand.shape[1]
    q_step = 16
    step = q_step * LANES

    def body(r_ref, e_ref, m_ref, o_ref):
        flat = jnp.dot(r_ref[...], e_ref[...], precision=lax.Precision.HIGHEST,
                       preferred_element_type=F32) + m_ref[...]
        for qq in range(q_step):
            o_ref[:, qq, :] = flat[:, qq * LANES:(qq + 1) * LANES]

    return pl.pallas_call(
        body, name="na_bias_tiles", grid=(width // step,),
        out_shape=jax.ShapeDtypeStruct((n, GRID_W, LANES), F32),
        in_specs=[pl.BlockSpec(rows2.shape, lambda i: (0, 0)), pl.BlockSpec((expand.shape[0], step), lambda i: (0, i)),
                  pl.BlockSpec((1, step), lambda i: (0, i))],
        out_specs=pl.BlockSpec((n, q_step, LANES), lambda i: (0, i, 0)),
        compiler_params=_cparams(("arbitrary",)),
    )(rows2, expand, mask)


def _na_prepare(k_ref, v_ref, km, vm):
    lane = lax.broadcasted_iota(jnp.int32, k_ref.shape, 1)
    low = lane < HEAD_DIM
    kv = k_ref[...]
    vv = v_ref[...]
    zero = jnp.zeros_like(kv)
    km[0] = jnp.where(low, kv, zero)
    km[1] = jnp.where(low, zero, kv)
    vm[0] = jnp.where(low, vv, zero)
    vm[1] = jnp.where(low, zero, vv)


def _na_window(r, n_rows):
    rs = jnp.clip(r - NA_ROWS // 2, 0, n_rows - NA_ROWS)
    return rs, r - rs


def _na_pair_window(ref, wrows):
    return jnp.concatenate([ref[0, wrows, :], ref[1, wrows, :]], axis=0)


def _na_scores(q, k2, tp_ref, off):
    bias = jnp.concatenate([tp_ref[h, 2 * w - off + (NA_ROWS - 1)] for h in range(2) for w in range(NA_ROWS // 2)],
                           axis=1)
    return _mm_nt(q, k2) * QK_SCALE + bias


def _pair_softmax(s):
    win = s.shape[1] // 2
    halves = []
    for h in range(2):
        sh = s[:, h * win:(h + 1) * win]
        e = jnp.exp(sh - jnp.max(sh, axis=-1, keepdims=True))
        halves.append(e / jnp.sum(e, axis=-1, keepdims=True))
    return jnp.concatenate(halves, axis=1)


def _na_forward(proj, tiles, batch, seq, rider=None):
    t = proj.shape[0]
    n_rows = seq // GRID_W
    n_pairs = NA_WIDTH // LANES
    win = NA_ROWS * GRID_W

    def body(q_ref, k_ref, v_ref, tp_ref, o_ref, km, vm):
        _na_prepare(k_ref, v_ref, km, vm)

        def scores(r):
            rs, off = _na_window(r, n_rows)
            rows = pl.ds(pl.multiple_of(r * GRID_W, GRID_W), GRID_W)
            wrows = pl.ds(pl.multiple_of(rs * GRID_W, GRID_W), win)
            return rows, wrows, _na_scores(q_ref[rows, :], _na_pair_window(km, wrows), tp_ref, off)

        def finish(rows, wrows, s):
            o_ref[rows, :] = _mm(_pair_softmax(s).astype(BF16), _na_pair_window(vm, wrows))

        def row_group(i, carry):
            for state in [scores(NA_GROUP * i + j) for j in range(NA_GROUP)]:
                finish(*state)
            return carry

        lax.fori_loop(0, n_rows // NA_GROUP, row_group, 0)

    return _hosted(
        body, rider, name="na_forward", grid=(batch, n_pairs),
        out_shape=[jax.ShapeDtypeStruct((t, NA_WIDTH), F32)],
        in_specs=[pl.BlockSpec((seq, LANES), lambda b, p: (b, p)),
                  pl.BlockSpec((seq, LANES), lambda b, p: (b, n_pairs + p)),
                  pl.BlockSpec((seq, LANES), lambda b, p: (b, 2 * n_pairs + p)),
                  pl.BlockSpec((2, 2 * NA_ROWS - 2, GRID_W, LANES), lambda b, p: (p, 0, 0, 0))],
        out_specs=[pl.BlockSpec((seq, LANES), lambda b, p: (b, p))],
        scratch_shapes=[pltpu.VMEM((2, seq, LANES), BF16), pltpu.VMEM((2, seq, LANES), BF16)],
        compiler_params=_cparams(("arbitrary", "arbitrary")), args=[proj, proj, proj, tiles])


def _sw_prepare(kv_ref, g, dst_lo, dst_hi, seq):
    lane = lax.broadcasted_iota(jnp.int32, kv_ref.shape, 1)
    mine = (lane // HEAD_DIM) == g
    kg = jnp.where(mine, kv_ref[...].astype(F32), 0.0)
    kr = pltpu.roll(kg, HEAD_DIM, 1)
    first = g == 0
    zero = jnp.zeros((SW_BLOCK, LANES), BF16)
    for dst, val in ((dst_lo, jnp.where(first, kg, kr)), (dst_hi, jnp.where(first, kr, kg))):
        dst[0:SW_BLOCK, :] = zero
        dst[SW_BLOCK:SW_BLOCK + seq, :] = val.astype(BF16)
        dst[SW_BLOCK + seq:, :] = zero


def _sw_mask(n, seq):
    qi = lax.broadcasted_iota(jnp.int32, (SW_BLOCK, 3 * SW_BLOCK), 0)
    kj = lax.broadcasted_iota(jnp.int32, (SW_BLOCK, 3 * SW_BLOCK), 1)
    kpos = n * SW_BLOCK - SW_BLOCK + kj
    return (jnp.abs(qi + SW_BLOCK - kj) <= SW_BLOCK) & (kpos >= 0) & (kpos < seq)


def _sw_probs(s2, ok, sinks):
    band = s2.shape[1] // 2
    halves, sink_p = [], []
    for i in range(2):
        s = jnp.where(ok, s2[:, i * band:(i + 1) * band], NEG)
        m = jnp.maximum(jnp.max(s, axis=-1, keepdims=True), sinks[i])
        p = jnp.exp(s - m)
        es = jnp.exp(sinks[i] - m)
        den = jnp.sum(p, axis=-1, keepdims=True) + es
        halves.append(p / den)
        sink_p.append(es / den)
    return jnp.concatenate(halves, axis=1), sink_p


def _sw_forward(proj, sink, batch, seq, rider=None):
    t = proj.shape[0]
    n_pairs = SW_WIDTH // LANES
    q_blk = 3 * NA_WIDTH // LANES
    k_blk = q_blk + n_pairs
    n_blocks = seq // SW_BLOCK
    pad = seq + 2 * SW_BLOCK

    def body(sink_ref, q_ref, k_ref, v_ref, o_ref, k_lo, k_hi, v_lo, v_hi):
        hp = pl.program_id(1)
        g = hp // 2
        _sw_prepare(k_ref, g, k_lo, k_hi, seq)
        _sw_prepare(v_ref, g, v_lo, v_hi, seq)

        sinks = (sink_ref[2 * hp], sink_ref[2 * hp + 1])

        def scores(n):
            rows = pl.ds(pl.multiple_of(n * SW_BLOCK, SW_BLOCK), SW_BLOCK)
            wrows = pl.ds(pl.multiple_of(n * SW_BLOCK, SW_BLOCK), 3 * SW_BLOCK)
            k2 = jnp.concatenate([k_lo[wrows, :], k_hi[wrows, :]], axis=0)
            return n, rows, wrows, _mm_nt(q_ref[rows, :], k2) * QK_SCALE

        def finish(n, rows, wrows, s2):
            p, _ = _sw_probs(s2, _sw_mask(n, seq), sinks)
            v2 = jnp.concatenate([v_lo[wrows, :], v_hi[wrows, :]], axis=0)
            o_ref[rows, :] = _mm(p.astype(BF16), v2)

        def block_group(i, carry):
            for state in [scores(SW_GROUP_BLOCKS * i + j) for j in range(SW_GROUP_BLOCKS)]:
                finish(*state)
            return carry

        lax.fori_loop(0, n_blocks // SW_GROUP_BLOCKS, block_group, 0)

    return _hosted(
        body, rider, name="sw_forward", grid=(batch, n_pairs),
        out_shape=[jax.ShapeDtypeStruct((t, SW_WIDTH), F32)],
        in_specs=[pl.BlockSpec(memory_space=pltpu.SMEM),
                  pl.BlockSpec((seq, LANES), lambda b, p: (b, q_blk + p)),
                  pl.BlockSpec((seq, LANES), lambda b, p: (b, k_blk)),
                  pl.BlockSpec((seq, LANES), lambda b, p: (b, k_blk + 1))],
        out_specs=[pl.BlockSpec((seq, LANES), lambda b, p: (b, p))],
        scratch_shapes=[pltpu.VMEM((pad, LANES), BF16)] * 4,
        compiler_params=_cparams(("arbitrary", "arbitrary")), args=[sink, proj, proj, proj])


def _out_proj(oa, ob, g_na, g_sw, w_out, x, mod3, g_ffn, seq):
    t, d = x.shape
    tm = TOKEN_TILE
    per_seq = seq // tm

    def body(oa_ref, ob_ref, gna_ref, gsw_ref, w_ref, x_ref, mod_ref, gf_ref, oab_ref, mix_ref, x1_ref, h2_ref):
        _, na = _rms_stats(oa_ref[...])
        _, nb = _rms_stats(ob_ref[...])
        oab = jnp.concatenate([na * gna_ref[...], nb * gsw_ref[...]], axis=1).astype(BF16)
        oab_ref[...] = oab
        mix = _mm(oab, w_ref[...])
        mix_ref[...] = mix
        gate_a = mod_ref[0, :, 2 * d:3 * d]
        shift_f, scale_f = mod_ref[0, :, 3 * d:4 * d], mod_ref[0, :, 4 * d:5 * d]
        x1 = x_ref[...] + gate_a * mix
        x1_ref[...] = x1
        _, xn = _rms_stats(x1)
        h2_ref[...] = ((xn * gf_ref[...]) * (1.0 + scale_f) + shift_f).astype(BF16)

    tile = lambda w: pl.BlockSpec((tm, w), lambda i: (i, 0))
    vec = lambda w: pl.BlockSpec((1, w), lambda i: (0, 0))
    return pl.pallas_call(
        body, name="out_proj", grid=(t // tm,),
        out_shape=(jax.ShapeDtypeStruct((t, d), BF16), jax.ShapeDtypeStruct((t, d), F32),
                   jax.ShapeDtypeStruct((t, d), F32), jax.ShapeDtypeStruct((t, d), BF16)),
        in_specs=[tile(NA_WIDTH), tile(SW_WIDTH), vec(NA_WIDTH), vec(SW_WIDTH),
                  pl.BlockSpec((d, d), lambda i: (0, 0)), tile(d),
                  pl.BlockSpec((1, 1, 6 * d), lambda i: (i // per_seq, 0, 0)), vec(d)],
        out_specs=(tile(d), tile(d), tile(d), tile(d)),
        compiler_params=_cparams(("arbitrary",), VMEM_BIG),
    )(oa, ob, g_na, g_sw, w_out, x, mod3, g_ffn)


def _up_proj(h2, w_up_halves, rider=None):
    t, d = h2.shape
    tm = TOKEN_TILE
    w_a, w_b = w_up_halves
    half, wcol = w_a.shape[1], w_a.shape[2]

    def body(h_ref, wa_ref, wb_ref, u_ref):
        u_ref[0] = (_mm(h_ref[:, :half], wa_ref[0]) + _mm(h_ref[:, half:], wb_ref[0])).astype(BF16)

    w_spec = pl.BlockSpec((1, half, wcol), lambda j, i: (j, 0, 0))
    return _hosted(
        body, rider, name="up_proj", grid=(N_SHARD, t // tm),
        out_shape=[jax.ShapeDtypeStruct((2, t, D_FF), BF16)],
        in_specs=[pl.BlockSpec((tm, d), lambda j, i: (i, 0)), w_spec, w_spec],
        out_specs=[pl.BlockSpec((1, tm, wcol), lambda j, i: (j // 2, i, j % 2))],
        scratch_shapes=[], compiler_params=_cparams(("arbitrary", "arbitrary"), VMEM_BIG), args=[h2, w_a, w_b])


def _taps_chunk(load, s, rows, seq):
    halo = 2 * SUBLANES
    cur = load(s, rows)
    above = load(pl.multiple_of(jnp.maximum(s - halo, 0), halo), halo)
    below = load(pl.multiple_of(jnp.minimum(s + rows, seq - halo), halo), halo)
    up = jnp.where(s > 0, above[halo - 1:halo, :], 0.0)
    dn = jnp.where(s + rows < seq, below[0:1, :], 0.0)
    row = lax.broadcasted_iota(jnp.int32, cur.shape, 0)
    prev = jnp.where(row == 0, up, pltpu.roll(cur, 1, 0))
    nxt = jnp.where(row == rows - 1, dn, pltpu.roll(cur, rows - 1, 0))
    return cur, prev, nxt


def _conv_gate(u, conv_w, conv_b, batch, seq):
    t = u.shape[1]
    cw = FF_TILE
    rows = CONV_CHUNK

    def body(u_ref, w_ref, b_ref, a_ref):
        def chunk(i, carry):
            s = pl.multiple_of(i * rows, rows)
            gt, prev, nxt = _taps_chunk(lambda at, n: u_ref[1, pl.ds(at, n), :].astype(F32), s, rows, seq)
            gc = prev * w_ref[0:1, :] + gt * w_ref[1:2, :] + nxt * w_ref[2:3, :] + b_ref[...]
            a_ref[pl.ds(s, rows), :] = ((gc * _sigmoid(gc)) * u_ref[0, pl.ds(s, rows), :].astype(F32)).astype(BF16)
            return carry

        lax.fori_loop(0, seq // rows, chunk, 0)

    return pl.pallas_call(
        body, name="conv_gate", grid=(batch, D_FF // cw),
        out_shape=jax.ShapeDtypeStruct((t, D_FF), BF16),
        in_specs=[pl.BlockSpec((2, seq, cw), lambda b, j: (0, b, j)),
                  pl.BlockSpec((3, cw), lambda b, j: (0, j)), pl.BlockSpec((1, cw), lambda b, j: (0, j))],
        out_specs=pl.BlockSpec((seq, cw), lambda b, j: (b, j)),
        compiler_params=_cparams(("arbitrary", "arbitrary"), VMEM_BIG),
    )(u, conv_w, conv_b)


def _down_and_loss(a, w_down, x1, mod3, g_final, target, seq):
    t, d = x1.shape
    tm = TOKEN_TILE
    per_seq = seq // tm
    batch = t // seq

    def body(a_ref, w_ref, x1_ref, mod_ref, g_ref, tgt_ref, dx2_ref, dffn_ref, loss_ref, dgate_ref, dg_ref):
        i = pl.program_id(0)
        f = _mm(a_ref[...], w_ref[...])
        gate_f = mod_ref[0, :, 5 * d:6 * d]
        x2 = x1_ref[...] + gate_f * f
        r, xn = _rms_stats(x2)
        err = xn * g_ref[...] - tgt_ref[...]
        part = 0.5 * jnp.sum(jnp.mean(err * err, axis=-1, keepdims=True))
        dy = err / d
        dx2 = _rms_bwd(dy * g_ref[...], xn, r)
        dx2_ref[...] = dx2
        dffn_ref[...] = (dx2 * gate_f).astype(BF16)

        @pl.when(i == 0)
        def _():
            loss_ref[...] = jnp.zeros_like(loss_ref)
            dg_ref[...] = jnp.zeros_like(dg_ref)

        @pl.when(i % per_seq == 0)
        def _():
            dgate_ref[...] = jnp.zeros_like(dgate_ref)

        loss_ref[...] += part
        dg_ref[...] += jnp.sum(dy * xn, axis=0, keepdims=True)
        dgate_ref[0] += jnp.sum(dx2 * f, axis=0, keepdims=True)

    tile = lambda w: pl.BlockSpec((tm, w), lambda i: (i, 0))
    return pl.pallas_call(
        body, name="down_loss", grid=(t // tm,),
        out_shape=(jax.ShapeDtypeStruct((t, d), F32), jax.ShapeDtypeStruct((t, d), BF16),
                   jax.ShapeDtypeStruct((SUBLANES, LANES), F32), jax.ShapeDtypeStruct((batch, 1, d), F32),
                   jax.ShapeDtypeStruct((1, d), F32)),
        in_specs=[tile(D_FF), pl.BlockSpec((D_FF, d), lambda i: (0, 0)), tile(d),
                  pl.BlockSpec((1, 1, 6 * d), lambda i: (i // per_seq, 0, 0)),
                  pl.BlockSpec((1, d), lambda i: (0, 0)), tile(d)],
        out_specs=(tile(d), tile(d), pl.BlockSpec((SUBLANES, LANES), lambda i: (0, 0)),
                   pl.BlockSpec((1, 1, d), lambda i: (i // per_seq, 0, 0)), pl.BlockSpec((1, d), lambda i: (0, 0))),
        compiler_params=_cparams(("arbitrary",), VMEM_BIG),
    )(a, w_down, x1, mod3, g_final, target)


def _down_weight_grad(a, dffn):
    t, dff = a.shape
    d = dffn.shape[1]
    tk = TOKEN_TILE
    n_k = t // tk

    def body(a_ref, df_ref, g_ref, gb_ref):
        k = pl.program_id(0)

        @pl.when(k == 0)
        def _():
            g_ref[...] = jnp.zeros_like(g_ref)

        g_ref[...] += _mm_tn(a_ref[...], df_ref[...])

        @pl.when(k == n_k - 1)
        def _():
            gb_ref[...] = g_ref[...].astype(BF16)

    whole = pl.BlockSpec((dff, d), lambda k: (0, 0))
    return pl.pallas_call(
        body, name="down_weight_grad", grid=(n_k,),
        out_shape=(jax.ShapeDtypeStruct((dff, d), F32), jax.ShapeDtypeStruct((dff, d), BF16)),
        in_specs=[pl.BlockSpec((tk, dff), lambda k: (k, 0)), pl.BlockSpec((tk, d), lambda k: (k, 0))],
        out_specs=(whole, whole),
        compiler_params=_cparams(("arbitrary",), VMEM_BIG),
    )(a, dffn)


def _ffn_backward(dffn, w_down, u, conv_w, conv_b, batch, seq, rider=None):
    t, d = dffn.shape
    cw = FF_TILE
    rows = CONV_CHUNK

    def body(df_ref, wd_ref, u_ref, w_ref, b_ref, du_ref, gcw_ref, gcb_ref, da_scr, dgc_scr):
        b = pl.program_id(1)
        da_scr[...] = _mm_nt(df_ref[...], wd_ref[...])

        @pl.when(b == 0)
        def _():
            gcw_ref[...] = jnp.zeros_like(gcw_ref)
            gcb_ref[...] = jnp.zeros_like(gcb_ref)

        def fold(v):
            return jnp.sum(v.reshape(rows // SUBLANES, SUBLANES, cw), axis=0)

        def chunk(i, carry):
            s = pl.multiple_of(i * rows, rows)
            here = pl.ds(s, rows)
            gt, prev, nxt = _taps_chunk(lambda at, n: u_ref[1, pl.ds(at, n), :].astype(F32), s, rows, seq)
            val, da = u_ref[0, here, :].astype(F32), da_scr[here, :]
            gc = prev * w_ref[0:1, :] + gt * w_ref[1:2, :] + nxt * w_ref[2:3, :] + b_ref[...]
            sg = _sigmoid(gc)
            sl = gc * sg
            du_ref[0, here, :] = (da * sl).astype(BF16)
            dgc = (da * val) * (sg * (1.0 + gc * (1.0 - sg)))
            dgc_scr[here, :] = dgc
            cb, c0, c1, c2 = carry
            return cb + fold(dgc), c0 + fold(dgc * prev), c1 + fold(dgc * gt), c2 + fold(dgc * nxt)

        zero = jnp.zeros((SUBLANES, cw), F32)
        cb, c0, c1, c2 = lax.fori_loop(0, seq // rows, chunk, (zero, zero, zero, zero))
        gcb_ref[...] += jnp.sum(cb, axis=0, keepdims=True)
        gcw_ref[0:1, :] += jnp.sum(c0, axis=0, keepdims=True)
        gcw_ref[1:2, :] += jnp.sum(c1, axis=0, keepdims=True)
        gcw_ref[2:3, :] += jnp.sum(c2, axis=0, keepdims=True)

        def chunk2(i, carry):
            s = pl.multiple_of(i * rows, rows)
            dgc, dprev, dnxt = _taps_chunk(lambda at, n: dgc_scr[pl.ds(at, n), :], s, rows, seq)
            du_ref[1, pl.ds(s, rows), :] = (dnxt * w_ref[0:1, :] + dgc * w_ref[1:2, :]
                                            + dprev * w_ref[2:3, :]).astype(BF16)
            return carry

        lax.fori_loop(0, seq // rows, chunk2, 0)

    return _hosted(
        body, rider, name="ffn_backward", grid=(D_FF // cw, batch),
        out_shape=[jax.ShapeDtypeStruct((2, t, D_FF), BF16),
                   jax.ShapeDtypeStruct((3, D_FF), F32), jax.ShapeDtypeStruct((1, D_FF), F32)],
        in_specs=[pl.BlockSpec((seq, d), lambda j, b: (b, 0)), pl.BlockSpec((cw, d), lambda j, b: (j, 0)),
                  pl.BlockSpec((2, seq, cw), lambda j, b: (0, b, j)),
                  pl.BlockSpec((3, cw), lambda j, b: (0, j)), pl.BlockSpec((1, cw), lambda j, b: (0, j))],
        out_specs=[pl.BlockSpec((2, seq, cw), lambda j, b: (0, b, j)),
                   pl.BlockSpec((3, cw), lambda j, b: (0, j)), pl.BlockSpec((1, cw), lambda j, b: (0, j))],
        scratch_shapes=[pltpu.VMEM((seq, cw), F32), pltpu.VMEM((seq, cw), F32)],
        compiler_params=_cparams(("arbitrary", "arbitrary"), VMEM_BIG), args=[dffn, w_down, u, conv_w, conv_b])


def _up_backward(du, w_up, x1, mod3, g_ffn, dx2, mix, seq, rider=None):
    _, t, _ = du.shape
    d = x1.shape[1]
    tm = TOKEN_TILE // 2
    per_seq = seq // tm
    batch = t // seq
    w_a, w_b = w_up
    half, wcol = w_a.shape[1], w_a.shape[2]

    def body(du_ref, wa_ref, wb_ref, x1_ref, mod_ref, g_ref, dx2_ref, mix_ref,
             dx1_ref, dmix_ref, dsh_ref, dsc_ref, dga_ref, dg_ref):
        i = pl.program_id(0)
        parts = []
        for w_ref in (wa_ref, wb_ref):
            acc = jnp.zeros((tm, half), F32)
            for j in range(N_SHARD):
                acc = acc + _mm_nt(du_ref[j // 2, :, (j % 2) * wcol:(j % 2 + 1) * wcol], w_ref[j])
            parts.append(acc)
        dh = jnp.concatenate(parts, axis=1)
        gate_a = mod_ref[0, :, 2 * d:3 * d]
        scale_f = mod_ref[0, :, 4 * d:5 * d]
        r, xn = _rms_stats(x1_ref[...])
        xg = xn * g_ref[...]
        dxg = dh * (1.0 + scale_f)
        dx1 = dx2_ref[...] + _rms_bwd(dxg * g_ref[...], xn, r)
        dx1_ref[...] = dx1
        dmix_ref[...] = (dx1 * gate_a).astype(BF16)

        @pl.when(i == 0)
        def _():
            dg_ref[...] = jnp.zeros_like(dg_ref)

        @pl.when(i % per_seq == 0)
        def _():
            dsh_ref[...] = jnp.zeros_like(dsh_ref)
            dsc_ref[...] = jnp.zeros_like(dsc_ref)
            dga_ref[...] = jnp.zeros_like(dga_ref)

        dg_ref[...] += jnp.sum(dxg * xn, axis=0, keepdims=True)
        dsh_ref[0] += jnp.sum(dh, axis=0, keepdims=True)
        dsc_ref[0] += jnp.sum(dh * xg, axis=0, keepdims=True)
        dga_ref[0] += jnp.sum(dx1 * mix_ref[...], axis=0, keepdims=True)

    tile = lambda w: pl.BlockSpec((tm, w), lambda i: (i, 0))
    per_b = pl.BlockSpec((1, 1, d), lambda i: (i // per_seq, 0, 0))
    small = jax.ShapeDtypeStruct((batch, 1, d), F32)
    return _hosted(
        body, rider, name="up_backward", grid=(t // tm,),
        out_shape=[jax.ShapeDtypeStruct((t, d), F32), jax.ShapeDtypeStruct((t, d), BF16), small, small, small,
                   jax.ShapeDtypeStruct((1, d), F32)],
        in_specs=[pl.BlockSpec((2, tm, D_FF), lambda i: (0, i, 0)),
                  pl.BlockSpec((N_SHARD, half, wcol), lambda i: (0, 0, 0)),
                  pl.BlockSpec((N_SHARD, half, wcol), lambda i: (0, 0, 0)), tile(d),
                  pl.BlockSpec((1, 1, 6 * d), lambda i: (i // per_seq, 0, 0)),
                  pl.BlockSpec((1, d), lambda i: (0, 0)), tile(d), tile(d)],
        out_specs=[tile(d), tile(d), per_b, per_b, per_b, pl.BlockSpec((1, d), lambda i: (0, 0))],
        scratch_shapes=[], compiler_params=_cparams(("arbitrary",), VMEM_BIG),
        args=[du, w_a, w_b, x1, mod3, g_ffn, dx2, mix])


def _up_weight_grad(h2, du, rider=None):
    t, d = h2.shape
    tk = TOKEN_TILE
    wcol = D_FF // 2
    half = d // 2
    n_k = t // tk

    def body(h_ref, du_ref, ga_ref, gb_ref, ga16_ref, gb16_ref):
        k = pl.program_id(1)

        @pl.when(k == 0)
        def _():
            ga_ref[...] = jnp.zeros_like(ga_ref)
            gb_ref[...] = jnp.zeros_like(gb_ref)

        du = du_ref[0]
        ga_ref[0] += _mm_tn(h_ref[:, :half], du)
        gb_ref[0] += _mm_tn(h_ref[:, half:], du)

        @pl.when(k == n_k - 1)
        def _():
            ga16_ref[...] = ga_ref[...].astype(BF16)
            gb16_ref[...] = gb_ref[...].astype(BF16)

    g_spec = pl.BlockSpec((1, half, wcol), lambda j, k: (j, 0, 0))
    f32_out = jax.ShapeDtypeStruct((N_SHARD, half, wcol), F32)
    b16_out = jax.ShapeDtypeStruct((N_SHARD, half, wcol), BF16)
    return _hosted(
        body, rider, name="up_weight_grad", grid=(N_SHARD, n_k),
        out_shape=[f32_out, f32_out, b16_out, b16_out],
        in_specs=[pl.BlockSpec((tk, d), lambda j, k: (k, 0)),
                  pl.BlockSpec((1, tk, wcol), lambda j, k: (j // 2, k, j % 2))],
        out_specs=[g_spec, g_spec, g_spec, g_spec], scratch_shapes=[],
        compiler_params=_cparams(("arbitrary", "arbitrary"), VMEM_BIG), args=[h2, du])


def _out_backward(dmix, w_out, oab, oa, ob, g_na, g_sw):
    t, d = dmix.shape
    tm = TOKEN_TILE
    hw = NA_WIDTH

    def body(dm_ref, w_ref, oab_ref, oa_ref, ob_ref, gna_ref, gsw_ref,
             doa_ref, dob_ref, gw_ref, gwb_ref, dgna_ref, dgsw_ref):
        @pl.when(pl.program_id(0) == 0)
        def _():
            gw_ref[...] = jnp.zeros_like(gw_ref)
            dgna_ref[...] = jnp.zeros_like(dgna_ref)
            dgsw_ref[...] = jnp.zeros_like(dgsw_ref)

        dm = dm_ref[...]
        gw_ref[...] += _mm_tn(oab_ref[...], dm)

        @pl.when(pl.program_id(0) == t // tm - 1)
        def _():
            gwb_ref[...] = gw_ref[...].astype(BF16)

        do = _mm_nt(dm, w_ref[...])
        for raw_ref, g_ref, dst_ref, dg_ref, lo in ((oa_ref, gna_ref, doa_ref, dgna_ref, 0),
                                                     (ob_ref, gsw_ref, dob_ref, dgsw_ref, hw)):
            r, xn = _rms_stats(raw_ref[...])
            dpart = do[:, lo:lo + hw]
            dg_ref[...] += jnp.sum(dpart * xn, axis=0, keepdims=True)
            dst_ref[...] = _rms_bwd(dpart * g_ref[...], xn, r).astype(BF16)

    tile = lambda w: pl.BlockSpec((tm, w), lambda i: (i, 0))
    vec = lambda w: pl.BlockSpec((1, w), lambda i: (0, 0))
    return pl.pallas_call(
        body, name="out_backward", grid=(t // tm,),
        out_shape=(jax.ShapeDtypeStruct((t, hw), BF16), jax.ShapeDtypeStruct((t, hw), BF16),
                   jax.ShapeDtypeStruct((d, d), F32), jax.ShapeDtypeStruct((d, d), BF16),
                   jax.ShapeDtypeStruct((1, hw), F32), jax.ShapeDtypeStruct((1, hw), F32)),
        in_specs=[tile(d), pl.BlockSpec((d, d), lambda i: (0, 0)), tile(d), tile(hw), tile(hw), vec(hw), vec(hw)],
        out_specs=(tile(hw), tile(hw), pl.BlockSpec((d, d), lambda i: (0, 0)), pl.BlockSpec((d, d), lambda i: (0, 0)),
                   vec(hw), vec(hw)),
        compiler_params=_cparams(("arbitrary",), VMEM_BIG),
    )(dmix, w_out, oab, oa, ob, g_na, g_sw)


def _na_backward(proj, d_o, tiles, batch, seq, rider=None):
    t = proj.shape[0]
    n_rows = seq // GRID_W
    n_pairs = NA_WIDTH // LANES
    win = NA_ROWS * GRID_W
    n_tiles = 2 * NA_ROWS - 2

    def body(q_ref, k_ref, v_ref, do_ref, tp_ref, dq_ref, dk_ref, dv_ref, dtp_ref, km, vm, dk_acc, dv_acc):
        @pl.when(pl.program_id(1) == 0)
        def _():
            dtp_ref[...] = jnp.zeros_like(dtp_ref)

        _na_prepare(k_ref, v_ref, km, vm)
        dk_acc[...] = jnp.zeros_like(dk_acc)
        dv_acc[...] = jnp.zeros_like(dv_acc)
        low = lax.broadcasted_iota(jnp.int32, (win, LANES), 1) < HEAD_DIM

        def scores(r):
            rs, off = _na_window(r, n_rows)
            rows = pl.ds(pl.multiple_of(r * GRID_W, GRID_W), GRID_W)
            wrows = pl.ds(pl.multiple_of(rs * GRID_W, GRID_W), win)
            q, do = q_ref[rows, :], do_ref[rows, :]
            k2 = _na_pair_window(km, wrows)
            s = _na_scores(q, k2, tp_ref, off)
            dp = _mm_nt(do, _na_pair_window(vm, wrows))
            return rows, wrows, off, q, do, k2, s, dp

        def finish(rows, wrows, off, q, do, k2, s, dp):
            p = _pair_softmax(s)
            parts = []
            for h in range(2):
                ph, dph = p[:, h * win:(h + 1) * win], dp[:, h * win:(h + 1) * win]
                dsh = ph * (dph - jnp.sum(ph * dph, axis=-1, keepdims=True))
                for w in range(NA_ROWS // 2):
                    dtp_ref[h, 2 * w - off + (NA_ROWS - 1)] += dsh[:, w * LANES:(w + 1) * LANES]
                parts.append(dsh)
            dsb = (jnp.concatenate(parts, axis=1) * QK_SCALE).astype(BF16)
            dq_ref[rows, :] = _mm(dsb, k2).astype(BF16)
            dk2 = _mm_tn(dsb, q)
            dv2 = _mm_tn(p.astype(BF16), do)
            dk_acc[wrows, :] += jnp.where(low, dk2[:win], dk2[win:])
            dv_acc[wrows, :] += jnp.where(low, dv2[:win], dv2[win:])

        def row_group(i, carry):
            for state in [scores(NA_GROUP * i + j) for j in range(NA_GROUP)]:
                finish(*state)
            return carry

        lax.fori_loop(0, n_rows // NA_GROUP, row_group, 0)
        dk_ref[...] = dk_acc[...].astype(BF16)
        dv_ref[...] = dv_acc[...].astype(BF16)

    blk = lambda off: pl.BlockSpec((seq, LANES), lambda p, b: (b, off + p))
    out = jax.ShapeDtypeStruct((t, NA_WIDTH), BF16)
    return _hosted(
        body, rider, name="na_backward", grid=(n_pairs, batch),
        out_shape=[out, out, out, jax.ShapeDtypeStruct(tiles.shape, F32)],
        in_specs=[blk(0), blk(n_pairs), blk(2 * n_pairs), blk(0),
                  pl.BlockSpec((2, n_tiles, GRID_W, LANES), lambda p, b: (p, 0, 0, 0))],
        out_specs=[blk(0), blk(0), blk(0), pl.BlockSpec((2, n_tiles, GRID_W, LANES), lambda p, b: (p, 0, 0, 0))],
        scratch_shapes=[pltpu.VMEM((2, seq, LANES), BF16), pltpu.VMEM((2, seq, LANES), BF16),
                        pltpu.VMEM((seq, LANES), F32), pltpu.VMEM((seq, LANES), F32)],
        compiler_params=_cparams(("arbitrary", "arbitrary")), args=[proj, proj, proj, d_o, tiles])


def _na_bias_grad(dtiles, expand):
    n = dtiles.shape[0]

    def body(t_ref, e_ref, o_ref):
        flat = jnp.concatenate([t_ref[:, qq, :] for qq in range(GRID_W)], axis=1)
        o_ref[...] = lax.dot_general(flat, e_ref[...], (((1,), (1,)), ((), ())),
                                     precision=lax.Precision.HIGHEST, preferred_element_type=F32)

    return pl.pallas_call(
        body, name="na_bias_grad",
        out_shape=jax.ShapeDtypeStruct((n, expand.shape[0]), F32),
        compiler_params=_cparams(vmem=VMEM_BIG),
    )(dtiles, expand)


def _sw_backward(proj, d_o, sink, batch, seq, rider=None):
    t = proj.shape[0]
    n_pairs = SW_WIDTH // LANES
    q_blk = 3 * NA_WIDTH // LANES
    k_blk = q_blk + n_pairs
    n_blocks = seq // SW_BLOCK
    pad = seq + 2 * SW_BLOCK

    def body(sink_ref, q_ref, k_ref, v_ref, do_ref, dq_ref, dk_ref, dv_ref, dsk_ref,
             k_lo, k_hi, v_lo, v_hi, dk_loc, dv_loc, dk_tot, dv_tot):
        hp = pl.program_id(1)
        g = hp // 2
        _sw_prepare(k_ref, g, k_lo, k_hi, seq)
        _sw_prepare(v_ref, g, v_lo, v_hi, seq)
        dk_loc[...] = jnp.zeros_like(dk_loc)
        dv_loc[...] = jnp.zeros_like(dv_loc)

        @pl.when(hp == 0)
        def _():
            dk_tot[...] = jnp.zeros_like(dk_tot)
            dv_tot[...] = jnp.zeros_like(dv_tot)

        band = 3 * SW_BLOCK
        low = lax.broadcasted_iota(jnp.int32, (band, LANES), 1) < HEAD_DIM

        sinks = (sink_ref[2 * hp], sink_ref[2 * hp + 1])

        def scores(n):
            rows = pl.ds(pl.multiple_of(n * SW_BLOCK, SW_BLOCK), SW_BLOCK)
            wrows = pl.ds(pl.multiple_of(n * SW_BLOCK, SW_BLOCK), band)
            qb, do = q_ref[rows, :], do_ref[rows, :]
            k2 = jnp.concatenate([k_lo[wrows, :], k_hi[wrows, :]], axis=0)
            v2 = jnp.concatenate([v_lo[wrows, :], v_hi[wrows, :]], axis=0)
            return n, rows, wrows, qb, do, k2, _mm_nt(qb, k2) * QK_SCALE, _mm_nt(do, v2)

        def finish(sink_acc, n, rows, wrows, qb, do, k2, s2, dp):
            p, ps = _sw_probs(s2, _sw_mask(n, seq), sinks)
            parts, new = [], []
            for i in range(2):
                ph, dph = p[:, i * band:(i + 1) * band], dp[:, i * band:(i + 1) * band]
                delta = jnp.sum(ph * dph, axis=-1, keepdims=True)
                parts.append(ph * (dph - delta))
                new.append(sink_acc[i] - ps[i] * delta)
            dsb = (jnp.concatenate(parts, axis=1) * QK_SCALE).astype(BF16)
            dq_ref[rows, :] = _mm(dsb, k2)
            dk2 = _mm_tn(dsb, qb)
            dv2 = _mm_tn(p.astype(BF16), do)
            dk_loc[wrows, :] += jnp.where(low, dk2[:band], dk2[band:])
            dv_loc[wrows, :] += jnp.where(low, dv2[:band], dv2[band:])
            return tuple(new)

        def block_group(i, carry):
            for state in [scores(SW_GROUP_BLOCKS * i + j) for j in range(SW_GROUP_BLOCKS)]:
                carry = finish(carry, *state)
            return carry

        zero = jnp.zeros((SW_BLOCK, 1), F32)
        s0, s1 = lax.fori_loop(0, n_blocks // SW_GROUP_BLOCKS, block_group, (zero, zero))
        row = lax.broadcasted_iota(jnp.int32, (SUBLANES, LANES), 0)
        dsk_ref[0, 0] = jnp.where(row == 0, jnp.sum(s0), jnp.where(row == 1, jnp.sum(s1), 0.0))

        lane_s = lax.broadcasted_iota(jnp.int32, (seq, LANES), 1)
        mine_g = (lane_s // HEAD_DIM) == g
        for loc, tot in ((dk_loc, dk_tot), (dv_loc, dv_tot)):
            part = loc[SW_BLOCK:SW_BLOCK + seq, :]
            tot[...] += jnp.where(mine_g, part + pltpu.roll(part, HEAD_DIM, 1), 0.0)

        @pl.when(hp == n_pairs - 1)
        def _():
            dk_ref[...] = dk_tot[...]
            dv_ref[...] = dv_tot[...].astype(BF16)

    return _hosted(
        body, rider, name="sw_backward", grid=(batch, n_pairs),
        out_shape=[jax.ShapeDtypeStruct((t, SW_WIDTH), F32), jax.ShapeDtypeStruct((t, LANES), F32),
                   jax.ShapeDtypeStruct((t, LANES), BF16), jax.ShapeDtypeStruct((batch, n_pairs, SUBLANES, LANES), F32)],
        in_specs=[pl.BlockSpec(memory_space=pltpu.SMEM),
                  pl.BlockSpec((seq, LANES), lambda b, p: (b, q_blk + p)),
                  pl.BlockSpec((seq, LANES), lambda b, p: (b, k_blk)),
                  pl.BlockSpec((seq, LANES), lambda b, p: (b, k_blk + 1)),
                  pl.BlockSpec((seq, LANES), lambda b, p: (b, p))],
        out_specs=[pl.BlockSpec((seq, LANES), lambda b, p: (b, p)), pl.BlockSpec((seq, LANES), lambda b, p: (b, 0)),
                   pl.BlockSpec((seq, LANES), lambda b, p: (b, 0)),
                   pl.BlockSpec((1, 1, SUBLANES, LANES), lambda b, p: (b, p, 0, 0))],
        scratch_shapes=[pltpu.VMEM((pad, LANES), BF16)] * 4 + [pltpu.VMEM((pad, LANES), F32)] * 2
        + [pltpu.VMEM((seq, LANES), F32)] * 2,
        compiler_params=_cparams(("arbitrary", "arbitrary")), args=[sink, proj, proj, proj, d_o])


def _in_backward(dqkv_a, dq_b, dk_b, dv_b, w_in_t, h1, x, mod3, g_attn, dx1, cos_t, sin_t, seq):
    t, d = x.shape
    tm = TOKEN_TILE // 2
    per_seq = seq // tm
    batch = t // seq
    dqa, dka, dva = dqkv_a
    n_q = SW_WIDTH // LANES

    def body(dqa_ref, dka_ref, dva_ref, dqb_ref, dkb_ref, dvb_ref, w_ref, h_ref, x_ref, mod_ref, g_ref, dx1_ref,
             cos_ref, sin_ref, dx_ref, gw_ref, gwb_ref, dsh_ref, dsc_ref, dg_ref):
        i = pl.program_id(0)

        @pl.when(i == 0)
        def _():
            gw_ref[...] = jnp.zeros_like(gw_ref)
            dg_ref[...] = jnp.zeros_like(dg_ref)

        @pl.when(i % per_seq == 0)
        def _():
            dsh_ref[...] = jnp.zeros_like(dsh_ref)
            dsc_ref[...] = jnp.zeros_like(dsc_ref)

        dr = jnp.concatenate([dqb_ref[...], dkb_ref[...]], axis=1)
        cos = jnp.concatenate([cos_ref[...]] * (n_q + 1), axis=1)
        sin = jnp.concatenate([sin_ref[...]] * (n_q + 1), axis=1)
        dr = dr * cos + _rope_rot(dr * sin)
        dproj = jnp.concatenate([dqa_ref[...], dka_ref[...], dva_ref[...], dr.astype(BF16), dvb_ref[...]], axis=1)
        gw_ref[...] += _mm_tn(dproj, h_ref[...])

        @pl.when(i == t // tm - 1)
        def _():
            gwb_ref[...] = gw_ref[...].astype(BF16)

        dh = _mm(dproj, w_ref[...])
        scale = mod_ref[0, :, d:2 * d]
        r, xn = _rms_stats(x_ref[...])
        xg = xn * g_ref[...]
        dxg = dh * (1.0 + scale)
        dx_ref[...] = dx1_ref[...] + _rms_bwd(dxg * g_ref[...], xn, r)
        dg_ref[...] += jnp.sum(dxg * xn, axis=0, keepdims=True)
        dsh_ref[0] += jnp.sum(dh, axis=0, keepdims=True)
        dsc_ref[0] += jnp.sum(dh * xg, axis=0, keepdims=True)

    tile = lambda w: pl.BlockSpec((tm, w), lambda i: (i, 0))
    per_b = pl.BlockSpec((1, 1, d), lambda i: (i // per_seq, 0, 0))
    small = jax.ShapeDtypeStruct((batch, 1, d), F32)
    rope = pl.BlockSpec((tm, LANES), lambda i: (i % per_seq, 0))
    return pl.pallas_call(
        body, name="in_backward", grid=(t // tm,),
        out_shape=(jax.ShapeDtypeStruct((t, d), F32), jax.ShapeDtypeStruct((IN_WIDTH, d), F32),
                   jax.ShapeDtypeStruct((IN_WIDTH, d), BF16), small, small, jax.ShapeDtypeStruct((1, d), F32)),
        in_specs=[tile(NA_WIDTH), tile(NA_WIDTH), tile(NA_WIDTH), tile(SW_WIDTH), tile(LANES), tile(LANES),
                  pl.BlockSpec((IN_WIDTH, d), lambda i: (0, 0)), tile(d), tile(d),
                  pl.BlockSpec((1, 1, 6 * d), lambda i: (i // per_seq, 0, 0)),
                  pl.BlockSpec((1, d), lambda i: (0, 0)), tile(d), rope, rope],
        out_specs=(tile(d), pl.BlockSpec((IN_WIDTH, d), lambda i: (0, 0)), pl.BlockSpec((IN_WIDTH, d), lambda i: (0, 0)),
                   per_b, per_b, pl.BlockSpec((1, d), lambda i: (0, 0))),
        compiler_params=_cparams(("arbitrary",), VMEM_BIG),
    )(dqa, dka, dva, dq_b, dk_b, dv_b, w_in_t, h1, x, mod3, g_attn, dx1, cos_t, sin_t)


def _ada_weight_grad(sc_all, dmod_cols):
    d = sc_all.shape[1]
    ncol = dmod_cols.shape[1]

    def body(s_ref, m_ref, o_ref):
        o_ref[...] = _mm_tn(s_ref[...].astype(BF16), m_ref[...].astype(BF16))

    return pl.pallas_call(
        body, name="ada_weight_grad",
        out_shape=jax.ShapeDtypeStruct((d, ncol), F32),
        compiler_params=_cparams(vmem=VMEM_BIG),
    )(sc_all, dmod_cols)


def _row_tile(rows, cols):
    target = max(SUBLANES, (1 << 20) // (4 * cols))
    best = rows
    for cand in range(SUBLANES, rows + 1, SUBLANES):
        if rows % cand == 0 and cand <= target:
            best = cand
    return best if rows % SUBLANES == 0 else rows


def _sum_slots(parts, name):
    n = len(parts)
    _, rows, cols = parts[0][0].shape
    tr = _row_tile(rows, cols)
    per = rows // tr

    def body(*refs):
        o_ref = refs[-1]
        for q in range(n):
            @pl.when(pl.program_id(0) == q)
            def _(q=q):
                p_ref, own_ref = refs[2 * q], refs[2 * q + 1]
                o_ref[...] = ((own_ref[...] + p_ref[0].astype(F32)) + p_ref[1].astype(F32)) + p_ref[2].astype(F32)

    in_specs, args = [], []
    for q, (recv, own) in enumerate(parts):
        in_specs.append(pl.BlockSpec((N_SHARD - 1, tr, cols), lambda p, i, q=q: (0, jnp.where(p == q, i, 0), 0)))
        in_specs.append(pl.BlockSpec((tr, cols), lambda p, i, q=q: (jnp.where(p == q, i, 0), 0)))
        args += [recv, own]
    return pl.pallas_call(
        body, name=name, grid=(n, per),
        out_shape=jax.ShapeDtypeStruct((n * rows, cols), F32),
        in_specs=in_specs, out_specs=pl.BlockSpec((tr, cols), lambda p, i: (p * per + i, 0)),
        compiler_params=_cparams(("arbitrary", "arbitrary")),
    )(*args)


def _adamw(w, grads, m, v, name):
    rows, cols = w.shape
    tr = _row_tile(rows, cols)
    ng = len(grads)

    def body(*refs):
        w_ref = refs[0]
        g_refs = refs[1:1 + ng]
        m_ref, v_ref = refs[1 + ng], refs[2 + ng]
        g_out, d_out, m_out, v_out = refs[3 + ng:]
        g = g_refs[0][...]
        for extra in g_refs[1:]:
            g = g + extra[...]
        g_out[...] = g
        m2 = ADAM_B1 * m_ref[...] + (1.0 - ADAM_B1) * g
        v2 = ADAM_B2 * v_ref[...] + (1.0 - ADAM_B2) * (g * g)
        m_out[...] = m2
        v_out[...] = v2
        m_hat = m2 / (1.0 - ADAM_B1 ** ADAM_STEP)
        v_hat = v2 / (1.0 - ADAM_B2 ** ADAM_STEP)
        d_out[...] = -ADAM_LR * (m_hat / (jnp.sqrt(v_hat) + ADAM_EPS) + ADAM_WD * w_ref[...])

    spec = pl.BlockSpec((tr, cols), lambda i: (i, 0))
    out = jax.ShapeDtypeStruct((rows, cols), F32)
    return pl.pallas_call(
        body, name=name, grid=(rows // tr,),
        out_shape=(out, out, out, out),
        in_specs=[spec] * (3 + ng), out_specs=(spec, spec, spec, spec),
        compiler_params=_cparams(("arbitrary",)),
    )(w, *grads, m, v)


def _pack_rows(arrays):
    tile = SUBLANES * LANES
    rows, offsets, at = [], [], 0
    for a in arrays:
        flat = a.reshape(-1).astype(F32)
        n = -(-flat.shape[0] // tile) * tile
        rows.append(jnp.pad(flat, (0, n - flat.shape[0])).reshape(-1, LANES))
        offsets.append(at)
        at += n // LANES
    return jnp.concatenate(rows, axis=0), offsets


def _unpack_rows(packed, offsets, shapes):
    out = []
    for off, shape in zip(offsets, shapes):
        n = 1
        for s in shape:
            n *= s
        nrow = -(-n // LANES)
        out.append(packed[off:off + nrow].reshape(-1)[:n].reshape(shape))
    return out


def _rope_tables(seq):
    half = HEAD_DIM // 2
    inv = np.float32(ROPE_THETA) ** (-np.arange(half, dtype=np.float32) / np.float32(half))
    ang = (np.arange(seq, dtype=np.float32)[:, None] * inv[None, :]).astype(np.float64)
    cos, sin = np.cos(ang).astype(np.float32), np.sin(ang).astype(np.float32)
    cos_t = np.concatenate([cos, cos, cos, cos], axis=1)
    sin_t = np.concatenate([-sin, sin, -sin, sin], axis=1)
    return jnp.asarray(cos_t), jnp.asarray(sin_t)


def kernel(x, c, w_ada, b_ada, g_attn, w_in, na_rpb, sw_sink, g_na_out, g_sw_out, w_out, g_ffn, w_up, conv_w, conv_b, w_down, g_final, loss_target, m_w_ada, m_b_ada, m_g_attn, m_w_in, m_na_rpb, m_sw_sink, m_g_na_out, m_g_sw_out, m_w_out, m_g_ffn, m_w_up, m_conv_w, m_conv_b, m_w_down, m_g_final, v_w_ada, v_b_ada, v_g_attn, v_w_in, v_na_rpb, v_sw_sink, v_g_na_out, v_g_sw_out, v_w_out, v_g_ffn, v_w_up, v_conv_w, v_conv_b, v_w_down, v_g_final):
    batch, seq, d = x.shape
    t = batch * seq
    assert d == D_MODEL and seq % (NA_ROWS * GRID_W) == 0 and seq % TOKEN_TILE == 0 and batch <= SUBLANES
    shard = 2 * lax.axis_index("x") + lax.axis_index("y")
    xt = x.reshape(t, d)
    tgt = loss_target.reshape(t, d)

    c8 = jnp.pad(c, ((0, SUBLANES - batch), (0, 0)))
    w_in_t_s = jnp.transpose(w_in[0]).astype(BF16)
    (mod8, sc_all), (w_in_g,) = _ada_forward(c8, w_ada[0], b_ada, _Rider("gather", [w_in_t_s]))
    mod3 = mod8[:batch].reshape(batch, 1, 6 * d)
    w_in_t = w_in_g.reshape(IN_WIDTH, d)

    cos_t, sin_t = _rope_tables(seq)
    (h1, proj), (w_out_g,) = _in_proj(xt, mod3, g_attn, w_in_t, cos_t, sin_t, seq,
                                      _Rider("gather", [w_out[0].astype(BF16)]))
    n_heads = NA_WIDTH // HEAD_DIM
    n_tiles, n_dc = 2 * NA_ROWS - 2, 2 * NA_COLS - 1
    expand, neg_mask = _na_bias_pattern()
    rpb = na_rpb[0]
    rows2 = jnp.concatenate([rpb[:, :-1, :], rpb[:, 1:, :]], axis=2).reshape(n_heads * n_tiles, 2 * n_dc)
    rows2 = jnp.pad(rows2, ((0, 0), (0, GRID_W - 2 * n_dc)))
    tiles = _na_bias_tiles(rows2, expand, neg_mask).reshape(n_heads, n_tiles, GRID_W, LANES)
    sink = sw_sink[0]
    w_up_b16 = w_up[0].astype(BF16)
    (oa,), (w_up_a,) = _na_forward(proj, tiles, batch, seq, _Rider("gather", [w_up_b16[:d // 2]]))
    (ob,), (w_up_b, conv_w_g) = _sw_forward(proj, sink, batch, seq, _Rider("gather", [w_up_b16[d // 2:], conv_w[0]]))
    w_up_f = (w_up_a, w_up_b)
    w_out_f = w_out_g.reshape(d, d)
    conv_w_f = jnp.transpose(conv_w_g, (1, 0, 2)).reshape(3, D_FF)
    oab, mix, x1, h2 = _out_proj(oa, ob, g_na_out, g_sw_out, w_out_f, xt, mod3, g_ffn, seq)
    (u,), (w_down_g,) = _up_proj(h2, w_up_f, _Rider("gather", [w_down[0].astype(BF16)]))
    w_down_f = w_down_g.reshape(D_FF, d)
    a = _conv_gate(u, conv_w_f, conv_b, batch, seq)
    dx2, dffn, loss_part, dgate_f, dg_final = _down_and_loss(a, w_down_f, x1, mod3, g_final.reshape(1, d), tgt, seq)

    gw_down, gw_down_b = _down_weight_grad(a, dffn)
    blocks = lambda g, rows: g.reshape(N_SHARD, rows // N_SHARD, d)
    (du, gconv_w, gconv_b), (recv_down, own_down) = _ffn_backward(
        dffn, w_down_f, u, conv_w_f, conv_b, batch, seq,
        _Rider("scatter", [blocks(gw_down_b, D_FF)], [blocks(gw_down, D_FF)]))
    (gw_up_top, gw_up_bot, gw_up_top_b, gw_up_bot_b), _ = _up_weight_grad(h2, du)
    (dx1, dmix, dshift_f, dscale_f, dgate_a, dg_ffn), (recv_up_top, own_up_top) = _up_backward(
        du, w_up_f, x1, mod3, g_ffn, dx2, mix, seq, _Rider("scatter", [gw_up_top_b], [gw_up_top]))
    doa, dob, gw_out, gw_out_b, dg_na, dg_sw = _out_backward(dmix, w_out_f, oab, oa, ob, g_na_out, g_sw_out)
    (dqa, dka, dva, dtiles), (recv_up_bot, own_up_bot) = _na_backward(
        proj, doa, tiles, batch, seq, _Rider("scatter", [gw_up_bot_b], [gw_up_bot]))
    (dq_b, dk_b, dv_b, dsink_parts), (recv_out, own_out) = _sw_backward(
        proj, dob, sink, batch, seq, _Rider("scatter", [blocks(gw_out_b, d)], [blocks(gw_out, d)]))
    gx, gw_in_t, gw_in_b, dshift_a, dscale_a, dg_attn = _in_backward(
        (dqa, dka, dva), dq_b, dk_b, dv_b, w_in_t, h1, xt, mod3, g_attn, dx1, cos_t, sin_t, seq)

    red = _na_bias_grad(dtiles.reshape(n_heads * n_tiles, GRID_W, LANES), expand)[:, :2 * n_dc]
    red = red.reshape(n_heads, n_tiles, 2, n_dc)
    zero_row = jnp.zeros((n_heads, 1, n_dc), F32)
    g_rpb = (jnp.concatenate([red[:, :, 0, :], zero_row], axis=1)
             + jnp.concatenate([zero_row, red[:, :, 1, :]], axis=1))
    g_sink = jnp.sum(dsink_parts[:, :, :2, 0], axis=0).reshape(SW_WIDTH // HEAD_DIM)

    dmod = jnp.concatenate([dshift_a, dscale_a, dgate_a, dshift_f, dscale_f, dgate_f], axis=2).reshape(batch, 6 * d)
    small_parts = [jnp.sum(dmod, axis=0), dg_attn, g_rpb, g_sink, dg_na, dg_sw, dg_ffn, gconv_w, gconv_b, dg_final,
                   loss_part[0, 0:1]]
    packed, offsets = _pack_rows(small_parts + [dmod])
    (summed, every), (recv_in, own_in) = _allreduce_small(
        packed, _Rider("scatter", [blocks(gw_in_b, IN_WIDTH)], [blocks(gw_in_t, IN_WIDTH)]))
    mine = [_sum_slots([(recv_in, own_in)], "sum_w_in"), _sum_slots([(recv_out, own_out)], "sum_w_out"),
            _sum_slots([(recv_up_top, own_up_top), (recv_up_bot, own_up_bot)], "sum_w_up"),
            _sum_slots([(recv_down, own_down)], "sum_w_down")]
    theirs = _ride_alone(_Rider("swap", mine), "swap_sibling")
    small_shapes = [(1, 6 * d), (1, d), na_rpb.shape, sw_sink.shape, (1, NA_WIDTH), (1, SW_WIDTH), (1, d),
                    (3, D_FF), (1, D_FF), (d,), ()]
    (g_b_ada, g_g_attn, g_na_rpb, g_sw_sink, g_g_na, g_g_sw, g_g_ffn, g_conv_w_full, g_conv_b, g_g_final,
     loss) = _unpack_rows(summed, offsets[:-1], small_shapes)
    dmod_rows = every[:, offsets[-1]:offsets[-1] + batch * 6 * d // LANES, :].reshape(N_DEV, batch, 6 * d)
    dmod_rows = jnp.pad(dmod_rows, ((0, 0), (0, SUBLANES - batch), (0, 0))).reshape(N_DEV * SUBLANES, 6 * d)
    ncol = w_ada.shape[2]
    g_w_ada = _ada_weight_grad(sc_all, lax.dynamic_slice(dmod_rows, (0, shard * ncol), (N_DEV * SUBLANES, ncol)))
    cshard = conv_w.shape[2]
    g_conv_w = lax.dynamic_slice(g_conv_w_full, (0, shard * cshard), (3, cshard)).reshape(conv_w.shape)

    def big(w, m, v, g_parts, name):
        shape = w.shape
        outs = _adamw(w[0], g_parts, m[0], v[0], name)
        return [o.reshape(shape) for o in outs]

    r_w_ada = big(w_ada, m_w_ada, v_w_ada, [g_w_ada], "adamw_w_ada")
    r_w_in = [jnp.transpose(o).reshape(w_in.shape) for o in
              _adamw(jnp.transpose(w_in[0]), [mine[0], theirs[0]], jnp.transpose(m_w_in[0]), jnp.transpose(v_w_in[0]),
                     "adamw_w_in")]
    r_w_out = big(w_out, m_w_out, v_w_out, [mine[1], theirs[1]], "adamw_w_out")
    r_w_up = big(w_up, m_w_up, v_w_up, [mine[2], theirs[2]], "adamw_w_up")
    r_w_down = big(w_down, m_w_down, v_w_down, [mine[3], theirs[3]], "adamw_w_down")

    small_w = [b_ada, g_attn, na_rpb, sw_sink, g_na_out, g_sw_out, g_ffn, conv_w, conv_b, g_final]
    small_m = [m_b_ada, m_g_attn, m_na_rpb, m_sw_sink, m_g_na_out, m_g_sw_out, m_g_ffn, m_conv_w, m_conv_b, m_g_final]
    small_v = [v_b_ada, v_g_attn, v_na_rpb, v_sw_sink, v_g_na_out, v_g_sw_out, v_g_ffn, v_conv_w, v_conv_b, v_g_final]
    small_g = [g_b_ada, g_g_attn, g_na_rpb, g_sw_sink, g_g_na, g_g_sw, g_g_ffn, g_conv_w, g_conv_b, g_g_final]
    pw, offs = _pack_rows(small_w)
    pg, _ = _pack_rows(small_g)
    pm, _ = _pack_rows(small_m)
    pv, _ = _pack_rows(small_v)
    shapes = [w.shape for w in small_w]
    r_small = [_unpack_rows(o, offs, shapes) for o in _adamw(pw, [pg], pm, pv, "adamw_small")]

    def pick(k):
        b_, ga_, rpb_, sk_, gna_, gsw_, gf_, cw_, cb_, gfin_ = r_small[k]
        return [r_w_ada[k], b_, ga_, r_w_in[k], rpb_, sk_, gna_, gsw_, r_w_out[k], gf_, r_w_up[k], cw_, cb_,
                r_w_down[k], gfin_]

    return (loss, gx.reshape(batch, seq, d), *pick(0), *pick(1), *pick(2), *pick(3))
```

```python
import functools

import jax
import jax.numpy as jnp
import numpy as np
from jax import lax
from jax.experimental import pallas as pl
from jax.experimental.pallas import tpu as pltpu

F32 = jnp.float32
BF16 = jnp.bfloat16
MESH = pl.DeviceIdType.MESH

D_MODEL = 1024
HEAD_DIM = 64
NA_WIDTH = 512
SW_WIDTH = 512
SW_KV_WIDTH = 128
IN_WIDTH = 2304
D_FF = 2816
GRID_W = 64
NA_ROWS = 8
NA_COLS = 16
SW_BLOCK = 128
ROPE_THETA = 10000.0
EPS = 1e-6
NEG = -1e30
QK_SCALE = HEAD_DIM ** -0.5

ADAM_LR = 0.001
ADAM_B1 = 0.9
ADAM_B2 = 0.999
ADAM_EPS = 1e-08
ADAM_WD = 0.01
ADAM_STEP = 10

N_SHARD = 4
N_DEV = 8
LANES = 128
SUBLANES = 8
TOKEN_TILE = 512
FF_TILE = 256
CONV_CHUNK = 64
NA_GROUP = 4
SW_GROUP_BLOCKS = 4
VMEM_BIG = 56 * 1024 * 1024


def _mm(a, b):
    return jnp.dot(a, b, preferred_element_type=F32)


def _mm_nt(a, b):
    return lax.dot_general(a, b, (((1,), (1,)), ((), ())), preferred_element_type=F32)


def _mm_tn(a, b):
    return lax.dot_general(a, b, (((0,), (0,)), ((), ())), preferred_element_type=F32)


def _cparams(sem=None, vmem=None):
    kw = {}
    if sem is not None:
        kw["dimension_semantics"] = sem
    if vmem is not None:
        kw["vmem_limit_bytes"] = vmem
    return pltpu.CompilerParams(**kw)


def _resident(shape):
    return pl.BlockSpec(shape, lambda i: (0,) * len(shape), pipeline_mode=pl.Buffered(1))


def _sigmoid(x):
    return 1.0 / (1.0 + jnp.exp(-x))


def _rms_stats(x):
    r = lax.rsqrt(jnp.mean(x * x, axis=-1, keepdims=True) + EPS)
    return r, x * r


def _rms_bwd(dxn, xn, r):
    return r * (dxn - xn * jnp.mean(dxn * xn, axis=-1, keepdims=True))


def _my_pos():
    return lax.axis_index("x"), lax.axis_index("y"), lax.axis_index("c")


def _flip(v, bit):
    return 1 - v if bit else v


def _ada_forward(c8, w_ada, b_ada, rider):
    d = c8.shape[1]
    ncol = w_ada.shape[1]

    def body(c_ref, w_ref, b_ref, mod_ref, sc_ref, m_scr, mod_buf, ssem, rsem, ssem2, rsem2):
        x, y, c = _my_pos()
        me = 4 * x + 2 * y + c
        shard = 2 * x + y
        cv = c_ref[...]
        my_rows = pl.ds(pl.multiple_of(me * SUBLANES, SUBLANES), SUBLANES)
        sc_ref[my_rows, :] = cv * _sigmoid(cv)

        def copy1(k):
            peer = (_flip(x, (k >> 2) & 1), _flip(y, (k >> 1) & 1), _flip(c, k & 1))
            return pltpu.make_async_remote_copy(
                src_ref=sc_ref.at[my_rows, :], dst_ref=sc_ref.at[my_rows, :],
                send_sem=ssem.at[k - 1], recv_sem=rsem.at[k - 1], device_id=peer, device_id_type=MESH)

        sends = [copy1(k) for k in range(1, N_DEV)]
        for cp in sends:
            cp.start()
        for cp in sends:
            cp.wait_recv()
        m_scr[...] = _mm(sc_ref[...].astype(BF16), w_ref[...].astype(BF16))

        def copy2(k):
            px, py = _flip(x, (k >> 1) & 1), _flip(y, k & 1)
            rows = pl.ds(pl.multiple_of((4 * px + 2 * py + c) * SUBLANES, SUBLANES), SUBLANES)
            return pltpu.make_async_remote_copy(
                src_ref=m_scr.at[rows, :], dst_ref=mod_buf.at[shard],
                send_sem=ssem2.at[k - 1], recv_sem=rsem2.at[k - 1], device_id=(px, py, c), device_id_type=MESH)

        sends2 = [copy2(k) for k in range(1, N_SHARD)]
        for cp in sends2:
            cp.start()
        mod_buf[shard] = m_scr[my_rows, :]
        for cp in sends2:
            cp.wait_recv()
        for s in range(N_SHARD):
            mod_ref[:, s * ncol:(s + 1) * ncol] = mod_buf[s] + b_ref[:, s * ncol:(s + 1) * ncol]
        for cp in sends + sends2:
            cp.wait_send()

    vm = pl.BlockSpec(memory_space=pltpu.VMEM)
    return _hosted(
        body, rider, name="ada_forward", grid=(),
        out_shape=(jax.ShapeDtypeStruct((SUBLANES, N_SHARD * ncol), F32),
                   jax.ShapeDtypeStruct((N_DEV * SUBLANES, d), F32)),
        in_specs=[vm, vm, vm], out_specs=(vm, vm),
        scratch_shapes=[pltpu.VMEM((N_DEV * SUBLANES, ncol), F32), pltpu.VMEM((N_SHARD, SUBLANES, ncol), F32),
                        pltpu.SemaphoreType.DMA((N_DEV - 1,)), pltpu.SemaphoreType.DMA((N_DEV - 1,)),
                        pltpu.SemaphoreType.DMA((N_SHARD - 1,)), pltpu.SemaphoreType.DMA((N_SHARD - 1,))],
        compiler_params=_cparams(vmem=VMEM_BIG), args=[c8, w_ada, b_ada])


class _Rider:
    def __init__(self, kind, srcs, owns=()):
        self.kind, self.srcs, self.owns = kind, list(srcs), list(owns)
        n = len(self.srcs)
        sds = jax.ShapeDtypeStruct
        dma = pltpu.SemaphoreType.DMA
        if kind == "gather":
            self.out_shapes = [sds((N_SHARD,) + s.shape, s.dtype) for s in self.srcs]
            self.sems = [dma((n, N_SHARD - 1)), dma((n, N_SHARD - 1)), dma((n,)),
                         dma((n, N_SHARD - 1)), dma((n, N_SHARD - 1))]
        elif kind == "scatter":
            self.out_shapes = ([sds((N_SHARD - 1,) + s.shape[1:], s.dtype) for s in self.srcs]
                               + [sds(o.shape[1:], o.dtype) for o in self.owns])
            self.sems = [dma((n, N_SHARD - 1)), dma((n, N_SHARD - 1)), dma((max(len(self.owns), 1),))]
        else:
            self.out_shapes = [sds(s.shape, s.dtype) for s in self.srcs]
            self.sems = [dma((n,)), dma((n,))]

    @property
    def inputs(self):
        return self.srcs + self.owns

    def _halved(self, i):
        a = self.srcs[i]
        tile_rows = SUBLANES * (4 // jnp.dtype(a.dtype).itemsize)
        return self.kind == "gather" and a.shape[0] % (2 * tile_rows) == 0

    def copies(self, ins, outs, sems):
        n = len(self.srcs)
        x, y, c = _my_pos()
        shard = 2 * x + y
        local, remote, relay = [], [], []
        if self.kind == "swap":
            ssem, rsem = sems
            for i in range(n):
                remote.append(pltpu.make_async_remote_copy(
                    src_ref=ins[i], dst_ref=outs[i], send_sem=ssem.at[i], recv_sem=rsem.at[i],
                    device_id=(x, y, 1 - c), device_id_type=MESH))
            return local, remote, relay
        if self.kind == "gather":
            ssem, rsem, lsem, ssem2, rsem2 = sems
        else:
            ssem, rsem, lsem = sems
        for i in range(n):
            if self.kind == "gather":
                local.append(pltpu.make_async_copy(ins[i], outs[i].at[shard], lsem.at[i]))
                half = ins[i].shape[0] // 2
                mine = pl.ds(pl.multiple_of(c * half, half), half) if self._halved(i) else None
            for k in range(1, N_SHARD):
                px, py = _flip(x, (k >> 1) & 1), _flip(y, k & 1)
                if self.kind == "gather":
                    src, dst = ins[i], outs[i].at[shard]
                    if mine is not None:
                        src, dst = src.at[mine], dst.at[mine]
                        got = outs[i].at[2 * px + py].at[mine]
                        relay.append(pltpu.make_async_remote_copy(
                            src_ref=got, dst_ref=got, send_sem=ssem2.at[i, k - 1], recv_sem=rsem2.at[i, k - 1],
                            device_id=(x, y, 1 - c), device_id_type=MESH))
                else:
                    src, dst = ins[i].at[2 * px + py], outs[i].at[k - 1]
                remote.append(pltpu.make_async_remote_copy(
                    src_ref=src, dst_ref=dst, send_sem=ssem.at[i, k - 1], recv_sem=rsem.at[i, k - 1],
                    device_id=(px, py, c), device_id_type=MESH))
        if self.kind == "scatter":
            for i in range(len(self.owns)):
                local.append(pltpu.make_async_copy(ins[n + i].at[shard], outs[n + i], lsem.at[i]))
        return local, remote, relay

    def start(self, ins, outs, sems):
        local, remote, _ = self.copies(ins, outs, sems)
        for cp in local + remote:
            cp.start()

    def wait(self, ins, outs, sems):
        local, remote, relay = self.copies(ins, outs, sems)
        for cp in remote:
            cp.wait_recv()
        for cp in relay:
            cp.start()
        for cp in relay:
            cp.wait_recv()
        for cp in remote + relay:
            cp.wait_send()
        for cp in local:
            cp.wait()


def _hosted(body, rider, *, name, grid, out_shape, in_specs, out_specs, scratch_shapes, compiler_params, args):
    out_shape, out_specs = list(out_shape), list(out_specs)
    if rider is None:
        outs = pl.pallas_call(body, name=name, grid=grid, out_shape=tuple(out_shape), in_specs=list(in_specs),
                              out_specs=tuple(out_specs), scratch_shapes=list(scratch_shapes),
                              compiler_params=compiler_params)(*args)
        return list(outs), []
    n_in, n_out, n_scr = len(in_specs), len(out_shape), len(scratch_shapes)
    nr_in, nr_out = len(rider.inputs), len(rider.out_shapes)
    n_steps = 1
    for size in grid:
        n_steps *= size

    def full(*refs):
        ins, refs = refs[:n_in], refs[n_in:]
        r_in, refs = refs[:nr_in], refs[nr_in:]
        outs, refs = refs[:n_out], refs[n_out:]
        r_out, refs = refs[:nr_out], refs[nr_out:]
        scr, sems = refs[:n_scr], refs[n_scr:]
        if grid:
            step = 0
            for ax, size in enumerate(grid):
                step = step * size + pl.program_id(ax)
            pl.when(step == 0)(lambda: rider.start(r_in, r_out, sems))
            body(*ins, *outs, *scr)
            pl.when(step == n_steps - 1)(lambda: rider.wait(r_in, r_out, sems))
        else:
            rider.start(r_in, r_out, sems)
            body(*ins, *outs, *scr)
            rider.wait(r_in, r_out, sems)

    hbm = pl.BlockSpec(memory_space=pl.ANY)
    res = pl.pallas_call(
        full, name=name, grid=grid, out_shape=tuple(out_shape + rider.out_shapes),
        in_specs=list(in_specs) + [hbm] * nr_in, out_specs=tuple(out_specs + [hbm] * nr_out),
        scratch_shapes=list(scratch_shapes) + rider.sems, compiler_params=compiler_params,
    )(*args, *rider.inputs)
    return list(res[:n_out]), list(res[n_out:])


def _ride_alone(rider, name):
    return _hosted(lambda: None, rider, name=name, grid=(), out_shape=[], in_specs=[], out_specs=[], scratch_shapes=[],
                   compiler_params=_cparams(), args=[])[1]


def _allreduce_small(packed, rider=None):
    r = packed.shape[0]

    def body(p_ref, sum_ref, all_ref, ssem, rsem):
        x, y, c = _my_pos()
        me = 4 * x + 2 * y + c
        all_ref[me] = p_ref[...]
        cps = []
        for k in range(1, N_DEV):
            peer = (_flip(x, (k >> 2) & 1), _flip(y, (k >> 1) & 1), _flip(c, k & 1))
            cps.append(pltpu.make_async_remote_copy(
                src_ref=all_ref.at[me], dst_ref=all_ref.at[me], send_sem=ssem.at[k - 1], recv_sem=rsem.at[k - 1],
                device_id=peer, device_id_type=MESH))
        for cp in cps:
            cp.start()
        for cp in cps:
            cp.wait_recv()
        acc = all_ref[0]
        for dev in range(1, N_DEV):
            acc = acc + all_ref[dev]
        sum_ref[...] = acc
        for cp in cps:
            cp.wait_send()

    vm = pl.BlockSpec(memory_space=pltpu.VMEM)
    return _hosted(
        body, rider, name="allreduce_small", grid=(),
        out_shape=[jax.ShapeDtypeStruct((r, LANES), F32), jax.ShapeDtypeStruct((N_DEV, r, LANES), F32)],
        in_specs=[vm], out_specs=[vm, vm],
        scratch_shapes=[pltpu.SemaphoreType.DMA((N_DEV - 1,)), pltpu.SemaphoreType.DMA((N_DEV - 1,))],
        compiler_params=_cparams(), args=[packed])


def _rope_rot(t):
    w = t.shape[1]
    lane = lax.broadcasted_iota(jnp.int32, t.shape, 1)
    first = (lane % HEAD_DIM) < (HEAD_DIM // 2)
    return jnp.where(first, pltpu.roll(t, w - HEAD_DIM // 2, 1), pltpu.roll(t, HEAD_DIM // 2, 1))


def _in_proj(x, mod3, g_attn, w_in_t, cos_t, sin_t, seq, rider=None):
    t, d = x.shape
    tm = TOKEN_TILE
    per_seq = seq // tm
    rope_lo, rope_hi = 3 * NA_WIDTH, 3 * NA_WIDTH + SW_WIDTH + SW_KV_WIDTH
    n_rep = (rope_hi - rope_lo) // LANES

    def body(x_ref, mod_ref, g_ref, w_ref, cos_ref, sin_ref, h_ref, p_ref):
        r, xn = _rms_stats(x_ref[...])
        shift, scale = mod_ref[0, :, 0:d], mod_ref[0, :, d:2 * d]
        hb = ((xn * g_ref[...]) * (1.0 + scale) + shift).astype(BF16)
        h_ref[...] = hb
        p_ref[:, :rope_lo] = _mm_nt(hb, w_ref[:rope_lo, :]).astype(BF16)
        pr = _mm_nt(hb, w_ref[rope_lo:rope_hi, :])
        cos = jnp.concatenate([cos_ref[...]] * n_rep, axis=1)
        sin = jnp.concatenate([sin_ref[...]] * n_rep, axis=1)
        p_ref[:, rope_lo:rope_hi] = (pr * cos + _rope_rot(pr) * sin).astype(BF16)
        p_ref[:, rope_hi:] = _mm_nt(hb, w_ref[rope_hi:, :]).astype(BF16)

    return _hosted(
        body, rider, name="in_proj", grid=(t // tm,),
        out_shape=[jax.ShapeDtypeStruct((t, d), BF16), jax.ShapeDtypeStruct((t, IN_WIDTH), BF16)],
        in_specs=[pl.BlockSpec((tm, d), lambda i: (i, 0)),
                  pl.BlockSpec((1, 1, 6 * d), lambda i: (i // per_seq, 0, 0)),
                  pl.BlockSpec((1, d), lambda i: (0, 0)),
                  pl.BlockSpec((IN_WIDTH, d), lambda i: (0, 0)),
                  pl.BlockSpec((tm, LANES), lambda i: (i % per_seq, 0)),
                  pl.BlockSpec((tm, LANES), lambda i: (i % per_seq, 0))],
        out_specs=[pl.BlockSpec((tm, d), lambda i: (i, 0)), pl.BlockSpec((tm, IN_WIDTH), lambda i: (i, 0))],
        scratch_shapes=[], compiler_params=_cparams(("arbitrary",), VMEM_BIG),
        args=[x, mod3, g_attn, w_in_t, cos_t, sin_t])


def _na_bias_pattern():
    n_dc = 2 * NA_COLS - 1
    j = np.arange(GRID_W)[:, None]
    m = np.arange(GRID_W * LANES)[None, :]
    q, lane = m // LANES, m % LANES
    k = lane % GRID_W
    cs = np.clip(q - NA_COLS // 2, 0, GRID_W - NA_COLS)
    ok = (k >= cs) & (k < cs + NA_COLS)
    hit = ok & (j < 2 * n_dc) & (lane // GRID_W == j // n_dc) & (k - q + (NA_COLS - 1) == j % n_dc)
    return jnp.asarray(hit.astype(np.float32)), jnp.asarray(np.where(ok, 0.0, NEG).astype(np.float32))


def _na_bias_tiles(rows2, expand, mask):
    n, width = rows2.shape[0], expand.shape[1]
    q_step = 16
    step = q_step * LANES

    def body(r_ref, e_ref, m_ref, o_ref):
        flat = jnp.dot(r_ref[...], e_ref[...], precision=lax.Precision.HIGHEST,
                       preferred_element_type=F32) + m_ref[...]
        for qq in range(q_step):
            o_ref[:, qq, :] = flat[:, qq * LANES:(qq + 1) * LANES]

    return pl.pallas_call(
        body, name="na_bias_tiles", grid=(width // step,),
        out_shape=jax.ShapeDtypeStruct((n, GRID_W, LANES), F32),
        in_specs=[pl.BlockSpec(rows2.shape, lambda i: (0, 0)), pl.BlockSpec((expand.shape[0], step), lambda i: (0, i)),
                  pl.BlockSpec((1, step), lambda i: (0, i))],
        out_specs=pl.BlockSpec((n, q_step, LANES), lambda i: (0, i, 0)),
        compiler_params=_cparams(("arbitrary",)),
    )(rows2, expand, mask)


def _na_prepare(k_ref, v_ref, km, vm):
    lane = lax.broadcasted_iota(jnp.int32, k_ref.shape, 1)
    low = lane < HEAD_DIM
    kv = k_ref[...]
    vv = v_ref[...]
    zero = jnp.zeros_like(kv)
    km[0] = jnp.where(low, kv, zero)
    km[1] = jnp.where(low, zero, kv)
    vm[0] = jnp.where(low, vv, zero)
    vm[1] = jnp.where(low, zero, vv)


def _na_window(r, n_rows):
    rs = jnp.clip(r - NA_ROWS // 2, 0, n_rows - NA_ROWS)
    return rs, r - rs


def _na_pair_window(ref, wrows):
    return jnp.concatenate([ref[0, wrows, :], ref[1, wrows, :]], axis=0)


def _na_scores(q, k2, tp_ref, off):
    bias = jnp.concatenate([tp_ref[h, 2 * w - off + (NA_ROWS - 1)] for h in range(2) for w in range(NA_ROWS // 2)],
                           axis=1)
    return _mm_nt(q, k2) * QK_SCALE + bias


def _pair_softmax(s):
    win = s.shape[1] // 2
    halves = []
    for h in range(2):
        sh = s[:, h * win:(h + 1) * win]
        e = jnp.exp(sh - jnp.max(sh, axis=-1, keepdims=True))
        halves.append(e / jnp.sum(e, axis=-1, keepdims=True))
    return jnp.concatenate(halves, axis=1)


def _na_forward(proj, tiles, batch, seq, rider=None):
    t = proj.shape[0]
    n_rows = seq // GRID_W
    n_pairs = NA_WIDTH // LANES
    win = NA_ROWS * GRID_W

    def body(q_ref, k_ref, v_ref, tp_ref, o_ref, km, vm):
        _na_prepare(k_ref, v_ref, km, vm)

        def scores(r):
            rs, off = _na_window(r, n_rows)
            rows = pl.ds(pl.multiple_of(r * GRID_W, GRID_W), GRID_W)
            wrows = pl.ds(pl.multiple_of(rs * GRID_W, GRID_W), win)
            return rows, wrows, _na_scores(q_ref[rows, :], _na_pair_window(km, wrows), tp_ref, off)

        def finish(rows, wrows, s):
            o_ref[rows, :] = _mm(_pair_softmax(s).astype(BF16), _na_pair_window(vm, wrows))

        def row_group(i, carry):
            for state in [scores(NA_GROUP * i + j) for j in range(NA_GROUP)]:
                finish(*state)
            return carry

        lax.fori_loop(0, n_rows // NA_GROUP, row_group, 0)

    return _hosted(
        body, rider, name="na_forward", grid=(batch, n_pairs),
        out_shape=[jax.ShapeDtypeStruct((t, NA_WIDTH), F32)],
        in_specs=[pl.BlockSpec((seq, LANES), lambda b, p: (b, p)),
                  pl.BlockSpec((seq, LANES), lambda b, p: (b, n_pairs + p)),
                  pl.BlockSpec((seq, LANES), lambda b, p: (b, 2 * n_pairs + p)),
                  pl.BlockSpec((2, 2 * NA_ROWS - 2, GRID_W, LANES), lambda b, p: (p, 0, 0, 0))],
        out_specs=[pl.BlockSpec((seq, LANES), lambda b, p: (b, p))],
        scratch_shapes=[pltpu.VMEM((2, seq, LANES), BF16), pltpu.VMEM((2, seq, LANES), BF16)],
        compiler_params=_cparams(("arbitrary", "arbitrary")), args=[proj, proj, proj, tiles])


def _sw_prepare(kv_ref, g, dst_lo, dst_hi, seq):
    lane = lax.broadcasted_iota(jnp.int32, kv_ref.shape, 1)
    mine = (lane // HEAD_DIM) == g
    kg = jnp.where(mine, kv_ref[...].astype(F32), 0.0)
    kr = pltpu.roll(kg, HEAD_DIM, 1)
    first = g == 0
    zero = jnp.zeros((SW_BLOCK, LANES), BF16)
    for dst, val in ((dst_lo, jnp.where(first, kg, kr)), (dst_hi, jnp.where(first, kr, kg))):
        dst[0:SW_BLOCK, :] = zero
        dst[SW_BLOCK:SW_BLOCK + seq, :] = val.astype(BF16)
        dst[SW_BLOCK + seq:, :] = zero


def _sw_mask(n, seq):
    qi = lax.broadcasted_iota(jnp.int32, (SW_BLOCK, 3 * SW_BLOCK), 0)
    kj = lax.broadcasted_iota(jnp.int32, (SW_BLOCK, 3 * SW_BLOCK), 1)
    kpos = n * SW_BLOCK - SW_BLOCK + kj
    return (jnp.abs(qi + SW_BLOCK - kj) <= SW_BLOCK) & (kpos >= 0) & (kpos < seq)


def _sw_probs(s2, ok, sinks):
    band = s2.shape[1] // 2
    halves, sink_p = [], []
    for i in range(2):
        s = jnp.where(ok, s2[:, i * band:(i + 1) * band], NEG)
        m = jnp.maximum(jnp.max(s, axis=-1, keepdims=True), sinks[i])
        p = jnp.exp(s - m)
        es = jnp.exp(sinks[i] - m)
        den = jnp.sum(p, axis=-1, keepdims=True) + es
        halves.append(p / den)
        sink_p.append(es / den)
    return jnp.concatenate(halves, axis=1), sink_p


def _sw_forward(proj, sink, batch, seq, rider=None):
    t = proj.shape[0]
    n_pairs = SW_WIDTH // LANES
    q_blk = 3 * NA_WIDTH // LANES
    k_blk = q_blk + n_pairs
    n_blocks = seq // SW_BLOCK
    pad = seq + 2 * SW_BLOCK

    def body(sink_ref, q_ref, k_ref, v_ref, o_ref, k_lo, k_hi, v_lo, v_hi):
        hp = pl.program_id(1)
        g = hp // 2
        _sw_prepare(k_ref, g, k_lo, k_hi, seq)
        _sw_prepare(v_ref, g, v_lo, v_hi, seq)

        sinks = (sink_ref[2 * hp], sink_ref[2 * hp + 1])

        def scores(n):
            rows = pl.ds(pl.multiple_of(n * SW_BLOCK, SW_BLOCK), SW_BLOCK)
            wrows = pl.ds(pl.multiple_of(n * SW_BLOCK, SW_BLOCK), 3 * SW_BLOCK)
            k2 = jnp.concatenate([k_lo[wrows, :], k_hi[wrows, :]], axis=0)
            return n, rows, wrows, _mm_nt(q_ref[rows, :], k2) * QK_SCALE

        def finish(n, rows, wrows, s2):
            p, _ = _sw_probs(s2, _sw_mask(n, seq), sinks)
            v2 = jnp.concatenate([v_lo[wrows, :], v_hi[wrows, :]], axis=0)
            o_ref[rows, :] = _mm(p.astype(BF16), v2)

        def block_group(i, carry):
            for state in [scores(SW_GROUP_BLOCKS * i + j) for j in range(SW_GROUP_BLOCKS)]:
                finish(*state)
            return carry

        lax.fori_loop(0, n_blocks // SW_GROUP_BLOCKS, block_group, 0)

    return _hosted(
        body, rider, name="sw_forward", grid=(batch, n_pairs),
        out_shape=[jax.ShapeDtypeStruct((t, SW_WIDTH), F32)],
        in_specs=[pl.BlockSpec(memory_space=pltpu.SMEM),
                  pl.BlockSpec((seq, LANES), lambda b, p: (b, q_blk + p)),
                  pl.BlockSpec((seq, LANES), lambda b, p: (b, k_blk)),
                  pl.BlockSpec((seq, LANES), lambda b, p: (b, k_blk + 1))],
        out_specs=[pl.BlockSpec((seq, LANES), lambda b, p: (b, p))],
        scratch_shapes=[pltpu.VMEM((pad, LANES), BF16)] * 4,
        compiler_params=_cparams(("arbitrary", "arbitrary")), args=[sink, proj, proj, proj])


def _out_proj(oa, ob, g_na, g_sw, w_out, x, mod3, g_ffn, seq):
    t, d = x.shape
    tm = TOKEN_TILE
    per_seq = seq // tm

    def body(oa_ref, ob_ref, gna_ref, gsw_ref, w_ref, x_ref, mod_ref, gf_ref, oab_ref, mix_ref, x1_ref, h2_ref):
        _, na = _rms_stats(oa_ref[...])
        _, nb = _rms_stats(ob_ref[...])
        oab = jnp.concatenate([na * gna_ref[...], nb * gsw_ref[...]], axis=1).astype(BF16)
        oab_ref[...] = oab
        mix = _mm(oab, w_ref[...])
        mix_ref[...] = mix
        gate_a = mod_ref[0, :, 2 * d:3 * d]
        shift_f, scale_f = mod_ref[0, :, 3 * d:4 * d], mod_ref[0, :, 4 * d:5 * d]
        x1 = x_ref[...] + gate_a * mix
        x1_ref[...] = x1
        _, xn = _rms_stats(x1)
        h2_ref[...] = ((xn * gf_ref[...]) * (1.0 + scale_f) + shift_f).astype(BF16)

    tile = lambda w: pl.BlockSpec((tm, w), lambda i: (i, 0))
    vec = lambda w: pl.BlockSpec((1, w), lambda i: (0, 0))
    return pl.pallas_call(
        body, name="out_proj", grid=(t // tm,),
        out_shape=(jax.ShapeDtypeStruct((t, d), BF16), jax.ShapeDtypeStruct((t, d), F32),
                   jax.ShapeDtypeStruct((t, d), F32), jax.ShapeDtypeStruct((t, d), BF16)),
        in_specs=[tile(NA_WIDTH), tile(SW_WIDTH), vec(NA_WIDTH), vec(SW_WIDTH),
                  pl.BlockSpec((d, d), lambda i: (0, 0)), tile(d),
                  pl.BlockSpec((1, 1, 6 * d), lambda i: (i // per_seq, 0, 0)), vec(d)],
        out_specs=(tile(d), tile(d), tile(d), tile(d)),
        compiler_params=_cparams(("arbitrary",), VMEM_BIG),
    )(oa, ob, g_na, g_sw, w_out, x, mod3, g_ffn)


def _up_proj(h2, w_up_halves, rider=None):
    t, d = h2.shape
    tm = TOKEN_TILE
    w_a, w_b = w_up_halves
    half, wcol = w_a.shape[1], w_a.shape[2]

    def body(h_ref, wa_ref, wb_ref, u_ref):
        u_ref[0] = (_mm(h_ref[:, :half], wa_ref[0]) + _mm(h_ref[:, half:], wb_ref[0])).astype(BF16)

    w_spec = pl.BlockSpec((1, half, wcol), lambda j, i: (j, 0, 0))
    return _hosted(
        body, rider, name="up_proj", grid=(N_SHARD, t // tm),
        out_shape=[jax.ShapeDtypeStruct((2, t, D_FF), BF16)],
        in_specs=[pl.BlockSpec((tm, d), lambda j, i: (i, 0)), w_spec, w_spec],
        out_specs=[pl.BlockSpec((1, tm, wcol), lambda j, i: (j // 2, i, j % 2))],
        scratch_shapes=[], compiler_params=_cparams(("arbitrary", "arbitrary"), VMEM_BIG), args=[h2, w_a, w_b])


def _taps_chunk(load, s, rows, seq):
    halo = 2 * SUBLANES
    cur = load(s, rows)
    above = load(pl.multiple_of(jnp.maximum(s - halo, 0), halo), halo)
    below = load(pl.multiple_of(jnp.minimum(s + rows, seq - halo), halo), halo)
    up = jnp.where(s > 0, above[halo - 1:halo, :], 0.0)
    dn = jnp.where(s + rows < seq, below[0:1, :], 0.0)
    row = lax.broadcasted_iota(jnp.int32, cur.shape, 0)
    prev = jnp.where(row == 0, up, pltpu.roll(cur, 1, 0))
    nxt = jnp.where(row == rows - 1, dn, pltpu.roll(cur, rows - 1, 0))
    return cur, prev, nxt


def _conv_gate(u, conv_w, conv_b, batch, seq):
    t = u.shape[1]
    cw = FF_TILE
    rows = CONV_CHUNK

    def body(u_ref, w_ref, b_ref, a_ref):
        def chunk(i, carry):
            s = pl.multiple_of(i * rows, rows)
            gt, prev, nxt = _taps_chunk(lambda at, n: u_ref[1, pl.ds(at, n), :].astype(F32), s, rows, seq)
            gc = prev * w_ref[0:1, :] + gt * w_ref[1:2, :] + nxt * w_ref[2:3, :] + b_ref[...]
            a_ref[pl.ds(s, rows), :] = ((gc * _sigmoid(gc)) * u_ref[0, pl.ds(s, rows), :].astype(F32)).astype(BF16)
            return carry

        lax.fori_loop(0, seq // rows, chunk, 0)

    return pl.pallas_call(
        body, name="conv_gate", grid=(batch, D_FF // cw),
        out_shape=jax.ShapeDtypeStruct((t, D_FF), BF16),
        in_specs=[pl.BlockSpec((2, seq, cw), lambda b, j: (0, b, j)),
                  pl.BlockSpec((3, cw), lambda b, j: (0, j)), pl.BlockSpec((1, cw), lambda b, j: (0, j))],
        out_specs=pl.BlockSpec((seq, cw), lambda b, j: (b, j)),
        compiler_params=_cparams(("arbitrary", "arbitrary"), VMEM_BIG),
    )(u, conv_w, conv_b)


def _down_and_loss(a, w_down, x1, mod3, g_final, target, seq):
    t, d = x1.shape
    tm = TOKEN_TILE
    per_seq = seq // tm
    batch = t // seq

    def body(a_ref, w_ref, x1_ref, mod_ref, g_ref, tgt_ref, dx2_ref, dffn_ref, loss_ref, dgate_ref, dg_ref):
        i = pl.program_id(0)
        f = _mm(a_ref[...], w_ref[...])
        gate_f = mod_ref[0, :, 5 * d:6 * d]
        x2 = x1_ref[...] + gate_f * f
        r, xn = _rms_stats(x2)
        err = xn * g_ref[...] - tgt_ref[...]
        part = 0.5 * jnp.sum(jnp.mean(err * err, axis=-1, keepdims=True))
        dy = err / d
        dx2 = _rms_bwd(dy * g_ref[...], xn, r)
        dx2_ref[...] = dx2
        dffn_ref[...] = (dx2 * gate_f).astype(BF16)

        @pl.when(i == 0)
        def _():
            loss_ref[...] = jnp.zeros_like(loss_ref)
            dg_ref[...] = jnp.zeros_like(dg_ref)

        @pl.when(i % per_seq == 0)
        def _():
            dgate_ref[...] = jnp.zeros_like(dgate_ref)

        loss_ref[...] += part
        dg_ref[...] += jnp.sum(dy * xn, axis=0, keepdims=True)
        dgate_ref[0] += jnp.sum(dx2 * f, axis=0, keepdims=True)

    tile = lambda w: pl.BlockSpec((tm, w), lambda i: (i, 0))
    return pl.pallas_call(
        body, name="down_loss", grid=(t // tm,),
        out_shape=(jax.ShapeDtypeStruct((t, d), F32), jax.ShapeDtypeStruct((t, d), BF16),
                   jax.ShapeDtypeStruct((SUBLANES, LANES), F32), jax.ShapeDtypeStruct((batch, 1, d), F32),
                   jax.ShapeDtypeStruct((1, d), F32)),
        in_specs=[tile(D_FF), _resident((D_FF, d)), tile(d),
                  pl.BlockSpec((1, 1, 6 * d), lambda i: (i // per_seq, 0, 0)),
                  pl.BlockSpec((1, d), lambda i: (0, 0)), tile(d)],
        out_specs=(tile(d), tile(d), pl.BlockSpec((SUBLANES, LANES), lambda i: (0, 0)),
                   pl.BlockSpec((1, 1, d), lambda i: (i // per_seq, 0, 0)), pl.BlockSpec((1, d), lambda i: (0, 0))),
        compiler_params=_cparams(("arbitrary",), VMEM_BIG),
    )(a, w_down, x1, mod3, g_final, target)


def _down_weight_grad(a, dffn):
    t, dff = a.shape
    d = dffn.shape[1]
    tk = TOKEN_TILE
    n_k = t // tk

    def body(a_ref, df_ref, g_ref, gb_ref):
        k = pl.program_id(0)

        @pl.when(k == 0)
        def _():
            g_ref[...] = jnp.zeros_like(g_ref)

        g_ref[...] += _mm_tn(a_ref[...], df_ref[...])

        @pl.when(k == n_k - 1)
        def _():
            gb_ref[...] = g_ref[...].astype(BF16)

    whole = pl.BlockSpec((dff, d), lambda k: (0, 0))
    return pl.pallas_call(
        body, name="down_weight_grad", grid=(n_k,),
        out_shape=(jax.ShapeDtypeStruct((dff, d), F32), jax.ShapeDtypeStruct((dff, d), BF16)),
        in_specs=[pl.BlockSpec((tk, dff), lambda k: (k, 0)), pl.BlockSpec((tk, d), lambda k: (k, 0))],
        out_specs=(whole, whole),
        compiler_params=_cparams(("arbitrary",), VMEM_BIG),
    )(a, dffn)


def _ffn_backward(dffn, w_down, u, conv_w, conv_b, batch, seq, rider=None):
    t, d = dffn.shape
    cw = FF_TILE
    rows = CONV_CHUNK

    def body(df_ref, wd_ref, u_ref, w_ref, b_ref, du_ref, gcw_ref, gcb_ref, da_scr, dgc_scr):
        b = pl.program_id(1)
        da_scr[...] = _mm_nt(df_ref[...], wd_ref[...])

        @pl.when(b == 0)
        def _():
            gcw_ref[...] = jnp.zeros_like(gcw_ref)
            gcb_ref[...] = jnp.zeros_like(gcb_ref)

        def fold(v):
            return jnp.sum(v.reshape(rows // SUBLANES, SUBLANES, cw), axis=0)

        def chunk(i, carry):
            s = pl.multiple_of(i * rows, rows)
            here = pl.ds(s, rows)
            gt, prev, nxt = _taps_chunk(lambda at, n: u_ref[1, pl.ds(at, n), :].astype(F32), s, rows, seq)
            val, da = u_ref[0, here, :].astype(F32), da_scr[here, :]
            gc = prev * w_ref[0:1, :] + gt * w_ref[1:2, :] + nxt * w_ref[2:3, :] + b_ref[...]
            sg = _sigmoid(gc)
            sl = gc * sg
            du_ref[0, here, :] = (da * sl).astype(BF16)
            dgc = (da * val) * (sg * (1.0 + gc * (1.0 - sg)))
            dgc_scr[here, :] = dgc
            cb, c0, c1, c2 = carry
            return cb + fold(dgc), c0 + fold(dgc * prev), c1 + fold(dgc * gt), c2 + fold(dgc * nxt)

        zero = jnp.zeros((SUBLANES, cw), F32)
        cb, c0, c1, c2 = lax.fori_loop(0, seq // rows, chunk, (zero, zero, zero, zero))
        gcb_ref[...] += jnp.sum(cb, axis=0, keepdims=True)
        gcw_ref[0:1, :] += jnp.sum(c0, axis=0, keepdims=True)
        gcw_ref[1:2, :] += jnp.sum(c1, axis=0, keepdims=True)
        gcw_ref[2:3, :] += jnp.sum(c2, axis=0, keepdims=True)

        def chunk2(i, carry):
            s = pl.multiple_of(i * rows, rows)
            dgc, dprev, dnxt = _taps_chunk(lambda at, n: dgc_scr[pl.ds(at, n), :], s, rows, seq)
            du_ref[1, pl.ds(s, rows), :] = (dnxt * w_ref[0:1, :] + dgc * w_ref[1:2, :]
                                            + dprev * w_ref[2:3, :]).astype(BF16)
            return carry

        lax.fori_loop(0, seq // rows, chunk2, 0)

    return _hosted(
        body, rider, name="ffn_backward", grid=(D_FF // cw, batch),
        out_shape=[jax.ShapeDtypeStruct((2, t, D_FF), BF16),
                   jax.ShapeDtypeStruct((3, D_FF), F32), jax.ShapeDtypeStruct((1, D_FF), F32)],
        in_specs=[pl.BlockSpec((seq, d), lambda j, b: (b, 0)), pl.BlockSpec((cw, d), lambda j, b: (j, 0)),
                  pl.BlockSpec((2, seq, cw), lambda j, b: (0, b, j)),
                  pl.BlockSpec((3, cw), lambda j, b: (0, j)), pl.BlockSpec((1, cw), lambda j, b: (0, j))],
        out_specs=[pl.BlockSpec((2, seq, cw), lambda j, b: (0, b, j)),
                   pl.BlockSpec((3, cw), lambda j, b: (0, j)), pl.BlockSpec((1, cw), lambda j, b: (0, j))],
        scratch_shapes=[pltpu.VMEM((seq, cw), F32), pltpu.VMEM((seq, cw), F32)],
        compiler_params=_cparams(("arbitrary", "arbitrary"), VMEM_BIG), args=[dffn, w_down, u, conv_w, conv_b])


def _up_backward(du, w_up, x1, mod3, g_ffn, dx2, mix, seq, rider=None):
    _, t, _ = du.shape
    d = x1.shape[1]
    tm = TOKEN_TILE
    per_seq = seq // tm
    batch = t // seq
    w_a, w_b = w_up
    half, wcol = w_a.shape[1], w_a.shape[2]

    def body(du_ref, wa_ref, wb_ref, x1_ref, mod_ref, g_ref, dx2_ref, mix_ref,
             dx1_ref, dmix_ref, dsh_ref, dsc_ref, dga_ref, dg_ref):
        i = pl.program_id(0)
        parts = []
        for w_ref in (wa_ref, wb_ref):
            acc = jnp.zeros((tm, half), F32)
            for j in range(N_SHARD):
                acc = acc + _mm_nt(du_ref[j // 2, :, (j % 2) * wcol:(j % 2 + 1) * wcol], w_ref[j])
            parts.append(acc)
        dh = jnp.concatenate(parts, axis=1)
        gate_a = mod_ref[0, :, 2 * d:3 * d]
        scale_f = mod_ref[0, :, 4 * d:5 * d]
        r, xn = _rms_stats(x1_ref[...])
        xg = xn * g_ref[...]
        dxg = dh * (1.0 + scale_f)
        dx1 = dx2_ref[...] + _rms_bwd(dxg * g_ref[...], xn, r)
        dx1_ref[...] = dx1
        dmix_ref[...] = (dx1 * gate_a).astype(BF16)

        @pl.when(i == 0)
        def _():
            dg_ref[...] = jnp.zeros_like(dg_ref)

        @pl.when(i % per_seq == 0)
        def _():
            dsh_ref[...] = jnp.zeros_like(dsh_ref)
            dsc_ref[...] = jnp.zeros_like(dsc_ref)
            dga_ref[...] = jnp.zeros_like(dga_ref)

        dg_ref[...] += jnp.sum(dxg * xn, axis=0, keepdims=True)
        dsh_ref[0] += jnp.sum(dh, axis=0, keepdims=True)
        dsc_ref[0] += jnp.sum(dh * xg, axis=0, keepdims=True)
        dga_ref[0] += jnp.sum(dx1 * mix_ref[...], axis=0, keepdims=True)

    tile = lambda w: pl.BlockSpec((tm, w), lambda i: (i, 0))
    per_b = pl.BlockSpec((1, 1, d), lambda i: (i // per_seq, 0, 0))
    small = jax.ShapeDtypeStruct((batch, 1, d), F32)
    return _hosted(
        body, rider, name="up_backward", grid=(t // tm,),
        out_shape=[jax.ShapeDtypeStruct((t, d), F32), jax.ShapeDtypeStruct((t, d), BF16), small, small, small,
                   jax.ShapeDtypeStruct((1, d), F32)],
        in_specs=[pl.BlockSpec((2, tm, D_FF), lambda i: (0, i, 0)),
                  _resident((N_SHARD, half, wcol)), _resident((N_SHARD, half, wcol)), tile(d),
                  pl.BlockSpec((1, 1, 6 * d), lambda i: (i // per_seq, 0, 0)),
                  pl.BlockSpec((1, d), lambda i: (0, 0)), tile(d), tile(d)],
        out_specs=[tile(d), tile(d), per_b, per_b, per_b, pl.BlockSpec((1, d), lambda i: (0, 0))],
        scratch_shapes=[], compiler_params=_cparams(("arbitrary",), VMEM_BIG),
        args=[du, w_a, w_b, x1, mod3, g_ffn, dx2, mix])


def _up_weight_grad(h2, du, rider=None):
    t, d = h2.shape
    tk = TOKEN_TILE
    wcol = D_FF // 2
    half = d // 2
    n_k = t // tk

    def body(h_ref, du_ref, ga_ref, gb_ref, ga16_ref, gb16_ref):
        k = pl.program_id(1)

        @pl.when(k == 0)
        def _():
            ga_ref[...] = jnp.zeros_like(ga_ref)
            gb_ref[...] = jnp.zeros_like(gb_ref)

        du = du_ref[0]
        ga_ref[0] += _mm_tn(h_ref[:, :half], du)
        gb_ref[0] += _mm_tn(h_ref[:, half:], du)

        @pl.when(k == n_k - 1)
        def _():
            ga16_ref[...] = ga_ref[...].astype(BF16)
            gb16_ref[...] = gb_ref[...].astype(BF16)

    g_spec = pl.BlockSpec((1, half, wcol), lambda j, k: (j, 0, 0))
    f32_out = jax.ShapeDtypeStruct((N_SHARD, half, wcol), F32)
    b16_out = jax.ShapeDtypeStruct((N_SHARD, half, wcol), BF16)
    return _hosted(
        body, rider, name="up_weight_grad", grid=(N_SHARD, n_k),
        out_shape=[f32_out, f32_out, b16_out, b16_out],
        in_specs=[pl.BlockSpec((tk, d), lambda j, k: (k, 0)),
                  pl.BlockSpec((1, tk, wcol), lambda j, k: (j // 2, k, j % 2))],
        out_specs=[g_spec, g_spec, g_spec, g_spec], scratch_shapes=[],
        compiler_params=_cparams(("arbitrary", "arbitrary"), VMEM_BIG), args=[h2, du])


def _out_backward(dmix, w_out, oab, oa, ob, g_na, g_sw):
    t, d = dmix.shape
    tm = TOKEN_TILE
    hw = NA_WIDTH

    def body(dm_ref, w_ref, oab_ref, oa_ref, ob_ref, gna_ref, gsw_ref,
             doa_ref, dob_ref, gw_ref, gwb_ref, dgna_ref, dgsw_ref):
        @pl.when(pl.program_id(0) == 0)
        def _():
            gw_ref[...] = jnp.zeros_like(gw_ref)
            dgna_ref[...] = jnp.zeros_like(dgna_ref)
            dgsw_ref[...] = jnp.zeros_like(dgsw_ref)

        dm = dm_ref[...]
        gw_ref[...] += _mm_tn(oab_ref[...], dm)

        @pl.when(pl.program_id(0) == t // tm - 1)
        def _():
            gwb_ref[...] = gw_ref[...].astype(BF16)

        do = _mm_nt(dm, w_ref[...])
        for raw_ref, g_ref, dst_ref, dg_ref, lo in ((oa_ref, gna_ref, doa_ref, dgna_ref, 0),
                                                     (ob_ref, gsw_ref, dob_ref, dgsw_ref, hw)):
            r, xn = _rms_stats(raw_ref[...])
            dpart = do[:, lo:lo + hw]
            dg_ref[...] += jnp.sum(dpart * xn, axis=0, keepdims=True)
            dst_ref[...] = _rms_bwd(dpart * g_ref[...], xn, r).astype(BF16)

    tile = lambda w: pl.BlockSpec((tm, w), lambda i: (i, 0))
    vec = lambda w: pl.BlockSpec((1, w), lambda i: (0, 0))
    return pl.pallas_call(
        body, name="out_backward", grid=(t // tm,),
        out_shape=(jax.ShapeDtypeStruct((t, hw), BF16), jax.ShapeDtypeStruct((t, hw), BF16),
                   jax.ShapeDtypeStruct((d, d), F32), jax.ShapeDtypeStruct((d, d), BF16),
                   jax.ShapeDtypeStruct((1, hw), F32), jax.ShapeDtypeStruct((1, hw), F32)),
        in_specs=[tile(d), pl.BlockSpec((d, d), lambda i: (0, 0)), tile(d), tile(hw), tile(hw), vec(hw), vec(hw)],
        out_specs=(tile(hw), tile(hw), pl.BlockSpec((d, d), lambda i: (0, 0)), pl.BlockSpec((d, d), lambda i: (0, 0)),
                   vec(hw), vec(hw)),
        compiler_params=_cparams(("arbitrary",), VMEM_BIG),
    )(dmix, w_out, oab, oa, ob, g_na, g_sw)


def _na_backward(proj, d_o, tiles, batch, seq, rider=None):
    t = proj.shape[0]
    n_rows = seq // GRID_W
    n_pairs = NA_WIDTH // LANES
    win = NA_ROWS * GRID_W
    n_tiles = 2 * NA_ROWS - 2

    def body(q_ref, k_ref, v_ref, do_ref, tp_ref, dq_ref, dk_ref, dv_ref, dtp_ref, km, vm, dk_acc, dv_acc):
        @pl.when(pl.program_id(1) == 0)
        def _():
            dtp_ref[...] = jnp.zeros_like(dtp_ref)

        _na_prepare(k_ref, v_ref, km, vm)
        dk_acc[...] = jnp.zeros_like(dk_acc)
        dv_acc[...] = jnp.zeros_like(dv_acc)
        low = lax.broadcasted_iota(jnp.int32, (win, LANES), 1) < HEAD_DIM

        def scores(r):
            rs, off = _na_window(r, n_rows)
            rows = pl.ds(pl.multiple_of(r * GRID_W, GRID_W), GRID_W)
            wrows = pl.ds(pl.multiple_of(rs * GRID_W, GRID_W), win)
            q, do = q_ref[rows, :], do_ref[rows, :]
            k2 = _na_pair_window(km, wrows)
            s = _na_scores(q, k2, tp_ref, off)
            dp = _mm_nt(do, _na_pair_window(vm, wrows))
            return rows, wrows, off, q, do, k2, s, dp

        def finish(rows, wrows, off, q, do, k2, s, dp):
            p = _pair_softmax(s)
            parts = []
            for h in range(2):
                ph, dph = p[:, h * win:(h + 1) * win], dp[:, h * win:(h + 1) * win]
                dsh = ph * (dph - jnp.sum(ph * dph, axis=-1, keepdims=True))
                for w in range(NA_ROWS // 2):
                    dtp_ref[h, 2 * w - off + (NA_ROWS - 1)] += dsh[:, w * LANES:(w + 1) * LANES]
                parts.append(dsh)
            dsb = (jnp.concatenate(parts, axis=1) * QK_SCALE).astype(BF16)
            dq_ref[rows, :] = _mm(dsb, k2).astype(BF16)
            dk2 = _mm_tn(dsb, q)
            dv2 = _mm_tn(p.astype(BF16), do)
            dk_acc[wrows, :] += jnp.where(low, dk2[:win], dk2[win:])
            dv_acc[wrows, :] += jnp.where(low, dv2[:win], dv2[win:])

        def row_group(i, carry):
            for state in [scores(NA_GROUP * i + j) for j in range(NA_GROUP)]:
                finish(*state)
            return carry

        lax.fori_loop(0, n_rows // NA_GROUP, row_group, 0)
        dk_ref[...] = dk_acc[...].astype(BF16)
        dv_ref[...] = dv_acc[...].astype(BF16)

    blk = lambda off: pl.BlockSpec((seq, LANES), lambda p, b: (b, off + p))
    out = jax.ShapeDtypeStruct((t, NA_WIDTH), BF16)
    return _hosted(
        body, rider, name="na_backward", grid=(n_pairs, batch),
        out_shape=[out, out, out, jax.ShapeDtypeStruct(tiles.shape, F32)],
        in_specs=[blk(0), blk(n_pairs), blk(2 * n_pairs), blk(0),
                  pl.BlockSpec((2, n_tiles, GRID_W, LANES), lambda p, b: (p, 0, 0, 0))],
        out_specs=[blk(0), blk(0), blk(0), pl.BlockSpec((2, n_tiles, GRID_W, LANES), lambda p, b: (p, 0, 0, 0))],
        scratch_shapes=[pltpu.VMEM((2, seq, LANES), BF16), pltpu.VMEM((2, seq, LANES), BF16),
                        pltpu.VMEM((seq, LANES), F32), pltpu.VMEM((seq, LANES), F32)],
        compiler_params=_cparams(("arbitrary", "arbitrary")), args=[proj, proj, proj, d_o, tiles])


def _na_bias_grad(dtiles, expand):
    n = dtiles.shape[0]

    def body(t_ref, e_ref, o_ref):
        flat = jnp.concatenate([t_ref[:, qq, :] for qq in range(GRID_W)], axis=1)
        o_ref[...] = lax.dot_general(flat, e_ref[...], (((1,), (1,)), ((), ())),
                                     precision=lax.Precision.HIGHEST, preferred_element_type=F32)

    return pl.pallas_call(
        body, name="na_bias_grad",
        out_shape=jax.ShapeDtypeStruct((n, expand.shape[0]), F32),
        compiler_params=_cparams(vmem=VMEM_BIG),
    )(dtiles, expand)


def _sw_backward(proj, d_o, sink, batch, seq, rider=None):
    t = proj.shape[0]
    n_pairs = SW_WIDTH // LANES
    q_blk = 3 * NA_WIDTH // LANES
    k_blk = q_blk + n_pairs
    n_blocks = seq // SW_BLOCK
    pad = seq + 2 * SW_BLOCK

    def body(sink_ref, q_ref, k_ref, v_ref, do_ref, dq_ref, dk_ref, dv_ref, dsk_ref,
             k_lo, k_hi, v_lo, v_hi, dk_loc, dv_loc, dk_tot, dv_tot):
        hp = pl.program_id(1)
        g = hp // 2
        _sw_prepare(k_ref, g, k_lo, k_hi, seq)
        _sw_prepare(v_ref, g, v_lo, v_hi, seq)
        dk_loc[...] = jnp.zeros_like(dk_loc)
        dv_loc[...] = jnp.zeros_like(dv_loc)

        @pl.when(hp == 0)
        def _():
            dk_tot[...] = jnp.zeros_like(dk_tot)
            dv_tot[...] = jnp.zeros_like(dv_tot)

        band = 3 * SW_BLOCK
        low = lax.broadcasted_iota(jnp.int32, (band, LANES), 1) < HEAD_DIM

        sinks = (sink_ref[2 * hp], sink_ref[2 * hp + 1])

        def scores(n):
            rows = pl.ds(pl.multiple_of(n * SW_BLOCK, SW_BLOCK), SW_BLOCK)
            wrows = pl.ds(pl.multiple_of(n * SW_BLOCK, SW_BLOCK), band)
            qb, do = q_ref[rows, :], do_ref[rows, :]
            k2 = jnp.concatenate([k_lo[wrows, :], k_hi[wrows, :]], axis=0)
            v2 = jnp.concatenate([v_lo[wrows, :], v_hi[wrows, :]], axis=0)
            return n, rows, wrows, qb, do, k2, _mm_nt(qb, k2) * QK_SCALE, _mm_nt(do, v2)

        def finish(sink_acc, n, rows, wrows, qb, do, k2, s2, dp):
            p, ps = _sw_probs(s2, _sw_mask(n, seq), sinks)
            parts, new = [], []
            for i in range(2):
                ph, dph = p[:, i * band:(i + 1) * band], dp[:, i * band:(i + 1) * band]
                delta = jnp.sum(ph * dph, axis=-1, keepdims=True)
                parts.append(ph * (dph - delta))
                new.append(sink_acc[i] - ps[i] * delta)
            dsb = (jnp.concatenate(parts, axis=1) * QK_SCALE).astype(BF16)
            dq_ref[rows, :] = _mm(dsb, k2)
            dk2 = _mm_tn(dsb, qb)
            dv2 = _mm_tn(p.astype(BF16), do)
            dk_loc[wrows, :] += jnp.where(low, dk2[:band], dk2[band:])
            dv_loc[wrows, :] += jnp.where(low, dv2[:band], dv2[band:])
            return tuple(new)

        def block_group(i, carry):
            for state in [scores(SW_GROUP_BLOCKS * i + j) for j in range(SW_GROUP_BLOCKS)]:
                carry = finish(carry, *state)
            return carry

        zero = jnp.zeros((SW_BLOCK, 1), F32)
        s0, s1 = lax.fori_loop(0, n_blocks // SW_GROUP_BLOCKS, block_group, (zero, zero))
        row = lax.broadcasted_iota(jnp.int32, (SUBLANES, LANES), 0)
        dsk_ref[0, 0] = jnp.where(row == 0, jnp.sum(s0), jnp.where(row == 1, jnp.sum(s1), 0.0))

        lane_s = lax.broadcasted_iota(jnp.int32, (seq, LANES), 1)
        mine_g = (lane_s // HEAD_DIM) == g
        for loc, tot in ((dk_loc, dk_tot), (dv_loc, dv_tot)):
            part = loc[SW_BLOCK:SW_BLOCK + seq, :]
            tot[...] += jnp.where(mine_g, part + pltpu.roll(part, HEAD_DIM, 1), 0.0)

        @pl.when(hp == n_pairs - 1)
        def _():
            dk_ref[...] = dk_tot[...]
            dv_ref[...] = dv_tot[...].astype(BF16)

    return _hosted(
        body, rider, name="sw_backward", grid=(batch, n_pairs),
        out_shape=[jax.ShapeDtypeStruct((t, SW_WIDTH), F32), jax.ShapeDtypeStruct((t, LANES), F32),
                   jax.ShapeDtypeStruct((t, LANES), BF16), jax.ShapeDtypeStruct((batch, n_pairs, SUBLANES, LANES), F32)],
        in_specs=[pl.BlockSpec(memory_space=pltpu.SMEM),
                  pl.BlockSpec((seq, LANES), lambda b, p: (b, q_blk + p)),
                  pl.BlockSpec((seq, LANES), lambda b, p: (b, k_blk)),
                  pl.BlockSpec((seq, LANES), lambda b, p: (b, k_blk + 1)),
                  pl.BlockSpec((seq, LANES), lambda b, p: (b, p))],
        out_specs=[pl.BlockSpec((seq, LANES), lambda b, p: (b, p)), pl.BlockSpec((seq, LANES), lambda b, p: (b, 0)),
                   pl.BlockSpec((seq, LANES), lambda b, p: (b, 0)),
                   pl.BlockSpec((1, 1, SUBLANES, LANES), lambda b, p: (b, p, 0, 0))],
        scratch_shapes=[pltpu.VMEM((pad, LANES), BF16)] * 4 + [pltpu.VMEM((pad, LANES), F32)] * 2
        + [pltpu.VMEM((seq, LANES), F32)] * 2,
        compiler_params=_cparams(("arbitrary", "arbitrary")), args=[sink, proj, proj, proj, d_o])


def _in_backward(dqkv_a, dq_b, dk_b, dv_b, w_in_t, h1, x, mod3, g_attn, dx1, cos_t, sin_t, seq):
    t, d = x.shape
    tm = TOKEN_TILE
    per_seq = seq // tm
    batch = t // seq
    dqa, dka, dva = dqkv_a
    n_q = SW_WIDTH // LANES

    def body(dqa_ref, dka_ref, dva_ref, dqb_ref, dkb_ref, dvb_ref, w_ref, h_ref, x_ref, mod_ref, g_ref, dx1_ref,
             cos_ref, sin_ref, dx_ref, gw_ref, gwb_ref, dsh_ref, dsc_ref, dg_ref):
        i = pl.program_id(0)

        @pl.when(i == 0)
        def _():
            gw_ref[...] = jnp.zeros_like(gw_ref)
            dg_ref[...] = jnp.zeros_like(dg_ref)

        @pl.when(i % per_seq == 0)
        def _():
            dsh_ref[...] = jnp.zeros_like(dsh_ref)
            dsc_ref[...] = jnp.zeros_like(dsc_ref)

        dr = jnp.concatenate([dqb_ref[...], dkb_ref[...]], axis=1)
        cos = jnp.concatenate([cos_ref[...]] * (n_q + 1), axis=1)
        sin = jnp.concatenate([sin_ref[...]] * (n_q + 1), axis=1)
        dr = dr * cos + _rope_rot(dr * sin)
        dproj = jnp.concatenate([dqa_ref[...], dka_ref[...], dva_ref[...], dr.astype(BF16), dvb_ref[...]], axis=1)
        gw_ref[...] += _mm_tn(dproj, h_ref[...])

        @pl.when(i == t // tm - 1)
        def _():
            gwb_ref[...] = gw_ref[...].astype(BF16)

        dh = _mm(dproj, w_ref[...])
        scale = mod_ref[0, :, d:2 * d]
        r, xn = _rms_stats(x_ref[...])
        xg = xn * g_ref[...]
        dxg = dh * (1.0 + scale)
        dx_ref[...] = dx1_ref[...] + _rms_bwd(dxg * g_ref[...], xn, r)
        dg_ref[...] += jnp.sum(dxg * xn, axis=0, keepdims=True)
        dsh_ref[0] += jnp.sum(dh, axis=0, keepdims=True)
        dsc_ref[0] += jnp.sum(dh * xg, axis=0, keepdims=True)

    tile = lambda w: pl.BlockSpec((tm, w), lambda i: (i, 0))
    per_b = pl.BlockSpec((1, 1, d), lambda i: (i // per_seq, 0, 0))
    small = jax.ShapeDtypeStruct((batch, 1, d), F32)
    rope = pl.BlockSpec((tm, LANES), lambda i: (i % per_seq, 0))
    return pl.pallas_call(
        body, name="in_backward", grid=(t // tm,),
        out_shape=(jax.ShapeDtypeStruct((t, d), F32), jax.ShapeDtypeStruct((IN_WIDTH, d), F32),
                   jax.ShapeDtypeStruct((IN_WIDTH, d), BF16), small, small, jax.ShapeDtypeStruct((1, d), F32)),
        in_specs=[tile(NA_WIDTH), tile(NA_WIDTH), tile(NA_WIDTH), tile(SW_WIDTH), tile(LANES), tile(LANES),
                  _resident((IN_WIDTH, d)), tile(d), tile(d),
                  pl.BlockSpec((1, 1, 6 * d), lambda i: (i // per_seq, 0, 0)),
                  pl.BlockSpec((1, d), lambda i: (0, 0)), tile(d), rope, rope],
        out_specs=(tile(d), _resident((IN_WIDTH, d)), _resident((IN_WIDTH, d)),
                   per_b, per_b, pl.BlockSpec((1, d), lambda i: (0, 0))),
        compiler_params=_cparams(("arbitrary",), VMEM_BIG),
    )(dqa, dka, dva, dq_b, dk_b, dv_b, w_in_t, h1, x, mod3, g_attn, dx1, cos_t, sin_t)


def _ada_weight_grad(sc_all, dmod_cols):
    d = sc_all.shape[1]
    ncol = dmod_cols.shape[1]

    def body(s_ref, m_ref, o_ref):
        o_ref[...] = _mm_tn(s_ref[...].astype(BF16), m_ref[...].astype(BF16))

    return pl.pallas_call(
        body, name="ada_weight_grad",
        out_shape=jax.ShapeDtypeStruct((d, ncol), F32),
        compiler_params=_cparams(vmem=VMEM_BIG),
    )(sc_all, dmod_cols)


def _row_tile(rows, cols):
    target = max(SUBLANES, (1 << 20) // (4 * cols))
    best = rows
    for cand in range(SUBLANES, rows + 1, SUBLANES):
        if rows % cand == 0 and cand <= target:
            best = cand
    return best if rows % SUBLANES == 0 else rows


def _sum_slots(parts, name):
    n = len(parts)
    _, rows, cols = parts[0][0].shape
    tr = _row_tile(rows, cols)
    per = rows // tr

    def body(*refs):
        o_ref = refs[-1]
        for q in range(n):
            @pl.when(pl.program_id(0) == q)
            def _(q=q):
                p_ref, own_ref = refs[2 * q], refs[2 * q + 1]
                o_ref[...] = ((own_ref[...] + p_ref[0].astype(F32)) + p_ref[1].astype(F32)) + p_ref[2].astype(F32)

    in_specs, args = [], []
    for q, (recv, own) in enumerate(parts):
        in_specs.append(pl.BlockSpec((N_SHARD - 1, tr, cols), lambda p, i, q=q: (0, jnp.where(p == q, i, 0), 0)))
        in_specs.append(pl.BlockSpec((tr, cols), lambda p, i, q=q: (jnp.where(p == q, i, 0), 0)))
        args += [recv, own]
    return pl.pallas_call(
        body, name=name, grid=(n, per),
        out_shape=jax.ShapeDtypeStruct((n * rows, cols), F32),
        in_specs=in_specs, out_specs=pl.BlockSpec((tr, cols), lambda p, i: (p * per + i, 0)),
        compiler_params=_cparams(("arbitrary", "arbitrary")),
    )(*args)


def _adamw_math(w, g, m, v):
    m2 = ADAM_B1 * m + (1.0 - ADAM_B1) * g
    v2 = ADAM_B2 * v + (1.0 - ADAM_B2) * (g * g)
    m_hat = m2 / (1.0 - ADAM_B1 ** ADAM_STEP)
    v_hat = v2 / (1.0 - ADAM_B2 ** ADAM_STEP)
    return -ADAM_LR * (m_hat / (jnp.sqrt(v_hat) + ADAM_EPS) + ADAM_WD * w), m2, v2


def _small_step(partials, states, dmod, b_ada_state, rider=None):
    n_upd = len(states)
    moving = list(partials) + [dmod]
    n_mov = len(moving)
    all_states = list(states) + [b_ada_state]

    def body(*refs):
        mov, refs = refs[:n_mov], refs[n_mov:]
        wmv, refs = refs[:3 * (n_upd + 1)], refs[3 * (n_upd + 1):]
        res, refs = refs[:4 * (n_upd + 1)], refs[4 * (n_upd + 1):]
        sums_out, refs = refs[:n_mov - n_upd - 1], refs[n_mov - n_upd - 1:]
        dmod_out, refs = refs[0], refs[1:]
        everyone, (ssem, rsem) = refs[:n_mov], refs[n_mov:]
        x, y, c = _my_pos()
        me = 4 * x + 2 * y + c
        cps = []
        for a in range(n_mov):
            everyone[a][me] = mov[a][...]
            for k in range(1, N_DEV):
                peer = (_flip(x, (k >> 2) & 1), _flip(y, (k >> 1) & 1), _flip(c, k & 1))
                cps.append(pltpu.make_async_remote_copy(
                    src_ref=everyone[a].at[me], dst_ref=everyone[a].at[me], send_sem=ssem.at[a, k - 1],
                    recv_sem=rsem.at[a, k - 1], device_id=peer, device_id_type=MESH))
        for cp in cps:
            cp.start()
        for cp in cps:
            cp.wait_recv()

        def total(a):
            acc = everyone[a][0]
            for dev in range(1, N_DEV):
                acc = acc + everyone[a][dev]
            return acc

        grads = [total(a) for a in range(n_upd)]
        grads.append(jnp.sum(total(n_mov - 1), axis=0, keepdims=True))
        for j, g in enumerate(grads):
            delta, m2, v2 = _adamw_math(wmv[3 * j][...], g, wmv[3 * j + 1][...], wmv[3 * j + 2][...])
            res[4 * j][...] = g
            res[4 * j + 1][...] = delta
            res[4 * j + 2][...] = m2
            res[4 * j + 3][...] = v2
        for j in range(n_mov - n_upd - 1):
            sums_out[j][...] = total(n_upd + j)
        dmod_out[...] = everyone[n_mov - 1][...]
        for cp in cps:
            cp.wait_send()

    vm = pl.BlockSpec(memory_space=pltpu.VMEM)
    sds = jax.ShapeDtypeStruct
    out_shape = []
    for w, _, _ in all_states:
        out_shape += [sds(w.shape, F32)] * 4
    out_shape += [sds(p.shape, F32) for p in partials[n_upd:]]
    out_shape.append(sds((N_DEV,) + dmod.shape, F32))
    args = moving + [a for st in all_states for a in st]
    outs, rides = _hosted(
        body, rider, name="small_step", grid=(), out_shape=out_shape,
        in_specs=[vm] * len(args), out_specs=[vm] * len(out_shape),
        scratch_shapes=[pltpu.VMEM((N_DEV,) + a.shape, F32) for a in moving]
        + [pltpu.SemaphoreType.DMA((n_mov, N_DEV - 1)), pltpu.SemaphoreType.DMA((n_mov, N_DEV - 1))],
        compiler_params=_cparams(vmem=VMEM_BIG), args=args)
    return outs, rides


def _adamw(w, grads, m, v, name):
    rows, cols = w.shape
    tr = _row_tile(rows, cols)
    ng = len(grads)

    def body(*refs):
        w_ref = refs[0]
        g_refs = refs[1:1 + ng]
        m_ref, v_ref = refs[1 + ng], refs[2 + ng]
        g_out, d_out, m_out, v_out = refs[3 + ng:]
        g = g_refs[0][...]
        for extra in g_refs[1:]:
            g = g + extra[...]
        g_out[...] = g
        d_out[...], m_out[...], v_out[...] = _adamw_math(w_ref[...], g, m_ref[...], v_ref[...])

    spec = pl.BlockSpec((tr, cols), lambda i: (i, 0))
    out = jax.ShapeDtypeStruct((rows, cols), F32)
    return pl.pallas_call(
        body, name=name, grid=(rows // tr,),
        out_shape=(out, out, out, out),
        in_specs=[spec] * (3 + ng), out_specs=(spec, spec, spec, spec),
        compiler_params=_cparams(("arbitrary",)),
    )(w, *grads, m, v)


def _pack_rows(arrays):
    tile = SUBLANES * LANES
    rows, offsets, at = [], [], 0
    for a in arrays:
        flat = a.reshape(-1).astype(F32)
        n = -(-flat.shape[0] // tile) * tile
        rows.append(jnp.pad(flat, (0, n - flat.shape[0])).reshape(-1, LANES))
        offsets.append(at)
        at += n // LANES
    return jnp.concatenate(rows, axis=0), offsets


def _unpack_rows(packed, offsets, shapes):
    out = []
    for off, shape in zip(offsets, shapes):
        n = 1
        for s in shape:
            n *= s
        nrow = -(-n // LANES)
        out.append(packed[off:off + nrow].reshape(-1)[:n].reshape(shape))
    return out


def _rope_tables(seq):
    half = HEAD_DIM // 2
    inv = np.float32(ROPE_THETA) ** (-np.arange(half, dtype=np.float32) / np.float32(half))
    ang = (np.arange(seq, dtype=np.float32)[:, None] * inv[None, :]).astype(np.float64)
    cos, sin = np.cos(ang).astype(np.float32), np.sin(ang).astype(np.float32)
    cos_t = np.concatenate([cos, cos, cos, cos], axis=1)
    sin_t = np.concatenate([-sin, sin, -sin, sin], axis=1)
    return jnp.asarray(cos_t), jnp.asarray(sin_t)


def kernel(x, c, w_ada, b_ada, g_attn, w_in, na_rpb, sw_sink, g_na_out, g_sw_out, w_out, g_ffn, w_up, conv_w, conv_b, w_down, g_final, loss_target, m_w_ada, m_b_ada, m_g_attn, m_w_in, m_na_rpb, m_sw_sink, m_g_na_out, m_g_sw_out, m_w_out, m_g_ffn, m_w_up, m_conv_w, m_conv_b, m_w_down, m_g_final, v_w_ada, v_b_ada, v_g_attn, v_w_in, v_na_rpb, v_sw_sink, v_g_na_out, v_g_sw_out, v_w_out, v_g_ffn, v_w_up, v_conv_w, v_conv_b, v_w_down, v_g_final):
    batch, seq, d = x.shape
    t = batch * seq
    assert d == D_MODEL and seq % (NA_ROWS * GRID_W) == 0 and seq % TOKEN_TILE == 0 and batch <= SUBLANES
    shard = 2 * lax.axis_index("x") + lax.axis_index("y")
    xt = x.reshape(t, d)
    tgt = loss_target.reshape(t, d)

    c8 = jnp.pad(c, ((0, SUBLANES - batch), (0, 0)))
    w_in_t_s = jnp.transpose(w_in[0]).astype(BF16)
    (mod8, sc_all), (w_in_g,) = _ada_forward(c8, w_ada[0], b_ada, _Rider("gather", [w_in_t_s]))
    mod3 = mod8[:batch].reshape(batch, 1, 6 * d)
    w_in_t = w_in_g.reshape(IN_WIDTH, d)

    cos_t, sin_t = _rope_tables(seq)
    (h1, proj), (w_out_g,) = _in_proj(xt, mod3, g_attn, w_in_t, cos_t, sin_t, seq,
                                      _Rider("gather", [w_out[0].astype(BF16)]))
    n_heads = NA_WIDTH // HEAD_DIM
    n_tiles, n_dc = 2 * NA_ROWS - 2, 2 * NA_COLS - 1
    expand, neg_mask = _na_bias_pattern()
    rpb = na_rpb[0]
    rows2 = jnp.concatenate([rpb[:, :-1, :], rpb[:, 1:, :]], axis=2).reshape(n_heads * n_tiles, 2 * n_dc)
    rows2 = jnp.pad(rows2, ((0, 0), (0, GRID_W - 2 * n_dc)))
    tiles = _na_bias_tiles(rows2, expand, neg_mask).reshape(n_heads, n_tiles, GRID_W, LANES)
    sink = sw_sink[0]
    w_up_b16 = w_up[0].astype(BF16)
    (oa,), (w_up_a,) = _na_forward(proj, tiles, batch, seq, _Rider("gather", [w_up_b16[:d // 2]]))
    (ob,), (w_up_b, conv_w_g) = _sw_forward(proj, sink, batch, seq, _Rider("gather", [w_up_b16[d // 2:], conv_w[0]]))
    w_up_f = (w_up_a, w_up_b)
    w_out_f = w_out_g.reshape(d, d)
    conv_w_f = jnp.transpose(conv_w_g, (1, 0, 2)).reshape(3, D_FF)
    oab, mix, x1, h2 = _out_proj(oa, ob, g_na_out, g_sw_out, w_out_f, xt, mod3, g_ffn, seq)
    (u,), (w_down_g,) = _up_proj(h2, w_up_f, _Rider("gather", [w_down[0].astype(BF16)]))
    w_down_f = w_down_g.reshape(D_FF, d)
    a = _conv_gate(u, conv_w_f, conv_b, batch, seq)
    dx2, dffn, loss_part, dgate_f, dg_final = _down_and_loss(a, w_down_f, x1, mod3, g_final.reshape(1, d), tgt, seq)

    gw_down, gw_down_b = _down_weight_grad(a, dffn)
    blocks = lambda g, rows: g.reshape(N_SHARD, rows // N_SHARD, d)
    (du, gconv_w, gconv_b), (recv_down, own_down) = _ffn_backward(
        dffn, w_down_f, u, conv_w_f, conv_b, batch, seq,
        _Rider("scatter", [blocks(gw_down_b, D_FF)], [blocks(gw_down, D_FF)]))
    (gw_up_top, gw_up_bot, gw_up_top_b, gw_up_bot_b), _ = _up_weight_grad(h2, du)
    (dx1, dmix, dshift_f, dscale_f, dgate_a, dg_ffn), (recv_up_top, own_up_top) = _up_backward(
        du, w_up_f, x1, mod3, g_ffn, dx2, mix, seq, _Rider("scatter", [gw_up_top_b], [gw_up_top]))
    doa, dob, gw_out, gw_out_b, dg_na, dg_sw = _out_backward(dmix, w_out_f, oab, oa, ob, g_na_out, g_sw_out)
    (dqa, dka, dva, dtiles), (recv_up_bot, own_up_bot) = _na_backward(
        proj, doa, tiles, batch, seq, _Rider("scatter", [gw_up_bot_b], [gw_up_bot]))
    (dq_b, dk_b, dv_b, dsink_parts), (recv_out, own_out) = _sw_backward(
        proj, dob, sink, batch, seq, _Rider("scatter", [blocks(gw_out_b, d)], [blocks(gw_out, d)]))
    gx, gw_in_t, gw_in_b, dshift_a, dscale_a, dg_attn = _in_backward(
        (dqa, dka, dva), dq_b, dk_b, dv_b, w_in_t, h1, xt, mod3, g_attn, dx1, cos_t, sin_t, seq)

    red = _na_bias_grad(dtiles.reshape(n_heads * n_tiles, GRID_W, LANES), expand)[:, :2 * n_dc]
    red = red.reshape(n_heads, n_tiles, 2, n_dc)
    zero_row = jnp.zeros((n_heads, 1, n_dc), F32)
    g_rpb = (jnp.concatenate([red[:, :, 0, :], zero_row], axis=1)
             + jnp.concatenate([zero_row, red[:, :, 1, :]], axis=1))
    g_sink = jnp.sum(dsink_parts[:, :, :2, 0], axis=0).reshape(SW_WIDTH // HEAD_DIM)

    dmod = jnp.concatenate([dshift_a, dscale_a, dgate_a, dshift_f, dscale_f, dgate_f], axis=2).reshape(batch, 6 * d)
    rpb_shape = na_rpb.shape[1:]
    states = [(g_attn, m_g_attn, v_g_attn),
              (na_rpb.reshape(rpb_shape), m_na_rpb.reshape(rpb_shape), v_na_rpb.reshape(rpb_shape)),
              (sw_sink, m_sw_sink, v_sw_sink), (g_na_out, m_g_na_out, v_g_na_out), (g_sw_out, m_g_sw_out, v_g_sw_out),
              (g_ffn, m_g_ffn, v_g_ffn), (conv_b, m_conv_b, v_conv_b),
              (g_final.reshape(1, d), m_g_final.reshape(1, d), v_g_final.reshape(1, d))]
    partials = [dg_attn, g_rpb, g_sink.reshape(sw_sink.shape), dg_na, dg_sw, dg_ffn, gconv_b, dg_final,
                gconv_w, loss_part]
    small, (recv_in, own_in) = _small_step(
        partials, states, dmod, (b_ada, m_b_ada, v_b_ada),
        _Rider("scatter", [blocks(gw_in_b, IN_WIDTH)], [blocks(gw_in_t, IN_WIDTH)]))
    r_small = [small[4 * j:4 * j + 4] for j in range(len(states) + 1)]
    g_conv_w_full, loss_sum, dmod_all = small[4 * (len(states) + 1):]
    loss = loss_sum[0, 0]
    mine = [_sum_slots([(recv_in, own_in)], "sum_w_in"), _sum_slots([(recv_out, own_out)], "sum_w_out"),
            _sum_slots([(recv_up_top, own_up_top), (recv_up_bot, own_up_bot)], "sum_w_up"),
            _sum_slots([(recv_down, own_down)], "sum_w_down")]
    theirs = _ride_alone(_Rider("swap", mine), "swap_sibling")
    dmod_rows = jnp.pad(dmod_all, ((0, 0), (0, SUBLANES - batch), (0, 0))).reshape(N_DEV * SUBLANES, 6 * d)
    ncol = w_ada.shape[2]
    g_w_ada = _ada_weight_grad(sc_all, lax.dynamic_slice(dmod_rows, (0, shard * ncol), (N_DEV * SUBLANES, ncol)))
    cshard = conv_w.shape[2]
    g_conv_w = lax.dynamic_slice(g_conv_w_full, (0, shard * cshard), (3, cshard))

    def big(w, m, v, g_parts, name):
        shape = w.shape
        outs = _adamw(w[0], g_parts, m[0], v[0], name)
        return [o.reshape(shape) for o in outs]

    r_w_ada = big(w_ada, m_w_ada, v_w_ada, [g_w_ada], "adamw_w_ada")
    r_w_in = [jnp.transpose(o).reshape(w_in.shape) for o in
              _adamw(jnp.transpose(w_in[0]), [mine[0], theirs[0]], jnp.transpose(m_w_in[0]), jnp.transpose(v_w_in[0]),
                     "adamw_w_in")]
    r_w_out = big(w_out, m_w_out, v_w_out, [mine[1], theirs[1]], "adamw_w_out")
    r_w_up = big(w_up, m_w_up, v_w_up, [mine[2], theirs[2]], "adamw_w_up")
    r_w_down = big(w_down, m_w_down, v_w_down, [mine[3], theirs[3]], "adamw_w_down")

    r_conv_w = big(conv_w, m_conv_w, v_conv_w, [g_conv_w], "adamw_conv_w")

    def pick(k):
        ga_, rpb_, sk_, gna_, gsw_, gf_, cb_, gfin_, b_ = [r[k] for r in r_small]
        return [r_w_ada[k], b_, ga_, r_w_in[k], rpb_.reshape(na_rpb.shape), sk_, gna_, gsw_, r_w_out[k], gf_,
                r_w_up[k], r_conv_w[k], cb_, r_w_down[k], gfin_.reshape(d)]

    return (loss, gx.reshape(batch, seq, d), *pick(0), *pick(1), *pick(2), *pick(3))
```

```python
import functools

import jax
import jax.numpy as jnp
import numpy as np
from jax import lax
from jax.experimental import pallas as pl
from jax.experimental.pallas import tpu as pltpu

F32 = jnp.float32
BF16 = jnp.bfloat16
MESH = pl.DeviceIdType.MESH

D_MODEL = 1024
HEAD_DIM = 64
NA_WIDTH = 512
SW_WIDTH = 512
SW_KV_WIDTH = 128
IN_WIDTH = 2304
D_FF = 2816
GRID_W = 64
NA_ROWS = 8
NA_COLS = 16
SW_BLOCK = 128
ROPE_THETA = 10000.0
EPS = 1e-6
NEG = -1e30
QK_SCALE = HEAD_DIM ** -0.5

ADAM_LR = 0.001
ADAM_B1 = 0.9
ADAM_B2 = 0.999
ADAM_EPS = 1e-08
ADAM_WD = 0.01
ADAM_STEP = 10

N_SHARD = 4
N_DEV = 8
LANES = 128
SUBLANES = 8
TOKEN_TILE = 512
FF_TILE = 256
CONV_CHUNK = 64
NA_GROUP = 4
SW_GROUP_BLOCKS = 4
VMEM_BIG = 56 * 1024 * 1024


def _mm(a, b):
    return jnp.dot(a, b, preferred_element_type=F32)


def _mm_nt(a, b):
    return lax.dot_general(a, b, (((1,), (1,)), ((), ())), preferred_element_type=F32)


def _mm_tn(a, b):
    return lax.dot_general(a, b, (((0,), (0,)), ((), ())), preferred_element_type=F32)


def _cparams(sem=None, vmem=None):
    kw = {}
    if sem is not None:
        kw["dimension_semantics"] = sem
    if vmem is not None:
        kw["vmem_limit_bytes"] = vmem
    return pltpu.CompilerParams(**kw)


def _resident(shape):
    return pl.BlockSpec(shape, lambda i: (0,) * len(shape), pipeline_mode=pl.Buffered(1))


def _sigmoid(x):
    return 1.0 / (1.0 + jnp.exp(-x))


def _rms_stats(x):
    r = lax.rsqrt(jnp.mean(x * x, axis=-1, keepdims=True) + EPS)
    return r, x * r


def _rms_bwd(dxn, xn, r):
    return r * (dxn - xn * jnp.mean(dxn * xn, axis=-1, keepdims=True))


def _my_pos():
    return lax.axis_index("x"), lax.axis_index("y"), lax.axis_index("c")


def _flip(v, bit):
    return 1 - v if bit else v


def _ada_forward(c8, w_ada, b_ada, rider):
    d = c8.shape[1]
    ncol = w_ada.shape[1]

    def body(c_ref, w_ref, b_ref, mod_ref, sc_ref, m_scr, mod_buf, ssem, rsem, ssem2, rsem2):
        x, y, c = _my_pos()
        me = 4 * x + 2 * y + c
        shard = 2 * x + y
        cv = c_ref[...]
        my_rows = pl.ds(pl.multiple_of(me * SUBLANES, SUBLANES), SUBLANES)
        sc_ref[my_rows, :] = cv * _sigmoid(cv)

        def copy1(k):
            peer = (_flip(x, (k >> 2) & 1), _flip(y, (k >> 1) & 1), _flip(c, k & 1))
            return pltpu.make_async_remote_copy(
                src_ref=sc_ref.at[my_rows, :], dst_ref=sc_ref.at[my_rows, :],
                send_sem=ssem.at[k - 1], recv_sem=rsem.at[k - 1], device_id=peer, device_id_type=MESH)

        sends = [copy1(k) for k in range(1, N_DEV)]
        for cp in sends:
            cp.start()
        for cp in sends:
            cp.wait_recv()
        m_scr[...] = _mm(sc_ref[...].astype(BF16), w_ref[...].astype(BF16))

        def copy2(k):
            px, py = _flip(x, (k >> 1) & 1), _flip(y, k & 1)
            rows = pl.ds(pl.multiple_of((4 * px + 2 * py + c) * SUBLANES, SUBLANES), SUBLANES)
            return pltpu.make_async_remote_copy(
                src_ref=m_scr.at[rows, :], dst_ref=mod_buf.at[shard],
                send_sem=ssem2.at[k - 1], recv_sem=rsem2.at[k - 1], device_id=(px, py, c), device_id_type=MESH)

        sends2 = [copy2(k) for k in range(1, N_SHARD)]
        for cp in sends2:
            cp.start()
        mod_buf[shard] = m_scr[my_rows, :]
        for cp in sends2:
            cp.wait_recv()
        for s in range(N_SHARD):
            mod_ref[:, s * ncol:(s + 1) * ncol] = mod_buf[s] + b_ref[:, s * ncol:(s + 1) * ncol]
        for cp in sends + sends2:
            cp.wait_send()

    vm = pl.BlockSpec(memory_space=pltpu.VMEM)
    return _hosted(
        body, rider, name="ada_forward", grid=(),
        out_shape=(jax.ShapeDtypeStruct((SUBLANES, N_SHARD * ncol), F32),
                   jax.ShapeDtypeStruct((N_DEV * SUBLANES, d), F32)),
        in_specs=[vm, vm, vm], out_specs=(vm, vm),
        scratch_shapes=[pltpu.VMEM((N_DEV * SUBLANES, ncol), F32), pltpu.VMEM((N_SHARD, SUBLANES, ncol), F32),
                        pltpu.SemaphoreType.DMA((N_DEV - 1,)), pltpu.SemaphoreType.DMA((N_DEV - 1,)),
                        pltpu.SemaphoreType.DMA((N_SHARD - 1,)), pltpu.SemaphoreType.DMA((N_SHARD - 1,))],
        compiler_params=_cparams(vmem=VMEM_BIG), args=[c8, w_ada, b_ada])


class _Rider:
    def __init__(self, kind, srcs, owns=()):
        self.kind, self.srcs, self.owns = kind, list(srcs), list(owns)
        n = len(self.srcs)
        sds = jax.ShapeDtypeStruct
        dma = pltpu.SemaphoreType.DMA
        if kind == "gather":
            self.out_shapes = [sds((N_SHARD,) + s.shape, s.dtype) for s in self.srcs]
            self.sems = [dma((n, N_SHARD - 1)), dma((n, N_SHARD - 1)), dma((n,)),
                         dma((n, N_SHARD - 1)), dma((n, N_SHARD - 1))]
        elif kind == "scatter":
            self.out_shapes = ([sds((N_SHARD - 1,) + s.shape[1:], s.dtype) for s in self.srcs]
                               + [sds(o.shape[1:], o.dtype) for o in self.owns])
            self.sems = [dma((n, N_SHARD - 1)), dma((n, N_SHARD - 1)), dma((max(len(self.owns), 1),))]
        else:
            self.out_shapes = [sds(s.shape, s.dtype) for s in self.srcs]
            self.sems = [dma((n,)), dma((n,))]

    @property
    def inputs(self):
        return self.srcs + self.owns

    def _halved(self, i):
        a = self.srcs[i]
        tile_rows = SUBLANES * (4 // jnp.dtype(a.dtype).itemsize)
        return self.kind == "gather" and a.shape[0] % (2 * tile_rows) == 0

    def copies(self, ins, outs, sems):
        n = len(self.srcs)
        x, y, c = _my_pos()
        shard = 2 * x + y
        local, remote, relay = [], [], []
        if self.kind == "swap":
            ssem, rsem = sems
            for i in range(n):
                remote.append(pltpu.make_async_remote_copy(
                    src_ref=ins[i], dst_ref=outs[i], send_sem=ssem.at[i], recv_sem=rsem.at[i],
                    device_id=(x, y, 1 - c), device_id_type=MESH))
            return local, remote, relay
        if self.kind == "gather":
            ssem, rsem, lsem, ssem2, rsem2 = sems
        else:
            ssem, rsem, lsem = sems
        for i in range(n):
            if self.kind == "gather":
                local.append(pltpu.make_async_copy(ins[i], outs[i].at[shard], lsem.at[i]))
                half = ins[i].shape[0] // 2
                mine = pl.ds(pl.multiple_of(c * half, half), half) if self._halved(i) else None
            for k in range(1, N_SHARD):
                px, py = _flip(x, (k >> 1) & 1), _flip(y, k & 1)
                if self.kind == "gather":
                    src, dst = ins[i], outs[i].at[shard]
                    if mine is not None:
                        src, dst = src.at[mine], dst.at[mine]
                        got = outs[i].at[2 * px + py].at[mine]
                        relay.append(pltpu.make_async_remote_copy(
                            src_ref=got, dst_ref=got, send_sem=ssem2.at[i, k - 1], recv_sem=rsem2.at[i, k - 1],
                            device_id=(x, y, 1 - c), device_id_type=MESH))
                else:
                    src, dst = ins[i].at[2 * px + py], outs[i].at[k - 1]
                remote.append(pltpu.make_async_remote_copy(
                    src_ref=src, dst_ref=dst, send_sem=ssem.at[i, k - 1], recv_sem=rsem.at[i, k - 1],
                    device_id=(px, py, c), device_id_type=MESH))
        if self.kind == "scatter":
            for i in range(len(self.owns)):
                local.append(pltpu.make_async_copy(ins[n + i].at[shard], outs[n + i], lsem.at[i]))
        return local, remote, relay

    def start(self, ins, outs, sems):
        _, remote, _ = self.copies(ins, outs, sems)
        for cp in remote:
            cp.start()

    def wait(self, ins, outs, sems):
        local, remote, relay = self.copies(ins, outs, sems)
        for cp in local:
            cp.start()
        for cp in remote:
            cp.wait_recv()
        for cp in relay:
            cp.start()
        for cp in relay:
            cp.wait_recv()
        for cp in remote + relay:
            cp.wait_send()
        for cp in local:
            cp.wait()


def _hosted(body, rider, *, name, grid, out_shape, in_specs, out_specs, scratch_shapes, compiler_params, args):
    out_shape, out_specs = list(out_shape), list(out_specs)
    if rider is None:
        outs = pl.pallas_call(body, name=name, grid=grid, out_shape=tuple(out_shape), in_specs=list(in_specs),
                              out_specs=tuple(out_specs), scratch_shapes=list(scratch_shapes),
                              compiler_params=compiler_params)(*args)
        return list(outs), []
    n_in, n_out, n_scr = len(in_specs), len(out_shape), len(scratch_shapes)
    nr_in, nr_out = len(rider.inputs), len(rider.out_shapes)
    n_steps = 1
    for size in grid:
        n_steps *= size

    def full(*refs):
        ins, refs = refs[:n_in], refs[n_in:]
        r_in, refs = refs[:nr_in], refs[nr_in:]
        outs, refs = refs[:n_out], refs[n_out:]
        r_out, refs = refs[:nr_out], refs[nr_out:]
        scr, sems = refs[:n_scr], refs[n_scr:]
        if grid:
            step = 0
            for ax, size in enumerate(grid):
                step = step * size + pl.program_id(ax)
            pl.when(step == 0)(lambda: rider.start(r_in, r_out, sems))
            body(*ins, *outs, *scr)
            pl.when(step == n_steps - 1)(lambda: rider.wait(r_in, r_out, sems))
        else:
            rider.start(r_in, r_out, sems)
            body(*ins, *outs, *scr)
            rider.wait(r_in, r_out, sems)

    hbm = pl.BlockSpec(memory_space=pl.ANY)
    res = pl.pallas_call(
        full, name=name, grid=grid, out_shape=tuple(out_shape + rider.out_shapes),
        in_specs=list(in_specs) + [hbm] * nr_in, out_specs=tuple(out_specs + [hbm] * nr_out),
        scratch_shapes=list(scratch_shapes) + rider.sems, compiler_params=compiler_params,
    )(*args, *rider.inputs)
    return list(res[:n_out]), list(res[n_out:])


def _ride_alone(rider, name):
    return _hosted(lambda: None, rider, name=name, grid=(), out_shape=[], in_specs=[], out_specs=[], scratch_shapes=[],
                   compiler_params=_cparams(), args=[])[1]


def _allreduce_small(packed, rider=None):
    r = packed.shape[0]

    def body(p_ref, sum_ref, all_ref, ssem, rsem):
        x, y, c = _my_pos()
        me = 4 * x + 2 * y + c
        all_ref[me] = p_ref[...]
        cps = []
        for k in range(1, N_DEV):
            peer = (_flip(x, (k >> 2) & 1), _flip(y, (k >> 1) & 1), _flip(c, k & 1))
            cps.append(pltpu.make_async_remote_copy(
                src_ref=all_ref.at[me], dst_ref=all_ref.at[me], send_sem=ssem.at[k - 1], recv_sem=rsem.at[k - 1],
                device_id=peer, device_id_type=MESH))
        for cp in cps:
            cp.start()
        for cp in cps:
            cp.wait_recv()
        acc = all_ref[0]
        for dev in range(1, N_DEV):
            acc = acc + all_ref[dev]
        sum_ref[...] = acc
        for cp in cps:
            cp.wait_send()

    vm = pl.BlockSpec(memory_space=pltpu.VMEM)
    return _hosted(
        body, rider, name="allreduce_small", grid=(),
        out_shape=[jax.ShapeDtypeStruct((r, LANES), F32), jax.ShapeDtypeStruct((N_DEV, r, LANES), F32)],
        in_specs=[vm], out_specs=[vm, vm],
        scratch_shapes=[pltpu.SemaphoreType.DMA((N_DEV - 1,)), pltpu.SemaphoreType.DMA((N_DEV - 1,))],
        compiler_params=_cparams(), args=[packed])


def _rope_rot(t):
    w = t.shape[1]
    lane = lax.broadcasted_iota(jnp.int32, t.shape, 1)
    first = (lane % HEAD_DIM) < (HEAD_DIM // 2)
    return jnp.where(first, pltpu.roll(t, w - HEAD_DIM // 2, 1), pltpu.roll(t, HEAD_DIM // 2, 1))


def _in_proj(x, mod3, g_attn, w_in_t, cos_t, sin_t, seq, rider=None):
    t, d = x.shape
    tm = TOKEN_TILE
    per_seq = seq // tm
    rope_lo, rope_hi = 3 * NA_WIDTH, 3 * NA_WIDTH + SW_WIDTH + SW_KV_WIDTH
    n_rep = (rope_hi - rope_lo) // LANES

    def body(x_ref, mod_ref, g_ref, w_ref, cos_ref, sin_ref, h_ref, p_ref):
        r, xn = _rms_stats(x_ref[...])
        shift, scale = mod_ref[0, :, 0:d], mod_ref[0, :, d:2 * d]
        hb = ((xn * g_ref[...]) * (1.0 + scale) + shift).astype(BF16)
        h_ref[...] = hb
        p_ref[:, :rope_lo] = _mm_nt(hb, w_ref[:rope_lo, :]).astype(BF16)
        pr = _mm_nt(hb, w_ref[rope_lo:rope_hi, :])
        cos = jnp.concatenate([cos_ref[...]] * n_rep, axis=1)
        sin = jnp.concatenate([sin_ref[...]] * n_rep, axis=1)
        p_ref[:, rope_lo:rope_hi] = (pr * cos + _rope_rot(pr) * sin).astype(BF16)
        p_ref[:, rope_hi:] = _mm_nt(hb, w_ref[rope_hi:, :]).astype(BF16)

    return _hosted(
        body, rider, name="in_proj", grid=(t // tm,),
        out_shape=[jax.ShapeDtypeStruct((t, d), BF16), jax.ShapeDtypeStruct((t, IN_WIDTH), BF16)],
        in_specs=[pl.BlockSpec((tm, d), lambda i: (i, 0)),
                  pl.BlockSpec((1, 1, 6 * d), lambda i: (i // per_seq, 0, 0)),
                  pl.BlockSpec((1, d), lambda i: (0, 0)),
                  pl.BlockSpec((IN_WIDTH, d), lambda i: (0, 0)),
                  pl.BlockSpec((tm, LANES), lambda i: (i % per_seq, 0)),
                  pl.BlockSpec((tm, LANES), lambda i: (i % per_seq, 0))],
        out_specs=[pl.BlockSpec((tm, d), lambda i: (i, 0)), pl.BlockSpec((tm, IN_WIDTH), lambda i: (i, 0))],
        scratch_shapes=[], compiler_params=_cparams(("arbitrary",), VMEM_BIG),
        args=[x, mod3, g_attn, w_in_t, cos_t, sin_t])


def _na_bias_pattern():
    n_dc = 2 * NA_COLS - 1
    j = np.arange(GRID_W)[:, None]
    m = np.arange(GRID_W * LANES)[None, :]
    q, lane = m // LANES, m % LANES
    k = lane % GRID_W
    cs = np.clip(q - NA_COLS // 2, 0, GRID_W - NA_COLS)
    ok = (k >= cs) & (k < cs + NA_COLS)
    hit = ok & (j < 2 * n_dc) & (lane // GRID_W == j // n_dc) & (k - q + (NA_COLS - 1) == j % n_dc)
    return jnp.asarray(hit.astype(np.float32)), jnp.asarray(np.where(ok, 0.0, NEG).astype(np.float32))


def _na_bias_tiles(rows2, expand, mask):
    n, width = rows2.shape[0], expand.shape[1]
    q_step = 16
    step = q_step * LANES

    def body(r_ref, e_ref, m_ref, o_ref):
        flat = jnp.dot(r_ref[...], e_ref[...], precision=lax.Precision.HIGHEST,
                       preferred_element_type=F32) + m_ref[...]
        for qq in range(q_step):
            o_ref[:, qq, :] = flat[:, qq * LANES:(qq + 1) * LANES]

    return pl.pallas_call(
        body, name="na_bias_tiles", grid=(width // step,),
        out_shape=jax.ShapeDtypeStruct((n, GRID_W, LANES), F32),
        in_specs=[pl.BlockSpec(rows2.shape, lambda i: (0, 0)), pl.BlockSpec((expand.shape[0], step), lambda i: (0, i)),
                  pl.BlockSpec((1, step), lambda i: (0, i))],
        out_specs=pl.BlockSpec((n, q_step, LANES), lambda i: (0, i, 0)),
        compiler_params=_cparams(("arbitrary",)),
    )(rows2, expand, mask)


def _na_prepare(k_ref, v_ref, km, vm):
    lane = lax.broadcasted_iota(jnp.int32, k_ref.shape, 1)
    low = lane < HEAD_DIM
    kv = k_ref[...]
    vv = v_ref[...]
    zero = jnp.zeros_like(kv)
    km[0] = jnp.where(low, kv, zero)
    km[1] = jnp.where(low, zero, kv)
    vm[0] = jnp.where(low, vv, zero)
    vm[1] = jnp.where(low, zero, vv)


def _na_window(r, n_rows):
    rs = jnp.clip(r - NA_ROWS // 2, 0, n_rows - NA_ROWS)
    return rs, r - rs


def _na_pair_window(ref, wrows):
    return jnp.concatenate([ref[0, wrows, :], ref[1, wrows, :]], axis=0)


def _na_scores(q, k2, tp_ref, off):
    bias = jnp.concatenate([tp_ref[h, 2 * w - off + (NA_ROWS - 1)] for h in range(2) for w in range(NA_ROWS // 2)],
                           axis=1)
    return _mm_nt(q, k2) * QK_SCALE + bias


def _pair_softmax(s):
    win = s.shape[1] // 2
    halves = []
    for h in range(2):
        sh = s[:, h * win:(h + 1) * win]
        e = jnp.exp(sh - jnp.max(sh, axis=-1, keepdims=True))
        halves.append(e / jnp.sum(e, axis=-1, keepdims=True))
    return jnp.concatenate(halves, axis=1)


def _na_forward(proj, tiles, batch, seq, rider=None):
    t = proj.shape[0]
    n_rows = seq // GRID_W
    n_pairs = NA_WIDTH // LANES
    win = NA_ROWS * GRID_W

    def body(q_ref, k_ref, v_ref, tp_ref, o_ref, km, vm):
        _na_prepare(k_ref, v_ref, km, vm)

        def scores(r):
            rs, off = _na_window(r, n_rows)
            rows = pl.ds(pl.multiple_of(r * GRID_W, GRID_W), GRID_W)
            wrows = pl.ds(pl.multiple_of(rs * GRID_W, GRID_W), win)
            return rows, wrows, _na_scores(q_ref[rows, :], _na_pair_window(km, wrows), tp_ref, off)

        def finish(rows, wrows, s):
            o_ref[rows, :] = _mm(_pair_softmax(s).astype(BF16), _na_pair_window(vm, wrows))

        def row_group(i, carry):
            for state in [scores(NA_GROUP * i + j) for j in range(NA_GROUP)]:
                finish(*state)
            return carry

        lax.fori_loop(0, n_rows // NA_GROUP, row_group, 0)

    return _hosted(
        body, rider, name="na_forward", grid=(batch, n_pairs),
        out_shape=[jax.ShapeDtypeStruct((t, NA_WIDTH), F32)],
        in_specs=[pl.BlockSpec((seq, LANES), lambda b, p: (b, p)),
                  pl.BlockSpec((seq, LANES), lambda b, p: (b, n_pairs + p)),
                  pl.BlockSpec((seq, LANES), lambda b, p: (b, 2 * n_pairs + p)),
                  pl.BlockSpec((2, 2 * NA_ROWS - 2, GRID_W, LANES), lambda b, p: (p, 0, 0, 0))],
        out_specs=[pl.BlockSpec((seq, LANES), lambda b, p: (b, p))],
        scratch_shapes=[pltpu.VMEM((2, seq, LANES), BF16), pltpu.VMEM((2, seq, LANES), BF16)],
        compiler_params=_cparams(("arbitrary", "arbitrary")), args=[proj, proj, proj, tiles])


def _sw_prepare(kv_ref, g, dst_lo, dst_hi, seq):
    lane = lax.broadcasted_iota(jnp.int32, kv_ref.shape, 1)
    mine = (lane // HEAD_DIM) == g
    kg = jnp.where(mine, kv_ref[...].astype(F32), 0.0)
    kr = pltpu.roll(kg, HEAD_DIM, 1)
    first = g == 0
    zero = jnp.zeros((SW_BLOCK, LANES), BF16)
    for dst, val in ((dst_lo, jnp.where(first, kg, kr)), (dst_hi, jnp.where(first, kr, kg))):
        dst[0:SW_BLOCK, :] = zero
        dst[SW_BLOCK:SW_BLOCK + seq, :] = val.astype(BF16)
        dst[SW_BLOCK + seq:, :] = zero


def _sw_mask(n, seq):
    qi = lax.broadcasted_iota(jnp.int32, (SW_BLOCK, 3 * SW_BLOCK), 0)
    kj = lax.broadcasted_iota(jnp.int32, (SW_BLOCK, 3 * SW_BLOCK), 1)
    kpos = n * SW_BLOCK - SW_BLOCK + kj
    return (jnp.abs(qi + SW_BLOCK - kj) <= SW_BLOCK) & (kpos >= 0) & (kpos < seq)


def _sw_probs(s2, ok, sinks):
    band = s2.shape[1] // 2
    halves, sink_p = [], []
    for i in range(2):
        s = jnp.where(ok, s2[:, i * band:(i + 1) * band], NEG)
        m = jnp.maximum(jnp.max(s, axis=-1, keepdims=True), sinks[i])
        p = jnp.exp(s - m)
        es = jnp.exp(sinks[i] - m)
        den = jnp.sum(p, axis=-1, keepdims=True) + es
        halves.append(p / den)
        sink_p.append(es / den)
    return jnp.concatenate(halves, axis=1), sink_p


def _sw_forward(proj, sink, batch, seq, rider=None):
    t = proj.shape[0]
    n_pairs = SW_WIDTH // LANES
    q_blk = 3 * NA_WIDTH // LANES
    k_blk = q_blk + n_pairs
    n_blocks = seq // SW_BLOCK
    pad = seq + 2 * SW_BLOCK

    def body(sink_ref, q_ref, k_ref, v_ref, o_ref, k_lo, k_hi, v_lo, v_hi):
        hp = pl.program_id(1)
        g = hp // 2
        _sw_prepare(k_ref, g, k_lo, k_hi, seq)
        _sw_prepare(v_ref, g, v_lo, v_hi, seq)

        sinks = (sink_ref[2 * hp], sink_ref[2 * hp + 1])

        def scores(n):
            rows = pl.ds(pl.multiple_of(n * SW_BLOCK, SW_BLOCK), SW_BLOCK)
            wrows = pl.ds(pl.multiple_of(n * SW_BLOCK, SW_BLOCK), 3 * SW_BLOCK)
            k2 = jnp.concatenate([k_lo[wrows, :], k_hi[wrows, :]], axis=0)
            return n, rows, wrows, _mm_nt(q_ref[rows, :], k2) * QK_SCALE

        def finish(n, rows, wrows, s2):
            p, _ = _sw_probs(s2, _sw_mask(n, seq), sinks)
            v2 = jnp.concatenate([v_lo[wrows, :], v_hi[wrows, :]], axis=0)
            o_ref[rows, :] = _mm(p.astype(BF16), v2)

        def block_group(i, carry):
            for state in [scores(SW_GROUP_BLOCKS * i + j) for j in range(SW_GROUP_BLOCKS)]:
                finish(*state)
            return carry

        lax.fori_loop(0, n_blocks // SW_GROUP_BLOCKS, block_group, 0)

    return _hosted(
        body, rider, name="sw_forward", grid=(batch, n_pairs),
        out_shape=[jax.ShapeDtypeStruct((t, SW_WIDTH), F32)],
        in_specs=[pl.BlockSpec(memory_space=pltpu.SMEM),
                  pl.BlockSpec((seq, LANES), lambda b, p: (b, q_blk + p)),
                  pl.BlockSpec((seq, LANES), lambda b, p: (b, k_blk)),
                  pl.BlockSpec((seq, LANES), lambda b, p: (b, k_blk + 1))],
        out_specs=[pl.BlockSpec((seq, LANES), lambda b, p: (b, p))],
        scratch_shapes=[pltpu.VMEM((pad, LANES), BF16)] * 4,
        compiler_params=_cparams(("arbitrary", "arbitrary")), args=[sink, proj, proj, proj])


def _out_proj(oa, ob, g_na, g_sw, w_out, x, mod3, g_ffn, seq):
    t, d = x.shape
    tm = TOKEN_TILE
    per_seq = seq // tm

    def body(oa_ref, ob_ref, gna_ref, gsw_ref, w_ref, x_ref, mod_ref, gf_ref, oab_ref, mix_ref, x1_ref, h2_ref):
        _, na = _rms_stats(oa_ref[...])
        _, nb = _rms_stats(ob_ref[...])
        oab = jnp.concatenate([na * gna_ref[...], nb * gsw_ref[...]], axis=1).astype(BF16)
        oab_ref[...] = oab
        mix = _mm(oab, w_ref[...])
        mix_ref[...] = mix
        gate_a = mod_ref[0, :, 2 * d:3 * d]
        shift_f, scale_f = mod_ref[0, :, 3 * d:4 * d], mod_ref[0, :, 4 * d:5 * d]
        x1 = x_ref[...] + gate_a * mix
        x1_ref[...] = x1
        _, xn = _rms_stats(x1)
        h2_ref[...] = ((xn * gf_ref[...]) * (1.0 + scale_f) + shift_f).astype(BF16)

    tile = lambda w: pl.BlockSpec((tm, w), lambda i: (i, 0))
    vec = lambda w: pl.BlockSpec((1, w), lambda i: (0, 0))
    return pl.pallas_call(
        body, name="out_proj", grid=(t // tm,),
        out_shape=(jax.ShapeDtypeStruct((t, d), BF16), jax.ShapeDtypeStruct((t, d), F32),
                   jax.ShapeDtypeStruct((t, d), F32), jax.ShapeDtypeStruct((t, d), BF16)),
        in_specs=[tile(NA_WIDTH), tile(SW_WIDTH), vec(NA_WIDTH), vec(SW_WIDTH),
                  pl.BlockSpec((d, d), lambda i: (0, 0)), tile(d),
                  pl.BlockSpec((1, 1, 6 * d), lambda i: (i // per_seq, 0, 0)), vec(d)],
        out_specs=(tile(d), tile(d), tile(d), tile(d)),
        compiler_params=_cparams(("arbitrary",), VMEM_BIG),
    )(oa, ob, g_na, g_sw, w_out, x, mod3, g_ffn)


def _up_proj(h2, w_up_halves, rider=None):
    t, d = h2.shape
    tm = TOKEN_TILE
    w_a, w_b = w_up_halves
    half, wcol = w_a.shape[1], w_a.shape[2]

    def body(h_ref, wa_ref, wb_ref, u_ref):
        u_ref[0] = (_mm(h_ref[:, :half], wa_ref[0]) + _mm(h_ref[:, half:], wb_ref[0])).astype(BF16)

    w_spec = pl.BlockSpec((1, half, wcol), lambda j, i: (j, 0, 0))
    return _hosted(
        body, rider, name="up_proj", grid=(N_SHARD, t // tm),
        out_shape=[jax.ShapeDtypeStruct((2, t, D_FF), BF16)],
        in_specs=[pl.BlockSpec((tm, d), lambda j, i: (i, 0)), w_spec, w_spec],
        out_specs=[pl.BlockSpec((1, tm, wcol), lambda j, i: (j // 2, i, j % 2))],
        scratch_shapes=[], compiler_params=_cparams(("arbitrary", "arbitrary"), VMEM_BIG), args=[h2, w_a, w_b])


def _taps_chunk(load, s, rows, seq):
    halo = 2 * SUBLANES
    cur = load(s, rows)
    above = load(pl.multiple_of(jnp.maximum(s - halo, 0), halo), halo)
    below = load(pl.multiple_of(jnp.minimum(s + rows, seq - halo), halo), halo)
    up = jnp.where(s > 0, above[halo - 1:halo, :], 0.0)
    dn = jnp.where(s + rows < seq, below[0:1, :], 0.0)
    row = lax.broadcasted_iota(jnp.int32, cur.shape, 0)
    prev = jnp.where(row == 0, up, pltpu.roll(cur, 1, 0))
    nxt = jnp.where(row == rows - 1, dn, pltpu.roll(cur, rows - 1, 0))
    return cur, prev, nxt


def _conv_gate(u, conv_w, conv_b, batch, seq):
    t = u.shape[1]
    cw = FF_TILE
    rows = CONV_CHUNK

    def body(u_ref, w_ref, b_ref, a_ref):
        def chunk(i, carry):
            s = pl.multiple_of(i * rows, rows)
            gt, prev, nxt = _taps_chunk(lambda at, n: u_ref[1, pl.ds(at, n), :].astype(F32), s, rows, seq)
            gc = prev * w_ref[0:1, :] + gt * w_ref[1:2, :] + nxt * w_ref[2:3, :] + b_ref[...]
            a_ref[pl.ds(s, rows), :] = ((gc * _sigmoid(gc)) * u_ref[0, pl.ds(s, rows), :].astype(F32)).astype(BF16)
            return carry

        lax.fori_loop(0, seq // rows, chunk, 0)

    return pl.pallas_call(
        body, name="conv_gate", grid=(batch, D_FF // cw),
        out_shape=jax.ShapeDtypeStruct((t, D_FF), BF16),
        in_specs=[pl.BlockSpec((2, seq, cw), lambda b, j: (0, b, j)),
                  pl.BlockSpec((3, cw), lambda b, j: (0, j)), pl.BlockSpec((1, cw), lambda b, j: (0, j))],
        out_specs=pl.BlockSpec((seq, cw), lambda b, j: (b, j)),
        compiler_params=_cparams(("arbitrary", "arbitrary"), VMEM_BIG),
    )(u, conv_w, conv_b)


def _down_and_loss(a, w_down, x1, mod3, g_final, target, seq):
    t, d = x1.shape
    tm = TOKEN_TILE
    per_seq = seq // tm
    batch = t // seq

    def body(a_ref, w_ref, x1_ref, mod_ref, g_ref, tgt_ref, dx2_ref, dffn_ref, loss_ref, dgate_ref, dg_ref):
        i = pl.program_id(0)
        f = _mm(a_ref[...], w_ref[...])
        gate_f = mod_ref[0, :, 5 * d:6 * d]
        x2 = x1_ref[...] + gate_f * f
        r, xn = _rms_stats(x2)
        err = xn * g_ref[...] - tgt_ref[...]
        part = 0.5 * jnp.sum(jnp.mean(err * err, axis=-1, keepdims=True))
        dy = err / d
        dx2 = _rms_bwd(dy * g_ref[...], xn, r)
        dx2_ref[...] = dx2
        dffn_ref[...] = (dx2 * gate_f).astype(BF16)

        @pl.when(i == 0)
        def _():
            loss_ref[...] = jnp.zeros_like(loss_ref)
            dg_ref[...] = jnp.zeros_like(dg_ref)

        @pl.when(i % per_seq == 0)
        def _():
            dgate_ref[...] = jnp.zeros_like(dgate_ref)

        loss_ref[...] += part
        dg_ref[...] += jnp.sum(dy * xn, axis=0, keepdims=True)
        dgate_ref[0] += jnp.sum(dx2 * f, axis=0, keepdims=True)

    tile = lambda w: pl.BlockSpec((tm, w), lambda i: (i, 0))
    return pl.pallas_call(
        body, name="down_loss", grid=(t // tm,),
        out_shape=(jax.ShapeDtypeStruct((t, d), F32), jax.ShapeDtypeStruct((t, d), BF16),
                   jax.ShapeDtypeStruct((SUBLANES, LANES), F32), jax.ShapeDtypeStruct((batch, 1, d), F32),
                   jax.ShapeDtypeStruct((1, d), F32)),
        in_specs=[tile(D_FF), _resident((D_FF, d)), tile(d),
                  pl.BlockSpec((1, 1, 6 * d), lambda i: (i // per_seq, 0, 0)),
                  pl.BlockSpec((1, d), lambda i: (0, 0)), tile(d)],
        out_specs=(tile(d), tile(d), pl.BlockSpec((SUBLANES, LANES), lambda i: (0, 0)),
                   pl.BlockSpec((1, 1, d), lambda i: (i // per_seq, 0, 0)), pl.BlockSpec((1, d), lambda i: (0, 0))),
        compiler_params=_cparams(("arbitrary",), VMEM_BIG),
    )(a, w_down, x1, mod3, g_final, target)


def _down_weight_grad(a, dffn):
    t, dff = a.shape
    d = dffn.shape[1]
    tk = TOKEN_TILE
    n_k = t // tk

    def body(a_ref, df_ref, g_ref, gb_ref):
        k = pl.program_id(0)

        @pl.when(k == 0)
        def _():
            g_ref[...] = jnp.zeros_like(g_ref)

        g_ref[...] += _mm_tn(a_ref[...], df_ref[...])

        @pl.when(k == n_k - 1)
        def _():
            gb_ref[...] = g_ref[...].astype(BF16)

    whole = pl.BlockSpec((dff, d), lambda k: (0, 0))
    return pl.pallas_call(
        body, name="down_weight_grad", grid=(n_k,),
        out_shape=(jax.ShapeDtypeStruct((dff, d), F32), jax.ShapeDtypeStruct((dff, d), BF16)),
        in_specs=[pl.BlockSpec((tk, dff), lambda k: (k, 0)), pl.BlockSpec((tk, d), lambda k: (k, 0))],
        out_specs=(whole, whole),
        compiler_params=_cparams(("arbitrary",), VMEM_BIG),
    )(a, dffn)


def _ffn_backward(dffn, w_down, u, conv_w, conv_b, batch, seq, rider=None):
    t, d = dffn.shape
    cw = FF_TILE
    rows = CONV_CHUNK

    def body(df_ref, wd_ref, u_ref, w_ref, b_ref, du_ref, gcw_ref, gcb_ref, da_scr, dgc_scr):
        b = pl.program_id(1)
        da_scr[...] = _mm_nt(df_ref[...], wd_ref[...])

        @pl.when(b == 0)
        def _():
            gcw_ref[...] = jnp.zeros_like(gcw_ref)
            gcb_ref[...] = jnp.zeros_like(gcb_ref)

        def fold(v):
            return jnp.sum(v.reshape(rows // SUBLANES, SUBLANES, cw), axis=0)

        def chunk(i, carry):
            s = pl.multiple_of(i * rows, rows)
            here = pl.ds(s, rows)
            gt, prev, nxt = _taps_chunk(lambda at, n: u_ref[1, pl.ds(at, n), :].astype(F32), s, rows, seq)
            val, da = u_ref[0, here, :].astype(F32), da_scr[here, :]
            gc = prev * w_ref[0:1, :] + gt * w_ref[1:2, :] + nxt * w_ref[2:3, :] + b_ref[...]
            sg = _sigmoid(gc)
            sl = gc * sg
            du_ref[0, here, :] = (da * sl).astype(BF16)
            dgc = (da * val) * (sg * (1.0 + gc * (1.0 - sg)))
            dgc_scr[here, :] = dgc
            cb, c0, c1, c2 = carry
            return cb + fold(dgc), c0 + fold(dgc * prev), c1 + fold(dgc * gt), c2 + fold(dgc * nxt)

        zero = jnp.zeros((SUBLANES, cw), F32)
        cb, c0, c1, c2 = lax.fori_loop(0, seq // rows, chunk, (zero, zero, zero, zero))
        gcb_ref[...] += jnp.sum(cb, axis=0, keepdims=True)
        gcw_ref[0:1, :] += jnp.sum(c0, axis=0, keepdims=True)
        gcw_ref[1:2, :] += jnp.sum(c1, axis=0, keepdims=True)
        gcw_ref[2:3, :] += jnp.sum(c2, axis=0, keepdims=True)

        def chunk2(i, carry):
            s = pl.multiple_of(i * rows, rows)
            dgc, dprev, dnxt = _taps_chunk(lambda at, n: dgc_scr[pl.ds(at, n), :], s, rows, seq)
            du_ref[1, pl.ds(s, rows), :] = (dnxt * w_ref[0:1, :] + dgc * w_ref[1:2, :]
                                            + dprev * w_ref[2:3, :]).astype(BF16)
            return carry

        lax.fori_loop(0, seq // rows, chunk2, 0)

    return _hosted(
        body, rider, name="ffn_backward", grid=(D_FF // cw, batch),
        out_shape=[jax.ShapeDtypeStruct((2, t, D_FF), BF16),
                   jax.ShapeDtypeStruct((3, D_FF), F32), jax.ShapeDtypeStruct((1, D_FF), F32)],
        in_specs=[pl.BlockSpec((seq, d), lambda j, b: (b, 0)), pl.BlockSpec((cw, d), lambda j, b: (j, 0)),
                  pl.BlockSpec((2, seq, cw), lambda j, b: (0, b, j)),
                  pl.BlockSpec((3, cw), lambda j, b: (0, j)), pl.BlockSpec((1, cw), lambda j, b: (0, j))],
        out_specs=[pl.BlockSpec((2, seq, cw), lambda j, b: (0, b, j)),
                   pl.BlockSpec((3, cw), lambda j, b: (0, j)), pl.BlockSpec((1, cw), lambda j, b: (0, j))],
        scratch_shapes=[pltpu.VMEM((seq, cw), F32), pltpu.VMEM((seq, cw), F32)],
        compiler_params=_cparams(("arbitrary", "arbitrary"), VMEM_BIG), args=[dffn, w_down, u, conv_w, conv_b])


def _up_backward(du, w_up, x1, mod3, g_ffn, dx2, mix, seq, rider=None):
    _, t, _ = du.shape
    d = x1.shape[1]
    tm = TOKEN_TILE
    per_seq = seq // tm
    batch = t // seq
    w_a, w_b = w_up
    half, wcol = w_a.shape[1], w_a.shape[2]

    def body(du_ref, wa_ref, wb_ref, x1_ref, mod_ref, g_ref, dx2_ref, mix_ref,
             dx1_ref, dmix_ref, dsh_ref, dsc_ref, dga_ref, dg_ref):
        i = pl.program_id(0)
        parts = []
        for w_ref in (wa_ref, wb_ref):
            acc = jnp.zeros((tm, half), F32)
            for j in range(N_SHARD):
                acc = acc + _mm_nt(du_ref[j // 2, :, (j % 2) * wcol:(j % 2 + 1) * wcol], w_ref[j])
            parts.append(acc)
        dh = jnp.concatenate(parts, axis=1)
        gate_a = mod_ref[0, :, 2 * d:3 * d]
        scale_f = mod_ref[0, :, 4 * d:5 * d]
        r, xn = _rms_stats(x1_ref[...])
        xg = xn * g_ref[...]
        dxg = dh * (1.0 + scale_f)
        dx1 = dx2_ref[...] + _rms_bwd(dxg * g_ref[...], xn, r)
        dx1_ref[...] = dx1
        dmix_ref[...] = (dx1 * gate_a).astype(BF16)

        @pl.when(i == 0)
        def _():
            dg_ref[...] = jnp.zeros_like(dg_ref)

        @pl.when(i % per_seq == 0)
        def _():
            dsh_ref[...] = jnp.zeros_like(dsh_ref)
            dsc_ref[...] = jnp.zeros_like(dsc_ref)
            dga_ref[...] = jnp.zeros_like(dga_ref)

        dg_ref[...] += jnp.sum(dxg * xn, axis=0, keepdims=True)
        dsh_ref[0] += jnp.sum(dh, axis=0, keepdims=True)
        dsc_ref[0] += jnp.sum(dh * xg, axis=0, keepdims=True)
        dga_ref[0] += jnp.sum(dx1 * mix_ref[...], axis=0, keepdims=True)

    tile = lambda w: pl.BlockSpec((tm, w), lambda i: (i, 0))
    per_b = pl.BlockSpec((1, 1, d), lambda i: (i // per_seq, 0, 0))
    small = jax.ShapeDtypeStruct((batch, 1, d), F32)
    return _hosted(
        body, rider, name="up_backward", grid=(t // tm,),
        out_shape=[jax.ShapeDtypeStruct((t, d), F32), jax.ShapeDtypeStruct((t, d), BF16), small, small, small,
                   jax.ShapeDtypeStruct((1, d), F32)],
        in_specs=[pl.BlockSpec((2, tm, D_FF), lambda i: (0, i, 0)),
                  _resident((N_SHARD, half, wcol)), _resident((N_SHARD, half, wcol)), tile(d),
                  pl.BlockSpec((1, 1, 6 * d), lambda i: (i // per_seq, 0, 0)),
                  pl.BlockSpec((1, d), lambda i: (0, 0)), tile(d), tile(d)],
        out_specs=[tile(d), tile(d), per_b, per_b, per_b, pl.BlockSpec((1, d), lambda i: (0, 0))],
        scratch_shapes=[], compiler_params=_cparams(("arbitrary",), VMEM_BIG),
        args=[du, w_a, w_b, x1, mod3, g_ffn, dx2, mix])


def _up_weight_grad(h2, du, rider=None):
    t, d = h2.shape
    tk = TOKEN_TILE
    wcol = D_FF // 2
    half = d // 2
    n_k = t // tk

    def body(h_ref, du_ref, ga_ref, gb_ref, ga16_ref, gb16_ref):
        k = pl.program_id(1)

        @pl.when(k == 0)
        def _():
            ga_ref[...] = jnp.zeros_like(ga_ref)
            gb_ref[...] = jnp.zeros_like(gb_ref)

        du = du_ref[0]
        ga_ref[0] += _mm_tn(h_ref[:, :half], du)
        gb_ref[0] += _mm_tn(h_ref[:, half:], du)

        @pl.when(k == n_k - 1)
        def _():
            ga16_ref[...] = ga_ref[...].astype(BF16)
            gb16_ref[...] = gb_ref[...].astype(BF16)

    g_spec = pl.BlockSpec((1, half, wcol), lambda j, k: (j, 0, 0))
    f32_out = jax.ShapeDtypeStruct((N_SHARD, half, wcol), F32)
    b16_out = jax.ShapeDtypeStruct((N_SHARD, half, wcol), BF16)
    return _hosted(
        body, rider, name="up_weight_grad", grid=(N_SHARD, n_k),
        out_shape=[f32_out, f32_out, b16_out, b16_out],
        in_specs=[pl.BlockSpec((tk, d), lambda j, k: (k, 0)),
                  pl.BlockSpec((1, tk, wcol), lambda j, k: (j // 2, k, j % 2))],
        out_specs=[g_spec, g_spec, g_spec, g_spec], scratch_shapes=[],
        compiler_params=_cparams(("arbitrary", "arbitrary"), VMEM_BIG), args=[h2, du])


def _out_backward(dmix, w_out, oab, oa, ob, g_na, g_sw):
    t, d = dmix.shape
    tm = TOKEN_TILE
    hw = NA_WIDTH

    def body(dm_ref, w_ref, oab_ref, oa_ref, ob_ref, gna_ref, gsw_ref,
             doa_ref, dob_ref, gw_ref, gwb_ref, dgna_ref, dgsw_ref):
        @pl.when(pl.program_id(0) == 0)
        def _():
            gw_ref[...] = jnp.zeros_like(gw_ref)
            dgna_ref[...] = jnp.zeros_like(dgna_ref)
            dgsw_ref[...] = jnp.zeros_like(dgsw_ref)

        dm = dm_ref[...]
        gw_ref[...] += _mm_tn(oab_ref[...], dm)

        @pl.when(pl.program_id(0) == t // tm - 1)
        def _():
            gwb_ref[...] = gw_ref[...].astype(BF16)

        do = _mm_nt(dm, w_ref[...])
        for raw_ref, g_ref, dst_ref, dg_ref, lo in ((oa_ref, gna_ref, doa_ref, dgna_ref, 0),
                                                     (ob_ref, gsw_ref, dob_ref, dgsw_ref, hw)):
            r, xn = _rms_stats(raw_ref[...])
            dpart = do[:, lo:lo + hw]
            dg_ref[...] += jnp.sum(dpart * xn, axis=0, keepdims=True)
            dst_ref[...] = _rms_bwd(dpart * g_ref[...], xn, r).astype(BF16)

    tile = lambda w: pl.BlockSpec((tm, w), lambda i: (i, 0))
    vec = lambda w: pl.BlockSpec((1, w), lambda i: (0, 0))
    return pl.pallas_call(
        body, name="out_backward", grid=(t // tm,),
        out_shape=(jax.ShapeDtypeStruct((t, hw), BF16), jax.ShapeDtypeStruct((t, hw), BF16),
                   jax.ShapeDtypeStruct((d, d), F32), jax.ShapeDtypeStruct((d, d), BF16),
                   jax.ShapeDtypeStruct((1, hw), F32), jax.ShapeDtypeStruct((1, hw), F32)),
        in_specs=[tile(d), pl.BlockSpec((d, d), lambda i: (0, 0)), tile(d), tile(hw), tile(hw), vec(hw), vec(hw)],
        out_specs=(tile(hw), tile(hw), pl.BlockSpec((d, d), lambda i: (0, 0)), pl.BlockSpec((d, d), lambda i: (0, 0)),
                   vec(hw), vec(hw)),
        compiler_params=_cparams(("arbitrary",), VMEM_BIG),
    )(dmix, w_out, oab, oa, ob, g_na, g_sw)


def _na_backward(proj, d_o, tiles, batch, seq, rider=None):
    t = proj.shape[0]
    n_rows = seq // GRID_W
    n_pairs = NA_WIDTH // LANES
    win = NA_ROWS * GRID_W
    n_tiles = 2 * NA_ROWS - 2

    def body(q_ref, k_ref, v_ref, do_ref, tp_ref, dq_ref, dk_ref, dv_ref, dtp_ref, km, vm, dk_acc, dv_acc):
        @pl.when(pl.program_id(1) == 0)
        def _():
            dtp_ref[...] = jnp.zeros_like(dtp_ref)

        _na_prepare(k_ref, v_ref, km, vm)
        dk_acc[...] = jnp.zeros_like(dk_acc)
        dv_acc[...] = jnp.zeros_like(dv_acc)
        low = lax.broadcasted_iota(jnp.int32, (win, LANES), 1) < HEAD_DIM

        def scores(r):
            rs, off = _na_window(r, n_rows)
            rows = pl.ds(pl.multiple_of(r * GRID_W, GRID_W), GRID_W)
            wrows = pl.ds(pl.multiple_of(rs * GRID_W, GRID_W), win)
            q, do = q_ref[rows, :], do_ref[rows, :]
            k2 = _na_pair_window(km, wrows)
            s = _na_scores(q, k2, tp_ref, off)
            dp = _mm_nt(do, _na_pair_window(vm, wrows))
            return rows, wrows, off, q, do, k2, s, dp

        def finish(rows, wrows, off, q, do, k2, s, dp):
            p = _pair_softmax(s)
            parts = []
            for h in range(2):
                ph, dph = p[:, h * win:(h + 1) * win], dp[:, h * win:(h + 1) * win]
                dsh = ph * (dph - jnp.sum(ph * dph, axis=-1, keepdims=True))
                for w in range(NA_ROWS // 2):
                    dtp_ref[h, 2 * w - off + (NA_ROWS - 1)] += dsh[:, w * LANES:(w + 1) * LANES]
                parts.append(dsh)
            dsb = (jnp.concatenate(parts, axis=1) * QK_SCALE).astype(BF16)
            dq_ref[rows, :] = _mm(dsb, k2).astype(BF16)
            dk2 = _mm_tn(dsb, q)
            dv2 = _mm_tn(p.astype(BF16), do)
            dk_acc[wrows, :] += jnp.where(low, dk2[:win], dk2[win:])
            dv_acc[wrows, :] += jnp.where(low, dv2[:win], dv2[win:])

        def row_group(i, carry):
            for state in [scores(NA_GROUP * i + j) for j in range(NA_GROUP)]:
                finish(*state)
            return carry

        lax.fori_loop(0, n_rows // NA_GROUP, row_group, 0)
        dk_ref[...] = dk_acc[...].astype(BF16)
        dv_ref[...] = dv_acc[...].astype(BF16)

    blk = lambda off: pl.BlockSpec((seq, LANES), lambda p, b: (b, off + p))
    out = jax.ShapeDtypeStruct((t, NA_WIDTH), BF16)
    return _hosted(
        body, rider, name="na_backward", grid=(n_pairs, batch),
        out_shape=[out, out, out, jax.ShapeDtypeStruct(tiles.shape, F32)],
        in_specs=[blk(0), blk(n_pairs), blk(2 * n_pairs), blk(0),
                  pl.BlockSpec((2, n_tiles, GRID_W, LANES), lambda p, b: (p, 0, 0, 0))],
        out_specs=[blk(0), blk(0), blk(0), pl.BlockSpec((2, n_tiles, GRID_W, LANES), lambda p, b: (p, 0, 0, 0))],
        scratch_shapes=[pltpu.VMEM((2, seq, LANES), BF16), pltpu.VMEM((2, seq, LANES), BF16),
                        pltpu.VMEM((seq, LANES), F32), pltpu.VMEM((seq, LANES), F32)],
        compiler_params=_cparams(("arbitrary", "arbitrary")), args=[proj, proj, proj, d_o, tiles])


def _na_bias_grad(dtiles, expand):
    n = dtiles.shape[0]

    def body(t_ref, e_ref, o_ref):
        flat = jnp.concatenate([t_ref[:, qq, :] for qq in range(GRID_W)], axis=1)
        o_ref[...] = lax.dot_general(flat, e_ref[...], (((1,), (1,)), ((), ())),
                                     precision=lax.Precision.HIGHEST, preferred_element_type=F32)

    return pl.pallas_call(
        body, name="na_bias_grad",
        out_shape=jax.ShapeDtypeStruct((n, expand.shape[0]), F32),
        compiler_params=_cparams(vmem=VMEM_BIG),
    )(dtiles, expand)


def _sw_backward(proj, d_o, sink, batch, seq, rider=None):
    t = proj.shape[0]
    n_pairs = SW_WIDTH // LANES
    q_blk = 3 * NA_WIDTH // LANES
    k_blk = q_blk + n_pairs
    n_blocks = seq // SW_BLOCK
    pad = seq + 2 * SW_BLOCK

    def body(sink_ref, q_ref, k_ref, v_ref, do_ref, dq_ref, dk_ref, dv_ref, dsk_ref,
             k_lo, k_hi, v_lo, v_hi, dk_loc, dv_loc, dk_tot, dv_tot):
        hp = pl.program_id(1)
        g = hp // 2
        _sw_prepare(k_ref, g, k_lo, k_hi, seq)
        _sw_prepare(v_ref, g, v_lo, v_hi, seq)
        dk_loc[...] = jnp.zeros_like(dk_loc)
        dv_loc[...] = jnp.zeros_like(dv_loc)

        @pl.when(hp == 0)
        def _():
            dk_tot[...] = jnp.zeros_like(dk_tot)
            dv_tot[...] = jnp.zeros_like(dv_tot)

        band = 3 * SW_BLOCK
        low = lax.broadcasted_iota(jnp.int32, (band, LANES), 1) < HEAD_DIM

        sinks = (sink_ref[2 * hp], sink_ref[2 * hp + 1])

        def scores(n):
            rows = pl.ds(pl.multiple_of(n * SW_BLOCK, SW_BLOCK), SW_BLOCK)
            wrows = pl.ds(pl.multiple_of(n * SW_BLOCK, SW_BLOCK), band)
            qb, do = q_ref[rows, :], do_ref[rows, :]
            k2 = jnp.concatenate([k_lo[wrows, :], k_hi[wrows, :]], axis=0)
            v2 = jnp.concatenate([v_lo[wrows, :], v_hi[wrows, :]], axis=0)
            return n, rows, wrows, qb, do, k2, _mm_nt(qb, k2) * QK_SCALE, _mm_nt(do, v2)

        def finish(sink_acc, n, rows, wrows, qb, do, k2, s2, dp):
            p, ps = _sw_probs(s2, _sw_mask(n, seq), sinks)
            parts, new = [], []
            for i in range(2):
                ph, dph = p[:, i * band:(i + 1) * band], dp[:, i * band:(i + 1) * band]
                delta = jnp.sum(ph * dph, axis=-1, keepdims=True)
                parts.append(ph * (dph - delta))
                new.append(sink_acc[i] - ps[i] * delta)
            dsb = (jnp.concatenate(parts, axis=1) * QK_SCALE).astype(BF16)
            dq_ref[rows, :] = _mm(dsb, k2)
            dk2 = _mm_tn(dsb, qb)
            dv2 = _mm_tn(p.astype(BF16), do)
            dk_loc[wrows, :] += jnp.where(low, dk2[:band], dk2[band:])
            dv_loc[wrows, :] += jnp.where(low, dv2[:band], dv2[band:])
            return tuple(new)

        def block_group(i, carry):
            for state in [scores(SW_GROUP_BLOCKS * i + j) for j in range(SW_GROUP_BLOCKS)]:
                carry = finish(carry, *state)
            return carry

        zero = jnp.zeros((SW_BLOCK, 1), F32)
        s0, s1 = lax.fori_loop(0, n_blocks // SW_GROUP_BLOCKS, block_group, (zero, zero))
        row = lax.broadcasted_iota(jnp.int32, (SUBLANES, LANES), 0)
        dsk_ref[0, 0] = jnp.where(row == 0, jnp.sum(s0), jnp.where(row == 1, jnp.sum(s1), 0.0))

        lane_s = lax.broadcasted_iota(jnp.int32, (seq, LANES), 1)
        mine_g = (lane_s // HEAD_DIM) == g
        for loc, tot in ((dk_loc, dk_tot), (dv_loc, dv_tot)):
            part = loc[SW_BLOCK:SW_BLOCK + seq, :]
            tot[...] += jnp.where(mine_g, part + pltpu.roll(part, HEAD_DIM, 1), 0.0)

        @pl.when(hp == n_pairs - 1)
        def _():
            dk_ref[...] = dk_tot[...]
            dv_ref[...] = dv_tot[...].astype(BF16)

    return _hosted(
        body, rider, name="sw_backward", grid=(batch, n_pairs),
        out_shape=[jax.ShapeDtypeStruct((t, SW_WIDTH), F32), jax.ShapeDtypeStruct((t, LANES), F32),
                   jax.ShapeDtypeStruct((t, LANES), BF16), jax.ShapeDtypeStruct((batch, n_pairs, SUBLANES, LANES), F32)],
        in_specs=[pl.BlockSpec(memory_space=pltpu.SMEM),
                  pl.BlockSpec((seq, LANES), lambda b, p: (b, q_blk + p)),
                  pl.BlockSpec((seq, LANES), lambda b, p: (b, k_blk)),
                  pl.BlockSpec((seq, LANES), lambda b, p: (b, k_blk + 1)),
                  pl.BlockSpec((seq, LANES), lambda b, p: (b, p))],
        out_specs=[pl.BlockSpec((seq, LANES), lambda b, p: (b, p)), pl.BlockSpec((seq, LANES), lambda b, p: (b, 0)),
                   pl.BlockSpec((seq, LANES), lambda b, p: (b, 0)),
                   pl.BlockSpec((1, 1, SUBLANES, LANES), lambda b, p: (b, p, 0, 0))],
        scratch_shapes=[pltpu.VMEM((pad, LANES), BF16)] * 4 + [pltpu.VMEM((pad, LANES), F32)] * 2
        + [pltpu.VMEM((seq, LANES), F32)] * 2,
        compiler_params=_cparams(("arbitrary", "arbitrary")), args=[sink, proj, proj, proj, d_o])


def _in_backward(dqkv_a, dq_b, dk_b, dv_b, w_in_t, h1, x, mod3, g_attn, dx1, cos_t, sin_t, seq):
    t, d = x.shape
    tm = TOKEN_TILE
    per_seq = seq // tm
    batch = t // seq
    dqa, dka, dva = dqkv_a
    n_q = SW_WIDTH // LANES

    def body(dqa_ref, dka_ref, dva_ref, dqb_ref, dkb_ref, dvb_ref, w_ref, h_ref, x_ref, mod_ref, g_ref, dx1_ref,
             cos_ref, sin_ref, dx_ref, gw_ref, gwb_ref, dsh_ref, dsc_ref, dg_ref):
        i = pl.program_id(0)

        @pl.when(i == 0)
        def _():
            gw_ref[...] = jnp.zeros_like(gw_ref)
            dg_ref[...] = jnp.zeros_like(dg_ref)

        @pl.when(i % per_seq == 0)
        def _():
            dsh_ref[...] = jnp.zeros_like(dsh_ref)
            dsc_ref[...] = jnp.zeros_like(dsc_ref)

        dr = jnp.concatenate([dqb_ref[...], dkb_ref[...]], axis=1)
        cos = jnp.concatenate([cos_ref[...]] * (n_q + 1), axis=1)
        sin = jnp.concatenate([sin_ref[...]] * (n_q + 1), axis=1)
        dr = dr * cos + _rope_rot(dr * sin)
        dproj = jnp.concatenate([dqa_ref[...], dka_ref[...], dva_ref[...], dr.astype(BF16), dvb_ref[...]], axis=1)
        gw_ref[...] += _mm_tn(dproj, h_ref[...])

        @pl.when(i == t // tm - 1)
        def _():
            gwb_ref[...] = gw_ref[...].astype(BF16)

        dh = _mm(dproj, w_ref[...])
        scale = mod_ref[0, :, d:2 * d]
        r, xn = _rms_stats(x_ref[...])
        xg = xn * g_ref[...]
        dxg = dh * (1.0 + scale)
        dx_ref[...] = dx1_ref[...] + _rms_bwd(dxg * g_ref[...], xn, r)
        dg_ref[...] += jnp.sum(dxg * xn, axis=0, keepdims=True)
        dsh_ref[0] += jnp.sum(dh, axis=0, keepdims=True)
        dsc_ref[0] += jnp.sum(dh * xg, axis=0, keepdims=True)

    tile = lambda w: pl.BlockSpec((tm, w), lambda i: (i, 0))
    per_b = pl.BlockSpec((1, 1, d), lambda i: (i // per_seq, 0, 0))
    small = jax.ShapeDtypeStruct((batch, 1, d), F32)
    rope = pl.BlockSpec((tm, LANES), lambda i: (i % per_seq, 0))
    return pl.pallas_call(
        body, name="in_backward", grid=(t // tm,),
        out_shape=(jax.ShapeDtypeStruct((t, d), F32), jax.ShapeDtypeStruct((IN_WIDTH, d), F32),
                   jax.ShapeDtypeStruct((IN_WIDTH, d), BF16), small, small, jax.ShapeDtypeStruct((1, d), F32)),
        in_specs=[tile(NA_WIDTH), tile(NA_WIDTH), tile(NA_WIDTH), tile(SW_WIDTH), tile(LANES), tile(LANES),
                  _resident((IN_WIDTH, d)), tile(d), tile(d),
                  pl.BlockSpec((1, 1, 6 * d), lambda i: (i // per_seq, 0, 0)),
                  pl.BlockSpec((1, d), lambda i: (0, 0)), tile(d), rope, rope],
        out_specs=(tile(d), _resident((IN_WIDTH, d)), _resident((IN_WIDTH, d)),
                   per_b, per_b, pl.BlockSpec((1, d), lambda i: (0, 0))),
        compiler_params=_cparams(("arbitrary",), VMEM_BIG),
    )(dqa, dka, dva, dq_b, dk_b, dv_b, w_in_t, h1, x, mod3, g_attn, dx1, cos_t, sin_t)


def _ada_weight_grad(sc_all, dmod_cols):
    d = sc_all.shape[1]
    ncol = dmod_cols.shape[1]

    def body(s_ref, m_ref, o_ref):
        o_ref[...] = _mm_tn(s_ref[...].astype(BF16), m_ref[...].astype(BF16))

    return pl.pallas_call(
        body, name="ada_weight_grad",
        out_shape=jax.ShapeDtypeStruct((d, ncol), F32),
        compiler_params=_cparams(vmem=VMEM_BIG),
    )(sc_all, dmod_cols)


def _row_tile(rows, cols):
    target = max(SUBLANES, (1 << 20) // (4 * cols))
    best = rows
    for cand in range(SUBLANES, rows + 1, SUBLANES):
        if rows % cand == 0 and cand <= target:
            best = cand
    return best if rows % SUBLANES == 0 else rows


def _sum_slots(parts, name):
    n = len(parts)
    _, rows, cols = parts[0][0].shape
    tr = _row_tile(rows, cols)
    per = rows // tr

    def body(*refs):
        o_ref = refs[-1]
        for q in range(n):
            @pl.when(pl.program_id(0) == q)
            def _(q=q):
                p_ref, own_ref = refs[2 * q], refs[2 * q + 1]
                o_ref[...] = ((own_ref[...] + p_ref[0].astype(F32)) + p_ref[1].astype(F32)) + p_ref[2].astype(F32)

    in_specs, args = [], []
    for q, (recv, own) in enumerate(parts):
        in_specs.append(pl.BlockSpec((N_SHARD - 1, tr, cols), lambda p, i, q=q: (0, jnp.where(p == q, i, 0), 0)))
        in_specs.append(pl.BlockSpec((tr, cols), lambda p, i, q=q: (jnp.where(p == q, i, 0), 0)))
        args += [recv, own]
    return pl.pallas_call(
        body, name=name, grid=(n, per),
        out_shape=jax.ShapeDtypeStruct((n * rows, cols), F32),
        in_specs=in_specs, out_specs=pl.BlockSpec((tr, cols), lambda p, i: (p * per + i, 0)),
        compiler_params=_cparams(("arbitrary", "arbitrary")),
    )(*args)


def _adamw_math(w, g, m, v):
    m2 = ADAM_B1 * m + (1.0 - ADAM_B1) * g
    v2 = ADAM_B2 * v + (1.0 - ADAM_B2) * (g * g)
    m_hat = m2 / (1.0 - ADAM_B1 ** ADAM_STEP)
    v_hat = v2 / (1.0 - ADAM_B2 ** ADAM_STEP)
    return -ADAM_LR * (m_hat / (jnp.sqrt(v_hat) + ADAM_EPS) + ADAM_WD * w), m2, v2


def _small_step(partials, states, dmod, b_ada_state, rider=None):
    n_upd = len(states)
    moving = list(partials) + [dmod]
    n_mov = len(moving)
    all_states = list(states) + [b_ada_state]

    def body(*refs):
        mov, refs = refs[:n_mov], refs[n_mov:]
        wmv, refs = refs[:3 * (n_upd + 1)], refs[3 * (n_upd + 1):]
        res, refs = refs[:4 * (n_upd + 1)], refs[4 * (n_upd + 1):]
        sums_out, refs = refs[:n_mov - n_upd - 1], refs[n_mov - n_upd - 1:]
        dmod_out, refs = refs[0], refs[1:]
        everyone, (ssem, rsem) = refs[:n_mov], refs[n_mov:]
        x, y, c = _my_pos()
        me = 4 * x + 2 * y + c
        cps = []
        for a in range(n_mov):
            everyone[a][me] = mov[a][...]
            for k in range(1, N_DEV):
                peer = (_flip(x, (k >> 2) & 1), _flip(y, (k >> 1) & 1), _flip(c, k & 1))
                cps.append(pltpu.make_async_remote_copy(
                    src_ref=everyone[a].at[me], dst_ref=everyone[a].at[me], send_sem=ssem.at[a, k - 1],
                    recv_sem=rsem.at[a, k - 1], device_id=peer, device_id_type=MESH))
        for cp in cps:
            cp.start()
        for cp in cps:
            cp.wait_recv()

        def total(a):
            acc = everyone[a][0]
            for dev in range(1, N_DEV):
                acc = acc + everyone[a][dev]
            return acc

        grads = [total(a) for a in range(n_upd)]
        grads.append(jnp.sum(total(n_mov - 1), axis=0, keepdims=True))
        for j, g in enumerate(grads):
            delta, m2, v2 = _adamw_math(wmv[3 * j][...], g, wmv[3 * j + 1][...], wmv[3 * j + 2][...])
            res[4 * j][...] = g
            res[4 * j + 1][...] = delta
            res[4 * j + 2][...] = m2
            res[4 * j + 3][...] = v2
        for j in range(n_mov - n_upd - 1):
            sums_out[j][...] = total(n_upd + j)
        dmod_out[...] = everyone[n_mov - 1][...]
        for cp in cps:
            cp.wait_send()

    vm = pl.BlockSpec(memory_space=pltpu.VMEM)
    sds = jax.ShapeDtypeStruct
    out_shape = []
    for w, _, _ in all_states:
        out_shape += [sds(w.shape, F32)] * 4
    out_shape += [sds(p.shape, F32) for p in partials[n_upd:]]
    out_shape.append(sds((N_DEV,) + dmod.shape, F32))
    args = moving + [a for st in all_states for a in st]
    outs, rides = _hosted(
        body, rider, name="small_step", grid=(), out_shape=out_shape,
        in_specs=[vm] * len(args), out_specs=[vm] * len(out_shape),
        scratch_shapes=[pltpu.VMEM((N_DEV,) + a.shape, F32) for a in moving]
        + [pltpu.SemaphoreType.DMA((n_mov, N_DEV - 1)), pltpu.SemaphoreType.DMA((n_mov, N_DEV - 1))],
        compiler_params=_cparams(vmem=VMEM_BIG), args=args)
    return outs, rides


def _adamw(w, grads, m, v, name):
    rows, cols = w.shape
    tr = _row_tile(rows, cols)
    ng = len(grads)

    def body(*refs):
        w_ref = refs[0]
        g_refs = refs[1:1 + ng]
        m_ref, v_ref = refs[1 + ng], refs[2 + ng]
        g_out, d_out, m_out, v_out = refs[3 + ng:]
        g = g_refs[0][...]
        for extra in g_refs[1:]:
            g = g + extra[...]
        g_out[...] = g
        d_out[...], m_out[...], v_out[...] = _adamw_math(w_ref[...], g, m_ref[...], v_ref[...])

    spec = pl.BlockSpec((tr, cols), lambda i: (i, 0))
    out = jax.ShapeDtypeStruct((rows, cols), F32)
    return pl.pallas_call(
        body, name=name, grid=(rows // tr,),
        out_shape=(out, out, out, out),
        in_specs=[spec] * (3 + ng), out_specs=(spec, spec, spec, spec),
        compiler_params=_cparams(("arbitrary",)),
    )(w, *grads, m, v)


def _pack_rows(arrays):
    tile = SUBLANES * LANES
    rows, offsets, at = [], [], 0
    for a in arrays:
        flat = a.reshape(-1).astype(F32)
        n = -(-flat.shape[0] // tile) * tile
        rows.append(jnp.pad(flat, (0, n - flat.shape[0])).reshape(-1, LANES))
        offsets.append(at)
        at += n // LANES
    return jnp.concatenate(rows, axis=0), offsets


def _unpack_rows(packed, offsets, shapes):
    out = []
    for off, shape in zip(offsets, shapes):
        n = 1
        for s in shape:
            n *= s
        nrow = -(-n // LANES)
        out.append(packed[off:off + nrow].reshape(-1)[:n].reshape(shape))
    return out


def _rope_tables(seq):
    half = HEAD_DIM // 2
    inv = np.float32(ROPE_THETA) ** (-np.arange(half, dtype=np.float32) / np.float32(half))
    ang = (np.arange(seq, dtype=np.float32)[:, None] * inv[None, :]).astype(np.float64)
    cos, sin = np.cos(ang).astype(np.float32), np.sin(ang).astype(np.float32)
    cos_t = np.concatenate([cos, cos, cos, cos], axis=1)
    sin_t = np.concatenate([-sin, sin, -sin, sin], axis=1)
    return jnp.asarray(cos_t), jnp.asarray(sin_t)


def kernel(x, c, w_ada, b_ada, g_attn, w_in, na_rpb, sw_sink, g_na_out, g_sw_out, w_out, g_ffn, w_up, conv_w, conv_b, w_down, g_final, loss_target, m_w_ada, m_b_ada, m_g_attn, m_w_in, m_na_rpb, m_sw_sink, m_g_na_out, m_g_sw_out, m_w_out, m_g_ffn, m_w_up, m_conv_w, m_conv_b, m_w_down, m_g_final, v_w_ada, v_b_ada, v_g_attn, v_w_in, v_na_rpb, v_sw_sink, v_g_na_out, v_g_sw_out, v_w_out, v_g_ffn, v_w_up, v_conv_w, v_conv_b, v_w_down, v_g_final):
    batch, seq, d = x.shape
    t = batch * seq
    assert d == D_MODEL and seq % (NA_ROWS * GRID_W) == 0 and seq % TOKEN_TILE == 0 and batch <= SUBLANES
    shard = 2 * lax.axis_index("x") + lax.axis_index("y")
    xt = x.reshape(t, d)
    tgt = loss_target.reshape(t, d)

    c8 = jnp.pad(c, ((0, SUBLANES - batch), (0, 0)))
    w_in_t_s = jnp.transpose(w_in[0]).astype(BF16)
    (mod8, sc_all), (w_in_g,) = _ada_forward(c8, w_ada[0], b_ada, _Rider("gather", [w_in_t_s]))
    mod3 = mod8[:batch].reshape(batch, 1, 6 * d)
    w_in_t = w_in_g.reshape(IN_WIDTH, d)

    cos_t, sin_t = _rope_tables(seq)
    (h1, proj), (w_out_g,) = _in_proj(xt, mod3, g_attn, w_in_t, cos_t, sin_t, seq,
                                      _Rider("gather", [w_out[0].astype(BF16)]))
    n_heads = NA_WIDTH // HEAD_DIM
    n_tiles, n_dc = 2 * NA_ROWS - 2, 2 * NA_COLS - 1
    expand, neg_mask = _na_bias_pattern()
    rpb = na_rpb[0]
    rows2 = jnp.concatenate([rpb[:, :-1, :], rpb[:, 1:, :]], axis=2).reshape(n_heads * n_tiles, 2 * n_dc)
    rows2 = jnp.pad(rows2, ((0, 0), (0, GRID_W - 2 * n_dc)))
    tiles = _na_bias_tiles(rows2, expand, neg_mask).reshape(n_heads, n_tiles, GRID_W, LANES)
    sink = sw_sink[0]
    w_up_b16 = w_up[0].astype(BF16)
    (oa,), (w_up_a,) = _na_forward(proj, tiles, batch, seq, _Rider("gather", [w_up_b16[:d // 2]]))
    (ob,), (w_up_b, conv_w_g) = _sw_forward(proj, sink, batch, seq, _Rider("gather", [w_up_b16[d // 2:], conv_w[0]]))
    w_up_f = (w_up_a, w_up_b)
    w_out_f = w_out_g.reshape(d, d)
    conv_w_f = jnp.transpose(conv_w_g, (1, 0, 2)).reshape(3, D_FF)
    oab, mix, x1, h2 = _out_proj(oa, ob, g_na_out, g_sw_out, w_out_f, xt, mod3, g_ffn, seq)
    (u,), (w_down_g,) = _up_proj(h2, w_up_f, _Rider("gather", [w_down[0].astype(BF16)]))
    w_down_f = w_down_g.reshape(D_FF, d)
    a = _conv_gate(u, conv_w_f, conv_b, batch, seq)
    dx2, dffn, loss_part, dgate_f, dg_final = _down_and_loss(a, w_down_f, x1, mod3, g_final.reshape(1, d), tgt, seq)

    gw_down, gw_down_b = _down_weight_grad(a, dffn)
    blocks = lambda g, rows: g.reshape(N_SHARD, rows // N_SHARD, d)
    (du, gconv_w, gconv_b), (recv_down, own_down) = _ffn_backward(
        dffn, w_down_f, u, conv_w_f, conv_b, batch, seq,
        _Rider("scatter", [blocks(gw_down_b, D_FF)], [blocks(gw_down, D_FF)]))
    (gw_up_top, gw_up_bot, gw_up_top_b, gw_up_bot_b), _ = _up_weight_grad(h2, du)
    (dx1, dmix, dshift_f, dscale_f, dgate_a, dg_ffn), (recv_up_top, own_up_top) = _up_backward(
        du, w_up_f, x1, mod3, g_ffn, dx2, mix, seq, _Rider("scatter", [gw_up_top_b], [gw_up_top]))
    doa, dob, gw_out, gw_out_b, dg_na, dg_sw = _out_backward(dmix, w_out_f, oab, oa, ob, g_na_out, g_sw_out)
    (dqa, dka, dva, dtiles), (recv_up_bot, own_up_bot) = _na_backward(
        proj, doa, tiles, batch, seq, _Rider("scatter", [gw_up_bot_b], [gw_up_bot]))
    (dq_b, dk_b, dv_b, dsink_parts), (recv_out, own_out) = _sw_backward(
        proj, dob, sink, batch, seq, _Rider("scatter", [blocks(gw_out_b, d)], [blocks(gw_out, d)]))
    gx, gw_in_t, gw_in_b, dshift_a, dscale_a, dg_attn = _in_backward(
        (dqa, dka, dva), dq_b, dk_b, dv_b, w_in_t, h1, xt, mod3, g_attn, dx1, cos_t, sin_t, seq)

    red = _na_bias_grad(dtiles.reshape(n_heads * n_tiles, GRID_W, LANES), expand)[:, :2 * n_dc]
    red = red.reshape(n_heads, n_tiles, 2, n_dc)
    zero_row = jnp.zeros((n_heads, 1, n_dc), F32)
    g_rpb = (jnp.concatenate([red[:, :, 0, :], zero_row], axis=1)
             + jnp.concatenate([zero_row, red[:, :, 1, :]], axis=1))
    g_sink = jnp.sum(dsink_parts[:, :, :2, 0], axis=0).reshape(SW_WIDTH // HEAD_DIM)

    dmod = jnp.concatenate([dshift_a, dscale_a, dgate_a, dshift_f, dscale_f, dgate_f], axis=2).reshape(batch, 6 * d)
    rpb_shape = na_rpb.shape[1:]
    states = [(g_attn, m_g_attn, v_g_attn),
              (na_rpb.reshape(rpb_shape), m_na_rpb.reshape(rpb_shape), v_na_rpb.reshape(rpb_shape)),
              (sw_sink, m_sw_sink, v_sw_sink), (g_na_out, m_g_na_out, v_g_na_out), (g_sw_out, m_g_sw_out, v_g_sw_out),
              (g_ffn, m_g_ffn, v_g_ffn), (conv_b, m_conv_b, v_conv_b),
              (g_final.reshape(1, d), m_g_final.reshape(1, d), v_g_final.reshape(1, d))]
    partials = [dg_attn, g_rpb, g_sink.reshape(sw_sink.shape), dg_na, dg_sw, dg_ffn, gconv_b, dg_final,
                gconv_w, loss_part]
    small, (recv_in, own_in) = _small_step(
        partials, states, dmod, (b_ada, m_b_ada, v_b_ada),
        _Rider("scatter", [blocks(gw_in_b, IN_WIDTH)], [blocks(gw_in_t, IN_WIDTH)]))
    r_small = [small[4 * j:4 * j + 4] for j in range(len(states) + 1)]
    g_conv_w_full, loss_sum, dmod_all = small[4 * (len(states) + 1):]
    loss = loss_sum[0, 0]
    mine = [_sum_slots([(recv_in, own_in)], "sum_w_in"), _sum_slots([(recv_out, own_out)], "sum_w_out"),
            _sum_slots([(recv_up_top, own_up_top), (recv_up_bot, own_up_bot)], "sum_w_up"),
            _sum_slots([(recv_down, own_down)], "sum_w_down")]
    theirs = _ride_alone(_Rider("swap", mine), "swap_sibling")
    dmod_rows = jnp.pad(dmod_all, ((0, 0), (0, SUBLANES - batch), (0, 0))).reshape(N_DEV * SUBLANES, 6 * d)
    ncol = w_ada.shape[2]
    g_w_ada = _ada_weight_grad(sc_all, lax.dynamic_slice(dmod_rows, (0, shard * ncol), (N_DEV * SUBLANES, ncol)))
    cshard = conv_w.shape[2]
    g_conv_w = lax.dynamic_slice(g_conv_w_full, (0, shard * cshard), (3, cshard))

    def big(w, m, v, g_parts, name):
        shape = w.shape
        outs = _adamw(w[0], g_parts, m[0], v[0], name)
        return [o.reshape(shape) for o in outs]

    r_w_ada = big(w_ada, m_w_ada, v_w_ada, [g_w_ada], "adamw_w_ada")
    r_w_in = [jnp.transpose(o).reshape(w_in.shape) for o in
              _adamw(jnp.transpose(w_in[0]), [mine[0], theirs[0]], jnp.transpose(m_w_in[0]), jnp.transpose(v_w_in[0]),
                     "adamw_w_in")]
    r_w_out = big(w_out, m_w_out, v_w_out, [mine[1], theirs[1]], "adamw_w_out")
    r_w_up = big(w_up, m_w_up, v_w_up, [mine[2], theirs[2]], "adamw_w_up")
    r_w_down = big(w_down, m_w_down, v_w_down, [mine[3], theirs[3]], "adamw_w_down")

    r_conv_w = big(conv_w, m_conv_w, v_conv_w, [g_conv_w], "adamw_conv_w")

    def pick(k):
        ga_, rpb_, sk_, gna_, gsw_, gf_, cb_, gfin_, b_ = [r[k] for r in r_small]
        return [r_w_ada[k], b_, ga_, r_w_in[k], rpb_.reshape(na_rpb.shape), sk_, gna_, gsw_, r_w_out[k], gf_,
                r_w_up[k], r_conv_w[k], cb_, r_w_down[k], gfin_.reshape(d)]

    return (loss, gx.reshape(batch, seq, d), *pick(0), *pick(1), *pick(2), *pick(3))
```

```python
import functools

import jax
import jax.numpy as jnp
import numpy as np
from jax import lax
from jax.experimental import pallas as pl
from jax.experimental.pallas import tpu as pltpu

F32 = jnp.float32
BF16 = jnp.bfloat16
MESH = pl.DeviceIdType.MESH

D_MODEL = 1024
HEAD_DIM = 64
NA_WIDTH = 512
SW_WIDTH = 512
SW_KV_WIDTH = 128
IN_WIDTH = 2304
D_FF = 2816
GRID_W = 64
NA_ROWS = 8
NA_COLS = 16
SW_BLOCK = 128
ROPE_THETA = 10000.0
EPS = 1e-6
NEG = -1e30
QK_SCALE = HEAD_DIM ** -0.5

ADAM_LR = 0.001
ADAM_B1 = 0.9
ADAM_B2 = 0.999
ADAM_EPS = 1e-08
ADAM_WD = 0.01
ADAM_STEP = 10

N_SHARD = 4
N_DEV = 8
LANES = 128
SUBLANES = 8
TOKEN_TILE = 512
FF_TILE = 256
CONV_CHUNK = 64
NA_GROUP = 4
SW_GROUP_BLOCKS = 4
VMEM_BIG = 56 * 1024 * 1024


def _mm(a, b):
    return jnp.dot(a, b, preferred_element_type=F32)


def _mm_nt(a, b):
    return lax.dot_general(a, b, (((1,), (1,)), ((), ())), preferred_element_type=F32)


def _mm_tn(a, b):
    return lax.dot_general(a, b, (((0,), (0,)), ((), ())), preferred_element_type=F32)


def _cparams(sem=None, vmem=None):
    kw = {}
    if sem is not None:
        kw["dimension_semantics"] = sem
    if vmem is not None:
        kw["vmem_limit_bytes"] = vmem
    return pltpu.CompilerParams(**kw)


def _resident(shape):
    return pl.BlockSpec(shape, lambda i: (0,) * len(shape), pipeline_mode=pl.Buffered(1))


def _sigmoid(x):
    return 1.0 / (1.0 + jnp.exp(-x))


def _rms_stats(x):
    r = lax.rsqrt(jnp.mean(x * x, axis=-1, keepdims=True) + EPS)
    return r, x * r


def _rms_bwd(dxn, xn, r):
    return r * (dxn - xn * jnp.mean(dxn * xn, axis=-1, keepdims=True))


def _my_pos():
    return lax.axis_index("x"), lax.axis_index("y"), lax.axis_index("c")


def _flip(v, bit):
    return 1 - v if bit else v


def _ada_forward(c8, w_ada, b_ada, rider):
    d = c8.shape[1]
    ncol = w_ada.shape[1]

    def body(c_ref, w_ref, b_ref, mod_ref, sc_ref, m_scr, mod_buf, ssem, rsem, ssem2, rsem2):
        x, y, c = _my_pos()
        me = 4 * x + 2 * y + c
        shard = 2 * x + y
        cv = c_ref[...]
        my_rows = pl.ds(pl.multiple_of(me * SUBLANES, SUBLANES), SUBLANES)
        sc_ref[my_rows, :] = cv * _sigmoid(cv)

        def copy1(k):
            peer = (_flip(x, (k >> 2) & 1), _flip(y, (k >> 1) & 1), _flip(c, k & 1))
            return pltpu.make_async_remote_copy(
                src_ref=sc_ref.at[my_rows, :], dst_ref=sc_ref.at[my_rows, :],
                send_sem=ssem.at[k - 1], recv_sem=rsem.at[k - 1], device_id=peer, device_id_type=MESH)

        sends = [copy1(k) for k in range(1, N_DEV)]
        for cp in sends:
            cp.start()
        for cp in sends:
            cp.wait_recv()
        m_scr[...] = _mm(sc_ref[...].astype(BF16), w_ref[...].astype(BF16))

        def copy2(k):
            px, py = _flip(x, (k >> 1) & 1), _flip(y, k & 1)
            rows = pl.ds(pl.multiple_of((4 * px + 2 * py + c) * SUBLANES, SUBLANES), SUBLANES)
            return pltpu.make_async_remote_copy(
                src_ref=m_scr.at[rows, :], dst_ref=mod_buf.at[shard],
                send_sem=ssem2.at[k - 1], recv_sem=rsem2.at[k - 1], device_id=(px, py, c), device_id_type=MESH)

        sends2 = [copy2(k) for k in range(1, N_SHARD)]
        for cp in sends2:
            cp.start()
        mod_buf[shard] = m_scr[my_rows, :]
        for cp in sends2:
            cp.wait_recv()
        for s in range(N_SHARD):
            mod_ref[:, s * ncol:(s + 1) * ncol] = mod_buf[s] + b_ref[:, s * ncol:(s + 1) * ncol]
        for cp in sends + sends2:
            cp.wait_send()

    vm = pl.BlockSpec(memory_space=pltpu.VMEM)
    return _hosted(
        body, rider, name="ada_forward", grid=(),
        out_shape=(jax.ShapeDtypeStruct((SUBLANES, N_SHARD * ncol), F32),
                   jax.ShapeDtypeStruct((N_DEV * SUBLANES, d), F32)),
        in_specs=[vm, vm, vm], out_specs=(vm, vm),
        scratch_shapes=[pltpu.VMEM((N_DEV * SUBLANES, ncol), F32), pltpu.VMEM((N_SHARD, SUBLANES, ncol), F32),
                        pltpu.SemaphoreType.DMA((N_DEV - 1,)), pltpu.SemaphoreType.DMA((N_DEV - 1,)),
                        pltpu.SemaphoreType.DMA((N_SHARD - 1,)), pltpu.SemaphoreType.DMA((N_SHARD - 1,))],
        compiler_params=_cparams(vmem=VMEM_BIG), args=[c8, w_ada, b_ada])


class _Rider:
    def __init__(self, kind, srcs, owns=()):
        self.kind, self.srcs, self.owns = kind, list(srcs), list(owns)
        n = len(self.srcs)
        sds = jax.ShapeDtypeStruct
        dma = pltpu.SemaphoreType.DMA
        if kind == "gather":
            self.out_shapes = [sds((N_SHARD,) + s.shape, s.dtype) for s in self.srcs]
            self.sems = [dma((n, N_SHARD - 1)), dma((n, N_SHARD - 1)), dma((n, N_SHARD - 1)), dma((n, N_SHARD - 1)),
                         dma((n,)), dma((n,))]
        elif kind == "scatter":
            self.out_shapes = ([sds((N_SHARD - 1,) + s.shape[1:], s.dtype) for s in self.srcs]
                               + [sds(o.shape[1:], o.dtype) for o in self.owns])
            m = max(len(self.owns), 1)
            self.sems = [dma((n, N_SHARD - 1)), dma((n, N_SHARD - 1)), dma((m,)), dma((m,))]
        else:
            self.out_shapes = [sds(s.shape, s.dtype) for s in self.srcs]
            self.sems = [dma((n,)), dma((n,))]

    @property
    def inputs(self):
        return self.srcs + self.owns

    def _halved(self, i):
        a = self.srcs[i]
        tile_rows = SUBLANES * (4 // jnp.dtype(a.dtype).itemsize)
        return self.kind == "gather" and a.shape[0] % (2 * tile_rows) == 0

    def copies(self, ins, outs, sems):
        n = len(self.srcs)
        x, y, c = _my_pos()
        shard = 2 * x + y
        remote, relay = [], []
        if self.kind == "swap":
            ssem, rsem = sems
            for i in range(n):
                remote.append(pltpu.make_async_remote_copy(
                    src_ref=ins[i], dst_ref=outs[i], send_sem=ssem.at[i], recv_sem=rsem.at[i],
                    device_id=(x, y, 1 - c), device_id_type=MESH))
            return remote, relay
        if self.kind == "gather":
            ssem, rsem, ssem2, rsem2, sib_s, sib_r = sems
        else:
            ssem, rsem, sib_s, sib_r = sems
        for i in range(n):
            if self.kind == "gather":
                remote.append(pltpu.make_async_remote_copy(
                    src_ref=ins[i], dst_ref=outs[i].at[shard], send_sem=sib_s.at[i], recv_sem=sib_r.at[i],
                    device_id=(x, y, 1 - c), device_id_type=MESH))
                half = ins[i].shape[0] // 2
                mine = pl.ds(pl.multiple_of(c * half, half), half) if self._halved(i) else None
            for k in range(1, N_SHARD):
                px, py = _flip(x, (k >> 1) & 1), _flip(y, k & 1)
                if self.kind == "gather":
                    src, dst = ins[i], outs[i].at[shard]
                    if mine is not None:
                        src, dst = src.at[mine], dst.at[mine]
                        got = outs[i].at[2 * px + py].at[mine]
                        relay.append(pltpu.make_async_remote_copy(
                            src_ref=got, dst_ref=got, send_sem=ssem2.at[i, k - 1], recv_sem=rsem2.at[i, k - 1],
                            device_id=(x, y, 1 - c), device_id_type=MESH))
                else:
                    src, dst = ins[i].at[2 * px + py], outs[i].at[k - 1]
                remote.append(pltpu.make_async_remote_copy(
                    src_ref=src, dst_ref=dst, send_sem=ssem.at[i, k - 1], recv_sem=rsem.at[i, k - 1],
                    device_id=(px, py, c), device_id_type=MESH))
        if self.kind == "scatter":
            for i in range(len(self.owns)):
                remote.append(pltpu.make_async_remote_copy(
                    src_ref=ins[n + i].at[shard], dst_ref=outs[n + i], send_sem=sib_s.at[i], recv_sem=sib_r.at[i],
                    device_id=(x, y, 1 - c), device_id_type=MESH))
        return remote, relay

    def start(self, ins, outs, sems):
        remote, _ = self.copies(ins, outs, sems)
        for cp in remote:
            cp.start()

    def wait(self, ins, outs, sems):
        remote, relay = self.copies(ins, outs, sems)
        for cp in remote:
            cp.wait_recv()
        for cp in relay:
            cp.start()
        for cp in relay:
            cp.wait_recv()
        for cp in remote + relay:
            cp.wait_send()


def _hosted(body, rider, *, name, grid, out_shape, in_specs, out_specs, scratch_shapes, compiler_params, args):
    out_shape, out_specs = list(out_shape), list(out_specs)
    if rider is None:
        outs = pl.pallas_call(body, name=name, grid=grid, out_shape=tuple(out_shape), in_specs=list(in_specs),
                              out_specs=tuple(out_specs), scratch_shapes=list(scratch_shapes),
                              compiler_params=compiler_params)(*args)
        return list(outs), []
    n_in, n_out, n_scr = len(in_specs), len(out_shape), len(scratch_shapes)
    nr_in, nr_out = len(rider.inputs), len(rider.out_shapes)
    n_steps = 1
    for size in grid:
        n_steps *= size

    def full(*refs):
        ins, refs = refs[:n_in], refs[n_in:]
        r_in, refs = refs[:nr_in], refs[nr_in:]
        outs, refs = refs[:n_out], refs[n_out:]
        r_out, refs = refs[:nr_out], refs[nr_out:]
        scr, sems = refs[:n_scr], refs[n_scr:]
        if grid:
            step = 0
            for ax, size in enumerate(grid):
                step = step * size + pl.program_id(ax)
            pl.when(step == 0)(lambda: rider.start(r_in, r_out, sems))
            body(*ins, *outs, *scr)
            pl.when(step == n_steps - 1)(lambda: rider.wait(r_in, r_out, sems))
        else:
            rider.start(r_in, r_out, sems)
            body(*ins, *outs, *scr)
            rider.wait(r_in, r_out, sems)

    hbm = pl.BlockSpec(memory_space=pl.ANY)
    res = pl.pallas_call(
        full, name=name, grid=grid, out_shape=tuple(out_shape + rider.out_shapes),
        in_specs=list(in_specs) + [hbm] * nr_in, out_specs=tuple(out_specs + [hbm] * nr_out),
        scratch_shapes=list(scratch_shapes) + rider.sems, compiler_params=compiler_params,
    )(*args, *rider.inputs)
    return list(res[:n_out]), list(res[n_out:])


def _ride_alone(rider, name):
    return _hosted(lambda: None, rider, name=name, grid=(), out_shape=[], in_specs=[], out_specs=[], scratch_shapes=[],
                   compiler_params=_cparams(), args=[])[1]


def _allreduce_small(packed, rider=None):
    r = packed.shape[0]

    def body(p_ref, sum_ref, all_ref, ssem, rsem):
        x, y, c = _my_pos()
        me = 4 * x + 2 * y + c
        all_ref[me] = p_ref[...]
        cps = []
        for k in range(1, N_DEV):
            peer = (_flip(x, (k >> 2) & 1), _flip(y, (k >> 1) & 1), _flip(c, k & 1))
            cps.append(pltpu.make_async_remote_copy(
                src_ref=all_ref.at[me], dst_ref=all_ref.at[me], send_sem=ssem.at[k - 1], recv_sem=rsem.at[k - 1],
                device_id=peer, device_id_type=MESH))
        for cp in cps:
            cp.start()
        for cp in cps:
            cp.wait_recv()
        acc = all_ref[0]
        for dev in range(1, N_DEV):
            acc = acc + all_ref[dev]
        sum_ref[...] = acc
        for cp in cps:
            cp.wait_send()

    vm = pl.BlockSpec(memory_space=pltpu.VMEM)
    return _hosted(
        body, rider, name="allreduce_small", grid=(),
        out_shape=[jax.ShapeDtypeStruct((r, LANES), F32), jax.ShapeDtypeStruct((N_DEV, r, LANES), F32)],
        in_specs=[vm], out_specs=[vm, vm],
        scratch_shapes=[pltpu.SemaphoreType.DMA((N_DEV - 1,)), pltpu.SemaphoreType.DMA((N_DEV - 1,))],
        compiler_params=_cparams(), args=[packed])


def _rope_rot(t):
    w = t.shape[1]
    lane = lax.broadcasted_iota(jnp.int32, t.shape, 1)
    first = (lane % HEAD_DIM) < (HEAD_DIM // 2)
    return jnp.where(first, pltpu.roll(t, w - HEAD_DIM // 2, 1), pltpu.roll(t, HEAD_DIM // 2, 1))


def _in_proj(x, mod3, g_attn, w_in_t, cos_t, sin_t, seq, rider=None):
    t, d = x.shape
    tm = TOKEN_TILE
    per_seq = seq // tm
    rope_lo, rope_hi = 3 * NA_WIDTH, 3 * NA_WIDTH + SW_WIDTH + SW_KV_WIDTH
    n_rep = (rope_hi - rope_lo) // LANES

    def body(x_ref, mod_ref, g_ref, w_ref, cos_ref, sin_ref, h_ref, p_ref):
        r, xn = _rms_stats(x_ref[...])
        shift, scale = mod_ref[0, :, 0:d], mod_ref[0, :, d:2 * d]
        hb = ((xn * g_ref[...]) * (1.0 + scale) + shift).astype(BF16)
        h_ref[...] = hb
        p_ref[:, :rope_lo] = _mm_nt(hb, w_ref[:rope_lo, :]).astype(BF16)
        pr = _mm_nt(hb, w_ref[rope_lo:rope_hi, :])
        cos = jnp.concatenate([cos_ref[...]] * n_rep, axis=1)
        sin = jnp.concatenate([sin_ref[...]] * n_rep, axis=1)
        p_ref[:, rope_lo:rope_hi] = (pr * cos + _rope_rot(pr) * sin).astype(BF16)
        p_ref[:, rope_hi:] = _mm_nt(hb, w_ref[rope_hi:, :]).astype(BF16)

    return _hosted(
        body, rider, name="in_proj", grid=(t // tm,),
        out_shape=[jax.ShapeDtypeStruct((t, d), BF16), jax.ShapeDtypeStruct((t, IN_WIDTH), BF16)],
        in_specs=[pl.BlockSpec((tm, d), lambda i: (i, 0)),
                  pl.BlockSpec((1, 1, 6 * d), lambda i: (i // per_seq, 0, 0)),
                  pl.BlockSpec((1, d), lambda i: (0, 0)),
                  pl.BlockSpec((IN_WIDTH, d), lambda i: (0, 0)),
                  pl.BlockSpec((tm, LANES), lambda i: (i % per_seq, 0)),
                  pl.BlockSpec((tm, LANES), lambda i: (i % per_seq, 0))],
        out_specs=[pl.BlockSpec((tm, d), lambda i: (i, 0)), pl.BlockSpec((tm, IN_WIDTH), lambda i: (i, 0))],
        scratch_shapes=[], compiler_params=_cparams(("arbitrary",), VMEM_BIG),
        args=[x, mod3, g_attn, w_in_t, cos_t, sin_t])


def _na_bias_pattern():
    n_dc = 2 * NA_COLS - 1
    j = np.arange(GRID_W)[:, None]
    m = np.arange(GRID_W * LANES)[None, :]
    q, lane = m // LANES, m % LANES
    k = lane % GRID_W
    cs = np.clip(q - NA_COLS // 2, 0, GRID_W - NA_COLS)
    ok = (k >= cs) & (k < cs + NA_COLS)
    hit = ok & (j < 2 * n_dc) & (lane // GRID_W == j // n_dc) & (k - q + (NA_COLS - 1) == j % n_dc)
    return jnp.asarray(hit.astype(np.float32)), jnp.asarray(np.where(ok, 0.0, NEG).astype(np.float32))


def _na_bias_tiles(rows2, expand, mask):
    n, width = rows2.shape[0], expand.shape[1]
    q_step = 16
    step = q_step * LANES

    def body(r_ref, e_ref, m_ref, o_ref):
        flat = jnp.dot(r_ref[...], e_ref[...], precision=lax.Precision.HIGHEST,
                       preferred_element_type=F32) + m_ref[...]
        for qq in range(q_step):
            o_ref[:, qq, :] = flat[:, qq * LANES:(qq + 1) * LANES]

    return pl.pallas_call(
        body, name="na_bias_tiles", grid=(width // step,),
        out_shape=jax.ShapeDtypeStruct((n, GRID_W, LANES), F32),
        in_specs=[pl.BlockSpec(rows2.shape, lambda i: (0, 0)), pl.BlockSpec((expand.shape[0], step), lambda i: (0, i)),
                  pl.BlockSpec((1, step), lambda i: (0, i))],
        out_specs=pl.BlockSpec((n, q_step, LANES), lambda i: (0, i, 0)),
        compiler_params=_cparams(("arbitrary",)),
    )(rows2, expand, mask)


def _na_prepare(k_ref, v_ref, km, vm):
    lane = lax.broadcasted_iota(jnp.int32, k_ref.shape, 1)
    low = lane < HEAD_DIM
    kv = k_ref[...]
    vv = v_ref[...]
    zero = jnp.zeros_like(kv)
    km[0] = jnp.where(low, kv, zero)
    km[1] = jnp.where(low, zero, kv)
    vm[0] = jnp.where(low, vv, zero)
    vm[1] = jnp.where(low, zero, vv)


def _na_window(r, n_rows):
    rs = jnp.clip(r - NA_ROWS // 2, 0, n_rows - NA_ROWS)
    return rs, r - rs


def _na_pair_window(ref, wrows):
    return jnp.concatenate([ref[0, wrows, :], ref[1, wrows, :]], axis=0)


def _na_scores(q, k2, tp_ref, off):
    bias = jnp.concatenate([tp_ref[h, 2 * w - off + (NA_ROWS - 1)] for h in range(2) for w in range(NA_ROWS // 2)],
                           axis=1)
    return _mm_nt(q, k2) * QK_SCALE + bias


def _pair_softmax(s):
    win = s.shape[1] // 2
    halves = []
    for h in range(2):
        sh = s[:, h * win:(h + 1) * win]
        e = jnp.exp(sh - jnp.max(sh, axis=-1, keepdims=True))
        halves.append(e / jnp.sum(e, axis=-1, keepdims=True))
    return jnp.concatenate(halves, axis=1)


def _na_forward(proj, tiles, batch, seq, rider=None):
    t = proj.shape[0]
    n_rows = seq // GRID_W
    n_pairs = NA_WIDTH // LANES
    win = NA_ROWS * GRID_W

    def body(q_ref, k_ref, v_ref, tp_ref, o_ref, km, vm):
        _na_prepare(k_ref, v_ref, km, vm)

        def scores(r):
            rs, off = _na_window(r, n_rows)
            rows = pl.ds(pl.multiple_of(r * GRID_W, GRID_W), GRID_W)
            wrows = pl.ds(pl.multiple_of(rs * GRID_W, GRID_W), win)
            return rows, wrows, _na_scores(q_ref[rows, :], _na_pair_window(km, wrows), tp_ref, off)

        def finish(rows, wrows, s):
            o_ref[rows, :] = _mm(_pair_softmax(s).astype(BF16), _na_pair_window(vm, wrows))

        def row_group(i, carry):
            for state in [scores(NA_GROUP * i + j) for j in range(NA_GROUP)]:
                finish(*state)
            return carry

        lax.fori_loop(0, n_rows // NA_GROUP, row_group, 0)

    return _hosted(
        body, rider, name="na_forward", grid=(batch, n_pairs),
        out_shape=[jax.ShapeDtypeStruct((t, NA_WIDTH), F32)],
        in_specs=[pl.BlockSpec((seq, LANES), lambda b, p: (b, p)),
                  pl.BlockSpec((seq, LANES), lambda b, p: (b, n_pairs + p)),
                  pl.BlockSpec((seq, LANES), lambda b, p: (b, 2 * n_pairs + p)),
                  pl.BlockSpec((2, 2 * NA_ROWS - 2, GRID_W, LANES), lambda b, p: (p, 0, 0, 0))],
        out_specs=[pl.BlockSpec((seq, LANES), lambda b, p: (b, p))],
        scratch_shapes=[pltpu.VMEM((2, seq, LANES), BF16), pltpu.VMEM((2, seq, LANES), BF16)],
        compiler_params=_cparams(("arbitrary", "arbitrary")), args=[proj, proj, proj, tiles])


def _sw_prepare(kv_ref, g, dst_lo, dst_hi, seq):
    lane = lax.broadcasted_iota(jnp.int32, kv_ref.shape, 1)
    mine = (lane // HEAD_DIM) == g
    kg = jnp.where(mine, kv_ref[...].astype(F32), 0.0)
    kr = pltpu.roll(kg, HEAD_DIM, 1)
    first = g == 0
    zero = jnp.zeros((SW_BLOCK, LANES), BF16)
    for dst, val in ((dst_lo, jnp.where(first, kg, kr)), (dst_hi, jnp.where(first, kr, kg))):
        dst[0:SW_BLOCK, :] = zero
        dst[SW_BLOCK:SW_BLOCK + seq, :] = val.astype(BF16)
        dst[SW_BLOCK + seq:, :] = zero


def _sw_mask(n, seq):
    qi = lax.broadcasted_iota(jnp.int32, (SW_BLOCK, 3 * SW_BLOCK), 0)
    kj = lax.broadcasted_iota(jnp.int32, (SW_BLOCK, 3 * SW_BLOCK), 1)
    kpos = n * SW_BLOCK - SW_BLOCK + kj
    return (jnp.abs(qi + SW_BLOCK - kj) <= SW_BLOCK) & (kpos >= 0) & (kpos < seq)


def _sw_probs(s2, ok, sinks):
    band = s2.shape[1] // 2
    halves, sink_p = [], []
    for i in range(2):
        s = jnp.where(ok, s2[:, i * band:(i + 1) * band], NEG)
        m = jnp.maximum(jnp.max(s, axis=-1, keepdims=True), sinks[i])
        p = jnp.exp(s - m)
        es = jnp.exp(sinks[i] - m)
        den = jnp.sum(p, axis=-1, keepdims=True) + es
        halves.append(p / den)
        sink_p.append(es / den)
    return jnp.concatenate(halves, axis=1), sink_p


def _sw_forward(proj, sink, batch, seq, rider=None):
    t = proj.shape[0]
    n_pairs = SW_WIDTH // LANES
    q_blk = 3 * NA_WIDTH // LANES
    k_blk = q_blk + n_pairs
    n_blocks = seq // SW_BLOCK
    pad = seq + 2 * SW_BLOCK

    def body(sink_ref, q_ref, k_ref, v_ref, o_ref, k_lo, k_hi, v_lo, v_hi):
        hp = pl.program_id(1)
        g = hp // 2
        _sw_prepare(k_ref, g, k_lo, k_hi, seq)
        _sw_prepare(v_ref, g, v_lo, v_hi, seq)

        sinks = (sink_ref[2 * hp], sink_ref[2 * hp + 1])

        def scores(n):
            rows = pl.ds(pl.multiple_of(n * SW_BLOCK, SW_BLOCK), SW_BLOCK)
            wrows = pl.ds(pl.multiple_of(n * SW_BLOCK, SW_BLOCK), 3 * SW_BLOCK)
            k2 = jnp.concatenate([k_lo[wrows, :], k_hi[wrows, :]], axis=0)
            return n, rows, wrows, _mm_nt(q_ref[rows, :], k2) * QK_SCALE

        def finish(n, rows, wrows, s2):
            p, _ = _sw_probs(s2, _sw_mask(n, seq), sinks)
            v2 = jnp.concatenate([v_lo[wrows, :], v_hi[wrows, :]], axis=0)
            o_ref[rows, :] = _mm(p.astype(BF16), v2)

        def block_group(i, carry):
            for state in [scores(SW_GROUP_BLOCKS * i + j) for j in range(SW_GROUP_BLOCKS)]:
                finish(*state)
            return carry

        lax.fori_loop(0, n_blocks // SW_GROUP_BLOCKS, block_group, 0)

    return _hosted(
        body, rider, name="sw_forward", grid=(batch, n_pairs),
        out_shape=[jax.ShapeDtypeStruct((t, SW_WIDTH), F32)],
        in_specs=[pl.BlockSpec(memory_space=pltpu.SMEM),
                  pl.BlockSpec((seq, LANES), lambda b, p: (b, q_blk + p)),
                  pl.BlockSpec((seq, LANES), lambda b, p: (b, k_blk)),
                  pl.BlockSpec((seq, LANES), lambda b, p: (b, k_blk + 1))],
        out_specs=[pl.BlockSpec((seq, LANES), lambda b, p: (b, p))],
        scratch_shapes=[pltpu.VMEM((pad, LANES), BF16)] * 4,
        compiler_params=_cparams(("arbitrary", "arbitrary")), args=[sink, proj, proj, proj])


def _out_proj(oa, ob, g_na, g_sw, w_out, x, mod3, g_ffn, seq):
    t, d = x.shape
    tm = TOKEN_TILE
    per_seq = seq // tm

    def body(oa_ref, ob_ref, gna_ref, gsw_ref, w_ref, x_ref, mod_ref, gf_ref, oab_ref, mix_ref, x1_ref, h2_ref):
        _, na = _rms_stats(oa_ref[...])
        _, nb = _rms_stats(ob_ref[...])
        oab = jnp.concatenate([na * gna_ref[...], nb * gsw_ref[...]], axis=1).astype(BF16)
        oab_ref[...] = oab
        mix = _mm(oab, w_ref[...])
        mix_ref[...] = mix
        gate_a = mod_ref[0, :, 2 * d:3 * d]
        shift_f, scale_f = mod_ref[0, :, 3 * d:4 * d], mod_ref[0, :, 4 * d:5 * d]
        x1 = x_ref[...] + gate_a * mix
        x1_ref[...] = x1
        _, xn = _rms_stats(x1)
        h2_ref[...] = ((xn * gf_ref[...]) * (1.0 + scale_f) + shift_f).astype(BF16)

    tile = lambda w: pl.BlockSpec((tm, w), lambda i: (i, 0))
    vec = lambda w: pl.BlockSpec((1, w), lambda i: (0, 0))
    return pl.pallas_call(
        body, name="out_proj", grid=(t // tm,),
        out_shape=(jax.ShapeDtypeStruct((t, d), BF16), jax.ShapeDtypeStruct((t, d), F32),
                   jax.ShapeDtypeStruct((t, d), F32), jax.ShapeDtypeStruct((t, d), BF16)),
        in_specs=[tile(NA_WIDTH), tile(SW_WIDTH), vec(NA_WIDTH), vec(SW_WIDTH),
                  pl.BlockSpec((d, d), lambda i: (0, 0)), tile(d),
                  pl.BlockSpec((1, 1, 6 * d), lambda i: (i // per_seq, 0, 0)), vec(d)],
        out_specs=(tile(d), tile(d), tile(d), tile(d)),
        compiler_params=_cparams(("arbitrary",), VMEM_BIG),
    )(oa, ob, g_na, g_sw, w_out, x, mod3, g_ffn)


def _up_proj(h2, w_up_halves, rider=None):
    t, d = h2.shape
    tm = TOKEN_TILE
    w_a, w_b = w_up_halves
    half, wcol = w_a.shape[1], w_a.shape[2]

    def body(h_ref, wa_ref, wb_ref, u_ref):
        u_ref[0] = (_mm(h_ref[:, :half], wa_ref[0]) + _mm(h_ref[:, half:], wb_ref[0])).astype(BF16)

    w_spec = pl.BlockSpec((1, half, wcol), lambda j, i: (j, 0, 0))
    return _hosted(
        body, rider, name="up_proj", grid=(N_SHARD, t // tm),
        out_shape=[jax.ShapeDtypeStruct((2, t, D_FF), BF16)],
        in_specs=[pl.BlockSpec((tm, d), lambda j, i: (i, 0)), w_spec, w_spec],
        out_specs=[pl.BlockSpec((1, tm, wcol), lambda j, i: (j // 2, i, j % 2))],
        scratch_shapes=[], compiler_params=_cparams(("arbitrary", "arbitrary"), VMEM_BIG), args=[h2, w_a, w_b])


def _taps_chunk(load, s, rows, seq):
    halo = 2 * SUBLANES
    cur = load(s, rows)
    above = load(pl.multiple_of(jnp.maximum(s - halo, 0), halo), halo)
    below = load(pl.multiple_of(jnp.minimum(s + rows, seq - halo), halo), halo)
    up = jnp.where(s > 0, above[halo - 1:halo, :], 0.0)
    dn = jnp.where(s + rows < seq, below[0:1, :], 0.0)
    row = lax.broadcasted_iota(jnp.int32, cur.shape, 0)
    prev = jnp.where(row == 0, up, pltpu.roll(cur, 1, 0))
    nxt = jnp.where(row == rows - 1, dn, pltpu.roll(cur, rows - 1, 0))
    return cur, prev, nxt


def _conv_gate(u, conv_w, conv_b, batch, seq):
    t = u.shape[1]
    cw = FF_TILE
    rows = CONV_CHUNK

    def body(u_ref, w_ref, b_ref, a_ref):
        def chunk(i, carry):
            s = pl.multiple_of(i * rows, rows)
            gt, prev, nxt = _taps_chunk(lambda at, n: u_ref[1, pl.ds(at, n), :].astype(F32), s, rows, seq)
            gc = prev * w_ref[0:1, :] + gt * w_ref[1:2, :] + nxt * w_ref[2:3, :] + b_ref[...]
            a_ref[pl.ds(s, rows), :] = ((gc * _sigmoid(gc)) * u_ref[0, pl.ds(s, rows), :].astype(F32)).astype(BF16)
            return carry

        lax.fori_loop(0, seq // rows, chunk, 0)

    return pl.pallas_call(
        body, name="conv_gate", grid=(batch, D_FF // cw),
        out_shape=jax.ShapeDtypeStruct((t, D_FF), BF16),
        in_specs=[pl.BlockSpec((2, seq, cw), lambda b, j: (0, b, j)),
                  pl.BlockSpec((3, cw), lambda b, j: (0, j)), pl.BlockSpec((1, cw), lambda b, j: (0, j))],
        out_specs=pl.BlockSpec((seq, cw), lambda b, j: (b, j)),
        compiler_params=_cparams(("arbitrary", "arbitrary"), VMEM_BIG),
    )(u, conv_w, conv_b)


def _down_and_loss(a, w_down, x1, mod3, g_final, target, seq):
    t, d = x1.shape
    tm = TOKEN_TILE
    per_seq = seq // tm
    batch = t // seq

    def body(a_ref, w_ref, x1_ref, mod_ref, g_ref, tgt_ref, dx2_ref, dffn_ref, loss_ref, dgate_ref, dg_ref):
        i = pl.program_id(0)
        f = _mm(a_ref[...], w_ref[...])
        gate_f = mod_ref[0, :, 5 * d:6 * d]
        x2 = x1_ref[...] + gate_f * f
        r, xn = _rms_stats(x2)
        err = xn * g_ref[...] - tgt_ref[...]
        part = 0.5 * jnp.sum(jnp.mean(err * err, axis=-1, keepdims=True))
        dy = err / d
        dx2 = _rms_bwd(dy * g_ref[...], xn, r)
        dx2_ref[...] = dx2
        dffn_ref[...] = (dx2 * gate_f).astype(BF16)

        @pl.when(i == 0)
        def _():
            loss_ref[...] = jnp.zeros_like(loss_ref)
            dg_ref[...] = jnp.zeros_like(dg_ref)

        @pl.when(i % per_seq == 0)
        def _():
            dgate_ref[...] = jnp.zeros_like(dgate_ref)

        loss_ref[...] += part
        dg_ref[...] += jnp.sum(dy * xn, axis=0, keepdims=True)
        dgate_ref[0] += jnp.sum(dx2 * f, axis=0, keepdims=True)

    tile = lambda w: pl.BlockSpec((tm, w), lambda i: (i, 0))
    return pl.pallas_call(
        body, name="down_loss", grid=(t // tm,),
        out_shape=(jax.ShapeDtypeStruct((t, d), F32), jax.ShapeDtypeStruct((t, d), BF16),
                   jax.ShapeDtypeStruct((SUBLANES, LANES), F32), jax.ShapeDtypeStruct((batch, 1, d), F32),
                   jax.ShapeDtypeStruct((1, d), F32)),
        in_specs=[tile(D_FF), _resident((D_FF, d)), tile(d),
                  pl.BlockSpec((1, 1, 6 * d), lambda i: (i // per_seq, 0, 0)),
                  pl.BlockSpec((1, d), lambda i: (0, 0)), tile(d)],
        out_specs=(tile(d), tile(d), pl.BlockSpec((SUBLANES, LANES), lambda i: (0, 0)),
                   pl.BlockSpec((1, 1, d), lambda i: (i // per_seq, 0, 0)), pl.BlockSpec((1, d), lambda i: (0, 0))),
        compiler_params=_cparams(("arbitrary",), VMEM_BIG),
    )(a, w_down, x1, mod3, g_final, target)


def _down_weight_grad(a, dffn):
    t, dff = a.shape
    d = dffn.shape[1]
    tk = TOKEN_TILE
    n_k = t // tk

    def body(a_ref, df_ref, g_ref, gb_ref):
        k = pl.program_id(0)

        @pl.when(k == 0)
        def _():
            g_ref[...] = jnp.zeros_like(g_ref)

        g_ref[...] += _mm_tn(a_ref[...], df_ref[...])

        @pl.when(k == n_k - 1)
        def _():
            gb_ref[...] = g_ref[...].astype(BF16)

    whole = pl.BlockSpec((dff, d), lambda k: (0, 0))
    return pl.pallas_call(
        body, name="down_weight_grad", grid=(n_k,),
        out_shape=(jax.ShapeDtypeStruct((dff, d), F32), jax.ShapeDtypeStruct((dff, d), BF16)),
        in_specs=[pl.BlockSpec((tk, dff), lambda k: (k, 0)), pl.BlockSpec((tk, d), lambda k: (k, 0))],
        out_specs=(whole, whole),
        compiler_params=_cparams(("arbitrary",), VMEM_BIG),
    )(a, dffn)


def _ffn_backward(dffn, w_down, u, conv_w, conv_b, batch, seq, rider=None):
    t, d = dffn.shape
    cw = FF_TILE
    rows = CONV_CHUNK

    def body(df_ref, wd_ref, u_ref, w_ref, b_ref, du_ref, gcw_ref, gcb_ref, da_scr, dgc_scr):
        b = pl.program_id(1)
        da_scr[...] = _mm_nt(df_ref[...], wd_ref[...])

        @pl.when(b == 0)
        def _():
            gcw_ref[...] = jnp.zeros_like(gcw_ref)
            gcb_ref[...] = jnp.zeros_like(gcb_ref)

        def fold(v):
            return jnp.sum(v.reshape(rows // SUBLANES, SUBLANES, cw), axis=0)

        def chunk(i, carry):
            s = pl.multiple_of(i * rows, rows)
            here = pl.ds(s, rows)
            gt, prev, nxt = _taps_chunk(lambda at, n: u_ref[1, pl.ds(at, n), :].astype(F32), s, rows, seq)
            val, da = u_ref[0, here, :].astype(F32), da_scr[here, :]
            gc = prev * w_ref[0:1, :] + gt * w_ref[1:2, :] + nxt * w_ref[2:3, :] + b_ref[...]
            sg = _sigmoid(gc)
            sl = gc * sg
            du_ref[0, here, :] = (da * sl).astype(BF16)
            dgc = (da * val) * (sg * (1.0 + gc * (1.0 - sg)))
            dgc_scr[here, :] = dgc
            cb, c0, c1, c2 = carry
            return cb + fold(dgc), c0 + fold(dgc * prev), c1 + fold(dgc * gt), c2 + fold(dgc * nxt)

        zero = jnp.zeros((SUBLANES, cw), F32)
        cb, c0, c1, c2 = lax.fori_loop(0, seq // rows, chunk, (zero, zero, zero, zero))
        gcb_ref[...] += jnp.sum(cb, axis=0, keepdims=True)
        gcw_ref[0:1, :] += jnp.sum(c0, axis=0, keepdims=True)
        gcw_ref[1:2, :] += jnp.sum(c1, axis=0, keepdims=True)
        gcw_ref[2:3, :] += jnp.sum(c2, axis=0, keepdims=True)

        def chunk2(i, carry):
            s = pl.multiple_of(i * rows, rows)
            dgc, dprev, dnxt = _taps_chunk(lambda at, n: dgc_scr[pl.ds(at, n), :], s, rows, seq)
            du_ref[1, pl.ds(s, rows), :] = (dnxt * w_ref[0:1, :] + dgc * w_ref[1:2, :]
                                            + dprev * w_ref[2:3, :]).astype(BF16)
            return carry

        lax.fori_loop(0, seq // rows, chunk2, 0)

    return _hosted(
        body, rider, name="ffn_backward", grid=(D_FF // cw, batch),
        out_shape=[jax.ShapeDtypeStruct((2, t, D_FF), BF16),
                   jax.ShapeDtypeStruct((3, D_FF), F32), jax.ShapeDtypeStruct((1, D_FF), F32)],
        in_specs=[pl.BlockSpec((seq, d), lambda j, b: (b, 0)), pl.BlockSpec((cw, d), lambda j, b: (j, 0)),
                  pl.BlockSpec((2, seq, cw), lambda j, b: (0, b, j)),
                  pl.BlockSpec((3, cw), lambda j, b: (0, j)), pl.BlockSpec((1, cw), lambda j, b: (0, j))],
        out_specs=[pl.BlockSpec((2, seq, cw), lambda j, b: (0, b, j)),
                   pl.BlockSpec((3, cw), lambda j, b: (0, j)), pl.BlockSpec((1, cw), lambda j, b: (0, j))],
        scratch_shapes=[pltpu.VMEM((seq, cw), F32), pltpu.VMEM((seq, cw), F32)],
        compiler_params=_cparams(("arbitrary", "arbitrary"), VMEM_BIG), args=[dffn, w_down, u, conv_w, conv_b])


def _up_backward(du, w_up, x1, mod3, g_ffn, dx2, mix, seq, rider=None):
    _, t, _ = du.shape
    d = x1.shape[1]
    tm = TOKEN_TILE
    per_seq = seq // tm
    batch = t // seq
    w_a, w_b = w_up
    half, wcol = w_a.shape[1], w_a.shape[2]

    def body(du_ref, wa_ref, wb_ref, x1_ref, mod_ref, g_ref, dx2_ref, mix_ref,
             dx1_ref, dmix_ref, dsh_ref, dsc_ref, dga_ref, dg_ref):
        i = pl.program_id(0)
        parts = []
        for w_ref in (wa_ref, wb_ref):
            acc = jnp.zeros((tm, half), F32)
            for j in range(N_SHARD):
                acc = acc + _mm_nt(du_ref[j // 2, :, (j % 2) * wcol:(j % 2 + 1) * wcol], w_ref[j])
            parts.append(acc)
        dh = jnp.concatenate(parts, axis=1)
        gate_a = mod_ref[0, :, 2 * d:3 * d]
        scale_f = mod_ref[0, :, 4 * d:5 * d]
        r, xn = _rms_stats(x1_ref[...])
        xg = xn * g_ref[...]
        dxg = dh * (1.0 + scale_f)
        dx1 = dx2_ref[...] + _rms_bwd(dxg * g_ref[...], xn, r)
        dx1_ref[...] = dx1
        dmix_ref[...] = (dx1 * gate_a).astype(BF16)

        @pl.when(i == 0)
        def _():
            dg_ref[...] = jnp.zeros_like(dg_ref)

        @pl.when(i % per_seq == 0)
        def _():
            dsh_ref[...] = jnp.zeros_like(dsh_ref)
            dsc_ref[...] = jnp.zeros_like(dsc_ref)
            dga_ref[...] = jnp.zeros_like(dga_ref)

        dg_ref[...] += jnp.sum(dxg * xn, axis=0, keepdims=True)
        dsh_ref[0] += jnp.sum(dh, axis=0, keepdims=True)
        dsc_ref[0] += jnp.sum(dh * xg, axis=0, keepdims=True)
        dga_ref[0] += jnp.sum(dx1 * mix_ref[...], axis=0, keepdims=True)

    tile = lambda w: pl.BlockSpec((tm, w), lambda i: (i, 0))
    per_b = pl.BlockSpec((1, 1, d), lambda i: (i // per_seq, 0, 0))
    small = jax.ShapeDtypeStruct((batch, 1, d), F32)
    return _hosted(
        body, rider, name="up_backward", grid=(t // tm,),
        out_shape=[jax.ShapeDtypeStruct((t, d), F32), jax.ShapeDtypeStruct((t, d), BF16), small, small, small,
                   jax.ShapeDtypeStruct((1, d), F32)],
        in_specs=[pl.BlockSpec((2, tm, D_FF), lambda i: (0, i, 0)),
                  _resident((N_SHARD, half, wcol)), _resident((N_SHARD, half, wcol)), tile(d),
                  pl.BlockSpec((1, 1, 6 * d), lambda i: (i // per_seq, 0, 0)),
                  pl.BlockSpec((1, d), lambda i: (0, 0)), tile(d), tile(d)],
        out_specs=[tile(d), tile(d), per_b, per_b, per_b, pl.BlockSpec((1, d), lambda i: (0, 0))],
        scratch_shapes=[], compiler_params=_cparams(("arbitrary",), VMEM_BIG),
        args=[du, w_a, w_b, x1, mod3, g_ffn, dx2, mix])


def _up_weight_grad(h2, du, rider=None):
    t, d = h2.shape
    tk = TOKEN_TILE
    wcol = D_FF // 2
    half = d // 2
    n_k = t // tk

    def body(h_ref, du_ref, ga_ref, gb_ref, ga16_ref, gb16_ref):
        k = pl.program_id(1)

        @pl.when(k == 0)
        def _():
            ga_ref[...] = jnp.zeros_like(ga_ref)
            gb_ref[...] = jnp.zeros_like(gb_ref)

        du = du_ref[0]
        ga_ref[0] += _mm_tn(h_ref[:, :half], du)
        gb_ref[0] += _mm_tn(h_ref[:, half:], du)

        @pl.when(k == n_k - 1)
        def _():
            ga16_ref[...] = ga_ref[...].astype(BF16)
            gb16_ref[...] = gb_ref[...].astype(BF16)

    g_spec = pl.BlockSpec((1, half, wcol), lambda j, k: (j, 0, 0))
    f32_out = jax.ShapeDtypeStruct((N_SHARD, half, wcol), F32)
    b16_out = jax.ShapeDtypeStruct((N_SHARD, half, wcol), BF16)
    return _hosted(
        body, rider, name="up_weight_grad", grid=(N_SHARD, n_k),
        out_shape=[f32_out, f32_out, b16_out, b16_out],
        in_specs=[pl.BlockSpec((tk, d), lambda j, k: (k, 0)),
                  pl.BlockSpec((1, tk, wcol), lambda j, k: (j // 2, k, j % 2))],
        out_specs=[g_spec, g_spec, g_spec, g_spec], scratch_shapes=[],
        compiler_params=_cparams(("arbitrary", "arbitrary"), VMEM_BIG), args=[h2, du])


def _out_backward(dmix, w_out, oab, oa, ob, g_na, g_sw):
    t, d = dmix.shape
    tm = TOKEN_TILE
    hw = NA_WIDTH

    def body(dm_ref, w_ref, oab_ref, oa_ref, ob_ref, gna_ref, gsw_ref,
             doa_ref, dob_ref, gw_ref, gwb_ref, dgna_ref, dgsw_ref):
        @pl.when(pl.program_id(0) == 0)
        def _():
            gw_ref[...] = jnp.zeros_like(gw_ref)
            dgna_ref[...] = jnp.zeros_like(dgna_ref)
            dgsw_ref[...] = jnp.zeros_like(dgsw_ref)

        dm = dm_ref[...]
        gw_ref[...] += _mm_tn(oab_ref[...], dm)

        @pl.when(pl.program_id(0) == t // tm - 1)
        def _():
            gwb_ref[...] = gw_ref[...].astype(BF16)

        do = _mm_nt(dm, w_ref[...])
        for raw_ref, g_ref, dst_ref, dg_ref, lo in ((oa_ref, gna_ref, doa_ref, dgna_ref, 0),
                                                     (ob_ref, gsw_ref, dob_ref, dgsw_ref, hw)):
            r, xn = _rms_stats(raw_ref[...])
            dpart = do[:, lo:lo + hw]
            dg_ref[...] += jnp.sum(dpart * xn, axis=0, keepdims=True)
            dst_ref[...] = _rms_bwd(dpart * g_ref[...], xn, r).astype(BF16)

    tile = lambda w: pl.BlockSpec((tm, w), lambda i: (i, 0))
    vec = lambda w: pl.BlockSpec((1, w), lambda i: (0, 0))
    return pl.pallas_call(
        body, name="out_backward", grid=(t // tm,),
        out_shape=(jax.ShapeDtypeStruct((t, hw), BF16), jax.ShapeDtypeStruct((t, hw), BF16),
                   jax.ShapeDtypeStruct((d, d), F32), jax.ShapeDtypeStruct((d, d), BF16),
                   jax.ShapeDtypeStruct((1, hw), F32), jax.ShapeDtypeStruct((1, hw), F32)),
        in_specs=[tile(d), pl.BlockSpec((d, d), lambda i: (0, 0)), tile(d), tile(hw), tile(hw), vec(hw), vec(hw)],
        out_specs=(tile(hw), tile(hw), pl.BlockSpec((d, d), lambda i: (0, 0)), pl.BlockSpec((d, d), lambda i: (0, 0)),
                   vec(hw), vec(hw)),
        compiler_params=_cparams(("arbitrary",), VMEM_BIG),
    )(dmix, w_out, oab, oa, ob, g_na, g_sw)


def _na_backward(proj, d_o, tiles, batch, seq, rider=None):
    t = proj.shape[0]
    n_rows = seq // GRID_W
    n_pairs = NA_WIDTH // LANES
    win = NA_ROWS * GRID_W
    n_tiles = 2 * NA_ROWS - 2

    def body(q_ref, k_ref, v_ref, do_ref, tp_ref, dq_ref, dk_ref, dv_ref, dtp_ref, km, vm, dk_acc, dv_acc):
        @pl.when(pl.program_id(1) == 0)
        def _():
            dtp_ref[...] = jnp.zeros_like(dtp_ref)

        _na_prepare(k_ref, v_ref, km, vm)
        dk_acc[...] = jnp.zeros_like(dk_acc)
        dv_acc[...] = jnp.zeros_like(dv_acc)
        low = lax.broadcasted_iota(jnp.int32, (win, LANES), 1) < HEAD_DIM

        def scores(r):
            rs, off = _na_window(r, n_rows)
            rows = pl.ds(pl.multiple_of(r * GRID_W, GRID_W), GRID_W)
            wrows = pl.ds(pl.multiple_of(rs * GRID_W, GRID_W), win)
            q, do = q_ref[rows, :], do_ref[rows, :]
            k2 = _na_pair_window(km, wrows)
            s = _na_scores(q, k2, tp_ref, off)
            dp = _mm_nt(do, _na_pair_window(vm, wrows))
            return rows, wrows, off, q, do, k2, s, dp

        def finish(rows, wrows, off, q, do, k2, s, dp):
            p = _pair_softmax(s)
            parts = []
            for h in range(2):
                ph, dph = p[:, h * win:(h + 1) * win], dp[:, h * win:(h + 1) * win]
                dsh = ph * (dph - jnp.sum(ph * dph, axis=-1, keepdims=True))
                for w in range(NA_ROWS // 2):
                    dtp_ref[h, 2 * w - off + (NA_ROWS - 1)] += dsh[:, w * LANES:(w + 1) * LANES]
                parts.append(dsh)
            dsb = (jnp.concatenate(parts, axis=1) * QK_SCALE).astype(BF16)
            dq_ref[rows, :] = _mm(dsb, k2).astype(BF16)
            dk2 = _mm_tn(dsb, q)
            dv2 = _mm_tn(p.astype(BF16), do)
            dk_acc[wrows, :] += jnp.where(low, dk2[:win], dk2[win:])
            dv_acc[wrows, :] += jnp.where(low, dv2[:win], dv2[win:])

        def row_group(i, carry):
            for state in [scores(NA_GROUP * i + j) for j in range(NA_GROUP)]:
                finish(*state)
            return carry

        lax.fori_loop(0, n_rows // NA_GROUP, row_group, 0)
        dk_ref[...] = dk_acc[...].astype(BF16)
        dv_ref[...] = dv_acc[...].astype(BF16)

    blk = lambda off: pl.BlockSpec((seq, LANES), lambda p, b: (b, off + p))
    out = jax.ShapeDtypeStruct((t, NA_WIDTH), BF16)
    return _hosted(
        body, rider, name="na_backward", grid=(n_pairs, batch),
        out_shape=[out, out, out, jax.ShapeDtypeStruct(tiles.shape, F32)],
        in_specs=[blk(0), blk(n_pairs), blk(2 * n_pairs), blk(0),
                  pl.BlockSpec((2, n_tiles, GRID_W, LANES), lambda p, b: (p, 0, 0, 0))],
        out_specs=[blk(0), blk(0), blk(0), pl.BlockSpec((2, n_tiles, GRID_W, LANES), lambda p, b: (p, 0, 0, 0))],
        scratch_shapes=[pltpu.VMEM((2, seq, LANES), BF16), pltpu.VMEM((2, seq, LANES), BF16),
                        pltpu.VMEM((seq, LANES), F32), pltpu.VMEM((seq, LANES), F32)],
        compiler_params=_cparams(("arbitrary", "arbitrary")), args=[proj, proj, proj, d_o, tiles])


def _na_bias_grad(dtiles, expand):
    n = dtiles.shape[0]

    def body(t_ref, e_ref, o_ref):
        flat = jnp.concatenate([t_ref[:, qq, :] for qq in range(GRID_W)], axis=1)
        o_ref[...] = lax.dot_general(flat, e_ref[...], (((1,), (1,)), ((), ())),
                                     precision=lax.Precision.HIGHEST, preferred_element_type=F32)

    return pl.pallas_call(
        body, name="na_bias_grad",
        out_shape=jax.ShapeDtypeStruct((n, expand.shape[0]), F32),
        compiler_params=_cparams(vmem=VMEM_BIG),
    )(dtiles, expand)


def _sw_backward(proj, d_o, sink, batch, seq, rider=None):
    t = proj.shape[0]
    n_pairs = SW_WIDTH // LANES
    q_blk = 3 * NA_WIDTH // LANES
    k_blk = q_blk + n_pairs
    n_blocks = seq // SW_BLOCK
    pad = seq + 2 * SW_BLOCK

    def body(sink_ref, q_ref, k_ref, v_ref, do_ref, dq_ref, dk_ref, dv_ref, dsk_ref,
             k_lo, k_hi, v_lo, v_hi, dk_loc, dv_loc, dk_tot, dv_tot):
        hp = pl.program_id(1)
        g = hp // 2
        _sw_prepare(k_ref, g, k_lo, k_hi, seq)
        _sw_prepare(v_ref, g, v_lo, v_hi, seq)
        dk_loc[...] = jnp.zeros_like(dk_loc)
        dv_loc[...] = jnp.zeros_like(dv_loc)

        @pl.when(hp == 0)
        def _():
            dk_tot[...] = jnp.zeros_like(dk_tot)
            dv_tot[...] = jnp.zeros_like(dv_tot)

        band = 3 * SW_BLOCK
        low = lax.broadcasted_iota(jnp.int32, (band, LANES), 1) < HEAD_DIM

        sinks = (sink_ref[2 * hp], sink_ref[2 * hp + 1])

        def scores(n):
            rows = pl.ds(pl.multiple_of(n * SW_BLOCK, SW_BLOCK), SW_BLOCK)
            wrows = pl.ds(pl.multiple_of(n * SW_BLOCK, SW_BLOCK), band)
            qb, do = q_ref[rows, :], do_ref[rows, :]
            k2 = jnp.concatenate([k_lo[wrows, :], k_hi[wrows, :]], axis=0)
            v2 = jnp.concatenate([v_lo[wrows, :], v_hi[wrows, :]], axis=0)
            return n, rows, wrows, qb, do, k2, _mm_nt(qb, k2) * QK_SCALE, _mm_nt(do, v2)

        def finish(sink_acc, n, rows, wrows, qb, do, k2, s2, dp):
            p, ps = _sw_probs(s2, _sw_mask(n, seq), sinks)
            parts, new = [], []
            for i in range(2):
                ph, dph = p[:, i * band:(i + 1) * band], dp[:, i * band:(i + 1) * band]
                delta = jnp.sum(ph * dph, axis=-1, keepdims=True)
                parts.append(ph * (dph - delta))
                new.append(sink_acc[i] - ps[i] * delta)
            dsb = (jnp.concatenate(parts, axis=1) * QK_SCALE).astype(BF16)
            dq_ref[rows, :] = _mm(dsb, k2)
            dk2 = _mm_tn(dsb, qb)
            dv2 = _mm_tn(p.astype(BF16), do)
            dk_loc[wrows, :] += jnp.where(low, dk2[:band], dk2[band:])
            dv_loc[wrows, :] += jnp.where(low, dv2[:band], dv2[band:])
            return tuple(new)

        def block_group(i, carry):
            for state in [scores(SW_GROUP_BLOCKS * i + j) for j in range(SW_GROUP_BLOCKS)]:
                carry = finish(carry, *state)
            return carry

        zero = jnp.zeros((SW_BLOCK, 1), F32)
        s0, s1 = lax.fori_loop(0, n_blocks // SW_GROUP_BLOCKS, block_group, (zero, zero))
        row = lax.broadcasted_iota(jnp.int32, (SUBLANES, LANES), 0)
        dsk_ref[0, 0] = jnp.where(row == 0, jnp.sum(s0), jnp.where(row == 1, jnp.sum(s1), 0.0))

        lane_s = lax.broadcasted_iota(jnp.int32, (seq, LANES), 1)
        mine_g = (lane_s // HEAD_DIM) == g
        for loc, tot in ((dk_loc, dk_tot), (dv_loc, dv_tot)):
            part = loc[SW_BLOCK:SW_BLOCK + seq, :]
            tot[...] += jnp.where(mine_g, part + pltpu.roll(part, HEAD_DIM, 1), 0.0)

        @pl.when(hp == n_pairs - 1)
        def _():
            dk_ref[...] = dk_tot[...]
            dv_ref[...] = dv_tot[...].astype(BF16)

    return _hosted(
        body, rider, name="sw_backward", grid=(batch, n_pairs),
        out_shape=[jax.ShapeDtypeStruct((t, SW_WIDTH), F32), jax.ShapeDtypeStruct((t, LANES), F32),
                   jax.ShapeDtypeStruct((t, LANES), BF16), jax.ShapeDtypeStruct((batch, n_pairs, SUBLANES, LANES), F32)],
        in_specs=[pl.BlockSpec(memory_space=pltpu.SMEM),
                  pl.BlockSpec((seq, LANES), lambda b, p: (b, q_blk + p)),
                  pl.BlockSpec((seq, LANES), lambda b, p: (b, k_blk)),
                  pl.BlockSpec((seq, LANES), lambda b, p: (b, k_blk + 1)),
                  pl.BlockSpec((seq, LANES), lambda b, p: (b, p))],
        out_specs=[pl.BlockSpec((seq, LANES), lambda b, p: (b, p)), pl.BlockSpec((seq, LANES), lambda b, p: (b, 0)),
                   pl.BlockSpec((seq, LANES), lambda b, p: (b, 0)),
                   pl.BlockSpec((1, 1, SUBLANES, LANES), lambda b, p: (b, p, 0, 0))],
        scratch_shapes=[pltpu.VMEM((pad, LANES), BF16)] * 4 + [pltpu.VMEM((pad, LANES), F32)] * 2
        + [pltpu.VMEM((seq, LANES), F32)] * 2,
        compiler_params=_cparams(("arbitrary", "arbitrary")), args=[sink, proj, proj, proj, d_o])


def _in_backward(dqkv_a, dq_b, dk_b, dv_b, w_in_t, h1, x, mod3, g_attn, dx1, cos_t, sin_t, seq):
    t, d = x.shape
    tm = TOKEN_TILE
    per_seq = seq // tm
    batch = t // seq
    dqa, dka, dva = dqkv_a
    n_q = SW_WIDTH // LANES

    def body(dqa_ref, dka_ref, dva_ref, dqb_ref, dkb_ref, dvb_ref, w_ref, h_ref, x_ref, mod_ref, g_ref, dx1_ref,
             cos_ref, sin_ref, dx_ref, gw_ref, gwb_ref, dsh_ref, dsc_ref, dg_ref):
        i = pl.program_id(0)

        @pl.when(i == 0)
        def _():
            gw_ref[...] = jnp.zeros_like(gw_ref)
            dg_ref[...] = jnp.zeros_like(dg_ref)

        @pl.when(i % per_seq == 0)
        def _():
            dsh_ref[...] = jnp.zeros_like(dsh_ref)
            dsc_ref[...] = jnp.zeros_like(dsc_ref)

        dr = jnp.concatenate([dqb_ref[...], dkb_ref[...]], axis=1)
        cos = jnp.concatenate([cos_ref[...]] * (n_q + 1), axis=1)
        sin = jnp.concatenate([sin_ref[...]] * (n_q + 1), axis=1)
        dr = dr * cos + _rope_rot(dr * sin)
        dproj = jnp.concatenate([dqa_ref[...], dka_ref[...], dva_ref[...], dr.astype(BF16), dvb_ref[...]], axis=1)
        gw_ref[...] += _mm_tn(dproj, h_ref[...])

        @pl.when(i == t // tm - 1)
        def _():
            gwb_ref[...] = gw_ref[...].astype(BF16)

        dh = _mm(dproj, w_ref[...])
        scale = mod_ref[0, :, d:2 * d]
        r, xn = _rms_stats(x_ref[...])
        xg = xn * g_ref[...]
        dxg = dh * (1.0 + scale)
        dx_ref[...] = dx1_ref[...] + _rms_bwd(dxg * g_ref[...], xn, r)
        dg_ref[...] += jnp.sum(dxg * xn, axis=0, keepdims=True)
        dsh_ref[0] += jnp.sum(dh, axis=0, keepdims=True)
        dsc_ref[0] += jnp.sum(dh * xg, axis=0, keepdims=True)

    tile = lambda w: pl.BlockSpec((tm, w), lambda i: (i, 0))
    per_b = pl.BlockSpec((1, 1, d), lambda i: (i // per_seq, 0, 0))
    small = jax.ShapeDtypeStruct((batch, 1, d), F32)
    rope = pl.BlockSpec((tm, LANES), lambda i: (i % per_seq, 0))
    return pl.pallas_call(
        body, name="in_backward", grid=(t // tm,),
        out_shape=(jax.ShapeDtypeStruct((t, d), F32), jax.ShapeDtypeStruct((IN_WIDTH, d), F32),
                   jax.ShapeDtypeStruct((IN_WIDTH, d), BF16), small, small, jax.ShapeDtypeStruct((1, d), F32)),
        in_specs=[tile(NA_WIDTH), tile(NA_WIDTH), tile(NA_WIDTH), tile(SW_WIDTH), tile(LANES), tile(LANES),
                  _resident((IN_WIDTH, d)), tile(d), tile(d),
                  pl.BlockSpec((1, 1, 6 * d), lambda i: (i // per_seq, 0, 0)),
                  pl.BlockSpec((1, d), lambda i: (0, 0)), tile(d), rope, rope],
        out_specs=(tile(d), _resident((IN_WIDTH, d)), _resident((IN_WIDTH, d)),
                   per_b, per_b, pl.BlockSpec((1, d), lambda i: (0, 0))),
        compiler_params=_cparams(("arbitrary",), VMEM_BIG),
    )(dqa, dka, dva, dq_b, dk_b, dv_b, w_in_t, h1, x, mod3, g_attn, dx1, cos_t, sin_t)


def _ada_weight_grad(sc_all, dmod_cols):
    d = sc_all.shape[1]
    ncol = dmod_cols.shape[1]

    def body(s_ref, m_ref, o_ref):
        o_ref[...] = _mm_tn(s_ref[...].astype(BF16), m_ref[...].astype(BF16))

    return pl.pallas_call(
        body, name="ada_weight_grad",
        out_shape=jax.ShapeDtypeStruct((d, ncol), F32),
        compiler_params=_cparams(vmem=VMEM_BIG),
    )(sc_all, dmod_cols)


def _row_tile(rows, cols):
    target = max(SUBLANES, (1 << 20) // (4 * cols))
    best = rows
    for cand in range(SUBLANES, rows + 1, SUBLANES):
        if rows % cand == 0 and cand <= target:
            best = cand
    return best if rows % SUBLANES == 0 else rows


def _sum_slots(parts, name):
    n = len(parts)
    _, rows, cols = parts[0][0].shape
    tr = _row_tile(rows, cols)
    per = rows // tr

    def body(*refs):
        o_ref = refs[-1]
        for q in range(n):
            @pl.when(pl.program_id(0) == q)
            def _(q=q):
                p_ref, own_ref = refs[2 * q], refs[2 * q + 1]
                o_ref[...] = ((own_ref[...] + p_ref[0].astype(F32)) + p_ref[1].astype(F32)) + p_ref[2].astype(F32)

    in_specs, args = [], []
    for q, (recv, own) in enumerate(parts):
        in_specs.append(pl.BlockSpec((N_SHARD - 1, tr, cols), lambda p, i, q=q: (0, jnp.where(p == q, i, 0), 0)))
        in_specs.append(pl.BlockSpec((tr, cols), lambda p, i, q=q: (jnp.where(p == q, i, 0), 0)))
        args += [recv, own]
    return pl.pallas_call(
        body, name=name, grid=(n, per),
        out_shape=jax.ShapeDtypeStruct((n * rows, cols), F32),
        in_specs=in_specs, out_specs=pl.BlockSpec((tr, cols), lambda p, i: (p * per + i, 0)),
        compiler_params=_cparams(("arbitrary", "arbitrary")),
    )(*args)


def _adamw_math(w, g, m, v):
    m2 = ADAM_B1 * m + (1.0 - ADAM_B1) * g
    v2 = ADAM_B2 * v + (1.0 - ADAM_B2) * (g * g)
    m_hat = m2 / (1.0 - ADAM_B1 ** ADAM_STEP)
    v_hat = v2 / (1.0 - ADAM_B2 ** ADAM_STEP)
    return -ADAM_LR * (m_hat / (jnp.sqrt(v_hat) + ADAM_EPS) + ADAM_WD * w), m2, v2


def _small_step(partials, states, dmod, b_ada_state, rider=None):
    n_upd = len(states)
    moving = list(partials) + [dmod]
    n_mov = len(moving)
    all_states = list(states) + [b_ada_state]

    def body(*refs):
        mov, refs = refs[:n_mov], refs[n_mov:]
        wmv, refs = refs[:3 * (n_upd + 1)], refs[3 * (n_upd + 1):]
        res, refs = refs[:4 * (n_upd + 1)], refs[4 * (n_upd + 1):]
        sums_out, refs = refs[:n_mov - n_upd - 1], refs[n_mov - n_upd - 1:]
        dmod_out, refs = refs[0], refs[1:]
        everyone, (ssem, rsem) = refs[:n_mov], refs[n_mov:]
        x, y, c = _my_pos()
        me = 4 * x + 2 * y + c
        cps = []
        for a in range(n_mov):
            everyone[a][me] = mov[a][...]
            for k in range(1, N_DEV):
                peer = (_flip(x, (k >> 2) & 1), _flip(y, (k >> 1) & 1), _flip(c, k & 1))
                cps.append(pltpu.make_async_remote_copy(
                    src_ref=everyone[a].at[me], dst_ref=everyone[a].at[me], send_sem=ssem.at[a, k - 1],
                    recv_sem=rsem.at[a, k - 1], device_id=peer, device_id_type=MESH))
        for cp in cps:
            cp.start()
        for cp in cps:
            cp.wait_recv()

        def total(a):
            acc = everyone[a][0]
            for dev in range(1, N_DEV):
                acc = acc + everyone[a][dev]
            return acc

        grads = [total(a) for a in range(n_upd)]
        grads.append(jnp.sum(total(n_mov - 1), axis=0, keepdims=True))
        for j, g in enumerate(grads):
            delta, m2, v2 = _adamw_math(wmv[3 * j][...], g, wmv[3 * j + 1][...], wmv[3 * j + 2][...])
            res[4 * j][...] = g
            res[4 * j + 1][...] = delta
            res[4 * j + 2][...] = m2
            res[4 * j + 3][...] = v2
        for j in range(n_mov - n_upd - 1):
            sums_out[j][...] = total(n_upd + j)
        dmod_out[...] = everyone[n_mov - 1][...]
        for cp in cps:
            cp.wait_send()

    vm = pl.BlockSpec(memory_space=pltpu.VMEM)
    sds = jax.ShapeDtypeStruct
    out_shape = []
    for w, _, _ in all_states:
        out_shape += [sds(w.shape, F32)] * 4
    out_shape += [sds(p.shape, F32) for p in partials[n_upd:]]
    out_shape.append(sds((N_DEV,) + dmod.shape, F32))
    args = moving + [a for st in all_states for a in st]
    outs, rides = _hosted(
        body, rider, name="small_step", grid=(), out_shape=out_shape,
        in_specs=[vm] * len(args), out_specs=[vm] * len(out_shape),
        scratch_shapes=[pltpu.VMEM((N_DEV,) + a.shape, F32) for a in moving]
        + [pltpu.SemaphoreType.DMA((n_mov, N_DEV - 1)), pltpu.SemaphoreType.DMA((n_mov, N_DEV - 1))],
        compiler_params=_cparams(vmem=VMEM_BIG), args=args)
    return outs, rides


def _adamw(w, grads, m, v, name):
    rows, cols = w.shape
    tr = _row_tile(rows, cols)
    ng = len(grads)

    def body(*refs):
        w_ref = refs[0]
        g_refs = refs[1:1 + ng]
        m_ref, v_ref = refs[1 + ng], refs[2 + ng]
        g_out, d_out, m_out, v_out = refs[3 + ng:]
        g = g_refs[0][...]
        for extra in g_refs[1:]:
            g = g + extra[...]
        g_out[...] = g
        d_out[...], m_out[...], v_out[...] = _adamw_math(w_ref[...], g, m_ref[...], v_ref[...])

    spec = pl.BlockSpec((tr, cols), lambda i: (i, 0))
    out = jax.ShapeDtypeStruct((rows, cols), F32)
    return pl.pallas_call(
        body, name=name, grid=(rows // tr,),
        out_shape=(out, out, out, out),
        in_specs=[spec] * (3 + ng), out_specs=(spec, spec, spec, spec),
        compiler_params=_cparams(("arbitrary",)),
    )(w, *grads, m, v)


def _pack_rows(arrays):
    tile = SUBLANES * LANES
    rows, offsets, at = [], [], 0
    for a in arrays:
        flat = a.reshape(-1).astype(F32)
        n = -(-flat.shape[0] // tile) * tile
        rows.append(jnp.pad(flat, (0, n - flat.shape[0])).reshape(-1, LANES))
        offsets.append(at)
        at += n // LANES
    return jnp.concatenate(rows, axis=0), offsets


def _unpack_rows(packed, offsets, shapes):
    out = []
    for off, shape in zip(offsets, shapes):
        n = 1
        for s in shape:
            n *= s
        nrow = -(-n // LANES)
        out.append(packed[off:off + nrow].reshape(-1)[:n].reshape(shape))
    return out


def _rope_tables(seq):
    half = HEAD_DIM // 2
    inv = np.float32(ROPE_THETA) ** (-np.arange(half, dtype=np.float32) / np.float32(half))
    ang = (np.arange(seq, dtype=np.float32)[:, None] * inv[None, :]).astype(np.float64)
    cos, sin = np.cos(ang).astype(np.float32), np.sin(ang).astype(np.float32)
    cos_t = np.concatenate([cos, cos, cos, cos], axis=1)
    sin_t = np.concatenate([-sin, sin, -sin, sin], axis=1)
    return jnp.asarray(cos_t), jnp.asarray(sin_t)


def kernel(x, c, w_ada, b_ada, g_attn, w_in, na_rpb, sw_sink, g_na_out, g_sw_out, w_out, g_ffn, w_up, conv_w, conv_b, w_down, g_final, loss_target, m_w_ada, m_b_ada, m_g_attn, m_w_in, m_na_rpb, m_sw_sink, m_g_na_out, m_g_sw_out, m_w_out, m_g_ffn, m_w_up, m_conv_w, m_conv_b, m_w_down, m_g_final, v_w_ada, v_b_ada, v_g_attn, v_w_in, v_na_rpb, v_sw_sink, v_g_na_out, v_g_sw_out, v_w_out, v_g_ffn, v_w_up, v_conv_w, v_conv_b, v_w_down, v_g_final):
    batch, seq, d = x.shape
    t = batch * seq
    assert d == D_MODEL and seq % (NA_ROWS * GRID_W) == 0 and seq % TOKEN_TILE == 0 and batch <= SUBLANES
    shard = 2 * lax.axis_index("x") + lax.axis_index("y")
    xt = x.reshape(t, d)
    tgt = loss_target.reshape(t, d)

    c8 = jnp.pad(c, ((0, SUBLANES - batch), (0, 0)))
    w_in_t_s = jnp.transpose(w_in[0]).astype(BF16)
    (mod8, sc_all), (w_in_g,) = _ada_forward(c8, w_ada[0], b_ada, _Rider("gather", [w_in_t_s]))
    mod3 = mod8[:batch].reshape(batch, 1, 6 * d)
    w_in_t = w_in_g.reshape(IN_WIDTH, d)

    cos_t, sin_t = _rope_tables(seq)
    (h1, proj), (w_out_g,) = _in_proj(xt, mod3, g_attn, w_in_t, cos_t, sin_t, seq,
                                      _Rider("gather", [w_out[0].astype(BF16)]))
    n_heads = NA_WIDTH // HEAD_DIM
    n_tiles, n_dc = 2 * NA_ROWS - 2, 2 * NA_COLS - 1
    expand, neg_mask = _na_bias_pattern()
    rpb = na_rpb[0]
    rows2 = jnp.concatenate([rpb[:, :-1, :], rpb[:, 1:, :]], axis=2).reshape(n_heads * n_tiles, 2 * n_dc)
    rows2 = jnp.pad(rows2, ((0, 0), (0, GRID_W - 2 * n_dc)))
    tiles = _na_bias_tiles(rows2, expand, neg_mask).reshape(n_heads, n_tiles, GRID_W, LANES)
    sink = sw_sink[0]
    w_up_b16 = w_up[0].astype(BF16)
    (oa,), (w_up_a,) = _na_forward(proj, tiles, batch, seq, _Rider("gather", [w_up_b16[:d // 2]]))
    (ob,), (w_up_b, conv_w_g) = _sw_forward(proj, sink, batch, seq, _Rider("gather", [w_up_b16[d // 2:], conv_w[0]]))
    w_up_f = (w_up_a, w_up_b)
    w_out_f = w_out_g.reshape(d, d)
    conv_w_f = jnp.transpose(conv_w_g, (1, 0, 2)).reshape(3, D_FF)
    oab, mix, x1, h2 = _out_proj(oa, ob, g_na_out, g_sw_out, w_out_f, xt, mod3, g_ffn, seq)
    (u,), (w_down_g,) = _up_proj(h2, w_up_f, _Rider("gather", [w_down[0].astype(BF16)]))
    w_down_f = w_down_g.reshape(D_FF, d)
    a = _conv_gate(u, conv_w_f, conv_b, batch, seq)
    dx2, dffn, loss_part, dgate_f, dg_final = _down_and_loss(a, w_down_f, x1, mod3, g_final.reshape(1, d), tgt, seq)

    gw_down, gw_down_b = _down_weight_grad(a, dffn)
    blocks = lambda g, rows: g.reshape(N_SHARD, rows // N_SHARD, d)
    (du, gconv_w, gconv_b), (recv_down, own_down) = _ffn_backward(
        dffn, w_down_f, u, conv_w_f, conv_b, batch, seq,
        _Rider("scatter", [blocks(gw_down_b, D_FF)], [blocks(gw_down, D_FF)]))
    (gw_up_top, gw_up_bot, gw_up_top_b, gw_up_bot_b), _ = _up_weight_grad(h2, du)
    (dx1, dmix, dshift_f, dscale_f, dgate_a, dg_ffn), (recv_up_top, own_up_top) = _up_backward(
        du, w_up_f, x1, mod3, g_ffn, dx2, mix, seq, _Rider("scatter", [gw_up_top_b], [gw_up_top]))
    doa, dob, gw_out, gw_out_b, dg_na, dg_sw = _out_backward(dmix, w_out_f, oab, oa, ob, g_na_out, g_sw_out)
    (dqa, dka, dva, dtiles), (recv_up_bot, own_up_bot) = _na_backward(
        proj, doa, tiles, batch, seq, _Rider("scatter", [gw_up_bot_b], [gw_up_bot]))
    (dq_b, dk_b, dv_b, dsink_parts), (recv_out, own_out) = _sw_backward(
        proj, dob, sink, batch, seq, _Rider("scatter", [blocks(gw_out_b, d)], [blocks(gw_out, d)]))
    gx, gw_in_t, gw_in_b, dshift_a, dscale_a, dg_attn = _in_backward(
        (dqa, dka, dva), dq_b, dk_b, dv_b, w_in_t, h1, xt, mod3, g_attn, dx1, cos_t, sin_t, seq)

    red = _na_bias_grad(dtiles.reshape(n_heads * n_tiles, GRID_W, LANES), expand)[:, :2 * n_dc]
    red = red.reshape(n_heads, n_tiles, 2, n_dc)
    zero_row = jnp.zeros((n_heads, 1, n_dc), F32)
    g_rpb = (jnp.concatenate([red[:, :, 0, :], zero_row], axis=1)
             + jnp.concatenate([zero_row, red[:, :, 1, :]], axis=1))
    g_sink = jnp.sum(dsink_parts[:, :, :2, 0], axis=0).reshape(SW_WIDTH // HEAD_DIM)

    dmod = jnp.concatenate([dshift_a, dscale_a, dgate_a, dshift_f, dscale_f, dgate_f], axis=2).reshape(batch, 6 * d)
    rpb_shape = na_rpb.shape[1:]
    states = [(g_attn, m_g_attn, v_g_attn),
              (na_rpb.reshape(rpb_shape), m_na_rpb.reshape(rpb_shape), v_na_rpb.reshape(rpb_shape)),
              (sw_sink, m_sw_sink, v_sw_sink), (g_na_out, m_g_na_out, v_g_na_out), (g_sw_out, m_g_sw_out, v_g_sw_out),
              (g_ffn, m_g_ffn, v_g_ffn), (conv_b, m_conv_b, v_conv_b),
              (g_final.reshape(1, d), m_g_final.reshape(1, d), v_g_final.reshape(1, d))]
    partials = [dg_attn, g_rpb, g_sink.reshape(sw_sink.shape), dg_na, dg_sw, dg_ffn, gconv_b, dg_final,
                gconv_w, loss_part]
    small, (recv_in, own_in) = _small_step(
        partials, states, dmod, (b_ada, m_b_ada, v_b_ada),
        _Rider("scatter", [blocks(gw_in_b, IN_WIDTH)], [blocks(gw_in_t, IN_WIDTH)]))
    r_small = [small[4 * j:4 * j + 4] for j in range(len(states) + 1)]
    g_conv_w_full, loss_sum, dmod_all = small[4 * (len(states) + 1):]
    loss = loss_sum[0, 0]
    mine = [_sum_slots([(recv_in, own_in)], "sum_w_in"), _sum_slots([(recv_out, own_out)], "sum_w_out"),
            _sum_slots([(recv_up_top, own_up_top), (recv_up_bot, own_up_bot)], "sum_w_up"),
            _sum_slots([(recv_down, own_down)], "sum_w_down")]
    theirs = _ride_alone(_Rider("swap", mine), "swap_sibling")
    dmod_rows = jnp.pad(dmod_all, ((0, 0), (0, SUBLANES - batch), (0, 0))).reshape(N_DEV * SUBLANES, 6 * d)
    ncol = w_ada.shape[2]
    g_w_ada = _ada_weight_grad(sc_all, lax.dynamic_slice(dmod_rows, (0, shard * ncol), (N_DEV * SUBLANES, ncol)))
    cshard = conv_w.shape[2]
    g_conv_w = lax.dynamic_slice(g_conv_w_full, (0, shard * cshard), (3, cshard))

    def big(w, m, v, g_parts, name):
        shape = w.shape
        outs = _adamw(w[0], g_parts, m[0], v[0], name)
        return [o.reshape(shape) for o in outs]

    r_w_ada = big(w_ada, m_w_ada, v_w_ada, [g_w_ada], "adamw_w_ada")
    r_w_in = [jnp.transpose(o).reshape(w_in.shape) for o in
              _adamw(jnp.transpose(w_in[0]), [mine[0], theirs[0]], jnp.transpose(m_w_in[0]), jnp.transpose(v_w_in[0]),
                     "adamw_w_in")]
    r_w_out = big(w_out, m_w_out, v_w_out, [mine[1], theirs[1]], "adamw_w_out")
    r_w_up = big(w_up, m_w_up, v_w_up, [mine[2], theirs[2]], "adamw_w_up")
    r_w_down = big(w_down, m_w_down, v_w_down, [mine[3], theirs[3]], "adamw_w_down")

    r_conv_w = big(conv_w, m_conv_w, v_conv_w, [g_conv_w], "adamw_conv_w")

    def pick(k):
        ga_, rpb_, sk_, gna_, gsw_, gf_, cb_, gfin_, b_ = [r[k] for r in r_small]
        return [r_w_ada[k], b_, ga_, r_w_in[k], rpb_.reshape(na_rpb.shape), sk_, gna_, gsw_, r_w_out[k], gf_,
                r_w_up[k], r_conv_w[k], cb_, r_w_down[k], gfin_.reshape(d)]

    return (loss, gx.reshape(batch, seq, d), *pick(0), *pick(1), *pick(2), *pick(3))
```

```python
import functools

import jax
import jax.numpy as jnp
import numpy as np
from jax import lax
from jax.experimental import pallas as pl
from jax.experimental.pallas import tpu as pltpu

F32 = jnp.float32
BF16 = jnp.bfloat16
MESH = pl.DeviceIdType.MESH

D_MODEL = 1024
HEAD_DIM = 64
NA_WIDTH = 512
SW_WIDTH = 512
SW_KV_WIDTH = 128
IN_WIDTH = 2304
D_FF = 2816
GRID_W = 64
NA_ROWS = 8
NA_COLS = 16
SW_BLOCK = 128
ROPE_THETA = 10000.0
EPS = 1e-6
NEG = -1e30
QK_SCALE = HEAD_DIM ** -0.5

ADAM_LR = 0.001
ADAM_B1 = 0.9
ADAM_B2 = 0.999
ADAM_EPS = 1e-08
ADAM_WD = 0.01
ADAM_STEP = 10

N_SHARD = 4
N_DEV = 8
LANES = 128
SUBLANES = 8
TOKEN_TILE = 512
FF_TILE = 256
CONV_CHUNK = 64
NA_GROUP = 4
SW_GROUP_BLOCKS = 4
VMEM_BIG = 56 * 1024 * 1024


def _mm(a, b):
    return jnp.dot(a, b, preferred_element_type=F32)


def _mm_nt(a, b):
    return lax.dot_general(a, b, (((1,), (1,)), ((), ())), preferred_element_type=F32)


def _mm_tn(a, b):
    return lax.dot_general(a, b, (((0,), (0,)), ((), ())), preferred_element_type=F32)


def _cparams(sem=None, vmem=None):
    kw = {}
    if sem is not None:
        kw["dimension_semantics"] = sem
    if vmem is not None:
        kw["vmem_limit_bytes"] = vmem
    return pltpu.CompilerParams(**kw)


def _resident(shape):
    return pl.BlockSpec(shape, lambda i: (0,) * len(shape), pipeline_mode=pl.Buffered(1))


def _sigmoid(x):
    return 1.0 / (1.0 + jnp.exp(-x))


def _rms_stats(x):
    r = lax.rsqrt(jnp.mean(x * x, axis=-1, keepdims=True) + EPS)
    return r, x * r


def _rms_bwd(dxn, xn, r):
    return r * (dxn - xn * jnp.mean(dxn * xn, axis=-1, keepdims=True))


def _my_pos():
    return lax.axis_index("x"), lax.axis_index("y"), lax.axis_index("c")


def _flip(v, bit):
    return 1 - v if bit else v


def _ada_forward(c8, w_ada, b_ada, rider):
    d = c8.shape[1]
    ncol = w_ada.shape[1]

    def body(c_ref, w_ref, b_ref, mod_ref, sc_ref, m_scr, mod_buf, ssem, rsem, ssem2, rsem2):
        x, y, c = _my_pos()
        me = 4 * x + 2 * y + c
        shard = 2 * x + y
        cv = c_ref[...]
        my_rows = pl.ds(pl.multiple_of(me * SUBLANES, SUBLANES), SUBLANES)
        sc_ref[my_rows, :] = cv * _sigmoid(cv)

        def copy1(k):
            peer = (_flip(x, (k >> 2) & 1), _flip(y, (k >> 1) & 1), _flip(c, k & 1))
            return pltpu.make_async_remote_copy(
                src_ref=sc_ref.at[my_rows, :], dst_ref=sc_ref.at[my_rows, :],
                send_sem=ssem.at[k - 1], recv_sem=rsem.at[k - 1], device_id=peer, device_id_type=MESH)

        sends = [copy1(k) for k in range(1, N_DEV)]
        for cp in sends:
            cp.start()
        for cp in sends:
            cp.wait_recv()
        m_scr[...] = _mm(sc_ref[...].astype(BF16), w_ref[...].astype(BF16))

        def copy2(k):
            px, py = _flip(x, (k >> 1) & 1), _flip(y, k & 1)
            rows = pl.ds(pl.multiple_of((4 * px + 2 * py + c) * SUBLANES, SUBLANES), SUBLANES)
            return pltpu.make_async_remote_copy(
                src_ref=m_scr.at[rows, :], dst_ref=mod_buf.at[shard],
                send_sem=ssem2.at[k - 1], recv_sem=rsem2.at[k - 1], device_id=(px, py, c), device_id_type=MESH)

        sends2 = [copy2(k) for k in range(1, N_SHARD)]
        for cp in sends2:
            cp.start()
        mod_buf[shard] = m_scr[my_rows, :]
        for cp in sends2:
            cp.wait_recv()
        for s in range(N_SHARD):
            mod_ref[:, s * ncol:(s + 1) * ncol] = mod_buf[s] + b_ref[:, s * ncol:(s + 1) * ncol]
        for cp in sends + sends2:
            cp.wait_send()

    vm = pl.BlockSpec(memory_space=pltpu.VMEM)
    return _hosted(
        body, rider, name="ada_forward", grid=(),
        out_shape=(jax.ShapeDtypeStruct((SUBLANES, N_SHARD * ncol), F32),
                   jax.ShapeDtypeStruct((N_DEV * SUBLANES, d), F32)),
        in_specs=[vm, vm, vm], out_specs=(vm, vm),
        scratch_shapes=[pltpu.VMEM((N_DEV * SUBLANES, ncol), F32), pltpu.VMEM((N_SHARD, SUBLANES, ncol), F32),
                        pltpu.SemaphoreType.DMA((N_DEV - 1,)), pltpu.SemaphoreType.DMA((N_DEV - 1,)),
                        pltpu.SemaphoreType.DMA((N_SHARD - 1,)), pltpu.SemaphoreType.DMA((N_SHARD - 1,))],
        compiler_params=_cparams(vmem=VMEM_BIG), args=[c8, w_ada, b_ada])


class _Rider:
    def __init__(self, kind, srcs, owns=()):
        self.kind, self.srcs, self.owns = kind, list(srcs), list(owns)
        n = len(self.srcs)
        sds = jax.ShapeDtypeStruct
        dma = pltpu.SemaphoreType.DMA
        if kind == "gather":
            self.out_shapes = [sds((N_SHARD,) + s.shape, s.dtype) for s in self.srcs]
            self.sems = [dma((n, N_SHARD - 1)), dma((n, N_SHARD - 1)), dma((n, N_SHARD - 1)), dma((n, N_SHARD - 1)),
                         dma((n,)), dma((n,))]
        elif kind == "scatter":
            self.out_shapes = ([sds((N_SHARD - 1,) + s.shape[1:], s.dtype) for s in self.srcs]
                               + [sds(o.shape[1:], o.dtype) for o in self.owns])
            m = max(len(self.owns), 1)
            self.sems = [dma((n, N_SHARD - 1)), dma((n, N_SHARD - 1)), dma((m,)), dma((m,))]
        else:
            self.out_shapes = [sds(s.shape, s.dtype) for s in self.srcs]
            self.sems = [dma((n,)), dma((n,))]

    @property
    def inputs(self):
        return self.srcs + self.owns

    def _halved(self, i):
        a = self.srcs[i]
        tile_rows = SUBLANES * (4 // jnp.dtype(a.dtype).itemsize)
        return self.kind == "gather" and a.shape[0] % (2 * tile_rows) == 0

    def copies(self, ins, outs, sems):
        n = len(self.srcs)
        x, y, c = _my_pos()
        shard = 2 * x + y
        remote, relay = [], []
        if self.kind == "swap":
            ssem, rsem = sems
            for i in range(n):
                remote.append(pltpu.make_async_remote_copy(
                    src_ref=ins[i], dst_ref=outs[i], send_sem=ssem.at[i], recv_sem=rsem.at[i],
                    device_id=(x, y, 1 - c), device_id_type=MESH))
            return remote, relay
        if self.kind == "gather":
            ssem, rsem, ssem2, rsem2, sib_s, sib_r = sems
        else:
            ssem, rsem, sib_s, sib_r = sems
        for i in range(n):
            if self.kind == "gather":
                remote.append(pltpu.make_async_remote_copy(
                    src_ref=ins[i], dst_ref=outs[i].at[shard], send_sem=sib_s.at[i], recv_sem=sib_r.at[i],
                    device_id=(x, y, 1 - c), device_id_type=MESH))
                half = ins[i].shape[0] // 2
                mine = pl.ds(pl.multiple_of(c * half, half), half) if self._halved(i) else None
            for k in range(1, N_SHARD):
                px, py = _flip(x, (k >> 1) & 1), _flip(y, k & 1)
                if self.kind == "gather":
                    src, dst = ins[i], outs[i].at[shard]
                    if mine is not None:
                        src, dst = src.at[mine], dst.at[mine]
                        got = outs[i].at[2 * px + py].at[mine]
                        relay.append(pltpu.make_async_remote_copy(
                            src_ref=got, dst_ref=got, send_sem=ssem2.at[i, k - 1], recv_sem=rsem2.at[i, k - 1],
                            device_id=(x, y, 1 - c), device_id_type=MESH))
                else:
                    src, dst = ins[i].at[2 * px + py], outs[i].at[k - 1]
                remote.append(pltpu.make_async_remote_copy(
                    src_ref=src, dst_ref=dst, send_sem=ssem.at[i, k - 1], recv_sem=rsem.at[i, k - 1],
                    device_id=(px, py, c), device_id_type=MESH))
        if self.kind == "scatter":
            for i in range(len(self.owns)):
                remote.append(pltpu.make_async_remote_copy(
                    src_ref=ins[n + i].at[shard], dst_ref=outs[n + i], send_sem=sib_s.at[i], recv_sem=sib_r.at[i],
                    device_id=(x, y, 1 - c), device_id_type=MESH))
        return remote, relay

    def start(self, ins, outs, sems):
        remote, _ = self.copies(ins, outs, sems)
        for cp in remote:
            cp.start()

    def wait(self, ins, outs, sems):
        remote, relay = self.copies(ins, outs, sems)
        for cp in remote:
            cp.wait_recv()
        for cp in relay:
            cp.start()
        for cp in relay:
            cp.wait_recv()
        for cp in remote + relay:
            cp.wait_send()


def _hosted(body, rider, *, name, grid, out_shape, in_specs, out_specs, scratch_shapes, compiler_params, args):
    out_shape, out_specs = list(out_shape), list(out_specs)
    if rider is None:
        outs = pl.pallas_call(body, name=name, grid=grid, out_shape=tuple(out_shape), in_specs=list(in_specs),
                              out_specs=tuple(out_specs), scratch_shapes=list(scratch_shapes),
                              compiler_params=compiler_params)(*args)
        return list(outs), []
    n_in, n_out, n_scr = len(in_specs), len(out_shape), len(scratch_shapes)
    nr_in, nr_out = len(rider.inputs), len(rider.out_shapes)
    n_steps = 1
    for size in grid:
        n_steps *= size

    def full(*refs):
        ins, refs = refs[:n_in], refs[n_in:]
        r_in, refs = refs[:nr_in], refs[nr_in:]
        outs, refs = refs[:n_out], refs[n_out:]
        r_out, refs = refs[:nr_out], refs[nr_out:]
        scr, sems = refs[:n_scr], refs[n_scr:]
        if grid:
            step = 0
            for ax, size in enumerate(grid):
                step = step * size + pl.program_id(ax)
            pl.when(step == 0)(lambda: rider.start(r_in, r_out, sems))
            body(*ins, *outs, *scr)
            pl.when(step == n_steps - 1)(lambda: rider.wait(r_in, r_out, sems))
        else:
            rider.start(r_in, r_out, sems)
            body(*ins, *outs, *scr)
            rider.wait(r_in, r_out, sems)

    hbm = pl.BlockSpec(memory_space=pl.ANY)
    res = pl.pallas_call(
        full, name=name, grid=grid, out_shape=tuple(out_shape + rider.out_shapes),
        in_specs=list(in_specs) + [hbm] * nr_in, out_specs=tuple(out_specs + [hbm] * nr_out),
        scratch_shapes=list(scratch_shapes) + rider.sems, compiler_params=compiler_params,
    )(*args, *rider.inputs)
    return list(res[:n_out]), list(res[n_out:])


def _ride_alone(rider, name):
    return _hosted(lambda: None, rider, name=name, grid=(), out_shape=[], in_specs=[], out_specs=[], scratch_shapes=[],
                   compiler_params=_cparams(), args=[])[1]


def _allreduce_small(packed, rider=None):
    r = packed.shape[0]

    def body(p_ref, sum_ref, all_ref, ssem, rsem):
        x, y, c = _my_pos()
        me = 4 * x + 2 * y + c
        all_ref[me] = p_ref[...]
        cps = []
        for k in range(1, N_DEV):
            peer = (_flip(x, (k >> 2) & 1), _flip(y, (k >> 1) & 1), _flip(c, k & 1))
            cps.append(pltpu.make_async_remote_copy(
                src_ref=all_ref.at[me], dst_ref=all_ref.at[me], send_sem=ssem.at[k - 1], recv_sem=rsem.at[k - 1],
                device_id=peer, device_id_type=MESH))
        for cp in cps:
            cp.start()
        for cp in cps:
            cp.wait_recv()
        acc = all_ref[0]
        for dev in range(1, N_DEV):
            acc = acc + all_ref[dev]
        sum_ref[...] = acc
        for cp in cps:
            cp.wait_send()

    vm = pl.BlockSpec(memory_space=pltpu.VMEM)
    return _hosted(
        body, rider, name="allreduce_small", grid=(),
        out_shape=[jax.ShapeDtypeStruct((r, LANES), F32), jax.ShapeDtypeStruct((N_DEV, r, LANES), F32)],
        in_specs=[vm], out_specs=[vm, vm],
        scratch_shapes=[pltpu.SemaphoreType.DMA((N_DEV - 1,)), pltpu.SemaphoreType.DMA((N_DEV - 1,))],
        compiler_params=_cparams(), args=[packed])


def _rope_rot(t):
    w = t.shape[1]
    lane = lax.broadcasted_iota(jnp.int32, t.shape, 1)
    first = (lane % HEAD_DIM) < (HEAD_DIM // 2)
    return jnp.where(first, pltpu.roll(t, w - HEAD_DIM // 2, 1), pltpu.roll(t, HEAD_DIM // 2, 1))


def _in_proj(x, mod3, g_attn, w_in_t, cos_t, sin_t, seq, rider=None):
    t, d = x.shape
    tm = TOKEN_TILE
    per_seq = seq // tm
    rope_lo, rope_hi = 3 * NA_WIDTH, 3 * NA_WIDTH + SW_WIDTH + SW_KV_WIDTH
    n_rep = (rope_hi - rope_lo) // LANES

    def body(x_ref, mod_ref, g_ref, w_ref, cos_ref, sin_ref, h_ref, p_ref):
        r, xn = _rms_stats(x_ref[...])
        shift, scale = mod_ref[0, :, 0:d], mod_ref[0, :, d:2 * d]
        hb = ((xn * g_ref[...]) * (1.0 + scale) + shift).astype(BF16)
        h_ref[...] = hb
        p_ref[:, :rope_lo] = _mm_nt(hb, w_ref[:rope_lo, :]).astype(BF16)
        pr = _mm_nt(hb, w_ref[rope_lo:rope_hi, :])
        cos = jnp.concatenate([cos_ref[...]] * n_rep, axis=1)
        sin = jnp.concatenate([sin_ref[...]] * n_rep, axis=1)
        p_ref[:, rope_lo:rope_hi] = (pr * cos + _rope_rot(pr) * sin).astype(BF16)
        p_ref[:, rope_hi:] = _mm_nt(hb, w_ref[rope_hi:, :]).astype(BF16)

    return _hosted(
        body, rider, name="in_proj", grid=(t // tm,),
        out_shape=[jax.ShapeDtypeStruct((t, d), BF16), jax.ShapeDtypeStruct((t, IN_WIDTH), BF16)],
        in_specs=[pl.BlockSpec((tm, d), lambda i: (i, 0)),
                  pl.BlockSpec((1, 1, 6 * d), lambda i: (i // per_seq, 0, 0)),
                  pl.BlockSpec((1, d), lambda i: (0, 0)),
                  pl.BlockSpec((IN_WIDTH, d), lambda i: (0, 0)),
                  pl.BlockSpec((tm, LANES), lambda i: (i % per_seq, 0)),
                  pl.BlockSpec((tm, LANES), lambda i: (i % per_seq, 0))],
        out_specs=[pl.BlockSpec((tm, d), lambda i: (i, 0)), pl.BlockSpec((tm, IN_WIDTH), lambda i: (i, 0))],
        scratch_shapes=[], compiler_params=_cparams(("arbitrary",), VMEM_BIG),
        args=[x, mod3, g_attn, w_in_t, cos_t, sin_t])


def _na_bias_pattern():
    n_dc = 2 * NA_COLS - 1
    j = np.arange(GRID_W)[:, None]
    m = np.arange(GRID_W * LANES)[None, :]
    q, lane = m // LANES, m % LANES
    k = lane % GRID_W
    cs = np.clip(q - NA_COLS // 2, 0, GRID_W - NA_COLS)
    ok = (k >= cs) & (k < cs + NA_COLS)
    hit = ok & (j < 2 * n_dc) & (lane // GRID_W == j // n_dc) & (k - q + (NA_COLS - 1) == j % n_dc)
    return jnp.asarray(hit.astype(np.float32)), jnp.asarray(np.where(ok, 0.0, NEG).astype(np.float32))


def _na_bias_tiles(rows2, expand, mask):
    n, width = rows2.shape[0], expand.shape[1]
    q_step = 16
    step = q_step * LANES

    def body(r_ref, e_ref, m_ref, o_ref):
        flat = jnp.dot(r_ref[...], e_ref[...], precision=lax.Precision.HIGHEST,
                       preferred_element_type=F32) + m_ref[...]
        for qq in range(q_step):
            o_ref[:, qq, :] = flat[:, qq * LANES:(qq + 1) * LANES]

    return pl.pallas_call(
        body, name="na_bias_tiles", grid=(width // step,),
        out_shape=jax.ShapeDtypeStruct((n, GRID_W, LANES), F32),
        in_specs=[pl.BlockSpec(rows2.shape, lambda i: (0, 0)), pl.BlockSpec((expand.shape[0], step), lambda i: (0, i)),
                  pl.BlockSpec((1, step), lambda i: (0, i))],
        out_specs=pl.BlockSpec((n, q_step, LANES), lambda i: (0, i, 0)),
        compiler_params=_cparams(("arbitrary",)),
    )(rows2, expand, mask)


def _na_prepare(k_ref, v_ref, km, vm):
    lane = lax.broadcasted_iota(jnp.int32, k_ref.shape, 1)
    low = lane < HEAD_DIM
    kv = k_ref[...]
    vv = v_ref[...]
    zero = jnp.zeros_like(kv)
    km[0] = jnp.where(low, kv, zero)
    km[1] = jnp.where(low, zero, kv)
    vm[0] = jnp.where(low, vv, zero)
    vm[1] = jnp.where(low, zero, vv)


def _na_window(r, n_rows):
    rs = jnp.clip(r - NA_ROWS // 2, 0, n_rows - NA_ROWS)
    return rs, r - rs


def _na_pair_window(ref, wrows):
    return jnp.concatenate([ref[0, wrows, :], ref[1, wrows, :]], axis=0)


def _na_scores(q, k2, tp_ref, off):
    bias = jnp.concatenate([tp_ref[h, 2 * w - off + (NA_ROWS - 1)] for h in range(2) for w in range(NA_ROWS // 2)],
                           axis=1)
    return _mm_nt(q, k2) * QK_SCALE + bias


def _pair_lse_block(lse):
    lane = lax.broadcasted_iota(jnp.int32, (lse[0].shape[0], LANES), 1)
    return jnp.where(lane < HEAD_DIM, lse[0], lse[1])


def _pair_softmax(s):
    win = s.shape[1] // 2
    halves, lse = [], []
    for h in range(2):
        sh = s[:, h * win:(h + 1) * win]
        m = jnp.max(sh, axis=-1, keepdims=True)
        e = jnp.exp(sh - m)
        l = jnp.sum(e, axis=-1, keepdims=True)
        halves.append(e / l)
        lse.append(m + jnp.log(l))
    return jnp.concatenate(halves, axis=1), _pair_lse_block(lse)


def _pair_probs_from_lse(s, lse_block):
    win = s.shape[1] // 2
    return jnp.concatenate([jnp.exp(s[:, h * win:(h + 1) * win] - lse_block[:, h * HEAD_DIM:h * HEAD_DIM + 1])
                            for h in range(2)], axis=1)


def _na_forward(proj, tiles, batch, seq, rider=None):
    t = proj.shape[0]
    n_rows = seq // GRID_W
    n_pairs = NA_WIDTH // LANES
    win = NA_ROWS * GRID_W

    def body(q_ref, k_ref, v_ref, tp_ref, o_ref, lse_ref, km, vm):
        _na_prepare(k_ref, v_ref, km, vm)

        def scores(r):
            rs, off = _na_window(r, n_rows)
            rows = pl.ds(pl.multiple_of(r * GRID_W, GRID_W), GRID_W)
            wrows = pl.ds(pl.multiple_of(rs * GRID_W, GRID_W), win)
            return rows, wrows, _na_scores(q_ref[rows, :], _na_pair_window(km, wrows), tp_ref, off)

        def finish(rows, wrows, s):
            p, lse = _pair_softmax(s)
            lse_ref[rows, :] = lse
            o_ref[rows, :] = _mm(p.astype(BF16), _na_pair_window(vm, wrows))

        def row_group(i, carry):
            for state in [scores(NA_GROUP * i + j) for j in range(NA_GROUP)]:
                finish(*state)
            return carry

        lax.fori_loop(0, n_rows // NA_GROUP, row_group, 0)

    return _hosted(
        body, rider, name="na_forward", grid=(batch, n_pairs),
        out_shape=[jax.ShapeDtypeStruct((t, NA_WIDTH), F32), jax.ShapeDtypeStruct((t, NA_WIDTH), F32)],
        in_specs=[pl.BlockSpec((seq, LANES), lambda b, p: (b, p)),
                  pl.BlockSpec((seq, LANES), lambda b, p: (b, n_pairs + p)),
                  pl.BlockSpec((seq, LANES), lambda b, p: (b, 2 * n_pairs + p)),
                  pl.BlockSpec((2, 2 * NA_ROWS - 2, GRID_W, LANES), lambda b, p: (p, 0, 0, 0))],
        out_specs=[pl.BlockSpec((seq, LANES), lambda b, p: (b, p)), pl.BlockSpec((seq, LANES), lambda b, p: (b, p))],
        scratch_shapes=[pltpu.VMEM((2, seq, LANES), BF16), pltpu.VMEM((2, seq, LANES), BF16)],
        compiler_params=_cparams(("arbitrary", "arbitrary")), args=[proj, proj, proj, tiles])


def _sw_prepare(kv_ref, g, dst_lo, dst_hi, seq):
    lane = lax.broadcasted_iota(jnp.int32, kv_ref.shape, 1)
    mine = (lane // HEAD_DIM) == g
    kg = jnp.where(mine, kv_ref[...].astype(F32), 0.0)
    kr = pltpu.roll(kg, HEAD_DIM, 1)
    first = g == 0
    zero = jnp.zeros((SW_BLOCK, LANES), BF16)
    for dst, val in ((dst_lo, jnp.where(first, kg, kr)), (dst_hi, jnp.where(first, kr, kg))):
        dst[0:SW_BLOCK, :] = zero
        dst[SW_BLOCK:SW_BLOCK + seq, :] = val.astype(BF16)
        dst[SW_BLOCK + seq:, :] = zero


def _sw_mask(n, seq):
    qi = lax.broadcasted_iota(jnp.int32, (SW_BLOCK, 3 * SW_BLOCK), 0)
    kj = lax.broadcasted_iota(jnp.int32, (SW_BLOCK, 3 * SW_BLOCK), 1)
    kpos = n * SW_BLOCK - SW_BLOCK + kj
    return (jnp.abs(qi + SW_BLOCK - kj) <= SW_BLOCK) & (kpos >= 0) & (kpos < seq)


def _sw_probs(s2, ok, sinks):
    band = s2.shape[1] // 2
    halves, lse = [], []
    for i in range(2):
        s = jnp.where(ok, s2[:, i * band:(i + 1) * band], NEG)
        m = jnp.maximum(jnp.max(s, axis=-1, keepdims=True), sinks[i])
        p = jnp.exp(s - m)
        den = jnp.sum(p, axis=-1, keepdims=True) + jnp.exp(sinks[i] - m)
        halves.append(p / den)
        lse.append(m + jnp.log(den))
    return jnp.concatenate(halves, axis=1), _pair_lse_block(lse)


def _sw_probs_from_lse(s2, ok, sinks, lse_block):
    band = s2.shape[1] // 2
    halves, sink_p = [], []
    for i in range(2):
        lse = lse_block[:, i * HEAD_DIM:i * HEAD_DIM + 1]
        halves.append(jnp.exp(jnp.where(ok, s2[:, i * band:(i + 1) * band], NEG) - lse))
        sink_p.append(jnp.exp(sinks[i] - lse))
    return jnp.concatenate(halves, axis=1), sink_p


def _sw_forward(proj, sink, batch, seq, rider=None):
    t = proj.shape[0]
    n_pairs = SW_WIDTH // LANES
    q_blk = 3 * NA_WIDTH // LANES
    k_blk = q_blk + n_pairs
    n_blocks = seq // SW_BLOCK
    pad = seq + 2 * SW_BLOCK

    def body(sink_ref, q_ref, k_ref, v_ref, o_ref, lse_ref, k_lo, k_hi, v_lo, v_hi):
        hp = pl.program_id(1)
        g = hp // 2
        _sw_prepare(k_ref, g, k_lo, k_hi, seq)
        _sw_prepare(v_ref, g, v_lo, v_hi, seq)

        sinks = (sink_ref[2 * hp], sink_ref[2 * hp + 1])

        def scores(n):
            rows = pl.ds(pl.multiple_of(n * SW_BLOCK, SW_BLOCK), SW_BLOCK)
            wrows = pl.ds(pl.multiple_of(n * SW_BLOCK, SW_BLOCK), 3 * SW_BLOCK)
            k2 = jnp.concatenate([k_lo[wrows, :], k_hi[wrows, :]], axis=0)
            return n, rows, wrows, _mm_nt(q_ref[rows, :], k2) * QK_SCALE

        def finish(n, rows, wrows, s2):
            p, lse = _sw_probs(s2, _sw_mask(n, seq), sinks)
            lse_ref[rows, :] = lse
            v2 = jnp.concatenate([v_lo[wrows, :], v_hi[wrows, :]], axis=0)
            o_ref[rows, :] = _mm(p.astype(BF16), v2)

        def block_group(i, carry):
            for state in [scores(SW_GROUP_BLOCKS * i + j) for j in range(SW_GROUP_BLOCKS)]:
                finish(*state)
            return carry

        lax.fori_loop(0, n_blocks // SW_GROUP_BLOCKS, block_group, 0)

    return _hosted(
        body, rider, name="sw_forward", grid=(batch, n_pairs),
        out_shape=[jax.ShapeDtypeStruct((t, SW_WIDTH), F32), jax.ShapeDtypeStruct((t, SW_WIDTH), F32)],
        in_specs=[pl.BlockSpec(memory_space=pltpu.SMEM),
                  pl.BlockSpec((seq, LANES), lambda b, p: (b, q_blk + p)),
                  pl.BlockSpec((seq, LANES), lambda b, p: (b, k_blk)),
                  pl.BlockSpec((seq, LANES), lambda b, p: (b, k_blk + 1))],
        out_specs=[pl.BlockSpec((seq, LANES), lambda b, p: (b, p)), pl.BlockSpec((seq, LANES), lambda b, p: (b, p))],
        scratch_shapes=[pltpu.VMEM((pad, LANES), BF16)] * 4,
        compiler_params=_cparams(("arbitrary", "arbitrary")), args=[sink, proj, proj, proj])


def _out_proj(oa, ob, g_na, g_sw, w_out, x, mod3, g_ffn, seq):
    t, d = x.shape
    tm = TOKEN_TILE
    per_seq = seq // tm

    def body(oa_ref, ob_ref, gna_ref, gsw_ref, w_ref, x_ref, mod_ref, gf_ref, oab_ref, mix_ref, x1_ref, h2_ref):
        _, na = _rms_stats(oa_ref[...])
        _, nb = _rms_stats(ob_ref[...])
        oab = jnp.concatenate([na * gna_ref[...], nb * gsw_ref[...]], axis=1).astype(BF16)
        oab_ref[...] = oab
        mix = _mm(oab, w_ref[...])
        mix_ref[...] = mix
        gate_a = mod_ref[0, :, 2 * d:3 * d]
        shift_f, scale_f = mod_ref[0, :, 3 * d:4 * d], mod_ref[0, :, 4 * d:5 * d]
        x1 = x_ref[...] + gate_a * mix
        x1_ref[...] = x1
        _, xn = _rms_stats(x1)
        h2_ref[...] = ((xn * gf_ref[...]) * (1.0 + scale_f) + shift_f).astype(BF16)

    tile = lambda w: pl.BlockSpec((tm, w), lambda i: (i, 0))
    vec = lambda w: pl.BlockSpec((1, w), lambda i: (0, 0))
    return pl.pallas_call(
        body, name="out_proj", grid=(t // tm,),
        out_shape=(jax.ShapeDtypeStruct((t, d), BF16), jax.ShapeDtypeStruct((t, d), F32),
                   jax.ShapeDtypeStruct((t, d), F32), jax.ShapeDtypeStruct((t, d), BF16)),
        in_specs=[tile(NA_WIDTH), tile(SW_WIDTH), vec(NA_WIDTH), vec(SW_WIDTH),
                  pl.BlockSpec((d, d), lambda i: (0, 0)), tile(d),
                  pl.BlockSpec((1, 1, 6 * d), lambda i: (i // per_seq, 0, 0)), vec(d)],
        out_specs=(tile(d), tile(d), tile(d), tile(d)),
        compiler_params=_cparams(("arbitrary",), VMEM_BIG),
    )(oa, ob, g_na, g_sw, w_out, x, mod3, g_ffn)


def _up_proj(h2, w_up_halves, rider=None):
    t, d = h2.shape
    tm = TOKEN_TILE
    w_a, w_b = w_up_halves
    half, wcol = w_a.shape[1], w_a.shape[2]

    def body(h_ref, wa_ref, wb_ref, u_ref):
        u_ref[0] = (_mm(h_ref[:, :half], wa_ref[0]) + _mm(h_ref[:, half:], wb_ref[0])).astype(BF16)

    w_spec = pl.BlockSpec((1, half, wcol), lambda j, i: (j, 0, 0))
    return _hosted(
        body, rider, name="up_proj", grid=(N_SHARD, t // tm),
        out_shape=[jax.ShapeDtypeStruct((2, t, D_FF), BF16)],
        in_specs=[pl.BlockSpec((tm, d), lambda j, i: (i, 0)), w_spec, w_spec],
        out_specs=[pl.BlockSpec((1, tm, wcol), lambda j, i: (j // 2, i, j % 2))],
        scratch_shapes=[], compiler_params=_cparams(("arbitrary", "arbitrary"), VMEM_BIG), args=[h2, w_a, w_b])


def _taps_chunk(load, s, rows, seq):
    halo = 2 * SUBLANES
    cur = load(s, rows)
    above = load(pl.multiple_of(jnp.maximum(s - halo, 0), halo), halo)
    below = load(pl.multiple_of(jnp.minimum(s + rows, seq - halo), halo), halo)
    up = jnp.where(s > 0, above[halo - 1:halo, :], 0.0)
    dn = jnp.where(s + rows < seq, below[0:1, :], 0.0)
    row = lax.broadcasted_iota(jnp.int32, cur.shape, 0)
    prev = jnp.where(row == 0, up, pltpu.roll(cur, 1, 0))
    nxt = jnp.where(row == rows - 1, dn, pltpu.roll(cur, rows - 1, 0))
    return cur, prev, nxt


def _conv_gate(u, conv_w, conv_b, batch, seq):
    t = u.shape[1]
    cw = FF_TILE
    rows = CONV_CHUNK

    def body(u_ref, w_ref, b_ref, a_ref):
        def chunk(i, carry):
            s = pl.multiple_of(i * rows, rows)
            gt, prev, nxt = _taps_chunk(lambda at, n: u_ref[1, pl.ds(at, n), :].astype(F32), s, rows, seq)
            gc = prev * w_ref[0:1, :] + gt * w_ref[1:2, :] + nxt * w_ref[2:3, :] + b_ref[...]
            a_ref[pl.ds(s, rows), :] = ((gc * _sigmoid(gc)) * u_ref[0, pl.ds(s, rows), :].astype(F32)).astype(BF16)
            return carry

        lax.fori_loop(0, seq // rows, chunk, 0)

    return pl.pallas_call(
        body, name="conv_gate", grid=(batch, D_FF // cw),
        out_shape=jax.ShapeDtypeStruct((t, D_FF), BF16),
        in_specs=[pl.BlockSpec((2, seq, cw), lambda b, j: (0, b, j)),
                  pl.BlockSpec((3, cw), lambda b, j: (0, j)), pl.BlockSpec((1, cw), lambda b, j: (0, j))],
        out_specs=pl.BlockSpec((seq, cw), lambda b, j: (b, j)),
        compiler_params=_cparams(("arbitrary", "arbitrary"), VMEM_BIG),
    )(u, conv_w, conv_b)


def _down_and_loss(a, w_down, x1, mod3, g_final, target, seq):
    t, d = x1.shape
    tm = TOKEN_TILE
    per_seq = seq // tm
    batch = t // seq

    def body(a_ref, w_ref, x1_ref, mod_ref, g_ref, tgt_ref, dx2_ref, dffn_ref, loss_ref, dgate_ref, dg_ref):
        i = pl.program_id(0)
        f = _mm(a_ref[...], w_ref[...])
        gate_f = mod_ref[0, :, 5 * d:6 * d]
        x2 = x1_ref[...] + gate_f * f
        r, xn = _rms_stats(x2)
        err = xn * g_ref[...] - tgt_ref[...]
        part = 0.5 * jnp.sum(jnp.mean(err * err, axis=-1, keepdims=True))
        dy = err / d
        dx2 = _rms_bwd(dy * g_ref[...], xn, r)
        dx2_ref[...] = dx2
        dffn_ref[...] = (dx2 * gate_f).astype(BF16)

        @pl.when(i == 0)
        def _():
            loss_ref[...] = jnp.zeros_like(loss_ref)
            dg_ref[...] = jnp.zeros_like(dg_ref)

        @pl.when(i % per_seq == 0)
        def _():
            dgate_ref[...] = jnp.zeros_like(dgate_ref)

        loss_ref[...] += part
        dg_ref[...] += jnp.sum(dy * xn, axis=0, keepdims=True)
        dgate_ref[0] += jnp.sum(dx2 * f, axis=0, keepdims=True)

    tile = lambda w: pl.BlockSpec((tm, w), lambda i: (i, 0))
    return pl.pallas_call(
        body, name="down_loss", grid=(t // tm,),
        out_shape=(jax.ShapeDtypeStruct((t, d), F32), jax.ShapeDtypeStruct((t, d), BF16),
                   jax.ShapeDtypeStruct((SUBLANES, LANES), F32), jax.ShapeDtypeStruct((batch, 1, d), F32),
                   jax.ShapeDtypeStruct((1, d), F32)),
        in_specs=[tile(D_FF), _resident((D_FF, d)), tile(d),
                  pl.BlockSpec((1, 1, 6 * d), lambda i: (i // per_seq, 0, 0)),
                  pl.BlockSpec((1, d), lambda i: (0, 0)), tile(d)],
        out_specs=(tile(d), tile(d), pl.BlockSpec((SUBLANES, LANES), lambda i: (0, 0)),
                   pl.BlockSpec((1, 1, d), lambda i: (i // per_seq, 0, 0)), pl.BlockSpec((1, d), lambda i: (0, 0))),
        compiler_params=_cparams(("arbitrary",), VMEM_BIG),
    )(a, w_down, x1, mod3, g_final, target)


def _down_weight_grad(a, dffn):
    t, dff = a.shape
    d = dffn.shape[1]
    tk = TOKEN_TILE
    n_k = t // tk

    def body(a_ref, df_ref, g_ref, gb_ref):
        k = pl.program_id(0)

        @pl.when(k == 0)
        def _():
            g_ref[...] = jnp.zeros_like(g_ref)

        g_ref[...] += _mm_tn(a_ref[...], df_ref[...])

        @pl.when(k == n_k - 1)
        def _():
            gb_ref[...] = g_ref[...].astype(BF16)

    whole = pl.BlockSpec((dff, d), lambda k: (0, 0))
    return pl.pallas_call(
        body, name="down_weight_grad", grid=(n_k,),
        out_shape=(jax.ShapeDtypeStruct((dff, d), F32), jax.ShapeDtypeStruct((dff, d), BF16)),
        in_specs=[pl.BlockSpec((tk, dff), lambda k: (k, 0)), pl.BlockSpec((tk, d), lambda k: (k, 0))],
        out_specs=(whole, whole),
        compiler_params=_cparams(("arbitrary",), VMEM_BIG),
    )(a, dffn)


def _ffn_backward(dffn, w_down, u, conv_w, conv_b, batch, seq, rider=None):
    t, d = dffn.shape
    cw = FF_TILE
    rows = CONV_CHUNK

    def body(df_ref, wd_ref, u_ref, w_ref, b_ref, du_ref, gcw_ref, gcb_ref, da_scr, dgc_scr):
        b = pl.program_id(1)
        da_scr[...] = _mm_nt(df_ref[...], wd_ref[...])

        @pl.when(b == 0)
        def _():
            gcw_ref[...] = jnp.zeros_like(gcw_ref)
            gcb_ref[...] = jnp.zeros_like(gcb_ref)

        def fold(v):
            return jnp.sum(v.reshape(rows // SUBLANES, SUBLANES, cw), axis=0)

        def chunk(i, carry):
            s = pl.multiple_of(i * rows, rows)
            here = pl.ds(s, rows)
            gt, prev, nxt = _taps_chunk(lambda at, n: u_ref[1, pl.ds(at, n), :].astype(F32), s, rows, seq)
            val, da = u_ref[0, here, :].astype(F32), da_scr[here, :]
            gc = prev * w_ref[0:1, :] + gt * w_ref[1:2, :] + nxt * w_ref[2:3, :] + b_ref[...]
            sg = _sigmoid(gc)
            sl = gc * sg
            du_ref[0, here, :] = (da * sl).astype(BF16)
            dgc = (da * val) * (sg * (1.0 + gc * (1.0 - sg)))
            dgc_scr[here, :] = dgc
            cb, c0, c1, c2 = carry
            return cb + fold(dgc), c0 + fold(dgc * prev), c1 + fold(dgc * gt), c2 + fold(dgc * nxt)

        zero = jnp.zeros((SUBLANES, cw), F32)
        cb, c0, c1, c2 = lax.fori_loop(0, seq // rows, chunk, (zero, zero, zero, zero))
        gcb_ref[...] += jnp.sum(cb, axis=0, keepdims=True)
        gcw_ref[0:1, :] += jnp.sum(c0, axis=0, keepdims=True)
        gcw_ref[1:2, :] += jnp.sum(c1, axis=0, keepdims=True)
        gcw_ref[2:3, :] += jnp.sum(c2, axis=0, keepdims=True)

        def chunk2(i, carry):
            s = pl.multiple_of(i * rows, rows)
            dgc, dprev, dnxt = _taps_chunk(lambda at, n: dgc_scr[pl.ds(at, n), :], s, rows, seq)
            du_ref[1, pl.ds(s, rows), :] = (dnxt * w_ref[0:1, :] + dgc * w_ref[1:2, :]
                                            + dprev * w_ref[2:3, :]).astype(BF16)
            return carry

        lax.fori_loop(0, seq // rows, chunk2, 0)

    return _hosted(
        body, rider, name="ffn_backward", grid=(D_FF // cw, batch),
        out_shape=[jax.ShapeDtypeStruct((2, t, D_FF), BF16),
                   jax.ShapeDtypeStruct((3, D_FF), F32), jax.ShapeDtypeStruct((1, D_FF), F32)],
        in_specs=[pl.BlockSpec((seq, d), lambda j, b: (b, 0)), pl.BlockSpec((cw, d), lambda j, b: (j, 0)),
                  pl.BlockSpec((2, seq, cw), lambda j, b: (0, b, j)),
                  pl.BlockSpec((3, cw), lambda j, b: (0, j)), pl.BlockSpec((1, cw), lambda j, b: (0, j))],
        out_specs=[pl.BlockSpec((2, seq, cw), lambda j, b: (0, b, j)),
                   pl.BlockSpec((3, cw), lambda j, b: (0, j)), pl.BlockSpec((1, cw), lambda j, b: (0, j))],
        scratch_shapes=[pltpu.VMEM((seq, cw), F32), pltpu.VMEM((seq, cw), F32)],
        compiler_params=_cparams(("arbitrary", "arbitrary"), VMEM_BIG), args=[dffn, w_down, u, conv_w, conv_b])


def _up_backward(du, w_up, x1, mod3, g_ffn, dx2, mix, seq, rider=None):
    _, t, _ = du.shape
    d = x1.shape[1]
    tm = TOKEN_TILE
    per_seq = seq // tm
    batch = t // seq
    w_a, w_b = w_up
    half, wcol = w_a.shape[1], w_a.shape[2]

    def body(du_ref, wa_ref, wb_ref, x1_ref, mod_ref, g_ref, dx2_ref, mix_ref,
             dx1_ref, dmix_ref, dsh_ref, dsc_ref, dga_ref, dg_ref):
        i = pl.program_id(0)
        parts = []
        for w_ref in (wa_ref, wb_ref):
            acc = jnp.zeros((tm, half), F32)
            for j in range(N_SHARD):
                acc = acc + _mm_nt(du_ref[j // 2, :, (j % 2) * wcol:(j % 2 + 1) * wcol], w_ref[j])
            parts.append(acc)
        dh = jnp.concatenate(parts, axis=1)
        gate_a = mod_ref[0, :, 2 * d:3 * d]
        scale_f = mod_ref[0, :, 4 * d:5 * d]
        r, xn = _rms_stats(x1_ref[...])
        xg = xn * g_ref[...]
        dxg = dh * (1.0 + scale_f)
        dx1 = dx2_ref[...] + _rms_bwd(dxg * g_ref[...], xn, r)
        dx1_ref[...] = dx1
        dmix_ref[...] = (dx1 * gate_a).astype(BF16)

        @pl.when(i == 0)
        def _():
            dg_ref[...] = jnp.zeros_like(dg_ref)

        @pl.when(i % per_seq == 0)
        def _():
            dsh_ref[...] = jnp.zeros_like(dsh_ref)
            dsc_ref[...] = jnp.zeros_like(dsc_ref)
            dga_ref[...] = jnp.zeros_like(dga_ref)

        dg_ref[...] += jnp.sum(dxg * xn, axis=0, keepdims=True)
        dsh_ref[0] += jnp.sum(dh, axis=0, keepdims=True)
        dsc_ref[0] += jnp.sum(dh * xg, axis=0, keepdims=True)
        dga_ref[0] += jnp.sum(dx1 * mix_ref[...], axis=0, keepdims=True)

    tile = lambda w: pl.BlockSpec((tm, w), lambda i: (i, 0))
    per_b = pl.BlockSpec((1, 1, d), lambda i: (i // per_seq, 0, 0))
    small = jax.ShapeDtypeStruct((batch, 1, d), F32)
    return _hosted(
        body, rider, name="up_backward", grid=(t // tm,),
        out_shape=[jax.ShapeDtypeStruct((t, d), F32), jax.ShapeDtypeStruct((t, d), BF16), small, small, small,
                   jax.ShapeDtypeStruct((1, d), F32)],
        in_specs=[pl.BlockSpec((2, tm, D_FF), lambda i: (0, i, 0)),
                  _resident((N_SHARD, half, wcol)), _resident((N_SHARD, half, wcol)), tile(d),
                  pl.BlockSpec((1, 1, 6 * d), lambda i: (i // per_seq, 0, 0)),
                  pl.BlockSpec((1, d), lambda i: (0, 0)), tile(d), tile(d)],
        out_specs=[tile(d), tile(d), per_b, per_b, per_b, pl.BlockSpec((1, d), lambda i: (0, 0))],
        scratch_shapes=[], compiler_params=_cparams(("arbitrary",), VMEM_BIG),
        args=[du, w_a, w_b, x1, mod3, g_ffn, dx2, mix])


def _up_weight_grad(h2, du, rider=None):
    t, d = h2.shape
    tk = TOKEN_TILE
    wcol = D_FF // 2
    half = d // 2
    n_k = t // tk

    def body(h_ref, du_ref, ga_ref, gb_ref, ga16_ref, gb16_ref):
        k = pl.program_id(1)

        @pl.when(k == 0)
        def _():
            ga_ref[...] = jnp.zeros_like(ga_ref)
            gb_ref[...] = jnp.zeros_like(gb_ref)

        du = du_ref[0]
        ga_ref[0] += _mm_tn(h_ref[:, :half], du)
        gb_ref[0] += _mm_tn(h_ref[:, half:], du)

        @pl.when(k == n_k - 1)
        def _():
            ga16_ref[...] = ga_ref[...].astype(BF16)
            gb16_ref[...] = gb_ref[...].astype(BF16)

    g_spec = pl.BlockSpec((1, half, wcol), lambda j, k: (j, 0, 0))
    f32_out = jax.ShapeDtypeStruct((N_SHARD, half, wcol), F32)
    b16_out = jax.ShapeDtypeStruct((N_SHARD, half, wcol), BF16)
    return _hosted(
        body, rider, name="up_weight_grad", grid=(N_SHARD, n_k),
        out_shape=[f32_out, f32_out, b16_out, b16_out],
        in_specs=[pl.BlockSpec((tk, d), lambda j, k: (k, 0)),
                  pl.BlockSpec((1, tk, wcol), lambda j, k: (j // 2, k, j % 2))],
        out_specs=[g_spec, g_spec, g_spec, g_spec], scratch_shapes=[],
        compiler_params=_cparams(("arbitrary", "arbitrary"), VMEM_BIG), args=[h2, du])


def _out_backward(dmix, w_out, oab, oa, ob, g_na, g_sw):
    t, d = dmix.shape
    tm = TOKEN_TILE
    hw = NA_WIDTH

    def body(dm_ref, w_ref, oab_ref, oa_ref, ob_ref, gna_ref, gsw_ref,
             doa_ref, dob_ref, gw_ref, gwb_ref, dgna_ref, dgsw_ref):
        @pl.when(pl.program_id(0) == 0)
        def _():
            gw_ref[...] = jnp.zeros_like(gw_ref)
            dgna_ref[...] = jnp.zeros_like(dgna_ref)
            dgsw_ref[...] = jnp.zeros_like(dgsw_ref)

        dm = dm_ref[...]
        gw_ref[...] += _mm_tn(oab_ref[...], dm)

        @pl.when(pl.program_id(0) == t // tm - 1)
        def _():
            gwb_ref[...] = gw_ref[...].astype(BF16)

        do = _mm_nt(dm, w_ref[...])
        for raw_ref, g_ref, dst_ref, dg_ref, lo in ((oa_ref, gna_ref, doa_ref, dgna_ref, 0),
                                                     (ob_ref, gsw_ref, dob_ref, dgsw_ref, hw)):
            r, xn = _rms_stats(raw_ref[...])
            dpart = do[:, lo:lo + hw]
            dg_ref[...] += jnp.sum(dpart * xn, axis=0, keepdims=True)
            dst_ref[...] = _rms_bwd(dpart * g_ref[...], xn, r).astype(BF16)

    tile = lambda w: pl.BlockSpec((tm, w), lambda i: (i, 0))
    vec = lambda w: pl.BlockSpec((1, w), lambda i: (0, 0))
    return pl.pallas_call(
        body, name="out_backward", grid=(t // tm,),
        out_shape=(jax.ShapeDtypeStruct((t, hw), BF16), jax.ShapeDtypeStruct((t, hw), BF16),
                   jax.ShapeDtypeStruct((d, d), F32), jax.ShapeDtypeStruct((d, d), BF16),
                   jax.ShapeDtypeStruct((1, hw), F32), jax.ShapeDtypeStruct((1, hw), F32)),
        in_specs=[tile(d), pl.BlockSpec((d, d), lambda i: (0, 0)), tile(d), tile(hw), tile(hw), vec(hw), vec(hw)],
        out_specs=(tile(hw), tile(hw), pl.BlockSpec((d, d), lambda i: (0, 0)), pl.BlockSpec((d, d), lambda i: (0, 0)),
                   vec(hw), vec(hw)),
        compiler_params=_cparams(("arbitrary",), VMEM_BIG),
    )(dmix, w_out, oab, oa, ob, g_na, g_sw)


def _na_backward(proj, d_o, lse, tiles, batch, seq, rider=None):
    t = proj.shape[0]
    n_rows = seq // GRID_W
    n_pairs = NA_WIDTH // LANES
    win = NA_ROWS * GRID_W
    n_tiles = 2 * NA_ROWS - 2

    def body(q_ref, k_ref, v_ref, do_ref, lse_ref, tp_ref, dq_ref, dk_ref, dv_ref, dtp_ref, km, vm, dk_acc, dv_acc):
        @pl.when(pl.program_id(1) == 0)
        def _():
            dtp_ref[...] = jnp.zeros_like(dtp_ref)

        _na_prepare(k_ref, v_ref, km, vm)
        dk_acc[...] = jnp.zeros_like(dk_acc)
        dv_acc[...] = jnp.zeros_like(dv_acc)
        low = lax.broadcasted_iota(jnp.int32, (win, LANES), 1) < HEAD_DIM

        def scores(r):
            rs, off = _na_window(r, n_rows)
            rows = pl.ds(pl.multiple_of(r * GRID_W, GRID_W), GRID_W)
            wrows = pl.ds(pl.multiple_of(rs * GRID_W, GRID_W), win)
            q, do = q_ref[rows, :], do_ref[rows, :]
            k2 = _na_pair_window(km, wrows)
            s = _na_scores(q, k2, tp_ref, off)
            dp = _mm_nt(do, _na_pair_window(vm, wrows))
            return rows, wrows, off, q, do, k2, s, dp

        def finish(rows, wrows, off, q, do, k2, s, dp):
            p = _pair_probs_from_lse(s, lse_ref[rows, :])
            parts = []
            for h in range(2):
                ph, dph = p[:, h * win:(h + 1) * win], dp[:, h * win:(h + 1) * win]
                dsh = ph * (dph - jnp.sum(ph * dph, axis=-1, keepdims=True))
                for w in range(NA_ROWS // 2):
                    dtp_ref[h, 2 * w - off + (NA_ROWS - 1)] += dsh[:, w * LANES:(w + 1) * LANES]
                parts.append(dsh)
            dsb = (jnp.concatenate(parts, axis=1) * QK_SCALE).astype(BF16)
            dq_ref[rows, :] = _mm(dsb, k2).astype(BF16)
            dk2 = _mm_tn(dsb, q)
            dv2 = _mm_tn(p.astype(BF16), do)
            dk_acc[wrows, :] += jnp.where(low, dk2[:win], dk2[win:])
            dv_acc[wrows, :] += jnp.where(low, dv2[:win], dv2[win:])

        def row_group(i, carry):
            for state in [scores(NA_GROUP * i + j) for j in range(NA_GROUP)]:
                finish(*state)
            return carry

        lax.fori_loop(0, n_rows // NA_GROUP, row_group, 0)
        dk_ref[...] = dk_acc[...].astype(BF16)
        dv_ref[...] = dv_acc[...].astype(BF16)

    blk = lambda off: pl.BlockSpec((seq, LANES), lambda p, b: (b, off + p))
    out = jax.ShapeDtypeStruct((t, NA_WIDTH), BF16)
    return _hosted(
        body, rider, name="na_backward", grid=(n_pairs, batch),
        out_shape=[out, out, out, jax.ShapeDtypeStruct(tiles.shape, F32)],
        in_specs=[blk(0), blk(n_pairs), blk(2 * n_pairs), blk(0), blk(0),
                  pl.BlockSpec((2, n_tiles, GRID_W, LANES), lambda p, b: (p, 0, 0, 0))],
        out_specs=[blk(0), blk(0), blk(0), pl.BlockSpec((2, n_tiles, GRID_W, LANES), lambda p, b: (p, 0, 0, 0))],
        scratch_shapes=[pltpu.VMEM((2, seq, LANES), BF16), pltpu.VMEM((2, seq, LANES), BF16),
                        pltpu.VMEM((seq, LANES), F32), pltpu.VMEM((seq, LANES), F32)],
        compiler_params=_cparams(("arbitrary", "arbitrary")), args=[proj, proj, proj, d_o, lse, tiles])


def _na_bias_grad(dtiles, expand):
    n = dtiles.shape[0]

    def body(t_ref, e_ref, o_ref):
        flat = jnp.concatenate([t_ref[:, qq, :] for qq in range(GRID_W)], axis=1)
        o_ref[...] = lax.dot_general(flat, e_ref[...], (((1,), (1,)), ((), ())),
                                     precision=lax.Precision.HIGHEST, preferred_element_type=F32)

    return pl.pallas_call(
        body, name="na_bias_grad",
        out_shape=jax.ShapeDtypeStruct((n, expand.shape[0]), F32),
        compiler_params=_cparams(vmem=VMEM_BIG),
    )(dtiles, expand)


def _sw_backward(proj, d_o, lse, sink, batch, seq, rider=None):
    t = proj.shape[0]
    n_pairs = SW_WIDTH // LANES
    q_blk = 3 * NA_WIDTH // LANES
    k_blk = q_blk + n_pairs
    n_blocks = seq // SW_BLOCK
    pad = seq + 2 * SW_BLOCK

    def body(sink_ref, q_ref, k_ref, v_ref, do_ref, lse_ref, dq_ref, dk_ref, dv_ref, dsk_ref,
             k_lo, k_hi, v_lo, v_hi, dk_loc, dv_loc, dk_tot, dv_tot):
        hp = pl.program_id(1)
        g = hp // 2
        _sw_prepare(k_ref, g, k_lo, k_hi, seq)
        _sw_prepare(v_ref, g, v_lo, v_hi, seq)
        dk_loc[...] = jnp.zeros_like(dk_loc)
        dv_loc[...] = jnp.zeros_like(dv_loc)

        @pl.when(hp == 0)
        def _():
            dk_tot[...] = jnp.zeros_like(dk_tot)
            dv_tot[...] = jnp.zeros_like(dv_tot)

        band = 3 * SW_BLOCK
        low = lax.broadcasted_iota(jnp.int32, (band, LANES), 1) < HEAD_DIM

        sinks = (sink_ref[2 * hp], sink_ref[2 * hp + 1])

        def scores(n):
            rows = pl.ds(pl.multiple_of(n * SW_BLOCK, SW_BLOCK), SW_BLOCK)
            wrows = pl.ds(pl.multiple_of(n * SW_BLOCK, SW_BLOCK), band)
            qb, do = q_ref[rows, :], do_ref[rows, :]
            k2 = jnp.concatenate([k_lo[wrows, :], k_hi[wrows, :]], axis=0)
            v2 = jnp.concatenate([v_lo[wrows, :], v_hi[wrows, :]], axis=0)
            return n, rows, wrows, qb, do, k2, _mm_nt(qb, k2) * QK_SCALE, _mm_nt(do, v2)

        def finish(sink_acc, n, rows, wrows, qb, do, k2, s2, dp):
            p, ps = _sw_probs_from_lse(s2, _sw_mask(n, seq), sinks, lse_ref[rows, :])
            parts, new = [], []
            for i in range(2):
                ph, dph = p[:, i * band:(i + 1) * band], dp[:, i * band:(i + 1) * band]
                delta = jnp.sum(ph * dph, axis=-1, keepdims=True)
                parts.append(ph * (dph - delta))
                new.append(sink_acc[i] - ps[i] * delta)
            dsb = (jnp.concatenate(parts, axis=1) * QK_SCALE).astype(BF16)
            dq_ref[rows, :] = _mm(dsb, k2)
            dk2 = _mm_tn(dsb, qb)
            dv2 = _mm_tn(p.astype(BF16), do)
            dk_loc[wrows, :] += jnp.where(low, dk2[:band], dk2[band:])
            dv_loc[wrows, :] += jnp.where(low, dv2[:band], dv2[band:])
            return tuple(new)

        def block_group(i, carry):
            for state in [scores(SW_GROUP_BLOCKS * i + j) for j in range(SW_GROUP_BLOCKS)]:
                carry = finish(carry, *state)
            return carry

        zero = jnp.zeros((SW_BLOCK, 1), F32)
        s0, s1 = lax.fori_loop(0, n_blocks // SW_GROUP_BLOCKS, block_group, (zero, zero))
        row = lax.broadcasted_iota(jnp.int32, (SUBLANES, LANES), 0)
        dsk_ref[0, 0] = jnp.where(row == 0, jnp.sum(s0), jnp.where(row == 1, jnp.sum(s1), 0.0))

        lane_s = lax.broadcasted_iota(jnp.int32, (seq, LANES), 1)
        mine_g = (lane_s // HEAD_DIM) == g
        for loc, tot in ((dk_loc, dk_tot), (dv_loc, dv_tot)):
            part = loc[SW_BLOCK:SW_BLOCK + seq, :]
            tot[...] += jnp.where(mine_g, part + pltpu.roll(part, HEAD_DIM, 1), 0.0)

        @pl.when(hp == n_pairs - 1)
        def _():
            dk_ref[...] = dk_tot[...]
            dv_ref[...] = dv_tot[...].astype(BF16)

    return _hosted(
        body, rider, name="sw_backward", grid=(batch, n_pairs),
        out_shape=[jax.ShapeDtypeStruct((t, SW_WIDTH), F32), jax.ShapeDtypeStruct((t, LANES), F32),
                   jax.ShapeDtypeStruct((t, LANES), BF16), jax.ShapeDtypeStruct((batch, n_pairs, SUBLANES, LANES), F32)],
        in_specs=[pl.BlockSpec(memory_space=pltpu.SMEM),
                  pl.BlockSpec((seq, LANES), lambda b, p: (b, q_blk + p)),
                  pl.BlockSpec((seq, LANES), lambda b, p: (b, k_blk)),
                  pl.BlockSpec((seq, LANES), lambda b, p: (b, k_blk + 1)),
                  pl.BlockSpec((seq, LANES), lambda b, p: (b, p)), pl.BlockSpec((seq, LANES), lambda b, p: (b, p))],
        out_specs=[pl.BlockSpec((seq, LANES), lambda b, p: (b, p)), pl.BlockSpec((seq, LANES), lambda b, p: (b, 0)),
                   pl.BlockSpec((seq, LANES), lambda b, p: (b, 0)),
                   pl.BlockSpec((1, 1, SUBLANES, LANES), lambda b, p: (b, p, 0, 0))],
        scratch_shapes=[pltpu.VMEM((pad, LANES), BF16)] * 4 + [pltpu.VMEM((pad, LANES), F32)] * 2
        + [pltpu.VMEM((seq, LANES), F32)] * 2,
        compiler_params=_cparams(("arbitrary", "arbitrary")), args=[sink, proj, proj, proj, d_o, lse])


def _in_backward(dqkv_a, dq_b, dk_b, dv_b, w_in_t, h1, x, mod3, g_attn, dx1, cos_t, sin_t, seq):
    t, d = x.shape
    tm = TOKEN_TILE
    per_seq = seq // tm
    batch = t // seq
    dqa, dka, dva = dqkv_a
    n_q = SW_WIDTH // LANES

    def body(dqa_ref, dka_ref, dva_ref, dqb_ref, dkb_ref, dvb_ref, w_ref, h_ref, x_ref, mod_ref, g_ref, dx1_ref,
             cos_ref, sin_ref, dx_ref, gw_ref, gwb_ref, dsh_ref, dsc_ref, dg_ref):
        i = pl.program_id(0)

        @pl.when(i == 0)
        def _():
            gw_ref[...] = jnp.zeros_like(gw_ref)
            dg_ref[...] = jnp.zeros_like(dg_ref)

        @pl.when(i % per_seq == 0)
        def _():
            dsh_ref[...] = jnp.zeros_like(dsh_ref)
            dsc_ref[...] = jnp.zeros_like(dsc_ref)

        dr = jnp.concatenate([dqb_ref[...], dkb_ref[...]], axis=1)
        cos = jnp.concatenate([cos_ref[...]] * (n_q + 1), axis=1)
        sin = jnp.concatenate([sin_ref[...]] * (n_q + 1), axis=1)
        dr = dr * cos + _rope_rot(dr * sin)
        dproj = jnp.concatenate([dqa_ref[...], dka_ref[...], dva_ref[...], dr.astype(BF16), dvb_ref[...]], axis=1)
        gw_ref[...] += _mm_tn(dproj, h_ref[...])

        @pl.when(i == t // tm - 1)
        def _():
            gwb_ref[...] = gw_ref[...].astype(BF16)

        dh = _mm(dproj, w_ref[...])
        scale = mod_ref[0, :, d:2 * d]
        r, xn = _rms_stats(x_ref[...])
        xg = xn * g_ref[...]
        dxg = dh * (1.0 + scale)
        dx_ref[...] = dx1_ref[...] + _rms_bwd(dxg * g_ref[...], xn, r)
        dg_ref[...] += jnp.sum(dxg * xn, axis=0, keepdims=True)
        dsh_ref[0] += jnp.sum(dh, axis=0, keepdims=True)
        dsc_ref[0] += jnp.sum(dh * xg, axis=0, keepdims=True)

    tile = lambda w: pl.BlockSpec((tm, w), lambda i: (i, 0))
    per_b = pl.BlockSpec((1, 1, d), lambda i: (i // per_seq, 0, 0))
    small = jax.ShapeDtypeStruct((batch, 1, d), F32)
    rope = pl.BlockSpec((tm, LANES), lambda i: (i % per_seq, 0))
    return pl.pallas_call(
        body, name="in_backward", grid=(t // tm,),
        out_shape=(jax.ShapeDtypeStruct((t, d), F32), jax.ShapeDtypeStruct((IN_WIDTH, d), F32),
                   jax.ShapeDtypeStruct((IN_WIDTH, d), BF16), small, small, jax.ShapeDtypeStruct((1, d), F32)),
        in_specs=[tile(NA_WIDTH), tile(NA_WIDTH), tile(NA_WIDTH), tile(SW_WIDTH), tile(LANES), tile(LANES),
                  _resident((IN_WIDTH, d)), tile(d), tile(d),
                  pl.BlockSpec((1, 1, 6 * d), lambda i: (i // per_seq, 0, 0)),
                  pl.BlockSpec((1, d), lambda i: (0, 0)), tile(d), rope, rope],
        out_specs=(tile(d), _resident((IN_WIDTH, d)), _resident((IN_WIDTH, d)),
                   per_b, per_b, pl.BlockSpec((1, d), lambda i: (0, 0))),
        compiler_params=_cparams(("arbitrary",), VMEM_BIG),
    )(dqa, dka, dva, dq_b, dk_b, dv_b, w_in_t, h1, x, mod3, g_attn, dx1, cos_t, sin_t)


def _ada_weight_grad(sc_all, dmod_cols):
    d = sc_all.shape[1]
    ncol = dmod_cols.shape[1]

    def body(s_ref, m_ref, o_ref):
        o_ref[...] = _mm_tn(s_ref[...].astype(BF16), m_ref[...].astype(BF16))

    return pl.pallas_call(
        body, name="ada_weight_grad",
        out_shape=jax.ShapeDtypeStruct((d, ncol), F32),
        compiler_params=_cparams(vmem=VMEM_BIG),
    )(sc_all, dmod_cols)


def _row_tile(rows, cols):
    target = max(SUBLANES, (1 << 20) // (4 * cols))
    best = rows
    for cand in range(SUBLANES, rows + 1, SUBLANES):
        if rows % cand == 0 and cand <= target:
            best = cand
    return best if rows % SUBLANES == 0 else rows


def _sum_slots(parts, name):
    n = len(parts)
    _, rows, cols = parts[0][0].shape
    tr = _row_tile(rows, cols)
    per = rows // tr

    def body(*refs):
        o_ref = refs[-1]
        for q in range(n):
            @pl.when(pl.program_id(0) == q)
            def _(q=q):
                p_ref, own_ref = refs[2 * q], refs[2 * q + 1]
                o_ref[...] = ((own_ref[...] + p_ref[0].astype(F32)) + p_ref[1].astype(F32)) + p_ref[2].astype(F32)

    in_specs, args = [], []
    for q, (recv, own) in enumerate(parts):
        in_specs.append(pl.BlockSpec((N_SHARD - 1, tr, cols), lambda p, i, q=q: (0, jnp.where(p == q, i, 0), 0)))
        in_specs.append(pl.BlockSpec((tr, cols), lambda p, i, q=q: (jnp.where(p == q, i, 0), 0)))
        args += [recv, own]
    return pl.pallas_call(
        body, name=name, grid=(n, per),
        out_shape=jax.ShapeDtypeStruct((n * rows, cols), F32),
        in_specs=in_specs, out_specs=pl.BlockSpec((tr, cols), lambda p, i: (p * per + i, 0)),
        compiler_params=_cparams(("arbitrary", "arbitrary")),
    )(*args)


def _adamw_math(w, g, m, v):
    m2 = ADAM_B1 * m + (1.0 - ADAM_B1) * g
    v2 = ADAM_B2 * v + (1.0 - ADAM_B2) * (g * g)
    m_hat = m2 / (1.0 - ADAM_B1 ** ADAM_STEP)
    v_hat = v2 / (1.0 - ADAM_B2 ** ADAM_STEP)
    return -ADAM_LR * (m_hat / (jnp.sqrt(v_hat) + ADAM_EPS) + ADAM_WD * w), m2, v2


def _small_step(partials, states, dmod, b_ada_state, rider=None):
    n_upd = len(states)
    moving = list(partials) + [dmod]
    n_mov = len(moving)
    all_states = list(states) + [b_ada_state]

    def body(*refs):
        mov, refs = refs[:n_mov], refs[n_mov:]
        wmv, refs = refs[:3 * (n_upd + 1)], refs[3 * (n_upd + 1):]
        res, refs = refs[:4 * (n_upd + 1)], refs[4 * (n_upd + 1):]
        sums_out, refs = refs[:n_mov - n_upd - 1], refs[n_mov - n_upd - 1:]
        dmod_out, refs = refs[0], refs[1:]
        everyone, (ssem, rsem) = refs[:n_mov], refs[n_mov:]
        x, y, c = _my_pos()
        me = 4 * x + 2 * y + c
        cps = []
        for a in range(n_mov):
            everyone[a][me] = mov[a][...]
            for k in range(1, N_DEV):
                peer = (_flip(x, (k >> 2) & 1), _flip(y, (k >> 1) & 1), _flip(c, k & 1))
                cps.append(pltpu.make_async_remote_copy(
                    src_ref=everyone[a].at[me], dst_ref=everyone[a].at[me], send_sem=ssem.at[a, k - 1],
                    recv_sem=rsem.at[a, k - 1], device_id=peer, device_id_type=MESH))
        for cp in cps:
            cp.start()
        for cp in cps:
            cp.wait_recv()

        def total(a):
            acc = everyone[a][0]
            for dev in range(1, N_DEV):
                acc = acc + everyone[a][dev]
            return acc

        grads = [total(a) for a in range(n_upd)]
        grads.append(jnp.sum(total(n_mov - 1), axis=0, keepdims=True))
        for j, g in enumerate(grads):
            delta, m2, v2 = _adamw_math(wmv[3 * j][...], g, wmv[3 * j + 1][...], wmv[3 * j + 2][...])
            res[4 * j][...] = g
            res[4 * j + 1][...] = delta
            res[4 * j + 2][...] = m2
            res[4 * j + 3][...] = v2
        for j in range(n_mov - n_upd - 1):
            sums_out[j][...] = total(n_upd + j)
        dmod_out[...] = everyone[n_mov - 1][...]
        for cp in cps:
            cp.wait_send()

    vm = pl.BlockSpec(memory_space=pltpu.VMEM)
    sds = jax.ShapeDtypeStruct
    out_shape = []
    for w, _, _ in all_states:
        out_shape += [sds(w.shape, F32)] * 4
    out_shape += [sds(p.shape, F32) for p in partials[n_upd:]]
    out_shape.append(sds((N_DEV,) + dmod.shape, F32))
    args = moving + [a for st in all_states for a in st]
    outs, rides = _hosted(
        body, rider, name="small_step", grid=(), out_shape=out_shape,
        in_specs=[vm] * len(args), out_specs=[vm] * len(out_shape),
        scratch_shapes=[pltpu.VMEM((N_DEV,) + a.shape, F32) for a in moving]
        + [pltpu.SemaphoreType.DMA((n_mov, N_DEV - 1)), pltpu.SemaphoreType.DMA((n_mov, N_DEV - 1))],
        compiler_params=_cparams(vmem=VMEM_BIG), args=args)
    return outs, rides


def _adamw(w, grads, m, v, name):
    rows, cols = w.shape
    tr = _row_tile(rows, cols)
    ng = len(grads)

    def body(*refs):
        w_ref = refs[0]
        g_refs = refs[1:1 + ng]
        m_ref, v_ref = refs[1 + ng], refs[2 + ng]
        g_out, d_out, m_out, v_out = refs[3 + ng:]
        g = g_refs[0][...]
        for extra in g_refs[1:]:
            g = g + extra[...]
        g_out[...] = g
        d_out[...], m_out[...], v_out[...] = _adamw_math(w_ref[...], g, m_ref[...], v_ref[...])

    spec = pl.BlockSpec((tr, cols), lambda i: (i, 0))
    out = jax.ShapeDtypeStruct((rows, cols), F32)
    return pl.pallas_call(
        body, name=name, grid=(rows // tr,),
        out_shape=(out, out, out, out),
        in_specs=[spec] * (3 + ng), out_specs=(spec, spec, spec, spec),
        compiler_params=_cparams(("arbitrary",)),
    )(w, *grads, m, v)


def _pack_rows(arrays):
    tile = SUBLANES * LANES
    rows, offsets, at = [], [], 0
    for a in arrays:
        flat = a.reshape(-1).astype(F32)
        n = -(-flat.shape[0] // tile) * tile
        rows.append(jnp.pad(flat, (0, n - flat.shape[0])).reshape(-1, LANES))
        offsets.append(at)
        at += n // LANES
    return jnp.concatenate(rows, axis=0), offsets


def _unpack_rows(packed, offsets, shapes):
    out = []
    for off, shape in zip(offsets, shapes):
        n = 1
        for s in shape:
            n *= s
        nrow = -(-n // LANES)
        out.append(packed[off:off + nrow].reshape(-1)[:n].reshape(shape))
    return out


def _rope_tables(seq):
    half = HEAD_DIM // 2
    inv = np.float32(ROPE_THETA) ** (-np.arange(half, dtype=np.float32) / np.float32(half))
    ang = (np.arange(seq, dtype=np.float32)[:, None] * inv[None, :]).astype(np.float64)
    cos, sin = np.cos(ang).astype(np.float32), np.sin(ang).astype(np.float32)
    cos_t = np.concatenate([cos, cos, cos, cos], axis=1)
    sin_t = np.concatenate([-sin, sin, -sin, sin], axis=1)
    return jnp.asarray(cos_t), jnp.asarray(sin_t)


def kernel(x, c, w_ada, b_ada, g_attn, w_in, na_rpb, sw_sink, g_na_out, g_sw_out, w_out, g_ffn, w_up, conv_w, conv_b, w_down, g_final, loss_target, m_w_ada, m_b_ada, m_g_attn, m_w_in, m_na_rpb, m_sw_sink, m_g_na_out, m_g_sw_out, m_w_out, m_g_ffn, m_w_up, m_conv_w, m_conv_b, m_w_down, m_g_final, v_w_ada, v_b_ada, v_g_attn, v_w_in, v_na_rpb, v_sw_sink, v_g_na_out, v_g_sw_out, v_w_out, v_g_ffn, v_w_up, v_conv_w, v_conv_b, v_w_down, v_g_final):
    batch, seq, d = x.shape
    t = batch * seq
    assert d == D_MODEL and seq % (NA_ROWS * GRID_W) == 0 and seq % TOKEN_TILE == 0 and batch <= SUBLANES
    shard = 2 * lax.axis_index("x") + lax.axis_index("y")
    xt = x.reshape(t, d)
    tgt = loss_target.reshape(t, d)

    c8 = jnp.pad(c, ((0, SUBLANES - batch), (0, 0)))
    w_in_t_s = jnp.transpose(w_in[0]).astype(BF16)
    (mod8, sc_all), (w_in_g,) = _ada_forward(c8, w_ada[0], b_ada, _Rider("gather", [w_in_t_s]))
    mod3 = mod8[:batch].reshape(batch, 1, 6 * d)
    w_in_t = w_in_g.reshape(IN_WIDTH, d)

    cos_t, sin_t = _rope_tables(seq)
    (h1, proj), (w_out_g,) = _in_proj(xt, mod3, g_attn, w_in_t, cos_t, sin_t, seq,
                                      _Rider("gather", [w_out[0].astype(BF16)]))
    n_heads = NA_WIDTH // HEAD_DIM
    n_tiles, n_dc = 2 * NA_ROWS - 2, 2 * NA_COLS - 1
    expand, neg_mask = _na_bias_pattern()
    rpb = na_rpb[0]
    rows2 = jnp.concatenate([rpb[:, :-1, :], rpb[:, 1:, :]], axis=2).reshape(n_heads * n_tiles, 2 * n_dc)
    rows2 = jnp.pad(rows2, ((0, 0), (0, GRID_W - 2 * n_dc)))
    tiles = _na_bias_tiles(rows2, expand, neg_mask).reshape(n_heads, n_tiles, GRID_W, LANES)
    sink = sw_sink[0]
    w_up_b16 = w_up[0].astype(BF16)
    (oa, lse_a), (w_up_a,) = _na_forward(proj, tiles, batch, seq, _Rider("gather", [w_up_b16[:d // 2]]))
    (ob, lse_b), (w_up_b, conv_w_g) = _sw_forward(proj, sink, batch, seq,
                                                  _Rider("gather", [w_up_b16[d // 2:], conv_w[0]]))
    w_up_f = (w_up_a, w_up_b)
    w_out_f = w_out_g.reshape(d, d)
    conv_w_f = jnp.transpose(conv_w_g, (1, 0, 2)).reshape(3, D_FF)
    oab, mix, x1, h2 = _out_proj(oa, ob, g_na_out, g_sw_out, w_out_f, xt, mod3, g_ffn, seq)
    (u,), (w_down_g,) = _up_proj(h2, w_up_f, _Rider("gather", [w_down[0].astype(BF16)]))
    w_down_f = w_down_g.reshape(D_FF, d)
    a = _conv_gate(u, conv_w_f, conv_b, batch, seq)
    dx2, dffn, loss_part, dgate_f, dg_final = _down_and_loss(a, w_down_f, x1, mod3, g_final.reshape(1, d), tgt, seq)

    gw_down, gw_down_b = _down_weight_grad(a, dffn)
    blocks = lambda g, rows: g.reshape(N_SHARD, rows // N_SHARD, d)
    (du, gconv_w, gconv_b), (recv_down, own_down) = _ffn_backward(
        dffn, w_down_f, u, conv_w_f, conv_b, batch, seq,
        _Rider("scatter", [blocks(gw_down_b, D_FF)], [blocks(gw_down, D_FF)]))
    (gw_up_top, gw_up_bot, gw_up_top_b, gw_up_bot_b), _ = _up_weight_grad(h2, du)
    (dx1, dmix, dshift_f, dscale_f, dgate_a, dg_ffn), (recv_up_top, own_up_top) = _up_backward(
        du, w_up_f, x1, mod3, g_ffn, dx2, mix, seq, _Rider("scatter", [gw_up_top_b], [gw_up_top]))
    doa, dob, gw_out, gw_out_b, dg_na, dg_sw = _out_backward(dmix, w_out_f, oab, oa, ob, g_na_out, g_sw_out)
    (dqa, dka, dva, dtiles), (recv_up_bot, own_up_bot) = _na_backward(
        proj, doa, lse_a, tiles, batch, seq, _Rider("scatter", [gw_up_bot_b], [gw_up_bot]))
    (dq_b, dk_b, dv_b, dsink_parts), (recv_out, own_out) = _sw_backward(
        proj, dob, lse_b, sink, batch, seq, _Rider("scatter", [blocks(gw_out_b, d)], [blocks(gw_out, d)]))
    gx, gw_in_t, gw_in_b, dshift_a, dscale_a, dg_attn = _in_backward(
        (dqa, dka, dva), dq_b, dk_b, dv_b, w_in_t, h1, xt, mod3, g_attn, dx1, cos_t, sin_t, seq)

    red = _na_bias_grad(dtiles.reshape(n_heads * n_tiles, GRID_W, LANES), expand)[:, :2 * n_dc]
    red = red.reshape(n_heads, n_tiles, 2, n_dc)
    zero_row = jnp.zeros((n_heads, 1, n_dc), F32)
    g_rpb = (jnp.concatenate([red[:, :, 0, :], zero_row], axis=1)
             + jnp.concatenate([zero_row, red[:, :, 1, :]], axis=1))
    g_sink = jnp.sum(dsink_parts[:, :, :2, 0], axis=0).reshape(SW_WIDTH // HEAD_DIM)

    dmod = jnp.concatenate([dshift_a, dscale_a, dgate_a, dshift_f, dscale_f, dgate_f], axis=2).reshape(batch, 6 * d)
    rpb_shape = na_rpb.shape[1:]
    states = [(g_attn, m_g_attn, v_g_attn),
              (na_rpb.reshape(rpb_shape), m_na_rpb.reshape(rpb_shape), v_na_rpb.reshape(rpb_shape)),
              (sw_sink, m_sw_sink, v_sw_sink), (g_na_out, m_g_na_out, v_g_na_out), (g_sw_out, m_g_sw_out, v_g_sw_out),
              (g_ffn, m_g_ffn, v_g_ffn), (conv_b, m_conv_b, v_conv_b),
              (g_final.reshape(1, d), m_g_final.reshape(1, d), v_g_final.reshape(1, d))]
    partials = [dg_attn, g_rpb, g_sink.reshape(sw_sink.shape), dg_na, dg_sw, dg_ffn, gconv_b, dg_final,
                gconv_w, loss_part]
    small, (recv_in, own_in) = _small_step(
        partials, states, dmod, (b_ada, m_b_ada, v_b_ada),
        _Rider("scatter", [blocks(gw_in_b, IN_WIDTH)], [blocks(gw_in_t, IN_WIDTH)]))
    r_small = [small[4 * j:4 * j + 4] for j in range(len(states) + 1)]
    g_conv_w_full, loss_sum, dmod_all = small[4 * (len(states) + 1):]
    loss = loss_sum[0, 0]
    mine = [_sum_slots([(recv_in, own_in)], "sum_w_in"), _sum_slots([(recv_out, own_out)], "sum_w_out"),
            _sum_slots([(recv_up_top, own_up_top), (recv_up_bot, own_up_bot)], "sum_w_up"),
            _sum_slots([(recv_down, own_down)], "sum_w_down")]
    theirs = _ride_alone(_Rider("swap", mine), "swap_sibling")
    dmod_rows = jnp.pad(dmod_all, ((0, 0), (0, SUBLANES - batch), (0, 0))).reshape(N_DEV * SUBLANES, 6 * d)
    ncol = w_ada.shape[2]
    g_w_ada = _ada_weight_grad(sc_all, lax.dynamic_slice(dmod_rows, (0, shard * ncol), (N_DEV * SUBLANES, ncol)))
    cshard = conv_w.shape[2]
    g_conv_w = lax.dynamic_slice(g_conv_w_full, (0, shard * cshard), (3, cshard))

    def big(w, m, v, g_parts, name):
        shape = w.shape
        outs = _adamw(w[0], g_parts, m[0], v[0], name)
        return [o.reshape(shape) for o in outs]

    r_w_ada = big(w_ada, m_w_ada, v_w_ada, [g_w_ada], "adamw_w_ada")
    r_w_in = [jnp.transpose(o).reshape(w_in.shape) for o in
              _adamw(jnp.transpose(w_in[0]), [mine[0], theirs[0]], jnp.transpose(m_w_in[0]), jnp.transpose(v_w_in[0]),
                     "adamw_w_in")]
    r_w_out = big(w_out, m_w_out, v_w_out, [mine[1], theirs[1]], "adamw_w_out")
    r_w_up = big(w_up, m_w_up, v_w_up, [mine[2], theirs[2]], "adamw_w_up")
    r_w_down = big(w_down, m_w_down, v_w_down, [mine[3], theirs[3]], "adamw_w_down")

    r_conv_w = big(conv_w, m_conv_w, v_conv_w, [g_conv_w], "adamw_conv_w")

    def pick(k):
        ga_, rpb_, sk_, gna_, gsw_, gf_, cb_, gfin_, b_ = [r[k] for r in r_small]
        return [r_w_ada[k], b_, ga_, r_w_in[k], rpb_.reshape(na_rpb.shape), sk_, gna_, gsw_, r_w_out[k], gf_,
                r_w_up[k], r_conv_w[k], cb_, r_w_down[k], gfin_.reshape(d)]

    return (loss, gx.reshape(batch, seq, d), *pick(0), *pick(1), *pick(2), *pick(3))
```

```python
import functools

import jax
import jax.numpy as jnp
import numpy as np
from jax import lax
from jax.experimental import pallas as pl
from jax.experimental.pallas import tpu as pltpu

F32 = jnp.float32
BF16 = jnp.bfloat16
MESH = pl.DeviceIdType.MESH

D_MODEL = 1024
HEAD_DIM = 64
NA_WIDTH = 512
SW_WIDTH = 512
SW_KV_WIDTH = 128
IN_WIDTH = 2304
D_FF = 2816
GRID_W = 64
NA_ROWS = 8
NA_COLS = 16
SW_BLOCK = 128
ROPE_THETA = 10000.0
EPS = 1e-6
NEG = -1e30
QK_SCALE = HEAD_DIM ** -0.5

ADAM_LR = 0.001
ADAM_B1 = 0.9
ADAM_B2 = 0.999
ADAM_EPS = 1e-08
ADAM_WD = 0.01
ADAM_STEP = 10

N_SHARD = 4
N_DEV = 8
LANES = 128
SUBLANES = 8
TOKEN_TILE = 512
FF_TILE = 256
CONV_CHUNK = 64
NA_GROUP = 4
SW_GROUP_BLOCKS = 4
VMEM_BIG = 56 * 1024 * 1024


def _mm(a, b):
    return jnp.dot(a, b, preferred_element_type=F32)


def _mm_nt(a, b):
    return lax.dot_general(a, b, (((1,), (1,)), ((), ())), preferred_element_type=F32)


def _mm_tn(a, b):
    return lax.dot_general(a, b, (((0,), (0,)), ((), ())), preferred_element_type=F32)


def _cparams(sem=None, vmem=None):
    kw = {}
    if sem is not None:
        kw["dimension_semantics"] = sem
    if vmem is not None:
        kw["vmem_limit_bytes"] = vmem
    return pltpu.CompilerParams(**kw)


def _resident(shape):
    return pl.BlockSpec(shape, lambda i: (0,) * len(shape), pipeline_mode=pl.Buffered(1))


def _sigmoid(x):
    return 1.0 / (1.0 + jnp.exp(-x))


def _rms_stats(x):
    r = lax.rsqrt(jnp.mean(x * x, axis=-1, keepdims=True) + EPS)
    return r, x * r


def _rms_bwd(dxn, xn, r):
    return r * (dxn - xn * jnp.mean(dxn * xn, axis=-1, keepdims=True))


def _my_pos():
    return lax.axis_index("x"), lax.axis_index("y"), lax.axis_index("c")


def _flip(v, bit):
    return 1 - v if bit else v


def _ada_forward(c8, w_ada, b_ada, rider):
    d = c8.shape[1]
    ncol = w_ada.shape[1]

    def body(c_ref, w_ref, b_ref, mod_ref, sc_ref, m_scr, mod_buf, ssem, rsem, ssem2, rsem2):
        x, y, c = _my_pos()
        me = 4 * x + 2 * y + c
        shard = 2 * x + y
        cv = c_ref[...]
        my_rows = pl.ds(pl.multiple_of(me * SUBLANES, SUBLANES), SUBLANES)
        sc_ref[my_rows, :] = cv * _sigmoid(cv)

        def copy1(k):
            peer = (_flip(x, (k >> 2) & 1), _flip(y, (k >> 1) & 1), _flip(c, k & 1))
            return pltpu.make_async_remote_copy(
                src_ref=sc_ref.at[my_rows, :], dst_ref=sc_ref.at[my_rows, :],
                send_sem=ssem.at[k - 1], recv_sem=rsem.at[k - 1], device_id=peer, device_id_type=MESH)

        sends = [copy1(k) for k in range(1, N_DEV)]
        for cp in sends:
            cp.start()
        for cp in sends:
            cp.wait_recv()
        m_scr[...] = _mm(sc_ref[...].astype(BF16), w_ref[...].astype(BF16))

        def copy2(k):
            px, py = _flip(x, (k >> 1) & 1), _flip(y, k & 1)
            rows = pl.ds(pl.multiple_of((4 * px + 2 * py + c) * SUBLANES, SUBLANES), SUBLANES)
            return pltpu.make_async_remote_copy(
                src_ref=m_scr.at[rows, :], dst_ref=mod_buf.at[shard],
                send_sem=ssem2.at[k - 1], recv_sem=rsem2.at[k - 1], device_id=(px, py, c), device_id_type=MESH)

        sends2 = [copy2(k) for k in range(1, N_SHARD)]
        for cp in sends2:
            cp.start()
        mod_buf[shard] = m_scr[my_rows, :]
        for cp in sends2:
            cp.wait_recv()
        for s in range(N_SHARD):
            mod_ref[:, s * ncol:(s + 1) * ncol] = mod_buf[s] + b_ref[:, s * ncol:(s + 1) * ncol]
        for cp in sends + sends2:
            cp.wait_send()

    vm = pl.BlockSpec(memory_space=pltpu.VMEM)
    return _hosted(
        body, rider, name="ada_forward", grid=(),
        out_shape=(jax.ShapeDtypeStruct((SUBLANES, N_SHARD * ncol), F32),
                   jax.ShapeDtypeStruct((N_DEV * SUBLANES, d), F32)),
        in_specs=[vm, vm, vm], out_specs=(vm, vm),
        scratch_shapes=[pltpu.VMEM((N_DEV * SUBLANES, ncol), F32), pltpu.VMEM((N_SHARD, SUBLANES, ncol), F32),
                        pltpu.SemaphoreType.DMA((N_DEV - 1,)), pltpu.SemaphoreType.DMA((N_DEV - 1,)),
                        pltpu.SemaphoreType.DMA((N_SHARD - 1,)), pltpu.SemaphoreType.DMA((N_SHARD - 1,))],
        compiler_params=_cparams(vmem=VMEM_BIG), args=[c8, w_ada, b_ada])


class _Rider:
    def __init__(self, kind, srcs, owns=()):
        self.kind, self.srcs, self.owns = kind, list(srcs), list(owns)
        n = len(self.srcs)
        sds = jax.ShapeDtypeStruct
        dma = pltpu.SemaphoreType.DMA
        if kind == "gather":
            self.out_shapes = [sds((N_SHARD,) + s.shape, s.dtype) for s in self.srcs]
            self.sems = [dma((n, N_SHARD - 1)), dma((n, N_SHARD - 1)), dma((n, N_SHARD - 1)), dma((n, N_SHARD - 1)),
                         dma((n,)), dma((n,))]
        elif kind == "scatter":
            self.out_shapes = ([sds((N_SHARD - 1,) + s.shape[1:], s.dtype) for s in self.srcs]
                               + [sds(o.shape[1:], o.dtype) for o in self.owns])
            m = max(len(self.owns), 1)
            self.sems = [dma((n, N_SHARD - 1)), dma((n, N_SHARD - 1)), dma((m,)), dma((m,))]
        else:
            self.out_shapes = [sds(s.shape, s.dtype) for s in self.srcs]
            self.sems = [dma((n,)), dma((n,))]

    @property
    def inputs(self):
        return self.srcs + self.owns

    def _halved(self, i):
        a = self.srcs[i]
        tile_rows = SUBLANES * (4 // jnp.dtype(a.dtype).itemsize)
        return self.kind == "gather" and a.shape[0] % (2 * tile_rows) == 0

    def copies(self, ins, outs, sems):
        n = len(self.srcs)
        x, y, c = _my_pos()
        shard = 2 * x + y
        remote, relay = [], []
        if self.kind == "swap":
            ssem, rsem = sems
            for i in range(n):
                remote.append(pltpu.make_async_remote_copy(
                    src_ref=ins[i], dst_ref=outs[i], send_sem=ssem.at[i], recv_sem=rsem.at[i],
                    device_id=(x, y, 1 - c), device_id_type=MESH))
            return remote, relay
        if self.kind == "gather":
            ssem, rsem, ssem2, rsem2, sib_s, sib_r = sems
        else:
            ssem, rsem, sib_s, sib_r = sems
        for i in range(n):
            if self.kind == "gather":
                remote.append(pltpu.make_async_remote_copy(
                    src_ref=ins[i], dst_ref=outs[i].at[shard], send_sem=sib_s.at[i], recv_sem=sib_r.at[i],
                    device_id=(x, y, 1 - c), device_id_type=MESH))
                half = ins[i].shape[0] // 2
                mine = pl.ds(pl.multiple_of(c * half, half), half) if self._halved(i) else None
            for k in range(1, N_SHARD):
                px, py = _flip(x, (k >> 1) & 1), _flip(y, k & 1)
                if self.kind == "gather":
                    src, dst = ins[i], outs[i].at[shard]
                    if mine is not None:
                        src, dst = src.at[mine], dst.at[mine]
                        got = outs[i].at[2 * px + py].at[mine]
                        relay.append(pltpu.make_async_remote_copy(
                            src_ref=got, dst_ref=got, send_sem=ssem2.at[i, k - 1], recv_sem=rsem2.at[i, k - 1],
                            device_id=(x, y, 1 - c), device_id_type=MESH))
                else:
                    src, dst = ins[i].at[2 * px + py], outs[i].at[k - 1]
                remote.append(pltpu.make_async_remote_copy(
                    src_ref=src, dst_ref=dst, send_sem=ssem.at[i, k - 1], recv_sem=rsem.at[i, k - 1],
                    device_id=(px, py, c), device_id_type=MESH))
        if self.kind == "scatter":
            for i in range(len(self.owns)):
                remote.append(pltpu.make_async_remote_copy(
                    src_ref=ins[n + i].at[shard], dst_ref=outs[n + i], send_sem=sib_s.at[i], recv_sem=sib_r.at[i],
                    device_id=(x, y, 1 - c), device_id_type=MESH))
        return remote, relay

    def start(self, ins, outs, sems):
        remote, _ = self.copies(ins, outs, sems)
        for cp in remote:
            cp.start()

    def wait(self, ins, outs, sems):
        remote, relay = self.copies(ins, outs, sems)
        for cp in remote:
            cp.wait_recv()
        for cp in relay:
            cp.start()
        for cp in relay:
            cp.wait_recv()
        for cp in remote + relay:
            cp.wait_send()


class _Riders:
    def __init__(self, riders):
        self.riders = list(riders)
        self.inputs = [a for r in self.riders for a in r.inputs]
        self.out_shapes = [s for r in self.riders for s in r.out_shapes]
        self.sems = [s for r in self.riders for s in r.sems]

    def _split(self, ins, outs, sems):
        for r in self.riders:
            ni, no, ns = len(r.inputs), len(r.out_shapes), len(r.sems)
            yield r, ins[:ni], outs[:no], sems[:ns]
            ins, outs, sems = ins[ni:], outs[no:], sems[ns:]

    def start(self, ins, outs, sems):
        for r, i, o, s in self._split(ins, outs, sems):
            r.start(i, o, s)

    def wait(self, ins, outs, sems):
        for r, i, o, s in self._split(ins, outs, sems):
            r.wait(i, o, s)


def _hosted(body, rider, *, name, grid, out_shape, in_specs, out_specs, scratch_shapes, compiler_params, args):
    out_shape, out_specs = list(out_shape), list(out_specs)
    if rider is None:
        outs = pl.pallas_call(body, name=name, grid=grid, out_shape=tuple(out_shape), in_specs=list(in_specs),
                              out_specs=tuple(out_specs), scratch_shapes=list(scratch_shapes),
                              compiler_params=compiler_params)(*args)
        return list(outs), []
    n_in, n_out, n_scr = len(in_specs), len(out_shape), len(scratch_shapes)
    nr_in, nr_out = len(rider.inputs), len(rider.out_shapes)
    n_steps = 1
    for size in grid:
        n_steps *= size

    def full(*refs):
        ins, refs = refs[:n_in], refs[n_in:]
        r_in, refs = refs[:nr_in], refs[nr_in:]
        outs, refs = refs[:n_out], refs[n_out:]
        r_out, refs = refs[:nr_out], refs[nr_out:]
        scr, sems = refs[:n_scr], refs[n_scr:]
        if grid:
            step = 0
            for ax, size in enumerate(grid):
                step = step * size + pl.program_id(ax)
            pl.when(step == 0)(lambda: rider.start(r_in, r_out, sems))
            body(*ins, *outs, *scr)
            pl.when(step == n_steps - 1)(lambda: rider.wait(r_in, r_out, sems))
        else:
            rider.start(r_in, r_out, sems)
            body(*ins, *outs, *scr)
            rider.wait(r_in, r_out, sems)

    hbm = pl.BlockSpec(memory_space=pl.ANY)
    res = pl.pallas_call(
        full, name=name, grid=grid, out_shape=tuple(out_shape + rider.out_shapes),
        in_specs=list(in_specs) + [hbm] * nr_in, out_specs=tuple(out_specs + [hbm] * nr_out),
        scratch_shapes=list(scratch_shapes) + rider.sems, compiler_params=compiler_params,
    )(*args, *rider.inputs)
    return list(res[:n_out]), list(res[n_out:])


def _ride_alone(rider, name):
    return _hosted(lambda: None, rider, name=name, grid=(), out_shape=[], in_specs=[], out_specs=[], scratch_shapes=[],
                   compiler_params=_cparams(), args=[])[1]


def _allreduce_small(packed, rider=None):
    r = packed.shape[0]

    def body(p_ref, sum_ref, all_ref, ssem, rsem):
        x, y, c = _my_pos()
        me = 4 * x + 2 * y + c
        all_ref[me] = p_ref[...]
        cps = []
        for k in range(1, N_DEV):
            peer = (_flip(x, (k >> 2) & 1), _flip(y, (k >> 1) & 1), _flip(c, k & 1))
            cps.append(pltpu.make_async_remote_copy(
                src_ref=all_ref.at[me], dst_ref=all_ref.at[me], send_sem=ssem.at[k - 1], recv_sem=rsem.at[k - 1],
                device_id=peer, device_id_type=MESH))
        for cp in cps:
            cp.start()
        for cp in cps:
            cp.wait_recv()
        acc = all_ref[0]
        for dev in range(1, N_DEV):
            acc = acc + all_ref[dev]
        sum_ref[...] = acc
        for cp in cps:
            cp.wait_send()

    vm = pl.BlockSpec(memory_space=pltpu.VMEM)
    return _hosted(
        body, rider, name="allreduce_small", grid=(),
        out_shape=[jax.ShapeDtypeStruct((r, LANES), F32), jax.ShapeDtypeStruct((N_DEV, r, LANES), F32)],
        in_specs=[vm], out_specs=[vm, vm],
        scratch_shapes=[pltpu.SemaphoreType.DMA((N_DEV - 1,)), pltpu.SemaphoreType.DMA((N_DEV - 1,))],
        compiler_params=_cparams(), args=[packed])


def _rope_rot(t):
    w = t.shape[1]
    lane = lax.broadcasted_iota(jnp.int32, t.shape, 1)
    first = (lane % HEAD_DIM) < (HEAD_DIM // 2)
    return jnp.where(first, pltpu.roll(t, w - HEAD_DIM // 2, 1), pltpu.roll(t, HEAD_DIM // 2, 1))


def _in_proj(x, mod3, g_attn, w_in_t, cos_t, sin_t, seq, rider=None):
    t, d = x.shape
    tm = TOKEN_TILE
    per_seq = seq // tm
    rope_lo, rope_hi = 3 * NA_WIDTH, 3 * NA_WIDTH + SW_WIDTH + SW_KV_WIDTH
    n_rep = (rope_hi - rope_lo) // LANES

    def body(x_ref, mod_ref, g_ref, w_ref, cos_ref, sin_ref, h_ref, p_ref):
        r, xn = _rms_stats(x_ref[...])
        shift, scale = mod_ref[0, :, 0:d], mod_ref[0, :, d:2 * d]
        hb = ((xn * g_ref[...]) * (1.0 + scale) + shift).astype(BF16)
        h_ref[...] = hb
        p_ref[:, :rope_lo] = _mm_nt(hb, w_ref[:rope_lo, :]).astype(BF16)
        pr = _mm_nt(hb, w_ref[rope_lo:rope_hi, :])
        cos = jnp.concatenate([cos_ref[...]] * n_rep, axis=1)
        sin = jnp.concatenate([sin_ref[...]] * n_rep, axis=1)
        p_ref[:, rope_lo:rope_hi] = (pr * cos + _rope_rot(pr) * sin).astype(BF16)
        p_ref[:, rope_hi:] = _mm_nt(hb, w_ref[rope_hi:, :]).astype(BF16)

    return _hosted(
        body, rider, name="in_proj", grid=(t // tm,),
        out_shape=[jax.ShapeDtypeStruct((t, d), BF16), jax.ShapeDtypeStruct((t, IN_WIDTH), BF16)],
        in_specs=[pl.BlockSpec((tm, d), lambda i: (i, 0)),
                  pl.BlockSpec((1, 1, 6 * d), lambda i: (i // per_seq, 0, 0)),
                  pl.BlockSpec((1, d), lambda i: (0, 0)),
                  pl.BlockSpec((IN_WIDTH, d), lambda i: (0, 0)),
                  pl.BlockSpec((tm, LANES), lambda i: (i % per_seq, 0)),
                  pl.BlockSpec((tm, LANES), lambda i: (i % per_seq, 0))],
        out_specs=[pl.BlockSpec((tm, d), lambda i: (i, 0)), pl.BlockSpec((tm, IN_WIDTH), lambda i: (i, 0))],
        scratch_shapes=[], compiler_params=_cparams(("arbitrary",), VMEM_BIG),
        args=[x, mod3, g_attn, w_in_t, cos_t, sin_t])


def _na_bias_pattern():
    n_dc = 2 * NA_COLS - 1
    j = np.arange(GRID_W)[:, None]
    m = np.arange(GRID_W * LANES)[None, :]
    q, lane = m // LANES, m % LANES
    k = lane % GRID_W
    cs = np.clip(q - NA_COLS // 2, 0, GRID_W - NA_COLS)
    ok = (k >= cs) & (k < cs + NA_COLS)
    hit = ok & (j < 2 * n_dc) & (lane // GRID_W == j // n_dc) & (k - q + (NA_COLS - 1) == j % n_dc)
    return jnp.asarray(hit.astype(np.float32)), jnp.asarray(np.where(ok, 0.0, NEG).astype(np.float32))


def _na_bias_tiles(rows2, expand, mask):
    n, width = rows2.shape[0], expand.shape[1]
    q_step = 16
    step = q_step * LANES

    def body(r_ref, e_ref, m_ref, o_ref):
        flat = jnp.dot(r_ref[...], e_ref[...], precision=lax.Precision.HIGHEST,
                       preferred_element_type=F32) + m_ref[...]
        for qq in range(q_step):
            o_ref[:, qq, :] = flat[:, qq * LANES:(qq + 1) * LANES]

    return pl.pallas_call(
        body, name="na_bias_tiles", grid=(width // step,),
        out_shape=jax.ShapeDtypeStruct((n, GRID_W, LANES), F32),
        in_specs=[pl.BlockSpec(rows2.shape, lambda i: (0, 0)), pl.BlockSpec((expand.shape[0], step), lambda i: (0, i)),
                  pl.BlockSpec((1, step), lambda i: (0, i))],
        out_specs=pl.BlockSpec((n, q_step, LANES), lambda i: (0, i, 0)),
        compiler_params=_cparams(("arbitrary",)),
    )(rows2, expand, mask)


def _na_prepare(k_ref, v_ref, km, vm):
    lane = lax.broadcasted_iota(jnp.int32, k_ref.shape, 1)
    low = lane < HEAD_DIM
    kv = k_ref[...]
    vv = v_ref[...]
    zero = jnp.zeros_like(kv)
    km[0] = jnp.where(low, kv, zero)
    km[1] = jnp.where(low, zero, kv)
    vm[0] = jnp.where(low, vv, zero)
    vm[1] = jnp.where(low, zero, vv)


def _na_window(r, n_rows):
    rs = jnp.clip(r - NA_ROWS // 2, 0, n_rows - NA_ROWS)
    return rs, r - rs


def _na_pair_window(ref, wrows):
    return jnp.concatenate([ref[0, wrows, :], ref[1, wrows, :]], axis=0)


def _na_scores(q, k2, tp_ref, off):
    bias = jnp.concatenate([tp_ref[h, 2 * w - off + (NA_ROWS - 1)] for h in range(2) for w in range(NA_ROWS // 2)],
                           axis=1)
    return _mm_nt(q, k2) * QK_SCALE + bias


def _pair_lse_block(lse):
    lane = lax.broadcasted_iota(jnp.int32, (lse[0].shape[0], LANES), 1)
    return jnp.where(lane < HEAD_DIM, lse[0], lse[1])


def _pair_softmax(s):
    win = s.shape[1] // 2
    halves, lse = [], []
    for h in range(2):
        sh = s[:, h * win:(h + 1) * win]
        m = jnp.max(sh, axis=-1, keepdims=True)
        e = jnp.exp(sh - m)
        l = jnp.sum(e, axis=-1, keepdims=True)
        halves.append(e / l)
        lse.append(m + jnp.log(l))
    return jnp.concatenate(halves, axis=1), _pair_lse_block(lse)


def _pair_probs_from_lse(s, lse_block):
    win = s.shape[1] // 2
    return jnp.concatenate([jnp.exp(s[:, h * win:(h + 1) * win] - lse_block[:, h * HEAD_DIM:h * HEAD_DIM + 1])
                            for h in range(2)], axis=1)


def _na_forward(proj, tiles, batch, seq, rider=None):
    t = proj.shape[0]
    n_rows = seq // GRID_W
    n_pairs = NA_WIDTH // LANES
    win = NA_ROWS * GRID_W

    def body(q_ref, k_ref, v_ref, tp_ref, o_ref, lse_ref, km, vm):
        _na_prepare(k_ref, v_ref, km, vm)

        def scores(r):
            rs, off = _na_window(r, n_rows)
            rows = pl.ds(pl.multiple_of(r * GRID_W, GRID_W), GRID_W)
            wrows = pl.ds(pl.multiple_of(rs * GRID_W, GRID_W), win)
            return rows, wrows, _na_scores(q_ref[rows, :], _na_pair_window(km, wrows), tp_ref, off)

        def finish(rows, wrows, s):
            p, lse = _pair_softmax(s)
            lse_ref[rows, :] = lse
            o_ref[rows, :] = _mm(p.astype(BF16), _na_pair_window(vm, wrows))

        def row_group(i, carry):
            for state in [scores(NA_GROUP * i + j) for j in range(NA_GROUP)]:
                finish(*state)
            return carry

        lax.fori_loop(0, n_rows // NA_GROUP, row_group, 0)

    return _hosted(
        body, rider, name="na_forward", grid=(batch, n_pairs),
        out_shape=[jax.ShapeDtypeStruct((t, NA_WIDTH), F32), jax.ShapeDtypeStruct((t, NA_WIDTH), F32)],
        in_specs=[pl.BlockSpec((seq, LANES), lambda b, p: (b, p)),
                  pl.BlockSpec((seq, LANES), lambda b, p: (b, n_pairs + p)),
                  pl.BlockSpec((seq, LANES), lambda b, p: (b, 2 * n_pairs + p)),
                  pl.BlockSpec((2, 2 * NA_ROWS - 2, GRID_W, LANES), lambda b, p: (p, 0, 0, 0))],
        out_specs=[pl.BlockSpec((seq, LANES), lambda b, p: (b, p)), pl.BlockSpec((seq, LANES), lambda b, p: (b, p))],
        scratch_shapes=[pltpu.VMEM((2, seq, LANES), BF16), pltpu.VMEM((2, seq, LANES), BF16)],
        compiler_params=_cparams(("arbitrary", "arbitrary")), args=[proj, proj, proj, tiles])


def _sw_prepare(kv_ref, g, dst_lo, dst_hi, seq):
    lane = lax.broadcasted_iota(jnp.int32, kv_ref.shape, 1)
    mine = (lane // HEAD_DIM) == g
    kg = jnp.where(mine, kv_ref[...].astype(F32), 0.0)
    kr = pltpu.roll(kg, HEAD_DIM, 1)
    first = g == 0
    zero = jnp.zeros((SW_BLOCK, LANES), BF16)
    for dst, val in ((dst_lo, jnp.where(first, kg, kr)), (dst_hi, jnp.where(first, kr, kg))):
        dst[0:SW_BLOCK, :] = zero
        dst[SW_BLOCK:SW_BLOCK + seq, :] = val.astype(BF16)
        dst[SW_BLOCK + seq:, :] = zero


def _sw_mask(n, seq):
    qi = lax.broadcasted_iota(jnp.int32, (SW_BLOCK, 3 * SW_BLOCK), 0)
    kj = lax.broadcasted_iota(jnp.int32, (SW_BLOCK, 3 * SW_BLOCK), 1)
    kpos = n * SW_BLOCK - SW_BLOCK + kj
    return (jnp.abs(qi + SW_BLOCK - kj) <= SW_BLOCK) & (kpos >= 0) & (kpos < seq)


def _sw_probs(s2, ok, sinks):
    band = s2.shape[1] // 2
    halves, lse = [], []
    for i in range(2):
        s = jnp.where(ok, s2[:, i * band:(i + 1) * band], NEG)
        m = jnp.maximum(jnp.max(s, axis=-1, keepdims=True), sinks[i])
        p = jnp.exp(s - m)
        den = jnp.sum(p, axis=-1, keepdims=True) + jnp.exp(sinks[i] - m)
        halves.append(p / den)
        lse.append(m + jnp.log(den))
    return jnp.concatenate(halves, axis=1), _pair_lse_block(lse)


def _sw_probs_from_lse(s2, ok, sinks, lse_block):
    band = s2.shape[1] // 2
    halves, sink_p = [], []
    for i in range(2):
        lse = lse_block[:, i * HEAD_DIM:i * HEAD_DIM + 1]
        halves.append(jnp.exp(jnp.where(ok, s2[:, i * band:(i + 1) * band], NEG) - lse))
        sink_p.append(jnp.exp(sinks[i] - lse))
    return jnp.concatenate(halves, axis=1), sink_p


def _sw_forward(proj, sink, batch, seq, rider=None):
    t = proj.shape[0]
    n_pairs = SW_WIDTH // LANES
    q_blk = 3 * NA_WIDTH // LANES
    k_blk = q_blk + n_pairs
    n_blocks = seq // SW_BLOCK
    pad = seq + 2 * SW_BLOCK

    def body(sink_ref, q_ref, k_ref, v_ref, o_ref, lse_ref, k_lo, k_hi, v_lo, v_hi):
        hp = pl.program_id(1)
        g = hp // 2
        _sw_prepare(k_ref, g, k_lo, k_hi, seq)
        _sw_prepare(v_ref, g, v_lo, v_hi, seq)

        sinks = (sink_ref[2 * hp], sink_ref[2 * hp + 1])

        def scores(n):
            rows = pl.ds(pl.multiple_of(n * SW_BLOCK, SW_BLOCK), SW_BLOCK)
            wrows = pl.ds(pl.multiple_of(n * SW_BLOCK, SW_BLOCK), 3 * SW_BLOCK)
            k2 = jnp.concatenate([k_lo[wrows, :], k_hi[wrows, :]], axis=0)
            return n, rows, wrows, _mm_nt(q_ref[rows, :], k2) * QK_SCALE

        def finish(n, rows, wrows, s2):
            p, lse = _sw_probs(s2, _sw_mask(n, seq), sinks)
            lse_ref[rows, :] = lse
            v2 = jnp.concatenate([v_lo[wrows, :], v_hi[wrows, :]], axis=0)
            o_ref[rows, :] = _mm(p.astype(BF16), v2)

        def block_group(i, carry):
            for state in [scores(SW_GROUP_BLOCKS * i + j) for j in range(SW_GROUP_BLOCKS)]:
                finish(*state)
            return carry

        lax.fori_loop(0, n_blocks // SW_GROUP_BLOCKS, block_group, 0)

    return _hosted(
        body, rider, name="sw_forward", grid=(batch, n_pairs),
        out_shape=[jax.ShapeDtypeStruct((t, SW_WIDTH), F32), jax.ShapeDtypeStruct((t, SW_WIDTH), F32)],
        in_specs=[pl.BlockSpec(memory_space=pltpu.SMEM),
                  pl.BlockSpec((seq, LANES), lambda b, p: (b, q_blk + p)),
                  pl.BlockSpec((seq, LANES), lambda b, p: (b, k_blk)),
                  pl.BlockSpec((seq, LANES), lambda b, p: (b, k_blk + 1))],
        out_specs=[pl.BlockSpec((seq, LANES), lambda b, p: (b, p)), pl.BlockSpec((seq, LANES), lambda b, p: (b, p))],
        scratch_shapes=[pltpu.VMEM((pad, LANES), BF16)] * 4,
        compiler_params=_cparams(("arbitrary", "arbitrary")), args=[sink, proj, proj, proj])


def _out_proj(oa, ob, g_na, g_sw, w_out, x, mod3, g_ffn, seq):
    t, d = x.shape
    tm = TOKEN_TILE
    per_seq = seq // tm

    def body(oa_ref, ob_ref, gna_ref, gsw_ref, w_ref, x_ref, mod_ref, gf_ref, oab_ref, mix_ref, x1_ref, h2_ref):
        _, na = _rms_stats(oa_ref[...])
        _, nb = _rms_stats(ob_ref[...])
        oab = jnp.concatenate([na * gna_ref[...], nb * gsw_ref[...]], axis=1).astype(BF16)
        oab_ref[...] = oab
        mix = _mm(oab, w_ref[...])
        mix_ref[...] = mix
        gate_a = mod_ref[0, :, 2 * d:3 * d]
        shift_f, scale_f = mod_ref[0, :, 3 * d:4 * d], mod_ref[0, :, 4 * d:5 * d]
        x1 = x_ref[...] + gate_a * mix
        x1_ref[...] = x1
        _, xn = _rms_stats(x1)
        h2_ref[...] = ((xn * gf_ref[...]) * (1.0 + scale_f) + shift_f).astype(BF16)

    tile = lambda w: pl.BlockSpec((tm, w), lambda i: (i, 0))
    vec = lambda w: pl.BlockSpec((1, w), lambda i: (0, 0))
    return pl.pallas_call(
        body, name="out_proj", grid=(t // tm,),
        out_shape=(jax.ShapeDtypeStruct((t, d), BF16), jax.ShapeDtypeStruct((t, d), F32),
                   jax.ShapeDtypeStruct((t, d), F32), jax.ShapeDtypeStruct((t, d), BF16)),
        in_specs=[tile(NA_WIDTH), tile(SW_WIDTH), vec(NA_WIDTH), vec(SW_WIDTH),
                  pl.BlockSpec((d, d), lambda i: (0, 0)), tile(d),
                  pl.BlockSpec((1, 1, 6 * d), lambda i: (i // per_seq, 0, 0)), vec(d)],
        out_specs=(tile(d), tile(d), tile(d), tile(d)),
        compiler_params=_cparams(("arbitrary",), VMEM_BIG),
    )(oa, ob, g_na, g_sw, w_out, x, mod3, g_ffn)


def _up_proj(h2, w_up_halves, rider=None):
    t, d = h2.shape
    tm = TOKEN_TILE
    w_a, w_b = w_up_halves
    half, wcol = w_a.shape[1], w_a.shape[2]

    def body(h_ref, wa_ref, wb_ref, u_ref):
        u_ref[0] = (_mm(h_ref[:, :half], wa_ref[0]) + _mm(h_ref[:, half:], wb_ref[0])).astype(BF16)

    w_spec = pl.BlockSpec((1, half, wcol), lambda j, i: (j, 0, 0))
    return _hosted(
        body, rider, name="up_proj", grid=(N_SHARD, t // tm),
        out_shape=[jax.ShapeDtypeStruct((2, t, D_FF), BF16)],
        in_specs=[pl.BlockSpec((tm, d), lambda j, i: (i, 0)), w_spec, w_spec],
        out_specs=[pl.BlockSpec((1, tm, wcol), lambda j, i: (j // 2, i, j % 2))],
        scratch_shapes=[], compiler_params=_cparams(("arbitrary", "arbitrary"), VMEM_BIG), args=[h2, w_a, w_b])


def _taps_chunk(load, s, rows, seq):
    halo = 2 * SUBLANES
    cur = load(s, rows)
    above = load(pl.multiple_of(jnp.maximum(s - halo, 0), halo), halo)
    below = load(pl.multiple_of(jnp.minimum(s + rows, seq - halo), halo), halo)
    up = jnp.where(s > 0, above[halo - 1:halo, :], 0.0)
    dn = jnp.where(s + rows < seq, below[0:1, :], 0.0)
    row = lax.broadcasted_iota(jnp.int32, cur.shape, 0)
    prev = jnp.where(row == 0, up, pltpu.roll(cur, 1, 0))
    nxt = jnp.where(row == rows - 1, dn, pltpu.roll(cur, rows - 1, 0))
    return cur, prev, nxt


def _conv_gate(u, conv_w, conv_b, batch, seq):
    t = u.shape[1]
    cw = FF_TILE
    rows = CONV_CHUNK

    def body(u_ref, w_ref, b_ref, a_ref):
        def chunk(i, carry):
            s = pl.multiple_of(i * rows, rows)
            gt, prev, nxt = _taps_chunk(lambda at, n: u_ref[1, pl.ds(at, n), :].astype(F32), s, rows, seq)
            gc = prev * w_ref[0:1, :] + gt * w_ref[1:2, :] + nxt * w_ref[2:3, :] + b_ref[...]
            a_ref[pl.ds(s, rows), :] = ((gc * _sigmoid(gc)) * u_ref[0, pl.ds(s, rows), :].astype(F32)).astype(BF16)
            return carry

        lax.fori_loop(0, seq // rows, chunk, 0)

    return pl.pallas_call(
        body, name="conv_gate", grid=(batch, D_FF // cw),
        out_shape=jax.ShapeDtypeStruct((t, D_FF), BF16),
        in_specs=[pl.BlockSpec((2, seq, cw), lambda b, j: (0, b, j)),
                  pl.BlockSpec((3, cw), lambda b, j: (0, j)), pl.BlockSpec((1, cw), lambda b, j: (0, j))],
        out_specs=pl.BlockSpec((seq, cw), lambda b, j: (b, j)),
        compiler_params=_cparams(("arbitrary", "arbitrary"), VMEM_BIG),
    )(u, conv_w, conv_b)


def _down_and_loss(a, w_down, x1, mod3, g_final, target, seq):
    t, d = x1.shape
    tm = TOKEN_TILE
    per_seq = seq // tm
    batch = t // seq

    def body(a_ref, w_ref, x1_ref, mod_ref, g_ref, tgt_ref, dx2_ref, dffn_ref, loss_ref, dgate_ref, dg_ref):
        i = pl.program_id(0)
        f = _mm(a_ref[...], w_ref[...])
        gate_f = mod_ref[0, :, 5 * d:6 * d]
        x2 = x1_ref[...] + gate_f * f
        r, xn = _rms_stats(x2)
        err = xn * g_ref[...] - tgt_ref[...]
        part = 0.5 * jnp.sum(jnp.mean(err * err, axis=-1, keepdims=True))
        dy = err / d
        dx2 = _rms_bwd(dy * g_ref[...], xn, r)
        dx2_ref[...] = dx2
        dffn_ref[...] = (dx2 * gate_f).astype(BF16)

        @pl.when(i == 0)
        def _():
            loss_ref[...] = jnp.zeros_like(loss_ref)
            dg_ref[...] = jnp.zeros_like(dg_ref)

        @pl.when(i % per_seq == 0)
        def _():
            dgate_ref[...] = jnp.zeros_like(dgate_ref)

        loss_ref[...] += part
        dg_ref[...] += jnp.sum(dy * xn, axis=0, keepdims=True)
        dgate_ref[0] += jnp.sum(dx2 * f, axis=0, keepdims=True)

    tile = lambda w: pl.BlockSpec((tm, w), lambda i: (i, 0))
    return pl.pallas_call(
        body, name="down_loss", grid=(t // tm,),
        out_shape=(jax.ShapeDtypeStruct((t, d), F32), jax.ShapeDtypeStruct((t, d), BF16),
                   jax.ShapeDtypeStruct((SUBLANES, LANES), F32), jax.ShapeDtypeStruct((batch, 1, d), F32),
                   jax.ShapeDtypeStruct((1, d), F32)),
        in_specs=[tile(D_FF), _resident((D_FF, d)), tile(d),
                  pl.BlockSpec((1, 1, 6 * d), lambda i: (i // per_seq, 0, 0)),
                  pl.BlockSpec((1, d), lambda i: (0, 0)), tile(d)],
        out_specs=(tile(d), tile(d), pl.BlockSpec((SUBLANES, LANES), lambda i: (0, 0)),
                   pl.BlockSpec((1, 1, d), lambda i: (i // per_seq, 0, 0)), pl.BlockSpec((1, d), lambda i: (0, 0))),
        compiler_params=_cparams(("arbitrary",), VMEM_BIG),
    )(a, w_down, x1, mod3, g_final, target)


def _down_weight_grad(a, dffn):
    t, dff = a.shape
    d = dffn.shape[1]
    tk = TOKEN_TILE
    n_k = t // tk

    def body(a_ref, df_ref, g_ref, gb_ref):
        k = pl.program_id(0)

        @pl.when(k == 0)
        def _():
            g_ref[...] = jnp.zeros_like(g_ref)

        g_ref[...] += _mm_tn(a_ref[...], df_ref[...])

        @pl.when(k == n_k - 1)
        def _():
            gb_ref[...] = g_ref[...].astype(BF16)

    whole = pl.BlockSpec((dff, d), lambda k: (0, 0))
    return pl.pallas_call(
        body, name="down_weight_grad", grid=(n_k,),
        out_shape=(jax.ShapeDtypeStruct((dff, d), F32), jax.ShapeDtypeStruct((dff, d), BF16)),
        in_specs=[pl.BlockSpec((tk, dff), lambda k: (k, 0)), pl.BlockSpec((tk, d), lambda k: (k, 0))],
        out_specs=(whole, whole),
        compiler_params=_cparams(("arbitrary",), VMEM_BIG),
    )(a, dffn)


def _ffn_backward(dffn, w_down, u, conv_w, conv_b, batch, seq, rider=None):
    t, d = dffn.shape
    cw = FF_TILE
    rows = CONV_CHUNK

    def body(df_ref, wd_ref, u_ref, w_ref, b_ref, du_ref, gcw_ref, gcb_ref, da_scr, dgc_scr):
        b = pl.program_id(1)
        da_scr[...] = _mm_nt(df_ref[...], wd_ref[...])

        @pl.when(b == 0)
        def _():
            gcw_ref[...] = jnp.zeros_like(gcw_ref)
            gcb_ref[...] = jnp.zeros_like(gcb_ref)

        def fold(v):
            return jnp.sum(v.reshape(rows // SUBLANES, SUBLANES, cw), axis=0)

        def chunk(i, carry):
            s = pl.multiple_of(i * rows, rows)
            here = pl.ds(s, rows)
            gt, prev, nxt = _taps_chunk(lambda at, n: u_ref[1, pl.ds(at, n), :].astype(F32), s, rows, seq)
            val, da = u_ref[0, here, :].astype(F32), da_scr[here, :]
            gc = prev * w_ref[0:1, :] + gt * w_ref[1:2, :] + nxt * w_ref[2:3, :] + b_ref[...]
            sg = _sigmoid(gc)
            sl = gc * sg
            du_ref[0, here, :] = (da * sl).astype(BF16)
            dgc = (da * val) * (sg * (1.0 + gc * (1.0 - sg)))
            dgc_scr[here, :] = dgc
            cb, c0, c1, c2 = carry
            return cb + fold(dgc), c0 + fold(dgc * prev), c1 + fold(dgc * gt), c2 + fold(dgc * nxt)

        zero = jnp.zeros((SUBLANES, cw), F32)
        cb, c0, c1, c2 = lax.fori_loop(0, seq // rows, chunk, (zero, zero, zero, zero))
        gcb_ref[...] += jnp.sum(cb, axis=0, keepdims=True)
        gcw_ref[0:1, :] += jnp.sum(c0, axis=0, keepdims=True)
        gcw_ref[1:2, :] += jnp.sum(c1, axis=0, keepdims=True)
        gcw_ref[2:3, :] += jnp.sum(c2, axis=0, keepdims=True)

        def chunk2(i, carry):
            s = pl.multiple_of(i * rows, rows)
            dgc, dprev, dnxt = _taps_chunk(lambda at, n: dgc_scr[pl.ds(at, n), :], s, rows, seq)
            du_ref[1, pl.ds(s, rows), :] = (dnxt * w_ref[0:1, :] + dgc * w_ref[1:2, :]
                                            + dprev * w_ref[2:3, :]).astype(BF16)
            return carry

        lax.fori_loop(0, seq // rows, chunk2, 0)

    return _hosted(
        body, rider, name="ffn_backward", grid=(D_FF // cw, batch),
        out_shape=[jax.ShapeDtypeStruct((2, t, D_FF), BF16),
                   jax.ShapeDtypeStruct((3, D_FF), F32), jax.ShapeDtypeStruct((1, D_FF), F32)],
        in_specs=[pl.BlockSpec((seq, d), lambda j, b: (b, 0)), pl.BlockSpec((cw, d), lambda j, b: (j, 0)),
                  pl.BlockSpec((2, seq, cw), lambda j, b: (0, b, j)),
                  pl.BlockSpec((3, cw), lambda j, b: (0, j)), pl.BlockSpec((1, cw), lambda j, b: (0, j))],
        out_specs=[pl.BlockSpec((2, seq, cw), lambda j, b: (0, b, j)),
                   pl.BlockSpec((3, cw), lambda j, b: (0, j)), pl.BlockSpec((1, cw), lambda j, b: (0, j))],
        scratch_shapes=[pltpu.VMEM((seq, cw), F32), pltpu.VMEM((seq, cw), F32)],
        compiler_params=_cparams(("arbitrary", "arbitrary"), VMEM_BIG), args=[dffn, w_down, u, conv_w, conv_b])


def _up_backward(du, w_up, x1, mod3, g_ffn, dx2, mix, seq, rider=None):
    _, t, _ = du.shape
    d = x1.shape[1]
    tm = TOKEN_TILE
    per_seq = seq // tm
    batch = t // seq
    w_a, w_b = w_up
    half, wcol = w_a.shape[1], w_a.shape[2]

    def body(du_ref, wa_ref, wb_ref, x1_ref, mod_ref, g_ref, dx2_ref, mix_ref,
             dx1_ref, dmix_ref, dsh_ref, dsc_ref, dga_ref, dg_ref):
        i = pl.program_id(0)
        parts = []
        for w_ref in (wa_ref, wb_ref):
            acc = jnp.zeros((tm, half), F32)
            for j in range(N_SHARD):
                acc = acc + _mm_nt(du_ref[j // 2, :, (j % 2) * wcol:(j % 2 + 1) * wcol], w_ref[j])
            parts.append(acc)
        dh = jnp.concatenate(parts, axis=1)
        gate_a = mod_ref[0, :, 2 * d:3 * d]
        scale_f = mod_ref[0, :, 4 * d:5 * d]
        r, xn = _rms_stats(x1_ref[...])
        xg = xn * g_ref[...]
        dxg = dh * (1.0 + scale_f)
        dx1 = dx2_ref[...] + _rms_bwd(dxg * g_ref[...], xn, r)
        dx1_ref[...] = dx1
        dmix_ref[...] = (dx1 * gate_a).astype(BF16)

        @pl.when(i == 0)
        def _():
            dg_ref[...] = jnp.zeros_like(dg_ref)

        @pl.when(i % per_seq == 0)
        def _():
            dsh_ref[...] = jnp.zeros_like(dsh_ref)
            dsc_ref[...] = jnp.zeros_like(dsc_ref)
            dga_ref[...] = jnp.zeros_like(dga_ref)

        dg_ref[...] += jnp.sum(dxg * xn, axis=0, keepdims=True)
        dsh_ref[0] += jnp.sum(dh, axis=0, keepdims=True)
        dsc_ref[0] += jnp.sum(dh * xg, axis=0, keepdims=True)
        dga_ref[0] += jnp.sum(dx1 * mix_ref[...], axis=0, keepdims=True)

    tile = lambda w: pl.BlockSpec((tm, w), lambda i: (i, 0))
    per_b = pl.BlockSpec((1, 1, d), lambda i: (i // per_seq, 0, 0))
    small = jax.ShapeDtypeStruct((batch, 1, d), F32)
    return _hosted(
        body, rider, name="up_backward", grid=(t // tm,),
        out_shape=[jax.ShapeDtypeStruct((t, d), F32), jax.ShapeDtypeStruct((t, d), BF16), small, small, small,
                   jax.ShapeDtypeStruct((1, d), F32)],
        in_specs=[pl.BlockSpec((2, tm, D_FF), lambda i: (0, i, 0)),
                  _resident((N_SHARD, half, wcol)), _resident((N_SHARD, half, wcol)), tile(d),
                  pl.BlockSpec((1, 1, 6 * d), lambda i: (i // per_seq, 0, 0)),
                  pl.BlockSpec((1, d), lambda i: (0, 0)), tile(d), tile(d)],
        out_specs=[tile(d), tile(d), per_b, per_b, per_b, pl.BlockSpec((1, d), lambda i: (0, 0))],
        scratch_shapes=[], compiler_params=_cparams(("arbitrary",), VMEM_BIG),
        args=[du, w_a, w_b, x1, mod3, g_ffn, dx2, mix])


def _up_weight_grad(h2, du, rider=None):
    t, d = h2.shape
    tk = TOKEN_TILE
    wcol = D_FF // 2
    half = d // 2
    n_k = t // tk

    def body(h_ref, du_ref, ga_ref, gb_ref, ga16_ref, gb16_ref):
        k = pl.program_id(1)

        @pl.when(k == 0)
        def _():
            ga_ref[...] = jnp.zeros_like(ga_ref)
            gb_ref[...] = jnp.zeros_like(gb_ref)

        du = du_ref[0]
        ga_ref[0] += _mm_tn(h_ref[:, :half], du)
        gb_ref[0] += _mm_tn(h_ref[:, half:], du)

        @pl.when(k == n_k - 1)
        def _():
            ga16_ref[...] = ga_ref[...].astype(BF16)
            gb16_ref[...] = gb_ref[...].astype(BF16)

    g_spec = pl.BlockSpec((1, half, wcol), lambda j, k: (j, 0, 0))
    f32_out = jax.ShapeDtypeStruct((N_SHARD, half, wcol), F32)
    b16_out = jax.ShapeDtypeStruct((N_SHARD, half, wcol), BF16)
    return _hosted(
        body, rider, name="up_weight_grad", grid=(N_SHARD, n_k),
        out_shape=[f32_out, f32_out, b16_out, b16_out],
        in_specs=[pl.BlockSpec((tk, d), lambda j, k: (k, 0)),
                  pl.BlockSpec((1, tk, wcol), lambda j, k: (j // 2, k, j % 2))],
        out_specs=[g_spec, g_spec, g_spec, g_spec], scratch_shapes=[],
        compiler_params=_cparams(("arbitrary", "arbitrary"), VMEM_BIG), args=[h2, du])


def _out_backward(dmix, w_out, oab, oa, ob, g_na, g_sw):
    t, d = dmix.shape
    tm = TOKEN_TILE
    hw = NA_WIDTH

    def body(dm_ref, w_ref, oab_ref, oa_ref, ob_ref, gna_ref, gsw_ref,
             doa_ref, dob_ref, gw_ref, gwb_ref, dgna_ref, dgsw_ref):
        @pl.when(pl.program_id(0) == 0)
        def _():
            gw_ref[...] = jnp.zeros_like(gw_ref)
            dgna_ref[...] = jnp.zeros_like(dgna_ref)
            dgsw_ref[...] = jnp.zeros_like(dgsw_ref)

        dm = dm_ref[...]
        gw_ref[...] += _mm_tn(oab_ref[...], dm)

        @pl.when(pl.program_id(0) == t // tm - 1)
        def _():
            gwb_ref[...] = gw_ref[...].astype(BF16)

        do = _mm_nt(dm, w_ref[...])
        for raw_ref, g_ref, dst_ref, dg_ref, lo in ((oa_ref, gna_ref, doa_ref, dgna_ref, 0),
                                                     (ob_ref, gsw_ref, dob_ref, dgsw_ref, hw)):
            r, xn = _rms_stats(raw_ref[...])
            dpart = do[:, lo:lo + hw]
            dg_ref[...] += jnp.sum(dpart * xn, axis=0, keepdims=True)
            dst_ref[...] = _rms_bwd(dpart * g_ref[...], xn, r).astype(BF16)

    tile = lambda w: pl.BlockSpec((tm, w), lambda i: (i, 0))
    vec = lambda w: pl.BlockSpec((1, w), lambda i: (0, 0))
    return pl.pallas_call(
        body, name="out_backward", grid=(t // tm,),
        out_shape=(jax.ShapeDtypeStruct((t, hw), BF16), jax.ShapeDtypeStruct((t, hw), BF16),
                   jax.ShapeDtypeStruct((d, d), F32), jax.ShapeDtypeStruct((d, d), BF16),
                   jax.ShapeDtypeStruct((1, hw), F32), jax.ShapeDtypeStruct((1, hw), F32)),
        in_specs=[tile(d), pl.BlockSpec((d, d), lambda i: (0, 0)), tile(d), tile(hw), tile(hw), vec(hw), vec(hw)],
        out_specs=(tile(hw), tile(hw), pl.BlockSpec((d, d), lambda i: (0, 0)), pl.BlockSpec((d, d), lambda i: (0, 0)),
                   vec(hw), vec(hw)),
        compiler_params=_cparams(("arbitrary",), VMEM_BIG),
    )(dmix, w_out, oab, oa, ob, g_na, g_sw)


def _na_backward(proj, d_o, lse, tiles, batch, seq, rider=None):
    t = proj.shape[0]
    n_rows = seq // GRID_W
    n_pairs = NA_WIDTH // LANES
    win = NA_ROWS * GRID_W
    n_tiles = 2 * NA_ROWS - 2

    def body(q_ref, k_ref, v_ref, do_ref, lse_ref, tp_ref, dq_ref, dk_ref, dv_ref, dtp_ref, km, vm, dk_acc, dv_acc):
        @pl.when(pl.program_id(1) == 0)
        def _():
            dtp_ref[...] = jnp.zeros_like(dtp_ref)

        _na_prepare(k_ref, v_ref, km, vm)
        dk_acc[...] = jnp.zeros_like(dk_acc)
        dv_acc[...] = jnp.zeros_like(dv_acc)
        low = lax.broadcasted_iota(jnp.int32, (win, LANES), 1) < HEAD_DIM

        def scores(r):
            rs, off = _na_window(r, n_rows)
            rows = pl.ds(pl.multiple_of(r * GRID_W, GRID_W), GRID_W)
            wrows = pl.ds(pl.multiple_of(rs * GRID_W, GRID_W), win)
            q, do = q_ref[rows, :], do_ref[rows, :]
            k2 = _na_pair_window(km, wrows)
            s = _na_scores(q, k2, tp_ref, off)
            dp = _mm_nt(do, _na_pair_window(vm, wrows))
            return rows, wrows, off, q, do, k2, s, dp

        def finish(rows, wrows, off, q, do, k2, s, dp):
            p = _pair_probs_from_lse(s, lse_ref[rows, :])
            parts = []
            for h in range(2):
                ph, dph = p[:, h * win:(h + 1) * win], dp[:, h * win:(h + 1) * win]
                dsh = ph * (dph - jnp.sum(ph * dph, axis=-1, keepdims=True))
                for w in range(NA_ROWS // 2):
                    dtp_ref[h, 2 * w - off + (NA_ROWS - 1)] += dsh[:, w * LANES:(w + 1) * LANES]
                parts.append(dsh)
            dsb = (jnp.concatenate(parts, axis=1) * QK_SCALE).astype(BF16)
            dq_ref[rows, :] = _mm(dsb, k2).astype(BF16)
            dk2 = _mm_tn(dsb, q)
            dv2 = _mm_tn(p.astype(BF16), do)
            dk_acc[wrows, :] += jnp.where(low, dk2[:win], dk2[win:])
            dv_acc[wrows, :] += jnp.where(low, dv2[:win], dv2[win:])

        def row_group(i, carry):
            for state in [scores(NA_GROUP * i + j) for j in range(NA_GROUP)]:
                finish(*state)
            return carry

        lax.fori_loop(0, n_rows // NA_GROUP, row_group, 0)
        dk_ref[...] = dk_acc[...].astype(BF16)
        dv_ref[...] = dv_acc[...].astype(BF16)

    blk = lambda off: pl.BlockSpec((seq, LANES), lambda p, b: (b, off + p))
    out = jax.ShapeDtypeStruct((t, NA_WIDTH), BF16)
    return _hosted(
        body, rider, name="na_backward", grid=(n_pairs, batch),
        out_shape=[out, out, out, jax.ShapeDtypeStruct(tiles.shape, F32)],
        in_specs=[blk(0), blk(n_pairs), blk(2 * n_pairs), blk(0), blk(0),
                  pl.BlockSpec((2, n_tiles, GRID_W, LANES), lambda p, b: (p, 0, 0, 0))],
        out_specs=[blk(0), blk(0), blk(0), pl.BlockSpec((2, n_tiles, GRID_W, LANES), lambda p, b: (p, 0, 0, 0))],
        scratch_shapes=[pltpu.VMEM((2, seq, LANES), BF16), pltpu.VMEM((2, seq, LANES), BF16),
                        pltpu.VMEM((seq, LANES), F32), pltpu.VMEM((seq, LANES), F32)],
        compiler_params=_cparams(("arbitrary", "arbitrary")), args=[proj, proj, proj, d_o, lse, tiles])


def _na_bias_grad(dtiles, expand):
    n = dtiles.shape[0]

    def body(t_ref, e_ref, o_ref):
        flat = jnp.concatenate([t_ref[:, qq, :] for qq in range(GRID_W)], axis=1)
        o_ref[...] = lax.dot_general(flat, e_ref[...], (((1,), (1,)), ((), ())),
                                     precision=lax.Precision.HIGHEST, preferred_element_type=F32)

    return pl.pallas_call(
        body, name="na_bias_grad",
        out_shape=jax.ShapeDtypeStruct((n, expand.shape[0]), F32),
        compiler_params=_cparams(vmem=VMEM_BIG),
    )(dtiles, expand)


def _sw_backward(proj, d_o, lse, sink, batch, seq, rider=None):
    t = proj.shape[0]
    n_pairs = SW_WIDTH // LANES
    q_blk = 3 * NA_WIDTH // LANES
    k_blk = q_blk + n_pairs
    n_blocks = seq // SW_BLOCK
    pad = seq + 2 * SW_BLOCK

    def body(sink_ref, q_ref, k_ref, v_ref, do_ref, lse_ref, dq_ref, dk_ref, dv_ref, dsk_ref,
             k_lo, k_hi, v_lo, v_hi, dk_loc, dv_loc, dk_tot, dv_tot):
        hp = pl.program_id(1)
        g = hp // 2
        _sw_prepare(k_ref, g, k_lo, k_hi, seq)
        _sw_prepare(v_ref, g, v_lo, v_hi, seq)
        dk_loc[...] = jnp.zeros_like(dk_loc)
        dv_loc[...] = jnp.zeros_like(dv_loc)

        @pl.when(hp == 0)
        def _():
            dk_tot[...] = jnp.zeros_like(dk_tot)
            dv_tot[...] = jnp.zeros_like(dv_tot)

        band = 3 * SW_BLOCK
        low = lax.broadcasted_iota(jnp.int32, (band, LANES), 1) < HEAD_DIM

        sinks = (sink_ref[2 * hp], sink_ref[2 * hp + 1])

        def scores(n):
            rows = pl.ds(pl.multiple_of(n * SW_BLOCK, SW_BLOCK), SW_BLOCK)
            wrows = pl.ds(pl.multiple_of(n * SW_BLOCK, SW_BLOCK), band)
            qb, do = q_ref[rows, :], do_ref[rows, :]
            k2 = jnp.concatenate([k_lo[wrows, :], k_hi[wrows, :]], axis=0)
            v2 = jnp.concatenate([v_lo[wrows, :], v_hi[wrows, :]], axis=0)
            return n, rows, wrows, qb, do, k2, _mm_nt(qb, k2) * QK_SCALE, _mm_nt(do, v2)

        def finish(sink_acc, n, rows, wrows, qb, do, k2, s2, dp):
            p, ps = _sw_probs_from_lse(s2, _sw_mask(n, seq), sinks, lse_ref[rows, :])
            parts, new = [], []
            for i in range(2):
                ph, dph = p[:, i * band:(i + 1) * band], dp[:, i * band:(i + 1) * band]
                delta = jnp.sum(ph * dph, axis=-1, keepdims=True)
                parts.append(ph * (dph - delta))
                new.append(sink_acc[i] - ps[i] * delta)
            dsb = (jnp.concatenate(parts, axis=1) * QK_SCALE).astype(BF16)
            dq_ref[rows, :] = _mm(dsb, k2)
            dk2 = _mm_tn(dsb, qb)
            dv2 = _mm_tn(p.astype(BF16), do)
            dk_loc[wrows, :] += jnp.where(low, dk2[:band], dk2[band:])
            dv_loc[wrows, :] += jnp.where(low, dv2[:band], dv2[band:])
            return tuple(new)

        def block_group(i, carry):
            for state in [scores(SW_GROUP_BLOCKS * i + j) for j in range(SW_GROUP_BLOCKS)]:
                carry = finish(carry, *state)
            return carry

        zero = jnp.zeros((SW_BLOCK, 1), F32)
        s0, s1 = lax.fori_loop(0, n_blocks // SW_GROUP_BLOCKS, block_group, (zero, zero))
        row = lax.broadcasted_iota(jnp.int32, (SUBLANES, LANES), 0)
        dsk_ref[0, 0] = jnp.where(row == 0, jnp.sum(s0), jnp.where(row == 1, jnp.sum(s1), 0.0))

        lane_s = lax.broadcasted_iota(jnp.int32, (seq, LANES), 1)
        mine_g = (lane_s // HEAD_DIM) == g
        for loc, tot in ((dk_loc, dk_tot), (dv_loc, dv_tot)):
            part = loc[SW_BLOCK:SW_BLOCK + seq, :]
            tot[...] += jnp.where(mine_g, part + pltpu.roll(part, HEAD_DIM, 1), 0.0)

        @pl.when(hp == n_pairs - 1)
        def _():
            dk_ref[...] = dk_tot[...]
            dv_ref[...] = dv_tot[...].astype(BF16)

    return _hosted(
        body, rider, name="sw_backward", grid=(batch, n_pairs),
        out_shape=[jax.ShapeDtypeStruct((t, SW_WIDTH), F32), jax.ShapeDtypeStruct((t, LANES), F32),
                   jax.ShapeDtypeStruct((t, LANES), BF16), jax.ShapeDtypeStruct((batch, n_pairs, SUBLANES, LANES), F32)],
        in_specs=[pl.BlockSpec(memory_space=pltpu.SMEM),
                  pl.BlockSpec((seq, LANES), lambda b, p: (b, q_blk + p)),
                  pl.BlockSpec((seq, LANES), lambda b, p: (b, k_blk)),
                  pl.BlockSpec((seq, LANES), lambda b, p: (b, k_blk + 1)),
                  pl.BlockSpec((seq, LANES), lambda b, p: (b, p)), pl.BlockSpec((seq, LANES), lambda b, p: (b, p))],
        out_specs=[pl.BlockSpec((seq, LANES), lambda b, p: (b, p)), pl.BlockSpec((seq, LANES), lambda b, p: (b, 0)),
                   pl.BlockSpec((seq, LANES), lambda b, p: (b, 0)),
                   pl.BlockSpec((1, 1, SUBLANES, LANES), lambda b, p: (b, p, 0, 0))],
        scratch_shapes=[pltpu.VMEM((pad, LANES), BF16)] * 4 + [pltpu.VMEM((pad, LANES), F32)] * 2
        + [pltpu.VMEM((seq, LANES), F32)] * 2,
        compiler_params=_cparams(("arbitrary", "arbitrary")), args=[sink, proj, proj, proj, d_o, lse])


def _in_backward(dqkv_a, dq_b, dk_b, dv_b, w_in_t, h1, x, mod3, g_attn, dx1, cos_t, sin_t, seq):
    t, d = x.shape
    tm = TOKEN_TILE
    per_seq = seq // tm
    batch = t // seq
    dqa, dka, dva = dqkv_a
    n_q = SW_WIDTH // LANES

    def body(dqa_ref, dka_ref, dva_ref, dqb_ref, dkb_ref, dvb_ref, w_ref, h_ref, x_ref, mod_ref, g_ref, dx1_ref,
             cos_ref, sin_ref, dx_ref, gw_ref, gwb_ref, dsh_ref, dsc_ref, dg_ref):
        i = pl.program_id(0)

        @pl.when(i == 0)
        def _():
            gw_ref[...] = jnp.zeros_like(gw_ref)
            dg_ref[...] = jnp.zeros_like(dg_ref)

        @pl.when(i % per_seq == 0)
        def _():
            dsh_ref[...] = jnp.zeros_like(dsh_ref)
            dsc_ref[...] = jnp.zeros_like(dsc_ref)

        dr = jnp.concatenate([dqb_ref[...], dkb_ref[...]], axis=1)
        cos = jnp.concatenate([cos_ref[...]] * (n_q + 1), axis=1)
        sin = jnp.concatenate([sin_ref[...]] * (n_q + 1), axis=1)
        dr = dr * cos + _rope_rot(dr * sin)
        dproj = jnp.concatenate([dqa_ref[...], dka_ref[...], dva_ref[...], dr.astype(BF16), dvb_ref[...]], axis=1)
        gw_ref[...] += _mm_tn(dproj, h_ref[...])

        @pl.when(i == t // tm - 1)
        def _():
            gwb_ref[...] = gw_ref[...].astype(BF16)

        dh = _mm(dproj, w_ref[...])
        scale = mod_ref[0, :, d:2 * d]
        r, xn = _rms_stats(x_ref[...])
        xg = xn * g_ref[...]
        dxg = dh * (1.0 + scale)
        dx_ref[...] = dx1_ref[...] + _rms_bwd(dxg * g_ref[...], xn, r)
        dg_ref[...] += jnp.sum(dxg * xn, axis=0, keepdims=True)
        dsh_ref[0] += jnp.sum(dh, axis=0, keepdims=True)
        dsc_ref[0] += jnp.sum(dh * xg, axis=0, keepdims=True)

    tile = lambda w: pl.BlockSpec((tm, w), lambda i: (i, 0))
    per_b = pl.BlockSpec((1, 1, d), lambda i: (i // per_seq, 0, 0))
    small = jax.ShapeDtypeStruct((batch, 1, d), F32)
    rope = pl.BlockSpec((tm, LANES), lambda i: (i % per_seq, 0))
    return pl.pallas_call(
        body, name="in_backward", grid=(t // tm,),
        out_shape=(jax.ShapeDtypeStruct((t, d), F32), jax.ShapeDtypeStruct((IN_WIDTH, d), F32),
                   jax.ShapeDtypeStruct((IN_WIDTH, d), BF16), small, small, jax.ShapeDtypeStruct((1, d), F32)),
        in_specs=[tile(NA_WIDTH), tile(NA_WIDTH), tile(NA_WIDTH), tile(SW_WIDTH), tile(LANES), tile(LANES),
                  _resident((IN_WIDTH, d)), tile(d), tile(d),
                  pl.BlockSpec((1, 1, 6 * d), lambda i: (i // per_seq, 0, 0)),
                  pl.BlockSpec((1, d), lambda i: (0, 0)), tile(d), rope, rope],
        out_specs=(tile(d), _resident((IN_WIDTH, d)), _resident((IN_WIDTH, d)),
                   per_b, per_b, pl.BlockSpec((1, d), lambda i: (0, 0))),
        compiler_params=_cparams(("arbitrary",), VMEM_BIG),
    )(dqa, dka, dva, dq_b, dk_b, dv_b, w_in_t, h1, x, mod3, g_attn, dx1, cos_t, sin_t)


def _ada_weight_grad(sc_all, dmod_cols):
    d = sc_all.shape[1]
    ncol = dmod_cols.shape[1]

    def body(s_ref, m_ref, o_ref):
        o_ref[...] = _mm_tn(s_ref[...].astype(BF16), m_ref[...].astype(BF16))

    return pl.pallas_call(
        body, name="ada_weight_grad",
        out_shape=jax.ShapeDtypeStruct((d, ncol), F32),
        compiler_params=_cparams(vmem=VMEM_BIG),
    )(sc_all, dmod_cols)


def _row_tile(rows, cols):
    target = max(SUBLANES, (1 << 20) // (4 * cols))
    best = rows
    for cand in range(SUBLANES, rows + 1, SUBLANES):
        if rows % cand == 0 and cand <= target:
            best = cand
    return best if rows % SUBLANES == 0 else rows


def _sum_slots(parts, name):
    n = len(parts)
    _, rows, cols = parts[0][0].shape
    tr = _row_tile(rows, cols)
    per = rows // tr

    def body(*refs):
        o_ref = refs[-1]
        for q in range(n):
            @pl.when(pl.program_id(0) == q)
            def _(q=q):
                p_ref, own_ref = refs[2 * q], refs[2 * q + 1]
                o_ref[...] = ((own_ref[...] + p_ref[0].astype(F32)) + p_ref[1].astype(F32)) + p_ref[2].astype(F32)

    in_specs, args = [], []
    for q, (recv, own) in enumerate(parts):
        in_specs.append(pl.BlockSpec((N_SHARD - 1, tr, cols), lambda p, i, q=q: (0, jnp.where(p == q, i, 0), 0)))
        in_specs.append(pl.BlockSpec((tr, cols), lambda p, i, q=q: (jnp.where(p == q, i, 0), 0)))
        args += [recv, own]
    return pl.pallas_call(
        body, name=name, grid=(n, per),
        out_shape=jax.ShapeDtypeStruct((n * rows, cols), F32),
        in_specs=in_specs, out_specs=pl.BlockSpec((tr, cols), lambda p, i: (p * per + i, 0)),
        compiler_params=_cparams(("arbitrary", "arbitrary")),
    )(*args)


def _adamw_math(w, g, m, v):
    m2 = ADAM_B1 * m + (1.0 - ADAM_B1) * g
    v2 = ADAM_B2 * v + (1.0 - ADAM_B2) * (g * g)
    m_hat = m2 / (1.0 - ADAM_B1 ** ADAM_STEP)
    v_hat = v2 / (1.0 - ADAM_B2 ** ADAM_STEP)
    return -ADAM_LR * (m_hat / (jnp.sqrt(v_hat) + ADAM_EPS) + ADAM_WD * w), m2, v2


def _small_step(partials, states, dmod, b_ada_state, rider=None):
    n_upd = len(states)
    moving = list(partials) + [dmod]
    n_mov = len(moving)
    all_states = list(states) + [b_ada_state]

    def body(*refs):
        mov, refs = refs[:n_mov], refs[n_mov:]
        wmv, refs = refs[:3 * (n_upd + 1)], refs[3 * (n_upd + 1):]
        res, refs = refs[:4 * (n_upd + 1)], refs[4 * (n_upd + 1):]
        sums_out, refs = refs[:n_mov - n_upd - 1], refs[n_mov - n_upd - 1:]
        dmod_out, refs = refs[0], refs[1:]
        everyone, (ssem, rsem) = refs[:n_mov], refs[n_mov:]
        x, y, c = _my_pos()
        me = 4 * x + 2 * y + c
        cps = []
        for a in range(n_mov):
            everyone[a][me] = mov[a][...]
            for k in range(1, N_DEV):
                peer = (_flip(x, (k >> 2) & 1), _flip(y, (k >> 1) & 1), _flip(c, k & 1))
                cps.append(pltpu.make_async_remote_copy(
                    src_ref=everyone[a].at[me], dst_ref=everyone[a].at[me], send_sem=ssem.at[a, k - 1],
                    recv_sem=rsem.at[a, k - 1], device_id=peer, device_id_type=MESH))
        for cp in cps:
            cp.start()
        for cp in cps:
            cp.wait_recv()

        def total(a):
            acc = everyone[a][0]
            for dev in range(1, N_DEV):
                acc = acc + everyone[a][dev]
            return acc

        grads = [total(a) for a in range(n_upd)]
        grads.append(jnp.sum(total(n_mov - 1), axis=0, keepdims=True))
        for j, g in enumerate(grads):
            delta, m2, v2 = _adamw_math(wmv[3 * j][...], g, wmv[3 * j + 1][...], wmv[3 * j + 2][...])
            res[4 * j][...] = g
            res[4 * j + 1][...] = delta
            res[4 * j + 2][...] = m2
            res[4 * j + 3][...] = v2
        for j in range(n_mov - n_upd - 1):
            sums_out[j][...] = total(n_upd + j)
        dmod_out[...] = everyone[n_mov - 1][...]
        for cp in cps:
            cp.wait_send()

    vm = pl.BlockSpec(memory_space=pltpu.VMEM)
    sds = jax.ShapeDtypeStruct
    out_shape = []
    for w, _, _ in all_states:
        out_shape += [sds(w.shape, F32)] * 4
    out_shape += [sds(p.shape, F32) for p in partials[n_upd:]]
    out_shape.append(sds((N_DEV,) + dmod.shape, F32))
    args = moving + [a for st in all_states for a in st]
    outs, rides = _hosted(
        body, rider, name="small_step", grid=(), out_shape=out_shape,
        in_specs=[vm] * len(args), out_specs=[vm] * len(out_shape),
        scratch_shapes=[pltpu.VMEM((N_DEV,) + a.shape, F32) for a in moving]
        + [pltpu.SemaphoreType.DMA((n_mov, N_DEV - 1)), pltpu.SemaphoreType.DMA((n_mov, N_DEV - 1))],
        compiler_params=_cparams(vmem=VMEM_BIG), args=args)
    return outs, rides


def _adamw(w, grads, m, v, name):
    rows, cols = w.shape
    tr = _row_tile(rows, cols)
    ng = len(grads)

    def body(*refs):
        w_ref = refs[0]
        g_refs = refs[1:1 + ng]
        m_ref, v_ref = refs[1 + ng], refs[2 + ng]
        g_out, d_out, m_out, v_out = refs[3 + ng:]
        g = g_refs[0][...]
        for extra in g_refs[1:]:
            g = g + extra[...]
        g_out[...] = g
        d_out[...], m_out[...], v_out[...] = _adamw_math(w_ref[...], g, m_ref[...], v_ref[...])

    spec = pl.BlockSpec((tr, cols), lambda i: (i, 0))
    out = jax.ShapeDtypeStruct((rows, cols), F32)
    return pl.pallas_call(
        body, name=name, grid=(rows // tr,),
        out_shape=(out, out, out, out),
        in_specs=[spec] * (3 + ng), out_specs=(spec, spec, spec, spec),
        compiler_params=_cparams(("arbitrary",)),
    )(w, *grads, m, v)


def _pack_rows(arrays):
    tile = SUBLANES * LANES
    rows, offsets, at = [], [], 0
    for a in arrays:
        flat = a.reshape(-1).astype(F32)
        n = -(-flat.shape[0] // tile) * tile
        rows.append(jnp.pad(flat, (0, n - flat.shape[0])).reshape(-1, LANES))
        offsets.append(at)
        at += n // LANES
    return jnp.concatenate(rows, axis=0), offsets


def _unpack_rows(packed, offsets, shapes):
    out = []
    for off, shape in zip(offsets, shapes):
        n = 1
        for s in shape:
            n *= s
        nrow = -(-n // LANES)
        out.append(packed[off:off + nrow].reshape(-1)[:n].reshape(shape))
    return out


def _rope_tables(seq):
    half = HEAD_DIM // 2
    inv = np.float32(ROPE_THETA) ** (-np.arange(half, dtype=np.float32) / np.float32(half))
    ang = (np.arange(seq, dtype=np.float32)[:, None] * inv[None, :]).astype(np.float64)
    cos, sin = np.cos(ang).astype(np.float32), np.sin(ang).astype(np.float32)
    cos_t = np.concatenate([cos, cos, cos, cos], axis=1)
    sin_t = np.concatenate([-sin, sin, -sin, sin], axis=1)
    return jnp.asarray(cos_t), jnp.asarray(sin_t)


def kernel(x, c, w_ada, b_ada, g_attn, w_in, na_rpb, sw_sink, g_na_out, g_sw_out, w_out, g_ffn, w_up, conv_w, conv_b, w_down, g_final, loss_target, m_w_ada, m_b_ada, m_g_attn, m_w_in, m_na_rpb, m_sw_sink, m_g_na_out, m_g_sw_out, m_w_out, m_g_ffn, m_w_up, m_conv_w, m_conv_b, m_w_down, m_g_final, v_w_ada, v_b_ada, v_g_attn, v_w_in, v_na_rpb, v_sw_sink, v_g_na_out, v_g_sw_out, v_w_out, v_g_ffn, v_w_up, v_conv_w, v_conv_b, v_w_down, v_g_final):
    batch, seq, d = x.shape
    t = batch * seq
    assert d == D_MODEL and seq % (NA_ROWS * GRID_W) == 0 and seq % TOKEN_TILE == 0 and batch <= SUBLANES
    shard = 2 * lax.axis_index("x") + lax.axis_index("y")
    xt = x.reshape(t, d)
    tgt = loss_target.reshape(t, d)

    c8 = jnp.pad(c, ((0, SUBLANES - batch), (0, 0)))
    w_in_t_s = jnp.transpose(w_in[0]).astype(BF16)
    (mod8, sc_all), (w_in_g,) = _ada_forward(c8, w_ada[0], b_ada, _Rider("gather", [w_in_t_s]))
    mod3 = mod8[:batch].reshape(batch, 1, 6 * d)
    w_in_t = w_in_g.reshape(IN_WIDTH, d)

    cos_t, sin_t = _rope_tables(seq)
    (h1, proj), (w_out_g,) = _in_proj(xt, mod3, g_attn, w_in_t, cos_t, sin_t, seq,
                                      _Rider("gather", [w_out[0].astype(BF16)]))
    n_heads = NA_WIDTH // HEAD_DIM
    n_tiles, n_dc = 2 * NA_ROWS - 2, 2 * NA_COLS - 1
    expand, neg_mask = _na_bias_pattern()
    rpb = na_rpb[0]
    rows2 = jnp.concatenate([rpb[:, :-1, :], rpb[:, 1:, :]], axis=2).reshape(n_heads * n_tiles, 2 * n_dc)
    rows2 = jnp.pad(rows2, ((0, 0), (0, GRID_W - 2 * n_dc)))
    tiles = _na_bias_tiles(rows2, expand, neg_mask).reshape(n_heads, n_tiles, GRID_W, LANES)
    sink = sw_sink[0]
    w_up_b16 = w_up[0].astype(BF16)
    (oa, lse_a), (w_up_a, w_down_g) = _na_forward(proj, tiles, batch, seq,
                                                  _Rider("gather", [w_up_b16[:d // 2], w_down[0].astype(BF16)]))
    (ob, lse_b), (w_up_b, conv_w_g) = _sw_forward(proj, sink, batch, seq,
                                                  _Rider("gather", [w_up_b16[d // 2:], conv_w[0]]))
    w_up_f = (w_up_a, w_up_b)
    w_out_f = w_out_g.reshape(d, d)
    conv_w_f = jnp.transpose(conv_w_g, (1, 0, 2)).reshape(3, D_FF)
    oab, mix, x1, h2 = _out_proj(oa, ob, g_na_out, g_sw_out, w_out_f, xt, mod3, g_ffn, seq)
    (u,), _ = _up_proj(h2, w_up_f)
    w_down_f = w_down_g.reshape(D_FF, d)
    a = _conv_gate(u, conv_w_f, conv_b, batch, seq)
    dx2, dffn, loss_part, dgate_f, dg_final = _down_and_loss(a, w_down_f, x1, mod3, g_final.reshape(1, d), tgt, seq)

    gw_down, gw_down_b = _down_weight_grad(a, dffn)
    blocks = lambda g, rows: g.reshape(N_SHARD, rows // N_SHARD, d)
    (du, gconv_w, gconv_b), (recv_down, own_down) = _ffn_backward(
        dffn, w_down_f, u, conv_w_f, conv_b, batch, seq,
        _Rider("scatter", [blocks(gw_down_b, D_FF)], [blocks(gw_down, D_FF)]))
    (gw_up_top, gw_up_bot, gw_up_top_b, gw_up_bot_b), _ = _up_weight_grad(h2, du)
    (dx1, dmix, dshift_f, dscale_f, dgate_a, dg_ffn), (recv_up_top, own_up_top) = _up_backward(
        du, w_up_f, x1, mod3, g_ffn, dx2, mix, seq, _Rider("scatter", [gw_up_top_b], [gw_up_top]))
    doa, dob, gw_out, gw_out_b, dg_na, dg_sw = _out_backward(dmix, w_out_f, oab, oa, ob, g_na_out, g_sw_out)
    (dqa, dka, dva, dtiles), (recv_up_bot, own_up_bot) = _na_backward(
        proj, doa, lse_a, tiles, batch, seq, _Rider("scatter", [gw_up_bot_b], [gw_up_bot]))
    (dq_b, dk_b, dv_b, dsink_parts), (recv_out, own_out) = _sw_backward(
        proj, dob, lse_b, sink, batch, seq, _Rider("scatter", [blocks(gw_out_b, d)], [blocks(gw_out, d)]))
    gx, gw_in_t, gw_in_b, dshift_a, dscale_a, dg_attn = _in_backward(
        (dqa, dka, dva), dq_b, dk_b, dv_b, w_in_t, h1, xt, mod3, g_attn, dx1, cos_t, sin_t, seq)

    red = _na_bias_grad(dtiles.reshape(n_heads * n_tiles, GRID_W, LANES), expand)[:, :2 * n_dc]
    red = red.reshape(n_heads, n_tiles, 2, n_dc)
    zero_row = jnp.zeros((n_heads, 1, n_dc), F32)
    g_rpb = (jnp.concatenate([red[:, :, 0, :], zero_row], axis=1)
             + jnp.concatenate([zero_row, red[:, :, 1, :]], axis=1))
    g_sink = jnp.sum(dsink_parts[:, :, :2, 0], axis=0).reshape(SW_WIDTH // HEAD_DIM)

    dmod = jnp.concatenate([dshift_a, dscale_a, dgate_a, dshift_f, dscale_f, dgate_f], axis=2).reshape(batch, 6 * d)
    rpb_shape = na_rpb.shape[1:]
    states = [(g_attn, m_g_attn, v_g_attn),
              (na_rpb.reshape(rpb_shape), m_na_rpb.reshape(rpb_shape), v_na_rpb.reshape(rpb_shape)),
              (sw_sink, m_sw_sink, v_sw_sink), (g_na_out, m_g_na_out, v_g_na_out), (g_sw_out, m_g_sw_out, v_g_sw_out),
              (g_ffn, m_g_ffn, v_g_ffn), (conv_b, m_conv_b, v_conv_b),
              (g_final.reshape(1, d), m_g_final.reshape(1, d), v_g_final.reshape(1, d))]
    partials = [dg_attn, g_rpb, g_sink.reshape(sw_sink.shape), dg_na, dg_sw, dg_ffn, gconv_b, dg_final,
                gconv_w, loss_part]
    mine = [None, _sum_slots([(recv_out, own_out)], "sum_w_out"),
            _sum_slots([(recv_up_top, own_up_top), (recv_up_bot, own_up_bot)], "sum_w_up"),
            _sum_slots([(recv_down, own_down)], "sum_w_down")]
    small, (recv_in, own_in, *theirs) = _small_step(
        partials, states, dmod, (b_ada, m_b_ada, v_b_ada),
        _Riders([_Rider("scatter", [blocks(gw_in_b, IN_WIDTH)], [blocks(gw_in_t, IN_WIDTH)]),
                 _Rider("swap", mine[1:])]))
    r_small = [small[4 * j:4 * j + 4] for j in range(len(states) + 1)]
    g_conv_w_full, loss_sum, dmod_all = small[4 * (len(states) + 1):]
    loss = loss_sum[0, 0]
    mine[0] = _sum_slots([(recv_in, own_in)], "sum_w_in")
    theirs = _ride_alone(_Rider("swap", mine[:1]), "swap_sibling") + theirs
    dmod_rows = jnp.pad(dmod_all, ((0, 0), (0, SUBLANES - batch), (0, 0))).reshape(N_DEV * SUBLANES, 6 * d)
    ncol = w_ada.shape[2]
    g_w_ada = _ada_weight_grad(sc_all, lax.dynamic_slice(dmod_rows, (0, shard * ncol), (N_DEV * SUBLANES, ncol)))
    cshard = conv_w.shape[2]
    g_conv_w = lax.dynamic_slice(g_conv_w_full, (0, shard * cshard), (3, cshard))

    def big(w, m, v, g_parts, name):
        shape = w.shape
        outs = _adamw(w[0], g_parts, m[0], v[0], name)
        return [o.reshape(shape) for o in outs]

    r_w_ada = big(w_ada, m_w_ada, v_w_ada, [g_w_ada], "adamw_w_ada")
    r_w_in = [jnp.transpose(o).reshape(w_in.shape) for o in
              _adamw(jnp.transpose(w_in[0]), [mine[0], theirs[0]], jnp.transpose(m_w_in[0]), jnp.transpose(v_w_in[0]),
                     "adamw_w_in")]
    r_w_out = big(w_out, m_w_out, v_w_out, [mine[1], theirs[1]], "adamw_w_out")
    r_w_up = big(w_up, m_w_up, v_w_up, [mine[2], theirs[2]], "adamw_w_up")
    r_w_down = big(w_down, m_w_down, v_w_down, [mine[3], theirs[3]], "adamw_w_down")

    r_conv_w = big(conv_w, m_conv_w, v_conv_w, [g_conv_w], "adamw_conv_w")

    def pick(k):
        ga_, rpb_, sk_, gna_, gsw_, gf_, cb_, gfin_, b_ = [r[k] for r in r_small]
        return [r_w_ada[k], b_, ga_, r_w_in[k], rpb_.reshape(na_rpb.shape), sk_, gna_, gsw_, r_w_out[k], gf_,
                r_w_up[k], r_conv_w[k], cb_, r_w_down[k], gfin_.reshape(d)]

    return (loss, gx.reshape(batch, seq, d), *pick(0), *pick(1), *pick(2), *pick(3))
```

```python
import functools

import jax
import jax.numpy as jnp
import numpy as np
from jax import lax
from jax.experimental import pallas as pl
from jax.experimental.pallas import tpu as pltpu

F32 = jnp.float32
BF16 = jnp.bfloat16
MESH = pl.DeviceIdType.MESH

D_MODEL = 1024
HEAD_DIM = 64
NA_WIDTH = 512
SW_WIDTH = 512
SW_KV_WIDTH = 128
IN_WIDTH = 2304
D_FF = 2816
GRID_W = 64
NA_ROWS = 8
NA_COLS = 16
SW_BLOCK = 128
ROPE_THETA = 10000.0
EPS = 1e-6
NEG = -1e30
QK_SCALE = HEAD_DIM ** -0.5

ADAM_LR = 0.001
ADAM_B1 = 0.9
ADAM_B2 = 0.999
ADAM_EPS = 1e-08
ADAM_WD = 0.01
ADAM_STEP = 10

N_SHARD = 4
N_DEV = 8
LANES = 128
SUBLANES = 8
TOKEN_TILE = 512
FF_TILE = 256
CONV_CHUNK = 64
NA_GROUP = 4
SW_GROUP_BLOCKS = 4
VMEM_BIG = 56 * 1024 * 1024


def _mm(a, b):
    return jnp.dot(a, b, preferred_element_type=F32)


def _mm_nt(a, b):
    return lax.dot_general(a, b, (((1,), (1,)), ((), ())), preferred_element_type=F32)


def _mm_tn(a, b):
    return lax.dot_general(a, b, (((0,), (0,)), ((), ())), preferred_element_type=F32)


def _cparams(sem=None, vmem=None):
    kw = {}
    if sem is not None:
        kw["dimension_semantics"] = sem
    if vmem is not None:
        kw["vmem_limit_bytes"] = vmem
    return pltpu.CompilerParams(**kw)


def _resident(shape):
    return pl.BlockSpec(shape, lambda i: (0,) * len(shape), pipeline_mode=pl.Buffered(1))


def _sigmoid(x):
    return 1.0 / (1.0 + jnp.exp(-x))


def _rms_stats(x):
    r = lax.rsqrt(jnp.mean(x * x, axis=-1, keepdims=True) + EPS)
    return r, x * r


def _rms_bwd(dxn, xn, r):
    return r * (dxn - xn * jnp.mean(dxn * xn, axis=-1, keepdims=True))


def _my_pos():
    return lax.axis_index("x"), lax.axis_index("y"), lax.axis_index("c")


def _flip(v, bit):
    return 1 - v if bit else v


def _ada_forward(c8, w_ada, b_ada, rider):
    d = c8.shape[1]
    ncol = w_ada.shape[1]

    def body(c_ref, w_ref, b_ref, mod_ref, sc_ref, m_scr, mod_buf, ssem, rsem, ssem2, rsem2):
        x, y, c = _my_pos()
        me = 4 * x + 2 * y + c
        shard = 2 * x + y
        cv = c_ref[...]
        my_rows = pl.ds(pl.multiple_of(me * SUBLANES, SUBLANES), SUBLANES)
        sc_ref[my_rows, :] = cv * _sigmoid(cv)

        def copy1(k):
            peer = (_flip(x, (k >> 2) & 1), _flip(y, (k >> 1) & 1), _flip(c, k & 1))
            return pltpu.make_async_remote_copy(
                src_ref=sc_ref.at[my_rows, :], dst_ref=sc_ref.at[my_rows, :],
                send_sem=ssem.at[k - 1], recv_sem=rsem.at[k - 1], device_id=peer, device_id_type=MESH)

        sends = [copy1(k) for k in range(1, N_DEV)]
        for cp in sends:
            cp.start()
        for cp in sends:
            cp.wait_recv()
        m_scr[...] = _mm(sc_ref[...].astype(BF16), w_ref[...].astype(BF16))

        def copy2(k):
            px, py = _flip(x, (k >> 1) & 1), _flip(y, k & 1)
            rows = pl.ds(pl.multiple_of((4 * px + 2 * py + c) * SUBLANES, SUBLANES), SUBLANES)
            return pltpu.make_async_remote_copy(
                src_ref=m_scr.at[rows, :], dst_ref=mod_buf.at[shard],
                send_sem=ssem2.at[k - 1], recv_sem=rsem2.at[k - 1], device_id=(px, py, c), device_id_type=MESH)

        sends2 = [copy2(k) for k in range(1, N_SHARD)]
        for cp in sends2:
            cp.start()
        mod_buf[shard] = m_scr[my_rows, :]
        for cp in sends2:
            cp.wait_recv()
        for s in range(N_SHARD):
            mod_ref[:, s * ncol:(s + 1) * ncol] = mod_buf[s] + b_ref[:, s * ncol:(s + 1) * ncol]
        for cp in sends + sends2:
            cp.wait_send()

    vm = pl.BlockSpec(memory_space=pltpu.VMEM)
    return _hosted(
        body, rider, name="ada_forward", grid=(),
        out_shape=(jax.ShapeDtypeStruct((SUBLANES, N_SHARD * ncol), F32),
                   jax.ShapeDtypeStruct((N_DEV * SUBLANES, d), F32)),
        in_specs=[vm, vm, vm], out_specs=(vm, vm),
        scratch_shapes=[pltpu.VMEM((N_DEV * SUBLANES, ncol), F32), pltpu.VMEM((N_SHARD, SUBLANES, ncol), F32),
                        pltpu.SemaphoreType.DMA((N_DEV - 1,)), pltpu.SemaphoreType.DMA((N_DEV - 1,)),
                        pltpu.SemaphoreType.DMA((N_SHARD - 1,)), pltpu.SemaphoreType.DMA((N_SHARD - 1,))],
        compiler_params=_cparams(vmem=VMEM_BIG), args=[c8, w_ada, b_ada])


class _Rider:
    def __init__(self, kind, srcs, owns=()):
        self.kind, self.srcs, self.owns = kind, list(srcs), list(owns)
        n = len(self.srcs)
        sds = jax.ShapeDtypeStruct
        dma = pltpu.SemaphoreType.DMA
        if kind == "gather":
            self.out_shapes = [sds((N_SHARD,) + s.shape, s.dtype) for s in self.srcs]
            self.sems = [dma((n, N_SHARD - 1)), dma((n, N_SHARD - 1)), dma((n, N_SHARD - 1)), dma((n, N_SHARD - 1)),
                         dma((n,)), dma((n,))]
        elif kind == "scatter":
            self.out_shapes = ([sds((N_SHARD - 1,) + s.shape[1:], s.dtype) for s in self.srcs]
                               + [sds(o.shape[1:], o.dtype) for o in self.owns])
            m = max(len(self.owns), 1)
            self.sems = [dma((n, N_SHARD - 1)), dma((n, N_SHARD - 1)), dma((m,)), dma((m,))]
        else:
            self.out_shapes = [sds(s.shape, s.dtype) for s in self.srcs]
            self.sems = [dma((n,)), dma((n,))]

    @property
    def inputs(self):
        return self.srcs + self.owns

    def _halved(self, i):
        a = self.srcs[i]
        tile_rows = SUBLANES * (4 // jnp.dtype(a.dtype).itemsize)
        return self.kind == "gather" and a.shape[0] % (2 * tile_rows) == 0

    def copies(self, ins, outs, sems):
        n = len(self.srcs)
        x, y, c = _my_pos()
        shard = 2 * x + y
        remote, relay = [], []
        if self.kind == "swap":
            ssem, rsem = sems
            for i in range(n):
                remote.append(pltpu.make_async_remote_copy(
                    src_ref=ins[i], dst_ref=outs[i], send_sem=ssem.at[i], recv_sem=rsem.at[i],
                    device_id=(x, y, 1 - c), device_id_type=MESH))
            return remote, relay
        if self.kind == "gather":
            ssem, rsem, ssem2, rsem2, sib_s, sib_r = sems
        else:
            ssem, rsem, sib_s, sib_r = sems
        for i in range(n):
            if self.kind == "gather":
                remote.append(pltpu.make_async_remote_copy(
                    src_ref=ins[i], dst_ref=outs[i].at[shard], send_sem=sib_s.at[i], recv_sem=sib_r.at[i],
                    device_id=(x, y, 1 - c), device_id_type=MESH))
                half = ins[i].shape[0] // 2
                mine = pl.ds(pl.multiple_of(c * half, half), half) if self._halved(i) else None
            for k in range(1, N_SHARD):
                px, py = _flip(x, (k >> 1) & 1), _flip(y, k & 1)
                if self.kind == "gather":
                    src, dst = ins[i], outs[i].at[shard]
                    if mine is not None:
                        src, dst = src.at[mine], dst.at[mine]
                        got = outs[i].at[2 * px + py].at[mine]
                        relay.append(pltpu.make_async_remote_copy(
                            src_ref=got, dst_ref=got, send_sem=ssem2.at[i, k - 1], recv_sem=rsem2.at[i, k - 1],
                            device_id=(x, y, 1 - c), device_id_type=MESH))
                else:
                    src, dst = ins[i].at[2 * px + py], outs[i].at[k - 1]
                remote.append(pltpu.make_async_remote_copy(
                    src_ref=src, dst_ref=dst, send_sem=ssem.at[i, k - 1], recv_sem=rsem.at[i, k - 1],
                    device_id=(px, py, c), device_id_type=MESH))
        if self.kind == "scatter":
            for i in range(len(self.owns)):
                remote.append(pltpu.make_async_remote_copy(
                    src_ref=ins[n + i].at[shard], dst_ref=outs[n + i], send_sem=sib_s.at[i], recv_sem=sib_r.at[i],
                    device_id=(x, y, 1 - c), device_id_type=MESH))
        return remote, relay

    def start(self, ins, outs, sems):
        remote, _ = self.copies(ins, outs, sems)
        for cp in remote:
            cp.start()

    def wait(self, ins, outs, sems):
        remote, relay = self.copies(ins, outs, sems)
        for cp in remote:
            cp.wait_recv()
        for cp in relay:
            cp.start()
        for cp in relay:
            cp.wait_recv()
        for cp in remote + relay:
            cp.wait_send()


class _Riders:
    def __init__(self, riders):
        self.riders = list(riders)
        self.inputs = [a for r in self.riders for a in r.inputs]
        self.out_shapes = [s for r in self.riders for s in r.out_shapes]
        self.sems = [s for r in self.riders for s in r.sems]

    def _split(self, ins, outs, sems):
        for r in self.riders:
            ni, no, ns = len(r.inputs), len(r.out_shapes), len(r.sems)
            yield r, ins[:ni], outs[:no], sems[:ns]
            ins, outs, sems = ins[ni:], outs[no:], sems[ns:]

    def start(self, ins, outs, sems):
        for r, i, o, s in self._split(ins, outs, sems):
            r.start(i, o, s)

    def wait(self, ins, outs, sems):
        for r, i, o, s in self._split(ins, outs, sems):
            r.wait(i, o, s)


def _hosted(body, rider, *, name, grid, out_shape, in_specs, out_specs, scratch_shapes, compiler_params, args):
    out_shape, out_specs = list(out_shape), list(out_specs)
    if rider is None:
        outs = pl.pallas_call(body, name=name, grid=grid, out_shape=tuple(out_shape), in_specs=list(in_specs),
                              out_specs=tuple(out_specs), scratch_shapes=list(scratch_shapes),
                              compiler_params=compiler_params)(*args)
        return list(outs), []
    n_in, n_out, n_scr = len(in_specs), len(out_shape), len(scratch_shapes)
    nr_in, nr_out = len(rider.inputs), len(rider.out_shapes)
    n_steps = 1
    for size in grid:
        n_steps *= size

    def full(*refs):
        ins, refs = refs[:n_in], refs[n_in:]
        r_in, refs = refs[:nr_in], refs[nr_in:]
        outs, refs = refs[:n_out], refs[n_out:]
        r_out, refs = refs[:nr_out], refs[nr_out:]
        scr, sems = refs[:n_scr], refs[n_scr:]
        if grid:
            step = 0
            for ax, size in enumerate(grid):
                step = step * size + pl.program_id(ax)
            pl.when(step == 0)(lambda: rider.start(r_in, r_out, sems))
            body(*ins, *outs, *scr)
            pl.when(step == n_steps - 1)(lambda: rider.wait(r_in, r_out, sems))
        else:
            rider.start(r_in, r_out, sems)
            body(*ins, *outs, *scr)
            rider.wait(r_in, r_out, sems)

    hbm = pl.BlockSpec(memory_space=pl.ANY)
    res = pl.pallas_call(
        full, name=name, grid=grid, out_shape=tuple(out_shape + rider.out_shapes),
        in_specs=list(in_specs) + [hbm] * nr_in, out_specs=tuple(out_specs + [hbm] * nr_out),
        scratch_shapes=list(scratch_shapes) + rider.sems, compiler_params=compiler_params,
    )(*args, *rider.inputs)
    return list(res[:n_out]), list(res[n_out:])


def _ride_alone(rider, name):
    return _hosted(lambda: None, rider, name=name, grid=(), out_shape=[], in_specs=[], out_specs=[], scratch_shapes=[],
                   compiler_params=_cparams(), args=[])[1]


def _allreduce_small(packed, rider=None):
    r = packed.shape[0]

    def body(p_ref, sum_ref, all_ref, ssem, rsem):
        x, y, c = _my_pos()
        me = 4 * x + 2 * y + c
        all_ref[me] = p_ref[...]
        cps = []
        for k in range(1, N_DEV):
            peer = (_flip(x, (k >> 2) & 1), _flip(y, (k >> 1) & 1), _flip(c, k & 1))
            cps.append(pltpu.make_async_remote_copy(
                src_ref=all_ref.at[me], dst_ref=all_ref.at[me], send_sem=ssem.at[k - 1], recv_sem=rsem.at[k - 1],
                device_id=peer, device_id_type=MESH))
        for cp in cps:
            cp.start()
        for cp in cps:
            cp.wait_recv()
        acc = all_ref[0]
        for dev in range(1, N_DEV):
            acc = acc + all_ref[dev]
        sum_ref[...] = acc
        for cp in cps:
            cp.wait_send()

    vm = pl.BlockSpec(memory_space=pltpu.VMEM)
    return _hosted(
        body, rider, name="allreduce_small", grid=(),
        out_shape=[jax.ShapeDtypeStruct((r, LANES), F32), jax.ShapeDtypeStruct((N_DEV, r, LANES), F32)],
        in_specs=[vm], out_specs=[vm, vm],
        scratch_shapes=[pltpu.SemaphoreType.DMA((N_DEV - 1,)), pltpu.SemaphoreType.DMA((N_DEV - 1,))],
        compiler_params=_cparams(), args=[packed])


def _rope_rot(t):
    w = t.shape[1]
    lane = lax.broadcasted_iota(jnp.int32, t.shape, 1)
    first = (lane % HEAD_DIM) < (HEAD_DIM // 2)
    return jnp.where(first, pltpu.roll(t, w - HEAD_DIM // 2, 1), pltpu.roll(t, HEAD_DIM // 2, 1))


def _in_proj(x, mod3, g_attn, w_in_t, cos_t, sin_t, seq, rider=None):
    t, d = x.shape
    tm = TOKEN_TILE
    per_seq = seq // tm
    rope_lo, rope_hi = 3 * NA_WIDTH, 3 * NA_WIDTH + SW_WIDTH + SW_KV_WIDTH
    n_rep = (rope_hi - rope_lo) // LANES

    def body(x_ref, mod_ref, g_ref, w_ref, cos_ref, sin_ref, h_ref, p_ref):
        r, xn = _rms_stats(x_ref[...])
        shift, scale = mod_ref[0, :, 0:d], mod_ref[0, :, d:2 * d]
        hb = ((xn * g_ref[...]) * (1.0 + scale) + shift).astype(BF16)
        h_ref[...] = hb
        p_ref[:, :rope_lo] = _mm_nt(hb, w_ref[:rope_lo, :]).astype(BF16)
        pr = _mm_nt(hb, w_ref[rope_lo:rope_hi, :])
        cos = jnp.concatenate([cos_ref[...]] * n_rep, axis=1)
        sin = jnp.concatenate([sin_ref[...]] * n_rep, axis=1)
        p_ref[:, rope_lo:rope_hi] = (pr * cos + _rope_rot(pr) * sin).astype(BF16)
        p_ref[:, rope_hi:] = _mm_nt(hb, w_ref[rope_hi:, :]).astype(BF16)

    return _hosted(
        body, rider, name="in_proj", grid=(t // tm,),
        out_shape=[jax.ShapeDtypeStruct((t, d), BF16), jax.ShapeDtypeStruct((t, IN_WIDTH), BF16)],
        in_specs=[pl.BlockSpec((tm, d), lambda i: (i, 0)),
                  pl.BlockSpec((1, 1, 6 * d), lambda i: (i // per_seq, 0, 0)),
                  pl.BlockSpec((1, d), lambda i: (0, 0)),
                  pl.BlockSpec((IN_WIDTH, d), lambda i: (0, 0)),
                  pl.BlockSpec((tm, LANES), lambda i: (i % per_seq, 0)),
                  pl.BlockSpec((tm, LANES), lambda i: (i % per_seq, 0))],
        out_specs=[pl.BlockSpec((tm, d), lambda i: (i, 0)), pl.BlockSpec((tm, IN_WIDTH), lambda i: (i, 0))],
        scratch_shapes=[], compiler_params=_cparams(("arbitrary",), VMEM_BIG),
        args=[x, mod3, g_attn, w_in_t, cos_t, sin_t])


def _na_bias_pattern():
    n_dc = 2 * NA_COLS - 1
    j = np.arange(GRID_W)[:, None]
    m = np.arange(GRID_W * LANES)[None, :]
    q, lane = m // LANES, m % LANES
    k = lane % GRID_W
    cs = np.clip(q - NA_COLS // 2, 0, GRID_W - NA_COLS)
    ok = (k >= cs) & (k < cs + NA_COLS)
    hit = ok & (j < 2 * n_dc) & (lane // GRID_W == j // n_dc) & (k - q + (NA_COLS - 1) == j % n_dc)
    return jnp.asarray(hit.astype(np.float32)), jnp.asarray(np.where(ok, 0.0, NEG).astype(np.float32))


def _na_bias_tiles(rows2, expand, mask):
    n, width = rows2.shape[0], expand.shape[1]
    q_step = 16
    step = q_step * LANES

    def body(r_ref, e_ref, m_ref, o_ref):
        flat = jnp.dot(r_ref[...], e_ref[...], precision=lax.Precision.HIGHEST,
                       preferred_element_type=F32) + m_ref[...]
        for qq in range(q_step):
            o_ref[:, qq, :] = flat[:, qq * LANES:(qq + 1) * LANES]

    return pl.pallas_call(
        body, name="na_bias_tiles", grid=(width // step,),
        out_shape=jax.ShapeDtypeStruct((n, GRID_W, LANES), F32),
        in_specs=[pl.BlockSpec(rows2.shape, lambda i: (0, 0)), pl.BlockSpec((expand.shape[0], step), lambda i: (0, i)),
                  pl.BlockSpec((1, step), lambda i: (0, i))],
        out_specs=pl.BlockSpec((n, q_step, LANES), lambda i: (0, i, 0)),
        compiler_params=_cparams(("arbitrary",)),
    )(rows2, expand, mask)


def _na_prepare(k_ref, v_ref, km, vm):
    lane = lax.broadcasted_iota(jnp.int32, k_ref.shape, 1)
    low = lane < HEAD_DIM
    kv = k_ref[...]
    vv = v_ref[...]
    zero = jnp.zeros_like(kv)
    km[0] = jnp.where(low, kv, zero)
    km[1] = jnp.where(low, zero, kv)
    vm[0] = jnp.where(low, vv, zero)
    vm[1] = jnp.where(low, zero, vv)


def _na_window(r, n_rows):
    rs = jnp.clip(r - NA_ROWS // 2, 0, n_rows - NA_ROWS)
    return rs, r - rs


def _na_pair_window(ref, wrows):
    return jnp.concatenate([ref[0, wrows, :], ref[1, wrows, :]], axis=0)


def _na_scores(q, k2, tp_ref, off):
    bias = jnp.concatenate([tp_ref[h, 2 * w - off + (NA_ROWS - 1)] for h in range(2) for w in range(NA_ROWS // 2)],
                           axis=1)
    return _mm_nt(q, k2) * QK_SCALE + bias


def _pair_lse_block(lse):
    lane = lax.broadcasted_iota(jnp.int32, (lse[0].shape[0], LANES), 1)
    return jnp.where(lane < HEAD_DIM, lse[0], lse[1])


def _pair_softmax(s):
    win = s.shape[1] // 2
    halves, lse = [], []
    for h in range(2):
        sh = s[:, h * win:(h + 1) * win]
        m = jnp.max(sh, axis=-1, keepdims=True)
        e = jnp.exp(sh - m)
        l = jnp.sum(e, axis=-1, keepdims=True)
        halves.append(e / l)
        lse.append(m + jnp.log(l))
    return jnp.concatenate(halves, axis=1), _pair_lse_block(lse)


def _pair_probs_from_lse(s, lse_block):
    win = s.shape[1] // 2
    return jnp.concatenate([jnp.exp(s[:, h * win:(h + 1) * win] - lse_block[:, h * HEAD_DIM:h * HEAD_DIM + 1])
                            for h in range(2)], axis=1)


def _na_forward(proj, tiles, batch, seq, rider=None):
    t = proj.shape[0]
    n_rows = seq // GRID_W
    n_pairs = NA_WIDTH // LANES
    win = NA_ROWS * GRID_W

    def body(q_ref, k_ref, v_ref, tp_ref, o_ref, lse_ref, km, vm):
        _na_prepare(k_ref, v_ref, km, vm)

        def scores(r):
            rs, off = _na_window(r, n_rows)
            rows = pl.ds(pl.multiple_of(r * GRID_W, GRID_W), GRID_W)
            wrows = pl.ds(pl.multiple_of(rs * GRID_W, GRID_W), win)
            return rows, wrows, _na_scores(q_ref[rows, :], _na_pair_window(km, wrows), tp_ref, off)

        def finish(rows, wrows, s):
            p, lse = _pair_softmax(s)
            lse_ref[rows, :] = lse
            o_ref[rows, :] = _mm(p.astype(BF16), _na_pair_window(vm, wrows))

        def row_group(i, carry):
            for state in [scores(NA_GROUP * i + j) for j in range(NA_GROUP)]:
                finish(*state)
            return carry

        lax.fori_loop(0, n_rows // NA_GROUP, row_group, 0)

    return _hosted(
        body, rider, name="na_forward", grid=(batch, n_pairs),
        out_shape=[jax.ShapeDtypeStruct((t, NA_WIDTH), F32), jax.ShapeDtypeStruct((t, NA_WIDTH), F32)],
        in_specs=[pl.BlockSpec((seq, LANES), lambda b, p: (b, p)),
                  pl.BlockSpec((seq, LANES), lambda b, p: (b, n_pairs + p)),
                  pl.BlockSpec((seq, LANES), lambda b, p: (b, 2 * n_pairs + p)),
                  pl.BlockSpec((2, 2 * NA_ROWS - 2, GRID_W, LANES), lambda b, p: (p, 0, 0, 0))],
        out_specs=[pl.BlockSpec((seq, LANES), lambda b, p: (b, p)), pl.BlockSpec((seq, LANES), lambda b, p: (b, p))],
        scratch_shapes=[pltpu.VMEM((2, seq, LANES), BF16), pltpu.VMEM((2, seq, LANES), BF16)],
        compiler_params=_cparams(("arbitrary", "arbitrary")), args=[proj, proj, proj, tiles])


def _sw_prepare(kv_ref, g, dst_lo, dst_hi, seq):
    lane = lax.broadcasted_iota(jnp.int32, kv_ref.shape, 1)
    mine = (lane // HEAD_DIM) == g
    kg = jnp.where(mine, kv_ref[...].astype(F32), 0.0)
    kr = pltpu.roll(kg, HEAD_DIM, 1)
    first = g == 0
    zero = jnp.zeros((SW_BLOCK, LANES), BF16)
    for dst, val in ((dst_lo, jnp.where(first, kg, kr)), (dst_hi, jnp.where(first, kr, kg))):
        dst[0:SW_BLOCK, :] = zero
        dst[SW_BLOCK:SW_BLOCK + seq, :] = val.astype(BF16)
        dst[SW_BLOCK + seq:, :] = zero


def _sw_mask(n, seq):
    qi = lax.broadcasted_iota(jnp.int32, (SW_BLOCK, 3 * SW_BLOCK), 0)
    kj = lax.broadcasted_iota(jnp.int32, (SW_BLOCK, 3 * SW_BLOCK), 1)
    kpos = n * SW_BLOCK - SW_BLOCK + kj
    return (jnp.abs(qi + SW_BLOCK - kj) <= SW_BLOCK) & (kpos >= 0) & (kpos < seq)


def _sw_probs(s2, ok, sinks):
    band = s2.shape[1] // 2
    halves, lse = [], []
    for i in range(2):
        s = jnp.where(ok, s2[:, i * band:(i + 1) * band], NEG)
        m = jnp.maximum(jnp.max(s, axis=-1, keepdims=True), sinks[i])
        p = jnp.exp(s - m)
        den = jnp.sum(p, axis=-1, keepdims=True) + jnp.exp(sinks[i] - m)
        halves.append(p / den)
        lse.append(m + jnp.log(den))
    return jnp.concatenate(halves, axis=1), _pair_lse_block(lse)


def _sw_probs_from_lse(s2, ok, sinks, lse_block):
    band = s2.shape[1] // 2
    halves, sink_p = [], []
    for i in range(2):
        lse = lse_block[:, i * HEAD_DIM:i * HEAD_DIM + 1]
        halves.append(jnp.exp(jnp.where(ok, s2[:, i * band:(i + 1) * band], NEG) - lse))
        sink_p.append(jnp.exp(sinks[i] - lse))
    return jnp.concatenate(halves, axis=1), sink_p


def _sw_forward(proj, sink, batch, seq, rider=None):
    t = proj.shape[0]
    n_pairs = SW_WIDTH // LANES
    q_blk = 3 * NA_WIDTH // LANES
    k_blk = q_blk + n_pairs
    n_blocks = seq // SW_BLOCK
    pad = seq + 2 * SW_BLOCK

    def body(sink_ref, q_ref, k_ref, v_ref, o_ref, lse_ref, k_lo, k_hi, v_lo, v_hi):
        hp = pl.program_id(1)
        g = hp // 2
        _sw_prepare(k_ref, g, k_lo, k_hi, seq)
        _sw_prepare(v_ref, g, v_lo, v_hi, seq)

        sinks = (sink_ref[2 * hp], sink_ref[2 * hp + 1])

        def scores(n):
            rows = pl.ds(pl.multiple_of(n * SW_BLOCK, SW_BLOCK), SW_BLOCK)
            wrows = pl.ds(pl.multiple_of(n * SW_BLOCK, SW_BLOCK), 3 * SW_BLOCK)
            k2 = jnp.concatenate([k_lo[wrows, :], k_hi[wrows, :]], axis=0)
            return n, rows, wrows, _mm_nt(q_ref[rows, :], k2) * QK_SCALE

        def finish(n, rows, wrows, s2):
            p, lse = _sw_probs(s2, _sw_mask(n, seq), sinks)
            lse_ref[rows, :] = lse
            v2 = jnp.concatenate([v_lo[wrows, :], v_hi[wrows, :]], axis=0)
            o_ref[rows, :] = _mm(p.astype(BF16), v2)

        def block_group(i, carry):
            for state in [scores(SW_GROUP_BLOCKS * i + j) for j in range(SW_GROUP_BLOCKS)]:
                finish(*state)
            return carry

        lax.fori_loop(0, n_blocks // SW_GROUP_BLOCKS, block_group, 0)

    return _hosted(
        body, rider, name="sw_forward", grid=(batch, n_pairs),
        out_shape=[jax.ShapeDtypeStruct((t, SW_WIDTH), F32), jax.ShapeDtypeStruct((t, SW_WIDTH), F32)],
        in_specs=[pl.BlockSpec(memory_space=pltpu.SMEM),
                  pl.BlockSpec((seq, LANES), lambda b, p: (b, q_blk + p)),
                  pl.BlockSpec((seq, LANES), lambda b, p: (b, k_blk)),
                  pl.BlockSpec((seq, LANES), lambda b, p: (b, k_blk + 1))],
        out_specs=[pl.BlockSpec((seq, LANES), lambda b, p: (b, p)), pl.BlockSpec((seq, LANES), lambda b, p: (b, p))],
        scratch_shapes=[pltpu.VMEM((pad, LANES), BF16)] * 4,
        compiler_params=_cparams(("arbitrary", "arbitrary")), args=[sink, proj, proj, proj])


def _out_proj(oa, ob, g_na, g_sw, w_out, x, mod3, g_ffn, seq):
    t, d = x.shape
    tm = TOKEN_TILE
    per_seq = seq // tm

    def body(oa_ref, ob_ref, gna_ref, gsw_ref, w_ref, x_ref, mod_ref, gf_ref, oab_ref, mix_ref, x1_ref, h2_ref):
        _, na = _rms_stats(oa_ref[...])
        _, nb = _rms_stats(ob_ref[...])
        oab = jnp.concatenate([na * gna_ref[...], nb * gsw_ref[...]], axis=1).astype(BF16)
        oab_ref[...] = oab
        mix = _mm(oab, w_ref[...])
        mix_ref[...] = mix
        gate_a = mod_ref[0, :, 2 * d:3 * d]
        shift_f, scale_f = mod_ref[0, :, 3 * d:4 * d], mod_ref[0, :, 4 * d:5 * d]
        x1 = x_ref[...] + gate_a * mix
        x1_ref[...] = x1
        _, xn = _rms_stats(x1)
        h2_ref[...] = ((xn * gf_ref[...]) * (1.0 + scale_f) + shift_f).astype(BF16)

    tile = lambda w: pl.BlockSpec((tm, w), lambda i: (i, 0))
    vec = lambda w: pl.BlockSpec((1, w), lambda i: (0, 0))
    return pl.pallas_call(
        body, name="out_proj", grid=(t // tm,),
        out_shape=(jax.ShapeDtypeStruct((t, d), BF16), jax.ShapeDtypeStruct((t, d), F32),
                   jax.ShapeDtypeStruct((t, d), F32), jax.ShapeDtypeStruct((t, d), BF16)),
        in_specs=[tile(NA_WIDTH), tile(SW_WIDTH), vec(NA_WIDTH), vec(SW_WIDTH),
                  pl.BlockSpec((d, d), lambda i: (0, 0)), tile(d),
                  pl.BlockSpec((1, 1, 6 * d), lambda i: (i // per_seq, 0, 0)), vec(d)],
        out_specs=(tile(d), tile(d), tile(d), tile(d)),
        compiler_params=_cparams(("arbitrary",), VMEM_BIG),
    )(oa, ob, g_na, g_sw, w_out, x, mod3, g_ffn)


def _up_proj(h2, w_up_halves, rider=None):
    t, d = h2.shape
    tm = TOKEN_TILE
    w_a, w_b = w_up_halves
    half, wcol = w_a.shape[1], w_a.shape[2]

    def body(h_ref, wa_ref, wb_ref, u_ref):
        u_ref[0] = (_mm(h_ref[:, :half], wa_ref[0]) + _mm(h_ref[:, half:], wb_ref[0])).astype(BF16)

    w_spec = pl.BlockSpec((1, half, wcol), lambda j, i: (j, 0, 0))
    return _hosted(
        body, rider, name="up_proj", grid=(N_SHARD, t // tm),
        out_shape=[jax.ShapeDtypeStruct((2, t, D_FF), BF16)],
        in_specs=[pl.BlockSpec((tm, d), lambda j, i: (i, 0)), w_spec, w_spec],
        out_specs=[pl.BlockSpec((1, tm, wcol), lambda j, i: (j // 2, i, j % 2))],
        scratch_shapes=[], compiler_params=_cparams(("arbitrary", "arbitrary"), VMEM_BIG), args=[h2, w_a, w_b])


def _taps_chunk(load, s, rows, seq):
    halo = 2 * SUBLANES
    cur = load(s, rows)
    above = load(pl.multiple_of(jnp.maximum(s - halo, 0), halo), halo)
    below = load(pl.multiple_of(jnp.minimum(s + rows, seq - halo), halo), halo)
    up = jnp.where(s > 0, above[halo - 1:halo, :], 0.0)
    dn = jnp.where(s + rows < seq, below[0:1, :], 0.0)
    row = lax.broadcasted_iota(jnp.int32, cur.shape, 0)
    prev = jnp.where(row == 0, up, pltpu.roll(cur, 1, 0))
    nxt = jnp.where(row == rows - 1, dn, pltpu.roll(cur, rows - 1, 0))
    return cur, prev, nxt


def _conv_gate(u, conv_w, conv_b, batch, seq):
    t = u.shape[1]
    cw = FF_TILE
    rows = CONV_CHUNK

    def body(u_ref, w_ref, b_ref, a_ref):
        def chunk(i, carry):
            s = pl.multiple_of(i * rows, rows)
            gt, prev, nxt = _taps_chunk(lambda at, n: u_ref[1, pl.ds(at, n), :].astype(F32), s, rows, seq)
            gc = prev * w_ref[0:1, :] + gt * w_ref[1:2, :] + nxt * w_ref[2:3, :] + b_ref[...]
            a_ref[pl.ds(s, rows), :] = ((gc * _sigmoid(gc)) * u_ref[0, pl.ds(s, rows), :].astype(F32)).astype(BF16)
            return carry

        lax.fori_loop(0, seq // rows, chunk, 0)

    return pl.pallas_call(
        body, name="conv_gate", grid=(batch, D_FF // cw),
        out_shape=jax.ShapeDtypeStruct((t, D_FF), BF16),
        in_specs=[pl.BlockSpec((2, seq, cw), lambda b, j: (0, b, j)),
                  pl.BlockSpec((3, cw), lambda b, j: (0, j)), pl.BlockSpec((1, cw), lambda b, j: (0, j))],
        out_specs=pl.BlockSpec((seq, cw), lambda b, j: (b, j)),
        compiler_params=_cparams(("arbitrary", "arbitrary"), VMEM_BIG),
    )(u, conv_w, conv_b)


def _down_and_loss(a, w_down, x1, mod3, g_final, target, seq):
    t, d = x1.shape
    tm = TOKEN_TILE
    per_seq = seq // tm
    batch = t // seq

    def body(a_ref, w_ref, x1_ref, mod_ref, g_ref, tgt_ref, dx2_ref, dffn_ref, loss_ref, dgate_ref, dg_ref):
        i = pl.program_id(0)
        f = _mm(a_ref[...], w_ref[...])
        gate_f = mod_ref[0, :, 5 * d:6 * d]
        x2 = x1_ref[...] + gate_f * f
        r, xn = _rms_stats(x2)
        err = xn * g_ref[...] - tgt_ref[...]
        part = 0.5 * jnp.sum(jnp.mean(err * err, axis=-1, keepdims=True))
        dy = err / d
        dx2 = _rms_bwd(dy * g_ref[...], xn, r)
        dx2_ref[...] = dx2
        dffn_ref[...] = (dx2 * gate_f).astype(BF16)

        @pl.when(i == 0)
        def _():
            loss_ref[...] = jnp.zeros_like(loss_ref)
            dg_ref[...] = jnp.zeros_like(dg_ref)

        @pl.when(i % per_seq == 0)
        def _():
            dgate_ref[...] = jnp.zeros_like(dgate_ref)

        loss_ref[...] += part
        dg_ref[...] += jnp.sum(dy * xn, axis=0, keepdims=True)
        dgate_ref[0] += jnp.sum(dx2 * f, axis=0, keepdims=True)

    tile = lambda w: pl.BlockSpec((tm, w), lambda i: (i, 0))
    return pl.pallas_call(
        body, name="down_loss", grid=(t // tm,),
        out_shape=(jax.ShapeDtypeStruct((t, d), F32), jax.ShapeDtypeStruct((t, d), BF16),
                   jax.ShapeDtypeStruct((SUBLANES, LANES), F32), jax.ShapeDtypeStruct((batch, 1, d), F32),
                   jax.ShapeDtypeStruct((1, d), F32)),
        in_specs=[tile(D_FF), _resident((D_FF, d)), tile(d),
                  pl.BlockSpec((1, 1, 6 * d), lambda i: (i // per_seq, 0, 0)),
                  pl.BlockSpec((1, d), lambda i: (0, 0)), tile(d)],
        out_specs=(tile(d), tile(d), pl.BlockSpec((SUBLANES, LANES), lambda i: (0, 0)),
                   pl.BlockSpec((1, 1, d), lambda i: (i // per_seq, 0, 0)), pl.BlockSpec((1, d), lambda i: (0, 0))),
        compiler_params=_cparams(("arbitrary",), VMEM_BIG),
    )(a, w_down, x1, mod3, g_final, target)


def _down_weight_grad(a, dffn):
    t, dff = a.shape
    d = dffn.shape[1]
    tk = TOKEN_TILE
    n_k = t // tk

    def body(a_ref, df_ref, g_ref, gb_ref):
        k = pl.program_id(0)

        @pl.when(k == 0)
        def _():
            g_ref[...] = jnp.zeros_like(g_ref)

        g_ref[...] += _mm_tn(a_ref[...], df_ref[...])

        @pl.when(k == n_k - 1)
        def _():
            gb_ref[...] = g_ref[...].astype(BF16)

    whole = pl.BlockSpec((dff, d), lambda k: (0, 0))
    return pl.pallas_call(
        body, name="down_weight_grad", grid=(n_k,),
        out_shape=(jax.ShapeDtypeStruct((dff, d), F32), jax.ShapeDtypeStruct((dff, d), BF16)),
        in_specs=[pl.BlockSpec((tk, dff), lambda k: (k, 0)), pl.BlockSpec((tk, d), lambda k: (k, 0))],
        out_specs=(whole, whole),
        compiler_params=_cparams(("arbitrary",), VMEM_BIG),
    )(a, dffn)


def _ffn_backward(dffn, w_down, u, conv_w, conv_b, batch, seq, rider=None):
    t, d = dffn.shape
    cw = FF_TILE
    rows = CONV_CHUNK

    def body(df_ref, wd_ref, u_ref, w_ref, b_ref, du_ref, gcw_ref, gcb_ref, da_scr, dgc_scr):
        b = pl.program_id(1)
        da_scr[...] = _mm_nt(df_ref[...], wd_ref[...])

        @pl.when(b == 0)
        def _():
            gcw_ref[...] = jnp.zeros_like(gcw_ref)
            gcb_ref[...] = jnp.zeros_like(gcb_ref)

        def fold(v):
            return jnp.sum(v.reshape(rows // SUBLANES, SUBLANES, cw), axis=0)

        def chunk(i, carry):
            s = pl.multiple_of(i * rows, rows)
            here = pl.ds(s, rows)
            gt, prev, nxt = _taps_chunk(lambda at, n: u_ref[1, pl.ds(at, n), :].astype(F32), s, rows, seq)
            val, da = u_ref[0, here, :].astype(F32), da_scr[here, :]
            gc = prev * w_ref[0:1, :] + gt * w_ref[1:2, :] + nxt * w_ref[2:3, :] + b_ref[...]
            sg = _sigmoid(gc)
            sl = gc * sg
            du_ref[0, here, :] = (da * sl).astype(BF16)
            dgc = (da * val) * (sg * (1.0 + gc * (1.0 - sg)))
            dgc_scr[here, :] = dgc
            cb, c0, c1, c2 = carry
            return cb + fold(dgc), c0 + fold(dgc * prev), c1 + fold(dgc * gt), c2 + fold(dgc * nxt)

        zero = jnp.zeros((SUBLANES, cw), F32)
        cb, c0, c1, c2 = lax.fori_loop(0, seq // rows, chunk, (zero, zero, zero, zero))
        gcb_ref[...] += jnp.sum(cb, axis=0, keepdims=True)
        gcw_ref[0:1, :] += jnp.sum(c0, axis=0, keepdims=True)
        gcw_ref[1:2, :] += jnp.sum(c1, axis=0, keepdims=True)
        gcw_ref[2:3, :] += jnp.sum(c2, axis=0, keepdims=True)

        def chunk2(i, carry):
            s = pl.multiple_of(i * rows, rows)
            dgc, dprev, dnxt = _taps_chunk(lambda at, n: dgc_scr[pl.ds(at, n), :], s, rows, seq)
            du_ref[1, pl.ds(s, rows), :] = (dnxt * w_ref[0:1, :] + dgc * w_ref[1:2, :]
                                            + dprev * w_ref[2:3, :]).astype(BF16)
            return carry

        lax.fori_loop(0, seq // rows, chunk2, 0)

    return _hosted(
        body, rider, name="ffn_backward", grid=(D_FF // cw, batch),
        out_shape=[jax.ShapeDtypeStruct((2, t, D_FF), BF16),
                   jax.ShapeDtypeStruct((3, D_FF), F32), jax.ShapeDtypeStruct((1, D_FF), F32)],
        in_specs=[pl.BlockSpec((seq, d), lambda j, b: (b, 0)), pl.BlockSpec((cw, d), lambda j, b: (j, 0)),
                  pl.BlockSpec((2, seq, cw), lambda j, b: (0, b, j)),
                  pl.BlockSpec((3, cw), lambda j, b: (0, j)), pl.BlockSpec((1, cw), lambda j, b: (0, j))],
        out_specs=[pl.BlockSpec((2, seq, cw), lambda j, b: (0, b, j)),
                   pl.BlockSpec((3, cw), lambda j, b: (0, j)), pl.BlockSpec((1, cw), lambda j, b: (0, j))],
        scratch_shapes=[pltpu.VMEM((seq, cw), F32), pltpu.VMEM((seq, cw), F32)],
        compiler_params=_cparams(("arbitrary", "arbitrary"), VMEM_BIG), args=[dffn, w_down, u, conv_w, conv_b])


def _up_backward(du, w_up, x1, mod3, g_ffn, dx2, mix, seq, rider=None):
    _, t, _ = du.shape
    d = x1.shape[1]
    tm = TOKEN_TILE
    per_seq = seq // tm
    batch = t // seq
    w_a, w_b = w_up
    half, wcol = w_a.shape[1], w_a.shape[2]

    def body(du_ref, wa_ref, wb_ref, x1_ref, mod_ref, g_ref, dx2_ref, mix_ref,
             dx1_ref, dmix_ref, dsh_ref, dsc_ref, dga_ref, dg_ref):
        i = pl.program_id(0)
        parts = []
        for w_ref in (wa_ref, wb_ref):
            acc = jnp.zeros((tm, half), F32)
            for j in range(N_SHARD):
                acc = acc + _mm_nt(du_ref[j // 2, :, (j % 2) * wcol:(j % 2 + 1) * wcol], w_ref[j])
            parts.append(acc)
        dh = jnp.concatenate(parts, axis=1)
        gate_a = mod_ref[0, :, 2 * d:3 * d]
        scale_f = mod_ref[0, :, 4 * d:5 * d]
        r, xn = _rms_stats(x1_ref[...])
        xg = xn * g_ref[...]
        dxg = dh * (1.0 + scale_f)
        dx1 = dx2_ref[...] + _rms_bwd(dxg * g_ref[...], xn, r)
        dx1_ref[...] = dx1
        dmix_ref[...] = (dx1 * gate_a).astype(BF16)

        @pl.when(i == 0)
        def _():
            dg_ref[...] = jnp.zeros_like(dg_ref)

        @pl.when(i % per_seq == 0)
        def _():
            dsh_ref[...] = jnp.zeros_like(dsh_ref)
            dsc_ref[...] = jnp.zeros_like(dsc_ref)
            dga_ref[...] = jnp.zeros_like(dga_ref)

        dg_ref[...] += jnp.sum(dxg * xn, axis=0, keepdims=True)
        dsh_ref[0] += jnp.sum(dh, axis=0, keepdims=True)
        dsc_ref[0] += jnp.sum(dh * xg, axis=0, keepdims=True)
        dga_ref[0] += jnp.sum(dx1 * mix_ref[...], axis=0, keepdims=True)

    tile = lambda w: pl.BlockSpec((tm, w), lambda i: (i, 0))
    per_b = pl.BlockSpec((1, 1, d), lambda i: (i // per_seq, 0, 0))
    small = jax.ShapeDtypeStruct((batch, 1, d), F32)
    return _hosted(
        body, rider, name="up_backward", grid=(t // tm,),
        out_shape=[jax.ShapeDtypeStruct((t, d), F32), jax.ShapeDtypeStruct((t, d), BF16), small, small, small,
                   jax.ShapeDtypeStruct((1, d), F32)],
        in_specs=[pl.BlockSpec((2, tm, D_FF), lambda i: (0, i, 0)),
                  _resident((N_SHARD, half, wcol)), _resident((N_SHARD, half, wcol)), tile(d),
                  pl.BlockSpec((1, 1, 6 * d), lambda i: (i // per_seq, 0, 0)),
                  pl.BlockSpec((1, d), lambda i: (0, 0)), tile(d), tile(d)],
        out_specs=[tile(d), tile(d), per_b, per_b, per_b, pl.BlockSpec((1, d), lambda i: (0, 0))],
        scratch_shapes=[], compiler_params=_cparams(("arbitrary",), VMEM_BIG),
        args=[du, w_a, w_b, x1, mod3, g_ffn, dx2, mix])


def _up_weight_grad(h2, du, rider=None):
    t, d = h2.shape
    tk = TOKEN_TILE
    wcol = D_FF // 2
    half = d // 2
    n_k = t // tk

    def body(h_ref, du_ref, ga_ref, gb_ref, ga16_ref, gb16_ref):
        k = pl.program_id(1)

        @pl.when(k == 0)
        def _():
            ga_ref[...] = jnp.zeros_like(ga_ref)
            gb_ref[...] = jnp.zeros_like(gb_ref)

        du = du_ref[0]
        ga_ref[0] += _mm_tn(h_ref[:, :half], du)
        gb_ref[0] += _mm_tn(h_ref[:, half:], du)

        @pl.when(k == n_k - 1)
        def _():
            ga16_ref[...] = ga_ref[...].astype(BF16)
            gb16_ref[...] = gb_ref[...].astype(BF16)

    g_spec = pl.BlockSpec((1, half, wcol), lambda j, k: (j, 0, 0))
    f32_out = jax.ShapeDtypeStruct((N_SHARD, half, wcol), F32)
    b16_out = jax.ShapeDtypeStruct((N_SHARD, half, wcol), BF16)
    return _hosted(
        body, rider, name="up_weight_grad", grid=(N_SHARD, n_k),
        out_shape=[f32_out, f32_out, b16_out, b16_out],
        in_specs=[pl.BlockSpec((tk, d), lambda j, k: (k, 0)),
                  pl.BlockSpec((1, tk, wcol), lambda j, k: (j // 2, k, j % 2))],
        out_specs=[g_spec, g_spec, g_spec, g_spec], scratch_shapes=[],
        compiler_params=_cparams(("arbitrary", "arbitrary"), VMEM_BIG), args=[h2, du])


def _out_backward(dmix, w_out, oab, oa, ob, g_na, g_sw):
    t, d = dmix.shape
    tm = TOKEN_TILE
    hw = NA_WIDTH

    def body(dm_ref, w_ref, oab_ref, oa_ref, ob_ref, gna_ref, gsw_ref,
             doa_ref, dob_ref, gw_ref, gwb_ref, dgna_ref, dgsw_ref):
        @pl.when(pl.program_id(0) == 0)
        def _():
            gw_ref[...] = jnp.zeros_like(gw_ref)
            dgna_ref[...] = jnp.zeros_like(dgna_ref)
            dgsw_ref[...] = jnp.zeros_like(dgsw_ref)

        dm = dm_ref[...]
        gw_ref[...] += _mm_tn(oab_ref[...], dm)

        @pl.when(pl.program_id(0) == t // tm - 1)
        def _():
            gwb_ref[...] = gw_ref[...].astype(BF16)

        do = _mm_nt(dm, w_ref[...])
        for raw_ref, g_ref, dst_ref, dg_ref, lo in ((oa_ref, gna_ref, doa_ref, dgna_ref, 0),
                                                     (ob_ref, gsw_ref, dob_ref, dgsw_ref, hw)):
            r, xn = _rms_stats(raw_ref[...])
            dpart = do[:, lo:lo + hw]
            dg_ref[...] += jnp.sum(dpart * xn, axis=0, keepdims=True)
            dst_ref[...] = _rms_bwd(dpart * g_ref[...], xn, r).astype(BF16)

    tile = lambda w: pl.BlockSpec((tm, w), lambda i: (i, 0))
    vec = lambda w: pl.BlockSpec((1, w), lambda i: (0, 0))
    return pl.pallas_call(
        body, name="out_backward", grid=(t // tm,),
        out_shape=(jax.ShapeDtypeStruct((t, hw), BF16), jax.ShapeDtypeStruct((t, hw), BF16),
                   jax.ShapeDtypeStruct((d, d), F32), jax.ShapeDtypeStruct((d, d), BF16),
                   jax.ShapeDtypeStruct((1, hw), F32), jax.ShapeDtypeStruct((1, hw), F32)),
        in_specs=[tile(d), pl.BlockSpec((d, d), lambda i: (0, 0)), tile(d), tile(hw), tile(hw), vec(hw), vec(hw)],
        out_specs=(tile(hw), tile(hw), pl.BlockSpec((d, d), lambda i: (0, 0)), pl.BlockSpec((d, d), lambda i: (0, 0)),
                   vec(hw), vec(hw)),
        compiler_params=_cparams(("arbitrary",), VMEM_BIG),
    )(dmix, w_out, oab, oa, ob, g_na, g_sw)


def _na_backward(proj, d_o, lse, tiles, batch, seq, rider=None):
    t = proj.shape[0]
    n_rows = seq // GRID_W
    n_pairs = NA_WIDTH // LANES
    win = NA_ROWS * GRID_W
    n_tiles = 2 * NA_ROWS - 2

    def body(q_ref, k_ref, v_ref, do_ref, lse_ref, tp_ref, dq_ref, dk_ref, dv_ref, dtp_ref, km, vm, dk_acc, dv_acc):
        @pl.when(pl.program_id(1) == 0)
        def _():
            dtp_ref[...] = jnp.zeros_like(dtp_ref)

        _na_prepare(k_ref, v_ref, km, vm)
        dk_acc[...] = jnp.zeros_like(dk_acc)
        dv_acc[...] = jnp.zeros_like(dv_acc)
        low = lax.broadcasted_iota(jnp.int32, (win, LANES), 1) < HEAD_DIM

        def scores(r):
            rs, off = _na_window(r, n_rows)
            rows = pl.ds(pl.multiple_of(r * GRID_W, GRID_W), GRID_W)
            wrows = pl.ds(pl.multiple_of(rs * GRID_W, GRID_W), win)
            q, do = q_ref[rows, :], do_ref[rows, :]
            k2 = _na_pair_window(km, wrows)
            s = _na_scores(q, k2, tp_ref, off)
            dp = _mm_nt(do, _na_pair_window(vm, wrows))
            return rows, wrows, off, q, do, k2, s, dp

        def finish(rows, wrows, off, q, do, k2, s, dp):
            p = _pair_probs_from_lse(s, lse_ref[rows, :])
            parts = []
            for h in range(2):
                ph, dph = p[:, h * win:(h + 1) * win], dp[:, h * win:(h + 1) * win]
                dsh = ph * (dph - jnp.sum(ph * dph, axis=-1, keepdims=True))
                for w in range(NA_ROWS // 2):
                    dtp_ref[h, 2 * w - off + (NA_ROWS - 1)] += dsh[:, w * LANES:(w + 1) * LANES]
                parts.append(dsh)
            dsb = (jnp.concatenate(parts, axis=1) * QK_SCALE).astype(BF16)
            dq_ref[rows, :] = _mm(dsb, k2).astype(BF16)
            dk2 = _mm_tn(dsb, q)
            dv2 = _mm_tn(p.astype(BF16), do)
            dk_acc[wrows, :] += jnp.where(low, dk2[:win], dk2[win:])
            dv_acc[wrows, :] += jnp.where(low, dv2[:win], dv2[win:])

        def row_group(i, carry):
            for state in [scores(NA_GROUP * i + j) for j in range(NA_GROUP)]:
                finish(*state)
            return carry

        lax.fori_loop(0, n_rows // NA_GROUP, row_group, 0)
        dk_ref[...] = dk_acc[...].astype(BF16)
        dv_ref[...] = dv_acc[...].astype(BF16)

    blk = lambda off: pl.BlockSpec((seq, LANES), lambda p, b: (b, off + p))
    out = jax.ShapeDtypeStruct((t, NA_WIDTH), BF16)
    return _hosted(
        body, rider, name="na_backward", grid=(n_pairs, batch),
        out_shape=[out, out, out, jax.ShapeDtypeStruct(tiles.shape, F32)],
        in_specs=[blk(0), blk(n_pairs), blk(2 * n_pairs), blk(0), blk(0),
                  pl.BlockSpec((2, n_tiles, GRID_W, LANES), lambda p, b: (p, 0, 0, 0))],
        out_specs=[blk(0), blk(0), blk(0), pl.BlockSpec((2, n_tiles, GRID_W, LANES), lambda p, b: (p, 0, 0, 0))],
        scratch_shapes=[pltpu.VMEM((2, seq, LANES), BF16), pltpu.VMEM((2, seq, LANES), BF16),
                        pltpu.VMEM((seq, LANES), F32), pltpu.VMEM((seq, LANES), F32)],
        compiler_params=_cparams(("arbitrary", "arbitrary")), args=[proj, proj, proj, d_o, lse, tiles])


def _na_bias_grad(dtiles, expand):
    n = dtiles.shape[0]

    def body(t_ref, e_ref, o_ref):
        flat = jnp.concatenate([t_ref[:, qq, :] for qq in range(GRID_W)], axis=1)
        o_ref[...] = lax.dot_general(flat, e_ref[...], (((1,), (1,)), ((), ())),
                                     precision=lax.Precision.HIGHEST, preferred_element_type=F32)

    return pl.pallas_call(
        body, name="na_bias_grad",
        out_shape=jax.ShapeDtypeStruct((n, expand.shape[0]), F32),
        compiler_params=_cparams(vmem=VMEM_BIG),
    )(dtiles, expand)


def _sw_backward(proj, d_o, lse, sink, batch, seq, rider=None):
    t = proj.shape[0]
    n_pairs = SW_WIDTH // LANES
    q_blk = 3 * NA_WIDTH // LANES
    k_blk = q_blk + n_pairs
    n_blocks = seq // SW_BLOCK
    pad = seq + 2 * SW_BLOCK

    def body(sink_ref, q_ref, k_ref, v_ref, do_ref, lse_ref, dq_ref, dk_ref, dv_ref, dsk_ref,
             k_lo, k_hi, v_lo, v_hi, dk_loc, dv_loc, dk_tot, dv_tot):
        hp = pl.program_id(1)
        g = hp // 2
        _sw_prepare(k_ref, g, k_lo, k_hi, seq)
        _sw_prepare(v_ref, g, v_lo, v_hi, seq)
        dk_loc[...] = jnp.zeros_like(dk_loc)
        dv_loc[...] = jnp.zeros_like(dv_loc)

        @pl.when(hp == 0)
        def _():
            dk_tot[...] = jnp.zeros_like(dk_tot)
            dv_tot[...] = jnp.zeros_like(dv_tot)

        band = 3 * SW_BLOCK
        low = lax.broadcasted_iota(jnp.int32, (band, LANES), 1) < HEAD_DIM

        sinks = (sink_ref[2 * hp], sink_ref[2 * hp + 1])

        def scores(n):
            rows = pl.ds(pl.multiple_of(n * SW_BLOCK, SW_BLOCK), SW_BLOCK)
            wrows = pl.ds(pl.multiple_of(n * SW_BLOCK, SW_BLOCK), band)
            qb, do = q_ref[rows, :], do_ref[rows, :]
            k2 = jnp.concatenate([k_lo[wrows, :], k_hi[wrows, :]], axis=0)
            v2 = jnp.concatenate([v_lo[wrows, :], v_hi[wrows, :]], axis=0)
            return n, rows, wrows, qb, do, k2, _mm_nt(qb, k2) * QK_SCALE, _mm_nt(do, v2)

        def finish(sink_acc, n, rows, wrows, qb, do, k2, s2, dp):
            p, ps = _sw_probs_from_lse(s2, _sw_mask(n, seq), sinks, lse_ref[rows, :])
            parts, new = [], []
            for i in range(2):
                ph, dph = p[:, i * band:(i + 1) * band], dp[:, i * band:(i + 1) * band]
                delta = jnp.sum(ph * dph, axis=-1, keepdims=True)
                parts.append(ph * (dph - delta))
                new.append(sink_acc[i] - ps[i] * delta)
            dsb = (jnp.concatenate(parts, axis=1) * QK_SCALE).astype(BF16)
            dq_ref[rows, :] = _mm(dsb, k2)
            dk2 = _mm_tn(dsb, qb)
            dv2 = _mm_tn(p.astype(BF16), do)
            dk_loc[wrows, :] += jnp.where(low, dk2[:band], dk2[band:])
            dv_loc[wrows, :] += jnp.where(low, dv2[:band], dv2[band:])
            return tuple(new)

        def block_group(i, carry):
            for state in [scores(SW_GROUP_BLOCKS * i + j) for j in range(SW_GROUP_BLOCKS)]:
                carry = finish(carry, *state)
            return carry

        zero = jnp.zeros((SW_BLOCK, 1), F32)
        s0, s1 = lax.fori_loop(0, n_blocks // SW_GROUP_BLOCKS, block_group, (zero, zero))
        row = lax.broadcasted_iota(jnp.int32, (SUBLANES, LANES), 0)
        dsk_ref[0, 0] = jnp.where(row == 0, jnp.sum(s0), jnp.where(row == 1, jnp.sum(s1), 0.0))

        lane_s = lax.broadcasted_iota(jnp.int32, (seq, LANES), 1)
        mine_g = (lane_s // HEAD_DIM) == g
        for loc, tot in ((dk_loc, dk_tot), (dv_loc, dv_tot)):
            part = loc[SW_BLOCK:SW_BLOCK + seq, :]
            tot[...] += jnp.where(mine_g, part + pltpu.roll(part, HEAD_DIM, 1), 0.0)

        @pl.when(hp == n_pairs - 1)
        def _():
            dk_ref[...] = dk_tot[...]
            dv_ref[...] = dv_tot[...].astype(BF16)

    return _hosted(
        body, rider, name="sw_backward", grid=(batch, n_pairs),
        out_shape=[jax.ShapeDtypeStruct((t, SW_WIDTH), F32), jax.ShapeDtypeStruct((t, LANES), F32),
                   jax.ShapeDtypeStruct((t, LANES), BF16), jax.ShapeDtypeStruct((batch, n_pairs, SUBLANES, LANES), F32)],
        in_specs=[pl.BlockSpec(memory_space=pltpu.SMEM),
                  pl.BlockSpec((seq, LANES), lambda b, p: (b, q_blk + p)),
                  pl.BlockSpec((seq, LANES), lambda b, p: (b, k_blk)),
                  pl.BlockSpec((seq, LANES), lambda b, p: (b, k_blk + 1)),
                  pl.BlockSpec((seq, LANES), lambda b, p: (b, p)), pl.BlockSpec((seq, LANES), lambda b, p: (b, p))],
        out_specs=[pl.BlockSpec((seq, LANES), lambda b, p: (b, p)), pl.BlockSpec((seq, LANES), lambda b, p: (b, 0)),
                   pl.BlockSpec((seq, LANES), lambda b, p: (b, 0)),
                   pl.BlockSpec((1, 1, SUBLANES, LANES), lambda b, p: (b, p, 0, 0))],
        scratch_shapes=[pltpu.VMEM((pad, LANES), BF16)] * 4 + [pltpu.VMEM((pad, LANES), F32)] * 2
        + [pltpu.VMEM((seq, LANES), F32)] * 2,
        compiler_params=_cparams(("arbitrary", "arbitrary")), args=[sink, proj, proj, proj, d_o, lse])


def _in_backward(dqkv_a, dq_b, dk_b, dv_b, w_in_t, h1, x, mod3, g_attn, dx1, cos_t, sin_t, seq):
    t, d = x.shape
    tm = TOKEN_TILE
    per_seq = seq // tm
    batch = t // seq
    dqa, dka, dva = dqkv_a
    n_q = SW_WIDTH // LANES

    def body(dqa_ref, dka_ref, dva_ref, dqb_ref, dkb_ref, dvb_ref, w_ref, h_ref, x_ref, mod_ref, g_ref, dx1_ref,
             cos_ref, sin_ref, dx_ref, gw_ref, gwb_ref, dsh_ref, dsc_ref, dg_ref):
        i = pl.program_id(0)

        @pl.when(i == 0)
        def _():
            gw_ref[...] = jnp.zeros_like(gw_ref)
            dg_ref[...] = jnp.zeros_like(dg_ref)

        @pl.when(i % per_seq == 0)
        def _():
            dsh_ref[...] = jnp.zeros_like(dsh_ref)
            dsc_ref[...] = jnp.zeros_like(dsc_ref)

        dr = jnp.concatenate([dqb_ref[...], dkb_ref[...]], axis=1)
        cos = jnp.concatenate([cos_ref[...]] * (n_q + 1), axis=1)
        sin = jnp.concatenate([sin_ref[...]] * (n_q + 1), axis=1)
        dr = dr * cos + _rope_rot(dr * sin)
        dproj = jnp.concatenate([dqa_ref[...], dka_ref[...], dva_ref[...], dr.astype(BF16), dvb_ref[...]], axis=1)
        gw_ref[...] += _mm_tn(dproj, h_ref[...])

        @pl.when(i == t // tm - 1)
        def _():
            gwb_ref[...] = gw_ref[...].astype(BF16)

        dh = _mm(dproj, w_ref[...])
        scale = mod_ref[0, :, d:2 * d]
        r, xn = _rms_stats(x_ref[...])
        xg = xn * g_ref[...]
        dxg = dh * (1.0 + scale)
        dx_ref[...] = dx1_ref[...] + _rms_bwd(dxg * g_ref[...], xn, r)
        dg_ref[...] += jnp.sum(dxg * xn, axis=0, keepdims=True)
        dsh_ref[0] += jnp.sum(dh, axis=0, keepdims=True)
        dsc_ref[0] += jnp.sum(dh * xg, axis=0, keepdims=True)

    tile = lambda w: pl.BlockSpec((tm, w), lambda i: (i, 0))
    per_b = pl.BlockSpec((1, 1, d), lambda i: (i // per_seq, 0, 0))
    small = jax.ShapeDtypeStruct((batch, 1, d), F32)
    rope = pl.BlockSpec((tm, LANES), lambda i: (i % per_seq, 0))
    return pl.pallas_call(
        body, name="in_backward", grid=(t // tm,),
        out_shape=(jax.ShapeDtypeStruct((t, d), F32), jax.ShapeDtypeStruct((IN_WIDTH, d), F32),
                   jax.ShapeDtypeStruct((IN_WIDTH, d), BF16), small, small, jax.ShapeDtypeStruct((1, d), F32)),
        in_specs=[tile(NA_WIDTH), tile(NA_WIDTH), tile(NA_WIDTH), tile(SW_WIDTH), tile(LANES), tile(LANES),
                  _resident((IN_WIDTH, d)), tile(d), tile(d),
                  pl.BlockSpec((1, 1, 6 * d), lambda i: (i // per_seq, 0, 0)),
                  pl.BlockSpec((1, d), lambda i: (0, 0)), tile(d), rope, rope],
        out_specs=(tile(d), _resident((IN_WIDTH, d)), _resident((IN_WIDTH, d)),
                   per_b, per_b, pl.BlockSpec((1, d), lambda i: (0, 0))),
        compiler_params=_cparams(("arbitrary",), VMEM_BIG),
    )(dqa, dka, dva, dq_b, dk_b, dv_b, w_in_t, h1, x, mod3, g_attn, dx1, cos_t, sin_t)


def _ada_weight_grad(sc_all, dmod_cols):
    d = sc_all.shape[1]
    ncol = dmod_cols.shape[1]

    def body(s_ref, m_ref, o_ref):
        o_ref[...] = _mm_tn(s_ref[...].astype(BF16), m_ref[...].astype(BF16))

    return pl.pallas_call(
        body, name="ada_weight_grad",
        out_shape=jax.ShapeDtypeStruct((d, ncol), F32),
        compiler_params=_cparams(vmem=VMEM_BIG),
    )(sc_all, dmod_cols)


def _row_tile(rows, cols):
    target = max(SUBLANES, (1 << 20) // (4 * cols))
    best = rows
    for cand in range(SUBLANES, rows + 1, SUBLANES):
        if rows % cand == 0 and cand <= target:
            best = cand
    return best if rows % SUBLANES == 0 else rows


def _sum_slots(parts, name):
    n = len(parts)
    _, rows, cols = parts[0][0].shape
    tr = _row_tile(rows, cols)
    per = rows // tr

    def body(*refs):
        o_ref = refs[-1]
        for q in range(n):
            @pl.when(pl.program_id(0) == q)
            def _(q=q):
                p_ref, own_ref = refs[2 * q], refs[2 * q + 1]
                o_ref[...] = ((own_ref[...] + p_ref[0].astype(F32)) + p_ref[1].astype(F32)) + p_ref[2].astype(F32)

    in_specs, args = [], []
    for q, (recv, own) in enumerate(parts):
        in_specs.append(pl.BlockSpec((N_SHARD - 1, tr, cols), lambda p, i, q=q: (0, jnp.where(p == q, i, 0), 0)))
        in_specs.append(pl.BlockSpec((tr, cols), lambda p, i, q=q: (jnp.where(p == q, i, 0), 0)))
        args += [recv, own]
    return pl.pallas_call(
        body, name=name, grid=(n, per),
        out_shape=jax.ShapeDtypeStruct((n * rows, cols), F32),
        in_specs=in_specs, out_specs=pl.BlockSpec((tr, cols), lambda p, i: (p * per + i, 0)),
        compiler_params=_cparams(("arbitrary", "arbitrary")),
    )(*args)


def _adamw_math(w, g, m, v):
    m2 = ADAM_B1 * m + (1.0 - ADAM_B1) * g
    v2 = ADAM_B2 * v + (1.0 - ADAM_B2) * (g * g)
    m_hat = m2 / (1.0 - ADAM_B1 ** ADAM_STEP)
    v_hat = v2 / (1.0 - ADAM_B2 ** ADAM_STEP)
    return -ADAM_LR * (m_hat / (jnp.sqrt(v_hat) + ADAM_EPS) + ADAM_WD * w), m2, v2


def _small_step(partials, states, dmod, b_ada_state, rider=None):
    n_upd = len(states)
    moving = list(partials) + [dmod]
    n_mov = len(moving)
    all_states = list(states) + [b_ada_state]

    def body(*refs):
        mov, refs = refs[:n_mov], refs[n_mov:]
        wmv, refs = refs[:3 * (n_upd + 1)], refs[3 * (n_upd + 1):]
        res, refs = refs[:4 * (n_upd + 1)], refs[4 * (n_upd + 1):]
        sums_out, refs = refs[:n_mov - n_upd - 1], refs[n_mov - n_upd - 1:]
        dmod_out, refs = refs[0], refs[1:]
        everyone, (ssem, rsem) = refs[:n_mov], refs[n_mov:]
        x, y, c = _my_pos()
        me = 4 * x + 2 * y + c
        cps = []
        for a in range(n_mov):
            everyone[a][me] = mov[a][...]
            for k in range(1, N_DEV):
                peer = (_flip(x, (k >> 2) & 1), _flip(y, (k >> 1) & 1), _flip(c, k & 1))
                cps.append(pltpu.make_async_remote_copy(
                    src_ref=everyone[a].at[me], dst_ref=everyone[a].at[me], send_sem=ssem.at[a, k - 1],
                    recv_sem=rsem.at[a, k - 1], device_id=peer, device_id_type=MESH))
        for cp in cps:
            cp.start()
        for cp in cps:
            cp.wait_recv()

        def total(a):
            acc = everyone[a][0]
            for dev in range(1, N_DEV):
                acc = acc + everyone[a][dev]
            return acc

        grads = [total(a) for a in range(n_upd)]
        grads.append(jnp.sum(total(n_mov - 1), axis=0, keepdims=True))
        for j, g in enumerate(grads):
            delta, m2, v2 = _adamw_math(wmv[3 * j][...], g, wmv[3 * j + 1][...], wmv[3 * j + 2][...])
            res[4 * j][...] = g
            res[4 * j + 1][...] = delta
            res[4 * j + 2][...] = m2
            res[4 * j + 3][...] = v2
        for j in range(n_mov - n_upd - 1):
            sums_out[j][...] = total(n_upd + j)
        dmod_out[...] = everyone[n_mov - 1][...]
        for cp in cps:
            cp.wait_send()

    vm = pl.BlockSpec(memory_space=pltpu.VMEM)
    sds = jax.ShapeDtypeStruct
    out_shape = []
    for w, _, _ in all_states:
        out_shape += [sds(w.shape, F32)] * 4
    out_shape += [sds(p.shape, F32) for p in partials[n_upd:]]
    out_shape.append(sds((N_DEV,) + dmod.shape, F32))
    args = moving + [a for st in all_states for a in st]
    outs, rides = _hosted(
        body, rider, name="small_step", grid=(), out_shape=out_shape,
        in_specs=[vm] * len(args), out_specs=[vm] * len(out_shape),
        scratch_shapes=[pltpu.VMEM((N_DEV,) + a.shape, F32) for a in moving]
        + [pltpu.SemaphoreType.DMA((n_mov, N_DEV - 1)), pltpu.SemaphoreType.DMA((n_mov, N_DEV - 1))],
        compiler_params=_cparams(vmem=VMEM_BIG), args=args)
    return outs, rides


def _adamw(w, grads, m, v, name):
    rows, cols = w.shape
    tr = _row_tile(rows, cols)
    ng = len(grads)

    def body(*refs):
        w_ref = refs[0]
        g_refs = refs[1:1 + ng]
        m_ref, v_ref = refs[1 + ng], refs[2 + ng]
        g_out, d_out, m_out, v_out = refs[3 + ng:]
        g = g_refs[0][...]
        for extra in g_refs[1:]:
            g = g + extra[...]
        g_out[...] = g
        d_out[...], m_out[...], v_out[...] = _adamw_math(w_ref[...], g, m_ref[...], v_ref[...])

    spec = pl.BlockSpec((tr, cols), lambda i: (i, 0))
    out = jax.ShapeDtypeStruct((rows, cols), F32)
    return pl.pallas_call(
        body, name=name, grid=(rows // tr,),
        out_shape=(out, out, out, out),
        in_specs=[spec] * (3 + ng), out_specs=(spec, spec, spec, spec),
        compiler_params=_cparams(("arbitrary",)),
    )(w, *grads, m, v)


def _pack_rows(arrays):
    tile = SUBLANES * LANES
    rows, offsets, at = [], [], 0
    for a in arrays:
        flat = a.reshape(-1).astype(F32)
        n = -(-flat.shape[0] // tile) * tile
        rows.append(jnp.pad(flat, (0, n - flat.shape[0])).reshape(-1, LANES))
        offsets.append(at)
        at += n // LANES
    return jnp.concatenate(rows, axis=0), offsets


def _unpack_rows(packed, offsets, shapes):
    out = []
    for off, shape in zip(offsets, shapes):
        n = 1
        for s in shape:
            n *= s
        nrow = -(-n // LANES)
        out.append(packed[off:off + nrow].reshape(-1)[:n].reshape(shape))
    return out


def _rope_tables(seq):
    half = HEAD_DIM // 2
    inv = np.float32(ROPE_THETA) ** (-np.arange(half, dtype=np.float32) / np.float32(half))
    ang = (np.arange(seq, dtype=np.float32)[:, None] * inv[None, :]).astype(np.float64)
    cos, sin = np.cos(ang).astype(np.float32), np.sin(ang).astype(np.float32)
    cos_t = np.concatenate([cos, cos, cos, cos], axis=1)
    sin_t = np.concatenate([-sin, sin, -sin, sin], axis=1)
    return jnp.asarray(cos_t), jnp.asarray(sin_t)


def kernel(x, c, w_ada, b_ada, g_attn, w_in, na_rpb, sw_sink, g_na_out, g_sw_out, w_out, g_ffn, w_up, conv_w, conv_b, w_down, g_final, loss_target, m_w_ada, m_b_ada, m_g_attn, m_w_in, m_na_rpb, m_sw_sink, m_g_na_out, m_g_sw_out, m_w_out, m_g_ffn, m_w_up, m_conv_w, m_conv_b, m_w_down, m_g_final, v_w_ada, v_b_ada, v_g_attn, v_w_in, v_na_rpb, v_sw_sink, v_g_na_out, v_g_sw_out, v_w_out, v_g_ffn, v_w_up, v_conv_w, v_conv_b, v_w_down, v_g_final):
    batch, seq, d = x.shape
    t = batch * seq
    assert d == D_MODEL and seq % (NA_ROWS * GRID_W) == 0 and seq % TOKEN_TILE == 0 and batch <= SUBLANES
    shard = 2 * lax.axis_index("x") + lax.axis_index("y")
    xt = x.reshape(t, d)
    tgt = loss_target.reshape(t, d)

    c8 = jnp.pad(c, ((0, SUBLANES - batch), (0, 0)))
    w_in_t_s = jnp.transpose(w_in[0]).astype(BF16)
    (mod8, sc_all), (w_in_g,) = _ada_forward(c8, w_ada[0], b_ada, _Rider("gather", [w_in_t_s]))
    mod3 = mod8[:batch].reshape(batch, 1, 6 * d)
    w_in_t = w_in_g.reshape(IN_WIDTH, d)

    cos_t, sin_t = _rope_tables(seq)
    (h1, proj), _ = _in_proj(xt, mod3, g_attn, w_in_t, cos_t, sin_t, seq)
    n_heads = NA_WIDTH // HEAD_DIM
    n_tiles, n_dc = 2 * NA_ROWS - 2, 2 * NA_COLS - 1
    expand, neg_mask = _na_bias_pattern()
    rpb = na_rpb[0]
    rows2 = jnp.concatenate([rpb[:, :-1, :], rpb[:, 1:, :]], axis=2).reshape(n_heads * n_tiles, 2 * n_dc)
    rows2 = jnp.pad(rows2, ((0, 0), (0, GRID_W - 2 * n_dc)))
    tiles = _na_bias_tiles(rows2, expand, neg_mask).reshape(n_heads, n_tiles, GRID_W, LANES)
    sink = sw_sink[0]
    w_up_b16 = w_up[0].astype(BF16)
    (oa, lse_a), (w_up_a, w_down_g) = _na_forward(proj, tiles, batch, seq,
                                                  _Rider("gather", [w_up_b16[:d // 2], w_down[0].astype(BF16)]))
    (ob, lse_b), (w_up_b, conv_w_g, w_out_g) = _sw_forward(
        proj, sink, batch, seq, _Rider("gather", [w_up_b16[d // 2:], conv_w[0], w_out[0].astype(BF16)]))
    w_up_f = (w_up_a, w_up_b)
    w_out_f = w_out_g.reshape(d, d)
    conv_w_f = jnp.transpose(conv_w_g, (1, 0, 2)).reshape(3, D_FF)
    oab, mix, x1, h2 = _out_proj(oa, ob, g_na_out, g_sw_out, w_out_f, xt, mod3, g_ffn, seq)
    (u,), _ = _up_proj(h2, w_up_f)
    w_down_f = w_down_g.reshape(D_FF, d)
    a = _conv_gate(u, conv_w_f, conv_b, batch, seq)
    dx2, dffn, loss_part, dgate_f, dg_final = _down_and_loss(a, w_down_f, x1, mod3, g_final.reshape(1, d), tgt, seq)

    gw_down, gw_down_b = _down_weight_grad(a, dffn)
    blocks = lambda g, rows: g.reshape(N_SHARD, rows // N_SHARD, d)
    (du, gconv_w, gconv_b), (recv_down, own_down) = _ffn_backward(
        dffn, w_down_f, u, conv_w_f, conv_b, batch, seq,
        _Rider("scatter", [blocks(gw_down_b, D_FF)], [blocks(gw_down, D_FF)]))
    (gw_up_top, gw_up_bot, gw_up_top_b, gw_up_bot_b), _ = _up_weight_grad(h2, du)
    (dx1, dmix, dshift_f, dscale_f, dgate_a, dg_ffn), _ = _up_backward(du, w_up_f, x1, mod3, g_ffn, dx2, mix, seq)
    doa, dob, gw_out, gw_out_b, dg_na, dg_sw = _out_backward(dmix, w_out_f, oab, oa, ob, g_na_out, g_sw_out)
    (dqa, dka, dva, dtiles), (recv_up_bot, own_up_bot) = _na_backward(
        proj, doa, lse_a, tiles, batch, seq, _Rider("scatter", [gw_up_bot_b], [gw_up_bot]))
    (dq_b, dk_b, dv_b, dsink_parts), (recv_out, recv_up_top, own_out, own_up_top) = _sw_backward(
        proj, dob, lse_b, sink, batch, seq,
        _Rider("scatter", [blocks(gw_out_b, d), gw_up_top_b], [blocks(gw_out, d), gw_up_top]))
    gx, gw_in_t, gw_in_b, dshift_a, dscale_a, dg_attn = _in_backward(
        (dqa, dka, dva), dq_b, dk_b, dv_b, w_in_t, h1, xt, mod3, g_attn, dx1, cos_t, sin_t, seq)

    red = _na_bias_grad(dtiles.reshape(n_heads * n_tiles, GRID_W, LANES), expand)[:, :2 * n_dc]
    red = red.reshape(n_heads, n_tiles, 2, n_dc)
    zero_row = jnp.zeros((n_heads, 1, n_dc), F32)
    g_rpb = (jnp.concatenate([red[:, :, 0, :], zero_row], axis=1)
             + jnp.concatenate([zero_row, red[:, :, 1, :]], axis=1))
    g_sink = jnp.sum(dsink_parts[:, :, :2, 0], axis=0).reshape(SW_WIDTH // HEAD_DIM)

    dmod = jnp.concatenate([dshift_a, dscale_a, dgate_a, dshift_f, dscale_f, dgate_f], axis=2).reshape(batch, 6 * d)
    rpb_shape = na_rpb.shape[1:]
    states = [(g_attn, m_g_attn, v_g_attn),
              (na_rpb.reshape(rpb_shape), m_na_rpb.reshape(rpb_shape), v_na_rpb.reshape(rpb_shape)),
              (sw_sink, m_sw_sink, v_sw_sink), (g_na_out, m_g_na_out, v_g_na_out), (g_sw_out, m_g_sw_out, v_g_sw_out),
              (g_ffn, m_g_ffn, v_g_ffn), (conv_b, m_conv_b, v_conv_b),
              (g_final.reshape(1, d), m_g_final.reshape(1, d), v_g_final.reshape(1, d))]
    partials = [dg_attn, g_rpb, g_sink.reshape(sw_sink.shape), dg_na, dg_sw, dg_ffn, gconv_b, dg_final,
                gconv_w, loss_part]
    mine = [None, _sum_slots([(recv_out, own_out)], "sum_w_out"),
            _sum_slots([(recv_up_top, own_up_top), (recv_up_bot, own_up_bot)], "sum_w_up"),
            _sum_slots([(recv_down, own_down)], "sum_w_down")]
    small, (recv_in, own_in, *theirs) = _small_step(
        partials, states, dmod, (b_ada, m_b_ada, v_b_ada),
        _Riders([_Rider("scatter", [blocks(gw_in_b, IN_WIDTH)], [blocks(gw_in_t, IN_WIDTH)]),
                 _Rider("swap", mine[1:])]))
    r_small = [small[4 * j:4 * j + 4] for j in range(len(states) + 1)]
    g_conv_w_full, loss_sum, dmod_all = small[4 * (len(states) + 1):]
    loss = loss_sum[0, 0]
    mine[0] = _sum_slots([(recv_in, own_in)], "sum_w_in")
    theirs = _ride_alone(_Rider("swap", mine[:1]), "swap_sibling") + theirs
    dmod_rows = jnp.pad(dmod_all, ((0, 0), (0, SUBLANES - batch), (0, 0))).reshape(N_DEV * SUBLANES, 6 * d)
    ncol = w_ada.shape[2]
    g_w_ada = _ada_weight_grad(sc_all, lax.dynamic_slice(dmod_rows, (0, shard * ncol), (N_DEV * SUBLANES, ncol)))
    cshard = conv_w.shape[2]
    g_conv_w = lax.dynamic_slice(g_conv_w_full, (0, shard * cshard), (3, cshard))

    def big(w, m, v, g_parts, name):
        shape = w.shape
        outs = _adamw(w[0], g_parts, m[0], v[0], name)
        return [o.reshape(shape) for o in outs]

    r_w_ada = big(w_ada, m_w_ada, v_w_ada, [g_w_ada], "adamw_w_ada")
    r_w_in = [jnp.transpose(o).reshape(w_in.shape) for o in
              _adamw(jnp.transpose(w_in[0]), [mine[0], theirs[0]], jnp.transpose(m_w_in[0]), jnp.transpose(v_w_in[0]),
                     "adamw_w_in")]
    r_w_out = big(w_out, m_w_out, v_w_out, [mine[1], theirs[1]], "adamw_w_out")
    r_w_up = big(w_up, m_w_up, v_w_up, [mine[2], theirs[2]], "adamw_w_up")
    r_w_down = big(w_down, m_w_down, v_w_down, [mine[3], theirs[3]], "adamw_w_down")

    r_conv_w = big(conv_w, m_conv_w, v_conv_w, [g_conv_w], "adamw_conv_w")

    def pick(k):
        ga_, rpb_, sk_, gna_, gsw_, gf_, cb_, gfin_, b_ = [r[k] for r in r_small]
        return [r_w_ada[k], b_, ga_, r_w_in[k], rpb_.reshape(na_rpb.shape), sk_, gna_, gsw_, r_w_out[k], gf_,
                r_w_up[k], r_conv_w[k], cb_, r_w_down[k], gfin_.reshape(d)]

    return (loss, gx.reshape(batch, seq, d), *pick(0), *pick(1), *pick(2), *pick(3))
```

```python
import functools

import jax
import jax.numpy as jnp
import numpy as np
from jax import lax
from jax.experimental import pallas as pl
from jax.experimental.pallas import tpu as pltpu

F32 = jnp.float32
BF16 = jnp.bfloat16
MESH = pl.DeviceIdType.MESH

D_MODEL = 1024
HEAD_DIM = 64
NA_WIDTH = 512
SW_WIDTH = 512
SW_KV_WIDTH = 128
IN_WIDTH = 2304
D_FF = 2816
GRID_W = 64
NA_ROWS = 8
NA_COLS = 16
SW_BLOCK = 128
ROPE_THETA = 10000.0
EPS = 1e-6
NEG = -1e30
QK_SCALE = HEAD_DIM ** -0.5

ADAM_LR = 0.001
ADAM_B1 = 0.9
ADAM_B2 = 0.999
ADAM_EPS = 1e-08
ADAM_WD = 0.01
ADAM_STEP = 10

N_SHARD = 4
N_DEV = 8
LANES = 128
SUBLANES = 8
TOKEN_TILE = 512
FF_TILE = 256
CONV_CHUNK = 64
NA_GROUP = 8
SW_GROUP_BLOCKS = 8
VMEM_BIG = 56 * 1024 * 1024


def _mm(a, b):
    return jnp.dot(a, b, preferred_element_type=F32)


def _mm_nt(a, b):
    return lax.dot_general(a, b, (((1,), (1,)), ((), ())), preferred_element_type=F32)


def _mm_tn(a, b):
    return lax.dot_general(a, b, (((0,), (0,)), ((), ())), preferred_element_type=F32)


def _cparams(sem=None, vmem=None):
    kw = {}
    if sem is not None:
        kw["dimension_semantics"] = sem
    if vmem is not None:
        kw["vmem_limit_bytes"] = vmem
    return pltpu.CompilerParams(**kw)


def _resident(shape):
    return pl.BlockSpec(shape, lambda i: (0,) * len(shape), pipeline_mode=pl.Buffered(1))


def _sigmoid(x):
    return 1.0 / (1.0 + jnp.exp(-x))


def _rms_stats(x):
    r = lax.rsqrt(jnp.mean(x * x, axis=-1, keepdims=True) + EPS)
    return r, x * r


def _rms_bwd(dxn, xn, r):
    return r * (dxn - xn * jnp.mean(dxn * xn, axis=-1, keepdims=True))


def _my_pos():
    return lax.axis_index("x"), lax.axis_index("y"), lax.axis_index("c")


def _flip(v, bit):
    return 1 - v if bit else v


def _ada_forward(c8, w_ada, b_ada, rider):
    d = c8.shape[1]
    ncol = w_ada.shape[1]

    def body(c_ref, w_ref, b_ref, mod_ref, sc_ref, m_scr, mod_buf, ssem, rsem, ssem2, rsem2):
        x, y, c = _my_pos()
        me = 4 * x + 2 * y + c
        shard = 2 * x + y
        cv = c_ref[...]
        my_rows = pl.ds(pl.multiple_of(me * SUBLANES, SUBLANES), SUBLANES)
        sc_ref[my_rows, :] = cv * _sigmoid(cv)

        def copy1(k):
            peer = (_flip(x, (k >> 2) & 1), _flip(y, (k >> 1) & 1), _flip(c, k & 1))
            return pltpu.make_async_remote_copy(
                src_ref=sc_ref.at[my_rows, :], dst_ref=sc_ref.at[my_rows, :],
                send_sem=ssem.at[k - 1], recv_sem=rsem.at[k - 1], device_id=peer, device_id_type=MESH)

        sends = [copy1(k) for k in range(1, N_DEV)]
        for cp in sends:
            cp.start()
        for cp in sends:
            cp.wait_recv()
        m_scr[...] = _mm(sc_ref[...].astype(BF16), w_ref[...].astype(BF16))

        def copy2(k):
            px, py = _flip(x, (k >> 1) & 1), _flip(y, k & 1)
            rows = pl.ds(pl.multiple_of((4 * px + 2 * py + c) * SUBLANES, SUBLANES), SUBLANES)
            return pltpu.make_async_remote_copy(
                src_ref=m_scr.at[rows, :], dst_ref=mod_buf.at[shard],
                send_sem=ssem2.at[k - 1], recv_sem=rsem2.at[k - 1], device_id=(px, py, c), device_id_type=MESH)

        sends2 = [copy2(k) for k in range(1, N_SHARD)]
        for cp in sends2:
            cp.start()
        mod_buf[shard] = m_scr[my_rows, :]
        for cp in sends2:
            cp.wait_recv()
        for s in range(N_SHARD):
            mod_ref[:, s * ncol:(s + 1) * ncol] = mod_buf[s] + b_ref[:, s * ncol:(s + 1) * ncol]
        for cp in sends + sends2:
            cp.wait_send()

    vm = pl.BlockSpec(memory_space=pltpu.VMEM)
    return _hosted(
        body, rider, name="ada_forward", grid=(),
        out_shape=(jax.ShapeDtypeStruct((SUBLANES, N_SHARD * ncol), F32),
                   jax.ShapeDtypeStruct((N_DEV * SUBLANES, d), F32)),
        in_specs=[vm, vm, vm], out_specs=(vm, vm),
        scratch_shapes=[pltpu.VMEM((N_DEV * SUBLANES, ncol), F32), pltpu.VMEM((N_SHARD, SUBLANES, ncol), F32),
                        pltpu.SemaphoreType.DMA((N_DEV - 1,)), pltpu.SemaphoreType.DMA((N_DEV - 1,)),
                        pltpu.SemaphoreType.DMA((N_SHARD - 1,)), pltpu.SemaphoreType.DMA((N_SHARD - 1,))],
        compiler_params=_cparams(vmem=VMEM_BIG), args=[c8, w_ada, b_ada])


class _Rider:
    def __init__(self, kind, srcs, owns=()):
        self.kind, self.srcs, self.owns = kind, list(srcs), list(owns)
        n = len(self.srcs)
        sds = jax.ShapeDtypeStruct
        dma = pltpu.SemaphoreType.DMA
        if kind == "gather":
            self.out_shapes = [sds((N_SHARD,) + s.shape, s.dtype) for s in self.srcs]
            self.sems = [dma((n, N_SHARD - 1)), dma((n, N_SHARD - 1)), dma((n, N_SHARD - 1)), dma((n, N_SHARD - 1)),
                         dma((n,)), dma((n,))]
        elif kind == "scatter":
            self.out_shapes = ([sds((N_SHARD - 1,) + s.shape[1:], s.dtype) for s in self.srcs]
                               + [sds(o.shape[1:], o.dtype) for o in self.owns])
            m = max(len(self.owns), 1)
            self.sems = [dma((n, N_SHARD - 1)), dma((n, N_SHARD - 1)), dma((m,)), dma((m,))]
        else:
            self.out_shapes = [sds(s.shape, s.dtype) for s in self.srcs]
            self.sems = [dma((n,)), dma((n,))]

    @property
    def inputs(self):
        return self.srcs + self.owns

    def _halved(self, i):
        a = self.srcs[i]
        tile_rows = SUBLANES * (4 // jnp.dtype(a.dtype).itemsize)
        return self.kind == "gather" and a.shape[0] % (2 * tile_rows) == 0

    def copies(self, ins, outs, sems):
        n = len(self.srcs)
        x, y, c = _my_pos()
        shard = 2 * x + y
        remote, relay = [], []
        if self.kind == "swap":
            ssem, rsem = sems
            for i in range(n):
                remote.append(pltpu.make_async_remote_copy(
                    src_ref=ins[i], dst_ref=outs[i], send_sem=ssem.at[i], recv_sem=rsem.at[i],
                    device_id=(x, y, 1 - c), device_id_type=MESH))
            return remote, relay
        if self.kind == "gather":
            ssem, rsem, ssem2, rsem2, sib_s, sib_r = sems
        else:
            ssem, rsem, sib_s, sib_r = sems
        for i in range(n):
            if self.kind == "gather":
                remote.append(pltpu.make_async_remote_copy(
                    src_ref=ins[i], dst_ref=outs[i].at[shard], send_sem=sib_s.at[i], recv_sem=sib_r.at[i],
                    device_id=(x, y, 1 - c), device_id_type=MESH))
                half = ins[i].shape[0] // 2
                mine = pl.ds(pl.multiple_of(c * half, half), half) if self._halved(i) else None
            for k in range(1, N_SHARD):
                px, py = _flip(x, (k >> 1) & 1), _flip(y, k & 1)
                if self.kind == "gather":
                    src, dst = ins[i], outs[i].at[shard]
                    if mine is not None:
                        src, dst = src.at[mine], dst.at[mine]
                        got = outs[i].at[2 * px + py].at[mine]
                        relay.append(pltpu.make_async_remote_copy(
                            src_ref=got, dst_ref=got, send_sem=ssem2.at[i, k - 1], recv_sem=rsem2.at[i, k - 1],
                            device_id=(x, y, 1 - c), device_id_type=MESH))
                else:
                    src, dst = ins[i].at[2 * px + py], outs[i].at[k - 1]
                remote.append(pltpu.make_async_remote_copy(
                    src_ref=src, dst_ref=dst, send_sem=ssem.at[i, k - 1], recv_sem=rsem.at[i, k - 1],
                    device_id=(px, py, c), device_id_type=MESH))
        if self.kind == "scatter":
            for i in range(len(self.owns)):
                remote.append(pltpu.make_async_remote_copy(
                    src_ref=ins[n + i].at[shard], dst_ref=outs[n + i], send_sem=sib_s.at[i], recv_sem=sib_r.at[i],
                    device_id=(x, y, 1 - c), device_id_type=MESH))
        return remote, relay

    def start(self, ins, outs, sems):
        remote, _ = self.copies(ins, outs, sems)
        for cp in remote:
            cp.start()

    def wait(self, ins, outs, sems):
        remote, relay = self.copies(ins, outs, sems)
        for cp in remote:
            cp.wait_recv()
        for cp in relay:
            cp.start()
        for cp in relay:
            cp.wait_recv()
        for cp in remote + relay:
            cp.wait_send()


class _Riders:
    def __init__(self, riders):
        self.riders = list(riders)
        self.inputs = [a for r in self.riders for a in r.inputs]
        self.out_shapes = [s for r in self.riders for s in r.out_shapes]
        self.sems = [s for r in self.riders for s in r.sems]

    def _split(self, ins, outs, sems):
        for r in self.riders:
            ni, no, ns = len(r.inputs), len(r.out_shapes), len(r.sems)
            yield r, ins[:ni], outs[:no], sems[:ns]
            ins, outs, sems = ins[ni:], outs[no:], sems[ns:]

    def start(self, ins, outs, sems):
        for r, i, o, s in self._split(ins, outs, sems):
            r.start(i, o, s)

    def wait(self, ins, outs, sems):
        for r, i, o, s in self._split(ins, outs, sems):
            r.wait(i, o, s)


def _hosted(body, rider, *, name, grid, out_shape, in_specs, out_specs, scratch_shapes, compiler_params, args):
    out_shape, out_specs = list(out_shape), list(out_specs)
    if rider is None:
        outs = pl.pallas_call(body, name=name, grid=grid, out_shape=tuple(out_shape), in_specs=list(in_specs),
                              out_specs=tuple(out_specs), scratch_shapes=list(scratch_shapes),
                              compiler_params=compiler_params)(*args)
        return list(outs), []
    n_in, n_out, n_scr = len(in_specs), len(out_shape), len(scratch_shapes)
    nr_in, nr_out = len(rider.inputs), len(rider.out_shapes)
    n_steps = 1
    for size in grid:
        n_steps *= size

    def full(*refs):
        ins, refs = refs[:n_in], refs[n_in:]
        r_in, refs = refs[:nr_in], refs[nr_in:]
        outs, refs = refs[:n_out], refs[n_out:]
        r_out, refs = refs[:nr_out], refs[nr_out:]
        scr, sems = refs[:n_scr], refs[n_scr:]
        if grid:
            step = 0
            for ax, size in enumerate(grid):
                step = step * size + pl.program_id(ax)
            pl.when(step == 0)(lambda: rider.start(r_in, r_out, sems))
            body(*ins, *outs, *scr)
            pl.when(step == n_steps - 1)(lambda: rider.wait(r_in, r_out, sems))
        else:
            rider.start(r_in, r_out, sems)
            body(*ins, *outs, *scr)
            rider.wait(r_in, r_out, sems)

    hbm = pl.BlockSpec(memory_space=pl.ANY)
    res = pl.pallas_call(
        full, name=name, grid=grid, out_shape=tuple(out_shape + rider.out_shapes),
        in_specs=list(in_specs) + [hbm] * nr_in, out_specs=tuple(out_specs + [hbm] * nr_out),
        scratch_shapes=list(scratch_shapes) + rider.sems, compiler_params=compiler_params,
    )(*args, *rider.inputs)
    return list(res[:n_out]), list(res[n_out:])


def _ride_alone(rider, name):
    return _hosted(lambda: None, rider, name=name, grid=(), out_shape=[], in_specs=[], out_specs=[], scratch_shapes=[],
                   compiler_params=_cparams(), args=[])[1]


def _allreduce_small(packed, rider=None):
    r = packed.shape[0]

    def body(p_ref, sum_ref, all_ref, ssem, rsem):
        x, y, c = _my_pos()
        me = 4 * x + 2 * y + c
        all_ref[me] = p_ref[...]
        cps = []
        for k in range(1, N_DEV):
            peer = (_flip(x, (k >> 2) & 1), _flip(y, (k >> 1) & 1), _flip(c, k & 1))
            cps.append(pltpu.make_async_remote_copy(
                src_ref=all_ref.at[me], dst_ref=all_ref.at[me], send_sem=ssem.at[k - 1], recv_sem=rsem.at[k - 1],
                device_id=peer, device_id_type=MESH))
        for cp in cps:
            cp.start()
        for cp in cps:
            cp.wait_recv()
        acc = all_ref[0]
        for dev in range(1, N_DEV):
            acc = acc + all_ref[dev]
        sum_ref[...] = acc
        for cp in cps:
            cp.wait_send()

    vm = pl.BlockSpec(memory_space=pltpu.VMEM)
    return _hosted(
        body, rider, name="allreduce_small", grid=(),
        out_shape=[jax.ShapeDtypeStruct((r, LANES), F32), jax.ShapeDtypeStruct((N_DEV, r, LANES), F32)],
        in_specs=[vm], out_specs=[vm, vm],
        scratch_shapes=[pltpu.SemaphoreType.DMA((N_DEV - 1,)), pltpu.SemaphoreType.DMA((N_DEV - 1,))],
        compiler_params=_cparams(), args=[packed])


def _rope_rot(t):
    w = t.shape[1]
    lane = lax.broadcasted_iota(jnp.int32, t.shape, 1)
    first = (lane % HEAD_DIM) < (HEAD_DIM // 2)
    return jnp.where(first, pltpu.roll(t, w - HEAD_DIM // 2, 1), pltpu.roll(t, HEAD_DIM // 2, 1))


def _in_proj(x, mod3, g_attn, w_in_t, cos_t, sin_t, seq, rider=None):
    t, d = x.shape
    tm = TOKEN_TILE
    per_seq = seq // tm
    rope_lo, rope_hi = 3 * NA_WIDTH, 3 * NA_WIDTH + SW_WIDTH + SW_KV_WIDTH
    n_rep = (rope_hi - rope_lo) // LANES

    def body(x_ref, mod_ref, g_ref, w_ref, cos_ref, sin_ref, h_ref, p_ref):
        r, xn = _rms_stats(x_ref[...])
        shift, scale = mod_ref[0, :, 0:d], mod_ref[0, :, d:2 * d]
        hb = ((xn * g_ref[...]) * (1.0 + scale) + shift).astype(BF16)
        h_ref[...] = hb
        p_ref[:, :rope_lo] = _mm_nt(hb, w_ref[:rope_lo, :]).astype(BF16)
        pr = _mm_nt(hb, w_ref[rope_lo:rope_hi, :])
        cos = jnp.concatenate([cos_ref[...]] * n_rep, axis=1)
        sin = jnp.concatenate([sin_ref[...]] * n_rep, axis=1)
        p_ref[:, rope_lo:rope_hi] = (pr * cos + _rope_rot(pr) * sin).astype(BF16)
        p_ref[:, rope_hi:] = _mm_nt(hb, w_ref[rope_hi:, :]).astype(BF16)

    return _hosted(
        body, rider, name="in_proj", grid=(t // tm,),
        out_shape=[jax.ShapeDtypeStruct((t, d), BF16), jax.ShapeDtypeStruct((t, IN_WIDTH), BF16)],
        in_specs=[pl.BlockSpec((tm, d), lambda i: (i, 0)),
                  pl.BlockSpec((1, 1, 6 * d), lambda i: (i // per_seq, 0, 0)),
                  pl.BlockSpec((1, d), lambda i: (0, 0)),
                  pl.BlockSpec((IN_WIDTH, d), lambda i: (0, 0)),
                  pl.BlockSpec((tm, LANES), lambda i: (i % per_seq, 0)),
                  pl.BlockSpec((tm, LANES), lambda i: (i % per_seq, 0))],
        out_specs=[pl.BlockSpec((tm, d), lambda i: (i, 0)), pl.BlockSpec((tm, IN_WIDTH), lambda i: (i, 0))],
        scratch_shapes=[], compiler_params=_cparams(("arbitrary",), VMEM_BIG),
        args=[x, mod3, g_attn, w_in_t, cos_t, sin_t])


def _na_bias_pattern():
    n_dc = 2 * NA_COLS - 1
    j = np.arange(GRID_W)[:, None]
    m = np.arange(GRID_W * LANES)[None, :]
    q, lane = m // LANES, m % LANES
    k = lane % GRID_W
    cs = np.clip(q - NA_COLS // 2, 0, GRID_W - NA_COLS)
    ok = (k >= cs) & (k < cs + NA_COLS)
    hit = ok & (j < 2 * n_dc) & (lane // GRID_W == j // n_dc) & (k - q + (NA_COLS - 1) == j % n_dc)
    return jnp.asarray(hit.astype(np.float32)), jnp.asarray(np.where(ok, 0.0, NEG).astype(np.float32))


def _na_bias_tiles(rows2, expand, mask):
    n, width = rows2.shape[0], expand.shape[1]
    q_step = 16
    step = q_step * LANES

    def body(r_ref, e_ref, m_ref, o_ref):
        flat = jnp.dot(r_ref[...], e_ref[...], precision=lax.Precision.HIGHEST,
                       preferred_element_type=F32) + m_ref[...]
        for qq in range(q_step):
            o_ref[:, qq, :] = flat[:, qq * LANES:(qq + 1) * LANES]

    return pl.pallas_call(
        body, name="na_bias_tiles", grid=(width // step,),
        out_shape=jax.ShapeDtypeStruct((n, GRID_W, LANES), F32),
        in_specs=[pl.BlockSpec(rows2.shape, lambda i: (0, 0)), pl.BlockSpec((expand.shape[0], step), lambda i: (0, i)),
                  pl.BlockSpec((1, step), lambda i: (0, i))],
        out_specs=pl.BlockSpec((n, q_step, LANES), lambda i: (0, i, 0)),
        compiler_params=_cparams(("arbitrary",)),
    )(rows2, expand, mask)


def _na_prepare(k_ref, v_ref, km, vm):
    lane = lax.broadcasted_iota(jnp.int32, k_ref.shape, 1)
    low = lane < HEAD_DIM
    kv = k_ref[...]
    vv = v_ref[...]
    zero = jnp.zeros_like(kv)
    km[0] = jnp.where(low, kv, zero)
    km[1] = jnp.where(low, zero, kv)
    vm[0] = jnp.where(low, vv, zero)
    vm[1] = jnp.where(low, zero, vv)


def _na_window(r, n_rows):
    rs = jnp.clip(r - NA_ROWS // 2, 0, n_rows - NA_ROWS)
    return rs, r - rs


def _na_pair_window(ref, wrows):
    return jnp.concatenate([ref[0, wrows, :], ref[1, wrows, :]], axis=0)


def _na_scores(q, k2, tp_ref, off):
    bias = jnp.concatenate([tp_ref[h, 2 * w - off + (NA_ROWS - 1)] for h in range(2) for w in range(NA_ROWS // 2)],
                           axis=1)
    return _mm_nt(q, k2) * QK_SCALE + bias


def _pair_lse_block(lse):
    lane = lax.broadcasted_iota(jnp.int32, (lse[0].shape[0], LANES), 1)
    return jnp.where(lane < HEAD_DIM, lse[0], lse[1])


def _pair_softmax(s):
    win = s.shape[1] // 2
    halves, lse = [], []
    for h in range(2):
        sh = s[:, h * win:(h + 1) * win]
        m = jnp.max(sh, axis=-1, keepdims=True)
        e = jnp.exp(sh - m)
        l = jnp.sum(e, axis=-1, keepdims=True)
        halves.append(e / l)
        lse.append(m + jnp.log(l))
    return jnp.concatenate(halves, axis=1), _pair_lse_block(lse)


def _pair_probs_from_lse(s, lse_block):
    win = s.shape[1] // 2
    return jnp.concatenate([jnp.exp(s[:, h * win:(h + 1) * win] - lse_block[:, h * HEAD_DIM:h * HEAD_DIM + 1])
                            for h in range(2)], axis=1)


def _na_forward(proj, tiles, batch, seq, rider=None):
    t = proj.shape[0]
    n_rows = seq // GRID_W
    n_pairs = NA_WIDTH // LANES
    win = NA_ROWS * GRID_W

    def body(q_ref, k_ref, v_ref, tp_ref, o_ref, lse_ref, km, vm):
        _na_prepare(k_ref, v_ref, km, vm)

        def scores(r):
            rs, off = _na_window(r, n_rows)
            rows = pl.ds(pl.multiple_of(r * GRID_W, GRID_W), GRID_W)
            wrows = pl.ds(pl.multiple_of(rs * GRID_W, GRID_W), win)
            return rows, wrows, _na_scores(q_ref[rows, :], _na_pair_window(km, wrows), tp_ref, off)

        def finish(rows, wrows, s):
            p, lse = _pair_softmax(s)
            lse_ref[rows, :] = lse
            o_ref[rows, :] = _mm(p.astype(BF16), _na_pair_window(vm, wrows))

        def row_group(i, carry):
            for state in [scores(NA_GROUP * i + j) for j in range(NA_GROUP)]:
                finish(*state)
            return carry

        lax.fori_loop(0, n_rows // NA_GROUP, row_group, 0)

    return _hosted(
        body, rider, name="na_forward", grid=(batch, n_pairs),
        out_shape=[jax.ShapeDtypeStruct((t, NA_WIDTH), F32), jax.ShapeDtypeStruct((t, NA_WIDTH), F32)],
        in_specs=[pl.BlockSpec((seq, LANES), lambda b, p: (b, p)),
                  pl.BlockSpec((seq, LANES), lambda b, p: (b, n_pairs + p)),
                  pl.BlockSpec((seq, LANES), lambda b, p: (b, 2 * n_pairs + p)),
                  pl.BlockSpec((2, 2 * NA_ROWS - 2, GRID_W, LANES), lambda b, p: (p, 0, 0, 0))],
        out_specs=[pl.BlockSpec((seq, LANES), lambda b, p: (b, p)), pl.BlockSpec((seq, LANES), lambda b, p: (b, p))],
        scratch_shapes=[pltpu.VMEM((2, seq, LANES), BF16), pltpu.VMEM((2, seq, LANES), BF16)],
        compiler_params=_cparams(("arbitrary", "arbitrary")), args=[proj, proj, proj, tiles])


def _sw_prepare(kv_ref, g, dst_lo, dst_hi, seq):
    lane = lax.broadcasted_iota(jnp.int32, kv_ref.shape, 1)
    mine = (lane // HEAD_DIM) == g
    kg = jnp.where(mine, kv_ref[...].astype(F32), 0.0)
    kr = pltpu.roll(kg, HEAD_DIM, 1)
    first = g == 0
    zero = jnp.zeros((SW_BLOCK, LANES), BF16)
    for dst, val in ((dst_lo, jnp.where(first, kg, kr)), (dst_hi, jnp.where(first, kr, kg))):
        dst[0:SW_BLOCK, :] = zero
        dst[SW_BLOCK:SW_BLOCK + seq, :] = val.astype(BF16)
        dst[SW_BLOCK + seq:, :] = zero


def _sw_mask(n, seq):
    qi = lax.broadcasted_iota(jnp.int32, (SW_BLOCK, 3 * SW_BLOCK), 0)
    kj = lax.broadcasted_iota(jnp.int32, (SW_BLOCK, 3 * SW_BLOCK), 1)
    kpos = n * SW_BLOCK - SW_BLOCK + kj
    return (jnp.abs(qi + SW_BLOCK - kj) <= SW_BLOCK) & (kpos >= 0) & (kpos < seq)


def _sw_probs(s2, ok, sinks):
    band = s2.shape[1] // 2
    halves, lse = [], []
    for i in range(2):
        s = jnp.where(ok, s2[:, i * band:(i + 1) * band], NEG)
        m = jnp.maximum(jnp.max(s, axis=-1, keepdims=True), sinks[i])
        p = jnp.exp(s - m)
        den = jnp.sum(p, axis=-1, keepdims=True) + jnp.exp(sinks[i] - m)
        halves.append(p / den)
        lse.append(m + jnp.log(den))
    return jnp.concatenate(halves, axis=1), _pair_lse_block(lse)


def _sw_probs_from_lse(s2, ok, sinks, lse_block):
    band = s2.shape[1] // 2
    halves, sink_p = [], []
    for i in range(2):
        lse = lse_block[:, i * HEAD_DIM:i * HEAD_DIM + 1]
        halves.append(jnp.exp(jnp.where(ok, s2[:, i * band:(i + 1) * band], NEG) - lse))
        sink_p.append(jnp.exp(sinks[i] - lse))
    return jnp.concatenate(halves, axis=1), sink_p


def _sw_forward(proj, sink, batch, seq, rider=None):
    t = proj.shape[0]
    n_pairs = SW_WIDTH // LANES
    q_blk = 3 * NA_WIDTH // LANES
    k_blk = q_blk + n_pairs
    n_blocks = seq // SW_BLOCK
    pad = seq + 2 * SW_BLOCK

    def body(sink_ref, q_ref, k_ref, v_ref, o_ref, lse_ref, k_lo, k_hi, v_lo, v_hi):
        hp = pl.program_id(1)
        g = hp // 2
        _sw_prepare(k_ref, g, k_lo, k_hi, seq)
        _sw_prepare(v_ref, g, v_lo, v_hi, seq)

        sinks = (sink_ref[2 * hp], sink_ref[2 * hp + 1])

        def scores(n):
            rows = pl.ds(pl.multiple_of(n * SW_BLOCK, SW_BLOCK), SW_BLOCK)
            wrows = pl.ds(pl.multiple_of(n * SW_BLOCK, SW_BLOCK), 3 * SW_BLOCK)
            k2 = jnp.concatenate([k_lo[wrows, :], k_hi[wrows, :]], axis=0)
            return n, rows, wrows, _mm_nt(q_ref[rows, :], k2) * QK_SCALE

        def finish(n, rows, wrows, s2):
            p, lse = _sw_probs(s2, _sw_mask(n, seq), sinks)
            lse_ref[rows, :] = lse
            v2 = jnp.concatenate([v_lo[wrows, :], v_hi[wrows, :]], axis=0)
            o_ref[rows, :] = _mm(p.astype(BF16), v2)

        def block_group(i, carry):
            for state in [scores(SW_GROUP_BLOCKS * i + j) for j in range(SW_GROUP_BLOCKS)]:
                finish(*state)
            return carry

        lax.fori_loop(0, n_blocks // SW_GROUP_BLOCKS, block_group, 0)

    return _hosted(
        body, rider, name="sw_forward", grid=(batch, n_pairs),
        out_shape=[jax.ShapeDtypeStruct((t, SW_WIDTH), F32), jax.ShapeDtypeStruct((t, SW_WIDTH), F32)],
        in_specs=[pl.BlockSpec(memory_space=pltpu.SMEM),
                  pl.BlockSpec((seq, LANES), lambda b, p: (b, q_blk + p)),
                  pl.BlockSpec((seq, LANES), lambda b, p: (b, k_blk)),
                  pl.BlockSpec((seq, LANES), lambda b, p: (b, k_blk + 1))],
        out_specs=[pl.BlockSpec((seq, LANES), lambda b, p: (b, p)), pl.BlockSpec((seq, LANES), lambda b, p: (b, p))],
        scratch_shapes=[pltpu.VMEM((pad, LANES), BF16)] * 4,
        compiler_params=_cparams(("arbitrary", "arbitrary")), args=[sink, proj, proj, proj])


def _out_proj(oa, ob, g_na, g_sw, w_out, x, mod3, g_ffn, seq):
    t, d = x.shape
    tm = TOKEN_TILE
    per_seq = seq // tm

    def body(oa_ref, ob_ref, gna_ref, gsw_ref, w_ref, x_ref, mod_ref, gf_ref, oab_ref, mix_ref, x1_ref, h2_ref):
        _, na = _rms_stats(oa_ref[...])
        _, nb = _rms_stats(ob_ref[...])
        oab = jnp.concatenate([na * gna_ref[...], nb * gsw_ref[...]], axis=1).astype(BF16)
        oab_ref[...] = oab
        mix = _mm(oab, w_ref[...])
        mix_ref[...] = mix
        gate_a = mod_ref[0, :, 2 * d:3 * d]
        shift_f, scale_f = mod_ref[0, :, 3 * d:4 * d], mod_ref[0, :, 4 * d:5 * d]
        x1 = x_ref[...] + gate_a * mix
        x1_ref[...] = x1
        _, xn = _rms_stats(x1)
        h2_ref[...] = ((xn * gf_ref[...]) * (1.0 + scale_f) + shift_f).astype(BF16)

    tile = lambda w: pl.BlockSpec((tm, w), lambda i: (i, 0))
    vec = lambda w: pl.BlockSpec((1, w), lambda i: (0, 0))
    return pl.pallas_call(
        body, name="out_proj", grid=(t // tm,),
        out_shape=(jax.ShapeDtypeStruct((t, d), BF16), jax.ShapeDtypeStruct((t, d), F32),
                   jax.ShapeDtypeStruct((t, d), F32), jax.ShapeDtypeStruct((t, d), BF16)),
        in_specs=[tile(NA_WIDTH), tile(SW_WIDTH), vec(NA_WIDTH), vec(SW_WIDTH),
                  pl.BlockSpec((d, d), lambda i: (0, 0)), tile(d),
                  pl.BlockSpec((1, 1, 6 * d), lambda i: (i // per_seq, 0, 0)), vec(d)],
        out_specs=(tile(d), tile(d), tile(d), tile(d)),
        compiler_params=_cparams(("arbitrary",), VMEM_BIG),
    )(oa, ob, g_na, g_sw, w_out, x, mod3, g_ffn)


def _up_proj(h2, w_up_halves, rider=None):
    t, d = h2.shape
    tm = TOKEN_TILE
    w_a, w_b = w_up_halves
    half, wcol = w_a.shape[1], w_a.shape[2]

    def body(h_ref, wa_ref, wb_ref, u_ref):
        u_ref[0] = (_mm(h_ref[:, :half], wa_ref[0]) + _mm(h_ref[:, half:], wb_ref[0])).astype(BF16)

    w_spec = pl.BlockSpec((1, half, wcol), lambda j, i: (j, 0, 0))
    return _hosted(
        body, rider, name="up_proj", grid=(N_SHARD, t // tm),
        out_shape=[jax.ShapeDtypeStruct((2, t, D_FF), BF16)],
        in_specs=[pl.BlockSpec((tm, d), lambda j, i: (i, 0)), w_spec, w_spec],
        out_specs=[pl.BlockSpec((1, tm, wcol), lambda j, i: (j // 2, i, j % 2))],
        scratch_shapes=[], compiler_params=_cparams(("arbitrary", "arbitrary"), VMEM_BIG), args=[h2, w_a, w_b])


def _taps_chunk(load, s, rows, seq):
    halo = 2 * SUBLANES
    cur = load(s, rows)
    above = load(pl.multiple_of(jnp.maximum(s - halo, 0), halo), halo)
    below = load(pl.multiple_of(jnp.minimum(s + rows, seq - halo), halo), halo)
    up = jnp.where(s > 0, above[halo - 1:halo, :], 0.0)
    dn = jnp.where(s + rows < seq, below[0:1, :], 0.0)
    row = lax.broadcasted_iota(jnp.int32, cur.shape, 0)
    prev = jnp.where(row == 0, up, pltpu.roll(cur, 1, 0))
    nxt = jnp.where(row == rows - 1, dn, pltpu.roll(cur, rows - 1, 0))
    return cur, prev, nxt


def _conv_gate(u, conv_w, conv_b, batch, seq):
    t = u.shape[1]
    cw = FF_TILE
    rows = CONV_CHUNK

    def body(u_ref, w_ref, b_ref, a_ref):
        def chunk(i, carry):
            s = pl.multiple_of(i * rows, rows)
            gt, prev, nxt = _taps_chunk(lambda at, n: u_ref[1, pl.ds(at, n), :].astype(F32), s, rows, seq)
            gc = prev * w_ref[0:1, :] + gt * w_ref[1:2, :] + nxt * w_ref[2:3, :] + b_ref[...]
            a_ref[pl.ds(s, rows), :] = ((gc * _sigmoid(gc)) * u_ref[0, pl.ds(s, rows), :].astype(F32)).astype(BF16)
            return carry

        lax.fori_loop(0, seq // rows, chunk, 0)

    return pl.pallas_call(
        body, name="conv_gate", grid=(batch, D_FF // cw),
        out_shape=jax.ShapeDtypeStruct((t, D_FF), BF16),
        in_specs=[pl.BlockSpec((2, seq, cw), lambda b, j: (0, b, j)),
                  pl.BlockSpec((3, cw), lambda b, j: (0, j)), pl.BlockSpec((1, cw), lambda b, j: (0, j))],
        out_specs=pl.BlockSpec((seq, cw), lambda b, j: (b, j)),
        compiler_params=_cparams(("arbitrary", "arbitrary"), VMEM_BIG),
    )(u, conv_w, conv_b)


def _down_and_loss(a, w_down, x1, mod3, g_final, target, seq):
    t, d = x1.shape
    tm = TOKEN_TILE
    per_seq = seq // tm
    batch = t // seq

    def body(a_ref, w_ref, x1_ref, mod_ref, g_ref, tgt_ref, dx2_ref, dffn_ref, loss_ref, dgate_ref, dg_ref):
        i = pl.program_id(0)
        f = _mm(a_ref[...], w_ref[...])
        gate_f = mod_ref[0, :, 5 * d:6 * d]
        x2 = x1_ref[...] + gate_f * f
        r, xn = _rms_stats(x2)
        err = xn * g_ref[...] - tgt_ref[...]
        part = 0.5 * jnp.sum(jnp.mean(err * err, axis=-1, keepdims=True))
        dy = err / d
        dx2 = _rms_bwd(dy * g_ref[...], xn, r)
        dx2_ref[...] = dx2
        dffn_ref[...] = (dx2 * gate_f).astype(BF16)

        @pl.when(i == 0)
        def _():
            loss_ref[...] = jnp.zeros_like(loss_ref)
            dg_ref[...] = jnp.zeros_like(dg_ref)

        @pl.when(i % per_seq == 0)
        def _():
            dgate_ref[...] = jnp.zeros_like(dgate_ref)

        loss_ref[...] += part
        dg_ref[...] += jnp.sum(dy * xn, axis=0, keepdims=True)
        dgate_ref[0] += jnp.sum(dx2 * f, axis=0, keepdims=True)

    tile = lambda w: pl.BlockSpec((tm, w), lambda i: (i, 0))
    return pl.pallas_call(
        body, name="down_loss", grid=(t // tm,),
        out_shape=(jax.ShapeDtypeStruct((t, d), F32), jax.ShapeDtypeStruct((t, d), BF16),
                   jax.ShapeDtypeStruct((SUBLANES, LANES), F32), jax.ShapeDtypeStruct((batch, 1, d), F32),
                   jax.ShapeDtypeStruct((1, d), F32)),
        in_specs=[tile(D_FF), _resident((D_FF, d)), tile(d),
                  pl.BlockSpec((1, 1, 6 * d), lambda i: (i // per_seq, 0, 0)),
                  pl.BlockSpec((1, d), lambda i: (0, 0)), tile(d)],
        out_specs=(tile(d), tile(d), pl.BlockSpec((SUBLANES, LANES), lambda i: (0, 0)),
                   pl.BlockSpec((1, 1, d), lambda i: (i // per_seq, 0, 0)), pl.BlockSpec((1, d), lambda i: (0, 0))),
        compiler_params=_cparams(("arbitrary",), VMEM_BIG),
    )(a, w_down, x1, mod3, g_final, target)


def _down_weight_grad(a, dffn):
    t, dff = a.shape
    d = dffn.shape[1]
    tk = TOKEN_TILE
    n_k = t // tk

    def body(a_ref, df_ref, g_ref, gb_ref):
        k = pl.program_id(0)

        @pl.when(k == 0)
        def _():
            g_ref[...] = jnp.zeros_like(g_ref)

        g_ref[...] += _mm_tn(a_ref[...], df_ref[...])

        @pl.when(k == n_k - 1)
        def _():
            gb_ref[...] = g_ref[...].astype(BF16)

    whole = pl.BlockSpec((dff, d), lambda k: (0, 0))
    return pl.pallas_call(
        body, name="down_weight_grad", grid=(n_k,),
        out_shape=(jax.ShapeDtypeStruct((dff, d), F32), jax.ShapeDtypeStruct((dff, d), BF16)),
        in_specs=[pl.BlockSpec((tk, dff), lambda k: (k, 0)), pl.BlockSpec((tk, d), lambda k: (k, 0))],
        out_specs=(whole, whole),
        compiler_params=_cparams(("arbitrary",), VMEM_BIG),
    )(a, dffn)


def _ffn_backward(dffn, w_down, u, conv_w, conv_b, batch, seq, rider=None):
    t, d = dffn.shape
    cw = FF_TILE
    rows = CONV_CHUNK

    def body(df_ref, wd_ref, u_ref, w_ref, b_ref, du_ref, gcw_ref, gcb_ref, da_scr, dgc_scr):
        b = pl.program_id(1)
        da_scr[...] = _mm_nt(df_ref[...], wd_ref[...])

        @pl.when(b == 0)
        def _():
            gcw_ref[...] = jnp.zeros_like(gcw_ref)
            gcb_ref[...] = jnp.zeros_like(gcb_ref)

        def fold(v):
            return jnp.sum(v.reshape(rows // SUBLANES, SUBLANES, cw), axis=0)

        def chunk(i, carry):
            s = pl.multiple_of(i * rows, rows)
            here = pl.ds(s, rows)
            gt, prev, nxt = _taps_chunk(lambda at, n: u_ref[1, pl.ds(at, n), :].astype(F32), s, rows, seq)
            val, da = u_ref[0, here, :].astype(F32), da_scr[here, :]
            gc = prev * w_ref[0:1, :] + gt * w_ref[1:2, :] + nxt * w_ref[2:3, :] + b_ref[...]
            sg = _sigmoid(gc)
            sl = gc * sg
            du_ref[0, here, :] = (da * sl).astype(BF16)
            dgc = (da * val) * (sg * (1.0 + gc * (1.0 - sg)))
            dgc_scr[here, :] = dgc
            cb, c0, c1, c2 = carry
            return cb + fold(dgc), c0 + fold(dgc * prev), c1 + fold(dgc * gt), c2 + fold(dgc * nxt)

        zero = jnp.zeros((SUBLANES, cw), F32)
        cb, c0, c1, c2 = lax.fori_loop(0, seq // rows, chunk, (zero, zero, zero, zero))
        gcb_ref[...] += jnp.sum(cb, axis=0, keepdims=True)
        gcw_ref[0:1, :] += jnp.sum(c0, axis=0, keepdims=True)
        gcw_ref[1:2, :] += jnp.sum(c1, axis=0, keepdims=True)
        gcw_ref[2:3, :] += jnp.sum(c2, axis=0, keepdims=True)

        def chunk2(i, carry):
            s = pl.multiple_of(i * rows, rows)
            dgc, dprev, dnxt = _taps_chunk(lambda at, n: dgc_scr[pl.ds(at, n), :], s, rows, seq)
            du_ref[1, pl.ds(s, rows), :] = (dnxt * w_ref[0:1, :] + dgc * w_ref[1:2, :]
                                            + dprev * w_ref[2:3, :]).astype(BF16)
            return carry

        lax.fori_loop(0, seq // rows, chunk2, 0)

    return _hosted(
        body, rider, name="ffn_backward", grid=(D_FF // cw, batch),
        out_shape=[jax.ShapeDtypeStruct((2, t, D_FF), BF16),
                   jax.ShapeDtypeStruct((3, D_FF), F32), jax.ShapeDtypeStruct((1, D_FF), F32)],
        in_specs=[pl.BlockSpec((seq, d), lambda j, b: (b, 0)), pl.BlockSpec((cw, d), lambda j, b: (j, 0)),
                  pl.BlockSpec((2, seq, cw), lambda j, b: (0, b, j)),
                  pl.BlockSpec((3, cw), lambda j, b: (0, j)), pl.BlockSpec((1, cw), lambda j, b: (0, j))],
        out_specs=[pl.BlockSpec((2, seq, cw), lambda j, b: (0, b, j)),
                   pl.BlockSpec((3, cw), lambda j, b: (0, j)), pl.BlockSpec((1, cw), lambda j, b: (0, j))],
        scratch_shapes=[pltpu.VMEM((seq, cw), F32), pltpu.VMEM((seq, cw), F32)],
        compiler_params=_cparams(("arbitrary", "arbitrary"), VMEM_BIG), args=[dffn, w_down, u, conv_w, conv_b])


def _up_backward(du, w_up, x1, mod3, g_ffn, dx2, mix, seq, rider=None):
    _, t, _ = du.shape
    d = x1.shape[1]
    tm = TOKEN_TILE
    per_seq = seq // tm
    batch = t // seq
    w_a, w_b = w_up
    half, wcol = w_a.shape[1], w_a.shape[2]

    def body(du_ref, wa_ref, wb_ref, x1_ref, mod_ref, g_ref, dx2_ref, mix_ref,
             dx1_ref, dmix_ref, dsh_ref, dsc_ref, dga_ref, dg_ref):
        i = pl.program_id(0)
        parts = []
        for w_ref in (wa_ref, wb_ref):
            acc = jnp.zeros((tm, half), F32)
            for j in range(N_SHARD):
                acc = acc + _mm_nt(du_ref[j // 2, :, (j % 2) * wcol:(j % 2 + 1) * wcol], w_ref[j])
            parts.append(acc)
        dh = jnp.concatenate(parts, axis=1)
        gate_a = mod_ref[0, :, 2 * d:3 * d]
        scale_f = mod_ref[0, :, 4 * d:5 * d]
        r, xn = _rms_stats(x1_ref[...])
        xg = xn * g_ref[...]
        dxg = dh * (1.0 + scale_f)
        dx1 = dx2_ref[...] + _rms_bwd(dxg * g_ref[...], xn, r)
        dx1_ref[...] = dx1
        dmix_ref[...] = (dx1 * gate_a).astype(BF16)

        @pl.when(i == 0)
        def _():
            dg_ref[...] = jnp.zeros_like(dg_ref)

        @pl.when(i % per_seq == 0)
        def _():
            dsh_ref[...] = jnp.zeros_like(dsh_ref)
            dsc_ref[...] = jnp.zeros_like(dsc_ref)
            dga_ref[...] = jnp.zeros_like(dga_ref)

        dg_ref[...] += jnp.sum(dxg * xn, axis=0, keepdims=True)
        dsh_ref[0] += jnp.sum(dh, axis=0, keepdims=True)
        dsc_ref[0] += jnp.sum(dh * xg, axis=0, keepdims=True)
        dga_ref[0] += jnp.sum(dx1 * mix_ref[...], axis=0, keepdims=True)

    tile = lambda w: pl.BlockSpec((tm, w), lambda i: (i, 0))
    per_b = pl.BlockSpec((1, 1, d), lambda i: (i // per_seq, 0, 0))
    small = jax.ShapeDtypeStruct((batch, 1, d), F32)
    return _hosted(
        body, rider, name="up_backward", grid=(t // tm,),
        out_shape=[jax.ShapeDtypeStruct((t, d), F32), jax.ShapeDtypeStruct((t, d), BF16), small, small, small,
                   jax.ShapeDtypeStruct((1, d), F32)],
        in_specs=[pl.BlockSpec((2, tm, D_FF), lambda i: (0, i, 0)),
                  _resident((N_SHARD, half, wcol)), _resident((N_SHARD, half, wcol)), tile(d),
                  pl.BlockSpec((1, 1, 6 * d), lambda i: (i // per_seq, 0, 0)),
                  pl.BlockSpec((1, d), lambda i: (0, 0)), tile(d), tile(d)],
        out_specs=[tile(d), tile(d), per_b, per_b, per_b, pl.BlockSpec((1, d), lambda i: (0, 0))],
        scratch_shapes=[], compiler_params=_cparams(("arbitrary",), VMEM_BIG),
        args=[du, w_a, w_b, x1, mod3, g_ffn, dx2, mix])


def _up_weight_grad(h2, du, rider=None):
    t, d = h2.shape
    tk = TOKEN_TILE
    wcol = D_FF // 2
    half = d // 2
    n_k = t // tk

    def body(h_ref, du_ref, ga_ref, gb_ref, ga16_ref, gb16_ref):
        k = pl.program_id(1)

        @pl.when(k == 0)
        def _():
            ga_ref[...] = jnp.zeros_like(ga_ref)
            gb_ref[...] = jnp.zeros_like(gb_ref)

        du = du_ref[0]
        ga_ref[0] += _mm_tn(h_ref[:, :half], du)
        gb_ref[0] += _mm_tn(h_ref[:, half:], du)

        @pl.when(k == n_k - 1)
        def _():
            ga16_ref[...] = ga_ref[...].astype(BF16)
            gb16_ref[...] = gb_ref[...].astype(BF16)

    g_spec = pl.BlockSpec((1, half, wcol), lambda j, k: (j, 0, 0))
    f32_out = jax.ShapeDtypeStruct((N_SHARD, half, wcol), F32)
    b16_out = jax.ShapeDtypeStruct((N_SHARD, half, wcol), BF16)
    return _hosted(
        body, rider, name="up_weight_grad", grid=(N_SHARD, n_k),
        out_shape=[f32_out, f32_out, b16_out, b16_out],
        in_specs=[pl.BlockSpec((tk, d), lambda j, k: (k, 0)),
                  pl.BlockSpec((1, tk, wcol), lambda j, k: (j // 2, k, j % 2))],
        out_specs=[g_spec, g_spec, g_spec, g_spec], scratch_shapes=[],
        compiler_params=_cparams(("arbitrary", "arbitrary"), VMEM_BIG), args=[h2, du])


def _out_backward(dmix, w_out, oab, oa, ob, g_na, g_sw):
    t, d = dmix.shape
    tm = TOKEN_TILE
    hw = NA_WIDTH

    def body(dm_ref, w_ref, oab_ref, oa_ref, ob_ref, gna_ref, gsw_ref,
             doa_ref, dob_ref, gw_ref, gwb_ref, dgna_ref, dgsw_ref):
        @pl.when(pl.program_id(0) == 0)
        def _():
            gw_ref[...] = jnp.zeros_like(gw_ref)
            dgna_ref[...] = jnp.zeros_like(dgna_ref)
            dgsw_ref[...] = jnp.zeros_like(dgsw_ref)

        dm = dm_ref[...]
        gw_ref[...] += _mm_tn(oab_ref[...], dm)

        @pl.when(pl.program_id(0) == t // tm - 1)
        def _():
            gwb_ref[...] = gw_ref[...].astype(BF16)

        do = _mm_nt(dm, w_ref[...])
        for raw_ref, g_ref, dst_ref, dg_ref, lo in ((oa_ref, gna_ref, doa_ref, dgna_ref, 0),
                                                     (ob_ref, gsw_ref, dob_ref, dgsw_ref, hw)):
            r, xn = _rms_stats(raw_ref[...])
            dpart = do[:, lo:lo + hw]
            dg_ref[...] += jnp.sum(dpart * xn, axis=0, keepdims=True)
            dst_ref[...] = _rms_bwd(dpart * g_ref[...], xn, r).astype(BF16)

    tile = lambda w: pl.BlockSpec((tm, w), lambda i: (i, 0))
    vec = lambda w: pl.BlockSpec((1, w), lambda i: (0, 0))
    return pl.pallas_call(
        body, name="out_backward", grid=(t // tm,),
        out_shape=(jax.ShapeDtypeStruct((t, hw), BF16), jax.ShapeDtypeStruct((t, hw), BF16),
                   jax.ShapeDtypeStruct((d, d), F32), jax.ShapeDtypeStruct((d, d), BF16),
                   jax.ShapeDtypeStruct((1, hw), F32), jax.ShapeDtypeStruct((1, hw), F32)),
        in_specs=[tile(d), pl.BlockSpec((d, d), lambda i: (0, 0)), tile(d), tile(hw), tile(hw), vec(hw), vec(hw)],
        out_specs=(tile(hw), tile(hw), pl.BlockSpec((d, d), lambda i: (0, 0)), pl.BlockSpec((d, d), lambda i: (0, 0)),
                   vec(hw), vec(hw)),
        compiler_params=_cparams(("arbitrary",), VMEM_BIG),
    )(dmix, w_out, oab, oa, ob, g_na, g_sw)


def _na_backward(proj, d_o, lse, tiles, batch, seq, rider=None):
    t = proj.shape[0]
    n_rows = seq // GRID_W
    n_pairs = NA_WIDTH // LANES
    win = NA_ROWS * GRID_W
    n_tiles = 2 * NA_ROWS - 2

    def body(q_ref, k_ref, v_ref, do_ref, lse_ref, tp_ref, dq_ref, dk_ref, dv_ref, dtp_ref, km, vm, dk_acc, dv_acc):
        @pl.when(pl.program_id(1) == 0)
        def _():
            dtp_ref[...] = jnp.zeros_like(dtp_ref)

        _na_prepare(k_ref, v_ref, km, vm)
        dk_acc[...] = jnp.zeros_like(dk_acc)
        dv_acc[...] = jnp.zeros_like(dv_acc)
        low = lax.broadcasted_iota(jnp.int32, (win, LANES), 1) < HEAD_DIM

        def scores(r):
            rs, off = _na_window(r, n_rows)
            rows = pl.ds(pl.multiple_of(r * GRID_W, GRID_W), GRID_W)
            wrows = pl.ds(pl.multiple_of(rs * GRID_W, GRID_W), win)
            q, do = q_ref[rows, :], do_ref[rows, :]
            k2 = _na_pair_window(km, wrows)
            s = _na_scores(q, k2, tp_ref, off)
            dp = _mm_nt(do, _na_pair_window(vm, wrows))
            return rows, wrows, off, q, do, k2, s, dp

        def finish(rows, wrows, off, q, do, k2, s, dp):
            p = _pair_probs_from_lse(s, lse_ref[rows, :])
            parts = []
            for h in range(2):
                ph, dph = p[:, h * win:(h + 1) * win], dp[:, h * win:(h + 1) * win]
                dsh = ph * (dph - jnp.sum(ph * dph, axis=-1, keepdims=True))
                for w in range(NA_ROWS // 2):
                    dtp_ref[h, 2 * w - off + (NA_ROWS - 1)] += dsh[:, w * LANES:(w + 1) * LANES]
                parts.append(dsh)
            dsb = (jnp.concatenate(parts, axis=1) * QK_SCALE).astype(BF16)
            dq_ref[rows, :] = _mm(dsb, k2).astype(BF16)
            dk2 = _mm_tn(dsb, q)
            dv2 = _mm_tn(p.astype(BF16), do)
            dk_acc[wrows, :] += jnp.where(low, dk2[:win], dk2[win:])
            dv_acc[wrows, :] += jnp.where(low, dv2[:win], dv2[win:])

        def row_group(i, carry):
            for state in [scores(NA_GROUP * i + j) for j in range(NA_GROUP)]:
                finish(*state)
            return carry

        lax.fori_loop(0, n_rows // NA_GROUP, row_group, 0)
        dk_ref[...] = dk_acc[...].astype(BF16)
        dv_ref[...] = dv_acc[...].astype(BF16)

    blk = lambda off: pl.BlockSpec((seq, LANES), lambda p, b: (b, off + p))
    out = jax.ShapeDtypeStruct((t, NA_WIDTH), BF16)
    return _hosted(
        body, rider, name="na_backward", grid=(n_pairs, batch),
        out_shape=[out, out, out, jax.ShapeDtypeStruct(tiles.shape, F32)],
        in_specs=[blk(0), blk(n_pairs), blk(2 * n_pairs), blk(0), blk(0),
                  pl.BlockSpec((2, n_tiles, GRID_W, LANES), lambda p, b: (p, 0, 0, 0))],
        out_specs=[blk(0), blk(0), blk(0), pl.BlockSpec((2, n_tiles, GRID_W, LANES), lambda p, b: (p, 0, 0, 0))],
        scratch_shapes=[pltpu.VMEM((2, seq, LANES), BF16), pltpu.VMEM((2, seq, LANES), BF16),
                        pltpu.VMEM((seq, LANES), F32), pltpu.VMEM((seq, LANES), F32)],
        compiler_params=_cparams(("arbitrary", "arbitrary")), args=[proj, proj, proj, d_o, lse, tiles])


def _na_bias_grad(dtiles, expand):
    n = dtiles.shape[0]

    def body(t_ref, e_ref, o_ref):
        flat = jnp.concatenate([t_ref[:, qq, :] for qq in range(GRID_W)], axis=1)
        o_ref[...] = lax.dot_general(flat, e_ref[...], (((1,), (1,)), ((), ())),
                                     precision=lax.Precision.HIGHEST, preferred_element_type=F32)

    return pl.pallas_call(
        body, name="na_bias_grad",
        out_shape=jax.ShapeDtypeStruct((n, expand.shape[0]), F32),
        compiler_params=_cparams(vmem=VMEM_BIG),
    )(dtiles, expand)


def _sw_backward(proj, d_o, lse, sink, batch, seq, rider=None):
    t = proj.shape[0]
    n_pairs = SW_WIDTH // LANES
    q_blk = 3 * NA_WIDTH // LANES
    k_blk = q_blk + n_pairs
    n_blocks = seq // SW_BLOCK
    pad = seq + 2 * SW_BLOCK

    def body(sink_ref, q_ref, k_ref, v_ref, do_ref, lse_ref, dq_ref, dk_ref, dv_ref, dsk_ref,
             k_lo, k_hi, v_lo, v_hi, dk_loc, dv_loc, dk_tot, dv_tot):
        hp = pl.program_id(1)
        g = hp // 2
        _sw_prepare(k_ref, g, k_lo, k_hi, seq)
        _sw_prepare(v_ref, g, v_lo, v_hi, seq)
        dk_loc[...] = jnp.zeros_like(dk_loc)
        dv_loc[...] = jnp.zeros_like(dv_loc)

        @pl.when(hp == 0)
        def _():
            dk_tot[...] = jnp.zeros_like(dk_tot)
            dv_tot[...] = jnp.zeros_like(dv_tot)

        band = 3 * SW_BLOCK
        low = lax.broadcasted_iota(jnp.int32, (band, LANES), 1) < HEAD_DIM

        sinks = (sink_ref[2 * hp], sink_ref[2 * hp + 1])

        def scores(n):
            rows = pl.ds(pl.multiple_of(n * SW_BLOCK, SW_BLOCK), SW_BLOCK)
            wrows = pl.ds(pl.multiple_of(n * SW_BLOCK, SW_BLOCK), band)
            qb, do = q_ref[rows, :], do_ref[rows, :]
            k2 = jnp.concatenate([k_lo[wrows, :], k_hi[wrows, :]], axis=0)
            v2 = jnp.concatenate([v_lo[wrows, :], v_hi[wrows, :]], axis=0)
            return n, rows, wrows, qb, do, k2, _mm_nt(qb, k2) * QK_SCALE, _mm_nt(do, v2)

        def finish(sink_acc, n, rows, wrows, qb, do, k2, s2, dp):
            p, ps = _sw_probs_from_lse(s2, _sw_mask(n, seq), sinks, lse_ref[rows, :])
            parts, new = [], []
            for i in range(2):
                ph, dph = p[:, i * band:(i + 1) * band], dp[:, i * band:(i + 1) * band]
                delta = jnp.sum(ph * dph, axis=-1, keepdims=True)
                parts.append(ph * (dph - delta))
                new.append(sink_acc[i] - ps[i] * delta)
            dsb = (jnp.concatenate(parts, axis=1) * QK_SCALE).astype(BF16)
            dq_ref[rows, :] = _mm(dsb, k2)
            dk2 = _mm_tn(dsb, qb)
            dv2 = _mm_tn(p.astype(BF16), do)
            dk_loc[wrows, :] += jnp.where(low, dk2[:band], dk2[band:])
            dv_loc[wrows, :] += jnp.where(low, dv2[:band], dv2[band:])
            return tuple(new)

        def block_group(i, carry):
            for state in [scores(SW_GROUP_BLOCKS * i + j) for j in range(SW_GROUP_BLOCKS)]:
                carry = finish(carry, *state)
            return carry

        zero = jnp.zeros((SW_BLOCK, 1), F32)
        s0, s1 = lax.fori_loop(0, n_blocks // SW_GROUP_BLOCKS, block_group, (zero, zero))
        row = lax.broadcasted_iota(jnp.int32, (SUBLANES, LANES), 0)
        dsk_ref[0, 0] = jnp.where(row == 0, jnp.sum(s0), jnp.where(row == 1, jnp.sum(s1), 0.0))

        lane_s = lax.broadcasted_iota(jnp.int32, (seq, LANES), 1)
        mine_g = (lane_s // HEAD_DIM) == g
        for loc, tot in ((dk_loc, dk_tot), (dv_loc, dv_tot)):
            part = loc[SW_BLOCK:SW_BLOCK + seq, :]
            tot[...] += jnp.where(mine_g, part + pltpu.roll(part, HEAD_DIM, 1), 0.0)

        @pl.when(hp == n_pairs - 1)
        def _():
            dk_ref[...] = dk_tot[...]
            dv_ref[...] = dv_tot[...].astype(BF16)

    return _hosted(
        body, rider, name="sw_backward", grid=(batch, n_pairs),
        out_shape=[jax.ShapeDtypeStruct((t, SW_WIDTH), F32), jax.ShapeDtypeStruct((t, LANES), F32),
                   jax.ShapeDtypeStruct((t, LANES), BF16), jax.ShapeDtypeStruct((batch, n_pairs, SUBLANES, LANES), F32)],
        in_specs=[pl.BlockSpec(memory_space=pltpu.SMEM),
                  pl.BlockSpec((seq, LANES), lambda b, p: (b, q_blk + p)),
                  pl.BlockSpec((seq, LANES), lambda b, p: (b, k_blk)),
                  pl.BlockSpec((seq, LANES), lambda b, p: (b, k_blk + 1)),
                  pl.BlockSpec((seq, LANES), lambda b, p: (b, p)), pl.BlockSpec((seq, LANES), lambda b, p: (b, p))],
        out_specs=[pl.BlockSpec((seq, LANES), lambda b, p: (b, p)), pl.BlockSpec((seq, LANES), lambda b, p: (b, 0)),
                   pl.BlockSpec((seq, LANES), lambda b, p: (b, 0)),
                   pl.BlockSpec((1, 1, SUBLANES, LANES), lambda b, p: (b, p, 0, 0))],
        scratch_shapes=[pltpu.VMEM((pad, LANES), BF16)] * 4 + [pltpu.VMEM((pad, LANES), F32)] * 2
        + [pltpu.VMEM((seq, LANES), F32)] * 2,
        compiler_params=_cparams(("arbitrary", "arbitrary")), args=[sink, proj, proj, proj, d_o, lse])


def _in_backward(dqkv_a, dq_b, dk_b, dv_b, w_in_t, h1, x, mod3, g_attn, dx1, cos_t, sin_t, seq):
    t, d = x.shape
    tm = TOKEN_TILE
    per_seq = seq // tm
    batch = t // seq
    dqa, dka, dva = dqkv_a
    n_q = SW_WIDTH // LANES

    def body(dqa_ref, dka_ref, dva_ref, dqb_ref, dkb_ref, dvb_ref, w_ref, h_ref, x_ref, mod_ref, g_ref, dx1_ref,
             cos_ref, sin_ref, dx_ref, gw_ref, gwb_ref, dsh_ref, dsc_ref, dg_ref):
        i = pl.program_id(0)

        @pl.when(i == 0)
        def _():
            gw_ref[...] = jnp.zeros_like(gw_ref)
            dg_ref[...] = jnp.zeros_like(dg_ref)

        @pl.when(i % per_seq == 0)
        def _():
            dsh_ref[...] = jnp.zeros_like(dsh_ref)
            dsc_ref[...] = jnp.zeros_like(dsc_ref)

        dr = jnp.concatenate([dqb_ref[...], dkb_ref[...]], axis=1)
        cos = jnp.concatenate([cos_ref[...]] * (n_q + 1), axis=1)
        sin = jnp.concatenate([sin_ref[...]] * (n_q + 1), axis=1)
        dr = dr * cos + _rope_rot(dr * sin)
        dproj = jnp.concatenate([dqa_ref[...], dka_ref[...], dva_ref[...], dr.astype(BF16), dvb_ref[...]], axis=1)
        gw_ref[...] += _mm_tn(dproj, h_ref[...])

        @pl.when(i == t // tm - 1)
        def _():
            gwb_ref[...] = gw_ref[...].astype(BF16)

        dh = _mm(dproj, w_ref[...])
        scale = mod_ref[0, :, d:2 * d]
        r, xn = _rms_stats(x_ref[...])
        xg = xn * g_ref[...]
        dxg = dh * (1.0 + scale)
        dx_ref[...] = dx1_ref[...] + _rms_bwd(dxg * g_ref[...], xn, r)
        dg_ref[...] += jnp.sum(dxg * xn, axis=0, keepdims=True)
        dsh_ref[0] += jnp.sum(dh, axis=0, keepdims=True)
        dsc_ref[0] += jnp.sum(dh * xg, axis=0, keepdims=True)

    tile = lambda w: pl.BlockSpec((tm, w), lambda i: (i, 0))
    per_b = pl.BlockSpec((1, 1, d), lambda i: (i // per_seq, 0, 0))
    small = jax.ShapeDtypeStruct((batch, 1, d), F32)
    rope = pl.BlockSpec((tm, LANES), lambda i: (i % per_seq, 0))
    return pl.pallas_call(
        body, name="in_backward", grid=(t // tm,),
        out_shape=(jax.ShapeDtypeStruct((t, d), F32), jax.ShapeDtypeStruct((IN_WIDTH, d), F32),
                   jax.ShapeDtypeStruct((IN_WIDTH, d), BF16), small, small, jax.ShapeDtypeStruct((1, d), F32)),
        in_specs=[tile(NA_WIDTH), tile(NA_WIDTH), tile(NA_WIDTH), tile(SW_WIDTH), tile(LANES), tile(LANES),
                  _resident((IN_WIDTH, d)), tile(d), tile(d),
                  pl.BlockSpec((1, 1, 6 * d), lambda i: (i // per_seq, 0, 0)),
                  pl.BlockSpec((1, d), lambda i: (0, 0)), tile(d), rope, rope],
        out_specs=(tile(d), _resident((IN_WIDTH, d)), _resident((IN_WIDTH, d)),
                   per_b, per_b, pl.BlockSpec((1, d), lambda i: (0, 0))),
        compiler_params=_cparams(("arbitrary",), VMEM_BIG),
    )(dqa, dka, dva, dq_b, dk_b, dv_b, w_in_t, h1, x, mod3, g_attn, dx1, cos_t, sin_t)


def _ada_weight_grad(sc_all, dmod_cols):
    d = sc_all.shape[1]
    ncol = dmod_cols.shape[1]

    def body(s_ref, m_ref, o_ref):
        o_ref[...] = _mm_tn(s_ref[...].astype(BF16), m_ref[...].astype(BF16))

    return pl.pallas_call(
        body, name="ada_weight_grad",
        out_shape=jax.ShapeDtypeStruct((d, ncol), F32),
        compiler_params=_cparams(vmem=VMEM_BIG),
    )(sc_all, dmod_cols)


def _row_tile(rows, cols):
    target = max(SUBLANES, (1 << 20) // (4 * cols))
    best = rows
    for cand in range(SUBLANES, rows + 1, SUBLANES):
        if rows % cand == 0 and cand <= target:
            best = cand
    return best if rows % SUBLANES == 0 else rows


def _sum_slots(parts, name):
    n = len(parts)
    _, rows, cols = parts[0][0].shape
    tr = _row_tile(rows, cols)
    per = rows // tr

    def body(*refs):
        o_ref = refs[-1]
        for q in range(n):
            @pl.when(pl.program_id(0) == q)
            def _(q=q):
                p_ref, own_ref = refs[2 * q], refs[2 * q + 1]
                o_ref[...] = ((own_ref[...] + p_ref[0].astype(F32)) + p_ref[1].astype(F32)) + p_ref[2].astype(F32)

    in_specs, args = [], []
    for q, (recv, own) in enumerate(parts):
        in_specs.append(pl.BlockSpec((N_SHARD - 1, tr, cols), lambda p, i, q=q: (0, jnp.where(p == q, i, 0), 0)))
        in_specs.append(pl.BlockSpec((tr, cols), lambda p, i, q=q: (jnp.where(p == q, i, 0), 0)))
        args += [recv, own]
    return pl.pallas_call(
        body, name=name, grid=(n, per),
        out_shape=jax.ShapeDtypeStruct((n * rows, cols), F32),
        in_specs=in_specs, out_specs=pl.BlockSpec((tr, cols), lambda p, i: (p * per + i, 0)),
        compiler_params=_cparams(("arbitrary", "arbitrary")),
    )(*args)


def _adamw_math(w, g, m, v):
    m2 = ADAM_B1 * m + (1.0 - ADAM_B1) * g
    v2 = ADAM_B2 * v + (1.0 - ADAM_B2) * (g * g)
    m_hat = m2 / (1.0 - ADAM_B1 ** ADAM_STEP)
    v_hat = v2 / (1.0 - ADAM_B2 ** ADAM_STEP)
    return -ADAM_LR * (m_hat / (jnp.sqrt(v_hat) + ADAM_EPS) + ADAM_WD * w), m2, v2


def _small_step(partials, states, dmod, b_ada_state, rider=None):
    n_upd = len(states)
    moving = list(partials) + [dmod]
    n_mov = len(moving)
    all_states = list(states) + [b_ada_state]

    def body(*refs):
        mov, refs = refs[:n_mov], refs[n_mov:]
        wmv, refs = refs[:3 * (n_upd + 1)], refs[3 * (n_upd + 1):]
        res, refs = refs[:4 * (n_upd + 1)], refs[4 * (n_upd + 1):]
        sums_out, refs = refs[:n_mov - n_upd - 1], refs[n_mov - n_upd - 1:]
        dmod_out, refs = refs[0], refs[1:]
        everyone, (ssem, rsem) = refs[:n_mov], refs[n_mov:]
        x, y, c = _my_pos()
        me = 4 * x + 2 * y + c
        cps = []
        for a in range(n_mov):
            everyone[a][me] = mov[a][...]
            for k in range(1, N_DEV):
                peer = (_flip(x, (k >> 2) & 1), _flip(y, (k >> 1) & 1), _flip(c, k & 1))
                cps.append(pltpu.make_async_remote_copy(
                    src_ref=everyone[a].at[me], dst_ref=everyone[a].at[me], send_sem=ssem.at[a, k - 1],
                    recv_sem=rsem.at[a, k - 1], device_id=peer, device_id_type=MESH))
        for cp in cps:
            cp.start()
        for cp in cps:
            cp.wait_recv()

        def total(a):
            acc = everyone[a][0]
            for dev in range(1, N_DEV):
                acc = acc + everyone[a][dev]
            return acc

        grads = [total(a) for a in range(n_upd)]
        grads.append(jnp.sum(total(n_mov - 1), axis=0, keepdims=True))
        for j, g in enumerate(grads):
            delta, m2, v2 = _adamw_math(wmv[3 * j][...], g, wmv[3 * j + 1][...], wmv[3 * j + 2][...])
            res[4 * j][...] = g
            res[4 * j + 1][...] = delta
            res[4 * j + 2][...] = m2
            res[4 * j + 3][...] = v2
        for j in range(n_mov - n_upd - 1):
            sums_out[j][...] = total(n_upd + j)
        dmod_out[...] = everyone[n_mov - 1][...]
        for cp in cps:
            cp.wait_send()

    vm = pl.BlockSpec(memory_space=pltpu.VMEM)
    sds = jax.ShapeDtypeStruct
    out_shape = []
    for w, _, _ in all_states:
        out_shape += [sds(w.shape, F32)] * 4
    out_shape += [sds(p.shape, F32) for p in partials[n_upd:]]
    out_shape.append(sds((N_DEV,) + dmod.shape, F32))
    args = moving + [a for st in all_states for a in st]
    outs, rides = _hosted(
        body, rider, name="small_step", grid=(), out_shape=out_shape,
        in_specs=[vm] * len(args), out_specs=[vm] * len(out_shape),
        scratch_shapes=[pltpu.VMEM((N_DEV,) + a.shape, F32) for a in moving]
        + [pltpu.SemaphoreType.DMA((n_mov, N_DEV - 1)), pltpu.SemaphoreType.DMA((n_mov, N_DEV - 1))],
        compiler_params=_cparams(vmem=VMEM_BIG), args=args)
    return outs, rides


def _adamw(w, grads, m, v, name):
    rows, cols = w.shape
    tr = _row_tile(rows, cols)
    ng = len(grads)

    def body(*refs):
        w_ref = refs[0]
        g_refs = refs[1:1 + ng]
        m_ref, v_ref = refs[1 + ng], refs[2 + ng]
        g_out, d_out, m_out, v_out = refs[3 + ng:]
        g = g_refs[0][...]
        for extra in g_refs[1:]:
            g = g + extra[...]
        g_out[...] = g
        d_out[...], m_out[...], v_out[...] = _adamw_math(w_ref[...], g, m_ref[...], v_ref[...])

    spec = pl.BlockSpec((tr, cols), lambda i: (i, 0))
    out = jax.ShapeDtypeStruct((rows, cols), F32)
    return pl.pallas_call(
        body, name=name, grid=(rows // tr,),
        out_shape=(out, out, out, out),
        in_specs=[spec] * (3 + ng), out_specs=(spec, spec, spec, spec),
        compiler_params=_cparams(("arbitrary",)),
    )(w, *grads, m, v)


def _pack_rows(arrays):
    tile = SUBLANES * LANES
    rows, offsets, at = [], [], 0
    for a in arrays:
        flat = a.reshape(-1).astype(F32)
        n = -(-flat.shape[0] // tile) * tile
        rows.append(jnp.pad(flat, (0, n - flat.shape[0])).reshape(-1, LANES))
        offsets.append(at)
        at += n // LANES
    return jnp.concatenate(rows, axis=0), offsets


def _unpack_rows(packed, offsets, shapes):
    out = []
    for off, shape in zip(offsets, shapes):
        n = 1
        for s in shape:
            n *= s
        nrow = -(-n // LANES)
        out.append(packed[off:off + nrow].reshape(-1)[:n].reshape(shape))
    return out


def _rope_tables(seq):
    half = HEAD_DIM // 2
    inv = np.float32(ROPE_THETA) ** (-np.arange(half, dtype=np.float32) / np.float32(half))
    ang = (np.arange(seq, dtype=np.float32)[:, None] * inv[None, :]).astype(np.float64)
    cos, sin = np.cos(ang).astype(np.float32), np.sin(ang).astype(np.float32)
    cos_t = np.concatenate([cos, cos, cos, cos], axis=1)
    sin_t = np.concatenate([-sin, sin, -sin, sin], axis=1)
    return jnp.asarray(cos_t), jnp.asarray(sin_t)


def kernel(x, c, w_ada, b_ada, g_attn, w_in, na_rpb, sw_sink, g_na_out, g_sw_out, w_out, g_ffn, w_up, conv_w, conv_b, w_down, g_final, loss_target, m_w_ada, m_b_ada, m_g_attn, m_w_in, m_na_rpb, m_sw_sink, m_g_na_out, m_g_sw_out, m_w_out, m_g_ffn, m_w_up, m_conv_w, m_conv_b, m_w_down, m_g_final, v_w_ada, v_b_ada, v_g_attn, v_w_in, v_na_rpb, v_sw_sink, v_g_na_out, v_g_sw_out, v_w_out, v_g_ffn, v_w_up, v_conv_w, v_conv_b, v_w_down, v_g_final):
    batch, seq, d = x.shape
    t = batch * seq
    assert d == D_MODEL and seq % (NA_ROWS * GRID_W) == 0 and seq % TOKEN_TILE == 0 and batch <= SUBLANES
    shard = 2 * lax.axis_index("x") + lax.axis_index("y")
    xt = x.reshape(t, d)
    tgt = loss_target.reshape(t, d)

    c8 = jnp.pad(c, ((0, SUBLANES - batch), (0, 0)))
    w_in_t_s = jnp.transpose(w_in[0]).astype(BF16)
    (mod8, sc_all), (w_in_g,) = _ada_forward(c8, w_ada[0], b_ada, _Rider("gather", [w_in_t_s]))
    mod3 = mod8[:batch].reshape(batch, 1, 6 * d)
    w_in_t = w_in_g.reshape(IN_WIDTH, d)

    cos_t, sin_t = _rope_tables(seq)
    (h1, proj), _ = _in_proj(xt, mod3, g_attn, w_in_t, cos_t, sin_t, seq)
    n_heads = NA_WIDTH // HEAD_DIM
    n_tiles, n_dc = 2 * NA_ROWS - 2, 2 * NA_COLS - 1
    expand, neg_mask = _na_bias_pattern()
    rpb = na_rpb[0]
    rows2 = jnp.concatenate([rpb[:, :-1, :], rpb[:, 1:, :]], axis=2).reshape(n_heads * n_tiles, 2 * n_dc)
    rows2 = jnp.pad(rows2, ((0, 0), (0, GRID_W - 2 * n_dc)))
    tiles = _na_bias_tiles(rows2, expand, neg_mask).reshape(n_heads, n_tiles, GRID_W, LANES)
    sink = sw_sink[0]
    w_up_b16 = w_up[0].astype(BF16)
    (oa, lse_a), (w_up_a, w_down_g) = _na_forward(proj, tiles, batch, seq,
                                                  _Rider("gather", [w_up_b16[:d // 2], w_down[0].astype(BF16)]))
    (ob, lse_b), (w_up_b, conv_w_g, w_out_g) = _sw_forward(
        proj, sink, batch, seq, _Rider("gather", [w_up_b16[d // 2:], conv_w[0], w_out[0].astype(BF16)]))
    w_up_f = (w_up_a, w_up_b)
    w_out_f = w_out_g.reshape(d, d)
    conv_w_f = jnp.transpose(conv_w_g, (1, 0, 2)).reshape(3, D_FF)
    oab, mix, x1, h2 = _out_proj(oa, ob, g_na_out, g_sw_out, w_out_f, xt, mod3, g_ffn, seq)
    (u,), _ = _up_proj(h2, w_up_f)
    w_down_f = w_down_g.reshape(D_FF, d)
    a = _conv_gate(u, conv_w_f, conv_b, batch, seq)
    dx2, dffn, loss_part, dgate_f, dg_final = _down_and_loss(a, w_down_f, x1, mod3, g_final.reshape(1, d), tgt, seq)

    gw_down, gw_down_b = _down_weight_grad(a, dffn)
    blocks = lambda g, rows: g.reshape(N_SHARD, rows // N_SHARD, d)
    (du, gconv_w, gconv_b), (recv_down, own_down) = _ffn_backward(
        dffn, w_down_f, u, conv_w_f, conv_b, batch, seq,
        _Rider("scatter", [blocks(gw_down_b, D_FF)], [blocks(gw_down, D_FF)]))
    (gw_up_top, gw_up_bot, gw_up_top_b, gw_up_bot_b), _ = _up_weight_grad(h2, du)
    (dx1, dmix, dshift_f, dscale_f, dgate_a, dg_ffn), _ = _up_backward(du, w_up_f, x1, mod3, g_ffn, dx2, mix, seq)
    doa, dob, gw_out, gw_out_b, dg_na, dg_sw = _out_backward(dmix, w_out_f, oab, oa, ob, g_na_out, g_sw_out)
    (dqa, dka, dva, dtiles), (recv_up_bot, own_up_bot) = _na_backward(
        proj, doa, lse_a, tiles, batch, seq, _Rider("scatter", [gw_up_bot_b], [gw_up_bot]))
    (dq_b, dk_b, dv_b, dsink_parts), (recv_out, recv_up_top, own_out, own_up_top) = _sw_backward(
        proj, dob, lse_b, sink, batch, seq,
        _Rider("scatter", [blocks(gw_out_b, d), gw_up_top_b], [blocks(gw_out, d), gw_up_top]))
    gx, gw_in_t, gw_in_b, dshift_a, dscale_a, dg_attn = _in_backward(
        (dqa, dka, dva), dq_b, dk_b, dv_b, w_in_t, h1, xt, mod3, g_attn, dx1, cos_t, sin_t, seq)

    red = _na_bias_grad(dtiles.reshape(n_heads * n_tiles, GRID_W, LANES), expand)[:, :2 * n_dc]
    red = red.reshape(n_heads, n_tiles, 2, n_dc)
    zero_row = jnp.zeros((n_heads, 1, n_dc), F32)
    g_rpb = (jnp.concatenate([red[:, :, 0, :], zero_row], axis=1)
             + jnp.concatenate([zero_row, red[:, :, 1, :]], axis=1))
    g_sink = jnp.sum(dsink_parts[:, :, :2, 0], axis=0).reshape(SW_WIDTH // HEAD_DIM)

    dmod = jnp.concatenate([dshift_a, dscale_a, dgate_a, dshift_f, dscale_f, dgate_f], axis=2).reshape(batch, 6 * d)
    rpb_shape = na_rpb.shape[1:]
    states = [(g_attn, m_g_attn, v_g_attn),
              (na_rpb.reshape(rpb_shape), m_na_rpb.reshape(rpb_shape), v_na_rpb.reshape(rpb_shape)),
              (sw_sink, m_sw_sink, v_sw_sink), (g_na_out, m_g_na_out, v_g_na_out), (g_sw_out, m_g_sw_out, v_g_sw_out),
              (g_ffn, m_g_ffn, v_g_ffn), (conv_b, m_conv_b, v_conv_b),
              (g_final.reshape(1, d), m_g_final.reshape(1, d), v_g_final.reshape(1, d))]
    partials = [dg_attn, g_rpb, g_sink.reshape(sw_sink.shape), dg_na, dg_sw, dg_ffn, gconv_b, dg_final,
                gconv_w, loss_part]
    mine = [None, _sum_slots([(recv_out, own_out)], "sum_w_out"),
            _sum_slots([(recv_up_top, own_up_top), (recv_up_bot, own_up_bot)], "sum_w_up"),
            _sum_slots([(recv_down, own_down)], "sum_w_down")]
    small, (recv_in, own_in, *theirs) = _small_step(
        partials, states, dmod, (b_ada, m_b_ada, v_b_ada),
        _Riders([_Rider("scatter", [blocks(gw_in_b, IN_WIDTH)], [blocks(gw_in_t, IN_WIDTH)]),
                 _Rider("swap", mine[1:])]))
    r_small = [small[4 * j:4 * j + 4] for j in range(len(states) + 1)]
    g_conv_w_full, loss_sum, dmod_all = small[4 * (len(states) + 1):]
    loss = loss_sum[0, 0]
    mine[0] = _sum_slots([(recv_in, own_in)], "sum_w_in")
    theirs = _ride_alone(_Rider("swap", mine[:1]), "swap_sibling") + theirs
    dmod_rows = jnp.pad(dmod_all, ((0, 0), (0, SUBLANES - batch), (0, 0))).reshape(N_DEV * SUBLANES, 6 * d)
    ncol = w_ada.shape[2]
    g_w_ada = _ada_weight_grad(sc_all, lax.dynamic_slice(dmod_rows, (0, shard * ncol), (N_DEV * SUBLANES, ncol)))
    cshard = conv_w.shape[2]
    g_conv_w = lax.dynamic_slice(g_conv_w_full, (0, shard * cshard), (3, cshard))

    def big(w, m, v, g_parts, name):
        shape = w.shape
        outs = _adamw(w[0], g_parts, m[0], v[0], name)
        return [o.reshape(shape) for o in outs]

    r_w_ada = big(w_ada, m_w_ada, v_w_ada, [g_w_ada], "adamw_w_ada")
    r_w_in = [jnp.transpose(o).reshape(w_in.shape) for o in
              _adamw(jnp.transpose(w_in[0]), [mine[0], theirs[0]], jnp.transpose(m_w_in[0]), jnp.transpose(v_w_in[0]),
                     "adamw_w_in")]
    r_w_out = big(w_out, m_w_out, v_w_out, [mine[1], theirs[1]], "adamw_w_out")
    r_w_up = big(w_up, m_w_up, v_w_up, [mine[2], theirs[2]], "adamw_w_up")
    r_w_down = big(w_down, m_w_down, v_w_down, [mine[3], theirs[3]], "adamw_w_down")

    r_conv_w = big(conv_w, m_conv_w, v_conv_w, [g_conv_w], "adamw_conv_w")

    def pick(k):
        ga_, rpb_, sk_, gna_, gsw_, gf_, cb_, gfin_, b_ = [r[k] for r in r_small]
        return [r_w_ada[k], b_, ga_, r_w_in[k], rpb_.reshape(na_rpb.shape), sk_, gna_, gsw_, r_w_out[k], gf_,
                r_w_up[k], r_conv_w[k], cb_, r_w_down[k], gfin_.reshape(d)]

    return (loss, gx.reshape(batch, seq, d), *pick(0), *pick(1), *pick(2), *pick(3))
```

```python
import jax
import jax.numpy as jnp
import numpy as np
from jax import lax
from jax.experimental import pallas as pl
from jax.experimental.pallas import tpu as pltpu

F32 = jnp.float32
BF16 = jnp.bfloat16
MESH = pl.DeviceIdType.MESH

D_MODEL = 1024
HEAD_DIM = 64
NA_WIDTH = 512
SW_WIDTH = 512
SW_KV_WIDTH = 128
IN_WIDTH = 2304
D_FF = 2816
GRID_W = 64
NA_ROWS = 8
NA_COLS = 16
SW_BLOCK = 128
ROPE_THETA = 10000.0
EPS = 1e-6
NEG = -1e30
QK_SCALE = HEAD_DIM ** -0.5

ADAM_LR = 0.001
ADAM_B1 = 0.9
ADAM_B2 = 0.999
ADAM_EPS = 1e-08
ADAM_WD = 0.01
ADAM_STEP = 10

N_SHARD = 4
N_DEV = 8
LANES = 128
SUBLANES = 8
TOKEN_TILE = 512
FF_TILE = 256
CONV_CHUNK = 256
NA_GROUP = 8
SW_GROUP_BLOCKS = 8
VMEM_BIG = 56 * 1024 * 1024


def _mm(a, b):
    return jnp.dot(a, b, preferred_element_type=F32)


def _mm_nt(a, b):
    return lax.dot_general(a, b, (((1,), (1,)), ((), ())), preferred_element_type=F32)


def _mm_tn(a, b):
    return lax.dot_general(a, b, (((0,), (0,)), ((), ())), preferred_element_type=F32)


def _cparams(sem=None, vmem=None):
    kw = {}
    if sem is not None:
        kw["dimension_semantics"] = sem
    if vmem is not None:
        kw["vmem_limit_bytes"] = vmem
    return pltpu.CompilerParams(**kw)


def _resident(shape):
    return pl.BlockSpec(shape, lambda i: (0,) * len(shape), pipeline_mode=pl.Buffered(1))


def _sigmoid(x):
    return 1.0 / (1.0 + jnp.exp(-x))


def _rms_stats(x):
    r = lax.rsqrt(jnp.mean(x * x, axis=-1, keepdims=True) + EPS)
    return r, x * r


def _rms_bwd(dxn, xn, r):
    return r * (dxn - xn * jnp.mean(dxn * xn, axis=-1, keepdims=True))


def _my_pos():
    return lax.axis_index("x"), lax.axis_index("y"), lax.axis_index("c")


def _flip(v, bit):
    return 1 - v if bit else v


def _ada_forward(c8, w_ada, b_ada, rider):
    d = c8.shape[1]
    ncol = w_ada.shape[1]

    def body(c_ref, w_ref, b_ref, mod_ref, sc_ref, m_scr, mod_buf, ssem, rsem, ssem2, rsem2):
        x, y, c = _my_pos()
        me = 4 * x + 2 * y + c
        shard = 2 * x + y
        cv = c_ref[...]
        my_rows = pl.ds(pl.multiple_of(me * SUBLANES, SUBLANES), SUBLANES)
        sc_ref[my_rows, :] = cv * _sigmoid(cv)

        def copy1(k):
            peer = (_flip(x, (k >> 2) & 1), _flip(y, (k >> 1) & 1), _flip(c, k & 1))
            return pltpu.make_async_remote_copy(
                src_ref=sc_ref.at[my_rows, :], dst_ref=sc_ref.at[my_rows, :],
                send_sem=ssem.at[k - 1], recv_sem=rsem.at[k - 1], device_id=peer, device_id_type=MESH)

        sends = [copy1(k) for k in range(1, N_DEV)]
        for cp in sends:
            cp.start()
        for cp in sends:
            cp.wait_recv()
        m_scr[...] = _mm(sc_ref[...].astype(BF16), w_ref[...].astype(BF16))

        def copy2(k):
            px, py = _flip(x, (k >> 1) & 1), _flip(y, k & 1)
            rows = pl.ds(pl.multiple_of((4 * px + 2 * py + c) * SUBLANES, SUBLANES), SUBLANES)
            return pltpu.make_async_remote_copy(
                src_ref=m_scr.at[rows, :], dst_ref=mod_buf.at[shard],
                send_sem=ssem2.at[k - 1], recv_sem=rsem2.at[k - 1], device_id=(px, py, c), device_id_type=MESH)

        sends2 = [copy2(k) for k in range(1, N_SHARD)]
        for cp in sends2:
            cp.start()
        mod_buf[shard] = m_scr[my_rows, :]
        for cp in sends2:
            cp.wait_recv()
        for s in range(N_SHARD):
            mod_ref[:, s * ncol:(s + 1) * ncol] = mod_buf[s] + b_ref[:, s * ncol:(s + 1) * ncol]
        for cp in sends + sends2:
            cp.wait_send()

    vm = pl.BlockSpec(memory_space=pltpu.VMEM)
    return _hosted(
        body, rider, name="ada_forward", grid=(),
        out_shape=(jax.ShapeDtypeStruct((SUBLANES, N_SHARD * ncol), F32),
                   jax.ShapeDtypeStruct((N_DEV * SUBLANES, d), F32)),
        in_specs=[vm, vm, vm], out_specs=(vm, vm),
        scratch_shapes=[pltpu.VMEM((N_DEV * SUBLANES, ncol), F32), pltpu.VMEM((N_SHARD, SUBLANES, ncol), F32),
                        pltpu.SemaphoreType.DMA((N_DEV - 1,)), pltpu.SemaphoreType.DMA((N_DEV - 1,)),
                        pltpu.SemaphoreType.DMA((N_SHARD - 1,)), pltpu.SemaphoreType.DMA((N_SHARD - 1,))],
        compiler_params=_cparams(vmem=VMEM_BIG), args=[c8, w_ada, b_ada])


class _Rider:
    def __init__(self, kind, srcs, owns=()):
        self.kind, self.srcs, self.owns = kind, list(srcs), list(owns)
        n = len(self.srcs)
        sds = jax.ShapeDtypeStruct
        dma = pltpu.SemaphoreType.DMA
        if kind == "gather":
            self.out_shapes = [sds((N_SHARD,) + s.shape, s.dtype) for s in self.srcs]
            self.sems = [dma((n, N_SHARD - 1)), dma((n, N_SHARD - 1)), dma((n, N_SHARD - 1)), dma((n, N_SHARD - 1)),
                         dma((n,)), dma((n,))]
        elif kind == "scatter":
            self.out_shapes = ([sds((N_SHARD - 1,) + s.shape[1:], s.dtype) for s in self.srcs]
                               + [sds(o.shape[1:], o.dtype) for o in self.owns])
            m = max(len(self.owns), 1)
            self.sems = [dma((n, N_SHARD - 1)), dma((n, N_SHARD - 1)), dma((m,)), dma((m,))]
        else:
            self.out_shapes = [sds(s.shape, s.dtype) for s in self.srcs]
            self.sems = [dma((n,)), dma((n,))]

    @property
    def inputs(self):
        return self.srcs + self.owns

    def _halved(self, i):
        a = self.srcs[i]
        tile_rows = SUBLANES * (4 // jnp.dtype(a.dtype).itemsize)
        return self.kind == "gather" and a.shape[0] % (2 * tile_rows) == 0

    def copies(self, ins, outs, sems):
        n = len(self.srcs)
        x, y, c = _my_pos()
        shard = 2 * x + y
        remote, relay = [], []
        if self.kind == "swap":
            ssem, rsem = sems
            for i in range(n):
                remote.append(pltpu.make_async_remote_copy(
                    src_ref=ins[i], dst_ref=outs[i], send_sem=ssem.at[i], recv_sem=rsem.at[i],
                    device_id=(x, y, 1 - c), device_id_type=MESH))
            return remote, relay
        if self.kind == "gather":
            ssem, rsem, ssem2, rsem2, sib_s, sib_r = sems
        else:
            ssem, rsem, sib_s, sib_r = sems
        for i in range(n):
            if self.kind == "gather":
                remote.append(pltpu.make_async_remote_copy(
                    src_ref=ins[i], dst_ref=outs[i].at[shard], send_sem=sib_s.at[i], recv_sem=sib_r.at[i],
                    device_id=(x, y, 1 - c), device_id_type=MESH))
                half = ins[i].shape[0] // 2
                mine = pl.ds(pl.multiple_of(c * half, half), half) if self._halved(i) else None
            for k in range(1, N_SHARD):
                px, py = _flip(x, (k >> 1) & 1), _flip(y, k & 1)
                if self.kind == "gather":
                    src, dst = ins[i], outs[i].at[shard]
                    if mine is not None:
                        src, dst = src.at[mine], dst.at[mine]
                        got = outs[i].at[2 * px + py].at[mine]
                        relay.append(pltpu.make_async_remote_copy(
                            src_ref=got, dst_ref=got, send_sem=ssem2.at[i, k - 1], recv_sem=rsem2.at[i, k - 1],
                            device_id=(x, y, 1 - c), device_id_type=MESH))
                else:
                    src, dst = ins[i].at[2 * px + py], outs[i].at[k - 1]
                remote.append(pltpu.make_async_remote_copy(
                    src_ref=src, dst_ref=dst, send_sem=ssem.at[i, k - 1], recv_sem=rsem.at[i, k - 1],
                    device_id=(px, py, c), device_id_type=MESH))
        if self.kind == "scatter":
            for i in range(len(self.owns)):
                remote.append(pltpu.make_async_remote_copy(
                    src_ref=ins[n + i].at[shard], dst_ref=outs[n + i], send_sem=sib_s.at[i], recv_sem=sib_r.at[i],
                    device_id=(x, y, 1 - c), device_id_type=MESH))
        return remote, relay

    def start(self, ins, outs, sems):
        remote, _ = self.copies(ins, outs, sems)
        for cp in remote:
            cp.start()

    def wait(self, ins, outs, sems):
        remote, relay = self.copies(ins, outs, sems)
        for cp in remote:
            cp.wait_recv()
        for cp in relay:
            cp.start()
        for cp in relay:
            cp.wait_recv()
        for cp in remote + relay:
            cp.wait_send()


class _Riders:
    def __init__(self, riders):
        self.riders = list(riders)
        self.inputs = [a for r in self.riders for a in r.inputs]
        self.out_shapes = [s for r in self.riders for s in r.out_shapes]
        self.sems = [s for r in self.riders for s in r.sems]

    def _split(self, ins, outs, sems):
        for r in self.riders:
            ni, no, ns = len(r.inputs), len(r.out_shapes), len(r.sems)
            yield r, ins[:ni], outs[:no], sems[:ns]
            ins, outs, sems = ins[ni:], outs[no:], sems[ns:]

    def start(self, ins, outs, sems):
        for r, i, o, s in self._split(ins, outs, sems):
            r.start(i, o, s)

    def wait(self, ins, outs, sems):
        for r, i, o, s in self._split(ins, outs, sems):
            r.wait(i, o, s)


def _hosted(body, rider, *, name, grid, out_shape, in_specs, out_specs, scratch_shapes, compiler_params, args):
    out_shape, out_specs = list(out_shape), list(out_specs)
    if rider is None:
        outs = pl.pallas_call(body, name=name, grid=grid, out_shape=tuple(out_shape), in_specs=list(in_specs),
                              out_specs=tuple(out_specs), scratch_shapes=list(scratch_shapes),
                              compiler_params=compiler_params)(*args)
        return list(outs), []
    n_in, n_out, n_scr = len(in_specs), len(out_shape), len(scratch_shapes)
    nr_in, nr_out = len(rider.inputs), len(rider.out_shapes)
    n_steps = 1
    for size in grid:
        n_steps *= size

    def full(*refs):
        ins, refs = refs[:n_in], refs[n_in:]
        r_in, refs = refs[:nr_in], refs[nr_in:]
        outs, refs = refs[:n_out], refs[n_out:]
        r_out, refs = refs[:nr_out], refs[nr_out:]
        scr, sems = refs[:n_scr], refs[n_scr:]
        if grid:
            step = 0
            for ax, size in enumerate(grid):
                step = step * size + pl.program_id(ax)
            pl.when(step == 0)(lambda: rider.start(r_in, r_out, sems))
            body(*ins, *outs, *scr)
            pl.when(step == n_steps - 1)(lambda: rider.wait(r_in, r_out, sems))
        else:
            rider.start(r_in, r_out, sems)
            body(*ins, *outs, *scr)
            rider.wait(r_in, r_out, sems)

    hbm = pl.BlockSpec(memory_space=pl.ANY)
    res = pl.pallas_call(
        full, name=name, grid=grid, out_shape=tuple(out_shape + rider.out_shapes),
        in_specs=list(in_specs) + [hbm] * nr_in, out_specs=tuple(out_specs + [hbm] * nr_out),
        scratch_shapes=list(scratch_shapes) + rider.sems, compiler_params=compiler_params,
    )(*args, *rider.inputs)
    return list(res[:n_out]), list(res[n_out:])


def _ride_alone(rider, name):
    return _hosted(lambda: None, rider, name=name, grid=(), out_shape=[], in_specs=[], out_specs=[], scratch_shapes=[],
                   compiler_params=_cparams(), args=[])[1]


def _rope_rot(t):
    w = t.shape[1]
    lane = lax.broadcasted_iota(jnp.int32, t.shape, 1)
    first = (lane % HEAD_DIM) < (HEAD_DIM // 2)
    return jnp.where(first, pltpu.roll(t, w - HEAD_DIM // 2, 1), pltpu.roll(t, HEAD_DIM // 2, 1))


def _in_proj(x, mod3, g_attn, w_in_t, cos_t, sin_t, seq, rider=None):
    t, d = x.shape
    tm = TOKEN_TILE
    per_seq = seq // tm
    rope_lo, rope_hi = 3 * NA_WIDTH, 3 * NA_WIDTH + SW_WIDTH + SW_KV_WIDTH
    n_rep = (rope_hi - rope_lo) // LANES

    def body(x_ref, mod_ref, g_ref, w_ref, cos_ref, sin_ref, h_ref, p_ref):
        r, xn = _rms_stats(x_ref[...])
        shift, scale = mod_ref[0, :, 0:d], mod_ref[0, :, d:2 * d]
        hb = ((xn * g_ref[...]) * (1.0 + scale) + shift).astype(BF16)
        h_ref[...] = hb
        p_ref[:, :rope_lo] = _mm_nt(hb, w_ref[:rope_lo, :]).astype(BF16)
        pr = _mm_nt(hb, w_ref[rope_lo:rope_hi, :])
        cos = jnp.concatenate([cos_ref[...]] * n_rep, axis=1)
        sin = jnp.concatenate([sin_ref[...]] * n_rep, axis=1)
        p_ref[:, rope_lo:rope_hi] = (pr * cos + _rope_rot(pr) * sin).astype(BF16)
        p_ref[:, rope_hi:] = _mm_nt(hb, w_ref[rope_hi:, :]).astype(BF16)

    return _hosted(
        body, rider, name="in_proj", grid=(t // tm,),
        out_shape=[jax.ShapeDtypeStruct((t, d), BF16), jax.ShapeDtypeStruct((t, IN_WIDTH), BF16)],
        in_specs=[pl.BlockSpec((tm, d), lambda i: (i, 0)),
                  pl.BlockSpec((1, 1, 6 * d), lambda i: (i // per_seq, 0, 0)),
                  pl.BlockSpec((1, d), lambda i: (0, 0)),
                  pl.BlockSpec((IN_WIDTH, d), lambda i: (0, 0)),
                  pl.BlockSpec((tm, LANES), lambda i: (i % per_seq, 0)),
                  pl.BlockSpec((tm, LANES), lambda i: (i % per_seq, 0))],
        out_specs=[pl.BlockSpec((tm, d), lambda i: (i, 0)), pl.BlockSpec((tm, IN_WIDTH), lambda i: (i, 0))],
        scratch_shapes=[], compiler_params=_cparams(("arbitrary",), VMEM_BIG),
        args=[x, mod3, g_attn, w_in_t, cos_t, sin_t])


def _na_bias_pattern():
    n_dc = 2 * NA_COLS - 1
    j = np.arange(GRID_W)[:, None]
    m = np.arange(GRID_W * LANES)[None, :]
    q, lane = m // LANES, m % LANES
    k = lane % GRID_W
    cs = np.clip(q - NA_COLS // 2, 0, GRID_W - NA_COLS)
    ok = (k >= cs) & (k < cs + NA_COLS)
    hit = ok & (j < 2 * n_dc) & (lane // GRID_W == j // n_dc) & (k - q + (NA_COLS - 1) == j % n_dc)
    return jnp.asarray(hit.astype(np.float32)), jnp.asarray(np.where(ok, 0.0, NEG).astype(np.float32))


def _na_bias_tiles(rows2, expand, mask):
    n, width = rows2.shape[0], expand.shape[1]
    q_step = 16
    step = q_step * LANES

    def body(r_ref, e_ref, m_ref, o_ref):
        flat = jnp.dot(r_ref[...], e_ref[...], precision=lax.Precision.HIGHEST,
                       preferred_element_type=F32) + m_ref[...]
        for qq in range(q_step):
            o_ref[:, qq, :] = flat[:, qq * LANES:(qq + 1) * LANES]

    return pl.pallas_call(
        body, name="na_bias_tiles", grid=(width // step,),
        out_shape=jax.ShapeDtypeStruct((n, GRID_W, LANES), F32),
        in_specs=[pl.BlockSpec(rows2.shape, lambda i: (0, 0)), pl.BlockSpec((expand.shape[0], step), lambda i: (0, i)),
                  pl.BlockSpec((1, step), lambda i: (0, i))],
        out_specs=pl.BlockSpec((n, q_step, LANES), lambda i: (0, i, 0)),
        compiler_params=_cparams(("arbitrary",)),
    )(rows2, expand, mask)


def _na_prepare(k_ref, v_ref, km, vm):
    lane = lax.broadcasted_iota(jnp.int32, k_ref.shape, 1)
    low = lane < HEAD_DIM
    kv = k_ref[...]
    vv = v_ref[...]
    zero = jnp.zeros_like(kv)
    km[0] = jnp.where(low, kv, zero)
    km[1] = jnp.where(low, zero, kv)
    vm[0] = jnp.where(low, vv, zero)
    vm[1] = jnp.where(low, zero, vv)


def _na_window(r, n_rows):
    rs = jnp.clip(r - NA_ROWS // 2, 0, n_rows - NA_ROWS)
    return rs, r - rs


def _na_pair_window(ref, wrows):
    return jnp.concatenate([ref[0, wrows, :], ref[1, wrows, :]], axis=0)


def _na_scores(q, k2, tp_ref, off):
    bias = jnp.concatenate([tp_ref[h, 2 * w - off + (NA_ROWS - 1)] for h in range(2) for w in range(NA_ROWS // 2)],
                           axis=1)
    return _mm_nt(q, k2) * QK_SCALE + bias


def _pair_lse_block(lse):
    lane = lax.broadcasted_iota(jnp.int32, (lse[0].shape[0], LANES), 1)
    return jnp.where(lane < HEAD_DIM, lse[0], lse[1])


def _pair_softmax(s):
    win = s.shape[1] // 2
    halves, lse = [], []
    for h in range(2):
        sh = s[:, h * win:(h + 1) * win]
        m = jnp.max(sh, axis=-1, keepdims=True)
        e = jnp.exp(sh - m)
        l = jnp.sum(e, axis=-1, keepdims=True)
        halves.append(e / l)
        lse.append(m + jnp.log(l))
    return jnp.concatenate(halves, axis=1), _pair_lse_block(lse)


def _pair_probs_from_lse(s, lse_block):
    win = s.shape[1] // 2
    return jnp.concatenate([jnp.exp(s[:, h * win:(h + 1) * win] - lse_block[:, h * HEAD_DIM:h * HEAD_DIM + 1])
                            for h in range(2)], axis=1)


def _na_forward(proj, tiles, batch, seq, rider=None):
    t = proj.shape[0]
    n_rows = seq // GRID_W
    n_pairs = NA_WIDTH // LANES
    win = NA_ROWS * GRID_W

    def body(q_ref, k_ref, v_ref, tp_ref, o_ref, lse_ref, km, vm):
        _na_prepare(k_ref, v_ref, km, vm)

        def scores(r):
            rs, off = _na_window(r, n_rows)
            rows = pl.ds(pl.multiple_of(r * GRID_W, GRID_W), GRID_W)
            wrows = pl.ds(pl.multiple_of(rs * GRID_W, GRID_W), win)
            return rows, wrows, _na_scores(q_ref[rows, :], _na_pair_window(km, wrows), tp_ref, off)

        def finish(rows, wrows, s):
            p, lse = _pair_softmax(s)
            lse_ref[rows, :] = lse
            o_ref[rows, :] = _mm(p.astype(BF16), _na_pair_window(vm, wrows))

        def row_group(i, carry):
            for state in [scores(NA_GROUP * i + j) for j in range(NA_GROUP)]:
                finish(*state)
            return carry

        lax.fori_loop(0, n_rows // NA_GROUP, row_group, 0)

    return _hosted(
        body, rider, name="na_forward", grid=(batch, n_pairs),
        out_shape=[jax.ShapeDtypeStruct((t, NA_WIDTH), F32), jax.ShapeDtypeStruct((t, NA_WIDTH), F32)],
        in_specs=[pl.BlockSpec((seq, LANES), lambda b, p: (b, p)),
                  pl.BlockSpec((seq, LANES), lambda b, p: (b, n_pairs + p)),
                  pl.BlockSpec((seq, LANES), lambda b, p: (b, 2 * n_pairs + p)),
                  pl.BlockSpec((2, 2 * NA_ROWS - 2, GRID_W, LANES), lambda b, p: (p, 0, 0, 0))],
        out_specs=[pl.BlockSpec((seq, LANES), lambda b, p: (b, p)), pl.BlockSpec((seq, LANES), lambda b, p: (b, p))],
        scratch_shapes=[pltpu.VMEM((2, seq, LANES), BF16), pltpu.VMEM((2, seq, LANES), BF16)],
        compiler_params=_cparams(("arbitrary", "arbitrary")), args=[proj, proj, proj, tiles])


def _sw_prepare(kv_ref, g, dst_lo, dst_hi, seq):
    lane = lax.broadcasted_iota(jnp.int32, kv_ref.shape, 1)
    mine = (lane // HEAD_DIM) == g
    kg = jnp.where(mine, kv_ref[...].astype(F32), 0.0)
    kr = pltpu.roll(kg, HEAD_DIM, 1)
    first = g == 0
    zero = jnp.zeros((SW_BLOCK, LANES), BF16)
    for dst, val in ((dst_lo, jnp.where(first, kg, kr)), (dst_hi, jnp.where(first, kr, kg))):
        dst[0:SW_BLOCK, :] = zero
        dst[SW_BLOCK:SW_BLOCK + seq, :] = val.astype(BF16)
        dst[SW_BLOCK + seq:, :] = zero


def _sw_mask(n, seq):
    qi = lax.broadcasted_iota(jnp.int32, (SW_BLOCK, 3 * SW_BLOCK), 0)
    kj = lax.broadcasted_iota(jnp.int32, (SW_BLOCK, 3 * SW_BLOCK), 1)
    kpos = n * SW_BLOCK - SW_BLOCK + kj
    return (jnp.abs(qi + SW_BLOCK - kj) <= SW_BLOCK) & (kpos >= 0) & (kpos < seq)


def _sw_probs(s2, ok, sinks):
    band = s2.shape[1] // 2
    halves, lse = [], []
    for i in range(2):
        s = jnp.where(ok, s2[:, i * band:(i + 1) * band], NEG)
        m = jnp.maximum(jnp.max(s, axis=-1, keepdims=True), sinks[i])
        p = jnp.exp(s - m)
        den = jnp.sum(p, axis=-1, keepdims=True) + jnp.exp(sinks[i] - m)
        halves.append(p / den)
        lse.append(m + jnp.log(den))
    return jnp.concatenate(halves, axis=1), _pair_lse_block(lse)


def _sw_probs_from_lse(s2, ok, sinks, lse_block):
    band = s2.shape[1] // 2
    halves, sink_p = [], []
    for i in range(2):
        lse = lse_block[:, i * HEAD_DIM:i * HEAD_DIM + 1]
        halves.append(jnp.exp(jnp.where(ok, s2[:, i * band:(i + 1) * band], NEG) - lse))
        sink_p.append(jnp.exp(sinks[i] - lse))
    return jnp.concatenate(halves, axis=1), sink_p


def _sw_forward(proj, sink, batch, seq, rider=None):
    t = proj.shape[0]
    n_pairs = SW_WIDTH // LANES
    q_blk = 3 * NA_WIDTH // LANES
    k_blk = q_blk + n_pairs
    n_blocks = seq // SW_BLOCK
    pad = seq + 2 * SW_BLOCK

    def body(sink_ref, q_ref, k_ref, v_ref, o_ref, lse_ref, k_lo, k_hi, v_lo, v_hi):
        hp = pl.program_id(1)
        g = hp // 2
        _sw_prepare(k_ref, g, k_lo, k_hi, seq)
        _sw_prepare(v_ref, g, v_lo, v_hi, seq)

        sinks = (sink_ref[2 * hp], sink_ref[2 * hp + 1])

        def scores(n):
            rows = pl.ds(pl.multiple_of(n * SW_BLOCK, SW_BLOCK), SW_BLOCK)
            wrows = pl.ds(pl.multiple_of(n * SW_BLOCK, SW_BLOCK), 3 * SW_BLOCK)
            k2 = jnp.concatenate([k_lo[wrows, :], k_hi[wrows, :]], axis=0)
            return n, rows, wrows, _mm_nt(q_ref[rows, :], k2) * QK_SCALE

        def finish(n, rows, wrows, s2):
            p, lse = _sw_probs(s2, _sw_mask(n, seq), sinks)
            lse_ref[rows, :] = lse
            v2 = jnp.concatenate([v_lo[wrows, :], v_hi[wrows, :]], axis=0)
            o_ref[rows, :] = _mm(p.astype(BF16), v2)

        def block_group(i, carry):
            for state in [scores(SW_GROUP_BLOCKS * i + j) for j in range(SW_GROUP_BLOCKS)]:
                finish(*state)
            return carry

        lax.fori_loop(0, n_blocks // SW_GROUP_BLOCKS, block_group, 0)

    return _hosted(
        body, rider, name="sw_forward", grid=(batch, n_pairs),
        out_shape=[jax.ShapeDtypeStruct((t, SW_WIDTH), F32), jax.ShapeDtypeStruct((t, SW_WIDTH), F32)],
        in_specs=[pl.BlockSpec(memory_space=pltpu.SMEM),
                  pl.BlockSpec((seq, LANES), lambda b, p: (b, q_blk + p)),
                  pl.BlockSpec((seq, LANES), lambda b, p: (b, k_blk)),
                  pl.BlockSpec((seq, LANES), lambda b, p: (b, k_blk + 1))],
        out_specs=[pl.BlockSpec((seq, LANES), lambda b, p: (b, p)), pl.BlockSpec((seq, LANES), lambda b, p: (b, p))],
        scratch_shapes=[pltpu.VMEM((pad, LANES), BF16)] * 4,
        compiler_params=_cparams(("arbitrary", "arbitrary")), args=[sink, proj, proj, proj])


def _out_proj(oa, ob, g_na, g_sw, w_out, x, mod3, g_ffn, seq):
    t, d = x.shape
    tm = TOKEN_TILE
    per_seq = seq // tm

    def body(oa_ref, ob_ref, gna_ref, gsw_ref, w_ref, x_ref, mod_ref, gf_ref, oab_ref, mix_ref, x1_ref, h2_ref):
        _, na = _rms_stats(oa_ref[...])
        _, nb = _rms_stats(ob_ref[...])
        oab = jnp.concatenate([na * gna_ref[...], nb * gsw_ref[...]], axis=1).astype(BF16)
        oab_ref[...] = oab
        mix = _mm(oab, w_ref[...])
        mix_ref[...] = mix
        gate_a = mod_ref[0, :, 2 * d:3 * d]
        shift_f, scale_f = mod_ref[0, :, 3 * d:4 * d], mod_ref[0, :, 4 * d:5 * d]
        x1 = x_ref[...] + gate_a * mix
        x1_ref[...] = x1
        _, xn = _rms_stats(x1)
        h2_ref[...] = ((xn * gf_ref[...]) * (1.0 + scale_f) + shift_f).astype(BF16)

    tile = lambda w: pl.BlockSpec((tm, w), lambda i: (i, 0))
    vec = lambda w: pl.BlockSpec((1, w), lambda i: (0, 0))
    return pl.pallas_call(
        body, name="out_proj", grid=(t // tm,),
        out_shape=(jax.ShapeDtypeStruct((t, d), BF16), jax.ShapeDtypeStruct((t, d), F32),
                   jax.ShapeDtypeStruct((t, d), F32), jax.ShapeDtypeStruct((t, d), BF16)),
        in_specs=[tile(NA_WIDTH), tile(SW_WIDTH), vec(NA_WIDTH), vec(SW_WIDTH),
                  pl.BlockSpec((d, d), lambda i: (0, 0)), tile(d),
                  pl.BlockSpec((1, 1, 6 * d), lambda i: (i // per_seq, 0, 0)), vec(d)],
        out_specs=(tile(d), tile(d), tile(d), tile(d)),
        compiler_params=_cparams(("arbitrary",), VMEM_BIG),
    )(oa, ob, g_na, g_sw, w_out, x, mod3, g_ffn)


def _up_proj(h2, w_up_halves, rider=None):
    t, d = h2.shape
    tm = TOKEN_TILE
    w_a, w_b = w_up_halves
    half, wcol = w_a.shape[1], w_a.shape[2]

    def body(h_ref, wa_ref, wb_ref, u_ref):
        u_ref[0] = (_mm(h_ref[:, :half], wa_ref[0]) + _mm(h_ref[:, half:], wb_ref[0])).astype(BF16)

    w_spec = pl.BlockSpec((1, half, wcol), lambda j, i: (j, 0, 0))
    return _hosted(
        body, rider, name="up_proj", grid=(N_SHARD, t // tm),
        out_shape=[jax.ShapeDtypeStruct((2, t, D_FF), BF16)],
        in_specs=[pl.BlockSpec((tm, d), lambda j, i: (i, 0)), w_spec, w_spec],
        out_specs=[pl.BlockSpec((1, tm, wcol), lambda j, i: (j // 2, i, j % 2))],
        scratch_shapes=[], compiler_params=_cparams(("arbitrary", "arbitrary"), VMEM_BIG), args=[h2, w_a, w_b])


def _taps_chunk(load, s, rows, seq):
    halo = 2 * SUBLANES
    cur = load(s, rows)
    above = load(pl.multiple_of(jnp.maximum(s - halo, 0), halo), halo)
    below = load(pl.multiple_of(jnp.minimum(s + rows, seq - halo), halo), halo)
    up = jnp.where(s > 0, above[halo - 1:halo, :], 0.0)
    dn = jnp.where(s + rows < seq, below[0:1, :], 0.0)
    row = lax.broadcasted_iota(jnp.int32, cur.shape, 0)
    prev = jnp.where(row == 0, up, pltpu.roll(cur, 1, 0))
    nxt = jnp.where(row == rows - 1, dn, pltpu.roll(cur, rows - 1, 0))
    return cur, prev, nxt


def _conv_gate(u, conv_w, conv_b, batch, seq):
    t = u.shape[1]
    cw = FF_TILE
    rows = CONV_CHUNK

    def body(u_ref, w_ref, b_ref, a_ref):
        def chunk(i, carry):
            s = pl.multiple_of(i * rows, rows)
            gt, prev, nxt = _taps_chunk(lambda at, n: u_ref[1, pl.ds(at, n), :].astype(F32), s, rows, seq)
            gc = prev * w_ref[0:1, :] + gt * w_ref[1:2, :] + nxt * w_ref[2:3, :] + b_ref[...]
            a_ref[pl.ds(s, rows), :] = ((gc * _sigmoid(gc)) * u_ref[0, pl.ds(s, rows), :].astype(F32)).astype(BF16)
            return carry

        lax.fori_loop(0, seq // rows, chunk, 0)

    return pl.pallas_call(
        body, name="conv_gate", grid=(batch, D_FF // cw),
        out_shape=jax.ShapeDtypeStruct((t, D_FF), BF16),
        in_specs=[pl.BlockSpec((2, seq, cw), lambda b, j: (0, b, j)),
                  pl.BlockSpec((3, cw), lambda b, j: (0, j)), pl.BlockSpec((1, cw), lambda b, j: (0, j))],
        out_specs=pl.BlockSpec((seq, cw), lambda b, j: (b, j)),
        compiler_params=_cparams(("arbitrary", "arbitrary"), VMEM_BIG),
    )(u, conv_w, conv_b)


def _down_and_loss(a, w_down, x1, mod3, g_final, target, seq):
    t, d = x1.shape
    tm = TOKEN_TILE
    per_seq = seq // tm
    batch = t // seq

    def body(a_ref, w_ref, x1_ref, mod_ref, g_ref, tgt_ref, dx2_ref, dffn_ref, loss_ref, dgate_ref, dg_ref):
        i = pl.program_id(0)
        f = _mm(a_ref[...], w_ref[...])
        gate_f = mod_ref[0, :, 5 * d:6 * d]
        x2 = x1_ref[...] + gate_f * f
        r, xn = _rms_stats(x2)
        err = xn * g_ref[...] - tgt_ref[...]
        part = 0.5 * jnp.sum(jnp.mean(err * err, axis=-1, keepdims=True))
        dy = err / d
        dx2 = _rms_bwd(dy * g_ref[...], xn, r)
        dx2_ref[...] = dx2
        dffn_ref[...] = (dx2 * gate_f).astype(BF16)

        @pl.when(i == 0)
        def _():
            loss_ref[...] = jnp.zeros_like(loss_ref)
            dg_ref[...] = jnp.zeros_like(dg_ref)

        @pl.when(i % per_seq == 0)
        def _():
            dgate_ref[...] = jnp.zeros_like(dgate_ref)

        loss_ref[...] += part
        dg_ref[...] += jnp.sum(dy * xn, axis=0, keepdims=True)
        dgate_ref[0] += jnp.sum(dx2 * f, axis=0, keepdims=True)

    tile = lambda w: pl.BlockSpec((tm, w), lambda i: (i, 0))
    return pl.pallas_call(
        body, name="down_loss", grid=(t // tm,),
        out_shape=(jax.ShapeDtypeStruct((t, d), F32), jax.ShapeDtypeStruct((t, d), BF16),
                   jax.ShapeDtypeStruct((SUBLANES, LANES), F32), jax.ShapeDtypeStruct((batch, 1, d), F32),
                   jax.ShapeDtypeStruct((1, d), F32)),
        in_specs=[tile(D_FF), _resident((D_FF, d)), tile(d),
                  pl.BlockSpec((1, 1, 6 * d), lambda i: (i // per_seq, 0, 0)),
                  pl.BlockSpec((1, d), lambda i: (0, 0)), tile(d)],
        out_specs=(tile(d), tile(d), pl.BlockSpec((SUBLANES, LANES), lambda i: (0, 0)),
                   pl.BlockSpec((1, 1, d), lambda i: (i // per_seq, 0, 0)), pl.BlockSpec((1, d), lambda i: (0, 0))),
        compiler_params=_cparams(("arbitrary",), VMEM_BIG),
    )(a, w_down, x1, mod3, g_final, target)


def _down_weight_grad(a, dffn):
    t, dff = a.shape
    d = dffn.shape[1]
    tk = TOKEN_TILE
    n_k = t // tk

    def body(a_ref, df_ref, g_ref, gb_ref):
        k = pl.program_id(0)

        @pl.when(k == 0)
        def _():
            g_ref[...] = jnp.zeros_like(g_ref)

        g_ref[...] += _mm_tn(a_ref[...], df_ref[...])

        @pl.when(k == n_k - 1)
        def _():
            gb_ref[...] = g_ref[...].astype(BF16)

    whole = pl.BlockSpec((dff, d), lambda k: (0, 0))
    return pl.pallas_call(
        body, name="down_weight_grad", grid=(n_k,),
        out_shape=(jax.ShapeDtypeStruct((dff, d), F32), jax.ShapeDtypeStruct((dff, d), BF16)),
        in_specs=[pl.BlockSpec((tk, dff), lambda k: (k, 0)), pl.BlockSpec((tk, d), lambda k: (k, 0))],
        out_specs=(whole, whole),
        compiler_params=_cparams(("arbitrary",), VMEM_BIG),
    )(a, dffn)


def _ffn_backward(dffn, w_down, u, conv_w, conv_b, batch, seq, rider=None):
    t, d = dffn.shape
    cw = FF_TILE
    rows = CONV_CHUNK

    def body(df_ref, wd_ref, u_ref, w_ref, b_ref, du_ref, gcw_ref, gcb_ref, da_scr, dgc_scr):
        b = pl.program_id(1)
        da_scr[...] = _mm_nt(df_ref[...], wd_ref[...])

        @pl.when(b == 0)
        def _():
            gcw_ref[...] = jnp.zeros_like(gcw_ref)
            gcb_ref[...] = jnp.zeros_like(gcb_ref)

        def fold(v):
            return jnp.sum(v.reshape(rows // SUBLANES, SUBLANES, cw), axis=0)

        def chunk(i, carry):
            s = pl.multiple_of(i * rows, rows)
            here = pl.ds(s, rows)
            gt, prev, nxt = _taps_chunk(lambda at, n: u_ref[1, pl.ds(at, n), :].astype(F32), s, rows, seq)
            val, da = u_ref[0, here, :].astype(F32), da_scr[here, :]
            gc = prev * w_ref[0:1, :] + gt * w_ref[1:2, :] + nxt * w_ref[2:3, :] + b_ref[...]
            sg = _sigmoid(gc)
            sl = gc * sg
            du_ref[0, here, :] = (da * sl).astype(BF16)
            dgc = (da * val) * (sg * (1.0 + gc * (1.0 - sg)))
            dgc_scr[here, :] = dgc
            cb, c0, c1, c2 = carry
            return cb + fold(dgc), c0 + fold(dgc * prev), c1 + fold(dgc * gt), c2 + fold(dgc * nxt)

        zero = jnp.zeros((SUBLANES, cw), F32)
        cb, c0, c1, c2 = lax.fori_loop(0, seq // rows, chunk, (zero, zero, zero, zero))
        gcb_ref[...] += jnp.sum(cb, axis=0, keepdims=True)
        gcw_ref[0:1, :] += jnp.sum(c0, axis=0, keepdims=True)
        gcw_ref[1:2, :] += jnp.sum(c1, axis=0, keepdims=True)
        gcw_ref[2:3, :] += jnp.sum(c2, axis=0, keepdims=True)

        def chunk2(i, carry):
            s = pl.multiple_of(i * rows, rows)
            dgc, dprev, dnxt = _taps_chunk(lambda at, n: dgc_scr[pl.ds(at, n), :], s, rows, seq)
            du_ref[1, pl.ds(s, rows), :] = (dnxt * w_ref[0:1, :] + dgc * w_ref[1:2, :]
                                            + dprev * w_ref[2:3, :]).astype(BF16)
            return carry

        lax.fori_loop(0, seq // rows, chunk2, 0)

    return _hosted(
        body, rider, name="ffn_backward", grid=(D_FF // cw, batch),
        out_shape=[jax.ShapeDtypeStruct((2, t, D_FF), BF16),
                   jax.ShapeDtypeStruct((3, D_FF), F32), jax.ShapeDtypeStruct((1, D_FF), F32)],
        in_specs=[pl.BlockSpec((seq, d), lambda j, b: (b, 0)), pl.BlockSpec((cw, d), lambda j, b: (j, 0)),
                  pl.BlockSpec((2, seq, cw), lambda j, b: (0, b, j)),
                  pl.BlockSpec((3, cw), lambda j, b: (0, j)), pl.BlockSpec((1, cw), lambda j, b: (0, j))],
        out_specs=[pl.BlockSpec((2, seq, cw), lambda j, b: (0, b, j)),
                   pl.BlockSpec((3, cw), lambda j, b: (0, j)), pl.BlockSpec((1, cw), lambda j, b: (0, j))],
        scratch_shapes=[pltpu.VMEM((seq, cw), F32), pltpu.VMEM((seq, cw), F32)],
        compiler_params=_cparams(("arbitrary", "arbitrary"), VMEM_BIG), args=[dffn, w_down, u, conv_w, conv_b])


def _up_backward(du, w_up, x1, mod3, g_ffn, dx2, mix, seq, rider=None):
    _, t, _ = du.shape
    d = x1.shape[1]
    tm = TOKEN_TILE
    per_seq = seq // tm
    batch = t // seq
    w_a, w_b = w_up
    half, wcol = w_a.shape[1], w_a.shape[2]

    def body(du_ref, wa_ref, wb_ref, x1_ref, mod_ref, g_ref, dx2_ref, mix_ref,
             dx1_ref, dmix_ref, dsh_ref, dsc_ref, dga_ref, dg_ref):
        i = pl.program_id(0)
        parts = []
        for w_ref in (wa_ref, wb_ref):
            acc = jnp.zeros((tm, half), F32)
            for j in range(N_SHARD):
                acc = acc + _mm_nt(du_ref[j // 2, :, (j % 2) * wcol:(j % 2 + 1) * wcol], w_ref[j])
            parts.append(acc)
        dh = jnp.concatenate(parts, axis=1)
        gate_a = mod_ref[0, :, 2 * d:3 * d]
        scale_f = mod_ref[0, :, 4 * d:5 * d]
        r, xn = _rms_stats(x1_ref[...])
        xg = xn * g_ref[...]
        dxg = dh * (1.0 + scale_f)
        dx1 = dx2_ref[...] + _rms_bwd(dxg * g_ref[...], xn, r)
        dx1_ref[...] = dx1
        dmix_ref[...] = (dx1 * gate_a).astype(BF16)

        @pl.when(i == 0)
        def _():
            dg_ref[...] = jnp.zeros_like(dg_ref)

        @pl.when(i % per_seq == 0)
        def _():
            dsh_ref[...] = jnp.zeros_like(dsh_ref)
            dsc_ref[...] = jnp.zeros_like(dsc_ref)
            dga_ref[...] = jnp.zeros_like(dga_ref)

        dg_ref[...] += jnp.sum(dxg * xn, axis=0, keepdims=True)
        dsh_ref[0] += jnp.sum(dh, axis=0, keepdims=True)
        dsc_ref[0] += jnp.sum(dh * xg, axis=0, keepdims=True)
        dga_ref[0] += jnp.sum(dx1 * mix_ref[...], axis=0, keepdims=True)

    tile = lambda w: pl.BlockSpec((tm, w), lambda i: (i, 0))
    per_b = pl.BlockSpec((1, 1, d), lambda i: (i // per_seq, 0, 0))
    small = jax.ShapeDtypeStruct((batch, 1, d), F32)
    return _hosted(
        body, rider, name="up_backward", grid=(t // tm,),
        out_shape=[jax.ShapeDtypeStruct((t, d), F32), jax.ShapeDtypeStruct((t, d), BF16), small, small, small,
                   jax.ShapeDtypeStruct((1, d), F32)],
        in_specs=[pl.BlockSpec((2, tm, D_FF), lambda i: (0, i, 0)),
                  _resident((N_SHARD, half, wcol)), _resident((N_SHARD, half, wcol)), tile(d),
                  pl.BlockSpec((1, 1, 6 * d), lambda i: (i // per_seq, 0, 0)),
                  pl.BlockSpec((1, d), lambda i: (0, 0)), tile(d), tile(d)],
        out_specs=[tile(d), tile(d), per_b, per_b, per_b, pl.BlockSpec((1, d), lambda i: (0, 0))],
        scratch_shapes=[], compiler_params=_cparams(("arbitrary",), VMEM_BIG),
        args=[du, w_a, w_b, x1, mod3, g_ffn, dx2, mix])


def _up_weight_grad(h2, du, rider=None):
    t, d = h2.shape
    tk = TOKEN_TILE
    wcol = D_FF // 2
    half = d // 2
    n_k = t // tk

    def body(h_ref, du_ref, ga_ref, gb_ref, ga16_ref, gb16_ref):
        k = pl.program_id(1)

        @pl.when(k == 0)
        def _():
            ga_ref[...] = jnp.zeros_like(ga_ref)
            gb_ref[...] = jnp.zeros_like(gb_ref)

        du = du_ref[0]
        ga_ref[0] += _mm_tn(h_ref[:, :half], du)
        gb_ref[0] += _mm_tn(h_ref[:, half:], du)

        @pl.when(k == n_k - 1)
        def _():
            ga16_ref[...] = ga_ref[...].astype(BF16)
            gb16_ref[...] = gb_ref[...].astype(BF16)

    g_spec = pl.BlockSpec((1, half, wcol), lambda j, k: (j, 0, 0))
    f32_out = jax.ShapeDtypeStruct((N_SHARD, half, wcol), F32)
    b16_out = jax.ShapeDtypeStruct((N_SHARD, half, wcol), BF16)
    return _hosted(
        body, rider, name="up_weight_grad", grid=(N_SHARD, n_k),
        out_shape=[f32_out, f32_out, b16_out, b16_out],
        in_specs=[pl.BlockSpec((tk, d), lambda j, k: (k, 0)),
                  pl.BlockSpec((1, tk, wcol), lambda j, k: (j // 2, k, j % 2))],
        out_specs=[g_spec, g_spec, g_spec, g_spec], scratch_shapes=[],
        compiler_params=_cparams(("arbitrary", "arbitrary"), VMEM_BIG), args=[h2, du])


def _out_backward(dmix, w_out, oab, oa, ob, g_na, g_sw):
    t, d = dmix.shape
    tm = TOKEN_TILE
    hw = NA_WIDTH

    def body(dm_ref, w_ref, oab_ref, oa_ref, ob_ref, gna_ref, gsw_ref,
             doa_ref, dob_ref, gw_ref, gwb_ref, dgna_ref, dgsw_ref):
        @pl.when(pl.program_id(0) == 0)
        def _():
            gw_ref[...] = jnp.zeros_like(gw_ref)
            dgna_ref[...] = jnp.zeros_like(dgna_ref)
            dgsw_ref[...] = jnp.zeros_like(dgsw_ref)

        dm = dm_ref[...]
        gw_ref[...] += _mm_tn(oab_ref[...], dm)

        @pl.when(pl.program_id(0) == t // tm - 1)
        def _():
            gwb_ref[...] = gw_ref[...].astype(BF16)

        do = _mm_nt(dm, w_ref[...])
        for raw_ref, g_ref, dst_ref, dg_ref, lo in ((oa_ref, gna_ref, doa_ref, dgna_ref, 0),
                                                     (ob_ref, gsw_ref, dob_ref, dgsw_ref, hw)):
            r, xn = _rms_stats(raw_ref[...])
            dpart = do[:, lo:lo + hw]
            dg_ref[...] += jnp.sum(dpart * xn, axis=0, keepdims=True)
            dst_ref[...] = _rms_bwd(dpart * g_ref[...], xn, r).astype(BF16)

    tile = lambda w: pl.BlockSpec((tm, w), lambda i: (i, 0))
    vec = lambda w: pl.BlockSpec((1, w), lambda i: (0, 0))
    return pl.pallas_call(
        body, name="out_backward", grid=(t // tm,),
        out_shape=(jax.ShapeDtypeStruct((t, hw), BF16), jax.ShapeDtypeStruct((t, hw), BF16),
                   jax.ShapeDtypeStruct((d, d), F32), jax.ShapeDtypeStruct((d, d), BF16),
                   jax.ShapeDtypeStruct((1, hw), F32), jax.ShapeDtypeStruct((1, hw), F32)),
        in_specs=[tile(d), pl.BlockSpec((d, d), lambda i: (0, 0)), tile(d), tile(hw), tile(hw), vec(hw), vec(hw)],
        out_specs=(tile(hw), tile(hw), pl.BlockSpec((d, d), lambda i: (0, 0)), pl.BlockSpec((d, d), lambda i: (0, 0)),
                   vec(hw), vec(hw)),
        compiler_params=_cparams(("arbitrary",), VMEM_BIG),
    )(dmix, w_out, oab, oa, ob, g_na, g_sw)


def _na_backward(proj, d_o, lse, tiles, batch, seq, rider=None):
    t = proj.shape[0]
    n_rows = seq // GRID_W
    n_pairs = NA_WIDTH // LANES
    win = NA_ROWS * GRID_W
    n_tiles = 2 * NA_ROWS - 2

    def body(q_ref, k_ref, v_ref, do_ref, lse_ref, tp_ref, dq_ref, dk_ref, dv_ref, dtp_ref, km, vm, dk_acc, dv_acc):
        @pl.when(pl.program_id(1) == 0)
        def _():
            dtp_ref[...] = jnp.zeros_like(dtp_ref)

        _na_prepare(k_ref, v_ref, km, vm)
        dk_acc[...] = jnp.zeros_like(dk_acc)
        dv_acc[...] = jnp.zeros_like(dv_acc)
        low = lax.broadcasted_iota(jnp.int32, (win, LANES), 1) < HEAD_DIM

        def scores(r):
            rs, off = _na_window(r, n_rows)
            rows = pl.ds(pl.multiple_of(r * GRID_W, GRID_W), GRID_W)
            wrows = pl.ds(pl.multiple_of(rs * GRID_W, GRID_W), win)
            q, do = q_ref[rows, :], do_ref[rows, :]
            k2 = _na_pair_window(km, wrows)
            s = _na_scores(q, k2, tp_ref, off)
            dp = _mm_nt(do, _na_pair_window(vm, wrows))
            return rows, wrows, off, q, do, k2, s, dp

        def finish(rows, wrows, off, q, do, k2, s, dp):
            p = _pair_probs_from_lse(s, lse_ref[rows, :])
            parts = []
            for h in range(2):
                ph, dph = p[:, h * win:(h + 1) * win], dp[:, h * win:(h + 1) * win]
                dsh = ph * (dph - jnp.sum(ph * dph, axis=-1, keepdims=True))
                for w in range(NA_ROWS // 2):
                    dtp_ref[h, 2 * w - off + (NA_ROWS - 1)] += dsh[:, w * LANES:(w + 1) * LANES]
                parts.append(dsh)
            dsb = (jnp.concatenate(parts, axis=1) * QK_SCALE).astype(BF16)
            dq_ref[rows, :] = _mm(dsb, k2).astype(BF16)
            dk2 = _mm_tn(dsb, q)
            dv2 = _mm_tn(p.astype(BF16), do)
            dk_acc[wrows, :] += jnp.where(low, dk2[:win], dk2[win:])
            dv_acc[wrows, :] += jnp.where(low, dv2[:win], dv2[win:])

        def row_group(i, carry):
            for state in [scores(NA_GROUP * i + j) for j in range(NA_GROUP)]:
                finish(*state)
            return carry

        lax.fori_loop(0, n_rows // NA_GROUP, row_group, 0)
        dk_ref[...] = dk_acc[...].astype(BF16)
        dv_ref[...] = dv_acc[...].astype(BF16)

    blk = lambda off: pl.BlockSpec((seq, LANES), lambda p, b: (b, off + p))
    out = jax.ShapeDtypeStruct((t, NA_WIDTH), BF16)
    return _hosted(
        body, rider, name="na_backward", grid=(n_pairs, batch),
        out_shape=[out, out, out, jax.ShapeDtypeStruct(tiles.shape, F32)],
        in_specs=[blk(0), blk(n_pairs), blk(2 * n_pairs), blk(0), blk(0),
                  pl.BlockSpec((2, n_tiles, GRID_W, LANES), lambda p, b: (p, 0, 0, 0))],
        out_specs=[blk(0), blk(0), blk(0), pl.BlockSpec((2, n_tiles, GRID_W, LANES), lambda p, b: (p, 0, 0, 0))],
        scratch_shapes=[pltpu.VMEM((2, seq, LANES), BF16), pltpu.VMEM((2, seq, LANES), BF16),
                        pltpu.VMEM((seq, LANES), F32), pltpu.VMEM((seq, LANES), F32)],
        compiler_params=_cparams(("arbitrary", "arbitrary")), args=[proj, proj, proj, d_o, lse, tiles])


def _na_bias_grad(dtiles, expand):
    n = dtiles.shape[0]

    def body(t_ref, e_ref, o_ref):
        flat = jnp.concatenate([t_ref[:, qq, :] for qq in range(GRID_W)], axis=1)
        o_ref[...] = lax.dot_general(flat, e_ref[...], (((1,), (1,)), ((), ())),
                                     precision=lax.Precision.HIGHEST, preferred_element_type=F32)

    return pl.pallas_call(
        body, name="na_bias_grad",
        out_shape=jax.ShapeDtypeStruct((n, expand.shape[0]), F32),
        compiler_params=_cparams(vmem=VMEM_BIG),
    )(dtiles, expand)


def _sw_backward(proj, d_o, lse, sink, batch, seq, rider=None):
    t = proj.shape[0]
    n_pairs = SW_WIDTH // LANES
    q_blk = 3 * NA_WIDTH // LANES
    k_blk = q_blk + n_pairs
    n_blocks = seq // SW_BLOCK
    pad = seq + 2 * SW_BLOCK

    def body(sink_ref, q_ref, k_ref, v_ref, do_ref, lse_ref, dq_ref, dk_ref, dv_ref, dsk_ref,
             k_lo, k_hi, v_lo, v_hi, dk_loc, dv_loc, dk_tot, dv_tot):
        hp = pl.program_id(1)
        g = hp // 2
        _sw_prepare(k_ref, g, k_lo, k_hi, seq)
        _sw_prepare(v_ref, g, v_lo, v_hi, seq)
        dk_loc[...] = jnp.zeros_like(dk_loc)
        dv_loc[...] = jnp.zeros_like(dv_loc)

        @pl.when(hp == 0)
        def _():
            dk_tot[...] = jnp.zeros_like(dk_tot)
            dv_tot[...] = jnp.zeros_like(dv_tot)

        band = 3 * SW_BLOCK
        low = lax.broadcasted_iota(jnp.int32, (band, LANES), 1) < HEAD_DIM

        sinks = (sink_ref[2 * hp], sink_ref[2 * hp + 1])

        def scores(n):
            rows = pl.ds(pl.multiple_of(n * SW_BLOCK, SW_BLOCK), SW_BLOCK)
            wrows = pl.ds(pl.multiple_of(n * SW_BLOCK, SW_BLOCK), band)
            qb, do = q_ref[rows, :], do_ref[rows, :]
            k2 = jnp.concatenate([k_lo[wrows, :], k_hi[wrows, :]], axis=0)
            v2 = jnp.concatenate([v_lo[wrows, :], v_hi[wrows, :]], axis=0)
            return n, rows, wrows, qb, do, k2, _mm_nt(qb, k2) * QK_SCALE, _mm_nt(do, v2)

        def finish(sink_acc, n, rows, wrows, qb, do, k2, s2, dp):
            p, ps = _sw_probs_from_lse(s2, _sw_mask(n, seq), sinks, lse_ref[rows, :])
            parts, new = [], []
            for i in range(2):
                ph, dph = p[:, i * band:(i + 1) * band], dp[:, i * band:(i + 1) * band]
                delta = jnp.sum(ph * dph, axis=-1, keepdims=True)
                parts.append(ph * (dph - delta))
                new.append(sink_acc[i] - ps[i] * delta)
            dsb = (jnp.concatenate(parts, axis=1) * QK_SCALE).astype(BF16)
            dq_ref[rows, :] = _mm(dsb, k2)
            dk2 = _mm_tn(dsb, qb)
            dv2 = _mm_tn(p.astype(BF16), do)
            dk_loc[wrows, :] += jnp.where(low, dk2[:band], dk2[band:])
            dv_loc[wrows, :] += jnp.where(low, dv2[:band], dv2[band:])
            return tuple(new)

        def block_group(i, carry):
            for state in [scores(SW_GROUP_BLOCKS * i + j) for j in range(SW_GROUP_BLOCKS)]:
                carry = finish(carry, *state)
            return carry

        zero = jnp.zeros((SW_BLOCK, 1), F32)
        s0, s1 = lax.fori_loop(0, n_blocks // SW_GROUP_BLOCKS, block_group, (zero, zero))
        row = lax.broadcasted_iota(jnp.int32, (SUBLANES, LANES), 0)
        dsk_ref[0, 0] = jnp.where(row == 0, jnp.sum(s0), jnp.where(row == 1, jnp.sum(s1), 0.0))

        lane_s = lax.broadcasted_iota(jnp.int32, (seq, LANES), 1)
        mine_g = (lane_s // HEAD_DIM) == g
        for loc, tot in ((dk_loc, dk_tot), (dv_loc, dv_tot)):
            part = loc[SW_BLOCK:SW_BLOCK + seq, :]
            tot[...] += jnp.where(mine_g, part + pltpu.roll(part, HEAD_DIM, 1), 0.0)

        @pl.when(hp == n_pairs - 1)
        def _():
            dk_ref[...] = dk_tot[...]
            dv_ref[...] = dv_tot[...].astype(BF16)

    return _hosted(
        body, rider, name="sw_backward", grid=(batch, n_pairs),
        out_shape=[jax.ShapeDtypeStruct((t, SW_WIDTH), F32), jax.ShapeDtypeStruct((t, LANES), F32),
                   jax.ShapeDtypeStruct((t, LANES), BF16), jax.ShapeDtypeStruct((batch, n_pairs, SUBLANES, LANES), F32)],
        in_specs=[pl.BlockSpec(memory_space=pltpu.SMEM),
                  pl.BlockSpec((seq, LANES), lambda b, p: (b, q_blk + p)),
                  pl.BlockSpec((seq, LANES), lambda b, p: (b, k_blk)),
                  pl.BlockSpec((seq, LANES), lambda b, p: (b, k_blk + 1)),
                  pl.BlockSpec((seq, LANES), lambda b, p: (b, p)), pl.BlockSpec((seq, LANES), lambda b, p: (b, p))],
        out_specs=[pl.BlockSpec((seq, LANES), lambda b, p: (b, p)), pl.BlockSpec((seq, LANES), lambda b, p: (b, 0)),
                   pl.BlockSpec((seq, LANES), lambda b, p: (b, 0)),
                   pl.BlockSpec((1, 1, SUBLANES, LANES), lambda b, p: (b, p, 0, 0))],
        scratch_shapes=[pltpu.VMEM((pad, LANES), BF16)] * 4 + [pltpu.VMEM((pad, LANES), F32)] * 2
        + [pltpu.VMEM((seq, LANES), F32)] * 2,
        compiler_params=_cparams(("arbitrary", "arbitrary")), args=[sink, proj, proj, proj, d_o, lse])


def _in_backward(dqkv_a, dq_b, dk_b, dv_b, w_in_t, h1, x, mod3, g_attn, dx1, cos_t, sin_t, seq):
    t, d = x.shape
    tm = TOKEN_TILE
    per_seq = seq // tm
    batch = t // seq
    dqa, dka, dva = dqkv_a
    n_q = SW_WIDTH // LANES

    def body(dqa_ref, dka_ref, dva_ref, dqb_ref, dkb_ref, dvb_ref, w_ref, h_ref, x_ref, mod_ref, g_ref, dx1_ref,
             cos_ref, sin_ref, dx_ref, gw_ref, gwb_ref, dsh_ref, dsc_ref, dg_ref):
        i = pl.program_id(0)

        @pl.when(i == 0)
        def _():
            gw_ref[...] = jnp.zeros_like(gw_ref)
            dg_ref[...] = jnp.zeros_like(dg_ref)

        @pl.when(i % per_seq == 0)
        def _():
            dsh_ref[...] = jnp.zeros_like(dsh_ref)
            dsc_ref[...] = jnp.zeros_like(dsc_ref)

        dr = jnp.concatenate([dqb_ref[...], dkb_ref[...]], axis=1)
        cos = jnp.concatenate([cos_ref[...]] * (n_q + 1), axis=1)
        sin = jnp.concatenate([sin_ref[...]] * (n_q + 1), axis=1)
        dr = dr * cos + _rope_rot(dr * sin)
        dproj = jnp.concatenate([dqa_ref[...], dka_ref[...], dva_ref[...], dr.astype(BF16), dvb_ref[...]], axis=1)
        gw_ref[...] += _mm_tn(dproj, h_ref[...])

        @pl.when(i == t // tm - 1)
        def _():
            gwb_ref[...] = gw_ref[...].astype(BF16)

        dh = _mm(dproj, w_ref[...])
        scale = mod_ref[0, :, d:2 * d]
        r, xn = _rms_stats(x_ref[...])
        xg = xn * g_ref[...]
        dxg = dh * (1.0 + scale)
        dx_ref[...] = dx1_ref[...] + _rms_bwd(dxg * g_ref[...], xn, r)
        dg_ref[...] += jnp.sum(dxg * xn, axis=0, keepdims=True)
        dsh_ref[0] += jnp.sum(dh, axis=0, keepdims=True)
        dsc_ref[0] += jnp.sum(dh * xg, axis=0, keepdims=True)

    tile = lambda w: pl.BlockSpec((tm, w), lambda i: (i, 0))
    per_b = pl.BlockSpec((1, 1, d), lambda i: (i // per_seq, 0, 0))
    small = jax.ShapeDtypeStruct((batch, 1, d), F32)
    rope = pl.BlockSpec((tm, LANES), lambda i: (i % per_seq, 0))
    return pl.pallas_call(
        body, name="in_backward", grid=(t // tm,),
        out_shape=(jax.ShapeDtypeStruct((t, d), F32), jax.ShapeDtypeStruct((IN_WIDTH, d), F32),
                   jax.ShapeDtypeStruct((IN_WIDTH, d), BF16), small, small, jax.ShapeDtypeStruct((1, d), F32)),
        in_specs=[tile(NA_WIDTH), tile(NA_WIDTH), tile(NA_WIDTH), tile(SW_WIDTH), tile(LANES), tile(LANES),
                  _resident((IN_WIDTH, d)), tile(d), tile(d),
                  pl.BlockSpec((1, 1, 6 * d), lambda i: (i // per_seq, 0, 0)),
                  pl.BlockSpec((1, d), lambda i: (0, 0)), tile(d), rope, rope],
        out_specs=(tile(d), _resident((IN_WIDTH, d)), _resident((IN_WIDTH, d)),
                   per_b, per_b, pl.BlockSpec((1, d), lambda i: (0, 0))),
        compiler_params=_cparams(("arbitrary",), VMEM_BIG),
    )(dqa, dka, dva, dq_b, dk_b, dv_b, w_in_t, h1, x, mod3, g_attn, dx1, cos_t, sin_t)


def _ada_weight_grad(sc_all, dmod_cols):
    d = sc_all.shape[1]
    ncol = dmod_cols.shape[1]

    def body(s_ref, m_ref, o_ref):
        o_ref[...] = _mm_tn(s_ref[...].astype(BF16), m_ref[...].astype(BF16))

    return pl.pallas_call(
        body, name="ada_weight_grad",
        out_shape=jax.ShapeDtypeStruct((d, ncol), F32),
        compiler_params=_cparams(vmem=VMEM_BIG),
    )(sc_all, dmod_cols)


def _row_tile(rows, cols):
    target = max(SUBLANES, (1 << 20) // (4 * cols))
    best = rows
    for cand in range(SUBLANES, rows + 1, SUBLANES):
        if rows % cand == 0 and cand <= target:
            best = cand
    return best if rows % SUBLANES == 0 else rows


def _sum_slots(parts, name):
    n = len(parts)
    _, rows, cols = parts[0][0].shape
    tr = _row_tile(rows, cols)
    per = rows // tr

    def body(*refs):
        o_ref = refs[-1]
        for q in range(n):
            @pl.when(pl.program_id(0) == q)
            def _(q=q):
                p_ref, own_ref = refs[2 * q], refs[2 * q + 1]
                o_ref[...] = ((own_ref[...] + p_ref[0].astype(F32)) + p_ref[1].astype(F32)) + p_ref[2].astype(F32)

    in_specs, args = [], []
    for q, (recv, own) in enumerate(parts):
        in_specs.append(pl.BlockSpec((N_SHARD - 1, tr, cols), lambda p, i, q=q: (0, jnp.where(p == q, i, 0), 0)))
        in_specs.append(pl.BlockSpec((tr, cols), lambda p, i, q=q: (jnp.where(p == q, i, 0), 0)))
        args += [recv, own]
    return pl.pallas_call(
        body, name=name, grid=(n, per),
        out_shape=jax.ShapeDtypeStruct((n * rows, cols), F32),
        in_specs=in_specs, out_specs=pl.BlockSpec((tr, cols), lambda p, i: (p * per + i, 0)),
        compiler_params=_cparams(("arbitrary", "arbitrary")),
    )(*args)


def _adamw_math(w, g, m, v):
    m2 = ADAM_B1 * m + (1.0 - ADAM_B1) * g
    v2 = ADAM_B2 * v + (1.0 - ADAM_B2) * (g * g)
    m_hat = m2 / (1.0 - ADAM_B1 ** ADAM_STEP)
    v_hat = v2 / (1.0 - ADAM_B2 ** ADAM_STEP)
    return -ADAM_LR * (m_hat / (jnp.sqrt(v_hat) + ADAM_EPS) + ADAM_WD * w), m2, v2


def _small_step(partials, states, dmod, b_ada_state, rider=None):
    n_upd = len(states)
    moving = list(partials) + [dmod]
    n_mov = len(moving)
    all_states = list(states) + [b_ada_state]

    def body(*refs):
        mov, refs = refs[:n_mov], refs[n_mov:]
        wmv, refs = refs[:3 * (n_upd + 1)], refs[3 * (n_upd + 1):]
        res, refs = refs[:4 * (n_upd + 1)], refs[4 * (n_upd + 1):]
        sums_out, refs = refs[:n_mov - n_upd - 1], refs[n_mov - n_upd - 1:]
        dmod_out, refs = refs[0], refs[1:]
        everyone, (ssem, rsem) = refs[:n_mov], refs[n_mov:]
        x, y, c = _my_pos()
        me = 4 * x + 2 * y + c
        cps = []
        for a in range(n_mov):
            everyone[a][me] = mov[a][...]
            for k in range(1, N_DEV):
                peer = (_flip(x, (k >> 2) & 1), _flip(y, (k >> 1) & 1), _flip(c, k & 1))
                cps.append(pltpu.make_async_remote_copy(
                    src_ref=everyone[a].at[me], dst_ref=everyone[a].at[me], send_sem=ssem.at[a, k - 1],
                    recv_sem=rsem.at[a, k - 1], device_id=peer, device_id_type=MESH))
        for cp in cps:
            cp.start()
        for cp in cps:
            cp.wait_recv()

        def total(a):
            acc = everyone[a][0]
            for dev in range(1, N_DEV):
                acc = acc + everyone[a][dev]
            return acc

        grads = [total(a) for a in range(n_upd)]
        grads.append(jnp.sum(total(n_mov - 1), axis=0, keepdims=True))
        for j, g in enumerate(grads):
            delta, m2, v2 = _adamw_math(wmv[3 * j][...], g, wmv[3 * j + 1][...], wmv[3 * j + 2][...])
            res[4 * j][...] = g
            res[4 * j + 1][...] = delta
            res[4 * j + 2][...] = m2
            res[4 * j + 3][...] = v2
        for j in range(n_mov - n_upd - 1):
            sums_out[j][...] = total(n_upd + j)
        dmod_out[...] = everyone[n_mov - 1][...]
        for cp in cps:
            cp.wait_send()

    vm = pl.BlockSpec(memory_space=pltpu.VMEM)
    sds = jax.ShapeDtypeStruct
    out_shape = []
    for w, _, _ in all_states:
        out_shape += [sds(w.shape, F32)] * 4
    out_shape += [sds(p.shape, F32) for p in partials[n_upd:]]
    out_shape.append(sds((N_DEV,) + dmod.shape, F32))
    args = moving + [a for st in all_states for a in st]
    outs, rides = _hosted(
        body, rider, name="small_step", grid=(), out_shape=out_shape,
        in_specs=[vm] * len(args), out_specs=[vm] * len(out_shape),
        scratch_shapes=[pltpu.VMEM((N_DEV,) + a.shape, F32) for a in moving]
        + [pltpu.SemaphoreType.DMA((n_mov, N_DEV - 1)), pltpu.SemaphoreType.DMA((n_mov, N_DEV - 1))],
        compiler_params=_cparams(vmem=VMEM_BIG), args=args)
    return outs, rides


def _adamw(w, grads, m, v, name):
    rows, cols = w.shape
    tr = _row_tile(rows, cols)
    ng = len(grads)

    def body(*refs):
        w_ref = refs[0]
        g_refs = refs[1:1 + ng]
        m_ref, v_ref = refs[1 + ng], refs[2 + ng]
        g_out, d_out, m_out, v_out = refs[3 + ng:]
        g = g_refs[0][...]
        for extra in g_refs[1:]:
            g = g + extra[...]
        g_out[...] = g
        d_out[...], m_out[...], v_out[...] = _adamw_math(w_ref[...], g, m_ref[...], v_ref[...])

    spec = pl.BlockSpec((tr, cols), lambda i: (i, 0))
    out = jax.ShapeDtypeStruct((rows, cols), F32)
    return pl.pallas_call(
        body, name=name, grid=(rows // tr,),
        out_shape=(out, out, out, out),
        in_specs=[spec] * (3 + ng), out_specs=(spec, spec, spec, spec),
        compiler_params=_cparams(("arbitrary",)),
    )(w, *grads, m, v)


def _rope_tables(seq):
    half = HEAD_DIM // 2
    inv = np.float32(ROPE_THETA) ** (-np.arange(half, dtype=np.float32) / np.float32(half))
    ang = (np.arange(seq, dtype=np.float32)[:, None] * inv[None, :]).astype(np.float64)
    cos, sin = np.cos(ang).astype(np.float32), np.sin(ang).astype(np.float32)
    cos_t = np.concatenate([cos, cos, cos, cos], axis=1)
    sin_t = np.concatenate([-sin, sin, -sin, sin], axis=1)
    return jnp.asarray(cos_t), jnp.asarray(sin_t)


def kernel(x, c, w_ada, b_ada, g_attn, w_in, na_rpb, sw_sink, g_na_out, g_sw_out, w_out, g_ffn, w_up, conv_w, conv_b, w_down, g_final, loss_target, m_w_ada, m_b_ada, m_g_attn, m_w_in, m_na_rpb, m_sw_sink, m_g_na_out, m_g_sw_out, m_w_out, m_g_ffn, m_w_up, m_conv_w, m_conv_b, m_w_down, m_g_final, v_w_ada, v_b_ada, v_g_attn, v_w_in, v_na_rpb, v_sw_sink, v_g_na_out, v_g_sw_out, v_w_out, v_g_ffn, v_w_up, v_conv_w, v_conv_b, v_w_down, v_g_final):
    batch, seq, d = x.shape
    t = batch * seq
    assert d == D_MODEL and seq % (NA_ROWS * GRID_W) == 0 and seq % TOKEN_TILE == 0 and batch <= SUBLANES
    shard = 2 * lax.axis_index("x") + lax.axis_index("y")
    xt = x.reshape(t, d)
    tgt = loss_target.reshape(t, d)

    c8 = jnp.pad(c, ((0, SUBLANES - batch), (0, 0)))
    w_in_t_s = jnp.transpose(w_in[0]).astype(BF16)
    (mod8, sc_all), (w_in_g,) = _ada_forward(c8, w_ada[0], b_ada, _Rider("gather", [w_in_t_s]))
    mod3 = mod8[:batch].reshape(batch, 1, 6 * d)
    w_in_t = w_in_g.reshape(IN_WIDTH, d)

    cos_t, sin_t = _rope_tables(seq)
    (h1, proj), _ = _in_proj(xt, mod3, g_attn, w_in_t, cos_t, sin_t, seq)
    n_heads = NA_WIDTH // HEAD_DIM
    n_tiles, n_dc = 2 * NA_ROWS - 2, 2 * NA_COLS - 1
    expand, neg_mask = _na_bias_pattern()
    rpb = na_rpb[0]
    rows2 = jnp.concatenate([rpb[:, :-1, :], rpb[:, 1:, :]], axis=2).reshape(n_heads * n_tiles, 2 * n_dc)
    rows2 = jnp.pad(rows2, ((0, 0), (0, GRID_W - 2 * n_dc)))
    tiles = _na_bias_tiles(rows2, expand, neg_mask).reshape(n_heads, n_tiles, GRID_W, LANES)
    sink = sw_sink[0]
    w_up_b16 = w_up[0].astype(BF16)
    (oa, lse_a), (w_up_a, w_down_g) = _na_forward(proj, tiles, batch, seq,
                                                  _Rider("gather", [w_up_b16[:d // 2], w_down[0].astype(BF16)]))
    (ob, lse_b), (w_up_b, conv_w_g, w_out_g) = _sw_forward(
        proj, sink, batch, seq, _Rider("gather", [w_up_b16[d // 2:], conv_w[0], w_out[0].astype(BF16)]))
    w_up_f = (w_up_a, w_up_b)
    w_out_f = w_out_g.reshape(d, d)
    conv_w_f = jnp.transpose(conv_w_g, (1, 0, 2)).reshape(3, D_FF)
    oab, mix, x1, h2 = _out_proj(oa, ob, g_na_out, g_sw_out, w_out_f, xt, mod3, g_ffn, seq)
    (u,), _ = _up_proj(h2, w_up_f)
    w_down_f = w_down_g.reshape(D_FF, d)
    a = _conv_gate(u, conv_w_f, conv_b, batch, seq)
    dx2, dffn, loss_part, dgate_f, dg_final = _down_and_loss(a, w_down_f, x1, mod3, g_final.reshape(1, d), tgt, seq)

    gw_down, gw_down_b = _down_weight_grad(a, dffn)
    blocks = lambda g, rows: g.reshape(N_SHARD, rows // N_SHARD, d)
    (du, gconv_w, gconv_b), (recv_down, own_down) = _ffn_backward(
        dffn, w_down_f, u, conv_w_f, conv_b, batch, seq,
        _Rider("scatter", [blocks(gw_down_b, D_FF)], [blocks(gw_down, D_FF)]))
    (gw_up_top, gw_up_bot, gw_up_top_b, gw_up_bot_b), _ = _up_weight_grad(h2, du)
    (dx1, dmix, dshift_f, dscale_f, dgate_a, dg_ffn), _ = _up_backward(du, w_up_f, x1, mod3, g_ffn, dx2, mix, seq)
    doa, dob, gw_out, gw_out_b, dg_na, dg_sw = _out_backward(dmix, w_out_f, oab, oa, ob, g_na_out, g_sw_out)
    (dqa, dka, dva, dtiles), (recv_up_bot, own_up_bot) = _na_backward(
        proj, doa, lse_a, tiles, batch, seq, _Rider("scatter", [gw_up_bot_b], [gw_up_bot]))
    (dq_b, dk_b, dv_b, dsink_parts), (recv_out, recv_up_top, own_out, own_up_top) = _sw_backward(
        proj, dob, lse_b, sink, batch, seq,
        _Rider("scatter", [blocks(gw_out_b, d), gw_up_top_b], [blocks(gw_out, d), gw_up_top]))
    gx, gw_in_t, gw_in_b, dshift_a, dscale_a, dg_attn = _in_backward(
        (dqa, dka, dva), dq_b, dk_b, dv_b, w_in_t, h1, xt, mod3, g_attn, dx1, cos_t, sin_t, seq)

    red = _na_bias_grad(dtiles.reshape(n_heads * n_tiles, GRID_W, LANES), expand)[:, :2 * n_dc]
    red = red.reshape(n_heads, n_tiles, 2, n_dc)
    zero_row = jnp.zeros((n_heads, 1, n_dc), F32)
    g_rpb = (jnp.concatenate([red[:, :, 0, :], zero_row], axis=1)
             + jnp.concatenate([zero_row, red[:, :, 1, :]], axis=1))
    g_sink = jnp.sum(dsink_parts[:, :, :2, 0], axis=0).reshape(SW_WIDTH // HEAD_DIM)

    dmod = jnp.concatenate([dshift_a, dscale_a, dgate_a, dshift_f, dscale_f, dgate_f], axis=2).reshape(batch, 6 * d)
    rpb_shape = na_rpb.shape[1:]
    states = [(g_attn, m_g_attn, v_g_attn),
              (na_rpb.reshape(rpb_shape), m_na_rpb.reshape(rpb_shape), v_na_rpb.reshape(rpb_shape)),
              (sw_sink, m_sw_sink, v_sw_sink), (g_na_out, m_g_na_out, v_g_na_out), (g_sw_out, m_g_sw_out, v_g_sw_out),
              (g_ffn, m_g_ffn, v_g_ffn), (conv_b, m_conv_b, v_conv_b),
              (g_final.reshape(1, d), m_g_final.reshape(1, d), v_g_final.reshape(1, d))]
    partials = [dg_attn, g_rpb, g_sink.reshape(sw_sink.shape), dg_na, dg_sw, dg_ffn, gconv_b, dg_final,
                gconv_w, loss_part]
    mine = [None, _sum_slots([(recv_out, own_out)], "sum_w_out"),
            _sum_slots([(recv_up_top, own_up_top), (recv_up_bot, own_up_bot)], "sum_w_up"),
            _sum_slots([(recv_down, own_down)], "sum_w_down")]
    small, (recv_in, own_in, *theirs) = _small_step(
        partials, states, dmod, (b_ada, m_b_ada, v_b_ada),
        _Riders([_Rider("scatter", [blocks(gw_in_b, IN_WIDTH)], [blocks(gw_in_t, IN_WIDTH)]),
                 _Rider("swap", mine[1:])]))
    r_small = [small[4 * j:4 * j + 4] for j in range(len(states) + 1)]
    g_conv_w_full, loss_sum, dmod_all = small[4 * (len(states) + 1):]
    loss = loss_sum[0, 0]
    mine[0] = _sum_slots([(recv_in, own_in)], "sum_w_in")
    theirs = _ride_alone(_Rider("swap", mine[:1]), "swap_sibling") + theirs
    dmod_rows = jnp.pad(dmod_all, ((0, 0), (0, SUBLANES - batch), (0, 0))).reshape(N_DEV * SUBLANES, 6 * d)
    ncol = w_ada.shape[2]
    g_w_ada = _ada_weight_grad(sc_all, lax.dynamic_slice(dmod_rows, (0, shard * ncol), (N_DEV * SUBLANES, ncol)))
    cshard = conv_w.shape[2]
    g_conv_w = lax.dynamic_slice(g_conv_w_full, (0, shard * cshard), (3, cshard))

    def big(w, m, v, g_parts, name):
        shape = w.shape
        outs = _adamw(w[0], g_parts, m[0], v[0], name)
        return [o.reshape(shape) for o in outs]

    r_w_ada = big(w_ada, m_w_ada, v_w_ada, [g_w_ada], "adamw_w_ada")
    r_w_in = [jnp.transpose(o).reshape(w_in.shape) for o in
              _adamw(jnp.transpose(w_in[0]), [mine[0], theirs[0]], jnp.transpose(m_w_in[0]), jnp.transpose(v_w_in[0]),
                     "adamw_w_in")]
    r_w_out = big(w_out, m_w_out, v_w_out, [mine[1], theirs[1]], "adamw_w_out")
    r_w_up = big(w_up, m_w_up, v_w_up, [mine[2], theirs[2]], "adamw_w_up")
    r_w_down = big(w_down, m_w_down, v_w_down, [mine[3], theirs[3]], "adamw_w_down")

    r_conv_w = big(conv_w, m_conv_w, v_conv_w, [g_conv_w], "adamw_conv_w")

    def pick(k):
        ga_, rpb_, sk_, gna_, gsw_, gf_, cb_, gfin_, b_ = [r[k] for r in r_small]
        return [r_w_ada[k], b_, ga_, r_w_in[k], rpb_.reshape(na_rpb.shape), sk_, gna_, gsw_, r_w_out[k], gf_,
                r_w_up[k], r_conv_w[k], cb_, r_w_down[k], gfin_.reshape(d)]

    return (loss, gx.reshape(batch, seq, d), *pick(0), *pick(1), *pick(2), *pick(3))
```

```python
import jax
import jax.numpy as jnp
import numpy as np
from jax import lax
from jax.experimental import pallas as pl
from jax.experimental.pallas import tpu as pltpu

F32 = jnp.float32
BF16 = jnp.bfloat16
MESH = pl.DeviceIdType.MESH

D_MODEL = 1024
HEAD_DIM = 64
NA_WIDTH = 512
SW_WIDTH = 512
SW_KV_WIDTH = 128
IN_WIDTH = 2304
D_FF = 2816
GRID_W = 64
NA_ROWS = 8
NA_COLS = 16
SW_BLOCK = 128
ROPE_THETA = 10000.0
EPS = 1e-6
NEG = -1e30
QK_SCALE = HEAD_DIM ** -0.5

ADAM_LR = 0.001
ADAM_B1 = 0.9
ADAM_B2 = 0.999
ADAM_EPS = 1e-08
ADAM_WD = 0.01
ADAM_STEP = 10

N_SHARD = 4
N_DEV = 8
LANES = 128
SUBLANES = 8
TOKEN_TILE = 512
FF_TILE = 256
CONV_CHUNK = 512
NA_GROUP = 8
SW_GROUP_BLOCKS = 8
VMEM_BIG = 56 * 1024 * 1024


def _mm(a, b):
    return jnp.dot(a, b, preferred_element_type=F32)


def _mm_nt(a, b):
    return lax.dot_general(a, b, (((1,), (1,)), ((), ())), preferred_element_type=F32)


def _mm_tn(a, b):
    return lax.dot_general(a, b, (((0,), (0,)), ((), ())), preferred_element_type=F32)


def _cparams(sem=None, vmem=None):
    kw = {}
    if sem is not None:
        kw["dimension_semantics"] = sem
    if vmem is not None:
        kw["vmem_limit_bytes"] = vmem
    return pltpu.CompilerParams(**kw)


def _resident(shape):
    return pl.BlockSpec(shape, lambda i: (0,) * len(shape), pipeline_mode=pl.Buffered(1))


def _sigmoid(x):
    return 1.0 / (1.0 + jnp.exp(-x))


def _rms_stats(x):
    r = lax.rsqrt(jnp.mean(x * x, axis=-1, keepdims=True) + EPS)
    return r, x * r


def _rms_bwd(dxn, xn, r):
    return r * (dxn - xn * jnp.mean(dxn * xn, axis=-1, keepdims=True))


def _my_pos():
    return lax.axis_index("x"), lax.axis_index("y"), lax.axis_index("c")


def _flip(v, bit):
    return 1 - v if bit else v


def _ada_forward(c8, w_ada, b_ada, rider):
    d = c8.shape[1]
    ncol = w_ada.shape[1]

    def body(c_ref, w_ref, b_ref, mod_ref, sc_ref, m_scr, mod_buf, ssem, rsem, ssem2, rsem2):
        x, y, c = _my_pos()
        me = 4 * x + 2 * y + c
        shard = 2 * x + y
        cv = c_ref[...]
        my_rows = pl.ds(pl.multiple_of(me * SUBLANES, SUBLANES), SUBLANES)
        sc_ref[my_rows, :] = cv * _sigmoid(cv)

        def copy1(k):
            peer = (_flip(x, (k >> 2) & 1), _flip(y, (k >> 1) & 1), _flip(c, k & 1))
            return pltpu.make_async_remote_copy(
                src_ref=sc_ref.at[my_rows, :], dst_ref=sc_ref.at[my_rows, :],
                send_sem=ssem.at[k - 1], recv_sem=rsem.at[k - 1], device_id=peer, device_id_type=MESH)

        sends = [copy1(k) for k in range(1, N_DEV)]
        for cp in sends:
            cp.start()
        for cp in sends:
            cp.wait_recv()
        m_scr[...] = _mm(sc_ref[...].astype(BF16), w_ref[...].astype(BF16))

        def copy2(k):
            px, py = _flip(x, (k >> 1) & 1), _flip(y, k & 1)
            rows = pl.ds(pl.multiple_of((4 * px + 2 * py + c) * SUBLANES, SUBLANES), SUBLANES)
            return pltpu.make_async_remote_copy(
                src_ref=m_scr.at[rows, :], dst_ref=mod_buf.at[shard],
                send_sem=ssem2.at[k - 1], recv_sem=rsem2.at[k - 1], device_id=(px, py, c), device_id_type=MESH)

        sends2 = [copy2(k) for k in range(1, N_SHARD)]
        for cp in sends2:
            cp.start()
        mod_buf[shard] = m_scr[my_rows, :]
        for cp in sends2:
            cp.wait_recv()
        for s in range(N_SHARD):
            mod_ref[:, s * ncol:(s + 1) * ncol] = mod_buf[s] + b_ref[:, s * ncol:(s + 1) * ncol]
        for cp in sends + sends2:
            cp.wait_send()

    vm = pl.BlockSpec(memory_space=pltpu.VMEM)
    return _hosted(
        body, rider, name="ada_forward", grid=(),
        out_shape=(jax.ShapeDtypeStruct((SUBLANES, N_SHARD * ncol), F32),
                   jax.ShapeDtypeStruct((N_DEV * SUBLANES, d), F32)),
        in_specs=[vm, vm, vm], out_specs=(vm, vm),
        scratch_shapes=[pltpu.VMEM((N_DEV * SUBLANES, ncol), F32), pltpu.VMEM((N_SHARD, SUBLANES, ncol), F32),
                        pltpu.SemaphoreType.DMA((N_DEV - 1,)), pltpu.SemaphoreType.DMA((N_DEV - 1,)),
                        pltpu.SemaphoreType.DMA((N_SHARD - 1,)), pltpu.SemaphoreType.DMA((N_SHARD - 1,))],
        compiler_params=_cparams(vmem=VMEM_BIG), args=[c8, w_ada, b_ada])


class _Rider:
    def __init__(self, kind, srcs, owns=()):
        self.kind, self.srcs, self.owns = kind, list(srcs), list(owns)
        n = len(self.srcs)
        sds = jax.ShapeDtypeStruct
        dma = pltpu.SemaphoreType.DMA
        if kind == "gather":
            self.out_shapes = [sds((N_SHARD,) + s.shape, s.dtype) for s in self.srcs]
            self.sems = [dma((n, N_SHARD - 1)), dma((n, N_SHARD - 1)), dma((n, N_SHARD - 1)), dma((n, N_SHARD - 1)),
                         dma((n,)), dma((n,))]
        elif kind == "scatter":
            self.out_shapes = ([sds((N_SHARD - 1,) + s.shape[1:], s.dtype) for s in self.srcs]
                               + [sds(o.shape[1:], o.dtype) for o in self.owns])
            m = max(len(self.owns), 1)
            self.sems = [dma((n, N_SHARD - 1)), dma((n, N_SHARD - 1)), dma((m,)), dma((m,))]
        else:
            self.out_shapes = [sds(s.shape, s.dtype) for s in self.srcs]
            self.sems = [dma((n,)), dma((n,))]

    @property
    def inputs(self):
        return self.srcs + self.owns

    def _halved(self, i):
        a = self.srcs[i]
        tile_rows = SUBLANES * (4 // jnp.dtype(a.dtype).itemsize)
        return self.kind == "gather" and a.shape[0] % (2 * tile_rows) == 0

    def copies(self, ins, outs, sems):
        n = len(self.srcs)
        x, y, c = _my_pos()
        shard = 2 * x + y
        remote, relay = [], []
        if self.kind == "swap":
            ssem, rsem = sems
            for i in range(n):
                remote.append(pltpu.make_async_remote_copy(
                    src_ref=ins[i], dst_ref=outs[i], send_sem=ssem.at[i], recv_sem=rsem.at[i],
                    device_id=(x, y, 1 - c), device_id_type=MESH))
            return remote, relay
        if self.kind == "gather":
            ssem, rsem, ssem2, rsem2, sib_s, sib_r = sems
        else:
            ssem, rsem, sib_s, sib_r = sems
        for i in range(n):
            if self.kind == "gather":
                remote.append(pltpu.make_async_remote_copy(
                    src_ref=ins[i], dst_ref=outs[i].at[shard], send_sem=sib_s.at[i], recv_sem=sib_r.at[i],
                    device_id=(x, y, 1 - c), device_id_type=MESH))
                half = ins[i].shape[0] // 2
                mine = pl.ds(pl.multiple_of(c * half, half), half) if self._halved(i) else None
            for k in range(1, N_SHARD):
                px, py = _flip(x, (k >> 1) & 1), _flip(y, k & 1)
                if self.kind == "gather":
                    src, dst = ins[i], outs[i].at[shard]
                    if mine is not None:
                        src, dst = src.at[mine], dst.at[mine]
                        got = outs[i].at[2 * px + py].at[mine]
                        relay.append(pltpu.make_async_remote_copy(
                            src_ref=got, dst_ref=got, send_sem=ssem2.at[i, k - 1], recv_sem=rsem2.at[i, k - 1],
                            device_id=(x, y, 1 - c), device_id_type=MESH))
                else:
                    src, dst = ins[i].at[2 * px + py], outs[i].at[k - 1]
                remote.append(pltpu.make_async_remote_copy(
                    src_ref=src, dst_ref=dst, send_sem=ssem.at[i, k - 1], recv_sem=rsem.at[i, k - 1],
                    device_id=(px, py, c), device_id_type=MESH))
        if self.kind == "scatter":
            for i in range(len(self.owns)):
                remote.append(pltpu.make_async_remote_copy(
                    src_ref=ins[n + i].at[shard], dst_ref=outs[n + i], send_sem=sib_s.at[i], recv_sem=sib_r.at[i],
                    device_id=(x, y, 1 - c), device_id_type=MESH))
        return remote, relay

    def start(self, ins, outs, sems):
        remote, _ = self.copies(ins, outs, sems)
        for cp in remote:
            cp.start()

    def wait(self, ins, outs, sems):
        remote, relay = self.copies(ins, outs, sems)
        for cp in remote:
            cp.wait_recv()
        for cp in relay:
            cp.start()
        for cp in relay:
            cp.wait_recv()
        for cp in remote + relay:
            cp.wait_send()


class _Riders:
    def __init__(self, riders):
        self.riders = list(riders)
        self.inputs = [a for r in self.riders for a in r.inputs]
        self.out_shapes = [s for r in self.riders for s in r.out_shapes]
        self.sems = [s for r in self.riders for s in r.sems]

    def _split(self, ins, outs, sems):
        for r in self.riders:
            ni, no, ns = len(r.inputs), len(r.out_shapes), len(r.sems)
            yield r, ins[:ni], outs[:no], sems[:ns]
            ins, outs, sems = ins[ni:], outs[no:], sems[ns:]

    def start(self, ins, outs, sems):
        for r, i, o, s in self._split(ins, outs, sems):
            r.start(i, o, s)

    def wait(self, ins, outs, sems):
        for r, i, o, s in self._split(ins, outs, sems):
            r.wait(i, o, s)


def _hosted(body, rider, *, name, grid, out_shape, in_specs, out_specs, scratch_shapes, compiler_params, args):
    out_shape, out_specs = list(out_shape), list(out_specs)
    if rider is None:
        outs = pl.pallas_call(body, name=name, grid=grid, out_shape=tuple(out_shape), in_specs=list(in_specs),
                              out_specs=tuple(out_specs), scratch_shapes=list(scratch_shapes),
                              compiler_params=compiler_params)(*args)
        return list(outs), []
    n_in, n_out, n_scr = len(in_specs), len(out_shape), len(scratch_shapes)
    nr_in, nr_out = len(rider.inputs), len(rider.out_shapes)
    n_steps = 1
    for size in grid:
        n_steps *= size

    def full(*refs):
        ins, refs = refs[:n_in], refs[n_in:]
        r_in, refs = refs[:nr_in], refs[nr_in:]
        outs, refs = refs[:n_out], refs[n_out:]
        r_out, refs = refs[:nr_out], refs[nr_out:]
        scr, sems = refs[:n_scr], refs[n_scr:]
        if grid:
            step = 0
            for ax, size in enumerate(grid):
                step = step * size + pl.program_id(ax)
            pl.when(step == 0)(lambda: rider.start(r_in, r_out, sems))
            body(*ins, *outs, *scr)
            pl.when(step == n_steps - 1)(lambda: rider.wait(r_in, r_out, sems))
        else:
            rider.start(r_in, r_out, sems)
            body(*ins, *outs, *scr)
            rider.wait(r_in, r_out, sems)

    hbm = pl.BlockSpec(memory_space=pl.ANY)
    res = pl.pallas_call(
        full, name=name, grid=grid, out_shape=tuple(out_shape + rider.out_shapes),
        in_specs=list(in_specs) + [hbm] * nr_in, out_specs=tuple(out_specs + [hbm] * nr_out),
        scratch_shapes=list(scratch_shapes) + rider.sems, compiler_params=compiler_params,
    )(*args, *rider.inputs)
    return list(res[:n_out]), list(res[n_out:])


def _ride_alone(rider, name):
    return _hosted(lambda: None, rider, name=name, grid=(), out_shape=[], in_specs=[], out_specs=[], scratch_shapes=[],
                   compiler_params=_cparams(), args=[])[1]


def _rope_rot(t):
    w = t.shape[1]
    lane = lax.broadcasted_iota(jnp.int32, t.shape, 1)
    first = (lane % HEAD_DIM) < (HEAD_DIM // 2)
    return jnp.where(first, pltpu.roll(t, w - HEAD_DIM // 2, 1), pltpu.roll(t, HEAD_DIM // 2, 1))


def _in_proj(x, mod3, g_attn, w_in_t, cos_t, sin_t, seq, rider=None):
    t, d = x.shape
    tm = TOKEN_TILE
    per_seq = seq // tm
    rope_lo, rope_hi = 3 * NA_WIDTH, 3 * NA_WIDTH + SW_WIDTH + SW_KV_WIDTH
    n_rep = (rope_hi - rope_lo) // LANES

    def body(x_ref, mod_ref, g_ref, w_ref, cos_ref, sin_ref, h_ref, p_ref):
        r, xn = _rms_stats(x_ref[...])
        shift, scale = mod_ref[0, :, 0:d], mod_ref[0, :, d:2 * d]
        hb = ((xn * g_ref[...]) * (1.0 + scale) + shift).astype(BF16)
        h_ref[...] = hb
        p_ref[:, :rope_lo] = _mm_nt(hb, w_ref[:rope_lo, :]).astype(BF16)
        pr = _mm_nt(hb, w_ref[rope_lo:rope_hi, :])
        cos = jnp.concatenate([cos_ref[...]] * n_rep, axis=1)
        sin = jnp.concatenate([sin_ref[...]] * n_rep, axis=1)
        p_ref[:, rope_lo:rope_hi] = (pr * cos + _rope_rot(pr) * sin).astype(BF16)
        p_ref[:, rope_hi:] = _mm_nt(hb, w_ref[rope_hi:, :]).astype(BF16)

    return _hosted(
        body, rider, name="in_proj", grid=(t // tm,),
        out_shape=[jax.ShapeDtypeStruct((t, d), BF16), jax.ShapeDtypeStruct((t, IN_WIDTH), BF16)],
        in_specs=[pl.BlockSpec((tm, d), lambda i: (i, 0)),
                  pl.BlockSpec((1, 1, 6 * d), lambda i: (i // per_seq, 0, 0)),
                  pl.BlockSpec((1, d), lambda i: (0, 0)),
                  pl.BlockSpec((IN_WIDTH, d), lambda i: (0, 0)),
                  pl.BlockSpec((tm, LANES), lambda i: (i % per_seq, 0)),
                  pl.BlockSpec((tm, LANES), lambda i: (i % per_seq, 0))],
        out_specs=[pl.BlockSpec((tm, d), lambda i: (i, 0)), pl.BlockSpec((tm, IN_WIDTH), lambda i: (i, 0))],
        scratch_shapes=[], compiler_params=_cparams(("arbitrary",), VMEM_BIG),
        args=[x, mod3, g_attn, w_in_t, cos_t, sin_t])


def _na_bias_pattern():
    n_dc = 2 * NA_COLS - 1
    j = np.arange(GRID_W)[:, None]
    m = np.arange(GRID_W * LANES)[None, :]
    q, lane = m // LANES, m % LANES
    k = lane % GRID_W
    cs = np.clip(q - NA_COLS // 2, 0, GRID_W - NA_COLS)
    ok = (k >= cs) & (k < cs + NA_COLS)
    hit = ok & (j < 2 * n_dc) & (lane // GRID_W == j // n_dc) & (k - q + (NA_COLS - 1) == j % n_dc)
    return jnp.asarray(hit.astype(np.float32)), jnp.asarray(np.where(ok, 0.0, NEG).astype(np.float32))


def _na_bias_tiles(rows2, expand, mask):
    n, width = rows2.shape[0], expand.shape[1]
    q_step = 16
    step = q_step * LANES

    def body(r_ref, e_ref, m_ref, o_ref):
        flat = jnp.dot(r_ref[...], e_ref[...], precision=lax.Precision.HIGHEST,
                       preferred_element_type=F32) + m_ref[...]
        for qq in range(q_step):
            o_ref[:, qq, :] = flat[:, qq * LANES:(qq + 1) * LANES]

    return pl.pallas_call(
        body, name="na_bias_tiles", grid=(width // step,),
        out_shape=jax.ShapeDtypeStruct((n, GRID_W, LANES), F32),
        in_specs=[pl.BlockSpec(rows2.shape, lambda i: (0, 0)), pl.BlockSpec((expand.shape[0], step), lambda i: (0, i)),
                  pl.BlockSpec((1, step), lambda i: (0, i))],
        out_specs=pl.BlockSpec((n, q_step, LANES), lambda i: (0, i, 0)),
        compiler_params=_cparams(("arbitrary",)),
    )(rows2, expand, mask)


def _na_prepare(k_ref, v_ref, km, vm):
    lane = lax.broadcasted_iota(jnp.int32, k_ref.shape, 1)
    low = lane < HEAD_DIM
    kv = k_ref[...]
    vv = v_ref[...]
    zero = jnp.zeros_like(kv)
    km[0] = jnp.where(low, kv, zero)
    km[1] = jnp.where(low, zero, kv)
    vm[0] = jnp.where(low, vv, zero)
    vm[1] = jnp.where(low, zero, vv)


def _na_window(r, n_rows):
    rs = jnp.clip(r - NA_ROWS // 2, 0, n_rows - NA_ROWS)
    return rs, r - rs


def _na_pair_window(ref, wrows):
    return jnp.concatenate([ref[0, wrows, :], ref[1, wrows, :]], axis=0)


def _na_scores(q, k2, tp_ref, off):
    bias = jnp.concatenate([tp_ref[h, 2 * w - off + (NA_ROWS - 1)] for h in range(2) for w in range(NA_ROWS // 2)],
                           axis=1)
    return _mm_nt(q, k2) * QK_SCALE + bias


def _pair_lse_block(lse):
    lane = lax.broadcasted_iota(jnp.int32, (lse[0].shape[0], LANES), 1)
    return jnp.where(lane < HEAD_DIM, lse[0], lse[1])


def _pair_softmax(s):
    win = s.shape[1] // 2
    halves, lse = [], []
    for h in range(2):
        sh = s[:, h * win:(h + 1) * win]
        m = jnp.max(sh, axis=-1, keepdims=True)
        e = jnp.exp(sh - m)
        l = jnp.sum(e, axis=-1, keepdims=True)
        halves.append(e / l)
        lse.append(m + jnp.log(l))
    return jnp.concatenate(halves, axis=1), _pair_lse_block(lse)


def _pair_probs_from_lse(s, lse_block):
    win = s.shape[1] // 2
    return jnp.concatenate([jnp.exp(s[:, h * win:(h + 1) * win] - lse_block[:, h * HEAD_DIM:h * HEAD_DIM + 1])
                            for h in range(2)], axis=1)


def _na_forward(proj, tiles, batch, seq, rider=None):
    t = proj.shape[0]
    n_rows = seq // GRID_W
    n_pairs = NA_WIDTH // LANES
    win = NA_ROWS * GRID_W

    def body(q_ref, k_ref, v_ref, tp_ref, o_ref, lse_ref, km, vm):
        _na_prepare(k_ref, v_ref, km, vm)

        def scores(r):
            rs, off = _na_window(r, n_rows)
            rows = pl.ds(pl.multiple_of(r * GRID_W, GRID_W), GRID_W)
            wrows = pl.ds(pl.multiple_of(rs * GRID_W, GRID_W), win)
            return rows, wrows, _na_scores(q_ref[rows, :], _na_pair_window(km, wrows), tp_ref, off)

        def finish(rows, wrows, s):
            p, lse = _pair_softmax(s)
            lse_ref[rows, :] = lse
            o_ref[rows, :] = _mm(p.astype(BF16), _na_pair_window(vm, wrows))

        def row_group(i, carry):
            for state in [scores(NA_GROUP * i + j) for j in range(NA_GROUP)]:
                finish(*state)
            return carry

        lax.fori_loop(0, n_rows // NA_GROUP, row_group, 0)

    return _hosted(
        body, rider, name="na_forward", grid=(batch, n_pairs),
        out_shape=[jax.ShapeDtypeStruct((t, NA_WIDTH), F32), jax.ShapeDtypeStruct((t, NA_WIDTH), F32)],
        in_specs=[pl.BlockSpec((seq, LANES), lambda b, p: (b, p)),
                  pl.BlockSpec((seq, LANES), lambda b, p: (b, n_pairs + p)),
                  pl.BlockSpec((seq, LANES), lambda b, p: (b, 2 * n_pairs + p)),
                  pl.BlockSpec((2, 2 * NA_ROWS - 2, GRID_W, LANES), lambda b, p: (p, 0, 0, 0))],
        out_specs=[pl.BlockSpec((seq, LANES), lambda b, p: (b, p)), pl.BlockSpec((seq, LANES), lambda b, p: (b, p))],
        scratch_shapes=[pltpu.VMEM((2, seq, LANES), BF16), pltpu.VMEM((2, seq, LANES), BF16)],
        compiler_params=_cparams(("arbitrary", "arbitrary")), args=[proj, proj, proj, tiles])


def _sw_prepare(kv_ref, g, dst_lo, dst_hi, seq):
    lane = lax.broadcasted_iota(jnp.int32, kv_ref.shape, 1)
    mine = (lane // HEAD_DIM) == g
    kg = jnp.where(mine, kv_ref[...].astype(F32), 0.0)
    kr = pltpu.roll(kg, HEAD_DIM, 1)
    first = g == 0
    zero = jnp.zeros((SW_BLOCK, LANES), BF16)
    for dst, val in ((dst_lo, jnp.where(first, kg, kr)), (dst_hi, jnp.where(first, kr, kg))):
        dst[0:SW_BLOCK, :] = zero
        dst[SW_BLOCK:SW_BLOCK + seq, :] = val.astype(BF16)
        dst[SW_BLOCK + seq:, :] = zero


def _sw_mask(n, seq):
    qi = lax.broadcasted_iota(jnp.int32, (SW_BLOCK, 3 * SW_BLOCK), 0)
    kj = lax.broadcasted_iota(jnp.int32, (SW_BLOCK, 3 * SW_BLOCK), 1)
    kpos = n * SW_BLOCK - SW_BLOCK + kj
    return (jnp.abs(qi + SW_BLOCK - kj) <= SW_BLOCK) & (kpos >= 0) & (kpos < seq)


def _sw_probs(s2, ok, sinks):
    band = s2.shape[1] // 2
    halves, lse = [], []
    for i in range(2):
        s = jnp.where(ok, s2[:, i * band:(i + 1) * band], NEG)
        m = jnp.maximum(jnp.max(s, axis=-1, keepdims=True), sinks[i])
        p = jnp.exp(s - m)
        den = jnp.sum(p, axis=-1, keepdims=True) + jnp.exp(sinks[i] - m)
        halves.append(p / den)
        lse.append(m + jnp.log(den))
    return jnp.concatenate(halves, axis=1), _pair_lse_block(lse)


def _sw_probs_from_lse(s2, ok, sinks, lse_block):
    band = s2.shape[1] // 2
    halves, sink_p = [], []
    for i in range(2):
        lse = lse_block[:, i * HEAD_DIM:i * HEAD_DIM + 1]
        halves.append(jnp.exp(jnp.where(ok, s2[:, i * band:(i + 1) * band], NEG) - lse))
        sink_p.append(jnp.exp(sinks[i] - lse))
    return jnp.concatenate(halves, axis=1), sink_p


def _sw_forward(proj, sink, batch, seq, rider=None):
    t = proj.shape[0]
    n_pairs = SW_WIDTH // LANES
    q_blk = 3 * NA_WIDTH // LANES
    k_blk = q_blk + n_pairs
    n_blocks = seq // SW_BLOCK
    pad = seq + 2 * SW_BLOCK

    def body(sink_ref, q_ref, k_ref, v_ref, o_ref, lse_ref, k_lo, k_hi, v_lo, v_hi):
        hp = pl.program_id(1)
        g = hp // 2
        _sw_prepare(k_ref, g, k_lo, k_hi, seq)
        _sw_prepare(v_ref, g, v_lo, v_hi, seq)

        sinks = (sink_ref[2 * hp], sink_ref[2 * hp + 1])

        def scores(n):
            rows = pl.ds(pl.multiple_of(n * SW_BLOCK, SW_BLOCK), SW_BLOCK)
            wrows = pl.ds(pl.multiple_of(n * SW_BLOCK, SW_BLOCK), 3 * SW_BLOCK)
            k2 = jnp.concatenate([k_lo[wrows, :], k_hi[wrows, :]], axis=0)
            return n, rows, wrows, _mm_nt(q_ref[rows, :], k2) * QK_SCALE

        def finish(n, rows, wrows, s2):
            p, lse = _sw_probs(s2, _sw_mask(n, seq), sinks)
            lse_ref[rows, :] = lse
            v2 = jnp.concatenate([v_lo[wrows, :], v_hi[wrows, :]], axis=0)
            o_ref[rows, :] = _mm(p.astype(BF16), v2)

        def block_group(i, carry):
            for state in [scores(SW_GROUP_BLOCKS * i + j) for j in range(SW_GROUP_BLOCKS)]:
                finish(*state)
            return carry

        lax.fori_loop(0, n_blocks // SW_GROUP_BLOCKS, block_group, 0)

    return _hosted(
        body, rider, name="sw_forward", grid=(batch, n_pairs),
        out_shape=[jax.ShapeDtypeStruct((t, SW_WIDTH), F32), jax.ShapeDtypeStruct((t, SW_WIDTH), F32)],
        in_specs=[pl.BlockSpec(memory_space=pltpu.SMEM),
                  pl.BlockSpec((seq, LANES), lambda b, p: (b, q_blk + p)),
                  pl.BlockSpec((seq, LANES), lambda b, p: (b, k_blk)),
                  pl.BlockSpec((seq, LANES), lambda b, p: (b, k_blk + 1))],
        out_specs=[pl.BlockSpec((seq, LANES), lambda b, p: (b, p)), pl.BlockSpec((seq, LANES), lambda b, p: (b, p))],
        scratch_shapes=[pltpu.VMEM((pad, LANES), BF16)] * 4,
        compiler_params=_cparams(("arbitrary", "arbitrary")), args=[sink, proj, proj, proj])


def _out_proj(oa, ob, g_na, g_sw, w_out, x, mod3, g_ffn, seq):
    t, d = x.shape
    tm = TOKEN_TILE
    per_seq = seq // tm

    def body(oa_ref, ob_ref, gna_ref, gsw_ref, w_ref, x_ref, mod_ref, gf_ref, oab_ref, mix_ref, x1_ref, h2_ref):
        _, na = _rms_stats(oa_ref[...])
        _, nb = _rms_stats(ob_ref[...])
        oab = jnp.concatenate([na * gna_ref[...], nb * gsw_ref[...]], axis=1).astype(BF16)
        oab_ref[...] = oab
        mix = _mm(oab, w_ref[...])
        mix_ref[...] = mix
        gate_a = mod_ref[0, :, 2 * d:3 * d]
        shift_f, scale_f = mod_ref[0, :, 3 * d:4 * d], mod_ref[0, :, 4 * d:5 * d]
        x1 = x_ref[...] + gate_a * mix
        x1_ref[...] = x1
        _, xn = _rms_stats(x1)
        h2_ref[...] = ((xn * gf_ref[...]) * (1.0 + scale_f) + shift_f).astype(BF16)

    tile = lambda w: pl.BlockSpec((tm, w), lambda i: (i, 0))
    vec = lambda w: pl.BlockSpec((1, w), lambda i: (0, 0))
    return pl.pallas_call(
        body, name="out_proj", grid=(t // tm,),
        out_shape=(jax.ShapeDtypeStruct((t, d), BF16), jax.ShapeDtypeStruct((t, d), F32),
                   jax.ShapeDtypeStruct((t, d), F32), jax.ShapeDtypeStruct((t, d), BF16)),
        in_specs=[tile(NA_WIDTH), tile(SW_WIDTH), vec(NA_WIDTH), vec(SW_WIDTH),
                  pl.BlockSpec((d, d), lambda i: (0, 0)), tile(d),
                  pl.BlockSpec((1, 1, 6 * d), lambda i: (i // per_seq, 0, 0)), vec(d)],
        out_specs=(tile(d), tile(d), tile(d), tile(d)),
        compiler_params=_cparams(("arbitrary",), VMEM_BIG),
    )(oa, ob, g_na, g_sw, w_out, x, mod3, g_ffn)


def _up_proj(h2, w_up_halves, rider=None):
    t, d = h2.shape
    tm = TOKEN_TILE
    w_a, w_b = w_up_halves
    half, wcol = w_a.shape[1], w_a.shape[2]

    def body(h_ref, wa_ref, wb_ref, u_ref):
        u_ref[0] = (_mm(h_ref[:, :half], wa_ref[0]) + _mm(h_ref[:, half:], wb_ref[0])).astype(BF16)

    w_spec = pl.BlockSpec((1, half, wcol), lambda j, i: (j, 0, 0))
    return _hosted(
        body, rider, name="up_proj", grid=(N_SHARD, t // tm),
        out_shape=[jax.ShapeDtypeStruct((2, t, D_FF), BF16)],
        in_specs=[pl.BlockSpec((tm, d), lambda j, i: (i, 0)), w_spec, w_spec],
        out_specs=[pl.BlockSpec((1, tm, wcol), lambda j, i: (j // 2, i, j % 2))],
        scratch_shapes=[], compiler_params=_cparams(("arbitrary", "arbitrary"), VMEM_BIG), args=[h2, w_a, w_b])


def _taps_chunk(load, s, rows, seq):
    halo = 2 * SUBLANES
    cur = load(s, rows)
    above = load(pl.multiple_of(jnp.maximum(s - halo, 0), halo), halo)
    below = load(pl.multiple_of(jnp.minimum(s + rows, seq - halo), halo), halo)
    up = jnp.where(s > 0, above[halo - 1:halo, :], 0.0)
    dn = jnp.where(s + rows < seq, below[0:1, :], 0.0)
    row = lax.broadcasted_iota(jnp.int32, cur.shape, 0)
    prev = jnp.where(row == 0, up, pltpu.roll(cur, 1, 0))
    nxt = jnp.where(row == rows - 1, dn, pltpu.roll(cur, rows - 1, 0))
    return cur, prev, nxt


def _conv_gate(u, conv_w, conv_b, batch, seq):
    t = u.shape[1]
    cw = FF_TILE
    rows = CONV_CHUNK

    def body(u_ref, w_ref, b_ref, a_ref):
        def chunk(i, carry):
            s = pl.multiple_of(i * rows, rows)
            gt, prev, nxt = _taps_chunk(lambda at, n: u_ref[1, pl.ds(at, n), :].astype(F32), s, rows, seq)
            gc = prev * w_ref[0:1, :] + gt * w_ref[1:2, :] + nxt * w_ref[2:3, :] + b_ref[...]
            a_ref[pl.ds(s, rows), :] = ((gc * _sigmoid(gc)) * u_ref[0, pl.ds(s, rows), :].astype(F32)).astype(BF16)
            return carry

        lax.fori_loop(0, seq // rows, chunk, 0)

    return pl.pallas_call(
        body, name="conv_gate", grid=(batch, D_FF // cw),
        out_shape=jax.ShapeDtypeStruct((t, D_FF), BF16),
        in_specs=[pl.BlockSpec((2, seq, cw), lambda b, j: (0, b, j)),
                  pl.BlockSpec((3, cw), lambda b, j: (0, j)), pl.BlockSpec((1, cw), lambda b, j: (0, j))],
        out_specs=pl.BlockSpec((seq, cw), lambda b, j: (b, j)),
        compiler_params=_cparams(("arbitrary", "arbitrary"), VMEM_BIG),
    )(u, conv_w, conv_b)


def _down_and_loss(a, w_down, x1, mod3, g_final, target, seq):
    t, d = x1.shape
    tm = TOKEN_TILE
    per_seq = seq // tm
    batch = t // seq

    def body(a_ref, w_ref, x1_ref, mod_ref, g_ref, tgt_ref, dx2_ref, dffn_ref, loss_ref, dgate_ref, dg_ref):
        i = pl.program_id(0)
        f = _mm(a_ref[...], w_ref[...])
        gate_f = mod_ref[0, :, 5 * d:6 * d]
        x2 = x1_ref[...] + gate_f * f
        r, xn = _rms_stats(x2)
        err = xn * g_ref[...] - tgt_ref[...]
        part = 0.5 * jnp.sum(jnp.mean(err * err, axis=-1, keepdims=True))
        dy = err / d
        dx2 = _rms_bwd(dy * g_ref[...], xn, r)
        dx2_ref[...] = dx2
        dffn_ref[...] = (dx2 * gate_f).astype(BF16)

        @pl.when(i == 0)
        def _():
            loss_ref[...] = jnp.zeros_like(loss_ref)
            dg_ref[...] = jnp.zeros_like(dg_ref)

        @pl.when(i % per_seq == 0)
        def _():
            dgate_ref[...] = jnp.zeros_like(dgate_ref)

        loss_ref[...] += part
        dg_ref[...] += jnp.sum(dy * xn, axis=0, keepdims=True)
        dgate_ref[0] += jnp.sum(dx2 * f, axis=0, keepdims=True)

    tile = lambda w: pl.BlockSpec((tm, w), lambda i: (i, 0))
    return pl.pallas_call(
        body, name="down_loss", grid=(t // tm,),
        out_shape=(jax.ShapeDtypeStruct((t, d), F32), jax.ShapeDtypeStruct((t, d), BF16),
                   jax.ShapeDtypeStruct((SUBLANES, LANES), F32), jax.ShapeDtypeStruct((batch, 1, d), F32),
                   jax.ShapeDtypeStruct((1, d), F32)),
        in_specs=[tile(D_FF), _resident((D_FF, d)), tile(d),
                  pl.BlockSpec((1, 1, 6 * d), lambda i: (i // per_seq, 0, 0)),
                  pl.BlockSpec((1, d), lambda i: (0, 0)), tile(d)],
        out_specs=(tile(d), tile(d), pl.BlockSpec((SUBLANES, LANES), lambda i: (0, 0)),
                   pl.BlockSpec((1, 1, d), lambda i: (i // per_seq, 0, 0)), pl.BlockSpec((1, d), lambda i: (0, 0))),
        compiler_params=_cparams(("arbitrary",), VMEM_BIG),
    )(a, w_down, x1, mod3, g_final, target)


def _down_weight_grad(a, dffn):
    t, dff = a.shape
    d = dffn.shape[1]
    tk = 2 * TOKEN_TILE
    n_k = t // tk

    def body(a_ref, df_ref, g_ref, gb_ref):
        k = pl.program_id(0)

        @pl.when(k == 0)
        def _():
            g_ref[...] = jnp.zeros_like(g_ref)

        g_ref[...] += _mm_tn(a_ref[...], df_ref[...])

        @pl.when(k == n_k - 1)
        def _():
            gb_ref[...] = g_ref[...].astype(BF16)

    whole = _resident((dff, d))
    return pl.pallas_call(
        body, name="down_weight_grad", grid=(n_k,),
        out_shape=(jax.ShapeDtypeStruct((dff, d), F32), jax.ShapeDtypeStruct((dff, d), BF16)),
        in_specs=[pl.BlockSpec((tk, dff), lambda k: (k, 0)), pl.BlockSpec((tk, d), lambda k: (k, 0))],
        out_specs=(whole, whole),
        compiler_params=_cparams(("arbitrary",), VMEM_BIG),
    )(a, dffn)


def _ffn_backward(dffn, w_down, u, conv_w, conv_b, batch, seq, rider=None):
    t, d = dffn.shape
    cw = FF_TILE
    rows = CONV_CHUNK

    def body(df_ref, wd_ref, u_ref, w_ref, b_ref, du_ref, gcw_ref, gcb_ref, da_scr, dgc_scr):
        b = pl.program_id(1)
        da_scr[...] = _mm_nt(df_ref[...], wd_ref[...])

        @pl.when(b == 0)
        def _():
            gcw_ref[...] = jnp.zeros_like(gcw_ref)
            gcb_ref[...] = jnp.zeros_like(gcb_ref)

        def fold(v):
            return jnp.sum(v.reshape(rows // SUBLANES, SUBLANES, cw), axis=0)

        def chunk(i, carry):
            s = pl.multiple_of(i * rows, rows)
            here = pl.ds(s, rows)
            gt, prev, nxt = _taps_chunk(lambda at, n: u_ref[1, pl.ds(at, n), :].astype(F32), s, rows, seq)
            val, da = u_ref[0, here, :].astype(F32), da_scr[here, :]
            gc = prev * w_ref[0:1, :] + gt * w_ref[1:2, :] + nxt * w_ref[2:3, :] + b_ref[...]
            sg = _sigmoid(gc)
            sl = gc * sg
            du_ref[0, here, :] = (da * sl).astype(BF16)
            dgc = (da * val) * (sg * (1.0 + gc * (1.0 - sg)))
            dgc_scr[here, :] = dgc
            cb, c0, c1, c2 = carry
            return cb + fold(dgc), c0 + fold(dgc * prev), c1 + fold(dgc * gt), c2 + fold(dgc * nxt)

        zero = jnp.zeros((SUBLANES, cw), F32)
        cb, c0, c1, c2 = lax.fori_loop(0, seq // rows, chunk, (zero, zero, zero, zero))
        gcb_ref[...] += jnp.sum(cb, axis=0, keepdims=True)
        gcw_ref[0:1, :] += jnp.sum(c0, axis=0, keepdims=True)
        gcw_ref[1:2, :] += jnp.sum(c1, axis=0, keepdims=True)
        gcw_ref[2:3, :] += jnp.sum(c2, axis=0, keepdims=True)

        def chunk2(i, carry):
            s = pl.multiple_of(i * rows, rows)
            dgc, dprev, dnxt = _taps_chunk(lambda at, n: dgc_scr[pl.ds(at, n), :], s, rows, seq)
            du_ref[1, pl.ds(s, rows), :] = (dnxt * w_ref[0:1, :] + dgc * w_ref[1:2, :]
                                            + dprev * w_ref[2:3, :]).astype(BF16)
            return carry

        lax.fori_loop(0, seq // rows, chunk2, 0)

    return _hosted(
        body, rider, name="ffn_backward", grid=(D_FF // cw, batch),
        out_shape=[jax.ShapeDtypeStruct((2, t, D_FF), BF16),
                   jax.ShapeDtypeStruct((3, D_FF), F32), jax.ShapeDtypeStruct((1, D_FF), F32)],
        in_specs=[pl.BlockSpec((seq, d), lambda j, b: (b, 0)), pl.BlockSpec((cw, d), lambda j, b: (j, 0)),
                  pl.BlockSpec((2, seq, cw), lambda j, b: (0, b, j)),
                  pl.BlockSpec((3, cw), lambda j, b: (0, j)), pl.BlockSpec((1, cw), lambda j, b: (0, j))],
        out_specs=[pl.BlockSpec((2, seq, cw), lambda j, b: (0, b, j)),
                   pl.BlockSpec((3, cw), lambda j, b: (0, j)), pl.BlockSpec((1, cw), lambda j, b: (0, j))],
        scratch_shapes=[pltpu.VMEM((seq, cw), F32), pltpu.VMEM((seq, cw), F32)],
        compiler_params=_cparams(("arbitrary", "arbitrary"), VMEM_BIG), args=[dffn, w_down, u, conv_w, conv_b])


def _up_backward(du, w_up, x1, mod3, g_ffn, dx2, mix, seq, rider=None):
    _, t, _ = du.shape
    d = x1.shape[1]
    tm = TOKEN_TILE
    per_seq = seq // tm
    batch = t // seq
    w_a, w_b = w_up
    half, wcol = w_a.shape[1], w_a.shape[2]

    def body(du_ref, wa_ref, wb_ref, x1_ref, mod_ref, g_ref, dx2_ref, mix_ref,
             dx1_ref, dmix_ref, dsh_ref, dsc_ref, dga_ref, dg_ref):
        i = pl.program_id(0)
        parts = []
        for w_ref in (wa_ref, wb_ref):
            acc = jnp.zeros((tm, half), F32)
            for j in range(N_SHARD):
                acc = acc + _mm_nt(du_ref[j // 2, :, (j % 2) * wcol:(j % 2 + 1) * wcol], w_ref[j])
            parts.append(acc)
        dh = jnp.concatenate(parts, axis=1)
        gate_a = mod_ref[0, :, 2 * d:3 * d]
        scale_f = mod_ref[0, :, 4 * d:5 * d]
        r, xn = _rms_stats(x1_ref[...])
        xg = xn * g_ref[...]
        dxg = dh * (1.0 + scale_f)
        dx1 = dx2_ref[...] + _rms_bwd(dxg * g_ref[...], xn, r)
        dx1_ref[...] = dx1
        dmix_ref[...] = (dx1 * gate_a).astype(BF16)

        @pl.when(i == 0)
        def _():
            dg_ref[...] = jnp.zeros_like(dg_ref)

        @pl.when(i % per_seq == 0)
        def _():
            dsh_ref[...] = jnp.zeros_like(dsh_ref)
            dsc_ref[...] = jnp.zeros_like(dsc_ref)
            dga_ref[...] = jnp.zeros_like(dga_ref)

        dg_ref[...] += jnp.sum(dxg * xn, axis=0, keepdims=True)
        dsh_ref[0] += jnp.sum(dh, axis=0, keepdims=True)
        dsc_ref[0] += jnp.sum(dh * xg, axis=0, keepdims=True)
        dga_ref[0] += jnp.sum(dx1 * mix_ref[...], axis=0, keepdims=True)

    tile = lambda w: pl.BlockSpec((tm, w), lambda i: (i, 0))
    per_b = pl.BlockSpec((1, 1, d), lambda i: (i // per_seq, 0, 0))
    small = jax.ShapeDtypeStruct((batch, 1, d), F32)
    return _hosted(
        body, rider, name="up_backward", grid=(t // tm,),
        out_shape=[jax.ShapeDtypeStruct((t, d), F32), jax.ShapeDtypeStruct((t, d), BF16), small, small, small,
                   jax.ShapeDtypeStruct((1, d), F32)],
        in_specs=[pl.BlockSpec((2, tm, D_FF), lambda i: (0, i, 0)),
                  _resident((N_SHARD, half, wcol)), _resident((N_SHARD, half, wcol)), tile(d),
                  pl.BlockSpec((1, 1, 6 * d), lambda i: (i // per_seq, 0, 0)),
                  pl.BlockSpec((1, d), lambda i: (0, 0)), tile(d), tile(d)],
        out_specs=[tile(d), tile(d), per_b, per_b, per_b, pl.BlockSpec((1, d), lambda i: (0, 0))],
        scratch_shapes=[], compiler_params=_cparams(("arbitrary",), VMEM_BIG),
        args=[du, w_a, w_b, x1, mod3, g_ffn, dx2, mix])


def _up_weight_grad(h2, du, rider=None):
    t, d = h2.shape
    tk = 2 * TOKEN_TILE
    wcol = D_FF // 2
    half = d // 2
    n_k = t // tk

    def body(h_ref, du_ref, ga_ref, gb_ref, ga16_ref, gb16_ref):
        k = pl.program_id(1)

        @pl.when(k == 0)
        def _():
            ga_ref[...] = jnp.zeros_like(ga_ref)
            gb_ref[...] = jnp.zeros_like(gb_ref)

        du = du_ref[0]
        ga_ref[0] += _mm_tn(h_ref[:, :half], du)
        gb_ref[0] += _mm_tn(h_ref[:, half:], du)

        @pl.when(k == n_k - 1)
        def _():
            ga16_ref[...] = ga_ref[...].astype(BF16)
            gb16_ref[...] = gb_ref[...].astype(BF16)

    g_spec = pl.BlockSpec((1, half, wcol), lambda j, k: (j, 0, 0))
    f32_out = jax.ShapeDtypeStruct((N_SHARD, half, wcol), F32)
    b16_out = jax.ShapeDtypeStruct((N_SHARD, half, wcol), BF16)
    return _hosted(
        body, rider, name="up_weight_grad", grid=(N_SHARD, n_k),
        out_shape=[f32_out, f32_out, b16_out, b16_out],
        in_specs=[pl.BlockSpec((tk, d), lambda j, k: (k, 0)),
                  pl.BlockSpec((1, tk, wcol), lambda j, k: (j // 2, k, j % 2))],
        out_specs=[g_spec, g_spec, g_spec, g_spec], scratch_shapes=[],
        compiler_params=_cparams(("arbitrary", "arbitrary"), VMEM_BIG), args=[h2, du])


def _out_backward(dmix, w_out, oab, oa, ob, g_na, g_sw):
    t, d = dmix.shape
    tm = 2 * TOKEN_TILE
    hw = NA_WIDTH

    def body(dm_ref, w_ref, oab_ref, oa_ref, ob_ref, gna_ref, gsw_ref,
             doa_ref, dob_ref, gw_ref, gwb_ref, dgna_ref, dgsw_ref):
        @pl.when(pl.program_id(0) == 0)
        def _():
            gw_ref[...] = jnp.zeros_like(gw_ref)
            dgna_ref[...] = jnp.zeros_like(dgna_ref)
            dgsw_ref[...] = jnp.zeros_like(dgsw_ref)

        dm = dm_ref[...]
        gw_ref[...] += _mm_tn(oab_ref[...], dm)

        @pl.when(pl.program_id(0) == t // tm - 1)
        def _():
            gwb_ref[...] = gw_ref[...].astype(BF16)

        do = _mm_nt(dm, w_ref[...])
        for raw_ref, g_ref, dst_ref, dg_ref, lo in ((oa_ref, gna_ref, doa_ref, dgna_ref, 0),
                                                     (ob_ref, gsw_ref, dob_ref, dgsw_ref, hw)):
            r, xn = _rms_stats(raw_ref[...])
            dpart = do[:, lo:lo + hw]
            dg_ref[...] += jnp.sum(dpart * xn, axis=0, keepdims=True)
            dst_ref[...] = _rms_bwd(dpart * g_ref[...], xn, r).astype(BF16)

    tile = lambda w: pl.BlockSpec((tm, w), lambda i: (i, 0))
    vec = lambda w: pl.BlockSpec((1, w), lambda i: (0, 0))
    return pl.pallas_call(
        body, name="out_backward", grid=(t // tm,),
        out_shape=(jax.ShapeDtypeStruct((t, hw), BF16), jax.ShapeDtypeStruct((t, hw), BF16),
                   jax.ShapeDtypeStruct((d, d), F32), jax.ShapeDtypeStruct((d, d), BF16),
                   jax.ShapeDtypeStruct((1, hw), F32), jax.ShapeDtypeStruct((1, hw), F32)),
        in_specs=[tile(d), pl.BlockSpec((d, d), lambda i: (0, 0)), tile(d), tile(hw), tile(hw), vec(hw), vec(hw)],
        out_specs=(tile(hw), tile(hw), pl.BlockSpec((d, d), lambda i: (0, 0)), pl.BlockSpec((d, d), lambda i: (0, 0)),
                   vec(hw), vec(hw)),
        compiler_params=_cparams(("arbitrary",), VMEM_BIG),
    )(dmix, w_out, oab, oa, ob, g_na, g_sw)


def _na_backward(proj, d_o, lse, tiles, batch, seq, rider=None):
    t = proj.shape[0]
    n_rows = seq // GRID_W
    n_pairs = NA_WIDTH // LANES
    win = NA_ROWS * GRID_W
    n_tiles = 2 * NA_ROWS - 2

    def body(q_ref, k_ref, v_ref, do_ref, lse_ref, tp_ref, dq_ref, dk_ref, dv_ref, dtp_ref, km, vm, dk_acc, dv_acc):
        @pl.when(pl.program_id(1) == 0)
        def _():
            dtp_ref[...] = jnp.zeros_like(dtp_ref)

        _na_prepare(k_ref, v_ref, km, vm)
        dk_acc[...] = jnp.zeros_like(dk_acc)
        dv_acc[...] = jnp.zeros_like(dv_acc)
        low = lax.broadcasted_iota(jnp.int32, (win, LANES), 1) < HEAD_DIM

        def scores(r):
            rs, off = _na_window(r, n_rows)
            rows = pl.ds(pl.multiple_of(r * GRID_W, GRID_W), GRID_W)
            wrows = pl.ds(pl.multiple_of(rs * GRID_W, GRID_W), win)
            q, do = q_ref[rows, :], do_ref[rows, :]
            k2 = _na_pair_window(km, wrows)
            s = _na_scores(q, k2, tp_ref, off)
            dp = _mm_nt(do, _na_pair_window(vm, wrows))
            return rows, wrows, off, q, do, k2, s, dp

        def finish(rows, wrows, off, q, do, k2, s, dp):
            p = _pair_probs_from_lse(s, lse_ref[rows, :])
            parts = []
            for h in range(2):
                ph, dph = p[:, h * win:(h + 1) * win], dp[:, h * win:(h + 1) * win]
                dsh = ph * (dph - jnp.sum(ph * dph, axis=-1, keepdims=True))
                for w in range(NA_ROWS // 2):
                    dtp_ref[h, 2 * w - off + (NA_ROWS - 1)] += dsh[:, w * LANES:(w + 1) * LANES]
                parts.append(dsh)
            dsb = (jnp.concatenate(parts, axis=1) * QK_SCALE).astype(BF16)
            dq_ref[rows, :] = _mm(dsb, k2).astype(BF16)
            dk2 = _mm_tn(dsb, q)
            dv2 = _mm_tn(p.astype(BF16), do)
            dk_acc[wrows, :] += jnp.where(low, dk2[:win], dk2[win:])
            dv_acc[wrows, :] += jnp.where(low, dv2[:win], dv2[win:])

        def row_group(i, carry):
            for state in [scores(NA_GROUP * i + j) for j in range(NA_GROUP)]:
                finish(*state)
            return carry

        lax.fori_loop(0, n_rows // NA_GROUP, row_group, 0)
        dk_ref[...] = dk_acc[...].astype(BF16)
        dv_ref[...] = dv_acc[...].astype(BF16)

    blk = lambda off: pl.BlockSpec((seq, LANES), lambda p, b: (b, off + p))
    out = jax.ShapeDtypeStruct((t, NA_WIDTH), BF16)
    return _hosted(
        body, rider, name="na_backward", grid=(n_pairs, batch),
        out_shape=[out, out, out, jax.ShapeDtypeStruct(tiles.shape, F32)],
        in_specs=[blk(0), blk(n_pairs), blk(2 * n_pairs), blk(0), blk(0),
                  pl.BlockSpec((2, n_tiles, GRID_W, LANES), lambda p, b: (p, 0, 0, 0))],
        out_specs=[blk(0), blk(0), blk(0), pl.BlockSpec((2, n_tiles, GRID_W, LANES), lambda p, b: (p, 0, 0, 0))],
        scratch_shapes=[pltpu.VMEM((2, seq, LANES), BF16), pltpu.VMEM((2, seq, LANES), BF16),
                        pltpu.VMEM((seq, LANES), F32), pltpu.VMEM((seq, LANES), F32)],
        compiler_params=_cparams(("arbitrary", "arbitrary")), args=[proj, proj, proj, d_o, lse, tiles])


def _na_bias_grad(dtiles, expand):
    n = dtiles.shape[0]

    def body(t_ref, e_ref, o_ref):
        flat = jnp.concatenate([t_ref[:, qq, :] for qq in range(GRID_W)], axis=1)
        o_ref[...] = lax.dot_general(flat, e_ref[...], (((1,), (1,)), ((), ())),
                                     precision=lax.Precision.HIGHEST, preferred_element_type=F32)

    return pl.pallas_call(
        body, name="na_bias_grad",
        out_shape=jax.ShapeDtypeStruct((n, expand.shape[0]), F32),
        compiler_params=_cparams(vmem=VMEM_BIG),
    )(dtiles, expand)


def _sw_backward(proj, d_o, lse, sink, batch, seq, rider=None):
    t = proj.shape[0]
    n_pairs = SW_WIDTH // LANES
    q_blk = 3 * NA_WIDTH // LANES
    k_blk = q_blk + n_pairs
    n_blocks = seq // SW_BLOCK
    pad = seq + 2 * SW_BLOCK

    def body(sink_ref, q_ref, k_ref, v_ref, do_ref, lse_ref, dq_ref, dk_ref, dv_ref, dsk_ref,
             k_lo, k_hi, v_lo, v_hi, dk_loc, dv_loc, dk_tot, dv_tot):
        hp = pl.program_id(1)
        g = hp // 2
        _sw_prepare(k_ref, g, k_lo, k_hi, seq)
        _sw_prepare(v_ref, g, v_lo, v_hi, seq)
        dk_loc[...] = jnp.zeros_like(dk_loc)
        dv_loc[...] = jnp.zeros_like(dv_loc)

        @pl.when(hp == 0)
        def _():
            dk_tot[...] = jnp.zeros_like(dk_tot)
            dv_tot[...] = jnp.zeros_like(dv_tot)

        band = 3 * SW_BLOCK
        low = lax.broadcasted_iota(jnp.int32, (band, LANES), 1) < HEAD_DIM

        sinks = (sink_ref[2 * hp], sink_ref[2 * hp + 1])

        def scores(n):
            rows = pl.ds(pl.multiple_of(n * SW_BLOCK, SW_BLOCK), SW_BLOCK)
            wrows = pl.ds(pl.multiple_of(n * SW_BLOCK, SW_BLOCK), band)
            qb, do = q_ref[rows, :], do_ref[rows, :]
            k2 = jnp.concatenate([k_lo[wrows, :], k_hi[wrows, :]], axis=0)
            v2 = jnp.concatenate([v_lo[wrows, :], v_hi[wrows, :]], axis=0)
            return n, rows, wrows, qb, do, k2, _mm_nt(qb, k2) * QK_SCALE, _mm_nt(do, v2)

        def finish(sink_acc, n, rows, wrows, qb, do, k2, s2, dp):
            p, ps = _sw_probs_from_lse(s2, _sw_mask(n, seq), sinks, lse_ref[rows, :])
            parts, new = [], []
            for i in range(2):
                ph, dph = p[:, i * band:(i + 1) * band], dp[:, i * band:(i + 1) * band]
                delta = jnp.sum(ph * dph, axis=-1, keepdims=True)
                parts.append(ph * (dph - delta))
                new.append(sink_acc[i] - ps[i] * delta)
            dsb = (jnp.concatenate(parts, axis=1) * QK_SCALE).astype(BF16)
            dq_ref[rows, :] = _mm(dsb, k2)
            dk2 = _mm_tn(dsb, qb)
            dv2 = _mm_tn(p.astype(BF16), do)
            dk_loc[wrows, :] += jnp.where(low, dk2[:band], dk2[band:])
            dv_loc[wrows, :] += jnp.where(low, dv2[:band], dv2[band:])
            return tuple(new)

        def block_group(i, carry):
            for state in [scores(SW_GROUP_BLOCKS * i + j) for j in range(SW_GROUP_BLOCKS)]:
                carry = finish(carry, *state)
            return carry

        zero = jnp.zeros((SW_BLOCK, 1), F32)
        s0, s1 = lax.fori_loop(0, n_blocks // SW_GROUP_BLOCKS, block_group, (zero, zero))
        row = lax.broadcasted_iota(jnp.int32, (SUBLANES, LANES), 0)
        dsk_ref[0, 0] = jnp.where(row == 0, jnp.sum(s0), jnp.where(row == 1, jnp.sum(s1), 0.0))

        lane_s = lax.broadcasted_iota(jnp.int32, (seq, LANES), 1)
        mine_g = (lane_s // HEAD_DIM) == g
        for loc, tot in ((dk_loc, dk_tot), (dv_loc, dv_tot)):
            part = loc[SW_BLOCK:SW_BLOCK + seq, :]
            tot[...] += jnp.where(mine_g, part + pltpu.roll(part, HEAD_DIM, 1), 0.0)

        @pl.when(hp == n_pairs - 1)
        def _():
            dk_ref[...] = dk_tot[...]
            dv_ref[...] = dv_tot[...].astype(BF16)

    return _hosted(
        body, rider, name="sw_backward", grid=(batch, n_pairs),
        out_shape=[jax.ShapeDtypeStruct((t, SW_WIDTH), F32), jax.ShapeDtypeStruct((t, LANES), F32),
                   jax.ShapeDtypeStruct((t, LANES), BF16), jax.ShapeDtypeStruct((batch, n_pairs, SUBLANES, LANES), F32)],
        in_specs=[pl.BlockSpec(memory_space=pltpu.SMEM),
                  pl.BlockSpec((seq, LANES), lambda b, p: (b, q_blk + p)),
                  pl.BlockSpec((seq, LANES), lambda b, p: (b, k_blk)),
                  pl.BlockSpec((seq, LANES), lambda b, p: (b, k_blk + 1)),
                  pl.BlockSpec((seq, LANES), lambda b, p: (b, p)), pl.BlockSpec((seq, LANES), lambda b, p: (b, p))],
        out_specs=[pl.BlockSpec((seq, LANES), lambda b, p: (b, p)), pl.BlockSpec((seq, LANES), lambda b, p: (b, 0)),
                   pl.BlockSpec((seq, LANES), lambda b, p: (b, 0)),
                   pl.BlockSpec((1, 1, SUBLANES, LANES), lambda b, p: (b, p, 0, 0))],
        scratch_shapes=[pltpu.VMEM((pad, LANES), BF16)] * 4 + [pltpu.VMEM((pad, LANES), F32)] * 2
        + [pltpu.VMEM((seq, LANES), F32)] * 2,
        compiler_params=_cparams(("arbitrary", "arbitrary")), args=[sink, proj, proj, proj, d_o, lse])


def _in_backward(dqkv_a, dq_b, dk_b, dv_b, w_in_t, h1, x, mod3, g_attn, dx1, cos_t, sin_t, seq):
    t, d = x.shape
    tm = TOKEN_TILE
    per_seq = seq // tm
    batch = t // seq
    dqa, dka, dva = dqkv_a
    n_q = SW_WIDTH // LANES

    def body(dqa_ref, dka_ref, dva_ref, dqb_ref, dkb_ref, dvb_ref, w_ref, h_ref, x_ref, mod_ref, g_ref, dx1_ref,
             cos_ref, sin_ref, dx_ref, gw_ref, gwb_ref, dsh_ref, dsc_ref, dg_ref):
        i = pl.program_id(0)

        @pl.when(i == 0)
        def _():
            gw_ref[...] = jnp.zeros_like(gw_ref)
            dg_ref[...] = jnp.zeros_like(dg_ref)

        @pl.when(i % per_seq == 0)
        def _():
            dsh_ref[...] = jnp.zeros_like(dsh_ref)
            dsc_ref[...] = jnp.zeros_like(dsc_ref)

        dr = jnp.concatenate([dqb_ref[...], dkb_ref[...]], axis=1)
        cos = jnp.concatenate([cos_ref[...]] * (n_q + 1), axis=1)
        sin = jnp.concatenate([sin_ref[...]] * (n_q + 1), axis=1)
        dr = dr * cos + _rope_rot(dr * sin)
        dproj = jnp.concatenate([dqa_ref[...], dka_ref[...], dva_ref[...], dr.astype(BF16), dvb_ref[...]], axis=1)
        gw_ref[...] += _mm_tn(dproj, h_ref[...])

        @pl.when(i == t // tm - 1)
        def _():
            gwb_ref[...] = gw_ref[...].astype(BF16)

        dh = _mm(dproj, w_ref[...])
        scale = mod_ref[0, :, d:2 * d]
        r, xn = _rms_stats(x_ref[...])
        xg = xn * g_ref[...]
        dxg = dh * (1.0 + scale)
        dx_ref[...] = dx1_ref[...] + _rms_bwd(dxg * g_ref[...], xn, r)
        dg_ref[...] += jnp.sum(dxg * xn, axis=0, keepdims=True)
        dsh_ref[0] += jnp.sum(dh, axis=0, keepdims=True)
        dsc_ref[0] += jnp.sum(dh * xg, axis=0, keepdims=True)

    tile = lambda w: pl.BlockSpec((tm, w), lambda i: (i, 0))
    per_b = pl.BlockSpec((1, 1, d), lambda i: (i // per_seq, 0, 0))
    small = jax.ShapeDtypeStruct((batch, 1, d), F32)
    rope = pl.BlockSpec((tm, LANES), lambda i: (i % per_seq, 0))
    return pl.pallas_call(
        body, name="in_backward", grid=(t // tm,),
        out_shape=(jax.ShapeDtypeStruct((t, d), F32), jax.ShapeDtypeStruct((IN_WIDTH, d), F32),
                   jax.ShapeDtypeStruct((IN_WIDTH, d), BF16), small, small, jax.ShapeDtypeStruct((1, d), F32)),
        in_specs=[tile(NA_WIDTH), tile(NA_WIDTH), tile(NA_WIDTH), tile(SW_WIDTH), tile(LANES), tile(LANES),
                  _resident((IN_WIDTH, d)), tile(d), tile(d),
                  pl.BlockSpec((1, 1, 6 * d), lambda i: (i // per_seq, 0, 0)),
                  pl.BlockSpec((1, d), lambda i: (0, 0)), tile(d), rope, rope],
        out_specs=(tile(d), _resident((IN_WIDTH, d)), _resident((IN_WIDTH, d)),
                   per_b, per_b, pl.BlockSpec((1, d), lambda i: (0, 0))),
        compiler_params=_cparams(("arbitrary",), VMEM_BIG),
    )(dqa, dka, dva, dq_b, dk_b, dv_b, w_in_t, h1, x, mod3, g_attn, dx1, cos_t, sin_t)


def _ada_weight_grad(sc_all, dmod_cols):
    d = sc_all.shape[1]
    ncol = dmod_cols.shape[1]

    def body(s_ref, m_ref, o_ref):
        o_ref[...] = _mm_tn(s_ref[...].astype(BF16), m_ref[...].astype(BF16))

    return pl.pallas_call(
        body, name="ada_weight_grad",
        out_shape=jax.ShapeDtypeStruct((d, ncol), F32),
        compiler_params=_cparams(vmem=VMEM_BIG),
    )(sc_all, dmod_cols)


def _row_tile(rows, cols):
    target = max(SUBLANES, (1 << 20) // (4 * cols))
    best = rows
    for cand in range(SUBLANES, rows + 1, SUBLANES):
        if rows % cand == 0 and cand <= target:
            best = cand
    return best if rows % SUBLANES == 0 else rows


def _sum_slots(parts, name):
    n = len(parts)
    _, rows, cols = parts[0][0].shape
    tr = _row_tile(rows, cols)
    per = rows // tr

    def body(*refs):
        o_ref = refs[-1]
        for q in range(n):
            @pl.when(pl.program_id(0) == q)
            def _(q=q):
                p_ref, own_ref = refs[2 * q], refs[2 * q + 1]
                o_ref[...] = ((own_ref[...] + p_ref[0].astype(F32)) + p_ref[1].astype(F32)) + p_ref[2].astype(F32)

    in_specs, args = [], []
    for q, (recv, own) in enumerate(parts):
        in_specs.append(pl.BlockSpec((N_SHARD - 1, tr, cols), lambda p, i, q=q: (0, jnp.where(p == q, i, 0), 0)))
        in_specs.append(pl.BlockSpec((tr, cols), lambda p, i, q=q: (jnp.where(p == q, i, 0), 0)))
        args += [recv, own]
    return pl.pallas_call(
        body, name=name, grid=(n, per),
        out_shape=jax.ShapeDtypeStruct((n * rows, cols), F32),
        in_specs=in_specs, out_specs=pl.BlockSpec((tr, cols), lambda p, i: (p * per + i, 0)),
        compiler_params=_cparams(("arbitrary", "arbitrary")),
    )(*args)


def _adamw_math(w, g, m, v):
    m2 = ADAM_B1 * m + (1.0 - ADAM_B1) * g
    v2 = ADAM_B2 * v + (1.0 - ADAM_B2) * (g * g)
    m_hat = m2 / (1.0 - ADAM_B1 ** ADAM_STEP)
    v_hat = v2 / (1.0 - ADAM_B2 ** ADAM_STEP)
    return -ADAM_LR * (m_hat / (jnp.sqrt(v_hat) + ADAM_EPS) + ADAM_WD * w), m2, v2


def _small_step(partials, states, dmod, b_ada_state, rider=None):
    n_upd = len(states)
    moving = list(partials) + [dmod]
    n_mov = len(moving)
    all_states = list(states) + [b_ada_state]

    def body(*refs):
        mov, refs = refs[:n_mov], refs[n_mov:]
        wmv, refs = refs[:3 * (n_upd + 1)], refs[3 * (n_upd + 1):]
        res, refs = refs[:4 * (n_upd + 1)], refs[4 * (n_upd + 1):]
        sums_out, refs = refs[:n_mov - n_upd - 1], refs[n_mov - n_upd - 1:]
        dmod_out, refs = refs[0], refs[1:]
        everyone, (ssem, rsem) = refs[:n_mov], refs[n_mov:]
        x, y, c = _my_pos()
        me = 4 * x + 2 * y + c
        cps = []
        for a in range(n_mov):
            everyone[a][me] = mov[a][...]
            for k in range(1, N_DEV):
                peer = (_flip(x, (k >> 2) & 1), _flip(y, (k >> 1) & 1), _flip(c, k & 1))
                cps.append(pltpu.make_async_remote_copy(
                    src_ref=everyone[a].at[me], dst_ref=everyone[a].at[me], send_sem=ssem.at[a, k - 1],
                    recv_sem=rsem.at[a, k - 1], device_id=peer, device_id_type=MESH))
        for cp in cps:
            cp.start()
        for cp in cps:
            cp.wait_recv()

        def total(a):
            acc = everyone[a][0]
            for dev in range(1, N_DEV):
                acc = acc + everyone[a][dev]
            return acc

        grads = [total(a) for a in range(n_upd)]
        grads.append(jnp.sum(total(n_mov - 1), axis=0, keepdims=True))
        for j, g in enumerate(grads):
            delta, m2, v2 = _adamw_math(wmv[3 * j][...], g, wmv[3 * j + 1][...], wmv[3 * j + 2][...])
            res[4 * j][...] = g
            res[4 * j + 1][...] = delta
            res[4 * j + 2][...] = m2
            res[4 * j + 3][...] = v2
        for j in range(n_mov - n_upd - 1):
            sums_out[j][...] = total(n_upd + j)
        dmod_out[...] = everyone[n_mov - 1][...]
        for cp in cps:
            cp.wait_send()

    vm = pl.BlockSpec(memory_space=pltpu.VMEM)
    sds = jax.ShapeDtypeStruct
    out_shape = []
    for w, _, _ in all_states:
        out_shape += [sds(w.shape, F32)] * 4
    out_shape += [sds(p.shape, F32) for p in partials[n_upd:]]
    out_shape.append(sds((N_DEV,) + dmod.shape, F32))
    args = moving + [a for st in all_states for a in st]
    outs, rides = _hosted(
        body, rider, name="small_step", grid=(), out_shape=out_shape,
        in_specs=[vm] * len(args), out_specs=[vm] * len(out_shape),
        scratch_shapes=[pltpu.VMEM((N_DEV,) + a.shape, F32) for a in moving]
        + [pltpu.SemaphoreType.DMA((n_mov, N_DEV - 1)), pltpu.SemaphoreType.DMA((n_mov, N_DEV - 1))],
        compiler_params=_cparams(vmem=VMEM_BIG), args=args)
    return outs, rides


def _adamw(w, grads, m, v, name):
    rows, cols = w.shape
    tr = _row_tile(rows, cols)
    ng = len(grads)

    def body(*refs):
        w_ref = refs[0]
        g_refs = refs[1:1 + ng]
        m_ref, v_ref = refs[1 + ng], refs[2 + ng]
        g_out, d_out, m_out, v_out = refs[3 + ng:]
        g = g_refs[0][...]
        for extra in g_refs[1:]:
            g = g + extra[...]
        g_out[...] = g
        d_out[...], m_out[...], v_out[...] = _adamw_math(w_ref[...], g, m_ref[...], v_ref[...])

    spec = pl.BlockSpec((tr, cols), lambda i: (i, 0))
    out = jax.ShapeDtypeStruct((rows, cols), F32)
    return pl.pallas_call(
        body, name=name, grid=(rows // tr,),
        out_shape=(out, out, out, out),
        in_specs=[spec] * (3 + ng), out_specs=(spec, spec, spec, spec),
        compiler_params=_cparams(("arbitrary",)),
    )(w, *grads, m, v)


def _rope_tables(seq):
    half = HEAD_DIM // 2
    inv = np.float32(ROPE_THETA) ** (-np.arange(half, dtype=np.float32) / np.float32(half))
    ang = (np.arange(seq, dtype=np.float32)[:, None] * inv[None, :]).astype(np.float64)
    cos, sin = np.cos(ang).astype(np.float32), np.sin(ang).astype(np.float32)
    cos_t = np.concatenate([cos, cos, cos, cos], axis=1)
    sin_t = np.concatenate([-sin, sin, -sin, sin], axis=1)
    return jnp.asarray(cos_t), jnp.asarray(sin_t)


def kernel(x, c, w_ada, b_ada, g_attn, w_in, na_rpb, sw_sink, g_na_out, g_sw_out, w_out, g_ffn, w_up, conv_w, conv_b, w_down, g_final, loss_target, m_w_ada, m_b_ada, m_g_attn, m_w_in, m_na_rpb, m_sw_sink, m_g_na_out, m_g_sw_out, m_w_out, m_g_ffn, m_w_up, m_conv_w, m_conv_b, m_w_down, m_g_final, v_w_ada, v_b_ada, v_g_attn, v_w_in, v_na_rpb, v_sw_sink, v_g_na_out, v_g_sw_out, v_w_out, v_g_ffn, v_w_up, v_conv_w, v_conv_b, v_w_down, v_g_final):
    batch, seq, d = x.shape
    t = batch * seq
    assert d == D_MODEL and seq % (NA_ROWS * GRID_W) == 0 and seq % TOKEN_TILE == 0 and batch <= SUBLANES
    shard = 2 * lax.axis_index("x") + lax.axis_index("y")
    xt = x.reshape(t, d)
    tgt = loss_target.reshape(t, d)

    c8 = jnp.pad(c, ((0, SUBLANES - batch), (0, 0)))
    w_in_t_s = jnp.transpose(w_in[0]).astype(BF16)
    (mod8, sc_all), (w_in_g,) = _ada_forward(c8, w_ada[0], b_ada, _Rider("gather", [w_in_t_s]))
    mod3 = mod8[:batch].reshape(batch, 1, 6 * d)
    w_in_t = w_in_g.reshape(IN_WIDTH, d)

    cos_t, sin_t = _rope_tables(seq)
    (h1, proj), _ = _in_proj(xt, mod3, g_attn, w_in_t, cos_t, sin_t, seq)
    n_heads = NA_WIDTH // HEAD_DIM
    n_tiles, n_dc = 2 * NA_ROWS - 2, 2 * NA_COLS - 1
    expand, neg_mask = _na_bias_pattern()
    rpb = na_rpb[0]
    rows2 = jnp.concatenate([rpb[:, :-1, :], rpb[:, 1:, :]], axis=2).reshape(n_heads * n_tiles, 2 * n_dc)
    rows2 = jnp.pad(rows2, ((0, 0), (0, GRID_W - 2 * n_dc)))
    tiles = _na_bias_tiles(rows2, expand, neg_mask).reshape(n_heads, n_tiles, GRID_W, LANES)
    sink = sw_sink[0]
    w_up_b16 = w_up[0].astype(BF16)
    (oa, lse_a), (w_up_a, w_down_g) = _na_forward(proj, tiles, batch, seq,
                                                  _Rider("gather", [w_up_b16[:d // 2], w_down[0].astype(BF16)]))
    (ob, lse_b), (w_up_b, conv_w_g, w_out_g) = _sw_forward(
        proj, sink, batch, seq, _Rider("gather", [w_up_b16[d // 2:], conv_w[0], w_out[0].astype(BF16)]))
    w_up_f = (w_up_a, w_up_b)
    w_out_f = w_out_g.reshape(d, d)
    conv_w_f = jnp.transpose(conv_w_g, (1, 0, 2)).reshape(3, D_FF)
    oab, mix, x1, h2 = _out_proj(oa, ob, g_na_out, g_sw_out, w_out_f, xt, mod3, g_ffn, seq)
    (u,), _ = _up_proj(h2, w_up_f)
    w_down_f = w_down_g.reshape(D_FF, d)
    a = _conv_gate(u, conv_w_f, conv_b, batch, seq)
    dx2, dffn, loss_part, dgate_f, dg_final = _down_and_loss(a, w_down_f, x1, mod3, g_final.reshape(1, d), tgt, seq)

    gw_down, gw_down_b = _down_weight_grad(a, dffn)
    blocks = lambda g, rows: g.reshape(N_SHARD, rows // N_SHARD, d)
    (du, gconv_w, gconv_b), (recv_down, own_down) = _ffn_backward(
        dffn, w_down_f, u, conv_w_f, conv_b, batch, seq,
        _Rider("scatter", [blocks(gw_down_b, D_FF)], [blocks(gw_down, D_FF)]))
    (gw_up_top, gw_up_bot, gw_up_top_b, gw_up_bot_b), _ = _up_weight_grad(h2, du)
    (dx1, dmix, dshift_f, dscale_f, dgate_a, dg_ffn), _ = _up_backward(du, w_up_f, x1, mod3, g_ffn, dx2, mix, seq)
    doa, dob, gw_out, gw_out_b, dg_na, dg_sw = _out_backward(dmix, w_out_f, oab, oa, ob, g_na_out, g_sw_out)
    (dqa, dka, dva, dtiles), (recv_up_bot, own_up_bot) = _na_backward(
        proj, doa, lse_a, tiles, batch, seq, _Rider("scatter", [gw_up_bot_b], [gw_up_bot]))
    (dq_b, dk_b, dv_b, dsink_parts), (recv_out, recv_up_top, own_out, own_up_top) = _sw_backward(
        proj, dob, lse_b, sink, batch, seq,
        _Rider("scatter", [blocks(gw_out_b, d), gw_up_top_b], [blocks(gw_out, d), gw_up_top]))
    gx, gw_in_t, gw_in_b, dshift_a, dscale_a, dg_attn = _in_backward(
        (dqa, dka, dva), dq_b, dk_b, dv_b, w_in_t, h1, xt, mod3, g_attn, dx1, cos_t, sin_t, seq)

    red = _na_bias_grad(dtiles.reshape(n_heads * n_tiles, GRID_W, LANES), expand)[:, :2 * n_dc]
    red = red.reshape(n_heads, n_tiles, 2, n_dc)
    zero_row = jnp.zeros((n_heads, 1, n_dc), F32)
    g_rpb = (jnp.concatenate([red[:, :, 0, :], zero_row], axis=1)
             + jnp.concatenate([zero_row, red[:, :, 1, :]], axis=1))
    g_sink = jnp.sum(dsink_parts[:, :, :2, 0], axis=0).reshape(SW_WIDTH // HEAD_DIM)

    dmod = jnp.concatenate([dshift_a, dscale_a, dgate_a, dshift_f, dscale_f, dgate_f], axis=2).reshape(batch, 6 * d)
    rpb_shape = na_rpb.shape[1:]
    states = [(g_attn, m_g_attn, v_g_attn),
              (na_rpb.reshape(rpb_shape), m_na_rpb.reshape(rpb_shape), v_na_rpb.reshape(rpb_shape)),
              (sw_sink, m_sw_sink, v_sw_sink), (g_na_out, m_g_na_out, v_g_na_out), (g_sw_out, m_g_sw_out, v_g_sw_out),
              (g_ffn, m_g_ffn, v_g_ffn), (conv_b, m_conv_b, v_conv_b),
              (g_final.reshape(1, d), m_g_final.reshape(1, d), v_g_final.reshape(1, d))]
    partials = [dg_attn, g_rpb, g_sink.reshape(sw_sink.shape), dg_na, dg_sw, dg_ffn, gconv_b, dg_final,
                gconv_w, loss_part]
    mine = [None, _sum_slots([(recv_out, own_out)], "sum_w_out"),
            _sum_slots([(recv_up_top, own_up_top), (recv_up_bot, own_up_bot)], "sum_w_up"),
            _sum_slots([(recv_down, own_down)], "sum_w_down")]
    small, (recv_in, own_in, *theirs) = _small_step(
        partials, states, dmod, (b_ada, m_b_ada, v_b_ada),
        _Riders([_Rider("scatter", [blocks(gw_in_b, IN_WIDTH)], [blocks(gw_in_t, IN_WIDTH)]),
                 _Rider("swap", mine[1:])]))
    r_small = [small[4 * j:4 * j + 4] for j in range(len(states) + 1)]
    g_conv_w_full, loss_sum, dmod_all = small[4 * (len(states) + 1):]
    loss = loss_sum[0, 0]
    mine[0] = _sum_slots([(recv_in, own_in)], "sum_w_in")
    theirs = _ride_alone(_Rider("swap", mine[:1]), "swap_sibling") + theirs
    dmod_rows = jnp.pad(dmod_all, ((0, 0), (0, SUBLANES - batch), (0, 0))).reshape(N_DEV * SUBLANES, 6 * d)
    ncol = w_ada.shape[2]
    g_w_ada = _ada_weight_grad(sc_all, lax.dynamic_slice(dmod_rows, (0, shard * ncol), (N_DEV * SUBLANES, ncol)))
    cshard = conv_w.shape[2]
    g_conv_w = lax.dynamic_slice(g_conv_w_full, (0, shard * cshard), (3, cshard))

    def big(w, m, v, g_parts, name):
        shape = w.shape
        outs = _adamw(w[0], g_parts, m[0], v[0], name)
        return [o.reshape(shape) for o in outs]

    r_w_ada = big(w_ada, m_w_ada, v_w_ada, [g_w_ada], "adamw_w_ada")
    r_w_in = [jnp.transpose(o).reshape(w_in.shape) for o in
              _adamw(jnp.transpose(w_in[0]), [mine[0], theirs[0]], jnp.transpose(m_w_in[0]), jnp.transpose(v_w_in[0]),
                     "adamw_w_in")]
    r_w_out = big(w_out, m_w_out, v_w_out, [mine[1], theirs[1]], "adamw_w_out")
    r_w_up = big(w_up, m_w_up, v_w_up, [mine[2], theirs[2]], "adamw_w_up")
    r_w_down = big(w_down, m_w_down, v_w_down, [mine[3], theirs[3]], "adamw_w_down")

    r_conv_w = big(conv_w, m_conv_w, v_conv_w, [g_conv_w], "adamw_conv_w")

    def pick(k):
        ga_, rpb_, sk_, gna_, gsw_, gf_, cb_, gfin_, b_ = [r[k] for r in r_small]
        return [r_w_ada[k], b_, ga_, r_w_in[k], rpb_.reshape(na_rpb.shape), sk_, gna_, gsw_, r_w_out[k], gf_,
                r_w_up[k], r_conv_w[k], cb_, r_w_down[k], gfin_.reshape(d)]

    return (loss, gx.reshape(batch, seq, d), *pick(0), *pick(1), *pick(2), *pick(3))
```

```python
import jax
import jax.numpy as jnp
import numpy as np
from jax import lax
from jax.experimental import pallas as pl
from jax.experimental.pallas import tpu as pltpu

F32 = jnp.float32
BF16 = jnp.bfloat16
MESH = pl.DeviceIdType.MESH

D_MODEL = 1024
HEAD_DIM = 64
NA_WIDTH = 512
SW_WIDTH = 512
SW_KV_WIDTH = 128
IN_WIDTH = 2304
D_FF = 2816
GRID_W = 64
NA_ROWS = 8
NA_COLS = 16
SW_BLOCK = 128
ROPE_THETA = 10000.0
EPS = 1e-6
NEG = -1e30
QK_SCALE = HEAD_DIM ** -0.5

ADAM_LR = 0.001
ADAM_B1 = 0.9
ADAM_B2 = 0.999
ADAM_EPS = 1e-08
ADAM_WD = 0.01
ADAM_STEP = 10

N_SHARD = 4
N_DEV = 8
LANES = 128
SUBLANES = 8
TOKEN_TILE = 512
FF_TILE = 256
CONV_CHUNK = 512
NA_GROUP = 8
SW_GROUP_BLOCKS = 8
VMEM_BIG = 56 * 1024 * 1024


def _mm(a, b):
    return jnp.dot(a, b, preferred_element_type=F32)


def _mm_nt(a, b):
    return lax.dot_general(a, b, (((1,), (1,)), ((), ())), preferred_element_type=F32)


def _mm_tn(a, b):
    return lax.dot_general(a, b, (((0,), (0,)), ((), ())), preferred_element_type=F32)


def _cparams(sem=None, vmem=None):
    kw = {}
    if sem is not None:
        kw["dimension_semantics"] = sem
    if vmem is not None:
        kw["vmem_limit_bytes"] = vmem
    return pltpu.CompilerParams(**kw)


def _resident(shape):
    return pl.BlockSpec(shape, lambda i: (0,) * len(shape), pipeline_mode=pl.Buffered(1))


def _sigmoid(x):
    return 1.0 / (1.0 + jnp.exp(-x))


def _rms_stats(x):
    r = lax.rsqrt(jnp.mean(x * x, axis=-1, keepdims=True) + EPS)
    return r, x * r


def _rms_bwd(dxn, xn, r):
    return r * (dxn - xn * jnp.mean(dxn * xn, axis=-1, keepdims=True))


def _my_pos():
    return lax.axis_index("x"), lax.axis_index("y"), lax.axis_index("c")


def _flip(v, bit):
    return 1 - v if bit else v


def _ada_forward(c8, w_ada, b_ada, rider):
    d = c8.shape[1]
    ncol = w_ada.shape[1]

    def body(c_ref, w_ref, b_ref, mod_ref, sc_ref, m_scr, mod_buf, ssem, rsem, ssem2, rsem2):
        x, y, c = _my_pos()
        me = 4 * x + 2 * y + c
        shard = 2 * x + y
        cv = c_ref[...]
        my_rows = pl.ds(pl.multiple_of(me * SUBLANES, SUBLANES), SUBLANES)
        sc_ref[my_rows, :] = cv * _sigmoid(cv)

        def copy1(k):
            peer = (_flip(x, (k >> 2) & 1), _flip(y, (k >> 1) & 1), _flip(c, k & 1))
            return pltpu.make_async_remote_copy(
                src_ref=sc_ref.at[my_rows, :], dst_ref=sc_ref.at[my_rows, :],
                send_sem=ssem.at[k - 1], recv_sem=rsem.at[k - 1], device_id=peer, device_id_type=MESH)

        sends = [copy1(k) for k in range(1, N_DEV)]
        for cp in sends:
            cp.start()
        for cp in sends:
            cp.wait_recv()
        m_scr[...] = _mm(sc_ref[...].astype(BF16), w_ref[...].astype(BF16))

        def copy2(k):
            px, py = _flip(x, (k >> 1) & 1), _flip(y, k & 1)
            rows = pl.ds(pl.multiple_of((4 * px + 2 * py + c) * SUBLANES, SUBLANES), SUBLANES)
            return pltpu.make_async_remote_copy(
                src_ref=m_scr.at[rows, :], dst_ref=mod_buf.at[shard],
                send_sem=ssem2.at[k - 1], recv_sem=rsem2.at[k - 1], device_id=(px, py, c), device_id_type=MESH)

        sends2 = [copy2(k) for k in range(1, N_SHARD)]
        for cp in sends2:
            cp.start()
        mod_buf[shard] = m_scr[my_rows, :]
        for cp in sends2:
            cp.wait_recv()
        for s in range(N_SHARD):
            mod_ref[:, s * ncol:(s + 1) * ncol] = mod_buf[s] + b_ref[:, s * ncol:(s + 1) * ncol]
        for cp in sends + sends2:
            cp.wait_send()

    vm = pl.BlockSpec(memory_space=pltpu.VMEM)
    return _hosted(
        body, rider, name="ada_forward", grid=(),
        out_shape=(jax.ShapeDtypeStruct((SUBLANES, N_SHARD * ncol), F32),
                   jax.ShapeDtypeStruct((N_DEV * SUBLANES, d), F32)),
        in_specs=[vm, vm, vm], out_specs=(vm, vm),
        scratch_shapes=[pltpu.VMEM((N_DEV * SUBLANES, ncol), F32), pltpu.VMEM((N_SHARD, SUBLANES, ncol), F32),
                        pltpu.SemaphoreType.DMA((N_DEV - 1,)), pltpu.SemaphoreType.DMA((N_DEV - 1,)),
                        pltpu.SemaphoreType.DMA((N_SHARD - 1,)), pltpu.SemaphoreType.DMA((N_SHARD - 1,))],
        compiler_params=_cparams(vmem=VMEM_BIG), args=[c8, w_ada, b_ada])


class _Rider:
    def __init__(self, kind, srcs, owns=()):
        self.kind, self.srcs, self.owns = kind, list(srcs), list(owns)
        n = len(self.srcs)
        sds = jax.ShapeDtypeStruct
        dma = pltpu.SemaphoreType.DMA
        if kind == "gather":
            self.out_shapes = [sds((N_SHARD,) + s.shape, s.dtype) for s in self.srcs]
            self.sems = [dma((n, N_SHARD - 1)), dma((n, N_SHARD - 1)), dma((n, N_SHARD - 1)), dma((n, N_SHARD - 1)),
                         dma((n,)), dma((n,))]
        elif kind == "scatter":
            self.out_shapes = ([sds((N_SHARD - 1,) + s.shape[1:], s.dtype) for s in self.srcs]
                               + [sds(o.shape[1:], o.dtype) for o in self.owns])
            m = max(len(self.owns), 1)
            self.sems = [dma((n, N_SHARD - 1)), dma((n, N_SHARD - 1)), dma((m,)), dma((m,))]
        else:
            self.out_shapes = [sds(s.shape, s.dtype) for s in self.srcs]
            self.sems = [dma((n,)), dma((n,))]

    @property
    def inputs(self):
        return self.srcs + self.owns

    def _halved(self, i):
        a = self.srcs[i]
        tile_rows = SUBLANES * (4 // jnp.dtype(a.dtype).itemsize)
        return self.kind == "gather" and a.shape[0] % (2 * tile_rows) == 0

    def copies(self, ins, outs, sems):
        n = len(self.srcs)
        x, y, c = _my_pos()
        shard = 2 * x + y
        remote, relay = [], []
        if self.kind == "swap":
            ssem, rsem = sems
            for i in range(n):
                remote.append(pltpu.make_async_remote_copy(
                    src_ref=ins[i], dst_ref=outs[i], send_sem=ssem.at[i], recv_sem=rsem.at[i],
                    device_id=(x, y, 1 - c), device_id_type=MESH))
            return remote, relay
        if self.kind == "gather":
            ssem, rsem, ssem2, rsem2, sib_s, sib_r = sems
        else:
            ssem, rsem, sib_s, sib_r = sems
        for i in range(n):
            if self.kind == "gather":
                remote.append(pltpu.make_async_remote_copy(
                    src_ref=ins[i], dst_ref=outs[i].at[shard], send_sem=sib_s.at[i], recv_sem=sib_r.at[i],
                    device_id=(x, y, 1 - c), device_id_type=MESH))
                half = ins[i].shape[0] // 2
                mine = pl.ds(pl.multiple_of(c * half, half), half) if self._halved(i) else None
            for k in range(1, N_SHARD):
                px, py = _flip(x, (k >> 1) & 1), _flip(y, k & 1)
                if self.kind == "gather":
                    src, dst = ins[i], outs[i].at[shard]
                    if mine is not None:
                        src, dst = src.at[mine], dst.at[mine]
                        got = outs[i].at[2 * px + py].at[mine]
                        relay.append(pltpu.make_async_remote_copy(
                            src_ref=got, dst_ref=got, send_sem=ssem2.at[i, k - 1], recv_sem=rsem2.at[i, k - 1],
                            device_id=(x, y, 1 - c), device_id_type=MESH))
                else:
                    src, dst = ins[i].at[2 * px + py], outs[i].at[k - 1]
                remote.append(pltpu.make_async_remote_copy(
                    src_ref=src, dst_ref=dst, send_sem=ssem.at[i, k - 1], recv_sem=rsem.at[i, k - 1],
                    device_id=(px, py, c), device_id_type=MESH))
        if self.kind == "scatter":
            for i in range(len(self.owns)):
                remote.append(pltpu.make_async_remote_copy(
                    src_ref=ins[n + i].at[shard], dst_ref=outs[n + i], send_sem=sib_s.at[i], recv_sem=sib_r.at[i],
                    device_id=(x, y, 1 - c), device_id_type=MESH))
        return remote, relay

    def start(self, ins, outs, sems):
        remote, _ = self.copies(ins, outs, sems)
        for cp in remote:
            cp.start()

    def wait(self, ins, outs, sems):
        remote, relay = self.copies(ins, outs, sems)
        for cp in remote:
            cp.wait_recv()
        for cp in relay:
            cp.start()
        for cp in relay:
            cp.wait_recv()
        for cp in remote + relay:
            cp.wait_send()


class _Riders:
    def __init__(self, riders):
        self.riders = list(riders)
        self.inputs = [a for r in self.riders for a in r.inputs]
        self.out_shapes = [s for r in self.riders for s in r.out_shapes]
        self.sems = [s for r in self.riders for s in r.sems]

    def _split(self, ins, outs, sems):
        for r in self.riders:
            ni, no, ns = len(r.inputs), len(r.out_shapes), len(r.sems)
            yield r, ins[:ni], outs[:no], sems[:ns]
            ins, outs, sems = ins[ni:], outs[no:], sems[ns:]

    def start(self, ins, outs, sems):
        for r, i, o, s in self._split(ins, outs, sems):
            r.start(i, o, s)

    def wait(self, ins, outs, sems):
        for r, i, o, s in self._split(ins, outs, sems):
            r.wait(i, o, s)


def _hosted(body, rider, *, name, grid, out_shape, in_specs, out_specs, scratch_shapes, compiler_params, args):
    out_shape, out_specs = list(out_shape), list(out_specs)
    if rider is None:
        outs = pl.pallas_call(body, name=name, grid=grid, out_shape=tuple(out_shape), in_specs=list(in_specs),
                              out_specs=tuple(out_specs), scratch_shapes=list(scratch_shapes),
                              compiler_params=compiler_params)(*args)
        return list(outs), []
    n_in, n_out, n_scr = len(in_specs), len(out_shape), len(scratch_shapes)
    nr_in, nr_out = len(rider.inputs), len(rider.out_shapes)
    n_steps = 1
    for size in grid:
        n_steps *= size

    def full(*refs):
        ins, refs = refs[:n_in], refs[n_in:]
        r_in, refs = refs[:nr_in], refs[nr_in:]
        outs, refs = refs[:n_out], refs[n_out:]
        r_out, refs = refs[:nr_out], refs[nr_out:]
        scr, sems = refs[:n_scr], refs[n_scr:]
        if grid:
            step = 0
            for ax, size in enumerate(grid):
                step = step * size + pl.program_id(ax)
            pl.when(step == 0)(lambda: rider.start(r_in, r_out, sems))
            body(*ins, *outs, *scr)
            pl.when(step == n_steps - 1)(lambda: rider.wait(r_in, r_out, sems))
        else:
            rider.start(r_in, r_out, sems)
            body(*ins, *outs, *scr)
            rider.wait(r_in, r_out, sems)

    hbm = pl.BlockSpec(memory_space=pl.ANY)
    res = pl.pallas_call(
        full, name=name, grid=grid, out_shape=tuple(out_shape + rider.out_shapes),
        in_specs=list(in_specs) + [hbm] * nr_in, out_specs=tuple(out_specs + [hbm] * nr_out),
        scratch_shapes=list(scratch_shapes) + rider.sems, compiler_params=compiler_params,
    )(*args, *rider.inputs)
    return list(res[:n_out]), list(res[n_out:])


def _ride_alone(rider, name):
    return _hosted(lambda: None, rider, name=name, grid=(), out_shape=[], in_specs=[], out_specs=[], scratch_shapes=[],
                   compiler_params=_cparams(), args=[])[1]


def _rope_rot(t):
    w = t.shape[1]
    lane = lax.broadcasted_iota(jnp.int32, t.shape, 1)
    first = (lane % HEAD_DIM) < (HEAD_DIM // 2)
    return jnp.where(first, pltpu.roll(t, w - HEAD_DIM // 2, 1), pltpu.roll(t, HEAD_DIM // 2, 1))


def _in_proj(x, mod3, g_attn, w_in_t, cos_t, sin_t, seq, rider=None):
    t, d = x.shape
    tm = 2 * TOKEN_TILE
    per_seq = seq // tm
    rope_lo, rope_hi = 3 * NA_WIDTH, 3 * NA_WIDTH + SW_WIDTH + SW_KV_WIDTH
    n_rep = (rope_hi - rope_lo) // LANES

    def body(x_ref, mod_ref, g_ref, w_ref, cos_ref, sin_ref, h_ref, p_ref):
        r, xn = _rms_stats(x_ref[...])
        shift, scale = mod_ref[0, :, 0:d], mod_ref[0, :, d:2 * d]
        hb = ((xn * g_ref[...]) * (1.0 + scale) + shift).astype(BF16)
        h_ref[...] = hb
        p_ref[:, :rope_lo] = _mm_nt(hb, w_ref[:rope_lo, :]).astype(BF16)
        pr = _mm_nt(hb, w_ref[rope_lo:rope_hi, :])
        cos = jnp.concatenate([cos_ref[...]] * n_rep, axis=1)
        sin = jnp.concatenate([sin_ref[...]] * n_rep, axis=1)
        p_ref[:, rope_lo:rope_hi] = (pr * cos + _rope_rot(pr) * sin).astype(BF16)
        p_ref[:, rope_hi:] = _mm_nt(hb, w_ref[rope_hi:, :]).astype(BF16)

    return _hosted(
        body, rider, name="in_proj", grid=(t // tm,),
        out_shape=[jax.ShapeDtypeStruct((t, d), BF16), jax.ShapeDtypeStruct((t, IN_WIDTH), BF16)],
        in_specs=[pl.BlockSpec((tm, d), lambda i: (i, 0)),
                  pl.BlockSpec((1, 1, 6 * d), lambda i: (i // per_seq, 0, 0)),
                  pl.BlockSpec((1, d), lambda i: (0, 0)),
                  pl.BlockSpec((IN_WIDTH, d), lambda i: (0, 0)),
                  pl.BlockSpec((tm, LANES), lambda i: (i % per_seq, 0)),
                  pl.BlockSpec((tm, LANES), lambda i: (i % per_seq, 0))],
        out_specs=[pl.BlockSpec((tm, d), lambda i: (i, 0)), pl.BlockSpec((tm, IN_WIDTH), lambda i: (i, 0))],
        scratch_shapes=[], compiler_params=_cparams(("arbitrary",), VMEM_BIG),
        args=[x, mod3, g_attn, w_in_t, cos_t, sin_t])


def _na_bias_pattern():
    n_dc = 2 * NA_COLS - 1
    j = np.arange(GRID_W)[:, None]
    m = np.arange(GRID_W * LANES)[None, :]
    q, lane = m // LANES, m % LANES
    k = lane % GRID_W
    cs = np.clip(q - NA_COLS // 2, 0, GRID_W - NA_COLS)
    ok = (k >= cs) & (k < cs + NA_COLS)
    hit = ok & (j < 2 * n_dc) & (lane // GRID_W == j // n_dc) & (k - q + (NA_COLS - 1) == j % n_dc)
    return jnp.asarray(hit.astype(np.float32)), jnp.asarray(np.where(ok, 0.0, NEG).astype(np.float32))


def _na_bias_tiles(rows2, expand, mask):
    n, width = rows2.shape[0], expand.shape[1]
    q_step = 16
    step = q_step * LANES

    def body(r_ref, e_ref, m_ref, o_ref):
        flat = jnp.dot(r_ref[...], e_ref[...], precision=lax.Precision.HIGHEST,
                       preferred_element_type=F32) + m_ref[...]
        for qq in range(q_step):
            o_ref[:, qq, :] = flat[:, qq * LANES:(qq + 1) * LANES]

    return pl.pallas_call(
        body, name="na_bias_tiles", grid=(width // step,),
        out_shape=jax.ShapeDtypeStruct((n, GRID_W, LANES), F32),
        in_specs=[pl.BlockSpec(rows2.shape, lambda i: (0, 0)), pl.BlockSpec((expand.shape[0], step), lambda i: (0, i)),
                  pl.BlockSpec((1, step), lambda i: (0, i))],
        out_specs=pl.BlockSpec((n, q_step, LANES), lambda i: (0, i, 0)),
        compiler_params=_cparams(("arbitrary",)),
    )(rows2, expand, mask)


def _na_prepare(k_ref, v_ref, km, vm):
    lane = lax.broadcasted_iota(jnp.int32, k_ref.shape, 1)
    low = lane < HEAD_DIM
    kv = k_ref[...]
    vv = v_ref[...]
    zero = jnp.zeros_like(kv)
    km[0] = jnp.where(low, kv, zero)
    km[1] = jnp.where(low, zero, kv)
    vm[0] = jnp.where(low, vv, zero)
    vm[1] = jnp.where(low, zero, vv)


def _na_window(r, n_rows):
    rs = jnp.clip(r - NA_ROWS // 2, 0, n_rows - NA_ROWS)
    return rs, r - rs


def _na_pair_window(ref, wrows):
    return jnp.concatenate([ref[0, wrows, :], ref[1, wrows, :]], axis=0)


def _na_scores(q, k2, tp_ref, off):
    bias = jnp.concatenate([tp_ref[h, 2 * w - off + (NA_ROWS - 1)] for h in range(2) for w in range(NA_ROWS // 2)],
                           axis=1)
    return _mm_nt(q, k2) * QK_SCALE + bias


def _pair_lse_block(lse):
    lane = lax.broadcasted_iota(jnp.int32, (lse[0].shape[0], LANES), 1)
    return jnp.where(lane < HEAD_DIM, lse[0], lse[1])


def _pair_softmax(s):
    win = s.shape[1] // 2
    halves, lse = [], []
    for h in range(2):
        sh = s[:, h * win:(h + 1) * win]
        m = jnp.max(sh, axis=-1, keepdims=True)
        e = jnp.exp(sh - m)
        l = jnp.sum(e, axis=-1, keepdims=True)
        halves.append(e / l)
        lse.append(m + jnp.log(l))
    return jnp.concatenate(halves, axis=1), _pair_lse_block(lse)


def _pair_probs_from_lse(s, lse_block):
    win = s.shape[1] // 2
    return jnp.concatenate([jnp.exp(s[:, h * win:(h + 1) * win] - lse_block[:, h * HEAD_DIM:h * HEAD_DIM + 1])
                            for h in range(2)], axis=1)


def _na_forward(proj, tiles, batch, seq, rider=None):
    t = proj.shape[0]
    n_rows = seq // GRID_W
    n_pairs = NA_WIDTH // LANES
    win = NA_ROWS * GRID_W

    def body(q_ref, k_ref, v_ref, tp_ref, o_ref, lse_ref, km, vm):
        _na_prepare(k_ref, v_ref, km, vm)

        def scores(r):
            rs, off = _na_window(r, n_rows)
            rows = pl.ds(pl.multiple_of(r * GRID_W, GRID_W), GRID_W)
            wrows = pl.ds(pl.multiple_of(rs * GRID_W, GRID_W), win)
            return rows, wrows, _na_scores(q_ref[rows, :], _na_pair_window(km, wrows), tp_ref, off)

        def finish(rows, wrows, s):
            p, lse = _pair_softmax(s)
            lse_ref[rows, :] = lse
            o_ref[rows, :] = _mm(p.astype(BF16), _na_pair_window(vm, wrows))

        def row_group(i, carry):
            for state in [scores(NA_GROUP * i + j) for j in range(NA_GROUP)]:
                finish(*state)
            return carry

        lax.fori_loop(0, n_rows // NA_GROUP, row_group, 0)

    return _hosted(
        body, rider, name="na_forward", grid=(batch, n_pairs),
        out_shape=[jax.ShapeDtypeStruct((t, NA_WIDTH), F32), jax.ShapeDtypeStruct((t, NA_WIDTH), F32)],
        in_specs=[pl.BlockSpec((seq, LANES), lambda b, p: (b, p)),
                  pl.BlockSpec((seq, LANES), lambda b, p: (b, n_pairs + p)),
                  pl.BlockSpec((seq, LANES), lambda b, p: (b, 2 * n_pairs + p)),
                  pl.BlockSpec((2, 2 * NA_ROWS - 2, GRID_W, LANES), lambda b, p: (p, 0, 0, 0))],
        out_specs=[pl.BlockSpec((seq, LANES), lambda b, p: (b, p)), pl.BlockSpec((seq, LANES), lambda b, p: (b, p))],
        scratch_shapes=[pltpu.VMEM((2, seq, LANES), BF16), pltpu.VMEM((2, seq, LANES), BF16)],
        compiler_params=_cparams(("arbitrary", "arbitrary")), args=[proj, proj, proj, tiles])


def _sw_prepare(kv_ref, g, dst_lo, dst_hi, seq):
    lane = lax.broadcasted_iota(jnp.int32, kv_ref.shape, 1)
    mine = (lane // HEAD_DIM) == g
    kg = jnp.where(mine, kv_ref[...].astype(F32), 0.0)
    kr = pltpu.roll(kg, HEAD_DIM, 1)
    first = g == 0
    zero = jnp.zeros((SW_BLOCK, LANES), BF16)
    for dst, val in ((dst_lo, jnp.where(first, kg, kr)), (dst_hi, jnp.where(first, kr, kg))):
        dst[0:SW_BLOCK, :] = zero
        dst[SW_BLOCK:SW_BLOCK + seq, :] = val.astype(BF16)
        dst[SW_BLOCK + seq:, :] = zero


def _sw_mask(n, seq):
    qi = lax.broadcasted_iota(jnp.int32, (SW_BLOCK, 3 * SW_BLOCK), 0)
    kj = lax.broadcasted_iota(jnp.int32, (SW_BLOCK, 3 * SW_BLOCK), 1)
    kpos = n * SW_BLOCK - SW_BLOCK + kj
    return (jnp.abs(qi + SW_BLOCK - kj) <= SW_BLOCK) & (kpos >= 0) & (kpos < seq)


def _sw_probs(s2, ok, sinks):
    band = s2.shape[1] // 2
    halves, lse = [], []
    for i in range(2):
        s = jnp.where(ok, s2[:, i * band:(i + 1) * band], NEG)
        m = jnp.maximum(jnp.max(s, axis=-1, keepdims=True), sinks[i])
        p = jnp.exp(s - m)
        den = jnp.sum(p, axis=-1, keepdims=True) + jnp.exp(sinks[i] - m)
        halves.append(p / den)
        lse.append(m + jnp.log(den))
    return jnp.concatenate(halves, axis=1), _pair_lse_block(lse)


def _sw_probs_from_lse(s2, ok, sinks, lse_block):
    band = s2.shape[1] // 2
    halves, sink_p = [], []
    for i in range(2):
        lse = lse_block[:, i * HEAD_DIM:i * HEAD_DIM + 1]
        halves.append(jnp.exp(jnp.where(ok, s2[:, i * band:(i + 1) * band], NEG) - lse))
        sink_p.append(jnp.exp(sinks[i] - lse))
    return jnp.concatenate(halves, axis=1), sink_p


def _sw_forward(proj, sink, batch, seq, rider=None):
    t = proj.shape[0]
    n_pairs = SW_WIDTH // LANES
    q_blk = 3 * NA_WIDTH // LANES
    k_blk = q_blk + n_pairs
    n_blocks = seq // SW_BLOCK
    pad = seq + 2 * SW_BLOCK

    def body(sink_ref, q_ref, k_ref, v_ref, o_ref, lse_ref, k_lo, k_hi, v_lo, v_hi):
        hp = pl.program_id(1)
        g = hp // 2
        _sw_prepare(k_ref, g, k_lo, k_hi, seq)
        _sw_prepare(v_ref, g, v_lo, v_hi, seq)

        sinks = (sink_ref[2 * hp], sink_ref[2 * hp + 1])

        def scores(n):
            rows = pl.ds(pl.multiple_of(n * SW_BLOCK, SW_BLOCK), SW_BLOCK)
            wrows = pl.ds(pl.multiple_of(n * SW_BLOCK, SW_BLOCK), 3 * SW_BLOCK)
            k2 = jnp.concatenate([k_lo[wrows, :], k_hi[wrows, :]], axis=0)
            return n, rows, wrows, _mm_nt(q_ref[rows, :], k2) * QK_SCALE

        def finish(n, rows, wrows, s2):
            p, lse = _sw_probs(s2, _sw_mask(n, seq), sinks)
            lse_ref[rows, :] = lse
            v2 = jnp.concatenate([v_lo[wrows, :], v_hi[wrows, :]], axis=0)
            o_ref[rows, :] = _mm(p.astype(BF16), v2)

        def block_group(i, carry):
            for state in [scores(SW_GROUP_BLOCKS * i + j) for j in range(SW_GROUP_BLOCKS)]:
                finish(*state)
            return carry

        lax.fori_loop(0, n_blocks // SW_GROUP_BLOCKS, block_group, 0)

    return _hosted(
        body, rider, name="sw_forward", grid=(batch, n_pairs),
        out_shape=[jax.ShapeDtypeStruct((t, SW_WIDTH), F32), jax.ShapeDtypeStruct((t, SW_WIDTH), F32)],
        in_specs=[pl.BlockSpec(memory_space=pltpu.SMEM),
                  pl.BlockSpec((seq, LANES), lambda b, p: (b, q_blk + p)),
                  pl.BlockSpec((seq, LANES), lambda b, p: (b, k_blk)),
                  pl.BlockSpec((seq, LANES), lambda b, p: (b, k_blk + 1))],
        out_specs=[pl.BlockSpec((seq, LANES), lambda b, p: (b, p)), pl.BlockSpec((seq, LANES), lambda b, p: (b, p))],
        scratch_shapes=[pltpu.VMEM((pad, LANES), BF16)] * 4,
        compiler_params=_cparams(("arbitrary", "arbitrary")), args=[sink, proj, proj, proj])


def _out_proj(oa, ob, g_na, g_sw, w_out, x, mod3, g_ffn, seq):
    t, d = x.shape
    tm = TOKEN_TILE
    per_seq = seq // tm

    def body(oa_ref, ob_ref, gna_ref, gsw_ref, w_ref, x_ref, mod_ref, gf_ref, oab_ref, mix_ref, x1_ref, h2_ref):
        _, na = _rms_stats(oa_ref[...])
        _, nb = _rms_stats(ob_ref[...])
        oab = jnp.concatenate([na * gna_ref[...], nb * gsw_ref[...]], axis=1).astype(BF16)
        oab_ref[...] = oab
        mix = _mm(oab, w_ref[...])
        mix_ref[...] = mix
        gate_a = mod_ref[0, :, 2 * d:3 * d]
        shift_f, scale_f = mod_ref[0, :, 3 * d:4 * d], mod_ref[0, :, 4 * d:5 * d]
        x1 = x_ref[...] + gate_a * mix
        x1_ref[...] = x1
        _, xn = _rms_stats(x1)
        h2_ref[...] = ((xn * gf_ref[...]) * (1.0 + scale_f) + shift_f).astype(BF16)

    tile = lambda w: pl.BlockSpec((tm, w), lambda i: (i, 0))
    vec = lambda w: pl.BlockSpec((1, w), lambda i: (0, 0))
    return pl.pallas_call(
        body, name="out_proj", grid=(t // tm,),
        out_shape=(jax.ShapeDtypeStruct((t, d), BF16), jax.ShapeDtypeStruct((t, d), F32),
                   jax.ShapeDtypeStruct((t, d), F32), jax.ShapeDtypeStruct((t, d), BF16)),
        in_specs=[tile(NA_WIDTH), tile(SW_WIDTH), vec(NA_WIDTH), vec(SW_WIDTH),
                  pl.BlockSpec((d, d), lambda i: (0, 0)), tile(d),
                  pl.BlockSpec((1, 1, 6 * d), lambda i: (i // per_seq, 0, 0)), vec(d)],
        out_specs=(tile(d), tile(d), tile(d), tile(d)),
        compiler_params=_cparams(("arbitrary",), VMEM_BIG),
    )(oa, ob, g_na, g_sw, w_out, x, mod3, g_ffn)


def _up_proj(h2, w_up_halves, rider=None):
    t, d = h2.shape
    tm = 2 * TOKEN_TILE
    w_a, w_b = w_up_halves
    half, wcol = w_a.shape[1], w_a.shape[2]

    def body(h_ref, wa_ref, wb_ref, u_ref):
        u_ref[0] = (_mm(h_ref[:, :half], wa_ref[0]) + _mm(h_ref[:, half:], wb_ref[0])).astype(BF16)

    w_spec = pl.BlockSpec((1, half, wcol), lambda j, i: (j, 0, 0))
    return _hosted(
        body, rider, name="up_proj", grid=(N_SHARD, t // tm),
        out_shape=[jax.ShapeDtypeStruct((2, t, D_FF), BF16)],
        in_specs=[pl.BlockSpec((tm, d), lambda j, i: (i, 0)), w_spec, w_spec],
        out_specs=[pl.BlockSpec((1, tm, wcol), lambda j, i: (j // 2, i, j % 2))],
        scratch_shapes=[], compiler_params=_cparams(("arbitrary", "arbitrary"), VMEM_BIG), args=[h2, w_a, w_b])


def _taps_chunk(load, s, rows, seq):
    halo = 2 * SUBLANES
    cur = load(s, rows)
    above = load(pl.multiple_of(jnp.maximum(s - halo, 0), halo), halo)
    below = load(pl.multiple_of(jnp.minimum(s + rows, seq - halo), halo), halo)
    up = jnp.where(s > 0, above[halo - 1:halo, :], 0.0)
    dn = jnp.where(s + rows < seq, below[0:1, :], 0.0)
    row = lax.broadcasted_iota(jnp.int32, cur.shape, 0)
    prev = jnp.where(row == 0, up, pltpu.roll(cur, 1, 0))
    nxt = jnp.where(row == rows - 1, dn, pltpu.roll(cur, rows - 1, 0))
    return cur, prev, nxt


def _conv_gate(u, conv_w, conv_b, batch, seq):
    t = u.shape[1]
    cw = FF_TILE
    rows = CONV_CHUNK

    def body(u_ref, w_ref, b_ref, a_ref):
        def chunk(i, carry):
            s = pl.multiple_of(i * rows, rows)
            gt, prev, nxt = _taps_chunk(lambda at, n: u_ref[1, pl.ds(at, n), :].astype(F32), s, rows, seq)
            gc = prev * w_ref[0:1, :] + gt * w_ref[1:2, :] + nxt * w_ref[2:3, :] + b_ref[...]
            a_ref[pl.ds(s, rows), :] = ((gc * _sigmoid(gc)) * u_ref[0, pl.ds(s, rows), :].astype(F32)).astype(BF16)
            return carry

        lax.fori_loop(0, seq // rows, chunk, 0)

    return pl.pallas_call(
        body, name="conv_gate", grid=(batch, D_FF // cw),
        out_shape=jax.ShapeDtypeStruct((t, D_FF), BF16),
        in_specs=[pl.BlockSpec((2, seq, cw), lambda b, j: (0, b, j)),
                  pl.BlockSpec((3, cw), lambda b, j: (0, j)), pl.BlockSpec((1, cw), lambda b, j: (0, j))],
        out_specs=pl.BlockSpec((seq, cw), lambda b, j: (b, j)),
        compiler_params=_cparams(("arbitrary", "arbitrary"), VMEM_BIG),
    )(u, conv_w, conv_b)


def _down_and_loss(a, w_down, x1, mod3, g_final, target, seq):
    t, d = x1.shape
    tm = TOKEN_TILE
    per_seq = seq // tm
    batch = t // seq

    def body(a_ref, w_ref, x1_ref, mod_ref, g_ref, tgt_ref, dx2_ref, dffn_ref, loss_ref, dgate_ref, dg_ref):
        i = pl.program_id(0)
        f = _mm(a_ref[...], w_ref[...])
        gate_f = mod_ref[0, :, 5 * d:6 * d]
        x2 = x1_ref[...] + gate_f * f
        r, xn = _rms_stats(x2)
        err = xn * g_ref[...] - tgt_ref[...]
        part = 0.5 * jnp.sum(jnp.mean(err * err, axis=-1, keepdims=True))
        dy = err / d
        dx2 = _rms_bwd(dy * g_ref[...], xn, r)
        dx2_ref[...] = dx2
        dffn_ref[...] = (dx2 * gate_f).astype(BF16)

        @pl.when(i == 0)
        def _():
            loss_ref[...] = jnp.zeros_like(loss_ref)
            dg_ref[...] = jnp.zeros_like(dg_ref)

        @pl.when(i % per_seq == 0)
        def _():
            dgate_ref[...] = jnp.zeros_like(dgate_ref)

        loss_ref[...] += part
        dg_ref[...] += jnp.sum(dy * xn, axis=0, keepdims=True)
        dgate_ref[0] += jnp.sum(dx2 * f, axis=0, keepdims=True)

    tile = lambda w: pl.BlockSpec((tm, w), lambda i: (i, 0))
    return pl.pallas_call(
        body, name="down_loss", grid=(t // tm,),
        out_shape=(jax.ShapeDtypeStruct((t, d), F32), jax.ShapeDtypeStruct((t, d), BF16),
                   jax.ShapeDtypeStruct((SUBLANES, LANES), F32), jax.ShapeDtypeStruct((batch, 1, d), F32),
                   jax.ShapeDtypeStruct((1, d), F32)),
        in_specs=[tile(D_FF), _resident((D_FF, d)), tile(d),
                  pl.BlockSpec((1, 1, 6 * d), lambda i: (i // per_seq, 0, 0)),
                  pl.BlockSpec((1, d), lambda i: (0, 0)), tile(d)],
        out_specs=(tile(d), tile(d), pl.BlockSpec((SUBLANES, LANES), lambda i: (0, 0)),
                   pl.BlockSpec((1, 1, d), lambda i: (i // per_seq, 0, 0)), pl.BlockSpec((1, d), lambda i: (0, 0))),
        compiler_params=_cparams(("arbitrary",), VMEM_BIG),
    )(a, w_down, x1, mod3, g_final, target)


def _down_weight_grad(a, dffn):
    t, dff = a.shape
    d = dffn.shape[1]
    tk = 2 * TOKEN_TILE
    n_k = t // tk

    def body(a_ref, df_ref, g_ref, gb_ref):
        k = pl.program_id(0)

        @pl.when(k == 0)
        def _():
            g_ref[...] = jnp.zeros_like(g_ref)

        g_ref[...] += _mm_tn(a_ref[...], df_ref[...])

        @pl.when(k == n_k - 1)
        def _():
            gb_ref[...] = g_ref[...].astype(BF16)

    whole = _resident((dff, d))
    return pl.pallas_call(
        body, name="down_weight_grad", grid=(n_k,),
        out_shape=(jax.ShapeDtypeStruct((dff, d), F32), jax.ShapeDtypeStruct((dff, d), BF16)),
        in_specs=[pl.BlockSpec((tk, dff), lambda k: (k, 0)), pl.BlockSpec((tk, d), lambda k: (k, 0))],
        out_specs=(whole, whole),
        compiler_params=_cparams(("arbitrary",), VMEM_BIG),
    )(a, dffn)


def _ffn_backward(dffn, w_down, u, conv_w, conv_b, batch, seq, rider=None):
    t, d = dffn.shape
    cw = FF_TILE
    rows = CONV_CHUNK

    def body(df_ref, wd_ref, u_ref, w_ref, b_ref, du_ref, gcw_ref, gcb_ref, da_scr, dgc_scr):
        b = pl.program_id(1)
        da_scr[...] = _mm_nt(df_ref[...], wd_ref[...])

        @pl.when(b == 0)
        def _():
            gcw_ref[...] = jnp.zeros_like(gcw_ref)
            gcb_ref[...] = jnp.zeros_like(gcb_ref)

        def fold(v):
            return jnp.sum(v.reshape(rows // SUBLANES, SUBLANES, cw), axis=0)

        def chunk(i, carry):
            s = pl.multiple_of(i * rows, rows)
            here = pl.ds(s, rows)
            gt, prev, nxt = _taps_chunk(lambda at, n: u_ref[1, pl.ds(at, n), :].astype(F32), s, rows, seq)
            val, da = u_ref[0, here, :].astype(F32), da_scr[here, :]
            gc = prev * w_ref[0:1, :] + gt * w_ref[1:2, :] + nxt * w_ref[2:3, :] + b_ref[...]
            sg = _sigmoid(gc)
            sl = gc * sg
            du_ref[0, here, :] = (da * sl).astype(BF16)
            dgc = (da * val) * (sg * (1.0 + gc * (1.0 - sg)))
            dgc_scr[here, :] = dgc
            cb, c0, c1, c2 = carry
            return cb + fold(dgc), c0 + fold(dgc * prev), c1 + fold(dgc * gt), c2 + fold(dgc * nxt)

        zero = jnp.zeros((SUBLANES, cw), F32)
        cb, c0, c1, c2 = lax.fori_loop(0, seq // rows, chunk, (zero, zero, zero, zero))
        gcb_ref[...] += jnp.sum(cb, axis=0, keepdims=True)
        gcw_ref[0:1, :] += jnp.sum(c0, axis=0, keepdims=True)
        gcw_ref[1:2, :] += jnp.sum(c1, axis=0, keepdims=True)
        gcw_ref[2:3, :] += jnp.sum(c2, axis=0, keepdims=True)

        def chunk2(i, carry):
            s = pl.multiple_of(i * rows, rows)
            dgc, dprev, dnxt = _taps_chunk(lambda at, n: dgc_scr[pl.ds(at, n), :], s, rows, seq)
            du_ref[1, pl.ds(s, rows), :] = (dnxt * w_ref[0:1, :] + dgc * w_ref[1:2, :]
                                            + dprev * w_ref[2:3, :]).astype(BF16)
            return carry

        lax.fori_loop(0, seq // rows, chunk2, 0)

    return _hosted(
        body, rider, name="ffn_backward", grid=(D_FF // cw, batch),
        out_shape=[jax.ShapeDtypeStruct((2, t, D_FF), BF16),
                   jax.ShapeDtypeStruct((3, D_FF), F32), jax.ShapeDtypeStruct((1, D_FF), F32)],
        in_specs=[pl.BlockSpec((seq, d), lambda j, b: (b, 0)), pl.BlockSpec((cw, d), lambda j, b: (j, 0)),
                  pl.BlockSpec((2, seq, cw), lambda j, b: (0, b, j)),
                  pl.BlockSpec((3, cw), lambda j, b: (0, j)), pl.BlockSpec((1, cw), lambda j, b: (0, j))],
        out_specs=[pl.BlockSpec((2, seq, cw), lambda j, b: (0, b, j)),
                   pl.BlockSpec((3, cw), lambda j, b: (0, j)), pl.BlockSpec((1, cw), lambda j, b: (0, j))],
        scratch_shapes=[pltpu.VMEM((seq, cw), F32), pltpu.VMEM((seq, cw), F32)],
        compiler_params=_cparams(("arbitrary", "arbitrary"), VMEM_BIG), args=[dffn, w_down, u, conv_w, conv_b])


def _up_backward(du, w_up, x1, mod3, g_ffn, dx2, mix, seq, rider=None):
    _, t, _ = du.shape
    d = x1.shape[1]
    tm = TOKEN_TILE
    per_seq = seq // tm
    batch = t // seq
    w_a, w_b = w_up
    half, wcol = w_a.shape[1], w_a.shape[2]

    def body(du_ref, wa_ref, wb_ref, x1_ref, mod_ref, g_ref, dx2_ref, mix_ref,
             dx1_ref, dmix_ref, dsh_ref, dsc_ref, dga_ref, dg_ref):
        i = pl.program_id(0)
        parts = []
        for w_ref in (wa_ref, wb_ref):
            acc = jnp.zeros((tm, half), F32)
            for j in range(N_SHARD):
                acc = acc + _mm_nt(du_ref[j // 2, :, (j % 2) * wcol:(j % 2 + 1) * wcol], w_ref[j])
            parts.append(acc)
        dh = jnp.concatenate(parts, axis=1)
        gate_a = mod_ref[0, :, 2 * d:3 * d]
        scale_f = mod_ref[0, :, 4 * d:5 * d]
        r, xn = _rms_stats(x1_ref[...])
        xg = xn * g_ref[...]
        dxg = dh * (1.0 + scale_f)
        dx1 = dx2_ref[...] + _rms_bwd(dxg * g_ref[...], xn, r)
        dx1_ref[...] = dx1
        dmix_ref[...] = (dx1 * gate_a).astype(BF16)

        @pl.when(i == 0)
        def _():
            dg_ref[...] = jnp.zeros_like(dg_ref)

        @pl.when(i % per_seq == 0)
        def _():
            dsh_ref[...] = jnp.zeros_like(dsh_ref)
            dsc_ref[...] = jnp.zeros_like(dsc_ref)
            dga_ref[...] = jnp.zeros_like(dga_ref)

        dg_ref[...] += jnp.sum(dxg * xn, axis=0, keepdims=True)
        dsh_ref[0] += jnp.sum(dh, axis=0, keepdims=True)
        dsc_ref[0] += jnp.sum(dh * xg, axis=0, keepdims=True)
        dga_ref[0] += jnp.sum(dx1 * mix_ref[...], axis=0, keepdims=True)

    tile = lambda w: pl.BlockSpec((tm, w), lambda i: (i, 0))
    per_b = pl.BlockSpec((1, 1, d), lambda i: (i // per_seq, 0, 0))
    small = jax.ShapeDtypeStruct((batch, 1, d), F32)
    return _hosted(
        body, rider, name="up_backward", grid=(t // tm,),
        out_shape=[jax.ShapeDtypeStruct((t, d), F32), jax.ShapeDtypeStruct((t, d), BF16), small, small, small,
                   jax.ShapeDtypeStruct((1, d), F32)],
        in_specs=[pl.BlockSpec((2, tm, D_FF), lambda i: (0, i, 0)),
                  _resident((N_SHARD, half, wcol)), _resident((N_SHARD, half, wcol)), tile(d),
                  pl.BlockSpec((1, 1, 6 * d), lambda i: (i // per_seq, 0, 0)),
                  pl.BlockSpec((1, d), lambda i: (0, 0)), tile(d), tile(d)],
        out_specs=[tile(d), tile(d), per_b, per_b, per_b, pl.BlockSpec((1, d), lambda i: (0, 0))],
        scratch_shapes=[], compiler_params=_cparams(("arbitrary",), VMEM_BIG),
        args=[du, w_a, w_b, x1, mod3, g_ffn, dx2, mix])


def _up_weight_grad(h2, du, rider=None):
    t, d = h2.shape
    tk = 2 * TOKEN_TILE
    wcol = D_FF // 2
    half = d // 2
    n_k = t // tk

    def body(h_ref, du_ref, ga_ref, gb_ref, ga16_ref, gb16_ref):
        k = pl.program_id(1)

        @pl.when(k == 0)
        def _():
            ga_ref[...] = jnp.zeros_like(ga_ref)
            gb_ref[...] = jnp.zeros_like(gb_ref)

        du = du_ref[0]
        ga_ref[0] += _mm_tn(h_ref[:, :half], du)
        gb_ref[0] += _mm_tn(h_ref[:, half:], du)

        @pl.when(k == n_k - 1)
        def _():
            ga16_ref[...] = ga_ref[...].astype(BF16)
            gb16_ref[...] = gb_ref[...].astype(BF16)

    g_spec = pl.BlockSpec((1, half, wcol), lambda j, k: (j, 0, 0))
    f32_out = jax.ShapeDtypeStruct((N_SHARD, half, wcol), F32)
    b16_out = jax.ShapeDtypeStruct((N_SHARD, half, wcol), BF16)
    return _hosted(
        body, rider, name="up_weight_grad", grid=(N_SHARD, n_k),
        out_shape=[f32_out, f32_out, b16_out, b16_out],
        in_specs=[pl.BlockSpec((tk, d), lambda j, k: (k, 0)),
                  pl.BlockSpec((1, tk, wcol), lambda j, k: (j // 2, k, j % 2))],
        out_specs=[g_spec, g_spec, g_spec, g_spec], scratch_shapes=[],
        compiler_params=_cparams(("arbitrary", "arbitrary"), VMEM_BIG), args=[h2, du])


def _out_backward(dmix, w_out, oab, oa, ob, g_na, g_sw):
    t, d = dmix.shape
    tm = 2 * TOKEN_TILE
    hw = NA_WIDTH

    def body(dm_ref, w_ref, oab_ref, oa_ref, ob_ref, gna_ref, gsw_ref,
             doa_ref, dob_ref, gw_ref, gwb_ref, dgna_ref, dgsw_ref):
        @pl.when(pl.program_id(0) == 0)
        def _():
            gw_ref[...] = jnp.zeros_like(gw_ref)
            dgna_ref[...] = jnp.zeros_like(dgna_ref)
            dgsw_ref[...] = jnp.zeros_like(dgsw_ref)

        dm = dm_ref[...]
        gw_ref[...] += _mm_tn(oab_ref[...], dm)

        @pl.when(pl.program_id(0) == t // tm - 1)
        def _():
            gwb_ref[...] = gw_ref[...].astype(BF16)

        do = _mm_nt(dm, w_ref[...])
        for raw_ref, g_ref, dst_ref, dg_ref, lo in ((oa_ref, gna_ref, doa_ref, dgna_ref, 0),
                                                     (ob_ref, gsw_ref, dob_ref, dgsw_ref, hw)):
            r, xn = _rms_stats(raw_ref[...])
            dpart = do[:, lo:lo + hw]
            dg_ref[...] += jnp.sum(dpart * xn, axis=0, keepdims=True)
            dst_ref[...] = _rms_bwd(dpart * g_ref[...], xn, r).astype(BF16)

    tile = lambda w: pl.BlockSpec((tm, w), lambda i: (i, 0))
    vec = lambda w: pl.BlockSpec((1, w), lambda i: (0, 0))
    return pl.pallas_call(
        body, name="out_backward", grid=(t // tm,),
        out_shape=(jax.ShapeDtypeStruct((t, hw), BF16), jax.ShapeDtypeStruct((t, hw), BF16),
                   jax.ShapeDtypeStruct((d, d), F32), jax.ShapeDtypeStruct((d, d), BF16),
                   jax.ShapeDtypeStruct((1, hw), F32), jax.ShapeDtypeStruct((1, hw), F32)),
        in_specs=[tile(d), pl.BlockSpec((d, d), lambda i: (0, 0)), tile(d), tile(hw), tile(hw), vec(hw), vec(hw)],
        out_specs=(tile(hw), tile(hw), pl.BlockSpec((d, d), lambda i: (0, 0)), pl.BlockSpec((d, d), lambda i: (0, 0)),
                   vec(hw), vec(hw)),
        compiler_params=_cparams(("arbitrary",), VMEM_BIG),
    )(dmix, w_out, oab, oa, ob, g_na, g_sw)


def _na_backward(proj, d_o, lse, tiles, batch, seq, rider=None):
    t = proj.shape[0]
    n_rows = seq // GRID_W
    n_pairs = NA_WIDTH // LANES
    win = NA_ROWS * GRID_W
    n_tiles = 2 * NA_ROWS - 2

    def body(q_ref, k_ref, v_ref, do_ref, lse_ref, tp_ref, dq_ref, dk_ref, dv_ref, dtp_ref, km, vm, dk_acc, dv_acc):
        @pl.when(pl.program_id(1) == 0)
        def _():
            dtp_ref[...] = jnp.zeros_like(dtp_ref)

        _na_prepare(k_ref, v_ref, km, vm)
        dk_acc[...] = jnp.zeros_like(dk_acc)
        dv_acc[...] = jnp.zeros_like(dv_acc)
        low = lax.broadcasted_iota(jnp.int32, (win, LANES), 1) < HEAD_DIM

        def scores(r):
            rs, off = _na_window(r, n_rows)
            rows = pl.ds(pl.multiple_of(r * GRID_W, GRID_W), GRID_W)
            wrows = pl.ds(pl.multiple_of(rs * GRID_W, GRID_W), win)
            q, do = q_ref[rows, :], do_ref[rows, :]
            k2 = _na_pair_window(km, wrows)
            s = _na_scores(q, k2, tp_ref, off)
            dp = _mm_nt(do, _na_pair_window(vm, wrows))
            return rows, wrows, off, q, do, k2, s, dp

        def finish(rows, wrows, off, q, do, k2, s, dp):
            p = _pair_probs_from_lse(s, lse_ref[rows, :])
            parts = []
            for h in range(2):
                ph, dph = p[:, h * win:(h + 1) * win], dp[:, h * win:(h + 1) * win]
                dsh = ph * (dph - jnp.sum(ph * dph, axis=-1, keepdims=True))
                for w in range(NA_ROWS // 2):
                    dtp_ref[h, 2 * w - off + (NA_ROWS - 1)] += dsh[:, w * LANES:(w + 1) * LANES]
                parts.append(dsh)
            dsb = (jnp.concatenate(parts, axis=1) * QK_SCALE).astype(BF16)
            dq_ref[rows, :] = _mm(dsb, k2).astype(BF16)
            dk2 = _mm_tn(dsb, q)
            dv2 = _mm_tn(p.astype(BF16), do)
            dk_acc[wrows, :] += jnp.where(low, dk2[:win], dk2[win:])
            dv_acc[wrows, :] += jnp.where(low, dv2[:win], dv2[win:])

        def row_group(i, carry):
            for state in [scores(NA_GROUP * i + j) for j in range(NA_GROUP)]:
                finish(*state)
            return carry

        lax.fori_loop(0, n_rows // NA_GROUP, row_group, 0)
        dk_ref[...] = dk_acc[...].astype(BF16)
        dv_ref[...] = dv_acc[...].astype(BF16)

    blk = lambda off: pl.BlockSpec((seq, LANES), lambda p, b: (b, off + p))
    out = jax.ShapeDtypeStruct((t, NA_WIDTH), BF16)
    return _hosted(
        body, rider, name="na_backward", grid=(n_pairs, batch),
        out_shape=[out, out, out, jax.ShapeDtypeStruct(tiles.shape, F32)],
        in_specs=[blk(0), blk(n_pairs), blk(2 * n_pairs), blk(0), blk(0),
                  pl.BlockSpec((2, n_tiles, GRID_W, LANES), lambda p, b: (p, 0, 0, 0))],
        out_specs=[blk(0), blk(0), blk(0), pl.BlockSpec((2, n_tiles, GRID_W, LANES), lambda p, b: (p, 0, 0, 0))],
        scratch_shapes=[pltpu.VMEM((2, seq, LANES), BF16), pltpu.VMEM((2, seq, LANES), BF16),
                        pltpu.VMEM((seq, LANES), F32), pltpu.VMEM((seq, LANES), F32)],
        compiler_params=_cparams(("arbitrary", "arbitrary")), args=[proj, proj, proj, d_o, lse, tiles])


def _na_bias_grad(dtiles, expand):
    n = dtiles.shape[0]

    def body(t_ref, e_ref, o_ref):
        flat = jnp.concatenate([t_ref[:, qq, :] for qq in range(GRID_W)], axis=1)
        o_ref[...] = lax.dot_general(flat, e_ref[...], (((1,), (1,)), ((), ())),
                                     precision=lax.Precision.HIGHEST, preferred_element_type=F32)

    return pl.pallas_call(
        body, name="na_bias_grad",
        out_shape=jax.ShapeDtypeStruct((n, expand.shape[0]), F32),
        compiler_params=_cparams(vmem=VMEM_BIG),
    )(dtiles, expand)


def _sw_backward(proj, d_o, lse, sink, batch, seq, rider=None):
    t = proj.shape[0]
    n_pairs = SW_WIDTH // LANES
    q_blk = 3 * NA_WIDTH // LANES
    k_blk = q_blk + n_pairs
    n_blocks = seq // SW_BLOCK
    pad = seq + 2 * SW_BLOCK

    def body(sink_ref, q_ref, k_ref, v_ref, do_ref, lse_ref, dq_ref, dk_ref, dv_ref, dsk_ref,
             k_lo, k_hi, v_lo, v_hi, dk_loc, dv_loc, dk_tot, dv_tot):
        hp = pl.program_id(1)
        g = hp // 2
        _sw_prepare(k_ref, g, k_lo, k_hi, seq)
        _sw_prepare(v_ref, g, v_lo, v_hi, seq)
        dk_loc[...] = jnp.zeros_like(dk_loc)
        dv_loc[...] = jnp.zeros_like(dv_loc)

        @pl.when(hp == 0)
        def _():
            dk_tot[...] = jnp.zeros_like(dk_tot)
            dv_tot[...] = jnp.zeros_like(dv_tot)

        band = 3 * SW_BLOCK
        low = lax.broadcasted_iota(jnp.int32, (band, LANES), 1) < HEAD_DIM

        sinks = (sink_ref[2 * hp], sink_ref[2 * hp + 1])

        def scores(n):
            rows = pl.ds(pl.multiple_of(n * SW_BLOCK, SW_BLOCK), SW_BLOCK)
            wrows = pl.ds(pl.multiple_of(n * SW_BLOCK, SW_BLOCK), band)
            qb, do = q_ref[rows, :], do_ref[rows, :]
            k2 = jnp.concatenate([k_lo[wrows, :], k_hi[wrows, :]], axis=0)
            v2 = jnp.concatenate([v_lo[wrows, :], v_hi[wrows, :]], axis=0)
            return n, rows, wrows, qb, do, k2, _mm_nt(qb, k2) * QK_SCALE, _mm_nt(do, v2)

        def finish(sink_acc, n, rows, wrows, qb, do, k2, s2, dp):
            p, ps = _sw_probs_from_lse(s2, _sw_mask(n, seq), sinks, lse_ref[rows, :])
            parts, new = [], []
            for i in range(2):
                ph, dph = p[:, i * band:(i + 1) * band], dp[:, i * band:(i + 1) * band]
                delta = jnp.sum(ph * dph, axis=-1, keepdims=True)
                parts.append(ph * (dph - delta))
                new.append(sink_acc[i] - ps[i] * delta)
            dsb = (jnp.concatenate(parts, axis=1) * QK_SCALE).astype(BF16)
            dq_ref[rows, :] = _mm(dsb, k2)
            dk2 = _mm_tn(dsb, qb)
            dv2 = _mm_tn(p.astype(BF16), do)
            dk_loc[wrows, :] += jnp.where(low, dk2[:band], dk2[band:])
            dv_loc[wrows, :] += jnp.where(low, dv2[:band], dv2[band:])
            return tuple(new)

        def block_group(i, carry):
            for state in [scores(SW_GROUP_BLOCKS * i + j) for j in range(SW_GROUP_BLOCKS)]:
                carry = finish(carry, *state)
            return carry

        zero = jnp.zeros((SW_BLOCK, 1), F32)
        s0, s1 = lax.fori_loop(0, n_blocks // SW_GROUP_BLOCKS, block_group, (zero, zero))
        row = lax.broadcasted_iota(jnp.int32, (SUBLANES, LANES), 0)
        dsk_ref[0, 0] = jnp.where(row == 0, jnp.sum(s0), jnp.where(row == 1, jnp.sum(s1), 0.0))

        lane_s = lax.broadcasted_iota(jnp.int32, (seq, LANES), 1)
        mine_g = (lane_s // HEAD_DIM) == g
        for loc, tot in ((dk_loc, dk_tot), (dv_loc, dv_tot)):
            part = loc[SW_BLOCK:SW_BLOCK + seq, :]
            tot[...] += jnp.where(mine_g, part + pltpu.roll(part, HEAD_DIM, 1), 0.0)

        @pl.when(hp == n_pairs - 1)
        def _():
            dk_ref[...] = dk_tot[...]
            dv_ref[...] = dv_tot[...].astype(BF16)

    return _hosted(
        body, rider, name="sw_backward", grid=(batch, n_pairs),
        out_shape=[jax.ShapeDtypeStruct((t, SW_WIDTH), F32), jax.ShapeDtypeStruct((t, LANES), F32),
                   jax.ShapeDtypeStruct((t, LANES), BF16), jax.ShapeDtypeStruct((batch, n_pairs, SUBLANES, LANES), F32)],
        in_specs=[pl.BlockSpec(memory_space=pltpu.SMEM),
                  pl.BlockSpec((seq, LANES), lambda b, p: (b, q_blk + p)),
                  pl.BlockSpec((seq, LANES), lambda b, p: (b, k_blk)),
                  pl.BlockSpec((seq, LANES), lambda b, p: (b, k_blk + 1)),
                  pl.BlockSpec((seq, LANES), lambda b, p: (b, p)), pl.BlockSpec((seq, LANES), lambda b, p: (b, p))],
        out_specs=[pl.BlockSpec((seq, LANES), lambda b, p: (b, p)), pl.BlockSpec((seq, LANES), lambda b, p: (b, 0)),
                   pl.BlockSpec((seq, LANES), lambda b, p: (b, 0)),
                   pl.BlockSpec((1, 1, SUBLANES, LANES), lambda b, p: (b, p, 0, 0))],
        scratch_shapes=[pltpu.VMEM((pad, LANES), BF16)] * 4 + [pltpu.VMEM((pad, LANES), F32)] * 2
        + [pltpu.VMEM((seq, LANES), F32)] * 2,
        compiler_params=_cparams(("arbitrary", "arbitrary")), args=[sink, proj, proj, proj, d_o, lse])


def _in_backward(dqkv_a, dq_b, dk_b, dv_b, w_in_t, h1, x, mod3, g_attn, dx1, cos_t, sin_t, seq):
    t, d = x.shape
    tm = TOKEN_TILE
    per_seq = seq // tm
    batch = t // seq
    dqa, dka, dva = dqkv_a
    n_q = SW_WIDTH // LANES

    def body(dqa_ref, dka_ref, dva_ref, dqb_ref, dkb_ref, dvb_ref, w_ref, h_ref, x_ref, mod_ref, g_ref, dx1_ref,
             cos_ref, sin_ref, dx_ref, gw_ref, gwb_ref, dsh_ref, dsc_ref, dg_ref):
        i = pl.program_id(0)

        @pl.when(i == 0)
        def _():
            gw_ref[...] = jnp.zeros_like(gw_ref)
            dg_ref[...] = jnp.zeros_like(dg_ref)

        @pl.when(i % per_seq == 0)
        def _():
            dsh_ref[...] = jnp.zeros_like(dsh_ref)
            dsc_ref[...] = jnp.zeros_like(dsc_ref)

        dr = jnp.concatenate([dqb_ref[...], dkb_ref[...]], axis=1)
        cos = jnp.concatenate([cos_ref[...]] * (n_q + 1), axis=1)
        sin = jnp.concatenate([sin_ref[...]] * (n_q + 1), axis=1)
        dr = dr * cos + _rope_rot(dr * sin)
        dproj = jnp.concatenate([dqa_ref[...], dka_ref[...], dva_ref[...], dr.astype(BF16), dvb_ref[...]], axis=1)
        gw_ref[...] += _mm_tn(dproj, h_ref[...])

        @pl.when(i == t // tm - 1)
        def _():
            gwb_ref[...] = gw_ref[...].astype(BF16)

        dh = _mm(dproj, w_ref[...])
        scale = mod_ref[0, :, d:2 * d]
        r, xn = _rms_stats(x_ref[...])
        xg = xn * g_ref[...]
        dxg = dh * (1.0 + scale)
        dx_ref[...] = dx1_ref[...] + _rms_bwd(dxg * g_ref[...], xn, r)
        dg_ref[...] += jnp.sum(dxg * xn, axis=0, keepdims=True)
        dsh_ref[0] += jnp.sum(dh, axis=0, keepdims=True)
        dsc_ref[0] += jnp.sum(dh * xg, axis=0, keepdims=True)

    tile = lambda w: pl.BlockSpec((tm, w), lambda i: (i, 0))
    per_b = pl.BlockSpec((1, 1, d), lambda i: (i // per_seq, 0, 0))
    small = jax.ShapeDtypeStruct((batch, 1, d), F32)
    rope = pl.BlockSpec((tm, LANES), lambda i: (i % per_seq, 0))
    return pl.pallas_call(
        body, name="in_backward", grid=(t // tm,),
        out_shape=(jax.ShapeDtypeStruct((t, d), F32), jax.ShapeDtypeStruct((IN_WIDTH, d), F32),
                   jax.ShapeDtypeStruct((IN_WIDTH, d), BF16), small, small, jax.ShapeDtypeStruct((1, d), F32)),
        in_specs=[tile(NA_WIDTH), tile(NA_WIDTH), tile(NA_WIDTH), tile(SW_WIDTH), tile(LANES), tile(LANES),
                  _resident((IN_WIDTH, d)), tile(d), tile(d),
                  pl.BlockSpec((1, 1, 6 * d), lambda i: (i // per_seq, 0, 0)),
                  pl.BlockSpec((1, d), lambda i: (0, 0)), tile(d), rope, rope],
        out_specs=(tile(d), _resident((IN_WIDTH, d)), _resident((IN_WIDTH, d)),
                   per_b, per_b, pl.BlockSpec((1, d), lambda i: (0, 0))),
        compiler_params=_cparams(("arbitrary",), VMEM_BIG),
    )(dqa, dka, dva, dq_b, dk_b, dv_b, w_in_t, h1, x, mod3, g_attn, dx1, cos_t, sin_t)


def _ada_weight_grad(sc_all, dmod_cols):
    d = sc_all.shape[1]
    ncol = dmod_cols.shape[1]

    def body(s_ref, m_ref, o_ref):
        o_ref[...] = _mm_tn(s_ref[...].astype(BF16), m_ref[...].astype(BF16))

    return pl.pallas_call(
        body, name="ada_weight_grad",
        out_shape=jax.ShapeDtypeStruct((d, ncol), F32),
        compiler_params=_cparams(vmem=VMEM_BIG),
    )(sc_all, dmod_cols)


def _row_tile(rows, cols):
    target = max(SUBLANES, (1 << 20) // (4 * cols))
    best = rows
    for cand in range(SUBLANES, rows + 1, SUBLANES):
        if rows % cand == 0 and cand <= target:
            best = cand
    return best if rows % SUBLANES == 0 else rows


def _sum_slots(parts, name):
    n = len(parts)
    _, rows, cols = parts[0][0].shape
    tr = _row_tile(rows, cols)
    per = rows // tr

    def body(*refs):
        o_ref = refs[-1]
        for q in range(n):
            @pl.when(pl.program_id(0) == q)
            def _(q=q):
                p_ref, own_ref = refs[2 * q], refs[2 * q + 1]
                o_ref[...] = ((own_ref[...] + p_ref[0].astype(F32)) + p_ref[1].astype(F32)) + p_ref[2].astype(F32)

    in_specs, args = [], []
    for q, (recv, own) in enumerate(parts):
        in_specs.append(pl.BlockSpec((N_SHARD - 1, tr, cols), lambda p, i, q=q: (0, jnp.where(p == q, i, 0), 0)))
        in_specs.append(pl.BlockSpec((tr, cols), lambda p, i, q=q: (jnp.where(p == q, i, 0), 0)))
        args += [recv, own]
    return pl.pallas_call(
        body, name=name, grid=(n, per),
        out_shape=jax.ShapeDtypeStruct((n * rows, cols), F32),
        in_specs=in_specs, out_specs=pl.BlockSpec((tr, cols), lambda p, i: (p * per + i, 0)),
        compiler_params=_cparams(("arbitrary", "arbitrary")),
    )(*args)


def _adamw_math(w, g, m, v):
    m2 = ADAM_B1 * m + (1.0 - ADAM_B1) * g
    v2 = ADAM_B2 * v + (1.0 - ADAM_B2) * (g * g)
    m_hat = m2 / (1.0 - ADAM_B1 ** ADAM_STEP)
    v_hat = v2 / (1.0 - ADAM_B2 ** ADAM_STEP)
    return -ADAM_LR * (m_hat / (jnp.sqrt(v_hat) + ADAM_EPS) + ADAM_WD * w), m2, v2


def _small_step(partials, states, dmod, b_ada_state, rider=None):
    n_upd = len(states)
    moving = list(partials) + [dmod]
    n_mov = len(moving)
    all_states = list(states) + [b_ada_state]

    def body(*refs):
        mov, refs = refs[:n_mov], refs[n_mov:]
        wmv, refs = refs[:3 * (n_upd + 1)], refs[3 * (n_upd + 1):]
        res, refs = refs[:4 * (n_upd + 1)], refs[4 * (n_upd + 1):]
        sums_out, refs = refs[:n_mov - n_upd - 1], refs[n_mov - n_upd - 1:]
        dmod_out, refs = refs[0], refs[1:]
        everyone, (ssem, rsem) = refs[:n_mov], refs[n_mov:]
        x, y, c = _my_pos()
        me = 4 * x + 2 * y + c
        cps = []
        for a in range(n_mov):
            everyone[a][me] = mov[a][...]
            for k in range(1, N_DEV):
                peer = (_flip(x, (k >> 2) & 1), _flip(y, (k >> 1) & 1), _flip(c, k & 1))
                cps.append(pltpu.make_async_remote_copy(
                    src_ref=everyone[a].at[me], dst_ref=everyone[a].at[me], send_sem=ssem.at[a, k - 1],
                    recv_sem=rsem.at[a, k - 1], device_id=peer, device_id_type=MESH))
        for cp in cps:
            cp.start()
        for cp in cps:
            cp.wait_recv()

        def total(a):
            acc = everyone[a][0]
            for dev in range(1, N_DEV):
                acc = acc + everyone[a][dev]
            return acc

        grads = [total(a) for a in range(n_upd)]
        grads.append(jnp.sum(total(n_mov - 1), axis=0, keepdims=True))
        for j, g in enumerate(grads):
            delta, m2, v2 = _adamw_math(wmv[3 * j][...], g, wmv[3 * j + 1][...], wmv[3 * j + 2][...])
            res[4 * j][...] = g
            res[4 * j + 1][...] = delta
            res[4 * j + 2][...] = m2
            res[4 * j + 3][...] = v2
        for j in range(n_mov - n_upd - 1):
            sums_out[j][...] = total(n_upd + j)
        dmod_out[...] = everyone[n_mov - 1][...]
        for cp in cps:
            cp.wait_send()

    vm = pl.BlockSpec(memory_space=pltpu.VMEM)
    sds = jax.ShapeDtypeStruct
    out_shape = []
    for w, _, _ in all_states:
        out_shape += [sds(w.shape, F32)] * 4
    out_shape += [sds(p.shape, F32) for p in partials[n_upd:]]
    out_shape.append(sds((N_DEV,) + dmod.shape, F32))
    args = moving + [a for st in all_states for a in st]
    outs, rides = _hosted(
        body, rider, name="small_step", grid=(), out_shape=out_shape,
        in_specs=[vm] * len(args), out_specs=[vm] * len(out_shape),
        scratch_shapes=[pltpu.VMEM((N_DEV,) + a.shape, F32) for a in moving]
        + [pltpu.SemaphoreType.DMA((n_mov, N_DEV - 1)), pltpu.SemaphoreType.DMA((n_mov, N_DEV - 1))],
        compiler_params=_cparams(vmem=VMEM_BIG), args=args)
    return outs, rides


def _adamw(w, grads, m, v, name):
    rows, cols = w.shape
    tr = _row_tile(rows, cols)
    ng = len(grads)

    def body(*refs):
        w_ref = refs[0]
        g_refs = refs[1:1 + ng]
        m_ref, v_ref = refs[1 + ng], refs[2 + ng]
        g_out, d_out, m_out, v_out = refs[3 + ng:]
        g = g_refs[0][...]
        for extra in g_refs[1:]:
            g = g + extra[...]
        g_out[...] = g
        d_out[...], m_out[...], v_out[...] = _adamw_math(w_ref[...], g, m_ref[...], v_ref[...])

    spec = pl.BlockSpec((tr, cols), lambda i: (i, 0))
    out = jax.ShapeDtypeStruct((rows, cols), F32)
    return pl.pallas_call(
        body, name=name, grid=(rows // tr,),
        out_shape=(out, out, out, out),
        in_specs=[spec] * (3 + ng), out_specs=(spec, spec, spec, spec),
        compiler_params=_cparams(("arbitrary",)),
    )(w, *grads, m, v)


def _rope_tables(seq):
    half = HEAD_DIM // 2
    inv = np.float32(ROPE_THETA) ** (-np.arange(half, dtype=np.float32) / np.float32(half))
    ang = (np.arange(seq, dtype=np.float32)[:, None] * inv[None, :]).astype(np.float64)
    cos, sin = np.cos(ang).astype(np.float32), np.sin(ang).astype(np.float32)
    cos_t = np.concatenate([cos, cos, cos, cos], axis=1)
    sin_t = np.concatenate([-sin, sin, -sin, sin], axis=1)
    return jnp.asarray(cos_t), jnp.asarray(sin_t)


def kernel(x, c, w_ada, b_ada, g_attn, w_in, na_rpb, sw_sink, g_na_out, g_sw_out, w_out, g_ffn, w_up, conv_w, conv_b, w_down, g_final, loss_target, m_w_ada, m_b_ada, m_g_attn, m_w_in, m_na_rpb, m_sw_sink, m_g_na_out, m_g_sw_out, m_w_out, m_g_ffn, m_w_up, m_conv_w, m_conv_b, m_w_down, m_g_final, v_w_ada, v_b_ada, v_g_attn, v_w_in, v_na_rpb, v_sw_sink, v_g_na_out, v_g_sw_out, v_w_out, v_g_ffn, v_w_up, v_conv_w, v_conv_b, v_w_down, v_g_final):
    batch, seq, d = x.shape
    t = batch * seq
    assert d == D_MODEL and seq % (NA_ROWS * GRID_W) == 0 and seq % TOKEN_TILE == 0 and batch <= SUBLANES
    shard = 2 * lax.axis_index("x") + lax.axis_index("y")
    xt = x.reshape(t, d)
    tgt = loss_target.reshape(t, d)

    c8 = jnp.pad(c, ((0, SUBLANES - batch), (0, 0)))
    w_in_t_s = jnp.transpose(w_in[0]).astype(BF16)
    (mod8, sc_all), (w_in_g,) = _ada_forward(c8, w_ada[0], b_ada, _Rider("gather", [w_in_t_s]))
    mod3 = mod8[:batch].reshape(batch, 1, 6 * d)
    w_in_t = w_in_g.reshape(IN_WIDTH, d)

    cos_t, sin_t = _rope_tables(seq)
    (h1, proj), _ = _in_proj(xt, mod3, g_attn, w_in_t, cos_t, sin_t, seq)
    n_heads = NA_WIDTH // HEAD_DIM
    n_tiles, n_dc = 2 * NA_ROWS - 2, 2 * NA_COLS - 1
    expand, neg_mask = _na_bias_pattern()
    rpb = na_rpb[0]
    rows2 = jnp.concatenate([rpb[:, :-1, :], rpb[:, 1:, :]], axis=2).reshape(n_heads * n_tiles, 2 * n_dc)
    rows2 = jnp.pad(rows2, ((0, 0), (0, GRID_W - 2 * n_dc)))
    tiles = _na_bias_tiles(rows2, expand, neg_mask).reshape(n_heads, n_tiles, GRID_W, LANES)
    sink = sw_sink[0]
    w_up_b16 = w_up[0].astype(BF16)
    (oa, lse_a), (w_up_a, w_down_g) = _na_forward(proj, tiles, batch, seq,
                                                  _Rider("gather", [w_up_b16[:d // 2], w_down[0].astype(BF16)]))
    (ob, lse_b), (w_up_b, conv_w_g, w_out_g) = _sw_forward(
        proj, sink, batch, seq, _Rider("gather", [w_up_b16[d // 2:], conv_w[0], w_out[0].astype(BF16)]))
    w_up_f = (w_up_a, w_up_b)
    w_out_f = w_out_g.reshape(d, d)
    conv_w_f = jnp.transpose(conv_w_g, (1, 0, 2)).reshape(3, D_FF)
    oab, mix, x1, h2 = _out_proj(oa, ob, g_na_out, g_sw_out, w_out_f, xt, mod3, g_ffn, seq)
    (u,), _ = _up_proj(h2, w_up_f)
    w_down_f = w_down_g.reshape(D_FF, d)
    a = _conv_gate(u, conv_w_f, conv_b, batch, seq)
    dx2, dffn, loss_part, dgate_f, dg_final = _down_and_loss(a, w_down_f, x1, mod3, g_final.reshape(1, d), tgt, seq)

    gw_down, gw_down_b = _down_weight_grad(a, dffn)
    blocks = lambda g, rows: g.reshape(N_SHARD, rows // N_SHARD, d)
    (du, gconv_w, gconv_b), (recv_down, own_down) = _ffn_backward(
        dffn, w_down_f, u, conv_w_f, conv_b, batch, seq,
        _Rider("scatter", [blocks(gw_down_b, D_FF)], [blocks(gw_down, D_FF)]))
    (gw_up_top, gw_up_bot, gw_up_top_b, gw_up_bot_b), _ = _up_weight_grad(h2, du)
    (dx1, dmix, dshift_f, dscale_f, dgate_a, dg_ffn), _ = _up_backward(du, w_up_f, x1, mod3, g_ffn, dx2, mix, seq)
    doa, dob, gw_out, gw_out_b, dg_na, dg_sw = _out_backward(dmix, w_out_f, oab, oa, ob, g_na_out, g_sw_out)
    (dqa, dka, dva, dtiles), (recv_up_bot, own_up_bot) = _na_backward(
        proj, doa, lse_a, tiles, batch, seq, _Rider("scatter", [gw_up_bot_b], [gw_up_bot]))
    (dq_b, dk_b, dv_b, dsink_parts), (recv_out, recv_up_top, own_out, own_up_top) = _sw_backward(
        proj, dob, lse_b, sink, batch, seq,
        _Rider("scatter", [blocks(gw_out_b, d), gw_up_top_b], [blocks(gw_out, d), gw_up_top]))
    gx, gw_in_t, gw_in_b, dshift_a, dscale_a, dg_attn = _in_backward(
        (dqa, dka, dva), dq_b, dk_b, dv_b, w_in_t, h1, xt, mod3, g_attn, dx1, cos_t, sin_t, seq)

    red = _na_bias_grad(dtiles.reshape(n_heads * n_tiles, GRID_W, LANES), expand)[:, :2 * n_dc]
    red = red.reshape(n_heads, n_tiles, 2, n_dc)
    zero_row = jnp.zeros((n_heads, 1, n_dc), F32)
    g_rpb = (jnp.concatenate([red[:, :, 0, :], zero_row], axis=1)
             + jnp.concatenate([zero_row, red[:, :, 1, :]], axis=1))
    g_sink = jnp.sum(dsink_parts[:, :, :2, 0], axis=0).reshape(SW_WIDTH // HEAD_DIM)

    dmod = jnp.concatenate([dshift_a, dscale_a, dgate_a, dshift_f, dscale_f, dgate_f], axis=2).reshape(batch, 6 * d)
    rpb_shape = na_rpb.shape[1:]
    states = [(g_attn, m_g_attn, v_g_attn),
              (na_rpb.reshape(rpb_shape), m_na_rpb.reshape(rpb_shape), v_na_rpb.reshape(rpb_shape)),
              (sw_sink, m_sw_sink, v_sw_sink), (g_na_out, m_g_na_out, v_g_na_out), (g_sw_out, m_g_sw_out, v_g_sw_out),
              (g_ffn, m_g_ffn, v_g_ffn), (conv_b, m_conv_b, v_conv_b),
              (g_final.reshape(1, d), m_g_final.reshape(1, d), v_g_final.reshape(1, d))]
    partials = [dg_attn, g_rpb, g_sink.reshape(sw_sink.shape), dg_na, dg_sw, dg_ffn, gconv_b, dg_final,
                gconv_w, loss_part]
    mine = [None, _sum_slots([(recv_out, own_out)], "sum_w_out"),
            _sum_slots([(recv_up_top, own_up_top), (recv_up_bot, own_up_bot)], "sum_w_up"),
            _sum_slots([(recv_down, own_down)], "sum_w_down")]
    small, (recv_in, own_in, *theirs) = _small_step(
        partials, states, dmod, (b_ada, m_b_ada, v_b_ada),
        _Riders([_Rider("scatter", [blocks(gw_in_b, IN_WIDTH)], [blocks(gw_in_t, IN_WIDTH)]),
                 _Rider("swap", mine[1:])]))
    r_small = [small[4 * j:4 * j + 4] for j in range(len(states) + 1)]
    g_conv_w_full, loss_sum, dmod_all = small[4 * (len(states) + 1):]
    loss = loss_sum[0, 0]
    mine[0] = _sum_slots([(recv_in, own_in)], "sum_w_in")
    theirs = _ride_alone(_Rider("swap", mine[:1]), "swap_sibling") + theirs
    dmod_rows = jnp.pad(dmod_all, ((0, 0), (0, SUBLANES - batch), (0, 0))).reshape(N_DEV * SUBLANES, 6 * d)
    ncol = w_ada.shape[2]
    g_w_ada = _ada_weight_grad(sc_all, lax.dynamic_slice(dmod_rows, (0, shard * ncol), (N_DEV * SUBLANES, ncol)))
    cshard = conv_w.shape[2]
    g_conv_w = lax.dynamic_slice(g_conv_w_full, (0, shard * cshard), (3, cshard))

    def big(w, m, v, g_parts, name):
        shape = w.shape
        outs = _adamw(w[0], g_parts, m[0], v[0], name)
        return [o.reshape(shape) for o in outs]

    r_w_ada = big(w_ada, m_w_ada, v_w_ada, [g_w_ada], "adamw_w_ada")
    r_w_in = [jnp.transpose(o).reshape(w_in.shape) for o in
              _adamw(jnp.transpose(w_in[0]), [mine[0], theirs[0]], jnp.transpose(m_w_in[0]), jnp.transpose(v_w_in[0]),
                     "adamw_w_in")]
    r_w_out = big(w_out, m_w_out, v_w_out, [mine[1], theirs[1]], "adamw_w_out")
    r_w_up = big(w_up, m_w_up, v_w_up, [mine[2], theirs[2]], "adamw_w_up")
    r_w_down = big(w_down, m_w_down, v_w_down, [mine[3], theirs[3]], "adamw_w_down")

    r_conv_w = big(conv_w, m_conv_w, v_conv_w, [g_conv_w], "adamw_conv_w")

    def pick(k):
        ga_, rpb_, sk_, gna_, gsw_, gf_, cb_, gfin_, b_ = [r[k] for r in r_small]
        return [r_w_ada[k], b_, ga_, r_w_in[k], rpb_.reshape(na_rpb.shape), sk_, gna_, gsw_, r_w_out[k], gf_,
                r_w_up[k], r_conv_w[k], cb_, r_w_down[k], gfin_.reshape(d)]

    return (loss, gx.reshape(batch, seq, d), *pick(0), *pick(1), *pick(2), *pick(3))
```

```python
import jax
import jax.numpy as jnp
import numpy as np
from jax import lax
from jax.experimental import pallas as pl
from jax.experimental.pallas import tpu as pltpu

F32 = jnp.float32
BF16 = jnp.bfloat16
MESH = pl.DeviceIdType.MESH

D_MODEL = 1024
HEAD_DIM = 64
NA_WIDTH = 512
SW_WIDTH = 512
SW_KV_WIDTH = 128
IN_WIDTH = 2304
D_FF = 2816
GRID_W = 64
NA_ROWS = 8
NA_COLS = 16
SW_BLOCK = 128
ROPE_THETA = 10000.0
EPS = 1e-6
NEG = -1e30
QK_SCALE = HEAD_DIM ** -0.5

ADAM_LR = 0.001
ADAM_B1 = 0.9
ADAM_B2 = 0.999
ADAM_EPS = 1e-08
ADAM_WD = 0.01
ADAM_STEP = 10

N_SHARD = 4
N_DEV = 8
LANES = 128
SUBLANES = 8
TOKEN_TILE = 512
FF_TILE = 256
CONV_CHUNK = 512
NA_GROUP = 8
SW_GROUP_BLOCKS = 8
VMEM_BIG = 56 * 1024 * 1024


def _mm(a, b):
    return jnp.dot(a, b, preferred_element_type=F32)


def _mm_nt(a, b):
    return lax.dot_general(a, b, (((1,), (1,)), ((), ())), preferred_element_type=F32)


def _mm_tn(a, b):
    return lax.dot_general(a, b, (((0,), (0,)), ((), ())), preferred_element_type=F32)


def _cparams(sem=None, vmem=None):
    kw = {}
    if sem is not None:
        kw["dimension_semantics"] = sem
    if vmem is not None:
        kw["vmem_limit_bytes"] = vmem
    return pltpu.CompilerParams(**kw)


def _resident(shape):
    return pl.BlockSpec(shape, lambda i: (0,) * len(shape), pipeline_mode=pl.Buffered(1))


def _sigmoid(x):
    return 1.0 / (1.0 + jnp.exp(-x))


def _rms_stats(x):
    r = lax.rsqrt(jnp.mean(x * x, axis=-1, keepdims=True) + EPS)
    return r, x * r


def _rms_bwd(dxn, xn, r):
    return r * (dxn - xn * jnp.mean(dxn * xn, axis=-1, keepdims=True))


def _my_pos():
    return lax.axis_index("x"), lax.axis_index("y"), lax.axis_index("c")


def _flip(v, bit):
    return 1 - v if bit else v


def _ada_forward(c8, w_ada, b_ada, rider):
    d = c8.shape[1]
    ncol = w_ada.shape[1]

    def body(c_ref, w_ref, b_ref, mod_ref, sc_ref, m_scr, mod_buf, ssem, rsem, ssem2, rsem2):
        x, y, c = _my_pos()
        me = 4 * x + 2 * y + c
        shard = 2 * x + y
        cv = c_ref[...]
        my_rows = pl.ds(pl.multiple_of(me * SUBLANES, SUBLANES), SUBLANES)
        sc_ref[my_rows, :] = cv * _sigmoid(cv)

        def copy1(k):
            peer = (_flip(x, (k >> 2) & 1), _flip(y, (k >> 1) & 1), _flip(c, k & 1))
            return pltpu.make_async_remote_copy(
                src_ref=sc_ref.at[my_rows, :], dst_ref=sc_ref.at[my_rows, :],
                send_sem=ssem.at[k - 1], recv_sem=rsem.at[k - 1], device_id=peer, device_id_type=MESH)

        sends = [copy1(k) for k in range(1, N_DEV)]
        for cp in sends:
            cp.start()
        for cp in sends:
            cp.wait_recv()
        m_scr[...] = _mm(sc_ref[...].astype(BF16), w_ref[...].astype(BF16))

        def copy2(k):
            px, py = _flip(x, (k >> 1) & 1), _flip(y, k & 1)
            rows = pl.ds(pl.multiple_of((4 * px + 2 * py + c) * SUBLANES, SUBLANES), SUBLANES)
            return pltpu.make_async_remote_copy(
                src_ref=m_scr.at[rows, :], dst_ref=mod_buf.at[shard],
                send_sem=ssem2.at[k - 1], recv_sem=rsem2.at[k - 1], device_id=(px, py, c), device_id_type=MESH)

        sends2 = [copy2(k) for k in range(1, N_SHARD)]
        for cp in sends2:
            cp.start()
        mod_buf[shard] = m_scr[my_rows, :]
        for cp in sends2:
            cp.wait_recv()
        for s in range(N_SHARD):
            mod_ref[:, s * ncol:(s + 1) * ncol] = mod_buf[s] + b_ref[:, s * ncol:(s + 1) * ncol]
        for cp in sends + sends2:
            cp.wait_send()

    vm = pl.BlockSpec(memory_space=pltpu.VMEM)
    return _hosted(
        body, rider, name="ada_forward", grid=(),
        out_shape=(jax.ShapeDtypeStruct((SUBLANES, N_SHARD * ncol), F32),
                   jax.ShapeDtypeStruct((N_DEV * SUBLANES, d), F32)),
        in_specs=[vm, vm, vm], out_specs=(vm, vm),
        scratch_shapes=[pltpu.VMEM((N_DEV * SUBLANES, ncol), F32), pltpu.VMEM((N_SHARD, SUBLANES, ncol), F32),
                        pltpu.SemaphoreType.DMA((N_DEV - 1,)), pltpu.SemaphoreType.DMA((N_DEV - 1,)),
                        pltpu.SemaphoreType.DMA((N_SHARD - 1,)), pltpu.SemaphoreType.DMA((N_SHARD - 1,))],
        compiler_params=_cparams(vmem=VMEM_BIG), args=[c8, w_ada, b_ada])


class _Rider:
    def __init__(self, kind, srcs, owns=()):
        self.kind, self.srcs, self.owns = kind, list(srcs), list(owns)
        n = len(self.srcs)
        sds = jax.ShapeDtypeStruct
        dma = pltpu.SemaphoreType.DMA
        if kind == "gather":
            self.out_shapes = [sds((N_SHARD,) + s.shape, s.dtype) for s in self.srcs]
            self.sems = [dma((n, N_SHARD - 1)), dma((n, N_SHARD - 1)), dma((n, N_SHARD - 1)), dma((n, N_SHARD - 1)),
                         dma((n,)), dma((n,))]
        elif kind == "scatter":
            self.out_shapes = ([sds((N_SHARD - 1,) + s.shape[1:], s.dtype) for s in self.srcs]
                               + [sds(o.shape[1:], o.dtype) for o in self.owns])
            m = max(len(self.owns), 1)
            self.sems = [dma((n, N_SHARD - 1)), dma((n, N_SHARD - 1)), dma((m,)), dma((m,))]
        else:
            self.out_shapes = [sds(s.shape, s.dtype) for s in self.srcs]
            self.sems = [dma((n,)), dma((n,))]

    @property
    def inputs(self):
        return self.srcs + self.owns

    def _halved(self, i):
        a = self.srcs[i]
        tile_rows = SUBLANES * (4 // jnp.dtype(a.dtype).itemsize)
        return self.kind == "gather" and a.shape[0] % (2 * tile_rows) == 0

    def copies(self, ins, outs, sems):
        n = len(self.srcs)
        x, y, c = _my_pos()
        shard = 2 * x + y
        remote, relay = [], []
        if self.kind == "swap":
            ssem, rsem = sems
            for i in range(n):
                remote.append(pltpu.make_async_remote_copy(
                    src_ref=ins[i], dst_ref=outs[i], send_sem=ssem.at[i], recv_sem=rsem.at[i],
                    device_id=(x, y, 1 - c), device_id_type=MESH))
            return remote, relay
        if self.kind == "gather":
            ssem, rsem, ssem2, rsem2, sib_s, sib_r = sems
        else:
            ssem, rsem, sib_s, sib_r = sems
        for i in range(n):
            if self.kind == "gather":
                remote.append(pltpu.make_async_remote_copy(
                    src_ref=ins[i], dst_ref=outs[i].at[shard], send_sem=sib_s.at[i], recv_sem=sib_r.at[i],
                    device_id=(x, y, 1 - c), device_id_type=MESH))
                half = ins[i].shape[0] // 2
                mine = pl.ds(pl.multiple_of(c * half, half), half) if self._halved(i) else None
            for k in range(1, N_SHARD):
                px, py = _flip(x, (k >> 1) & 1), _flip(y, k & 1)
                if self.kind == "gather":
                    src, dst = ins[i], outs[i].at[shard]
                    if mine is not None:
                        src, dst = src.at[mine], dst.at[mine]
                        got = outs[i].at[2 * px + py].at[mine]
                        relay.append(pltpu.make_async_remote_copy(
                            src_ref=got, dst_ref=got, send_sem=ssem2.at[i, k - 1], recv_sem=rsem2.at[i, k - 1],
                            device_id=(x, y, 1 - c), device_id_type=MESH))
                else:
                    src, dst = ins[i].at[2 * px + py], outs[i].at[k - 1]
                remote.append(pltpu.make_async_remote_copy(
                    src_ref=src, dst_ref=dst, send_sem=ssem.at[i, k - 1], recv_sem=rsem.at[i, k - 1],
                    device_id=(px, py, c), device_id_type=MESH))
        if self.kind == "scatter":
            for i in range(len(self.owns)):
                remote.append(pltpu.make_async_remote_copy(
                    src_ref=ins[n + i].at[shard], dst_ref=outs[n + i], send_sem=sib_s.at[i], recv_sem=sib_r.at[i],
                    device_id=(x, y, 1 - c), device_id_type=MESH))
        return remote, relay

    def start(self, ins, outs, sems):
        remote, _ = self.copies(ins, outs, sems)
        for cp in remote:
            cp.start()

    def wait(self, ins, outs, sems):
        remote, relay = self.copies(ins, outs, sems)
        for cp in remote:
            cp.wait_recv()
        for cp in relay:
            cp.start()
        for cp in relay:
            cp.wait_recv()
        for cp in remote + relay:
            cp.wait_send()


class _Riders:
    def __init__(self, riders):
        self.riders = list(riders)
        self.inputs = [a for r in self.riders for a in r.inputs]
        self.out_shapes = [s for r in self.riders for s in r.out_shapes]
        self.sems = [s for r in self.riders for s in r.sems]

    def _split(self, ins, outs, sems):
        for r in self.riders:
            ni, no, ns = len(r.inputs), len(r.out_shapes), len(r.sems)
            yield r, ins[:ni], outs[:no], sems[:ns]
            ins, outs, sems = ins[ni:], outs[no:], sems[ns:]

    def start(self, ins, outs, sems):
        for r, i, o, s in self._split(ins, outs, sems):
            r.start(i, o, s)

    def wait(self, ins, outs, sems):
        for r, i, o, s in self._split(ins, outs, sems):
            r.wait(i, o, s)


def _hosted(body, rider, *, name, grid, out_shape, in_specs, out_specs, scratch_shapes, compiler_params, args):
    out_shape, out_specs = list(out_shape), list(out_specs)
    if rider is None:
        outs = pl.pallas_call(body, name=name, grid=grid, out_shape=tuple(out_shape), in_specs=list(in_specs),
                              out_specs=tuple(out_specs), scratch_shapes=list(scratch_shapes),
                              compiler_params=compiler_params)(*args)
        return list(outs), []
    n_in, n_out, n_scr = len(in_specs), len(out_shape), len(scratch_shapes)
    nr_in, nr_out = len(rider.inputs), len(rider.out_shapes)
    n_steps = 1
    for size in grid:
        n_steps *= size

    def full(*refs):
        ins, refs = refs[:n_in], refs[n_in:]
        r_in, refs = refs[:nr_in], refs[nr_in:]
        outs, refs = refs[:n_out], refs[n_out:]
        r_out, refs = refs[:nr_out], refs[nr_out:]
        scr, sems = refs[:n_scr], refs[n_scr:]
        if grid:
            step = 0
            for ax, size in enumerate(grid):
                step = step * size + pl.program_id(ax)
            pl.when(step == 0)(lambda: rider.start(r_in, r_out, sems))
            body(*ins, *outs, *scr)
            pl.when(step == n_steps - 1)(lambda: rider.wait(r_in, r_out, sems))
        else:
            rider.start(r_in, r_out, sems)
            body(*ins, *outs, *scr)
            rider.wait(r_in, r_out, sems)

    hbm = pl.BlockSpec(memory_space=pl.ANY)
    res = pl.pallas_call(
        full, name=name, grid=grid, out_shape=tuple(out_shape + rider.out_shapes),
        in_specs=list(in_specs) + [hbm] * nr_in, out_specs=tuple(out_specs + [hbm] * nr_out),
        scratch_shapes=list(scratch_shapes) + rider.sems, compiler_params=compiler_params,
    )(*args, *rider.inputs)
    return list(res[:n_out]), list(res[n_out:])


def _ride_alone(rider, name):
    return _hosted(lambda: None, rider, name=name, grid=(), out_shape=[], in_specs=[], out_specs=[], scratch_shapes=[],
                   compiler_params=_cparams(), args=[])[1]


def _rope_rot(t):
    w = t.shape[1]
    lane = lax.broadcasted_iota(jnp.int32, t.shape, 1)
    first = (lane % HEAD_DIM) < (HEAD_DIM // 2)
    return jnp.where(first, pltpu.roll(t, w - HEAD_DIM // 2, 1), pltpu.roll(t, HEAD_DIM // 2, 1))


def _in_proj(x, mod3, g_attn, w_in_t, cos_t, sin_t, seq, rider=None):
    t, d = x.shape
    tm = 2 * TOKEN_TILE
    per_seq = seq // tm
    rope_lo, rope_hi = 3 * NA_WIDTH, 3 * NA_WIDTH + SW_WIDTH + SW_KV_WIDTH
    n_rep = (rope_hi - rope_lo) // LANES

    def body(x_ref, mod_ref, g_ref, w_ref, cos_ref, sin_ref, h_ref, p_ref):
        r, xn = _rms_stats(x_ref[...])
        shift, scale = mod_ref[0, :, 0:d], mod_ref[0, :, d:2 * d]
        hb = ((xn * g_ref[...]) * (1.0 + scale) + shift).astype(BF16)
        h_ref[...] = hb
        p_ref[:, :rope_lo] = _mm_nt(hb, w_ref[:rope_lo, :]).astype(BF16)
        pr = _mm_nt(hb, w_ref[rope_lo:rope_hi, :])
        cos = jnp.concatenate([cos_ref[...]] * n_rep, axis=1)
        sin = jnp.concatenate([sin_ref[...]] * n_rep, axis=1)
        p_ref[:, rope_lo:rope_hi] = (pr * cos + _rope_rot(pr) * sin).astype(BF16)
        p_ref[:, rope_hi:] = _mm_nt(hb, w_ref[rope_hi:, :]).astype(BF16)

    return _hosted(
        body, rider, name="in_proj", grid=(t // tm,),
        out_shape=[jax.ShapeDtypeStruct((t, d), BF16), jax.ShapeDtypeStruct((t, IN_WIDTH), BF16)],
        in_specs=[pl.BlockSpec((tm, d), lambda i: (i, 0)),
                  pl.BlockSpec((1, 1, 6 * d), lambda i: (i // per_seq, 0, 0)),
                  pl.BlockSpec((1, d), lambda i: (0, 0)),
                  pl.BlockSpec((IN_WIDTH, d), lambda i: (0, 0)),
                  pl.BlockSpec((tm, LANES), lambda i: (i % per_seq, 0)),
                  pl.BlockSpec((tm, LANES), lambda i: (i % per_seq, 0))],
        out_specs=[pl.BlockSpec((tm, d), lambda i: (i, 0)), pl.BlockSpec((tm, IN_WIDTH), lambda i: (i, 0))],
        scratch_shapes=[], compiler_params=_cparams(("arbitrary",), VMEM_BIG),
        args=[x, mod3, g_attn, w_in_t, cos_t, sin_t])


def _na_bias_pattern():
    n_dc = 2 * NA_COLS - 1
    j = np.arange(GRID_W)[:, None]
    m = np.arange(GRID_W * LANES)[None, :]
    q, lane = m // LANES, m % LANES
    k = lane % GRID_W
    cs = np.clip(q - NA_COLS // 2, 0, GRID_W - NA_COLS)
    ok = (k >= cs) & (k < cs + NA_COLS)
    hit = ok & (j < 2 * n_dc) & (lane // GRID_W == j // n_dc) & (k - q + (NA_COLS - 1) == j % n_dc)
    return jnp.asarray(hit.astype(np.float32)), jnp.asarray(np.where(ok, 0.0, NEG).astype(np.float32))


def _na_bias_tiles(rows2, expand, mask):
    n, width = rows2.shape[0], expand.shape[1]
    q_step = 16
    step = q_step * LANES

    def body(r_ref, e_ref, m_ref, o_ref):
        flat = jnp.dot(r_ref[...], e_ref[...], precision=lax.Precision.HIGHEST,
                       preferred_element_type=F32) + m_ref[...]
        for qq in range(q_step):
            o_ref[:, qq, :] = flat[:, qq * LANES:(qq + 1) * LANES]

    return pl.pallas_call(
        body, name="na_bias_tiles", grid=(width // step,),
        out_shape=jax.ShapeDtypeStruct((n, GRID_W, LANES), F32),
        in_specs=[pl.BlockSpec(rows2.shape, lambda i: (0, 0)), pl.BlockSpec((expand.shape[0], step), lambda i: (0, i)),
                  pl.BlockSpec((1, step), lambda i: (0, i))],
        out_specs=pl.BlockSpec((n, q_step, LANES), lambda i: (0, i, 0)),
        compiler_params=_cparams(("arbitrary",)),
    )(rows2, expand, mask)


def _na_prepare(k_ref, v_ref, km, vm):
    lane = lax.broadcasted_iota(jnp.int32, k_ref.shape, 1)
    low = lane < HEAD_DIM
    kv = k_ref[...]
    vv = v_ref[...]
    zero = jnp.zeros_like(kv)
    km[0] = jnp.where(low, kv, zero)
    km[1] = jnp.where(low, zero, kv)
    vm[0] = jnp.where(low, vv, zero)
    vm[1] = jnp.where(low, zero, vv)


def _na_window(r, n_rows):
    rs = jnp.clip(r - NA_ROWS // 2, 0, n_rows - NA_ROWS)
    return rs, r - rs


def _na_pair_window(ref, wrows):
    return jnp.concatenate([ref[0, wrows, :], ref[1, wrows, :]], axis=0)


def _na_scores(q, k2, tp_ref, off):
    bias = jnp.concatenate([tp_ref[h, 2 * w - off + (NA_ROWS - 1)] for h in range(2) for w in range(NA_ROWS // 2)],
                           axis=1)
    return _mm_nt(q, k2) * QK_SCALE + bias


def _pair_lse_block(lse):
    lane = lax.broadcasted_iota(jnp.int32, (lse[0].shape[0], LANES), 1)
    return jnp.where(lane < HEAD_DIM, lse[0], lse[1])


def _pair_softmax(s):
    win = s.shape[1] // 2
    halves, lse = [], []
    for h in range(2):
        sh = s[:, h * win:(h + 1) * win]
        m = jnp.max(sh, axis=-1, keepdims=True)
        e = jnp.exp(sh - m)
        l = jnp.sum(e, axis=-1, keepdims=True)
        halves.append(e / l)
        lse.append(m + jnp.log(l))
    return jnp.concatenate(halves, axis=1), _pair_lse_block(lse)


def _pair_probs_from_lse(s, lse_block):
    win = s.shape[1] // 2
    return jnp.concatenate([jnp.exp(s[:, h * win:(h + 1) * win] - lse_block[:, h * HEAD_DIM:h * HEAD_DIM + 1])
                            for h in range(2)], axis=1)


def _na_forward(proj, tiles, batch, seq, rider=None):
    t = proj.shape[0]
    n_rows = seq // GRID_W
    n_pairs = NA_WIDTH // LANES
    win = NA_ROWS * GRID_W

    def body(q_ref, k_ref, v_ref, tp_ref, o_ref, lse_ref, km, vm):
        _na_prepare(k_ref, v_ref, km, vm)

        def scores(r):
            rs, off = _na_window(r, n_rows)
            rows = pl.ds(pl.multiple_of(r * GRID_W, GRID_W), GRID_W)
            wrows = pl.ds(pl.multiple_of(rs * GRID_W, GRID_W), win)
            return rows, wrows, _na_scores(q_ref[rows, :], _na_pair_window(km, wrows), tp_ref, off)

        def finish(rows, wrows, s):
            p, lse = _pair_softmax(s)
            lse_ref[rows, :] = lse
            o_ref[rows, :] = _mm(p.astype(BF16), _na_pair_window(vm, wrows))

        def row_group(i, carry):
            for state in [scores(NA_GROUP * i + j) for j in range(NA_GROUP)]:
                finish(*state)
            return carry

        lax.fori_loop(0, n_rows // NA_GROUP, row_group, 0)

    return _hosted(
        body, rider, name="na_forward", grid=(batch, n_pairs),
        out_shape=[jax.ShapeDtypeStruct((t, NA_WIDTH), F32), jax.ShapeDtypeStruct((t, NA_WIDTH), F32)],
        in_specs=[pl.BlockSpec((seq, LANES), lambda b, p: (b, p)),
                  pl.BlockSpec((seq, LANES), lambda b, p: (b, n_pairs + p)),
                  pl.BlockSpec((seq, LANES), lambda b, p: (b, 2 * n_pairs + p)),
                  pl.BlockSpec((2, 2 * NA_ROWS - 2, GRID_W, LANES), lambda b, p: (p, 0, 0, 0))],
        out_specs=[pl.BlockSpec((seq, LANES), lambda b, p: (b, p)), pl.BlockSpec((seq, LANES), lambda b, p: (b, p))],
        scratch_shapes=[pltpu.VMEM((2, seq, LANES), BF16), pltpu.VMEM((2, seq, LANES), BF16)],
        compiler_params=_cparams(("arbitrary", "arbitrary")), args=[proj, proj, proj, tiles])


def _sw_prepare(kv_ref, g, dst_lo, dst_hi, seq):
    lane = lax.broadcasted_iota(jnp.int32, kv_ref.shape, 1)
    mine = (lane // HEAD_DIM) == g
    kg = jnp.where(mine, kv_ref[...].astype(F32), 0.0)
    kr = pltpu.roll(kg, HEAD_DIM, 1)
    first = g == 0
    zero = jnp.zeros((SW_BLOCK, LANES), BF16)
    for dst, val in ((dst_lo, jnp.where(first, kg, kr)), (dst_hi, jnp.where(first, kr, kg))):
        dst[0:SW_BLOCK, :] = zero
        dst[SW_BLOCK:SW_BLOCK + seq, :] = val.astype(BF16)
        dst[SW_BLOCK + seq:, :] = zero


def _sw_mask(n, seq):
    qi = lax.broadcasted_iota(jnp.int32, (SW_BLOCK, 3 * SW_BLOCK), 0)
    kj = lax.broadcasted_iota(jnp.int32, (SW_BLOCK, 3 * SW_BLOCK), 1)
    kpos = n * SW_BLOCK - SW_BLOCK + kj
    return (jnp.abs(qi + SW_BLOCK - kj) <= SW_BLOCK) & (kpos >= 0) & (kpos < seq)


def _sw_probs(s2, ok, sinks):
    band = s2.shape[1] // 2
    halves, lse = [], []
    for i in range(2):
        s = jnp.where(ok, s2[:, i * band:(i + 1) * band], NEG)
        m = jnp.maximum(jnp.max(s, axis=-1, keepdims=True), sinks[i])
        p = jnp.exp(s - m)
        den = jnp.sum(p, axis=-1, keepdims=True) + jnp.exp(sinks[i] - m)
        halves.append(p / den)
        lse.append(m + jnp.log(den))
    return jnp.concatenate(halves, axis=1), _pair_lse_block(lse)


def _sw_probs_from_lse(s2, ok, sinks, lse_block):
    band = s2.shape[1] // 2
    halves, sink_p = [], []
    for i in range(2):
        lse = lse_block[:, i * HEAD_DIM:i * HEAD_DIM + 1]
        halves.append(jnp.exp(jnp.where(ok, s2[:, i * band:(i + 1) * band], NEG) - lse))
        sink_p.append(jnp.exp(sinks[i] - lse))
    return jnp.concatenate(halves, axis=1), sink_p


def _sw_forward(proj, sink, batch, seq, rider=None):
    t = proj.shape[0]
    n_pairs = SW_WIDTH // LANES
    q_blk = 3 * NA_WIDTH // LANES
    k_blk = q_blk + n_pairs
    n_blocks = seq // SW_BLOCK
    pad = seq + 2 * SW_BLOCK

    def body(sink_ref, q_ref, k_ref, v_ref, o_ref, lse_ref, k_lo, k_hi, v_lo, v_hi):
        hp = pl.program_id(1)
        g = hp // 2
        _sw_prepare(k_ref, g, k_lo, k_hi, seq)
        _sw_prepare(v_ref, g, v_lo, v_hi, seq)

        sinks = (sink_ref[2 * hp], sink_ref[2 * hp + 1])

        def scores(n):
            rows = pl.ds(pl.multiple_of(n * SW_BLOCK, SW_BLOCK), SW_BLOCK)
            wrows = pl.ds(pl.multiple_of(n * SW_BLOCK, SW_BLOCK), 3 * SW_BLOCK)
            k2 = jnp.concatenate([k_lo[wrows, :], k_hi[wrows, :]], axis=0)
            return n, rows, wrows, _mm_nt(q_ref[rows, :], k2) * QK_SCALE

        def finish(n, rows, wrows, s2):
            p, lse = _sw_probs(s2, _sw_mask(n, seq), sinks)
            lse_ref[rows, :] = lse
            v2 = jnp.concatenate([v_lo[wrows, :], v_hi[wrows, :]], axis=0)
            o_ref[rows, :] = _mm(p.astype(BF16), v2)

        def block_group(i, carry):
            for state in [scores(SW_GROUP_BLOCKS * i + j) for j in range(SW_GROUP_BLOCKS)]:
                finish(*state)
            return carry

        lax.fori_loop(0, n_blocks // SW_GROUP_BLOCKS, block_group, 0)

    return _hosted(
        body, rider, name="sw_forward", grid=(batch, n_pairs),
        out_shape=[jax.ShapeDtypeStruct((t, SW_WIDTH), F32), jax.ShapeDtypeStruct((t, SW_WIDTH), F32)],
        in_specs=[pl.BlockSpec(memory_space=pltpu.SMEM),
                  pl.BlockSpec((seq, LANES), lambda b, p: (b, q_blk + p)),
                  pl.BlockSpec((seq, LANES), lambda b, p: (b, k_blk)),
                  pl.BlockSpec((seq, LANES), lambda b, p: (b, k_blk + 1))],
        out_specs=[pl.BlockSpec((seq, LANES), lambda b, p: (b, p)), pl.BlockSpec((seq, LANES), lambda b, p: (b, p))],
        scratch_shapes=[pltpu.VMEM((pad, LANES), BF16)] * 4,
        compiler_params=_cparams(("arbitrary", "arbitrary")), args=[sink, proj, proj, proj])


def _out_proj(oa, ob, g_na, g_sw, w_out, x, mod3, g_ffn, seq):
    t, d = x.shape
    tm = TOKEN_TILE
    per_seq = seq // tm

    def body(oa_ref, ob_ref, gna_ref, gsw_ref, w_ref, x_ref, mod_ref, gf_ref, oab_ref, mix_ref, x1_ref, h2_ref):
        _, na = _rms_stats(oa_ref[...])
        _, nb = _rms_stats(ob_ref[...])
        oab = jnp.concatenate([na * gna_ref[...], nb * gsw_ref[...]], axis=1).astype(BF16)
        oab_ref[...] = oab
        mix = _mm(oab, w_ref[...])
        mix_ref[...] = mix
        gate_a = mod_ref[0, :, 2 * d:3 * d]
        shift_f, scale_f = mod_ref[0, :, 3 * d:4 * d], mod_ref[0, :, 4 * d:5 * d]
        x1 = x_ref[...] + gate_a * mix
        x1_ref[...] = x1
        _, xn = _rms_stats(x1)
        h2_ref[...] = ((xn * gf_ref[...]) * (1.0 + scale_f) + shift_f).astype(BF16)

    tile = lambda w: pl.BlockSpec((tm, w), lambda i: (i, 0))
    vec = lambda w: pl.BlockSpec((1, w), lambda i: (0, 0))
    return pl.pallas_call(
        body, name="out_proj", grid=(t // tm,),
        out_shape=(jax.ShapeDtypeStruct((t, d), BF16), jax.ShapeDtypeStruct((t, d), F32),
                   jax.ShapeDtypeStruct((t, d), F32), jax.ShapeDtypeStruct((t, d), BF16)),
        in_specs=[tile(NA_WIDTH), tile(SW_WIDTH), vec(NA_WIDTH), vec(SW_WIDTH),
                  pl.BlockSpec((d, d), lambda i: (0, 0)), tile(d),
                  pl.BlockSpec((1, 1, 6 * d), lambda i: (i // per_seq, 0, 0)), vec(d)],
        out_specs=(tile(d), tile(d), tile(d), tile(d)),
        compiler_params=_cparams(("arbitrary",), VMEM_BIG),
    )(oa, ob, g_na, g_sw, w_out, x, mod3, g_ffn)


def _up_proj(h2, w_up_halves, rider=None):
    t, d = h2.shape
    tm = 2 * TOKEN_TILE
    w_a, w_b = w_up_halves
    half, wcol = w_a.shape[1], w_a.shape[2]

    def body(h_ref, wa_ref, wb_ref, u_ref):
        u_ref[0] = (_mm(h_ref[:, :half], wa_ref[0]) + _mm(h_ref[:, half:], wb_ref[0])).astype(BF16)

    w_spec = pl.BlockSpec((1, half, wcol), lambda j, i: (j, 0, 0))
    return _hosted(
        body, rider, name="up_proj", grid=(N_SHARD, t // tm),
        out_shape=[jax.ShapeDtypeStruct((2, t, D_FF), BF16)],
        in_specs=[pl.BlockSpec((tm, d), lambda j, i: (i, 0)), w_spec, w_spec],
        out_specs=[pl.BlockSpec((1, tm, wcol), lambda j, i: (j // 2, i, j % 2))],
        scratch_shapes=[], compiler_params=_cparams(("arbitrary", "arbitrary"), VMEM_BIG), args=[h2, w_a, w_b])


def _taps_chunk(load, s, rows, seq):
    halo = 2 * SUBLANES
    cur = load(s, rows)
    above = load(pl.multiple_of(jnp.maximum(s - halo, 0), halo), halo)
    below = load(pl.multiple_of(jnp.minimum(s + rows, seq - halo), halo), halo)
    up = jnp.where(s > 0, above[halo - 1:halo, :], 0.0)
    dn = jnp.where(s + rows < seq, below[0:1, :], 0.0)
    row = lax.broadcasted_iota(jnp.int32, cur.shape, 0)
    prev = jnp.where(row == 0, up, pltpu.roll(cur, 1, 0))
    nxt = jnp.where(row == rows - 1, dn, pltpu.roll(cur, rows - 1, 0))
    return cur, prev, nxt


def _conv_gate(u, conv_w, conv_b, batch, seq):
    t = u.shape[1]
    cw = FF_TILE
    rows = CONV_CHUNK

    def body(u_ref, w_ref, b_ref, a_ref):
        def chunk(i, carry):
            s = pl.multiple_of(i * rows, rows)
            gt, prev, nxt = _taps_chunk(lambda at, n: u_ref[1, pl.ds(at, n), :].astype(F32), s, rows, seq)
            gc = prev * w_ref[0:1, :] + gt * w_ref[1:2, :] + nxt * w_ref[2:3, :] + b_ref[...]
            a_ref[pl.ds(s, rows), :] = ((gc * _sigmoid(gc)) * u_ref[0, pl.ds(s, rows), :].astype(F32)).astype(BF16)
            return carry

        lax.fori_loop(0, seq // rows, chunk, 0)

    return pl.pallas_call(
        body, name="conv_gate", grid=(batch, D_FF // cw),
        out_shape=jax.ShapeDtypeStruct((t, D_FF), BF16),
        in_specs=[pl.BlockSpec((2, seq, cw), lambda b, j: (0, b, j)),
                  pl.BlockSpec((3, cw), lambda b, j: (0, j)), pl.BlockSpec((1, cw), lambda b, j: (0, j))],
        out_specs=pl.BlockSpec((seq, cw), lambda b, j: (b, j)),
        compiler_params=_cparams(("arbitrary", "arbitrary"), VMEM_BIG),
    )(u, conv_w, conv_b)


def _down_and_loss(a, w_down, x1, mod3, g_final, target, seq):
    t, d = x1.shape
    tm = TOKEN_TILE
    per_seq = seq // tm
    batch = t // seq

    def body(a_ref, w_ref, x1_ref, mod_ref, g_ref, tgt_ref, dx2_ref, dffn_ref, loss_ref, dgate_ref, dg_ref):
        i = pl.program_id(0)
        f = _mm(a_ref[...], w_ref[...])
        gate_f = mod_ref[0, :, 5 * d:6 * d]
        x2 = x1_ref[...] + gate_f * f
        r, xn = _rms_stats(x2)
        err = xn * g_ref[...] - tgt_ref[...]
        part = 0.5 * jnp.sum(jnp.mean(err * err, axis=-1, keepdims=True))
        dy = err / d
        dx2 = _rms_bwd(dy * g_ref[...], xn, r)
        dx2_ref[...] = dx2
        dffn_ref[...] = (dx2 * gate_f).astype(BF16)

        @pl.when(i == 0)
        def _():
            loss_ref[...] = jnp.zeros_like(loss_ref)
            dg_ref[...] = jnp.zeros_like(dg_ref)

        @pl.when(i % per_seq == 0)
        def _():
            dgate_ref[...] = jnp.zeros_like(dgate_ref)

        loss_ref[...] += part
        dg_ref[...] += jnp.sum(dy * xn, axis=0, keepdims=True)
        dgate_ref[0] += jnp.sum(dx2 * f, axis=0, keepdims=True)

    tile = lambda w: pl.BlockSpec((tm, w), lambda i: (i, 0))
    return pl.pallas_call(
        body, name="down_loss", grid=(t // tm,),
        out_shape=(jax.ShapeDtypeStruct((t, d), F32), jax.ShapeDtypeStruct((t, d), BF16),
                   jax.ShapeDtypeStruct((SUBLANES, LANES), F32), jax.ShapeDtypeStruct((batch, 1, d), F32),
                   jax.ShapeDtypeStruct((1, d), F32)),
        in_specs=[tile(D_FF), _resident((D_FF, d)), tile(d),
                  pl.BlockSpec((1, 1, 6 * d), lambda i: (i // per_seq, 0, 0)),
                  pl.BlockSpec((1, d), lambda i: (0, 0)), tile(d)],
        out_specs=(tile(d), tile(d), pl.BlockSpec((SUBLANES, LANES), lambda i: (0, 0)),
                   pl.BlockSpec((1, 1, d), lambda i: (i // per_seq, 0, 0)), pl.BlockSpec((1, d), lambda i: (0, 0))),
        compiler_params=_cparams(("arbitrary",), VMEM_BIG),
    )(a, w_down, x1, mod3, g_final, target)


def _down_weight_grad(a, dffn):
    t, dff = a.shape
    d = dffn.shape[1]
    tk = 2 * TOKEN_TILE
    n_k = t // tk

    def body(a_ref, df_ref, g_ref, gb_ref):
        k = pl.program_id(0)

        @pl.when(k == 0)
        def _():
            g_ref[...] = jnp.zeros_like(g_ref)

        g_ref[...] += _mm_tn(a_ref[...], df_ref[...])

        @pl.when(k == n_k - 1)
        def _():
            gb_ref[...] = g_ref[...].astype(BF16)

    whole = _resident((dff, d))
    return pl.pallas_call(
        body, name="down_weight_grad", grid=(n_k,),
        out_shape=(jax.ShapeDtypeStruct((dff, d), F32), jax.ShapeDtypeStruct((dff, d), BF16)),
        in_specs=[pl.BlockSpec((tk, dff), lambda k: (k, 0)), pl.BlockSpec((tk, d), lambda k: (k, 0))],
        out_specs=(whole, whole),
        compiler_params=_cparams(("arbitrary",), VMEM_BIG),
    )(a, dffn)


def _ffn_backward(dffn, w_down, u, conv_w, conv_b, batch, seq, rider=None):
    t, d = dffn.shape
    cw = FF_TILE
    rows = CONV_CHUNK

    def body(df_ref, wd_ref, u_ref, w_ref, b_ref, du_ref, gcw_ref, gcb_ref, da_scr, dgc_scr):
        b = pl.program_id(1)
        da_scr[...] = _mm_nt(df_ref[...], wd_ref[...])

        @pl.when(b == 0)
        def _():
            gcw_ref[...] = jnp.zeros_like(gcw_ref)
            gcb_ref[...] = jnp.zeros_like(gcb_ref)

        def fold(v):
            return jnp.sum(v.reshape(rows // SUBLANES, SUBLANES, cw), axis=0)

        def chunk(i, carry):
            s = pl.multiple_of(i * rows, rows)
            here = pl.ds(s, rows)
            gt, prev, nxt = _taps_chunk(lambda at, n: u_ref[1, pl.ds(at, n), :].astype(F32), s, rows, seq)
            val, da = u_ref[0, here, :].astype(F32), da_scr[here, :]
            gc = prev * w_ref[0:1, :] + gt * w_ref[1:2, :] + nxt * w_ref[2:3, :] + b_ref[...]
            sg = _sigmoid(gc)
            sl = gc * sg
            du_ref[0, here, :] = (da * sl).astype(BF16)
            dgc = (da * val) * (sg * (1.0 + gc * (1.0 - sg)))
            dgc_scr[here, :] = dgc
            cb, c0, c1, c2 = carry
            return cb + fold(dgc), c0 + fold(dgc * prev), c1 + fold(dgc * gt), c2 + fold(dgc * nxt)

        zero = jnp.zeros((SUBLANES, cw), F32)
        cb, c0, c1, c2 = lax.fori_loop(0, seq // rows, chunk, (zero, zero, zero, zero))
        gcb_ref[...] += jnp.sum(cb, axis=0, keepdims=True)
        gcw_ref[0:1, :] += jnp.sum(c0, axis=0, keepdims=True)
        gcw_ref[1:2, :] += jnp.sum(c1, axis=0, keepdims=True)
        gcw_ref[2:3, :] += jnp.sum(c2, axis=0, keepdims=True)

        def chunk2(i, carry):
            s = pl.multiple_of(i * rows, rows)
            dgc, dprev, dnxt = _taps_chunk(lambda at, n: dgc_scr[pl.ds(at, n), :], s, rows, seq)
            du_ref[1, pl.ds(s, rows), :] = (dnxt * w_ref[0:1, :] + dgc * w_ref[1:2, :]
                                            + dprev * w_ref[2:3, :]).astype(BF16)
            return carry

        lax.fori_loop(0, seq // rows, chunk2, 0)

    return _hosted(
        body, rider, name="ffn_backward", grid=(D_FF // cw, batch),
        out_shape=[jax.ShapeDtypeStruct((2, t, D_FF), BF16),
                   jax.ShapeDtypeStruct((3, D_FF), F32), jax.ShapeDtypeStruct((1, D_FF), F32)],
        in_specs=[pl.BlockSpec((seq, d), lambda j, b: (b, 0)), pl.BlockSpec((cw, d), lambda j, b: (j, 0)),
                  pl.BlockSpec((2, seq, cw), lambda j, b: (0, b, j)),
                  pl.BlockSpec((3, cw), lambda j, b: (0, j)), pl.BlockSpec((1, cw), lambda j, b: (0, j))],
        out_specs=[pl.BlockSpec((2, seq, cw), lambda j, b: (0, b, j)),
                   pl.BlockSpec((3, cw), lambda j, b: (0, j)), pl.BlockSpec((1, cw), lambda j, b: (0, j))],
        scratch_shapes=[pltpu.VMEM((seq, cw), F32), pltpu.VMEM((seq, cw), F32)],
        compiler_params=_cparams(("arbitrary", "arbitrary"), VMEM_BIG), args=[dffn, w_down, u, conv_w, conv_b])


def _up_backward(du, w_up, x1, mod3, g_ffn, dx2, mix, seq, rider=None):
    _, t, _ = du.shape
    d = x1.shape[1]
    tm = TOKEN_TILE
    per_seq = seq // tm
    batch = t // seq
    w_a, w_b = w_up
    half, wcol = w_a.shape[1], w_a.shape[2]

    def body(du_ref, wa_ref, wb_ref, x1_ref, mod_ref, g_ref, dx2_ref, mix_ref,
             dx1_ref, dmix_ref, dsh_ref, dsc_ref, dga_ref, dg_ref):
        i = pl.program_id(0)
        parts = []
        for w_ref in (wa_ref, wb_ref):
            acc = jnp.zeros((tm, half), F32)
            for j in range(N_SHARD):
                acc = acc + _mm_nt(du_ref[j // 2, :, (j % 2) * wcol:(j % 2 + 1) * wcol], w_ref[j])
            parts.append(acc)
        dh = jnp.concatenate(parts, axis=1)
        gate_a = mod_ref[0, :, 2 * d:3 * d]
        scale_f = mod_ref[0, :, 4 * d:5 * d]
        r, xn = _rms_stats(x1_ref[...])
        xg = xn * g_ref[...]
        dxg = dh * (1.0 + scale_f)
        dx1 = dx2_ref[...] + _rms_bwd(dxg * g_ref[...], xn, r)
        dx1_ref[...] = dx1
        dmix_ref[...] = (dx1 * gate_a).astype(BF16)

        @pl.when(i == 0)
        def _():
            dg_ref[...] = jnp.zeros_like(dg_ref)

        @pl.when(i % per_seq == 0)
        def _():
            dsh_ref[...] = jnp.zeros_like(dsh_ref)
            dsc_ref[...] = jnp.zeros_like(dsc_ref)
            dga_ref[...] = jnp.zeros_like(dga_ref)

        dg_ref[...] += jnp.sum(dxg * xn, axis=0, keepdims=True)
        dsh_ref[0] += jnp.sum(dh, axis=0, keepdims=True)
        dsc_ref[0] += jnp.sum(dh * xg, axis=0, keepdims=True)
        dga_ref[0] += jnp.sum(dx1 * mix_ref[...], axis=0, keepdims=True)

    tile = lambda w: pl.BlockSpec((tm, w), lambda i: (i, 0))
    per_b = pl.BlockSpec((1, 1, d), lambda i: (i // per_seq, 0, 0))
    small = jax.ShapeDtypeStruct((batch, 1, d), F32)
    return _hosted(
        body, rider, name="up_backward", grid=(t // tm,),
        out_shape=[jax.ShapeDtypeStruct((t, d), F32), jax.ShapeDtypeStruct((t, d), BF16), small, small, small,
                   jax.ShapeDtypeStruct((1, d), F32)],
        in_specs=[pl.BlockSpec((2, tm, D_FF), lambda i: (0, i, 0)),
                  _resident((N_SHARD, half, wcol)), _resident((N_SHARD, half, wcol)), tile(d),
                  pl.BlockSpec((1, 1, 6 * d), lambda i: (i // per_seq, 0, 0)),
                  pl.BlockSpec((1, d), lambda i: (0, 0)), tile(d), tile(d)],
        out_specs=[tile(d), tile(d), per_b, per_b, per_b, pl.BlockSpec((1, d), lambda i: (0, 0))],
        scratch_shapes=[], compiler_params=_cparams(("arbitrary",), VMEM_BIG),
        args=[du, w_a, w_b, x1, mod3, g_ffn, dx2, mix])


def _up_weight_grad(h2, du, rider=None):
    t, d = h2.shape
    tk = 2 * TOKEN_TILE
    wcol = D_FF // 2
    half = d // 2
    n_k = t // tk

    def body(h_ref, du_ref, ga_ref, gb_ref, ga16_ref, gb16_ref):
        k = pl.program_id(1)

        @pl.when(k == 0)
        def _():
            ga_ref[...] = jnp.zeros_like(ga_ref)
            gb_ref[...] = jnp.zeros_like(gb_ref)

        du = du_ref[0]
        ga_ref[0] += _mm_tn(h_ref[:, :half], du)
        gb_ref[0] += _mm_tn(h_ref[:, half:], du)

        @pl.when(k == n_k - 1)
        def _():
            ga16_ref[...] = ga_ref[...].astype(BF16)
            gb16_ref[...] = gb_ref[...].astype(BF16)

    g_spec = pl.BlockSpec((1, half, wcol), lambda j, k: (j, 0, 0))
    f32_out = jax.ShapeDtypeStruct((N_SHARD, half, wcol), F32)
    b16_out = jax.ShapeDtypeStruct((N_SHARD, half, wcol), BF16)
    return _hosted(
        body, rider, name="up_weight_grad", grid=(N_SHARD, n_k),
        out_shape=[f32_out, f32_out, b16_out, b16_out],
        in_specs=[pl.BlockSpec((tk, d), lambda j, k: (k, 0)),
                  pl.BlockSpec((1, tk, wcol), lambda j, k: (j // 2, k, j % 2))],
        out_specs=[g_spec, g_spec, g_spec, g_spec], scratch_shapes=[],
        compiler_params=_cparams(("arbitrary", "arbitrary"), VMEM_BIG), args=[h2, du])


def _out_backward(dmix, w_out, oab, oa, ob, g_na, g_sw):
    t, d = dmix.shape
    tm = 2 * TOKEN_TILE
    hw = NA_WIDTH

    def body(dm_ref, w_ref, oab_ref, oa_ref, ob_ref, gna_ref, gsw_ref,
             doa_ref, dob_ref, gw_ref, gwb_ref, dgna_ref, dgsw_ref):
        @pl.when(pl.program_id(0) == 0)
        def _():
            gw_ref[...] = jnp.zeros_like(gw_ref)
            dgna_ref[...] = jnp.zeros_like(dgna_ref)
            dgsw_ref[...] = jnp.zeros_like(dgsw_ref)

        dm = dm_ref[...]
        gw_ref[...] += _mm_tn(oab_ref[...], dm)

        @pl.when(pl.program_id(0) == t // tm - 1)
        def _():
            gwb_ref[...] = gw_ref[...].astype(BF16)

        do = _mm_nt(dm, w_ref[...])
        for raw_ref, g_ref, dst_ref, dg_ref, lo in ((oa_ref, gna_ref, doa_ref, dgna_ref, 0),
                                                     (ob_ref, gsw_ref, dob_ref, dgsw_ref, hw)):
            r, xn = _rms_stats(raw_ref[...])
            dpart = do[:, lo:lo + hw]
            dg_ref[...] += jnp.sum(dpart * xn, axis=0, keepdims=True)
            dst_ref[...] = _rms_bwd(dpart * g_ref[...], xn, r).astype(BF16)

    tile = lambda w: pl.BlockSpec((tm, w), lambda i: (i, 0))
    vec = lambda w: pl.BlockSpec((1, w), lambda i: (0, 0))
    return pl.pallas_call(
        body, name="out_backward", grid=(t // tm,),
        out_shape=(jax.ShapeDtypeStruct((t, hw), BF16), jax.ShapeDtypeStruct((t, hw), BF16),
                   jax.ShapeDtypeStruct((d, d), F32), jax.ShapeDtypeStruct((d, d), BF16),
                   jax.ShapeDtypeStruct((1, hw), F32), jax.ShapeDtypeStruct((1, hw), F32)),
        in_specs=[tile(d), pl.BlockSpec((d, d), lambda i: (0, 0)), tile(d), tile(hw), tile(hw), vec(hw), vec(hw)],
        out_specs=(tile(hw), tile(hw), pl.BlockSpec((d, d), lambda i: (0, 0)), pl.BlockSpec((d, d), lambda i: (0, 0)),
                   vec(hw), vec(hw)),
        compiler_params=_cparams(("arbitrary",), VMEM_BIG),
    )(dmix, w_out, oab, oa, ob, g_na, g_sw)


def _na_backward(proj, d_o, lse, tiles, batch, seq, rider=None):
    t = proj.shape[0]
    n_rows = seq // GRID_W
    n_pairs = NA_WIDTH // LANES
    win = NA_ROWS * GRID_W
    n_tiles = 2 * NA_ROWS - 2

    def body(q_ref, k_ref, v_ref, do_ref, lse_ref, tp_ref, dq_ref, dk_ref, dv_ref, dtp_ref, km, vm, dk_acc, dv_acc):
        @pl.when(pl.program_id(1) == 0)
        def _():
            dtp_ref[...] = jnp.zeros_like(dtp_ref)

        _na_prepare(k_ref, v_ref, km, vm)
        dk_acc[...] = jnp.zeros_like(dk_acc)
        dv_acc[...] = jnp.zeros_like(dv_acc)
        low = lax.broadcasted_iota(jnp.int32, (win, LANES), 1) < HEAD_DIM

        def scores(r):
            rs, off = _na_window(r, n_rows)
            rows = pl.ds(pl.multiple_of(r * GRID_W, GRID_W), GRID_W)
            wrows = pl.ds(pl.multiple_of(rs * GRID_W, GRID_W), win)
            q, do = q_ref[rows, :], do_ref[rows, :]
            k2 = _na_pair_window(km, wrows)
            s = _na_scores(q, k2, tp_ref, off)
            dp = _mm_nt(do, _na_pair_window(vm, wrows))
            return rows, wrows, off, q, do, k2, s, dp

        def finish(rows, wrows, off, q, do, k2, s, dp):
            p = _pair_probs_from_lse(s, lse_ref[rows, :])
            parts = []
            for h in range(2):
                ph, dph = p[:, h * win:(h + 1) * win], dp[:, h * win:(h + 1) * win]
                dsh = ph * (dph - jnp.sum(ph * dph, axis=-1, keepdims=True))
                for w in range(NA_ROWS // 2):
                    dtp_ref[h, 2 * w - off + (NA_ROWS - 1)] += dsh[:, w * LANES:(w + 1) * LANES]
                parts.append(dsh)
            dsb = (jnp.concatenate(parts, axis=1) * QK_SCALE).astype(BF16)
            dq_ref[rows, :] = _mm(dsb, k2).astype(BF16)
            dk2 = _mm_tn(dsb, q)
            dv2 = _mm_tn(p.astype(BF16), do)
            dk_acc[wrows, :] += jnp.where(low, dk2[:win], dk2[win:])
            dv_acc[wrows, :] += jnp.where(low, dv2[:win], dv2[win:])

        def row_group(i, carry):
            for state in [scores(NA_GROUP * i + j) for j in range(NA_GROUP)]:
                finish(*state)
            return carry

        lax.fori_loop(0, n_rows // NA_GROUP, row_group, 0)
        dk_ref[...] = dk_acc[...].astype(BF16)
        dv_ref[...] = dv_acc[...].astype(BF16)

    blk = lambda off: pl.BlockSpec((seq, LANES), lambda p, b: (b, off + p))
    out = jax.ShapeDtypeStruct((t, NA_WIDTH), BF16)
    return _hosted(
        body, rider, name="na_backward", grid=(n_pairs, batch),
        out_shape=[out, out, out, jax.ShapeDtypeStruct(tiles.shape, F32)],
        in_specs=[blk(0), blk(n_pairs), blk(2 * n_pairs), blk(0), blk(0),
                  pl.BlockSpec((2, n_tiles, GRID_W, LANES), lambda p, b: (p, 0, 0, 0))],
        out_specs=[blk(0), blk(0), blk(0), pl.BlockSpec((2, n_tiles, GRID_W, LANES), lambda p, b: (p, 0, 0, 0))],
        scratch_shapes=[pltpu.VMEM((2, seq, LANES), BF16), pltpu.VMEM((2, seq, LANES), BF16),
                        pltpu.VMEM((seq, LANES), F32), pltpu.VMEM((seq, LANES), F32)],
        compiler_params=_cparams(("arbitrary", "arbitrary")), args=[proj, proj, proj, d_o, lse, tiles])


def _na_bias_grad(dtiles, expand):
    n = dtiles.shape[0]

    def body(t_ref, e_ref, o_ref):
        flat = jnp.concatenate([t_ref[:, qq, :] for qq in range(GRID_W)], axis=1)
        o_ref[...] = lax.dot_general(flat, e_ref[...], (((1,), (1,)), ((), ())),
                                     precision=lax.Precision.HIGHEST, preferred_element_type=F32)

    return pl.pallas_call(
        body, name="na_bias_grad",
        out_shape=jax.ShapeDtypeStruct((n, expand.shape[0]), F32),
        compiler_params=_cparams(vmem=VMEM_BIG),
    )(dtiles, expand)


def _sw_backward(proj, d_o, lse, sink, batch, seq, rider=None):
    t = proj.shape[0]
    n_pairs = SW_WIDTH // LANES
    q_blk = 3 * NA_WIDTH // LANES
    k_blk = q_blk + n_pairs
    n_blocks = seq // SW_BLOCK
    pad = seq + 2 * SW_BLOCK

    def body(sink_ref, q_ref, k_ref, v_ref, do_ref, lse_ref, dq_ref, dk_ref, dv_ref, dsk_ref,
             k_lo, k_hi, v_lo, v_hi, dk_loc, dv_loc, dk_tot, dv_tot):
        hp = pl.program_id(1)
        g = hp // 2
        _sw_prepare(k_ref, g, k_lo, k_hi, seq)
        _sw_prepare(v_ref, g, v_lo, v_hi, seq)
        dk_loc[...] = jnp.zeros_like(dk_loc)
        dv_loc[...] = jnp.zeros_like(dv_loc)

        @pl.when(hp == 0)
        def _():
            dk_tot[...] = jnp.zeros_like(dk_tot)
            dv_tot[...] = jnp.zeros_like(dv_tot)

        band = 3 * SW_BLOCK
        low = lax.broadcasted_iota(jnp.int32, (band, LANES), 1) < HEAD_DIM

        sinks = (sink_ref[2 * hp], sink_ref[2 * hp + 1])

        def scores(n):
            rows = pl.ds(pl.multiple_of(n * SW_BLOCK, SW_BLOCK), SW_BLOCK)
            wrows = pl.ds(pl.multiple_of(n * SW_BLOCK, SW_BLOCK), band)
            qb, do = q_ref[rows, :], do_ref[rows, :]
            k2 = jnp.concatenate([k_lo[wrows, :], k_hi[wrows, :]], axis=0)
            v2 = jnp.concatenate([v_lo[wrows, :], v_hi[wrows, :]], axis=0)
            return n, rows, wrows, qb, do, k2, _mm_nt(qb, k2) * QK_SCALE, _mm_nt(do, v2)

        def finish(sink_acc, n, rows, wrows, qb, do, k2, s2, dp):
            p, ps = _sw_probs_from_lse(s2, _sw_mask(n, seq), sinks, lse_ref[rows, :])
            parts, new = [], []
            for i in range(2):
                ph, dph = p[:, i * band:(i + 1) * band], dp[:, i * band:(i + 1) * band]
                delta = jnp.sum(ph * dph, axis=-1, keepdims=True)
                parts.append(ph * (dph - delta))
                new.append(sink_acc[i] - ps[i] * delta)
            dsb = (jnp.concatenate(parts, axis=1) * QK_SCALE).astype(BF16)
            dq_ref[rows, :] = _mm(dsb, k2)
            dk2 = _mm_tn(dsb, qb)
            dv2 = _mm_tn(p.astype(BF16), do)
            dk_loc[wrows, :] += jnp.where(low, dk2[:band], dk2[band:])
            dv_loc[wrows, :] += jnp.where(low, dv2[:band], dv2[band:])
            return tuple(new)

        def block_group(i, carry):
            for state in [scores(SW_GROUP_BLOCKS * i + j) for j in range(SW_GROUP_BLOCKS)]:
                carry = finish(carry, *state)
            return carry

        zero = jnp.zeros((SW_BLOCK, 1), F32)
        s0, s1 = lax.fori_loop(0, n_blocks // SW_GROUP_BLOCKS, block_group, (zero, zero))
        row = lax.broadcasted_iota(jnp.int32, (SUBLANES, LANES), 0)
        dsk_ref[0, 0] = jnp.where(row == 0, jnp.sum(s0), jnp.where(row == 1, jnp.sum(s1), 0.0))

        lane_s = lax.broadcasted_iota(jnp.int32, (seq, LANES), 1)
        mine_g = (lane_s // HEAD_DIM) == g
        for loc, tot in ((dk_loc, dk_tot), (dv_loc, dv_tot)):
            part = loc[SW_BLOCK:SW_BLOCK + seq, :]
            tot[...] += jnp.where(mine_g, part + pltpu.roll(part, HEAD_DIM, 1), 0.0)

        @pl.when(hp == n_pairs - 1)
        def _():
            dk_ref[...] = dk_tot[...]
            dv_ref[...] = dv_tot[...].astype(BF16)

    return _hosted(
        body, rider, name="sw_backward", grid=(batch, n_pairs),
        out_shape=[jax.ShapeDtypeStruct((t, SW_WIDTH), F32), jax.ShapeDtypeStruct((t, LANES), F32),
                   jax.ShapeDtypeStruct((t, LANES), BF16), jax.ShapeDtypeStruct((batch, n_pairs, SUBLANES, LANES), F32)],
        in_specs=[pl.BlockSpec(memory_space=pltpu.SMEM),
                  pl.BlockSpec((seq, LANES), lambda b, p: (b, q_blk + p)),
                  pl.BlockSpec((seq, LANES), lambda b, p: (b, k_blk)),
                  pl.BlockSpec((seq, LANES), lambda b, p: (b, k_blk + 1)),
                  pl.BlockSpec((seq, LANES), lambda b, p: (b, p)), pl.BlockSpec((seq, LANES), lambda b, p: (b, p))],
        out_specs=[pl.BlockSpec((seq, LANES), lambda b, p: (b, p)), pl.BlockSpec((seq, LANES), lambda b, p: (b, 0)),
                   pl.BlockSpec((seq, LANES), lambda b, p: (b, 0)),
                   pl.BlockSpec((1, 1, SUBLANES, LANES), lambda b, p: (b, p, 0, 0))],
        scratch_shapes=[pltpu.VMEM((pad, LANES), BF16)] * 4 + [pltpu.VMEM((pad, LANES), F32)] * 2
        + [pltpu.VMEM((seq, LANES), F32)] * 2,
        compiler_params=_cparams(("arbitrary", "arbitrary")), args=[sink, proj, proj, proj, d_o, lse])


def _in_backward(dqkv_a, dq_b, dk_b, dv_b, w_in_t, h1, x, mod3, g_attn, dx1, cos_t, sin_t, seq):
    t, d = x.shape
    tm = TOKEN_TILE
    per_seq = seq // tm
    batch = t // seq
    dqa, dka, dva = dqkv_a
    n_q = SW_WIDTH // LANES

    def body(dqa_ref, dka_ref, dva_ref, dqb_ref, dkb_ref, dvb_ref, w_ref, h_ref, x_ref, mod_ref, g_ref, dx1_ref,
             cos_ref, sin_ref, dx_ref, gw_ref, gwb_ref, dsh_ref, dsc_ref, dg_ref):
        i = pl.program_id(0)

        @pl.when(i == 0)
        def _():
            gw_ref[...] = jnp.zeros_like(gw_ref)
            dg_ref[...] = jnp.zeros_like(dg_ref)

        @pl.when(i % per_seq == 0)
        def _():
            dsh_ref[...] = jnp.zeros_like(dsh_ref)
            dsc_ref[...] = jnp.zeros_like(dsc_ref)

        dr = jnp.concatenate([dqb_ref[...], dkb_ref[...]], axis=1)
        cos = jnp.concatenate([cos_ref[...]] * (n_q + 1), axis=1)
        sin = jnp.concatenate([sin_ref[...]] * (n_q + 1), axis=1)
        dr = dr * cos + _rope_rot(dr * sin)
        dproj = jnp.concatenate([dqa_ref[...], dka_ref[...], dva_ref[...], dr.astype(BF16), dvb_ref[...]], axis=1)
        gw_ref[...] += _mm_tn(dproj, h_ref[...])

        @pl.when(i == t // tm - 1)
        def _():
            gwb_ref[...] = gw_ref[...].astype(BF16)

        dh = _mm(dproj, w_ref[...])
        scale = mod_ref[0, :, d:2 * d]
        r, xn = _rms_stats(x_ref[...])
        xg = xn * g_ref[...]
        dxg = dh * (1.0 + scale)
        dx_ref[...] = dx1_ref[...] + _rms_bwd(dxg * g_ref[...], xn, r)
        dg_ref[...] += jnp.sum(dxg * xn, axis=0, keepdims=True)
        dsh_ref[0] += jnp.sum(dh, axis=0, keepdims=True)
        dsc_ref[0] += jnp.sum(dh * xg, axis=0, keepdims=True)

    tile = lambda w: pl.BlockSpec((tm, w), lambda i: (i, 0))
    per_b = pl.BlockSpec((1, 1, d), lambda i: (i // per_seq, 0, 0))
    small = jax.ShapeDtypeStruct((batch, 1, d), F32)
    rope = pl.BlockSpec((tm, LANES), lambda i: (i % per_seq, 0))
    return pl.pallas_call(
        body, name="in_backward", grid=(t // tm,),
        out_shape=(jax.ShapeDtypeStruct((t, d), F32), jax.ShapeDtypeStruct((IN_WIDTH, d), F32),
                   jax.ShapeDtypeStruct((IN_WIDTH, d), BF16), small, small, jax.ShapeDtypeStruct((1, d), F32)),
        in_specs=[tile(NA_WIDTH), tile(NA_WIDTH), tile(NA_WIDTH), tile(SW_WIDTH), tile(LANES), tile(LANES),
                  _resident((IN_WIDTH, d)), tile(d), tile(d),
                  pl.BlockSpec((1, 1, 6 * d), lambda i: (i // per_seq, 0, 0)),
                  pl.BlockSpec((1, d), lambda i: (0, 0)), tile(d), rope, rope],
        out_specs=(tile(d), _resident((IN_WIDTH, d)), _resident((IN_WIDTH, d)),
                   per_b, per_b, pl.BlockSpec((1, d), lambda i: (0, 0))),
        compiler_params=_cparams(("arbitrary",), VMEM_BIG),
    )(dqa, dka, dva, dq_b, dk_b, dv_b, w_in_t, h1, x, mod3, g_attn, dx1, cos_t, sin_t)


def _ada_weight_grad(sc_all, dmod_cols):
    d = sc_all.shape[1]
    ncol = dmod_cols.shape[1]

    def body(s_ref, m_ref, o_ref):
        o_ref[...] = _mm_tn(s_ref[...].astype(BF16), m_ref[...].astype(BF16))

    return pl.pallas_call(
        body, name="ada_weight_grad",
        out_shape=jax.ShapeDtypeStruct((d, ncol), F32),
        compiler_params=_cparams(vmem=VMEM_BIG),
    )(sc_all, dmod_cols)


def _row_tile(rows, cols):
    target = max(SUBLANES, (1 << 20) // (4 * cols))
    best = rows
    for cand in range(SUBLANES, rows + 1, SUBLANES):
        if rows % cand == 0 and cand <= target:
            best = cand
    return best if rows % SUBLANES == 0 else rows


def _sum_slots(parts, name):
    n = len(parts)
    _, rows, cols = parts[0][0].shape
    tr = _row_tile(rows, cols)
    per = rows // tr

    def body(*refs):
        o_ref = refs[-1]
        for q in range(n):
            @pl.when(pl.program_id(0) == q)
            def _(q=q):
                p_ref, own_ref = refs[2 * q], refs[2 * q + 1]
                o_ref[...] = ((own_ref[...] + p_ref[0].astype(F32)) + p_ref[1].astype(F32)) + p_ref[2].astype(F32)

    in_specs, args = [], []
    for q, (recv, own) in enumerate(parts):
        in_specs.append(pl.BlockSpec((N_SHARD - 1, tr, cols), lambda p, i, q=q: (0, jnp.where(p == q, i, 0), 0)))
        in_specs.append(pl.BlockSpec((tr, cols), lambda p, i, q=q: (jnp.where(p == q, i, 0), 0)))
        args += [recv, own]
    return pl.pallas_call(
        body, name=name, grid=(n, per),
        out_shape=jax.ShapeDtypeStruct((n * rows, cols), F32),
        in_specs=in_specs, out_specs=pl.BlockSpec((tr, cols), lambda p, i: (p * per + i, 0)),
        compiler_params=_cparams(("arbitrary", "arbitrary")),
    )(*args)


def _adamw_math(w, g, m, v):
    m2 = ADAM_B1 * m + (1.0 - ADAM_B1) * g
    v2 = ADAM_B2 * v + (1.0 - ADAM_B2) * (g * g)
    m_hat = m2 / (1.0 - ADAM_B1 ** ADAM_STEP)
    v_hat = v2 / (1.0 - ADAM_B2 ** ADAM_STEP)
    return -ADAM_LR * (m_hat / (jnp.sqrt(v_hat) + ADAM_EPS) + ADAM_WD * w), m2, v2


def _small_sums(partials, dmod, rider=None):
    moving = list(partials) + [dmod]
    n_mov = len(moving)

    def body(*refs):
        mov, refs = refs[:n_mov], refs[n_mov:]
        sums_out, refs = refs[:n_mov - 1], refs[n_mov - 1:]
        b_out, dmod_out, refs = refs[0], refs[1], refs[2:]
        everyone, (ssem, rsem) = refs[:n_mov], refs[n_mov:]
        x, y, c = _my_pos()
        me = 4 * x + 2 * y + c
        cps = []
        for a in range(n_mov):
            everyone[a][me] = mov[a][...]
            for k in range(1, N_DEV):
                peer = (_flip(x, (k >> 2) & 1), _flip(y, (k >> 1) & 1), _flip(c, k & 1))
                cps.append(pltpu.make_async_remote_copy(
                    src_ref=everyone[a].at[me], dst_ref=everyone[a].at[me], send_sem=ssem.at[a, k - 1],
                    recv_sem=rsem.at[a, k - 1], device_id=peer, device_id_type=MESH))
        for cp in cps:
            cp.start()
        for cp in cps:
            cp.wait_recv()

        def total(a):
            acc = everyone[a][0]
            for dev in range(1, N_DEV):
                acc = acc + everyone[a][dev]
            return acc

        for a in range(n_mov - 1):
            sums_out[a][...] = total(a)
        b_out[...] = jnp.sum(total(n_mov - 1), axis=0, keepdims=True)
        dmod_out[...] = everyone[n_mov - 1][...]
        for cp in cps:
            cp.wait_send()

    vm = pl.BlockSpec(memory_space=pltpu.VMEM)
    sds = jax.ShapeDtypeStruct
    out_shape = [sds(p.shape, F32) for p in partials]
    out_shape += [sds((1, dmod.shape[1]), F32), sds((N_DEV,) + dmod.shape, F32)]
    return _hosted(
        body, rider, name="small_sums", grid=(), out_shape=out_shape,
        in_specs=[vm] * n_mov, out_specs=[vm] * len(out_shape),
        scratch_shapes=[pltpu.VMEM((N_DEV,) + a.shape, F32) for a in moving]
        + [pltpu.SemaphoreType.DMA((n_mov, N_DEV - 1)), pltpu.SemaphoreType.DMA((n_mov, N_DEV - 1))],
        compiler_params=_cparams(vmem=VMEM_BIG), args=moving)


def _small_adamw(states, grads):
    n = len(states)

    def body(*refs):
        g_refs, wmv, res = refs[:n], refs[n:4 * n], refs[4 * n:]
        for j in range(n):
            g = g_refs[j][...]
            delta, m2, v2 = _adamw_math(wmv[3 * j][...], g, wmv[3 * j + 1][...], wmv[3 * j + 2][...])
            res[4 * j][...] = g
            res[4 * j + 1][...] = delta
            res[4 * j + 2][...] = m2
            res[4 * j + 3][...] = v2

    out_shape = []
    for w, _, _ in states:
        out_shape += [jax.ShapeDtypeStruct(w.shape, F32)] * 4
    outs = pl.pallas_call(body, name="small_adamw", out_shape=tuple(out_shape),
                          compiler_params=_cparams(vmem=VMEM_BIG))(*grads, *[a for st in states for a in st])
    return [outs[4 * j:4 * j + 4] for j in range(n)]


def _adamw(w, grads, m, v, name):
    rows, cols = w.shape
    tr = _row_tile(rows, cols)
    ng = len(grads)

    def body(*refs):
        w_ref = refs[0]
        g_refs = refs[1:1 + ng]
        m_ref, v_ref = refs[1 + ng], refs[2 + ng]
        g_out, d_out, m_out, v_out = refs[3 + ng:]
        g = g_refs[0][...]
        for extra in g_refs[1:]:
            g = g + extra[...]
        g_out[...] = g
        d_out[...], m_out[...], v_out[...] = _adamw_math(w_ref[...], g, m_ref[...], v_ref[...])

    spec = pl.BlockSpec((tr, cols), lambda i: (i, 0))
    out = jax.ShapeDtypeStruct((rows, cols), F32)
    return pl.pallas_call(
        body, name=name, grid=(rows // tr,),
        out_shape=(out, out, out, out),
        in_specs=[spec] * (3 + ng), out_specs=(spec, spec, spec, spec),
        compiler_params=_cparams(("arbitrary",)),
    )(w, *grads, m, v)


def _rope_tables(seq):
    half = HEAD_DIM // 2
    inv = np.float32(ROPE_THETA) ** (-np.arange(half, dtype=np.float32) / np.float32(half))
    ang = (np.arange(seq, dtype=np.float32)[:, None] * inv[None, :]).astype(np.float64)
    cos, sin = np.cos(ang).astype(np.float32), np.sin(ang).astype(np.float32)
    cos_t = np.concatenate([cos, cos, cos, cos], axis=1)
    sin_t = np.concatenate([-sin, sin, -sin, sin], axis=1)
    return jnp.asarray(cos_t), jnp.asarray(sin_t)


def kernel(x, c, w_ada, b_ada, g_attn, w_in, na_rpb, sw_sink, g_na_out, g_sw_out, w_out, g_ffn, w_up, conv_w, conv_b, w_down, g_final, loss_target, m_w_ada, m_b_ada, m_g_attn, m_w_in, m_na_rpb, m_sw_sink, m_g_na_out, m_g_sw_out, m_w_out, m_g_ffn, m_w_up, m_conv_w, m_conv_b, m_w_down, m_g_final, v_w_ada, v_b_ada, v_g_attn, v_w_in, v_na_rpb, v_sw_sink, v_g_na_out, v_g_sw_out, v_w_out, v_g_ffn, v_w_up, v_conv_w, v_conv_b, v_w_down, v_g_final):
    batch, seq, d = x.shape
    t = batch * seq
    assert d == D_MODEL and seq % (NA_ROWS * GRID_W) == 0 and seq % TOKEN_TILE == 0 and batch <= SUBLANES
    shard = 2 * lax.axis_index("x") + lax.axis_index("y")
    xt = x.reshape(t, d)
    tgt = loss_target.reshape(t, d)

    c8 = jnp.pad(c, ((0, SUBLANES - batch), (0, 0)))
    w_in_t_s = jnp.transpose(w_in[0]).astype(BF16)
    (mod8, sc_all), (w_in_g,) = _ada_forward(c8, w_ada[0], b_ada, _Rider("gather", [w_in_t_s]))
    mod3 = mod8[:batch].reshape(batch, 1, 6 * d)
    w_in_t = w_in_g.reshape(IN_WIDTH, d)

    cos_t, sin_t = _rope_tables(seq)
    (h1, proj), _ = _in_proj(xt, mod3, g_attn, w_in_t, cos_t, sin_t, seq)
    n_heads = NA_WIDTH // HEAD_DIM
    n_tiles, n_dc = 2 * NA_ROWS - 2, 2 * NA_COLS - 1
    expand, neg_mask = _na_bias_pattern()
    rpb = na_rpb[0]
    rows2 = jnp.concatenate([rpb[:, :-1, :], rpb[:, 1:, :]], axis=2).reshape(n_heads * n_tiles, 2 * n_dc)
    rows2 = jnp.pad(rows2, ((0, 0), (0, GRID_W - 2 * n_dc)))
    tiles = _na_bias_tiles(rows2, expand, neg_mask).reshape(n_heads, n_tiles, GRID_W, LANES)
    sink = sw_sink[0]
    w_up_b16 = w_up[0].astype(BF16)
    (oa, lse_a), (w_up_a, w_down_g) = _na_forward(proj, tiles, batch, seq,
                                                  _Rider("gather", [w_up_b16[:d // 2], w_down[0].astype(BF16)]))
    (ob, lse_b), (w_up_b, conv_w_g, w_out_g) = _sw_forward(
        proj, sink, batch, seq, _Rider("gather", [w_up_b16[d // 2:], conv_w[0], w_out[0].astype(BF16)]))
    w_up_f = (w_up_a, w_up_b)
    w_out_f = w_out_g.reshape(d, d)
    conv_w_f = jnp.transpose(conv_w_g, (1, 0, 2)).reshape(3, D_FF)
    oab, mix, x1, h2 = _out_proj(oa, ob, g_na_out, g_sw_out, w_out_f, xt, mod3, g_ffn, seq)
    (u,), _ = _up_proj(h2, w_up_f)
    w_down_f = w_down_g.reshape(D_FF, d)
    a = _conv_gate(u, conv_w_f, conv_b, batch, seq)
    dx2, dffn, loss_part, dgate_f, dg_final = _down_and_loss(a, w_down_f, x1, mod3, g_final.reshape(1, d), tgt, seq)

    gw_down, gw_down_b = _down_weight_grad(a, dffn)
    blocks = lambda g, rows: g.reshape(N_SHARD, rows // N_SHARD, d)
    (du, gconv_w, gconv_b), (recv_down, own_down) = _ffn_backward(
        dffn, w_down_f, u, conv_w_f, conv_b, batch, seq,
        _Rider("scatter", [blocks(gw_down_b, D_FF)], [blocks(gw_down, D_FF)]))
    (gw_up_top, gw_up_bot, gw_up_top_b, gw_up_bot_b), _ = _up_weight_grad(h2, du)
    (dx1, dmix, dshift_f, dscale_f, dgate_a, dg_ffn), _ = _up_backward(du, w_up_f, x1, mod3, g_ffn, dx2, mix, seq)
    doa, dob, gw_out, gw_out_b, dg_na, dg_sw = _out_backward(dmix, w_out_f, oab, oa, ob, g_na_out, g_sw_out)
    (dqa, dka, dva, dtiles), (recv_up_bot, own_up_bot) = _na_backward(
        proj, doa, lse_a, tiles, batch, seq, _Rider("scatter", [gw_up_bot_b], [gw_up_bot]))
    (dq_b, dk_b, dv_b, dsink_parts), (recv_out, recv_up_top, own_out, own_up_top) = _sw_backward(
        proj, dob, lse_b, sink, batch, seq,
        _Rider("scatter", [blocks(gw_out_b, d), gw_up_top_b], [blocks(gw_out, d), gw_up_top]))
    gx, gw_in_t, gw_in_b, dshift_a, dscale_a, dg_attn = _in_backward(
        (dqa, dka, dva), dq_b, dk_b, dv_b, w_in_t, h1, xt, mod3, g_attn, dx1, cos_t, sin_t, seq)

    red = _na_bias_grad(dtiles.reshape(n_heads * n_tiles, GRID_W, LANES), expand)[:, :2 * n_dc]
    red = red.reshape(n_heads, n_tiles, 2, n_dc)
    zero_row = jnp.zeros((n_heads, 1, n_dc), F32)
    g_rpb = (jnp.concatenate([red[:, :, 0, :], zero_row], axis=1)
             + jnp.concatenate([zero_row, red[:, :, 1, :]], axis=1))
    g_sink = jnp.sum(dsink_parts[:, :, :2, 0], axis=0).reshape(SW_WIDTH // HEAD_DIM)

    dmod = jnp.concatenate([dshift_a, dscale_a, dgate_a, dshift_f, dscale_f, dgate_f], axis=2).reshape(batch, 6 * d)
    rpb_shape = na_rpb.shape[1:]
    states = [(g_attn, m_g_attn, v_g_attn),
              (na_rpb.reshape(rpb_shape), m_na_rpb.reshape(rpb_shape), v_na_rpb.reshape(rpb_shape)),
              (sw_sink, m_sw_sink, v_sw_sink), (g_na_out, m_g_na_out, v_g_na_out), (g_sw_out, m_g_sw_out, v_g_sw_out),
              (g_ffn, m_g_ffn, v_g_ffn), (conv_b, m_conv_b, v_conv_b),
              (g_final.reshape(1, d), m_g_final.reshape(1, d), v_g_final.reshape(1, d))]
    partials = [dg_attn, g_rpb, g_sink.reshape(sw_sink.shape), dg_na, dg_sw, dg_ffn, gconv_b, dg_final,
                gconv_w, loss_part]
    mine = [None, _sum_slots([(recv_out, own_out)], "sum_w_out"),
            _sum_slots([(recv_up_top, own_up_top), (recv_up_bot, own_up_bot)], "sum_w_up"),
            _sum_slots([(recv_down, own_down)], "sum_w_down")]
    small, (recv_in, own_in, *theirs) = _small_sums(
        partials, dmod, _Riders([_Rider("scatter", [blocks(gw_in_b, IN_WIDTH)], [blocks(gw_in_t, IN_WIDTH)]),
                                 _Rider("swap", mine[1:])]))
    g_conv_w_full, loss_sum, g_b_ada, dmod_all = small[len(states):]
    r_small = _small_adamw(states + [(b_ada, m_b_ada, v_b_ada)], small[:len(states)] + [g_b_ada])
    loss = loss_sum[0, 0]
    mine[0] = _sum_slots([(recv_in, own_in)], "sum_w_in")
    theirs = _ride_alone(_Rider("swap", mine[:1]), "swap_sibling") + theirs
    dmod_rows = jnp.pad(dmod_all, ((0, 0), (0, SUBLANES - batch), (0, 0))).reshape(N_DEV * SUBLANES, 6 * d)
    ncol = w_ada.shape[2]
    g_w_ada = _ada_weight_grad(sc_all, lax.dynamic_slice(dmod_rows, (0, shard * ncol), (N_DEV * SUBLANES, ncol)))
    cshard = conv_w.shape[2]
    g_conv_w = lax.dynamic_slice(g_conv_w_full, (0, shard * cshard), (3, cshard))

    def big(w, m, v, g_parts, name):
        shape = w.shape
        outs = _adamw(w[0], g_parts, m[0], v[0], name)
        return [o.reshape(shape) for o in outs]

    r_w_ada = big(w_ada, m_w_ada, v_w_ada, [g_w_ada], "adamw_w_ada")
    r_w_in = [jnp.transpose(o).reshape(w_in.shape) for o in
              _adamw(jnp.transpose(w_in[0]), [mine[0], theirs[0]], jnp.transpose(m_w_in[0]), jnp.transpose(v_w_in[0]),
                     "adamw_w_in")]
    r_w_out = big(w_out, m_w_out, v_w_out, [mine[1], theirs[1]], "adamw_w_out")
    r_w_up = big(w_up, m_w_up, v_w_up, [mine[2], theirs[2]], "adamw_w_up")
    r_w_down = big(w_down, m_w_down, v_w_down, [mine[3], theirs[3]], "adamw_w_down")

    r_conv_w = big(conv_w, m_conv_w, v_conv_w, [g_conv_w], "adamw_conv_w")

    def pick(k):
        ga_, rpb_, sk_, gna_, gsw_, gf_, cb_, gfin_, b_ = [r[k] for r in r_small]
        return [r_w_ada[k], b_, ga_, r_w_in[k], rpb_.reshape(na_rpb.shape), sk_, gna_, gsw_, r_w_out[k], gf_,
                r_w_up[k], r_conv_w[k], cb_, r_w_down[k], gfin_.reshape(d)]

    return (loss, gx.reshape(batch, seq, d), *pick(0), *pick(1), *pick(2), *pick(3))
```

```python
import jax
import jax.numpy as jnp
import numpy as np
from jax import lax
from jax.experimental import pallas as pl
from jax.experimental.pallas import tpu as pltpu

F32 = jnp.float32
BF16 = jnp.bfloat16
MESH = pl.DeviceIdType.MESH

D_MODEL = 1024
HEAD_DIM = 64
NA_WIDTH = 512
SW_WIDTH = 512
SW_KV_WIDTH = 128
IN_WIDTH = 2304
D_FF = 2816
GRID_W = 64
NA_ROWS = 8
NA_COLS = 16
SW_BLOCK = 128
ROPE_THETA = 10000.0
EPS = 1e-6
NEG = -1e30
QK_SCALE = HEAD_DIM ** -0.5

ADAM_LR = 0.001
ADAM_B1 = 0.9
ADAM_B2 = 0.999
ADAM_EPS = 1e-08
ADAM_WD = 0.01
ADAM_STEP = 10

N_SHARD = 4
N_DEV = 8
LANES = 128
SUBLANES = 8
TOKEN_TILE = 512
FF_TILE = 256
CONV_CHUNK = 512
NA_GROUP = 8
SW_GROUP_BLOCKS = 8
VMEM_BIG = 56 * 1024 * 1024


def _mm(a, b):
    return jnp.dot(a, b, preferred_element_type=F32)


def _mm_nt(a, b):
    return lax.dot_general(a, b, (((1,), (1,)), ((), ())), preferred_element_type=F32)


def _mm_tn(a, b):
    return lax.dot_general(a, b, (((0,), (0,)), ((), ())), preferred_element_type=F32)


def _cparams(sem=None, vmem=None):
    kw = {}
    if sem is not None:
        kw["dimension_semantics"] = sem
    if vmem is not None:
        kw["vmem_limit_bytes"] = vmem
    return pltpu.CompilerParams(**kw)


def _resident(shape):
    return pl.BlockSpec(shape, lambda i: (0,) * len(shape), pipeline_mode=pl.Buffered(1))


def _sigmoid(x):
    return 1.0 / (1.0 + jnp.exp(-x))


def _rms_stats(x):
    r = lax.rsqrt(jnp.mean(x * x, axis=-1, keepdims=True) + EPS)
    return r, x * r


def _rms_bwd(dxn, xn, r):
    return r * (dxn - xn * jnp.mean(dxn * xn, axis=-1, keepdims=True))


def _my_pos():
    return lax.axis_index("x"), lax.axis_index("y"), lax.axis_index("c")


def _flip(v, bit):
    return 1 - v if bit else v


def _ada_forward(c8, w_ada, b_ada, rider):
    d = c8.shape[1]
    ncol = w_ada.shape[1]

    def body(c_ref, w_ref, b_ref, mod_ref, sc_ref, m_scr, mod_buf, ssem, rsem, ssem2, rsem2):
        x, y, c = _my_pos()
        me = 4 * x + 2 * y + c
        shard = 2 * x + y
        cv = c_ref[...]
        my_rows = pl.ds(pl.multiple_of(me * SUBLANES, SUBLANES), SUBLANES)
        sc_ref[my_rows, :] = cv * _sigmoid(cv)

        def copy1(k):
            peer = (_flip(x, (k >> 2) & 1), _flip(y, (k >> 1) & 1), _flip(c, k & 1))
            return pltpu.make_async_remote_copy(
                src_ref=sc_ref.at[my_rows, :], dst_ref=sc_ref.at[my_rows, :],
                send_sem=ssem.at[k - 1], recv_sem=rsem.at[k - 1], device_id=peer, device_id_type=MESH)

        sends = [copy1(k) for k in range(1, N_DEV)]
        for cp in sends:
            cp.start()
        for cp in sends:
            cp.wait_recv()
        m_scr[...] = _mm(sc_ref[...].astype(BF16), w_ref[...].astype(BF16))

        def copy2(k):
            px, py = _flip(x, (k >> 1) & 1), _flip(y, k & 1)
            rows = pl.ds(pl.multiple_of((4 * px + 2 * py + c) * SUBLANES, SUBLANES), SUBLANES)
            return pltpu.make_async_remote_copy(
                src_ref=m_scr.at[rows, :], dst_ref=mod_buf.at[shard],
                send_sem=ssem2.at[k - 1], recv_sem=rsem2.at[k - 1], device_id=(px, py, c), device_id_type=MESH)

        sends2 = [copy2(k) for k in range(1, N_SHARD)]
        for cp in sends2:
            cp.start()
        mod_buf[shard] = m_scr[my_rows, :]
        for cp in sends2:
            cp.wait_recv()
        for s in range(N_SHARD):
            mod_ref[:, s * ncol:(s + 1) * ncol] = mod_buf[s] + b_ref[:, s * ncol:(s + 1) * ncol]
        for cp in sends + sends2:
            cp.wait_send()

    vm = pl.BlockSpec(memory_space=pltpu.VMEM)
    return _hosted(
        body, rider, name="ada_forward", grid=(),
        out_shape=(jax.ShapeDtypeStruct((SUBLANES, N_SHARD * ncol), F32),
                   jax.ShapeDtypeStruct((N_DEV * SUBLANES, d), F32)),
        in_specs=[vm, vm, vm], out_specs=(vm, vm),
        scratch_shapes=[pltpu.VMEM((N_DEV * SUBLANES, ncol), F32), pltpu.VMEM((N_SHARD, SUBLANES, ncol), F32),
                        pltpu.SemaphoreType.DMA((N_DEV - 1,)), pltpu.SemaphoreType.DMA((N_DEV - 1,)),
                        pltpu.SemaphoreType.DMA((N_SHARD - 1,)), pltpu.SemaphoreType.DMA((N_SHARD - 1,))],
        compiler_params=_cparams(vmem=VMEM_BIG), args=[c8, w_ada, b_ada])


class _Rider:
    def __init__(self, kind, srcs, owns=()):
        self.kind, self.srcs, self.owns = kind, list(srcs), list(owns)
        n = len(self.srcs)
        sds = jax.ShapeDtypeStruct
        dma = pltpu.SemaphoreType.DMA
        if kind == "gather":
            self.out_shapes = [sds((N_SHARD,) + s.shape, s.dtype) for s in self.srcs]
            self.sems = [dma((n, N_SHARD - 1)), dma((n, N_SHARD - 1)), dma((n, N_SHARD - 1)), dma((n, N_SHARD - 1)),
                         dma((n,)), dma((n,))]
        elif kind == "scatter":
            self.out_shapes = ([sds((N_SHARD - 1,) + s.shape[1:], s.dtype) for s in self.srcs]
                               + [sds(o.shape[1:], o.dtype) for o in self.owns])
            m = max(len(self.owns), 1)
            self.sems = [dma((n, N_SHARD - 1)), dma((n, N_SHARD - 1)), dma((m,)), dma((m,))]
        else:
            self.out_shapes = [sds(s.shape, s.dtype) for s in self.srcs]
            self.sems = [dma((n,)), dma((n,))]

    @property
    def inputs(self):
        return self.srcs + self.owns

    def _halved(self, i):
        a = self.srcs[i]
        tile_rows = SUBLANES * (4 // jnp.dtype(a.dtype).itemsize)
        return self.kind == "gather" and a.shape[0] % (2 * tile_rows) == 0

    def copies(self, ins, outs, sems):
        n = len(self.srcs)
        x, y, c = _my_pos()
        shard = 2 * x + y
        remote, relay = [], []
        if self.kind == "swap":
            ssem, rsem = sems
            for i in range(n):
                remote.append(pltpu.make_async_remote_copy(
                    src_ref=ins[i], dst_ref=outs[i], send_sem=ssem.at[i], recv_sem=rsem.at[i],
                    device_id=(x, y, 1 - c), device_id_type=MESH))
            return remote, relay
        if self.kind == "gather":
            ssem, rsem, ssem2, rsem2, sib_s, sib_r = sems
        else:
            ssem, rsem, sib_s, sib_r = sems
        for i in range(n):
            if self.kind == "gather":
                remote.append(pltpu.make_async_remote_copy(
                    src_ref=ins[i], dst_ref=outs[i].at[shard], send_sem=sib_s.at[i], recv_sem=sib_r.at[i],
                    device_id=(x, y, 1 - c), device_id_type=MESH))
                half = ins[i].shape[0] // 2
                mine = pl.ds(pl.multiple_of(c * half, half), half) if self._halved(i) else None
            for k in range(1, N_SHARD):
                px, py = _flip(x, (k >> 1) & 1), _flip(y, k & 1)
                if self.kind == "gather":
                    src, dst = ins[i], outs[i].at[shard]
                    if mine is not None:
                        src, dst = src.at[mine], dst.at[mine]
                        got = outs[i].at[2 * px + py].at[mine]
                        relay.append(pltpu.make_async_remote_copy(
                            src_ref=got, dst_ref=got, send_sem=ssem2.at[i, k - 1], recv_sem=rsem2.at[i, k - 1],
                            device_id=(x, y, 1 - c), device_id_type=MESH))
                else:
                    src, dst = ins[i].at[2 * px + py], outs[i].at[k - 1]
                remote.append(pltpu.make_async_remote_copy(
                    src_ref=src, dst_ref=dst, send_sem=ssem.at[i, k - 1], recv_sem=rsem.at[i, k - 1],
                    device_id=(px, py, c), device_id_type=MESH))
        if self.kind == "scatter":
            for i in range(len(self.owns)):
                remote.append(pltpu.make_async_remote_copy(
                    src_ref=ins[n + i].at[shard], dst_ref=outs[n + i], send_sem=sib_s.at[i], recv_sem=sib_r.at[i],
                    device_id=(x, y, 1 - c), device_id_type=MESH))
        return remote, relay

    def start(self, ins, outs, sems):
        remote, _ = self.copies(ins, outs, sems)
        for cp in remote:
            cp.start()

    def wait(self, ins, outs, sems):
        remote, relay = self.copies(ins, outs, sems)
        for cp in remote:
            cp.wait_recv()
        for cp in relay:
            cp.start()
        for cp in relay:
            cp.wait_recv()
        for cp in remote + relay:
            cp.wait_send()


class _Riders:
    def __init__(self, riders):
        self.riders = list(riders)
        self.inputs = [a for r in self.riders for a in r.inputs]
        self.out_shapes = [s for r in self.riders for s in r.out_shapes]
        self.sems = [s for r in self.riders for s in r.sems]

    def _split(self, ins, outs, sems):
        for r in self.riders:
            ni, no, ns = len(r.inputs), len(r.out_shapes), len(r.sems)
            yield r, ins[:ni], outs[:no], sems[:ns]
            ins, outs, sems = ins[ni:], outs[no:], sems[ns:]

    def start(self, ins, outs, sems):
        for r, i, o, s in self._split(ins, outs, sems):
            r.start(i, o, s)

    def wait(self, ins, outs, sems):
        for r, i, o, s in self._split(ins, outs, sems):
            r.wait(i, o, s)


def _hosted(body, rider, *, name, grid, out_shape, in_specs, out_specs, scratch_shapes, compiler_params, args):
    out_shape, out_specs = list(out_shape), list(out_specs)
    if rider is None:
        outs = pl.pallas_call(body, name=name, grid=grid, out_shape=tuple(out_shape), in_specs=list(in_specs),
                              out_specs=tuple(out_specs), scratch_shapes=list(scratch_shapes),
                              compiler_params=compiler_params)(*args)
        return list(outs), []
    n_in, n_out, n_scr = len(in_specs), len(out_shape), len(scratch_shapes)
    nr_in, nr_out = len(rider.inputs), len(rider.out_shapes)
    n_steps = 1
    for size in grid:
        n_steps *= size

    def full(*refs):
        ins, refs = refs[:n_in], refs[n_in:]
        r_in, refs = refs[:nr_in], refs[nr_in:]
        outs, refs = refs[:n_out], refs[n_out:]
        r_out, refs = refs[:nr_out], refs[nr_out:]
        scr, sems = refs[:n_scr], refs[n_scr:]
        if grid:
            step = 0
            for ax, size in enumerate(grid):
                step = step * size + pl.program_id(ax)
            pl.when(step == 0)(lambda: rider.start(r_in, r_out, sems))
            body(*ins, *outs, *scr)
            pl.when(step == n_steps - 1)(lambda: rider.wait(r_in, r_out, sems))
        else:
            rider.start(r_in, r_out, sems)
            body(*ins, *outs, *scr)
            rider.wait(r_in, r_out, sems)

    hbm = pl.BlockSpec(memory_space=pl.ANY)
    res = pl.pallas_call(
        full, name=name, grid=grid, out_shape=tuple(out_shape + rider.out_shapes),
        in_specs=list(in_specs) + [hbm] * nr_in, out_specs=tuple(out_specs + [hbm] * nr_out),
        scratch_shapes=list(scratch_shapes) + rider.sems, compiler_params=compiler_params,
    )(*args, *rider.inputs)
    return list(res[:n_out]), list(res[n_out:])


def _ride_alone(rider, name):
    return _hosted(lambda: None, rider, name=name, grid=(), out_shape=[], in_specs=[], out_specs=[], scratch_shapes=[],
                   compiler_params=_cparams(), args=[])[1]


def _rope_rot(t):
    w = t.shape[1]
    lane = lax.broadcasted_iota(jnp.int32, t.shape, 1)
    first = (lane % HEAD_DIM) < (HEAD_DIM // 2)
    return jnp.where(first, pltpu.roll(t, w - HEAD_DIM // 2, 1), pltpu.roll(t, HEAD_DIM // 2, 1))


def _in_proj(x, mod3, g_attn, w_in_t, cos_t, sin_t, seq, rider=None):
    t, d = x.shape
    tm = 2 * TOKEN_TILE
    per_seq = seq // tm
    rope_lo, rope_hi = 3 * NA_WIDTH, 3 * NA_WIDTH + SW_WIDTH + SW_KV_WIDTH
    n_rep = (rope_hi - rope_lo) // LANES

    def body(x_ref, mod_ref, g_ref, w_ref, cos_ref, sin_ref, h_ref, p_ref):
        r, xn = _rms_stats(x_ref[...])
        shift, scale = mod_ref[0, :, 0:d], mod_ref[0, :, d:2 * d]
        hb = ((xn * g_ref[...]) * (1.0 + scale) + shift).astype(BF16)
        h_ref[...] = hb
        p_ref[:, :rope_lo] = _mm_nt(hb, w_ref[:rope_lo, :]).astype(BF16)
        pr = _mm_nt(hb, w_ref[rope_lo:rope_hi, :])
        cos = jnp.concatenate([cos_ref[...]] * n_rep, axis=1)
        sin = jnp.concatenate([sin_ref[...]] * n_rep, axis=1)
        p_ref[:, rope_lo:rope_hi] = (pr * cos + _rope_rot(pr) * sin).astype(BF16)
        p_ref[:, rope_hi:] = _mm_nt(hb, w_ref[rope_hi:, :]).astype(BF16)

    return _hosted(
        body, rider, name="in_proj", grid=(t // tm,),
        out_shape=[jax.ShapeDtypeStruct((t, d), BF16), jax.ShapeDtypeStruct((t, IN_WIDTH), BF16)],
        in_specs=[pl.BlockSpec((tm, d), lambda i: (i, 0)),
                  pl.BlockSpec((1, 1, 6 * d), lambda i: (i // per_seq, 0, 0)),
                  pl.BlockSpec((1, d), lambda i: (0, 0)),
                  pl.BlockSpec((IN_WIDTH, d), lambda i: (0, 0)),
                  pl.BlockSpec((tm, LANES), lambda i: (i % per_seq, 0)),
                  pl.BlockSpec((tm, LANES), lambda i: (i % per_seq, 0))],
        out_specs=[pl.BlockSpec((tm, d), lambda i: (i, 0)), pl.BlockSpec((tm, IN_WIDTH), lambda i: (i, 0))],
        scratch_shapes=[], compiler_params=_cparams(("arbitrary",), VMEM_BIG),
        args=[x, mod3, g_attn, w_in_t, cos_t, sin_t])


def _na_bias_pattern():
    n_dc = 2 * NA_COLS - 1
    j = np.arange(GRID_W)[:, None]
    m = np.arange(GRID_W * LANES)[None, :]
    q, lane = m // LANES, m % LANES
    k = lane % GRID_W
    cs = np.clip(q - NA_COLS // 2, 0, GRID_W - NA_COLS)
    ok = (k >= cs) & (k < cs + NA_COLS)
    hit = ok & (j < 2 * n_dc) & (lane // GRID_W == j // n_dc) & (k - q + (NA_COLS - 1) == j % n_dc)
    return jnp.asarray(hit.astype(np.float32)), jnp.asarray(np.where(ok, 0.0, NEG).astype(np.float32))


def _na_bias_tiles(rows2, expand, mask):
    n, width = rows2.shape[0], expand.shape[1]
    q_step = 16
    step = q_step * LANES

    def body(r_ref, e_ref, m_ref, o_ref):
        flat = jnp.dot(r_ref[...], e_ref[...], precision=lax.Precision.HIGHEST,
                       preferred_element_type=F32) + m_ref[...]
        for qq in range(q_step):
            o_ref[:, qq, :] = flat[:, qq * LANES:(qq + 1) * LANES]

    return pl.pallas_call(
        body, name="na_bias_tiles", grid=(width // step,),
        out_shape=jax.ShapeDtypeStruct((n, GRID_W, LANES), F32),
        in_specs=[pl.BlockSpec(rows2.shape, lambda i: (0, 0)), pl.BlockSpec((expand.shape[0], step), lambda i: (0, i)),
                  pl.BlockSpec((1, step), lambda i: (0, i))],
        out_specs=pl.BlockSpec((n, q_step, LANES), lambda i: (0, i, 0)),
        compiler_params=_cparams(("arbitrary",)),
    )(rows2, expand, mask)


def _na_prepare(k_ref, v_ref, km, vm):
    lane = lax.broadcasted_iota(jnp.int32, k_ref.shape, 1)
    low = lane < HEAD_DIM
    kv = k_ref[...]
    vv = v_ref[...]
    zero = jnp.zeros_like(kv)
    km[0] = jnp.where(low, kv, zero)
    km[1] = jnp.where(low, zero, kv)
    vm[0] = jnp.where(low, vv, zero)
    vm[1] = jnp.where(low, zero, vv)


def _na_window(r, n_rows):
    rs = jnp.clip(r - NA_ROWS // 2, 0, n_rows - NA_ROWS)
    return rs, r - rs


def _na_pair_window(ref, wrows):
    return jnp.concatenate([ref[0, wrows, :], ref[1, wrows, :]], axis=0)


def _na_scores(q, k2, tp_ref, off):
    bias = jnp.concatenate([tp_ref[h, 2 * w - off + (NA_ROWS - 1)] for h in range(2) for w in range(NA_ROWS // 2)],
                           axis=1)
    return _mm_nt(q, k2) * QK_SCALE + bias


def _pair_lse_block(lse):
    lane = lax.broadcasted_iota(jnp.int32, (lse[0].shape[0], LANES), 1)
    return jnp.where(lane < HEAD_DIM, lse[0], lse[1])


def _pair_softmax(s):
    win = s.shape[1] // 2
    halves, lse = [], []
    for h in range(2):
        sh = s[:, h * win:(h + 1) * win]
        m = jnp.max(sh, axis=-1, keepdims=True)
        e = jnp.exp(sh - m)
        l = jnp.sum(e, axis=-1, keepdims=True)
        halves.append(e / l)
        lse.append(m + jnp.log(l))
    return jnp.concatenate(halves, axis=1), _pair_lse_block(lse)


def _pair_probs_from_lse(s, lse_block):
    win = s.shape[1] // 2
    return jnp.concatenate([jnp.exp(s[:, h * win:(h + 1) * win] - lse_block[:, h * HEAD_DIM:h * HEAD_DIM + 1])
                            for h in range(2)], axis=1)


def _na_forward(proj, tiles, batch, seq, rider=None):
    t = proj.shape[0]
    n_rows = seq // GRID_W
    n_pairs = NA_WIDTH // LANES
    win = NA_ROWS * GRID_W

    def body(q_ref, k_ref, v_ref, tp_ref, o_ref, lse_ref, km, vm):
        _na_prepare(k_ref, v_ref, km, vm)

        def scores(r):
            rs, off = _na_window(r, n_rows)
            rows = pl.ds(pl.multiple_of(r * GRID_W, GRID_W), GRID_W)
            wrows = pl.ds(pl.multiple_of(rs * GRID_W, GRID_W), win)
            return rows, wrows, _na_scores(q_ref[rows, :], _na_pair_window(km, wrows), tp_ref, off)

        def finish(rows, wrows, s):
            p, lse = _pair_softmax(s)
            lse_ref[rows, :] = lse
            o_ref[rows, :] = _mm(p.astype(BF16), _na_pair_window(vm, wrows))

        def row_group(i, carry):
            for state in [scores(NA_GROUP * i + j) for j in range(NA_GROUP)]:
                finish(*state)
            return carry

        lax.fori_loop(0, n_rows // NA_GROUP, row_group, 0)

    return _hosted(
        body, rider, name="na_forward", grid=(batch, n_pairs),
        out_shape=[jax.ShapeDtypeStruct((t, NA_WIDTH), F32), jax.ShapeDtypeStruct((t, NA_WIDTH), F32)],
        in_specs=[pl.BlockSpec((seq, LANES), lambda b, p: (b, p)),
                  pl.BlockSpec((seq, LANES), lambda b, p: (b, n_pairs + p)),
                  pl.BlockSpec((seq, LANES), lambda b, p: (b, 2 * n_pairs + p)),
                  pl.BlockSpec((2, 2 * NA_ROWS - 2, GRID_W, LANES), lambda b, p: (p, 0, 0, 0))],
        out_specs=[pl.BlockSpec((seq, LANES), lambda b, p: (b, p)), pl.BlockSpec((seq, LANES), lambda b, p: (b, p))],
        scratch_shapes=[pltpu.VMEM((2, seq, LANES), BF16), pltpu.VMEM((2, seq, LANES), BF16)],
        compiler_params=_cparams(("arbitrary", "arbitrary")), args=[proj, proj, proj, tiles])


def _sw_prepare(kv_ref, g, dst_lo, dst_hi, seq):
    lane = lax.broadcasted_iota(jnp.int32, kv_ref.shape, 1)
    mine = (lane // HEAD_DIM) == g
    kg = jnp.where(mine, kv_ref[...].astype(F32), 0.0)
    kr = pltpu.roll(kg, HEAD_DIM, 1)
    first = g == 0
    zero = jnp.zeros((SW_BLOCK, LANES), BF16)
    for dst, val in ((dst_lo, jnp.where(first, kg, kr)), (dst_hi, jnp.where(first, kr, kg))):
        dst[0:SW_BLOCK, :] = zero
        dst[SW_BLOCK:SW_BLOCK + seq, :] = val.astype(BF16)
        dst[SW_BLOCK + seq:, :] = zero


def _sw_mask(n, seq):
    qi = lax.broadcasted_iota(jnp.int32, (SW_BLOCK, 3 * SW_BLOCK), 0)
    kj = lax.broadcasted_iota(jnp.int32, (SW_BLOCK, 3 * SW_BLOCK), 1)
    kpos = n * SW_BLOCK - SW_BLOCK + kj
    return (jnp.abs(qi + SW_BLOCK - kj) <= SW_BLOCK) & (kpos >= 0) & (kpos < seq)


def _sw_probs(s2, ok, sinks):
    band = s2.shape[1] // 2
    halves, lse = [], []
    for i in range(2):
        s = jnp.where(ok, s2[:, i * band:(i + 1) * band], NEG)
        m = jnp.maximum(jnp.max(s, axis=-1, keepdims=True), sinks[i])
        p = jnp.exp(s - m)
        den = jnp.sum(p, axis=-1, keepdims=True) + jnp.exp(sinks[i] - m)
        halves.append(p / den)
        lse.append(m + jnp.log(den))
    return jnp.concatenate(halves, axis=1), _pair_lse_block(lse)


def _sw_probs_from_lse(s2, ok, sinks, lse_block):
    band = s2.shape[1] // 2
    halves, sink_p = [], []
    for i in range(2):
        lse = lse_block[:, i * HEAD_DIM:i * HEAD_DIM + 1]
        halves.append(jnp.exp(jnp.where(ok, s2[:, i * band:(i + 1) * band], NEG) - lse))
        sink_p.append(jnp.exp(sinks[i] - lse))
    return jnp.concatenate(halves, axis=1), sink_p


def _sw_forward(proj, sink, batch, seq, rider=None):
    t = proj.shape[0]
    n_pairs = SW_WIDTH // LANES
    q_blk = 3 * NA_WIDTH // LANES
    k_blk = q_blk + n_pairs
    n_blocks = seq // SW_BLOCK
    pad = seq + 2 * SW_BLOCK

    def body(sink_ref, q_ref, k_ref, v_ref, o_ref, lse_ref, k_lo, k_hi, v_lo, v_hi):
        hp = pl.program_id(1)
        g = hp // 2
        _sw_prepare(k_ref, g, k_lo, k_hi, seq)
        _sw_prepare(v_ref, g, v_lo, v_hi, seq)

        sinks = (sink_ref[2 * hp], sink_ref[2 * hp + 1])

        def scores(n):
            rows = pl.ds(pl.multiple_of(n * SW_BLOCK, SW_BLOCK), SW_BLOCK)
            wrows = pl.ds(pl.multiple_of(n * SW_BLOCK, SW_BLOCK), 3 * SW_BLOCK)
            k2 = jnp.concatenate([k_lo[wrows, :], k_hi[wrows, :]], axis=0)
            return n, rows, wrows, _mm_nt(q_ref[rows, :], k2) * QK_SCALE

        def finish(n, rows, wrows, s2):
            p, lse = _sw_probs(s2, _sw_mask(n, seq), sinks)
            lse_ref[rows, :] = lse
            v2 = jnp.concatenate([v_lo[wrows, :], v_hi[wrows, :]], axis=0)
            o_ref[rows, :] = _mm(p.astype(BF16), v2)

        def block_group(i, carry):
            for state in [scores(SW_GROUP_BLOCKS * i + j) for j in range(SW_GROUP_BLOCKS)]:
                finish(*state)
            return carry

        lax.fori_loop(0, n_blocks // SW_GROUP_BLOCKS, block_group, 0)

    return _hosted(
        body, rider, name="sw_forward", grid=(batch, n_pairs),
        out_shape=[jax.ShapeDtypeStruct((t, SW_WIDTH), F32), jax.ShapeDtypeStruct((t, SW_WIDTH), F32)],
        in_specs=[pl.BlockSpec(memory_space=pltpu.SMEM),
                  pl.BlockSpec((seq, LANES), lambda b, p: (b, q_blk + p)),
                  pl.BlockSpec((seq, LANES), lambda b, p: (b, k_blk)),
                  pl.BlockSpec((seq, LANES), lambda b, p: (b, k_blk + 1))],
        out_specs=[pl.BlockSpec((seq, LANES), lambda b, p: (b, p)), pl.BlockSpec((seq, LANES), lambda b, p: (b, p))],
        scratch_shapes=[pltpu.VMEM((pad, LANES), BF16)] * 4,
        compiler_params=_cparams(("arbitrary", "arbitrary")), args=[sink, proj, proj, proj])


def _out_proj(oa, ob, g_na, g_sw, w_out, x, mod3, g_ffn, seq):
    t, d = x.shape
    tm = TOKEN_TILE
    per_seq = seq // tm

    def body(oa_ref, ob_ref, gna_ref, gsw_ref, w_ref, x_ref, mod_ref, gf_ref, oab_ref, mix_ref, x1_ref, h2_ref):
        _, na = _rms_stats(oa_ref[...])
        _, nb = _rms_stats(ob_ref[...])
        oab = jnp.concatenate([na * gna_ref[...], nb * gsw_ref[...]], axis=1).astype(BF16)
        oab_ref[...] = oab
        mix = _mm(oab, w_ref[...])
        mix_ref[...] = mix
        gate_a = mod_ref[0, :, 2 * d:3 * d]
        shift_f, scale_f = mod_ref[0, :, 3 * d:4 * d], mod_ref[0, :, 4 * d:5 * d]
        x1 = x_ref[...] + gate_a * mix
        x1_ref[...] = x1
        _, xn = _rms_stats(x1)
        h2_ref[...] = ((xn * gf_ref[...]) * (1.0 + scale_f) + shift_f).astype(BF16)

    tile = lambda w: pl.BlockSpec((tm, w), lambda i: (i, 0))
    vec = lambda w: pl.BlockSpec((1, w), lambda i: (0, 0))
    return pl.pallas_call(
        body, name="out_proj", grid=(t // tm,),
        out_shape=(jax.ShapeDtypeStruct((t, d), BF16), jax.ShapeDtypeStruct((t, d), F32),
                   jax.ShapeDtypeStruct((t, d), F32), jax.ShapeDtypeStruct((t, d), BF16)),
        in_specs=[tile(NA_WIDTH), tile(SW_WIDTH), vec(NA_WIDTH), vec(SW_WIDTH),
                  pl.BlockSpec((d, d), lambda i: (0, 0)), tile(d),
                  pl.BlockSpec((1, 1, 6 * d), lambda i: (i // per_seq, 0, 0)), vec(d)],
        out_specs=(tile(d), tile(d), tile(d), tile(d)),
        compiler_params=_cparams(("arbitrary",), VMEM_BIG),
    )(oa, ob, g_na, g_sw, w_out, x, mod3, g_ffn)


def _up_proj(h2, w_up_halves, rider=None):
    t, d = h2.shape
    tm = 2 * TOKEN_TILE
    w_a, w_b = w_up_halves
    half, wcol = w_a.shape[1], w_a.shape[2]

    def body(h_ref, wa_ref, wb_ref, u_ref):
        u_ref[0] = (_mm(h_ref[:, :half], wa_ref[0]) + _mm(h_ref[:, half:], wb_ref[0])).astype(BF16)

    w_spec = pl.BlockSpec((1, half, wcol), lambda j, i: (j, 0, 0))
    return _hosted(
        body, rider, name="up_proj", grid=(N_SHARD, t // tm),
        out_shape=[jax.ShapeDtypeStruct((2, t, D_FF), BF16)],
        in_specs=[pl.BlockSpec((tm, d), lambda j, i: (i, 0)), w_spec, w_spec],
        out_specs=[pl.BlockSpec((1, tm, wcol), lambda j, i: (j // 2, i, j % 2))],
        scratch_shapes=[], compiler_params=_cparams(("arbitrary", "arbitrary"), VMEM_BIG), args=[h2, w_a, w_b])


def _taps_chunk(load, s, rows, seq):
    halo = 2 * SUBLANES
    cur = load(s, rows)
    above = load(pl.multiple_of(jnp.maximum(s - halo, 0), halo), halo)
    below = load(pl.multiple_of(jnp.minimum(s + rows, seq - halo), halo), halo)
    up = jnp.where(s > 0, above[halo - 1:halo, :], 0.0)
    dn = jnp.where(s + rows < seq, below[0:1, :], 0.0)
    row = lax.broadcasted_iota(jnp.int32, cur.shape, 0)
    prev = jnp.where(row == 0, up, pltpu.roll(cur, 1, 0))
    nxt = jnp.where(row == rows - 1, dn, pltpu.roll(cur, rows - 1, 0))
    return cur, prev, nxt


def _conv_gate(u, conv_w, conv_b, batch, seq, rider=None):
    t = u.shape[1]
    cw = FF_TILE
    rows = CONV_CHUNK

    def body(u_ref, w_ref, b_ref, a_ref):
        def chunk(i, carry):
            s = pl.multiple_of(i * rows, rows)
            gt, prev, nxt = _taps_chunk(lambda at, n: u_ref[1, pl.ds(at, n), :].astype(F32), s, rows, seq)
            gc = prev * w_ref[0:1, :] + gt * w_ref[1:2, :] + nxt * w_ref[2:3, :] + b_ref[...]
            a_ref[pl.ds(s, rows), :] = ((gc * _sigmoid(gc)) * u_ref[0, pl.ds(s, rows), :].astype(F32)).astype(BF16)
            return carry

        lax.fori_loop(0, seq // rows, chunk, 0)

    return _hosted(
        body, rider, name="conv_gate", grid=(batch, D_FF // cw),
        out_shape=[jax.ShapeDtypeStruct((t, D_FF), BF16)],
        in_specs=[pl.BlockSpec((2, seq, cw), lambda b, j: (0, b, j)),
                  pl.BlockSpec((3, cw), lambda b, j: (0, j)), pl.BlockSpec((1, cw), lambda b, j: (0, j))],
        out_specs=[pl.BlockSpec((seq, cw), lambda b, j: (b, j))], scratch_shapes=[],
        compiler_params=_cparams(("arbitrary", "arbitrary"), VMEM_BIG), args=[u, conv_w, conv_b])


def _down_and_loss(a, w_down, x1, mod3, g_final, target, seq):
    t, d = x1.shape
    tm = TOKEN_TILE
    per_seq = seq // tm
    batch = t // seq

    def body(a_ref, w_ref, x1_ref, mod_ref, g_ref, tgt_ref, dx2_ref, dffn_ref, loss_ref, dgate_ref, dg_ref):
        i = pl.program_id(0)
        f = _mm(a_ref[...], w_ref[...])
        gate_f = mod_ref[0, :, 5 * d:6 * d]
        x2 = x1_ref[...] + gate_f * f
        r, xn = _rms_stats(x2)
        err = xn * g_ref[...] - tgt_ref[...]
        part = 0.5 * jnp.sum(jnp.mean(err * err, axis=-1, keepdims=True))
        dy = err / d
        dx2 = _rms_bwd(dy * g_ref[...], xn, r)
        dx2_ref[...] = dx2
        dffn_ref[...] = (dx2 * gate_f).astype(BF16)

        @pl.when(i == 0)
        def _():
            loss_ref[...] = jnp.zeros_like(loss_ref)
            dg_ref[...] = jnp.zeros_like(dg_ref)

        @pl.when(i % per_seq == 0)
        def _():
            dgate_ref[...] = jnp.zeros_like(dgate_ref)

        loss_ref[...] += part
        dg_ref[...] += jnp.sum(dy * xn, axis=0, keepdims=True)
        dgate_ref[0] += jnp.sum(dx2 * f, axis=0, keepdims=True)

    tile = lambda w: pl.BlockSpec((tm, w), lambda i: (i, 0))
    return pl.pallas_call(
        body, name="down_loss", grid=(t // tm,),
        out_shape=(jax.ShapeDtypeStruct((t, d), F32), jax.ShapeDtypeStruct((t, d), BF16),
                   jax.ShapeDtypeStruct((SUBLANES, LANES), F32), jax.ShapeDtypeStruct((batch, 1, d), F32),
                   jax.ShapeDtypeStruct((1, d), F32)),
        in_specs=[tile(D_FF), _resident((D_FF, d)), tile(d),
                  pl.BlockSpec((1, 1, 6 * d), lambda i: (i // per_seq, 0, 0)),
                  pl.BlockSpec((1, d), lambda i: (0, 0)), tile(d)],
        out_specs=(tile(d), tile(d), pl.BlockSpec((SUBLANES, LANES), lambda i: (0, 0)),
                   pl.BlockSpec((1, 1, d), lambda i: (i // per_seq, 0, 0)), pl.BlockSpec((1, d), lambda i: (0, 0))),
        compiler_params=_cparams(("arbitrary",), VMEM_BIG),
    )(a, w_down, x1, mod3, g_final, target)


def _down_weight_grad(a, dffn):
    t, dff = a.shape
    d = dffn.shape[1]
    tk = 2 * TOKEN_TILE
    n_k = t // tk

    def body(a_ref, df_ref, g_ref, gb_ref):
        k = pl.program_id(0)

        @pl.when(k == 0)
        def _():
            g_ref[...] = jnp.zeros_like(g_ref)

        g_ref[...] += _mm_tn(a_ref[...], df_ref[...])

        @pl.when(k == n_k - 1)
        def _():
            gb_ref[...] = g_ref[...].astype(BF16)

    whole = _resident((dff, d))
    return pl.pallas_call(
        body, name="down_weight_grad", grid=(n_k,),
        out_shape=(jax.ShapeDtypeStruct((dff, d), F32), jax.ShapeDtypeStruct((dff, d), BF16)),
        in_specs=[pl.BlockSpec((tk, dff), lambda k: (k, 0)), pl.BlockSpec((tk, d), lambda k: (k, 0))],
        out_specs=(whole, whole),
        compiler_params=_cparams(("arbitrary",), VMEM_BIG),
    )(a, dffn)


def _ffn_backward(dffn, w_down, u, conv_w, conv_b, batch, seq, rider=None):
    t, d = dffn.shape
    cw = FF_TILE
    rows = CONV_CHUNK

    def body(df_ref, wd_ref, u_ref, w_ref, b_ref, du_ref, gcw_ref, gcb_ref, da_scr, dgc_scr):
        b = pl.program_id(1)
        da_scr[...] = _mm_nt(df_ref[...], wd_ref[...])

        @pl.when(b == 0)
        def _():
            gcw_ref[...] = jnp.zeros_like(gcw_ref)
            gcb_ref[...] = jnp.zeros_like(gcb_ref)

        def fold(v):
            return jnp.sum(v.reshape(rows // SUBLANES, SUBLANES, cw), axis=0)

        def chunk(i, carry):
            s = pl.multiple_of(i * rows, rows)
            here = pl.ds(s, rows)
            gt, prev, nxt = _taps_chunk(lambda at, n: u_ref[1, pl.ds(at, n), :].astype(F32), s, rows, seq)
            val, da = u_ref[0, here, :].astype(F32), da_scr[here, :]
            gc = prev * w_ref[0:1, :] + gt * w_ref[1:2, :] + nxt * w_ref[2:3, :] + b_ref[...]
            sg = _sigmoid(gc)
            sl = gc * sg
            du_ref[0, here, :] = (da * sl).astype(BF16)
            dgc = (da * val) * (sg * (1.0 + gc * (1.0 - sg)))
            dgc_scr[here, :] = dgc
            cb, c0, c1, c2 = carry
            return cb + fold(dgc), c0 + fold(dgc * prev), c1 + fold(dgc * gt), c2 + fold(dgc * nxt)

        zero = jnp.zeros((SUBLANES, cw), F32)
        cb, c0, c1, c2 = lax.fori_loop(0, seq // rows, chunk, (zero, zero, zero, zero))
        gcb_ref[...] += jnp.sum(cb, axis=0, keepdims=True)
        gcw_ref[0:1, :] += jnp.sum(c0, axis=0, keepdims=True)
        gcw_ref[1:2, :] += jnp.sum(c1, axis=0, keepdims=True)
        gcw_ref[2:3, :] += jnp.sum(c2, axis=0, keepdims=True)

        def chunk2(i, carry):
            s = pl.multiple_of(i * rows, rows)
            dgc, dprev, dnxt = _taps_chunk(lambda at, n: dgc_scr[pl.ds(at, n), :], s, rows, seq)
            du_ref[1, pl.ds(s, rows), :] = (dnxt * w_ref[0:1, :] + dgc * w_ref[1:2, :]
                                            + dprev * w_ref[2:3, :]).astype(BF16)
            return carry

        lax.fori_loop(0, seq // rows, chunk2, 0)

    return _hosted(
        body, rider, name="ffn_backward", grid=(D_FF // cw, batch),
        out_shape=[jax.ShapeDtypeStruct((2, t, D_FF), BF16),
                   jax.ShapeDtypeStruct((3, D_FF), F32), jax.ShapeDtypeStruct((1, D_FF), F32)],
        in_specs=[pl.BlockSpec((seq, d), lambda j, b: (b, 0)), pl.BlockSpec((cw, d), lambda j, b: (j, 0)),
                  pl.BlockSpec((2, seq, cw), lambda j, b: (0, b, j)),
                  pl.BlockSpec((3, cw), lambda j, b: (0, j)), pl.BlockSpec((1, cw), lambda j, b: (0, j))],
        out_specs=[pl.BlockSpec((2, seq, cw), lambda j, b: (0, b, j)),
                   pl.BlockSpec((3, cw), lambda j, b: (0, j)), pl.BlockSpec((1, cw), lambda j, b: (0, j))],
        scratch_shapes=[pltpu.VMEM((seq, cw), F32), pltpu.VMEM((seq, cw), F32)],
        compiler_params=_cparams(("arbitrary", "arbitrary"), VMEM_BIG), args=[dffn, w_down, u, conv_w, conv_b])


def _up_backward(du, w_up, x1, mod3, g_ffn, dx2, mix, seq, rider=None):
    _, t, _ = du.shape
    d = x1.shape[1]
    tm = TOKEN_TILE
    per_seq = seq // tm
    batch = t // seq
    w_a, w_b = w_up
    half, wcol = w_a.shape[1], w_a.shape[2]

    def body(du_ref, wa_ref, wb_ref, x1_ref, mod_ref, g_ref, dx2_ref, mix_ref,
             dx1_ref, dmix_ref, dsh_ref, dsc_ref, dga_ref, dg_ref):
        i = pl.program_id(0)
        parts = []
        for w_ref in (wa_ref, wb_ref):
            acc = jnp.zeros((tm, half), F32)
            for j in range(N_SHARD):
                acc = acc + _mm_nt(du_ref[j // 2, :, (j % 2) * wcol:(j % 2 + 1) * wcol], w_ref[j])
            parts.append(acc)
        dh = jnp.concatenate(parts, axis=1)
        gate_a = mod_ref[0, :, 2 * d:3 * d]
        scale_f = mod_ref[0, :, 4 * d:5 * d]
        r, xn = _rms_stats(x1_ref[...])
        xg = xn * g_ref[...]
        dxg = dh * (1.0 + scale_f)
        dx1 = dx2_ref[...] + _rms_bwd(dxg * g_ref[...], xn, r)
        dx1_ref[...] = dx1
        dmix_ref[...] = (dx1 * gate_a).astype(BF16)

        @pl.when(i == 0)
        def _():
            dg_ref[...] = jnp.zeros_like(dg_ref)

        @pl.when(i % per_seq == 0)
        def _():
            dsh_ref[...] = jnp.zeros_like(dsh_ref)
            dsc_ref[...] = jnp.zeros_like(dsc_ref)
            dga_ref[...] = jnp.zeros_like(dga_ref)

        dg_ref[...] += jnp.sum(dxg * xn, axis=0, keepdims=True)
        dsh_ref[0] += jnp.sum(dh, axis=0, keepdims=True)
        dsc_ref[0] += jnp.sum(dh * xg, axis=0, keepdims=True)
        dga_ref[0] += jnp.sum(dx1 * mix_ref[...], axis=0, keepdims=True)

    tile = lambda w: pl.BlockSpec((tm, w), lambda i: (i, 0))
    per_b = pl.BlockSpec((1, 1, d), lambda i: (i // per_seq, 0, 0))
    small = jax.ShapeDtypeStruct((batch, 1, d), F32)
    return _hosted(
        body, rider, name="up_backward", grid=(t // tm,),
        out_shape=[jax.ShapeDtypeStruct((t, d), F32), jax.ShapeDtypeStruct((t, d), BF16), small, small, small,
                   jax.ShapeDtypeStruct((1, d), F32)],
        in_specs=[pl.BlockSpec((2, tm, D_FF), lambda i: (0, i, 0)),
                  _resident((N_SHARD, half, wcol)), _resident((N_SHARD, half, wcol)), tile(d),
                  pl.BlockSpec((1, 1, 6 * d), lambda i: (i // per_seq, 0, 0)),
                  pl.BlockSpec((1, d), lambda i: (0, 0)), tile(d), tile(d)],
        out_specs=[tile(d), tile(d), per_b, per_b, per_b, pl.BlockSpec((1, d), lambda i: (0, 0))],
        scratch_shapes=[], compiler_params=_cparams(("arbitrary",), VMEM_BIG),
        args=[du, w_a, w_b, x1, mod3, g_ffn, dx2, mix])


def _up_weight_grad(h2, du, rider=None):
    t, d = h2.shape
    tk = 2 * TOKEN_TILE
    wcol = D_FF // 2
    half = d // 2
    n_k = t // tk

    def body(h_ref, du_ref, ga_ref, gb_ref, ga16_ref, gb16_ref):
        k = pl.program_id(1)

        @pl.when(k == 0)
        def _():
            ga_ref[...] = jnp.zeros_like(ga_ref)
            gb_ref[...] = jnp.zeros_like(gb_ref)

        du = du_ref[0]
        ga_ref[0] += _mm_tn(h_ref[:, :half], du)
        gb_ref[0] += _mm_tn(h_ref[:, half:], du)

        @pl.when(k == n_k - 1)
        def _():
            ga16_ref[...] = ga_ref[...].astype(BF16)
            gb16_ref[...] = gb_ref[...].astype(BF16)

    g_spec = pl.BlockSpec((1, half, wcol), lambda j, k: (j, 0, 0))
    f32_out = jax.ShapeDtypeStruct((N_SHARD, half, wcol), F32)
    b16_out = jax.ShapeDtypeStruct((N_SHARD, half, wcol), BF16)
    return _hosted(
        body, rider, name="up_weight_grad", grid=(N_SHARD, n_k),
        out_shape=[f32_out, f32_out, b16_out, b16_out],
        in_specs=[pl.BlockSpec((tk, d), lambda j, k: (k, 0)),
                  pl.BlockSpec((1, tk, wcol), lambda j, k: (j // 2, k, j % 2))],
        out_specs=[g_spec, g_spec, g_spec, g_spec], scratch_shapes=[],
        compiler_params=_cparams(("arbitrary", "arbitrary"), VMEM_BIG), args=[h2, du])


def _out_backward(dmix, w_out, oab, oa, ob, g_na, g_sw):
    t, d = dmix.shape
    tm = 2 * TOKEN_TILE
    hw = NA_WIDTH

    def body(dm_ref, w_ref, oab_ref, oa_ref, ob_ref, gna_ref, gsw_ref,
             doa_ref, dob_ref, gw_ref, gwb_ref, dgna_ref, dgsw_ref):
        @pl.when(pl.program_id(0) == 0)
        def _():
            gw_ref[...] = jnp.zeros_like(gw_ref)
            dgna_ref[...] = jnp.zeros_like(dgna_ref)
            dgsw_ref[...] = jnp.zeros_like(dgsw_ref)

        dm = dm_ref[...]
        gw_ref[...] += _mm_tn(oab_ref[...], dm)

        @pl.when(pl.program_id(0) == t // tm - 1)
        def _():
            gwb_ref[...] = gw_ref[...].astype(BF16)

        do = _mm_nt(dm, w_ref[...])
        for raw_ref, g_ref, dst_ref, dg_ref, lo in ((oa_ref, gna_ref, doa_ref, dgna_ref, 0),
                                                     (ob_ref, gsw_ref, dob_ref, dgsw_ref, hw)):
            r, xn = _rms_stats(raw_ref[...])
            dpart = do[:, lo:lo + hw]
            dg_ref[...] += jnp.sum(dpart * xn, axis=0, keepdims=True)
            dst_ref[...] = _rms_bwd(dpart * g_ref[...], xn, r).astype(BF16)

    tile = lambda w: pl.BlockSpec((tm, w), lambda i: (i, 0))
    vec = lambda w: pl.BlockSpec((1, w), lambda i: (0, 0))
    return pl.pallas_call(
        body, name="out_backward", grid=(t // tm,),
        out_shape=(jax.ShapeDtypeStruct((t, hw), BF16), jax.ShapeDtypeStruct((t, hw), BF16),
                   jax.ShapeDtypeStruct((d, d), F32), jax.ShapeDtypeStruct((d, d), BF16),
                   jax.ShapeDtypeStruct((1, hw), F32), jax.ShapeDtypeStruct((1, hw), F32)),
        in_specs=[tile(d), pl.BlockSpec((d, d), lambda i: (0, 0)), tile(d), tile(hw), tile(hw), vec(hw), vec(hw)],
        out_specs=(tile(hw), tile(hw), pl.BlockSpec((d, d), lambda i: (0, 0)), pl.BlockSpec((d, d), lambda i: (0, 0)),
                   vec(hw), vec(hw)),
        compiler_params=_cparams(("arbitrary",), VMEM_BIG),
    )(dmix, w_out, oab, oa, ob, g_na, g_sw)


def _na_backward(proj, d_o, lse, tiles, batch, seq, rider=None):
    t = proj.shape[0]
    n_rows = seq // GRID_W
    n_pairs = NA_WIDTH // LANES
    win = NA_ROWS * GRID_W
    n_tiles = 2 * NA_ROWS - 2

    def body(q_ref, k_ref, v_ref, do_ref, lse_ref, tp_ref, dq_ref, dk_ref, dv_ref, dtp_ref, km, vm, dk_acc, dv_acc):
        @pl.when(pl.program_id(1) == 0)
        def _():
            dtp_ref[...] = jnp.zeros_like(dtp_ref)

        _na_prepare(k_ref, v_ref, km, vm)
        dk_acc[...] = jnp.zeros_like(dk_acc)
        dv_acc[...] = jnp.zeros_like(dv_acc)
        low = lax.broadcasted_iota(jnp.int32, (win, LANES), 1) < HEAD_DIM

        def scores(r):
            rs, off = _na_window(r, n_rows)
            rows = pl.ds(pl.multiple_of(r * GRID_W, GRID_W), GRID_W)
            wrows = pl.ds(pl.multiple_of(rs * GRID_W, GRID_W), win)
            q, do = q_ref[rows, :], do_ref[rows, :]
            k2 = _na_pair_window(km, wrows)
            s = _na_scores(q, k2, tp_ref, off)
            dp = _mm_nt(do, _na_pair_window(vm, wrows))
            return rows, wrows, off, q, do, k2, s, dp

        def finish(rows, wrows, off, q, do, k2, s, dp):
            p = _pair_probs_from_lse(s, lse_ref[rows, :])
            parts = []
            for h in range(2):
                ph, dph = p[:, h * win:(h + 1) * win], dp[:, h * win:(h + 1) * win]
                dsh = ph * (dph - jnp.sum(ph * dph, axis=-1, keepdims=True))
                for w in range(NA_ROWS // 2):
                    dtp_ref[h, 2 * w - off + (NA_ROWS - 1)] += dsh[:, w * LANES:(w + 1) * LANES]
                parts.append(dsh)
            dsb = (jnp.concatenate(parts, axis=1) * QK_SCALE).astype(BF16)
            dq_ref[rows, :] = _mm(dsb, k2).astype(BF16)
            dk2 = _mm_tn(dsb, q)
            dv2 = _mm_tn(p.astype(BF16), do)
            dk_acc[wrows, :] += jnp.where(low, dk2[:win], dk2[win:])
            dv_acc[wrows, :] += jnp.where(low, dv2[:win], dv2[win:])

        def row_group(i, carry):
            for state in [scores(NA_GROUP * i + j) for j in range(NA_GROUP)]:
                finish(*state)
            return carry

        lax.fori_loop(0, n_rows // NA_GROUP, row_group, 0)
        dk_ref[...] = dk_acc[...].astype(BF16)
        dv_ref[...] = dv_acc[...].astype(BF16)

    blk = lambda off: pl.BlockSpec((seq, LANES), lambda p, b: (b, off + p))
    out = jax.ShapeDtypeStruct((t, NA_WIDTH), BF16)
    return _hosted(
        body, rider, name="na_backward", grid=(n_pairs, batch),
        out_shape=[out, out, out, jax.ShapeDtypeStruct(tiles.shape, F32)],
        in_specs=[blk(0), blk(n_pairs), blk(2 * n_pairs), blk(0), blk(0),
                  pl.BlockSpec((2, n_tiles, GRID_W, LANES), lambda p, b: (p, 0, 0, 0))],
        out_specs=[blk(0), blk(0), blk(0), pl.BlockSpec((2, n_tiles, GRID_W, LANES), lambda p, b: (p, 0, 0, 0))],
        scratch_shapes=[pltpu.VMEM((2, seq, LANES), BF16), pltpu.VMEM((2, seq, LANES), BF16),
                        pltpu.VMEM((seq, LANES), F32), pltpu.VMEM((seq, LANES), F32)],
        compiler_params=_cparams(("arbitrary", "arbitrary")), args=[proj, proj, proj, d_o, lse, tiles])


def _na_bias_grad(dtiles, expand):
    n = dtiles.shape[0]

    def body(t_ref, e_ref, o_ref):
        flat = jnp.concatenate([t_ref[:, qq, :] for qq in range(GRID_W)], axis=1)
        o_ref[...] = lax.dot_general(flat, e_ref[...], (((1,), (1,)), ((), ())),
                                     precision=lax.Precision.HIGHEST, preferred_element_type=F32)

    return pl.pallas_call(
        body, name="na_bias_grad",
        out_shape=jax.ShapeDtypeStruct((n, expand.shape[0]), F32),
        compiler_params=_cparams(vmem=VMEM_BIG),
    )(dtiles, expand)


def _sw_backward(proj, d_o, lse, sink, batch, seq, rider=None):
    t = proj.shape[0]
    n_pairs = SW_WIDTH // LANES
    q_blk = 3 * NA_WIDTH // LANES
    k_blk = q_blk + n_pairs
    n_blocks = seq // SW_BLOCK
    pad = seq + 2 * SW_BLOCK

    def body(sink_ref, q_ref, k_ref, v_ref, do_ref, lse_ref, dq_ref, dk_ref, dv_ref, dsk_ref,
             k_lo, k_hi, v_lo, v_hi, dk_loc, dv_loc, dk_tot, dv_tot):
        hp = pl.program_id(1)
        g = hp // 2
        _sw_prepare(k_ref, g, k_lo, k_hi, seq)
        _sw_prepare(v_ref, g, v_lo, v_hi, seq)
        dk_loc[...] = jnp.zeros_like(dk_loc)
        dv_loc[...] = jnp.zeros_like(dv_loc)

        @pl.when(hp == 0)
        def _():
            dk_tot[...] = jnp.zeros_like(dk_tot)
            dv_tot[...] = jnp.zeros_like(dv_tot)

        band = 3 * SW_BLOCK
        low = lax.broadcasted_iota(jnp.int32, (band, LANES), 1) < HEAD_DIM

        sinks = (sink_ref[2 * hp], sink_ref[2 * hp + 1])

        def scores(n):
            rows = pl.ds(pl.multiple_of(n * SW_BLOCK, SW_BLOCK), SW_BLOCK)
            wrows = pl.ds(pl.multiple_of(n * SW_BLOCK, SW_BLOCK), band)
            qb, do = q_ref[rows, :], do_ref[rows, :]
            k2 = jnp.concatenate([k_lo[wrows, :], k_hi[wrows, :]], axis=0)
            v2 = jnp.concatenate([v_lo[wrows, :], v_hi[wrows, :]], axis=0)
            return n, rows, wrows, qb, do, k2, _mm_nt(qb, k2) * QK_SCALE, _mm_nt(do, v2)

        def finish(sink_acc, n, rows, wrows, qb, do, k2, s2, dp):
            p, ps = _sw_probs_from_lse(s2, _sw_mask(n, seq), sinks, lse_ref[rows, :])
            parts, new = [], []
            for i in range(2):
                ph, dph = p[:, i * band:(i + 1) * band], dp[:, i * band:(i + 1) * band]
                delta = jnp.sum(ph * dph, axis=-1, keepdims=True)
                parts.append(ph * (dph - delta))
                new.append(sink_acc[i] - ps[i] * delta)
            dsb = (jnp.concatenate(parts, axis=1) * QK_SCALE).astype(BF16)
            dq_ref[rows, :] = _mm(dsb, k2)
            dk2 = _mm_tn(dsb, qb)
            dv2 = _mm_tn(p.astype(BF16), do)
            dk_loc[wrows, :] += jnp.where(low, dk2[:band], dk2[band:])
            dv_loc[wrows, :] += jnp.where(low, dv2[:band], dv2[band:])
            return tuple(new)

        def block_group(i, carry):
            for state in [scores(SW_GROUP_BLOCKS * i + j) for j in range(SW_GROUP_BLOCKS)]:
                carry = finish(carry, *state)
            return carry

        zero = jnp.zeros((SW_BLOCK, 1), F32)
        s0, s1 = lax.fori_loop(0, n_blocks // SW_GROUP_BLOCKS, block_group, (zero, zero))
        row = lax.broadcasted_iota(jnp.int32, (SUBLANES, LANES), 0)
        dsk_ref[0, 0] = jnp.where(row == 0, jnp.sum(s0), jnp.where(row == 1, jnp.sum(s1), 0.0))

        lane_s = lax.broadcasted_iota(jnp.int32, (seq, LANES), 1)
        mine_g = (lane_s // HEAD_DIM) == g
        for loc, tot in ((dk_loc, dk_tot), (dv_loc, dv_tot)):
            part = loc[SW_BLOCK:SW_BLOCK + seq, :]
            tot[...] += jnp.where(mine_g, part + pltpu.roll(part, HEAD_DIM, 1), 0.0)

        @pl.when(hp == n_pairs - 1)
        def _():
            dk_ref[...] = dk_tot[...]
            dv_ref[...] = dv_tot[...].astype(BF16)

    return _hosted(
        body, rider, name="sw_backward", grid=(batch, n_pairs),
        out_shape=[jax.ShapeDtypeStruct((t, SW_WIDTH), F32), jax.ShapeDtypeStruct((t, LANES), F32),
                   jax.ShapeDtypeStruct((t, LANES), BF16), jax.ShapeDtypeStruct((batch, n_pairs, SUBLANES, LANES), F32)],
        in_specs=[pl.BlockSpec(memory_space=pltpu.SMEM),
                  pl.BlockSpec((seq, LANES), lambda b, p: (b, q_blk + p)),
                  pl.BlockSpec((seq, LANES), lambda b, p: (b, k_blk)),
                  pl.BlockSpec((seq, LANES), lambda b, p: (b, k_blk + 1)),
                  pl.BlockSpec((seq, LANES), lambda b, p: (b, p)), pl.BlockSpec((seq, LANES), lambda b, p: (b, p))],
        out_specs=[pl.BlockSpec((seq, LANES), lambda b, p: (b, p)), pl.BlockSpec((seq, LANES), lambda b, p: (b, 0)),
                   pl.BlockSpec((seq, LANES), lambda b, p: (b, 0)),
                   pl.BlockSpec((1, 1, SUBLANES, LANES), lambda b, p: (b, p, 0, 0))],
        scratch_shapes=[pltpu.VMEM((pad, LANES), BF16)] * 4 + [pltpu.VMEM((pad, LANES), F32)] * 2
        + [pltpu.VMEM((seq, LANES), F32)] * 2,
        compiler_params=_cparams(("arbitrary", "arbitrary")), args=[sink, proj, proj, proj, d_o, lse])


def _in_backward(dqkv_a, dq_b, dk_b, dv_b, w_in_t, h1, x, mod3, g_attn, dx1, cos_t, sin_t, seq):
    t, d = x.shape
    tm = TOKEN_TILE
    per_seq = seq // tm
    batch = t // seq
    dqa, dka, dva = dqkv_a
    n_q = SW_WIDTH // LANES

    def body(dqa_ref, dka_ref, dva_ref, dqb_ref, dkb_ref, dvb_ref, w_ref, h_ref, x_ref, mod_ref, g_ref, dx1_ref,
             cos_ref, sin_ref, dx_ref, gw_ref, gwb_ref, dsh_ref, dsc_ref, dg_ref):
        i = pl.program_id(0)

        @pl.when(i == 0)
        def _():
            gw_ref[...] = jnp.zeros_like(gw_ref)
            dg_ref[...] = jnp.zeros_like(dg_ref)

        @pl.when(i % per_seq == 0)
        def _():
            dsh_ref[...] = jnp.zeros_like(dsh_ref)
            dsc_ref[...] = jnp.zeros_like(dsc_ref)

        dr = jnp.concatenate([dqb_ref[...], dkb_ref[...]], axis=1)
        cos = jnp.concatenate([cos_ref[...]] * (n_q + 1), axis=1)
        sin = jnp.concatenate([sin_ref[...]] * (n_q + 1), axis=1)
        dr = dr * cos + _rope_rot(dr * sin)
        dproj = jnp.concatenate([dqa_ref[...], dka_ref[...], dva_ref[...], dr.astype(BF16), dvb_ref[...]], axis=1)
        gw_ref[...] += _mm_tn(dproj, h_ref[...])

        @pl.when(i == t // tm - 1)
        def _():
            gwb_ref[...] = gw_ref[...].astype(BF16)

        dh = _mm(dproj, w_ref[...])
        scale = mod_ref[0, :, d:2 * d]
        r, xn = _rms_stats(x_ref[...])
        xg = xn * g_ref[...]
        dxg = dh * (1.0 + scale)
        dx_ref[...] = dx1_ref[...] + _rms_bwd(dxg * g_ref[...], xn, r)
        dg_ref[...] += jnp.sum(dxg * xn, axis=0, keepdims=True)
        dsh_ref[0] += jnp.sum(dh, axis=0, keepdims=True)
        dsc_ref[0] += jnp.sum(dh * xg, axis=0, keepdims=True)

    tile = lambda w: pl.BlockSpec((tm, w), lambda i: (i, 0))
    per_b = pl.BlockSpec((1, 1, d), lambda i: (i // per_seq, 0, 0))
    small = jax.ShapeDtypeStruct((batch, 1, d), F32)
    rope = pl.BlockSpec((tm, LANES), lambda i: (i % per_seq, 0))
    return pl.pallas_call(
        body, name="in_backward", grid=(t // tm,),
        out_shape=(jax.ShapeDtypeStruct((t, d), F32), jax.ShapeDtypeStruct((IN_WIDTH, d), F32),
                   jax.ShapeDtypeStruct((IN_WIDTH, d), BF16), small, small, jax.ShapeDtypeStruct((1, d), F32)),
        in_specs=[tile(NA_WIDTH), tile(NA_WIDTH), tile(NA_WIDTH), tile(SW_WIDTH), tile(LANES), tile(LANES),
                  _resident((IN_WIDTH, d)), tile(d), tile(d),
                  pl.BlockSpec((1, 1, 6 * d), lambda i: (i // per_seq, 0, 0)),
                  pl.BlockSpec((1, d), lambda i: (0, 0)), tile(d), rope, rope],
        out_specs=(tile(d), _resident((IN_WIDTH, d)), _resident((IN_WIDTH, d)),
                   per_b, per_b, pl.BlockSpec((1, d), lambda i: (0, 0))),
        compiler_params=_cparams(("arbitrary",), VMEM_BIG),
    )(dqa, dka, dva, dq_b, dk_b, dv_b, w_in_t, h1, x, mod3, g_attn, dx1, cos_t, sin_t)


def _ada_weight_grad(sc_all, dmod_cols):
    d = sc_all.shape[1]
    ncol = dmod_cols.shape[1]

    def body(s_ref, m_ref, o_ref):
        o_ref[...] = _mm_tn(s_ref[...].astype(BF16), m_ref[...].astype(BF16))

    return pl.pallas_call(
        body, name="ada_weight_grad",
        out_shape=jax.ShapeDtypeStruct((d, ncol), F32),
        compiler_params=_cparams(vmem=VMEM_BIG),
    )(sc_all, dmod_cols)


def _row_tile(rows, cols):
    target = max(SUBLANES, (1 << 20) // (4 * cols))
    best = rows
    for cand in range(SUBLANES, rows + 1, SUBLANES):
        if rows % cand == 0 and cand <= target:
            best = cand
    return best if rows % SUBLANES == 0 else rows


def _sum_slots(parts, name):
    n = len(parts)
    _, rows, cols = parts[0][0].shape
    tr = _row_tile(rows, cols)
    per = rows // tr

    def body(*refs):
        o_ref = refs[-1]
        for q in range(n):
            @pl.when(pl.program_id(0) == q)
            def _(q=q):
                p_ref, own_ref = refs[2 * q], refs[2 * q + 1]
                o_ref[...] = ((own_ref[...] + p_ref[0].astype(F32)) + p_ref[1].astype(F32)) + p_ref[2].astype(F32)

    in_specs, args = [], []
    for q, (recv, own) in enumerate(parts):
        in_specs.append(pl.BlockSpec((N_SHARD - 1, tr, cols), lambda p, i, q=q: (0, jnp.where(p == q, i, 0), 0)))
        in_specs.append(pl.BlockSpec((tr, cols), lambda p, i, q=q: (jnp.where(p == q, i, 0), 0)))
        args += [recv, own]
    return pl.pallas_call(
        body, name=name, grid=(n, per),
        out_shape=jax.ShapeDtypeStruct((n * rows, cols), F32),
        in_specs=in_specs, out_specs=pl.BlockSpec((tr, cols), lambda p, i: (p * per + i, 0)),
        compiler_params=_cparams(("arbitrary", "arbitrary")),
    )(*args)


def _adamw_math(w, g, m, v):
    m2 = ADAM_B1 * m + (1.0 - ADAM_B1) * g
    v2 = ADAM_B2 * v + (1.0 - ADAM_B2) * (g * g)
    m_hat = m2 / (1.0 - ADAM_B1 ** ADAM_STEP)
    v_hat = v2 / (1.0 - ADAM_B2 ** ADAM_STEP)
    return -ADAM_LR * (m_hat / (jnp.sqrt(v_hat) + ADAM_EPS) + ADAM_WD * w), m2, v2


def _small_sums(partials, dmod, rider=None):
    moving = list(partials) + [dmod]
    n_mov = len(moving)

    def body(*refs):
        mov, refs = refs[:n_mov], refs[n_mov:]
        sums_out, refs = refs[:n_mov - 1], refs[n_mov - 1:]
        b_out, dmod_out, refs = refs[0], refs[1], refs[2:]
        everyone, (ssem, rsem) = refs[:n_mov], refs[n_mov:]
        x, y, c = _my_pos()
        me = 4 * x + 2 * y + c
        cps = []
        for a in range(n_mov):
            everyone[a][me] = mov[a][...]
            for k in range(1, N_DEV):
                peer = (_flip(x, (k >> 2) & 1), _flip(y, (k >> 1) & 1), _flip(c, k & 1))
                cps.append(pltpu.make_async_remote_copy(
                    src_ref=everyone[a].at[me], dst_ref=everyone[a].at[me], send_sem=ssem.at[a, k - 1],
                    recv_sem=rsem.at[a, k - 1], device_id=peer, device_id_type=MESH))
        for cp in cps:
            cp.start()
        for cp in cps:
            cp.wait_recv()

        def total(a):
            acc = everyone[a][0]
            for dev in range(1, N_DEV):
                acc = acc + everyone[a][dev]
            return acc

        for a in range(n_mov - 1):
            sums_out[a][...] = total(a)
        b_out[...] = jnp.sum(total(n_mov - 1), axis=0, keepdims=True)
        dmod_out[...] = everyone[n_mov - 1][...]
        for cp in cps:
            cp.wait_send()

    vm = pl.BlockSpec(memory_space=pltpu.VMEM)
    sds = jax.ShapeDtypeStruct
    out_shape = [sds(p.shape, F32) for p in partials]
    out_shape += [sds((1, dmod.shape[1]), F32), sds((N_DEV,) + dmod.shape, F32)]
    return _hosted(
        body, rider, name="small_sums", grid=(), out_shape=out_shape,
        in_specs=[vm] * n_mov, out_specs=[vm] * len(out_shape),
        scratch_shapes=[pltpu.VMEM((N_DEV,) + a.shape, F32) for a in moving]
        + [pltpu.SemaphoreType.DMA((n_mov, N_DEV - 1)), pltpu.SemaphoreType.DMA((n_mov, N_DEV - 1))],
        compiler_params=_cparams(vmem=VMEM_BIG), args=moving)


def _small_adamw(states, grads):
    n = len(states)

    def body(*refs):
        g_refs, wmv, res = refs[:n], refs[n:4 * n], refs[4 * n:]
        for j in range(n):
            g = g_refs[j][...]
            delta, m2, v2 = _adamw_math(wmv[3 * j][...], g, wmv[3 * j + 1][...], wmv[3 * j + 2][...])
            res[4 * j][...] = g
            res[4 * j + 1][...] = delta
            res[4 * j + 2][...] = m2
            res[4 * j + 3][...] = v2

    out_shape = []
    for w, _, _ in states:
        out_shape += [jax.ShapeDtypeStruct(w.shape, F32)] * 4
    outs = pl.pallas_call(body, name="small_adamw", out_shape=tuple(out_shape),
                          compiler_params=_cparams(vmem=VMEM_BIG))(*grads, *[a for st in states for a in st])
    return [outs[4 * j:4 * j + 4] for j in range(n)]


def _adamw(w, grads, m, v, name):
    rows, cols = w.shape
    tr = _row_tile(rows, cols)
    ng = len(grads)

    def body(*refs):
        w_ref = refs[0]
        g_refs = refs[1:1 + ng]
        m_ref, v_ref = refs[1 + ng], refs[2 + ng]
        g_out, d_out, m_out, v_out = refs[3 + ng:]
        g = g_refs[0][...]
        for extra in g_refs[1:]:
            g = g + extra[...]
        g_out[...] = g
        d_out[...], m_out[...], v_out[...] = _adamw_math(w_ref[...], g, m_ref[...], v_ref[...])

    spec = pl.BlockSpec((tr, cols), lambda i: (i, 0))
    out = jax.ShapeDtypeStruct((rows, cols), F32)
    return pl.pallas_call(
        body, name=name, grid=(rows // tr,),
        out_shape=(out, out, out, out),
        in_specs=[spec] * (3 + ng), out_specs=(spec, spec, spec, spec),
        compiler_params=_cparams(("arbitrary",)),
    )(w, *grads, m, v)


def _rope_tables(seq):
    half = HEAD_DIM // 2
    inv = np.float32(ROPE_THETA) ** (-np.arange(half, dtype=np.float32) / np.float32(half))
    ang = (np.arange(seq, dtype=np.float32)[:, None] * inv[None, :]).astype(np.float64)
    cos, sin = np.cos(ang).astype(np.float32), np.sin(ang).astype(np.float32)
    cos_t = np.concatenate([cos, cos, cos, cos], axis=1)
    sin_t = np.concatenate([-sin, sin, -sin, sin], axis=1)
    return jnp.asarray(cos_t), jnp.asarray(sin_t)


def kernel(x, c, w_ada, b_ada, g_attn, w_in, na_rpb, sw_sink, g_na_out, g_sw_out, w_out, g_ffn, w_up, conv_w, conv_b, w_down, g_final, loss_target, m_w_ada, m_b_ada, m_g_attn, m_w_in, m_na_rpb, m_sw_sink, m_g_na_out, m_g_sw_out, m_w_out, m_g_ffn, m_w_up, m_conv_w, m_conv_b, m_w_down, m_g_final, v_w_ada, v_b_ada, v_g_attn, v_w_in, v_na_rpb, v_sw_sink, v_g_na_out, v_g_sw_out, v_w_out, v_g_ffn, v_w_up, v_conv_w, v_conv_b, v_w_down, v_g_final):
    batch, seq, d = x.shape
    t = batch * seq
    assert d == D_MODEL and seq % (NA_ROWS * GRID_W) == 0 and seq % TOKEN_TILE == 0 and batch <= SUBLANES
    shard = 2 * lax.axis_index("x") + lax.axis_index("y")
    xt = x.reshape(t, d)
    tgt = loss_target.reshape(t, d)

    c8 = jnp.pad(c, ((0, SUBLANES - batch), (0, 0)))
    w_in_t_s = jnp.transpose(w_in[0]).astype(BF16)
    (mod8, sc_all), (w_in_g,) = _ada_forward(c8, w_ada[0], b_ada, _Rider("gather", [w_in_t_s]))
    mod3 = mod8[:batch].reshape(batch, 1, 6 * d)
    w_in_t = w_in_g.reshape(IN_WIDTH, d)

    cos_t, sin_t = _rope_tables(seq)
    (h1, proj), _ = _in_proj(xt, mod3, g_attn, w_in_t, cos_t, sin_t, seq)
    n_heads = NA_WIDTH // HEAD_DIM
    n_tiles, n_dc = 2 * NA_ROWS - 2, 2 * NA_COLS - 1
    expand, neg_mask = _na_bias_pattern()
    rpb = na_rpb[0]
    rows2 = jnp.concatenate([rpb[:, :-1, :], rpb[:, 1:, :]], axis=2).reshape(n_heads * n_tiles, 2 * n_dc)
    rows2 = jnp.pad(rows2, ((0, 0), (0, GRID_W - 2 * n_dc)))
    tiles = _na_bias_tiles(rows2, expand, neg_mask).reshape(n_heads, n_tiles, GRID_W, LANES)
    sink = sw_sink[0]
    w_up_b16 = w_up[0].astype(BF16)
    (oa, lse_a), (w_up_a,) = _na_forward(proj, tiles, batch, seq, _Rider("gather", [w_up_b16[:d // 2]]))
    (ob, lse_b), (w_up_b, conv_w_g, w_out_g) = _sw_forward(
        proj, sink, batch, seq, _Rider("gather", [w_up_b16[d // 2:], conv_w[0], w_out[0].astype(BF16)]))
    w_up_f = (w_up_a, w_up_b)
    w_out_f = w_out_g.reshape(d, d)
    conv_w_f = jnp.transpose(conv_w_g, (1, 0, 2)).reshape(3, D_FF)
    oab, mix, x1, h2 = _out_proj(oa, ob, g_na_out, g_sw_out, w_out_f, xt, mod3, g_ffn, seq)
    (u,), _ = _up_proj(h2, w_up_f)
    (a,), (w_down_g,) = _conv_gate(u, conv_w_f, conv_b, batch, seq, _Rider("gather", [w_down[0].astype(BF16)]))
    w_down_f = w_down_g.reshape(D_FF, d)
    dx2, dffn, loss_part, dgate_f, dg_final = _down_and_loss(a, w_down_f, x1, mod3, g_final.reshape(1, d), tgt, seq)

    gw_down, gw_down_b = _down_weight_grad(a, dffn)
    blocks = lambda g, rows: g.reshape(N_SHARD, rows // N_SHARD, d)
    (du, gconv_w, gconv_b), (recv_down, own_down) = _ffn_backward(
        dffn, w_down_f, u, conv_w_f, conv_b, batch, seq,
        _Rider("scatter", [blocks(gw_down_b, D_FF)], [blocks(gw_down, D_FF)]))
    (gw_up_top, gw_up_bot, gw_up_top_b, gw_up_bot_b), _ = _up_weight_grad(h2, du)
    (dx1, dmix, dshift_f, dscale_f, dgate_a, dg_ffn), _ = _up_backward(du, w_up_f, x1, mod3, g_ffn, dx2, mix, seq)
    doa, dob, gw_out, gw_out_b, dg_na, dg_sw = _out_backward(dmix, w_out_f, oab, oa, ob, g_na_out, g_sw_out)
    (dqa, dka, dva, dtiles), (recv_up_bot, own_up_bot) = _na_backward(
        proj, doa, lse_a, tiles, batch, seq, _Rider("scatter", [gw_up_bot_b], [gw_up_bot]))
    (dq_b, dk_b, dv_b, dsink_parts), (recv_out, recv_up_top, own_out, own_up_top) = _sw_backward(
        proj, dob, lse_b, sink, batch, seq,
        _Rider("scatter", [blocks(gw_out_b, d), gw_up_top_b], [blocks(gw_out, d), gw_up_top]))
    gx, gw_in_t, gw_in_b, dshift_a, dscale_a, dg_attn = _in_backward(
        (dqa, dka, dva), dq_b, dk_b, dv_b, w_in_t, h1, xt, mod3, g_attn, dx1, cos_t, sin_t, seq)

    red = _na_bias_grad(dtiles.reshape(n_heads * n_tiles, GRID_W, LANES), expand)[:, :2 * n_dc]
    red = red.reshape(n_heads, n_tiles, 2, n_dc)
    zero_row = jnp.zeros((n_heads, 1, n_dc), F32)
    g_rpb = (jnp.concatenate([red[:, :, 0, :], zero_row], axis=1)
             + jnp.concatenate([zero_row, red[:, :, 1, :]], axis=1))
    g_sink = jnp.sum(dsink_parts[:, :, :2, 0], axis=0).reshape(SW_WIDTH // HEAD_DIM)

    dmod = jnp.concatenate([dshift_a, dscale_a, dgate_a, dshift_f, dscale_f, dgate_f], axis=2).reshape(batch, 6 * d)
    rpb_shape = na_rpb.shape[1:]
    states = [(g_attn, m_g_attn, v_g_attn),
              (na_rpb.reshape(rpb_shape), m_na_rpb.reshape(rpb_shape), v_na_rpb.reshape(rpb_shape)),
              (sw_sink, m_sw_sink, v_sw_sink), (g_na_out, m_g_na_out, v_g_na_out), (g_sw_out, m_g_sw_out, v_g_sw_out),
              (g_ffn, m_g_ffn, v_g_ffn), (conv_b, m_conv_b, v_conv_b),
              (g_final.reshape(1, d), m_g_final.reshape(1, d), v_g_final.reshape(1, d))]
    partials = [dg_attn, g_rpb, g_sink.reshape(sw_sink.shape), dg_na, dg_sw, dg_ffn, gconv_b, dg_final,
                gconv_w, loss_part]
    mine = [None, _sum_slots([(recv_out, own_out)], "sum_w_out"),
            _sum_slots([(recv_up_top, own_up_top), (recv_up_bot, own_up_bot)], "sum_w_up"),
            _sum_slots([(recv_down, own_down)], "sum_w_down")]
    small, (recv_in, own_in, *theirs) = _small_sums(
        partials, dmod, _Riders([_Rider("scatter", [blocks(gw_in_b, IN_WIDTH)], [blocks(gw_in_t, IN_WIDTH)]),
                                 _Rider("swap", mine[1:])]))
    g_conv_w_full, loss_sum, g_b_ada, dmod_all = small[len(states):]
    r_small = _small_adamw(states + [(b_ada, m_b_ada, v_b_ada)], small[:len(states)] + [g_b_ada])
    loss = loss_sum[0, 0]
    mine[0] = _sum_slots([(recv_in, own_in)], "sum_w_in")
    theirs = _ride_alone(_Rider("swap", mine[:1]), "swap_sibling") + theirs
    dmod_rows = jnp.pad(dmod_all, ((0, 0), (0, SUBLANES - batch), (0, 0))).reshape(N_DEV * SUBLANES, 6 * d)
    ncol = w_ada.shape[2]
    g_w_ada = _ada_weight_grad(sc_all, lax.dynamic_slice(dmod_rows, (0, shard * ncol), (N_DEV * SUBLANES, ncol)))
    cshard = conv_w.shape[2]
    g_conv_w = lax.dynamic_slice(g_conv_w_full, (0, shard * cshard), (3, cshard))

    def big(w, m, v, g_parts, name):
        shape = w.shape
        outs = _adamw(w[0], g_parts, m[0], v[0], name)
        return [o.reshape(shape) for o in outs]

    r_w_ada = big(w_ada, m_w_ada, v_w_ada, [g_w_ada], "adamw_w_ada")
    r_w_in = [jnp.transpose(o).reshape(w_in.shape) for o in
              _adamw(jnp.transpose(w_in[0]), [mine[0], theirs[0]], jnp.transpose(m_w_in[0]), jnp.transpose(v_w_in[0]),
                     "adamw_w_in")]
    r_w_out = big(w_out, m_w_out, v_w_out, [mine[1], theirs[1]], "adamw_w_out")
    r_w_up = big(w_up, m_w_up, v_w_up, [mine[2], theirs[2]], "adamw_w_up")
    r_w_down = big(w_down, m_w_down, v_w_down, [mine[3], theirs[3]], "adamw_w_down")

    r_conv_w = big(conv_w, m_conv_w, v_conv_w, [g_conv_w], "adamw_conv_w")

    def pick(k):
        ga_, rpb_, sk_, gna_, gsw_, gf_, cb_, gfin_, b_ = [r[k] for r in r_small]
        return [r_w_ada[k], b_, ga_, r_w_in[k], rpb_.reshape(na_rpb.shape), sk_, gna_, gsw_, r_w_out[k], gf_,
                r_w_up[k], r_conv_w[k], cb_, r_w_down[k], gfin_.reshape(d)]

    return (loss, gx.reshape(batch, seq, d), *pick(0), *pick(1), *pick(2), *pick(3))
```

```python
import jax
import jax.numpy as jnp
import numpy as np
from jax import lax
from jax.experimental import pallas as pl
from jax.experimental.pallas import tpu as pltpu

F32 = jnp.float32
BF16 = jnp.bfloat16
MESH = pl.DeviceIdType.MESH

D_MODEL = 1024
HEAD_DIM = 64
NA_WIDTH = 512
SW_WIDTH = 512
SW_KV_WIDTH = 128
IN_WIDTH = 2304
D_FF = 2816
GRID_W = 64
NA_ROWS = 8
NA_COLS = 16
SW_BLOCK = 128
ROPE_THETA = 10000.0
EPS = 1e-6
NEG = -1e30
QK_SCALE = HEAD_DIM ** -0.5

ADAM_LR = 0.001
ADAM_B1 = 0.9
ADAM_B2 = 0.999
ADAM_EPS = 1e-08
ADAM_WD = 0.01
ADAM_STEP = 10

N_SHARD = 4
N_DEV = 8
LANES = 128
SUBLANES = 8
TOKEN_TILE = 512
FF_TILE = 256
CONV_CHUNK = 512
NA_GROUP = 8
SW_GROUP_BLOCKS = 8
VMEM_BIG = 56 * 1024 * 1024


def _mm(a, b):
    return jnp.dot(a, b, preferred_element_type=F32)


def _mm_nt(a, b):
    return lax.dot_general(a, b, (((1,), (1,)), ((), ())), preferred_element_type=F32)


def _mm_tn(a, b):
    return lax.dot_general(a, b, (((0,), (0,)), ((), ())), preferred_element_type=F32)


def _cparams(sem=None, vmem=None):
    kw = {}
    if sem is not None:
        kw["dimension_semantics"] = sem
    if vmem is not None:
        kw["vmem_limit_bytes"] = vmem
    return pltpu.CompilerParams(**kw)


def _resident(shape):
    return pl.BlockSpec(shape, lambda i: (0,) * len(shape), pipeline_mode=pl.Buffered(1))


def _sigmoid(x):
    return 1.0 / (1.0 + jnp.exp(-x))


def _rms_stats(x):
    r = lax.rsqrt(jnp.mean(x * x, axis=-1, keepdims=True) + EPS)
    return r, x * r


def _rms_bwd(dxn, xn, r):
    return r * (dxn - xn * jnp.mean(dxn * xn, axis=-1, keepdims=True))


def _my_pos():
    return lax.axis_index("x"), lax.axis_index("y"), lax.axis_index("c")


def _flip(v, bit):
    return 1 - v if bit else v


def _ada_forward(c8, w_ada, b_ada, rider):
    d = c8.shape[1]
    ncol = w_ada.shape[1]

    def body(c_ref, w_ref, b_ref, mod_ref, sc_ref, m_scr, mod_buf, ssem, rsem, ssem2, rsem2):
        x, y, c = _my_pos()
        me = 4 * x + 2 * y + c
        shard = 2 * x + y
        cv = c_ref[...]
        my_rows = pl.ds(pl.multiple_of(me * SUBLANES, SUBLANES), SUBLANES)
        sc_ref[my_rows, :] = cv * _sigmoid(cv)

        def copy1(k):
            peer = (_flip(x, (k >> 2) & 1), _flip(y, (k >> 1) & 1), _flip(c, k & 1))
            return pltpu.make_async_remote_copy(
                src_ref=sc_ref.at[my_rows, :], dst_ref=sc_ref.at[my_rows, :],
                send_sem=ssem.at[k - 1], recv_sem=rsem.at[k - 1], device_id=peer, device_id_type=MESH)

        sends = [copy1(k) for k in range(1, N_DEV)]
        for cp in sends:
            cp.start()
        for cp in sends:
            cp.wait_recv()
        m_scr[...] = _mm(sc_ref[...].astype(BF16), w_ref[...].astype(BF16))

        def copy2(k):
            px, py = _flip(x, (k >> 1) & 1), _flip(y, k & 1)
            rows = pl.ds(pl.multiple_of((4 * px + 2 * py + c) * SUBLANES, SUBLANES), SUBLANES)
            return pltpu.make_async_remote_copy(
                src_ref=m_scr.at[rows, :], dst_ref=mod_buf.at[shard],
                send_sem=ssem2.at[k - 1], recv_sem=rsem2.at[k - 1], device_id=(px, py, c), device_id_type=MESH)

        sends2 = [copy2(k) for k in range(1, N_SHARD)]
        for cp in sends2:
            cp.start()
        mod_buf[shard] = m_scr[my_rows, :]
        for cp in sends2:
            cp.wait_recv()
        for s in range(N_SHARD):
            mod_ref[:, s * ncol:(s + 1) * ncol] = mod_buf[s] + b_ref[:, s * ncol:(s + 1) * ncol]
        for cp in sends + sends2:
            cp.wait_send()

    vm = pl.BlockSpec(memory_space=pltpu.VMEM)
    return _hosted(
        body, rider, name="ada_forward", grid=(),
        out_shape=(jax.ShapeDtypeStruct((SUBLANES, N_SHARD * ncol), F32),
                   jax.ShapeDtypeStruct((N_DEV * SUBLANES, d), F32)),
        in_specs=[vm, vm, vm], out_specs=(vm, vm),
        scratch_shapes=[pltpu.VMEM((N_DEV * SUBLANES, ncol), F32), pltpu.VMEM((N_SHARD, SUBLANES, ncol), F32),
                        pltpu.SemaphoreType.DMA((N_DEV - 1,)), pltpu.SemaphoreType.DMA((N_DEV - 1,)),
                        pltpu.SemaphoreType.DMA((N_SHARD - 1,)), pltpu.SemaphoreType.DMA((N_SHARD - 1,))],
        compiler_params=_cparams(vmem=VMEM_BIG), args=[c8, w_ada, b_ada])


class _Rider:
    def __init__(self, kind, srcs, owns=()):
        self.kind, self.srcs, self.owns = kind, list(srcs), list(owns)
        n = len(self.srcs)
        sds = jax.ShapeDtypeStruct
        dma = pltpu.SemaphoreType.DMA
        if kind == "gather":
            self.out_shapes = [sds((N_SHARD,) + s.shape, s.dtype) for s in self.srcs]
            self.sems = [dma((n, N_SHARD - 1)), dma((n, N_SHARD - 1)), dma((n, N_SHARD - 1)), dma((n, N_SHARD - 1)),
                         dma((n,)), dma((n,))]
        elif kind == "scatter":
            self.out_shapes = ([sds((N_SHARD - 1,) + s.shape[1:], s.dtype) for s in self.srcs]
                               + [sds(o.shape[1:], o.dtype) for o in self.owns])
            m = max(len(self.owns), 1)
            self.sems = [dma((n, N_SHARD - 1)), dma((n, N_SHARD - 1)), dma((m,)), dma((m,))]
        else:
            self.out_shapes = [sds(s.shape, s.dtype) for s in self.srcs]
            self.sems = [dma((n,)), dma((n,))]

    @property
    def inputs(self):
        return self.srcs + self.owns

    def _halved(self, i):
        a = self.srcs[i]
        tile_rows = SUBLANES * (4 // jnp.dtype(a.dtype).itemsize)
        return self.kind == "gather" and a.shape[0] % (2 * tile_rows) == 0

    def copies(self, ins, outs, sems):
        n = len(self.srcs)
        x, y, c = _my_pos()
        shard = 2 * x + y
        remote, relay = [], []
        if self.kind == "swap":
            ssem, rsem = sems
            for i in range(n):
                remote.append(pltpu.make_async_remote_copy(
                    src_ref=ins[i], dst_ref=outs[i], send_sem=ssem.at[i], recv_sem=rsem.at[i],
                    device_id=(x, y, 1 - c), device_id_type=MESH))
            return remote, relay
        if self.kind == "gather":
            ssem, rsem, ssem2, rsem2, sib_s, sib_r = sems
        else:
            ssem, rsem, sib_s, sib_r = sems
        for i in range(n):
            if self.kind == "gather":
                remote.append(pltpu.make_async_remote_copy(
                    src_ref=ins[i], dst_ref=outs[i].at[shard], send_sem=sib_s.at[i], recv_sem=sib_r.at[i],
                    device_id=(x, y, 1 - c), device_id_type=MESH))
                half = ins[i].shape[0] // 2
                mine = pl.ds(pl.multiple_of(c * half, half), half) if self._halved(i) else None
            for k in range(1, N_SHARD):
                px, py = _flip(x, (k >> 1) & 1), _flip(y, k & 1)
                if self.kind == "gather":
                    src, dst = ins[i], outs[i].at[shard]
                    if mine is not None:
                        src, dst = src.at[mine], dst.at[mine]
                        got = outs[i].at[2 * px + py].at[mine]
                        relay.append(pltpu.make_async_remote_copy(
                            src_ref=got, dst_ref=got, send_sem=ssem2.at[i, k - 1], recv_sem=rsem2.at[i, k - 1],
                            device_id=(x, y, 1 - c), device_id_type=MESH))
                else:
                    src, dst = ins[i].at[2 * px + py], outs[i].at[k - 1]
                remote.append(pltpu.make_async_remote_copy(
                    src_ref=src, dst_ref=dst, send_sem=ssem.at[i, k - 1], recv_sem=rsem.at[i, k - 1],
                    device_id=(px, py, c), device_id_type=MESH))
        if self.kind == "scatter":
            for i in range(len(self.owns)):
                remote.append(pltpu.make_async_remote_copy(
                    src_ref=ins[n + i].at[shard], dst_ref=outs[n + i], send_sem=sib_s.at[i], recv_sem=sib_r.at[i],
                    device_id=(x, y, 1 - c), device_id_type=MESH))
        return remote, relay

    def start(self, ins, outs, sems):
        remote, _ = self.copies(ins, outs, sems)
        for cp in remote:
            cp.start()

    def wait(self, ins, outs, sems):
        remote, relay = self.copies(ins, outs, sems)
        for cp in remote:
            cp.wait_recv()
        for cp in relay:
            cp.start()
        for cp in relay:
            cp.wait_recv()
        for cp in remote + relay:
            cp.wait_send()


class _Riders:
    def __init__(self, riders):
        self.riders = list(riders)
        self.inputs = [a for r in self.riders for a in r.inputs]
        self.out_shapes = [s for r in self.riders for s in r.out_shapes]
        self.sems = [s for r in self.riders for s in r.sems]

    def _split(self, ins, outs, sems):
        for r in self.riders:
            ni, no, ns = len(r.inputs), len(r.out_shapes), len(r.sems)
            yield r, ins[:ni], outs[:no], sems[:ns]
            ins, outs, sems = ins[ni:], outs[no:], sems[ns:]

    def start(self, ins, outs, sems):
        for r, i, o, s in self._split(ins, outs, sems):
            r.start(i, o, s)

    def wait(self, ins, outs, sems):
        for r, i, o, s in self._split(ins, outs, sems):
            r.wait(i, o, s)


def _hosted(body, rider, *, name, grid, out_shape, in_specs, out_specs, scratch_shapes, compiler_params, args):
    out_shape, out_specs = list(out_shape), list(out_specs)
    if rider is None:
        outs = pl.pallas_call(body, name=name, grid=grid, out_shape=tuple(out_shape), in_specs=list(in_specs),
                              out_specs=tuple(out_specs), scratch_shapes=list(scratch_shapes),
                              compiler_params=compiler_params)(*args)
        return list(outs), []
    n_in, n_out, n_scr = len(in_specs), len(out_shape), len(scratch_shapes)
    nr_in, nr_out = len(rider.inputs), len(rider.out_shapes)
    n_steps = 1
    for size in grid:
        n_steps *= size

    def full(*refs):
        ins, refs = refs[:n_in], refs[n_in:]
        r_in, refs = refs[:nr_in], refs[nr_in:]
        outs, refs = refs[:n_out], refs[n_out:]
        r_out, refs = refs[:nr_out], refs[nr_out:]
        scr, sems = refs[:n_scr], refs[n_scr:]
        if grid:
            step = 0
            for ax, size in enumerate(grid):
                step = step * size + pl.program_id(ax)
            pl.when(step == 0)(lambda: rider.start(r_in, r_out, sems))
            body(*ins, *outs, *scr)
            pl.when(step == n_steps - 1)(lambda: rider.wait(r_in, r_out, sems))
        else:
            rider.start(r_in, r_out, sems)
            body(*ins, *outs, *scr)
            rider.wait(r_in, r_out, sems)

    hbm = pl.BlockSpec(memory_space=pl.ANY)
    res = pl.pallas_call(
        full, name=name, grid=grid, out_shape=tuple(out_shape + rider.out_shapes),
        in_specs=list(in_specs) + [hbm] * nr_in, out_specs=tuple(out_specs + [hbm] * nr_out),
        scratch_shapes=list(scratch_shapes) + rider.sems, compiler_params=compiler_params,
    )(*args, *rider.inputs)
    return list(res[:n_out]), list(res[n_out:])


def _ride_alone(rider, name):
    return _hosted(lambda: None, rider, name=name, grid=(), out_shape=[], in_specs=[], out_specs=[], scratch_shapes=[],
                   compiler_params=_cparams(), args=[])[1]


def _rope_rot(t):
    w = t.shape[1]
    lane = lax.broadcasted_iota(jnp.int32, t.shape, 1)
    first = (lane % HEAD_DIM) < (HEAD_DIM // 2)
    return jnp.where(first, pltpu.roll(t, w - HEAD_DIM // 2, 1), pltpu.roll(t, HEAD_DIM // 2, 1))


def _in_proj(x, mod3, g_attn, w_in_t, cos_t, sin_t, seq, rider=None):
    t, d = x.shape
    tm = 2 * TOKEN_TILE
    per_seq = seq // tm
    rope_lo, rope_hi = 3 * NA_WIDTH, 3 * NA_WIDTH + SW_WIDTH + SW_KV_WIDTH
    n_rep = (rope_hi - rope_lo) // LANES

    def body(x_ref, mod_ref, g_ref, w_ref, cos_ref, sin_ref, h_ref, p_ref):
        r, xn = _rms_stats(x_ref[...])
        shift, scale = mod_ref[0, :, 0:d], mod_ref[0, :, d:2 * d]
        hb = ((xn * g_ref[...]) * (1.0 + scale) + shift).astype(BF16)
        h_ref[...] = hb
        p_ref[:, :rope_lo] = _mm_nt(hb, w_ref[:rope_lo, :]).astype(BF16)
        pr = _mm_nt(hb, w_ref[rope_lo:rope_hi, :])
        cos = jnp.concatenate([cos_ref[...]] * n_rep, axis=1)
        sin = jnp.concatenate([sin_ref[...]] * n_rep, axis=1)
        p_ref[:, rope_lo:rope_hi] = (pr * cos + _rope_rot(pr) * sin).astype(BF16)
        p_ref[:, rope_hi:] = _mm_nt(hb, w_ref[rope_hi:, :]).astype(BF16)

    return _hosted(
        body, rider, name="in_proj", grid=(t // tm,),
        out_shape=[jax.ShapeDtypeStruct((t, d), BF16), jax.ShapeDtypeStruct((t, IN_WIDTH), BF16)],
        in_specs=[pl.BlockSpec((tm, d), lambda i: (i, 0)),
                  pl.BlockSpec((1, 1, 6 * d), lambda i: (i // per_seq, 0, 0)),
                  pl.BlockSpec((1, d), lambda i: (0, 0)),
                  pl.BlockSpec((IN_WIDTH, d), lambda i: (0, 0)),
                  pl.BlockSpec((tm, LANES), lambda i: (i % per_seq, 0)),
                  pl.BlockSpec((tm, LANES), lambda i: (i % per_seq, 0))],
        out_specs=[pl.BlockSpec((tm, d), lambda i: (i, 0)), pl.BlockSpec((tm, IN_WIDTH), lambda i: (i, 0))],
        scratch_shapes=[], compiler_params=_cparams(("arbitrary",), VMEM_BIG),
        args=[x, mod3, g_attn, w_in_t, cos_t, sin_t])


def _na_bias_pattern():
    n_dc = 2 * NA_COLS - 1
    j = np.arange(GRID_W)[:, None]
    m = np.arange(GRID_W * LANES)[None, :]
    q, lane = m // LANES, m % LANES
    k = lane % GRID_W
    cs = np.clip(q - NA_COLS // 2, 0, GRID_W - NA_COLS)
    ok = (k >= cs) & (k < cs + NA_COLS)
    hit = ok & (j < 2 * n_dc) & (lane // GRID_W == j // n_dc) & (k - q + (NA_COLS - 1) == j % n_dc)
    return jnp.asarray(hit.astype(np.float32)), jnp.asarray(np.where(ok, 0.0, NEG).astype(np.float32))


def _na_bias_tiles(rows2, expand, mask):
    n, width = rows2.shape[0], expand.shape[1]
    q_step = 16
    step = q_step * LANES

    def body(r_ref, e_ref, m_ref, o_ref):
        flat = jnp.dot(r_ref[...], e_ref[...], precision=lax.Precision.HIGHEST,
                       preferred_element_type=F32) + m_ref[...]
        for qq in range(q_step):
            o_ref[:, qq, :] = flat[:, qq * LANES:(qq + 1) * LANES]

    return pl.pallas_call(
        body, name="na_bias_tiles", grid=(width // step,),
        out_shape=jax.ShapeDtypeStruct((n, GRID_W, LANES), F32),
        in_specs=[pl.BlockSpec(rows2.shape, lambda i: (0, 0)), pl.BlockSpec((expand.shape[0], step), lambda i: (0, i)),
                  pl.BlockSpec((1, step), lambda i: (0, i))],
        out_specs=pl.BlockSpec((n, q_step, LANES), lambda i: (0, i, 0)),
        compiler_params=_cparams(("arbitrary",)),
    )(rows2, expand, mask)


def _na_prepare(k_ref, v_ref, km, vm):
    lane = lax.broadcasted_iota(jnp.int32, k_ref.shape, 1)
    low = lane < HEAD_DIM
    kv = k_ref[...]
    vv = v_ref[...]
    zero = jnp.zeros_like(kv)
    km[0] = jnp.where(low, kv, zero)
    km[1] = jnp.where(low, zero, kv)
    vm[0] = jnp.where(low, vv, zero)
    vm[1] = jnp.where(low, zero, vv)


def _na_window(r, n_rows):
    rs = jnp.clip(r - NA_ROWS // 2, 0, n_rows - NA_ROWS)
    return rs, r - rs


def _na_pair_window(ref, wrows):
    return jnp.concatenate([ref[0, wrows, :], ref[1, wrows, :]], axis=0)


def _na_scores(q, k2, tp_ref, off):
    bias = jnp.concatenate([tp_ref[h, 2 * w - off + (NA_ROWS - 1)] for h in range(2) for w in range(NA_ROWS // 2)],
                           axis=1)
    return _mm_nt(q, k2) * QK_SCALE + bias


def _pair_lse_block(lse):
    lane = lax.broadcasted_iota(jnp.int32, (lse[0].shape[0], LANES), 1)
    return jnp.where(lane < HEAD_DIM, lse[0], lse[1])


def _pair_softmax(s):
    win = s.shape[1] // 2
    halves, lse = [], []
    for h in range(2):
        sh = s[:, h * win:(h + 1) * win]
        m = jnp.max(sh, axis=-1, keepdims=True)
        e = jnp.exp(sh - m)
        l = jnp.sum(e, axis=-1, keepdims=True)
        halves.append(e / l)
        lse.append(m + jnp.log(l))
    return jnp.concatenate(halves, axis=1), _pair_lse_block(lse)


def _pair_grad(w2, x, low):
    keys = w2.shape[1] // 2
    zero = jnp.zeros_like(x)
    low_x = low[:x.shape[0]]
    stacked = jnp.concatenate([w2[:, :keys], w2[:, keys:]], axis=0)
    diag = jnp.concatenate([jnp.where(low_x, x, zero), jnp.where(low_x, zero, x)], axis=0)
    return _mm_tn(stacked, diag)


def _pair_probs_from_lse(s, lse_block):
    win = s.shape[1] // 2
    return jnp.concatenate([jnp.exp(s[:, h * win:(h + 1) * win] - lse_block[:, h * HEAD_DIM:h * HEAD_DIM + 1])
                            for h in range(2)], axis=1)


def _na_forward(proj, tiles, batch, seq, rider=None):
    t = proj.shape[0]
    n_rows = seq // GRID_W
    n_pairs = NA_WIDTH // LANES
    win = NA_ROWS * GRID_W

    def body(q_ref, k_ref, v_ref, tp_ref, o_ref, lse_ref, km, vm):
        _na_prepare(k_ref, v_ref, km, vm)

        def scores(r):
            rs, off = _na_window(r, n_rows)
            rows = pl.ds(pl.multiple_of(r * GRID_W, GRID_W), GRID_W)
            wrows = pl.ds(pl.multiple_of(rs * GRID_W, GRID_W), win)
            return rows, wrows, _na_scores(q_ref[rows, :], _na_pair_window(km, wrows), tp_ref, off)

        def finish(rows, wrows, s):
            p, lse = _pair_softmax(s)
            lse_ref[rows, :] = lse
            o_ref[rows, :] = _mm(p.astype(BF16), _na_pair_window(vm, wrows))

        def row_group(i, carry):
            for state in [scores(NA_GROUP * i + j) for j in range(NA_GROUP)]:
                finish(*state)
            return carry

        lax.fori_loop(0, n_rows // NA_GROUP, row_group, 0)

    return _hosted(
        body, rider, name="na_forward", grid=(batch, n_pairs),
        out_shape=[jax.ShapeDtypeStruct((t, NA_WIDTH), F32), jax.ShapeDtypeStruct((t, NA_WIDTH), F32)],
        in_specs=[pl.BlockSpec((seq, LANES), lambda b, p: (b, p)),
                  pl.BlockSpec((seq, LANES), lambda b, p: (b, n_pairs + p)),
                  pl.BlockSpec((seq, LANES), lambda b, p: (b, 2 * n_pairs + p)),
                  pl.BlockSpec((2, 2 * NA_ROWS - 2, GRID_W, LANES), lambda b, p: (p, 0, 0, 0))],
        out_specs=[pl.BlockSpec((seq, LANES), lambda b, p: (b, p)), pl.BlockSpec((seq, LANES), lambda b, p: (b, p))],
        scratch_shapes=[pltpu.VMEM((2, seq, LANES), BF16), pltpu.VMEM((2, seq, LANES), BF16)],
        compiler_params=_cparams(("arbitrary", "arbitrary")), args=[proj, proj, proj, tiles])


def _sw_prepare(kv_ref, g, dst_lo, dst_hi, seq):
    lane = lax.broadcasted_iota(jnp.int32, kv_ref.shape, 1)
    mine = (lane // HEAD_DIM) == g
    kg = jnp.where(mine, kv_ref[...].astype(F32), 0.0)
    kr = pltpu.roll(kg, HEAD_DIM, 1)
    first = g == 0
    zero = jnp.zeros((SW_BLOCK, LANES), BF16)
    for dst, val in ((dst_lo, jnp.where(first, kg, kr)), (dst_hi, jnp.where(first, kr, kg))):
        dst[0:SW_BLOCK, :] = zero
        dst[SW_BLOCK:SW_BLOCK + seq, :] = val.astype(BF16)
        dst[SW_BLOCK + seq:, :] = zero


def _sw_mask(n, seq):
    qi = lax.broadcasted_iota(jnp.int32, (SW_BLOCK, 3 * SW_BLOCK), 0)
    kj = lax.broadcasted_iota(jnp.int32, (SW_BLOCK, 3 * SW_BLOCK), 1)
    kpos = n * SW_BLOCK - SW_BLOCK + kj
    return (jnp.abs(qi + SW_BLOCK - kj) <= SW_BLOCK) & (kpos >= 0) & (kpos < seq)


def _sw_probs(s2, ok, sinks):
    band = s2.shape[1] // 2
    halves, lse = [], []
    for i in range(2):
        s = jnp.where(ok, s2[:, i * band:(i + 1) * band], NEG)
        m = jnp.maximum(jnp.max(s, axis=-1, keepdims=True), sinks[i])
        p = jnp.exp(s - m)
        den = jnp.sum(p, axis=-1, keepdims=True) + jnp.exp(sinks[i] - m)
        halves.append(p / den)
        lse.append(m + jnp.log(den))
    return jnp.concatenate(halves, axis=1), _pair_lse_block(lse)


def _sw_probs_from_lse(s2, ok, sinks, lse_block):
    band = s2.shape[1] // 2
    halves, sink_p = [], []
    for i in range(2):
        lse = lse_block[:, i * HEAD_DIM:i * HEAD_DIM + 1]
        halves.append(jnp.exp(jnp.where(ok, s2[:, i * band:(i + 1) * band], NEG) - lse))
        sink_p.append(jnp.exp(sinks[i] - lse))
    return jnp.concatenate(halves, axis=1), sink_p


def _sw_forward(proj, sink, batch, seq, rider=None):
    t = proj.shape[0]
    n_pairs = SW_WIDTH // LANES
    q_blk = 3 * NA_WIDTH // LANES
    k_blk = q_blk + n_pairs
    n_blocks = seq // SW_BLOCK
    pad = seq + 2 * SW_BLOCK

    def body(sink_ref, q_ref, k_ref, v_ref, o_ref, lse_ref, k_lo, k_hi, v_lo, v_hi):
        hp = pl.program_id(1)
        g = hp // 2
        _sw_prepare(k_ref, g, k_lo, k_hi, seq)
        _sw_prepare(v_ref, g, v_lo, v_hi, seq)

        sinks = (sink_ref[2 * hp], sink_ref[2 * hp + 1])

        def scores(n):
            rows = pl.ds(pl.multiple_of(n * SW_BLOCK, SW_BLOCK), SW_BLOCK)
            wrows = pl.ds(pl.multiple_of(n * SW_BLOCK, SW_BLOCK), 3 * SW_BLOCK)
            k2 = jnp.concatenate([k_lo[wrows, :], k_hi[wrows, :]], axis=0)
            return n, rows, wrows, _mm_nt(q_ref[rows, :], k2) * QK_SCALE

        def finish(n, rows, wrows, s2):
            p, lse = _sw_probs(s2, _sw_mask(n, seq), sinks)
            lse_ref[rows, :] = lse
            v2 = jnp.concatenate([v_lo[wrows, :], v_hi[wrows, :]], axis=0)
            o_ref[rows, :] = _mm(p.astype(BF16), v2)

        def block_group(i, carry):
            for state in [scores(SW_GROUP_BLOCKS * i + j) for j in range(SW_GROUP_BLOCKS)]:
                finish(*state)
            return carry

        lax.fori_loop(0, n_blocks // SW_GROUP_BLOCKS, block_group, 0)

    return _hosted(
        body, rider, name="sw_forward", grid=(batch, n_pairs),
        out_shape=[jax.ShapeDtypeStruct((t, SW_WIDTH), F32), jax.ShapeDtypeStruct((t, SW_WIDTH), F32)],
        in_specs=[pl.BlockSpec(memory_space=pltpu.SMEM),
                  pl.BlockSpec((seq, LANES), lambda b, p: (b, q_blk + p)),
                  pl.BlockSpec((seq, LANES), lambda b, p: (b, k_blk)),
                  pl.BlockSpec((seq, LANES), lambda b, p: (b, k_blk + 1))],
        out_specs=[pl.BlockSpec((seq, LANES), lambda b, p: (b, p)), pl.BlockSpec((seq, LANES), lambda b, p: (b, p))],
        scratch_shapes=[pltpu.VMEM((pad, LANES), BF16)] * 4,
        compiler_params=_cparams(("arbitrary", "arbitrary")), args=[sink, proj, proj, proj])


def _out_proj(oa, ob, g_na, g_sw, w_out, x, mod3, g_ffn, seq):
    t, d = x.shape
    tm = TOKEN_TILE
    per_seq = seq // tm

    def body(oa_ref, ob_ref, gna_ref, gsw_ref, w_ref, x_ref, mod_ref, gf_ref, oab_ref, mix_ref, x1_ref, h2_ref):
        _, na = _rms_stats(oa_ref[...])
        _, nb = _rms_stats(ob_ref[...])
        oab = jnp.concatenate([na * gna_ref[...], nb * gsw_ref[...]], axis=1).astype(BF16)
        oab_ref[...] = oab
        mix = _mm(oab, w_ref[...])
        mix_ref[...] = mix
        gate_a = mod_ref[0, :, 2 * d:3 * d]
        shift_f, scale_f = mod_ref[0, :, 3 * d:4 * d], mod_ref[0, :, 4 * d:5 * d]
        x1 = x_ref[...] + gate_a * mix
        x1_ref[...] = x1
        _, xn = _rms_stats(x1)
        h2_ref[...] = ((xn * gf_ref[...]) * (1.0 + scale_f) + shift_f).astype(BF16)

    tile = lambda w: pl.BlockSpec((tm, w), lambda i: (i, 0))
    vec = lambda w: pl.BlockSpec((1, w), lambda i: (0, 0))
    return pl.pallas_call(
        body, name="out_proj", grid=(t // tm,),
        out_shape=(jax.ShapeDtypeStruct((t, d), BF16), jax.ShapeDtypeStruct((t, d), F32),
                   jax.ShapeDtypeStruct((t, d), F32), jax.ShapeDtypeStruct((t, d), BF16)),
        in_specs=[tile(NA_WIDTH), tile(SW_WIDTH), vec(NA_WIDTH), vec(SW_WIDTH),
                  pl.BlockSpec((d, d), lambda i: (0, 0)), tile(d),
                  pl.BlockSpec((1, 1, 6 * d), lambda i: (i // per_seq, 0, 0)), vec(d)],
        out_specs=(tile(d), tile(d), tile(d), tile(d)),
        compiler_params=_cparams(("arbitrary",), VMEM_BIG),
    )(oa, ob, g_na, g_sw, w_out, x, mod3, g_ffn)


def _up_proj(h2, w_up_halves, rider=None):
    t, d = h2.shape
    tm = 2 * TOKEN_TILE
    w_a, w_b = w_up_halves
    half, wcol = w_a.shape[1], w_a.shape[2]

    def body(h_ref, wa_ref, wb_ref, u_ref):
        u_ref[0] = (_mm(h_ref[:, :half], wa_ref[0]) + _mm(h_ref[:, half:], wb_ref[0])).astype(BF16)

    w_spec = pl.BlockSpec((1, half, wcol), lambda j, i: (j, 0, 0))
    return _hosted(
        body, rider, name="up_proj", grid=(N_SHARD, t // tm),
        out_shape=[jax.ShapeDtypeStruct((2, t, D_FF), BF16)],
        in_specs=[pl.BlockSpec((tm, d), lambda j, i: (i, 0)), w_spec, w_spec],
        out_specs=[pl.BlockSpec((1, tm, wcol), lambda j, i: (j // 2, i, j % 2))],
        scratch_shapes=[], compiler_params=_cparams(("arbitrary", "arbitrary"), VMEM_BIG), args=[h2, w_a, w_b])


def _taps_chunk(load, s, rows, seq):
    halo = 2 * SUBLANES
    cur = load(s, rows)
    above = load(pl.multiple_of(jnp.maximum(s - halo, 0), halo), halo)
    below = load(pl.multiple_of(jnp.minimum(s + rows, seq - halo), halo), halo)
    up = jnp.where(s > 0, above[halo - 1:halo, :], 0.0)
    dn = jnp.where(s + rows < seq, below[0:1, :], 0.0)
    row = lax.broadcasted_iota(jnp.int32, cur.shape, 0)
    prev = jnp.where(row == 0, up, pltpu.roll(cur, 1, 0))
    nxt = jnp.where(row == rows - 1, dn, pltpu.roll(cur, rows - 1, 0))
    return cur, prev, nxt


def _conv_gate(u, conv_w, conv_b, batch, seq, rider=None):
    t = u.shape[1]
    cw = FF_TILE
    rows = CONV_CHUNK

    def body(u_ref, w_ref, b_ref, a_ref):
        def chunk(i, carry):
            s = pl.multiple_of(i * rows, rows)
            gt, prev, nxt = _taps_chunk(lambda at, n: u_ref[1, pl.ds(at, n), :].astype(F32), s, rows, seq)
            gc = prev * w_ref[0:1, :] + gt * w_ref[1:2, :] + nxt * w_ref[2:3, :] + b_ref[...]
            a_ref[pl.ds(s, rows), :] = ((gc * _sigmoid(gc)) * u_ref[0, pl.ds(s, rows), :].astype(F32)).astype(BF16)
            return carry

        lax.fori_loop(0, seq // rows, chunk, 0)

    return _hosted(
        body, rider, name="conv_gate", grid=(batch, D_FF // cw),
        out_shape=[jax.ShapeDtypeStruct((t, D_FF), BF16)],
        in_specs=[pl.BlockSpec((2, seq, cw), lambda b, j: (0, b, j)),
                  pl.BlockSpec((3, cw), lambda b, j: (0, j)), pl.BlockSpec((1, cw), lambda b, j: (0, j))],
        out_specs=[pl.BlockSpec((seq, cw), lambda b, j: (b, j))], scratch_shapes=[],
        compiler_params=_cparams(("arbitrary", "arbitrary"), VMEM_BIG), args=[u, conv_w, conv_b])


def _down_and_loss(a, w_down, x1, mod3, g_final, target, seq):
    t, d = x1.shape
    tm = TOKEN_TILE
    per_seq = seq // tm
    batch = t // seq

    def body(a_ref, w_ref, x1_ref, mod_ref, g_ref, tgt_ref, dx2_ref, dffn_ref, loss_ref, dgate_ref, dg_ref):
        i = pl.program_id(0)
        f = _mm(a_ref[...], w_ref[...])
        gate_f = mod_ref[0, :, 5 * d:6 * d]
        x2 = x1_ref[...] + gate_f * f
        r, xn = _rms_stats(x2)
        err = xn * g_ref[...] - tgt_ref[...]
        part = 0.5 * jnp.sum(jnp.mean(err * err, axis=-1, keepdims=True))
        dy = err / d
        dx2 = _rms_bwd(dy * g_ref[...], xn, r)
        dx2_ref[...] = dx2
        dffn_ref[...] = (dx2 * gate_f).astype(BF16)

        @pl.when(i == 0)
        def _():
            loss_ref[...] = jnp.zeros_like(loss_ref)
            dg_ref[...] = jnp.zeros_like(dg_ref)

        @pl.when(i % per_seq == 0)
        def _():
            dgate_ref[...] = jnp.zeros_like(dgate_ref)

        loss_ref[...] += part
        dg_ref[...] += jnp.sum(dy * xn, axis=0, keepdims=True)
        dgate_ref[0] += jnp.sum(dx2 * f, axis=0, keepdims=True)

    tile = lambda w: pl.BlockSpec((tm, w), lambda i: (i, 0))
    return pl.pallas_call(
        body, name="down_loss", grid=(t // tm,),
        out_shape=(jax.ShapeDtypeStruct((t, d), F32), jax.ShapeDtypeStruct((t, d), BF16),
                   jax.ShapeDtypeStruct((SUBLANES, LANES), F32), jax.ShapeDtypeStruct((batch, 1, d), F32),
                   jax.ShapeDtypeStruct((1, d), F32)),
        in_specs=[tile(D_FF), _resident((D_FF, d)), tile(d),
                  pl.BlockSpec((1, 1, 6 * d), lambda i: (i // per_seq, 0, 0)),
                  pl.BlockSpec((1, d), lambda i: (0, 0)), tile(d)],
        out_specs=(tile(d), tile(d), pl.BlockSpec((SUBLANES, LANES), lambda i: (0, 0)),
                   pl.BlockSpec((1, 1, d), lambda i: (i // per_seq, 0, 0)), pl.BlockSpec((1, d), lambda i: (0, 0))),
        compiler_params=_cparams(("arbitrary",), VMEM_BIG),
    )(a, w_down, x1, mod3, g_final, target)


def _down_weight_grad(a, dffn):
    t, dff = a.shape
    d = dffn.shape[1]
    tk = 2 * TOKEN_TILE
    n_k = t // tk

    def body(a_ref, df_ref, g_ref, gb_ref):
        k = pl.program_id(0)

        @pl.when(k == 0)
        def _():
            g_ref[...] = jnp.zeros_like(g_ref)

        g_ref[...] += _mm_tn(a_ref[...], df_ref[...])

        @pl.when(k == n_k - 1)
        def _():
            gb_ref[...] = g_ref[...].astype(BF16)

    whole = _resident((dff, d))
    return pl.pallas_call(
        body, name="down_weight_grad", grid=(n_k,),
        out_shape=(jax.ShapeDtypeStruct((dff, d), F32), jax.ShapeDtypeStruct((dff, d), BF16)),
        in_specs=[pl.BlockSpec((tk, dff), lambda k: (k, 0)), pl.BlockSpec((tk, d), lambda k: (k, 0))],
        out_specs=(whole, whole),
        compiler_params=_cparams(("arbitrary",), VMEM_BIG),
    )(a, dffn)


def _ffn_backward(dffn, w_down, u, conv_w, conv_b, batch, seq, rider=None):
    t, d = dffn.shape
    cw = FF_TILE
    rows = CONV_CHUNK

    def body(df_ref, wd_ref, u_ref, w_ref, b_ref, du_ref, gcw_ref, gcb_ref, da_scr, dgc_scr):
        b = pl.program_id(1)
        da_scr[...] = _mm_nt(df_ref[...], wd_ref[...])

        @pl.when(b == 0)
        def _():
            gcw_ref[...] = jnp.zeros_like(gcw_ref)
            gcb_ref[...] = jnp.zeros_like(gcb_ref)

        def fold(v):
            return jnp.sum(v.reshape(rows // SUBLANES, SUBLANES, cw), axis=0)

        def chunk(i, carry):
            s = pl.multiple_of(i * rows, rows)
            here = pl.ds(s, rows)
            gt, prev, nxt = _taps_chunk(lambda at, n: u_ref[1, pl.ds(at, n), :].astype(F32), s, rows, seq)
            val, da = u_ref[0, here, :].astype(F32), da_scr[here, :]
            gc = prev * w_ref[0:1, :] + gt * w_ref[1:2, :] + nxt * w_ref[2:3, :] + b_ref[...]
            sg = _sigmoid(gc)
            sl = gc * sg
            du_ref[0, here, :] = (da * sl).astype(BF16)
            dgc = (da * val) * (sg * (1.0 + gc * (1.0 - sg)))
            dgc_scr[here, :] = dgc
            cb, c0, c1, c2 = carry
            return cb + fold(dgc), c0 + fold(dgc * prev), c1 + fold(dgc * gt), c2 + fold(dgc * nxt)

        zero = jnp.zeros((SUBLANES, cw), F32)
        cb, c0, c1, c2 = lax.fori_loop(0, seq // rows, chunk, (zero, zero, zero, zero))
        gcb_ref[...] += jnp.sum(cb, axis=0, keepdims=True)
        gcw_ref[0:1, :] += jnp.sum(c0, axis=0, keepdims=True)
        gcw_ref[1:2, :] += jnp.sum(c1, axis=0, keepdims=True)
        gcw_ref[2:3, :] += jnp.sum(c2, axis=0, keepdims=True)

        def chunk2(i, carry):
            s = pl.multiple_of(i * rows, rows)
            dgc, dprev, dnxt = _taps_chunk(lambda at, n: dgc_scr[pl.ds(at, n), :], s, rows, seq)
            du_ref[1, pl.ds(s, rows), :] = (dnxt * w_ref[0:1, :] + dgc * w_ref[1:2, :]
                                            + dprev * w_ref[2:3, :]).astype(BF16)
            return carry

        lax.fori_loop(0, seq // rows, chunk2, 0)

    return _hosted(
        body, rider, name="ffn_backward", grid=(D_FF // cw, batch),
        out_shape=[jax.ShapeDtypeStruct((2, t, D_FF), BF16),
                   jax.ShapeDtypeStruct((3, D_FF), F32), jax.ShapeDtypeStruct((1, D_FF), F32)],
        in_specs=[pl.BlockSpec((seq, d), lambda j, b: (b, 0)), pl.BlockSpec((cw, d), lambda j, b: (j, 0)),
                  pl.BlockSpec((2, seq, cw), lambda j, b: (0, b, j)),
                  pl.BlockSpec((3, cw), lambda j, b: (0, j)), pl.BlockSpec((1, cw), lambda j, b: (0, j))],
        out_specs=[pl.BlockSpec((2, seq, cw), lambda j, b: (0, b, j)),
                   pl.BlockSpec((3, cw), lambda j, b: (0, j)), pl.BlockSpec((1, cw), lambda j, b: (0, j))],
        scratch_shapes=[pltpu.VMEM((seq, cw), F32), pltpu.VMEM((seq, cw), F32)],
        compiler_params=_cparams(("arbitrary", "arbitrary"), VMEM_BIG), args=[dffn, w_down, u, conv_w, conv_b])


def _up_backward(du, w_up, x1, mod3, g_ffn, dx2, mix, seq, rider=None):
    _, t, _ = du.shape
    d = x1.shape[1]
    tm = TOKEN_TILE
    per_seq = seq // tm
    batch = t // seq
    w_a, w_b = w_up
    half, wcol = w_a.shape[1], w_a.shape[2]

    def body(du_ref, wa_ref, wb_ref, x1_ref, mod_ref, g_ref, dx2_ref, mix_ref,
             dx1_ref, dmix_ref, dsh_ref, dsc_ref, dga_ref, dg_ref):
        i = pl.program_id(0)
        parts = []
        for w_ref in (wa_ref, wb_ref):
            acc = jnp.zeros((tm, half), F32)
            for j in range(N_SHARD):
                acc = acc + _mm_nt(du_ref[j // 2, :, (j % 2) * wcol:(j % 2 + 1) * wcol], w_ref[j])
            parts.append(acc)
        dh = jnp.concatenate(parts, axis=1)
        gate_a = mod_ref[0, :, 2 * d:3 * d]
        scale_f = mod_ref[0, :, 4 * d:5 * d]
        r, xn = _rms_stats(x1_ref[...])
        xg = xn * g_ref[...]
        dxg = dh * (1.0 + scale_f)
        dx1 = dx2_ref[...] + _rms_bwd(dxg * g_ref[...], xn, r)
        dx1_ref[...] = dx1
        dmix_ref[...] = (dx1 * gate_a).astype(BF16)

        @pl.when(i == 0)
        def _():
            dg_ref[...] = jnp.zeros_like(dg_ref)

        @pl.when(i % per_seq == 0)
        def _():
            dsh_ref[...] = jnp.zeros_like(dsh_ref)
            dsc_ref[...] = jnp.zeros_like(dsc_ref)
            dga_ref[...] = jnp.zeros_like(dga_ref)

        dg_ref[...] += jnp.sum(dxg * xn, axis=0, keepdims=True)
        dsh_ref[0] += jnp.sum(dh, axis=0, keepdims=True)
        dsc_ref[0] += jnp.sum(dh * xg, axis=0, keepdims=True)
        dga_ref[0] += jnp.sum(dx1 * mix_ref[...], axis=0, keepdims=True)

    tile = lambda w: pl.BlockSpec((tm, w), lambda i: (i, 0))
    per_b = pl.BlockSpec((1, 1, d), lambda i: (i // per_seq, 0, 0))
    small = jax.ShapeDtypeStruct((batch, 1, d), F32)
    return _hosted(
        body, rider, name="up_backward", grid=(t // tm,),
        out_shape=[jax.ShapeDtypeStruct((t, d), F32), jax.ShapeDtypeStruct((t, d), BF16), small, small, small,
                   jax.ShapeDtypeStruct((1, d), F32)],
        in_specs=[pl.BlockSpec((2, tm, D_FF), lambda i: (0, i, 0)),
                  _resident((N_SHARD, half, wcol)), _resident((N_SHARD, half, wcol)), tile(d),
                  pl.BlockSpec((1, 1, 6 * d), lambda i: (i // per_seq, 0, 0)),
                  pl.BlockSpec((1, d), lambda i: (0, 0)), tile(d), tile(d)],
        out_specs=[tile(d), tile(d), per_b, per_b, per_b, pl.BlockSpec((1, d), lambda i: (0, 0))],
        scratch_shapes=[], compiler_params=_cparams(("arbitrary",), VMEM_BIG),
        args=[du, w_a, w_b, x1, mod3, g_ffn, dx2, mix])


def _up_weight_grad(h2, du, rider=None):
    t, d = h2.shape
    tk = 2 * TOKEN_TILE
    wcol = D_FF // 2
    half = d // 2
    n_k = t // tk

    def body(h_ref, du_ref, ga_ref, gb_ref, ga16_ref, gb16_ref):
        k = pl.program_id(1)

        @pl.when(k == 0)
        def _():
            ga_ref[...] = jnp.zeros_like(ga_ref)
            gb_ref[...] = jnp.zeros_like(gb_ref)

        du = du_ref[0]
        ga_ref[0] += _mm_tn(h_ref[:, :half], du)
        gb_ref[0] += _mm_tn(h_ref[:, half:], du)

        @pl.when(k == n_k - 1)
        def _():
            ga16_ref[...] = ga_ref[...].astype(BF16)
            gb16_ref[...] = gb_ref[...].astype(BF16)

    g_spec = pl.BlockSpec((1, half, wcol), lambda j, k: (j, 0, 0))
    f32_out = jax.ShapeDtypeStruct((N_SHARD, half, wcol), F32)
    b16_out = jax.ShapeDtypeStruct((N_SHARD, half, wcol), BF16)
    return _hosted(
        body, rider, name="up_weight_grad", grid=(N_SHARD, n_k),
        out_shape=[f32_out, f32_out, b16_out, b16_out],
        in_specs=[pl.BlockSpec((tk, d), lambda j, k: (k, 0)),
                  pl.BlockSpec((1, tk, wcol), lambda j, k: (j // 2, k, j % 2))],
        out_specs=[g_spec, g_spec, g_spec, g_spec], scratch_shapes=[],
        compiler_params=_cparams(("arbitrary", "arbitrary"), VMEM_BIG), args=[h2, du])


def _out_backward(dmix, w_out, oab, oa, ob, g_na, g_sw):
    t, d = dmix.shape
    tm = 2 * TOKEN_TILE
    hw = NA_WIDTH

    def body(dm_ref, w_ref, oab_ref, oa_ref, ob_ref, gna_ref, gsw_ref,
             doa_ref, dob_ref, gw_ref, gwb_ref, dgna_ref, dgsw_ref):
        @pl.when(pl.program_id(0) == 0)
        def _():
            gw_ref[...] = jnp.zeros_like(gw_ref)
            dgna_ref[...] = jnp.zeros_like(dgna_ref)
            dgsw_ref[...] = jnp.zeros_like(dgsw_ref)

        dm = dm_ref[...]
        gw_ref[...] += _mm_tn(oab_ref[...], dm)

        @pl.when(pl.program_id(0) == t // tm - 1)
        def _():
            gwb_ref[...] = gw_ref[...].astype(BF16)

        do = _mm_nt(dm, w_ref[...])
        for raw_ref, g_ref, dst_ref, dg_ref, lo in ((oa_ref, gna_ref, doa_ref, dgna_ref, 0),
                                                     (ob_ref, gsw_ref, dob_ref, dgsw_ref, hw)):
            r, xn = _rms_stats(raw_ref[...])
            dpart = do[:, lo:lo + hw]
            dg_ref[...] += jnp.sum(dpart * xn, axis=0, keepdims=True)
            dst_ref[...] = _rms_bwd(dpart * g_ref[...], xn, r).astype(BF16)

    tile = lambda w: pl.BlockSpec((tm, w), lambda i: (i, 0))
    vec = lambda w: pl.BlockSpec((1, w), lambda i: (0, 0))
    return pl.pallas_call(
        body, name="out_backward", grid=(t // tm,),
        out_shape=(jax.ShapeDtypeStruct((t, hw), BF16), jax.ShapeDtypeStruct((t, hw), BF16),
                   jax.ShapeDtypeStruct((d, d), F32), jax.ShapeDtypeStruct((d, d), BF16),
                   jax.ShapeDtypeStruct((1, hw), F32), jax.ShapeDtypeStruct((1, hw), F32)),
        in_specs=[tile(d), pl.BlockSpec((d, d), lambda i: (0, 0)), tile(d), tile(hw), tile(hw), vec(hw), vec(hw)],
        out_specs=(tile(hw), tile(hw), pl.BlockSpec((d, d), lambda i: (0, 0)), pl.BlockSpec((d, d), lambda i: (0, 0)),
                   vec(hw), vec(hw)),
        compiler_params=_cparams(("arbitrary",), VMEM_BIG),
    )(dmix, w_out, oab, oa, ob, g_na, g_sw)


def _na_backward(proj, d_o, lse, tiles, batch, seq, rider=None):
    t = proj.shape[0]
    n_rows = seq // GRID_W
    n_pairs = NA_WIDTH // LANES
    win = NA_ROWS * GRID_W
    n_tiles = 2 * NA_ROWS - 2

    def body(q_ref, k_ref, v_ref, do_ref, lse_ref, tp_ref, dq_ref, dk_ref, dv_ref, dtp_ref, km, vm, dk_acc, dv_acc):
        @pl.when(pl.program_id(1) == 0)
        def _():
            dtp_ref[...] = jnp.zeros_like(dtp_ref)

        _na_prepare(k_ref, v_ref, km, vm)
        dk_acc[...] = jnp.zeros_like(dk_acc)
        dv_acc[...] = jnp.zeros_like(dv_acc)
        low = lax.broadcasted_iota(jnp.int32, (win, LANES), 1) < HEAD_DIM

        def scores(r):
            rs, off = _na_window(r, n_rows)
            rows = pl.ds(pl.multiple_of(r * GRID_W, GRID_W), GRID_W)
            wrows = pl.ds(pl.multiple_of(rs * GRID_W, GRID_W), win)
            q, do = q_ref[rows, :], do_ref[rows, :]
            k2 = _na_pair_window(km, wrows)
            s = _na_scores(q, k2, tp_ref, off)
            dp = _mm_nt(do, _na_pair_window(vm, wrows))
            return rows, wrows, off, q, do, k2, s, dp

        def finish(rows, wrows, off, q, do, k2, s, dp):
            p = _pair_probs_from_lse(s, lse_ref[rows, :])
            parts = []
            for h in range(2):
                ph, dph = p[:, h * win:(h + 1) * win], dp[:, h * win:(h + 1) * win]
                dsh = ph * (dph - jnp.sum(ph * dph, axis=-1, keepdims=True))
                for w in range(NA_ROWS // 2):
                    dtp_ref[h, 2 * w - off + (NA_ROWS - 1)] += dsh[:, w * LANES:(w + 1) * LANES]
                parts.append(dsh)
            dsb = (jnp.concatenate(parts, axis=1) * QK_SCALE).astype(BF16)
            dq_ref[rows, :] = _mm(dsb, k2).astype(BF16)
            dk_acc[wrows, :] += _pair_grad(dsb, q, low)
            dv_acc[wrows, :] += _pair_grad(p.astype(BF16), do, low)

        def row_group(i, carry):
            for state in [scores(NA_GROUP * i + j) for j in range(NA_GROUP)]:
                finish(*state)
            return carry

        lax.fori_loop(0, n_rows // NA_GROUP, row_group, 0)
        dk_ref[...] = dk_acc[...].astype(BF16)
        dv_ref[...] = dv_acc[...].astype(BF16)

    blk = lambda off: pl.BlockSpec((seq, LANES), lambda p, b: (b, off + p))
    out = jax.ShapeDtypeStruct((t, NA_WIDTH), BF16)
    return _hosted(
        body, rider, name="na_backward", grid=(n_pairs, batch),
        out_shape=[out, out, out, jax.ShapeDtypeStruct(tiles.shape, F32)],
        in_specs=[blk(0), blk(n_pairs), blk(2 * n_pairs), blk(0), blk(0),
                  pl.BlockSpec((2, n_tiles, GRID_W, LANES), lambda p, b: (p, 0, 0, 0))],
        out_specs=[blk(0), blk(0), blk(0), pl.BlockSpec((2, n_tiles, GRID_W, LANES), lambda p, b: (p, 0, 0, 0))],
        scratch_shapes=[pltpu.VMEM((2, seq, LANES), BF16), pltpu.VMEM((2, seq, LANES), BF16),
                        pltpu.VMEM((seq, LANES), F32), pltpu.VMEM((seq, LANES), F32)],
        compiler_params=_cparams(("arbitrary", "arbitrary")), args=[proj, proj, proj, d_o, lse, tiles])


def _na_bias_grad(dtiles, expand):
    n = dtiles.shape[0]

    def body(t_ref, e_ref, o_ref):
        flat = jnp.concatenate([t_ref[:, qq, :] for qq in range(GRID_W)], axis=1)
        o_ref[...] = lax.dot_general(flat, e_ref[...], (((1,), (1,)), ((), ())),
                                     precision=lax.Precision.HIGHEST, preferred_element_type=F32)

    return pl.pallas_call(
        body, name="na_bias_grad",
        out_shape=jax.ShapeDtypeStruct((n, expand.shape[0]), F32),
        compiler_params=_cparams(vmem=VMEM_BIG),
    )(dtiles, expand)


def _sw_backward(proj, d_o, lse, sink, batch, seq, rider=None):
    t = proj.shape[0]
    n_pairs = SW_WIDTH // LANES
    q_blk = 3 * NA_WIDTH // LANES
    k_blk = q_blk + n_pairs
    n_blocks = seq // SW_BLOCK
    pad = seq + 2 * SW_BLOCK

    def body(sink_ref, q_ref, k_ref, v_ref, do_ref, lse_ref, dq_ref, dk_ref, dv_ref, dsk_ref,
             k_lo, k_hi, v_lo, v_hi, dk_loc, dv_loc, dk_tot, dv_tot):
        hp = pl.program_id(1)
        g = hp // 2
        _sw_prepare(k_ref, g, k_lo, k_hi, seq)
        _sw_prepare(v_ref, g, v_lo, v_hi, seq)
        dk_loc[...] = jnp.zeros_like(dk_loc)
        dv_loc[...] = jnp.zeros_like(dv_loc)

        @pl.when(hp == 0)
        def _():
            dk_tot[...] = jnp.zeros_like(dk_tot)
            dv_tot[...] = jnp.zeros_like(dv_tot)

        band = 3 * SW_BLOCK
        low = lax.broadcasted_iota(jnp.int32, (band, LANES), 1) < HEAD_DIM

        sinks = (sink_ref[2 * hp], sink_ref[2 * hp + 1])

        def scores(n):
            rows = pl.ds(pl.multiple_of(n * SW_BLOCK, SW_BLOCK), SW_BLOCK)
            wrows = pl.ds(pl.multiple_of(n * SW_BLOCK, SW_BLOCK), band)
            qb, do = q_ref[rows, :], do_ref[rows, :]
            k2 = jnp.concatenate([k_lo[wrows, :], k_hi[wrows, :]], axis=0)
            v2 = jnp.concatenate([v_lo[wrows, :], v_hi[wrows, :]], axis=0)
            return n, rows, wrows, qb, do, k2, _mm_nt(qb, k2) * QK_SCALE, _mm_nt(do, v2)

        def finish(sink_acc, n, rows, wrows, qb, do, k2, s2, dp):
            p, ps = _sw_probs_from_lse(s2, _sw_mask(n, seq), sinks, lse_ref[rows, :])
            parts, new = [], []
            for i in range(2):
                ph, dph = p[:, i * band:(i + 1) * band], dp[:, i * band:(i + 1) * band]
                delta = jnp.sum(ph * dph, axis=-1, keepdims=True)
                parts.append(ph * (dph - delta))
                new.append(sink_acc[i] - ps[i] * delta)
            dsb = (jnp.concatenate(parts, axis=1) * QK_SCALE).astype(BF16)
            dq_ref[rows, :] = _mm(dsb, k2)
            dk_loc[wrows, :] += _pair_grad(dsb, qb, low)
            dv_loc[wrows, :] += _pair_grad(p.astype(BF16), do, low)
            return tuple(new)

        def block_group(i, carry):
            for state in [scores(SW_GROUP_BLOCKS * i + j) for j in range(SW_GROUP_BLOCKS)]:
                carry = finish(carry, *state)
            return carry

        zero = jnp.zeros((SW_BLOCK, 1), F32)
        s0, s1 = lax.fori_loop(0, n_blocks // SW_GROUP_BLOCKS, block_group, (zero, zero))
        row = lax.broadcasted_iota(jnp.int32, (SUBLANES, LANES), 0)
        dsk_ref[0, 0] = jnp.where(row == 0, jnp.sum(s0), jnp.where(row == 1, jnp.sum(s1), 0.0))

        lane_s = lax.broadcasted_iota(jnp.int32, (seq, LANES), 1)
        mine_g = (lane_s // HEAD_DIM) == g
        for loc, tot in ((dk_loc, dk_tot), (dv_loc, dv_tot)):
            part = loc[SW_BLOCK:SW_BLOCK + seq, :]
            tot[...] += jnp.where(mine_g, part + pltpu.roll(part, HEAD_DIM, 1), 0.0)

        @pl.when(hp == n_pairs - 1)
        def _():
            dk_ref[...] = dk_tot[...]
            dv_ref[...] = dv_tot[...].astype(BF16)

    return _hosted(
        body, rider, name="sw_backward", grid=(batch, n_pairs),
        out_shape=[jax.ShapeDtypeStruct((t, SW_WIDTH), F32), jax.ShapeDtypeStruct((t, LANES), F32),
                   jax.ShapeDtypeStruct((t, LANES), BF16), jax.ShapeDtypeStruct((batch, n_pairs, SUBLANES, LANES), F32)],
        in_specs=[pl.BlockSpec(memory_space=pltpu.SMEM),
                  pl.BlockSpec((seq, LANES), lambda b, p: (b, q_blk + p)),
                  pl.BlockSpec((seq, LANES), lambda b, p: (b, k_blk)),
                  pl.BlockSpec((seq, LANES), lambda b, p: (b, k_blk + 1)),
                  pl.BlockSpec((seq, LANES), lambda b, p: (b, p)), pl.BlockSpec((seq, LANES), lambda b, p: (b, p))],
        out_specs=[pl.BlockSpec((seq, LANES), lambda b, p: (b, p)), pl.BlockSpec((seq, LANES), lambda b, p: (b, 0)),
                   pl.BlockSpec((seq, LANES), lambda b, p: (b, 0)),
                   pl.BlockSpec((1, 1, SUBLANES, LANES), lambda b, p: (b, p, 0, 0))],
        scratch_shapes=[pltpu.VMEM((pad, LANES), BF16)] * 4 + [pltpu.VMEM((pad, LANES), F32)] * 2
        + [pltpu.VMEM((seq, LANES), F32)] * 2,
        compiler_params=_cparams(("arbitrary", "arbitrary")), args=[sink, proj, proj, proj, d_o, lse])


def _in_backward(dqkv_a, dq_b, dk_b, dv_b, w_in_t, h1, x, mod3, g_attn, dx1, cos_t, sin_t, seq):
    t, d = x.shape
    tm = TOKEN_TILE
    per_seq = seq // tm
    batch = t // seq
    dqa, dka, dva = dqkv_a
    n_q = SW_WIDTH // LANES

    def body(dqa_ref, dka_ref, dva_ref, dqb_ref, dkb_ref, dvb_ref, w_ref, h_ref, x_ref, mod_ref, g_ref, dx1_ref,
             cos_ref, sin_ref, dx_ref, gw_ref, gwb_ref, dsh_ref, dsc_ref, dg_ref):
        i = pl.program_id(0)

        @pl.when(i == 0)
        def _():
            gw_ref[...] = jnp.zeros_like(gw_ref)
            dg_ref[...] = jnp.zeros_like(dg_ref)

        @pl.when(i % per_seq == 0)
        def _():
            dsh_ref[...] = jnp.zeros_like(dsh_ref)
            dsc_ref[...] = jnp.zeros_like(dsc_ref)

        dr = jnp.concatenate([dqb_ref[...], dkb_ref[...]], axis=1)
        cos = jnp.concatenate([cos_ref[...]] * (n_q + 1), axis=1)
        sin = jnp.concatenate([sin_ref[...]] * (n_q + 1), axis=1)
        dr = dr * cos + _rope_rot(dr * sin)
        dproj = jnp.concatenate([dqa_ref[...], dka_ref[...], dva_ref[...], dr.astype(BF16), dvb_ref[...]], axis=1)
        gw_ref[...] += _mm_tn(dproj, h_ref[...])

        @pl.when(i == t // tm - 1)
        def _():
            gwb_ref[...] = gw_ref[...].astype(BF16)

        dh = _mm(dproj, w_ref[...])
        scale = mod_ref[0, :, d:2 * d]
        r, xn = _rms_stats(x_ref[...])
        xg = xn * g_ref[...]
        dxg = dh * (1.0 + scale)
        dx_ref[...] = dx1_ref[...] + _rms_bwd(dxg * g_ref[...], xn, r)
        dg_ref[...] += jnp.sum(dxg * xn, axis=0, keepdims=True)
        dsh_ref[0] += jnp.sum(dh, axis=0, keepdims=True)
        dsc_ref[0] += jnp.sum(dh * xg, axis=0, keepdims=True)

    tile = lambda w: pl.BlockSpec((tm, w), lambda i: (i, 0))
    per_b = pl.BlockSpec((1, 1, d), lambda i: (i // per_seq, 0, 0))
    small = jax.ShapeDtypeStruct((batch, 1, d), F32)
    rope = pl.BlockSpec((tm, LANES), lambda i: (i % per_seq, 0))
    return pl.pallas_call(
        body, name="in_backward", grid=(t // tm,),
        out_shape=(jax.ShapeDtypeStruct((t, d), F32), jax.ShapeDtypeStruct((IN_WIDTH, d), F32),
                   jax.ShapeDtypeStruct((IN_WIDTH, d), BF16), small, small, jax.ShapeDtypeStruct((1, d), F32)),
        in_specs=[tile(NA_WIDTH), tile(NA_WIDTH), tile(NA_WIDTH), tile(SW_WIDTH), tile(LANES), tile(LANES),
                  _resident((IN_WIDTH, d)), tile(d), tile(d),
                  pl.BlockSpec((1, 1, 6 * d), lambda i: (i // per_seq, 0, 0)),
                  pl.BlockSpec((1, d), lambda i: (0, 0)), tile(d), rope, rope],
        out_specs=(tile(d), _resident((IN_WIDTH, d)), _resident((IN_WIDTH, d)),
                   per_b, per_b, pl.BlockSpec((1, d), lambda i: (0, 0))),
        compiler_params=_cparams(("arbitrary",), VMEM_BIG),
    )(dqa, dka, dva, dq_b, dk_b, dv_b, w_in_t, h1, x, mod3, g_attn, dx1, cos_t, sin_t)


def _ada_weight_grad(sc_all, dmod_cols):
    d = sc_all.shape[1]
    ncol = dmod_cols.shape[1]

    def body(s_ref, m_ref, o_ref):
        o_ref[...] = _mm_tn(s_ref[...].astype(BF16), m_ref[...].astype(BF16))

    return pl.pallas_call(
        body, name="ada_weight_grad",
        out_shape=jax.ShapeDtypeStruct((d, ncol), F32),
        compiler_params=_cparams(vmem=VMEM_BIG),
    )(sc_all, dmod_cols)


def _row_tile(rows, cols):
    target = max(SUBLANES, (1 << 20) // (4 * cols))
    best = rows
    for cand in range(SUBLANES, rows + 1, SUBLANES):
        if rows % cand == 0 and cand <= target:
            best = cand
    return best if rows % SUBLANES == 0 else rows


def _sum_slots(parts, name):
    n = len(parts)
    _, rows, cols = parts[0][0].shape
    tr = _row_tile(rows, cols)
    per = rows // tr

    def body(*refs):
        o_ref = refs[-1]
        for q in range(n):
            @pl.when(pl.program_id(0) == q)
            def _(q=q):
                p_ref, own_ref = refs[2 * q], refs[2 * q + 1]
                o_ref[...] = ((own_ref[...] + p_ref[0].astype(F32)) + p_ref[1].astype(F32)) + p_ref[2].astype(F32)

    in_specs, args = [], []
    for q, (recv, own) in enumerate(parts):
        in_specs.append(pl.BlockSpec((N_SHARD - 1, tr, cols), lambda p, i, q=q: (0, jnp.where(p == q, i, 0), 0)))
        in_specs.append(pl.BlockSpec((tr, cols), lambda p, i, q=q: (jnp.where(p == q, i, 0), 0)))
        args += [recv, own]
    return pl.pallas_call(
        body, name=name, grid=(n, per),
        out_shape=jax.ShapeDtypeStruct((n * rows, cols), F32),
        in_specs=in_specs, out_specs=pl.BlockSpec((tr, cols), lambda p, i: (p * per + i, 0)),
        compiler_params=_cparams(("arbitrary", "arbitrary")),
    )(*args)


def _adamw_math(w, g, m, v):
    m2 = ADAM_B1 * m + (1.0 - ADAM_B1) * g
    v2 = ADAM_B2 * v + (1.0 - ADAM_B2) * (g * g)
    m_hat = m2 / (1.0 - ADAM_B1 ** ADAM_STEP)
    v_hat = v2 / (1.0 - ADAM_B2 ** ADAM_STEP)
    return -ADAM_LR * (m_hat / (jnp.sqrt(v_hat) + ADAM_EPS) + ADAM_WD * w), m2, v2


def _small_sums(partials, dmod, rider=None):
    moving = list(partials) + [dmod]
    n_mov = len(moving)

    def body(*refs):
        mov, refs = refs[:n_mov], refs[n_mov:]
        sums_out, refs = refs[:n_mov - 1], refs[n_mov - 1:]
        b_out, dmod_out, refs = refs[0], refs[1], refs[2:]
        everyone, (ssem, rsem) = refs[:n_mov], refs[n_mov:]
        x, y, c = _my_pos()
        me = 4 * x + 2 * y + c
        cps = []
        for a in range(n_mov):
            everyone[a][me] = mov[a][...]
            for k in range(1, N_DEV):
                peer = (_flip(x, (k >> 2) & 1), _flip(y, (k >> 1) & 1), _flip(c, k & 1))
                cps.append(pltpu.make_async_remote_copy(
                    src_ref=everyone[a].at[me], dst_ref=everyone[a].at[me], send_sem=ssem.at[a, k - 1],
                    recv_sem=rsem.at[a, k - 1], device_id=peer, device_id_type=MESH))
        for cp in cps:
            cp.start()
        for cp in cps:
            cp.wait_recv()

        def total(a):
            acc = everyone[a][0]
            for dev in range(1, N_DEV):
                acc = acc + everyone[a][dev]
            return acc

        for a in range(n_mov - 1):
            sums_out[a][...] = total(a)
        b_out[...] = jnp.sum(total(n_mov - 1), axis=0, keepdims=True)
        dmod_out[...] = everyone[n_mov - 1][...]
        for cp in cps:
            cp.wait_send()

    vm = pl.BlockSpec(memory_space=pltpu.VMEM)
    sds = jax.ShapeDtypeStruct
    out_shape = [sds(p.shape, F32) for p in partials]
    out_shape += [sds((1, dmod.shape[1]), F32), sds((N_DEV,) + dmod.shape, F32)]
    return _hosted(
        body, rider, name="small_sums", grid=(), out_shape=out_shape,
        in_specs=[vm] * n_mov, out_specs=[vm] * len(out_shape),
        scratch_shapes=[pltpu.VMEM((N_DEV,) + a.shape, F32) for a in moving]
        + [pltpu.SemaphoreType.DMA((n_mov, N_DEV - 1)), pltpu.SemaphoreType.DMA((n_mov, N_DEV - 1))],
        compiler_params=_cparams(vmem=VMEM_BIG), args=moving)


def _small_adamw(states, grads):
    n = len(states)

    def body(*refs):
        g_refs, wmv, res = refs[:n], refs[n:4 * n], refs[4 * n:]
        for j in range(n):
            g = g_refs[j][...]
            delta, m2, v2 = _adamw_math(wmv[3 * j][...], g, wmv[3 * j + 1][...], wmv[3 * j + 2][...])
            res[4 * j][...] = g
            res[4 * j + 1][...] = delta
            res[4 * j + 2][...] = m2
            res[4 * j + 3][...] = v2

    out_shape = []
    for w, _, _ in states:
        out_shape += [jax.ShapeDtypeStruct(w.shape, F32)] * 4
    outs = pl.pallas_call(body, name="small_adamw", out_shape=tuple(out_shape),
                          compiler_params=_cparams(vmem=VMEM_BIG))(*grads, *[a for st in states for a in st])
    return [outs[4 * j:4 * j + 4] for j in range(n)]


def _adamw(w, grads, m, v, name):
    rows, cols = w.shape
    tr = _row_tile(rows, cols)
    ng = len(grads)

    def body(*refs):
        w_ref = refs[0]
        g_refs = refs[1:1 + ng]
        m_ref, v_ref = refs[1 + ng], refs[2 + ng]
        g_out, d_out, m_out, v_out = refs[3 + ng:]
        g = g_refs[0][...]
        for extra in g_refs[1:]:
            g = g + extra[...]
        g_out[...] = g
        d_out[...], m_out[...], v_out[...] = _adamw_math(w_ref[...], g, m_ref[...], v_ref[...])

    spec = pl.BlockSpec((tr, cols), lambda i: (i, 0))
    out = jax.ShapeDtypeStruct((rows, cols), F32)
    return pl.pallas_call(
        body, name=name, grid=(rows // tr,),
        out_shape=(out, out, out, out),
        in_specs=[spec] * (3 + ng), out_specs=(spec, spec, spec, spec),
        compiler_params=_cparams(("arbitrary",)),
    )(w, *grads, m, v)


def _rope_tables(seq):
    half = HEAD_DIM // 2
    inv = np.float32(ROPE_THETA) ** (-np.arange(half, dtype=np.float32) / np.float32(half))
    ang = (np.arange(seq, dtype=np.float32)[:, None] * inv[None, :]).astype(np.float64)
    cos, sin = np.cos(ang).astype(np.float32), np.sin(ang).astype(np.float32)
    cos_t = np.concatenate([cos, cos, cos, cos], axis=1)
    sin_t = np.concatenate([-sin, sin, -sin, sin], axis=1)
    return jnp.asarray(cos_t), jnp.asarray(sin_t)


def kernel(x, c, w_ada, b_ada, g_attn, w_in, na_rpb, sw_sink, g_na_out, g_sw_out, w_out, g_ffn, w_up, conv_w, conv_b, w_down, g_final, loss_target, m_w_ada, m_b_ada, m_g_attn, m_w_in, m_na_rpb, m_sw_sink, m_g_na_out, m_g_sw_out, m_w_out, m_g_ffn, m_w_up, m_conv_w, m_conv_b, m_w_down, m_g_final, v_w_ada, v_b_ada, v_g_attn, v_w_in, v_na_rpb, v_sw_sink, v_g_na_out, v_g_sw_out, v_w_out, v_g_ffn, v_w_up, v_conv_w, v_conv_b, v_w_down, v_g_final):
    batch, seq, d = x.shape
    t = batch * seq
    assert d == D_MODEL and seq % (NA_ROWS * GRID_W) == 0 and seq % TOKEN_TILE == 0 and batch <= SUBLANES
    shard = 2 * lax.axis_index("x") + lax.axis_index("y")
    xt = x.reshape(t, d)
    tgt = loss_target.reshape(t, d)

    c8 = jnp.pad(c, ((0, SUBLANES - batch), (0, 0)))
    w_in_t_s = jnp.transpose(w_in[0]).astype(BF16)
    (mod8, sc_all), (w_in_g,) = _ada_forward(c8, w_ada[0], b_ada, _Rider("gather", [w_in_t_s]))
    mod3 = mod8[:batch].reshape(batch, 1, 6 * d)
    w_in_t = w_in_g.reshape(IN_WIDTH, d)

    cos_t, sin_t = _rope_tables(seq)
    (h1, proj), _ = _in_proj(xt, mod3, g_attn, w_in_t, cos_t, sin_t, seq)
    n_heads = NA_WIDTH // HEAD_DIM
    n_tiles, n_dc = 2 * NA_ROWS - 2, 2 * NA_COLS - 1
    expand, neg_mask = _na_bias_pattern()
    rpb = na_rpb[0]
    rows2 = jnp.concatenate([rpb[:, :-1, :], rpb[:, 1:, :]], axis=2).reshape(n_heads * n_tiles, 2 * n_dc)
    rows2 = jnp.pad(rows2, ((0, 0), (0, GRID_W - 2 * n_dc)))
    tiles = _na_bias_tiles(rows2, expand, neg_mask).reshape(n_heads, n_tiles, GRID_W, LANES)
    sink = sw_sink[0]
    w_up_b16 = w_up[0].astype(BF16)
    (oa, lse_a), (w_up_a,) = _na_forward(proj, tiles, batch, seq, _Rider("gather", [w_up_b16[:d // 2]]))
    (ob, lse_b), (w_up_b, conv_w_g, w_out_g) = _sw_forward(
        proj, sink, batch, seq, _Rider("gather", [w_up_b16[d // 2:], conv_w[0], w_out[0].astype(BF16)]))
    w_up_f = (w_up_a, w_up_b)
    w_out_f = w_out_g.reshape(d, d)
    conv_w_f = jnp.transpose(conv_w_g, (1, 0, 2)).reshape(3, D_FF)
    oab, mix, x1, h2 = _out_proj(oa, ob, g_na_out, g_sw_out, w_out_f, xt, mod3, g_ffn, seq)
    (u,), _ = _up_proj(h2, w_up_f)
    (a,), (w_down_g,) = _conv_gate(u, conv_w_f, conv_b, batch, seq, _Rider("gather", [w_down[0].astype(BF16)]))
    w_down_f = w_down_g.reshape(D_FF, d)
    dx2, dffn, loss_part, dgate_f, dg_final = _down_and_loss(a, w_down_f, x1, mod3, g_final.reshape(1, d), tgt, seq)

    gw_down, gw_down_b = _down_weight_grad(a, dffn)
    blocks = lambda g, rows: g.reshape(N_SHARD, rows // N_SHARD, d)
    (du, gconv_w, gconv_b), (recv_down, own_down) = _ffn_backward(
        dffn, w_down_f, u, conv_w_f, conv_b, batch, seq,
        _Rider("scatter", [blocks(gw_down_b, D_FF)], [blocks(gw_down, D_FF)]))
    (gw_up_top, gw_up_bot, gw_up_top_b, gw_up_bot_b), _ = _up_weight_grad(h2, du)
    (dx1, dmix, dshift_f, dscale_f, dgate_a, dg_ffn), _ = _up_backward(du, w_up_f, x1, mod3, g_ffn, dx2, mix, seq)
    doa, dob, gw_out, gw_out_b, dg_na, dg_sw = _out_backward(dmix, w_out_f, oab, oa, ob, g_na_out, g_sw_out)
    (dqa, dka, dva, dtiles), (recv_up_bot, own_up_bot) = _na_backward(
        proj, doa, lse_a, tiles, batch, seq, _Rider("scatter", [gw_up_bot_b], [gw_up_bot]))
    (dq_b, dk_b, dv_b, dsink_parts), (recv_out, recv_up_top, own_out, own_up_top) = _sw_backward(
        proj, dob, lse_b, sink, batch, seq,
        _Rider("scatter", [blocks(gw_out_b, d), gw_up_top_b], [blocks(gw_out, d), gw_up_top]))
    gx, gw_in_t, gw_in_b, dshift_a, dscale_a, dg_attn = _in_backward(
        (dqa, dka, dva), dq_b, dk_b, dv_b, w_in_t, h1, xt, mod3, g_attn, dx1, cos_t, sin_t, seq)

    red = _na_bias_grad(dtiles.reshape(n_heads * n_tiles, GRID_W, LANES), expand)[:, :2 * n_dc]
    red = red.reshape(n_heads, n_tiles, 2, n_dc)
    zero_row = jnp.zeros((n_heads, 1, n_dc), F32)
    g_rpb = (jnp.concatenate([red[:, :, 0, :], zero_row], axis=1)
             + jnp.concatenate([zero_row, red[:, :, 1, :]], axis=1))
    g_sink = jnp.sum(dsink_parts[:, :, :2, 0], axis=0).reshape(SW_WIDTH // HEAD_DIM)

    dmod = jnp.concatenate([dshift_a, dscale_a, dgate_a, dshift_f, dscale_f, dgate_f], axis=2).reshape(batch, 6 * d)
    rpb_shape = na_rpb.shape[1:]
    states = [(g_attn, m_g_attn, v_g_attn),
              (na_rpb.reshape(rpb_shape), m_na_rpb.reshape(rpb_shape), v_na_rpb.reshape(rpb_shape)),
              (sw_sink, m_sw_sink, v_sw_sink), (g_na_out, m_g_na_out, v_g_na_out), (g_sw_out, m_g_sw_out, v_g_sw_out),
              (g_ffn, m_g_ffn, v_g_ffn), (conv_b, m_conv_b, v_conv_b),
              (g_final.reshape(1, d), m_g_final.reshape(1, d), v_g_final.reshape(1, d))]
    partials = [dg_attn, g_rpb, g_sink.reshape(sw_sink.shape), dg_na, dg_sw, dg_ffn, gconv_b, dg_final,
                gconv_w, loss_part]
    mine = [None, _sum_slots([(recv_out, own_out)], "sum_w_out"),
            _sum_slots([(recv_up_top, own_up_top), (recv_up_bot, own_up_bot)], "sum_w_up"),
            _sum_slots([(recv_down, own_down)], "sum_w_down")]
    small, (recv_in, own_in, *theirs) = _small_sums(
        partials, dmod, _Riders([_Rider("scatter", [blocks(gw_in_b, IN_WIDTH)], [blocks(gw_in_t, IN_WIDTH)]),
                                 _Rider("swap", mine[1:])]))
    g_conv_w_full, loss_sum, g_b_ada, dmod_all = small[len(states):]
    r_small = _small_adamw(states + [(b_ada, m_b_ada, v_b_ada)], small[:len(states)] + [g_b_ada])
    loss = loss_sum[0, 0]
    mine[0] = _sum_slots([(recv_in, own_in)], "sum_w_in")
    theirs = _ride_alone(_Rider("swap", mine[:1]), "swap_sibling") + theirs
    dmod_rows = jnp.pad(dmod_all, ((0, 0), (0, SUBLANES - batch), (0, 0))).reshape(N_DEV * SUBLANES, 6 * d)
    ncol = w_ada.shape[2]
    g_w_ada = _ada_weight_grad(sc_all, lax.dynamic_slice(dmod_rows, (0, shard * ncol), (N_DEV * SUBLANES, ncol)))
    cshard = conv_w.shape[2]
    g_conv_w = lax.dynamic_slice(g_conv_w_full, (0, shard * cshard), (3, cshard))

    def big(w, m, v, g_parts, name):
        shape = w.shape
        outs = _adamw(w[0], g_parts, m[0], v[0], name)
        return [o.reshape(shape) for o in outs]

    r_w_ada = big(w_ada, m_w_ada, v_w_ada, [g_w_ada], "adamw_w_ada")
    r_w_in = [jnp.transpose(o).reshape(w_in.shape) for o in
              _adamw(jnp.transpose(w_in[0]), [mine[0], theirs[0]], jnp.transpose(m_w_in[0]), jnp.transpose(v_w_in[0]),
                     "adamw_w_in")]
    r_w_out = big(w_out, m_w_out, v_w_out, [mine[1], theirs[1]], "adamw_w_out")
    r_w_up = big(w_up, m_w_up, v_w_up, [mine[2], theirs[2]], "adamw_w_up")
    r_w_down = big(w_down, m_w_down, v_w_down, [mine[3], theirs[3]], "adamw_w_down")

    r_conv_w = big(conv_w, m_conv_w, v_conv_w, [g_conv_w], "adamw_conv_w")

    def pick(k):
        ga_, rpb_, sk_, gna_, gsw_, gf_, cb_, gfin_, b_ = [r[k] for r in r_small]
        return [r_w_ada[k], b_, ga_, r_w_in[k], rpb_.reshape(na_rpb.shape), sk_, gna_, gsw_, r_w_out[k], gf_,
                r_w_up[k], r_conv_w[k], cb_, r_w_down[k], gfin_.reshape(d)]

    return (loss, gx.reshape(batch, seq, d), *pick(0), *pick(1), *pick(2), *pick(3))
```

```python
import jax
import jax.numpy as jnp
import numpy as np
from jax import lax
from jax.experimental import pallas as pl
from jax.experimental.pallas import tpu as pltpu

F32 = jnp.float32
BF16 = jnp.bfloat16
MESH = pl.DeviceIdType.MESH

D_MODEL = 1024
HEAD_DIM = 64
NA_WIDTH = 512
SW_WIDTH = 512
SW_KV_WIDTH = 128
IN_WIDTH = 2304
D_FF = 2816
GRID_W = 64
NA_ROWS = 8
NA_COLS = 16
SW_BLOCK = 128
ROPE_THETA = 10000.0
EPS = 1e-6
NEG = -1e30
QK_SCALE = HEAD_DIM ** -0.5

ADAM_LR = 0.001
ADAM_B1 = 0.9
ADAM_B2 = 0.999
ADAM_EPS = 1e-08
ADAM_WD = 0.01
ADAM_STEP = 10

N_SHARD = 4
N_DEV = 8
LANES = 128
SUBLANES = 8
TOKEN_TILE = 512
FF_TILE = 256
CONV_CHUNK = 512
NA_GROUP = 8
SW_GROUP_BLOCKS = 8
VMEM_BIG = 56 * 1024 * 1024


def _mm(a, b):
    return jnp.dot(a, b, preferred_element_type=F32)


def _mm_nt(a, b):
    return lax.dot_general(a, b, (((1,), (1,)), ((), ())), preferred_element_type=F32)


def _mm_tn(a, b):
    return lax.dot_general(a, b, (((0,), (0,)), ((), ())), preferred_element_type=F32)


def _cparams(sem=None, vmem=None):
    kw = {}
    if sem is not None:
        kw["dimension_semantics"] = sem
    if vmem is not None:
        kw["vmem_limit_bytes"] = vmem
    return pltpu.CompilerParams(**kw)


def _resident(shape):
    return pl.BlockSpec(shape, lambda i: (0,) * len(shape), pipeline_mode=pl.Buffered(1))


def _sigmoid(x):
    return 1.0 / (1.0 + jnp.exp(-x))


def _rms_stats(x):
    r = lax.rsqrt(jnp.mean(x * x, axis=-1, keepdims=True) + EPS)
    return r, x * r


def _rms_bwd(dxn, xn, r):
    return r * (dxn - xn * jnp.mean(dxn * xn, axis=-1, keepdims=True))


def _my_pos():
    return lax.axis_index("x"), lax.axis_index("y"), lax.axis_index("c")


def _flip(v, bit):
    return 1 - v if bit else v


def _ada_forward(c8, w_ada, b_ada, rider):
    d = c8.shape[1]
    ncol = w_ada.shape[1]

    def body(c_ref, w_ref, b_ref, mod_ref, sc_ref, m_scr, mod_buf, ssem, rsem, ssem2, rsem2):
        x, y, c = _my_pos()
        me = 4 * x + 2 * y + c
        shard = 2 * x + y
        cv = c_ref[...]
        my_rows = pl.ds(pl.multiple_of(me * SUBLANES, SUBLANES), SUBLANES)
        sc_ref[my_rows, :] = cv * _sigmoid(cv)

        def copy1(k):
            peer = (_flip(x, (k >> 2) & 1), _flip(y, (k >> 1) & 1), _flip(c, k & 1))
            return pltpu.make_async_remote_copy(
                src_ref=sc_ref.at[my_rows, :], dst_ref=sc_ref.at[my_rows, :],
                send_sem=ssem.at[k - 1], recv_sem=rsem.at[k - 1], device_id=peer, device_id_type=MESH)

        sends = [copy1(k) for k in range(1, N_DEV)]
        for cp in sends:
            cp.start()
        for cp in sends:
            cp.wait_recv()
        m_scr[...] = _mm(sc_ref[...].astype(BF16), w_ref[...].astype(BF16))

        def copy2(k):
            px, py = _flip(x, (k >> 1) & 1), _flip(y, k & 1)
            rows = pl.ds(pl.multiple_of((4 * px + 2 * py + c) * SUBLANES, SUBLANES), SUBLANES)
            return pltpu.make_async_remote_copy(
                src_ref=m_scr.at[rows, :], dst_ref=mod_buf.at[shard],
                send_sem=ssem2.at[k - 1], recv_sem=rsem2.at[k - 1], device_id=(px, py, c), device_id_type=MESH)

        sends2 = [copy2(k) for k in range(1, N_SHARD)]
        for cp in sends2:
            cp.start()
        mod_buf[shard] = m_scr[my_rows, :]
        for cp in sends2:
            cp.wait_recv()
        for s in range(N_SHARD):
            mod_ref[:, s * ncol:(s + 1) * ncol] = mod_buf[s] + b_ref[:, s * ncol:(s + 1) * ncol]
        for cp in sends + sends2:
            cp.wait_send()

    vm = pl.BlockSpec(memory_space=pltpu.VMEM)
    return _hosted(
        body, rider, name="ada_forward", grid=(),
        out_shape=(jax.ShapeDtypeStruct((SUBLANES, N_SHARD * ncol), F32),
                   jax.ShapeDtypeStruct((N_DEV * SUBLANES, d), F32)),
        in_specs=[vm, vm, vm], out_specs=(vm, vm),
        scratch_shapes=[pltpu.VMEM((N_DEV * SUBLANES, ncol), F32), pltpu.VMEM((N_SHARD, SUBLANES, ncol), F32),
                        pltpu.SemaphoreType.DMA((N_DEV - 1,)), pltpu.SemaphoreType.DMA((N_DEV - 1,)),
                        pltpu.SemaphoreType.DMA((N_SHARD - 1,)), pltpu.SemaphoreType.DMA((N_SHARD - 1,))],
        compiler_params=_cparams(vmem=VMEM_BIG), args=[c8, w_ada, b_ada])


class _Rider:
    def __init__(self, kind, srcs, owns=()):
        self.kind, self.srcs, self.owns = kind, list(srcs), list(owns)
        n = len(self.srcs)
        sds = jax.ShapeDtypeStruct
        dma = pltpu.SemaphoreType.DMA
        if kind == "gather":
            self.out_shapes = [sds((N_SHARD,) + s.shape, s.dtype) for s in self.srcs]
            self.sems = [dma((n, N_SHARD - 1)), dma((n, N_SHARD - 1)), dma((n, N_SHARD - 1)), dma((n, N_SHARD - 1)),
                         dma((n,)), dma((n,))]
        elif kind == "scatter":
            self.out_shapes = ([sds((N_SHARD - 1,) + s.shape[1:], s.dtype) for s in self.srcs]
                               + [sds(o.shape[1:], o.dtype) for o in self.owns])
            m = max(len(self.owns), 1)
            self.sems = [dma((n, N_SHARD - 1)), dma((n, N_SHARD - 1)), dma((m,)), dma((m,))]
        else:
            self.out_shapes = [sds(s.shape, s.dtype) for s in self.srcs]
            self.sems = [dma((n,)), dma((n,))]

    @property
    def inputs(self):
        return self.srcs + self.owns

    def _halved(self, i):
        a = self.srcs[i]
        tile_rows = SUBLANES * (4 // jnp.dtype(a.dtype).itemsize)
        return self.kind == "gather" and a.shape[0] % (2 * tile_rows) == 0

    def copies(self, ins, outs, sems):
        n = len(self.srcs)
        x, y, c = _my_pos()
        shard = 2 * x + y
        remote, relay = [], []
        if self.kind == "swap":
            ssem, rsem = sems
            for i in range(n):
                remote.append(pltpu.make_async_remote_copy(
                    src_ref=ins[i], dst_ref=outs[i], send_sem=ssem.at[i], recv_sem=rsem.at[i],
                    device_id=(x, y, 1 - c), device_id_type=MESH))
            return remote, relay
        if self.kind == "gather":
            ssem, rsem, ssem2, rsem2, sib_s, sib_r = sems
        else:
            ssem, rsem, sib_s, sib_r = sems
        for i in range(n):
            if self.kind == "gather":
                remote.append(pltpu.make_async_remote_copy(
                    src_ref=ins[i], dst_ref=outs[i].at[shard], send_sem=sib_s.at[i], recv_sem=sib_r.at[i],
                    device_id=(x, y, 1 - c), device_id_type=MESH))
                half = ins[i].shape[0] // 2
                mine = pl.ds(pl.multiple_of(c * half, half), half) if self._halved(i) else None
            for k in range(1, N_SHARD):
                px, py = _flip(x, (k >> 1) & 1), _flip(y, k & 1)
                if self.kind == "gather":
                    src, dst = ins[i], outs[i].at[shard]
                    if mine is not None:
                        src, dst = src.at[mine], dst.at[mine]
                        got = outs[i].at[2 * px + py].at[mine]
                        relay.append(pltpu.make_async_remote_copy(
                            src_ref=got, dst_ref=got, send_sem=ssem2.at[i, k - 1], recv_sem=rsem2.at[i, k - 1],
                            device_id=(x, y, 1 - c), device_id_type=MESH))
                else:
                    src, dst = ins[i].at[2 * px + py], outs[i].at[k - 1]
                remote.append(pltpu.make_async_remote_copy(
                    src_ref=src, dst_ref=dst, send_sem=ssem.at[i, k - 1], recv_sem=rsem.at[i, k - 1],
                    device_id=(px, py, c), device_id_type=MESH))
        if self.kind == "scatter":
            for i in range(len(self.owns)):
                remote.append(pltpu.make_async_remote_copy(
                    src_ref=ins[n + i].at[shard], dst_ref=outs[n + i], send_sem=sib_s.at[i], recv_sem=sib_r.at[i],
                    device_id=(x, y, 1 - c), device_id_type=MESH))
        return remote, relay

    def start(self, ins, outs, sems):
        remote, _ = self.copies(ins, outs, sems)
        for cp in remote:
            cp.start()

    def wait(self, ins, outs, sems):
        remote, relay = self.copies(ins, outs, sems)
        for cp in remote:
            cp.wait_recv()
        for cp in relay:
            cp.start()
        for cp in relay:
            cp.wait_recv()
        for cp in remote + relay:
            cp.wait_send()


class _Riders:
    def __init__(self, riders):
        self.riders = list(riders)
        self.inputs = [a for r in self.riders for a in r.inputs]
        self.out_shapes = [s for r in self.riders for s in r.out_shapes]
        self.sems = [s for r in self.riders for s in r.sems]

    def _split(self, ins, outs, sems):
        for r in self.riders:
            ni, no, ns = len(r.inputs), len(r.out_shapes), len(r.sems)
            yield r, ins[:ni], outs[:no], sems[:ns]
            ins, outs, sems = ins[ni:], outs[no:], sems[ns:]

    def start(self, ins, outs, sems):
        for r, i, o, s in self._split(ins, outs, sems):
            r.start(i, o, s)

    def wait(self, ins, outs, sems):
        for r, i, o, s in self._split(ins, outs, sems):
            r.wait(i, o, s)


def _hosted(body, rider, *, name, grid, out_shape, in_specs, out_specs, scratch_shapes, compiler_params, args):
    out_shape, out_specs = list(out_shape), list(out_specs)
    if rider is None:
        outs = pl.pallas_call(body, name=name, grid=grid, out_shape=tuple(out_shape), in_specs=list(in_specs),
                              out_specs=tuple(out_specs), scratch_shapes=list(scratch_shapes),
                              compiler_params=compiler_params)(*args)
        return list(outs), []
    n_in, n_out, n_scr = len(in_specs), len(out_shape), len(scratch_shapes)
    nr_in, nr_out = len(rider.inputs), len(rider.out_shapes)
    n_steps = 1
    for size in grid:
        n_steps *= size

    def full(*refs):
        ins, refs = refs[:n_in], refs[n_in:]
        r_in, refs = refs[:nr_in], refs[nr_in:]
        outs, refs = refs[:n_out], refs[n_out:]
        r_out, refs = refs[:nr_out], refs[nr_out:]
        scr, sems = refs[:n_scr], refs[n_scr:]
        if grid:
            step = 0
            for ax, size in enumerate(grid):
                step = step * size + pl.program_id(ax)
            pl.when(step == 0)(lambda: rider.start(r_in, r_out, sems))
            body(*ins, *outs, *scr)
            pl.when(step == n_steps - 1)(lambda: rider.wait(r_in, r_out, sems))
        else:
            rider.start(r_in, r_out, sems)
            body(*ins, *outs, *scr)
            rider.wait(r_in, r_out, sems)

    hbm = pl.BlockSpec(memory_space=pl.ANY)
    res = pl.pallas_call(
        full, name=name, grid=grid, out_shape=tuple(out_shape + rider.out_shapes),
        in_specs=list(in_specs) + [hbm] * nr_in, out_specs=tuple(out_specs + [hbm] * nr_out),
        scratch_shapes=list(scratch_shapes) + rider.sems, compiler_params=compiler_params,
    )(*args, *rider.inputs)
    return list(res[:n_out]), list(res[n_out:])


def _ride_alone(rider, name):
    return _hosted(lambda: None, rider, name=name, grid=(), out_shape=[], in_specs=[], out_specs=[], scratch_shapes=[],
                   compiler_params=_cparams(), args=[])[1]


def _rope_rot(t):
    w = t.shape[1]
    lane = lax.broadcasted_iota(jnp.int32, t.shape, 1)
    first = (lane % HEAD_DIM) < (HEAD_DIM // 2)
    return jnp.where(first, pltpu.roll(t, w - HEAD_DIM // 2, 1), pltpu.roll(t, HEAD_DIM // 2, 1))


def _in_proj(x, mod3, g_attn, w_in_t, cos_t, sin_t, seq, rider=None):
    t, d = x.shape
    tm = 2 * TOKEN_TILE
    per_seq = seq // tm
    rope_lo, rope_hi = 3 * NA_WIDTH, 3 * NA_WIDTH + SW_WIDTH + SW_KV_WIDTH
    n_rep = (rope_hi - rope_lo) // LANES

    def body(x_ref, mod_ref, g_ref, w_ref, cos_ref, sin_ref, h_ref, p_ref):
        r, xn = _rms_stats(x_ref[...])
        shift, scale = mod_ref[0, :, 0:d], mod_ref[0, :, d:2 * d]
        hb = ((xn * g_ref[...]) * (1.0 + scale) + shift).astype(BF16)
        h_ref[...] = hb
        p_ref[:, :rope_lo] = _mm_nt(hb, w_ref[:rope_lo, :]).astype(BF16)
        pr = _mm_nt(hb, w_ref[rope_lo:rope_hi, :])
        cos = jnp.concatenate([cos_ref[...]] * n_rep, axis=1)
        sin = jnp.concatenate([sin_ref[...]] * n_rep, axis=1)
        p_ref[:, rope_lo:rope_hi] = (pr * cos + _rope_rot(pr) * sin).astype(BF16)
        p_ref[:, rope_hi:] = _mm_nt(hb, w_ref[rope_hi:, :]).astype(BF16)

    return _hosted(
        body, rider, name="in_proj", grid=(t // tm,),
        out_shape=[jax.ShapeDtypeStruct((t, d), BF16), jax.ShapeDtypeStruct((t, IN_WIDTH), BF16)],
        in_specs=[pl.BlockSpec((tm, d), lambda i: (i, 0)),
                  pl.BlockSpec((1, 1, 6 * d), lambda i: (i // per_seq, 0, 0)),
                  pl.BlockSpec((1, d), lambda i: (0, 0)),
                  pl.BlockSpec((IN_WIDTH, d), lambda i: (0, 0)),
                  pl.BlockSpec((tm, LANES), lambda i: (i % per_seq, 0)),
                  pl.BlockSpec((tm, LANES), lambda i: (i % per_seq, 0))],
        out_specs=[pl.BlockSpec((tm, d), lambda i: (i, 0)), pl.BlockSpec((tm, IN_WIDTH), lambda i: (i, 0))],
        scratch_shapes=[], compiler_params=_cparams(("arbitrary",), VMEM_BIG),
        args=[x, mod3, g_attn, w_in_t, cos_t, sin_t])


def _na_bias_pattern():
    n_dc = 2 * NA_COLS - 1
    j = np.arange(GRID_W)[:, None]
    m = np.arange(GRID_W * LANES)[None, :]
    q, lane = m // LANES, m % LANES
    k = lane % GRID_W
    cs = np.clip(q - NA_COLS // 2, 0, GRID_W - NA_COLS)
    ok = (k >= cs) & (k < cs + NA_COLS)
    hit = ok & (j < 2 * n_dc) & (lane // GRID_W == j // n_dc) & (k - q + (NA_COLS - 1) == j % n_dc)
    return jnp.asarray(hit.astype(np.float32)), jnp.asarray(np.where(ok, 0.0, NEG).astype(np.float32))


def _na_bias_tiles(rows2, expand, mask):
    n, width = rows2.shape[0], expand.shape[1]
    q_step = 16
    step = q_step * LANES

    def body(r_ref, e_ref, m_ref, o_ref):
        flat = jnp.dot(r_ref[...], e_ref[...], precision=lax.Precision.HIGHEST,
                       preferred_element_type=F32) + m_ref[...]
        for qq in range(q_step):
            o_ref[:, qq, :] = flat[:, qq * LANES:(qq + 1) * LANES]

    return pl.pallas_call(
        body, name="na_bias_tiles", grid=(width // step,),
        out_shape=jax.ShapeDtypeStruct((n, GRID_W, LANES), F32),
        in_specs=[pl.BlockSpec(rows2.shape, lambda i: (0, 0)), pl.BlockSpec((expand.shape[0], step), lambda i: (0, i)),
                  pl.BlockSpec((1, step), lambda i: (0, i))],
        out_specs=pl.BlockSpec((n, q_step, LANES), lambda i: (0, i, 0)),
        compiler_params=_cparams(("arbitrary",)),
    )(rows2, expand, mask)


def _na_prepare(k_ref, v_ref, km, vm):
    lane = lax.broadcasted_iota(jnp.int32, k_ref.shape, 1)
    low = lane < HEAD_DIM
    kv = k_ref[...]
    vv = v_ref[...]
    zero = jnp.zeros_like(kv)
    km[0] = jnp.where(low, kv, zero)
    km[1] = jnp.where(low, zero, kv)
    vm[0] = jnp.where(low, vv, zero)
    vm[1] = jnp.where(low, zero, vv)


def _na_window(r, n_rows):
    rs = jnp.clip(r - NA_ROWS // 2, 0, n_rows - NA_ROWS)
    return rs, r - rs


def _na_pair_window(ref, wrows):
    return jnp.concatenate([ref[0, wrows, :], ref[1, wrows, :]], axis=0)


def _na_scores(q, k2, tp_ref, off):
    bias = jnp.concatenate([tp_ref[h, 2 * w - off + (NA_ROWS - 1)] for h in range(2) for w in range(NA_ROWS // 2)],
                           axis=1)
    return _mm_nt(q, k2) * QK_SCALE + bias


def _pair_lse_block(lse):
    lane = lax.broadcasted_iota(jnp.int32, (lse[0].shape[0], LANES), 1)
    return jnp.where(lane < HEAD_DIM, lse[0], lse[1])


def _pair_softmax(s):
    win = s.shape[1] // 2
    halves, lse = [], []
    for h in range(2):
        sh = s[:, h * win:(h + 1) * win]
        m = jnp.max(sh, axis=-1, keepdims=True)
        e = jnp.exp(sh - m)
        l = jnp.sum(e, axis=-1, keepdims=True)
        halves.append(e / l)
        lse.append(m + jnp.log(l))
    return jnp.concatenate(halves, axis=1), _pair_lse_block(lse)


def _pair_grad(w2, x, low):
    keys = w2.shape[1] // 2
    zero = jnp.zeros_like(x)
    low_x = low[:x.shape[0]]
    stacked = jnp.concatenate([w2[:, :keys], w2[:, keys:]], axis=0)
    diag = jnp.concatenate([jnp.where(low_x, x, zero), jnp.where(low_x, zero, x)], axis=0)
    return _mm_tn(stacked, diag)


def _pair_probs_from_lse(s, lse_block):
    win = s.shape[1] // 2
    return jnp.concatenate([jnp.exp(s[:, h * win:(h + 1) * win] - lse_block[:, h * HEAD_DIM:h * HEAD_DIM + 1])
                            for h in range(2)], axis=1)


def _na_forward(proj, tiles, batch, seq, rider=None):
    t = proj.shape[0]
    n_rows = seq // GRID_W
    n_pairs = NA_WIDTH // LANES
    win = NA_ROWS * GRID_W

    def body(q_ref, k_ref, v_ref, tp_ref, o_ref, lse_ref, km, vm):
        _na_prepare(k_ref, v_ref, km, vm)

        def scores(r):
            rs, off = _na_window(r, n_rows)
            rows = pl.ds(pl.multiple_of(r * GRID_W, GRID_W), GRID_W)
            wrows = pl.ds(pl.multiple_of(rs * GRID_W, GRID_W), win)
            return rows, wrows, _na_scores(q_ref[rows, :], _na_pair_window(km, wrows), tp_ref, off)

        def finish(rows, wrows, s):
            p, lse = _pair_softmax(s)
            lse_ref[rows, :] = lse
            o_ref[rows, :] = _mm(p.astype(BF16), _na_pair_window(vm, wrows))

        def row_group(i, carry):
            for state in [scores(NA_GROUP * i + j) for j in range(NA_GROUP)]:
                finish(*state)
            return carry

        lax.fori_loop(0, n_rows // NA_GROUP, row_group, 0)

    return _hosted(
        body, rider, name="na_forward", grid=(batch, n_pairs),
        out_shape=[jax.ShapeDtypeStruct((t, NA_WIDTH), F32), jax.ShapeDtypeStruct((t, NA_WIDTH), F32)],
        in_specs=[pl.BlockSpec((seq, LANES), lambda b, p: (b, p)),
                  pl.BlockSpec((seq, LANES), lambda b, p: (b, n_pairs + p)),
                  pl.BlockSpec((seq, LANES), lambda b, p: (b, 2 * n_pairs + p)),
                  pl.BlockSpec((2, 2 * NA_ROWS - 2, GRID_W, LANES), lambda b, p: (p, 0, 0, 0))],
        out_specs=[pl.BlockSpec((seq, LANES), lambda b, p: (b, p)), pl.BlockSpec((seq, LANES), lambda b, p: (b, p))],
        scratch_shapes=[pltpu.VMEM((2, seq, LANES), BF16), pltpu.VMEM((2, seq, LANES), BF16)],
        compiler_params=_cparams(("arbitrary", "arbitrary")), args=[proj, proj, proj, tiles])


def _sw_prepare(kv_ref, g, dst_lo, dst_hi, seq):
    lane = lax.broadcasted_iota(jnp.int32, kv_ref.shape, 1)
    mine = (lane // HEAD_DIM) == g
    kg = jnp.where(mine, kv_ref[...].astype(F32), 0.0)
    kr = pltpu.roll(kg, HEAD_DIM, 1)
    first = g == 0
    zero = jnp.zeros((SW_BLOCK, LANES), BF16)
    for dst, val in ((dst_lo, jnp.where(first, kg, kr)), (dst_hi, jnp.where(first, kr, kg))):
        dst[0:SW_BLOCK, :] = zero
        dst[SW_BLOCK:SW_BLOCK + seq, :] = val.astype(BF16)
        dst[SW_BLOCK + seq:, :] = zero


def _sw_mask(n, seq):
    qi = lax.broadcasted_iota(jnp.int32, (SW_BLOCK, 3 * SW_BLOCK), 0)
    kj = lax.broadcasted_iota(jnp.int32, (SW_BLOCK, 3 * SW_BLOCK), 1)
    kpos = n * SW_BLOCK - SW_BLOCK + kj
    return (jnp.abs(qi + SW_BLOCK - kj) <= SW_BLOCK) & (kpos >= 0) & (kpos < seq)


def _sw_probs(s2, ok, sinks):
    band = s2.shape[1] // 2
    halves, lse = [], []
    for i in range(2):
        s = jnp.where(ok, s2[:, i * band:(i + 1) * band], NEG)
        m = jnp.maximum(jnp.max(s, axis=-1, keepdims=True), sinks[i])
        p = jnp.exp(s - m)
        den = jnp.sum(p, axis=-1, keepdims=True) + jnp.exp(sinks[i] - m)
        halves.append(p / den)
        lse.append(m + jnp.log(den))
    return jnp.concatenate(halves, axis=1), _pair_lse_block(lse)


def _sw_probs_from_lse(s2, ok, sinks, lse_block):
    band = s2.shape[1] // 2
    halves, sink_p = [], []
    for i in range(2):
        lse = lse_block[:, i * HEAD_DIM:i * HEAD_DIM + 1]
        halves.append(jnp.exp(jnp.where(ok, s2[:, i * band:(i + 1) * band], NEG) - lse))
        sink_p.append(jnp.exp(sinks[i] - lse))
    return jnp.concatenate(halves, axis=1), sink_p


def _sw_forward(proj, sink, batch, seq, rider=None):
    t = proj.shape[0]
    n_pairs = SW_WIDTH // LANES
    q_blk = 3 * NA_WIDTH // LANES
    k_blk = q_blk + n_pairs
    n_blocks = seq // SW_BLOCK
    pad = seq + 2 * SW_BLOCK

    def body(sink_ref, q_ref, k_ref, v_ref, o_ref, lse_ref, k_lo, k_hi, v_lo, v_hi):
        hp = pl.program_id(1)
        g = hp // 2

        @pl.when(hp % 2 == 0)
        def _():
            _sw_prepare(k_ref, g, k_lo, k_hi, seq)
            _sw_prepare(v_ref, g, v_lo, v_hi, seq)

        sinks = (sink_ref[2 * hp], sink_ref[2 * hp + 1])

        def scores(n):
            rows = pl.ds(pl.multiple_of(n * SW_BLOCK, SW_BLOCK), SW_BLOCK)
            wrows = pl.ds(pl.multiple_of(n * SW_BLOCK, SW_BLOCK), 3 * SW_BLOCK)
            k2 = jnp.concatenate([k_lo[wrows, :], k_hi[wrows, :]], axis=0)
            return n, rows, wrows, _mm_nt(q_ref[rows, :], k2) * QK_SCALE

        def finish(n, rows, wrows, s2):
            p, lse = _sw_probs(s2, _sw_mask(n, seq), sinks)
            lse_ref[rows, :] = lse
            v2 = jnp.concatenate([v_lo[wrows, :], v_hi[wrows, :]], axis=0)
            o_ref[rows, :] = _mm(p.astype(BF16), v2)

        def block_group(i, carry):
            for state in [scores(SW_GROUP_BLOCKS * i + j) for j in range(SW_GROUP_BLOCKS)]:
                finish(*state)
            return carry

        lax.fori_loop(0, n_blocks // SW_GROUP_BLOCKS, block_group, 0)

    return _hosted(
        body, rider, name="sw_forward", grid=(batch, n_pairs),
        out_shape=[jax.ShapeDtypeStruct((t, SW_WIDTH), F32), jax.ShapeDtypeStruct((t, SW_WIDTH), F32)],
        in_specs=[pl.BlockSpec(memory_space=pltpu.SMEM),
                  pl.BlockSpec((seq, LANES), lambda b, p: (b, q_blk + p)),
                  pl.BlockSpec((seq, LANES), lambda b, p: (b, k_blk)),
                  pl.BlockSpec((seq, LANES), lambda b, p: (b, k_blk + 1))],
        out_specs=[pl.BlockSpec((seq, LANES), lambda b, p: (b, p)), pl.BlockSpec((seq, LANES), lambda b, p: (b, p))],
        scratch_shapes=[pltpu.VMEM((pad, LANES), BF16)] * 4,
        compiler_params=_cparams(("arbitrary", "arbitrary")), args=[sink, proj, proj, proj])


def _out_proj(oa, ob, g_na, g_sw, w_out, x, mod3, g_ffn, seq):
    t, d = x.shape
    tm = TOKEN_TILE
    per_seq = seq // tm

    def body(oa_ref, ob_ref, gna_ref, gsw_ref, w_ref, x_ref, mod_ref, gf_ref, oab_ref, mix_ref, x1_ref, h2_ref):
        _, na = _rms_stats(oa_ref[...])
        _, nb = _rms_stats(ob_ref[...])
        oab = jnp.concatenate([na * gna_ref[...], nb * gsw_ref[...]], axis=1).astype(BF16)
        oab_ref[...] = oab
        mix = _mm(oab, w_ref[...])
        mix_ref[...] = mix
        gate_a = mod_ref[0, :, 2 * d:3 * d]
        shift_f, scale_f = mod_ref[0, :, 3 * d:4 * d], mod_ref[0, :, 4 * d:5 * d]
        x1 = x_ref[...] + gate_a * mix
        x1_ref[...] = x1
        _, xn = _rms_stats(x1)
        h2_ref[...] = ((xn * gf_ref[...]) * (1.0 + scale_f) + shift_f).astype(BF16)

    tile = lambda w: pl.BlockSpec((tm, w), lambda i: (i, 0))
    vec = lambda w: pl.BlockSpec((1, w), lambda i: (0, 0))
    return pl.pallas_call(
        body, name="out_proj", grid=(t // tm,),
        out_shape=(jax.ShapeDtypeStruct((t, d), BF16), jax.ShapeDtypeStruct((t, d), F32),
                   jax.ShapeDtypeStruct((t, d), F32), jax.ShapeDtypeStruct((t, d), BF16)),
        in_specs=[tile(NA_WIDTH), tile(SW_WIDTH), vec(NA_WIDTH), vec(SW_WIDTH),
                  pl.BlockSpec((d, d), lambda i: (0, 0)), tile(d),
                  pl.BlockSpec((1, 1, 6 * d), lambda i: (i // per_seq, 0, 0)), vec(d)],
        out_specs=(tile(d), tile(d), tile(d), tile(d)),
        compiler_params=_cparams(("arbitrary",), VMEM_BIG),
    )(oa, ob, g_na, g_sw, w_out, x, mod3, g_ffn)


def _up_proj(h2, w_up_halves, rider=None):
    t, d = h2.shape
    tm = 2 * TOKEN_TILE
    w_a, w_b = w_up_halves
    half, wcol = w_a.shape[1], w_a.shape[2]

    def body(h_ref, wa_ref, wb_ref, u_ref):
        u_ref[0] = (_mm(h_ref[:, :half], wa_ref[0]) + _mm(h_ref[:, half:], wb_ref[0])).astype(BF16)

    w_spec = pl.BlockSpec((1, half, wcol), lambda j, i: (j, 0, 0))
    return _hosted(
        body, rider, name="up_proj", grid=(N_SHARD, t // tm),
        out_shape=[jax.ShapeDtypeStruct((2, t, D_FF), BF16)],
        in_specs=[pl.BlockSpec((tm, d), lambda j, i: (i, 0)), w_spec, w_spec],
        out_specs=[pl.BlockSpec((1, tm, wcol), lambda j, i: (j // 2, i, j % 2))],
        scratch_shapes=[], compiler_params=_cparams(("arbitrary", "arbitrary"), VMEM_BIG), args=[h2, w_a, w_b])


def _taps_chunk(load, s, rows, seq):
    halo = 2 * SUBLANES
    cur = load(s, rows)
    above = load(pl.multiple_of(jnp.maximum(s - halo, 0), halo), halo)
    below = load(pl.multiple_of(jnp.minimum(s + rows, seq - halo), halo), halo)
    up = jnp.where(s > 0, above[halo - 1:halo, :], 0.0)
    dn = jnp.where(s + rows < seq, below[0:1, :], 0.0)
    row = lax.broadcasted_iota(jnp.int32, cur.shape, 0)
    prev = jnp.where(row == 0, up, pltpu.roll(cur, 1, 0))
    nxt = jnp.where(row == rows - 1, dn, pltpu.roll(cur, rows - 1, 0))
    return cur, prev, nxt


def _conv_gate(u, conv_w, conv_b, batch, seq, rider=None):
    t = u.shape[1]
    cw = FF_TILE
    rows = CONV_CHUNK

    def body(u_ref, w_ref, b_ref, a_ref):
        def chunk(i, carry):
            s = pl.multiple_of(i * rows, rows)
            gt, prev, nxt = _taps_chunk(lambda at, n: u_ref[1, pl.ds(at, n), :].astype(F32), s, rows, seq)
            gc = prev * w_ref[0:1, :] + gt * w_ref[1:2, :] + nxt * w_ref[2:3, :] + b_ref[...]
            a_ref[pl.ds(s, rows), :] = ((gc * _sigmoid(gc)) * u_ref[0, pl.ds(s, rows), :].astype(F32)).astype(BF16)
            return carry

        lax.fori_loop(0, seq // rows, chunk, 0)

    return _hosted(
        body, rider, name="conv_gate", grid=(batch, D_FF // cw),
        out_shape=[jax.ShapeDtypeStruct((t, D_FF), BF16)],
        in_specs=[pl.BlockSpec((2, seq, cw), lambda b, j: (0, b, j)),
                  pl.BlockSpec((3, cw), lambda b, j: (0, j)), pl.BlockSpec((1, cw), lambda b, j: (0, j))],
        out_specs=[pl.BlockSpec((seq, cw), lambda b, j: (b, j))], scratch_shapes=[],
        compiler_params=_cparams(("arbitrary", "arbitrary"), VMEM_BIG), args=[u, conv_w, conv_b])


def _down_and_loss(a, w_down, x1, mod3, g_final, target, seq):
    t, d = x1.shape
    tm = TOKEN_TILE
    per_seq = seq // tm
    batch = t // seq

    def body(a_ref, w_ref, x1_ref, mod_ref, g_ref, tgt_ref, dx2_ref, dffn_ref, loss_ref, dgate_ref, dg_ref):
        i = pl.program_id(0)
        f = _mm(a_ref[...], w_ref[...])
        gate_f = mod_ref[0, :, 5 * d:6 * d]
        x2 = x1_ref[...] + gate_f * f
        r, xn = _rms_stats(x2)
        err = xn * g_ref[...] - tgt_ref[...]
        part = 0.5 * jnp.sum(jnp.mean(err * err, axis=-1, keepdims=True))
        dy = err / d
        dx2 = _rms_bwd(dy * g_ref[...], xn, r)
        dx2_ref[...] = dx2
        dffn_ref[...] = (dx2 * gate_f).astype(BF16)

        @pl.when(i == 0)
        def _():
            loss_ref[...] = jnp.zeros_like(loss_ref)
            dg_ref[...] = jnp.zeros_like(dg_ref)

        @pl.when(i % per_seq == 0)
        def _():
            dgate_ref[...] = jnp.zeros_like(dgate_ref)

        loss_ref[...] += part
        dg_ref[...] += jnp.sum(dy * xn, axis=0, keepdims=True)
        dgate_ref[0] += jnp.sum(dx2 * f, axis=0, keepdims=True)

    tile = lambda w: pl.BlockSpec((tm, w), lambda i: (i, 0))
    return pl.pallas_call(
        body, name="down_loss", grid=(t // tm,),
        out_shape=(jax.ShapeDtypeStruct((t, d), F32), jax.ShapeDtypeStruct((t, d), BF16),
                   jax.ShapeDtypeStruct((SUBLANES, LANES), F32), jax.ShapeDtypeStruct((batch, 1, d), F32),
                   jax.ShapeDtypeStruct((1, d), F32)),
        in_specs=[tile(D_FF), _resident((D_FF, d)), tile(d),
                  pl.BlockSpec((1, 1, 6 * d), lambda i: (i // per_seq, 0, 0)),
                  pl.BlockSpec((1, d), lambda i: (0, 0)), tile(d)],
        out_specs=(tile(d), tile(d), pl.BlockSpec((SUBLANES, LANES), lambda i: (0, 0)),
                   pl.BlockSpec((1, 1, d), lambda i: (i // per_seq, 0, 0)), pl.BlockSpec((1, d), lambda i: (0, 0))),
        compiler_params=_cparams(("arbitrary",), VMEM_BIG),
    )(a, w_down, x1, mod3, g_final, target)


def _down_weight_grad(a, dffn):
    t, dff = a.shape
    d = dffn.shape[1]
    tk = 2 * TOKEN_TILE
    n_k = t // tk

    def body(a_ref, df_ref, g_ref, gb_ref):
        k = pl.program_id(0)

        @pl.when(k == 0)
        def _():
            g_ref[...] = jnp.zeros_like(g_ref)

        g_ref[...] += _mm_tn(a_ref[...], df_ref[...])

        @pl.when(k == n_k - 1)
        def _():
            gb_ref[...] = g_ref[...].astype(BF16)

    whole = _resident((dff, d))
    return pl.pallas_call(
        body, name="down_weight_grad", grid=(n_k,),
        out_shape=(jax.ShapeDtypeStruct((dff, d), F32), jax.ShapeDtypeStruct((dff, d), BF16)),
        in_specs=[pl.BlockSpec((tk, dff), lambda k: (k, 0)), pl.BlockSpec((tk, d), lambda k: (k, 0))],
        out_specs=(whole, whole),
        compiler_params=_cparams(("arbitrary",), VMEM_BIG),
    )(a, dffn)


def _ffn_backward(dffn, w_down, u, conv_w, conv_b, batch, seq, rider=None):
    t, d = dffn.shape
    cw = FF_TILE
    rows = CONV_CHUNK

    def body(df_ref, wd_ref, u_ref, w_ref, b_ref, du_ref, gcw_ref, gcb_ref, da_scr, dgc_scr):
        b = pl.program_id(1)
        da_scr[...] = _mm_nt(df_ref[...], wd_ref[...])

        @pl.when(b == 0)
        def _():
            gcw_ref[...] = jnp.zeros_like(gcw_ref)
            gcb_ref[...] = jnp.zeros_like(gcb_ref)

        def fold(v):
            return jnp.sum(v.reshape(rows // SUBLANES, SUBLANES, cw), axis=0)

        def chunk(i, carry):
            s = pl.multiple_of(i * rows, rows)
            here = pl.ds(s, rows)
            gt, prev, nxt = _taps_chunk(lambda at, n: u_ref[1, pl.ds(at, n), :].astype(F32), s, rows, seq)
            val, da = u_ref[0, here, :].astype(F32), da_scr[here, :]
            gc = prev * w_ref[0:1, :] + gt * w_ref[1:2, :] + nxt * w_ref[2:3, :] + b_ref[...]
            sg = _sigmoid(gc)
            sl = gc * sg
            du_ref[0, here, :] = (da * sl).astype(BF16)
            dgc = (da * val) * (sg * (1.0 + gc * (1.0 - sg)))
            dgc_scr[here, :] = dgc
            cb, c0, c1, c2 = carry
            return cb + fold(dgc), c0 + fold(dgc * prev), c1 + fold(dgc * gt), c2 + fold(dgc * nxt)

        zero = jnp.zeros((SUBLANES, cw), F32)
        cb, c0, c1, c2 = lax.fori_loop(0, seq // rows, chunk, (zero, zero, zero, zero))
        gcb_ref[...] += jnp.sum(cb, axis=0, keepdims=True)
        gcw_ref[0:1, :] += jnp.sum(c0, axis=0, keepdims=True)
        gcw_ref[1:2, :] += jnp.sum(c1, axis=0, keepdims=True)
        gcw_ref[2:3, :] += jnp.sum(c2, axis=0, keepdims=True)

        def chunk2(i, carry):
            s = pl.multiple_of(i * rows, rows)
            dgc, dprev, dnxt = _taps_chunk(lambda at, n: dgc_scr[pl.ds(at, n), :], s, rows, seq)
            du_ref[1, pl.ds(s, rows), :] = (dnxt * w_ref[0:1, :] + dgc * w_ref[1:2, :]
                                            + dprev * w_ref[2:3, :]).astype(BF16)
            return carry

        lax.fori_loop(0, seq // rows, chunk2, 0)

    return _hosted(
        body, rider, name="ffn_backward", grid=(D_FF // cw, batch),
        out_shape=[jax.ShapeDtypeStruct((2, t, D_FF), BF16),
                   jax.ShapeDtypeStruct((3, D_FF), F32), jax.ShapeDtypeStruct((1, D_FF), F32)],
        in_specs=[pl.BlockSpec((seq, d), lambda j, b: (b, 0)), pl.BlockSpec((cw, d), lambda j, b: (j, 0)),
                  pl.BlockSpec((2, seq, cw), lambda j, b: (0, b, j)),
                  pl.BlockSpec((3, cw), lambda j, b: (0, j)), pl.BlockSpec((1, cw), lambda j, b: (0, j))],
        out_specs=[pl.BlockSpec((2, seq, cw), lambda j, b: (0, b, j)),
                   pl.BlockSpec((3, cw), lambda j, b: (0, j)), pl.BlockSpec((1, cw), lambda j, b: (0, j))],
        scratch_shapes=[pltpu.VMEM((seq, cw), F32), pltpu.VMEM((seq, cw), F32)],
        compiler_params=_cparams(("arbitrary", "arbitrary"), VMEM_BIG), args=[dffn, w_down, u, conv_w, conv_b])


def _up_backward(du, w_up, x1, mod3, g_ffn, dx2, mix, seq, rider=None):
    _, t, _ = du.shape
    d = x1.shape[1]
    tm = TOKEN_TILE
    per_seq = seq // tm
    batch = t // seq
    w_a, w_b = w_up
    half, wcol = w_a.shape[1], w_a.shape[2]

    def body(du_ref, wa_ref, wb_ref, x1_ref, mod_ref, g_ref, dx2_ref, mix_ref,
             dx1_ref, dmix_ref, dsh_ref, dsc_ref, dga_ref, dg_ref):
        i = pl.program_id(0)
        parts = []
        for w_ref in (wa_ref, wb_ref):
            acc = jnp.zeros((tm, half), F32)
            for j in range(N_SHARD):
                acc = acc + _mm_nt(du_ref[j // 2, :, (j % 2) * wcol:(j % 2 + 1) * wcol], w_ref[j])
            parts.append(acc)
        dh = jnp.concatenate(parts, axis=1)
        gate_a = mod_ref[0, :, 2 * d:3 * d]
        scale_f = mod_ref[0, :, 4 * d:5 * d]
        r, xn = _rms_stats(x1_ref[...])
        xg = xn * g_ref[...]
        dxg = dh * (1.0 + scale_f)
        dx1 = dx2_ref[...] + _rms_bwd(dxg * g_ref[...], xn, r)
        dx1_ref[...] = dx1
        dmix_ref[...] = (dx1 * gate_a).astype(BF16)

        @pl.when(i == 0)
        def _():
            dg_ref[...] = jnp.zeros_like(dg_ref)

        @pl.when(i % per_seq == 0)
        def _():
            dsh_ref[...] = jnp.zeros_like(dsh_ref)
            dsc_ref[...] = jnp.zeros_like(dsc_ref)
            dga_ref[...] = jnp.zeros_like(dga_ref)

        dg_ref[...] += jnp.sum(dxg * xn, axis=0, keepdims=True)
        dsh_ref[0] += jnp.sum(dh, axis=0, keepdims=True)
        dsc_ref[0] += jnp.sum(dh * xg, axis=0, keepdims=True)
        dga_ref[0] += jnp.sum(dx1 * mix_ref[...], axis=0, keepdims=True)

    tile = lambda w: pl.BlockSpec((tm, w), lambda i: (i, 0))
    per_b = pl.BlockSpec((1, 1, d), lambda i: (i // per_seq, 0, 0))
    small = jax.ShapeDtypeStruct((batch, 1, d), F32)
    return _hosted(
        body, rider, name="up_backward", grid=(t // tm,),
        out_shape=[jax.ShapeDtypeStruct((t, d), F32), jax.ShapeDtypeStruct((t, d), BF16), small, small, small,
                   jax.ShapeDtypeStruct((1, d), F32)],
        in_specs=[pl.BlockSpec((2, tm, D_FF), lambda i: (0, i, 0)),
                  _resident((N_SHARD, half, wcol)), _resident((N_SHARD, half, wcol)), tile(d),
                  pl.BlockSpec((1, 1, 6 * d), lambda i: (i // per_seq, 0, 0)),
                  pl.BlockSpec((1, d), lambda i: (0, 0)), tile(d), tile(d)],
        out_specs=[tile(d), tile(d), per_b, per_b, per_b, pl.BlockSpec((1, d), lambda i: (0, 0))],
        scratch_shapes=[], compiler_params=_cparams(("arbitrary",), VMEM_BIG),
        args=[du, w_a, w_b, x1, mod3, g_ffn, dx2, mix])


def _up_weight_grad(h2, du, rider=None):
    t, d = h2.shape
    tk = 2 * TOKEN_TILE
    wcol = D_FF // 2
    half = d // 2
    n_k = t // tk

    def body(h_ref, du_ref, ga_ref, gb_ref, ga16_ref, gb16_ref):
        k = pl.program_id(1)

        @pl.when(k == 0)
        def _():
            ga_ref[...] = jnp.zeros_like(ga_ref)
            gb_ref[...] = jnp.zeros_like(gb_ref)

        du = du_ref[0]
        ga_ref[0] += _mm_tn(h_ref[:, :half], du)
        gb_ref[0] += _mm_tn(h_ref[:, half:], du)

        @pl.when(k == n_k - 1)
        def _():
            ga16_ref[...] = ga_ref[...].astype(BF16)
            gb16_ref[...] = gb_ref[...].astype(BF16)

    g_spec = pl.BlockSpec((1, half, wcol), lambda j, k: (j, 0, 0))
    f32_out = jax.ShapeDtypeStruct((N_SHARD, half, wcol), F32)
    b16_out = jax.ShapeDtypeStruct((N_SHARD, half, wcol), BF16)
    return _hosted(
        body, rider, name="up_weight_grad", grid=(N_SHARD, n_k),
        out_shape=[f32_out, f32_out, b16_out, b16_out],
        in_specs=[pl.BlockSpec((tk, d), lambda j, k: (k, 0)),
                  pl.BlockSpec((1, tk, wcol), lambda j, k: (j // 2, k, j % 2))],
        out_specs=[g_spec, g_spec, g_spec, g_spec], scratch_shapes=[],
        compiler_params=_cparams(("arbitrary", "arbitrary"), VMEM_BIG), args=[h2, du])


def _out_backward(dmix, w_out, oab, oa, ob, g_na, g_sw):
    t, d = dmix.shape
    tm = 2 * TOKEN_TILE
    hw = NA_WIDTH

    def body(dm_ref, w_ref, oab_ref, oa_ref, ob_ref, gna_ref, gsw_ref,
             doa_ref, dob_ref, gw_ref, gwb_ref, dgna_ref, dgsw_ref):
        @pl.when(pl.program_id(0) == 0)
        def _():
            gw_ref[...] = jnp.zeros_like(gw_ref)
            dgna_ref[...] = jnp.zeros_like(dgna_ref)
            dgsw_ref[...] = jnp.zeros_like(dgsw_ref)

        dm = dm_ref[...]
        gw_ref[...] += _mm_tn(oab_ref[...], dm)

        @pl.when(pl.program_id(0) == t // tm - 1)
        def _():
            gwb_ref[...] = gw_ref[...].astype(BF16)

        do = _mm_nt(dm, w_ref[...])
        for raw_ref, g_ref, dst_ref, dg_ref, lo in ((oa_ref, gna_ref, doa_ref, dgna_ref, 0),
                                                     (ob_ref, gsw_ref, dob_ref, dgsw_ref, hw)):
            r, xn = _rms_stats(raw_ref[...])
            dpart = do[:, lo:lo + hw]
            dg_ref[...] += jnp.sum(dpart * xn, axis=0, keepdims=True)
            dst_ref[...] = _rms_bwd(dpart * g_ref[...], xn, r).astype(BF16)

    tile = lambda w: pl.BlockSpec((tm, w), lambda i: (i, 0))
    vec = lambda w: pl.BlockSpec((1, w), lambda i: (0, 0))
    return pl.pallas_call(
        body, name="out_backward", grid=(t // tm,),
        out_shape=(jax.ShapeDtypeStruct((t, hw), BF16), jax.ShapeDtypeStruct((t, hw), BF16),
                   jax.ShapeDtypeStruct((d, d), F32), jax.ShapeDtypeStruct((d, d), BF16),
                   jax.ShapeDtypeStruct((1, hw), F32), jax.ShapeDtypeStruct((1, hw), F32)),
        in_specs=[tile(d), pl.BlockSpec((d, d), lambda i: (0, 0)), tile(d), tile(hw), tile(hw), vec(hw), vec(hw)],
        out_specs=(tile(hw), tile(hw), pl.BlockSpec((d, d), lambda i: (0, 0)), pl.BlockSpec((d, d), lambda i: (0, 0)),
                   vec(hw), vec(hw)),
        compiler_params=_cparams(("arbitrary",), VMEM_BIG),
    )(dmix, w_out, oab, oa, ob, g_na, g_sw)


def _na_backward(proj, d_o, lse, tiles, batch, seq, rider=None):
    t = proj.shape[0]
    n_rows = seq // GRID_W
    n_pairs = NA_WIDTH // LANES
    win = NA_ROWS * GRID_W
    n_tiles = 2 * NA_ROWS - 2

    def body(q_ref, k_ref, v_ref, do_ref, lse_ref, tp_ref, dq_ref, dk_ref, dv_ref, dtp_ref, km, vm, dk_acc, dv_acc):
        @pl.when(pl.program_id(1) == 0)
        def _():
            dtp_ref[...] = jnp.zeros_like(dtp_ref)

        _na_prepare(k_ref, v_ref, km, vm)
        dk_acc[...] = jnp.zeros_like(dk_acc)
        dv_acc[...] = jnp.zeros_like(dv_acc)
        low = lax.broadcasted_iota(jnp.int32, (win, LANES), 1) < HEAD_DIM

        def scores(r):
            rs, off = _na_window(r, n_rows)
            rows = pl.ds(pl.multiple_of(r * GRID_W, GRID_W), GRID_W)
            wrows = pl.ds(pl.multiple_of(rs * GRID_W, GRID_W), win)
            q, do = q_ref[rows, :], do_ref[rows, :]
            k2 = _na_pair_window(km, wrows)
            s = _na_scores(q, k2, tp_ref, off)
            dp = _mm_nt(do, _na_pair_window(vm, wrows))
            return rows, wrows, off, q, do, k2, s, dp

        def finish(rows, wrows, off, q, do, k2, s, dp):
            p = _pair_probs_from_lse(s, lse_ref[rows, :])
            parts = []
            for h in range(2):
                ph, dph = p[:, h * win:(h + 1) * win], dp[:, h * win:(h + 1) * win]
                dsh = ph * (dph - jnp.sum(ph * dph, axis=-1, keepdims=True))
                for w in range(NA_ROWS // 2):
                    dtp_ref[h, 2 * w - off + (NA_ROWS - 1)] += dsh[:, w * LANES:(w + 1) * LANES]
                parts.append(dsh)
            dsb = (jnp.concatenate(parts, axis=1) * QK_SCALE).astype(BF16)
            dq_ref[rows, :] = _mm(dsb, k2).astype(BF16)
            dk_acc[wrows, :] += _pair_grad(dsb, q, low)
            dv_acc[wrows, :] += _pair_grad(p.astype(BF16), do, low)

        def row_group(i, carry):
            for state in [scores(NA_GROUP * i + j) for j in range(NA_GROUP)]:
                finish(*state)
            return carry

        lax.fori_loop(0, n_rows // NA_GROUP, row_group, 0)
        dk_ref[...] = dk_acc[...].astype(BF16)
        dv_ref[...] = dv_acc[...].astype(BF16)

    blk = lambda off: pl.BlockSpec((seq, LANES), lambda p, b: (b, off + p))
    out = jax.ShapeDtypeStruct((t, NA_WIDTH), BF16)
    return _hosted(
        body, rider, name="na_backward", grid=(n_pairs, batch),
        out_shape=[out, out, out, jax.ShapeDtypeStruct(tiles.shape, F32)],
        in_specs=[blk(0), blk(n_pairs), blk(2 * n_pairs), blk(0), blk(0),
                  pl.BlockSpec((2, n_tiles, GRID_W, LANES), lambda p, b: (p, 0, 0, 0))],
        out_specs=[blk(0), blk(0), blk(0), pl.BlockSpec((2, n_tiles, GRID_W, LANES), lambda p, b: (p, 0, 0, 0))],
        scratch_shapes=[pltpu.VMEM((2, seq, LANES), BF16), pltpu.VMEM((2, seq, LANES), BF16),
                        pltpu.VMEM((seq, LANES), F32), pltpu.VMEM((seq, LANES), F32)],
        compiler_params=_cparams(("arbitrary", "arbitrary")), args=[proj, proj, proj, d_o, lse, tiles])


def _na_bias_grad(dtiles, expand):
    n = dtiles.shape[0]

    def body(t_ref, e_ref, o_ref):
        flat = jnp.concatenate([t_ref[:, qq, :] for qq in range(GRID_W)], axis=1)
        o_ref[...] = lax.dot_general(flat, e_ref[...], (((1,), (1,)), ((), ())),
                                     precision=lax.Precision.HIGHEST, preferred_element_type=F32)

    return pl.pallas_call(
        body, name="na_bias_grad",
        out_shape=jax.ShapeDtypeStruct((n, expand.shape[0]), F32),
        compiler_params=_cparams(vmem=VMEM_BIG),
    )(dtiles, expand)


def _sw_backward(proj, d_o, lse, sink, batch, seq, rider=None):
    t = proj.shape[0]
    n_pairs = SW_WIDTH // LANES
    q_blk = 3 * NA_WIDTH // LANES
    k_blk = q_blk + n_pairs
    n_blocks = seq // SW_BLOCK
    pad = seq + 2 * SW_BLOCK

    def body(sink_ref, q_ref, k_ref, v_ref, do_ref, lse_ref, dq_ref, dk_ref, dv_ref, dsk_ref,
             k_lo, k_hi, v_lo, v_hi, dk_loc, dv_loc, dk_tot, dv_tot):
        hp = pl.program_id(1)
        g = hp // 2

        @pl.when(hp % 2 == 0)
        def _():
            _sw_prepare(k_ref, g, k_lo, k_hi, seq)
            _sw_prepare(v_ref, g, v_lo, v_hi, seq)
            dk_loc[...] = jnp.zeros_like(dk_loc)
            dv_loc[...] = jnp.zeros_like(dv_loc)

        @pl.when(hp == 0)
        def _():
            dk_tot[...] = jnp.zeros_like(dk_tot)
            dv_tot[...] = jnp.zeros_like(dv_tot)

        band = 3 * SW_BLOCK
        low = lax.broadcasted_iota(jnp.int32, (band, LANES), 1) < HEAD_DIM

        sinks = (sink_ref[2 * hp], sink_ref[2 * hp + 1])

        def scores(n):
            rows = pl.ds(pl.multiple_of(n * SW_BLOCK, SW_BLOCK), SW_BLOCK)
            wrows = pl.ds(pl.multiple_of(n * SW_BLOCK, SW_BLOCK), band)
            qb, do = q_ref[rows, :], do_ref[rows, :]
            k2 = jnp.concatenate([k_lo[wrows, :], k_hi[wrows, :]], axis=0)
            v2 = jnp.concatenate([v_lo[wrows, :], v_hi[wrows, :]], axis=0)
            return n, rows, wrows, qb, do, k2, _mm_nt(qb, k2) * QK_SCALE, _mm_nt(do, v2)

        def finish(sink_acc, n, rows, wrows, qb, do, k2, s2, dp):
            p, ps = _sw_probs_from_lse(s2, _sw_mask(n, seq), sinks, lse_ref[rows, :])
            parts, new = [], []
            for i in range(2):
                ph, dph = p[:, i * band:(i + 1) * band], dp[:, i * band:(i + 1) * band]
                delta = jnp.sum(ph * dph, axis=-1, keepdims=True)
                parts.append(ph * (dph - delta))
                new.append(sink_acc[i] - ps[i] * delta)
            dsb = (jnp.concatenate(parts, axis=1) * QK_SCALE).astype(BF16)
            dq_ref[rows, :] = _mm(dsb, k2)
            dk_loc[wrows, :] += _pair_grad(dsb, qb, low)
            dv_loc[wrows, :] += _pair_grad(p.astype(BF16), do, low)
            return tuple(new)

        def block_group(i, carry):
            for state in [scores(SW_GROUP_BLOCKS * i + j) for j in range(SW_GROUP_BLOCKS)]:
                carry = finish(carry, *state)
            return carry

        zero = jnp.zeros((SW_BLOCK, 1), F32)
        s0, s1 = lax.fori_loop(0, n_blocks // SW_GROUP_BLOCKS, block_group, (zero, zero))
        row = lax.broadcasted_iota(jnp.int32, (SUBLANES, LANES), 0)
        dsk_ref[0, 0] = jnp.where(row == 0, jnp.sum(s0), jnp.where(row == 1, jnp.sum(s1), 0.0))

        @pl.when(hp % 2 == 1)
        def _():
            lane_s = lax.broadcasted_iota(jnp.int32, (seq, LANES), 1)
            mine_g = (lane_s // HEAD_DIM) == g
            for loc, tot in ((dk_loc, dk_tot), (dv_loc, dv_tot)):
                part = loc[SW_BLOCK:SW_BLOCK + seq, :]
                tot[...] += jnp.where(mine_g, part + pltpu.roll(part, HEAD_DIM, 1), 0.0)

        @pl.when(hp == n_pairs - 1)
        def _():
            dk_ref[...] = dk_tot[...]
            dv_ref[...] = dv_tot[...].astype(BF16)

    return _hosted(
        body, rider, name="sw_backward", grid=(batch, n_pairs),
        out_shape=[jax.ShapeDtypeStruct((t, SW_WIDTH), F32), jax.ShapeDtypeStruct((t, LANES), F32),
                   jax.ShapeDtypeStruct((t, LANES), BF16), jax.ShapeDtypeStruct((batch, n_pairs, SUBLANES, LANES), F32)],
        in_specs=[pl.BlockSpec(memory_space=pltpu.SMEM),
                  pl.BlockSpec((seq, LANES), lambda b, p: (b, q_blk + p)),
                  pl.BlockSpec((seq, LANES), lambda b, p: (b, k_blk)),
                  pl.BlockSpec((seq, LANES), lambda b, p: (b, k_blk + 1)),
                  pl.BlockSpec((seq, LANES), lambda b, p: (b, p)), pl.BlockSpec((seq, LANES), lambda b, p: (b, p))],
        out_specs=[pl.BlockSpec((seq, LANES), lambda b, p: (b, p)), pl.BlockSpec((seq, LANES), lambda b, p: (b, 0)),
                   pl.BlockSpec((seq, LANES), lambda b, p: (b, 0)),
                   pl.BlockSpec((1, 1, SUBLANES, LANES), lambda b, p: (b, p, 0, 0))],
        scratch_shapes=[pltpu.VMEM((pad, LANES), BF16)] * 4 + [pltpu.VMEM((pad, LANES), F32)] * 2
        + [pltpu.VMEM((seq, LANES), F32)] * 2,
        compiler_params=_cparams(("arbitrary", "arbitrary")), args=[sink, proj, proj, proj, d_o, lse])


def _in_backward(dqkv_a, dq_b, dk_b, dv_b, w_in_t, h1, x, mod3, g_attn, dx1, cos_t, sin_t, seq):
    t, d = x.shape
    tm = TOKEN_TILE
    per_seq = seq // tm
    batch = t // seq
    dqa, dka, dva = dqkv_a
    n_q = SW_WIDTH // LANES

    def body(dqa_ref, dka_ref, dva_ref, dqb_ref, dkb_ref, dvb_ref, w_ref, h_ref, x_ref, mod_ref, g_ref, dx1_ref,
             cos_ref, sin_ref, dx_ref, gw_ref, gwb_ref, dsh_ref, dsc_ref, dg_ref):
        i = pl.program_id(0)

        @pl.when(i == 0)
        def _():
            gw_ref[...] = jnp.zeros_like(gw_ref)
            dg_ref[...] = jnp.zeros_like(dg_ref)

        @pl.when(i % per_seq == 0)
        def _():
            dsh_ref[...] = jnp.zeros_like(dsh_ref)
            dsc_ref[...] = jnp.zeros_like(dsc_ref)

        dr = jnp.concatenate([dqb_ref[...], dkb_ref[...]], axis=1)
        cos = jnp.concatenate([cos_ref[...]] * (n_q + 1), axis=1)
        sin = jnp.concatenate([sin_ref[...]] * (n_q + 1), axis=1)
        dr = dr * cos + _rope_rot(dr * sin)
        dproj = jnp.concatenate([dqa_ref[...], dka_ref[...], dva_ref[...], dr.astype(BF16), dvb_ref[...]], axis=1)
        gw_ref[...] += _mm_tn(dproj, h_ref[...])

        @pl.when(i == t // tm - 1)
        def _():
            gwb_ref[...] = gw_ref[...].astype(BF16)

        dh = _mm(dproj, w_ref[...])
        scale = mod_ref[0, :, d:2 * d]
        r, xn = _rms_stats(x_ref[...])
        xg = xn * g_ref[...]
        dxg = dh * (1.0 + scale)
        dx_ref[...] = dx1_ref[...] + _rms_bwd(dxg * g_ref[...], xn, r)
        dg_ref[...] += jnp.sum(dxg * xn, axis=0, keepdims=True)
        dsh_ref[0] += jnp.sum(dh, axis=0, keepdims=True)
        dsc_ref[0] += jnp.sum(dh * xg, axis=0, keepdims=True)

    tile = lambda w: pl.BlockSpec((tm, w), lambda i: (i, 0))
    per_b = pl.BlockSpec((1, 1, d), lambda i: (i // per_seq, 0, 0))
    small = jax.ShapeDtypeStruct((batch, 1, d), F32)
    rope = pl.BlockSpec((tm, LANES), lambda i: (i % per_seq, 0))
    return pl.pallas_call(
        body, name="in_backward", grid=(t // tm,),
        out_shape=(jax.ShapeDtypeStruct((t, d), F32), jax.ShapeDtypeStruct((IN_WIDTH, d), F32),
                   jax.ShapeDtypeStruct((IN_WIDTH, d), BF16), small, small, jax.ShapeDtypeStruct((1, d), F32)),
        in_specs=[tile(NA_WIDTH), tile(NA_WIDTH), tile(NA_WIDTH), tile(SW_WIDTH), tile(LANES), tile(LANES),
                  _resident((IN_WIDTH, d)), tile(d), tile(d),
                  pl.BlockSpec((1, 1, 6 * d), lambda i: (i // per_seq, 0, 0)),
                  pl.BlockSpec((1, d), lambda i: (0, 0)), tile(d), rope, rope],
        out_specs=(tile(d), _resident((IN_WIDTH, d)), _resident((IN_WIDTH, d)),
                   per_b, per_b, pl.BlockSpec((1, d), lambda i: (0, 0))),
        compiler_params=_cparams(("arbitrary",), VMEM_BIG),
    )(dqa, dka, dva, dq_b, dk_b, dv_b, w_in_t, h1, x, mod3, g_attn, dx1, cos_t, sin_t)


def _ada_weight_grad(sc_all, dmod_cols):
    d = sc_all.shape[1]
    ncol = dmod_cols.shape[1]

    def body(s_ref, m_ref, o_ref):
        o_ref[...] = _mm_tn(s_ref[...].astype(BF16), m_ref[...].astype(BF16))

    return pl.pallas_call(
        body, name="ada_weight_grad",
        out_shape=jax.ShapeDtypeStruct((d, ncol), F32),
        compiler_params=_cparams(vmem=VMEM_BIG),
    )(sc_all, dmod_cols)


def _row_tile(rows, cols):
    target = max(SUBLANES, (1 << 20) // (4 * cols))
    best = rows
    for cand in range(SUBLANES, rows + 1, SUBLANES):
        if rows % cand == 0 and cand <= target:
            best = cand
    return best if rows % SUBLANES == 0 else rows


def _sum_slots(parts, name):
    n = len(parts)
    _, rows, cols = parts[0][0].shape
    tr = _row_tile(rows, cols)
    per = rows // tr

    def body(*refs):
        o_ref = refs[-1]
        for q in range(n):
            @pl.when(pl.program_id(0) == q)
            def _(q=q):
                p_ref, own_ref = refs[2 * q], refs[2 * q + 1]
                o_ref[...] = ((own_ref[...] + p_ref[0].astype(F32)) + p_ref[1].astype(F32)) + p_ref[2].astype(F32)

    in_specs, args = [], []
    for q, (recv, own) in enumerate(parts):
        in_specs.append(pl.BlockSpec((N_SHARD - 1, tr, cols), lambda p, i, q=q: (0, jnp.where(p == q, i, 0), 0)))
        in_specs.append(pl.BlockSpec((tr, cols), lambda p, i, q=q: (jnp.where(p == q, i, 0), 0)))
        args += [recv, own]
    return pl.pallas_call(
        body, name=name, grid=(n, per),
        out_shape=jax.ShapeDtypeStruct((n * rows, cols), F32),
        in_specs=in_specs, out_specs=pl.BlockSpec((tr, cols), lambda p, i: (p * per + i, 0)),
        compiler_params=_cparams(("arbitrary", "arbitrary")),
    )(*args)


def _adamw_math(w, g, m, v):
    m2 = ADAM_B1 * m + (1.0 - ADAM_B1) * g
    v2 = ADAM_B2 * v + (1.0 - ADAM_B2) * (g * g)
    m_hat = m2 / (1.0 - ADAM_B1 ** ADAM_STEP)
    v_hat = v2 / (1.0 - ADAM_B2 ** ADAM_STEP)
    return -ADAM_LR * (m_hat / (jnp.sqrt(v_hat) + ADAM_EPS) + ADAM_WD * w), m2, v2


def _small_sums(partials, dmod, rider=None):
    moving = list(partials) + [dmod]
    n_mov = len(moving)

    def body(*refs):
        mov, refs = refs[:n_mov], refs[n_mov:]
        sums_out, refs = refs[:n_mov - 1], refs[n_mov - 1:]
        b_out, dmod_out, refs = refs[0], refs[1], refs[2:]
        everyone, (ssem, rsem) = refs[:n_mov], refs[n_mov:]
        x, y, c = _my_pos()
        me = 4 * x + 2 * y + c
        cps = []
        for a in range(n_mov):
            everyone[a][me] = mov[a][...]
            for k in range(1, N_DEV):
                peer = (_flip(x, (k >> 2) & 1), _flip(y, (k >> 1) & 1), _flip(c, k & 1))
                cps.append(pltpu.make_async_remote_copy(
                    src_ref=everyone[a].at[me], dst_ref=everyone[a].at[me], send_sem=ssem.at[a, k - 1],
                    recv_sem=rsem.at[a, k - 1], device_id=peer, device_id_type=MESH))
        for cp in cps:
            cp.start()
        for cp in cps:
            cp.wait_recv()

        def total(a):
            acc = everyone[a][0]
            for dev in range(1, N_DEV):
                acc = acc + everyone[a][dev]
            return acc

        for a in range(n_mov - 1):
            sums_out[a][...] = total(a)
        b_out[...] = jnp.sum(total(n_mov - 1), axis=0, keepdims=True)
        dmod_out[...] = everyone[n_mov - 1][...]
        for cp in cps:
            cp.wait_send()

    vm = pl.BlockSpec(memory_space=pltpu.VMEM)
    sds = jax.ShapeDtypeStruct
    out_shape = [sds(p.shape, F32) for p in partials]
    out_shape += [sds((1, dmod.shape[1]), F32), sds((N_DEV,) + dmod.shape, F32)]
    return _hosted(
        body, rider, name="small_sums", grid=(), out_shape=out_shape,
        in_specs=[vm] * n_mov, out_specs=[vm] * len(out_shape),
        scratch_shapes=[pltpu.VMEM((N_DEV,) + a.shape, F32) for a in moving]
        + [pltpu.SemaphoreType.DMA((n_mov, N_DEV - 1)), pltpu.SemaphoreType.DMA((n_mov, N_DEV - 1))],
        compiler_params=_cparams(vmem=VMEM_BIG), args=moving)


def _small_adamw(states, grads):
    n = len(states)

    def body(*refs):
        g_refs, wmv, res = refs[:n], refs[n:4 * n], refs[4 * n:]
        for j in range(n):
            g = g_refs[j][...]
            delta, m2, v2 = _adamw_math(wmv[3 * j][...], g, wmv[3 * j + 1][...], wmv[3 * j + 2][...])
            res[4 * j][...] = g
            res[4 * j + 1][...] = delta
            res[4 * j + 2][...] = m2
            res[4 * j + 3][...] = v2

    out_shape = []
    for w, _, _ in states:
        out_shape += [jax.ShapeDtypeStruct(w.shape, F32)] * 4
    outs = pl.pallas_call(body, name="small_adamw", out_shape=tuple(out_shape),
                          compiler_params=_cparams(vmem=VMEM_BIG))(*grads, *[a for st in states for a in st])
    return [outs[4 * j:4 * j + 4] for j in range(n)]


def _adamw(w, grads, m, v, name):
    rows, cols = w.shape
    tr = _row_tile(rows, cols)
    ng = len(grads)

    def body(*refs):
        w_ref = refs[0]
        g_refs = refs[1:1 + ng]
        m_ref, v_ref = refs[1 + ng], refs[2 + ng]
        g_out, d_out, m_out, v_out = refs[3 + ng:]
        g = g_refs[0][...]
        for extra in g_refs[1:]:
            g = g + extra[...]
        g_out[...] = g
        d_out[...], m_out[...], v_out[...] = _adamw_math(w_ref[...], g, m_ref[...], v_ref[...])

    spec = pl.BlockSpec((tr, cols), lambda i: (i, 0))
    out = jax.ShapeDtypeStruct((rows, cols), F32)
    return pl.pallas_call(
        body, name=name, grid=(rows // tr,),
        out_shape=(out, out, out, out),
        in_specs=[spec] * (3 + ng), out_specs=(spec, spec, spec, spec),
        compiler_params=_cparams(("arbitrary",)),
    )(w, *grads, m, v)


def _rope_tables(seq):
    half = HEAD_DIM // 2
    inv = np.float32(ROPE_THETA) ** (-np.arange(half, dtype=np.float32) / np.float32(half))
    ang = (np.arange(seq, dtype=np.float32)[:, None] * inv[None, :]).astype(np.float64)
    cos, sin = np.cos(ang).astype(np.float32), np.sin(ang).astype(np.float32)
    cos_t = np.concatenate([cos, cos, cos, cos], axis=1)
    sin_t = np.concatenate([-sin, sin, -sin, sin], axis=1)
    return jnp.asarray(cos_t), jnp.asarray(sin_t)


def kernel(x, c, w_ada, b_ada, g_attn, w_in, na_rpb, sw_sink, g_na_out, g_sw_out, w_out, g_ffn, w_up, conv_w, conv_b, w_down, g_final, loss_target, m_w_ada, m_b_ada, m_g_attn, m_w_in, m_na_rpb, m_sw_sink, m_g_na_out, m_g_sw_out, m_w_out, m_g_ffn, m_w_up, m_conv_w, m_conv_b, m_w_down, m_g_final, v_w_ada, v_b_ada, v_g_attn, v_w_in, v_na_rpb, v_sw_sink, v_g_na_out, v_g_sw_out, v_w_out, v_g_ffn, v_w_up, v_conv_w, v_conv_b, v_w_down, v_g_final):
    batch, seq, d = x.shape
    t = batch * seq
    assert d == D_MODEL and seq % (NA_ROWS * GRID_W) == 0 and seq % TOKEN_TILE == 0 and batch <= SUBLANES
    shard = 2 * lax.axis_index("x") + lax.axis_index("y")
    xt = x.reshape(t, d)
    tgt = loss_target.reshape(t, d)

    c8 = jnp.pad(c, ((0, SUBLANES - batch), (0, 0)))
    w_in_t_s = jnp.transpose(w_in[0]).astype(BF16)
    (mod8, sc_all), (w_in_g,) = _ada_forward(c8, w_ada[0], b_ada, _Rider("gather", [w_in_t_s]))
    mod3 = mod8[:batch].reshape(batch, 1, 6 * d)
    w_in_t = w_in_g.reshape(IN_WIDTH, d)

    cos_t, sin_t = _rope_tables(seq)
    (h1, proj), _ = _in_proj(xt, mod3, g_attn, w_in_t, cos_t, sin_t, seq)
    n_heads = NA_WIDTH // HEAD_DIM
    n_tiles, n_dc = 2 * NA_ROWS - 2, 2 * NA_COLS - 1
    expand, neg_mask = _na_bias_pattern()
    rpb = na_rpb[0]
    rows2 = jnp.concatenate([rpb[:, :-1, :], rpb[:, 1:, :]], axis=2).reshape(n_heads * n_tiles, 2 * n_dc)
    rows2 = jnp.pad(rows2, ((0, 0), (0, GRID_W - 2 * n_dc)))
    tiles = _na_bias_tiles(rows2, expand, neg_mask).reshape(n_heads, n_tiles, GRID_W, LANES)
    sink = sw_sink[0]
    w_up_b16 = w_up[0].astype(BF16)
    (oa, lse_a), (w_up_a,) = _na_forward(proj, tiles, batch, seq, _Rider("gather", [w_up_b16[:d // 2]]))
    (ob, lse_b), (w_up_b, conv_w_g, w_out_g) = _sw_forward(
        proj, sink, batch, seq, _Rider("gather", [w_up_b16[d // 2:], conv_w[0], w_out[0].astype(BF16)]))
    w_up_f = (w_up_a, w_up_b)
    w_out_f = w_out_g.reshape(d, d)
    conv_w_f = jnp.transpose(conv_w_g, (1, 0, 2)).reshape(3, D_FF)
    oab, mix, x1, h2 = _out_proj(oa, ob, g_na_out, g_sw_out, w_out_f, xt, mod3, g_ffn, seq)
    (u,), _ = _up_proj(h2, w_up_f)
    (a,), (w_down_g,) = _conv_gate(u, conv_w_f, conv_b, batch, seq, _Rider("gather", [w_down[0].astype(BF16)]))
    w_down_f = w_down_g.reshape(D_FF, d)
    dx2, dffn, loss_part, dgate_f, dg_final = _down_and_loss(a, w_down_f, x1, mod3, g_final.reshape(1, d), tgt, seq)

    gw_down, gw_down_b = _down_weight_grad(a, dffn)
    blocks = lambda g, rows: g.reshape(N_SHARD, rows // N_SHARD, d)
    (du, gconv_w, gconv_b), (recv_down, own_down) = _ffn_backward(
        dffn, w_down_f, u, conv_w_f, conv_b, batch, seq,
        _Rider("scatter", [blocks(gw_down_b, D_FF)], [blocks(gw_down, D_FF)]))
    (gw_up_top, gw_up_bot, gw_up_top_b, gw_up_bot_b), _ = _up_weight_grad(h2, du)
    (dx1, dmix, dshift_f, dscale_f, dgate_a, dg_ffn), _ = _up_backward(du, w_up_f, x1, mod3, g_ffn, dx2, mix, seq)
    doa, dob, gw_out, gw_out_b, dg_na, dg_sw = _out_backward(dmix, w_out_f, oab, oa, ob, g_na_out, g_sw_out)
    (dqa, dka, dva, dtiles), (recv_up_bot, own_up_bot) = _na_backward(
        proj, doa, lse_a, tiles, batch, seq, _Rider("scatter", [gw_up_bot_b], [gw_up_bot]))
    (dq_b, dk_b, dv_b, dsink_parts), (recv_out, recv_up_top, own_out, own_up_top) = _sw_backward(
        proj, dob, lse_b, sink, batch, seq,
        _Rider("scatter", [blocks(gw_out_b, d), gw_up_top_b], [blocks(gw_out, d), gw_up_top]))
    gx, gw_in_t, gw_in_b, dshift_a, dscale_a, dg_attn = _in_backward(
        (dqa, dka, dva), dq_b, dk_b, dv_b, w_in_t, h1, xt, mod3, g_attn, dx1, cos_t, sin_t, seq)

    red = _na_bias_grad(dtiles.reshape(n_heads * n_tiles, GRID_W, LANES), expand)[:, :2 * n_dc]
    red = red.reshape(n_heads, n_tiles, 2, n_dc)
    zero_row = jnp.zeros((n_heads, 1, n_dc), F32)
    g_rpb = (jnp.concatenate([red[:, :, 0, :], zero_row], axis=1)
             + jnp.concatenate([zero_row, red[:, :, 1, :]], axis=1))
    g_sink = jnp.sum(dsink_parts[:, :, :2, 0], axis=0).reshape(SW_WIDTH // HEAD_DIM)

    dmod = jnp.concatenate([dshift_a, dscale_a, dgate_a, dshift_f, dscale_f, dgate_f], axis=2).reshape(batch, 6 * d)
    rpb_shape = na_rpb.shape[1:]
    states = [(g_attn, m_g_attn, v_g_attn),
              (na_rpb.reshape(rpb_shape), m_na_rpb.reshape(rpb_shape), v_na_rpb.reshape(rpb_shape)),
              (sw_sink, m_sw_sink, v_sw_sink), (g_na_out, m_g_na_out, v_g_na_out), (g_sw_out, m_g_sw_out, v_g_sw_out),
              (g_ffn, m_g_ffn, v_g_ffn), (conv_b, m_conv_b, v_conv_b),
              (g_final.reshape(1, d), m_g_final.reshape(1, d), v_g_final.reshape(1, d))]
    partials = [dg_attn, g_rpb, g_sink.reshape(sw_sink.shape), dg_na, dg_sw, dg_ffn, gconv_b, dg_final,
                gconv_w, loss_part]
    mine = [None, _sum_slots([(recv_out, own_out)], "sum_w_out"),
            _sum_slots([(recv_up_top, own_up_top), (recv_up_bot, own_up_bot)], "sum_w_up"),
            _sum_slots([(recv_down, own_down)], "sum_w_down")]
    small, (recv_in, own_in, *theirs) = _small_sums(
        partials, dmod, _Riders([_Rider("scatter", [blocks(gw_in_b, IN_WIDTH)], [blocks(gw_in_t, IN_WIDTH)]),
                                 _Rider("swap", mine[1:])]))
    g_conv_w_full, loss_sum, g_b_ada, dmod_all = small[len(states):]
    r_small = _small_adamw(states + [(b_ada, m_b_ada, v_b_ada)], small[:len(states)] + [g_b_ada])
    loss = loss_sum[0, 0]
    mine[0] = _sum_slots([(recv_in, own_in)], "sum_w_in")
    theirs = _ride_alone(_Rider("swap", mine[:1]), "swap_sibling") + theirs
    dmod_rows = jnp.pad(dmod_all, ((0, 0), (0, SUBLANES - batch), (0, 0))).reshape(N_DEV * SUBLANES, 6 * d)
    ncol = w_ada.shape[2]
    g_w_ada = _ada_weight_grad(sc_all, lax.dynamic_slice(dmod_rows, (0, shard * ncol), (N_DEV * SUBLANES, ncol)))
    cshard = conv_w.shape[2]
    g_conv_w = lax.dynamic_slice(g_conv_w_full, (0, shard * cshard), (3, cshard))

    def big(w, m, v, g_parts, name):
        shape = w.shape
        outs = _adamw(w[0], g_parts, m[0], v[0], name)
        return [o.reshape(shape) for o in outs]

    r_w_ada = big(w_ada, m_w_ada, v_w_ada, [g_w_ada], "adamw_w_ada")
    r_w_in = [jnp.transpose(o).reshape(w_in.shape) for o in
              _adamw(jnp.transpose(w_in[0]), [mine[0], theirs[0]], jnp.transpose(m_w_in[0]), jnp.transpose(v_w_in[0]),
                     "adamw_w_in")]
    r_w_out = big(w_out, m_w_out, v_w_out, [mine[1], theirs[1]], "adamw_w_out")
    r_w_up = big(w_up, m_w_up, v_w_up, [mine[2], theirs[2]], "adamw_w_up")
    r_w_down = big(w_down, m_w_down, v_w_down, [mine[3], theirs[3]], "adamw_w_down")

    r_conv_w = big(conv_w, m_conv_w, v_conv_w, [g_conv_w], "adamw_conv_w")

    def pick(k):
        ga_, rpb_, sk_, gna_, gsw_, gf_, cb_, gfin_, b_ = [r[k] for r in r_small]
        return [r_w_ada[k], b_, ga_, r_w_in[k], rpb_.reshape(na_rpb.shape), sk_, gna_, gsw_, r_w_out[k], gf_,
                r_w_up[k], r_conv_w[k], cb_, r_w_down[k], gfin_.reshape(d)]

    return (loss, gx.reshape(batch, seq, d), *pick(0), *pick(1), *pick(2), *pick(3))
```

```python
import jax
import jax.numpy as jnp
import numpy as np
from jax import lax
from jax.experimental import pallas as pl
from jax.experimental.pallas import tpu as pltpu

F32 = jnp.float32
BF16 = jnp.bfloat16
MESH = pl.DeviceIdType.MESH

D_MODEL = 1024
HEAD_DIM = 64
NA_WIDTH = 512
SW_WIDTH = 512
SW_KV_WIDTH = 128
IN_WIDTH = 2304
D_FF = 2816
GRID_W = 64
NA_ROWS = 8
NA_COLS = 16
SW_BLOCK = 128
ROPE_THETA = 10000.0
EPS = 1e-6
NEG = -1e30
QK_SCALE = HEAD_DIM ** -0.5

ADAM_LR = 0.001
ADAM_B1 = 0.9
ADAM_B2 = 0.999
ADAM_EPS = 1e-08
ADAM_WD = 0.01
ADAM_STEP = 10

N_SHARD = 4
N_DEV = 8
LANES = 128
SUBLANES = 8
TOKEN_TILE = 512
FF_TILE = 256
CONV_CHUNK = 512
NA_GROUP = 8
SW_GROUP_BLOCKS = 8
VMEM_BIG = 56 * 1024 * 1024


def _mm(a, b):
    return jnp.dot(a, b, preferred_element_type=F32)


def _mm_nt(a, b):
    return lax.dot_general(a, b, (((1,), (1,)), ((), ())), preferred_element_type=F32)


def _mm_tn(a, b):
    return lax.dot_general(a, b, (((0,), (0,)), ((), ())), preferred_element_type=F32)


def _cparams(sem=None, vmem=None):
    kw = {}
    if sem is not None:
        kw["dimension_semantics"] = sem
    if vmem is not None:
        kw["vmem_limit_bytes"] = vmem
    return pltpu.CompilerParams(**kw)


def _resident(shape):
    return pl.BlockSpec(shape, lambda i: (0,) * len(shape), pipeline_mode=pl.Buffered(1))


def _sigmoid(x):
    return 1.0 / (1.0 + jnp.exp(-x))


def _rms_stats(x):
    r = lax.rsqrt(jnp.mean(x * x, axis=-1, keepdims=True) + EPS)
    return r, x * r


def _rms_bwd(dxn, xn, r):
    return r * (dxn - xn * jnp.mean(dxn * xn, axis=-1, keepdims=True))


def _my_pos():
    return lax.axis_index("x"), lax.axis_index("y"), lax.axis_index("c")


def _flip(v, bit):
    return 1 - v if bit else v


def _ada_forward(c8, w_ada, b_ada, rider):
    d = c8.shape[1]
    ncol = w_ada.shape[1]

    def body(c_ref, w_ref, b_ref, mod_ref, sc_ref, m_scr, mod_buf, ssem, rsem, ssem2, rsem2):
        x, y, c = _my_pos()
        me = 4 * x + 2 * y + c
        shard = 2 * x + y
        cv = c_ref[...]
        my_rows = pl.ds(pl.multiple_of(me * SUBLANES, SUBLANES), SUBLANES)
        sc_ref[my_rows, :] = cv * _sigmoid(cv)

        def copy1(k):
            peer = (_flip(x, (k >> 2) & 1), _flip(y, (k >> 1) & 1), _flip(c, k & 1))
            return pltpu.make_async_remote_copy(
                src_ref=sc_ref.at[my_rows, :], dst_ref=sc_ref.at[my_rows, :],
                send_sem=ssem.at[k - 1], recv_sem=rsem.at[k - 1], device_id=peer, device_id_type=MESH)

        sends = [copy1(k) for k in range(1, N_DEV)]
        for cp in sends:
            cp.start()
        for cp in sends:
            cp.wait_recv()
        m_scr[...] = _mm(sc_ref[...].astype(BF16), w_ref[...].astype(BF16))

        def copy2(k):
            px, py = _flip(x, (k >> 1) & 1), _flip(y, k & 1)
            rows = pl.ds(pl.multiple_of((4 * px + 2 * py + c) * SUBLANES, SUBLANES), SUBLANES)
            return pltpu.make_async_remote_copy(
                src_ref=m_scr.at[rows, :], dst_ref=mod_buf.at[shard],
                send_sem=ssem2.at[k - 1], recv_sem=rsem2.at[k - 1], device_id=(px, py, c), device_id_type=MESH)

        sends2 = [copy2(k) for k in range(1, N_SHARD)]
        for cp in sends2:
            cp.start()
        mod_buf[shard] = m_scr[my_rows, :]
        for cp in sends2:
            cp.wait_recv()
        for s in range(N_SHARD):
            mod_ref[:, s * ncol:(s + 1) * ncol] = mod_buf[s] + b_ref[:, s * ncol:(s + 1) * ncol]
        for cp in sends + sends2:
            cp.wait_send()

    vm = pl.BlockSpec(memory_space=pltpu.VMEM)
    return _hosted(
        body, rider, name="ada_forward", grid=(),
        out_shape=(jax.ShapeDtypeStruct((SUBLANES, N_SHARD * ncol), F32),
                   jax.ShapeDtypeStruct((N_DEV * SUBLANES, d), F32)),
        in_specs=[vm, vm, vm], out_specs=(vm, vm),
        scratch_shapes=[pltpu.VMEM((N_DEV * SUBLANES, ncol), F32), pltpu.VMEM((N_SHARD, SUBLANES, ncol), F32),
                        pltpu.SemaphoreType.DMA((N_DEV - 1,)), pltpu.SemaphoreType.DMA((N_DEV - 1,)),
                        pltpu.SemaphoreType.DMA((N_SHARD - 1,)), pltpu.SemaphoreType.DMA((N_SHARD - 1,))],
        compiler_params=_cparams(vmem=VMEM_BIG), args=[c8, w_ada, b_ada])


class _Rider:
    def __init__(self, kind, srcs, owns=()):
        self.kind, self.srcs, self.owns = kind, list(srcs), list(owns)
        n = len(self.srcs)
        sds = jax.ShapeDtypeStruct
        dma = pltpu.SemaphoreType.DMA
        if kind == "gather":
            self.out_shapes = [sds((N_SHARD,) + s.shape, s.dtype) for s in self.srcs]
            self.sems = [dma((n, N_SHARD - 1)), dma((n, N_SHARD - 1)), dma((n, N_SHARD - 1)), dma((n, N_SHARD - 1)),
                         dma((n,)), dma((n,))]
        elif kind == "scatter":
            self.out_shapes = ([sds((N_SHARD - 1,) + s.shape[1:], s.dtype) for s in self.srcs]
                               + [sds(o.shape[1:], o.dtype) for o in self.owns])
            m = max(len(self.owns), 1)
            self.sems = [dma((n, N_SHARD - 1)), dma((n, N_SHARD - 1)), dma((m,)), dma((m,))]
        else:
            self.out_shapes = [sds(s.shape, s.dtype) for s in self.srcs]
            self.sems = [dma((n,)), dma((n,))]

    @property
    def inputs(self):
        return self.srcs + self.owns

    def _halved(self, i):
        a = self.srcs[i]
        tile_rows = SUBLANES * (4 // jnp.dtype(a.dtype).itemsize)
        return self.kind == "gather" and a.shape[0] % (2 * tile_rows) == 0

    def copies(self, ins, outs, sems):
        n = len(self.srcs)
        x, y, c = _my_pos()
        shard = 2 * x + y
        remote, relay = [], []
        if self.kind == "swap":
            ssem, rsem = sems
            for i in range(n):
                remote.append(pltpu.make_async_remote_copy(
                    src_ref=ins[i], dst_ref=outs[i], send_sem=ssem.at[i], recv_sem=rsem.at[i],
                    device_id=(x, y, 1 - c), device_id_type=MESH))
            return remote, relay
        if self.kind == "gather":
            ssem, rsem, ssem2, rsem2, sib_s, sib_r = sems
        else:
            ssem, rsem, sib_s, sib_r = sems
        for i in range(n):
            if self.kind == "gather":
                remote.append(pltpu.make_async_remote_copy(
                    src_ref=ins[i], dst_ref=outs[i].at[shard], send_sem=sib_s.at[i], recv_sem=sib_r.at[i],
                    device_id=(x, y, 1 - c), device_id_type=MESH))
                half = ins[i].shape[0] // 2
                mine = pl.ds(pl.multiple_of(c * half, half), half) if self._halved(i) else None
            for k in range(1, N_SHARD):
                px, py = _flip(x, (k >> 1) & 1), _flip(y, k & 1)
                if self.kind == "gather":
                    src, dst = ins[i], outs[i].at[shard]
                    if mine is not None:
                        src, dst = src.at[mine], dst.at[mine]
                        got = outs[i].at[2 * px + py].at[mine]
                        relay.append(pltpu.make_async_remote_copy(
                            src_ref=got, dst_ref=got, send_sem=ssem2.at[i, k - 1], recv_sem=rsem2.at[i, k - 1],
                            device_id=(x, y, 1 - c), device_id_type=MESH))
                else:
                    src, dst = ins[i].at[2 * px + py], outs[i].at[k - 1]
                remote.append(pltpu.make_async_remote_copy(
                    src_ref=src, dst_ref=dst, send_sem=ssem.at[i, k - 1], recv_sem=rsem.at[i, k - 1],
                    device_id=(px, py, c), device_id_type=MESH))
        if self.kind == "scatter":
            for i in range(len(self.owns)):
                remote.append(pltpu.make_async_remote_copy(
                    src_ref=ins[n + i].at[shard], dst_ref=outs[n + i], send_sem=sib_s.at[i], recv_sem=sib_r.at[i],
                    device_id=(x, y, 1 - c), device_id_type=MESH))
        return remote, relay

    def start(self, ins, outs, sems):
        remote, _ = self.copies(ins, outs, sems)
        for cp in remote:
            cp.start()

    def wait(self, ins, outs, sems):
        remote, relay = self.copies(ins, outs, sems)
        for cp in remote:
            cp.wait_recv()
        for cp in relay:
            cp.start()
        for cp in relay:
            cp.wait_recv()
        for cp in remote + relay:
            cp.wait_send()


class _Riders:
    def __init__(self, riders):
        self.riders = list(riders)
        self.inputs = [a for r in self.riders for a in r.inputs]
        self.out_shapes = [s for r in self.riders for s in r.out_shapes]
        self.sems = [s for r in self.riders for s in r.sems]

    def _split(self, ins, outs, sems):
        for r in self.riders:
            ni, no, ns = len(r.inputs), len(r.out_shapes), len(r.sems)
            yield r, ins[:ni], outs[:no], sems[:ns]
            ins, outs, sems = ins[ni:], outs[no:], sems[ns:]

    def start(self, ins, outs, sems):
        for r, i, o, s in self._split(ins, outs, sems):
            r.start(i, o, s)

    def wait(self, ins, outs, sems):
        for r, i, o, s in self._split(ins, outs, sems):
            r.wait(i, o, s)


def _hosted(body, rider, *, name, grid, out_shape, in_specs, out_specs, scratch_shapes, compiler_params, args):
    out_shape, out_specs = list(out_shape), list(out_specs)
    if rider is None:
        outs = pl.pallas_call(body, name=name, grid=grid, out_shape=tuple(out_shape), in_specs=list(in_specs),
                              out_specs=tuple(out_specs), scratch_shapes=list(scratch_shapes),
                              compiler_params=compiler_params)(*args)
        return list(outs), []
    n_in, n_out, n_scr = len(in_specs), len(out_shape), len(scratch_shapes)
    nr_in, nr_out = len(rider.inputs), len(rider.out_shapes)
    n_steps = 1
    for size in grid:
        n_steps *= size

    def full(*refs):
        ins, refs = refs[:n_in], refs[n_in:]
        r_in, refs = refs[:nr_in], refs[nr_in:]
        outs, refs = refs[:n_out], refs[n_out:]
        r_out, refs = refs[:nr_out], refs[nr_out:]
        scr, sems = refs[:n_scr], refs[n_scr:]
        if grid:
            step = 0
            for ax, size in enumerate(grid):
                step = step * size + pl.program_id(ax)
            pl.when(step == 0)(lambda: rider.start(r_in, r_out, sems))
            body(*ins, *outs, *scr)
            pl.when(step == n_steps - 1)(lambda: rider.wait(r_in, r_out, sems))
        else:
            rider.start(r_in, r_out, sems)
            body(*ins, *outs, *scr)
            rider.wait(r_in, r_out, sems)

    hbm = pl.BlockSpec(memory_space=pl.ANY)
    res = pl.pallas_call(
        full, name=name, grid=grid, out_shape=tuple(out_shape + rider.out_shapes),
        in_specs=list(in_specs) + [hbm] * nr_in, out_specs=tuple(out_specs + [hbm] * nr_out),
        scratch_shapes=list(scratch_shapes) + rider.sems, compiler_params=compiler_params,
    )(*args, *rider.inputs)
    return list(res[:n_out]), list(res[n_out:])


def _ride_alone(rider, name):
    return _hosted(lambda: None, rider, name=name, grid=(), out_shape=[], in_specs=[], out_specs=[], scratch_shapes=[],
                   compiler_params=_cparams(), args=[])[1]


def _rope_rot(t):
    w = t.shape[1]
    lane = lax.broadcasted_iota(jnp.int32, t.shape, 1)
    first = (lane % HEAD_DIM) < (HEAD_DIM // 2)
    return jnp.where(first, pltpu.roll(t, w - HEAD_DIM // 2, 1), pltpu.roll(t, HEAD_DIM // 2, 1))


def _in_proj(x, mod3, g_attn, w_in_t, cos_t, sin_t, seq, rider=None):
    t, d = x.shape
    tm = 2 * TOKEN_TILE
    per_seq = seq // tm
    rope_lo, rope_hi = 3 * NA_WIDTH, 3 * NA_WIDTH + SW_WIDTH + SW_KV_WIDTH
    n_rep = (rope_hi - rope_lo) // LANES

    def body(x_ref, mod_ref, g_ref, w_ref, cos_ref, sin_ref, h_ref, p_ref):
        r, xn = _rms_stats(x_ref[...])
        shift, scale = mod_ref[0, :, 0:d], mod_ref[0, :, d:2 * d]
        hb = ((xn * g_ref[...]) * (1.0 + scale) + shift).astype(BF16)
        h_ref[...] = hb
        p_ref[:, :rope_lo] = _mm_nt(hb, w_ref[:rope_lo, :]).astype(BF16)
        pr = _mm_nt(hb, w_ref[rope_lo:rope_hi, :])
        cos = jnp.concatenate([cos_ref[...]] * n_rep, axis=1)
        sin = jnp.concatenate([sin_ref[...]] * n_rep, axis=1)
        p_ref[:, rope_lo:rope_hi] = (pr * cos + _rope_rot(pr) * sin).astype(BF16)
        p_ref[:, rope_hi:] = _mm_nt(hb, w_ref[rope_hi:, :]).astype(BF16)

    return _hosted(
        body, rider, name="in_proj", grid=(t // tm,),
        out_shape=[jax.ShapeDtypeStruct((t, d), BF16), jax.ShapeDtypeStruct((t, IN_WIDTH), BF16)],
        in_specs=[pl.BlockSpec((tm, d), lambda i: (i, 0)),
                  pl.BlockSpec((1, 1, 6 * d), lambda i: (i // per_seq, 0, 0)),
                  pl.BlockSpec((1, d), lambda i: (0, 0)),
                  pl.BlockSpec((IN_WIDTH, d), lambda i: (0, 0)),
                  pl.BlockSpec((tm, LANES), lambda i: (i % per_seq, 0)),
                  pl.BlockSpec((tm, LANES), lambda i: (i % per_seq, 0))],
        out_specs=[pl.BlockSpec((tm, d), lambda i: (i, 0)), pl.BlockSpec((tm, IN_WIDTH), lambda i: (i, 0))],
        scratch_shapes=[], compiler_params=_cparams(("arbitrary",), VMEM_BIG),
        args=[x, mod3, g_attn, w_in_t, cos_t, sin_t])


def _na_bias_pattern():
    n_dc = 2 * NA_COLS - 1
    j = np.arange(GRID_W)[:, None]
    m = np.arange(GRID_W * LANES)[None, :]
    q, lane = m // LANES, m % LANES
    k = lane % GRID_W
    cs = np.clip(q - NA_COLS // 2, 0, GRID_W - NA_COLS)
    ok = (k >= cs) & (k < cs + NA_COLS)
    hit = ok & (j < 2 * n_dc) & (lane // GRID_W == j // n_dc) & (k - q + (NA_COLS - 1) == j % n_dc)
    return jnp.asarray(hit.astype(np.float32)), jnp.asarray(np.where(ok, 0.0, NEG).astype(np.float32))


def _na_bias_tiles(rows2, expand, mask):
    n, width = rows2.shape[0], expand.shape[1]
    q_step = 16
    step = q_step * LANES

    def body(r_ref, e_ref, m_ref, o_ref):
        flat = jnp.dot(r_ref[...], e_ref[...], precision=lax.Precision.HIGHEST,
                       preferred_element_type=F32) + m_ref[...]
        for qq in range(q_step):
            o_ref[:, qq, :] = flat[:, qq * LANES:(qq + 1) * LANES]

    return pl.pallas_call(
        body, name="na_bias_tiles", grid=(width // step,),
        out_shape=jax.ShapeDtypeStruct((n, GRID_W, LANES), F32),
        in_specs=[pl.BlockSpec(rows2.shape, lambda i: (0, 0)), pl.BlockSpec((expand.shape[0], step), lambda i: (0, i)),
                  pl.BlockSpec((1, step), lambda i: (0, i))],
        out_specs=pl.BlockSpec((n, q_step, LANES), lambda i: (0, i, 0)),
        compiler_params=_cparams(("arbitrary",)),
    )(rows2, expand, mask)


def _na_prepare(k_ref, v_ref, km, vm):
    lane = lax.broadcasted_iota(jnp.int32, k_ref.shape, 1)
    low = lane < HEAD_DIM
    kv = k_ref[...]
    vv = v_ref[...]
    zero = jnp.zeros_like(kv)
    km[0] = jnp.where(low, kv, zero)
    km[1] = jnp.where(low, zero, kv)
    vm[0] = jnp.where(low, vv, zero)
    vm[1] = jnp.where(low, zero, vv)


def _na_window(r, n_rows):
    rs = jnp.clip(r - NA_ROWS // 2, 0, n_rows - NA_ROWS)
    return rs, r - rs


def _na_pair_window(ref, wrows):
    return jnp.concatenate([ref[0, wrows, :], ref[1, wrows, :]], axis=0)


def _na_scores(q, k2, tp_ref, off):
    bias = jnp.concatenate([tp_ref[h, 2 * w - off + (NA_ROWS - 1)] for h in range(2) for w in range(NA_ROWS // 2)],
                           axis=1)
    return _mm_nt(q, k2) * QK_SCALE + bias


def _pair_lse_block(lse):
    lane = lax.broadcasted_iota(jnp.int32, (lse[0].shape[0], LANES), 1)
    return jnp.where(lane < HEAD_DIM, lse[0], lse[1])


def _pair_softmax(s):
    win = s.shape[1] // 2
    halves, lse = [], []
    for h in range(2):
        sh = s[:, h * win:(h + 1) * win]
        m = jnp.max(sh, axis=-1, keepdims=True)
        e = jnp.exp(sh - m)
        l = jnp.sum(e, axis=-1, keepdims=True)
        halves.append(e / l)
        lse.append(m + jnp.log(l))
    return jnp.concatenate(halves, axis=1), _pair_lse_block(lse)


def _pair_grad(w2, x, low):
    keys = w2.shape[1] // 2
    zero = jnp.zeros_like(x)
    low_x = low[:x.shape[0]]
    stacked = jnp.concatenate([w2[:, :keys], w2[:, keys:]], axis=0)
    diag = jnp.concatenate([jnp.where(low_x, x, zero), jnp.where(low_x, zero, x)], axis=0)
    return _mm_tn(stacked, diag)


def _pair_probs_from_lse(s, lse_block):
    win = s.shape[1] // 2
    return jnp.concatenate([jnp.exp(s[:, h * win:(h + 1) * win] - lse_block[:, h * HEAD_DIM:h * HEAD_DIM + 1])
                            for h in range(2)], axis=1)


def _na_forward(proj, tiles, batch, seq, rider=None):
    t = proj.shape[0]
    n_rows = seq // GRID_W
    n_pairs = NA_WIDTH // LANES
    win = NA_ROWS * GRID_W

    def body(q_ref, k_ref, v_ref, tp_ref, o_ref, lse_ref, km, vm):
        _na_prepare(k_ref, v_ref, km, vm)

        def scores(r):
            rs, off = _na_window(r, n_rows)
            rows = pl.ds(pl.multiple_of(r * GRID_W, GRID_W), GRID_W)
            wrows = pl.ds(pl.multiple_of(rs * GRID_W, GRID_W), win)
            return rows, wrows, _na_scores(q_ref[rows, :], _na_pair_window(km, wrows), tp_ref, off)

        def finish(rows, wrows, s):
            p, lse = _pair_softmax(s)
            lse_ref[rows, :] = lse
            o_ref[rows, :] = _mm(p.astype(BF16), _na_pair_window(vm, wrows))

        def row_group(i, carry):
            for state in [scores(NA_GROUP * i + j) for j in range(NA_GROUP)]:
                finish(*state)
            return carry

        lax.fori_loop(0, n_rows // NA_GROUP, row_group, 0)

    return _hosted(
        body, rider, name="na_forward", grid=(batch, n_pairs),
        out_shape=[jax.ShapeDtypeStruct((t, NA_WIDTH), F32), jax.ShapeDtypeStruct((t, NA_WIDTH), F32)],
        in_specs=[pl.BlockSpec((seq, LANES), lambda b, p: (b, p)),
                  pl.BlockSpec((seq, LANES), lambda b, p: (b, n_pairs + p)),
                  pl.BlockSpec((seq, LANES), lambda b, p: (b, 2 * n_pairs + p)),
                  pl.BlockSpec((2, 2 * NA_ROWS - 2, GRID_W, LANES), lambda b, p: (p, 0, 0, 0))],
        out_specs=[pl.BlockSpec((seq, LANES), lambda b, p: (b, p)), pl.BlockSpec((seq, LANES), lambda b, p: (b, p))],
        scratch_shapes=[pltpu.VMEM((2, seq, LANES), BF16), pltpu.VMEM((2, seq, LANES), BF16)],
        compiler_params=_cparams(("arbitrary", "arbitrary")), args=[proj, proj, proj, tiles])


def _sw_prepare(kv_ref, g, dst_lo, dst_hi, seq):
    lane = lax.broadcasted_iota(jnp.int32, kv_ref.shape, 1)
    mine = (lane // HEAD_DIM) == g
    kg = jnp.where(mine, kv_ref[...].astype(F32), 0.0)
    kr = pltpu.roll(kg, HEAD_DIM, 1)
    first = g == 0
    zero = jnp.zeros((SW_BLOCK, LANES), BF16)
    for dst, val in ((dst_lo, jnp.where(first, kg, kr)), (dst_hi, jnp.where(first, kr, kg))):
        dst[0:SW_BLOCK, :] = zero
        dst[SW_BLOCK:SW_BLOCK + seq, :] = val.astype(BF16)
        dst[SW_BLOCK + seq:, :] = zero


def _sw_mask(n, seq):
    qi = lax.broadcasted_iota(jnp.int32, (SW_BLOCK, 3 * SW_BLOCK), 0)
    kj = lax.broadcasted_iota(jnp.int32, (SW_BLOCK, 3 * SW_BLOCK), 1)
    kpos = n * SW_BLOCK - SW_BLOCK + kj
    return (jnp.abs(qi + SW_BLOCK - kj) <= SW_BLOCK) & (kpos >= 0) & (kpos < seq)


def _sw_probs(s2, ok, sinks):
    band = s2.shape[1] // 2
    halves, lse = [], []
    for i in range(2):
        s = jnp.where(ok, s2[:, i * band:(i + 1) * band], NEG)
        m = jnp.maximum(jnp.max(s, axis=-1, keepdims=True), sinks[i])
        p = jnp.exp(s - m)
        den = jnp.sum(p, axis=-1, keepdims=True) + jnp.exp(sinks[i] - m)
        halves.append(p / den)
        lse.append(m + jnp.log(den))
    return jnp.concatenate(halves, axis=1), _pair_lse_block(lse)


def _sw_probs_from_lse(s2, ok, sinks, lse_block):
    band = s2.shape[1] // 2
    halves, sink_p = [], []
    for i in range(2):
        lse = lse_block[:, i * HEAD_DIM:i * HEAD_DIM + 1]
        halves.append(jnp.exp(jnp.where(ok, s2[:, i * band:(i + 1) * band], NEG) - lse))
        sink_p.append(jnp.exp(sinks[i] - lse))
    return jnp.concatenate(halves, axis=1), sink_p


def _sw_forward(proj, sink, batch, seq, rider=None):
    t = proj.shape[0]
    n_pairs = SW_WIDTH // LANES
    q_blk = 3 * NA_WIDTH // LANES
    k_blk = q_blk + n_pairs
    n_blocks = seq // SW_BLOCK
    pad = seq + 2 * SW_BLOCK

    def body(sink_ref, q_ref, k_ref, v_ref, o_ref, lse_ref, k_lo, k_hi, v_lo, v_hi):
        hp = pl.program_id(1)
        g = hp // 2

        @pl.when(hp % 2 == 0)
        def _():
            _sw_prepare(k_ref, g, k_lo, k_hi, seq)
            _sw_prepare(v_ref, g, v_lo, v_hi, seq)

        sinks = (sink_ref[2 * hp], sink_ref[2 * hp + 1])

        def scores(n):
            rows = pl.ds(pl.multiple_of(n * SW_BLOCK, SW_BLOCK), SW_BLOCK)
            wrows = pl.ds(pl.multiple_of(n * SW_BLOCK, SW_BLOCK), 3 * SW_BLOCK)
            k2 = jnp.concatenate([k_lo[wrows, :], k_hi[wrows, :]], axis=0)
            return n, rows, wrows, _mm_nt(q_ref[rows, :], k2) * QK_SCALE

        def finish(n, rows, wrows, s2):
            p, lse = _sw_probs(s2, _sw_mask(n, seq), sinks)
            lse_ref[rows, :] = lse
            v2 = jnp.concatenate([v_lo[wrows, :], v_hi[wrows, :]], axis=0)
            o_ref[rows, :] = _mm(p.astype(BF16), v2)

        def block_group(i, carry):
            for state in [scores(SW_GROUP_BLOCKS * i + j) for j in range(SW_GROUP_BLOCKS)]:
                finish(*state)
            return carry

        lax.fori_loop(0, n_blocks // SW_GROUP_BLOCKS, block_group, 0)

    return _hosted(
        body, rider, name="sw_forward", grid=(batch, n_pairs),
        out_shape=[jax.ShapeDtypeStruct((t, SW_WIDTH), F32), jax.ShapeDtypeStruct((t, SW_WIDTH), F32)],
        in_specs=[pl.BlockSpec(memory_space=pltpu.SMEM),
                  pl.BlockSpec((seq, LANES), lambda b, p: (b, q_blk + p)),
                  pl.BlockSpec((seq, LANES), lambda b, p: (b, k_blk)),
                  pl.BlockSpec((seq, LANES), lambda b, p: (b, k_blk + 1))],
        out_specs=[pl.BlockSpec((seq, LANES), lambda b, p: (b, p)), pl.BlockSpec((seq, LANES), lambda b, p: (b, p))],
        scratch_shapes=[pltpu.VMEM((pad, LANES), BF16)] * 4,
        compiler_params=_cparams(("arbitrary", "arbitrary")), args=[sink, proj, proj, proj])


def _out_proj(oa, ob, g_na, g_sw, w_out, x, mod3, g_ffn, seq):
    t, d = x.shape
    tm = TOKEN_TILE
    per_seq = seq // tm

    def body(oa_ref, ob_ref, gna_ref, gsw_ref, w_ref, x_ref, mod_ref, gf_ref, oab_ref, mix_ref, x1_ref, h2_ref):
        _, na = _rms_stats(oa_ref[...])
        _, nb = _rms_stats(ob_ref[...])
        oab = jnp.concatenate([na * gna_ref[...], nb * gsw_ref[...]], axis=1).astype(BF16)
        oab_ref[...] = oab
        mix = _mm(oab, w_ref[...])
        mix_ref[...] = mix
        gate_a = mod_ref[0, :, 2 * d:3 * d]
        shift_f, scale_f = mod_ref[0, :, 3 * d:4 * d], mod_ref[0, :, 4 * d:5 * d]
        x1 = x_ref[...] + gate_a * mix
        x1_ref[...] = x1
        _, xn = _rms_stats(x1)
        h2_ref[...] = ((xn * gf_ref[...]) * (1.0 + scale_f) + shift_f).astype(BF16)

    tile = lambda w: pl.BlockSpec((tm, w), lambda i: (i, 0))
    vec = lambda w: pl.BlockSpec((1, w), lambda i: (0, 0))
    return pl.pallas_call(
        body, name="out_proj", grid=(t // tm,),
        out_shape=(jax.ShapeDtypeStruct((t, d), BF16), jax.ShapeDtypeStruct((t, d), F32),
                   jax.ShapeDtypeStruct((t, d), F32), jax.ShapeDtypeStruct((t, d), BF16)),
        in_specs=[tile(NA_WIDTH), tile(SW_WIDTH), vec(NA_WIDTH), vec(SW_WIDTH),
                  pl.BlockSpec((d, d), lambda i: (0, 0)), tile(d),
                  pl.BlockSpec((1, 1, 6 * d), lambda i: (i // per_seq, 0, 0)), vec(d)],
        out_specs=(tile(d), tile(d), tile(d), tile(d)),
        compiler_params=_cparams(("arbitrary",), VMEM_BIG),
    )(oa, ob, g_na, g_sw, w_out, x, mod3, g_ffn)


def _up_proj(h2, w_up_halves, rider=None):
    t, d = h2.shape
    tm = 2 * TOKEN_TILE
    w_a, w_b = w_up_halves
    half, wcol = w_a.shape[1], w_a.shape[2]

    def body(h_ref, wa_ref, wb_ref, u_ref):
        u_ref[0] = (_mm(h_ref[:, :half], wa_ref[0]) + _mm(h_ref[:, half:], wb_ref[0])).astype(BF16)

    w_spec = pl.BlockSpec((1, half, wcol), lambda j, i: (j, 0, 0))
    return _hosted(
        body, rider, name="up_proj", grid=(N_SHARD, t // tm),
        out_shape=[jax.ShapeDtypeStruct((2, t, D_FF), BF16)],
        in_specs=[pl.BlockSpec((tm, d), lambda j, i: (i, 0)), w_spec, w_spec],
        out_specs=[pl.BlockSpec((1, tm, wcol), lambda j, i: (j // 2, i, j % 2))],
        scratch_shapes=[], compiler_params=_cparams(("arbitrary", "arbitrary"), VMEM_BIG), args=[h2, w_a, w_b])


def _taps_chunk(load, s, rows, seq):
    halo = 2 * SUBLANES
    cur = load(s, rows)
    above = load(pl.multiple_of(jnp.maximum(s - halo, 0), halo), halo)
    below = load(pl.multiple_of(jnp.minimum(s + rows, seq - halo), halo), halo)
    up = jnp.where(s > 0, above[halo - 1:halo, :], 0.0)
    dn = jnp.where(s + rows < seq, below[0:1, :], 0.0)
    row = lax.broadcasted_iota(jnp.int32, cur.shape, 0)
    prev = jnp.where(row == 0, up, pltpu.roll(cur, 1, 0))
    nxt = jnp.where(row == rows - 1, dn, pltpu.roll(cur, rows - 1, 0))
    return cur, prev, nxt


def _conv_gate(u, conv_w, conv_b, batch, seq, rider=None):
    t = u.shape[1]
    cw = FF_TILE
    rows = CONV_CHUNK

    def body(u_ref, w_ref, b_ref, a_ref):
        def chunk(i, carry):
            s = pl.multiple_of(i * rows, rows)
            gt, prev, nxt = _taps_chunk(lambda at, n: u_ref[1, pl.ds(at, n), :].astype(F32), s, rows, seq)
            gc = prev * w_ref[0:1, :] + gt * w_ref[1:2, :] + nxt * w_ref[2:3, :] + b_ref[...]
            a_ref[pl.ds(s, rows), :] = ((gc * _sigmoid(gc)) * u_ref[0, pl.ds(s, rows), :].astype(F32)).astype(BF16)
            return carry

        lax.fori_loop(0, seq // rows, chunk, 0)

    return _hosted(
        body, rider, name="conv_gate", grid=(batch, D_FF // cw),
        out_shape=[jax.ShapeDtypeStruct((t, D_FF), BF16)],
        in_specs=[pl.BlockSpec((2, seq, cw), lambda b, j: (0, b, j)),
                  pl.BlockSpec((3, cw), lambda b, j: (0, j)), pl.BlockSpec((1, cw), lambda b, j: (0, j))],
        out_specs=[pl.BlockSpec((seq, cw), lambda b, j: (b, j))], scratch_shapes=[],
        compiler_params=_cparams(("arbitrary", "arbitrary"), VMEM_BIG), args=[u, conv_w, conv_b])


def _down_and_loss(a, w_down, x1, mod3, g_final, target, seq):
    t, d = x1.shape
    tm = TOKEN_TILE
    per_seq = seq // tm
    batch = t // seq

    def body(a_ref, w_ref, x1_ref, mod_ref, g_ref, tgt_ref, dx2_ref, dffn_ref, loss_ref, dgate_ref, dg_ref):
        i = pl.program_id(0)
        f = _mm(a_ref[...], w_ref[...])
        gate_f = mod_ref[0, :, 5 * d:6 * d]
        x2 = x1_ref[...] + gate_f * f
        r, xn = _rms_stats(x2)
        err = xn * g_ref[...] - tgt_ref[...]
        part = 0.5 * jnp.sum(jnp.mean(err * err, axis=-1, keepdims=True))
        dy = err / d
        dx2 = _rms_bwd(dy * g_ref[...], xn, r)
        dx2_ref[...] = dx2
        dffn_ref[...] = (dx2 * gate_f).astype(BF16)

        @pl.when(i == 0)
        def _():
            loss_ref[...] = jnp.zeros_like(loss_ref)
            dg_ref[...] = jnp.zeros_like(dg_ref)

        @pl.when(i % per_seq == 0)
        def _():
            dgate_ref[...] = jnp.zeros_like(dgate_ref)

        loss_ref[...] += part
        dg_ref[...] += jnp.sum(dy * xn, axis=0, keepdims=True)
        dgate_ref[0] += jnp.sum(dx2 * f, axis=0, keepdims=True)

    tile = lambda w: pl.BlockSpec((tm, w), lambda i: (i, 0))
    return pl.pallas_call(
        body, name="down_loss", grid=(t // tm,),
        out_shape=(jax.ShapeDtypeStruct((t, d), F32), jax.ShapeDtypeStruct((t, d), BF16),
                   jax.ShapeDtypeStruct((SUBLANES, LANES), F32), jax.ShapeDtypeStruct((batch, 1, d), F32),
                   jax.ShapeDtypeStruct((1, d), F32)),
        in_specs=[tile(D_FF), _resident((D_FF, d)), tile(d),
                  pl.BlockSpec((1, 1, 6 * d), lambda i: (i // per_seq, 0, 0)),
                  pl.BlockSpec((1, d), lambda i: (0, 0)), tile(d)],
        out_specs=(tile(d), tile(d), pl.BlockSpec((SUBLANES, LANES), lambda i: (0, 0)),
                   pl.BlockSpec((1, 1, d), lambda i: (i // per_seq, 0, 0)), pl.BlockSpec((1, d), lambda i: (0, 0))),
        compiler_params=_cparams(("arbitrary",), VMEM_BIG),
    )(a, w_down, x1, mod3, g_final, target)


def _down_weight_grad(a, dffn):
    t, dff = a.shape
    d = dffn.shape[1]
    tk = 2 * TOKEN_TILE
    n_k = t // tk

    def body(a_ref, df_ref, g_ref, gb_ref):
        k = pl.program_id(0)

        @pl.when(k == 0)
        def _():
            g_ref[...] = jnp.zeros_like(g_ref)

        g_ref[...] += _mm_tn(a_ref[...], df_ref[...])

        @pl.when(k == n_k - 1)
        def _():
            gb_ref[...] = g_ref[...].astype(BF16)

    whole = _resident((dff, d))
    return pl.pallas_call(
        body, name="down_weight_grad", grid=(n_k,),
        out_shape=(jax.ShapeDtypeStruct((dff, d), F32), jax.ShapeDtypeStruct((dff, d), BF16)),
        in_specs=[pl.BlockSpec((tk, dff), lambda k: (k, 0)), pl.BlockSpec((tk, d), lambda k: (k, 0))],
        out_specs=(whole, whole),
        compiler_params=_cparams(("arbitrary",), VMEM_BIG),
    )(a, dffn)


def _ffn_backward(dffn, w_down, u, conv_w, conv_b, batch, seq, rider=None):
    t, d = dffn.shape
    cw = FF_TILE
    rows = CONV_CHUNK

    def body(df_ref, wd_ref, u_ref, w_ref, b_ref, du_ref, gcw_ref, gcb_ref, da_scr, dgc_scr):
        b = pl.program_id(1)
        da_scr[...] = _mm_nt(df_ref[...], wd_ref[...])

        @pl.when(b == 0)
        def _():
            gcw_ref[...] = jnp.zeros_like(gcw_ref)
            gcb_ref[...] = jnp.zeros_like(gcb_ref)

        def fold(v):
            return jnp.sum(v.reshape(rows // SUBLANES, SUBLANES, cw), axis=0)

        def chunk(i, carry):
            s = pl.multiple_of(i * rows, rows)
            here = pl.ds(s, rows)
            gt, prev, nxt = _taps_chunk(lambda at, n: u_ref[1, pl.ds(at, n), :].astype(F32), s, rows, seq)
            val, da = u_ref[0, here, :].astype(F32), da_scr[here, :]
            gc = prev * w_ref[0:1, :] + gt * w_ref[1:2, :] + nxt * w_ref[2:3, :] + b_ref[...]
            sg = _sigmoid(gc)
            sl = gc * sg
            du_ref[0, here, :] = (da * sl).astype(BF16)
            dgc = (da * val) * (sg * (1.0 + gc * (1.0 - sg)))
            dgc_scr[here, :] = dgc
            cb, c0, c1, c2 = carry
            return cb + fold(dgc), c0 + fold(dgc * prev), c1 + fold(dgc * gt), c2 + fold(dgc * nxt)

        zero = jnp.zeros((SUBLANES, cw), F32)
        cb, c0, c1, c2 = lax.fori_loop(0, seq // rows, chunk, (zero, zero, zero, zero))
        gcb_ref[...] += jnp.sum(cb, axis=0, keepdims=True)
        gcw_ref[0:1, :] += jnp.sum(c0, axis=0, keepdims=True)
        gcw_ref[1:2, :] += jnp.sum(c1, axis=0, keepdims=True)
        gcw_ref[2:3, :] += jnp.sum(c2, axis=0, keepdims=True)

        def chunk2(i, carry):
            s = pl.multiple_of(i * rows, rows)
            dgc, dprev, dnxt = _taps_chunk(lambda at, n: dgc_scr[pl.ds(at, n), :], s, rows, seq)
            du_ref[1, pl.ds(s, rows), :] = (dnxt * w_ref[0:1, :] + dgc * w_ref[1:2, :]
                                            + dprev * w_ref[2:3, :]).astype(BF16)
            return carry

        lax.fori_loop(0, seq // rows, chunk2, 0)

    return _hosted(
        body, rider, name="ffn_backward", grid=(D_FF // cw, batch),
        out_shape=[jax.ShapeDtypeStruct((2, t, D_FF), BF16),
                   jax.ShapeDtypeStruct((3, D_FF), F32), jax.ShapeDtypeStruct((1, D_FF), F32)],
        in_specs=[pl.BlockSpec((seq, d), lambda j, b: (b, 0)), pl.BlockSpec((cw, d), lambda j, b: (j, 0)),
                  pl.BlockSpec((2, seq, cw), lambda j, b: (0, b, j)),
                  pl.BlockSpec((3, cw), lambda j, b: (0, j)), pl.BlockSpec((1, cw), lambda j, b: (0, j))],
        out_specs=[pl.BlockSpec((2, seq, cw), lambda j, b: (0, b, j)),
                   pl.BlockSpec((3, cw), lambda j, b: (0, j)), pl.BlockSpec((1, cw), lambda j, b: (0, j))],
        scratch_shapes=[pltpu.VMEM((seq, cw), F32), pltpu.VMEM((seq, cw), F32)],
        compiler_params=_cparams(("arbitrary", "arbitrary"), VMEM_BIG), args=[dffn, w_down, u, conv_w, conv_b])


def _up_backward(du, w_up, x1, mod3, g_ffn, dx2, mix, seq, rider=None):
    _, t, _ = du.shape
    d = x1.shape[1]
    tm = TOKEN_TILE
    per_seq = seq // tm
    batch = t // seq
    w_a, w_b = w_up
    half, wcol = w_a.shape[1], w_a.shape[2]

    def body(du_ref, wa_ref, wb_ref, x1_ref, mod_ref, g_ref, dx2_ref, mix_ref,
             dx1_ref, dmix_ref, dsh_ref, dsc_ref, dga_ref, dg_ref):
        i = pl.program_id(0)
        parts = []
        for w_ref in (wa_ref, wb_ref):
            acc = jnp.zeros((tm, half), F32)
            for j in range(N_SHARD):
                acc = acc + _mm_nt(du_ref[j // 2, :, (j % 2) * wcol:(j % 2 + 1) * wcol], w_ref[j])
            parts.append(acc)
        dh = jnp.concatenate(parts, axis=1)
        gate_a = mod_ref[0, :, 2 * d:3 * d]
        scale_f = mod_ref[0, :, 4 * d:5 * d]
        r, xn = _rms_stats(x1_ref[...])
        xg = xn * g_ref[...]
        dxg = dh * (1.0 + scale_f)
        dx1 = dx2_ref[...] + _rms_bwd(dxg * g_ref[...], xn, r)
        dx1_ref[...] = dx1
        dmix_ref[...] = (dx1 * gate_a).astype(BF16)

        @pl.when(i == 0)
        def _():
            dg_ref[...] = jnp.zeros_like(dg_ref)

        @pl.when(i % per_seq == 0)
        def _():
            dsh_ref[...] = jnp.zeros_like(dsh_ref)
            dsc_ref[...] = jnp.zeros_like(dsc_ref)
            dga_ref[...] = jnp.zeros_like(dga_ref)

        dg_ref[...] += jnp.sum(dxg * xn, axis=0, keepdims=True)
        dsh_ref[0] += jnp.sum(dh, axis=0, keepdims=True)
        dsc_ref[0] += jnp.sum(dh * xg, axis=0, keepdims=True)
        dga_ref[0] += jnp.sum(dx1 * mix_ref[...], axis=0, keepdims=True)

    tile = lambda w: pl.BlockSpec((tm, w), lambda i: (i, 0))
    per_b = pl.BlockSpec((1, 1, d), lambda i: (i // per_seq, 0, 0))
    small = jax.ShapeDtypeStruct((batch, 1, d), F32)
    return _hosted(
        body, rider, name="up_backward", grid=(t // tm,),
        out_shape=[jax.ShapeDtypeStruct((t, d), F32), jax.ShapeDtypeStruct((t, d), BF16), small, small, small,
                   jax.ShapeDtypeStruct((1, d), F32)],
        in_specs=[pl.BlockSpec((2, tm, D_FF), lambda i: (0, i, 0)),
                  _resident((N_SHARD, half, wcol)), _resident((N_SHARD, half, wcol)), tile(d),
                  pl.BlockSpec((1, 1, 6 * d), lambda i: (i // per_seq, 0, 0)),
                  pl.BlockSpec((1, d), lambda i: (0, 0)), tile(d), tile(d)],
        out_specs=[tile(d), tile(d), per_b, per_b, per_b, pl.BlockSpec((1, d), lambda i: (0, 0))],
        scratch_shapes=[], compiler_params=_cparams(("arbitrary",), VMEM_BIG),
        args=[du, w_a, w_b, x1, mod3, g_ffn, dx2, mix])


def _up_weight_grad(h2, du, rider=None):
    t, d = h2.shape
    tk = 2 * TOKEN_TILE
    wcol = D_FF // 2
    half = d // 2
    n_k = t // tk

    def body(h_ref, du_ref, ga_ref, gb_ref, ga16_ref, gb16_ref):
        k = pl.program_id(1)

        @pl.when(k == 0)
        def _():
            ga_ref[...] = jnp.zeros_like(ga_ref)
            gb_ref[...] = jnp.zeros_like(gb_ref)

        du = du_ref[0]
        ga_ref[0] += _mm_tn(h_ref[:, :half], du)
        gb_ref[0] += _mm_tn(h_ref[:, half:], du)

        @pl.when(k == n_k - 1)
        def _():
            ga16_ref[...] = ga_ref[...].astype(BF16)
            gb16_ref[...] = gb_ref[...].astype(BF16)

    g_spec = pl.BlockSpec((1, half, wcol), lambda j, k: (j, 0, 0))
    f32_out = jax.ShapeDtypeStruct((N_SHARD, half, wcol), F32)
    b16_out = jax.ShapeDtypeStruct((N_SHARD, half, wcol), BF16)
    return _hosted(
        body, rider, name="up_weight_grad", grid=(N_SHARD, n_k),
        out_shape=[f32_out, f32_out, b16_out, b16_out],
        in_specs=[pl.BlockSpec((tk, d), lambda j, k: (k, 0)),
                  pl.BlockSpec((1, tk, wcol), lambda j, k: (j // 2, k, j % 2))],
        out_specs=[g_spec, g_spec, g_spec, g_spec], scratch_shapes=[],
        compiler_params=_cparams(("arbitrary", "arbitrary"), VMEM_BIG), args=[h2, du])


def _out_backward(dmix, w_out, oab, oa, ob, g_na, g_sw):
    t, d = dmix.shape
    tm = 2 * TOKEN_TILE
    hw = NA_WIDTH

    def body(dm_ref, w_ref, oab_ref, oa_ref, ob_ref, gna_ref, gsw_ref,
             doa_ref, dob_ref, gw_ref, gwb_ref, dgna_ref, dgsw_ref):
        @pl.when(pl.program_id(0) == 0)
        def _():
            gw_ref[...] = jnp.zeros_like(gw_ref)
            dgna_ref[...] = jnp.zeros_like(dgna_ref)
            dgsw_ref[...] = jnp.zeros_like(dgsw_ref)

        dm = dm_ref[...]
        gw_ref[...] += _mm_tn(oab_ref[...], dm)

        @pl.when(pl.program_id(0) == t // tm - 1)
        def _():
            gwb_ref[...] = gw_ref[...].astype(BF16)

        do = _mm_nt(dm, w_ref[...])
        for raw_ref, g_ref, dst_ref, dg_ref, lo in ((oa_ref, gna_ref, doa_ref, dgna_ref, 0),
                                                     (ob_ref, gsw_ref, dob_ref, dgsw_ref, hw)):
            r, xn = _rms_stats(raw_ref[...])
            dpart = do[:, lo:lo + hw]
            dg_ref[...] += jnp.sum(dpart * xn, axis=0, keepdims=True)
            dst_ref[...] = _rms_bwd(dpart * g_ref[...], xn, r).astype(BF16)

    tile = lambda w: pl.BlockSpec((tm, w), lambda i: (i, 0))
    vec = lambda w: pl.BlockSpec((1, w), lambda i: (0, 0))
    return pl.pallas_call(
        body, name="out_backward", grid=(t // tm,),
        out_shape=(jax.ShapeDtypeStruct((t, hw), BF16), jax.ShapeDtypeStruct((t, hw), BF16),
                   jax.ShapeDtypeStruct((d, d), F32), jax.ShapeDtypeStruct((d, d), BF16),
                   jax.ShapeDtypeStruct((1, hw), F32), jax.ShapeDtypeStruct((1, hw), F32)),
        in_specs=[tile(d), pl.BlockSpec((d, d), lambda i: (0, 0)), tile(d), tile(hw), tile(hw), vec(hw), vec(hw)],
        out_specs=(tile(hw), tile(hw), pl.BlockSpec((d, d), lambda i: (0, 0)), pl.BlockSpec((d, d), lambda i: (0, 0)),
                   vec(hw), vec(hw)),
        compiler_params=_cparams(("arbitrary",), VMEM_BIG),
    )(dmix, w_out, oab, oa, ob, g_na, g_sw)


def _na_backward(proj, d_o, lse, tiles, batch, seq, rider=None):
    t = proj.shape[0]
    n_rows = seq // GRID_W
    n_pairs = NA_WIDTH // LANES
    win = NA_ROWS * GRID_W
    n_tiles = 2 * NA_ROWS - 2

    def body(q_ref, k_ref, v_ref, do_ref, lse_ref, tp_ref, dq_ref, dk_ref, dv_ref, dtp_ref, km, vm, dk_acc, dv_acc):
        @pl.when(pl.program_id(1) == 0)
        def _():
            dtp_ref[...] = jnp.zeros_like(dtp_ref)

        _na_prepare(k_ref, v_ref, km, vm)
        dk_acc[...] = jnp.zeros_like(dk_acc)
        dv_acc[...] = jnp.zeros_like(dv_acc)
        low = lax.broadcasted_iota(jnp.int32, (win, LANES), 1) < HEAD_DIM

        def scores(r):
            rs, off = _na_window(r, n_rows)
            rows = pl.ds(pl.multiple_of(r * GRID_W, GRID_W), GRID_W)
            wrows = pl.ds(pl.multiple_of(rs * GRID_W, GRID_W), win)
            q, do = q_ref[rows, :], do_ref[rows, :]
            k2 = _na_pair_window(km, wrows)
            s = _na_scores(q, k2, tp_ref, off)
            dp = _mm_nt(do, _na_pair_window(vm, wrows))
            return rows, wrows, off, q, do, k2, s, dp

        def finish(rows, wrows, off, q, do, k2, s, dp):
            p = _pair_probs_from_lse(s, lse_ref[rows, :])
            parts = []
            for h in range(2):
                ph, dph = p[:, h * win:(h + 1) * win], dp[:, h * win:(h + 1) * win]
                dsh = ph * (dph - jnp.sum(ph * dph, axis=-1, keepdims=True))
                for w in range(NA_ROWS // 2):
                    dtp_ref[h, 2 * w - off + (NA_ROWS - 1)] += dsh[:, w * LANES:(w + 1) * LANES]
                parts.append(dsh)
            dsb = (jnp.concatenate(parts, axis=1) * QK_SCALE).astype(BF16)
            dq_ref[rows, :] = _mm(dsb, k2).astype(BF16)
            dk_acc[wrows, :] += _pair_grad(dsb, q, low)
            dv_acc[wrows, :] += _pair_grad(p.astype(BF16), do, low)

        def row_group(i, carry):
            for state in [scores(NA_GROUP * i + j) for j in range(NA_GROUP)]:
                finish(*state)
            return carry

        lax.fori_loop(0, n_rows // NA_GROUP, row_group, 0)
        dk_ref[...] = dk_acc[...].astype(BF16)
        dv_ref[...] = dv_acc[...].astype(BF16)

    blk = lambda off: pl.BlockSpec((seq, LANES), lambda p, b: (b, off + p))
    out = jax.ShapeDtypeStruct((t, NA_WIDTH), BF16)
    return _hosted(
        body, rider, name="na_backward", grid=(n_pairs, batch),
        out_shape=[out, out, out, jax.ShapeDtypeStruct(tiles.shape, F32)],
        in_specs=[blk(0), blk(n_pairs), blk(2 * n_pairs), blk(0), blk(0),
                  pl.BlockSpec((2, n_tiles, GRID_W, LANES), lambda p, b: (p, 0, 0, 0))],
        out_specs=[blk(0), blk(0), blk(0), pl.BlockSpec((2, n_tiles, GRID_W, LANES), lambda p, b: (p, 0, 0, 0))],
        scratch_shapes=[pltpu.VMEM((2, seq, LANES), BF16), pltpu.VMEM((2, seq, LANES), BF16),
                        pltpu.VMEM((seq, LANES), F32), pltpu.VMEM((seq, LANES), F32)],
        compiler_params=_cparams(("arbitrary", "arbitrary")), args=[proj, proj, proj, d_o, lse, tiles])


def _na_bias_grad(dtiles, expand):
    n = dtiles.shape[0]

    def body(t_ref, e_ref, o_ref):
        flat = jnp.concatenate([t_ref[:, qq, :] for qq in range(GRID_W)], axis=1)
        o_ref[...] = lax.dot_general(flat, e_ref[...], (((1,), (1,)), ((), ())),
                                     precision=lax.Precision.HIGHEST, preferred_element_type=F32)

    return pl.pallas_call(
        body, name="na_bias_grad",
        out_shape=jax.ShapeDtypeStruct((n, expand.shape[0]), F32),
        compiler_params=_cparams(vmem=VMEM_BIG),
    )(dtiles, expand)


def _sw_backward(proj, d_o, lse, sink, batch, seq, rider=None):
    t = proj.shape[0]
    n_pairs = SW_WIDTH // LANES
    q_blk = 3 * NA_WIDTH // LANES
    k_blk = q_blk + n_pairs
    n_blocks = seq // SW_BLOCK
    pad = seq + 2 * SW_BLOCK

    def body(sink_ref, q_ref, k_ref, v_ref, do_ref, lse_ref, dq_ref, dk_ref, dv_ref, dsk_ref,
             k_lo, k_hi, v_lo, v_hi, dk_loc, dv_loc, dk_tot, dv_tot):
        hp = pl.program_id(1)
        g = hp // 2

        @pl.when(hp % 2 == 0)
        def _():
            _sw_prepare(k_ref, g, k_lo, k_hi, seq)
            _sw_prepare(v_ref, g, v_lo, v_hi, seq)
            dk_loc[...] = jnp.zeros_like(dk_loc)
            dv_loc[...] = jnp.zeros_like(dv_loc)

        @pl.when(hp == 0)
        def _():
            dk_tot[...] = jnp.zeros_like(dk_tot)
            dv_tot[...] = jnp.zeros_like(dv_tot)

        band = 3 * SW_BLOCK
        low = lax.broadcasted_iota(jnp.int32, (band, LANES), 1) < HEAD_DIM

        sinks = (sink_ref[2 * hp], sink_ref[2 * hp + 1])

        def scores(n):
            rows = pl.ds(pl.multiple_of(n * SW_BLOCK, SW_BLOCK), SW_BLOCK)
            wrows = pl.ds(pl.multiple_of(n * SW_BLOCK, SW_BLOCK), band)
            qb, do = q_ref[rows, :], do_ref[rows, :]
            k2 = jnp.concatenate([k_lo[wrows, :], k_hi[wrows, :]], axis=0)
            v2 = jnp.concatenate([v_lo[wrows, :], v_hi[wrows, :]], axis=0)
            return n, rows, wrows, qb, do, k2, _mm_nt(qb, k2) * QK_SCALE, _mm_nt(do, v2)

        def finish(sink_acc, n, rows, wrows, qb, do, k2, s2, dp):
            p, ps = _sw_probs_from_lse(s2, _sw_mask(n, seq), sinks, lse_ref[rows, :])
            parts, new = [], []
            for i in range(2):
                ph, dph = p[:, i * band:(i + 1) * band], dp[:, i * band:(i + 1) * band]
                delta = jnp.sum(ph * dph, axis=-1, keepdims=True)
                parts.append(ph * (dph - delta))
                new.append(sink_acc[i] - ps[i] * delta)
            dsb = (jnp.concatenate(parts, axis=1) * QK_SCALE).astype(BF16)
            dq_ref[rows, :] = _mm(dsb, k2)
            dk_loc[wrows, :] += _pair_grad(dsb, qb, low)
            dv_loc[wrows, :] += _pair_grad(p.astype(BF16), do, low)
            return tuple(new)

        def block_group(i, carry):
            for state in [scores(SW_GROUP_BLOCKS * i + j) for j in range(SW_GROUP_BLOCKS)]:
                carry = finish(carry, *state)
            return carry

        zero = jnp.zeros((SW_BLOCK, 1), F32)
        s0, s1 = lax.fori_loop(0, n_blocks // SW_GROUP_BLOCKS, block_group, (zero, zero))
        row = lax.broadcasted_iota(jnp.int32, (SUBLANES, LANES), 0)
        dsk_ref[0, 0] = jnp.where(row == 0, jnp.sum(s0), jnp.where(row == 1, jnp.sum(s1), 0.0))

        @pl.when(hp % 2 == 1)
        def _():
            lane_s = lax.broadcasted_iota(jnp.int32, (seq, LANES), 1)
            mine_g = (lane_s // HEAD_DIM) == g
            for loc, tot in ((dk_loc, dk_tot), (dv_loc, dv_tot)):
                part = loc[SW_BLOCK:SW_BLOCK + seq, :]
                tot[...] += jnp.where(mine_g, part + pltpu.roll(part, HEAD_DIM, 1), 0.0)

        @pl.when(hp == n_pairs - 1)
        def _():
            dk_ref[...] = dk_tot[...]
            dv_ref[...] = dv_tot[...].astype(BF16)

    return _hosted(
        body, rider, name="sw_backward", grid=(batch, n_pairs),
        out_shape=[jax.ShapeDtypeStruct((t, SW_WIDTH), F32), jax.ShapeDtypeStruct((t, LANES), F32),
                   jax.ShapeDtypeStruct((t, LANES), BF16), jax.ShapeDtypeStruct((batch, n_pairs, SUBLANES, LANES), F32)],
        in_specs=[pl.BlockSpec(memory_space=pltpu.SMEM),
                  pl.BlockSpec((seq, LANES), lambda b, p: (b, q_blk + p)),
                  pl.BlockSpec((seq, LANES), lambda b, p: (b, k_blk)),
                  pl.BlockSpec((seq, LANES), lambda b, p: (b, k_blk + 1)),
                  pl.BlockSpec((seq, LANES), lambda b, p: (b, p)), pl.BlockSpec((seq, LANES), lambda b, p: (b, p))],
        out_specs=[pl.BlockSpec((seq, LANES), lambda b, p: (b, p)), pl.BlockSpec((seq, LANES), lambda b, p: (b, 0)),
                   pl.BlockSpec((seq, LANES), lambda b, p: (b, 0)),
                   pl.BlockSpec((1, 1, SUBLANES, LANES), lambda b, p: (b, p, 0, 0))],
        scratch_shapes=[pltpu.VMEM((pad, LANES), BF16)] * 4 + [pltpu.VMEM((pad, LANES), F32)] * 2
        + [pltpu.VMEM((seq, LANES), F32)] * 2,
        compiler_params=_cparams(("arbitrary", "arbitrary")), args=[sink, proj, proj, proj, d_o, lse])


def _in_backward(dqkv_a, dq_b, dk_b, dv_b, w_in_t, h1, x, mod3, g_attn, dx1, cos_t, sin_t, seq):
    t, d = x.shape
    tm = TOKEN_TILE
    per_seq = seq // tm
    batch = t // seq
    dqa, dka, dva = dqkv_a
    n_q = SW_WIDTH // LANES

    def body(dqa_ref, dka_ref, dva_ref, dqb_ref, dkb_ref, dvb_ref, w_ref, h_ref, x_ref, mod_ref, g_ref, dx1_ref,
             cos_ref, sin_ref, dx_ref, gw_ref, gwb_ref, dsh_ref, dsc_ref, dg_ref):
        i = pl.program_id(0)

        @pl.when(i == 0)
        def _():
            gw_ref[...] = jnp.zeros_like(gw_ref)
            dg_ref[...] = jnp.zeros_like(dg_ref)

        @pl.when(i % per_seq == 0)
        def _():
            dsh_ref[...] = jnp.zeros_like(dsh_ref)
            dsc_ref[...] = jnp.zeros_like(dsc_ref)

        dr = jnp.concatenate([dqb_ref[...], dkb_ref[...]], axis=1)
        cos = jnp.concatenate([cos_ref[...]] * (n_q + 1), axis=1)
        sin = jnp.concatenate([sin_ref[...]] * (n_q + 1), axis=1)
        dr = dr * cos + _rope_rot(dr * sin)
        dproj = jnp.concatenate([dqa_ref[...], dka_ref[...], dva_ref[...], dr.astype(BF16), dvb_ref[...]], axis=1)
        gw_ref[...] += _mm_tn(dproj, h_ref[...])

        @pl.when(i == t // tm - 1)
        def _():
            gwb_ref[...] = gw_ref[...].astype(BF16)

        dh = _mm(dproj, w_ref[...])
        scale = mod_ref[0, :, d:2 * d]
        r, xn = _rms_stats(x_ref[...])
        xg = xn * g_ref[...]
        dxg = dh * (1.0 + scale)
        dx_ref[...] = dx1_ref[...] + _rms_bwd(dxg * g_ref[...], xn, r)
        dg_ref[...] += jnp.sum(dxg * xn, axis=0, keepdims=True)
        dsh_ref[0] += jnp.sum(dh, axis=0, keepdims=True)
        dsc_ref[0] += jnp.sum(dh * xg, axis=0, keepdims=True)

    tile = lambda w: pl.BlockSpec((tm, w), lambda i: (i, 0))
    per_b = pl.BlockSpec((1, 1, d), lambda i: (i // per_seq, 0, 0))
    small = jax.ShapeDtypeStruct((batch, 1, d), F32)
    rope = pl.BlockSpec((tm, LANES), lambda i: (i % per_seq, 0))
    return pl.pallas_call(
        body, name="in_backward", grid=(t // tm,),
        out_shape=(jax.ShapeDtypeStruct((t, d), F32), jax.ShapeDtypeStruct((IN_WIDTH, d), F32),
                   jax.ShapeDtypeStruct((IN_WIDTH, d), BF16), small, small, jax.ShapeDtypeStruct((1, d), F32)),
        in_specs=[tile(NA_WIDTH), tile(NA_WIDTH), tile(NA_WIDTH), tile(SW_WIDTH), tile(LANES), tile(LANES),
                  _resident((IN_WIDTH, d)), tile(d), tile(d),
                  pl.BlockSpec((1, 1, 6 * d), lambda i: (i // per_seq, 0, 0)),
                  pl.BlockSpec((1, d), lambda i: (0, 0)), tile(d), rope, rope],
        out_specs=(tile(d), _resident((IN_WIDTH, d)), _resident((IN_WIDTH, d)),
                   per_b, per_b, pl.BlockSpec((1, d), lambda i: (0, 0))),
        compiler_params=_cparams(("arbitrary",), VMEM_BIG),
    )(dqa, dka, dva, dq_b, dk_b, dv_b, w_in_t, h1, x, mod3, g_attn, dx1, cos_t, sin_t)


def _ada_weight_grad(sc_all, dmod_cols):
    d = sc_all.shape[1]
    ncol = dmod_cols.shape[1]

    def body(s_ref, m_ref, o_ref):
        o_ref[...] = _mm_tn(s_ref[...].astype(BF16), m_ref[...].astype(BF16))

    return pl.pallas_call(
        body, name="ada_weight_grad",
        out_shape=jax.ShapeDtypeStruct((d, ncol), F32),
        compiler_params=_cparams(vmem=VMEM_BIG),
    )(sc_all, dmod_cols)


def _row_tile(rows, cols):
    target = max(SUBLANES, (1 << 20) // (4 * cols))
    best = rows
    for cand in range(SUBLANES, rows + 1, SUBLANES):
        if rows % cand == 0 and cand <= target:
            best = cand
    return best if rows % SUBLANES == 0 else rows


def _sum_slots(parts, name):
    n = len(parts)
    _, rows, cols = parts[0][0].shape
    tr = _row_tile(rows, cols)
    per = rows // tr

    def body(*refs):
        o_ref = refs[-1]
        for q in range(n):
            @pl.when(pl.program_id(0) == q)
            def _(q=q):
                p_ref, own_ref = refs[2 * q], refs[2 * q + 1]
                o_ref[...] = ((own_ref[...] + p_ref[0].astype(F32)) + p_ref[1].astype(F32)) + p_ref[2].astype(F32)

    in_specs, args = [], []
    for q, (recv, own) in enumerate(parts):
        in_specs.append(pl.BlockSpec((N_SHARD - 1, tr, cols), lambda p, i, q=q: (0, jnp.where(p == q, i, 0), 0)))
        in_specs.append(pl.BlockSpec((tr, cols), lambda p, i, q=q: (jnp.where(p == q, i, 0), 0)))
        args += [recv, own]
    return pl.pallas_call(
        body, name=name, grid=(n, per),
        out_shape=jax.ShapeDtypeStruct((n * rows, cols), F32),
        in_specs=in_specs, out_specs=pl.BlockSpec((tr, cols), lambda p, i: (p * per + i, 0)),
        compiler_params=_cparams(("arbitrary", "arbitrary")),
    )(*args)


def _adamw_math(w, g, m, v):
    m2 = ADAM_B1 * m + (1.0 - ADAM_B1) * g
    v2 = ADAM_B2 * v + (1.0 - ADAM_B2) * (g * g)
    m_hat = m2 / (1.0 - ADAM_B1 ** ADAM_STEP)
    v_hat = v2 / (1.0 - ADAM_B2 ** ADAM_STEP)
    return -ADAM_LR * (m_hat / (jnp.sqrt(v_hat) + ADAM_EPS) + ADAM_WD * w), m2, v2


def _small_sums(partials, dmod, rider=None):
    moving = list(partials) + [dmod]
    n_mov = len(moving)

    def body(*refs):
        mov, refs = refs[:n_mov], refs[n_mov:]
        sums_out, refs = refs[:n_mov - 1], refs[n_mov - 1:]
        b_out, dmod_out, refs = refs[0], refs[1], refs[2:]
        everyone, (ssem, rsem) = refs[:n_mov], refs[n_mov:]
        x, y, c = _my_pos()
        me = 4 * x + 2 * y + c
        cps = []
        for a in range(n_mov):
            everyone[a][me] = mov[a][...]
            for k in range(1, N_DEV):
                peer = (_flip(x, (k >> 2) & 1), _flip(y, (k >> 1) & 1), _flip(c, k & 1))
                cps.append(pltpu.make_async_remote_copy(
                    src_ref=everyone[a].at[me], dst_ref=everyone[a].at[me], send_sem=ssem.at[a, k - 1],
                    recv_sem=rsem.at[a, k - 1], device_id=peer, device_id_type=MESH))
        for cp in cps:
            cp.start()
        for cp in cps:
            cp.wait_recv()

        def total(a):
            acc = everyone[a][0]
            for dev in range(1, N_DEV):
                acc = acc + everyone[a][dev]
            return acc

        for a in range(n_mov - 1):
            sums_out[a][...] = total(a)
        b_out[...] = jnp.sum(total(n_mov - 1), axis=0, keepdims=True)
        dmod_out[...] = everyone[n_mov - 1][...]
        for cp in cps:
            cp.wait_send()

    vm = pl.BlockSpec(memory_space=pltpu.VMEM)
    sds = jax.ShapeDtypeStruct
    out_shape = [sds(p.shape, F32) for p in partials]
    out_shape += [sds((1, dmod.shape[1]), F32), sds((N_DEV,) + dmod.shape, F32)]
    return _hosted(
        body, rider, name="small_sums", grid=(), out_shape=out_shape,
        in_specs=[vm] * n_mov, out_specs=[vm] * len(out_shape),
        scratch_shapes=[pltpu.VMEM((N_DEV,) + a.shape, F32) for a in moving]
        + [pltpu.SemaphoreType.DMA((n_mov, N_DEV - 1)), pltpu.SemaphoreType.DMA((n_mov, N_DEV - 1))],
        compiler_params=_cparams(vmem=VMEM_BIG), args=moving)


def _small_adamw(states, grads):
    n = len(states)

    def body(*refs):
        g_refs, wmv, res = refs[:n], refs[n:4 * n], refs[4 * n:]
        for j in range(n):
            g = g_refs[j][...]
            delta, m2, v2 = _adamw_math(wmv[3 * j][...], g, wmv[3 * j + 1][...], wmv[3 * j + 2][...])
            res[4 * j][...] = g
            res[4 * j + 1][...] = delta
            res[4 * j + 2][...] = m2
            res[4 * j + 3][...] = v2

    out_shape = []
    for w, _, _ in states:
        out_shape += [jax.ShapeDtypeStruct(w.shape, F32)] * 4
    outs = pl.pallas_call(body, name="small_adamw", out_shape=tuple(out_shape),
                          compiler_params=_cparams(vmem=VMEM_BIG))(*grads, *[a for st in states for a in st])
    return [outs[4 * j:4 * j + 4] for j in range(n)]


def _adamw(w, grads, m, v, name):
    rows, cols = w.shape
    tr = _row_tile(rows, cols)
    ng = len(grads)

    def body(*refs):
        w_ref = refs[0]
        g_refs = refs[1:1 + ng]
        m_ref, v_ref = refs[1 + ng], refs[2 + ng]
        g_out, d_out, m_out, v_out = refs[3 + ng:]
        g = g_refs[0][...]
        for extra in g_refs[1:]:
            g = g + extra[...]
        g_out[...] = g
        d_out[...], m_out[...], v_out[...] = _adamw_math(w_ref[...], g, m_ref[...], v_ref[...])

    spec = pl.BlockSpec((tr, cols), lambda i: (i, 0))
    out = jax.ShapeDtypeStruct((rows, cols), F32)
    return pl.pallas_call(
        body, name=name, grid=(rows // tr,),
        out_shape=(out, out, out, out),
        in_specs=[spec] * (3 + ng), out_specs=(spec, spec, spec, spec),
        compiler_params=_cparams(("arbitrary",)),
    )(w, *grads, m, v)


def _rope_tables(seq):
    half = HEAD_DIM // 2
    inv = np.float32(ROPE_THETA) ** (-np.arange(half, dtype=np.float32) / np.float32(half))
    ang = (np.arange(seq, dtype=np.float32)[:, None] * inv[None, :]).astype(np.float64)
    cos, sin = np.cos(ang).astype(np.float32), np.sin(ang).astype(np.float32)
    cos_t = np.concatenate([cos, cos, cos, cos], axis=1)
    sin_t = np.concatenate([-sin, sin, -sin, sin], axis=1)
    return jnp.asarray(cos_t), jnp.asarray(sin_t)


def kernel(x, c, w_ada, b_ada, g_attn, w_in, na_rpb, sw_sink, g_na_out, g_sw_out, w_out, g_ffn, w_up, conv_w, conv_b, w_down, g_final, loss_target, m_w_ada, m_b_ada, m_g_attn, m_w_in, m_na_rpb, m_sw_sink, m_g_na_out, m_g_sw_out, m_w_out, m_g_ffn, m_w_up, m_conv_w, m_conv_b, m_w_down, m_g_final, v_w_ada, v_b_ada, v_g_attn, v_w_in, v_na_rpb, v_sw_sink, v_g_na_out, v_g_sw_out, v_w_out, v_g_ffn, v_w_up, v_conv_w, v_conv_b, v_w_down, v_g_final):
    batch, seq, d = x.shape
    t = batch * seq
    assert d == D_MODEL and seq % (NA_ROWS * GRID_W) == 0 and seq % TOKEN_TILE == 0 and batch <= SUBLANES
    shard = 2 * lax.axis_index("x") + lax.axis_index("y")
    xt = x.reshape(t, d)
    tgt = loss_target.reshape(t, d)

    c8 = jnp.pad(c, ((0, SUBLANES - batch), (0, 0)))
    w_in_t_s = jnp.transpose(w_in[0]).astype(BF16)
    (mod8, sc_all), (w_in_g,) = _ada_forward(c8, w_ada[0], b_ada, _Rider("gather", [w_in_t_s]))
    mod3 = mod8[:batch].reshape(batch, 1, 6 * d)
    w_in_t = w_in_g.reshape(IN_WIDTH, d)

    cos_t, sin_t = _rope_tables(seq)
    (h1, proj), _ = _in_proj(xt, mod3, g_attn, w_in_t, cos_t, sin_t, seq)
    n_heads = NA_WIDTH // HEAD_DIM
    n_tiles, n_dc = 2 * NA_ROWS - 2, 2 * NA_COLS - 1
    expand, neg_mask = _na_bias_pattern()
    rpb = na_rpb[0]
    rows2 = jnp.concatenate([rpb[:, :-1, :], rpb[:, 1:, :]], axis=2).reshape(n_heads * n_tiles, 2 * n_dc)
    rows2 = jnp.pad(rows2, ((0, 0), (0, GRID_W - 2 * n_dc)))
    tiles = _na_bias_tiles(rows2, expand, neg_mask).reshape(n_heads, n_tiles, GRID_W, LANES)
    sink = sw_sink[0]
    w_up_b16 = w_up[0].astype(BF16)
    (oa, lse_a), (w_up_a,) = _na_forward(proj, tiles, batch, seq, _Rider("gather", [w_up_b16[:d // 2]]))
    (ob, lse_b), (w_up_b, conv_w_g, w_out_g) = _sw_forward(
        proj, sink, batch, seq, _Rider("gather", [w_up_b16[d // 2:], conv_w[0], w_out[0].astype(BF16)]))
    w_up_f = (w_up_a, w_up_b)
    w_out_f = w_out_g.reshape(d, d)
    conv_w_f = jnp.transpose(conv_w_g, (1, 0, 2)).reshape(3, D_FF)
    oab, mix, x1, h2 = _out_proj(oa, ob, g_na_out, g_sw_out, w_out_f, xt, mod3, g_ffn, seq)
    (u,), _ = _up_proj(h2, w_up_f)
    (a,), (w_down_g,) = _conv_gate(u, conv_w_f, conv_b, batch, seq, _Rider("gather", [w_down[0].astype(BF16)]))
    w_down_f = w_down_g.reshape(D_FF, d)
    dx2, dffn, loss_part, dgate_f, dg_final = _down_and_loss(a, w_down_f, x1, mod3, g_final.reshape(1, d), tgt, seq)

    gw_down, gw_down_b = _down_weight_grad(a, dffn)
    blocks = lambda g, rows: g.reshape(N_SHARD, rows // N_SHARD, d)
    (du, gconv_w, gconv_b), (recv_down, own_down) = _ffn_backward(
        dffn, w_down_f, u, conv_w_f, conv_b, batch, seq,
        _Rider("scatter", [blocks(gw_down_b, D_FF)], [blocks(gw_down, D_FF)]))
    (gw_up_top, gw_up_bot, gw_up_top_b, gw_up_bot_b), _ = _up_weight_grad(h2, du)
    (dx1, dmix, dshift_f, dscale_f, dgate_a, dg_ffn), _ = _up_backward(du, w_up_f, x1, mod3, g_ffn, dx2, mix, seq)
    doa, dob, gw_out, gw_out_b, dg_na, dg_sw = _out_backward(dmix, w_out_f, oab, oa, ob, g_na_out, g_sw_out)
    (dqa, dka, dva, dtiles), (recv_out, recv_up_bot, own_out, own_up_bot) = _na_backward(
        proj, doa, lse_a, tiles, batch, seq,
        _Rider("scatter", [blocks(gw_out_b, d), gw_up_bot_b], [blocks(gw_out, d), gw_up_bot]))
    (dq_b, dk_b, dv_b, dsink_parts), (recv_up_top, own_up_top) = _sw_backward(
        proj, dob, lse_b, sink, batch, seq, _Rider("scatter", [gw_up_top_b], [gw_up_top]))
    gx, gw_in_t, gw_in_b, dshift_a, dscale_a, dg_attn = _in_backward(
        (dqa, dka, dva), dq_b, dk_b, dv_b, w_in_t, h1, xt, mod3, g_attn, dx1, cos_t, sin_t, seq)

    red = _na_bias_grad(dtiles.reshape(n_heads * n_tiles, GRID_W, LANES), expand)[:, :2 * n_dc]
    red = red.reshape(n_heads, n_tiles, 2, n_dc)
    zero_row = jnp.zeros((n_heads, 1, n_dc), F32)
    g_rpb = (jnp.concatenate([red[:, :, 0, :], zero_row], axis=1)
             + jnp.concatenate([zero_row, red[:, :, 1, :]], axis=1))
    g_sink = jnp.sum(dsink_parts[:, :, :2, 0], axis=0).reshape(SW_WIDTH // HEAD_DIM)

    dmod = jnp.concatenate([dshift_a, dscale_a, dgate_a, dshift_f, dscale_f, dgate_f], axis=2).reshape(batch, 6 * d)
    rpb_shape = na_rpb.shape[1:]
    states = [(g_attn, m_g_attn, v_g_attn),
              (na_rpb.reshape(rpb_shape), m_na_rpb.reshape(rpb_shape), v_na_rpb.reshape(rpb_shape)),
              (sw_sink, m_sw_sink, v_sw_sink), (g_na_out, m_g_na_out, v_g_na_out), (g_sw_out, m_g_sw_out, v_g_sw_out),
              (g_ffn, m_g_ffn, v_g_ffn), (conv_b, m_conv_b, v_conv_b),
              (g_final.reshape(1, d), m_g_final.reshape(1, d), v_g_final.reshape(1, d))]
    partials = [dg_attn, g_rpb, g_sink.reshape(sw_sink.shape), dg_na, dg_sw, dg_ffn, gconv_b, dg_final,
                gconv_w, loss_part]
    mine = [None, _sum_slots([(recv_out, own_out)], "sum_w_out"),
            _sum_slots([(recv_up_top, own_up_top), (recv_up_bot, own_up_bot)], "sum_w_up"),
            _sum_slots([(recv_down, own_down)], "sum_w_down")]
    small, (recv_in, own_in, *theirs) = _small_sums(
        partials, dmod, _Riders([_Rider("scatter", [blocks(gw_in_b, IN_WIDTH)], [blocks(gw_in_t, IN_WIDTH)]),
                                 _Rider("swap", mine[1:])]))
    g_conv_w_full, loss_sum, g_b_ada, dmod_all = small[len(states):]
    r_small = _small_adamw(states + [(b_ada, m_b_ada, v_b_ada)], small[:len(states)] + [g_b_ada])
    loss = loss_sum[0, 0]
    mine[0] = _sum_slots([(recv_in, own_in)], "sum_w_in")
    theirs = _ride_alone(_Rider("swap", mine[:1]), "swap_sibling") + theirs
    dmod_rows = jnp.pad(dmod_all, ((0, 0), (0, SUBLANES - batch), (0, 0))).reshape(N_DEV * SUBLANES, 6 * d)
    ncol = w_ada.shape[2]
    g_w_ada = _ada_weight_grad(sc_all, lax.dynamic_slice(dmod_rows, (0, shard * ncol), (N_DEV * SUBLANES, ncol)))
    cshard = conv_w.shape[2]
    g_conv_w = lax.dynamic_slice(g_conv_w_full, (0, shard * cshard), (3, cshard))

    def big(w, m, v, g_parts, name):
        shape = w.shape
        outs = _adamw(w[0], g_parts, m[0], v[0], name)
        return [o.reshape(shape) for o in outs]

    r_w_ada = big(w_ada, m_w_ada, v_w_ada, [g_w_ada], "adamw_w_ada")
    r_w_in = [jnp.transpose(o).reshape(w_in.shape) for o in
              _adamw(jnp.transpose(w_in[0]), [mine[0], theirs[0]], jnp.transpose(m_w_in[0]), jnp.transpose(v_w_in[0]),
                     "adamw_w_in")]
    r_w_out = big(w_out, m_w_out, v_w_out, [mine[1], theirs[1]], "adamw_w_out")
    r_w_up = big(w_up, m_w_up, v_w_up, [mine[2], theirs[2]], "adamw_w_up")
    r_w_down = big(w_down, m_w_down, v_w_down, [mine[3], theirs[3]], "adamw_w_down")

    r_conv_w = big(conv_w, m_conv_w, v_conv_w, [g_conv_w], "adamw_conv_w")

    def pick(k):
        ga_, rpb_, sk_, gna_, gsw_, gf_, cb_, gfin_, b_ = [r[k] for r in r_small]
        return [r_w_ada[k], b_, ga_, r_w_in[k], rpb_.reshape(na_rpb.shape), sk_, gna_, gsw_, r_w_out[k], gf_,
                r_w_up[k], r_conv_w[k], cb_, r_w_down[k], gfin_.reshape(d)]

    return (loss, gx.reshape(batch, seq, d), *pick(0), *pick(1), *pick(2), *pick(3))
```

```python
import jax
import jax.numpy as jnp
import numpy as np
from jax import lax
from jax.experimental import pallas as pl
from jax.experimental.pallas import tpu as pltpu

F32 = jnp.float32
BF16 = jnp.bfloat16
MESH = pl.DeviceIdType.MESH

D_MODEL = 1024
HEAD_DIM = 64
NA_WIDTH = 512
SW_WIDTH = 512
SW_KV_WIDTH = 128
IN_WIDTH = 2304
D_FF = 2816
GRID_W = 64
NA_ROWS = 8
NA_COLS = 16
SW_BLOCK = 128
ROPE_THETA = 10000.0
EPS = 1e-6
NEG = -1e30
QK_SCALE = HEAD_DIM ** -0.5

ADAM_LR = 0.001
ADAM_B1 = 0.9
ADAM_B2 = 0.999
ADAM_EPS = 1e-08
ADAM_WD = 0.01
ADAM_STEP = 10

N_SHARD = 4
N_DEV = 8
LANES = 128
SUBLANES = 8
TOKEN_TILE = 512
FF_TILE = 256
CONV_CHUNK = 512
NA_GROUP = 8
SW_GROUP_BLOCKS = 8
VMEM_BIG = 56 * 1024 * 1024


def _mm(a, b):
    return jnp.dot(a, b, preferred_element_type=F32)


def _mm_nt(a, b):
    return lax.dot_general(a, b, (((1,), (1,)), ((), ())), preferred_element_type=F32)


def _mm_tn(a, b):
    return lax.dot_general(a, b, (((0,), (0,)), ((), ())), preferred_element_type=F32)


def _cparams(sem=None, vmem=None):
    kw = {}
    if sem is not None:
        kw["dimension_semantics"] = sem
    if vmem is not None:
        kw["vmem_limit_bytes"] = vmem
    return pltpu.CompilerParams(**kw)


def _resident(shape):
    return pl.BlockSpec(shape, lambda i: (0,) * len(shape), pipeline_mode=pl.Buffered(1))


def _sigmoid(x):
    return 1.0 / (1.0 + jnp.exp(-x))


def _rms_stats(x):
    r = lax.rsqrt(jnp.mean(x * x, axis=-1, keepdims=True) + EPS)
    return r, x * r


def _rms_bwd(dxn, xn, r):
    return r * (dxn - xn * jnp.mean(dxn * xn, axis=-1, keepdims=True))


def _my_pos():
    return lax.axis_index("x"), lax.axis_index("y"), lax.axis_index("c")


def _flip(v, bit):
    return 1 - v if bit else v


def _ada_forward(c8, w_ada, b_ada, rider):
    d = c8.shape[1]
    ncol = w_ada.shape[1]

    def body(c_ref, w_ref, b_ref, mod_ref, sc_ref, m_scr, mod_buf, ssem, rsem, ssem2, rsem2):
        x, y, c = _my_pos()
        me = 4 * x + 2 * y + c
        shard = 2 * x + y
        cv = c_ref[...]
        my_rows = pl.ds(pl.multiple_of(me * SUBLANES, SUBLANES), SUBLANES)
        sc_ref[my_rows, :] = cv * _sigmoid(cv)

        def copy1(k):
            peer = (_flip(x, (k >> 2) & 1), _flip(y, (k >> 1) & 1), _flip(c, k & 1))
            return pltpu.make_async_remote_copy(
                src_ref=sc_ref.at[my_rows, :], dst_ref=sc_ref.at[my_rows, :],
                send_sem=ssem.at[k - 1], recv_sem=rsem.at[k - 1], device_id=peer, device_id_type=MESH)

        sends = [copy1(k) for k in range(1, N_DEV)]
        for cp in sends:
            cp.start()
        for cp in sends:
            cp.wait_recv()
        m_scr[...] = _mm(sc_ref[...].astype(BF16), w_ref[...].astype(BF16))

        def copy2(k):
            px, py = _flip(x, (k >> 1) & 1), _flip(y, k & 1)
            rows = pl.ds(pl.multiple_of((4 * px + 2 * py + c) * SUBLANES, SUBLANES), SUBLANES)
            return pltpu.make_async_remote_copy(
                src_ref=m_scr.at[rows, :], dst_ref=mod_buf.at[shard],
                send_sem=ssem2.at[k - 1], recv_sem=rsem2.at[k - 1], device_id=(px, py, c), device_id_type=MESH)

        sends2 = [copy2(k) for k in range(1, N_SHARD)]
        for cp in sends2:
            cp.start()
        mod_buf[shard] = m_scr[my_rows, :]
        for cp in sends2:
            cp.wait_recv()
        for s in range(N_SHARD):
            mod_ref[:, s * ncol:(s + 1) * ncol] = mod_buf[s] + b_ref[:, s * ncol:(s + 1) * ncol]
        for cp in sends + sends2:
            cp.wait_send()

    vm = pl.BlockSpec(memory_space=pltpu.VMEM)
    return _hosted(
        body, rider, name="ada_forward", grid=(),
        out_shape=(jax.ShapeDtypeStruct((SUBLANES, N_SHARD * ncol), F32),
                   jax.ShapeDtypeStruct((N_DEV * SUBLANES, d), F32)),
        in_specs=[vm, vm, vm], out_specs=(vm, vm),
        scratch_shapes=[pltpu.VMEM((N_DEV * SUBLANES, ncol), F32), pltpu.VMEM((N_SHARD, SUBLANES, ncol), F32),
                        pltpu.SemaphoreType.DMA((N_DEV - 1,)), pltpu.SemaphoreType.DMA((N_DEV - 1,)),
                        pltpu.SemaphoreType.DMA((N_SHARD - 1,)), pltpu.SemaphoreType.DMA((N_SHARD - 1,))],
        compiler_params=_cparams(vmem=VMEM_BIG), args=[c8, w_ada, b_ada])


class _Rider:
    def __init__(self, kind, srcs, owns=()):
        self.kind, self.srcs, self.owns = kind, list(srcs), list(owns)
        n = len(self.srcs)
        sds = jax.ShapeDtypeStruct
        dma = pltpu.SemaphoreType.DMA
        if kind == "gather":
            self.out_shapes = [sds((N_SHARD,) + s.shape, s.dtype) for s in self.srcs]
            self.sems = [dma((n, N_SHARD - 1)), dma((n, N_SHARD - 1)), dma((n, N_SHARD - 1)), dma((n, N_SHARD - 1)),
                         dma((n,)), dma((n,))]
        elif kind == "scatter":
            self.out_shapes = ([sds((N_SHARD - 1,) + s.shape[1:], s.dtype) for s in self.srcs]
                               + [sds(o.shape[1:], o.dtype) for o in self.owns])
            m = max(len(self.owns), 1)
            self.sems = [dma((n, N_SHARD - 1)), dma((n, N_SHARD - 1)), dma((m,)), dma((m,))]
        else:
            self.out_shapes = [sds(s.shape, s.dtype) for s in self.srcs]
            self.sems = [dma((n,)), dma((n,))]

    @property
    def inputs(self):
        return self.srcs + self.owns

    def _halved(self, i):
        a = self.srcs[i]
        tile_rows = SUBLANES * (4 // jnp.dtype(a.dtype).itemsize)
        return self.kind == "gather" and a.shape[0] % (2 * tile_rows) == 0

    def copies(self, ins, outs, sems):
        n = len(self.srcs)
        x, y, c = _my_pos()
        shard = 2 * x + y
        remote, relay = [], []
        if self.kind == "swap":
            ssem, rsem = sems
            for i in range(n):
                remote.append(pltpu.make_async_remote_copy(
                    src_ref=ins[i], dst_ref=outs[i], send_sem=ssem.at[i], recv_sem=rsem.at[i],
                    device_id=(x, y, 1 - c), device_id_type=MESH))
            return remote, relay
        if self.kind == "gather":
            ssem, rsem, ssem2, rsem2, sib_s, sib_r = sems
        else:
            ssem, rsem, sib_s, sib_r = sems
        for i in range(n):
            if self.kind == "gather":
                remote.append(pltpu.make_async_remote_copy(
                    src_ref=ins[i], dst_ref=outs[i].at[shard], send_sem=sib_s.at[i], recv_sem=sib_r.at[i],
                    device_id=(x, y, 1 - c), device_id_type=MESH))
                half = ins[i].shape[0] // 2
                mine = pl.ds(pl.multiple_of(c * half, half), half) if self._halved(i) else None
            for k in range(1, N_SHARD):
                px, py = _flip(x, (k >> 1) & 1), _flip(y, k & 1)
                if self.kind == "gather":
                    src, dst = ins[i], outs[i].at[shard]
                    if mine is not None:
                        src, dst = src.at[mine], dst.at[mine]
                        got = outs[i].at[2 * px + py].at[mine]
                        relay.append(pltpu.make_async_remote_copy(
                            src_ref=got, dst_ref=got, send_sem=ssem2.at[i, k - 1], recv_sem=rsem2.at[i, k - 1],
                            device_id=(x, y, 1 - c), device_id_type=MESH))
                else:
                    src, dst = ins[i].at[2 * px + py], outs[i].at[k - 1]
                remote.append(pltpu.make_async_remote_copy(
                    src_ref=src, dst_ref=dst, send_sem=ssem.at[i, k - 1], recv_sem=rsem.at[i, k - 1],
                    device_id=(px, py, c), device_id_type=MESH))
        if self.kind == "scatter":
            for i in range(len(self.owns)):
                remote.append(pltpu.make_async_remote_copy(
                    src_ref=ins[n + i].at[shard], dst_ref=outs[n + i], send_sem=sib_s.at[i], recv_sem=sib_r.at[i],
                    device_id=(x, y, 1 - c), device_id_type=MESH))
        return remote, relay

    def start(self, ins, outs, sems):
        remote, _ = self.copies(ins, outs, sems)
        for cp in remote:
            cp.start()

    def wait(self, ins, outs, sems):
        remote, relay = self.copies(ins, outs, sems)
        for cp in remote:
            cp.wait_recv()
        for cp in relay:
            cp.start()
        for cp in relay:
            cp.wait_recv()
        for cp in remote + relay:
            cp.wait_send()


def _hosted(body, rider, *, name, grid, out_shape, in_specs, out_specs, scratch_shapes, compiler_params, args):
    out_shape, out_specs = list(out_shape), list(out_specs)
    if rider is None:
        outs = pl.pallas_call(body, name=name, grid=grid, out_shape=tuple(out_shape), in_specs=list(in_specs),
                              out_specs=tuple(out_specs), scratch_shapes=list(scratch_shapes),
                              compiler_params=compiler_params)(*args)
        return list(outs), []
    n_in, n_out, n_scr = len(in_specs), len(out_shape), len(scratch_shapes)
    nr_in, nr_out = len(rider.inputs), len(rider.out_shapes)
    n_steps = 1
    for size in grid:
        n_steps *= size

    def full(*refs):
        ins, refs = refs[:n_in], refs[n_in:]
        r_in, refs = refs[:nr_in], refs[nr_in:]
        outs, refs = refs[:n_out], refs[n_out:]
        r_out, refs = refs[:nr_out], refs[nr_out:]
        scr, sems = refs[:n_scr], refs[n_scr:]
        if grid:
            step = 0
            for ax, size in enumerate(grid):
                step = step * size + pl.program_id(ax)
            pl.when(step == 0)(lambda: rider.start(r_in, r_out, sems))
            body(*ins, *outs, *scr)
            pl.when(step == n_steps - 1)(lambda: rider.wait(r_in, r_out, sems))
        else:
            rider.start(r_in, r_out, sems)
            body(*ins, *outs, *scr)
            rider.wait(r_in, r_out, sems)

    hbm = pl.BlockSpec(memory_space=pl.ANY)
    res = pl.pallas_call(
        full, name=name, grid=grid, out_shape=tuple(out_shape + rider.out_shapes),
        in_specs=list(in_specs) + [hbm] * nr_in, out_specs=tuple(out_specs + [hbm] * nr_out),
        scratch_shapes=list(scratch_shapes) + rider.sems, compiler_params=compiler_params,
    )(*args, *rider.inputs)
    return list(res[:n_out]), list(res[n_out:])


def _rope_rot(t):
    w = t.shape[1]
    lane = lax.broadcasted_iota(jnp.int32, t.shape, 1)
    first = (lane % HEAD_DIM) < (HEAD_DIM // 2)
    return jnp.where(first, pltpu.roll(t, w - HEAD_DIM // 2, 1), pltpu.roll(t, HEAD_DIM // 2, 1))


def _in_proj(x, mod3, g_attn, w_in_t, cos_t, sin_t, seq, rider=None):
    t, d = x.shape
    tm = 2 * TOKEN_TILE
    per_seq = seq // tm
    rope_lo, rope_hi = 3 * NA_WIDTH, 3 * NA_WIDTH + SW_WIDTH + SW_KV_WIDTH
    n_rep = (rope_hi - rope_lo) // LANES

    def body(x_ref, mod_ref, g_ref, w_ref, cos_ref, sin_ref, h_ref, p_ref):
        r, xn = _rms_stats(x_ref[...])
        shift, scale = mod_ref[0, :, 0:d], mod_ref[0, :, d:2 * d]
        hb = ((xn * g_ref[...]) * (1.0 + scale) + shift).astype(BF16)
        h_ref[...] = hb
        p_ref[:, :rope_lo] = _mm_nt(hb, w_ref[:rope_lo, :]).astype(BF16)
        pr = _mm_nt(hb, w_ref[rope_lo:rope_hi, :])
        cos = jnp.concatenate([cos_ref[...]] * n_rep, axis=1)
        sin = jnp.concatenate([sin_ref[...]] * n_rep, axis=1)
        p_ref[:, rope_lo:rope_hi] = (pr * cos + _rope_rot(pr) * sin).astype(BF16)
        p_ref[:, rope_hi:] = _mm_nt(hb, w_ref[rope_hi:, :]).astype(BF16)

    return _hosted(
        body, rider, name="in_proj", grid=(t // tm,),
        out_shape=[jax.ShapeDtypeStruct((t, d), BF16), jax.ShapeDtypeStruct((t, IN_WIDTH), BF16)],
        in_specs=[pl.BlockSpec((tm, d), lambda i: (i, 0)),
                  pl.BlockSpec((1, 1, 6 * d), lambda i: (i // per_seq, 0, 0)),
                  pl.BlockSpec((1, d), lambda i: (0, 0)),
                  pl.BlockSpec((IN_WIDTH, d), lambda i: (0, 0)),
                  pl.BlockSpec((tm, LANES), lambda i: (i % per_seq, 0)),
                  pl.BlockSpec((tm, LANES), lambda i: (i % per_seq, 0))],
        out_specs=[pl.BlockSpec((tm, d), lambda i: (i, 0)), pl.BlockSpec((tm, IN_WIDTH), lambda i: (i, 0))],
        scratch_shapes=[], compiler_params=_cparams(("arbitrary",), VMEM_BIG),
        args=[x, mod3, g_attn, w_in_t, cos_t, sin_t])


def _na_bias_pattern():
    n_dc = 2 * NA_COLS - 1
    j = np.arange(GRID_W)[:, None]
    m = np.arange(GRID_W * LANES)[None, :]
    q, lane = m // LANES, m % LANES
    k = lane % GRID_W
    cs = np.clip(q - NA_COLS // 2, 0, GRID_W - NA_COLS)
    ok = (k >= cs) & (k < cs + NA_COLS)
    hit = ok & (j < 2 * n_dc) & (lane // GRID_W == j // n_dc) & (k - q + (NA_COLS - 1) == j % n_dc)
    return jnp.asarray(hit.astype(np.float32)), jnp.asarray(np.where(ok, 0.0, NEG).astype(np.float32))


def _na_bias_tiles(rows2, expand, mask):
    n, width = rows2.shape[0], expand.shape[1]
    q_step = 16
    step = q_step * LANES

    def body(r_ref, e_ref, m_ref, o_ref):
        flat = jnp.dot(r_ref[...], e_ref[...], precision=lax.Precision.HIGHEST,
                       preferred_element_type=F32) + m_ref[...]
        for qq in range(q_step):
            o_ref[:, qq, :] = flat[:, qq * LANES:(qq + 1) * LANES]

    return pl.pallas_call(
        body, name="na_bias_tiles", grid=(width // step,),
        out_shape=jax.ShapeDtypeStruct((n, GRID_W, LANES), F32),
        in_specs=[pl.BlockSpec(rows2.shape, lambda i: (0, 0)), pl.BlockSpec((expand.shape[0], step), lambda i: (0, i)),
                  pl.BlockSpec((1, step), lambda i: (0, i))],
        out_specs=pl.BlockSpec((n, q_step, LANES), lambda i: (0, i, 0)),
        compiler_params=_cparams(("arbitrary",)),
    )(rows2, expand, mask)


def _na_prepare(k_ref, v_ref, km, vm):
    lane = lax.broadcasted_iota(jnp.int32, k_ref.shape, 1)
    low = lane < HEAD_DIM
    kv = k_ref[...]
    vv = v_ref[...]
    zero = jnp.zeros_like(kv)
    km[0] = jnp.where(low, kv, zero)
    km[1] = jnp.where(low, zero, kv)
    vm[0] = jnp.where(low, vv, zero)
    vm[1] = jnp.where(low, zero, vv)


def _na_window(r, n_rows):
    rs = jnp.clip(r - NA_ROWS // 2, 0, n_rows - NA_ROWS)
    return rs, r - rs


def _na_pair_window(ref, wrows):
    return jnp.concatenate([ref[0, wrows, :], ref[1, wrows, :]], axis=0)


def _na_scores(q, k2, tp_ref, off):
    bias = jnp.concatenate([tp_ref[h, 2 * w - off + (NA_ROWS - 1)] for h in range(2) for w in range(NA_ROWS // 2)],
                           axis=1)
    return _mm_nt(q, k2) * QK_SCALE + bias


def _pair_lse_block(lse):
    lane = lax.broadcasted_iota(jnp.int32, (lse[0].shape[0], LANES), 1)
    return jnp.where(lane < HEAD_DIM, lse[0], lse[1])


def _pair_softmax(s):
    win = s.shape[1] // 2
    halves, lse = [], []
    for h in range(2):
        sh = s[:, h * win:(h + 1) * win]
        m = jnp.max(sh, axis=-1, keepdims=True)
        e = jnp.exp(sh - m)
        l = jnp.sum(e, axis=-1, keepdims=True)
        halves.append(e / l)
        lse.append(m + jnp.log(l))
    return jnp.concatenate(halves, axis=1), _pair_lse_block(lse)


def _pair_grad(w2, x, low):
    keys = w2.shape[1] // 2
    zero = jnp.zeros_like(x)
    low_x = low[:x.shape[0]]
    stacked = jnp.concatenate([w2[:, :keys], w2[:, keys:]], axis=0)
    diag = jnp.concatenate([jnp.where(low_x, x, zero), jnp.where(low_x, zero, x)], axis=0)
    return _mm_tn(stacked, diag)


def _pair_probs_from_lse(s, lse_block):
    win = s.shape[1] // 2
    return jnp.concatenate([jnp.exp(s[:, h * win:(h + 1) * win] - lse_block[:, h * HEAD_DIM:h * HEAD_DIM + 1])
                            for h in range(2)], axis=1)


def _na_forward(proj, tiles, batch, seq, rider=None):
    t = proj.shape[0]
    n_rows = seq // GRID_W
    n_pairs = NA_WIDTH // LANES
    win = NA_ROWS * GRID_W

    def body(q_ref, k_ref, v_ref, tp_ref, o_ref, lse_ref, km, vm):
        _na_prepare(k_ref, v_ref, km, vm)

        def scores(r):
            rs, off = _na_window(r, n_rows)
            rows = pl.ds(pl.multiple_of(r * GRID_W, GRID_W), GRID_W)
            wrows = pl.ds(pl.multiple_of(rs * GRID_W, GRID_W), win)
            return rows, wrows, _na_scores(q_ref[rows, :], _na_pair_window(km, wrows), tp_ref, off)

        def finish(rows, wrows, s):
            p, lse = _pair_softmax(s)
            lse_ref[rows, :] = lse
            o_ref[rows, :] = _mm(p.astype(BF16), _na_pair_window(vm, wrows))

        def row_group(i, carry):
            for state in [scores(NA_GROUP * i + j) for j in range(NA_GROUP)]:
                finish(*state)
            return carry

        lax.fori_loop(0, n_rows // NA_GROUP, row_group, 0)

    return _hosted(
        body, rider, name="na_forward", grid=(batch, n_pairs),
        out_shape=[jax.ShapeDtypeStruct((t, NA_WIDTH), F32), jax.ShapeDtypeStruct((t, NA_WIDTH), F32)],
        in_specs=[pl.BlockSpec((seq, LANES), lambda b, p: (b, p)),
                  pl.BlockSpec((seq, LANES), lambda b, p: (b, n_pairs + p)),
                  pl.BlockSpec((seq, LANES), lambda b, p: (b, 2 * n_pairs + p)),
                  pl.BlockSpec((2, 2 * NA_ROWS - 2, GRID_W, LANES), lambda b, p: (p, 0, 0, 0))],
        out_specs=[pl.BlockSpec((seq, LANES), lambda b, p: (b, p)), pl.BlockSpec((seq, LANES), lambda b, p: (b, p))],
        scratch_shapes=[pltpu.VMEM((2, seq, LANES), BF16), pltpu.VMEM((2, seq, LANES), BF16)],
        compiler_params=_cparams(("arbitrary", "arbitrary")), args=[proj, proj, proj, tiles])


def _sw_prepare(kv_ref, g, dst_lo, dst_hi, seq):
    lane = lax.broadcasted_iota(jnp.int32, kv_ref.shape, 1)
    mine = (lane // HEAD_DIM) == g
    kg = jnp.where(mine, kv_ref[...].astype(F32), 0.0)
    kr = pltpu.roll(kg, HEAD_DIM, 1)
    first = g == 0
    zero = jnp.zeros((SW_BLOCK, LANES), BF16)
    for dst, val in ((dst_lo, jnp.where(first, kg, kr)), (dst_hi, jnp.where(first, kr, kg))):
        dst[0:SW_BLOCK, :] = zero
        dst[SW_BLOCK:SW_BLOCK + seq, :] = val.astype(BF16)
        dst[SW_BLOCK + seq:, :] = zero


def _sw_mask(n, seq):
    qi = lax.broadcasted_iota(jnp.int32, (SW_BLOCK, 3 * SW_BLOCK), 0)
    kj = lax.broadcasted_iota(jnp.int32, (SW_BLOCK, 3 * SW_BLOCK), 1)
    kpos = n * SW_BLOCK - SW_BLOCK + kj
    return (jnp.abs(qi + SW_BLOCK - kj) <= SW_BLOCK) & (kpos >= 0) & (kpos < seq)


def _sw_probs(s2, ok, sinks):
    band = s2.shape[1] // 2
    halves, lse = [], []
    for i in range(2):
        s = jnp.where(ok, s2[:, i * band:(i + 1) * band], NEG)
        m = jnp.maximum(jnp.max(s, axis=-1, keepdims=True), sinks[i])
        p = jnp.exp(s - m)
        den = jnp.sum(p, axis=-1, keepdims=True) + jnp.exp(sinks[i] - m)
        halves.append(p / den)
        lse.append(m + jnp.log(den))
    return jnp.concatenate(halves, axis=1), _pair_lse_block(lse)


def _sw_probs_from_lse(s2, ok, sinks, lse_block):
    band = s2.shape[1] // 2
    halves, sink_p = [], []
    for i in range(2):
        lse = lse_block[:, i * HEAD_DIM:i * HEAD_DIM + 1]
        halves.append(jnp.exp(jnp.where(ok, s2[:, i * band:(i + 1) * band], NEG) - lse))
        sink_p.append(jnp.exp(sinks[i] - lse))
    return jnp.concatenate(halves, axis=1), sink_p


def _sw_forward(proj, sink, batch, seq, rider=None):
    t = proj.shape[0]
    n_pairs = SW_WIDTH // LANES
    q_blk = 3 * NA_WIDTH // LANES
    k_blk = q_blk + n_pairs
    n_blocks = seq // SW_BLOCK
    pad = seq + 2 * SW_BLOCK

    def body(sink_ref, q_ref, k_ref, v_ref, o_ref, lse_ref, k_lo, k_hi, v_lo, v_hi):
        hp = pl.program_id(1)
        g = hp // 2

        @pl.when(hp % 2 == 0)
        def _():
            _sw_prepare(k_ref, g, k_lo, k_hi, seq)
            _sw_prepare(v_ref, g, v_lo, v_hi, seq)

        sinks = (sink_ref[2 * hp], sink_ref[2 * hp + 1])

        def scores(n):
            rows = pl.ds(pl.multiple_of(n * SW_BLOCK, SW_BLOCK), SW_BLOCK)
            wrows = pl.ds(pl.multiple_of(n * SW_BLOCK, SW_BLOCK), 3 * SW_BLOCK)
            k2 = jnp.concatenate([k_lo[wrows, :], k_hi[wrows, :]], axis=0)
            return n, rows, wrows, _mm_nt(q_ref[rows, :], k2) * QK_SCALE

        def finish(n, rows, wrows, s2):
            p, lse = _sw_probs(s2, _sw_mask(n, seq), sinks)
            lse_ref[rows, :] = lse
            v2 = jnp.concatenate([v_lo[wrows, :], v_hi[wrows, :]], axis=0)
            o_ref[rows, :] = _mm(p.astype(BF16), v2)

        def block_group(i, carry):
            for state in [scores(SW_GROUP_BLOCKS * i + j) for j in range(SW_GROUP_BLOCKS)]:
                finish(*state)
            return carry

        lax.fori_loop(0, n_blocks // SW_GROUP_BLOCKS, block_group, 0)

    return _hosted(
        body, rider, name="sw_forward", grid=(batch, n_pairs),
        out_shape=[jax.ShapeDtypeStruct((t, SW_WIDTH), F32), jax.ShapeDtypeStruct((t, SW_WIDTH), F32)],
        in_specs=[pl.BlockSpec(memory_space=pltpu.SMEM),
                  pl.BlockSpec((seq, LANES), lambda b, p: (b, q_blk + p)),
                  pl.BlockSpec((seq, LANES), lambda b, p: (b, k_blk)),
                  pl.BlockSpec((seq, LANES), lambda b, p: (b, k_blk + 1))],
        out_specs=[pl.BlockSpec((seq, LANES), lambda b, p: (b, p)), pl.BlockSpec((seq, LANES), lambda b, p: (b, p))],
        scratch_shapes=[pltpu.VMEM((pad, LANES), BF16)] * 4,
        compiler_params=_cparams(("arbitrary", "arbitrary")), args=[sink, proj, proj, proj])


def _out_proj(oa, ob, g_na, g_sw, w_out, x, mod3, g_ffn, seq):
    t, d = x.shape
    tm = TOKEN_TILE
    per_seq = seq // tm

    def body(oa_ref, ob_ref, gna_ref, gsw_ref, w_ref, x_ref, mod_ref, gf_ref, oab_ref, mix_ref, x1_ref, h2_ref):
        _, na = _rms_stats(oa_ref[...])
        _, nb = _rms_stats(ob_ref[...])
        oab = jnp.concatenate([na * gna_ref[...], nb * gsw_ref[...]], axis=1).astype(BF16)
        oab_ref[...] = oab
        mix = _mm(oab, w_ref[...])
        mix_ref[...] = mix
        gate_a = mod_ref[0, :, 2 * d:3 * d]
        shift_f, scale_f = mod_ref[0, :, 3 * d:4 * d], mod_ref[0, :, 4 * d:5 * d]
        x1 = x_ref[...] + gate_a * mix
        x1_ref[...] = x1
        _, xn = _rms_stats(x1)
        h2_ref[...] = ((xn * gf_ref[...]) * (1.0 + scale_f) + shift_f).astype(BF16)

    tile = lambda w: pl.BlockSpec((tm, w), lambda i: (i, 0))
    vec = lambda w: pl.BlockSpec((1, w), lambda i: (0, 0))
    return pl.pallas_call(
        body, name="out_proj", grid=(t // tm,),
        out_shape=(jax.ShapeDtypeStruct((t, d), BF16), jax.ShapeDtypeStruct((t, d), F32),
                   jax.ShapeDtypeStruct((t, d), F32), jax.ShapeDtypeStruct((t, d), BF16)),
        in_specs=[tile(NA_WIDTH), tile(SW_WIDTH), vec(NA_WIDTH), vec(SW_WIDTH),
                  pl.BlockSpec((d, d), lambda i: (0, 0)), tile(d),
                  pl.BlockSpec((1, 1, 6 * d), lambda i: (i // per_seq, 0, 0)), vec(d)],
        out_specs=(tile(d), tile(d), tile(d), tile(d)),
        compiler_params=_cparams(("arbitrary",), VMEM_BIG),
    )(oa, ob, g_na, g_sw, w_out, x, mod3, g_ffn)


def _up_proj(h2, w_up_halves, rider=None):
    t, d = h2.shape
    tm = 2 * TOKEN_TILE
    w_a, w_b = w_up_halves
    half, wcol = w_a.shape[1], w_a.shape[2]

    def body(h_ref, wa_ref, wb_ref, u_ref):
        u_ref[0] = (_mm(h_ref[:, :half], wa_ref[0]) + _mm(h_ref[:, half:], wb_ref[0])).astype(BF16)

    w_spec = pl.BlockSpec((1, half, wcol), lambda j, i: (j, 0, 0))
    return _hosted(
        body, rider, name="up_proj", grid=(N_SHARD, t // tm),
        out_shape=[jax.ShapeDtypeStruct((2, t, D_FF), BF16)],
        in_specs=[pl.BlockSpec((tm, d), lambda j, i: (i, 0)), w_spec, w_spec],
        out_specs=[pl.BlockSpec((1, tm, wcol), lambda j, i: (j // 2, i, j % 2))],
        scratch_shapes=[], compiler_params=_cparams(("arbitrary", "arbitrary"), VMEM_BIG), args=[h2, w_a, w_b])


def _taps_chunk(load, s, rows, seq):
    halo = 2 * SUBLANES
    cur = load(s, rows)
    above = load(pl.multiple_of(jnp.maximum(s - halo, 0), halo), halo)
    below = load(pl.multiple_of(jnp.minimum(s + rows, seq - halo), halo), halo)
    up = jnp.where(s > 0, above[halo - 1:halo, :], 0.0)
    dn = jnp.where(s + rows < seq, below[0:1, :], 0.0)
    row = lax.broadcasted_iota(jnp.int32, cur.shape, 0)
    prev = jnp.where(row == 0, up, pltpu.roll(cur, 1, 0))
    nxt = jnp.where(row == rows - 1, dn, pltpu.roll(cur, rows - 1, 0))
    return cur, prev, nxt


def _conv_gate(u, conv_w, conv_b, batch, seq, rider=None):
    t = u.shape[1]
    cw = FF_TILE
    rows = CONV_CHUNK

    def body(u_ref, w_ref, b_ref, a_ref):
        def chunk(i, carry):
            s = pl.multiple_of(i * rows, rows)
            gt, prev, nxt = _taps_chunk(lambda at, n: u_ref[1, pl.ds(at, n), :].astype(F32), s, rows, seq)
            gc = prev * w_ref[0:1, :] + gt * w_ref[1:2, :] + nxt * w_ref[2:3, :] + b_ref[...]
            a_ref[pl.ds(s, rows), :] = ((gc * _sigmoid(gc)) * u_ref[0, pl.ds(s, rows), :].astype(F32)).astype(BF16)
            return carry

        lax.fori_loop(0, seq // rows, chunk, 0)

    return _hosted(
        body, rider, name="conv_gate", grid=(batch, D_FF // cw),
        out_shape=[jax.ShapeDtypeStruct((t, D_FF), BF16)],
        in_specs=[pl.BlockSpec((2, seq, cw), lambda b, j: (0, b, j)),
                  pl.BlockSpec((3, cw), lambda b, j: (0, j)), pl.BlockSpec((1, cw), lambda b, j: (0, j))],
        out_specs=[pl.BlockSpec((seq, cw), lambda b, j: (b, j))], scratch_shapes=[],
        compiler_params=_cparams(("arbitrary", "arbitrary"), VMEM_BIG), args=[u, conv_w, conv_b])


def _down_and_loss(a, w_down, x1, mod3, g_final, target, seq):
    t, d = x1.shape
    tm = TOKEN_TILE
    per_seq = seq // tm
    batch = t // seq

    def body(a_ref, w_ref, x1_ref, mod_ref, g_ref, tgt_ref, dx2_ref, dffn_ref, loss_ref, dgate_ref, dg_ref):
        i = pl.program_id(0)
        f = _mm(a_ref[...], w_ref[...])
        gate_f = mod_ref[0, :, 5 * d:6 * d]
        x2 = x1_ref[...] + gate_f * f
        r, xn = _rms_stats(x2)
        err = xn * g_ref[...] - tgt_ref[...]
        part = 0.5 * jnp.sum(jnp.mean(err * err, axis=-1, keepdims=True))
        dy = err / d
        dx2 = _rms_bwd(dy * g_ref[...], xn, r)
        dx2_ref[...] = dx2
        dffn_ref[...] = (dx2 * gate_f).astype(BF16)

        @pl.when(i == 0)
        def _():
            loss_ref[...] = jnp.zeros_like(loss_ref)
            dg_ref[...] = jnp.zeros_like(dg_ref)

        @pl.when(i % per_seq == 0)
        def _():
            dgate_ref[...] = jnp.zeros_like(dgate_ref)

        loss_ref[...] += part
        dg_ref[...] += jnp.sum(dy * xn, axis=0, keepdims=True)
        dgate_ref[0] += jnp.sum(dx2 * f, axis=0, keepdims=True)

    tile = lambda w: pl.BlockSpec((tm, w), lambda i: (i, 0))
    return pl.pallas_call(
        body, name="down_loss", grid=(t // tm,),
        out_shape=(jax.ShapeDtypeStruct((t, d), F32), jax.ShapeDtypeStruct((t, d), BF16),
                   jax.ShapeDtypeStruct((SUBLANES, LANES), F32), jax.ShapeDtypeStruct((batch, 1, d), F32),
                   jax.ShapeDtypeStruct((1, d), F32)),
        in_specs=[tile(D_FF), _resident((D_FF, d)), tile(d),
                  pl.BlockSpec((1, 1, 6 * d), lambda i: (i // per_seq, 0, 0)),
                  pl.BlockSpec((1, d), lambda i: (0, 0)), tile(d)],
        out_specs=(tile(d), tile(d), pl.BlockSpec((SUBLANES, LANES), lambda i: (0, 0)),
                   pl.BlockSpec((1, 1, d), lambda i: (i // per_seq, 0, 0)), pl.BlockSpec((1, d), lambda i: (0, 0))),
        compiler_params=_cparams(("arbitrary",), VMEM_BIG),
    )(a, w_down, x1, mod3, g_final, target)


def _down_weight_grad(a, dffn):
    t, dff = a.shape
    d = dffn.shape[1]
    tk = 2 * TOKEN_TILE
    n_k = t // tk

    def body(a_ref, df_ref, g_ref, gb_ref):
        k = pl.program_id(0)

        @pl.when(k == 0)
        def _():
            g_ref[...] = jnp.zeros_like(g_ref)

        g_ref[...] += _mm_tn(a_ref[...], df_ref[...])

        @pl.when(k == n_k - 1)
        def _():
            gb_ref[...] = g_ref[...].astype(BF16)

    whole = _resident((dff, d))
    return pl.pallas_call(
        body, name="down_weight_grad", grid=(n_k,),
        out_shape=(jax.ShapeDtypeStruct((dff, d), F32), jax.ShapeDtypeStruct((dff, d), BF16)),
        in_specs=[pl.BlockSpec((tk, dff), lambda k: (k, 0)), pl.BlockSpec((tk, d), lambda k: (k, 0))],
        out_specs=(whole, whole),
        compiler_params=_cparams(("arbitrary",), VMEM_BIG),
    )(a, dffn)


def _ffn_backward(dffn, w_down, u, conv_w, conv_b, batch, seq, rider=None):
    t, d = dffn.shape
    cw = FF_TILE
    rows = CONV_CHUNK

    def body(df_ref, wd_ref, u_ref, w_ref, b_ref, du_ref, gcw_ref, gcb_ref, da_scr, dgc_scr):
        b = pl.program_id(1)
        da_scr[...] = _mm_nt(df_ref[...], wd_ref[...])

        @pl.when(b == 0)
        def _():
            gcw_ref[...] = jnp.zeros_like(gcw_ref)
            gcb_ref[...] = jnp.zeros_like(gcb_ref)

        def fold(v):
            return jnp.sum(v.reshape(rows // SUBLANES, SUBLANES, cw), axis=0)

        def chunk(i, carry):
            s = pl.multiple_of(i * rows, rows)
            here = pl.ds(s, rows)
            gt, prev, nxt = _taps_chunk(lambda at, n: u_ref[1, pl.ds(at, n), :].astype(F32), s, rows, seq)
            val, da = u_ref[0, here, :].astype(F32), da_scr[here, :]
            gc = prev * w_ref[0:1, :] + gt * w_ref[1:2, :] + nxt * w_ref[2:3, :] + b_ref[...]
            sg = _sigmoid(gc)
            sl = gc * sg
            du_ref[0, here, :] = (da * sl).astype(BF16)
            dgc = (da * val) * (sg * (1.0 + gc * (1.0 - sg)))
            dgc_scr[here, :] = dgc
            cb, c0, c1, c2 = carry
            return cb + fold(dgc), c0 + fold(dgc * prev), c1 + fold(dgc * gt), c2 + fold(dgc * nxt)

        zero = jnp.zeros((SUBLANES, cw), F32)
        cb, c0, c1, c2 = lax.fori_loop(0, seq // rows, chunk, (zero, zero, zero, zero))
        gcb_ref[...] += jnp.sum(cb, axis=0, keepdims=True)
        gcw_ref[0:1, :] += jnp.sum(c0, axis=0, keepdims=True)
        gcw_ref[1:2, :] += jnp.sum(c1, axis=0, keepdims=True)
        gcw_ref[2:3, :] += jnp.sum(c2, axis=0, keepdims=True)

        def chunk2(i, carry):
            s = pl.multiple_of(i * rows, rows)
            dgc, dprev, dnxt = _taps_chunk(lambda at, n: dgc_scr[pl.ds(at, n), :], s, rows, seq)
            du_ref[1, pl.ds(s, rows), :] = (dnxt * w_ref[0:1, :] + dgc * w_ref[1:2, :]
                                            + dprev * w_ref[2:3, :]).astype(BF16)
            return carry

        lax.fori_loop(0, seq // rows, chunk2, 0)

    return _hosted(
        body, rider, name="ffn_backward", grid=(D_FF // cw, batch),
        out_shape=[jax.ShapeDtypeStruct((2, t, D_FF), BF16),
                   jax.ShapeDtypeStruct((3, D_FF), F32), jax.ShapeDtypeStruct((1, D_FF), F32)],
        in_specs=[pl.BlockSpec((seq, d), lambda j, b: (b, 0)), pl.BlockSpec((cw, d), lambda j, b: (j, 0)),
                  pl.BlockSpec((2, seq, cw), lambda j, b: (0, b, j)),
                  pl.BlockSpec((3, cw), lambda j, b: (0, j)), pl.BlockSpec((1, cw), lambda j, b: (0, j))],
        out_specs=[pl.BlockSpec((2, seq, cw), lambda j, b: (0, b, j)),
                   pl.BlockSpec((3, cw), lambda j, b: (0, j)), pl.BlockSpec((1, cw), lambda j, b: (0, j))],
        scratch_shapes=[pltpu.VMEM((seq, cw), F32), pltpu.VMEM((seq, cw), F32)],
        compiler_params=_cparams(("arbitrary", "arbitrary"), VMEM_BIG), args=[dffn, w_down, u, conv_w, conv_b])


def _up_backward(du, w_up, x1, mod3, g_ffn, dx2, mix, seq, rider=None):
    _, t, _ = du.shape
    d = x1.shape[1]
    tm = TOKEN_TILE
    per_seq = seq // tm
    batch = t // seq
    w_a, w_b = w_up
    half, wcol = w_a.shape[1], w_a.shape[2]

    def body(du_ref, wa_ref, wb_ref, x1_ref, mod_ref, g_ref, dx2_ref, mix_ref,
             dx1_ref, dmix_ref, dsh_ref, dsc_ref, dga_ref, dg_ref):
        i = pl.program_id(0)
        parts = []
        for w_ref in (wa_ref, wb_ref):
            acc = jnp.zeros((tm, half), F32)
            for j in range(N_SHARD):
                acc = acc + _mm_nt(du_ref[j // 2, :, (j % 2) * wcol:(j % 2 + 1) * wcol], w_ref[j])
            parts.append(acc)
        dh = jnp.concatenate(parts, axis=1)
        gate_a = mod_ref[0, :, 2 * d:3 * d]
        scale_f = mod_ref[0, :, 4 * d:5 * d]
        r, xn = _rms_stats(x1_ref[...])
        xg = xn * g_ref[...]
        dxg = dh * (1.0 + scale_f)
        dx1 = dx2_ref[...] + _rms_bwd(dxg * g_ref[...], xn, r)
        dx1_ref[...] = dx1
        dmix_ref[...] = (dx1 * gate_a).astype(BF16)

        @pl.when(i == 0)
        def _():
            dg_ref[...] = jnp.zeros_like(dg_ref)

        @pl.when(i % per_seq == 0)
        def _():
            dsh_ref[...] = jnp.zeros_like(dsh_ref)
            dsc_ref[...] = jnp.zeros_like(dsc_ref)
            dga_ref[...] = jnp.zeros_like(dga_ref)

        dg_ref[...] += jnp.sum(dxg * xn, axis=0, keepdims=True)
        dsh_ref[0] += jnp.sum(dh, axis=0, keepdims=True)
        dsc_ref[0] += jnp.sum(dh * xg, axis=0, keepdims=True)
        dga_ref[0] += jnp.sum(dx1 * mix_ref[...], axis=0, keepdims=True)

    tile = lambda w: pl.BlockSpec((tm, w), lambda i: (i, 0))
    per_b = pl.BlockSpec((1, 1, d), lambda i: (i // per_seq, 0, 0))
    small = jax.ShapeDtypeStruct((batch, 1, d), F32)
    return _hosted(
        body, rider, name="up_backward", grid=(t // tm,),
        out_shape=[jax.ShapeDtypeStruct((t, d), F32), jax.ShapeDtypeStruct((t, d), BF16), small, small, small,
                   jax.ShapeDtypeStruct((1, d), F32)],
        in_specs=[pl.BlockSpec((2, tm, D_FF), lambda i: (0, i, 0)),
                  _resident((N_SHARD, half, wcol)), _resident((N_SHARD, half, wcol)), tile(d),
                  pl.BlockSpec((1, 1, 6 * d), lambda i: (i // per_seq, 0, 0)),
                  pl.BlockSpec((1, d), lambda i: (0, 0)), tile(d), tile(d)],
        out_specs=[tile(d), tile(d), per_b, per_b, per_b, pl.BlockSpec((1, d), lambda i: (0, 0))],
        scratch_shapes=[], compiler_params=_cparams(("arbitrary",), VMEM_BIG),
        args=[du, w_a, w_b, x1, mod3, g_ffn, dx2, mix])


def _up_weight_grad(h2, du, rider=None):
    t, d = h2.shape
    tk = 2 * TOKEN_TILE
    wcol = D_FF // 2
    half = d // 2
    n_k = t // tk

    def body(h_ref, du_ref, ga_ref, gb_ref, ga16_ref, gb16_ref):
        k = pl.program_id(1)

        @pl.when(k == 0)
        def _():
            ga_ref[...] = jnp.zeros_like(ga_ref)
            gb_ref[...] = jnp.zeros_like(gb_ref)

        du = du_ref[0]
        ga_ref[0] += _mm_tn(h_ref[:, :half], du)
        gb_ref[0] += _mm_tn(h_ref[:, half:], du)

        @pl.when(k == n_k - 1)
        def _():
            ga16_ref[...] = ga_ref[...].astype(BF16)
            gb16_ref[...] = gb_ref[...].astype(BF16)

    g_spec = pl.BlockSpec((1, half, wcol), lambda j, k: (j, 0, 0))
    f32_out = jax.ShapeDtypeStruct((N_SHARD, half, wcol), F32)
    b16_out = jax.ShapeDtypeStruct((N_SHARD, half, wcol), BF16)
    return _hosted(
        body, rider, name="up_weight_grad", grid=(N_SHARD, n_k),
        out_shape=[f32_out, f32_out, b16_out, b16_out],
        in_specs=[pl.BlockSpec((tk, d), lambda j, k: (k, 0)),
                  pl.BlockSpec((1, tk, wcol), lambda j, k: (j // 2, k, j % 2))],
        out_specs=[g_spec, g_spec, g_spec, g_spec], scratch_shapes=[],
        compiler_params=_cparams(("arbitrary", "arbitrary"), VMEM_BIG), args=[h2, du])


def _out_backward(dmix, w_out, oab, oa, ob, g_na, g_sw):
    t, d = dmix.shape
    tm = 2 * TOKEN_TILE
    hw = NA_WIDTH

    def body(dm_ref, w_ref, oab_ref, oa_ref, ob_ref, gna_ref, gsw_ref,
             doa_ref, dob_ref, gw_ref, gwb_ref, dgna_ref, dgsw_ref):
        @pl.when(pl.program_id(0) == 0)
        def _():
            gw_ref[...] = jnp.zeros_like(gw_ref)
            dgna_ref[...] = jnp.zeros_like(dgna_ref)
            dgsw_ref[...] = jnp.zeros_like(dgsw_ref)

        dm = dm_ref[...]
        gw_ref[...] += _mm_tn(oab_ref[...], dm)

        @pl.when(pl.program_id(0) == t // tm - 1)
        def _():
            gwb_ref[...] = gw_ref[...].astype(BF16)

        do = _mm_nt(dm, w_ref[...])
        for raw_ref, g_ref, dst_ref, dg_ref, lo in ((oa_ref, gna_ref, doa_ref, dgna_ref, 0),
                                                     (ob_ref, gsw_ref, dob_ref, dgsw_ref, hw)):
            r, xn = _rms_stats(raw_ref[...])
            dpart = do[:, lo:lo + hw]
            dg_ref[...] += jnp.sum(dpart * xn, axis=0, keepdims=True)
            dst_ref[...] = _rms_bwd(dpart * g_ref[...], xn, r).astype(BF16)

    tile = lambda w: pl.BlockSpec((tm, w), lambda i: (i, 0))
    vec = lambda w: pl.BlockSpec((1, w), lambda i: (0, 0))
    return pl.pallas_call(
        body, name="out_backward", grid=(t // tm,),
        out_shape=(jax.ShapeDtypeStruct((t, hw), BF16), jax.ShapeDtypeStruct((t, hw), BF16),
                   jax.ShapeDtypeStruct((d, d), F32), jax.ShapeDtypeStruct((d, d), BF16),
                   jax.ShapeDtypeStruct((1, hw), F32), jax.ShapeDtypeStruct((1, hw), F32)),
        in_specs=[tile(d), pl.BlockSpec((d, d), lambda i: (0, 0)), tile(d), tile(hw), tile(hw), vec(hw), vec(hw)],
        out_specs=(tile(hw), tile(hw), pl.BlockSpec((d, d), lambda i: (0, 0)), pl.BlockSpec((d, d), lambda i: (0, 0)),
                   vec(hw), vec(hw)),
        compiler_params=_cparams(("arbitrary",), VMEM_BIG),
    )(dmix, w_out, oab, oa, ob, g_na, g_sw)


def _na_backward(proj, d_o, lse, tiles, batch, seq, rider=None):
    t = proj.shape[0]
    n_rows = seq // GRID_W
    n_pairs = NA_WIDTH // LANES
    win = NA_ROWS * GRID_W
    n_tiles = 2 * NA_ROWS - 2

    def body(q_ref, k_ref, v_ref, do_ref, lse_ref, tp_ref, dq_ref, dk_ref, dv_ref, dtp_ref, km, vm, dk_acc, dv_acc):
        @pl.when(pl.program_id(1) == 0)
        def _():
            dtp_ref[...] = jnp.zeros_like(dtp_ref)

        _na_prepare(k_ref, v_ref, km, vm)
        dk_acc[...] = jnp.zeros_like(dk_acc)
        dv_acc[...] = jnp.zeros_like(dv_acc)
        low = lax.broadcasted_iota(jnp.int32, (win, LANES), 1) < HEAD_DIM

        def scores(r):
            rs, off = _na_window(r, n_rows)
            rows = pl.ds(pl.multiple_of(r * GRID_W, GRID_W), GRID_W)
            wrows = pl.ds(pl.multiple_of(rs * GRID_W, GRID_W), win)
            q, do = q_ref[rows, :], do_ref[rows, :]
            k2 = _na_pair_window(km, wrows)
            s = _na_scores(q, k2, tp_ref, off)
            dp = _mm_nt(do, _na_pair_window(vm, wrows))
            return rows, wrows, off, q, do, k2, s, dp

        def finish(rows, wrows, off, q, do, k2, s, dp):
            p = _pair_probs_from_lse(s, lse_ref[rows, :])
            parts = []
            for h in range(2):
                ph, dph = p[:, h * win:(h + 1) * win], dp[:, h * win:(h + 1) * win]
                dsh = ph * (dph - jnp.sum(ph * dph, axis=-1, keepdims=True))
                for w in range(NA_ROWS // 2):
                    dtp_ref[h, 2 * w - off + (NA_ROWS - 1)] += dsh[:, w * LANES:(w + 1) * LANES]
                parts.append(dsh)
            dsb = (jnp.concatenate(parts, axis=1) * QK_SCALE).astype(BF16)
            dq_ref[rows, :] = _mm(dsb, k2).astype(BF16)
            dk_acc[wrows, :] += _pair_grad(dsb, q, low)
            dv_acc[wrows, :] += _pair_grad(p.astype(BF16), do, low)

        def row_group(i, carry):
            for state in [scores(NA_GROUP * i + j) for j in range(NA_GROUP)]:
                finish(*state)
            return carry

        lax.fori_loop(0, n_rows // NA_GROUP, row_group, 0)
        dk_ref[...] = dk_acc[...].astype(BF16)
        dv_ref[...] = dv_acc[...].astype(BF16)

    blk = lambda off: pl.BlockSpec((seq, LANES), lambda p, b: (b, off + p))
    out = jax.ShapeDtypeStruct((t, NA_WIDTH), BF16)
    return _hosted(
        body, rider, name="na_backward", grid=(n_pairs, batch),
        out_shape=[out, out, out, jax.ShapeDtypeStruct(tiles.shape, F32)],
        in_specs=[blk(0), blk(n_pairs), blk(2 * n_pairs), blk(0), blk(0),
                  pl.BlockSpec((2, n_tiles, GRID_W, LANES), lambda p, b: (p, 0, 0, 0))],
        out_specs=[blk(0), blk(0), blk(0), pl.BlockSpec((2, n_tiles, GRID_W, LANES), lambda p, b: (p, 0, 0, 0))],
        scratch_shapes=[pltpu.VMEM((2, seq, LANES), BF16), pltpu.VMEM((2, seq, LANES), BF16),
                        pltpu.VMEM((seq, LANES), F32), pltpu.VMEM((seq, LANES), F32)],
        compiler_params=_cparams(("arbitrary", "arbitrary")), args=[proj, proj, proj, d_o, lse, tiles])


def _na_bias_grad(dtiles, expand):
    n = dtiles.shape[0]

    def body(t_ref, e_ref, o_ref):
        flat = jnp.concatenate([t_ref[:, qq, :] for qq in range(GRID_W)], axis=1)
        o_ref[...] = lax.dot_general(flat, e_ref[...], (((1,), (1,)), ((), ())),
                                     precision=lax.Precision.HIGHEST, preferred_element_type=F32)

    return pl.pallas_call(
        body, name="na_bias_grad",
        out_shape=jax.ShapeDtypeStruct((n, expand.shape[0]), F32),
        compiler_params=_cparams(vmem=VMEM_BIG),
    )(dtiles, expand)


def _sw_backward(proj, d_o, lse, sink, batch, seq, rider=None):
    t = proj.shape[0]
    n_pairs = SW_WIDTH // LANES
    q_blk = 3 * NA_WIDTH // LANES
    k_blk = q_blk + n_pairs
    n_blocks = seq // SW_BLOCK
    pad = seq + 2 * SW_BLOCK

    def body(sink_ref, q_ref, k_ref, v_ref, do_ref, lse_ref, dq_ref, dk_ref, dv_ref, dsk_ref,
             k_lo, k_hi, v_lo, v_hi, dk_loc, dv_loc, dk_tot, dv_tot):
        hp = pl.program_id(1)
        g = hp // 2

        @pl.when(hp % 2 == 0)
        def _():
            _sw_prepare(k_ref, g, k_lo, k_hi, seq)
            _sw_prepare(v_ref, g, v_lo, v_hi, seq)
            dk_loc[...] = jnp.zeros_like(dk_loc)
            dv_loc[...] = jnp.zeros_like(dv_loc)

        @pl.when(hp == 0)
        def _():
            dk_tot[...] = jnp.zeros_like(dk_tot)
            dv_tot[...] = jnp.zeros_like(dv_tot)

        band = 3 * SW_BLOCK
        low = lax.broadcasted_iota(jnp.int32, (band, LANES), 1) < HEAD_DIM

        sinks = (sink_ref[2 * hp], sink_ref[2 * hp + 1])

        def scores(n):
            rows = pl.ds(pl.multiple_of(n * SW_BLOCK, SW_BLOCK), SW_BLOCK)
            wrows = pl.ds(pl.multiple_of(n * SW_BLOCK, SW_BLOCK), band)
            qb, do = q_ref[rows, :], do_ref[rows, :]
            k2 = jnp.concatenate([k_lo[wrows, :], k_hi[wrows, :]], axis=0)
            v2 = jnp.concatenate([v_lo[wrows, :], v_hi[wrows, :]], axis=0)
            return n, rows, wrows, qb, do, k2, _mm_nt(qb, k2) * QK_SCALE, _mm_nt(do, v2)

        def finish(sink_acc, n, rows, wrows, qb, do, k2, s2, dp):
            p, ps = _sw_probs_from_lse(s2, _sw_mask(n, seq), sinks, lse_ref[rows, :])
            parts, new = [], []
            for i in range(2):
                ph, dph = p[:, i * band:(i + 1) * band], dp[:, i * band:(i + 1) * band]
                delta = jnp.sum(ph * dph, axis=-1, keepdims=True)
                parts.append(ph * (dph - delta))
                new.append(sink_acc[i] - ps[i] * delta)
            dsb = (jnp.concatenate(parts, axis=1) * QK_SCALE).astype(BF16)
            dq_ref[rows, :] = _mm(dsb, k2)
            dk_loc[wrows, :] += _pair_grad(dsb, qb, low)
            dv_loc[wrows, :] += _pair_grad(p.astype(BF16), do, low)
            return tuple(new)

        def block_group(i, carry):
            for state in [scores(SW_GROUP_BLOCKS * i + j) for j in range(SW_GROUP_BLOCKS)]:
                carry = finish(carry, *state)
            return carry

        zero = jnp.zeros((SW_BLOCK, 1), F32)
        s0, s1 = lax.fori_loop(0, n_blocks // SW_GROUP_BLOCKS, block_group, (zero, zero))
        row = lax.broadcasted_iota(jnp.int32, (SUBLANES, LANES), 0)
        dsk_ref[0, 0] = jnp.where(row == 0, jnp.sum(s0), jnp.where(row == 1, jnp.sum(s1), 0.0))

        @pl.when(hp % 2 == 1)
        def _():
            lane_s = lax.broadcasted_iota(jnp.int32, (seq, LANES), 1)
            mine_g = (lane_s // HEAD_DIM) == g
            for loc, tot in ((dk_loc, dk_tot), (dv_loc, dv_tot)):
                part = loc[SW_BLOCK:SW_BLOCK + seq, :]
                tot[...] += jnp.where(mine_g, part + pltpu.roll(part, HEAD_DIM, 1), 0.0)

        @pl.when(hp == n_pairs - 1)
        def _():
            dk_ref[...] = dk_tot[...]
            dv_ref[...] = dv_tot[...].astype(BF16)

    return _hosted(
        body, rider, name="sw_backward", grid=(batch, n_pairs),
        out_shape=[jax.ShapeDtypeStruct((t, SW_WIDTH), F32), jax.ShapeDtypeStruct((t, LANES), F32),
                   jax.ShapeDtypeStruct((t, LANES), BF16), jax.ShapeDtypeStruct((batch, n_pairs, SUBLANES, LANES), F32)],
        in_specs=[pl.BlockSpec(memory_space=pltpu.SMEM),
                  pl.BlockSpec((seq, LANES), lambda b, p: (b, q_blk + p)),
                  pl.BlockSpec((seq, LANES), lambda b, p: (b, k_blk)),
                  pl.BlockSpec((seq, LANES), lambda b, p: (b, k_blk + 1)),
                  pl.BlockSpec((seq, LANES), lambda b, p: (b, p)), pl.BlockSpec((seq, LANES), lambda b, p: (b, p))],
        out_specs=[pl.BlockSpec((seq, LANES), lambda b, p: (b, p)), pl.BlockSpec((seq, LANES), lambda b, p: (b, 0)),
                   pl.BlockSpec((seq, LANES), lambda b, p: (b, 0)),
                   pl.BlockSpec((1, 1, SUBLANES, LANES), lambda b, p: (b, p, 0, 0))],
        scratch_shapes=[pltpu.VMEM((pad, LANES), BF16)] * 4 + [pltpu.VMEM((pad, LANES), F32)] * 2
        + [pltpu.VMEM((seq, LANES), F32)] * 2,
        compiler_params=_cparams(("arbitrary", "arbitrary")), args=[sink, proj, proj, proj, d_o, lse])


def _in_backward(dqkv_a, dq_b, dk_b, dv_b, w_in_t, h1, x, mod3, g_attn, dx1, cos_t, sin_t, seq):
    t, d = x.shape
    tm = TOKEN_TILE
    per_seq = seq // tm
    batch = t // seq
    dqa, dka, dva = dqkv_a
    n_q = SW_WIDTH // LANES

    def body(dqa_ref, dka_ref, dva_ref, dqb_ref, dkb_ref, dvb_ref, w_ref, h_ref, x_ref, mod_ref, g_ref, dx1_ref,
             cos_ref, sin_ref, dx_ref, gw_ref, gwb_ref, dsh_ref, dsc_ref, dg_ref):
        i = pl.program_id(0)

        @pl.when(i == 0)
        def _():
            gw_ref[...] = jnp.zeros_like(gw_ref)
            dg_ref[...] = jnp.zeros_like(dg_ref)

        @pl.when(i % per_seq == 0)
        def _():
            dsh_ref[...] = jnp.zeros_like(dsh_ref)
            dsc_ref[...] = jnp.zeros_like(dsc_ref)

        dr = jnp.concatenate([dqb_ref[...], dkb_ref[...]], axis=1)
        cos = jnp.concatenate([cos_ref[...]] * (n_q + 1), axis=1)
        sin = jnp.concatenate([sin_ref[...]] * (n_q + 1), axis=1)
        dr = dr * cos + _rope_rot(dr * sin)
        dproj = jnp.concatenate([dqa_ref[...], dka_ref[...], dva_ref[...], dr.astype(BF16), dvb_ref[...]], axis=1)
        gw_ref[...] += _mm_tn(dproj, h_ref[...])

        @pl.when(i == t // tm - 1)
        def _():
            gwb_ref[...] = gw_ref[...].astype(BF16)

        dh = _mm(dproj, w_ref[...])
        scale = mod_ref[0, :, d:2 * d]
        r, xn = _rms_stats(x_ref[...])
        xg = xn * g_ref[...]
        dxg = dh * (1.0 + scale)
        dx_ref[...] = dx1_ref[...] + _rms_bwd(dxg * g_ref[...], xn, r)
        dg_ref[...] += jnp.sum(dxg * xn, axis=0, keepdims=True)
        dsh_ref[0] += jnp.sum(dh, axis=0, keepdims=True)
        dsc_ref[0] += jnp.sum(dh * xg, axis=0, keepdims=True)

    tile = lambda w: pl.BlockSpec((tm, w), lambda i: (i, 0))
    per_b = pl.BlockSpec((1, 1, d), lambda i: (i // per_seq, 0, 0))
    small = jax.ShapeDtypeStruct((batch, 1, d), F32)
    rope = pl.BlockSpec((tm, LANES), lambda i: (i % per_seq, 0))
    return pl.pallas_call(
        body, name="in_backward", grid=(t // tm,),
        out_shape=(jax.ShapeDtypeStruct((t, d), F32), jax.ShapeDtypeStruct((IN_WIDTH, d), F32),
                   jax.ShapeDtypeStruct((IN_WIDTH, d), BF16), small, small, jax.ShapeDtypeStruct((1, d), F32)),
        in_specs=[tile(NA_WIDTH), tile(NA_WIDTH), tile(NA_WIDTH), tile(SW_WIDTH), tile(LANES), tile(LANES),
                  _resident((IN_WIDTH, d)), tile(d), tile(d),
                  pl.BlockSpec((1, 1, 6 * d), lambda i: (i // per_seq, 0, 0)),
                  pl.BlockSpec((1, d), lambda i: (0, 0)), tile(d), rope, rope],
        out_specs=(tile(d), _resident((IN_WIDTH, d)), _resident((IN_WIDTH, d)),
                   per_b, per_b, pl.BlockSpec((1, d), lambda i: (0, 0))),
        compiler_params=_cparams(("arbitrary",), VMEM_BIG),
    )(dqa, dka, dva, dq_b, dk_b, dv_b, w_in_t, h1, x, mod3, g_attn, dx1, cos_t, sin_t)


def _ada_weight_grad(sc_all, dmod_cols):
    d = sc_all.shape[1]
    ncol = dmod_cols.shape[1]

    def body(s_ref, m_ref, o_ref):
        o_ref[...] = _mm_tn(s_ref[...].astype(BF16), m_ref[...].astype(BF16))

    return pl.pallas_call(
        body, name="ada_weight_grad",
        out_shape=jax.ShapeDtypeStruct((d, ncol), F32),
        compiler_params=_cparams(vmem=VMEM_BIG),
    )(sc_all, dmod_cols)


def _row_tile(rows, cols):
    target = max(SUBLANES, (1 << 20) // (4 * cols))
    best = rows
    for cand in range(SUBLANES, rows + 1, SUBLANES):
        if rows % cand == 0 and cand <= target:
            best = cand
    return best if rows % SUBLANES == 0 else rows


def _sum_slots(results, name, rider=None):
    parts = [group for groups in results for group in groups]
    result_of = [k for k, groups in enumerate(results) for _ in groups]
    n_parts = len(parts)
    cols = [results[k][0][1].shape[1] for k in result_of]
    tr = [_row_tile(min(own.shape[0] for _, own in results[k]), c) for k, c in zip(result_of, cols)]
    assert all(own.shape[0] % r == 0 and own.shape[1] == c for (_, own), r, c in zip(parts, tr, cols))
    tiles = [own.shape[0] // r for (_, own), r in zip(parts, tr)]
    first = [sum(tiles[:q]) for q in range(n_parts)]

    def body(*refs):
        o_refs = refs[2 * n_parts:]
        step = pl.program_id(0)
        for q in range(n_parts):
            @pl.when((step >= first[q]) & (step < first[q] + tiles[q]))
            def _(q=q):
                p_ref, own_ref = refs[2 * q], refs[2 * q + 1]
                o_refs[result_of[q]][...] = (((own_ref[...] + p_ref[0].astype(F32)) + p_ref[1].astype(F32))
                                             + p_ref[2].astype(F32))

    def tile(start, count):
        return lambda i: jnp.clip(i - start, 0, count - 1)

    in_specs, args = [], []
    for q, (recv, own) in enumerate(parts):
        at = tile(first[q], tiles[q])
        in_specs.append(pl.BlockSpec((N_SHARD - 1, tr[q], cols[q]), lambda i, at=at: (0, at(i), 0)))
        in_specs.append(pl.BlockSpec((tr[q], cols[q]), lambda i, at=at: (at(i), 0)))
        args += [recv, own]
    out_shape, out_specs = [], []
    for k in range(len(results)):
        mine = [q for q in range(n_parts) if result_of[q] == k]
        count = sum(tiles[q] for q in mine)
        at = tile(first[mine[0]], count)
        out_shape.append(jax.ShapeDtypeStruct((count * tr[mine[0]], cols[mine[0]]), F32))
        out_specs.append(pl.BlockSpec((tr[mine[0]], cols[mine[0]]), lambda i, at=at: (at(i), 0)))
    return _hosted(body, rider, name=name, grid=(sum(tiles),), out_shape=out_shape, in_specs=in_specs,
                   out_specs=out_specs, scratch_shapes=[], compiler_params=_cparams(("arbitrary",), VMEM_BIG), args=args)


def _adamw_math(w, g, m, v):
    m2 = ADAM_B1 * m + (1.0 - ADAM_B1) * g
    v2 = ADAM_B2 * v + (1.0 - ADAM_B2) * (g * g)
    m_hat = m2 / (1.0 - ADAM_B1 ** ADAM_STEP)
    v_hat = v2 / (1.0 - ADAM_B2 ** ADAM_STEP)
    return -ADAM_LR * (m_hat / (jnp.sqrt(v_hat) + ADAM_EPS) + ADAM_WD * w), m2, v2


def _small_sums(partials, dmod, rider=None):
    moving = list(partials) + [dmod]
    n_mov = len(moving)

    def body(*refs):
        mov, refs = refs[:n_mov], refs[n_mov:]
        sums_out, refs = refs[:n_mov - 1], refs[n_mov - 1:]
        b_out, dmod_out, refs = refs[0], refs[1], refs[2:]
        everyone, (ssem, rsem) = refs[:n_mov], refs[n_mov:]
        x, y, c = _my_pos()
        me = 4 * x + 2 * y + c
        cps = []
        for a in range(n_mov):
            everyone[a][me] = mov[a][...]
            for k in range(1, N_DEV):
                peer = (_flip(x, (k >> 2) & 1), _flip(y, (k >> 1) & 1), _flip(c, k & 1))
                cps.append(pltpu.make_async_remote_copy(
                    src_ref=everyone[a].at[me], dst_ref=everyone[a].at[me], send_sem=ssem.at[a, k - 1],
                    recv_sem=rsem.at[a, k - 1], device_id=peer, device_id_type=MESH))
        for cp in cps:
            cp.start()
        for cp in cps:
            cp.wait_recv()

        def total(a):
            acc = everyone[a][0]
            for dev in range(1, N_DEV):
                acc = acc + everyone[a][dev]
            return acc

        for a in range(n_mov - 1):
            sums_out[a][...] = total(a)
        b_out[...] = jnp.sum(total(n_mov - 1), axis=0, keepdims=True)
        dmod_out[...] = everyone[n_mov - 1][...]
        for cp in cps:
            cp.wait_send()

    vm = pl.BlockSpec(memory_space=pltpu.VMEM)
    sds = jax.ShapeDtypeStruct
    out_shape = [sds(p.shape, F32) for p in partials]
    out_shape += [sds((1, dmod.shape[1]), F32), sds((N_DEV,) + dmod.shape, F32)]
    return _hosted(
        body, rider, name="small_sums", grid=(), out_shape=out_shape,
        in_specs=[vm] * n_mov, out_specs=[vm] * len(out_shape),
        scratch_shapes=[pltpu.VMEM((N_DEV,) + a.shape, F32) for a in moving]
        + [pltpu.SemaphoreType.DMA((n_mov, N_DEV - 1)), pltpu.SemaphoreType.DMA((n_mov, N_DEV - 1))],
        compiler_params=_cparams(vmem=VMEM_BIG), args=moving)


def _small_adamw(states, grads):
    n = len(states)

    def body(*refs):
        g_refs, wmv, res = refs[:n], refs[n:4 * n], refs[4 * n:]
        for j in range(n):
            g = g_refs[j][...]
            delta, m2, v2 = _adamw_math(wmv[3 * j][...], g, wmv[3 * j + 1][...], wmv[3 * j + 2][...])
            res[4 * j][...] = g
            res[4 * j + 1][...] = delta
            res[4 * j + 2][...] = m2
            res[4 * j + 3][...] = v2

    out_shape = []
    for w, _, _ in states:
        out_shape += [jax.ShapeDtypeStruct(w.shape, F32)] * 4
    outs = pl.pallas_call(body, name="small_adamw", out_shape=tuple(out_shape),
                          compiler_params=_cparams(vmem=VMEM_BIG))(*grads, *[a for st in states for a in st])
    return [outs[4 * j:4 * j + 4] for j in range(n)]


def _adamw(w, grads, m, v, name):
    rows, cols = w.shape
    tr = _row_tile(rows, cols)
    ng = len(grads)

    def body(*refs):
        w_ref = refs[0]
        g_refs = refs[1:1 + ng]
        m_ref, v_ref = refs[1 + ng], refs[2 + ng]
        g_out, d_out, m_out, v_out = refs[3 + ng:]
        g = g_refs[0][...]
        for extra in g_refs[1:]:
            g = g + extra[...]
        g_out[...] = g
        d_out[...], m_out[...], v_out[...] = _adamw_math(w_ref[...], g, m_ref[...], v_ref[...])

    spec = pl.BlockSpec((tr, cols), lambda i: (i, 0))
    out = jax.ShapeDtypeStruct((rows, cols), F32)
    return pl.pallas_call(
        body, name=name, grid=(rows // tr,),
        out_shape=(out, out, out, out),
        in_specs=[spec] * (3 + ng), out_specs=(spec, spec, spec, spec),
        compiler_params=_cparams(("arbitrary",)),
    )(w, *grads, m, v)


def _rope_tables(seq):
    half = HEAD_DIM // 2
    inv = np.float32(ROPE_THETA) ** (-np.arange(half, dtype=np.float32) / np.float32(half))
    ang = (np.arange(seq, dtype=np.float32)[:, None] * inv[None, :]).astype(np.float64)
    cos, sin = np.cos(ang).astype(np.float32), np.sin(ang).astype(np.float32)
    cos_t = np.concatenate([cos, cos, cos, cos], axis=1)
    sin_t = np.concatenate([-sin, sin, -sin, sin], axis=1)
    return jnp.asarray(cos_t), jnp.asarray(sin_t)


def kernel(x, c, w_ada, b_ada, g_attn, w_in, na_rpb, sw_sink, g_na_out, g_sw_out, w_out, g_ffn, w_up, conv_w, conv_b, w_down, g_final, loss_target, m_w_ada, m_b_ada, m_g_attn, m_w_in, m_na_rpb, m_sw_sink, m_g_na_out, m_g_sw_out, m_w_out, m_g_ffn, m_w_up, m_conv_w, m_conv_b, m_w_down, m_g_final, v_w_ada, v_b_ada, v_g_attn, v_w_in, v_na_rpb, v_sw_sink, v_g_na_out, v_g_sw_out, v_w_out, v_g_ffn, v_w_up, v_conv_w, v_conv_b, v_w_down, v_g_final):
    batch, seq, d = x.shape
    t = batch * seq
    assert d == D_MODEL and seq % (NA_ROWS * GRID_W) == 0 and seq % TOKEN_TILE == 0 and batch <= SUBLANES
    shard = 2 * lax.axis_index("x") + lax.axis_index("y")
    xt = x.reshape(t, d)
    tgt = loss_target.reshape(t, d)

    c8 = jnp.pad(c, ((0, SUBLANES - batch), (0, 0)))
    w_in_t_s = jnp.transpose(w_in[0]).astype(BF16)
    (mod8, sc_all), (w_in_g,) = _ada_forward(c8, w_ada[0], b_ada, _Rider("gather", [w_in_t_s]))
    mod3 = mod8[:batch].reshape(batch, 1, 6 * d)
    w_in_t = w_in_g.reshape(IN_WIDTH, d)

    cos_t, sin_t = _rope_tables(seq)
    (h1, proj), _ = _in_proj(xt, mod3, g_attn, w_in_t, cos_t, sin_t, seq)
    n_heads = NA_WIDTH // HEAD_DIM
    n_tiles, n_dc = 2 * NA_ROWS - 2, 2 * NA_COLS - 1
    expand, neg_mask = _na_bias_pattern()
    rpb = na_rpb[0]
    rows2 = jnp.concatenate([rpb[:, :-1, :], rpb[:, 1:, :]], axis=2).reshape(n_heads * n_tiles, 2 * n_dc)
    rows2 = jnp.pad(rows2, ((0, 0), (0, GRID_W - 2 * n_dc)))
    tiles = _na_bias_tiles(rows2, expand, neg_mask).reshape(n_heads, n_tiles, GRID_W, LANES)
    sink = sw_sink[0]
    w_up_b16 = w_up[0].astype(BF16)
    (oa, lse_a), (w_up_a,) = _na_forward(proj, tiles, batch, seq, _Rider("gather", [w_up_b16[:d // 2]]))
    (ob, lse_b), (w_up_b, conv_w_g, w_out_g) = _sw_forward(
        proj, sink, batch, seq, _Rider("gather", [w_up_b16[d // 2:], conv_w[0], w_out[0].astype(BF16)]))
    w_up_f = (w_up_a, w_up_b)
    w_out_f = w_out_g.reshape(d, d)
    conv_w_f = jnp.transpose(conv_w_g, (1, 0, 2)).reshape(3, D_FF)
    oab, mix, x1, h2 = _out_proj(oa, ob, g_na_out, g_sw_out, w_out_f, xt, mod3, g_ffn, seq)
    (u,), _ = _up_proj(h2, w_up_f)
    (a,), (w_down_g,) = _conv_gate(u, conv_w_f, conv_b, batch, seq, _Rider("gather", [w_down[0].astype(BF16)]))
    w_down_f = w_down_g.reshape(D_FF, d)
    dx2, dffn, loss_part, dgate_f, dg_final = _down_and_loss(a, w_down_f, x1, mod3, g_final.reshape(1, d), tgt, seq)

    gw_down, gw_down_b = _down_weight_grad(a, dffn)
    blocks = lambda g, rows: g.reshape(N_SHARD, rows // N_SHARD, d)
    (du, gconv_w, gconv_b), (recv_down, own_down) = _ffn_backward(
        dffn, w_down_f, u, conv_w_f, conv_b, batch, seq,
        _Rider("scatter", [blocks(gw_down_b, D_FF)], [blocks(gw_down, D_FF)]))
    (gw_up_top, gw_up_bot, gw_up_top_b, gw_up_bot_b), _ = _up_weight_grad(h2, du)
    (dx1, dmix, dshift_f, dscale_f, dgate_a, dg_ffn), _ = _up_backward(du, w_up_f, x1, mod3, g_ffn, dx2, mix, seq)
    doa, dob, gw_out, gw_out_b, dg_na, dg_sw = _out_backward(dmix, w_out_f, oab, oa, ob, g_na_out, g_sw_out)
    (dqa, dka, dva, dtiles), (recv_out, recv_up_bot, own_out, own_up_bot) = _na_backward(
        proj, doa, lse_a, tiles, batch, seq,
        _Rider("scatter", [blocks(gw_out_b, d), gw_up_bot_b], [blocks(gw_out, d), gw_up_bot]))
    (dq_b, dk_b, dv_b, dsink_parts), (recv_up_top, own_up_top) = _sw_backward(
        proj, dob, lse_b, sink, batch, seq, _Rider("scatter", [gw_up_top_b], [gw_up_top]))
    gx, gw_in_t, gw_in_b, dshift_a, dscale_a, dg_attn = _in_backward(
        (dqa, dka, dva), dq_b, dk_b, dv_b, w_in_t, h1, xt, mod3, g_attn, dx1, cos_t, sin_t, seq)

    red = _na_bias_grad(dtiles.reshape(n_heads * n_tiles, GRID_W, LANES), expand)[:, :2 * n_dc]
    red = red.reshape(n_heads, n_tiles, 2, n_dc)
    zero_row = jnp.zeros((n_heads, 1, n_dc), F32)
    g_rpb = (jnp.concatenate([red[:, :, 0, :], zero_row], axis=1)
             + jnp.concatenate([zero_row, red[:, :, 1, :]], axis=1))
    g_sink = jnp.sum(dsink_parts[:, :, :2, 0], axis=0).reshape(SW_WIDTH // HEAD_DIM)

    dmod = jnp.concatenate([dshift_a, dscale_a, dgate_a, dshift_f, dscale_f, dgate_f], axis=2).reshape(batch, 6 * d)
    rpb_shape = na_rpb.shape[1:]
    states = [(g_attn, m_g_attn, v_g_attn),
              (na_rpb.reshape(rpb_shape), m_na_rpb.reshape(rpb_shape), v_na_rpb.reshape(rpb_shape)),
              (sw_sink, m_sw_sink, v_sw_sink), (g_na_out, m_g_na_out, v_g_na_out), (g_sw_out, m_g_sw_out, v_g_sw_out),
              (g_ffn, m_g_ffn, v_g_ffn), (conv_b, m_conv_b, v_conv_b),
              (g_final.reshape(1, d), m_g_final.reshape(1, d), v_g_final.reshape(1, d))]
    partials = [dg_attn, g_rpb, g_sink.reshape(sw_sink.shape), dg_na, dg_sw, dg_ffn, gconv_b, dg_final,
                gconv_w, loss_part]
    late, (recv_in, own_in) = _sum_slots(
        [[(recv_out, own_out)], [(recv_up_top, own_up_top), (recv_up_bot, own_up_bot)], [(recv_down, own_down)]],
        "sum_w_out_up_down", _Rider("scatter", [blocks(gw_in_b, IN_WIDTH)], [blocks(gw_in_t, IN_WIDTH)]))
    mine = _sum_slots([[(recv_in, own_in)]], "sum_w_in")[0] + late
    small, theirs = _small_sums(partials, dmod, _Rider("swap", mine))
    g_conv_w_full, loss_sum, g_b_ada, dmod_all = small[len(states):]
    r_small = _small_adamw(states + [(b_ada, m_b_ada, v_b_ada)], small[:len(states)] + [g_b_ada])
    loss = loss_sum[0, 0]
    dmod_rows = jnp.pad(dmod_all, ((0, 0), (0, SUBLANES - batch), (0, 0))).reshape(N_DEV * SUBLANES, 6 * d)
    ncol = w_ada.shape[2]
    g_w_ada = _ada_weight_grad(sc_all, lax.dynamic_slice(dmod_rows, (0, shard * ncol), (N_DEV * SUBLANES, ncol)))
    cshard = conv_w.shape[2]
    g_conv_w = lax.dynamic_slice(g_conv_w_full, (0, shard * cshard), (3, cshard))

    def big(w, m, v, g_parts, name):
        shape = w.shape
        outs = _adamw(w[0], g_parts, m[0], v[0], name)
        return [o.reshape(shape) for o in outs]

    r_w_ada = big(w_ada, m_w_ada, v_w_ada, [g_w_ada], "adamw_w_ada")
    r_w_in = [jnp.transpose(o).reshape(w_in.shape) for o in
              _adamw(jnp.transpose(w_in[0]), [mine[0], theirs[0]], jnp.transpose(m_w_in[0]), jnp.transpose(v_w_in[0]),
                     "adamw_w_in")]
    r_w_out = big(w_out, m_w_out, v_w_out, [mine[1], theirs[1]], "adamw_w_out")
    r_w_up = big(w_up, m_w_up, v_w_up, [mine[2], theirs[2]], "adamw_w_up")
    r_w_down = big(w_down, m_w_down, v_w_down, [mine[3], theirs[3]], "adamw_w_down")

    r_conv_w = big(conv_w, m_conv_w, v_conv_w, [g_conv_w], "adamw_conv_w")

    def pick(k):
        ga_, rpb_, sk_, gna_, gsw_, gf_, cb_, gfin_, b_ = [r[k] for r in r_small]
        return [r_w_ada[k], b_, ga_, r_w_in[k], rpb_.reshape(na_rpb.shape), sk_, gna_, gsw_, r_w_out[k], gf_,
                r_w_up[k], r_conv_w[k], cb_, r_w_down[k], gfin_.reshape(d)]

    return (loss, gx.reshape(batch, seq, d), *pick(0), *pick(1), *pick(2), *pick(3))
```

```python
import jax
import jax.numpy as jnp
import numpy as np
from jax import lax
from jax.experimental import pallas as pl
from jax.experimental.pallas import tpu as pltpu

F32 = jnp.float32
BF16 = jnp.bfloat16
MESH = pl.DeviceIdType.MESH

D_MODEL = 1024
HEAD_DIM = 64
NA_WIDTH = 512
SW_WIDTH = 512
SW_KV_WIDTH = 128
IN_WIDTH = 2304
D_FF = 2816
GRID_W = 64
NA_ROWS = 8
NA_COLS = 16
SW_BLOCK = 128
ROPE_THETA = 10000.0
EPS = 1e-6
NEG = -1e30
QK_SCALE = HEAD_DIM ** -0.5

ADAM_LR = 0.001
ADAM_B1 = 0.9
ADAM_B2 = 0.999
ADAM_EPS = 1e-08
ADAM_WD = 0.01
ADAM_STEP = 10

N_SHARD = 4
N_DEV = 8
LANES = 128
SUBLANES = 8
TOKEN_TILE = 512
FF_TILE = 256
CONV_CHUNK = 512
NA_GROUP = 8
SW_GROUP_BLOCKS = 8
VMEM_BIG = 56 * 1024 * 1024


def _mm(a, b):
    return jnp.dot(a, b, preferred_element_type=F32)


def _mm_nt(a, b):
    return lax.dot_general(a, b, (((1,), (1,)), ((), ())), preferred_element_type=F32)


def _mm_tn(a, b):
    return lax.dot_general(a, b, (((0,), (0,)), ((), ())), preferred_element_type=F32)


def _cparams(sem=None, vmem=None):
    kw = {}
    if sem is not None:
        kw["dimension_semantics"] = sem
    if vmem is not None:
        kw["vmem_limit_bytes"] = vmem
    return pltpu.CompilerParams(**kw)


def _resident(shape):
    return pl.BlockSpec(shape, lambda i: (0,) * len(shape), pipeline_mode=pl.Buffered(1))


def _sigmoid(x):
    return 1.0 / (1.0 + jnp.exp(-x))


def _rms_stats(x):
    r = lax.rsqrt(jnp.mean(x * x, axis=-1, keepdims=True) + EPS)
    return r, x * r


def _rms_bwd(dxn, xn, r):
    return r * (dxn - xn * jnp.mean(dxn * xn, axis=-1, keepdims=True))


def _my_pos():
    return lax.axis_index("x"), lax.axis_index("y"), lax.axis_index("c")


def _flip(v, bit):
    return 1 - v if bit else v


def _ada_forward(c8, w_ada, b_ada, rider):
    d = c8.shape[1]
    ncol = w_ada.shape[1]

    def body(c_ref, w_ref, b_ref, mod_ref, sc_ref, m_scr, mod_buf, ssem, rsem, ssem2, rsem2):
        x, y, c = _my_pos()
        me = 4 * x + 2 * y + c
        shard = 2 * x + y
        cv = c_ref[...]
        my_rows = pl.ds(pl.multiple_of(me * SUBLANES, SUBLANES), SUBLANES)
        sc_ref[my_rows, :] = cv * _sigmoid(cv)

        def copy1(k):
            peer = (_flip(x, (k >> 2) & 1), _flip(y, (k >> 1) & 1), _flip(c, k & 1))
            return pltpu.make_async_remote_copy(
                src_ref=sc_ref.at[my_rows, :], dst_ref=sc_ref.at[my_rows, :],
                send_sem=ssem.at[k - 1], recv_sem=rsem.at[k - 1], device_id=peer, device_id_type=MESH)

        sends = [copy1(k) for k in range(1, N_DEV)]
        for cp in sends:
            cp.start()
        for cp in sends:
            cp.wait_recv()
        m_scr[...] = _mm(sc_ref[...].astype(BF16), w_ref[...].astype(BF16))

        def copy2(k):
            px, py = _flip(x, (k >> 1) & 1), _flip(y, k & 1)
            rows = pl.ds(pl.multiple_of((4 * px + 2 * py + c) * SUBLANES, SUBLANES), SUBLANES)
            return pltpu.make_async_remote_copy(
                src_ref=m_scr.at[rows, :], dst_ref=mod_buf.at[shard],
                send_sem=ssem2.at[k - 1], recv_sem=rsem2.at[k - 1], device_id=(px, py, c), device_id_type=MESH)

        sends2 = [copy2(k) for k in range(1, N_SHARD)]
        for cp in sends2:
            cp.start()
        mod_buf[shard] = m_scr[my_rows, :]
        for cp in sends2:
            cp.wait_recv()
        for s in range(N_SHARD):
            mod_ref[:, s * ncol:(s + 1) * ncol] = mod_buf[s] + b_ref[:, s * ncol:(s + 1) * ncol]
        for cp in sends + sends2:
            cp.wait_send()

    vm = pl.BlockSpec(memory_space=pltpu.VMEM)
    return _hosted(
        body, rider, name="ada_forward", grid=(),
        out_shape=(jax.ShapeDtypeStruct((SUBLANES, N_SHARD * ncol), F32),
                   jax.ShapeDtypeStruct((N_DEV * SUBLANES, d), F32)),
        in_specs=[vm, vm, vm], out_specs=(vm, vm),
        scratch_shapes=[pltpu.VMEM((N_DEV * SUBLANES, ncol), F32), pltpu.VMEM((N_SHARD, SUBLANES, ncol), F32),
                        pltpu.SemaphoreType.DMA((N_DEV - 1,)), pltpu.SemaphoreType.DMA((N_DEV - 1,)),
                        pltpu.SemaphoreType.DMA((N_SHARD - 1,)), pltpu.SemaphoreType.DMA((N_SHARD - 1,))],
        compiler_params=_cparams(vmem=VMEM_BIG), args=[c8, w_ada, b_ada])


class _Rider:
    def __init__(self, kind, srcs, owns=()):
        self.kind, self.srcs, self.owns = kind, list(srcs), list(owns)
        n = len(self.srcs)
        sds = jax.ShapeDtypeStruct
        dma = pltpu.SemaphoreType.DMA
        if kind == "gather":
            self.out_shapes = [sds((N_SHARD,) + s.shape, s.dtype) for s in self.srcs]
            self.sems = [dma((n, N_SHARD - 1)), dma((n, N_SHARD - 1)), dma((n, N_SHARD - 1)), dma((n, N_SHARD - 1)),
                         dma((n,)), dma((n,))]
        elif kind == "scatter":
            self.out_shapes = ([sds((N_SHARD - 1,) + s.shape[1:], s.dtype) for s in self.srcs]
                               + [sds(o.shape[1:], o.dtype) for o in self.owns])
            m = max(len(self.owns), 1)
            self.sems = [dma((n, N_SHARD - 1)), dma((n, N_SHARD - 1)), dma((m,)), dma((m,))]
        else:
            self.out_shapes = [sds(s.shape, s.dtype) for s in self.srcs]
            self.sems = [dma((n,)), dma((n,))]

    @property
    def inputs(self):
        return self.srcs + self.owns

    def _halved(self, i):
        a = self.srcs[i]
        tile_rows = SUBLANES * (4 // jnp.dtype(a.dtype).itemsize)
        return self.kind == "gather" and a.shape[0] % (2 * tile_rows) == 0

    def copies(self, ins, outs, sems):
        n = len(self.srcs)
        x, y, c = _my_pos()
        shard = 2 * x + y
        remote, relay = [], []
        if self.kind == "swap":
            ssem, rsem = sems
            for i in range(n):
                remote.append(pltpu.make_async_remote_copy(
                    src_ref=ins[i], dst_ref=outs[i], send_sem=ssem.at[i], recv_sem=rsem.at[i],
                    device_id=(x, y, 1 - c), device_id_type=MESH))
            return remote, relay
        if self.kind == "gather":
            ssem, rsem, ssem2, rsem2, sib_s, sib_r = sems
        else:
            ssem, rsem, sib_s, sib_r = sems
        for i in range(n):
            if self.kind == "gather":
                remote.append(pltpu.make_async_remote_copy(
                    src_ref=ins[i], dst_ref=outs[i].at[shard], send_sem=sib_s.at[i], recv_sem=sib_r.at[i],
                    device_id=(x, y, 1 - c), device_id_type=MESH))
                half = ins[i].shape[0] // 2
                mine = pl.ds(pl.multiple_of(c * half, half), half) if self._halved(i) else None
            for k in range(1, N_SHARD):
                px, py = _flip(x, (k >> 1) & 1), _flip(y, k & 1)
                if self.kind == "gather":
                    src, dst = ins[i], outs[i].at[shard]
                    if mine is not None:
                        src, dst = src.at[mine], dst.at[mine]
                        got = outs[i].at[2 * px + py].at[mine]
                        relay.append(pltpu.make_async_remote_copy(
                            src_ref=got, dst_ref=got, send_sem=ssem2.at[i, k - 1], recv_sem=rsem2.at[i, k - 1],
                            device_id=(x, y, 1 - c), device_id_type=MESH))
                else:
                    src, dst = ins[i].at[2 * px + py], outs[i].at[k - 1]
                remote.append(pltpu.make_async_remote_copy(
                    src_ref=src, dst_ref=dst, send_sem=ssem.at[i, k - 1], recv_sem=rsem.at[i, k - 1],
                    device_id=(px, py, c), device_id_type=MESH))
        if self.kind == "scatter":
            for i in range(len(self.owns)):
                remote.append(pltpu.make_async_remote_copy(
                    src_ref=ins[n + i].at[shard], dst_ref=outs[n + i], send_sem=sib_s.at[i], recv_sem=sib_r.at[i],
                    device_id=(x, y, 1 - c), device_id_type=MESH))
        return remote, relay

    def start(self, ins, outs, sems):
        remote, _ = self.copies(ins, outs, sems)
        for cp in remote:
            cp.start()

    def wait(self, ins, outs, sems):
        remote, relay = self.copies(ins, outs, sems)
        for cp in remote:
            cp.wait_recv()
        for cp in relay:
            cp.start()
        for cp in relay:
            cp.wait_recv()
        for cp in remote + relay:
            cp.wait_send()


def _hosted(body, rider, *, name, grid, out_shape, in_specs, out_specs, scratch_shapes, compiler_params, args):
    out_shape, out_specs = list(out_shape), list(out_specs)
    if rider is None:
        outs = pl.pallas_call(body, name=name, grid=grid, out_shape=tuple(out_shape), in_specs=list(in_specs),
                              out_specs=tuple(out_specs), scratch_shapes=list(scratch_shapes),
                              compiler_params=compiler_params)(*args)
        return list(outs), []
    n_in, n_out, n_scr = len(in_specs), len(out_shape), len(scratch_shapes)
    nr_in, nr_out = len(rider.inputs), len(rider.out_shapes)
    n_steps = 1
    for size in grid:
        n_steps *= size

    def full(*refs):
        ins, refs = refs[:n_in], refs[n_in:]
        r_in, refs = refs[:nr_in], refs[nr_in:]
        outs, refs = refs[:n_out], refs[n_out:]
        r_out, refs = refs[:nr_out], refs[nr_out:]
        scr, sems = refs[:n_scr], refs[n_scr:]
        if grid:
            step = 0
            for ax, size in enumerate(grid):
                step = step * size + pl.program_id(ax)
            pl.when(step == 0)(lambda: rider.start(r_in, r_out, sems))
            body(*ins, *outs, *scr)
            pl.when(step == n_steps - 1)(lambda: rider.wait(r_in, r_out, sems))
        else:
            rider.start(r_in, r_out, sems)
            body(*ins, *outs, *scr)
            rider.wait(r_in, r_out, sems)

    hbm = pl.BlockSpec(memory_space=pl.ANY)
    res = pl.pallas_call(
        full, name=name, grid=grid, out_shape=tuple(out_shape + rider.out_shapes),
        in_specs=list(in_specs) + [hbm] * nr_in, out_specs=tuple(out_specs + [hbm] * nr_out),
        scratch_shapes=list(scratch_shapes) + rider.sems, compiler_params=compiler_params,
    )(*args, *rider.inputs)
    return list(res[:n_out]), list(res[n_out:])


def _rope_rot(t):
    w = t.shape[1]
    lane = lax.broadcasted_iota(jnp.int32, t.shape, 1)
    first = (lane % HEAD_DIM) < (HEAD_DIM // 2)
    return jnp.where(first, pltpu.roll(t, w - HEAD_DIM // 2, 1), pltpu.roll(t, HEAD_DIM // 2, 1))


def _in_proj(x, mod3, g_attn, w_in_t, cos_t, sin_t, seq, rider=None):
    t, d = x.shape
    tm = 2 * TOKEN_TILE
    per_seq = seq // tm
    rope_lo, rope_hi = 3 * NA_WIDTH, 3 * NA_WIDTH + SW_WIDTH + SW_KV_WIDTH
    n_rep = (rope_hi - rope_lo) // LANES

    def body(x_ref, mod_ref, g_ref, w_ref, cos_ref, sin_ref, h_ref, p_ref):
        r, xn = _rms_stats(x_ref[...])
        shift, scale = mod_ref[0, :, 0:d], mod_ref[0, :, d:2 * d]
        hb = ((xn * g_ref[...]) * (1.0 + scale) + shift).astype(BF16)
        h_ref[...] = hb
        p_ref[:, :rope_lo] = _mm_nt(hb, w_ref[:rope_lo, :]).astype(BF16)
        pr = _mm_nt(hb, w_ref[rope_lo:rope_hi, :])
        cos = jnp.concatenate([cos_ref[...]] * n_rep, axis=1)
        sin = jnp.concatenate([sin_ref[...]] * n_rep, axis=1)
        p_ref[:, rope_lo:rope_hi] = (pr * cos + _rope_rot(pr) * sin).astype(BF16)
        p_ref[:, rope_hi:] = _mm_nt(hb, w_ref[rope_hi:, :]).astype(BF16)

    return _hosted(
        body, rider, name="in_proj", grid=(t // tm,),
        out_shape=[jax.ShapeDtypeStruct((t, d), BF16), jax.ShapeDtypeStruct((t, IN_WIDTH), BF16)],
        in_specs=[pl.BlockSpec((tm, d), lambda i: (i, 0)),
                  pl.BlockSpec((1, 1, 6 * d), lambda i: (i // per_seq, 0, 0)),
                  pl.BlockSpec((1, d), lambda i: (0, 0)),
                  pl.BlockSpec((IN_WIDTH, d), lambda i: (0, 0)),
                  pl.BlockSpec((tm, LANES), lambda i: (i % per_seq, 0)),
                  pl.BlockSpec((tm, LANES), lambda i: (i % per_seq, 0))],
        out_specs=[pl.BlockSpec((tm, d), lambda i: (i, 0)), pl.BlockSpec((tm, IN_WIDTH), lambda i: (i, 0))],
        scratch_shapes=[], compiler_params=_cparams(("arbitrary",), VMEM_BIG),
        args=[x, mod3, g_attn, w_in_t, cos_t, sin_t])


def _na_bias_pattern():
    n_dc = 2 * NA_COLS - 1
    j = np.arange(GRID_W)[:, None]
    m = np.arange(GRID_W * LANES)[None, :]
    q, lane = m // LANES, m % LANES
    k = lane % GRID_W
    cs = np.clip(q - NA_COLS // 2, 0, GRID_W - NA_COLS)
    ok = (k >= cs) & (k < cs + NA_COLS)
    hit = ok & (j < 2 * n_dc) & (lane // GRID_W == j // n_dc) & (k - q + (NA_COLS - 1) == j % n_dc)
    return jnp.asarray(hit.astype(np.float32)), jnp.asarray(np.where(ok, 0.0, NEG).astype(np.float32))


def _na_bias_tiles(rows2, expand, mask):
    n, width = rows2.shape[0], expand.shape[1]
    q_step = 16
    step = q_step * LANES

    def body(r_ref, e_ref, m_ref, o_ref):
        flat = jnp.dot(r_ref[...], e_ref[...], precision=lax.Precision.HIGHEST,
                       preferred_element_type=F32) + m_ref[...]
        for qq in range(q_step):
            o_ref[:, qq, :] = flat[:, qq * LANES:(qq + 1) * LANES]

    return pl.pallas_call(
        body, name="na_bias_tiles", grid=(width // step,),
        out_shape=jax.ShapeDtypeStruct((n, GRID_W, LANES), F32),
        in_specs=[pl.BlockSpec(rows2.shape, lambda i: (0, 0)), pl.BlockSpec((expand.shape[0], step), lambda i: (0, i)),
                  pl.BlockSpec((1, step), lambda i: (0, i))],
        out_specs=pl.BlockSpec((n, q_step, LANES), lambda i: (0, i, 0)),
        compiler_params=_cparams(("arbitrary",)),
    )(rows2, expand, mask)


def _na_prepare(k_ref, v_ref, km, vm):
    lane = lax.broadcasted_iota(jnp.int32, k_ref.shape, 1)
    low = lane < HEAD_DIM
    kv = k_ref[...]
    vv = v_ref[...]
    zero = jnp.zeros_like(kv)
    km[0] = jnp.where(low, kv, zero)
    km[1] = jnp.where(low, zero, kv)
    vm[0] = jnp.where(low, vv, zero)
    vm[1] = jnp.where(low, zero, vv)


def _na_window(r, n_rows):
    rs = jnp.clip(r - NA_ROWS // 2, 0, n_rows - NA_ROWS)
    return rs, r - rs


def _na_pair_window(ref, wrows):
    return jnp.concatenate([ref[0, wrows, :], ref[1, wrows, :]], axis=0)


def _na_scores(q, k2, tp_ref, off):
    bias = jnp.concatenate([tp_ref[h, 2 * w - off + (NA_ROWS - 1)] for h in range(2) for w in range(NA_ROWS // 2)],
                           axis=1)
    return _mm_nt(q, k2) * QK_SCALE + bias


def _pair_lse_block(lse):
    lane = lax.broadcasted_iota(jnp.int32, (lse[0].shape[0], LANES), 1)
    return jnp.where(lane < HEAD_DIM, lse[0], lse[1])


def _pair_softmax(s):
    win = s.shape[1] // 2
    halves, lse = [], []
    for h in range(2):
        sh = s[:, h * win:(h + 1) * win]
        m = jnp.max(sh, axis=-1, keepdims=True)
        e = jnp.exp(sh - m)
        l = jnp.sum(e, axis=-1, keepdims=True)
        halves.append(e / l)
        lse.append(m + jnp.log(l))
    return jnp.concatenate(halves, axis=1), _pair_lse_block(lse)


def _pair_grad(w2, x, low):
    keys = w2.shape[1] // 2
    zero = jnp.zeros_like(x)
    low_x = low[:x.shape[0]]
    stacked = jnp.concatenate([w2[:, :keys], w2[:, keys:]], axis=0)
    diag = jnp.concatenate([jnp.where(low_x, x, zero), jnp.where(low_x, zero, x)], axis=0)
    return _mm_tn(stacked, diag)


def _pair_probs_from_lse(s, lse_block):
    win = s.shape[1] // 2
    return jnp.concatenate([jnp.exp(s[:, h * win:(h + 1) * win] - lse_block[:, h * HEAD_DIM:h * HEAD_DIM + 1])
                            for h in range(2)], axis=1)


def _na_forward(proj, tiles, batch, seq, rider=None):
    t = proj.shape[0]
    n_rows = seq // GRID_W
    n_pairs = NA_WIDTH // LANES
    win = NA_ROWS * GRID_W

    def body(q_ref, k_ref, v_ref, tp_ref, o_ref, lse_ref, km, vm):
        _na_prepare(k_ref, v_ref, km, vm)

        def scores(r):
            rs, off = _na_window(r, n_rows)
            rows = pl.ds(pl.multiple_of(r * GRID_W, GRID_W), GRID_W)
            wrows = pl.ds(pl.multiple_of(rs * GRID_W, GRID_W), win)
            return rows, wrows, _na_scores(q_ref[rows, :], _na_pair_window(km, wrows), tp_ref, off)

        def finish(rows, wrows, s):
            p, lse = _pair_softmax(s)
            lse_ref[rows, :] = lse
            o_ref[rows, :] = _mm(p.astype(BF16), _na_pair_window(vm, wrows))

        def row_group(i, carry):
            for state in [scores(NA_GROUP * i + j) for j in range(NA_GROUP)]:
                finish(*state)
            return carry

        lax.fori_loop(0, n_rows // NA_GROUP, row_group, 0)

    return _hosted(
        body, rider, name="na_forward", grid=(batch, n_pairs),
        out_shape=[jax.ShapeDtypeStruct((t, NA_WIDTH), F32), jax.ShapeDtypeStruct((t, NA_WIDTH), F32)],
        in_specs=[pl.BlockSpec((seq, LANES), lambda b, p: (b, p)),
                  pl.BlockSpec((seq, LANES), lambda b, p: (b, n_pairs + p)),
                  pl.BlockSpec((seq, LANES), lambda b, p: (b, 2 * n_pairs + p)),
                  pl.BlockSpec((2, 2 * NA_ROWS - 2, GRID_W, LANES), lambda b, p: (p, 0, 0, 0))],
        out_specs=[pl.BlockSpec((seq, LANES), lambda b, p: (b, p)), pl.BlockSpec((seq, LANES), lambda b, p: (b, p))],
        scratch_shapes=[pltpu.VMEM((2, seq, LANES), BF16), pltpu.VMEM((2, seq, LANES), BF16)],
        compiler_params=_cparams(("arbitrary", "arbitrary")), args=[proj, proj, proj, tiles])


def _sw_prepare(kv_ref, g, dst_lo, dst_hi, seq):
    lane = lax.broadcasted_iota(jnp.int32, kv_ref.shape, 1)
    mine = (lane // HEAD_DIM) == g
    kg = jnp.where(mine, kv_ref[...].astype(F32), 0.0)
    kr = pltpu.roll(kg, HEAD_DIM, 1)
    first = g == 0
    zero = jnp.zeros((SW_BLOCK, LANES), BF16)
    for dst, val in ((dst_lo, jnp.where(first, kg, kr)), (dst_hi, jnp.where(first, kr, kg))):
        dst[0:SW_BLOCK, :] = zero
        dst[SW_BLOCK:SW_BLOCK + seq, :] = val.astype(BF16)
        dst[SW_BLOCK + seq:, :] = zero


def _sw_mask(n, seq):
    qi = lax.broadcasted_iota(jnp.int32, (SW_BLOCK, 3 * SW_BLOCK), 0)
    kj = lax.broadcasted_iota(jnp.int32, (SW_BLOCK, 3 * SW_BLOCK), 1)
    kpos = n * SW_BLOCK - SW_BLOCK + kj
    return (jnp.abs(qi + SW_BLOCK - kj) <= SW_BLOCK) & (kpos >= 0) & (kpos < seq)


def _sw_probs(s2, ok, sinks):
    band = s2.shape[1] // 2
    halves, lse = [], []
    for i in range(2):
        s = jnp.where(ok, s2[:, i * band:(i + 1) * band], NEG)
        m = jnp.maximum(jnp.max(s, axis=-1, keepdims=True), sinks[i])
        p = jnp.exp(s - m)
        den = jnp.sum(p, axis=-1, keepdims=True) + jnp.exp(sinks[i] - m)
        halves.append(p / den)
        lse.append(m + jnp.log(den))
    return jnp.concatenate(halves, axis=1), _pair_lse_block(lse)


def _sw_probs_from_lse(s2, ok, sinks, lse_block):
    band = s2.shape[1] // 2
    halves, sink_p = [], []
    for i in range(2):
        lse = lse_block[:, i * HEAD_DIM:i * HEAD_DIM + 1]
        halves.append(jnp.exp(jnp.where(ok, s2[:, i * band:(i + 1) * band], NEG) - lse))
        sink_p.append(jnp.exp(sinks[i] - lse))
    return jnp.concatenate(halves, axis=1), sink_p


def _sw_forward(proj, sink, batch, seq, rider=None):
    t = proj.shape[0]
    n_pairs = SW_WIDTH // LANES
    q_blk = 3 * NA_WIDTH // LANES
    k_blk = q_blk + n_pairs
    n_blocks = seq // SW_BLOCK
    pad = seq + 2 * SW_BLOCK

    def body(sink_ref, q_ref, k_ref, v_ref, o_ref, lse_ref, k_lo, k_hi, v_lo, v_hi):
        hp = pl.program_id(1)
        g = hp // 2

        @pl.when(hp % 2 == 0)
        def _():
            _sw_prepare(k_ref, g, k_lo, k_hi, seq)
            _sw_prepare(v_ref, g, v_lo, v_hi, seq)

        sinks = (sink_ref[2 * hp], sink_ref[2 * hp + 1])

        def scores(n):
            rows = pl.ds(pl.multiple_of(n * SW_BLOCK, SW_BLOCK), SW_BLOCK)
            wrows = pl.ds(pl.multiple_of(n * SW_BLOCK, SW_BLOCK), 3 * SW_BLOCK)
            k2 = jnp.concatenate([k_lo[wrows, :], k_hi[wrows, :]], axis=0)
            return n, rows, wrows, _mm_nt(q_ref[rows, :], k2) * QK_SCALE

        def finish(n, rows, wrows, s2):
            p, lse = _sw_probs(s2, _sw_mask(n, seq), sinks)
            lse_ref[rows, :] = lse
            v2 = jnp.concatenate([v_lo[wrows, :], v_hi[wrows, :]], axis=0)
            o_ref[rows, :] = _mm(p.astype(BF16), v2)

        def block_group(i, carry):
            for state in [scores(SW_GROUP_BLOCKS * i + j) for j in range(SW_GROUP_BLOCKS)]:
                finish(*state)
            return carry

        lax.fori_loop(0, n_blocks // SW_GROUP_BLOCKS, block_group, 0)

    return _hosted(
        body, rider, name="sw_forward", grid=(batch, n_pairs),
        out_shape=[jax.ShapeDtypeStruct((t, SW_WIDTH), F32), jax.ShapeDtypeStruct((t, SW_WIDTH), F32)],
        in_specs=[pl.BlockSpec(memory_space=pltpu.SMEM),
                  pl.BlockSpec((seq, LANES), lambda b, p: (b, q_blk + p)),
                  pl.BlockSpec((seq, LANES), lambda b, p: (b, k_blk)),
                  pl.BlockSpec((seq, LANES), lambda b, p: (b, k_blk + 1))],
        out_specs=[pl.BlockSpec((seq, LANES), lambda b, p: (b, p)), pl.BlockSpec((seq, LANES), lambda b, p: (b, p))],
        scratch_shapes=[pltpu.VMEM((pad, LANES), BF16)] * 4,
        compiler_params=_cparams(("arbitrary", "arbitrary")), args=[sink, proj, proj, proj])


def _out_proj(oa, ob, g_na, g_sw, w_out, x, mod3, g_ffn, seq):
    t, d = x.shape
    tm = TOKEN_TILE
    per_seq = seq // tm

    def body(oa_ref, ob_ref, gna_ref, gsw_ref, w_ref, x_ref, mod_ref, gf_ref, oab_ref, mix_ref, x1_ref, h2_ref):
        _, na = _rms_stats(oa_ref[...])
        _, nb = _rms_stats(ob_ref[...])
        oab = jnp.concatenate([na * gna_ref[...], nb * gsw_ref[...]], axis=1).astype(BF16)
        oab_ref[...] = oab
        mix = _mm(oab, w_ref[...])
        mix_ref[...] = mix
        gate_a = mod_ref[0, :, 2 * d:3 * d]
        shift_f, scale_f = mod_ref[0, :, 3 * d:4 * d], mod_ref[0, :, 4 * d:5 * d]
        x1 = x_ref[...] + gate_a * mix
        x1_ref[...] = x1
        _, xn = _rms_stats(x1)
        h2_ref[...] = ((xn * gf_ref[...]) * (1.0 + scale_f) + shift_f).astype(BF16)

    tile = lambda w: pl.BlockSpec((tm, w), lambda i: (i, 0))
    vec = lambda w: pl.BlockSpec((1, w), lambda i: (0, 0))
    return pl.pallas_call(
        body, name="out_proj", grid=(t // tm,),
        out_shape=(jax.ShapeDtypeStruct((t, d), BF16), jax.ShapeDtypeStruct((t, d), F32),
                   jax.ShapeDtypeStruct((t, d), F32), jax.ShapeDtypeStruct((t, d), BF16)),
        in_specs=[tile(NA_WIDTH), tile(SW_WIDTH), vec(NA_WIDTH), vec(SW_WIDTH),
                  pl.BlockSpec((d, d), lambda i: (0, 0)), tile(d),
                  pl.BlockSpec((1, 1, 6 * d), lambda i: (i // per_seq, 0, 0)), vec(d)],
        out_specs=(tile(d), tile(d), tile(d), tile(d)),
        compiler_params=_cparams(("arbitrary",), VMEM_BIG),
    )(oa, ob, g_na, g_sw, w_out, x, mod3, g_ffn)


def _up_proj(h2, w_up_halves, rider=None):
    t, d = h2.shape
    tm = 2 * TOKEN_TILE
    w_a, w_b = w_up_halves
    half, wcol = w_a.shape[1], w_a.shape[2]

    def body(h_ref, wa_ref, wb_ref, u_ref):
        u_ref[0] = (_mm(h_ref[:, :half], wa_ref[0]) + _mm(h_ref[:, half:], wb_ref[0])).astype(BF16)

    w_spec = pl.BlockSpec((1, half, wcol), lambda j, i: (j, 0, 0))
    return _hosted(
        body, rider, name="up_proj", grid=(N_SHARD, t // tm),
        out_shape=[jax.ShapeDtypeStruct((2, t, D_FF), BF16)],
        in_specs=[pl.BlockSpec((tm, d), lambda j, i: (i, 0)), w_spec, w_spec],
        out_specs=[pl.BlockSpec((1, tm, wcol), lambda j, i: (j // 2, i, j % 2))],
        scratch_shapes=[], compiler_params=_cparams(("arbitrary", "arbitrary"), VMEM_BIG), args=[h2, w_a, w_b])


def _taps_chunk(load, s, rows, seq):
    halo = 2 * SUBLANES
    cur = load(s, rows)
    above = load(pl.multiple_of(jnp.maximum(s - halo, 0), halo), halo)
    below = load(pl.multiple_of(jnp.minimum(s + rows, seq - halo), halo), halo)
    up = jnp.where(s > 0, above[halo - 1:halo, :], 0.0)
    dn = jnp.where(s + rows < seq, below[0:1, :], 0.0)
    row = lax.broadcasted_iota(jnp.int32, cur.shape, 0)
    prev = jnp.where(row == 0, up, pltpu.roll(cur, 1, 0))
    nxt = jnp.where(row == rows - 1, dn, pltpu.roll(cur, rows - 1, 0))
    return cur, prev, nxt


def _conv_gate(u, conv_w, conv_b, batch, seq, rider=None):
    t = u.shape[1]
    cw = FF_TILE
    rows = CONV_CHUNK

    def body(u_ref, w_ref, b_ref, a_ref):
        def chunk(i, carry):
            s = pl.multiple_of(i * rows, rows)
            gt, prev, nxt = _taps_chunk(lambda at, n: u_ref[1, pl.ds(at, n), :].astype(F32), s, rows, seq)
            gc = prev * w_ref[0:1, :] + gt * w_ref[1:2, :] + nxt * w_ref[2:3, :] + b_ref[...]
            a_ref[pl.ds(s, rows), :] = ((gc * _sigmoid(gc)) * u_ref[0, pl.ds(s, rows), :].astype(F32)).astype(BF16)
            return carry

        lax.fori_loop(0, seq // rows, chunk, 0)

    return _hosted(
        body, rider, name="conv_gate", grid=(batch, D_FF // cw),
        out_shape=[jax.ShapeDtypeStruct((t, D_FF), BF16)],
        in_specs=[pl.BlockSpec((2, seq, cw), lambda b, j: (0, b, j)),
                  pl.BlockSpec((3, cw), lambda b, j: (0, j)), pl.BlockSpec((1, cw), lambda b, j: (0, j))],
        out_specs=[pl.BlockSpec((seq, cw), lambda b, j: (b, j))], scratch_shapes=[],
        compiler_params=_cparams(("arbitrary", "arbitrary"), VMEM_BIG), args=[u, conv_w, conv_b])


def _down_and_loss(a, w_down, x1, mod3, g_final, target, seq):
    t, d = x1.shape
    tm = TOKEN_TILE
    per_seq = seq // tm
    batch = t // seq

    def body(a_ref, w_ref, x1_ref, mod_ref, g_ref, tgt_ref, dx2_ref, dffn_ref, loss_ref, dgate_ref, dg_ref):
        i = pl.program_id(0)
        f = _mm(a_ref[...], w_ref[...])
        gate_f = mod_ref[0, :, 5 * d:6 * d]
        x2 = x1_ref[...] + gate_f * f
        r, xn = _rms_stats(x2)
        err = xn * g_ref[...] - tgt_ref[...]
        part = 0.5 * jnp.sum(jnp.mean(err * err, axis=-1, keepdims=True))
        dy = err / d
        dx2 = _rms_bwd(dy * g_ref[...], xn, r)
        dx2_ref[...] = dx2
        dffn_ref[...] = (dx2 * gate_f).astype(BF16)

        @pl.when(i == 0)
        def _():
            loss_ref[...] = jnp.zeros_like(loss_ref)
            dg_ref[...] = jnp.zeros_like(dg_ref)

        @pl.when(i % per_seq == 0)
        def _():
            dgate_ref[...] = jnp.zeros_like(dgate_ref)

        loss_ref[...] += part
        dg_ref[...] += jnp.sum(dy * xn, axis=0, keepdims=True)
        dgate_ref[0] += jnp.sum(dx2 * f, axis=0, keepdims=True)

    tile = lambda w: pl.BlockSpec((tm, w), lambda i: (i, 0))
    return pl.pallas_call(
        body, name="down_loss", grid=(t // tm,),
        out_shape=(jax.ShapeDtypeStruct((t, d), F32), jax.ShapeDtypeStruct((t, d), BF16),
                   jax.ShapeDtypeStruct((SUBLANES, LANES), F32), jax.ShapeDtypeStruct((batch, 1, d), F32),
                   jax.ShapeDtypeStruct((1, d), F32)),
        in_specs=[tile(D_FF), _resident((D_FF, d)), tile(d),
                  pl.BlockSpec((1, 1, 6 * d), lambda i: (i // per_seq, 0, 0)),
                  pl.BlockSpec((1, d), lambda i: (0, 0)), tile(d)],
        out_specs=(tile(d), tile(d), pl.BlockSpec((SUBLANES, LANES), lambda i: (0, 0)),
                   pl.BlockSpec((1, 1, d), lambda i: (i // per_seq, 0, 0)), pl.BlockSpec((1, d), lambda i: (0, 0))),
        compiler_params=_cparams(("arbitrary",), VMEM_BIG),
    )(a, w_down, x1, mod3, g_final, target)


def _down_weight_grad(a, dffn):
    t, dff = a.shape
    d = dffn.shape[1]
    tk = 2 * TOKEN_TILE
    n_k = t // tk

    def body(a_ref, df_ref, g_ref, gb_ref):
        k = pl.program_id(0)

        @pl.when(k == 0)
        def _():
            g_ref[...] = jnp.zeros_like(g_ref)

        g_ref[...] += _mm_tn(a_ref[...], df_ref[...])

        @pl.when(k == n_k - 1)
        def _():
            gb_ref[...] = g_ref[...].astype(BF16)

    whole = _resident((dff, d))
    return pl.pallas_call(
        body, name="down_weight_grad", grid=(n_k,),
        out_shape=(jax.ShapeDtypeStruct((dff, d), F32), jax.ShapeDtypeStruct((dff, d), BF16)),
        in_specs=[pl.BlockSpec((tk, dff), lambda k: (k, 0)), pl.BlockSpec((tk, d), lambda k: (k, 0))],
        out_specs=(whole, whole),
        compiler_params=_cparams(("arbitrary",), VMEM_BIG),
    )(a, dffn)


def _ffn_backward(dffn, w_down, u, conv_w, conv_b, batch, seq, rider=None):
    t, d = dffn.shape
    cw = FF_TILE
    rows = CONV_CHUNK

    def body(df_ref, wd_ref, u_ref, w_ref, b_ref, du_ref, gcw_ref, gcb_ref, da_scr, dgc_scr):
        b = pl.program_id(1)
        da_scr[...] = _mm_nt(df_ref[...], wd_ref[...])

        @pl.when(b == 0)
        def _():
            gcw_ref[...] = jnp.zeros_like(gcw_ref)
            gcb_ref[...] = jnp.zeros_like(gcb_ref)

        def fold(v):
            return jnp.sum(v.reshape(rows // SUBLANES, SUBLANES, cw), axis=0)

        def chunk(i, carry):
            s = pl.multiple_of(i * rows, rows)
            here = pl.ds(s, rows)
            gt, prev, nxt = _taps_chunk(lambda at, n: u_ref[1, pl.ds(at, n), :].astype(F32), s, rows, seq)
            val, da = u_ref[0, here, :].astype(F32), da_scr[here, :]
            gc = prev * w_ref[0:1, :] + gt * w_ref[1:2, :] + nxt * w_ref[2:3, :] + b_ref[...]
            sg = _sigmoid(gc)
            sl = gc * sg
            du_ref[0, here, :] = (da * sl).astype(BF16)
            dgc = (da * val) * (sg * (1.0 + gc * (1.0 - sg)))
            dgc_scr[here, :] = dgc
            cb, c0, c1, c2 = carry
            return cb + fold(dgc), c0 + fold(dgc * prev), c1 + fold(dgc * gt), c2 + fold(dgc * nxt)

        zero = jnp.zeros((SUBLANES, cw), F32)
        cb, c0, c1, c2 = lax.fori_loop(0, seq // rows, chunk, (zero, zero, zero, zero))
        gcb_ref[...] += jnp.sum(cb, axis=0, keepdims=True)
        gcw_ref[0:1, :] += jnp.sum(c0, axis=0, keepdims=True)
        gcw_ref[1:2, :] += jnp.sum(c1, axis=0, keepdims=True)
        gcw_ref[2:3, :] += jnp.sum(c2, axis=0, keepdims=True)

        def chunk2(i, carry):
            s = pl.multiple_of(i * rows, rows)
            dgc, dprev, dnxt = _taps_chunk(lambda at, n: dgc_scr[pl.ds(at, n), :], s, rows, seq)
            du_ref[1, pl.ds(s, rows), :] = (dnxt * w_ref[0:1, :] + dgc * w_ref[1:2, :]
                                            + dprev * w_ref[2:3, :]).astype(BF16)
            return carry

        lax.fori_loop(0, seq // rows, chunk2, 0)

    return _hosted(
        body, rider, name="ffn_backward", grid=(D_FF // cw, batch),
        out_shape=[jax.ShapeDtypeStruct((2, t, D_FF), BF16),
                   jax.ShapeDtypeStruct((3, D_FF), F32), jax.ShapeDtypeStruct((1, D_FF), F32)],
        in_specs=[pl.BlockSpec((seq, d), lambda j, b: (b, 0)), pl.BlockSpec((cw, d), lambda j, b: (j, 0)),
                  pl.BlockSpec((2, seq, cw), lambda j, b: (0, b, j)),
                  pl.BlockSpec((3, cw), lambda j, b: (0, j)), pl.BlockSpec((1, cw), lambda j, b: (0, j))],
        out_specs=[pl.BlockSpec((2, seq, cw), lambda j, b: (0, b, j)),
                   pl.BlockSpec((3, cw), lambda j, b: (0, j)), pl.BlockSpec((1, cw), lambda j, b: (0, j))],
        scratch_shapes=[pltpu.VMEM((seq, cw), F32), pltpu.VMEM((seq, cw), F32)],
        compiler_params=_cparams(("arbitrary", "arbitrary"), VMEM_BIG), args=[dffn, w_down, u, conv_w, conv_b])


def _up_backward(du, w_up, x1, mod3, g_ffn, dx2, mix, seq, rider=None):
    _, t, _ = du.shape
    d = x1.shape[1]
    tm = TOKEN_TILE
    per_seq = seq // tm
    batch = t // seq
    w_a, w_b = w_up
    half, wcol = w_a.shape[1], w_a.shape[2]

    def body(du_ref, wa_ref, wb_ref, x1_ref, mod_ref, g_ref, dx2_ref, mix_ref,
             dx1_ref, dmix_ref, dsh_ref, dsc_ref, dga_ref, dg_ref):
        i = pl.program_id(0)
        parts = []
        for w_ref in (wa_ref, wb_ref):
            acc = jnp.zeros((tm, half), F32)
            for j in range(N_SHARD):
                acc = acc + _mm_nt(du_ref[j // 2, :, (j % 2) * wcol:(j % 2 + 1) * wcol], w_ref[j])
            parts.append(acc)
        dh = jnp.concatenate(parts, axis=1)
        gate_a = mod_ref[0, :, 2 * d:3 * d]
        scale_f = mod_ref[0, :, 4 * d:5 * d]
        r, xn = _rms_stats(x1_ref[...])
        xg = xn * g_ref[...]
        dxg = dh * (1.0 + scale_f)
        dx1 = dx2_ref[...] + _rms_bwd(dxg * g_ref[...], xn, r)
        dx1_ref[...] = dx1
        dmix_ref[...] = (dx1 * gate_a).astype(BF16)

        @pl.when(i == 0)
        def _():
            dg_ref[...] = jnp.zeros_like(dg_ref)

        @pl.when(i % per_seq == 0)
        def _():
            dsh_ref[...] = jnp.zeros_like(dsh_ref)
            dsc_ref[...] = jnp.zeros_like(dsc_ref)
            dga_ref[...] = jnp.zeros_like(dga_ref)

        dg_ref[...] += jnp.sum(dxg * xn, axis=0, keepdims=True)
        dsh_ref[0] += jnp.sum(dh, axis=0, keepdims=True)
        dsc_ref[0] += jnp.sum(dh * xg, axis=0, keepdims=True)
        dga_ref[0] += jnp.sum(dx1 * mix_ref[...], axis=0, keepdims=True)

    tile = lambda w: pl.BlockSpec((tm, w), lambda i: (i, 0))
    per_b = pl.BlockSpec((1, 1, d), lambda i: (i // per_seq, 0, 0))
    small = jax.ShapeDtypeStruct((batch, 1, d), F32)
    return _hosted(
        body, rider, name="up_backward", grid=(t // tm,),
        out_shape=[jax.ShapeDtypeStruct((t, d), F32), jax.ShapeDtypeStruct((t, d), BF16), small, small, small,
                   jax.ShapeDtypeStruct((1, d), F32)],
        in_specs=[pl.BlockSpec((2, tm, D_FF), lambda i: (0, i, 0)),
                  _resident((N_SHARD, half, wcol)), _resident((N_SHARD, half, wcol)), tile(d),
                  pl.BlockSpec((1, 1, 6 * d), lambda i: (i // per_seq, 0, 0)),
                  pl.BlockSpec((1, d), lambda i: (0, 0)), tile(d), tile(d)],
        out_specs=[tile(d), tile(d), per_b, per_b, per_b, pl.BlockSpec((1, d), lambda i: (0, 0))],
        scratch_shapes=[], compiler_params=_cparams(("arbitrary",), VMEM_BIG),
        args=[du, w_a, w_b, x1, mod3, g_ffn, dx2, mix])


def _up_weight_grad(h2, du, rider=None):
    t, d = h2.shape
    tk = 2 * TOKEN_TILE
    wcol = D_FF // 2
    half = d // 2
    n_k = t // tk

    def body(h_ref, du_ref, ga_ref, gb_ref, ga16_ref, gb16_ref):
        k = pl.program_id(1)

        @pl.when(k == 0)
        def _():
            ga_ref[...] = jnp.zeros_like(ga_ref)
            gb_ref[...] = jnp.zeros_like(gb_ref)

        du = du_ref[0]
        ga_ref[0] += _mm_tn(h_ref[:, :half], du)
        gb_ref[0] += _mm_tn(h_ref[:, half:], du)

        @pl.when(k == n_k - 1)
        def _():
            ga16_ref[...] = ga_ref[...].astype(BF16)
            gb16_ref[...] = gb_ref[...].astype(BF16)

    g_spec = pl.BlockSpec((1, half, wcol), lambda j, k: (j, 0, 0))
    f32_out = jax.ShapeDtypeStruct((N_SHARD, half, wcol), F32)
    b16_out = jax.ShapeDtypeStruct((N_SHARD, half, wcol), BF16)
    return _hosted(
        body, rider, name="up_weight_grad", grid=(N_SHARD, n_k),
        out_shape=[f32_out, f32_out, b16_out, b16_out],
        in_specs=[pl.BlockSpec((tk, d), lambda j, k: (k, 0)),
                  pl.BlockSpec((1, tk, wcol), lambda j, k: (j // 2, k, j % 2))],
        out_specs=[g_spec, g_spec, g_spec, g_spec], scratch_shapes=[],
        compiler_params=_cparams(("arbitrary", "arbitrary"), VMEM_BIG), args=[h2, du])


def _out_backward(dmix, w_out, oab, oa, ob, g_na, g_sw):
    t, d = dmix.shape
    tm = 2 * TOKEN_TILE
    hw = NA_WIDTH

    def body(dm_ref, w_ref, oab_ref, oa_ref, ob_ref, gna_ref, gsw_ref,
             doa_ref, dob_ref, gw_ref, gwb_ref, dgna_ref, dgsw_ref):
        @pl.when(pl.program_id(0) == 0)
        def _():
            gw_ref[...] = jnp.zeros_like(gw_ref)
            dgna_ref[...] = jnp.zeros_like(dgna_ref)
            dgsw_ref[...] = jnp.zeros_like(dgsw_ref)

        dm = dm_ref[...]
        gw_ref[...] += _mm_tn(oab_ref[...], dm)

        @pl.when(pl.program_id(0) == t // tm - 1)
        def _():
            gwb_ref[...] = gw_ref[...].astype(BF16)

        do = _mm_nt(dm, w_ref[...])
        for raw_ref, g_ref, dst_ref, dg_ref, lo in ((oa_ref, gna_ref, doa_ref, dgna_ref, 0),
                                                     (ob_ref, gsw_ref, dob_ref, dgsw_ref, hw)):
            r, xn = _rms_stats(raw_ref[...])
            dpart = do[:, lo:lo + hw]
            dg_ref[...] += jnp.sum(dpart * xn, axis=0, keepdims=True)
            dst_ref[...] = _rms_bwd(dpart * g_ref[...], xn, r).astype(BF16)

    tile = lambda w: pl.BlockSpec((tm, w), lambda i: (i, 0))
    vec = lambda w: pl.BlockSpec((1, w), lambda i: (0, 0))
    return pl.pallas_call(
        body, name="out_backward", grid=(t // tm,),
        out_shape=(jax.ShapeDtypeStruct((t, hw), BF16), jax.ShapeDtypeStruct((t, hw), BF16),
                   jax.ShapeDtypeStruct((d, d), F32), jax.ShapeDtypeStruct((d, d), BF16),
                   jax.ShapeDtypeStruct((1, hw), F32), jax.ShapeDtypeStruct((1, hw), F32)),
        in_specs=[tile(d), pl.BlockSpec((d, d), lambda i: (0, 0)), tile(d), tile(hw), tile(hw), vec(hw), vec(hw)],
        out_specs=(tile(hw), tile(hw), pl.BlockSpec((d, d), lambda i: (0, 0)), pl.BlockSpec((d, d), lambda i: (0, 0)),
                   vec(hw), vec(hw)),
        compiler_params=_cparams(("arbitrary",), VMEM_BIG),
    )(dmix, w_out, oab, oa, ob, g_na, g_sw)


def _na_backward(proj, d_o, lse, tiles, batch, seq, rider=None):
    t = proj.shape[0]
    n_rows = seq // GRID_W
    n_pairs = NA_WIDTH // LANES
    win = NA_ROWS * GRID_W
    n_tiles = 2 * NA_ROWS - 2

    def body(q_ref, k_ref, v_ref, do_ref, lse_ref, tp_ref, dq_ref, dk_ref, dv_ref, dtp_ref, km, vm, dk_acc, dv_acc):
        @pl.when(pl.program_id(1) == 0)
        def _():
            dtp_ref[...] = jnp.zeros_like(dtp_ref)

        _na_prepare(k_ref, v_ref, km, vm)
        dk_acc[...] = jnp.zeros_like(dk_acc)
        dv_acc[...] = jnp.zeros_like(dv_acc)
        low = lax.broadcasted_iota(jnp.int32, (win, LANES), 1) < HEAD_DIM

        def scores(r):
            rs, off = _na_window(r, n_rows)
            rows = pl.ds(pl.multiple_of(r * GRID_W, GRID_W), GRID_W)
            wrows = pl.ds(pl.multiple_of(rs * GRID_W, GRID_W), win)
            q, do = q_ref[rows, :], do_ref[rows, :]
            k2 = _na_pair_window(km, wrows)
            s = _na_scores(q, k2, tp_ref, off)
            dp = _mm_nt(do, _na_pair_window(vm, wrows))
            return rows, wrows, off, q, do, k2, s, dp

        def finish(rows, wrows, off, q, do, k2, s, dp):
            p = _pair_probs_from_lse(s, lse_ref[rows, :])
            parts = []
            for h in range(2):
                ph, dph = p[:, h * win:(h + 1) * win], dp[:, h * win:(h + 1) * win]
                dsh = ph * (dph - jnp.sum(ph * dph, axis=-1, keepdims=True))
                for w in range(NA_ROWS // 2):
                    dtp_ref[h, 2 * w - off + (NA_ROWS - 1)] += dsh[:, w * LANES:(w + 1) * LANES]
                parts.append(dsh)
            dsb = (jnp.concatenate(parts, axis=1) * QK_SCALE).astype(BF16)
            dq_ref[rows, :] = _mm(dsb, k2).astype(BF16)
            dk_acc[wrows, :] += _pair_grad(dsb, q, low)
            dv_acc[wrows, :] += _pair_grad(p.astype(BF16), do, low)

        def row_group(i, carry):
            for state in [scores(NA_GROUP * i + j) for j in range(NA_GROUP)]:
                finish(*state)
            return carry

        lax.fori_loop(0, n_rows // NA_GROUP, row_group, 0)
        dk_ref[...] = dk_acc[...].astype(BF16)
        dv_ref[...] = dv_acc[...].astype(BF16)

    blk = lambda off: pl.BlockSpec((seq, LANES), lambda p, b: (b, off + p))
    out = jax.ShapeDtypeStruct((t, NA_WIDTH), BF16)
    return _hosted(
        body, rider, name="na_backward", grid=(n_pairs, batch),
        out_shape=[out, out, out, jax.ShapeDtypeStruct(tiles.shape, F32)],
        in_specs=[blk(0), blk(n_pairs), blk(2 * n_pairs), blk(0), blk(0),
                  pl.BlockSpec((2, n_tiles, GRID_W, LANES), lambda p, b: (p, 0, 0, 0))],
        out_specs=[blk(0), blk(0), blk(0), pl.BlockSpec((2, n_tiles, GRID_W, LANES), lambda p, b: (p, 0, 0, 0))],
        scratch_shapes=[pltpu.VMEM((2, seq, LANES), BF16), pltpu.VMEM((2, seq, LANES), BF16),
                        pltpu.VMEM((seq, LANES), F32), pltpu.VMEM((seq, LANES), F32)],
        compiler_params=_cparams(("arbitrary", "arbitrary")), args=[proj, proj, proj, d_o, lse, tiles])


def _na_bias_grad(dtiles, expand):
    n = dtiles.shape[0]

    def body(t_ref, e_ref, o_ref):
        flat = jnp.concatenate([t_ref[:, qq, :] for qq in range(GRID_W)], axis=1)
        o_ref[...] = lax.dot_general(flat, e_ref[...], (((1,), (1,)), ((), ())),
                                     precision=lax.Precision.HIGHEST, preferred_element_type=F32)

    return body, [dtiles, expand], jax.ShapeDtypeStruct((n, expand.shape[0]), F32)


def _sw_backward(proj, d_o, lse, sink, batch, seq, rider=None):
    t = proj.shape[0]
    n_pairs = SW_WIDTH // LANES
    q_blk = 3 * NA_WIDTH // LANES
    k_blk = q_blk + n_pairs
    n_blocks = seq // SW_BLOCK
    pad = seq + 2 * SW_BLOCK

    def body(sink_ref, q_ref, k_ref, v_ref, do_ref, lse_ref, dq_ref, dk_ref, dv_ref, dsk_ref,
             k_lo, k_hi, v_lo, v_hi, dk_loc, dv_loc, dk_tot, dv_tot):
        hp = pl.program_id(1)
        g = hp // 2

        @pl.when(hp % 2 == 0)
        def _():
            _sw_prepare(k_ref, g, k_lo, k_hi, seq)
            _sw_prepare(v_ref, g, v_lo, v_hi, seq)
            dk_loc[...] = jnp.zeros_like(dk_loc)
            dv_loc[...] = jnp.zeros_like(dv_loc)

        @pl.when(hp == 0)
        def _():
            dk_tot[...] = jnp.zeros_like(dk_tot)
            dv_tot[...] = jnp.zeros_like(dv_tot)

        band = 3 * SW_BLOCK
        low = lax.broadcasted_iota(jnp.int32, (band, LANES), 1) < HEAD_DIM

        sinks = (sink_ref[2 * hp], sink_ref[2 * hp + 1])

        def scores(n):
            rows = pl.ds(pl.multiple_of(n * SW_BLOCK, SW_BLOCK), SW_BLOCK)
            wrows = pl.ds(pl.multiple_of(n * SW_BLOCK, SW_BLOCK), band)
            qb, do = q_ref[rows, :], do_ref[rows, :]
            k2 = jnp.concatenate([k_lo[wrows, :], k_hi[wrows, :]], axis=0)
            v2 = jnp.concatenate([v_lo[wrows, :], v_hi[wrows, :]], axis=0)
            return n, rows, wrows, qb, do, k2, _mm_nt(qb, k2) * QK_SCALE, _mm_nt(do, v2)

        def finish(sink_acc, n, rows, wrows, qb, do, k2, s2, dp):
            p, ps = _sw_probs_from_lse(s2, _sw_mask(n, seq), sinks, lse_ref[rows, :])
            parts, new = [], []
            for i in range(2):
                ph, dph = p[:, i * band:(i + 1) * band], dp[:, i * band:(i + 1) * band]
                delta = jnp.sum(ph * dph, axis=-1, keepdims=True)
                parts.append(ph * (dph - delta))
                new.append(sink_acc[i] - ps[i] * delta)
            dsb = (jnp.concatenate(parts, axis=1) * QK_SCALE).astype(BF16)
            dq_ref[rows, :] = _mm(dsb, k2)
            dk_loc[wrows, :] += _pair_grad(dsb, qb, low)
            dv_loc[wrows, :] += _pair_grad(p.astype(BF16), do, low)
            return tuple(new)

        def block_group(i, carry):
            for state in [scores(SW_GROUP_BLOCKS * i + j) for j in range(SW_GROUP_BLOCKS)]:
                carry = finish(carry, *state)
            return carry

        zero = jnp.zeros((SW_BLOCK, 1), F32)
        s0, s1 = lax.fori_loop(0, n_blocks // SW_GROUP_BLOCKS, block_group, (zero, zero))
        row = lax.broadcasted_iota(jnp.int32, (SUBLANES, LANES), 0)
        dsk_ref[0, 0] = jnp.where(row == 0, jnp.sum(s0), jnp.where(row == 1, jnp.sum(s1), 0.0))

        @pl.when(hp % 2 == 1)
        def _():
            lane_s = lax.broadcasted_iota(jnp.int32, (seq, LANES), 1)
            mine_g = (lane_s // HEAD_DIM) == g
            for loc, tot in ((dk_loc, dk_tot), (dv_loc, dv_tot)):
                part = loc[SW_BLOCK:SW_BLOCK + seq, :]
                tot[...] += jnp.where(mine_g, part + pltpu.roll(part, HEAD_DIM, 1), 0.0)

        @pl.when(hp == n_pairs - 1)
        def _():
            dk_ref[...] = dk_tot[...]
            dv_ref[...] = dv_tot[...].astype(BF16)

    return _hosted(
        body, rider, name="sw_backward", grid=(batch, n_pairs),
        out_shape=[jax.ShapeDtypeStruct((t, SW_WIDTH), F32), jax.ShapeDtypeStruct((t, LANES), F32),
                   jax.ShapeDtypeStruct((t, LANES), BF16), jax.ShapeDtypeStruct((batch, n_pairs, SUBLANES, LANES), F32)],
        in_specs=[pl.BlockSpec(memory_space=pltpu.SMEM),
                  pl.BlockSpec((seq, LANES), lambda b, p: (b, q_blk + p)),
                  pl.BlockSpec((seq, LANES), lambda b, p: (b, k_blk)),
                  pl.BlockSpec((seq, LANES), lambda b, p: (b, k_blk + 1)),
                  pl.BlockSpec((seq, LANES), lambda b, p: (b, p)), pl.BlockSpec((seq, LANES), lambda b, p: (b, p))],
        out_specs=[pl.BlockSpec((seq, LANES), lambda b, p: (b, p)), pl.BlockSpec((seq, LANES), lambda b, p: (b, 0)),
                   pl.BlockSpec((seq, LANES), lambda b, p: (b, 0)),
                   pl.BlockSpec((1, 1, SUBLANES, LANES), lambda b, p: (b, p, 0, 0))],
        scratch_shapes=[pltpu.VMEM((pad, LANES), BF16)] * 4 + [pltpu.VMEM((pad, LANES), F32)] * 2
        + [pltpu.VMEM((seq, LANES), F32)] * 2,
        compiler_params=_cparams(("arbitrary", "arbitrary")), args=[sink, proj, proj, proj, d_o, lse])


def _in_backward(dqkv_a, dq_b, dk_b, dv_b, w_in_t, h1, x, mod3, g_attn, dx1, cos_t, sin_t, seq):
    t, d = x.shape
    tm = TOKEN_TILE
    per_seq = seq // tm
    batch = t // seq
    dqa, dka, dva = dqkv_a
    n_q = SW_WIDTH // LANES

    def body(dqa_ref, dka_ref, dva_ref, dqb_ref, dkb_ref, dvb_ref, w_ref, h_ref, x_ref, mod_ref, g_ref, dx1_ref,
             cos_ref, sin_ref, dx_ref, gw_ref, gwb_ref, dsh_ref, dsc_ref, dg_ref):
        i = pl.program_id(0)

        @pl.when(i == 0)
        def _():
            gw_ref[...] = jnp.zeros_like(gw_ref)
            dg_ref[...] = jnp.zeros_like(dg_ref)

        @pl.when(i % per_seq == 0)
        def _():
            dsh_ref[...] = jnp.zeros_like(dsh_ref)
            dsc_ref[...] = jnp.zeros_like(dsc_ref)

        dr = jnp.concatenate([dqb_ref[...], dkb_ref[...]], axis=1)
        cos = jnp.concatenate([cos_ref[...]] * (n_q + 1), axis=1)
        sin = jnp.concatenate([sin_ref[...]] * (n_q + 1), axis=1)
        dr = dr * cos + _rope_rot(dr * sin)
        dproj = jnp.concatenate([dqa_ref[...], dka_ref[...], dva_ref[...], dr.astype(BF16), dvb_ref[...]], axis=1)
        gw_ref[...] += _mm_tn(dproj, h_ref[...])

        @pl.when(i == t // tm - 1)
        def _():
            gwb_ref[...] = gw_ref[...].astype(BF16)

        dh = _mm(dproj, w_ref[...])
        scale = mod_ref[0, :, d:2 * d]
        r, xn = _rms_stats(x_ref[...])
        xg = xn * g_ref[...]
        dxg = dh * (1.0 + scale)
        dx_ref[...] = dx1_ref[...] + _rms_bwd(dxg * g_ref[...], xn, r)
        dg_ref[...] += jnp.sum(dxg * xn, axis=0, keepdims=True)
        dsh_ref[0] += jnp.sum(dh, axis=0, keepdims=True)
        dsc_ref[0] += jnp.sum(dh * xg, axis=0, keepdims=True)

    tile = lambda w: pl.BlockSpec((tm, w), lambda i: (i, 0))
    per_b = pl.BlockSpec((1, 1, d), lambda i: (i // per_seq, 0, 0))
    small = jax.ShapeDtypeStruct((batch, 1, d), F32)
    rope = pl.BlockSpec((tm, LANES), lambda i: (i % per_seq, 0))
    return pl.pallas_call(
        body, name="in_backward", grid=(t // tm,),
        out_shape=(jax.ShapeDtypeStruct((t, d), F32), jax.ShapeDtypeStruct((IN_WIDTH, d), F32),
                   jax.ShapeDtypeStruct((IN_WIDTH, d), BF16), small, small, jax.ShapeDtypeStruct((1, d), F32)),
        in_specs=[tile(NA_WIDTH), tile(NA_WIDTH), tile(NA_WIDTH), tile(SW_WIDTH), tile(LANES), tile(LANES),
                  _resident((IN_WIDTH, d)), tile(d), tile(d),
                  pl.BlockSpec((1, 1, 6 * d), lambda i: (i // per_seq, 0, 0)),
                  pl.BlockSpec((1, d), lambda i: (0, 0)), tile(d), rope, rope],
        out_specs=(tile(d), _resident((IN_WIDTH, d)), _resident((IN_WIDTH, d)),
                   per_b, per_b, pl.BlockSpec((1, d), lambda i: (0, 0))),
        compiler_params=_cparams(("arbitrary",), VMEM_BIG),
    )(dqa, dka, dva, dq_b, dk_b, dv_b, w_in_t, h1, x, mod3, g_attn, dx1, cos_t, sin_t)


def _ada_weight_grad(sc_all, dmod_cols):
    d = sc_all.shape[1]
    ncol = dmod_cols.shape[1]

    def body(s_ref, m_ref, o_ref):
        o_ref[...] = _mm_tn(s_ref[...].astype(BF16), m_ref[...].astype(BF16))

    return pl.pallas_call(
        body, name="ada_weight_grad",
        out_shape=jax.ShapeDtypeStruct((d, ncol), F32),
        compiler_params=_cparams(vmem=VMEM_BIG),
    )(sc_all, dmod_cols)


def _row_tile(rows, cols):
    target = max(SUBLANES, (1 << 20) // (4 * cols))
    best = rows
    for cand in range(SUBLANES, rows + 1, SUBLANES):
        if rows % cand == 0 and cand <= target:
            best = cand
    return best if rows % SUBLANES == 0 else rows


def _sum_slots(results, name, rider=None, passenger=None):
    extra_body, extra_in, extra_out = passenger if passenger is not None else (None, [], None)
    n_extra = len(extra_in)
    parts = [group for groups in results for group in groups]
    result_of = [k for k, groups in enumerate(results) for _ in groups]
    n_parts = len(parts)
    cols = [results[k][0][1].shape[1] for k in result_of]
    tr = [_row_tile(min(own.shape[0] for _, own in results[k]), c) for k, c in zip(result_of, cols)]
    assert all(own.shape[0] % r == 0 and own.shape[1] == c for (_, own), r, c in zip(parts, tr, cols))
    tiles = [own.shape[0] // r for (_, own), r in zip(parts, tr)]
    first = [sum(tiles[:q]) for q in range(n_parts)]

    def body(*refs):
        o_refs = refs[2 * n_parts + n_extra:]
        step = pl.program_id(0)
        if passenger is not None:
            pl.when(step == 0)(lambda: extra_body(*refs[2 * n_parts:2 * n_parts + n_extra], o_refs[len(results)]))
        for q in range(n_parts):
            @pl.when((step >= first[q]) & (step < first[q] + tiles[q]))
            def _(q=q):
                p_ref, own_ref = refs[2 * q], refs[2 * q + 1]
                o_refs[result_of[q]][...] = (((own_ref[...] + p_ref[0].astype(F32)) + p_ref[1].astype(F32))
                                             + p_ref[2].astype(F32))

    def tile(start, count):
        return lambda i: jnp.clip(i - start, 0, count - 1)

    in_specs, args = [], []
    for q, (recv, own) in enumerate(parts):
        at = tile(first[q], tiles[q])
        in_specs.append(pl.BlockSpec((N_SHARD - 1, tr[q], cols[q]), lambda i, at=at: (0, at(i), 0)))
        in_specs.append(pl.BlockSpec((tr[q], cols[q]), lambda i, at=at: (at(i), 0)))
        args += [recv, own]
    out_shape, out_specs = [], []
    for k in range(len(results)):
        mine = [q for q in range(n_parts) if result_of[q] == k]
        count = sum(tiles[q] for q in mine)
        at = tile(first[mine[0]], count)
        out_shape.append(jax.ShapeDtypeStruct((count * tr[mine[0]], cols[mine[0]]), F32))
        out_specs.append(pl.BlockSpec((tr[mine[0]], cols[mine[0]]), lambda i, at=at: (at(i), 0)))
    whole = lambda a: pl.BlockSpec(a.shape, lambda i, nd=len(a.shape): (0,) * nd)
    in_specs += [whole(a) for a in extra_in]
    args += extra_in
    if passenger is not None:
        out_shape.append(extra_out)
        out_specs.append(whole(extra_out))
    return _hosted(body, rider, name=name, grid=(sum(tiles),), out_shape=out_shape, in_specs=in_specs,
                   out_specs=out_specs, scratch_shapes=[], compiler_params=_cparams(("arbitrary",), VMEM_BIG), args=args)


def _adamw_math(w, g, m, v):
    m2 = ADAM_B1 * m + (1.0 - ADAM_B1) * g
    v2 = ADAM_B2 * v + (1.0 - ADAM_B2) * (g * g)
    m_hat = m2 / (1.0 - ADAM_B1 ** ADAM_STEP)
    v_hat = v2 / (1.0 - ADAM_B2 ** ADAM_STEP)
    return -ADAM_LR * (m_hat / (jnp.sqrt(v_hat) + ADAM_EPS) + ADAM_WD * w), m2, v2


def _small_sums(partials, dmod, rider=None):
    moving = list(partials) + [dmod]
    n_mov = len(moving)

    def body(*refs):
        mov, refs = refs[:n_mov], refs[n_mov:]
        sums_out, refs = refs[:n_mov - 1], refs[n_mov - 1:]
        b_out, dmod_out, refs = refs[0], refs[1], refs[2:]
        everyone, (ssem, rsem) = refs[:n_mov], refs[n_mov:]
        x, y, c = _my_pos()
        me = 4 * x + 2 * y + c
        cps = []
        for a in range(n_mov):
            everyone[a][me] = mov[a][...]
            for k in range(1, N_DEV):
                peer = (_flip(x, (k >> 2) & 1), _flip(y, (k >> 1) & 1), _flip(c, k & 1))
                cps.append(pltpu.make_async_remote_copy(
                    src_ref=everyone[a].at[me], dst_ref=everyone[a].at[me], send_sem=ssem.at[a, k - 1],
                    recv_sem=rsem.at[a, k - 1], device_id=peer, device_id_type=MESH))
        for cp in cps:
            cp.start()
        for cp in cps:
            cp.wait_recv()

        def total(a):
            acc = everyone[a][0]
            for dev in range(1, N_DEV):
                acc = acc + everyone[a][dev]
            return acc

        for a in range(n_mov - 1):
            sums_out[a][...] = total(a)
        b_out[...] = jnp.sum(total(n_mov - 1), axis=0, keepdims=True)
        dmod_out[...] = everyone[n_mov - 1][...]
        for cp in cps:
            cp.wait_send()

    vm = pl.BlockSpec(memory_space=pltpu.VMEM)
    sds = jax.ShapeDtypeStruct
    out_shape = [sds(p.shape, F32) for p in partials]
    out_shape += [sds((1, dmod.shape[1]), F32), sds((N_DEV,) + dmod.shape, F32)]
    return _hosted(
        body, rider, name="small_sums", grid=(), out_shape=out_shape,
        in_specs=[vm] * n_mov, out_specs=[vm] * len(out_shape),
        scratch_shapes=[pltpu.VMEM((N_DEV,) + a.shape, F32) for a in moving]
        + [pltpu.SemaphoreType.DMA((n_mov, N_DEV - 1)), pltpu.SemaphoreType.DMA((n_mov, N_DEV - 1))],
        compiler_params=_cparams(vmem=VMEM_BIG), args=moving)


def _small_adamw(states, grads):
    n = len(states)

    def body(*refs):
        g_refs, wmv, res = refs[:n], refs[n:4 * n], refs[4 * n:]
        for j in range(n):
            g = g_refs[j][...]
            delta, m2, v2 = _adamw_math(wmv[3 * j][...], g, wmv[3 * j + 1][...], wmv[3 * j + 2][...])
            res[4 * j][...] = g
            res[4 * j + 1][...] = delta
            res[4 * j + 2][...] = m2
            res[4 * j + 3][...] = v2

    out_shape = []
    for w, _, _ in states:
        out_shape += [jax.ShapeDtypeStruct(w.shape, F32)] * 4
    outs = pl.pallas_call(body, name="small_adamw", out_shape=tuple(out_shape),
                          compiler_params=_cparams(vmem=VMEM_BIG))(*grads, *[a for st in states for a in st])
    return [outs[4 * j:4 * j + 4] for j in range(n)]


def _adamw(w, grads, m, v, name):
    rows, cols = w.shape
    tr = _row_tile(rows, cols)
    ng = len(grads)

    def body(*refs):
        w_ref = refs[0]
        g_refs = refs[1:1 + ng]
        m_ref, v_ref = refs[1 + ng], refs[2 + ng]
        g_out, d_out, m_out, v_out = refs[3 + ng:]
        g = g_refs[0][...]
        for extra in g_refs[1:]:
            g = g + extra[...]
        g_out[...] = g
        d_out[...], m_out[...], v_out[...] = _adamw_math(w_ref[...], g, m_ref[...], v_ref[...])

    spec = pl.BlockSpec((tr, cols), lambda i: (i, 0))
    out = jax.ShapeDtypeStruct((rows, cols), F32)
    return pl.pallas_call(
        body, name=name, grid=(rows // tr,),
        out_shape=(out, out, out, out),
        in_specs=[spec] * (3 + ng), out_specs=(spec, spec, spec, spec),
        compiler_params=_cparams(("arbitrary",)),
    )(w, *grads, m, v)


def _rope_tables(seq):
    half = HEAD_DIM // 2
    inv = np.float32(ROPE_THETA) ** (-np.arange(half, dtype=np.float32) / np.float32(half))
    ang = (np.arange(seq, dtype=np.float32)[:, None] * inv[None, :]).astype(np.float64)
    cos, sin = np.cos(ang).astype(np.float32), np.sin(ang).astype(np.float32)
    cos_t = np.concatenate([cos, cos, cos, cos], axis=1)
    sin_t = np.concatenate([-sin, sin, -sin, sin], axis=1)
    return jnp.asarray(cos_t), jnp.asarray(sin_t)


def kernel(x, c, w_ada, b_ada, g_attn, w_in, na_rpb, sw_sink, g_na_out, g_sw_out, w_out, g_ffn, w_up, conv_w, conv_b, w_down, g_final, loss_target, m_w_ada, m_b_ada, m_g_attn, m_w_in, m_na_rpb, m_sw_sink, m_g_na_out, m_g_sw_out, m_w_out, m_g_ffn, m_w_up, m_conv_w, m_conv_b, m_w_down, m_g_final, v_w_ada, v_b_ada, v_g_attn, v_w_in, v_na_rpb, v_sw_sink, v_g_na_out, v_g_sw_out, v_w_out, v_g_ffn, v_w_up, v_conv_w, v_conv_b, v_w_down, v_g_final):
    batch, seq, d = x.shape
    t = batch * seq
    assert d == D_MODEL and seq % (NA_ROWS * GRID_W) == 0 and seq % TOKEN_TILE == 0 and batch <= SUBLANES
    shard = 2 * lax.axis_index("x") + lax.axis_index("y")
    xt = x.reshape(t, d)
    tgt = loss_target.reshape(t, d)

    c8 = jnp.pad(c, ((0, SUBLANES - batch), (0, 0)))
    w_in_t_s = jnp.transpose(w_in[0]).astype(BF16)
    (mod8, sc_all), (w_in_g,) = _ada_forward(c8, w_ada[0], b_ada, _Rider("gather", [w_in_t_s]))
    mod3 = mod8[:batch].reshape(batch, 1, 6 * d)
    w_in_t = w_in_g.reshape(IN_WIDTH, d)

    cos_t, sin_t = _rope_tables(seq)
    (h1, proj), _ = _in_proj(xt, mod3, g_attn, w_in_t, cos_t, sin_t, seq)
    n_heads = NA_WIDTH // HEAD_DIM
    n_tiles, n_dc = 2 * NA_ROWS - 2, 2 * NA_COLS - 1
    expand, neg_mask = _na_bias_pattern()
    rpb = na_rpb[0]
    rows2 = jnp.concatenate([rpb[:, :-1, :], rpb[:, 1:, :]], axis=2).reshape(n_heads * n_tiles, 2 * n_dc)
    rows2 = jnp.pad(rows2, ((0, 0), (0, GRID_W - 2 * n_dc)))
    tiles = _na_bias_tiles(rows2, expand, neg_mask).reshape(n_heads, n_tiles, GRID_W, LANES)
    sink = sw_sink[0]
    w_up_b16 = w_up[0].astype(BF16)
    (oa, lse_a), (w_up_a,) = _na_forward(proj, tiles, batch, seq, _Rider("gather", [w_up_b16[:d // 2]]))
    (ob, lse_b), (w_up_b, conv_w_g, w_out_g) = _sw_forward(
        proj, sink, batch, seq, _Rider("gather", [w_up_b16[d // 2:], conv_w[0], w_out[0].astype(BF16)]))
    w_up_f = (w_up_a, w_up_b)
    w_out_f = w_out_g.reshape(d, d)
    conv_w_f = jnp.transpose(conv_w_g, (1, 0, 2)).reshape(3, D_FF)
    oab, mix, x1, h2 = _out_proj(oa, ob, g_na_out, g_sw_out, w_out_f, xt, mod3, g_ffn, seq)
    (u,), _ = _up_proj(h2, w_up_f)
    (a,), (w_down_g,) = _conv_gate(u, conv_w_f, conv_b, batch, seq, _Rider("gather", [w_down[0].astype(BF16)]))
    w_down_f = w_down_g.reshape(D_FF, d)
    dx2, dffn, loss_part, dgate_f, dg_final = _down_and_loss(a, w_down_f, x1, mod3, g_final.reshape(1, d), tgt, seq)

    gw_down, gw_down_b = _down_weight_grad(a, dffn)
    blocks = lambda g, rows: g.reshape(N_SHARD, rows // N_SHARD, d)
    (du, gconv_w, gconv_b), (recv_down, own_down) = _ffn_backward(
        dffn, w_down_f, u, conv_w_f, conv_b, batch, seq,
        _Rider("scatter", [blocks(gw_down_b, D_FF)], [blocks(gw_down, D_FF)]))
    (gw_up_top, gw_up_bot, gw_up_top_b, gw_up_bot_b), _ = _up_weight_grad(h2, du)
    (dx1, dmix, dshift_f, dscale_f, dgate_a, dg_ffn), _ = _up_backward(du, w_up_f, x1, mod3, g_ffn, dx2, mix, seq)
    doa, dob, gw_out, gw_out_b, dg_na, dg_sw = _out_backward(dmix, w_out_f, oab, oa, ob, g_na_out, g_sw_out)
    (dqa, dka, dva, dtiles), (recv_out, recv_up_bot, own_out, own_up_bot) = _na_backward(
        proj, doa, lse_a, tiles, batch, seq,
        _Rider("scatter", [blocks(gw_out_b, d), gw_up_bot_b], [blocks(gw_out, d), gw_up_bot]))
    (dq_b, dk_b, dv_b, dsink_parts), (recv_up_top, own_up_top) = _sw_backward(
        proj, dob, lse_b, sink, batch, seq, _Rider("scatter", [gw_up_top_b], [gw_up_top]))
    gx, gw_in_t, gw_in_b, dshift_a, dscale_a, dg_attn = _in_backward(
        (dqa, dka, dva), dq_b, dk_b, dv_b, w_in_t, h1, xt, mod3, g_attn, dx1, cos_t, sin_t, seq)

    late, (recv_in, own_in) = _sum_slots(
        [[(recv_out, own_out)], [(recv_up_top, own_up_top), (recv_up_bot, own_up_bot)], [(recv_down, own_down)]],
        "sum_w_out_up_down", _Rider("scatter", [blocks(gw_in_b, IN_WIDTH)], [blocks(gw_in_t, IN_WIDTH)]),
        _na_bias_grad(dtiles.reshape(n_heads * n_tiles, GRID_W, LANES), expand))

    red = late.pop()[:, :2 * n_dc]
    red = red.reshape(n_heads, n_tiles, 2, n_dc)
    zero_row = jnp.zeros((n_heads, 1, n_dc), F32)
    g_rpb = (jnp.concatenate([red[:, :, 0, :], zero_row], axis=1)
             + jnp.concatenate([zero_row, red[:, :, 1, :]], axis=1))
    g_sink = jnp.sum(dsink_parts[:, :, :2, 0], axis=0).reshape(SW_WIDTH // HEAD_DIM)

    dmod = jnp.concatenate([dshift_a, dscale_a, dgate_a, dshift_f, dscale_f, dgate_f], axis=2).reshape(batch, 6 * d)
    rpb_shape = na_rpb.shape[1:]
    states = [(g_attn, m_g_attn, v_g_attn),
              (na_rpb.reshape(rpb_shape), m_na_rpb.reshape(rpb_shape), v_na_rpb.reshape(rpb_shape)),
              (sw_sink, m_sw_sink, v_sw_sink), (g_na_out, m_g_na_out, v_g_na_out), (g_sw_out, m_g_sw_out, v_g_sw_out),
              (g_ffn, m_g_ffn, v_g_ffn), (conv_b, m_conv_b, v_conv_b),
              (g_final.reshape(1, d), m_g_final.reshape(1, d), v_g_final.reshape(1, d))]
    partials = [dg_attn, g_rpb, g_sink.reshape(sw_sink.shape), dg_na, dg_sw, dg_ffn, gconv_b, dg_final,
                gconv_w, loss_part]
    mine = _sum_slots([[(recv_in, own_in)]], "sum_w_in")[0] + late
    small, theirs = _small_sums(partials, dmod, _Rider("swap", mine))
    g_conv_w_full, loss_sum, g_b_ada, dmod_all = small[len(states):]
    r_small = _small_adamw(states + [(b_ada, m_b_ada, v_b_ada)], small[:len(states)] + [g_b_ada])
    loss = loss_sum[0, 0]
    dmod_rows = jnp.pad(dmod_all, ((0, 0), (0, SUBLANES - batch), (0, 0))).reshape(N_DEV * SUBLANES, 6 * d)
    ncol = w_ada.shape[2]
    g_w_ada = _ada_weight_grad(sc_all, lax.dynamic_slice(dmod_rows, (0, shard * ncol), (N_DEV * SUBLANES, ncol)))
    cshard = conv_w.shape[2]
    g_conv_w = lax.dynamic_slice(g_conv_w_full, (0, shard * cshard), (3, cshard))

    def big(w, m, v, g_parts, name):
        shape = w.shape
        outs = _adamw(w[0], g_parts, m[0], v[0], name)
        return [o.reshape(shape) for o in outs]

    r_w_ada = big(w_ada, m_w_ada, v_w_ada, [g_w_ada], "adamw_w_ada")
    r_w_in = [jnp.transpose(o).reshape(w_in.shape) for o in
              _adamw(jnp.transpose(w_in[0]), [mine[0], theirs[0]], jnp.transpose(m_w_in[0]), jnp.transpose(v_w_in[0]),
                     "adamw_w_in")]
    r_w_out = big(w_out, m_w_out, v_w_out, [mine[1], theirs[1]], "adamw_w_out")
    r_w_up = big(w_up, m_w_up, v_w_up, [mine[2], theirs[2]], "adamw_w_up")
    r_w_down = big(w_down, m_w_down, v_w_down, [mine[3], theirs[3]], "adamw_w_down")

    r_conv_w = big(conv_w, m_conv_w, v_conv_w, [g_conv_w], "adamw_conv_w")

    def pick(k):
        ga_, rpb_, sk_, gna_, gsw_, gf_, cb_, gfin_, b_ = [r[k] for r in r_small]
        return [r_w_ada[k], b_, ga_, r_w_in[k], rpb_.reshape(na_rpb.shape), sk_, gna_, gsw_, r_w_out[k], gf_,
                r_w_up[k], r_conv_w[k], cb_, r_w_down[k], gfin_.reshape(d)]

    return (loss, gx.reshape(batch, seq, d), *pick(0), *pick(1), *pick(2), *pick(3))
```

```python
import jax
import jax.numpy as jnp
import numpy as np
from jax import lax
from jax.experimental import pallas as pl
from jax.experimental.pallas import tpu as pltpu

F32 = jnp.float32
BF16 = jnp.bfloat16
MESH = pl.DeviceIdType.MESH

D_MODEL = 1024
HEAD_DIM = 64
NA_WIDTH = 512
SW_WIDTH = 512
SW_KV_WIDTH = 128
IN_WIDTH = 2304
D_FF = 2816
GRID_W = 64
NA_ROWS = 8
NA_COLS = 16
SW_BLOCK = 128
ROPE_THETA = 10000.0
EPS = 1e-6
NEG = -1e30
QK_SCALE = HEAD_DIM ** -0.5

ADAM_LR = 0.001
ADAM_B1 = 0.9
ADAM_B2 = 0.999
ADAM_EPS = 1e-08
ADAM_WD = 0.01
ADAM_STEP = 10

N_SHARD = 4
N_DEV = 8
LANES = 128
SUBLANES = 8
TOKEN_TILE = 512
FF_TILE = 256
CONV_CHUNK = 512
NA_GROUP = 8
SW_GROUP_BLOCKS = 8
VMEM_BIG = 56 * 1024 * 1024


def _mm(a, b):
    return jnp.dot(a, b, preferred_element_type=F32)


def _mm_nt(a, b):
    return lax.dot_general(a, b, (((1,), (1,)), ((), ())), preferred_element_type=F32)


def _mm_tn(a, b):
    return lax.dot_general(a, b, (((0,), (0,)), ((), ())), preferred_element_type=F32)


def _cparams(sem=None, vmem=None):
    kw = {}
    if sem is not None:
        kw["dimension_semantics"] = sem
    if vmem is not None:
        kw["vmem_limit_bytes"] = vmem
    return pltpu.CompilerParams(**kw)


def _resident(shape):
    return pl.BlockSpec(shape, lambda i: (0,) * len(shape), pipeline_mode=pl.Buffered(1))


def _sigmoid(x):
    return 1.0 / (1.0 + jnp.exp(-x))


def _rms_stats(x):
    r = lax.rsqrt(jnp.mean(x * x, axis=-1, keepdims=True) + EPS)
    return r, x * r


def _rms_bwd(dxn, xn, r):
    return r * (dxn - xn * jnp.mean(dxn * xn, axis=-1, keepdims=True))


def _my_pos():
    return lax.axis_index("x"), lax.axis_index("y"), lax.axis_index("c")


def _flip(v, bit):
    return 1 - v if bit else v


def _ada_forward(c8, w_ada, b_ada, rider, passenger):
    d = c8.shape[1]
    ncol = w_ada.shape[1]
    extra_body, extra_in, extra_out = passenger
    n_extra = len(extra_in)

    def body(c_ref, w_ref, b_ref, *refs):
        extra_refs = refs[:n_extra]
        mod_ref, sc_ref, extra_ref, m_scr, mod_buf, ssem, rsem, ssem2, rsem2 = refs[n_extra:]
        x, y, c = _my_pos()
        me = 4 * x + 2 * y + c
        shard = 2 * x + y
        cv = c_ref[...]
        my_rows = pl.ds(pl.multiple_of(me * SUBLANES, SUBLANES), SUBLANES)
        sc_ref[my_rows, :] = cv * _sigmoid(cv)

        def copy1(k):
            peer = (_flip(x, (k >> 2) & 1), _flip(y, (k >> 1) & 1), _flip(c, k & 1))
            return pltpu.make_async_remote_copy(
                src_ref=sc_ref.at[my_rows, :], dst_ref=sc_ref.at[my_rows, :],
                send_sem=ssem.at[k - 1], recv_sem=rsem.at[k - 1], device_id=peer, device_id_type=MESH)

        sends = [copy1(k) for k in range(1, N_DEV)]
        for cp in sends:
            cp.start()
        for cp in sends:
            cp.wait_recv()
        m_scr[...] = _mm(sc_ref[...].astype(BF16), w_ref[...].astype(BF16))

        def copy2(k):
            px, py = _flip(x, (k >> 1) & 1), _flip(y, k & 1)
            rows = pl.ds(pl.multiple_of((4 * px + 2 * py + c) * SUBLANES, SUBLANES), SUBLANES)
            return pltpu.make_async_remote_copy(
                src_ref=m_scr.at[rows, :], dst_ref=mod_buf.at[shard],
                send_sem=ssem2.at[k - 1], recv_sem=rsem2.at[k - 1], device_id=(px, py, c), device_id_type=MESH)

        sends2 = [copy2(k) for k in range(1, N_SHARD)]
        for cp in sends2:
            cp.start()
        mod_buf[shard] = m_scr[my_rows, :]
        for cp in sends2:
            cp.wait_recv()
        for s in range(N_SHARD):
            mod_ref[:, s * ncol:(s + 1) * ncol] = mod_buf[s] + b_ref[:, s * ncol:(s + 1) * ncol]
        extra_body(*extra_refs, extra_ref)
        for cp in sends + sends2:
            cp.wait_send()

    vm = pl.BlockSpec(memory_space=pltpu.VMEM)
    return _hosted(
        body, rider, name="ada_forward", grid=(),
        out_shape=(jax.ShapeDtypeStruct((SUBLANES, N_SHARD * ncol), F32),
                   jax.ShapeDtypeStruct((N_DEV * SUBLANES, d), F32), extra_out),
        in_specs=[vm] * (3 + n_extra), out_specs=(vm, vm, vm),
        scratch_shapes=[pltpu.VMEM((N_DEV * SUBLANES, ncol), F32), pltpu.VMEM((N_SHARD, SUBLANES, ncol), F32),
                        pltpu.SemaphoreType.DMA((N_DEV - 1,)), pltpu.SemaphoreType.DMA((N_DEV - 1,)),
                        pltpu.SemaphoreType.DMA((N_SHARD - 1,)), pltpu.SemaphoreType.DMA((N_SHARD - 1,))],
        compiler_params=_cparams(vmem=VMEM_BIG), args=[c8, w_ada, b_ada] + extra_in)


class _Rider:
    def __init__(self, kind, srcs, owns=()):
        self.kind, self.srcs, self.owns = kind, list(srcs), list(owns)
        n = len(self.srcs)
        sds = jax.ShapeDtypeStruct
        dma = pltpu.SemaphoreType.DMA
        if kind == "gather":
            self.out_shapes = [sds((N_SHARD,) + s.shape, s.dtype) for s in self.srcs]
            self.sems = [dma((n, N_SHARD - 1)), dma((n, N_SHARD - 1)), dma((n, N_SHARD - 1)), dma((n, N_SHARD - 1)),
                         dma((n,)), dma((n,))]
        elif kind == "scatter":
            self.out_shapes = ([sds((N_SHARD - 1,) + s.shape[1:], s.dtype) for s in self.srcs]
                               + [sds(o.shape[1:], o.dtype) for o in self.owns])
            m = max(len(self.owns), 1)
            self.sems = [dma((n, N_SHARD - 1)), dma((n, N_SHARD - 1)), dma((m,)), dma((m,))]
        else:
            self.out_shapes = [sds(s.shape, s.dtype) for s in self.srcs]
            self.sems = [dma((n,)), dma((n,))]

    @property
    def inputs(self):
        return self.srcs + self.owns

    def _halved(self, i):
        a = self.srcs[i]
        tile_rows = SUBLANES * (4 // jnp.dtype(a.dtype).itemsize)
        return self.kind == "gather" and a.shape[0] % (2 * tile_rows) == 0

    def copies(self, ins, outs, sems):
        n = len(self.srcs)
        x, y, c = _my_pos()
        shard = 2 * x + y
        remote, relay = [], []
        if self.kind == "swap":
            ssem, rsem = sems
            for i in range(n):
                remote.append(pltpu.make_async_remote_copy(
                    src_ref=ins[i], dst_ref=outs[i], send_sem=ssem.at[i], recv_sem=rsem.at[i],
                    device_id=(x, y, 1 - c), device_id_type=MESH))
            return remote, relay
        if self.kind == "gather":
            ssem, rsem, ssem2, rsem2, sib_s, sib_r = sems
        else:
            ssem, rsem, sib_s, sib_r = sems
        for i in range(n):
            if self.kind == "gather":
                remote.append(pltpu.make_async_remote_copy(
                    src_ref=ins[i], dst_ref=outs[i].at[shard], send_sem=sib_s.at[i], recv_sem=sib_r.at[i],
                    device_id=(x, y, 1 - c), device_id_type=MESH))
                half = ins[i].shape[0] // 2
                mine = pl.ds(pl.multiple_of(c * half, half), half) if self._halved(i) else None
            for k in range(1, N_SHARD):
                px, py = _flip(x, (k >> 1) & 1), _flip(y, k & 1)
                if self.kind == "gather":
                    src, dst = ins[i], outs[i].at[shard]
                    if mine is not None:
                        src, dst = src.at[mine], dst.at[mine]
                        got = outs[i].at[2 * px + py].at[mine]
                        relay.append(pltpu.make_async_remote_copy(
                            src_ref=got, dst_ref=got, send_sem=ssem2.at[i, k - 1], recv_sem=rsem2.at[i, k - 1],
                            device_id=(x, y, 1 - c), device_id_type=MESH))
                else:
                    src, dst = ins[i].at[2 * px + py], outs[i].at[k - 1]
                remote.append(pltpu.make_async_remote_copy(
                    src_ref=src, dst_ref=dst, send_sem=ssem.at[i, k - 1], recv_sem=rsem.at[i, k - 1],
                    device_id=(px, py, c), device_id_type=MESH))
        if self.kind == "scatter":
            for i in range(len(self.owns)):
                remote.append(pltpu.make_async_remote_copy(
                    src_ref=ins[n + i].at[shard], dst_ref=outs[n + i], send_sem=sib_s.at[i], recv_sem=sib_r.at[i],
                    device_id=(x, y, 1 - c), device_id_type=MESH))
        return remote, relay

    def start(self, ins, outs, sems):
        remote, _ = self.copies(ins, outs, sems)
        for cp in remote:
            cp.start()

    def wait(self, ins, outs, sems):
        remote, relay = self.copies(ins, outs, sems)
        for cp in remote:
            cp.wait_recv()
        for cp in relay:
            cp.start()
        for cp in relay:
            cp.wait_recv()
        for cp in remote + relay:
            cp.wait_send()


def _hosted(body, rider, *, name, grid, out_shape, in_specs, out_specs, scratch_shapes, compiler_params, args):
    out_shape, out_specs = list(out_shape), list(out_specs)
    if rider is None:
        outs = pl.pallas_call(body, name=name, grid=grid, out_shape=tuple(out_shape), in_specs=list(in_specs),
                              out_specs=tuple(out_specs), scratch_shapes=list(scratch_shapes),
                              compiler_params=compiler_params)(*args)
        return list(outs), []
    n_in, n_out, n_scr = len(in_specs), len(out_shape), len(scratch_shapes)
    nr_in, nr_out = len(rider.inputs), len(rider.out_shapes)
    n_steps = 1
    for size in grid:
        n_steps *= size

    def full(*refs):
        ins, refs = refs[:n_in], refs[n_in:]
        r_in, refs = refs[:nr_in], refs[nr_in:]
        outs, refs = refs[:n_out], refs[n_out:]
        r_out, refs = refs[:nr_out], refs[nr_out:]
        scr, sems = refs[:n_scr], refs[n_scr:]
        if grid:
            step = 0
            for ax, size in enumerate(grid):
                step = step * size + pl.program_id(ax)
            pl.when(step == 0)(lambda: rider.start(r_in, r_out, sems))
            body(*ins, *outs, *scr)
            pl.when(step == n_steps - 1)(lambda: rider.wait(r_in, r_out, sems))
        else:
            rider.start(r_in, r_out, sems)
            body(*ins, *outs, *scr)
            rider.wait(r_in, r_out, sems)

    hbm = pl.BlockSpec(memory_space=pl.ANY)
    res = pl.pallas_call(
        full, name=name, grid=grid, out_shape=tuple(out_shape + rider.out_shapes),
        in_specs=list(in_specs) + [hbm] * nr_in, out_specs=tuple(out_specs + [hbm] * nr_out),
        scratch_shapes=list(scratch_shapes) + rider.sems, compiler_params=compiler_params,
    )(*args, *rider.inputs)
    return list(res[:n_out]), list(res[n_out:])


def _rope_rot(t):
    w = t.shape[1]
    lane = lax.broadcasted_iota(jnp.int32, t.shape, 1)
    first = (lane % HEAD_DIM) < (HEAD_DIM // 2)
    return jnp.where(first, pltpu.roll(t, w - HEAD_DIM // 2, 1), pltpu.roll(t, HEAD_DIM // 2, 1))


def _in_proj(x, mod3, g_attn, w_in_t, cos_t, sin_t, seq, rider=None):
    t, d = x.shape
    tm = 2 * TOKEN_TILE
    per_seq = seq // tm
    rope_lo, rope_hi = 3 * NA_WIDTH, 3 * NA_WIDTH + SW_WIDTH + SW_KV_WIDTH
    n_rep = (rope_hi - rope_lo) // LANES

    def body(x_ref, mod_ref, g_ref, w_ref, cos_ref, sin_ref, h_ref, p_ref):
        r, xn = _rms_stats(x_ref[...])
        shift, scale = mod_ref[0, :, 0:d], mod_ref[0, :, d:2 * d]
        hb = ((xn * g_ref[...]) * (1.0 + scale) + shift).astype(BF16)
        h_ref[...] = hb
        p_ref[:, :rope_lo] = _mm_nt(hb, w_ref[:rope_lo, :]).astype(BF16)
        pr = _mm_nt(hb, w_ref[rope_lo:rope_hi, :])
        cos = jnp.concatenate([cos_ref[...]] * n_rep, axis=1)
        sin = jnp.concatenate([sin_ref[...]] * n_rep, axis=1)
        p_ref[:, rope_lo:rope_hi] = (pr * cos + _rope_rot(pr) * sin).astype(BF16)
        p_ref[:, rope_hi:] = _mm_nt(hb, w_ref[rope_hi:, :]).astype(BF16)

    return _hosted(
        body, rider, name="in_proj", grid=(t // tm,),
        out_shape=[jax.ShapeDtypeStruct((t, d), BF16), jax.ShapeDtypeStruct((t, IN_WIDTH), BF16)],
        in_specs=[pl.BlockSpec((tm, d), lambda i: (i, 0)),
                  pl.BlockSpec((1, 1, 6 * d), lambda i: (i // per_seq, 0, 0)),
                  pl.BlockSpec((1, d), lambda i: (0, 0)),
                  pl.BlockSpec((IN_WIDTH, d), lambda i: (0, 0)),
                  pl.BlockSpec((tm, LANES), lambda i: (i % per_seq, 0)),
                  pl.BlockSpec((tm, LANES), lambda i: (i % per_seq, 0))],
        out_specs=[pl.BlockSpec((tm, d), lambda i: (i, 0)), pl.BlockSpec((tm, IN_WIDTH), lambda i: (i, 0))],
        scratch_shapes=[], compiler_params=_cparams(("arbitrary",), VMEM_BIG),
        args=[x, mod3, g_attn, w_in_t, cos_t, sin_t])


def _na_bias_pattern():
    n_dc = 2 * NA_COLS - 1
    j = np.arange(GRID_W)[:, None]
    m = np.arange(GRID_W * LANES)[None, :]
    q, lane = m // LANES, m % LANES
    k = lane % GRID_W
    cs = np.clip(q - NA_COLS // 2, 0, GRID_W - NA_COLS)
    ok = (k >= cs) & (k < cs + NA_COLS)
    hit = ok & (j < 2 * n_dc) & (lane // GRID_W == j // n_dc) & (k - q + (NA_COLS - 1) == j % n_dc)
    return jnp.asarray(hit.astype(np.float32)), jnp.asarray(np.where(ok, 0.0, NEG).astype(np.float32))


def _na_bias_tiles(rows2, expand, mask):
    n, width = rows2.shape[0], expand.shape[1]
    q_step = 16
    step = q_step * LANES

    def body(r_ref, e_ref, m_ref, o_ref):
        for i in range(width // step):
            at = slice(i * step, (i + 1) * step)
            flat = jnp.dot(r_ref[...], e_ref[:, at], precision=lax.Precision.HIGHEST,
                           preferred_element_type=F32) + m_ref[:, at]
            for qq in range(q_step):
                o_ref[:, i * q_step + qq, :] = flat[:, qq * LANES:(qq + 1) * LANES]

    return body, [rows2, expand, mask], jax.ShapeDtypeStruct((n, GRID_W, LANES), F32)


def _na_prepare(k_ref, v_ref, km, vm):
    lane = lax.broadcasted_iota(jnp.int32, k_ref.shape, 1)
    low = lane < HEAD_DIM
    kv = k_ref[...]
    vv = v_ref[...]
    zero = jnp.zeros_like(kv)
    km[0] = jnp.where(low, kv, zero)
    km[1] = jnp.where(low, zero, kv)
    vm[0] = jnp.where(low, vv, zero)
    vm[1] = jnp.where(low, zero, vv)


def _na_window(r, n_rows):
    rs = jnp.clip(r - NA_ROWS // 2, 0, n_rows - NA_ROWS)
    return rs, r - rs


def _na_pair_window(ref, wrows):
    return jnp.concatenate([ref[0, wrows, :], ref[1, wrows, :]], axis=0)


def _na_scores(q, k2, tp_ref, off):
    bias = jnp.concatenate([tp_ref[h, 2 * w - off + (NA_ROWS - 1)] for h in range(2) for w in range(NA_ROWS // 2)],
                           axis=1)
    return _mm_nt(q, k2) * QK_SCALE + bias


def _pair_lse_block(lse):
    lane = lax.broadcasted_iota(jnp.int32, (lse[0].shape[0], LANES), 1)
    return jnp.where(lane < HEAD_DIM, lse[0], lse[1])


def _pair_softmax(s):
    win = s.shape[1] // 2
    halves, lse = [], []
    for h in range(2):
        sh = s[:, h * win:(h + 1) * win]
        m = jnp.max(sh, axis=-1, keepdims=True)
        e = jnp.exp(sh - m)
        l = jnp.sum(e, axis=-1, keepdims=True)
        halves.append(e / l)
        lse.append(m + jnp.log(l))
    return jnp.concatenate(halves, axis=1), _pair_lse_block(lse)


def _pair_grad(w2, x, low):
    keys = w2.shape[1] // 2
    zero = jnp.zeros_like(x)
    low_x = low[:x.shape[0]]
    stacked = jnp.concatenate([w2[:, :keys], w2[:, keys:]], axis=0)
    diag = jnp.concatenate([jnp.where(low_x, x, zero), jnp.where(low_x, zero, x)], axis=0)
    return _mm_tn(stacked, diag)


def _pair_probs_from_lse(s, lse_block):
    win = s.shape[1] // 2
    return jnp.concatenate([jnp.exp(s[:, h * win:(h + 1) * win] - lse_block[:, h * HEAD_DIM:h * HEAD_DIM + 1])
                            for h in range(2)], axis=1)


def _na_forward(proj, tiles, batch, seq, rider=None):
    t = proj.shape[0]
    n_rows = seq // GRID_W
    n_pairs = NA_WIDTH // LANES
    win = NA_ROWS * GRID_W

    def body(q_ref, k_ref, v_ref, tp_ref, o_ref, lse_ref, km, vm):
        _na_prepare(k_ref, v_ref, km, vm)

        def scores(r):
            rs, off = _na_window(r, n_rows)
            rows = pl.ds(pl.multiple_of(r * GRID_W, GRID_W), GRID_W)
            wrows = pl.ds(pl.multiple_of(rs * GRID_W, GRID_W), win)
            return rows, wrows, _na_scores(q_ref[rows, :], _na_pair_window(km, wrows), tp_ref, off)

        def finish(rows, wrows, s):
            p, lse = _pair_softmax(s)
            lse_ref[rows, :] = lse
            o_ref[rows, :] = _mm(p.astype(BF16), _na_pair_window(vm, wrows))

        def row_group(i, carry):
            for state in [scores(NA_GROUP * i + j) for j in range(NA_GROUP)]:
                finish(*state)
            return carry

        lax.fori_loop(0, n_rows // NA_GROUP, row_group, 0)

    return _hosted(
        body, rider, name="na_forward", grid=(batch, n_pairs),
        out_shape=[jax.ShapeDtypeStruct((t, NA_WIDTH), F32), jax.ShapeDtypeStruct((t, NA_WIDTH), F32)],
        in_specs=[pl.BlockSpec((seq, LANES), lambda b, p: (b, p)),
                  pl.BlockSpec((seq, LANES), lambda b, p: (b, n_pairs + p)),
                  pl.BlockSpec((seq, LANES), lambda b, p: (b, 2 * n_pairs + p)),
                  pl.BlockSpec((2, 2 * NA_ROWS - 2, GRID_W, LANES), lambda b, p: (p, 0, 0, 0))],
        out_specs=[pl.BlockSpec((seq, LANES), lambda b, p: (b, p)), pl.BlockSpec((seq, LANES), lambda b, p: (b, p))],
        scratch_shapes=[pltpu.VMEM((2, seq, LANES), BF16), pltpu.VMEM((2, seq, LANES), BF16)],
        compiler_params=_cparams(("arbitrary", "arbitrary")), args=[proj, proj, proj, tiles])


def _sw_prepare(kv_ref, g, dst_lo, dst_hi, seq):
    lane = lax.broadcasted_iota(jnp.int32, kv_ref.shape, 1)
    mine = (lane // HEAD_DIM) == g
    kg = jnp.where(mine, kv_ref[...].astype(F32), 0.0)
    kr = pltpu.roll(kg, HEAD_DIM, 1)
    first = g == 0
    zero = jnp.zeros((SW_BLOCK, LANES), BF16)
    for dst, val in ((dst_lo, jnp.where(first, kg, kr)), (dst_hi, jnp.where(first, kr, kg))):
        dst[0:SW_BLOCK, :] = zero
        dst[SW_BLOCK:SW_BLOCK + seq, :] = val.astype(BF16)
        dst[SW_BLOCK + seq:, :] = zero


def _sw_mask(n, seq):
    qi = lax.broadcasted_iota(jnp.int32, (SW_BLOCK, 3 * SW_BLOCK), 0)
    kj = lax.broadcasted_iota(jnp.int32, (SW_BLOCK, 3 * SW_BLOCK), 1)
    kpos = n * SW_BLOCK - SW_BLOCK + kj
    return (jnp.abs(qi + SW_BLOCK - kj) <= SW_BLOCK) & (kpos >= 0) & (kpos < seq)


def _sw_probs(s2, ok, sinks):
    band = s2.shape[1] // 2
    halves, lse = [], []
    for i in range(2):
        s = jnp.where(ok, s2[:, i * band:(i + 1) * band], NEG)
        m = jnp.maximum(jnp.max(s, axis=-1, keepdims=True), sinks[i])
        p = jnp.exp(s - m)
        den = jnp.sum(p, axis=-1, keepdims=True) + jnp.exp(sinks[i] - m)
        halves.append(p / den)
        lse.append(m + jnp.log(den))
    return jnp.concatenate(halves, axis=1), _pair_lse_block(lse)


def _sw_probs_from_lse(s2, ok, sinks, lse_block):
    band = s2.shape[1] // 2
    halves, sink_p = [], []
    for i in range(2):
        lse = lse_block[:, i * HEAD_DIM:i * HEAD_DIM + 1]
        halves.append(jnp.exp(jnp.where(ok, s2[:, i * band:(i + 1) * band], NEG) - lse))
        sink_p.append(jnp.exp(sinks[i] - lse))
    return jnp.concatenate(halves, axis=1), sink_p


def _sw_forward(proj, sink, batch, seq, rider=None):
    t = proj.shape[0]
    n_pairs = SW_WIDTH // LANES
    q_blk = 3 * NA_WIDTH // LANES
    k_blk = q_blk + n_pairs
    n_blocks = seq // SW_BLOCK
    pad = seq + 2 * SW_BLOCK

    def body(sink_ref, q_ref, k_ref, v_ref, o_ref, lse_ref, k_lo, k_hi, v_lo, v_hi):
        hp = pl.program_id(1)
        g = hp // 2

        @pl.when(hp % 2 == 0)
        def _():
            _sw_prepare(k_ref, g, k_lo, k_hi, seq)
            _sw_prepare(v_ref, g, v_lo, v_hi, seq)

        sinks = (sink_ref[2 * hp], sink_ref[2 * hp + 1])

        def scores(n):
            rows = pl.ds(pl.multiple_of(n * SW_BLOCK, SW_BLOCK), SW_BLOCK)
            wrows = pl.ds(pl.multiple_of(n * SW_BLOCK, SW_BLOCK), 3 * SW_BLOCK)
            k2 = jnp.concatenate([k_lo[wrows, :], k_hi[wrows, :]], axis=0)
            return n, rows, wrows, _mm_nt(q_ref[rows, :], k2) * QK_SCALE

        def finish(n, rows, wrows, s2):
            p, lse = _sw_probs(s2, _sw_mask(n, seq), sinks)
            lse_ref[rows, :] = lse
            v2 = jnp.concatenate([v_lo[wrows, :], v_hi[wrows, :]], axis=0)
            o_ref[rows, :] = _mm(p.astype(BF16), v2)

        def block_group(i, carry):
            for state in [scores(SW_GROUP_BLOCKS * i + j) for j in range(SW_GROUP_BLOCKS)]:
                finish(*state)
            return carry

        lax.fori_loop(0, n_blocks // SW_GROUP_BLOCKS, block_group, 0)

    return _hosted(
        body, rider, name="sw_forward", grid=(batch, n_pairs),
        out_shape=[jax.ShapeDtypeStruct((t, SW_WIDTH), F32), jax.ShapeDtypeStruct((t, SW_WIDTH), F32)],
        in_specs=[pl.BlockSpec(memory_space=pltpu.SMEM),
                  pl.BlockSpec((seq, LANES), lambda b, p: (b, q_blk + p)),
                  pl.BlockSpec((seq, LANES), lambda b, p: (b, k_blk)),
                  pl.BlockSpec((seq, LANES), lambda b, p: (b, k_blk + 1))],
        out_specs=[pl.BlockSpec((seq, LANES), lambda b, p: (b, p)), pl.BlockSpec((seq, LANES), lambda b, p: (b, p))],
        scratch_shapes=[pltpu.VMEM((pad, LANES), BF16)] * 4,
        compiler_params=_cparams(("arbitrary", "arbitrary")), args=[sink, proj, proj, proj])


def _out_proj(oa, ob, g_na, g_sw, w_out, x, mod3, g_ffn, seq):
    t, d = x.shape
    tm = TOKEN_TILE
    per_seq = seq // tm

    def body(oa_ref, ob_ref, gna_ref, gsw_ref, w_ref, x_ref, mod_ref, gf_ref, oab_ref, mix_ref, x1_ref, h2_ref):
        _, na = _rms_stats(oa_ref[...])
        _, nb = _rms_stats(ob_ref[...])
        oab = jnp.concatenate([na * gna_ref[...], nb * gsw_ref[...]], axis=1).astype(BF16)
        oab_ref[...] = oab
        mix = _mm(oab, w_ref[...])
        mix_ref[...] = mix
        gate_a = mod_ref[0, :, 2 * d:3 * d]
        shift_f, scale_f = mod_ref[0, :, 3 * d:4 * d], mod_ref[0, :, 4 * d:5 * d]
        x1 = x_ref[...] + gate_a * mix
        x1_ref[...] = x1
        _, xn = _rms_stats(x1)
        h2_ref[...] = ((xn * gf_ref[...]) * (1.0 + scale_f) + shift_f).astype(BF16)

    tile = lambda w: pl.BlockSpec((tm, w), lambda i: (i, 0))
    vec = lambda w: pl.BlockSpec((1, w), lambda i: (0, 0))
    return pl.pallas_call(
        body, name="out_proj", grid=(t // tm,),
        out_shape=(jax.ShapeDtypeStruct((t, d), BF16), jax.ShapeDtypeStruct((t, d), F32),
                   jax.ShapeDtypeStruct((t, d), F32), jax.ShapeDtypeStruct((t, d), BF16)),
        in_specs=[tile(NA_WIDTH), tile(SW_WIDTH), vec(NA_WIDTH), vec(SW_WIDTH),
                  pl.BlockSpec((d, d), lambda i: (0, 0)), tile(d),
                  pl.BlockSpec((1, 1, 6 * d), lambda i: (i // per_seq, 0, 0)), vec(d)],
        out_specs=(tile(d), tile(d), tile(d), tile(d)),
        compiler_params=_cparams(("arbitrary",), VMEM_BIG),
    )(oa, ob, g_na, g_sw, w_out, x, mod3, g_ffn)


def _up_proj(h2, w_up_halves, rider=None):
    t, d = h2.shape
    tm = 2 * TOKEN_TILE
    w_a, w_b = w_up_halves
    half, wcol = w_a.shape[1], w_a.shape[2]

    def body(h_ref, wa_ref, wb_ref, u_ref):
        u_ref[0] = (_mm(h_ref[:, :half], wa_ref[0]) + _mm(h_ref[:, half:], wb_ref[0])).astype(BF16)

    w_spec = pl.BlockSpec((1, half, wcol), lambda j, i: (j, 0, 0))
    return _hosted(
        body, rider, name="up_proj", grid=(N_SHARD, t // tm),
        out_shape=[jax.ShapeDtypeStruct((2, t, D_FF), BF16)],
        in_specs=[pl.BlockSpec((tm, d), lambda j, i: (i, 0)), w_spec, w_spec],
        out_specs=[pl.BlockSpec((1, tm, wcol), lambda j, i: (j // 2, i, j % 2))],
        scratch_shapes=[], compiler_params=_cparams(("arbitrary", "arbitrary"), VMEM_BIG), args=[h2, w_a, w_b])


def _taps_chunk(load, s, rows, seq):
    halo = 2 * SUBLANES
    cur = load(s, rows)
    above = load(pl.multiple_of(jnp.maximum(s - halo, 0), halo), halo)
    below = load(pl.multiple_of(jnp.minimum(s + rows, seq - halo), halo), halo)
    up = jnp.where(s > 0, above[halo - 1:halo, :], 0.0)
    dn = jnp.where(s + rows < seq, below[0:1, :], 0.0)
    row = lax.broadcasted_iota(jnp.int32, cur.shape, 0)
    prev = jnp.where(row == 0, up, pltpu.roll(cur, 1, 0))
    nxt = jnp.where(row == rows - 1, dn, pltpu.roll(cur, rows - 1, 0))
    return cur, prev, nxt


def _conv_gate(u, conv_w, conv_b, batch, seq, rider=None):
    t = u.shape[1]
    cw = FF_TILE
    rows = CONV_CHUNK

    def body(u_ref, w_ref, b_ref, a_ref):
        def chunk(i, carry):
            s = pl.multiple_of(i * rows, rows)
            gt, prev, nxt = _taps_chunk(lambda at, n: u_ref[1, pl.ds(at, n), :].astype(F32), s, rows, seq)
            gc = prev * w_ref[0:1, :] + gt * w_ref[1:2, :] + nxt * w_ref[2:3, :] + b_ref[...]
            a_ref[pl.ds(s, rows), :] = ((gc * _sigmoid(gc)) * u_ref[0, pl.ds(s, rows), :].astype(F32)).astype(BF16)
            return carry

        lax.fori_loop(0, seq // rows, chunk, 0)

    return _hosted(
        body, rider, name="conv_gate", grid=(batch, D_FF // cw),
        out_shape=[jax.ShapeDtypeStruct((t, D_FF), BF16)],
        in_specs=[pl.BlockSpec((2, seq, cw), lambda b, j: (0, b, j)),
                  pl.BlockSpec((3, cw), lambda b, j: (0, j)), pl.BlockSpec((1, cw), lambda b, j: (0, j))],
        out_specs=[pl.BlockSpec((seq, cw), lambda b, j: (b, j))], scratch_shapes=[],
        compiler_params=_cparams(("arbitrary", "arbitrary"), VMEM_BIG), args=[u, conv_w, conv_b])


def _down_and_loss(a, w_down, x1, mod3, g_final, target, seq):
    t, d = x1.shape
    tm = TOKEN_TILE
    per_seq = seq // tm
    batch = t // seq

    def body(a_ref, w_ref, x1_ref, mod_ref, g_ref, tgt_ref, dx2_ref, dffn_ref, loss_ref, dgate_ref, dg_ref):
        i = pl.program_id(0)
        f = _mm(a_ref[...], w_ref[...])
        gate_f = mod_ref[0, :, 5 * d:6 * d]
        x2 = x1_ref[...] + gate_f * f
        r, xn = _rms_stats(x2)
        err = xn * g_ref[...] - tgt_ref[...]
        part = 0.5 * jnp.sum(jnp.mean(err * err, axis=-1, keepdims=True))
        dy = err / d
        dx2 = _rms_bwd(dy * g_ref[...], xn, r)
        dx2_ref[...] = dx2
        dffn_ref[...] = (dx2 * gate_f).astype(BF16)

        @pl.when(i == 0)
        def _():
            loss_ref[...] = jnp.zeros_like(loss_ref)
            dg_ref[...] = jnp.zeros_like(dg_ref)

        @pl.when(i % per_seq == 0)
        def _():
            dgate_ref[...] = jnp.zeros_like(dgate_ref)

        loss_ref[...] += part
        dg_ref[...] += jnp.sum(dy * xn, axis=0, keepdims=True)
        dgate_ref[0] += jnp.sum(dx2 * f, axis=0, keepdims=True)

    tile = lambda w: pl.BlockSpec((tm, w), lambda i: (i, 0))
    return pl.pallas_call(
        body, name="down_loss", grid=(t // tm,),
        out_shape=(jax.ShapeDtypeStruct((t, d), F32), jax.ShapeDtypeStruct((t, d), BF16),
                   jax.ShapeDtypeStruct((SUBLANES, LANES), F32), jax.ShapeDtypeStruct((batch, 1, d), F32),
                   jax.ShapeDtypeStruct((1, d), F32)),
        in_specs=[tile(D_FF), _resident((D_FF, d)), tile(d),
                  pl.BlockSpec((1, 1, 6 * d), lambda i: (i // per_seq, 0, 0)),
                  pl.BlockSpec((1, d), lambda i: (0, 0)), tile(d)],
        out_specs=(tile(d), tile(d), pl.BlockSpec((SUBLANES, LANES), lambda i: (0, 0)),
                   pl.BlockSpec((1, 1, d), lambda i: (i // per_seq, 0, 0)), pl.BlockSpec((1, d), lambda i: (0, 0))),
        compiler_params=_cparams(("arbitrary",), VMEM_BIG),
    )(a, w_down, x1, mod3, g_final, target)


def _down_weight_grad(a, dffn):
    t, dff = a.shape
    d = dffn.shape[1]
    tk = 2 * TOKEN_TILE
    n_k = t // tk

    def body(a_ref, df_ref, g_ref, gb_ref):
        k = pl.program_id(0)

        @pl.when(k == 0)
        def _():
            g_ref[...] = jnp.zeros_like(g_ref)

        g_ref[...] += _mm_tn(a_ref[...], df_ref[...])

        @pl.when(k == n_k - 1)
        def _():
            gb_ref[...] = g_ref[...].astype(BF16)

    whole = _resident((dff, d))
    return pl.pallas_call(
        body, name="down_weight_grad", grid=(n_k,),
        out_shape=(jax.ShapeDtypeStruct((dff, d), F32), jax.ShapeDtypeStruct((dff, d), BF16)),
        in_specs=[pl.BlockSpec((tk, dff), lambda k: (k, 0)), pl.BlockSpec((tk, d), lambda k: (k, 0))],
        out_specs=(whole, whole),
        compiler_params=_cparams(("arbitrary",), VMEM_BIG),
    )(a, dffn)


def _ffn_backward(dffn, w_down, u, conv_w, conv_b, batch, seq, rider=None):
    t, d = dffn.shape
    cw = FF_TILE
    rows = CONV_CHUNK

    def body(df_ref, wd_ref, u_ref, w_ref, b_ref, du_ref, gcw_ref, gcb_ref, da_scr, dgc_scr):
        b = pl.program_id(1)
        da_scr[...] = _mm_nt(df_ref[...], wd_ref[...])

        @pl.when(b == 0)
        def _():
            gcw_ref[...] = jnp.zeros_like(gcw_ref)
            gcb_ref[...] = jnp.zeros_like(gcb_ref)

        def fold(v):
            return jnp.sum(v.reshape(rows // SUBLANES, SUBLANES, cw), axis=0)

        def chunk(i, carry):
            s = pl.multiple_of(i * rows, rows)
            here = pl.ds(s, rows)
            gt, prev, nxt = _taps_chunk(lambda at, n: u_ref[1, pl.ds(at, n), :].astype(F32), s, rows, seq)
            val, da = u_ref[0, here, :].astype(F32), da_scr[here, :]
            gc = prev * w_ref[0:1, :] + gt * w_ref[1:2, :] + nxt * w_ref[2:3, :] + b_ref[...]
            sg = _sigmoid(gc)
            sl = gc * sg
            du_ref[0, here, :] = (da * sl).astype(BF16)
            dgc = (da * val) * (sg * (1.0 + gc * (1.0 - sg)))
            dgc_scr[here, :] = dgc
            cb, c0, c1, c2 = carry
            return cb + fold(dgc), c0 + fold(dgc * prev), c1 + fold(dgc * gt), c2 + fold(dgc * nxt)

        zero = jnp.zeros((SUBLANES, cw), F32)
        cb, c0, c1, c2 = lax.fori_loop(0, seq // rows, chunk, (zero, zero, zero, zero))
        gcb_ref[...] += jnp.sum(cb, axis=0, keepdims=True)
        gcw_ref[0:1, :] += jnp.sum(c0, axis=0, keepdims=True)
        gcw_ref[1:2, :] += jnp.sum(c1, axis=0, keepdims=True)
        gcw_ref[2:3, :] += jnp.sum(c2, axis=0, keepdims=True)

        def chunk2(i, carry):
            s = pl.multiple_of(i * rows, rows)
            dgc, dprev, dnxt = _taps_chunk(lambda at, n: dgc_scr[pl.ds(at, n), :], s, rows, seq)
            du_ref[1, pl.ds(s, rows), :] = (dnxt * w_ref[0:1, :] + dgc * w_ref[1:2, :]
                                            + dprev * w_ref[2:3, :]).astype(BF16)
            return carry

        lax.fori_loop(0, seq // rows, chunk2, 0)

    return _hosted(
        body, rider, name="ffn_backward", grid=(D_FF // cw, batch),
        out_shape=[jax.ShapeDtypeStruct((2, t, D_FF), BF16),
                   jax.ShapeDtypeStruct((3, D_FF), F32), jax.ShapeDtypeStruct((1, D_FF), F32)],
        in_specs=[pl.BlockSpec((seq, d), lambda j, b: (b, 0)), pl.BlockSpec((cw, d), lambda j, b: (j, 0)),
                  pl.BlockSpec((2, seq, cw), lambda j, b: (0, b, j)),
                  pl.BlockSpec((3, cw), lambda j, b: (0, j)), pl.BlockSpec((1, cw), lambda j, b: (0, j))],
        out_specs=[pl.BlockSpec((2, seq, cw), lambda j, b: (0, b, j)),
                   pl.BlockSpec((3, cw), lambda j, b: (0, j)), pl.BlockSpec((1, cw), lambda j, b: (0, j))],
        scratch_shapes=[pltpu.VMEM((seq, cw), F32), pltpu.VMEM((seq, cw), F32)],
        compiler_params=_cparams(("arbitrary", "arbitrary"), VMEM_BIG), args=[dffn, w_down, u, conv_w, conv_b])


def _up_backward(du, w_up, x1, mod3, g_ffn, dx2, mix, seq, rider=None):
    _, t, _ = du.shape
    d = x1.shape[1]
    tm = TOKEN_TILE
    per_seq = seq // tm
    batch = t // seq
    w_a, w_b = w_up
    half, wcol = w_a.shape[1], w_a.shape[2]

    def body(du_ref, wa_ref, wb_ref, x1_ref, mod_ref, g_ref, dx2_ref, mix_ref,
             dx1_ref, dmix_ref, dsh_ref, dsc_ref, dga_ref, dg_ref):
        i = pl.program_id(0)
        parts = []
        for w_ref in (wa_ref, wb_ref):
            acc = jnp.zeros((tm, half), F32)
            for j in range(N_SHARD):
                acc = acc + _mm_nt(du_ref[j // 2, :, (j % 2) * wcol:(j % 2 + 1) * wcol], w_ref[j])
            parts.append(acc)
        dh = jnp.concatenate(parts, axis=1)
        gate_a = mod_ref[0, :, 2 * d:3 * d]
        scale_f = mod_ref[0, :, 4 * d:5 * d]
        r, xn = _rms_stats(x1_ref[...])
        xg = xn * g_ref[...]
        dxg = dh * (1.0 + scale_f)
        dx1 = dx2_ref[...] + _rms_bwd(dxg * g_ref[...], xn, r)
        dx1_ref[...] = dx1
        dmix_ref[...] = (dx1 * gate_a).astype(BF16)

        @pl.when(i == 0)
        def _():
            dg_ref[...] = jnp.zeros_like(dg_ref)

        @pl.when(i % per_seq == 0)
        def _():
            dsh_ref[...] = jnp.zeros_like(dsh_ref)
            dsc_ref[...] = jnp.zeros_like(dsc_ref)
            dga_ref[...] = jnp.zeros_like(dga_ref)

        dg_ref[...] += jnp.sum(dxg * xn, axis=0, keepdims=True)
        dsh_ref[0] += jnp.sum(dh, axis=0, keepdims=True)
        dsc_ref[0] += jnp.sum(dh * xg, axis=0, keepdims=True)
        dga_ref[0] += jnp.sum(dx1 * mix_ref[...], axis=0, keepdims=True)

    tile = lambda w: pl.BlockSpec((tm, w), lambda i: (i, 0))
    per_b = pl.BlockSpec((1, 1, d), lambda i: (i // per_seq, 0, 0))
    small = jax.ShapeDtypeStruct((batch, 1, d), F32)
    return _hosted(
        body, rider, name="up_backward", grid=(t // tm,),
        out_shape=[jax.ShapeDtypeStruct((t, d), F32), jax.ShapeDtypeStruct((t, d), BF16), small, small, small,
                   jax.ShapeDtypeStruct((1, d), F32)],
        in_specs=[pl.BlockSpec((2, tm, D_FF), lambda i: (0, i, 0)),
                  _resident((N_SHARD, half, wcol)), _resident((N_SHARD, half, wcol)), tile(d),
                  pl.BlockSpec((1, 1, 6 * d), lambda i: (i // per_seq, 0, 0)),
                  pl.BlockSpec((1, d), lambda i: (0, 0)), tile(d), tile(d)],
        out_specs=[tile(d), tile(d), per_b, per_b, per_b, pl.BlockSpec((1, d), lambda i: (0, 0))],
        scratch_shapes=[], compiler_params=_cparams(("arbitrary",), VMEM_BIG),
        args=[du, w_a, w_b, x1, mod3, g_ffn, dx2, mix])


def _up_weight_grad(h2, du, rider=None):
    t, d = h2.shape
    tk = 2 * TOKEN_TILE
    wcol = D_FF // 2
    half = d // 2
    n_k = t // tk

    def body(h_ref, du_ref, ga_ref, gb_ref, ga16_ref, gb16_ref):
        k = pl.program_id(1)

        @pl.when(k == 0)
        def _():
            ga_ref[...] = jnp.zeros_like(ga_ref)
            gb_ref[...] = jnp.zeros_like(gb_ref)

        du = du_ref[0]
        ga_ref[0] += _mm_tn(h_ref[:, :half], du)
        gb_ref[0] += _mm_tn(h_ref[:, half:], du)

        @pl.when(k == n_k - 1)
        def _():
            ga16_ref[...] = ga_ref[...].astype(BF16)
            gb16_ref[...] = gb_ref[...].astype(BF16)

    g_spec = pl.BlockSpec((1, half, wcol), lambda j, k: (j, 0, 0))
    f32_out = jax.ShapeDtypeStruct((N_SHARD, half, wcol), F32)
    b16_out = jax.ShapeDtypeStruct((N_SHARD, half, wcol), BF16)
    return _hosted(
        body, rider, name="up_weight_grad", grid=(N_SHARD, n_k),
        out_shape=[f32_out, f32_out, b16_out, b16_out],
        in_specs=[pl.BlockSpec((tk, d), lambda j, k: (k, 0)),
                  pl.BlockSpec((1, tk, wcol), lambda j, k: (j // 2, k, j % 2))],
        out_specs=[g_spec, g_spec, g_spec, g_spec], scratch_shapes=[],
        compiler_params=_cparams(("arbitrary", "arbitrary"), VMEM_BIG), args=[h2, du])


def _out_backward(dmix, w_out, oab, oa, ob, g_na, g_sw):
    t, d = dmix.shape
    tm = 2 * TOKEN_TILE
    hw = NA_WIDTH

    def body(dm_ref, w_ref, oab_ref, oa_ref, ob_ref, gna_ref, gsw_ref,
             doa_ref, dob_ref, gw_ref, gwb_ref, dgna_ref, dgsw_ref):
        @pl.when(pl.program_id(0) == 0)
        def _():
            gw_ref[...] = jnp.zeros_like(gw_ref)
            dgna_ref[...] = jnp.zeros_like(dgna_ref)
            dgsw_ref[...] = jnp.zeros_like(dgsw_ref)

        dm = dm_ref[...]
        gw_ref[...] += _mm_tn(oab_ref[...], dm)

        @pl.when(pl.program_id(0) == t // tm - 1)
        def _():
            gwb_ref[...] = gw_ref[...].astype(BF16)

        do = _mm_nt(dm, w_ref[...])
        for raw_ref, g_ref, dst_ref, dg_ref, lo in ((oa_ref, gna_ref, doa_ref, dgna_ref, 0),
                                                     (ob_ref, gsw_ref, dob_ref, dgsw_ref, hw)):
            r, xn = _rms_stats(raw_ref[...])
            dpart = do[:, lo:lo + hw]
            dg_ref[...] += jnp.sum(dpart * xn, axis=0, keepdims=True)
            dst_ref[...] = _rms_bwd(dpart * g_ref[...], xn, r).astype(BF16)

    tile = lambda w: pl.BlockSpec((tm, w), lambda i: (i, 0))
    vec = lambda w: pl.BlockSpec((1, w), lambda i: (0, 0))
    return pl.pallas_call(
        body, name="out_backward", grid=(t // tm,),
        out_shape=(jax.ShapeDtypeStruct((t, hw), BF16), jax.ShapeDtypeStruct((t, hw), BF16),
                   jax.ShapeDtypeStruct((d, d), F32), jax.ShapeDtypeStruct((d, d), BF16),
                   jax.ShapeDtypeStruct((1, hw), F32), jax.ShapeDtypeStruct((1, hw), F32)),
        in_specs=[tile(d), pl.BlockSpec((d, d), lambda i: (0, 0)), tile(d), tile(hw), tile(hw), vec(hw), vec(hw)],
        out_specs=(tile(hw), tile(hw), pl.BlockSpec((d, d), lambda i: (0, 0)), pl.BlockSpec((d, d), lambda i: (0, 0)),
                   vec(hw), vec(hw)),
        compiler_params=_cparams(("arbitrary",), VMEM_BIG),
    )(dmix, w_out, oab, oa, ob, g_na, g_sw)


def _na_backward(proj, d_o, lse, tiles, batch, seq, rider=None):
    t = proj.shape[0]
    n_rows = seq // GRID_W
    n_pairs = NA_WIDTH // LANES
    win = NA_ROWS * GRID_W
    n_tiles = 2 * NA_ROWS - 2

    def body(q_ref, k_ref, v_ref, do_ref, lse_ref, tp_ref, dq_ref, dk_ref, dv_ref, dtp_ref, km, vm, dk_acc, dv_acc):
        @pl.when(pl.program_id(1) == 0)
        def _():
            dtp_ref[...] = jnp.zeros_like(dtp_ref)

        _na_prepare(k_ref, v_ref, km, vm)
        dk_acc[...] = jnp.zeros_like(dk_acc)
        dv_acc[...] = jnp.zeros_like(dv_acc)
        low = lax.broadcasted_iota(jnp.int32, (win, LANES), 1) < HEAD_DIM

        def scores(r):
            rs, off = _na_window(r, n_rows)
            rows = pl.ds(pl.multiple_of(r * GRID_W, GRID_W), GRID_W)
            wrows = pl.ds(pl.multiple_of(rs * GRID_W, GRID_W), win)
            q, do = q_ref[rows, :], do_ref[rows, :]
            k2 = _na_pair_window(km, wrows)
            s = _na_scores(q, k2, tp_ref, off)
            dp = _mm_nt(do, _na_pair_window(vm, wrows))
            return rows, wrows, off, q, do, k2, s, dp

        def finish(rows, wrows, off, q, do, k2, s, dp):
            p = _pair_probs_from_lse(s, lse_ref[rows, :])
            parts = []
            for h in range(2):
                ph, dph = p[:, h * win:(h + 1) * win], dp[:, h * win:(h + 1) * win]
                dsh = ph * (dph - jnp.sum(ph * dph, axis=-1, keepdims=True))
                for w in range(NA_ROWS // 2):
                    dtp_ref[h, 2 * w - off + (NA_ROWS - 1)] += dsh[:, w * LANES:(w + 1) * LANES]
                parts.append(dsh)
            dsb = (jnp.concatenate(parts, axis=1) * QK_SCALE).astype(BF16)
            dq_ref[rows, :] = _mm(dsb, k2).astype(BF16)
            dk_acc[wrows, :] += _pair_grad(dsb, q, low)
            dv_acc[wrows, :] += _pair_grad(p.astype(BF16), do, low)

        def row_group(i, carry):
            for state in [scores(NA_GROUP * i + j) for j in range(NA_GROUP)]:
                finish(*state)
            return carry

        lax.fori_loop(0, n_rows // NA_GROUP, row_group, 0)
        dk_ref[...] = dk_acc[...].astype(BF16)
        dv_ref[...] = dv_acc[...].astype(BF16)

    blk = lambda off: pl.BlockSpec((seq, LANES), lambda p, b: (b, off + p))
    out = jax.ShapeDtypeStruct((t, NA_WIDTH), BF16)
    return _hosted(
        body, rider, name="na_backward", grid=(n_pairs, batch),
        out_shape=[out, out, out, jax.ShapeDtypeStruct(tiles.shape, F32)],
        in_specs=[blk(0), blk(n_pairs), blk(2 * n_pairs), blk(0), blk(0),
                  pl.BlockSpec((2, n_tiles, GRID_W, LANES), lambda p, b: (p, 0, 0, 0))],
        out_specs=[blk(0), blk(0), blk(0), pl.BlockSpec((2, n_tiles, GRID_W, LANES), lambda p, b: (p, 0, 0, 0))],
        scratch_shapes=[pltpu.VMEM((2, seq, LANES), BF16), pltpu.VMEM((2, seq, LANES), BF16),
                        pltpu.VMEM((seq, LANES), F32), pltpu.VMEM((seq, LANES), F32)],
        compiler_params=_cparams(("arbitrary", "arbitrary")), args=[proj, proj, proj, d_o, lse, tiles])


def _na_bias_grad(dtiles, expand):
    n = dtiles.shape[0]

    def body(t_ref, e_ref, o_ref):
        flat = jnp.concatenate([t_ref[:, qq, :] for qq in range(GRID_W)], axis=1)
        o_ref[...] = lax.dot_general(flat, e_ref[...], (((1,), (1,)), ((), ())),
                                     precision=lax.Precision.HIGHEST, preferred_element_type=F32)

    return body, [dtiles, expand], jax.ShapeDtypeStruct((n, expand.shape[0]), F32)


def _sw_backward(proj, d_o, lse, sink, batch, seq, rider=None):
    t = proj.shape[0]
    n_pairs = SW_WIDTH // LANES
    q_blk = 3 * NA_WIDTH // LANES
    k_blk = q_blk + n_pairs
    n_blocks = seq // SW_BLOCK
    pad = seq + 2 * SW_BLOCK

    def body(sink_ref, q_ref, k_ref, v_ref, do_ref, lse_ref, dq_ref, dk_ref, dv_ref, dsk_ref,
             k_lo, k_hi, v_lo, v_hi, dk_loc, dv_loc, dk_tot, dv_tot):
        hp = pl.program_id(1)
        g = hp // 2

        @pl.when(hp % 2 == 0)
        def _():
            _sw_prepare(k_ref, g, k_lo, k_hi, seq)
            _sw_prepare(v_ref, g, v_lo, v_hi, seq)
            dk_loc[...] = jnp.zeros_like(dk_loc)
            dv_loc[...] = jnp.zeros_like(dv_loc)

        @pl.when(hp == 0)
        def _():
            dk_tot[...] = jnp.zeros_like(dk_tot)
            dv_tot[...] = jnp.zeros_like(dv_tot)

        band = 3 * SW_BLOCK
        low = lax.broadcasted_iota(jnp.int32, (band, LANES), 1) < HEAD_DIM

        sinks = (sink_ref[2 * hp], sink_ref[2 * hp + 1])

        def scores(n):
            rows = pl.ds(pl.multiple_of(n * SW_BLOCK, SW_BLOCK), SW_BLOCK)
            wrows = pl.ds(pl.multiple_of(n * SW_BLOCK, SW_BLOCK), band)
            qb, do = q_ref[rows, :], do_ref[rows, :]
            k2 = jnp.concatenate([k_lo[wrows, :], k_hi[wrows, :]], axis=0)
            v2 = jnp.concatenate([v_lo[wrows, :], v_hi[wrows, :]], axis=0)
            return n, rows, wrows, qb, do, k2, _mm_nt(qb, k2) * QK_SCALE, _mm_nt(do, v2)

        def finish(sink_acc, n, rows, wrows, qb, do, k2, s2, dp):
            p, ps = _sw_probs_from_lse(s2, _sw_mask(n, seq), sinks, lse_ref[rows, :])
            parts, new = [], []
            for i in range(2):
                ph, dph = p[:, i * band:(i + 1) * band], dp[:, i * band:(i + 1) * band]
                delta = jnp.sum(ph * dph, axis=-1, keepdims=True)
                parts.append(ph * (dph - delta))
                new.append(sink_acc[i] - ps[i] * delta)
            dsb = (jnp.concatenate(parts, axis=1) * QK_SCALE).astype(BF16)
            dq_ref[rows, :] = _mm(dsb, k2)
            dk_loc[wrows, :] += _pair_grad(dsb, qb, low)
            dv_loc[wrows, :] += _pair_grad(p.astype(BF16), do, low)
            return tuple(new)

        def block_group(i, carry):
            for state in [scores(SW_GROUP_BLOCKS * i + j) for j in range(SW_GROUP_BLOCKS)]:
                carry = finish(carry, *state)
            return carry

        zero = jnp.zeros((SW_BLOCK, 1), F32)
        s0, s1 = lax.fori_loop(0, n_blocks // SW_GROUP_BLOCKS, block_group, (zero, zero))
        row = lax.broadcasted_iota(jnp.int32, (SUBLANES, LANES), 0)
        dsk_ref[0, 0] = jnp.where(row == 0, jnp.sum(s0), jnp.where(row == 1, jnp.sum(s1), 0.0))

        @pl.when(hp % 2 == 1)
        def _():
            lane_s = lax.broadcasted_iota(jnp.int32, (seq, LANES), 1)
            mine_g = (lane_s // HEAD_DIM) == g
            for loc, tot in ((dk_loc, dk_tot), (dv_loc, dv_tot)):
                part = loc[SW_BLOCK:SW_BLOCK + seq, :]
                tot[...] += jnp.where(mine_g, part + pltpu.roll(part, HEAD_DIM, 1), 0.0)

        @pl.when(hp == n_pairs - 1)
        def _():
            dk_ref[...] = dk_tot[...]
            dv_ref[...] = dv_tot[...].astype(BF16)

    return _hosted(
        body, rider, name="sw_backward", grid=(batch, n_pairs),
        out_shape=[jax.ShapeDtypeStruct((t, SW_WIDTH), F32), jax.ShapeDtypeStruct((t, LANES), F32),
                   jax.ShapeDtypeStruct((t, LANES), BF16), jax.ShapeDtypeStruct((batch, n_pairs, SUBLANES, LANES), F32)],
        in_specs=[pl.BlockSpec(memory_space=pltpu.SMEM),
                  pl.BlockSpec((seq, LANES), lambda b, p: (b, q_blk + p)),
                  pl.BlockSpec((seq, LANES), lambda b, p: (b, k_blk)),
                  pl.BlockSpec((seq, LANES), lambda b, p: (b, k_blk + 1)),
                  pl.BlockSpec((seq, LANES), lambda b, p: (b, p)), pl.BlockSpec((seq, LANES), lambda b, p: (b, p))],
        out_specs=[pl.BlockSpec((seq, LANES), lambda b, p: (b, p)), pl.BlockSpec((seq, LANES), lambda b, p: (b, 0)),
                   pl.BlockSpec((seq, LANES), lambda b, p: (b, 0)),
                   pl.BlockSpec((1, 1, SUBLANES, LANES), lambda b, p: (b, p, 0, 0))],
        scratch_shapes=[pltpu.VMEM((pad, LANES), BF16)] * 4 + [pltpu.VMEM((pad, LANES), F32)] * 2
        + [pltpu.VMEM((seq, LANES), F32)] * 2,
        compiler_params=_cparams(("arbitrary", "arbitrary")), args=[sink, proj, proj, proj, d_o, lse])


def _in_backward(dqkv_a, dq_b, dk_b, dv_b, w_in_t, h1, x, mod3, g_attn, dx1, cos_t, sin_t, seq):
    t, d = x.shape
    tm = TOKEN_TILE
    per_seq = seq // tm
    batch = t // seq
    dqa, dka, dva = dqkv_a
    n_q = SW_WIDTH // LANES

    def body(dqa_ref, dka_ref, dva_ref, dqb_ref, dkb_ref, dvb_ref, w_ref, h_ref, x_ref, mod_ref, g_ref, dx1_ref,
             cos_ref, sin_ref, dx_ref, gw_ref, gwb_ref, dsh_ref, dsc_ref, dg_ref):
        i = pl.program_id(0)

        @pl.when(i == 0)
        def _():
            gw_ref[...] = jnp.zeros_like(gw_ref)
            dg_ref[...] = jnp.zeros_like(dg_ref)

        @pl.when(i % per_seq == 0)
        def _():
            dsh_ref[...] = jnp.zeros_like(dsh_ref)
            dsc_ref[...] = jnp.zeros_like(dsc_ref)

        dr = jnp.concatenate([dqb_ref[...], dkb_ref[...]], axis=1)
        cos = jnp.concatenate([cos_ref[...]] * (n_q + 1), axis=1)
        sin = jnp.concatenate([sin_ref[...]] * (n_q + 1), axis=1)
        dr = dr * cos + _rope_rot(dr * sin)
        dproj = jnp.concatenate([dqa_ref[...], dka_ref[...], dva_ref[...], dr.astype(BF16), dvb_ref[...]], axis=1)
        gw_ref[...] += _mm_tn(dproj, h_ref[...])

        @pl.when(i == t // tm - 1)
        def _():
            gwb_ref[...] = gw_ref[...].astype(BF16)

        dh = _mm(dproj, w_ref[...])
        scale = mod_ref[0, :, d:2 * d]
        r, xn = _rms_stats(x_ref[...])
        xg = xn * g_ref[...]
        dxg = dh * (1.0 + scale)
        dx_ref[...] = dx1_ref[...] + _rms_bwd(dxg * g_ref[...], xn, r)
        dg_ref[...] += jnp.sum(dxg * xn, axis=0, keepdims=True)
        dsh_ref[0] += jnp.sum(dh, axis=0, keepdims=True)
        dsc_ref[0] += jnp.sum(dh * xg, axis=0, keepdims=True)

    tile = lambda w: pl.BlockSpec((tm, w), lambda i: (i, 0))
    per_b = pl.BlockSpec((1, 1, d), lambda i: (i // per_seq, 0, 0))
    small = jax.ShapeDtypeStruct((batch, 1, d), F32)
    rope = pl.BlockSpec((tm, LANES), lambda i: (i % per_seq, 0))
    return pl.pallas_call(
        body, name="in_backward", grid=(t // tm,),
        out_shape=(jax.ShapeDtypeStruct((t, d), F32), jax.ShapeDtypeStruct((IN_WIDTH, d), F32),
                   jax.ShapeDtypeStruct((IN_WIDTH, d), BF16), small, small, jax.ShapeDtypeStruct((1, d), F32)),
        in_specs=[tile(NA_WIDTH), tile(NA_WIDTH), tile(NA_WIDTH), tile(SW_WIDTH), tile(LANES), tile(LANES),
                  _resident((IN_WIDTH, d)), tile(d), tile(d),
                  pl.BlockSpec((1, 1, 6 * d), lambda i: (i // per_seq, 0, 0)),
                  pl.BlockSpec((1, d), lambda i: (0, 0)), tile(d), rope, rope],
        out_specs=(tile(d), _resident((IN_WIDTH, d)), _resident((IN_WIDTH, d)),
                   per_b, per_b, pl.BlockSpec((1, d), lambda i: (0, 0))),
        compiler_params=_cparams(("arbitrary",), VMEM_BIG),
    )(dqa, dka, dva, dq_b, dk_b, dv_b, w_in_t, h1, x, mod3, g_attn, dx1, cos_t, sin_t)


def _ada_weight_grad(sc_all, dmod_cols):
    d = sc_all.shape[1]
    ncol = dmod_cols.shape[1]

    def body(s_ref, m_ref, o_ref):
        o_ref[...] = _mm_tn(s_ref[...].astype(BF16), m_ref[...].astype(BF16))

    return pl.pallas_call(
        body, name="ada_weight_grad",
        out_shape=jax.ShapeDtypeStruct((d, ncol), F32),
        compiler_params=_cparams(vmem=VMEM_BIG),
    )(sc_all, dmod_cols)


def _row_tile(rows, cols):
    target = max(SUBLANES, (1 << 20) // (4 * cols))
    best = rows
    for cand in range(SUBLANES, rows + 1, SUBLANES):
        if rows % cand == 0 and cand <= target:
            best = cand
    return best if rows % SUBLANES == 0 else rows


def _sum_slots(results, name, rider=None, passenger=None):
    extra_body, extra_in, extra_out = passenger if passenger is not None else (None, [], None)
    n_extra = len(extra_in)
    parts = [group for groups in results for group in groups]
    result_of = [k for k, groups in enumerate(results) for _ in groups]
    n_parts = len(parts)
    cols = [results[k][0][1].shape[1] for k in result_of]
    tr = [_row_tile(min(own.shape[0] for _, own in results[k]), c) for k, c in zip(result_of, cols)]
    assert all(own.shape[0] % r == 0 and own.shape[1] == c for (_, own), r, c in zip(parts, tr, cols))
    tiles = [own.shape[0] // r for (_, own), r in zip(parts, tr)]
    first = [sum(tiles[:q]) for q in range(n_parts)]

    def body(*refs):
        o_refs = refs[2 * n_parts + n_extra:]
        step = pl.program_id(0)
        if passenger is not None:
            pl.when(step == 0)(lambda: extra_body(*refs[2 * n_parts:2 * n_parts + n_extra], o_refs[len(results)]))
        for q in range(n_parts):
            @pl.when((step >= first[q]) & (step < first[q] + tiles[q]))
            def _(q=q):
                p_ref, own_ref = refs[2 * q], refs[2 * q + 1]
                o_refs[result_of[q]][...] = (((own_ref[...] + p_ref[0].astype(F32)) + p_ref[1].astype(F32))
                                             + p_ref[2].astype(F32))

    def tile(start, count):
        return lambda i: jnp.clip(i - start, 0, count - 1)

    in_specs, args = [], []
    for q, (recv, own) in enumerate(parts):
        at = tile(first[q], tiles[q])
        in_specs.append(pl.BlockSpec((N_SHARD - 1, tr[q], cols[q]), lambda i, at=at: (0, at(i), 0)))
        in_specs.append(pl.BlockSpec((tr[q], cols[q]), lambda i, at=at: (at(i), 0)))
        args += [recv, own]
    out_shape, out_specs = [], []
    for k in range(len(results)):
        mine = [q for q in range(n_parts) if result_of[q] == k]
        count = sum(tiles[q] for q in mine)
        at = tile(first[mine[0]], count)
        out_shape.append(jax.ShapeDtypeStruct((count * tr[mine[0]], cols[mine[0]]), F32))
        out_specs.append(pl.BlockSpec((tr[mine[0]], cols[mine[0]]), lambda i, at=at: (at(i), 0)))
    whole = lambda a: pl.BlockSpec(a.shape, lambda i, nd=len(a.shape): (0,) * nd)
    in_specs += [whole(a) for a in extra_in]
    args += extra_in
    if passenger is not None:
        out_shape.append(extra_out)
        out_specs.append(whole(extra_out))
    return _hosted(body, rider, name=name, grid=(sum(tiles),), out_shape=out_shape, in_specs=in_specs,
                   out_specs=out_specs, scratch_shapes=[], compiler_params=_cparams(("arbitrary",), VMEM_BIG), args=args)


def _adamw_math(w, g, m, v):
    m2 = ADAM_B1 * m + (1.0 - ADAM_B1) * g
    v2 = ADAM_B2 * v + (1.0 - ADAM_B2) * (g * g)
    m_hat = m2 / (1.0 - ADAM_B1 ** ADAM_STEP)
    v_hat = v2 / (1.0 - ADAM_B2 ** ADAM_STEP)
    return -ADAM_LR * (m_hat / (jnp.sqrt(v_hat) + ADAM_EPS) + ADAM_WD * w), m2, v2


def _small_sums(partials, dmod, rider=None):
    moving = list(partials) + [dmod]
    n_mov = len(moving)

    def body(*refs):
        mov, refs = refs[:n_mov], refs[n_mov:]
        sums_out, refs = refs[:n_mov - 1], refs[n_mov - 1:]
        b_out, dmod_out, refs = refs[0], refs[1], refs[2:]
        everyone, (ssem, rsem) = refs[:n_mov], refs[n_mov:]
        x, y, c = _my_pos()
        me = 4 * x + 2 * y + c
        cps = []
        for a in range(n_mov):
            everyone[a][me] = mov[a][...]
            for k in range(1, N_DEV):
                peer = (_flip(x, (k >> 2) & 1), _flip(y, (k >> 1) & 1), _flip(c, k & 1))
                cps.append(pltpu.make_async_remote_copy(
                    src_ref=everyone[a].at[me], dst_ref=everyone[a].at[me], send_sem=ssem.at[a, k - 1],
                    recv_sem=rsem.at[a, k - 1], device_id=peer, device_id_type=MESH))
        for cp in cps:
            cp.start()
        for cp in cps:
            cp.wait_recv()

        def total(a):
            acc = everyone[a][0]
            for dev in range(1, N_DEV):
                acc = acc + everyone[a][dev]
            return acc

        for a in range(n_mov - 1):
            sums_out[a][...] = total(a)
        b_out[...] = jnp.sum(total(n_mov - 1), axis=0, keepdims=True)
        dmod_out[...] = everyone[n_mov - 1][...]
        for cp in cps:
            cp.wait_send()

    vm = pl.BlockSpec(memory_space=pltpu.VMEM)
    sds = jax.ShapeDtypeStruct
    out_shape = [sds(p.shape, F32) for p in partials]
    out_shape += [sds((1, dmod.shape[1]), F32), sds((N_DEV,) + dmod.shape, F32)]
    return _hosted(
        body, rider, name="small_sums", grid=(), out_shape=out_shape,
        in_specs=[vm] * n_mov, out_specs=[vm] * len(out_shape),
        scratch_shapes=[pltpu.VMEM((N_DEV,) + a.shape, F32) for a in moving]
        + [pltpu.SemaphoreType.DMA((n_mov, N_DEV - 1)), pltpu.SemaphoreType.DMA((n_mov, N_DEV - 1))],
        compiler_params=_cparams(vmem=VMEM_BIG), args=moving)


def _small_adamw(states, grads):
    n = len(states)

    def body(*refs):
        g_refs, wmv, res = refs[:n], refs[n:4 * n], refs[4 * n:]
        for j in range(n):
            g = g_refs[j][...]
            delta, m2, v2 = _adamw_math(wmv[3 * j][...], g, wmv[3 * j + 1][...], wmv[3 * j + 2][...])
            res[4 * j][...] = g
            res[4 * j + 1][...] = delta
            res[4 * j + 2][...] = m2
            res[4 * j + 3][...] = v2

    out_shape = []
    for w, _, _ in states:
        out_shape += [jax.ShapeDtypeStruct(w.shape, F32)] * 4
    outs = pl.pallas_call(body, name="small_adamw", out_shape=tuple(out_shape),
                          compiler_params=_cparams(vmem=VMEM_BIG))(*grads, *[a for st in states for a in st])
    return [outs[4 * j:4 * j + 4] for j in range(n)]


def _adamw(w, grads, m, v, name):
    rows, cols = w.shape
    tr = _row_tile(rows, cols)
    ng = len(grads)

    def body(*refs):
        w_ref = refs[0]
        g_refs = refs[1:1 + ng]
        m_ref, v_ref = refs[1 + ng], refs[2 + ng]
        g_out, d_out, m_out, v_out = refs[3 + ng:]
        g = g_refs[0][...]
        for extra in g_refs[1:]:
            g = g + extra[...]
        g_out[...] = g
        d_out[...], m_out[...], v_out[...] = _adamw_math(w_ref[...], g, m_ref[...], v_ref[...])

    spec = pl.BlockSpec((tr, cols), lambda i: (i, 0))
    out = jax.ShapeDtypeStruct((rows, cols), F32)
    return pl.pallas_call(
        body, name=name, grid=(rows // tr,),
        out_shape=(out, out, out, out),
        in_specs=[spec] * (3 + ng), out_specs=(spec, spec, spec, spec),
        compiler_params=_cparams(("arbitrary",)),
    )(w, *grads, m, v)


def _rope_tables(seq):
    half = HEAD_DIM // 2
    inv = np.float32(ROPE_THETA) ** (-np.arange(half, dtype=np.float32) / np.float32(half))
    ang = (np.arange(seq, dtype=np.float32)[:, None] * inv[None, :]).astype(np.float64)
    cos, sin = np.cos(ang).astype(np.float32), np.sin(ang).astype(np.float32)
    cos_t = np.concatenate([cos, cos, cos, cos], axis=1)
    sin_t = np.concatenate([-sin, sin, -sin, sin], axis=1)
    return jnp.asarray(cos_t), jnp.asarray(sin_t)


def kernel(x, c, w_ada, b_ada, g_attn, w_in, na_rpb, sw_sink, g_na_out, g_sw_out, w_out, g_ffn, w_up, conv_w, conv_b, w_down, g_final, loss_target, m_w_ada, m_b_ada, m_g_attn, m_w_in, m_na_rpb, m_sw_sink, m_g_na_out, m_g_sw_out, m_w_out, m_g_ffn, m_w_up, m_conv_w, m_conv_b, m_w_down, m_g_final, v_w_ada, v_b_ada, v_g_attn, v_w_in, v_na_rpb, v_sw_sink, v_g_na_out, v_g_sw_out, v_w_out, v_g_ffn, v_w_up, v_conv_w, v_conv_b, v_w_down, v_g_final):
    batch, seq, d = x.shape
    t = batch * seq
    assert d == D_MODEL and seq % (NA_ROWS * GRID_W) == 0 and seq % TOKEN_TILE == 0 and batch <= SUBLANES
    shard = 2 * lax.axis_index("x") + lax.axis_index("y")
    xt = x.reshape(t, d)
    tgt = loss_target.reshape(t, d)

    c8 = jnp.pad(c, ((0, SUBLANES - batch), (0, 0)))
    w_in_t_s = jnp.transpose(w_in[0]).astype(BF16)
    n_heads = NA_WIDTH // HEAD_DIM
    n_tiles, n_dc = 2 * NA_ROWS - 2, 2 * NA_COLS - 1
    expand, neg_mask = _na_bias_pattern()
    rpb = na_rpb[0]
    rows2 = jnp.concatenate([rpb[:, :-1, :], rpb[:, 1:, :]], axis=2).reshape(n_heads * n_tiles, 2 * n_dc)
    rows2 = jnp.pad(rows2, ((0, 0), (0, GRID_W - 2 * n_dc)))
    (mod8, sc_all, tiles), (w_in_g,) = _ada_forward(
        c8, w_ada[0], b_ada, _Rider("gather", [w_in_t_s]), _na_bias_tiles(rows2, expand, neg_mask))
    tiles = tiles.reshape(n_heads, n_tiles, GRID_W, LANES)
    mod3 = mod8[:batch].reshape(batch, 1, 6 * d)
    w_in_t = w_in_g.reshape(IN_WIDTH, d)

    cos_t, sin_t = _rope_tables(seq)
    (h1, proj), _ = _in_proj(xt, mod3, g_attn, w_in_t, cos_t, sin_t, seq)
    sink = sw_sink[0]
    w_up_b16 = w_up[0].astype(BF16)
    (oa, lse_a), (w_up_a,) = _na_forward(proj, tiles, batch, seq, _Rider("gather", [w_up_b16[:d // 2]]))
    (ob, lse_b), (w_up_b, conv_w_g, w_out_g) = _sw_forward(
        proj, sink, batch, seq, _Rider("gather", [w_up_b16[d // 2:], conv_w[0], w_out[0].astype(BF16)]))
    w_up_f = (w_up_a, w_up_b)
    w_out_f = w_out_g.reshape(d, d)
    conv_w_f = jnp.transpose(conv_w_g, (1, 0, 2)).reshape(3, D_FF)
    oab, mix, x1, h2 = _out_proj(oa, ob, g_na_out, g_sw_out, w_out_f, xt, mod3, g_ffn, seq)
    (u,), _ = _up_proj(h2, w_up_f)
    (a,), (w_down_g,) = _conv_gate(u, conv_w_f, conv_b, batch, seq, _Rider("gather", [w_down[0].astype(BF16)]))
    w_down_f = w_down_g.reshape(D_FF, d)
    dx2, dffn, loss_part, dgate_f, dg_final = _down_and_loss(a, w_down_f, x1, mod3, g_final.reshape(1, d), tgt, seq)

    gw_down, gw_down_b = _down_weight_grad(a, dffn)
    blocks = lambda g, rows: g.reshape(N_SHARD, rows // N_SHARD, d)
    (du, gconv_w, gconv_b), (recv_down, own_down) = _ffn_backward(
        dffn, w_down_f, u, conv_w_f, conv_b, batch, seq,
        _Rider("scatter", [blocks(gw_down_b, D_FF)], [blocks(gw_down, D_FF)]))
    (gw_up_top, gw_up_bot, gw_up_top_b, gw_up_bot_b), _ = _up_weight_grad(h2, du)
    (dx1, dmix, dshift_f, dscale_f, dgate_a, dg_ffn), _ = _up_backward(du, w_up_f, x1, mod3, g_ffn, dx2, mix, seq)
    doa, dob, gw_out, gw_out_b, dg_na, dg_sw = _out_backward(dmix, w_out_f, oab, oa, ob, g_na_out, g_sw_out)
    (dqa, dka, dva, dtiles), (recv_out, recv_up_bot, own_out, own_up_bot) = _na_backward(
        proj, doa, lse_a, tiles, batch, seq,
        _Rider("scatter", [blocks(gw_out_b, d), gw_up_bot_b], [blocks(gw_out, d), gw_up_bot]))
    (dq_b, dk_b, dv_b, dsink_parts), (recv_up_top, own_up_top) = _sw_backward(
        proj, dob, lse_b, sink, batch, seq, _Rider("scatter", [gw_up_top_b], [gw_up_top]))
    gx, gw_in_t, gw_in_b, dshift_a, dscale_a, dg_attn = _in_backward(
        (dqa, dka, dva), dq_b, dk_b, dv_b, w_in_t, h1, xt, mod3, g_attn, dx1, cos_t, sin_t, seq)

    late, (recv_in, own_in) = _sum_slots(
        [[(recv_out, own_out)], [(recv_up_top, own_up_top), (recv_up_bot, own_up_bot)], [(recv_down, own_down)]],
        "sum_w_out_up_down", _Rider("scatter", [blocks(gw_in_b, IN_WIDTH)], [blocks(gw_in_t, IN_WIDTH)]),
        _na_bias_grad(dtiles.reshape(n_heads * n_tiles, GRID_W, LANES), expand))

    red = late.pop()[:, :2 * n_dc]
    red = red.reshape(n_heads, n_tiles, 2, n_dc)
    zero_row = jnp.zeros((n_heads, 1, n_dc), F32)
    g_rpb = (jnp.concatenate([red[:, :, 0, :], zero_row], axis=1)
             + jnp.concatenate([zero_row, red[:, :, 1, :]], axis=1))
    g_sink = jnp.sum(dsink_parts[:, :, :2, 0], axis=0).reshape(SW_WIDTH // HEAD_DIM)

    dmod = jnp.concatenate([dshift_a, dscale_a, dgate_a, dshift_f, dscale_f, dgate_f], axis=2).reshape(batch, 6 * d)
    rpb_shape = na_rpb.shape[1:]
    states = [(g_attn, m_g_attn, v_g_attn),
              (na_rpb.reshape(rpb_shape), m_na_rpb.reshape(rpb_shape), v_na_rpb.reshape(rpb_shape)),
              (sw_sink, m_sw_sink, v_sw_sink), (g_na_out, m_g_na_out, v_g_na_out), (g_sw_out, m_g_sw_out, v_g_sw_out),
              (g_ffn, m_g_ffn, v_g_ffn), (conv_b, m_conv_b, v_conv_b),
              (g_final.reshape(1, d), m_g_final.reshape(1, d), v_g_final.reshape(1, d))]
    partials = [dg_attn, g_rpb, g_sink.reshape(sw_sink.shape), dg_na, dg_sw, dg_ffn, gconv_b, dg_final,
                gconv_w, loss_part]
    mine = _sum_slots([[(recv_in, own_in)]], "sum_w_in")[0] + late
    small, theirs = _small_sums(partials, dmod, _Rider("swap", mine))
    g_conv_w_full, loss_sum, g_b_ada, dmod_all = small[len(states):]
    r_small = _small_adamw(states + [(b_ada, m_b_ada, v_b_ada)], small[:len(states)] + [g_b_ada])
    loss = loss_sum[0, 0]
    dmod_rows = jnp.pad(dmod_all, ((0, 0), (0, SUBLANES - batch), (0, 0))).reshape(N_DEV * SUBLANES, 6 * d)
    ncol = w_ada.shape[2]
    g_w_ada = _ada_weight_grad(sc_all, lax.dynamic_slice(dmod_rows, (0, shard * ncol), (N_DEV * SUBLANES, ncol)))
    cshard = conv_w.shape[2]
    g_conv_w = lax.dynamic_slice(g_conv_w_full, (0, shard * cshard), (3, cshard))

    def big(w, m, v, g_parts, name):
        shape = w.shape
        outs = _adamw(w[0], g_parts, m[0], v[0], name)
        return [o.reshape(shape) for o in outs]

    r_w_ada = big(w_ada, m_w_ada, v_w_ada, [g_w_ada], "adamw_w_ada")
    r_w_in = [jnp.transpose(o).reshape(w_in.shape) for o in
              _adamw(jnp.transpose(w_in[0]), [mine[0], theirs[0]], jnp.transpose(m_w_in[0]), jnp.transpose(v_w_in[0]),
                     "adamw_w_in")]
    r_w_out = big(w_out, m_w_out, v_w_out, [mine[1], theirs[1]], "adamw_w_out")
    r_w_up = big(w_up, m_w_up, v_w_up, [mine[2], theirs[2]], "adamw_w_up")
    r_w_down = big(w_down, m_w_down, v_w_down, [mine[3], theirs[3]], "adamw_w_down")

    r_conv_w = big(conv_w, m_conv_w, v_conv_w, [g_conv_w], "adamw_conv_w")

    def pick(k):
        ga_, rpb_, sk_, gna_, gsw_, gf_, cb_, gfin_, b_ = [r[k] for r in r_small]
        return [r_w_ada[k], b_, ga_, r_w_in[k], rpb_.reshape(na_rpb.shape), sk_, gna_, gsw_, r_w_out[k], gf_,
                r_w_up[k], r_conv_w[k], cb_, r_w_down[k], gfin_.reshape(d)]

    return (loss, gx.reshape(batch, seq, d), *pick(0), *pick(1), *pick(2), *pick(3))
```

```python
import jax
import jax.numpy as jnp
import numpy as np
from jax import lax
from jax.experimental import pallas as pl
from jax.experimental.pallas import tpu as pltpu

F32 = jnp.float32
BF16 = jnp.bfloat16
MESH = pl.DeviceIdType.MESH

D_MODEL = 1024
HEAD_DIM = 64
NA_WIDTH = 512
SW_WIDTH = 512
SW_KV_WIDTH = 128
IN_WIDTH = 2304
D_FF = 2816
GRID_W = 64
NA_ROWS = 8
NA_COLS = 16
SW_BLOCK = 128
ROPE_THETA = 10000.0
EPS = 1e-6
NEG = -1e30
QK_SCALE = HEAD_DIM ** -0.5

ADAM_LR = 0.001
ADAM_B1 = 0.9
ADAM_B2 = 0.999
ADAM_EPS = 1e-08
ADAM_WD = 0.01
ADAM_STEP = 10

N_SHARD = 4
N_DEV = 8
LANES = 128
SUBLANES = 8
TOKEN_TILE = 512
FF_TILE = 256
CONV_CHUNK = 512
NA_GROUP = 8
SW_GROUP_BLOCKS = 8
VMEM_BIG = 56 * 1024 * 1024


def _mm(a, b):
    return jnp.dot(a, b, preferred_element_type=F32)


def _mm_nt(a, b):
    return lax.dot_general(a, b, (((1,), (1,)), ((), ())), preferred_element_type=F32)


def _mm_tn(a, b):
    return lax.dot_general(a, b, (((0,), (0,)), ((), ())), preferred_element_type=F32)


def _cparams(sem=None, vmem=None):
    kw = {}
    if sem is not None:
        kw["dimension_semantics"] = sem
    if vmem is not None:
        kw["vmem_limit_bytes"] = vmem
    return pltpu.CompilerParams(**kw)


def _resident(shape):
    return pl.BlockSpec(shape, lambda i: (0,) * len(shape), pipeline_mode=pl.Buffered(1))


def _sigmoid(x):
    return 1.0 / (1.0 + jnp.exp(-x))


def _rms_stats(x):
    r = lax.rsqrt(jnp.mean(x * x, axis=-1, keepdims=True) + EPS)
    return r, x * r


def _rms_bwd(dxn, xn, r):
    return r * (dxn - xn * jnp.mean(dxn * xn, axis=-1, keepdims=True))


def _my_pos():
    return lax.axis_index("x"), lax.axis_index("y"), lax.axis_index("c")


def _flip(v, bit):
    return 1 - v if bit else v


def _ada_forward(c8, w_ada, b_ada, rider, passenger):
    d = c8.shape[1]
    ncol = w_ada.shape[1]
    extra_body, extra_in, extra_out = passenger
    n_extra = len(extra_in)

    def body(c_ref, w_ref, b_ref, *refs):
        extra_refs = refs[:n_extra]
        mod_ref, sc_ref, extra_ref, m_scr, mod_buf, ssem, rsem, ssem2, rsem2 = refs[n_extra:]
        x, y, c = _my_pos()
        me = 4 * x + 2 * y + c
        shard = 2 * x + y
        cv = c_ref[...]
        my_rows = pl.ds(pl.multiple_of(me * SUBLANES, SUBLANES), SUBLANES)
        sc_ref[my_rows, :] = cv * _sigmoid(cv)

        def copy1(k):
            peer = (_flip(x, (k >> 2) & 1), _flip(y, (k >> 1) & 1), _flip(c, k & 1))
            return pltpu.make_async_remote_copy(
                src_ref=sc_ref.at[my_rows, :], dst_ref=sc_ref.at[my_rows, :],
                send_sem=ssem.at[k - 1], recv_sem=rsem.at[k - 1], device_id=peer, device_id_type=MESH)

        sends = [copy1(k) for k in range(1, N_DEV)]
        for cp in sends:
            cp.start()
        extra_body(*extra_refs, extra_ref)
        for cp in sends:
            cp.wait_recv()
        m_scr[...] = _mm(sc_ref[...].astype(BF16), w_ref[...].astype(BF16))

        def copy2(k):
            px, py = _flip(x, (k >> 1) & 1), _flip(y, k & 1)
            rows = pl.ds(pl.multiple_of((4 * px + 2 * py + c) * SUBLANES, SUBLANES), SUBLANES)
            return pltpu.make_async_remote_copy(
                src_ref=m_scr.at[rows, :], dst_ref=mod_buf.at[shard],
                send_sem=ssem2.at[k - 1], recv_sem=rsem2.at[k - 1], device_id=(px, py, c), device_id_type=MESH)

        sends2 = [copy2(k) for k in range(1, N_SHARD)]
        for cp in sends2:
            cp.start()
        mod_buf[shard] = m_scr[my_rows, :]
        for cp in sends2:
            cp.wait_recv()
        for s in range(N_SHARD):
            mod_ref[:, s * ncol:(s + 1) * ncol] = mod_buf[s] + b_ref[:, s * ncol:(s + 1) * ncol]
        for cp in sends + sends2:
            cp.wait_send()

    vm = pl.BlockSpec(memory_space=pltpu.VMEM)
    return _hosted(
        body, rider, name="ada_forward", grid=(),
        out_shape=(jax.ShapeDtypeStruct((SUBLANES, N_SHARD * ncol), F32),
                   jax.ShapeDtypeStruct((N_DEV * SUBLANES, d), F32), extra_out),
        in_specs=[vm] * (3 + n_extra), out_specs=(vm, vm, vm),
        scratch_shapes=[pltpu.VMEM((N_DEV * SUBLANES, ncol), F32), pltpu.VMEM((N_SHARD, SUBLANES, ncol), F32),
                        pltpu.SemaphoreType.DMA((N_DEV - 1,)), pltpu.SemaphoreType.DMA((N_DEV - 1,)),
                        pltpu.SemaphoreType.DMA((N_SHARD - 1,)), pltpu.SemaphoreType.DMA((N_SHARD - 1,))],
        compiler_params=_cparams(vmem=VMEM_BIG), args=[c8, w_ada, b_ada] + extra_in)


class _Rider:
    def __init__(self, kind, srcs, owns=()):
        self.kind, self.srcs, self.owns = kind, list(srcs), list(owns)
        n = len(self.srcs)
        sds = jax.ShapeDtypeStruct
        dma = pltpu.SemaphoreType.DMA
        if kind == "gather":
            self.out_shapes = [sds((N_SHARD,) + s.shape, s.dtype) for s in self.srcs]
            self.sems = [dma((n, N_SHARD - 1)), dma((n, N_SHARD - 1)), dma((n, N_SHARD - 1)), dma((n, N_SHARD - 1)),
                         dma((n,)), dma((n,))]
        elif kind == "scatter":
            self.out_shapes = ([sds((N_SHARD - 1,) + s.shape[1:], s.dtype) for s in self.srcs]
                               + [sds(o.shape[1:], o.dtype) for o in self.owns])
            m = max(len(self.owns), 1)
            self.sems = [dma((n, N_SHARD - 1)), dma((n, N_SHARD - 1)), dma((m,)), dma((m,))]
        else:
            self.out_shapes = [sds(s.shape, s.dtype) for s in self.srcs]
            self.sems = [dma((n,)), dma((n,))]

    @property
    def inputs(self):
        return self.srcs + self.owns

    def _halved(self, i):
        a = self.srcs[i]
        tile_rows = SUBLANES * (4 // jnp.dtype(a.dtype).itemsize)
        return self.kind == "gather" and a.shape[0] % (2 * tile_rows) == 0

    def copies(self, ins, outs, sems):
        n = len(self.srcs)
        x, y, c = _my_pos()
        shard = 2 * x + y
        remote, relay = [], []
        if self.kind == "swap":
            ssem, rsem = sems
            for i in range(n):
                remote.append(pltpu.make_async_remote_copy(
                    src_ref=ins[i], dst_ref=outs[i], send_sem=ssem.at[i], recv_sem=rsem.at[i],
                    device_id=(x, y, 1 - c), device_id_type=MESH))
            return remote, relay
        if self.kind == "gather":
            ssem, rsem, ssem2, rsem2, sib_s, sib_r = sems
        else:
            ssem, rsem, sib_s, sib_r = sems
        for i in range(n):
            if self.kind == "gather":
                remote.append(pltpu.make_async_remote_copy(
                    src_ref=ins[i], dst_ref=outs[i].at[shard], send_sem=sib_s.at[i], recv_sem=sib_r.at[i],
                    device_id=(x, y, 1 - c), device_id_type=MESH))
                half = ins[i].shape[0] // 2
                mine = pl.ds(pl.multiple_of(c * half, half), half) if self._halved(i) else None
            for k in range(1, N_SHARD):
                px, py = _flip(x, (k >> 1) & 1), _flip(y, k & 1)
                if self.kind == "gather":
                    src, dst = ins[i], outs[i].at[shard]
                    if mine is not None:
                        src, dst = src.at[mine], dst.at[mine]
                        got = outs[i].at[2 * px + py].at[mine]
                        relay.append(pltpu.make_async_remote_copy(
                            src_ref=got, dst_ref=got, send_sem=ssem2.at[i, k - 1], recv_sem=rsem2.at[i, k - 1],
                            device_id=(x, y, 1 - c), device_id_type=MESH))
                else:
                    src, dst = ins[i].at[2 * px + py], outs[i].at[k - 1]
                remote.append(pltpu.make_async_remote_copy(
                    src_ref=src, dst_ref=dst, send_sem=ssem.at[i, k - 1], recv_sem=rsem.at[i, k - 1],
                    device_id=(px, py, c), device_id_type=MESH))
        if self.kind == "scatter":
            for i in range(len(self.owns)):
                remote.append(pltpu.make_async_remote_copy(
                    src_ref=ins[n + i].at[shard], dst_ref=outs[n + i], send_sem=sib_s.at[i], recv_sem=sib_r.at[i],
                    device_id=(x, y, 1 - c), device_id_type=MESH))
        return remote, relay

    def start(self, ins, outs, sems):
        remote, _ = self.copies(ins, outs, sems)
        for cp in remote:
            cp.start()

    def wait(self, ins, outs, sems):
        remote, relay = self.copies(ins, outs, sems)
        for cp in remote:
            cp.wait_recv()
        for cp in relay:
            cp.start()
        for cp in relay:
            cp.wait_recv()
        for cp in remote + relay:
            cp.wait_send()


def _hosted(body, rider, *, name, grid, out_shape, in_specs, out_specs, scratch_shapes, compiler_params, args):
    out_shape, out_specs = list(out_shape), list(out_specs)
    if rider is None:
        outs = pl.pallas_call(body, name=name, grid=grid, out_shape=tuple(out_shape), in_specs=list(in_specs),
                              out_specs=tuple(out_specs), scratch_shapes=list(scratch_shapes),
                              compiler_params=compiler_params)(*args)
        return list(outs), []
    n_in, n_out, n_scr = len(in_specs), len(out_shape), len(scratch_shapes)
    nr_in, nr_out = len(rider.inputs), len(rider.out_shapes)
    n_steps = 1
    for size in grid:
        n_steps *= size

    def full(*refs):
        ins, refs = refs[:n_in], refs[n_in:]
        r_in, refs = refs[:nr_in], refs[nr_in:]
        outs, refs = refs[:n_out], refs[n_out:]
        r_out, refs = refs[:nr_out], refs[nr_out:]
        scr, sems = refs[:n_scr], refs[n_scr:]
        if grid:
            step = 0
            for ax, size in enumerate(grid):
                step = step * size + pl.program_id(ax)
            pl.when(step == 0)(lambda: rider.start(r_in, r_out, sems))
            body(*ins, *outs, *scr)
            pl.when(step == n_steps - 1)(lambda: rider.wait(r_in, r_out, sems))
        else:
            rider.start(r_in, r_out, sems)
            body(*ins, *outs, *scr)
            rider.wait(r_in, r_out, sems)

    hbm = pl.BlockSpec(memory_space=pl.ANY)
    res = pl.pallas_call(
        full, name=name, grid=grid, out_shape=tuple(out_shape + rider.out_shapes),
        in_specs=list(in_specs) + [hbm] * nr_in, out_specs=tuple(out_specs + [hbm] * nr_out),
        scratch_shapes=list(scratch_shapes) + rider.sems, compiler_params=compiler_params,
    )(*args, *rider.inputs)
    return list(res[:n_out]), list(res[n_out:])


def _rope_rot(t):
    w = t.shape[1]
    lane = lax.broadcasted_iota(jnp.int32, t.shape, 1)
    first = (lane % HEAD_DIM) < (HEAD_DIM // 2)
    return jnp.where(first, pltpu.roll(t, w - HEAD_DIM // 2, 1), pltpu.roll(t, HEAD_DIM // 2, 1))


def _in_proj(x, mod3, g_attn, w_in_t, cos_t, sin_t, seq, rider=None):
    t, d = x.shape
    tm = 2 * TOKEN_TILE
    per_seq = seq // tm
    rope_lo, rope_hi = 3 * NA_WIDTH, 3 * NA_WIDTH + SW_WIDTH + SW_KV_WIDTH
    n_rep = (rope_hi - rope_lo) // LANES

    def body(x_ref, mod_ref, g_ref, w_ref, cos_ref, sin_ref, h_ref, p_ref):
        r, xn = _rms_stats(x_ref[...])
        shift, scale = mod_ref[0, :, 0:d], mod_ref[0, :, d:2 * d]
        hb = ((xn * g_ref[...]) * (1.0 + scale) + shift).astype(BF16)
        h_ref[...] = hb
        p_ref[:, :rope_lo] = _mm_nt(hb, w_ref[:rope_lo, :]).astype(BF16)
        pr = _mm_nt(hb, w_ref[rope_lo:rope_hi, :])
        cos = jnp.concatenate([cos_ref[...]] * n_rep, axis=1)
        sin = jnp.concatenate([sin_ref[...]] * n_rep, axis=1)
        p_ref[:, rope_lo:rope_hi] = (pr * cos + _rope_rot(pr) * sin).astype(BF16)
        p_ref[:, rope_hi:] = _mm_nt(hb, w_ref[rope_hi:, :]).astype(BF16)

    return _hosted(
        body, rider, name="in_proj", grid=(t // tm,),
        out_shape=[jax.ShapeDtypeStruct((t, d), BF16), jax.ShapeDtypeStruct((t, IN_WIDTH), BF16)],
        in_specs=[pl.BlockSpec((tm, d), lambda i: (i, 0)),
                  pl.BlockSpec((1, 1, 6 * d), lambda i: (i // per_seq, 0, 0)),
                  pl.BlockSpec((1, d), lambda i: (0, 0)),
                  pl.BlockSpec((IN_WIDTH, d), lambda i: (0, 0)),
                  pl.BlockSpec((tm, LANES), lambda i: (i % per_seq, 0)),
                  pl.BlockSpec((tm, LANES), lambda i: (i % per_seq, 0))],
        out_specs=[pl.BlockSpec((tm, d), lambda i: (i, 0)), pl.BlockSpec((tm, IN_WIDTH), lambda i: (i, 0))],
        scratch_shapes=[], compiler_params=_cparams(("arbitrary",), VMEM_BIG),
        args=[x, mod3, g_attn, w_in_t, cos_t, sin_t])


def _na_bias_pattern():
    n_dc = 2 * NA_COLS - 1
    j = np.arange(GRID_W)[:, None]
    m = np.arange(GRID_W * LANES)[None, :]
    q, lane = m // LANES, m % LANES
    k = lane % GRID_W
    cs = np.clip(q - NA_COLS // 2, 0, GRID_W - NA_COLS)
    ok = (k >= cs) & (k < cs + NA_COLS)
    hit = ok & (j < 2 * n_dc) & (lane // GRID_W == j // n_dc) & (k - q + (NA_COLS - 1) == j % n_dc)
    return jnp.asarray(hit.astype(np.float32)), jnp.asarray(np.where(ok, 0.0, NEG).astype(np.float32))


def _na_bias_tiles(rows2, expand, mask):
    n, width = rows2.shape[0], expand.shape[1]
    q_step = 16
    step = q_step * LANES

    def body(r_ref, e_ref, m_ref, o_ref):
        for i in range(width // step):
            at = slice(i * step, (i + 1) * step)
            flat = jnp.dot(r_ref[...], e_ref[:, at], precision=lax.Precision.HIGHEST,
                           preferred_element_type=F32) + m_ref[:, at]
            for qq in range(q_step):
                o_ref[:, i * q_step + qq, :] = flat[:, qq * LANES:(qq + 1) * LANES]

    return body, [rows2, expand, mask], jax.ShapeDtypeStruct((n, GRID_W, LANES), F32)


def _na_prepare(k_ref, v_ref, km, vm):
    lane = lax.broadcasted_iota(jnp.int32, k_ref.shape, 1)
    low = lane < HEAD_DIM
    kv = k_ref[...]
    vv = v_ref[...]
    zero = jnp.zeros_like(kv)
    km[0] = jnp.where(low, kv, zero)
    km[1] = jnp.where(low, zero, kv)
    vm[0] = jnp.where(low, vv, zero)
    vm[1] = jnp.where(low, zero, vv)


def _na_window(r, n_rows):
    rs = jnp.clip(r - NA_ROWS // 2, 0, n_rows - NA_ROWS)
    return rs, r - rs


def _na_pair_window(ref, wrows):
    return jnp.concatenate([ref[0, wrows, :], ref[1, wrows, :]], axis=0)


def _na_scores(q, k2, tp_ref, off):
    bias = jnp.concatenate([tp_ref[h, 2 * w - off + (NA_ROWS - 1)] for h in range(2) for w in range(NA_ROWS // 2)],
                           axis=1)
    return _mm_nt(q, k2) * QK_SCALE + bias


def _pair_lse_block(lse):
    lane = lax.broadcasted_iota(jnp.int32, (lse[0].shape[0], LANES), 1)
    return jnp.where(lane < HEAD_DIM, lse[0], lse[1])


def _pair_softmax(s):
    win = s.shape[1] // 2
    halves, lse = [], []
    for h in range(2):
        sh = s[:, h * win:(h + 1) * win]
        m = jnp.max(sh, axis=-1, keepdims=True)
        e = jnp.exp(sh - m)
        l = jnp.sum(e, axis=-1, keepdims=True)
        halves.append(e / l)
        lse.append(m + jnp.log(l))
    return jnp.concatenate(halves, axis=1), _pair_lse_block(lse)


def _pair_grad(w2, x, low):
    keys = w2.shape[1] // 2
    zero = jnp.zeros_like(x)
    low_x = low[:x.shape[0]]
    stacked = jnp.concatenate([w2[:, :keys], w2[:, keys:]], axis=0)
    diag = jnp.concatenate([jnp.where(low_x, x, zero), jnp.where(low_x, zero, x)], axis=0)
    return _mm_tn(stacked, diag)


def _pair_probs_from_lse(s, lse_block):
    win = s.shape[1] // 2
    return jnp.concatenate([jnp.exp(s[:, h * win:(h + 1) * win] - lse_block[:, h * HEAD_DIM:h * HEAD_DIM + 1])
                            for h in range(2)], axis=1)


def _na_forward(proj, tiles, batch, seq, rider=None):
    t = proj.shape[0]
    n_rows = seq // GRID_W
    n_pairs = NA_WIDTH // LANES
    win = NA_ROWS * GRID_W

    def body(q_ref, k_ref, v_ref, tp_ref, o_ref, lse_ref, km, vm):
        _na_prepare(k_ref, v_ref, km, vm)

        def scores(r):
            rs, off = _na_window(r, n_rows)
            rows = pl.ds(pl.multiple_of(r * GRID_W, GRID_W), GRID_W)
            wrows = pl.ds(pl.multiple_of(rs * GRID_W, GRID_W), win)
            return rows, wrows, _na_scores(q_ref[rows, :], _na_pair_window(km, wrows), tp_ref, off)

        def finish(rows, wrows, s):
            p, lse = _pair_softmax(s)
            lse_ref[rows, :] = lse
            o_ref[rows, :] = _mm(p.astype(BF16), _na_pair_window(vm, wrows))

        def row_group(i, carry):
            for state in [scores(NA_GROUP * i + j) for j in range(NA_GROUP)]:
                finish(*state)
            return carry

        lax.fori_loop(0, n_rows // NA_GROUP, row_group, 0)

    return _hosted(
        body, rider, name="na_forward", grid=(batch, n_pairs),
        out_shape=[jax.ShapeDtypeStruct((t, NA_WIDTH), F32), jax.ShapeDtypeStruct((t, NA_WIDTH), F32)],
        in_specs=[pl.BlockSpec((seq, LANES), lambda b, p: (b, p)),
                  pl.BlockSpec((seq, LANES), lambda b, p: (b, n_pairs + p)),
                  pl.BlockSpec((seq, LANES), lambda b, p: (b, 2 * n_pairs + p)),
                  pl.BlockSpec((2, 2 * NA_ROWS - 2, GRID_W, LANES), lambda b, p: (p, 0, 0, 0))],
        out_specs=[pl.BlockSpec((seq, LANES), lambda b, p: (b, p)), pl.BlockSpec((seq, LANES), lambda b, p: (b, p))],
        scratch_shapes=[pltpu.VMEM((2, seq, LANES), BF16), pltpu.VMEM((2, seq, LANES), BF16)],
        compiler_params=_cparams(("arbitrary", "arbitrary")), args=[proj, proj, proj, tiles])


def _sw_prepare(kv_ref, g, dst_lo, dst_hi, seq):
    lane = lax.broadcasted_iota(jnp.int32, kv_ref.shape, 1)
    mine = (lane // HEAD_DIM) == g
    kg = jnp.where(mine, kv_ref[...].astype(F32), 0.0)
    kr = pltpu.roll(kg, HEAD_DIM, 1)
    first = g == 0
    zero = jnp.zeros((SW_BLOCK, LANES), BF16)
    for dst, val in ((dst_lo, jnp.where(first, kg, kr)), (dst_hi, jnp.where(first, kr, kg))):
        dst[0:SW_BLOCK, :] = zero
        dst[SW_BLOCK:SW_BLOCK + seq, :] = val.astype(BF16)
        dst[SW_BLOCK + seq:, :] = zero


def _sw_mask(n, seq):
    qi = lax.broadcasted_iota(jnp.int32, (SW_BLOCK, 3 * SW_BLOCK), 0)
    kj = lax.broadcasted_iota(jnp.int32, (SW_BLOCK, 3 * SW_BLOCK), 1)
    kpos = n * SW_BLOCK - SW_BLOCK + kj
    return (jnp.abs(qi + SW_BLOCK - kj) <= SW_BLOCK) & (kpos >= 0) & (kpos < seq)


def _sw_probs(s2, ok, sinks):
    band = s2.shape[1] // 2
    halves, lse = [], []
    for i in range(2):
        s = jnp.where(ok, s2[:, i * band:(i + 1) * band], NEG)
        m = jnp.maximum(jnp.max(s, axis=-1, keepdims=True), sinks[i])
        p = jnp.exp(s - m)
        den = jnp.sum(p, axis=-1, keepdims=True) + jnp.exp(sinks[i] - m)
        halves.append(p / den)
        lse.append(m + jnp.log(den))
    return jnp.concatenate(halves, axis=1), _pair_lse_block(lse)


def _sw_probs_from_lse(s2, ok, sinks, lse_block):
    band = s2.shape[1] // 2
    halves, sink_p = [], []
    for i in range(2):
        lse = lse_block[:, i * HEAD_DIM:i * HEAD_DIM + 1]
        halves.append(jnp.exp(jnp.where(ok, s2[:, i * band:(i + 1) * band], NEG) - lse))
        sink_p.append(jnp.exp(sinks[i] - lse))
    return jnp.concatenate(halves, axis=1), sink_p


def _sw_forward(proj, sink, batch, seq, rider=None):
    t = proj.shape[0]
    n_pairs = SW_WIDTH // LANES
    q_blk = 3 * NA_WIDTH // LANES
    k_blk = q_blk + n_pairs
    n_blocks = seq // SW_BLOCK
    pad = seq + 2 * SW_BLOCK

    def body(sink_ref, q_ref, k_ref, v_ref, o_ref, lse_ref, k_lo, k_hi, v_lo, v_hi):
        hp = pl.program_id(1)
        g = hp // 2

        @pl.when(hp % 2 == 0)
        def _():
            _sw_prepare(k_ref, g, k_lo, k_hi, seq)
            _sw_prepare(v_ref, g, v_lo, v_hi, seq)

        sinks = (sink_ref[2 * hp], sink_ref[2 * hp + 1])

        def scores(n):
            rows = pl.ds(pl.multiple_of(n * SW_BLOCK, SW_BLOCK), SW_BLOCK)
            wrows = pl.ds(pl.multiple_of(n * SW_BLOCK, SW_BLOCK), 3 * SW_BLOCK)
            k2 = jnp.concatenate([k_lo[wrows, :], k_hi[wrows, :]], axis=0)
            return n, rows, wrows, _mm_nt(q_ref[rows, :], k2) * QK_SCALE

        def finish(n, rows, wrows, s2):
            p, lse = _sw_probs(s2, _sw_mask(n, seq), sinks)
            lse_ref[rows, :] = lse
            v2 = jnp.concatenate([v_lo[wrows, :], v_hi[wrows, :]], axis=0)
            o_ref[rows, :] = _mm(p.astype(BF16), v2)

        def block_group(i, carry):
            for state in [scores(SW_GROUP_BLOCKS * i + j) for j in range(SW_GROUP_BLOCKS)]:
                finish(*state)
            return carry

        lax.fori_loop(0, n_blocks // SW_GROUP_BLOCKS, block_group, 0)

    return _hosted(
        body, rider, name="sw_forward", grid=(batch, n_pairs),
        out_shape=[jax.ShapeDtypeStruct((t, SW_WIDTH), F32), jax.ShapeDtypeStruct((t, SW_WIDTH), F32)],
        in_specs=[pl.BlockSpec(memory_space=pltpu.SMEM),
                  pl.BlockSpec((seq, LANES), lambda b, p: (b, q_blk + p)),
                  pl.BlockSpec((seq, LANES), lambda b, p: (b, k_blk)),
                  pl.BlockSpec((seq, LANES), lambda b, p: (b, k_blk + 1))],
        out_specs=[pl.BlockSpec((seq, LANES), lambda b, p: (b, p)), pl.BlockSpec((seq, LANES), lambda b, p: (b, p))],
        scratch_shapes=[pltpu.VMEM((pad, LANES), BF16)] * 4,
        compiler_params=_cparams(("arbitrary", "arbitrary")), args=[sink, proj, proj, proj])


def _out_proj(oa, ob, g_na, g_sw, w_out, x, mod3, g_ffn, seq):
    t, d = x.shape
    tm = TOKEN_TILE
    per_seq = seq // tm

    def body(oa_ref, ob_ref, gna_ref, gsw_ref, w_ref, x_ref, mod_ref, gf_ref, oab_ref, mix_ref, x1_ref, h2_ref):
        _, na = _rms_stats(oa_ref[...])
        _, nb = _rms_stats(ob_ref[...])
        oab = jnp.concatenate([na * gna_ref[...], nb * gsw_ref[...]], axis=1).astype(BF16)
        oab_ref[...] = oab
        mix = _mm(oab, w_ref[...])
        mix_ref[...] = mix
        gate_a = mod_ref[0, :, 2 * d:3 * d]
        shift_f, scale_f = mod_ref[0, :, 3 * d:4 * d], mod_ref[0, :, 4 * d:5 * d]
        x1 = x_ref[...] + gate_a * mix
        x1_ref[...] = x1
        _, xn = _rms_stats(x1)
        h2_ref[...] = ((xn * gf_ref[...]) * (1.0 + scale_f) + shift_f).astype(BF16)

    tile = lambda w: pl.BlockSpec((tm, w), lambda i: (i, 0))
    vec = lambda w: pl.BlockSpec((1, w), lambda i: (0, 0))
    return pl.pallas_call(
        body, name="out_proj", grid=(t // tm,),
        out_shape=(jax.ShapeDtypeStruct((t, d), BF16), jax.ShapeDtypeStruct((t, d), F32),
                   jax.ShapeDtypeStruct((t, d), F32), jax.ShapeDtypeStruct((t, d), BF16)),
        in_specs=[tile(NA_WIDTH), tile(SW_WIDTH), vec(NA_WIDTH), vec(SW_WIDTH),
                  pl.BlockSpec((d, d), lambda i: (0, 0)), tile(d),
                  pl.BlockSpec((1, 1, 6 * d), lambda i: (i // per_seq, 0, 0)), vec(d)],
        out_specs=(tile(d), tile(d), tile(d), tile(d)),
        compiler_params=_cparams(("arbitrary",), VMEM_BIG),
    )(oa, ob, g_na, g_sw, w_out, x, mod3, g_ffn)


def _up_proj(h2, w_up_halves, rider=None):
    t, d = h2.shape
    tm = 2 * TOKEN_TILE
    w_a, w_b = w_up_halves
    half, wcol = w_a.shape[1], w_a.shape[2]

    def body(h_ref, wa_ref, wb_ref, u_ref):
        u_ref[0] = (_mm(h_ref[:, :half], wa_ref[0]) + _mm(h_ref[:, half:], wb_ref[0])).astype(BF16)

    w_spec = pl.BlockSpec((1, half, wcol), lambda j, i: (j, 0, 0))
    return _hosted(
        body, rider, name="up_proj", grid=(N_SHARD, t // tm),
        out_shape=[jax.ShapeDtypeStruct((2, t, D_FF), BF16)],
        in_specs=[pl.BlockSpec((tm, d), lambda j, i: (i, 0)), w_spec, w_spec],
        out_specs=[pl.BlockSpec((1, tm, wcol), lambda j, i: (j // 2, i, j % 2))],
        scratch_shapes=[], compiler_params=_cparams(("arbitrary", "arbitrary"), VMEM_BIG), args=[h2, w_a, w_b])


def _taps_chunk(load, s, rows, seq):
    halo = 2 * SUBLANES
    cur = load(s, rows)
    above = load(pl.multiple_of(jnp.maximum(s - halo, 0), halo), halo)
    below = load(pl.multiple_of(jnp.minimum(s + rows, seq - halo), halo), halo)
    up = jnp.where(s > 0, above[halo - 1:halo, :], 0.0)
    dn = jnp.where(s + rows < seq, below[0:1, :], 0.0)
    row = lax.broadcasted_iota(jnp.int32, cur.shape, 0)
    prev = jnp.where(row == 0, up, pltpu.roll(cur, 1, 0))
    nxt = jnp.where(row == rows - 1, dn, pltpu.roll(cur, rows - 1, 0))
    return cur, prev, nxt


def _conv_gate(u, conv_w, conv_b, batch, seq, rider=None):
    t = u.shape[1]
    cw = FF_TILE
    rows = CONV_CHUNK

    def body(u_ref, w_ref, b_ref, a_ref):
        def chunk(i, carry):
            s = pl.multiple_of(i * rows, rows)
            gt, prev, nxt = _taps_chunk(lambda at, n: u_ref[1, pl.ds(at, n), :].astype(F32), s, rows, seq)
            gc = prev * w_ref[0:1, :] + gt * w_ref[1:2, :] + nxt * w_ref[2:3, :] + b_ref[...]
            a_ref[pl.ds(s, rows), :] = ((gc * _sigmoid(gc)) * u_ref[0, pl.ds(s, rows), :].astype(F32)).astype(BF16)
            return carry

        lax.fori_loop(0, seq // rows, chunk, 0)

    return _hosted(
        body, rider, name="conv_gate", grid=(batch, D_FF // cw),
        out_shape=[jax.ShapeDtypeStruct((t, D_FF), BF16)],
        in_specs=[pl.BlockSpec((2, seq, cw), lambda b, j: (0, b, j)),
                  pl.BlockSpec((3, cw), lambda b, j: (0, j)), pl.BlockSpec((1, cw), lambda b, j: (0, j))],
        out_specs=[pl.BlockSpec((seq, cw), lambda b, j: (b, j))], scratch_shapes=[],
        compiler_params=_cparams(("arbitrary", "arbitrary"), VMEM_BIG), args=[u, conv_w, conv_b])


def _down_and_loss(a, w_down, x1, mod3, g_final, target, seq):
    t, d = x1.shape
    tm = TOKEN_TILE
    per_seq = seq // tm
    batch = t // seq

    def body(a_ref, w_ref, x1_ref, mod_ref, g_ref, tgt_ref, dx2_ref, dffn_ref, loss_ref, dgate_ref, dg_ref):
        i = pl.program_id(0)
        f = _mm(a_ref[...], w_ref[...])
        gate_f = mod_ref[0, :, 5 * d:6 * d]
        x2 = x1_ref[...] + gate_f * f
        r, xn = _rms_stats(x2)
        err = xn * g_ref[...] - tgt_ref[...]
        part = 0.5 * jnp.sum(jnp.mean(err * err, axis=-1, keepdims=True))
        dy = err / d
        dx2 = _rms_bwd(dy * g_ref[...], xn, r)
        dx2_ref[...] = dx2
        dffn_ref[...] = (dx2 * gate_f).astype(BF16)

        @pl.when(i == 0)
        def _():
            loss_ref[...] = jnp.zeros_like(loss_ref)
            dg_ref[...] = jnp.zeros_like(dg_ref)

        @pl.when(i % per_seq == 0)
        def _():
            dgate_ref[...] = jnp.zeros_like(dgate_ref)

        loss_ref[...] += part
        dg_ref[...] += jnp.sum(dy * xn, axis=0, keepdims=True)
        dgate_ref[0] += jnp.sum(dx2 * f, axis=0, keepdims=True)

    tile = lambda w: pl.BlockSpec((tm, w), lambda i: (i, 0))
    return pl.pallas_call(
        body, name="down_loss", grid=(t // tm,),
        out_shape=(jax.ShapeDtypeStruct((t, d), F32), jax.ShapeDtypeStruct((t, d), BF16),
                   jax.ShapeDtypeStruct((SUBLANES, LANES), F32), jax.ShapeDtypeStruct((batch, 1, d), F32),
                   jax.ShapeDtypeStruct((1, d), F32)),
        in_specs=[tile(D_FF), _resident((D_FF, d)), tile(d),
                  pl.BlockSpec((1, 1, 6 * d), lambda i: (i // per_seq, 0, 0)),
                  pl.BlockSpec((1, d), lambda i: (0, 0)), tile(d)],
        out_specs=(tile(d), tile(d), pl.BlockSpec((SUBLANES, LANES), lambda i: (0, 0)),
                   pl.BlockSpec((1, 1, d), lambda i: (i // per_seq, 0, 0)), pl.BlockSpec((1, d), lambda i: (0, 0))),
        compiler_params=_cparams(("arbitrary",), VMEM_BIG),
    )(a, w_down, x1, mod3, g_final, target)


def _down_weight_grad(a, dffn):
    t, dff = a.shape
    d = dffn.shape[1]
    tk = 2 * TOKEN_TILE
    n_k = t // tk

    def body(a_ref, df_ref, g_ref, gb_ref):
        k = pl.program_id(0)

        @pl.when(k == 0)
        def _():
            g_ref[...] = jnp.zeros_like(g_ref)

        g_ref[...] += _mm_tn(a_ref[...], df_ref[...])

        @pl.when(k == n_k - 1)
        def _():
            gb_ref[...] = g_ref[...].astype(BF16)

    whole = _resident((dff, d))
    return pl.pallas_call(
        body, name="down_weight_grad", grid=(n_k,),
        out_shape=(jax.ShapeDtypeStruct((dff, d), F32), jax.ShapeDtypeStruct((dff, d), BF16)),
        in_specs=[pl.BlockSpec((tk, dff), lambda k: (k, 0)), pl.BlockSpec((tk, d), lambda k: (k, 0))],
        out_specs=(whole, whole),
        compiler_params=_cparams(("arbitrary",), VMEM_BIG),
    )(a, dffn)


def _ffn_backward(dffn, w_down, u, conv_w, conv_b, batch, seq, rider=None):
    t, d = dffn.shape
    cw = FF_TILE
    rows = CONV_CHUNK

    def body(df_ref, wd_ref, u_ref, w_ref, b_ref, du_ref, gcw_ref, gcb_ref, da_scr, dgc_scr):
        b = pl.program_id(1)
        da_scr[...] = _mm_nt(df_ref[...], wd_ref[...])

        @pl.when(b == 0)
        def _():
            gcw_ref[...] = jnp.zeros_like(gcw_ref)
            gcb_ref[...] = jnp.zeros_like(gcb_ref)

        def fold(v):
            return jnp.sum(v.reshape(rows // SUBLANES, SUBLANES, cw), axis=0)

        def chunk(i, carry):
            s = pl.multiple_of(i * rows, rows)
            here = pl.ds(s, rows)
            gt, prev, nxt = _taps_chunk(lambda at, n: u_ref[1, pl.ds(at, n), :].astype(F32), s, rows, seq)
            val, da = u_ref[0, here, :].astype(F32), da_scr[here, :]
            gc = prev * w_ref[0:1, :] + gt * w_ref[1:2, :] + nxt * w_ref[2:3, :] + b_ref[...]
            sg = _sigmoid(gc)
            sl = gc * sg
            du_ref[0, here, :] = (da * sl).astype(BF16)
            dgc = (da * val) * (sg * (1.0 + gc * (1.0 - sg)))
            dgc_scr[here, :] = dgc
            cb, c0, c1, c2 = carry
            return cb + fold(dgc), c0 + fold(dgc * prev), c1 + fold(dgc * gt), c2 + fold(dgc * nxt)

        zero = jnp.zeros((SUBLANES, cw), F32)
        cb, c0, c1, c2 = lax.fori_loop(0, seq // rows, chunk, (zero, zero, zero, zero))
        gcb_ref[...] += jnp.sum(cb, axis=0, keepdims=True)
        gcw_ref[0:1, :] += jnp.sum(c0, axis=0, keepdims=True)
        gcw_ref[1:2, :] += jnp.sum(c1, axis=0, keepdims=True)
        gcw_ref[2:3, :] += jnp.sum(c2, axis=0, keepdims=True)

        def chunk2(i, carry):
            s = pl.multiple_of(i * rows, rows)
            dgc, dprev, dnxt = _taps_chunk(lambda at, n: dgc_scr[pl.ds(at, n), :], s, rows, seq)
            du_ref[1, pl.ds(s, rows), :] = (dnxt * w_ref[0:1, :] + dgc * w_ref[1:2, :]
                                            + dprev * w_ref[2:3, :]).astype(BF16)
            return carry

        lax.fori_loop(0, seq // rows, chunk2, 0)

    return _hosted(
        body, rider, name="ffn_backward", grid=(D_FF // cw, batch),
        out_shape=[jax.ShapeDtypeStruct((2, t, D_FF), BF16),
                   jax.ShapeDtypeStruct((3, D_FF), F32), jax.ShapeDtypeStruct((1, D_FF), F32)],
        in_specs=[pl.BlockSpec((seq, d), lambda j, b: (b, 0)), pl.BlockSpec((cw, d), lambda j, b: (j, 0)),
                  pl.BlockSpec((2, seq, cw), lambda j, b: (0, b, j)),
                  pl.BlockSpec((3, cw), lambda j, b: (0, j)), pl.BlockSpec((1, cw), lambda j, b: (0, j))],
        out_specs=[pl.BlockSpec((2, seq, cw), lambda j, b: (0, b, j)),
                   pl.BlockSpec((3, cw), lambda j, b: (0, j)), pl.BlockSpec((1, cw), lambda j, b: (0, j))],
        scratch_shapes=[pltpu.VMEM((seq, cw), F32), pltpu.VMEM((seq, cw), F32)],
        compiler_params=_cparams(("arbitrary", "arbitrary"), VMEM_BIG), args=[dffn, w_down, u, conv_w, conv_b])


def _up_backward(du, w_up, x1, mod3, g_ffn, dx2, mix, seq, rider=None):
    _, t, _ = du.shape
    d = x1.shape[1]
    tm = TOKEN_TILE
    per_seq = seq // tm
    batch = t // seq
    w_a, w_b = w_up
    half, wcol = w_a.shape[1], w_a.shape[2]

    def body(du_ref, wa_ref, wb_ref, x1_ref, mod_ref, g_ref, dx2_ref, mix_ref,
             dx1_ref, dmix_ref, dsh_ref, dsc_ref, dga_ref, dg_ref):
        i = pl.program_id(0)
        parts = []
        for w_ref in (wa_ref, wb_ref):
            acc = jnp.zeros((tm, half), F32)
            for j in range(N_SHARD):
                acc = acc + _mm_nt(du_ref[j // 2, :, (j % 2) * wcol:(j % 2 + 1) * wcol], w_ref[j])
            parts.append(acc)
        dh = jnp.concatenate(parts, axis=1)
        gate_a = mod_ref[0, :, 2 * d:3 * d]
        scale_f = mod_ref[0, :, 4 * d:5 * d]
        r, xn = _rms_stats(x1_ref[...])
        xg = xn * g_ref[...]
        dxg = dh * (1.0 + scale_f)
        dx1 = dx2_ref[...] + _rms_bwd(dxg * g_ref[...], xn, r)
        dx1_ref[...] = dx1
        dmix_ref[...] = (dx1 * gate_a).astype(BF16)

        @pl.when(i == 0)
        def _():
            dg_ref[...] = jnp.zeros_like(dg_ref)

        @pl.when(i % per_seq == 0)
        def _():
            dsh_ref[...] = jnp.zeros_like(dsh_ref)
            dsc_ref[...] = jnp.zeros_like(dsc_ref)
            dga_ref[...] = jnp.zeros_like(dga_ref)

        dg_ref[...] += jnp.sum(dxg * xn, axis=0, keepdims=True)
        dsh_ref[0] += jnp.sum(dh, axis=0, keepdims=True)
        dsc_ref[0] += jnp.sum(dh * xg, axis=0, keepdims=True)
        dga_ref[0] += jnp.sum(dx1 * mix_ref[...], axis=0, keepdims=True)

    tile = lambda w: pl.BlockSpec((tm, w), lambda i: (i, 0))
    per_b = pl.BlockSpec((1, 1, d), lambda i: (i // per_seq, 0, 0))
    small = jax.ShapeDtypeStruct((batch, 1, d), F32)
    return _hosted(
        body, rider, name="up_backward", grid=(t // tm,),
        out_shape=[jax.ShapeDtypeStruct((t, d), F32), jax.ShapeDtypeStruct((t, d), BF16), small, small, small,
                   jax.ShapeDtypeStruct((1, d), F32)],
        in_specs=[pl.BlockSpec((2, tm, D_FF), lambda i: (0, i, 0)),
                  _resident((N_SHARD, half, wcol)), _resident((N_SHARD, half, wcol)), tile(d),
                  pl.BlockSpec((1, 1, 6 * d), lambda i: (i // per_seq, 0, 0)),
                  pl.BlockSpec((1, d), lambda i: (0, 0)), tile(d), tile(d)],
        out_specs=[tile(d), tile(d), per_b, per_b, per_b, pl.BlockSpec((1, d), lambda i: (0, 0))],
        scratch_shapes=[], compiler_params=_cparams(("arbitrary",), VMEM_BIG),
        args=[du, w_a, w_b, x1, mod3, g_ffn, dx2, mix])


def _up_weight_grad(h2, du, rider=None):
    t, d = h2.shape
    tk = 2 * TOKEN_TILE
    wcol = D_FF // 2
    half = d // 2
    n_k = t // tk

    def body(h_ref, du_ref, ga_ref, gb_ref, ga16_ref, gb16_ref):
        k = pl.program_id(1)

        @pl.when(k == 0)
        def _():
            ga_ref[...] = jnp.zeros_like(ga_ref)
            gb_ref[...] = jnp.zeros_like(gb_ref)

        du = du_ref[0]
        ga_ref[0] += _mm_tn(h_ref[:, :half], du)
        gb_ref[0] += _mm_tn(h_ref[:, half:], du)

        @pl.when(k == n_k - 1)
        def _():
            ga16_ref[...] = ga_ref[...].astype(BF16)
            gb16_ref[...] = gb_ref[...].astype(BF16)

    g_spec = pl.BlockSpec((1, half, wcol), lambda j, k: (j, 0, 0))
    f32_out = jax.ShapeDtypeStruct((N_SHARD, half, wcol), F32)
    b16_out = jax.ShapeDtypeStruct((N_SHARD, half, wcol), BF16)
    return _hosted(
        body, rider, name="up_weight_grad", grid=(N_SHARD, n_k),
        out_shape=[f32_out, f32_out, b16_out, b16_out],
        in_specs=[pl.BlockSpec((tk, d), lambda j, k: (k, 0)),
                  pl.BlockSpec((1, tk, wcol), lambda j, k: (j // 2, k, j % 2))],
        out_specs=[g_spec, g_spec, g_spec, g_spec], scratch_shapes=[],
        compiler_params=_cparams(("arbitrary", "arbitrary"), VMEM_BIG), args=[h2, du])


def _out_backward(dmix, w_out, oab, oa, ob, g_na, g_sw):
    t, d = dmix.shape
    tm = 2 * TOKEN_TILE
    hw = NA_WIDTH

    def body(dm_ref, w_ref, oab_ref, oa_ref, ob_ref, gna_ref, gsw_ref,
             doa_ref, dob_ref, gw_ref, gwb_ref, dgna_ref, dgsw_ref):
        @pl.when(pl.program_id(0) == 0)
        def _():
            gw_ref[...] = jnp.zeros_like(gw_ref)
            dgna_ref[...] = jnp.zeros_like(dgna_ref)
            dgsw_ref[...] = jnp.zeros_like(dgsw_ref)

        dm = dm_ref[...]
        gw_ref[...] += _mm_tn(oab_ref[...], dm)

        @pl.when(pl.program_id(0) == t // tm - 1)
        def _():
            gwb_ref[...] = gw_ref[...].astype(BF16)

        do = _mm_nt(dm, w_ref[...])
        for raw_ref, g_ref, dst_ref, dg_ref, lo in ((oa_ref, gna_ref, doa_ref, dgna_ref, 0),
                                                     (ob_ref, gsw_ref, dob_ref, dgsw_ref, hw)):
            r, xn = _rms_stats(raw_ref[...])
            dpart = do[:, lo:lo + hw]
            dg_ref[...] += jnp.sum(dpart * xn, axis=0, keepdims=True)
            dst_ref[...] = _rms_bwd(dpart * g_ref[...], xn, r).astype(BF16)

    tile = lambda w: pl.BlockSpec((tm, w), lambda i: (i, 0))
    vec = lambda w: pl.BlockSpec((1, w), lambda i: (0, 0))
    return pl.pallas_call(
        body, name="out_backward", grid=(t // tm,),
        out_shape=(jax.ShapeDtypeStruct((t, hw), BF16), jax.ShapeDtypeStruct((t, hw), BF16),
                   jax.ShapeDtypeStruct((d, d), F32), jax.ShapeDtypeStruct((d, d), BF16),
                   jax.ShapeDtypeStruct((1, hw), F32), jax.ShapeDtypeStruct((1, hw), F32)),
        in_specs=[tile(d), pl.BlockSpec((d, d), lambda i: (0, 0)), tile(d), tile(hw), tile(hw), vec(hw), vec(hw)],
        out_specs=(tile(hw), tile(hw), pl.BlockSpec((d, d), lambda i: (0, 0)), pl.BlockSpec((d, d), lambda i: (0, 0)),
                   vec(hw), vec(hw)),
        compiler_params=_cparams(("arbitrary",), VMEM_BIG),
    )(dmix, w_out, oab, oa, ob, g_na, g_sw)


def _na_backward(proj, d_o, lse, tiles, batch, seq, rider=None):
    t = proj.shape[0]
    n_rows = seq // GRID_W
    n_pairs = NA_WIDTH // LANES
    win = NA_ROWS * GRID_W
    n_tiles = 2 * NA_ROWS - 2

    def body(q_ref, k_ref, v_ref, do_ref, lse_ref, tp_ref, dq_ref, dk_ref, dv_ref, dtp_ref, km, vm, dk_acc, dv_acc):
        @pl.when(pl.program_id(1) == 0)
        def _():
            dtp_ref[...] = jnp.zeros_like(dtp_ref)

        _na_prepare(k_ref, v_ref, km, vm)
        dk_acc[...] = jnp.zeros_like(dk_acc)
        dv_acc[...] = jnp.zeros_like(dv_acc)
        low = lax.broadcasted_iota(jnp.int32, (win, LANES), 1) < HEAD_DIM

        def scores(r):
            rs, off = _na_window(r, n_rows)
            rows = pl.ds(pl.multiple_of(r * GRID_W, GRID_W), GRID_W)
            wrows = pl.ds(pl.multiple_of(rs * GRID_W, GRID_W), win)
            q, do = q_ref[rows, :], do_ref[rows, :]
            k2 = _na_pair_window(km, wrows)
            s = _na_scores(q, k2, tp_ref, off)
            dp = _mm_nt(do, _na_pair_window(vm, wrows))
            return rows, wrows, off, q, do, k2, s, dp

        def finish(rows, wrows, off, q, do, k2, s, dp):
            p = _pair_probs_from_lse(s, lse_ref[rows, :])
            parts = []
            for h in range(2):
                ph, dph = p[:, h * win:(h + 1) * win], dp[:, h * win:(h + 1) * win]
                dsh = ph * (dph - jnp.sum(ph * dph, axis=-1, keepdims=True))
                for w in range(NA_ROWS // 2):
                    dtp_ref[h, 2 * w - off + (NA_ROWS - 1)] += dsh[:, w * LANES:(w + 1) * LANES]
                parts.append(dsh)
            dsb = (jnp.concatenate(parts, axis=1) * QK_SCALE).astype(BF16)
            dq_ref[rows, :] = _mm(dsb, k2).astype(BF16)
            dk_acc[wrows, :] += _pair_grad(dsb, q, low)
            dv_acc[wrows, :] += _pair_grad(p.astype(BF16), do, low)

        def row_group(i, carry):
            for state in [scores(NA_GROUP * i + j) for j in range(NA_GROUP)]:
                finish(*state)
            return carry

        lax.fori_loop(0, n_rows // NA_GROUP, row_group, 0)
        dk_ref[...] = dk_acc[...].astype(BF16)
        dv_ref[...] = dv_acc[...].astype(BF16)

    blk = lambda off: pl.BlockSpec((seq, LANES), lambda p, b: (b, off + p))
    out = jax.ShapeDtypeStruct((t, NA_WIDTH), BF16)
    return _hosted(
        body, rider, name="na_backward", grid=(n_pairs, batch),
        out_shape=[out, out, out, jax.ShapeDtypeStruct(tiles.shape, F32)],
        in_specs=[blk(0), blk(n_pairs), blk(2 * n_pairs), blk(0), blk(0),
                  pl.BlockSpec((2, n_tiles, GRID_W, LANES), lambda p, b: (p, 0, 0, 0))],
        out_specs=[blk(0), blk(0), blk(0), pl.BlockSpec((2, n_tiles, GRID_W, LANES), lambda p, b: (p, 0, 0, 0))],
        scratch_shapes=[pltpu.VMEM((2, seq, LANES), BF16), pltpu.VMEM((2, seq, LANES), BF16),
                        pltpu.VMEM((seq, LANES), F32), pltpu.VMEM((seq, LANES), F32)],
        compiler_params=_cparams(("arbitrary", "arbitrary")), args=[proj, proj, proj, d_o, lse, tiles])


def _na_bias_grad(dtiles, expand):
    n = dtiles.shape[0]

    def body(t_ref, e_ref, o_ref):
        flat = jnp.concatenate([t_ref[:, qq, :] for qq in range(GRID_W)], axis=1)
        o_ref[...] = lax.dot_general(flat, e_ref[...], (((1,), (1,)), ((), ())),
                                     precision=lax.Precision.HIGHEST, preferred_element_type=F32)

    return body, [dtiles, expand], jax.ShapeDtypeStruct((n, expand.shape[0]), F32)


def _sw_backward(proj, d_o, lse, sink, batch, seq, rider=None):
    t = proj.shape[0]
    n_pairs = SW_WIDTH // LANES
    q_blk = 3 * NA_WIDTH // LANES
    k_blk = q_blk + n_pairs
    n_blocks = seq // SW_BLOCK
    pad = seq + 2 * SW_BLOCK

    def body(sink_ref, q_ref, k_ref, v_ref, do_ref, lse_ref, dq_ref, dk_ref, dv_ref, dsk_ref,
             k_lo, k_hi, v_lo, v_hi, dk_loc, dv_loc, dk_tot, dv_tot):
        hp = pl.program_id(1)
        g = hp // 2

        @pl.when(hp % 2 == 0)
        def _():
            _sw_prepare(k_ref, g, k_lo, k_hi, seq)
            _sw_prepare(v_ref, g, v_lo, v_hi, seq)
            dk_loc[...] = jnp.zeros_like(dk_loc)
            dv_loc[...] = jnp.zeros_like(dv_loc)

        @pl.when(hp == 0)
        def _():
            dk_tot[...] = jnp.zeros_like(dk_tot)
            dv_tot[...] = jnp.zeros_like(dv_tot)

        band = 3 * SW_BLOCK
        low = lax.broadcasted_iota(jnp.int32, (band, LANES), 1) < HEAD_DIM

        sinks = (sink_ref[2 * hp], sink_ref[2 * hp + 1])

        def scores(n):
            rows = pl.ds(pl.multiple_of(n * SW_BLOCK, SW_BLOCK), SW_BLOCK)
            wrows = pl.ds(pl.multiple_of(n * SW_BLOCK, SW_BLOCK), band)
            qb, do = q_ref[rows, :], do_ref[rows, :]
            k2 = jnp.concatenate([k_lo[wrows, :], k_hi[wrows, :]], axis=0)
            v2 = jnp.concatenate([v_lo[wrows, :], v_hi[wrows, :]], axis=0)
            return n, rows, wrows, qb, do, k2, _mm_nt(qb, k2) * QK_SCALE, _mm_nt(do, v2)

        def finish(sink_acc, n, rows, wrows, qb, do, k2, s2, dp):
            p, ps = _sw_probs_from_lse(s2, _sw_mask(n, seq), sinks, lse_ref[rows, :])
            parts, new = [], []
            for i in range(2):
                ph, dph = p[:, i * band:(i + 1) * band], dp[:, i * band:(i + 1) * band]
                delta = jnp.sum(ph * dph, axis=-1, keepdims=True)
                parts.append(ph * (dph - delta))
                new.append(sink_acc[i] - ps[i] * delta)
            dsb = (jnp.concatenate(parts, axis=1) * QK_SCALE).astype(BF16)
            dq_ref[rows, :] = _mm(dsb, k2)
            dk_loc[wrows, :] += _pair_grad(dsb, qb, low)
            dv_loc[wrows, :] += _pair_grad(p.astype(BF16), do, low)
            return tuple(new)

        def block_group(i, carry):
            for state in [scores(SW_GROUP_BLOCKS * i + j) for j in range(SW_GROUP_BLOCKS)]:
                carry = finish(carry, *state)
            return carry

        zero = jnp.zeros((SW_BLOCK, 1), F32)
        s0, s1 = lax.fori_loop(0, n_blocks // SW_GROUP_BLOCKS, block_group, (zero, zero))
        row = lax.broadcasted_iota(jnp.int32, (SUBLANES, LANES), 0)
        dsk_ref[0, 0] = jnp.where(row == 0, jnp.sum(s0), jnp.where(row == 1, jnp.sum(s1), 0.0))

        @pl.when(hp % 2 == 1)
        def _():
            lane_s = lax.broadcasted_iota(jnp.int32, (seq, LANES), 1)
            mine_g = (lane_s // HEAD_DIM) == g
            for loc, tot in ((dk_loc, dk_tot), (dv_loc, dv_tot)):
                part = loc[SW_BLOCK:SW_BLOCK + seq, :]
                tot[...] += jnp.where(mine_g, part + pltpu.roll(part, HEAD_DIM, 1), 0.0)

        @pl.when(hp == n_pairs - 1)
        def _():
            dk_ref[...] = dk_tot[...]
            dv_ref[...] = dv_tot[...].astype(BF16)

    return _hosted(
        body, rider, name="sw_backward", grid=(batch, n_pairs),
        out_shape=[jax.ShapeDtypeStruct((t, SW_WIDTH), F32), jax.ShapeDtypeStruct((t, LANES), F32),
                   jax.ShapeDtypeStruct((t, LANES), BF16), jax.ShapeDtypeStruct((batch, n_pairs, SUBLANES, LANES), F32)],
        in_specs=[pl.BlockSpec(memory_space=pltpu.SMEM),
                  pl.BlockSpec((seq, LANES), lambda b, p: (b, q_blk + p)),
                  pl.BlockSpec((seq, LANES), lambda b, p: (b, k_blk)),
                  pl.BlockSpec((seq, LANES), lambda b, p: (b, k_blk + 1)),
                  pl.BlockSpec((seq, LANES), lambda b, p: (b, p)), pl.BlockSpec((seq, LANES), lambda b, p: (b, p))],
        out_specs=[pl.BlockSpec((seq, LANES), lambda b, p: (b, p)), pl.BlockSpec((seq, LANES), lambda b, p: (b, 0)),
                   pl.BlockSpec((seq, LANES), lambda b, p: (b, 0)),
                   pl.BlockSpec((1, 1, SUBLANES, LANES), lambda b, p: (b, p, 0, 0))],
        scratch_shapes=[pltpu.VMEM((pad, LANES), BF16)] * 4 + [pltpu.VMEM((pad, LANES), F32)] * 2
        + [pltpu.VMEM((seq, LANES), F32)] * 2,
        compiler_params=_cparams(("arbitrary", "arbitrary")), args=[sink, proj, proj, proj, d_o, lse])


def _in_backward(dqkv_a, dq_b, dk_b, dv_b, w_in_t, h1, x, mod3, g_attn, dx1, cos_t, sin_t, seq):
    t, d = x.shape
    tm = TOKEN_TILE
    per_seq = seq // tm
    batch = t // seq
    dqa, dka, dva = dqkv_a
    n_q = SW_WIDTH // LANES

    def body(dqa_ref, dka_ref, dva_ref, dqb_ref, dkb_ref, dvb_ref, w_ref, h_ref, x_ref, mod_ref, g_ref, dx1_ref,
             cos_ref, sin_ref, dx_ref, gw_ref, gwb_ref, dsh_ref, dsc_ref, dg_ref):
        i = pl.program_id(0)

        @pl.when(i == 0)
        def _():
            gw_ref[...] = jnp.zeros_like(gw_ref)
            dg_ref[...] = jnp.zeros_like(dg_ref)

        @pl.when(i % per_seq == 0)
        def _():
            dsh_ref[...] = jnp.zeros_like(dsh_ref)
            dsc_ref[...] = jnp.zeros_like(dsc_ref)

        dr = jnp.concatenate([dqb_ref[...], dkb_ref[...]], axis=1)
        cos = jnp.concatenate([cos_ref[...]] * (n_q + 1), axis=1)
        sin = jnp.concatenate([sin_ref[...]] * (n_q + 1), axis=1)
        dr = dr * cos + _rope_rot(dr * sin)
        dproj = jnp.concatenate([dqa_ref[...], dka_ref[...], dva_ref[...], dr.astype(BF16), dvb_ref[...]], axis=1)
        gw_ref[...] += _mm_tn(dproj, h_ref[...])

        @pl.when(i == t // tm - 1)
        def _():
            gwb_ref[...] = gw_ref[...].astype(BF16)

        dh = _mm(dproj, w_ref[...])
        scale = mod_ref[0, :, d:2 * d]
        r, xn = _rms_stats(x_ref[...])
        xg = xn * g_ref[...]
        dxg = dh * (1.0 + scale)
        dx_ref[...] = dx1_ref[...] + _rms_bwd(dxg * g_ref[...], xn, r)
        dg_ref[...] += jnp.sum(dxg * xn, axis=0, keepdims=True)
        dsh_ref[0] += jnp.sum(dh, axis=0, keepdims=True)
        dsc_ref[0] += jnp.sum(dh * xg, axis=0, keepdims=True)

    tile = lambda w: pl.BlockSpec((tm, w), lambda i: (i, 0))
    per_b = pl.BlockSpec((1, 1, d), lambda i: (i // per_seq, 0, 0))
    small = jax.ShapeDtypeStruct((batch, 1, d), F32)
    rope = pl.BlockSpec((tm, LANES), lambda i: (i % per_seq, 0))
    return pl.pallas_call(
        body, name="in_backward", grid=(t // tm,),
        out_shape=(jax.ShapeDtypeStruct((t, d), F32), jax.ShapeDtypeStruct((IN_WIDTH, d), F32),
                   jax.ShapeDtypeStruct((IN_WIDTH, d), BF16), small, small, jax.ShapeDtypeStruct((1, d), F32)),
        in_specs=[tile(NA_WIDTH), tile(NA_WIDTH), tile(NA_WIDTH), tile(SW_WIDTH), tile(LANES), tile(LANES),
                  _resident((IN_WIDTH, d)), tile(d), tile(d),
                  pl.BlockSpec((1, 1, 6 * d), lambda i: (i // per_seq, 0, 0)),
                  pl.BlockSpec((1, d), lambda i: (0, 0)), tile(d), rope, rope],
        out_specs=(tile(d), _resident((IN_WIDTH, d)), _resident((IN_WIDTH, d)),
                   per_b, per_b, pl.BlockSpec((1, d), lambda i: (0, 0))),
        compiler_params=_cparams(("arbitrary",), VMEM_BIG),
    )(dqa, dka, dva, dq_b, dk_b, dv_b, w_in_t, h1, x, mod3, g_attn, dx1, cos_t, sin_t)


def _ada_weight_grad(sc_all, dmod_cols):
    d = sc_all.shape[1]
    ncol = dmod_cols.shape[1]

    def body(s_ref, m_ref, o_ref):
        o_ref[...] = _mm_tn(s_ref[...].astype(BF16), m_ref[...].astype(BF16))

    return pl.pallas_call(
        body, name="ada_weight_grad",
        out_shape=jax.ShapeDtypeStruct((d, ncol), F32),
        compiler_params=_cparams(vmem=VMEM_BIG),
    )(sc_all, dmod_cols)


def _row_tile(rows, cols):
    target = max(SUBLANES, (1 << 20) // (4 * cols))
    best = rows
    for cand in range(SUBLANES, rows + 1, SUBLANES):
        if rows % cand == 0 and cand <= target:
            best = cand
    return best if rows % SUBLANES == 0 else rows


def _sum_slots(results, name, rider=None, passenger=None):
    extra_body, extra_in, extra_out = passenger if passenger is not None else (None, [], None)
    n_extra = len(extra_in)
    parts = [group for groups in results for group in groups]
    result_of = [k for k, groups in enumerate(results) for _ in groups]
    n_parts = len(parts)
    cols = [results[k][0][1].shape[1] for k in result_of]
    tr = [_row_tile(min(own.shape[0] for _, own in results[k]), c) for k, c in zip(result_of, cols)]
    assert all(own.shape[0] % r == 0 and own.shape[1] == c for (_, own), r, c in zip(parts, tr, cols))
    tiles = [own.shape[0] // r for (_, own), r in zip(parts, tr)]
    first = [sum(tiles[:q]) for q in range(n_parts)]

    def body(*refs):
        o_refs = refs[2 * n_parts + n_extra:]
        step = pl.program_id(0)
        if passenger is not None:
            pl.when(step == 0)(lambda: extra_body(*refs[2 * n_parts:2 * n_parts + n_extra], o_refs[len(results)]))
        for q in range(n_parts):
            @pl.when((step >= first[q]) & (step < first[q] + tiles[q]))
            def _(q=q):
                p_ref, own_ref = refs[2 * q], refs[2 * q + 1]
                o_refs[result_of[q]][...] = (((own_ref[...] + p_ref[0].astype(F32)) + p_ref[1].astype(F32))
                                             + p_ref[2].astype(F32))

    def tile(start, count):
        return lambda i: jnp.clip(i - start, 0, count - 1)

    in_specs, args = [], []
    for q, (recv, own) in enumerate(parts):
        at = tile(first[q], tiles[q])
        in_specs.append(pl.BlockSpec((N_SHARD - 1, tr[q], cols[q]), lambda i, at=at: (0, at(i), 0)))
        in_specs.append(pl.BlockSpec((tr[q], cols[q]), lambda i, at=at: (at(i), 0)))
        args += [recv, own]
    out_shape, out_specs = [], []
    for k in range(len(results)):
        mine = [q for q in range(n_parts) if result_of[q] == k]
        count = sum(tiles[q] for q in mine)
        at = tile(first[mine[0]], count)
        out_shape.append(jax.ShapeDtypeStruct((count * tr[mine[0]], cols[mine[0]]), F32))
        out_specs.append(pl.BlockSpec((tr[mine[0]], cols[mine[0]]), lambda i, at=at: (at(i), 0)))
    whole = lambda a: pl.BlockSpec(a.shape, lambda i, nd=len(a.shape): (0,) * nd)
    in_specs += [whole(a) for a in extra_in]
    args += extra_in
    if passenger is not None:
        out_shape.append(extra_out)
        out_specs.append(whole(extra_out))
    return _hosted(body, rider, name=name, grid=(sum(tiles),), out_shape=out_shape, in_specs=in_specs,
                   out_specs=out_specs, scratch_shapes=[], compiler_params=_cparams(("arbitrary",), VMEM_BIG), args=args)


def _adamw_math(w, g, m, v):
    m2 = ADAM_B1 * m + (1.0 - ADAM_B1) * g
    v2 = ADAM_B2 * v + (1.0 - ADAM_B2) * (g * g)
    m_hat = m2 / (1.0 - ADAM_B1 ** ADAM_STEP)
    v_hat = v2 / (1.0 - ADAM_B2 ** ADAM_STEP)
    return -ADAM_LR * (m_hat / (jnp.sqrt(v_hat) + ADAM_EPS) + ADAM_WD * w), m2, v2


def _small_sums(partials, dmod, rider=None):
    moving = list(partials) + [dmod]
    n_mov = len(moving)

    def body(*refs):
        mov, refs = refs[:n_mov], refs[n_mov:]
        sums_out, refs = refs[:n_mov - 1], refs[n_mov - 1:]
        b_out, dmod_out, refs = refs[0], refs[1], refs[2:]
        everyone, (ssem, rsem) = refs[:n_mov], refs[n_mov:]
        x, y, c = _my_pos()
        me = 4 * x + 2 * y + c
        cps = []
        for a in range(n_mov):
            everyone[a][me] = mov[a][...]
            for k in range(1, N_DEV):
                peer = (_flip(x, (k >> 2) & 1), _flip(y, (k >> 1) & 1), _flip(c, k & 1))
                cps.append(pltpu.make_async_remote_copy(
                    src_ref=everyone[a].at[me], dst_ref=everyone[a].at[me], send_sem=ssem.at[a, k - 1],
                    recv_sem=rsem.at[a, k - 1], device_id=peer, device_id_type=MESH))
        for cp in cps:
            cp.start()
        for cp in cps:
            cp.wait_recv()

        def total(a):
            acc = everyone[a][0]
            for dev in range(1, N_DEV):
                acc = acc + everyone[a][dev]
            return acc

        for a in range(n_mov - 1):
            sums_out[a][...] = total(a)
        b_out[...] = jnp.sum(total(n_mov - 1), axis=0, keepdims=True)
        dmod_out[...] = everyone[n_mov - 1][...]
        for cp in cps:
            cp.wait_send()

    vm = pl.BlockSpec(memory_space=pltpu.VMEM)
    sds = jax.ShapeDtypeStruct
    out_shape = [sds(p.shape, F32) for p in partials]
    out_shape += [sds((1, dmod.shape[1]), F32), sds((N_DEV,) + dmod.shape, F32)]
    return _hosted(
        body, rider, name="small_sums", grid=(), out_shape=out_shape,
        in_specs=[vm] * n_mov, out_specs=[vm] * len(out_shape),
        scratch_shapes=[pltpu.VMEM((N_DEV,) + a.shape, F32) for a in moving]
        + [pltpu.SemaphoreType.DMA((n_mov, N_DEV - 1)), pltpu.SemaphoreType.DMA((n_mov, N_DEV - 1))],
        compiler_params=_cparams(vmem=VMEM_BIG), args=moving)


def _small_adamw(states, grads):
    n = len(states)

    def body(*refs):
        g_refs, wmv, res = refs[:n], refs[n:4 * n], refs[4 * n:]
        for j in range(n):
            g = g_refs[j][...]
            delta, m2, v2 = _adamw_math(wmv[3 * j][...], g, wmv[3 * j + 1][...], wmv[3 * j + 2][...])
            res[4 * j][...] = g
            res[4 * j + 1][...] = delta
            res[4 * j + 2][...] = m2
            res[4 * j + 3][...] = v2

    out_shape = []
    for w, _, _ in states:
        out_shape += [jax.ShapeDtypeStruct(w.shape, F32)] * 4
    outs = pl.pallas_call(body, name="small_adamw", out_shape=tuple(out_shape),
                          compiler_params=_cparams(vmem=VMEM_BIG))(*grads, *[a for st in states for a in st])
    return [outs[4 * j:4 * j + 4] for j in range(n)]


def _adamw(w, grads, m, v, name):
    rows, cols = w.shape
    tr = _row_tile(rows, cols)
    ng = len(grads)

    def body(*refs):
        w_ref = refs[0]
        g_refs = refs[1:1 + ng]
        m_ref, v_ref = refs[1 + ng], refs[2 + ng]
        g_out, d_out, m_out, v_out = refs[3 + ng:]
        g = g_refs[0][...]
        for extra in g_refs[1:]:
            g = g + extra[...]
        g_out[...] = g
        d_out[...], m_out[...], v_out[...] = _adamw_math(w_ref[...], g, m_ref[...], v_ref[...])

    spec = pl.BlockSpec((tr, cols), lambda i: (i, 0))
    out = jax.ShapeDtypeStruct((rows, cols), F32)
    return pl.pallas_call(
        body, name=name, grid=(rows // tr,),
        out_shape=(out, out, out, out),
        in_specs=[spec] * (3 + ng), out_specs=(spec, spec, spec, spec),
        compiler_params=_cparams(("arbitrary",)),
    )(w, *grads, m, v)


def _rope_tables(seq):
    half = HEAD_DIM // 2
    inv = np.float32(ROPE_THETA) ** (-np.arange(half, dtype=np.float32) / np.float32(half))
    ang = (np.arange(seq, dtype=np.float32)[:, None] * inv[None, :]).astype(np.float64)
    cos, sin = np.cos(ang).astype(np.float32), np.sin(ang).astype(np.float32)
    cos_t = np.concatenate([cos, cos, cos, cos], axis=1)
    sin_t = np.concatenate([-sin, sin, -sin, sin], axis=1)
    return jnp.asarray(cos_t), jnp.asarray(sin_t)


def kernel(x, c, w_ada, b_ada, g_attn, w_in, na_rpb, sw_sink, g_na_out, g_sw_out, w_out, g_ffn, w_up, conv_w, conv_b, w_down, g_final, loss_target, m_w_ada, m_b_ada, m_g_attn, m_w_in, m_na_rpb, m_sw_sink, m_g_na_out, m_g_sw_out, m_w_out, m_g_ffn, m_w_up, m_conv_w, m_conv_b, m_w_down, m_g_final, v_w_ada, v_b_ada, v_g_attn, v_w_in, v_na_rpb, v_sw_sink, v_g_na_out, v_g_sw_out, v_w_out, v_g_ffn, v_w_up, v_conv_w, v_conv_b, v_w_down, v_g_final):
    batch, seq, d = x.shape
    t = batch * seq
    assert d == D_MODEL and seq % (NA_ROWS * GRID_W) == 0 and seq % TOKEN_TILE == 0 and batch <= SUBLANES
    shard = 2 * lax.axis_index("x") + lax.axis_index("y")
    xt = x.reshape(t, d)
    tgt = loss_target.reshape(t, d)

    c8 = jnp.pad(c, ((0, SUBLANES - batch), (0, 0)))
    w_in_t_s = jnp.transpose(w_in[0]).astype(BF16)
    n_heads = NA_WIDTH // HEAD_DIM
    n_tiles, n_dc = 2 * NA_ROWS - 2, 2 * NA_COLS - 1
    expand, neg_mask = _na_bias_pattern()
    rpb = na_rpb[0]
    rows2 = jnp.concatenate([rpb[:, :-1, :], rpb[:, 1:, :]], axis=2).reshape(n_heads * n_tiles, 2 * n_dc)
    rows2 = jnp.pad(rows2, ((0, 0), (0, GRID_W - 2 * n_dc)))
    (mod8, sc_all, tiles), (w_in_g,) = _ada_forward(
        c8, w_ada[0], b_ada, _Rider("gather", [w_in_t_s]), _na_bias_tiles(rows2, expand, neg_mask))
    tiles = tiles.reshape(n_heads, n_tiles, GRID_W, LANES)
    mod3 = mod8[:batch].reshape(batch, 1, 6 * d)
    w_in_t = w_in_g.reshape(IN_WIDTH, d)

    cos_t, sin_t = _rope_tables(seq)
    (h1, proj), _ = _in_proj(xt, mod3, g_attn, w_in_t, cos_t, sin_t, seq)
    sink = sw_sink[0]
    w_up_b16 = w_up[0].astype(BF16)
    (oa, lse_a), (w_up_a,) = _na_forward(proj, tiles, batch, seq, _Rider("gather", [w_up_b16[:d // 2]]))
    (ob, lse_b), (w_up_b, conv_w_g, w_out_g) = _sw_forward(
        proj, sink, batch, seq, _Rider("gather", [w_up_b16[d // 2:], conv_w[0], w_out[0].astype(BF16)]))
    w_up_f = (w_up_a, w_up_b)
    w_out_f = w_out_g.reshape(d, d)
    conv_w_f = jnp.transpose(conv_w_g, (1, 0, 2)).reshape(3, D_FF)
    oab, mix, x1, h2 = _out_proj(oa, ob, g_na_out, g_sw_out, w_out_f, xt, mod3, g_ffn, seq)
    (u,), _ = _up_proj(h2, w_up_f)
    (a,), (w_down_g,) = _conv_gate(u, conv_w_f, conv_b, batch, seq, _Rider("gather", [w_down[0].astype(BF16)]))
    w_down_f = w_down_g.reshape(D_FF, d)
    dx2, dffn, loss_part, dgate_f, dg_final = _down_and_loss(a, w_down_f, x1, mod3, g_final.reshape(1, d), tgt, seq)

    gw_down, gw_down_b = _down_weight_grad(a, dffn)
    blocks = lambda g, rows: g.reshape(N_SHARD, rows // N_SHARD, d)
    (du, gconv_w, gconv_b), (recv_down, own_down) = _ffn_backward(
        dffn, w_down_f, u, conv_w_f, conv_b, batch, seq,
        _Rider("scatter", [blocks(gw_down_b, D_FF)], [blocks(gw_down, D_FF)]))
    (gw_up_top, gw_up_bot, gw_up_top_b, gw_up_bot_b), _ = _up_weight_grad(h2, du)
    (dx1, dmix, dshift_f, dscale_f, dgate_a, dg_ffn), _ = _up_backward(du, w_up_f, x1, mod3, g_ffn, dx2, mix, seq)
    doa, dob, gw_out, gw_out_b, dg_na, dg_sw = _out_backward(dmix, w_out_f, oab, oa, ob, g_na_out, g_sw_out)
    (dqa, dka, dva, dtiles), (recv_out, recv_up_bot, own_out, own_up_bot) = _na_backward(
        proj, doa, lse_a, tiles, batch, seq,
        _Rider("scatter", [blocks(gw_out_b, d), gw_up_bot_b], [blocks(gw_out, d), gw_up_bot]))
    (dq_b, dk_b, dv_b, dsink_parts), (recv_up_top, own_up_top) = _sw_backward(
        proj, dob, lse_b, sink, batch, seq, _Rider("scatter", [gw_up_top_b], [gw_up_top]))
    gx, gw_in_t, gw_in_b, dshift_a, dscale_a, dg_attn = _in_backward(
        (dqa, dka, dva), dq_b, dk_b, dv_b, w_in_t, h1, xt, mod3, g_attn, dx1, cos_t, sin_t, seq)

    late, (recv_in, own_in) = _sum_slots(
        [[(recv_out, own_out)], [(recv_up_top, own_up_top), (recv_up_bot, own_up_bot)], [(recv_down, own_down)]],
        "sum_w_out_up_down", _Rider("scatter", [blocks(gw_in_b, IN_WIDTH)], [blocks(gw_in_t, IN_WIDTH)]),
        _na_bias_grad(dtiles.reshape(n_heads * n_tiles, GRID_W, LANES), expand))

    red = late.pop()[:, :2 * n_dc]
    red = red.reshape(n_heads, n_tiles, 2, n_dc)
    zero_row = jnp.zeros((n_heads, 1, n_dc), F32)
    g_rpb = (jnp.concatenate([red[:, :, 0, :], zero_row], axis=1)
             + jnp.concatenate([zero_row, red[:, :, 1, :]], axis=1))
    g_sink = jnp.sum(dsink_parts[:, :, :2, 0], axis=0).reshape(SW_WIDTH // HEAD_DIM)

    dmod = jnp.concatenate([dshift_a, dscale_a, dgate_a, dshift_f, dscale_f, dgate_f], axis=2).reshape(batch, 6 * d)
    rpb_shape = na_rpb.shape[1:]
    states = [(g_attn, m_g_attn, v_g_attn),
              (na_rpb.reshape(rpb_shape), m_na_rpb.reshape(rpb_shape), v_na_rpb.reshape(rpb_shape)),
              (sw_sink, m_sw_sink, v_sw_sink), (g_na_out, m_g_na_out, v_g_na_out), (g_sw_out, m_g_sw_out, v_g_sw_out),
              (g_ffn, m_g_ffn, v_g_ffn), (conv_b, m_conv_b, v_conv_b),
              (g_final.reshape(1, d), m_g_final.reshape(1, d), v_g_final.reshape(1, d))]
    partials = [dg_attn, g_rpb, g_sink.reshape(sw_sink.shape), dg_na, dg_sw, dg_ffn, gconv_b, dg_final,
                gconv_w, loss_part]
    mine = _sum_slots([[(recv_in, own_in)]], "sum_w_in")[0] + late
    small, theirs = _small_sums(partials, dmod, _Rider("swap", mine))
    g_conv_w_full, loss_sum, g_b_ada, dmod_all = small[len(states):]
    r_small = _small_adamw(states + [(b_ada, m_b_ada, v_b_ada)], small[:len(states)] + [g_b_ada])
    loss = loss_sum[0, 0]
    dmod_rows = jnp.pad(dmod_all, ((0, 0), (0, SUBLANES - batch), (0, 0))).reshape(N_DEV * SUBLANES, 6 * d)
    ncol = w_ada.shape[2]
    g_w_ada = _ada_weight_grad(sc_all, lax.dynamic_slice(dmod_rows, (0, shard * ncol), (N_DEV * SUBLANES, ncol)))
    cshard = conv_w.shape[2]
    g_conv_w = lax.dynamic_slice(g_conv_w_full, (0, shard * cshard), (3, cshard))

    def big(w, m, v, g_parts, name):
        shape = w.shape
        outs = _adamw(w[0], g_parts, m[0], v[0], name)
        return [o.reshape(shape) for o in outs]

    r_w_ada = big(w_ada, m_w_ada, v_w_ada, [g_w_ada], "adamw_w_ada")
    r_w_in = [jnp.transpose(o).reshape(w_in.shape) for o in
              _adamw(jnp.transpose(w_in[0]), [mine[0], theirs[0]], jnp.transpose(m_w_in[0]), jnp.transpose(v_w_in[0]),
                     "adamw_w_in")]
    r_w_out = big(w_out, m_w_out, v_w_out, [mine[1], theirs[1]], "adamw_w_out")
    r_w_up = big(w_up, m_w_up, v_w_up, [mine[2], theirs[2]], "adamw_w_up")
    r_w_down = big(w_down, m_w_down, v_w_down, [mine[3], theirs[3]], "adamw_w_down")

    r_conv_w = big(conv_w, m_conv_w, v_conv_w, [g_conv_w], "adamw_conv_w")

    def pick(k):
        ga_, rpb_, sk_, gna_, gsw_, gf_, cb_, gfin_, b_ = [r[k] for r in r_small]
        return [r_w_ada[k], b_, ga_, r_w_in[k], rpb_.reshape(na_rpb.shape), sk_, gna_, gsw_, r_w_out[k], gf_,
                r_w_up[k], r_conv_w[k], cb_, r_w_down[k], gfin_.reshape(d)]

    return (loss, gx.reshape(batch, seq, d), *pick(0), *pick(1), *pick(2), *pick(3))
```

```python
import jax
import jax.numpy as jnp
import numpy as np
from jax import lax
from jax.experimental import pallas as pl
from jax.experimental.pallas import tpu as pltpu

F32 = jnp.float32
BF16 = jnp.bfloat16
MESH = pl.DeviceIdType.MESH

D_MODEL = 1024
HEAD_DIM = 64
NA_WIDTH = 512
SW_WIDTH = 512
SW_KV_WIDTH = 128
IN_WIDTH = 2304
D_FF = 2816
GRID_W = 64
NA_ROWS = 8
NA_COLS = 16
SW_BLOCK = 128
ROPE_THETA = 10000.0
EPS = 1e-6
NEG = -1e30
QK_SCALE = HEAD_DIM ** -0.5

ADAM_LR = 0.001
ADAM_B1 = 0.9
ADAM_B2 = 0.999
ADAM_EPS = 1e-08
ADAM_WD = 0.01
ADAM_STEP = 10

N_SHARD = 4
N_DEV = 8
LANES = 128
SUBLANES = 8
TOKEN_TILE = 512
FF_TILE = 256
CONV_CHUNK = 512
NA_GROUP = 8
SW_GROUP_BLOCKS = 8
VMEM_BIG = 56 * 1024 * 1024


def _mm(a, b):
    return jnp.dot(a, b, preferred_element_type=F32)


def _mm_nt(a, b):
    return lax.dot_general(a, b, (((1,), (1,)), ((), ())), preferred_element_type=F32)


def _mm_tn(a, b):
    return lax.dot_general(a, b, (((0,), (0,)), ((), ())), preferred_element_type=F32)


def _cparams(sem=None, vmem=None):
    kw = {}
    if sem is not None:
        kw["dimension_semantics"] = sem
    if vmem is not None:
        kw["vmem_limit_bytes"] = vmem
    return pltpu.CompilerParams(**kw)


def _resident(shape):
    return pl.BlockSpec(shape, lambda i: (0,) * len(shape), pipeline_mode=pl.Buffered(1))


def _sigmoid(x):
    return 1.0 / (1.0 + jnp.exp(-x))


def _rms_stats(x):
    r = lax.rsqrt(jnp.mean(x * x, axis=-1, keepdims=True) + EPS)
    return r, x * r


def _rms_bwd(dxn, xn, r):
    return r * (dxn - xn * jnp.mean(dxn * xn, axis=-1, keepdims=True))


def _my_pos():
    return lax.axis_index("x"), lax.axis_index("y"), lax.axis_index("c")


def _flip(v, bit):
    return 1 - v if bit else v


def _ada_forward(c8, w_ada, b_ada, rider, passenger):
    d = c8.shape[1]
    ncol = w_ada.shape[1]
    extra_body, extra_in, extra_out = passenger
    n_extra = len(extra_in)

    def body(c_ref, w_ref, b_ref, *refs):
        extra_refs = refs[:n_extra]
        mod_ref, sc_ref, extra_ref, m_scr, mod_buf, ssem, rsem, ssem2, rsem2 = refs[n_extra:]
        x, y, c = _my_pos()
        me = 4 * x + 2 * y + c
        shard = 2 * x + y
        cv = c_ref[...]
        my_rows = pl.ds(pl.multiple_of(me * SUBLANES, SUBLANES), SUBLANES)
        sc_ref[my_rows, :] = cv * _sigmoid(cv)

        def copy1(k):
            peer = (_flip(x, (k >> 2) & 1), _flip(y, (k >> 1) & 1), _flip(c, k & 1))
            return pltpu.make_async_remote_copy(
                src_ref=sc_ref.at[my_rows, :], dst_ref=sc_ref.at[my_rows, :],
                send_sem=ssem.at[k - 1], recv_sem=rsem.at[k - 1], device_id=peer, device_id_type=MESH)

        sends = [copy1(k) for k in range(1, N_DEV)]
        for cp in sends:
            cp.start()
        extra_body(*extra_refs, extra_ref)
        for cp in sends:
            cp.wait_recv()
        m_scr[...] = _mm(sc_ref[...].astype(BF16), w_ref[...].astype(BF16))

        def copy2(k):
            px, py = _flip(x, (k >> 1) & 1), _flip(y, k & 1)
            rows = pl.ds(pl.multiple_of((4 * px + 2 * py + c) * SUBLANES, SUBLANES), SUBLANES)
            return pltpu.make_async_remote_copy(
                src_ref=m_scr.at[rows, :], dst_ref=mod_buf.at[shard],
                send_sem=ssem2.at[k - 1], recv_sem=rsem2.at[k - 1], device_id=(px, py, c), device_id_type=MESH)

        sends2 = [copy2(k) for k in range(1, N_SHARD)]
        for cp in sends2:
            cp.start()
        mod_buf[shard] = m_scr[my_rows, :]
        for cp in sends2:
            cp.wait_recv()
        for s in range(N_SHARD):
            mod_ref[:, s * ncol:(s + 1) * ncol] = mod_buf[s] + b_ref[:, s * ncol:(s + 1) * ncol]
        for cp in sends + sends2:
            cp.wait_send()

    vm = pl.BlockSpec(memory_space=pltpu.VMEM)
    return _hosted(
        body, rider, name="ada_forward", grid=(),
        out_shape=(jax.ShapeDtypeStruct((SUBLANES, N_SHARD * ncol), F32),
                   jax.ShapeDtypeStruct((N_DEV * SUBLANES, d), F32), extra_out),
        in_specs=[vm] * (3 + n_extra), out_specs=(vm, vm, vm),
        scratch_shapes=[pltpu.VMEM((N_DEV * SUBLANES, ncol), F32), pltpu.VMEM((N_SHARD, SUBLANES, ncol), F32),
                        pltpu.SemaphoreType.DMA((N_DEV - 1,)), pltpu.SemaphoreType.DMA((N_DEV - 1,)),
                        pltpu.SemaphoreType.DMA((N_SHARD - 1,)), pltpu.SemaphoreType.DMA((N_SHARD - 1,))],
        compiler_params=_cparams(vmem=VMEM_BIG), args=[c8, w_ada, b_ada] + extra_in)


class _Rider:
    def __init__(self, kind, srcs, owns=()):
        self.kind, self.srcs, self.owns = kind, list(srcs), list(owns)
        n = len(self.srcs)
        sds = jax.ShapeDtypeStruct
        dma = pltpu.SemaphoreType.DMA
        if kind == "gather":
            self.out_shapes = [sds((N_SHARD,) + s.shape, s.dtype) for s in self.srcs]
            self.sems = [dma((n, N_SHARD - 1)), dma((n, N_SHARD - 1)), dma((n, N_SHARD - 1)), dma((n, N_SHARD - 1)),
                         dma((n,)), dma((n,))]
        elif kind == "scatter":
            self.out_shapes = ([sds((N_SHARD - 1,) + s.shape[1:], s.dtype) for s in self.srcs]
                               + [sds(o.shape[1:], o.dtype) for o in self.owns])
            m = max(len(self.owns), 1)
            self.sems = [dma((n, N_SHARD - 1)), dma((n, N_SHARD - 1)), dma((m,)), dma((m,))]
        else:
            self.out_shapes = [sds(s.shape, s.dtype) for s in self.srcs]
            self.sems = [dma((n,)), dma((n,))]

    @property
    def inputs(self):
        return self.srcs + self.owns

    def _halved(self, i):
        a = self.srcs[i]
        tile_rows = SUBLANES * (4 // jnp.dtype(a.dtype).itemsize)
        return self.kind == "gather" and a.shape[0] % (2 * tile_rows) == 0

    def copies(self, ins, outs, sems):
        n = len(self.srcs)
        x, y, c = _my_pos()
        shard = 2 * x + y
        remote, relay = [], []
        if self.kind == "swap":
            ssem, rsem = sems
            for i in range(n):
                remote.append(pltpu.make_async_remote_copy(
                    src_ref=ins[i], dst_ref=outs[i], send_sem=ssem.at[i], recv_sem=rsem.at[i],
                    device_id=(x, y, 1 - c), device_id_type=MESH))
            return remote, relay
        if self.kind == "gather":
            ssem, rsem, ssem2, rsem2, sib_s, sib_r = sems
        else:
            ssem, rsem, sib_s, sib_r = sems
        for i in range(n):
            if self.kind == "gather":
                remote.append(pltpu.make_async_remote_copy(
                    src_ref=ins[i], dst_ref=outs[i].at[shard], send_sem=sib_s.at[i], recv_sem=sib_r.at[i],
                    device_id=(x, y, 1 - c), device_id_type=MESH))
                half = ins[i].shape[0] // 2
                mine = pl.ds(pl.multiple_of(c * half, half), half) if self._halved(i) else None
            for k in range(1, N_SHARD):
                px, py = _flip(x, (k >> 1) & 1), _flip(y, k & 1)
                if self.kind == "gather":
                    src, dst = ins[i], outs[i].at[shard]
                    if mine is not None:
                        src, dst = src.at[mine], dst.at[mine]
                        got = outs[i].at[2 * px + py].at[mine]
                        relay.append(pltpu.make_async_remote_copy(
                            src_ref=got, dst_ref=got, send_sem=ssem2.at[i, k - 1], recv_sem=rsem2.at[i, k - 1],
                            device_id=(x, y, 1 - c), device_id_type=MESH))
                else:
                    src, dst = ins[i].at[2 * px + py], outs[i].at[k - 1]
                remote.append(pltpu.make_async_remote_copy(
                    src_ref=src, dst_ref=dst, send_sem=ssem.at[i, k - 1], recv_sem=rsem.at[i, k - 1],
                    device_id=(px, py, c), device_id_type=MESH))
        if self.kind == "scatter":
            for i in range(len(self.owns)):
                remote.append(pltpu.make_async_remote_copy(
                    src_ref=ins[n + i].at[shard], dst_ref=outs[n + i], send_sem=sib_s.at[i], recv_sem=sib_r.at[i],
                    device_id=(x, y, 1 - c), device_id_type=MESH))
        return remote, relay

    def start(self, ins, outs, sems):
        remote, _ = self.copies(ins, outs, sems)
        for cp in remote:
            cp.start()

    def wait(self, ins, outs, sems):
        remote, relay = self.copies(ins, outs, sems)
        for cp in remote:
            cp.wait_recv()
        for cp in relay:
            cp.start()
        for cp in relay:
            cp.wait_recv()
        for cp in remote + relay:
            cp.wait_send()


def _hosted(body, rider, *, name, grid, out_shape, in_specs, out_specs, scratch_shapes, compiler_params, args):
    out_shape, out_specs = list(out_shape), list(out_specs)
    if rider is None:
        outs = pl.pallas_call(body, name=name, grid=grid, out_shape=tuple(out_shape), in_specs=list(in_specs),
                              out_specs=tuple(out_specs), scratch_shapes=list(scratch_shapes),
                              compiler_params=compiler_params)(*args)
        return list(outs), []
    n_in, n_out, n_scr = len(in_specs), len(out_shape), len(scratch_shapes)
    nr_in, nr_out = len(rider.inputs), len(rider.out_shapes)
    n_steps = 1
    for size in grid:
        n_steps *= size

    def full(*refs):
        ins, refs = refs[:n_in], refs[n_in:]
        r_in, refs = refs[:nr_in], refs[nr_in:]
        outs, refs = refs[:n_out], refs[n_out:]
        r_out, refs = refs[:nr_out], refs[nr_out:]
        scr, sems = refs[:n_scr], refs[n_scr:]
        if grid:
            step = 0
            for ax, size in enumerate(grid):
                step = step * size + pl.program_id(ax)
            pl.when(step == 0)(lambda: rider.start(r_in, r_out, sems))
            body(*ins, *outs, *scr)
            pl.when(step == n_steps - 1)(lambda: rider.wait(r_in, r_out, sems))
        else:
            rider.start(r_in, r_out, sems)
            body(*ins, *outs, *scr)
            rider.wait(r_in, r_out, sems)

    hbm = pl.BlockSpec(memory_space=pl.ANY)
    res = pl.pallas_call(
        full, name=name, grid=grid, out_shape=tuple(out_shape + rider.out_shapes),
        in_specs=list(in_specs) + [hbm] * nr_in, out_specs=tuple(out_specs + [hbm] * nr_out),
        scratch_shapes=list(scratch_shapes) + rider.sems, compiler_params=compiler_params,
    )(*args, *rider.inputs)
    return list(res[:n_out]), list(res[n_out:])


def _rope_rot(t):
    w = t.shape[1]
    lane = lax.broadcasted_iota(jnp.int32, t.shape, 1)
    first = (lane % HEAD_DIM) < (HEAD_DIM // 2)
    return jnp.where(first, pltpu.roll(t, w - HEAD_DIM // 2, 1), pltpu.roll(t, HEAD_DIM // 2, 1))


def _in_proj(x, mod3, g_attn, w_in_t, cos_t, sin_t, seq, rider=None):
    t, d = x.shape
    tm = 2 * TOKEN_TILE
    per_seq = seq // tm
    rope_lo, rope_hi = 3 * NA_WIDTH, 3 * NA_WIDTH + SW_WIDTH + SW_KV_WIDTH
    n_rep = (rope_hi - rope_lo) // LANES

    def body(x_ref, mod_ref, g_ref, w_ref, cos_ref, sin_ref, h_ref, p_ref):
        r, xn = _rms_stats(x_ref[...])
        shift, scale = mod_ref[0, :, 0:d], mod_ref[0, :, d:2 * d]
        hb = ((xn * g_ref[...]) * (1.0 + scale) + shift).astype(BF16)
        h_ref[...] = hb
        p_ref[:, :rope_lo] = _mm_nt(hb, w_ref[:rope_lo, :]).astype(BF16)
        pr = _mm_nt(hb, w_ref[rope_lo:rope_hi, :])
        cos = jnp.concatenate([cos_ref[...]] * n_rep, axis=1)
        sin = jnp.concatenate([sin_ref[...]] * n_rep, axis=1)
        p_ref[:, rope_lo:rope_hi] = (pr * cos + _rope_rot(pr) * sin).astype(BF16)
        p_ref[:, rope_hi:] = _mm_nt(hb, w_ref[rope_hi:, :]).astype(BF16)

    return _hosted(
        body, rider, name="in_proj", grid=(t // tm,),
        out_shape=[jax.ShapeDtypeStruct((t, d), BF16), jax.ShapeDtypeStruct((t, IN_WIDTH), BF16)],
        in_specs=[pl.BlockSpec((tm, d), lambda i: (i, 0)),
                  pl.BlockSpec((1, 1, 6 * d), lambda i: (i // per_seq, 0, 0)),
                  pl.BlockSpec((1, d), lambda i: (0, 0)),
                  pl.BlockSpec((IN_WIDTH, d), lambda i: (0, 0)),
                  pl.BlockSpec((tm, LANES), lambda i: (i % per_seq, 0)),
                  pl.BlockSpec((tm, LANES), lambda i: (i % per_seq, 0))],
        out_specs=[pl.BlockSpec((tm, d), lambda i: (i, 0)), pl.BlockSpec((tm, IN_WIDTH), lambda i: (i, 0))],
        scratch_shapes=[], compiler_params=_cparams(("arbitrary",), VMEM_BIG),
        args=[x, mod3, g_attn, w_in_t, cos_t, sin_t])


def _na_bias_pattern():
    n_dc = 2 * NA_COLS - 1
    j = np.arange(GRID_W)[:, None]
    m = np.arange(GRID_W * LANES)[None, :]
    q, lane = m // LANES, m % LANES
    k = lane % GRID_W
    cs = np.clip(q - NA_COLS // 2, 0, GRID_W - NA_COLS)
    ok = (k >= cs) & (k < cs + NA_COLS)
    hit = ok & (j < 2 * n_dc) & (lane // GRID_W == j // n_dc) & (k - q + (NA_COLS - 1) == j % n_dc)
    return jnp.asarray(hit.astype(np.float32)), jnp.asarray(np.where(ok, 0.0, NEG).astype(np.float32))


def _na_bias_tiles(rows2, expand, mask):
    n, width = rows2.shape[0], expand.shape[1]
    q_step = 16
    step = q_step * LANES

    def body(r_ref, e_ref, m_ref, o_ref):
        for i in range(width // step):
            at = slice(i * step, (i + 1) * step)
            flat = jnp.dot(r_ref[...], e_ref[:, at], precision=lax.Precision.HIGHEST,
                           preferred_element_type=F32) + m_ref[:, at]
            for qq in range(q_step):
                o_ref[:, i * q_step + qq, :] = flat[:, qq * LANES:(qq + 1) * LANES]

    return body, [rows2, expand, mask], jax.ShapeDtypeStruct((n, GRID_W, LANES), F32)


def _na_prepare(k_ref, v_ref, km, vm):
    lane = lax.broadcasted_iota(jnp.int32, k_ref.shape, 1)
    low = lane < HEAD_DIM
    kv = k_ref[...]
    vv = v_ref[...]
    zero = jnp.zeros_like(kv)
    km[0] = jnp.where(low, kv, zero)
    km[1] = jnp.where(low, zero, kv)
    vm[0] = jnp.where(low, vv, zero)
    vm[1] = jnp.where(low, zero, vv)


def _na_window(r, n_rows):
    rs = jnp.clip(r - NA_ROWS // 2, 0, n_rows - NA_ROWS)
    return rs, r - rs


def _na_pair_window(ref, wrows):
    return jnp.concatenate([ref[0, wrows, :], ref[1, wrows, :]], axis=0)


def _na_scores(q, k2, tp_ref, off):
    bias = jnp.concatenate([tp_ref[h, 2 * w - off + (NA_ROWS - 1)] for h in range(2) for w in range(NA_ROWS // 2)],
                           axis=1)
    return _mm_nt(q, k2) * QK_SCALE + bias


def _pair_lse_block(lse):
    lane = lax.broadcasted_iota(jnp.int32, (lse[0].shape[0], LANES), 1)
    return jnp.where(lane < HEAD_DIM, lse[0], lse[1])


def _pair_softmax(s):
    win = s.shape[1] // 2
    halves, lse = [], []
    for h in range(2):
        sh = s[:, h * win:(h + 1) * win]
        m = jnp.max(sh, axis=-1, keepdims=True)
        e = jnp.exp(sh - m)
        l = jnp.sum(e, axis=-1, keepdims=True)
        halves.append(e / l)
        lse.append(m + jnp.log(l))
    return jnp.concatenate(halves, axis=1), _pair_lse_block(lse)


def _pair_grad(w2, x, low):
    keys = w2.shape[1] // 2
    zero = jnp.zeros_like(x)
    low_x = low[:x.shape[0]]
    stacked = jnp.concatenate([w2[:, :keys], w2[:, keys:]], axis=0)
    diag = jnp.concatenate([jnp.where(low_x, x, zero), jnp.where(low_x, zero, x)], axis=0)
    return _mm_tn(stacked, diag)


def _pair_probs_from_lse(s, lse_block):
    win = s.shape[1] // 2
    return jnp.concatenate([jnp.exp(s[:, h * win:(h + 1) * win] - lse_block[:, h * HEAD_DIM:h * HEAD_DIM + 1])
                            for h in range(2)], axis=1)


def _na_forward(proj, tiles, batch, seq, rider=None):
    t = proj.shape[0]
    n_rows = seq // GRID_W
    n_pairs = NA_WIDTH // LANES
    win = NA_ROWS * GRID_W

    def body(q_ref, k_ref, v_ref, tp_ref, o_ref, lse_ref, km, vm):
        _na_prepare(k_ref, v_ref, km, vm)

        def scores(r):
            rs, off = _na_window(r, n_rows)
            rows = pl.ds(pl.multiple_of(r * GRID_W, GRID_W), GRID_W)
            wrows = pl.ds(pl.multiple_of(rs * GRID_W, GRID_W), win)
            return rows, wrows, _na_scores(q_ref[rows, :], _na_pair_window(km, wrows), tp_ref, off)

        def finish(rows, wrows, s):
            p, lse = _pair_softmax(s)
            lse_ref[rows, :] = lse
            o_ref[rows, :] = _mm(p.astype(BF16), _na_pair_window(vm, wrows))

        def row_group(i, carry):
            for state in [scores(NA_GROUP * i + j) for j in range(NA_GROUP)]:
                finish(*state)
            return carry

        lax.fori_loop(0, n_rows // NA_GROUP, row_group, 0)

    return _hosted(
        body, rider, name="na_forward", grid=(batch, n_pairs),
        out_shape=[jax.ShapeDtypeStruct((t, NA_WIDTH), F32), jax.ShapeDtypeStruct((t, NA_WIDTH), F32)],
        in_specs=[pl.BlockSpec((seq, LANES), lambda b, p: (b, p)),
                  pl.BlockSpec((seq, LANES), lambda b, p: (b, n_pairs + p)),
                  pl.BlockSpec((seq, LANES), lambda b, p: (b, 2 * n_pairs + p)),
                  pl.BlockSpec((2, 2 * NA_ROWS - 2, GRID_W, LANES), lambda b, p: (p, 0, 0, 0))],
        out_specs=[pl.BlockSpec((seq, LANES), lambda b, p: (b, p)), pl.BlockSpec((seq, LANES), lambda b, p: (b, p))],
        scratch_shapes=[pltpu.VMEM((2, seq, LANES), BF16), pltpu.VMEM((2, seq, LANES), BF16)],
        compiler_params=_cparams(("arbitrary", "arbitrary")), args=[proj, proj, proj, tiles])


def _sw_prepare(kv_ref, g, dst_lo, dst_hi, seq):
    lane = lax.broadcasted_iota(jnp.int32, kv_ref.shape, 1)
    mine = (lane // HEAD_DIM) == g
    kg = jnp.where(mine, kv_ref[...].astype(F32), 0.0)
    kr = pltpu.roll(kg, HEAD_DIM, 1)
    first = g == 0
    zero = jnp.zeros((SW_BLOCK, LANES), BF16)
    for dst, val in ((dst_lo, jnp.where(first, kg, kr)), (dst_hi, jnp.where(first, kr, kg))):
        dst[0:SW_BLOCK, :] = zero
        dst[SW_BLOCK:SW_BLOCK + seq, :] = val.astype(BF16)
        dst[SW_BLOCK + seq:, :] = zero


def _sw_mask(n, seq):
    qi = lax.broadcasted_iota(jnp.int32, (SW_BLOCK, 3 * SW_BLOCK), 0)
    kj = lax.broadcasted_iota(jnp.int32, (SW_BLOCK, 3 * SW_BLOCK), 1)
    kpos = n * SW_BLOCK - SW_BLOCK + kj
    return (jnp.abs(qi + SW_BLOCK - kj) <= SW_BLOCK) & (kpos >= 0) & (kpos < seq)


def _sw_probs(s2, ok, sinks):
    band = s2.shape[1] // 2
    halves, lse = [], []
    for i in range(2):
        s = jnp.where(ok, s2[:, i * band:(i + 1) * band], NEG)
        m = jnp.maximum(jnp.max(s, axis=-1, keepdims=True), sinks[i])
        p = jnp.exp(s - m)
        den = jnp.sum(p, axis=-1, keepdims=True) + jnp.exp(sinks[i] - m)
        halves.append(p / den)
        lse.append(m + jnp.log(den))
    return jnp.concatenate(halves, axis=1), _pair_lse_block(lse)


def _sw_probs_from_lse(s2, ok, sinks, lse_block):
    band = s2.shape[1] // 2
    halves, sink_p = [], []
    for i in range(2):
        lse = lse_block[:, i * HEAD_DIM:i * HEAD_DIM + 1]
        halves.append(jnp.exp(jnp.where(ok, s2[:, i * band:(i + 1) * band], NEG) - lse))
        sink_p.append(jnp.exp(sinks[i] - lse))
    return jnp.concatenate(halves, axis=1), sink_p


def _sw_forward(proj, sink, batch, seq, rider=None):
    t = proj.shape[0]
    n_pairs = SW_WIDTH // LANES
    q_blk = 3 * NA_WIDTH // LANES
    k_blk = q_blk + n_pairs
    n_blocks = seq // SW_BLOCK
    pad = seq + 2 * SW_BLOCK

    def body(sink_ref, q_ref, k_ref, v_ref, o_ref, lse_ref, k_lo, k_hi, v_lo, v_hi):
        hp = pl.program_id(1)
        g = hp // 2

        @pl.when(hp % 2 == 0)
        def _():
            _sw_prepare(k_ref, g, k_lo, k_hi, seq)
            _sw_prepare(v_ref, g, v_lo, v_hi, seq)

        sinks = (sink_ref[2 * hp], sink_ref[2 * hp + 1])

        def scores(n):
            rows = pl.ds(pl.multiple_of(n * SW_BLOCK, SW_BLOCK), SW_BLOCK)
            wrows = pl.ds(pl.multiple_of(n * SW_BLOCK, SW_BLOCK), 3 * SW_BLOCK)
            k2 = jnp.concatenate([k_lo[wrows, :], k_hi[wrows, :]], axis=0)
            return n, rows, wrows, _mm_nt(q_ref[rows, :], k2) * QK_SCALE

        def finish(n, rows, wrows, s2):
            p, lse = _sw_probs(s2, _sw_mask(n, seq), sinks)
            lse_ref[rows, :] = lse
            v2 = jnp.concatenate([v_lo[wrows, :], v_hi[wrows, :]], axis=0)
            o_ref[rows, :] = _mm(p.astype(BF16), v2)

        def block_group(i, carry):
            for state in [scores(SW_GROUP_BLOCKS * i + j) for j in range(SW_GROUP_BLOCKS)]:
                finish(*state)
            return carry

        lax.fori_loop(0, n_blocks // SW_GROUP_BLOCKS, block_group, 0)

    return _hosted(
        body, rider, name="sw_forward", grid=(batch, n_pairs),
        out_shape=[jax.ShapeDtypeStruct((t, SW_WIDTH), F32), jax.ShapeDtypeStruct((t, SW_WIDTH), F32)],
        in_specs=[pl.BlockSpec(memory_space=pltpu.SMEM),
                  pl.BlockSpec((seq, LANES), lambda b, p: (b, q_blk + p)),
                  pl.BlockSpec((seq, LANES), lambda b, p: (b, k_blk)),
                  pl.BlockSpec((seq, LANES), lambda b, p: (b, k_blk + 1))],
        out_specs=[pl.BlockSpec((seq, LANES), lambda b, p: (b, p)), pl.BlockSpec((seq, LANES), lambda b, p: (b, p))],
        scratch_shapes=[pltpu.VMEM((pad, LANES), BF16)] * 4,
        compiler_params=_cparams(("arbitrary", "arbitrary")), args=[sink, proj, proj, proj])


def _out_proj(oa, ob, g_na, g_sw, w_out, x, mod3, g_ffn, seq):
    t, d = x.shape
    tm = TOKEN_TILE
    per_seq = seq // tm

    def body(oa_ref, ob_ref, gna_ref, gsw_ref, w_ref, x_ref, mod_ref, gf_ref, oab_ref, mix_ref, x1_ref, h2_ref):
        _, na = _rms_stats(oa_ref[...])
        _, nb = _rms_stats(ob_ref[...])
        oab = jnp.concatenate([na * gna_ref[...], nb * gsw_ref[...]], axis=1).astype(BF16)
        oab_ref[...] = oab
        mix = _mm(oab, w_ref[...])
        mix_ref[...] = mix
        gate_a = mod_ref[0, :, 2 * d:3 * d]
        shift_f, scale_f = mod_ref[0, :, 3 * d:4 * d], mod_ref[0, :, 4 * d:5 * d]
        x1 = x_ref[...] + gate_a * mix
        x1_ref[...] = x1
        _, xn = _rms_stats(x1)
        h2_ref[...] = ((xn * gf_ref[...]) * (1.0 + scale_f) + shift_f).astype(BF16)

    tile = lambda w: pl.BlockSpec((tm, w), lambda i: (i, 0))
    vec = lambda w: pl.BlockSpec((1, w), lambda i: (0, 0))
    return pl.pallas_call(
        body, name="out_proj", grid=(t // tm,),
        out_shape=(jax.ShapeDtypeStruct((t, d), BF16), jax.ShapeDtypeStruct((t, d), F32),
                   jax.ShapeDtypeStruct((t, d), F32), jax.ShapeDtypeStruct((t, d), BF16)),
        in_specs=[tile(NA_WIDTH), tile(SW_WIDTH), vec(NA_WIDTH), vec(SW_WIDTH),
                  pl.BlockSpec((d, d), lambda i: (0, 0)), tile(d),
                  pl.BlockSpec((1, 1, 6 * d), lambda i: (i // per_seq, 0, 0)), vec(d)],
        out_specs=(tile(d), tile(d), tile(d), tile(d)),
        compiler_params=_cparams(("arbitrary",), VMEM_BIG),
    )(oa, ob, g_na, g_sw, w_out, x, mod3, g_ffn)


def _up_proj(h2, w_up_halves, rider=None):
    t, d = h2.shape
    tm = 2 * TOKEN_TILE
    w_a, w_b = w_up_halves
    half, wcol = w_a.shape[1], w_a.shape[2]

    def body(h_ref, wa_ref, wb_ref, u_ref):
        u_ref[0] = (_mm(h_ref[:, :half], wa_ref[0]) + _mm(h_ref[:, half:], wb_ref[0])).astype(BF16)

    w_spec = pl.BlockSpec((1, half, wcol), lambda j, i: (j, 0, 0))
    return _hosted(
        body, rider, name="up_proj", grid=(N_SHARD, t // tm),
        out_shape=[jax.ShapeDtypeStruct((2, t, D_FF), BF16)],
        in_specs=[pl.BlockSpec((tm, d), lambda j, i: (i, 0)), w_spec, w_spec],
        out_specs=[pl.BlockSpec((1, tm, wcol), lambda j, i: (j // 2, i, j % 2))],
        scratch_shapes=[], compiler_params=_cparams(("arbitrary", "arbitrary"), VMEM_BIG), args=[h2, w_a, w_b])


def _taps_chunk(load, s, rows, seq):
    halo = 2 * SUBLANES
    cur = load(s, rows)
    above = load(pl.multiple_of(jnp.maximum(s - halo, 0), halo), halo)
    below = load(pl.multiple_of(jnp.minimum(s + rows, seq - halo), halo), halo)
    up = jnp.where(s > 0, above[halo - 1:halo, :], 0.0)
    dn = jnp.where(s + rows < seq, below[0:1, :], 0.0)
    row = lax.broadcasted_iota(jnp.int32, cur.shape, 0)
    prev = jnp.where(row == 0, up, pltpu.roll(cur, 1, 0))
    nxt = jnp.where(row == rows - 1, dn, pltpu.roll(cur, rows - 1, 0))
    return cur, prev, nxt


def _conv_gate(u, conv_w, conv_b, batch, seq, rider=None):
    t = u.shape[1]
    cw = FF_TILE
    rows = CONV_CHUNK

    def body(u_ref, w_ref, b_ref, a_ref):
        def chunk(i, carry):
            s = pl.multiple_of(i * rows, rows)
            gt, prev, nxt = _taps_chunk(lambda at, n: u_ref[1, pl.ds(at, n), :].astype(F32), s, rows, seq)
            gc = prev * w_ref[0:1, :] + gt * w_ref[1:2, :] + nxt * w_ref[2:3, :] + b_ref[...]
            a_ref[pl.ds(s, rows), :] = ((gc * _sigmoid(gc)) * u_ref[0, pl.ds(s, rows), :].astype(F32)).astype(BF16)
            return carry

        lax.fori_loop(0, seq // rows, chunk, 0)

    return _hosted(
        body, rider, name="conv_gate", grid=(batch, D_FF // cw),
        out_shape=[jax.ShapeDtypeStruct((t, D_FF), BF16)],
        in_specs=[pl.BlockSpec((2, seq, cw), lambda b, j: (0, b, j)),
                  pl.BlockSpec((3, cw), lambda b, j: (0, j)), pl.BlockSpec((1, cw), lambda b, j: (0, j))],
        out_specs=[pl.BlockSpec((seq, cw), lambda b, j: (b, j))], scratch_shapes=[],
        compiler_params=_cparams(("arbitrary", "arbitrary"), VMEM_BIG), args=[u, conv_w, conv_b])


def _down_and_loss(a, w_down, x1, mod3, g_final, target, seq):
    t, d = x1.shape
    tm = TOKEN_TILE
    per_seq = seq // tm
    batch = t // seq

    def body(a_ref, w_ref, x1_ref, mod_ref, g_ref, tgt_ref, dx2_ref, dffn_ref, loss_ref, dgate_ref, dg_ref):
        i = pl.program_id(0)
        f = _mm(a_ref[...], w_ref[...])
        gate_f = mod_ref[0, :, 5 * d:6 * d]
        x2 = x1_ref[...] + gate_f * f
        r, xn = _rms_stats(x2)
        err = xn * g_ref[...] - tgt_ref[...]
        part = 0.5 * jnp.sum(jnp.mean(err * err, axis=-1, keepdims=True))
        dy = err / d
        dx2 = _rms_bwd(dy * g_ref[...], xn, r)
        dx2_ref[...] = dx2
        dffn_ref[...] = (dx2 * gate_f).astype(BF16)

        @pl.when(i == 0)
        def _():
            loss_ref[...] = jnp.zeros_like(loss_ref)
            dg_ref[...] = jnp.zeros_like(dg_ref)

        @pl.when(i % per_seq == 0)
        def _():
            dgate_ref[...] = jnp.zeros_like(dgate_ref)

        loss_ref[...] += part
        dg_ref[...] += jnp.sum(dy * xn, axis=0, keepdims=True)
        dgate_ref[0] += jnp.sum(dx2 * f, axis=0, keepdims=True)

    tile = lambda w: pl.BlockSpec((tm, w), lambda i: (i, 0))
    return pl.pallas_call(
        body, name="down_loss", grid=(t // tm,),
        out_shape=(jax.ShapeDtypeStruct((t, d), F32), jax.ShapeDtypeStruct((t, d), BF16),
                   jax.ShapeDtypeStruct((SUBLANES, LANES), F32), jax.ShapeDtypeStruct((batch, 1, d), F32),
                   jax.ShapeDtypeStruct((1, d), F32)),
        in_specs=[tile(D_FF), _resident((D_FF, d)), tile(d),
                  pl.BlockSpec((1, 1, 6 * d), lambda i: (i // per_seq, 0, 0)),
                  pl.BlockSpec((1, d), lambda i: (0, 0)), tile(d)],
        out_specs=(tile(d), tile(d), pl.BlockSpec((SUBLANES, LANES), lambda i: (0, 0)),
                   pl.BlockSpec((1, 1, d), lambda i: (i // per_seq, 0, 0)), pl.BlockSpec((1, d), lambda i: (0, 0))),
        compiler_params=_cparams(("arbitrary",), VMEM_BIG),
    )(a, w_down, x1, mod3, g_final, target)


def _down_weight_grad(a, dffn):
    t, dff = a.shape
    d = dffn.shape[1]
    tk = 2 * TOKEN_TILE
    n_k = t // tk

    def body(a_ref, df_ref, g_ref, gb_ref):
        k = pl.program_id(0)

        @pl.when(k == 0)
        def _():
            g_ref[...] = jnp.zeros_like(g_ref)

        g_ref[...] += _mm_tn(a_ref[...], df_ref[...])

        @pl.when(k == n_k - 1)
        def _():
            gb_ref[...] = g_ref[...].astype(BF16)

    whole = _resident((dff, d))
    return pl.pallas_call(
        body, name="down_weight_grad", grid=(n_k,),
        out_shape=(jax.ShapeDtypeStruct((dff, d), F32), jax.ShapeDtypeStruct((dff, d), BF16)),
        in_specs=[pl.BlockSpec((tk, dff), lambda k: (k, 0)), pl.BlockSpec((tk, d), lambda k: (k, 0))],
        out_specs=(whole, whole),
        compiler_params=_cparams(("arbitrary",), VMEM_BIG),
    )(a, dffn)


def _ffn_backward(dffn, w_down, u, conv_w, conv_b, batch, seq, rider=None):
    t, d = dffn.shape
    cw = FF_TILE
    rows = CONV_CHUNK

    def body(df_ref, wd_ref, u_ref, w_ref, b_ref, du_ref, gcw_ref, gcb_ref, da_scr, dgc_scr):
        b = pl.program_id(1)
        da_scr[...] = _mm_nt(df_ref[...], wd_ref[...])

        @pl.when(b == 0)
        def _():
            gcw_ref[...] = jnp.zeros_like(gcw_ref)
            gcb_ref[...] = jnp.zeros_like(gcb_ref)

        def fold(v):
            return jnp.sum(v.reshape(rows // SUBLANES, SUBLANES, cw), axis=0)

        def chunk(i, carry):
            s = pl.multiple_of(i * rows, rows)
            here = pl.ds(s, rows)
            gt, prev, nxt = _taps_chunk(lambda at, n: u_ref[1, pl.ds(at, n), :].astype(F32), s, rows, seq)
            val, da = u_ref[0, here, :].astype(F32), da_scr[here, :]
            gc = prev * w_ref[0:1, :] + gt * w_ref[1:2, :] + nxt * w_ref[2:3, :] + b_ref[...]
            sg = _sigmoid(gc)
            sl = gc * sg
            du_ref[0, here, :] = (da * sl).astype(BF16)
            dgc = (da * val) * (sg * (1.0 + gc * (1.0 - sg)))
            dgc_scr[here, :] = dgc
            cb, c0, c1, c2 = carry
            return cb + fold(dgc), c0 + fold(dgc * prev), c1 + fold(dgc * gt), c2 + fold(dgc * nxt)

        zero = jnp.zeros((SUBLANES, cw), F32)
        cb, c0, c1, c2 = lax.fori_loop(0, seq // rows, chunk, (zero, zero, zero, zero))
        gcb_ref[...] += jnp.sum(cb, axis=0, keepdims=True)
        gcw_ref[0:1, :] += jnp.sum(c0, axis=0, keepdims=True)
        gcw_ref[1:2, :] += jnp.sum(c1, axis=0, keepdims=True)
        gcw_ref[2:3, :] += jnp.sum(c2, axis=0, keepdims=True)

        def chunk2(i, carry):
            s = pl.multiple_of(i * rows, rows)
            dgc, dprev, dnxt = _taps_chunk(lambda at, n: dgc_scr[pl.ds(at, n), :], s, rows, seq)
            du_ref[1, pl.ds(s, rows), :] = (dnxt * w_ref[0:1, :] + dgc * w_ref[1:2, :]
                                            + dprev * w_ref[2:3, :]).astype(BF16)
            return carry

        lax.fori_loop(0, seq // rows, chunk2, 0)

    return _hosted(
        body, rider, name="ffn_backward", grid=(D_FF // cw, batch),
        out_shape=[jax.ShapeDtypeStruct((2, t, D_FF), BF16),
                   jax.ShapeDtypeStruct((3, D_FF), F32), jax.ShapeDtypeStruct((1, D_FF), F32)],
        in_specs=[pl.BlockSpec((seq, d), lambda j, b: (b, 0)), pl.BlockSpec((cw, d), lambda j, b: (j, 0)),
                  pl.BlockSpec((2, seq, cw), lambda j, b: (0, b, j)),
                  pl.BlockSpec((3, cw), lambda j, b: (0, j)), pl.BlockSpec((1, cw), lambda j, b: (0, j))],
        out_specs=[pl.BlockSpec((2, seq, cw), lambda j, b: (0, b, j)),
                   pl.BlockSpec((3, cw), lambda j, b: (0, j)), pl.BlockSpec((1, cw), lambda j, b: (0, j))],
        scratch_shapes=[pltpu.VMEM((seq, cw), F32), pltpu.VMEM((seq, cw), F32)],
        compiler_params=_cparams(("arbitrary", "arbitrary"), VMEM_BIG), args=[dffn, w_down, u, conv_w, conv_b])


def _up_backward(du, w_up, x1, mod3, g_ffn, dx2, mix, seq, rider=None):
    _, t, _ = du.shape
    d = x1.shape[1]
    tm = TOKEN_TILE
    per_seq = seq // tm
    batch = t // seq
    w_a, w_b = w_up
    half, wcol = w_a.shape[1], w_a.shape[2]

    def body(du_ref, wa_ref, wb_ref, x1_ref, mod_ref, g_ref, dx2_ref, mix_ref,
             dx1_ref, dmix_ref, dsh_ref, dsc_ref, dga_ref, dg_ref):
        i = pl.program_id(0)
        parts = []
        for w_ref in (wa_ref, wb_ref):
            acc = jnp.zeros((tm, half), F32)
            for j in range(N_SHARD):
                acc = acc + _mm_nt(du_ref[j // 2, :, (j % 2) * wcol:(j % 2 + 1) * wcol], w_ref[j])
            parts.append(acc)
        dh = jnp.concatenate(parts, axis=1)
        gate_a = mod_ref[0, :, 2 * d:3 * d]
        scale_f = mod_ref[0, :, 4 * d:5 * d]
        r, xn = _rms_stats(x1_ref[...])
        xg = xn * g_ref[...]
        dxg = dh * (1.0 + scale_f)
        dx1 = dx2_ref[...] + _rms_bwd(dxg * g_ref[...], xn, r)
        dx1_ref[...] = dx1
        dmix_ref[...] = (dx1 * gate_a).astype(BF16)

        @pl.when(i == 0)
        def _():
            dg_ref[...] = jnp.zeros_like(dg_ref)

        @pl.when(i % per_seq == 0)
        def _():
            dsh_ref[...] = jnp.zeros_like(dsh_ref)
            dsc_ref[...] = jnp.zeros_like(dsc_ref)
            dga_ref[...] = jnp.zeros_like(dga_ref)

        dg_ref[...] += jnp.sum(dxg * xn, axis=0, keepdims=True)
        dsh_ref[0] += jnp.sum(dh, axis=0, keepdims=True)
        dsc_ref[0] += jnp.sum(dh * xg, axis=0, keepdims=True)
        dga_ref[0] += jnp.sum(dx1 * mix_ref[...], axis=0, keepdims=True)

    tile = lambda w: pl.BlockSpec((tm, w), lambda i: (i, 0))
    per_b = pl.BlockSpec((1, 1, d), lambda i: (i // per_seq, 0, 0))
    small = jax.ShapeDtypeStruct((batch, 1, d), F32)
    return _hosted(
        body, rider, name="up_backward", grid=(t // tm,),
        out_shape=[jax.ShapeDtypeStruct((t, d), F32), jax.ShapeDtypeStruct((t, d), BF16), small, small, small,
                   jax.ShapeDtypeStruct((1, d), F32)],
        in_specs=[pl.BlockSpec((2, tm, D_FF), lambda i: (0, i, 0)),
                  _resident((N_SHARD, half, wcol)), _resident((N_SHARD, half, wcol)), tile(d),
                  pl.BlockSpec((1, 1, 6 * d), lambda i: (i // per_seq, 0, 0)),
                  pl.BlockSpec((1, d), lambda i: (0, 0)), tile(d), tile(d)],
        out_specs=[tile(d), tile(d), per_b, per_b, per_b, pl.BlockSpec((1, d), lambda i: (0, 0))],
        scratch_shapes=[], compiler_params=_cparams(("arbitrary",), VMEM_BIG),
        args=[du, w_a, w_b, x1, mod3, g_ffn, dx2, mix])


def _up_weight_grad(h2, du, rider=None):
    t, d = h2.shape
    tk = 2 * TOKEN_TILE
    wcol = D_FF // 2
    half = d // 2
    n_k = t // tk

    def body(h_ref, du_ref, ga_ref, gb_ref, ga16_ref, gb16_ref):
        k = pl.program_id(1)

        @pl.when(k == 0)
        def _():
            ga_ref[...] = jnp.zeros_like(ga_ref)
            gb_ref[...] = jnp.zeros_like(gb_ref)

        du = du_ref[0]
        ga_ref[0] += _mm_tn(h_ref[:, :half], du)
        gb_ref[0] += _mm_tn(h_ref[:, half:], du)

        @pl.when(k == n_k - 1)
        def _():
            ga16_ref[...] = ga_ref[...].astype(BF16)
            gb16_ref[...] = gb_ref[...].astype(BF16)

    g_spec = pl.BlockSpec((1, half, wcol), lambda j, k: (j, 0, 0))
    f32_out = jax.ShapeDtypeStruct((N_SHARD, half, wcol), F32)
    b16_out = jax.ShapeDtypeStruct((N_SHARD, half, wcol), BF16)
    return _hosted(
        body, rider, name="up_weight_grad", grid=(N_SHARD, n_k),
        out_shape=[f32_out, f32_out, b16_out, b16_out],
        in_specs=[pl.BlockSpec((tk, d), lambda j, k: (k, 0)),
                  pl.BlockSpec((1, tk, wcol), lambda j, k: (j // 2, k, j % 2))],
        out_specs=[g_spec, g_spec, g_spec, g_spec], scratch_shapes=[],
        compiler_params=_cparams(("arbitrary", "arbitrary"), VMEM_BIG), args=[h2, du])


def _out_backward(dmix, w_out, oab, oa, ob, g_na, g_sw):
    t, d = dmix.shape
    tm = 2 * TOKEN_TILE
    hw = NA_WIDTH

    def body(dm_ref, w_ref, oab_ref, oa_ref, ob_ref, gna_ref, gsw_ref,
             doa_ref, dob_ref, gw_ref, gwb_ref, dgna_ref, dgsw_ref):
        @pl.when(pl.program_id(0) == 0)
        def _():
            gw_ref[...] = jnp.zeros_like(gw_ref)
            dgna_ref[...] = jnp.zeros_like(dgna_ref)
            dgsw_ref[...] = jnp.zeros_like(dgsw_ref)

        dm = dm_ref[...]
        gw_ref[...] += _mm_tn(oab_ref[...], dm)

        @pl.when(pl.program_id(0) == t // tm - 1)
        def _():
            gwb_ref[...] = gw_ref[...].astype(BF16)

        do = _mm_nt(dm, w_ref[...])
        for raw_ref, g_ref, dst_ref, dg_ref, lo in ((oa_ref, gna_ref, doa_ref, dgna_ref, 0),
                                                     (ob_ref, gsw_ref, dob_ref, dgsw_ref, hw)):
            r, xn = _rms_stats(raw_ref[...])
            dpart = do[:, lo:lo + hw]
            dg_ref[...] += jnp.sum(dpart * xn, axis=0, keepdims=True)
            dst_ref[...] = _rms_bwd(dpart * g_ref[...], xn, r).astype(BF16)

    tile = lambda w: pl.BlockSpec((tm, w), lambda i: (i, 0))
    vec = lambda w: pl.BlockSpec((1, w), lambda i: (0, 0))
    return pl.pallas_call(
        body, name="out_backward", grid=(t // tm,),
        out_shape=(jax.ShapeDtypeStruct((t, hw), BF16), jax.ShapeDtypeStruct((t, hw), BF16),
                   jax.ShapeDtypeStruct((d, d), F32), jax.ShapeDtypeStruct((d, d), BF16),
                   jax.ShapeDtypeStruct((1, hw), F32), jax.ShapeDtypeStruct((1, hw), F32)),
        in_specs=[tile(d), pl.BlockSpec((d, d), lambda i: (0, 0)), tile(d), tile(hw), tile(hw), vec(hw), vec(hw)],
        out_specs=(tile(hw), tile(hw), pl.BlockSpec((d, d), lambda i: (0, 0)), pl.BlockSpec((d, d), lambda i: (0, 0)),
                   vec(hw), vec(hw)),
        compiler_params=_cparams(("arbitrary",), VMEM_BIG),
    )(dmix, w_out, oab, oa, ob, g_na, g_sw)


def _na_backward(proj, d_o, lse, tiles, batch, seq, rider=None):
    t = proj.shape[0]
    n_rows = seq // GRID_W
    n_pairs = NA_WIDTH // LANES
    win = NA_ROWS * GRID_W
    n_tiles = 2 * NA_ROWS - 2

    def body(q_ref, k_ref, v_ref, do_ref, lse_ref, tp_ref, dq_ref, dk_ref, dv_ref, dtp_ref, km, vm, dk_acc, dv_acc):
        @pl.when(pl.program_id(1) == 0)
        def _():
            dtp_ref[...] = jnp.zeros_like(dtp_ref)

        _na_prepare(k_ref, v_ref, km, vm)
        dk_acc[...] = jnp.zeros_like(dk_acc)
        dv_acc[...] = jnp.zeros_like(dv_acc)
        low = lax.broadcasted_iota(jnp.int32, (win, LANES), 1) < HEAD_DIM

        def scores(r):
            rs, off = _na_window(r, n_rows)
            rows = pl.ds(pl.multiple_of(r * GRID_W, GRID_W), GRID_W)
            wrows = pl.ds(pl.multiple_of(rs * GRID_W, GRID_W), win)
            q, do = q_ref[rows, :], do_ref[rows, :]
            k2 = _na_pair_window(km, wrows)
            s = _na_scores(q, k2, tp_ref, off)
            dp = _mm_nt(do, _na_pair_window(vm, wrows))
            return rows, wrows, off, q, do, k2, s, dp

        def finish(rows, wrows, off, q, do, k2, s, dp):
            p = _pair_probs_from_lse(s, lse_ref[rows, :])
            parts = []
            for h in range(2):
                ph, dph = p[:, h * win:(h + 1) * win], dp[:, h * win:(h + 1) * win]
                dsh = ph * (dph - jnp.sum(ph * dph, axis=-1, keepdims=True))
                for w in range(NA_ROWS // 2):
                    dtp_ref[h, 2 * w - off + (NA_ROWS - 1)] += dsh[:, w * LANES:(w + 1) * LANES]
                parts.append(dsh)
            dsb = (jnp.concatenate(parts, axis=1) * QK_SCALE).astype(BF16)
            dq_ref[rows, :] = _mm(dsb, k2).astype(BF16)
            dk_acc[wrows, :] += _pair_grad(dsb, q, low)
            dv_acc[wrows, :] += _pair_grad(p.astype(BF16), do, low)

        def row_group(i, carry):
            for state in [scores(NA_GROUP * i + j) for j in range(NA_GROUP)]:
                finish(*state)
            return carry

        lax.fori_loop(0, n_rows // NA_GROUP, row_group, 0)
        dk_ref[...] = dk_acc[...].astype(BF16)
        dv_ref[...] = dv_acc[...].astype(BF16)

    blk = lambda off: pl.BlockSpec((seq, LANES), lambda p, b: (b, off + p))
    out = jax.ShapeDtypeStruct((t, NA_WIDTH), BF16)
    return _hosted(
        body, rider, name="na_backward", grid=(n_pairs, batch),
        out_shape=[out, out, out, jax.ShapeDtypeStruct(tiles.shape, F32)],
        in_specs=[blk(0), blk(n_pairs), blk(2 * n_pairs), blk(0), blk(0),
                  pl.BlockSpec((2, n_tiles, GRID_W, LANES), lambda p, b: (p, 0, 0, 0))],
        out_specs=[blk(0), blk(0), blk(0), pl.BlockSpec((2, n_tiles, GRID_W, LANES), lambda p, b: (p, 0, 0, 0))],
        scratch_shapes=[pltpu.VMEM((2, seq, LANES), BF16), pltpu.VMEM((2, seq, LANES), BF16),
                        pltpu.VMEM((seq, LANES), F32), pltpu.VMEM((seq, LANES), F32)],
        compiler_params=_cparams(("arbitrary", "arbitrary")), args=[proj, proj, proj, d_o, lse, tiles])


def _na_bias_grad(dtiles, expand):
    n = dtiles.shape[0]

    def body(t_ref, e_ref, o_ref):
        flat = jnp.concatenate([t_ref[:, qq, :] for qq in range(GRID_W)], axis=1)
        o_ref[...] = lax.dot_general(flat, e_ref[...], (((1,), (1,)), ((), ())),
                                     precision=lax.Precision.HIGHEST, preferred_element_type=F32)

    return body, [dtiles, expand], jax.ShapeDtypeStruct((n, expand.shape[0]), F32)


def _sw_backward(proj, d_o, lse, sink, batch, seq, rider=None):
    t = proj.shape[0]
    n_pairs = SW_WIDTH // LANES
    q_blk = 3 * NA_WIDTH // LANES
    k_blk = q_blk + n_pairs
    n_blocks = seq // SW_BLOCK
    pad = seq + 2 * SW_BLOCK

    def body(sink_ref, q_ref, k_ref, v_ref, do_ref, lse_ref, dq_ref, dk_ref, dv_ref, dsk_ref,
             k_lo, k_hi, v_lo, v_hi, dk_loc, dv_loc, dk_tot, dv_tot):
        hp = pl.program_id(1)
        g = hp // 2

        @pl.when(hp % 2 == 0)
        def _():
            _sw_prepare(k_ref, g, k_lo, k_hi, seq)
            _sw_prepare(v_ref, g, v_lo, v_hi, seq)
            dk_loc[...] = jnp.zeros_like(dk_loc)
            dv_loc[...] = jnp.zeros_like(dv_loc)

        @pl.when(hp == 0)
        def _():
            dk_tot[...] = jnp.zeros_like(dk_tot)
            dv_tot[...] = jnp.zeros_like(dv_tot)

        band = 3 * SW_BLOCK
        low = lax.broadcasted_iota(jnp.int32, (band, LANES), 1) < HEAD_DIM

        sinks = (sink_ref[2 * hp], sink_ref[2 * hp + 1])

        def scores(n):
            rows = pl.ds(pl.multiple_of(n * SW_BLOCK, SW_BLOCK), SW_BLOCK)
            wrows = pl.ds(pl.multiple_of(n * SW_BLOCK, SW_BLOCK), band)
            qb, do = q_ref[rows, :], do_ref[rows, :]
            k2 = jnp.concatenate([k_lo[wrows, :], k_hi[wrows, :]], axis=0)
            v2 = jnp.concatenate([v_lo[wrows, :], v_hi[wrows, :]], axis=0)
            return n, rows, wrows, qb, do, k2, _mm_nt(qb, k2) * QK_SCALE, _mm_nt(do, v2)

        def finish(sink_acc, n, rows, wrows, qb, do, k2, s2, dp):
            p, ps = _sw_probs_from_lse(s2, _sw_mask(n, seq), sinks, lse_ref[rows, :])
            parts, new = [], []
            for i in range(2):
                ph, dph = p[:, i * band:(i + 1) * band], dp[:, i * band:(i + 1) * band]
                delta = jnp.sum(ph * dph, axis=-1, keepdims=True)
                parts.append(ph * (dph - delta))
                new.append(sink_acc[i] - ps[i] * delta)
            dsb = (jnp.concatenate(parts, axis=1) * QK_SCALE).astype(BF16)
            dq_ref[rows, :] = _mm(dsb, k2)
            dk_loc[wrows, :] += _pair_grad(dsb, qb, low)
            dv_loc[wrows, :] += _pair_grad(p.astype(BF16), do, low)
            return tuple(new)

        def block_group(i, carry):
            for state in [scores(SW_GROUP_BLOCKS * i + j) for j in range(SW_GROUP_BLOCKS)]:
                carry = finish(carry, *state)
            return carry

        zero = jnp.zeros((SW_BLOCK, 1), F32)
        s0, s1 = lax.fori_loop(0, n_blocks // SW_GROUP_BLOCKS, block_group, (zero, zero))
        row = lax.broadcasted_iota(jnp.int32, (SUBLANES, LANES), 0)
        dsk_ref[0, 0] = jnp.where(row == 0, jnp.sum(s0), jnp.where(row == 1, jnp.sum(s1), 0.0))

        @pl.when(hp % 2 == 1)
        def _():
            lane_s = lax.broadcasted_iota(jnp.int32, (seq, LANES), 1)
            mine_g = (lane_s // HEAD_DIM) == g
            for loc, tot in ((dk_loc, dk_tot), (dv_loc, dv_tot)):
                part = loc[SW_BLOCK:SW_BLOCK + seq, :]
                tot[...] += jnp.where(mine_g, part + pltpu.roll(part, HEAD_DIM, 1), 0.0)

        @pl.when(hp == n_pairs - 1)
        def _():
            dk_ref[...] = dk_tot[...]
            dv_ref[...] = dv_tot[...].astype(BF16)

    return _hosted(
        body, rider, name="sw_backward", grid=(batch, n_pairs),
        out_shape=[jax.ShapeDtypeStruct((t, SW_WIDTH), F32), jax.ShapeDtypeStruct((t, LANES), F32),
                   jax.ShapeDtypeStruct((t, LANES), BF16), jax.ShapeDtypeStruct((batch, n_pairs, SUBLANES, LANES), F32)],
        in_specs=[pl.BlockSpec(memory_space=pltpu.SMEM),
                  pl.BlockSpec((seq, LANES), lambda b, p: (b, q_blk + p)),
                  pl.BlockSpec((seq, LANES), lambda b, p: (b, k_blk)),
                  pl.BlockSpec((seq, LANES), lambda b, p: (b, k_blk + 1)),
                  pl.BlockSpec((seq, LANES), lambda b, p: (b, p)), pl.BlockSpec((seq, LANES), lambda b, p: (b, p))],
        out_specs=[pl.BlockSpec((seq, LANES), lambda b, p: (b, p)), pl.BlockSpec((seq, LANES), lambda b, p: (b, 0)),
                   pl.BlockSpec((seq, LANES), lambda b, p: (b, 0)),
                   pl.BlockSpec((1, 1, SUBLANES, LANES), lambda b, p: (b, p, 0, 0))],
        scratch_shapes=[pltpu.VMEM((pad, LANES), BF16)] * 4 + [pltpu.VMEM((pad, LANES), F32)] * 2
        + [pltpu.VMEM((seq, LANES), F32)] * 2,
        compiler_params=_cparams(("arbitrary", "arbitrary")), args=[sink, proj, proj, proj, d_o, lse])


def _in_backward(dqkv_a, dq_b, dk_b, dv_b, w_in_t, h1, x, mod3, g_attn, dx1, cos_t, sin_t, seq):
    t, d = x.shape
    tm = TOKEN_TILE
    per_seq = seq // tm
    batch = t // seq
    dqa, dka, dva = dqkv_a
    n_q = SW_WIDTH // LANES

    def body(dqa_ref, dka_ref, dva_ref, dqb_ref, dkb_ref, dvb_ref, w_ref, h_ref, x_ref, mod_ref, g_ref, dx1_ref,
             cos_ref, sin_ref, dx_ref, gw_ref, gwb_ref, dsh_ref, dsc_ref, dg_ref):
        i = pl.program_id(0)

        @pl.when(i == 0)
        def _():
            gw_ref[...] = jnp.zeros_like(gw_ref)
            dg_ref[...] = jnp.zeros_like(dg_ref)

        @pl.when(i % per_seq == 0)
        def _():
            dsh_ref[...] = jnp.zeros_like(dsh_ref)
            dsc_ref[...] = jnp.zeros_like(dsc_ref)

        dr = jnp.concatenate([dqb_ref[...], dkb_ref[...]], axis=1)
        cos = jnp.concatenate([cos_ref[...]] * (n_q + 1), axis=1)
        sin = jnp.concatenate([sin_ref[...]] * (n_q + 1), axis=1)
        dr = dr * cos + _rope_rot(dr * sin)
        dproj = jnp.concatenate([dqa_ref[...], dka_ref[...], dva_ref[...], dr.astype(BF16), dvb_ref[...]], axis=1)
        gw_ref[...] += _mm_tn(dproj, h_ref[...])

        @pl.when(i == t // tm - 1)
        def _():
            gwb_ref[...] = gw_ref[...].astype(BF16)

        dh = _mm(dproj, w_ref[...])
        scale = mod_ref[0, :, d:2 * d]
        r, xn = _rms_stats(x_ref[...])
        xg = xn * g_ref[...]
        dxg = dh * (1.0 + scale)
        dx_ref[...] = dx1_ref[...] + _rms_bwd(dxg * g_ref[...], xn, r)
        dg_ref[...] += jnp.sum(dxg * xn, axis=0, keepdims=True)
        dsh_ref[0] += jnp.sum(dh, axis=0, keepdims=True)
        dsc_ref[0] += jnp.sum(dh * xg, axis=0, keepdims=True)

    tile = lambda w: pl.BlockSpec((tm, w), lambda i: (i, 0))
    per_b = pl.BlockSpec((1, 1, d), lambda i: (i // per_seq, 0, 0))
    small = jax.ShapeDtypeStruct((batch, 1, d), F32)
    rope = pl.BlockSpec((tm, LANES), lambda i: (i % per_seq, 0))
    return pl.pallas_call(
        body, name="in_backward", grid=(t // tm,),
        out_shape=(jax.ShapeDtypeStruct((t, d), F32), jax.ShapeDtypeStruct((IN_WIDTH, d), F32),
                   jax.ShapeDtypeStruct((IN_WIDTH, d), BF16), small, small, jax.ShapeDtypeStruct((1, d), F32)),
        in_specs=[tile(NA_WIDTH), tile(NA_WIDTH), tile(NA_WIDTH), tile(SW_WIDTH), tile(LANES), tile(LANES),
                  _resident((IN_WIDTH, d)), tile(d), tile(d),
                  pl.BlockSpec((1, 1, 6 * d), lambda i: (i // per_seq, 0, 0)),
                  pl.BlockSpec((1, d), lambda i: (0, 0)), tile(d), rope, rope],
        out_specs=(tile(d), _resident((IN_WIDTH, d)), _resident((IN_WIDTH, d)),
                   per_b, per_b, pl.BlockSpec((1, d), lambda i: (0, 0))),
        compiler_params=_cparams(("arbitrary",), VMEM_BIG),
    )(dqa, dka, dva, dq_b, dk_b, dv_b, w_in_t, h1, x, mod3, g_attn, dx1, cos_t, sin_t)


def _row_tile(rows, cols):
    target = max(SUBLANES, (1 << 20) // (4 * cols))
    best = rows
    for cand in range(SUBLANES, rows + 1, SUBLANES):
        if rows % cand == 0 and cand <= target:
            best = cand
    return best if rows % SUBLANES == 0 else rows


def _sum_slots(results, name, rider=None, passenger=None):
    extra_body, extra_in, extra_out = passenger if passenger is not None else (None, [], None)
    n_extra = len(extra_in)
    parts = [group for groups in results for group in groups]
    result_of = [k for k, groups in enumerate(results) for _ in groups]
    n_parts = len(parts)
    cols = [results[k][0][1].shape[1] for k in result_of]
    tr = [_row_tile(min(own.shape[0] for _, own in results[k]), c) for k, c in zip(result_of, cols)]
    assert all(own.shape[0] % r == 0 and own.shape[1] == c for (_, own), r, c in zip(parts, tr, cols))
    tiles = [own.shape[0] // r for (_, own), r in zip(parts, tr)]
    first = [sum(tiles[:q]) for q in range(n_parts)]

    def body(*refs):
        o_refs = refs[2 * n_parts + n_extra:]
        step = pl.program_id(0)
        if passenger is not None:
            pl.when(step == 0)(lambda: extra_body(*refs[2 * n_parts:2 * n_parts + n_extra], o_refs[len(results)]))
        for q in range(n_parts):
            @pl.when((step >= first[q]) & (step < first[q] + tiles[q]))
            def _(q=q):
                p_ref, own_ref = refs[2 * q], refs[2 * q + 1]
                o_refs[result_of[q]][...] = (((own_ref[...] + p_ref[0].astype(F32)) + p_ref[1].astype(F32))
                                             + p_ref[2].astype(F32))

    def tile(start, count):
        return lambda i: jnp.clip(i - start, 0, count - 1)

    in_specs, args = [], []
    for q, (recv, own) in enumerate(parts):
        at = tile(first[q], tiles[q])
        in_specs.append(pl.BlockSpec((N_SHARD - 1, tr[q], cols[q]), lambda i, at=at: (0, at(i), 0)))
        in_specs.append(pl.BlockSpec((tr[q], cols[q]), lambda i, at=at: (at(i), 0)))
        args += [recv, own]
    out_shape, out_specs = [], []
    for k in range(len(results)):
        mine = [q for q in range(n_parts) if result_of[q] == k]
        count = sum(tiles[q] for q in mine)
        at = tile(first[mine[0]], count)
        out_shape.append(jax.ShapeDtypeStruct((count * tr[mine[0]], cols[mine[0]]), F32))
        out_specs.append(pl.BlockSpec((tr[mine[0]], cols[mine[0]]), lambda i, at=at: (at(i), 0)))
    whole = lambda a: pl.BlockSpec(a.shape, lambda i, nd=len(a.shape): (0,) * nd)
    in_specs += [whole(a) for a in extra_in]
    args += extra_in
    if passenger is not None:
        out_shape.append(extra_out)
        out_specs.append(whole(extra_out))
    return _hosted(body, rider, name=name, grid=(sum(tiles),), out_shape=out_shape, in_specs=in_specs,
                   out_specs=out_specs, scratch_shapes=[], compiler_params=_cparams(("arbitrary",), VMEM_BIG), args=args)


def _adamw_math(w, g, m, v):
    m2 = ADAM_B1 * m + (1.0 - ADAM_B1) * g
    v2 = ADAM_B2 * v + (1.0 - ADAM_B2) * (g * g)
    m_hat = m2 / (1.0 - ADAM_B1 ** ADAM_STEP)
    v_hat = v2 / (1.0 - ADAM_B2 ** ADAM_STEP)
    return -ADAM_LR * (m_hat / (jnp.sqrt(v_hat) + ADAM_EPS) + ADAM_WD * w), m2, v2


def _small_sums(partials, dmod, rider=None):
    moving = list(partials) + [dmod]
    n_mov = len(moving)

    def body(*refs):
        mov, refs = refs[:n_mov], refs[n_mov:]
        sums_out, refs = refs[:n_mov - 1], refs[n_mov - 1:]
        b_out, dmod_out, refs = refs[0], refs[1], refs[2:]
        everyone, (ssem, rsem) = refs[:n_mov], refs[n_mov:]
        x, y, c = _my_pos()
        me = 4 * x + 2 * y + c
        cps = []
        for a in range(n_mov):
            everyone[a][me] = mov[a][...]
            for k in range(1, N_DEV):
                peer = (_flip(x, (k >> 2) & 1), _flip(y, (k >> 1) & 1), _flip(c, k & 1))
                cps.append(pltpu.make_async_remote_copy(
                    src_ref=everyone[a].at[me], dst_ref=everyone[a].at[me], send_sem=ssem.at[a, k - 1],
                    recv_sem=rsem.at[a, k - 1], device_id=peer, device_id_type=MESH))
        for cp in cps:
            cp.start()
        for cp in cps:
            cp.wait_recv()

        def total(a):
            acc = everyone[a][0]
            for dev in range(1, N_DEV):
                acc = acc + everyone[a][dev]
            return acc

        for a in range(n_mov - 1):
            sums_out[a][...] = total(a)
        b_out[...] = jnp.sum(total(n_mov - 1), axis=0, keepdims=True)
        dmod_out[...] = everyone[n_mov - 1][...]
        for cp in cps:
            cp.wait_send()

    vm = pl.BlockSpec(memory_space=pltpu.VMEM)
    sds = jax.ShapeDtypeStruct
    out_shape = [sds(p.shape, F32) for p in partials]
    out_shape += [sds((1, dmod.shape[1]), F32), sds((N_DEV,) + dmod.shape, F32)]
    return _hosted(
        body, rider, name="small_sums", grid=(), out_shape=out_shape,
        in_specs=[vm] * n_mov, out_specs=[vm] * len(out_shape),
        scratch_shapes=[pltpu.VMEM((N_DEV,) + a.shape, F32) for a in moving]
        + [pltpu.SemaphoreType.DMA((n_mov, N_DEV - 1)), pltpu.SemaphoreType.DMA((n_mov, N_DEV - 1))],
        compiler_params=_cparams(vmem=VMEM_BIG), args=moving)


def _small_adamw(states, grads):
    n = len(states)

    def body(*refs):
        g_refs, wmv, res = refs[:n], refs[n:4 * n], refs[4 * n:]
        for j in range(n):
            g = g_refs[j][...]
            delta, m2, v2 = _adamw_math(wmv[3 * j][...], g, wmv[3 * j + 1][...], wmv[3 * j + 2][...])
            res[4 * j][...] = g
            res[4 * j + 1][...] = delta
            res[4 * j + 2][...] = m2
            res[4 * j + 3][...] = v2

    out_shape = []
    for w, _, _ in states:
        out_shape += [jax.ShapeDtypeStruct(w.shape, F32)] * 4
    outs = pl.pallas_call(body, name="small_adamw", out_shape=tuple(out_shape),
                          compiler_params=_cparams(vmem=VMEM_BIG))(*grads, *[a for st in states for a in st])
    return [outs[4 * j:4 * j + 4] for j in range(n)]


def _adamw(w, grads, m, v, name, product=None):
    rows, cols = w.shape
    tr = _row_tile(rows, cols)
    if product is not None:
        assert not grads
        a, b = product
        grads = [a, b]
        g_specs = [pl.BlockSpec((tr, a.shape[1]), lambda i: (i, 0)), pl.BlockSpec(b.shape, lambda i: (0, 0))]
    ng = len(grads)

    def body(*refs):
        w_ref = refs[0]
        g_refs = refs[1:1 + ng]
        m_ref, v_ref = refs[1 + ng], refs[2 + ng]
        g_out, d_out, m_out, v_out = refs[3 + ng:]
        if product is not None:
            g = _mm(g_refs[0][...].astype(BF16), g_refs[1][...].astype(BF16))
        else:
            g = g_refs[0][...]
            for extra in g_refs[1:]:
                g = g + extra[...]
        g_out[...] = g
        d_out[...], m_out[...], v_out[...] = _adamw_math(w_ref[...], g, m_ref[...], v_ref[...])

    spec = pl.BlockSpec((tr, cols), lambda i: (i, 0))
    if product is None:
        g_specs = [spec] * ng
    out = jax.ShapeDtypeStruct((rows, cols), F32)
    return pl.pallas_call(
        body, name=name, grid=(rows // tr,),
        out_shape=(out, out, out, out),
        in_specs=[spec] + g_specs + [spec, spec], out_specs=(spec, spec, spec, spec),
        compiler_params=_cparams(("arbitrary",)),
    )(w, *grads, m, v)


def _rope_tables(seq):
    half = HEAD_DIM // 2
    inv = np.float32(ROPE_THETA) ** (-np.arange(half, dtype=np.float32) / np.float32(half))
    ang = (np.arange(seq, dtype=np.float32)[:, None] * inv[None, :]).astype(np.float64)
    cos, sin = np.cos(ang).astype(np.float32), np.sin(ang).astype(np.float32)
    cos_t = np.concatenate([cos, cos, cos, cos], axis=1)
    sin_t = np.concatenate([-sin, sin, -sin, sin], axis=1)
    return jnp.asarray(cos_t), jnp.asarray(sin_t)


def kernel(x, c, w_ada, b_ada, g_attn, w_in, na_rpb, sw_sink, g_na_out, g_sw_out, w_out, g_ffn, w_up, conv_w, conv_b, w_down, g_final, loss_target, m_w_ada, m_b_ada, m_g_attn, m_w_in, m_na_rpb, m_sw_sink, m_g_na_out, m_g_sw_out, m_w_out, m_g_ffn, m_w_up, m_conv_w, m_conv_b, m_w_down, m_g_final, v_w_ada, v_b_ada, v_g_attn, v_w_in, v_na_rpb, v_sw_sink, v_g_na_out, v_g_sw_out, v_w_out, v_g_ffn, v_w_up, v_conv_w, v_conv_b, v_w_down, v_g_final):
    batch, seq, d = x.shape
    t = batch * seq
    assert d == D_MODEL and seq % (NA_ROWS * GRID_W) == 0 and seq % TOKEN_TILE == 0 and batch <= SUBLANES
    shard = 2 * lax.axis_index("x") + lax.axis_index("y")
    xt = x.reshape(t, d)
    tgt = loss_target.reshape(t, d)

    c8 = jnp.pad(c, ((0, SUBLANES - batch), (0, 0)))
    w_in_t_s = jnp.transpose(w_in[0]).astype(BF16)
    n_heads = NA_WIDTH // HEAD_DIM
    n_tiles, n_dc = 2 * NA_ROWS - 2, 2 * NA_COLS - 1
    expand, neg_mask = _na_bias_pattern()
    rpb = na_rpb[0]
    rows2 = jnp.concatenate([rpb[:, :-1, :], rpb[:, 1:, :]], axis=2).reshape(n_heads * n_tiles, 2 * n_dc)
    rows2 = jnp.pad(rows2, ((0, 0), (0, GRID_W - 2 * n_dc)))
    (mod8, sc_all, tiles), (w_in_g,) = _ada_forward(
        c8, w_ada[0], b_ada, _Rider("gather", [w_in_t_s]), _na_bias_tiles(rows2, expand, neg_mask))
    tiles = tiles.reshape(n_heads, n_tiles, GRID_W, LANES)
    mod3 = mod8[:batch].reshape(batch, 1, 6 * d)
    w_in_t = w_in_g.reshape(IN_WIDTH, d)

    cos_t, sin_t = _rope_tables(seq)
    (h1, proj), _ = _in_proj(xt, mod3, g_attn, w_in_t, cos_t, sin_t, seq)
    sink = sw_sink[0]
    w_up_b16 = w_up[0].astype(BF16)
    (oa, lse_a), (w_up_a,) = _na_forward(proj, tiles, batch, seq, _Rider("gather", [w_up_b16[:d // 2]]))
    (ob, lse_b), (w_up_b, conv_w_g, w_out_g) = _sw_forward(
        proj, sink, batch, seq, _Rider("gather", [w_up_b16[d // 2:], conv_w[0], w_out[0].astype(BF16)]))
    w_up_f = (w_up_a, w_up_b)
    w_out_f = w_out_g.reshape(d, d)
    conv_w_f = jnp.transpose(conv_w_g, (1, 0, 2)).reshape(3, D_FF)
    oab, mix, x1, h2 = _out_proj(oa, ob, g_na_out, g_sw_out, w_out_f, xt, mod3, g_ffn, seq)
    (u,), _ = _up_proj(h2, w_up_f)
    (a,), (w_down_g,) = _conv_gate(u, conv_w_f, conv_b, batch, seq, _Rider("gather", [w_down[0].astype(BF16)]))
    w_down_f = w_down_g.reshape(D_FF, d)
    dx2, dffn, loss_part, dgate_f, dg_final = _down_and_loss(a, w_down_f, x1, mod3, g_final.reshape(1, d), tgt, seq)

    gw_down, gw_down_b = _down_weight_grad(a, dffn)
    blocks = lambda g, rows: g.reshape(N_SHARD, rows // N_SHARD, d)
    (du, gconv_w, gconv_b), (recv_down, own_down) = _ffn_backward(
        dffn, w_down_f, u, conv_w_f, conv_b, batch, seq,
        _Rider("scatter", [blocks(gw_down_b, D_FF)], [blocks(gw_down, D_FF)]))
    (gw_up_top, gw_up_bot, gw_up_top_b, gw_up_bot_b), _ = _up_weight_grad(h2, du)
    (dx1, dmix, dshift_f, dscale_f, dgate_a, dg_ffn), _ = _up_backward(du, w_up_f, x1, mod3, g_ffn, dx2, mix, seq)
    doa, dob, gw_out, gw_out_b, dg_na, dg_sw = _out_backward(dmix, w_out_f, oab, oa, ob, g_na_out, g_sw_out)
    (dqa, dka, dva, dtiles), (recv_out, recv_up_bot, own_out, own_up_bot) = _na_backward(
        proj, doa, lse_a, tiles, batch, seq,
        _Rider("scatter", [blocks(gw_out_b, d), gw_up_bot_b], [blocks(gw_out, d), gw_up_bot]))
    (dq_b, dk_b, dv_b, dsink_parts), (recv_up_top, own_up_top) = _sw_backward(
        proj, dob, lse_b, sink, batch, seq, _Rider("scatter", [gw_up_top_b], [gw_up_top]))
    gx, gw_in_t, gw_in_b, dshift_a, dscale_a, dg_attn = _in_backward(
        (dqa, dka, dva), dq_b, dk_b, dv_b, w_in_t, h1, xt, mod3, g_attn, dx1, cos_t, sin_t, seq)

    late, (recv_in, own_in) = _sum_slots(
        [[(recv_out, own_out)], [(recv_up_top, own_up_top), (recv_up_bot, own_up_bot)], [(recv_down, own_down)]],
        "sum_w_out_up_down", _Rider("scatter", [blocks(gw_in_b, IN_WIDTH)], [blocks(gw_in_t, IN_WIDTH)]),
        _na_bias_grad(dtiles.reshape(n_heads * n_tiles, GRID_W, LANES), expand))

    red = late.pop()[:, :2 * n_dc]
    red = red.reshape(n_heads, n_tiles, 2, n_dc)
    zero_row = jnp.zeros((n_heads, 1, n_dc), F32)
    g_rpb = (jnp.concatenate([red[:, :, 0, :], zero_row], axis=1)
             + jnp.concatenate([zero_row, red[:, :, 1, :]], axis=1))
    g_sink = jnp.sum(dsink_parts[:, :, :2, 0], axis=0).reshape(SW_WIDTH // HEAD_DIM)

    dmod = jnp.concatenate([dshift_a, dscale_a, dgate_a, dshift_f, dscale_f, dgate_f], axis=2).reshape(batch, 6 * d)
    rpb_shape = na_rpb.shape[1:]
    states = [(g_attn, m_g_attn, v_g_attn),
              (na_rpb.reshape(rpb_shape), m_na_rpb.reshape(rpb_shape), v_na_rpb.reshape(rpb_shape)),
              (sw_sink, m_sw_sink, v_sw_sink), (g_na_out, m_g_na_out, v_g_na_out), (g_sw_out, m_g_sw_out, v_g_sw_out),
              (g_ffn, m_g_ffn, v_g_ffn), (conv_b, m_conv_b, v_conv_b),
              (g_final.reshape(1, d), m_g_final.reshape(1, d), v_g_final.reshape(1, d))]
    partials = [dg_attn, g_rpb, g_sink.reshape(sw_sink.shape), dg_na, dg_sw, dg_ffn, gconv_b, dg_final,
                gconv_w, loss_part]
    mine = _sum_slots([[(recv_in, own_in)]], "sum_w_in")[0] + late
    small, theirs = _small_sums(partials, dmod, _Rider("swap", mine))
    g_conv_w_full, loss_sum, g_b_ada, dmod_all = small[len(states):]
    r_small = _small_adamw(states + [(b_ada, m_b_ada, v_b_ada)], small[:len(states)] + [g_b_ada])
    loss = loss_sum[0, 0]
    dmod_rows = jnp.pad(dmod_all, ((0, 0), (0, SUBLANES - batch), (0, 0))).reshape(N_DEV * SUBLANES, 6 * d)
    ncol = w_ada.shape[2]
    dmod_cols = lax.dynamic_slice(dmod_rows, (0, shard * ncol), (N_DEV * SUBLANES, ncol))
    cshard = conv_w.shape[2]
    g_conv_w = lax.dynamic_slice(g_conv_w_full, (0, shard * cshard), (3, cshard))

    def big(w, m, v, g_parts, name):
        shape = w.shape
        outs = _adamw(w[0], g_parts, m[0], v[0], name)
        return [o.reshape(shape) for o in outs]

    r_w_ada = [o.reshape(w_ada.shape) for o in
               _adamw(w_ada[0], [], m_w_ada[0], v_w_ada[0], "adamw_w_ada", product=(jnp.transpose(sc_all), dmod_cols))]
    r_w_in = [jnp.transpose(o).reshape(w_in.shape) for o in
              _adamw(jnp.transpose(w_in[0]), [mine[0], theirs[0]], jnp.transpose(m_w_in[0]), jnp.transpose(v_w_in[0]),
                     "adamw_w_in")]
    r_w_out = big(w_out, m_w_out, v_w_out, [mine[1], theirs[1]], "adamw_w_out")
    r_w_up = big(w_up, m_w_up, v_w_up, [mine[2], theirs[2]], "adamw_w_up")
    r_w_down = big(w_down, m_w_down, v_w_down, [mine[3], theirs[3]], "adamw_w_down")

    r_conv_w = big(conv_w, m_conv_w, v_conv_w, [g_conv_w], "adamw_conv_w")

    def pick(k):
        ga_, rpb_, sk_, gna_, gsw_, gf_, cb_, gfin_, b_ = [r[k] for r in r_small]
        return [r_w_ada[k], b_, ga_, r_w_in[k], rpb_.reshape(na_rpb.shape), sk_, gna_, gsw_, r_w_out[k], gf_,
                r_w_up[k], r_conv_w[k], cb_, r_w_down[k], gfin_.reshape(d)]

    return (loss, gx.reshape(batch, seq, d), *pick(0), *pick(1), *pick(2), *pick(3))
```

```python
import jax
import jax.numpy as jnp
import numpy as np
from jax import lax
from jax.experimental import pallas as pl
from jax.experimental.pallas import tpu as pltpu

F32 = jnp.float32
BF16 = jnp.bfloat16
MESH = pl.DeviceIdType.MESH

D_MODEL = 1024
HEAD_DIM = 64
NA_WIDTH = 512
SW_WIDTH = 512
SW_KV_WIDTH = 128
IN_WIDTH = 2304
D_FF = 2816
GRID_W = 64
NA_ROWS = 8
NA_COLS = 16
SW_BLOCK = 128
ROPE_THETA = 10000.0
EPS = 1e-6
NEG = -1e30
QK_SCALE = HEAD_DIM ** -0.5

ADAM_LR = 0.001
ADAM_B1 = 0.9
ADAM_B2 = 0.999
ADAM_EPS = 1e-08
ADAM_WD = 0.01
ADAM_STEP = 10

N_SHARD = 4
N_DEV = 8
LANES = 128
SUBLANES = 8
TOKEN_TILE = 512
FF_TILE = 256
CONV_CHUNK = 512
NA_GROUP = 8
SW_GROUP_BLOCKS = 8
VMEM_BIG = 56 * 1024 * 1024


def _mm(a, b):
    return jnp.dot(a, b, preferred_element_type=F32)


def _mm_nt(a, b):
    return lax.dot_general(a, b, (((1,), (1,)), ((), ())), preferred_element_type=F32)


def _mm_tn(a, b):
    return lax.dot_general(a, b, (((0,), (0,)), ((), ())), preferred_element_type=F32)


def _cparams(sem=None, vmem=None):
    kw = {}
    if sem is not None:
        kw["dimension_semantics"] = sem
    if vmem is not None:
        kw["vmem_limit_bytes"] = vmem
    return pltpu.CompilerParams(**kw)


def _resident(shape):
    return pl.BlockSpec(shape, lambda i: (0,) * len(shape), pipeline_mode=pl.Buffered(1))


def _sigmoid(x):
    return 1.0 / (1.0 + jnp.exp(-x))


def _rms_stats(x):
    r = lax.rsqrt(jnp.mean(x * x, axis=-1, keepdims=True) + EPS)
    return r, x * r


def _rms_bwd(dxn, xn, r):
    return r * (dxn - xn * jnp.mean(dxn * xn, axis=-1, keepdims=True))


def _my_pos():
    return lax.axis_index("x"), lax.axis_index("y"), lax.axis_index("c")


def _flip(v, bit):
    return 1 - v if bit else v


def _ada_forward(c8, w_ada, b_ada, rider, passenger):
    d = c8.shape[1]
    ncol = w_ada.shape[1]
    extra_body, extra_in, extra_out = passenger
    n_extra = len(extra_in)

    def body(c_ref, w_ref, b_ref, *refs):
        extra_refs = refs[:n_extra]
        mod_ref, sc_ref, extra_ref, m_scr, mod_buf, ssem, rsem, ssem2, rsem2 = refs[n_extra:]
        x, y, c = _my_pos()
        me = 4 * x + 2 * y + c
        shard = 2 * x + y
        cv = c_ref[...]
        my_rows = pl.ds(pl.multiple_of(me * SUBLANES, SUBLANES), SUBLANES)
        sc_ref[my_rows, :] = cv * _sigmoid(cv)

        def copy1(k):
            peer = (_flip(x, (k >> 2) & 1), _flip(y, (k >> 1) & 1), _flip(c, k & 1))
            return pltpu.make_async_remote_copy(
                src_ref=sc_ref.at[my_rows, :], dst_ref=sc_ref.at[my_rows, :],
                send_sem=ssem.at[k - 1], recv_sem=rsem.at[k - 1], device_id=peer, device_id_type=MESH)

        sends = [copy1(k) for k in range(1, N_DEV)]
        for cp in sends:
            cp.start()
        extra_body(*extra_refs, extra_ref)
        for cp in sends:
            cp.wait_recv()
        m_scr[...] = _mm(sc_ref[...].astype(BF16), w_ref[...].astype(BF16))

        def copy2(k):
            px, py = _flip(x, (k >> 1) & 1), _flip(y, k & 1)
            rows = pl.ds(pl.multiple_of((4 * px + 2 * py + c) * SUBLANES, SUBLANES), SUBLANES)
            return pltpu.make_async_remote_copy(
                src_ref=m_scr.at[rows, :], dst_ref=mod_buf.at[shard],
                send_sem=ssem2.at[k - 1], recv_sem=rsem2.at[k - 1], device_id=(px, py, c), device_id_type=MESH)

        sends2 = [copy2(k) for k in range(1, N_SHARD)]
        for cp in sends2:
            cp.start()
        mod_buf[shard] = m_scr[my_rows, :]
        for cp in sends2:
            cp.wait_recv()
        for s in range(N_SHARD):
            mod_ref[:, s * ncol:(s + 1) * ncol] = mod_buf[s] + b_ref[:, s * ncol:(s + 1) * ncol]
        for cp in sends + sends2:
            cp.wait_send()

    vm = pl.BlockSpec(memory_space=pltpu.VMEM)
    return _hosted(
        body, rider, name="ada_forward", grid=(),
        out_shape=(jax.ShapeDtypeStruct((SUBLANES, N_SHARD * ncol), F32),
                   jax.ShapeDtypeStruct((N_DEV * SUBLANES, d), F32), extra_out),
        in_specs=[vm] * (3 + n_extra), out_specs=(vm, vm, vm),
        scratch_shapes=[pltpu.VMEM((N_DEV * SUBLANES, ncol), F32), pltpu.VMEM((N_SHARD, SUBLANES, ncol), F32),
                        pltpu.SemaphoreType.DMA((N_DEV - 1,)), pltpu.SemaphoreType.DMA((N_DEV - 1,)),
                        pltpu.SemaphoreType.DMA((N_SHARD - 1,)), pltpu.SemaphoreType.DMA((N_SHARD - 1,))],
        compiler_params=_cparams(vmem=VMEM_BIG), args=[c8, w_ada, b_ada] + extra_in)


class _Rider:
    def __init__(self, kind, srcs, owns=()):
        self.kind, self.srcs, self.owns = kind, list(srcs), list(owns)
        n = len(self.srcs)
        sds = jax.ShapeDtypeStruct
        dma = pltpu.SemaphoreType.DMA
        if kind == "gather":
            self.out_shapes = [sds((N_SHARD,) + s.shape, s.dtype) for s in self.srcs]
            self.sems = [dma((n, N_SHARD - 1)), dma((n, N_SHARD - 1)), dma((n, N_SHARD - 1)), dma((n, N_SHARD - 1)),
                         dma((n,)), dma((n,))]
        elif kind == "scatter":
            self.out_shapes = ([sds((N_SHARD - 1,) + s.shape[1:], s.dtype) for s in self.srcs]
                               + [sds(o.shape[1:], o.dtype) for o in self.owns])
            m = max(len(self.owns), 1)
            self.sems = [dma((n, N_SHARD - 1)), dma((n, N_SHARD - 1)), dma((m,)), dma((m,))]
        else:
            self.out_shapes = [sds(s.shape, s.dtype) for s in self.srcs]
            self.sems = [dma((n,)), dma((n,))]

    @property
    def inputs(self):
        return self.srcs + self.owns

    def _halved(self, i):
        a = self.srcs[i]
        tile_rows = SUBLANES * (4 // jnp.dtype(a.dtype).itemsize)
        return self.kind == "gather" and a.shape[0] % (2 * tile_rows) == 0

    def copies(self, ins, outs, sems):
        n = len(self.srcs)
        x, y, c = _my_pos()
        shard = 2 * x + y
        remote, relay = [], []
        if self.kind == "swap":
            ssem, rsem = sems
            for i in range(n):
                remote.append(pltpu.make_async_remote_copy(
                    src_ref=ins[i], dst_ref=outs[i], send_sem=ssem.at[i], recv_sem=rsem.at[i],
                    device_id=(x, y, 1 - c), device_id_type=MESH))
            return remote, relay
        if self.kind == "gather":
            ssem, rsem, ssem2, rsem2, sib_s, sib_r = sems
        else:
            ssem, rsem, sib_s, sib_r = sems
        for i in range(n):
            if self.kind == "gather":
                remote.append(pltpu.make_async_remote_copy(
                    src_ref=ins[i], dst_ref=outs[i].at[shard], send_sem=sib_s.at[i], recv_sem=sib_r.at[i],
                    device_id=(x, y, 1 - c), device_id_type=MESH))
                half = ins[i].shape[0] // 2
                mine = pl.ds(pl.multiple_of(c * half, half), half) if self._halved(i) else None
            for k in range(1, N_SHARD):
                px, py = _flip(x, (k >> 1) & 1), _flip(y, k & 1)
                if self.kind == "gather":
                    src, dst = ins[i], outs[i].at[shard]
                    if mine is not None:
                        src, dst = src.at[mine], dst.at[mine]
                        got = outs[i].at[2 * px + py].at[mine]
                        relay.append(pltpu.make_async_remote_copy(
                            src_ref=got, dst_ref=got, send_sem=ssem2.at[i, k - 1], recv_sem=rsem2.at[i, k - 1],
                            device_id=(x, y, 1 - c), device_id_type=MESH))
                else:
                    src, dst = ins[i].at[2 * px + py], outs[i].at[k - 1]
                remote.append(pltpu.make_async_remote_copy(
                    src_ref=src, dst_ref=dst, send_sem=ssem.at[i, k - 1], recv_sem=rsem.at[i, k - 1],
                    device_id=(px, py, c), device_id_type=MESH))
        if self.kind == "scatter":
            for i in range(len(self.owns)):
                remote.append(pltpu.make_async_remote_copy(
                    src_ref=ins[n + i].at[shard], dst_ref=outs[n + i], send_sem=sib_s.at[i], recv_sem=sib_r.at[i],
                    device_id=(x, y, 1 - c), device_id_type=MESH))
        return remote, relay

    def start(self, ins, outs, sems):
        remote, _ = self.copies(ins, outs, sems)
        for cp in remote:
            cp.start()

    def wait(self, ins, outs, sems):
        remote, relay = self.copies(ins, outs, sems)
        for cp in remote:
            cp.wait_recv()
        for cp in relay:
            cp.start()
        for cp in relay:
            cp.wait_recv()
        for cp in remote + relay:
            cp.wait_send()


def _hosted(body, rider, *, name, grid, out_shape, in_specs, out_specs, scratch_shapes, compiler_params, args):
    out_shape, out_specs = list(out_shape), list(out_specs)
    if rider is None:
        outs = pl.pallas_call(body, name=name, grid=grid, out_shape=tuple(out_shape), in_specs=list(in_specs),
                              out_specs=tuple(out_specs), scratch_shapes=list(scratch_shapes),
                              compiler_params=compiler_params)(*args)
        return list(outs), []
    n_in, n_out, n_scr = len(in_specs), len(out_shape), len(scratch_shapes)
    nr_in, nr_out = len(rider.inputs), len(rider.out_shapes)
    n_steps = 1
    for size in grid:
        n_steps *= size

    def full(*refs):
        ins, refs = refs[:n_in], refs[n_in:]
        r_in, refs = refs[:nr_in], refs[nr_in:]
        outs, refs = refs[:n_out], refs[n_out:]
        r_out, refs = refs[:nr_out], refs[nr_out:]
        scr, sems = refs[:n_scr], refs[n_scr:]
        if grid:
            step = 0
            for ax, size in enumerate(grid):
                step = step * size + pl.program_id(ax)
            pl.when(step == 0)(lambda: rider.start(r_in, r_out, sems))
            body(*ins, *outs, *scr)
            pl.when(step == n_steps - 1)(lambda: rider.wait(r_in, r_out, sems))
        else:
            rider.start(r_in, r_out, sems)
            body(*ins, *outs, *scr)
            rider.wait(r_in, r_out, sems)

    hbm = pl.BlockSpec(memory_space=pl.ANY)
    res = pl.pallas_call(
        full, name=name, grid=grid, out_shape=tuple(out_shape + rider.out_shapes),
        in_specs=list(in_specs) + [hbm] * nr_in, out_specs=tuple(out_specs + [hbm] * nr_out),
        scratch_shapes=list(scratch_shapes) + rider.sems, compiler_params=compiler_params,
    )(*args, *rider.inputs)
    return list(res[:n_out]), list(res[n_out:])


def _rope_rot(t):
    w = t.shape[1]
    lane = lax.broadcasted_iota(jnp.int32, t.shape, 1)
    first = (lane % HEAD_DIM) < (HEAD_DIM // 2)
    return jnp.where(first, pltpu.roll(t, w - HEAD_DIM // 2, 1), pltpu.roll(t, HEAD_DIM // 2, 1))


def _in_proj(x, mod3, g_attn, w_in_t, cos_t, sin_t, seq, rider=None):
    t, d = x.shape
    tm = 2 * TOKEN_TILE
    per_seq = seq // tm
    rope_lo, rope_hi = 3 * NA_WIDTH, 3 * NA_WIDTH + SW_WIDTH + SW_KV_WIDTH
    n_rep = (rope_hi - rope_lo) // LANES

    def body(x_ref, mod_ref, g_ref, w_ref, cos_ref, sin_ref, h_ref, p_ref):
        r, xn = _rms_stats(x_ref[...])
        shift, scale = mod_ref[0, :, 0:d], mod_ref[0, :, d:2 * d]
        hb = ((xn * g_ref[...]) * (1.0 + scale) + shift).astype(BF16)
        h_ref[...] = hb
        p_ref[:, :rope_lo] = _mm_nt(hb, w_ref[:rope_lo, :]).astype(BF16)
        pr = _mm_nt(hb, w_ref[rope_lo:rope_hi, :])
        cos = jnp.concatenate([cos_ref[...]] * n_rep, axis=1)
        sin = jnp.concatenate([sin_ref[...]] * n_rep, axis=1)
        p_ref[:, rope_lo:rope_hi] = (pr * cos + _rope_rot(pr) * sin).astype(BF16)
        p_ref[:, rope_hi:] = _mm_nt(hb, w_ref[rope_hi:, :]).astype(BF16)

    return _hosted(
        body, rider, name="in_proj", grid=(t // tm,),
        out_shape=[jax.ShapeDtypeStruct((t, d), BF16), jax.ShapeDtypeStruct((t, IN_WIDTH), BF16)],
        in_specs=[pl.BlockSpec((tm, d), lambda i: (i, 0)),
                  pl.BlockSpec((1, 1, 6 * d), lambda i: (i // per_seq, 0, 0)),
                  pl.BlockSpec((1, d), lambda i: (0, 0)),
                  pl.BlockSpec((IN_WIDTH, d), lambda i: (0, 0)),
                  pl.BlockSpec((tm, LANES), lambda i: (i % per_seq, 0)),
                  pl.BlockSpec((tm, LANES), lambda i: (i % per_seq, 0))],
        out_specs=[pl.BlockSpec((tm, d), lambda i: (i, 0)), pl.BlockSpec((tm, IN_WIDTH), lambda i: (i, 0))],
        scratch_shapes=[], compiler_params=_cparams(("arbitrary",), VMEM_BIG),
        args=[x, mod3, g_attn, w_in_t, cos_t, sin_t])


def _na_bias_pattern():
    n_dc = 2 * NA_COLS - 1
    j = np.arange(GRID_W)[:, None]
    m = np.arange(GRID_W * LANES)[None, :]
    q, lane = m // LANES, m % LANES
    k = lane % GRID_W
    cs = np.clip(q - NA_COLS // 2, 0, GRID_W - NA_COLS)
    ok = (k >= cs) & (k < cs + NA_COLS)
    hit = ok & (j < 2 * n_dc) & (lane // GRID_W == j // n_dc) & (k - q + (NA_COLS - 1) == j % n_dc)
    return jnp.asarray(hit.astype(np.float32)), jnp.asarray(np.where(ok, 0.0, NEG).astype(np.float32))


def _na_bias_tiles(rows2, expand, mask):
    n, width = rows2.shape[0], expand.shape[1]
    q_step = 16
    step = q_step * LANES

    def body(r_ref, e_ref, m_ref, o_ref):
        for i in range(width // step):
            at = slice(i * step, (i + 1) * step)
            flat = jnp.dot(r_ref[...], e_ref[:, at], precision=lax.Precision.HIGHEST,
                           preferred_element_type=F32) + m_ref[:, at]
            for qq in range(q_step):
                o_ref[:, i * q_step + qq, :] = flat[:, qq * LANES:(qq + 1) * LANES]

    return body, [rows2, expand, mask], jax.ShapeDtypeStruct((n, GRID_W, LANES), F32)


def _na_prepare(k_ref, v_ref, km, vm):
    lane = lax.broadcasted_iota(jnp.int32, k_ref.shape, 1)
    low = lane < HEAD_DIM
    kv = k_ref[...]
    vv = v_ref[...]
    zero = jnp.zeros_like(kv)
    km[0] = jnp.where(low, kv, zero)
    km[1] = jnp.where(low, zero, kv)
    vm[0] = jnp.where(low, vv, zero)
    vm[1] = jnp.where(low, zero, vv)


def _na_window(r, n_rows):
    rs = jnp.clip(r - NA_ROWS // 2, 0, n_rows - NA_ROWS)
    return rs, r - rs


def _na_pair_window(ref, wrows):
    return jnp.concatenate([ref[0, wrows, :], ref[1, wrows, :]], axis=0)


def _na_scores(q, k2, tp_ref, off):
    bias = jnp.concatenate([tp_ref[h, 2 * w - off + (NA_ROWS - 1)] for h in range(2) for w in range(NA_ROWS // 2)],
                           axis=1)
    return _mm_nt(q, k2) * QK_SCALE + bias


def _pair_lse_block(lse):
    lane = lax.broadcasted_iota(jnp.int32, (lse[0].shape[0], LANES), 1)
    return jnp.where(lane < HEAD_DIM, lse[0], lse[1])


def _pair_softmax(s):
    win = s.shape[1] // 2
    halves, lse = [], []
    for h in range(2):
        sh = s[:, h * win:(h + 1) * win]
        m = jnp.max(sh, axis=-1, keepdims=True)
        e = jnp.exp(sh - m)
        l = jnp.sum(e, axis=-1, keepdims=True)
        halves.append(e / l)
        lse.append(m + jnp.log(l))
    return jnp.concatenate(halves, axis=1), _pair_lse_block(lse)


def _pair_grad(w2, x, low):
    keys = w2.shape[1] // 2
    zero = jnp.zeros_like(x)
    low_x = low[:x.shape[0]]
    stacked = jnp.concatenate([w2[:, :keys], w2[:, keys:]], axis=0)
    diag = jnp.concatenate([jnp.where(low_x, x, zero), jnp.where(low_x, zero, x)], axis=0)
    return _mm_tn(stacked, diag)


def _pair_probs_from_lse(s, lse_block):
    win = s.shape[1] // 2
    return jnp.concatenate([jnp.exp(s[:, h * win:(h + 1) * win] - lse_block[:, h * HEAD_DIM:h * HEAD_DIM + 1])
                            for h in range(2)], axis=1)


def _na_forward(proj, tiles, batch, seq, rider=None):
    t = proj.shape[0]
    n_rows = seq // GRID_W
    n_pairs = NA_WIDTH // LANES
    win = NA_ROWS * GRID_W

    def body(q_ref, k_ref, v_ref, tp_ref, o_ref, lse_ref, km, vm):
        _na_prepare(k_ref, v_ref, km, vm)

        def scores(r):
            rs, off = _na_window(r, n_rows)
            rows = pl.ds(pl.multiple_of(r * GRID_W, GRID_W), GRID_W)
            wrows = pl.ds(pl.multiple_of(rs * GRID_W, GRID_W), win)
            return rows, wrows, _na_scores(q_ref[rows, :], _na_pair_window(km, wrows), tp_ref, off)

        def finish(rows, wrows, s):
            p, lse = _pair_softmax(s)
            lse_ref[rows, :] = lse
            o_ref[rows, :] = _mm(p.astype(BF16), _na_pair_window(vm, wrows))

        def row_group(i, carry):
            for state in [scores(NA_GROUP * i + j) for j in range(NA_GROUP)]:
                finish(*state)
            return carry

        lax.fori_loop(0, n_rows // NA_GROUP, row_group, 0)

    return _hosted(
        body, rider, name="na_forward", grid=(batch, n_pairs),
        out_shape=[jax.ShapeDtypeStruct((t, NA_WIDTH), F32), jax.ShapeDtypeStruct((t, NA_WIDTH), F32)],
        in_specs=[pl.BlockSpec((seq, LANES), lambda b, p: (b, p)),
                  pl.BlockSpec((seq, LANES), lambda b, p: (b, n_pairs + p)),
                  pl.BlockSpec((seq, LANES), lambda b, p: (b, 2 * n_pairs + p)),
                  pl.BlockSpec((2, 2 * NA_ROWS - 2, GRID_W, LANES), lambda b, p: (p, 0, 0, 0))],
        out_specs=[pl.BlockSpec((seq, LANES), lambda b, p: (b, p)), pl.BlockSpec((seq, LANES), lambda b, p: (b, p))],
        scratch_shapes=[pltpu.VMEM((2, seq, LANES), BF16), pltpu.VMEM((2, seq, LANES), BF16)],
        compiler_params=_cparams(("arbitrary", "arbitrary")), args=[proj, proj, proj, tiles])


def _sw_prepare(kv_ref, g, dst_lo, dst_hi, seq):
    lane = lax.broadcasted_iota(jnp.int32, kv_ref.shape, 1)
    mine = (lane // HEAD_DIM) == g
    kg = jnp.where(mine, kv_ref[...].astype(F32), 0.0)
    kr = pltpu.roll(kg, HEAD_DIM, 1)
    first = g == 0
    zero = jnp.zeros((SW_BLOCK, LANES), BF16)
    for dst, val in ((dst_lo, jnp.where(first, kg, kr)), (dst_hi, jnp.where(first, kr, kg))):
        dst[0:SW_BLOCK, :] = zero
        dst[SW_BLOCK:SW_BLOCK + seq, :] = val.astype(BF16)
        dst[SW_BLOCK + seq:, :] = zero


def _sw_mask(n, seq):
    qi = lax.broadcasted_iota(jnp.int32, (SW_BLOCK, 3 * SW_BLOCK), 0)
    kj = lax.broadcasted_iota(jnp.int32, (SW_BLOCK, 3 * SW_BLOCK), 1)
    kpos = n * SW_BLOCK - SW_BLOCK + kj
    return (jnp.abs(qi + SW_BLOCK - kj) <= SW_BLOCK) & (kpos >= 0) & (kpos < seq)


def _sw_probs(s2, ok, sinks):
    band = s2.shape[1] // 2
    halves, lse = [], []
    for i in range(2):
        s = jnp.where(ok, s2[:, i * band:(i + 1) * band], NEG)
        m = jnp.maximum(jnp.max(s, axis=-1, keepdims=True), sinks[i])
        p = jnp.exp(s - m)
        den = jnp.sum(p, axis=-1, keepdims=True) + jnp.exp(sinks[i] - m)
        halves.append(p / den)
        lse.append(m + jnp.log(den))
    return jnp.concatenate(halves, axis=1), _pair_lse_block(lse)


def _sw_probs_from_lse(s2, ok, sinks, lse_block):
    band = s2.shape[1] // 2
    halves, sink_p = [], []
    for i in range(2):
        lse = lse_block[:, i * HEAD_DIM:i * HEAD_DIM + 1]
        halves.append(jnp.exp(jnp.where(ok, s2[:, i * band:(i + 1) * band], NEG) - lse))
        sink_p.append(jnp.exp(sinks[i] - lse))
    return jnp.concatenate(halves, axis=1), sink_p


def _sw_forward(proj, sink, batch, seq, rider=None):
    t = proj.shape[0]
    n_pairs = SW_WIDTH // LANES
    q_blk = 3 * NA_WIDTH // LANES
    k_blk = q_blk + n_pairs
    n_blocks = seq // SW_BLOCK
    pad = seq + 2 * SW_BLOCK

    def body(sink_ref, q_ref, k_ref, v_ref, o_ref, lse_ref, k_lo, k_hi, v_lo, v_hi):
        hp = pl.program_id(1)
        g = hp // 2

        @pl.when(hp % 2 == 0)
        def _():
            _sw_prepare(k_ref, g, k_lo, k_hi, seq)
            _sw_prepare(v_ref, g, v_lo, v_hi, seq)

        sinks = (sink_ref[2 * hp], sink_ref[2 * hp + 1])

        def scores(n):
            rows = pl.ds(pl.multiple_of(n * SW_BLOCK, SW_BLOCK), SW_BLOCK)
            wrows = pl.ds(pl.multiple_of(n * SW_BLOCK, SW_BLOCK), 3 * SW_BLOCK)
            k2 = jnp.concatenate([k_lo[wrows, :], k_hi[wrows, :]], axis=0)
            return n, rows, wrows, _mm_nt(q_ref[rows, :], k2) * QK_SCALE

        def finish(n, rows, wrows, s2):
            p, lse = _sw_probs(s2, _sw_mask(n, seq), sinks)
            lse_ref[rows, :] = lse
            v2 = jnp.concatenate([v_lo[wrows, :], v_hi[wrows, :]], axis=0)
            o_ref[rows, :] = _mm(p.astype(BF16), v2)

        def block_group(i, carry):
            for state in [scores(SW_GROUP_BLOCKS * i + j) for j in range(SW_GROUP_BLOCKS)]:
                finish(*state)
            return carry

        lax.fori_loop(0, n_blocks // SW_GROUP_BLOCKS, block_group, 0)

    return _hosted(
        body, rider, name="sw_forward", grid=(batch, n_pairs),
        out_shape=[jax.ShapeDtypeStruct((t, SW_WIDTH), F32), jax.ShapeDtypeStruct((t, SW_WIDTH), F32)],
        in_specs=[pl.BlockSpec(memory_space=pltpu.SMEM),
                  pl.BlockSpec((seq, LANES), lambda b, p: (b, q_blk + p)),
                  pl.BlockSpec((seq, LANES), lambda b, p: (b, k_blk)),
                  pl.BlockSpec((seq, LANES), lambda b, p: (b, k_blk + 1))],
        out_specs=[pl.BlockSpec((seq, LANES), lambda b, p: (b, p)), pl.BlockSpec((seq, LANES), lambda b, p: (b, p))],
        scratch_shapes=[pltpu.VMEM((pad, LANES), BF16)] * 4,
        compiler_params=_cparams(("arbitrary", "arbitrary")), args=[sink, proj, proj, proj])


def _out_proj(oa, ob, g_na, g_sw, w_out, x, mod3, g_ffn, seq):
    t, d = x.shape
    tm = TOKEN_TILE
    per_seq = seq // tm

    def body(oa_ref, ob_ref, gna_ref, gsw_ref, w_ref, x_ref, mod_ref, gf_ref, oab_ref, mix_ref, x1_ref, h2_ref):
        _, na = _rms_stats(oa_ref[...])
        _, nb = _rms_stats(ob_ref[...])
        oab = jnp.concatenate([na * gna_ref[...], nb * gsw_ref[...]], axis=1).astype(BF16)
        oab_ref[...] = oab
        mix = _mm(oab, w_ref[...])
        mix_ref[...] = mix
        gate_a = mod_ref[0, :, 2 * d:3 * d]
        shift_f, scale_f = mod_ref[0, :, 3 * d:4 * d], mod_ref[0, :, 4 * d:5 * d]
        x1 = x_ref[...] + gate_a * mix
        x1_ref[...] = x1
        _, xn = _rms_stats(x1)
        h2_ref[...] = ((xn * gf_ref[...]) * (1.0 + scale_f) + shift_f).astype(BF16)

    tile = lambda w: pl.BlockSpec((tm, w), lambda i: (i, 0))
    vec = lambda w: pl.BlockSpec((1, w), lambda i: (0, 0))
    return pl.pallas_call(
        body, name="out_proj", grid=(t // tm,),
        out_shape=(jax.ShapeDtypeStruct((t, d), BF16), jax.ShapeDtypeStruct((t, d), F32),
                   jax.ShapeDtypeStruct((t, d), F32), jax.ShapeDtypeStruct((t, d), BF16)),
        in_specs=[tile(NA_WIDTH), tile(SW_WIDTH), vec(NA_WIDTH), vec(SW_WIDTH),
                  pl.BlockSpec((d, d), lambda i: (0, 0)), tile(d),
                  pl.BlockSpec((1, 1, 6 * d), lambda i: (i // per_seq, 0, 0)), vec(d)],
        out_specs=(tile(d), tile(d), tile(d), tile(d)),
        compiler_params=_cparams(("arbitrary",), VMEM_BIG),
    )(oa, ob, g_na, g_sw, w_out, x, mod3, g_ffn)


def _up_proj(h2, w_up_halves, rider=None):
    t, d = h2.shape
    tm = 2 * TOKEN_TILE
    w_a, w_b = w_up_halves
    half, wcol = w_a.shape[1], w_a.shape[2]

    def body(h_ref, wa_ref, wb_ref, u_ref):
        u_ref[0] = (_mm(h_ref[:, :half], wa_ref[0]) + _mm(h_ref[:, half:], wb_ref[0])).astype(BF16)

    w_spec = pl.BlockSpec((1, half, wcol), lambda j, i: (j, 0, 0))
    return _hosted(
        body, rider, name="up_proj", grid=(N_SHARD, t // tm),
        out_shape=[jax.ShapeDtypeStruct((2, t, D_FF), BF16)],
        in_specs=[pl.BlockSpec((tm, d), lambda j, i: (i, 0)), w_spec, w_spec],
        out_specs=[pl.BlockSpec((1, tm, wcol), lambda j, i: (j // 2, i, j % 2))],
        scratch_shapes=[], compiler_params=_cparams(("arbitrary", "arbitrary"), VMEM_BIG), args=[h2, w_a, w_b])


def _taps_chunk(load, s, rows, seq):
    halo = 2 * SUBLANES
    cur = load(s, rows)
    above = load(pl.multiple_of(jnp.maximum(s - halo, 0), halo), halo)
    below = load(pl.multiple_of(jnp.minimum(s + rows, seq - halo), halo), halo)
    up = jnp.where(s > 0, above[halo - 1:halo, :], 0.0)
    dn = jnp.where(s + rows < seq, below[0:1, :], 0.0)
    row = lax.broadcasted_iota(jnp.int32, cur.shape, 0)
    prev = jnp.where(row == 0, up, pltpu.roll(cur, 1, 0))
    nxt = jnp.where(row == rows - 1, dn, pltpu.roll(cur, rows - 1, 0))
    return cur, prev, nxt


def _conv_gate(u, conv_w, conv_b, batch, seq, rider=None):
    t = u.shape[1]
    cw = FF_TILE
    rows = CONV_CHUNK

    def body(u_ref, w_ref, b_ref, a_ref):
        def chunk(i, carry):
            s = pl.multiple_of(i * rows, rows)
            gt, prev, nxt = _taps_chunk(lambda at, n: u_ref[1, pl.ds(at, n), :].astype(F32), s, rows, seq)
            gc = prev * w_ref[0:1, :] + gt * w_ref[1:2, :] + nxt * w_ref[2:3, :] + b_ref[...]
            a_ref[pl.ds(s, rows), :] = ((gc * _sigmoid(gc)) * u_ref[0, pl.ds(s, rows), :].astype(F32)).astype(BF16)
            return carry

        lax.fori_loop(0, seq // rows, chunk, 0)

    return _hosted(
        body, rider, name="conv_gate", grid=(batch, D_FF // cw),
        out_shape=[jax.ShapeDtypeStruct((t, D_FF), BF16)],
        in_specs=[pl.BlockSpec((2, seq, cw), lambda b, j: (0, b, j)),
                  pl.BlockSpec((3, cw), lambda b, j: (0, j)), pl.BlockSpec((1, cw), lambda b, j: (0, j))],
        out_specs=[pl.BlockSpec((seq, cw), lambda b, j: (b, j))], scratch_shapes=[],
        compiler_params=_cparams(("arbitrary", "arbitrary"), VMEM_BIG), args=[u, conv_w, conv_b])


def _down_and_loss(a, w_down, x1, mod3, g_final, target, seq):
    t, d = x1.shape
    tm = TOKEN_TILE
    per_seq = seq // tm
    batch = t // seq

    def body(a_ref, w_ref, x1_ref, mod_ref, g_ref, tgt_ref, dx2_ref, dffn_ref, loss_ref, dgate_ref, dg_ref):
        i = pl.program_id(0)
        f = _mm(a_ref[...], w_ref[...])
        gate_f = mod_ref[0, :, 5 * d:6 * d]
        x2 = x1_ref[...] + gate_f * f
        r, xn = _rms_stats(x2)
        err = xn * g_ref[...] - tgt_ref[...]
        part = 0.5 * jnp.sum(jnp.mean(err * err, axis=-1, keepdims=True))
        dy = err / d
        dx2 = _rms_bwd(dy * g_ref[...], xn, r)
        dx2_ref[...] = dx2
        dffn_ref[...] = (dx2 * gate_f).astype(BF16)

        @pl.when(i == 0)
        def _():
            loss_ref[...] = jnp.zeros_like(loss_ref)
            dg_ref[...] = jnp.zeros_like(dg_ref)

        @pl.when(i % per_seq == 0)
        def _():
            dgate_ref[...] = jnp.zeros_like(dgate_ref)

        loss_ref[...] += part
        dg_ref[...] += jnp.sum(dy * xn, axis=0, keepdims=True)
        dgate_ref[0] += jnp.sum(dx2 * f, axis=0, keepdims=True)

    tile = lambda w: pl.BlockSpec((tm, w), lambda i: (i, 0))
    return pl.pallas_call(
        body, name="down_loss", grid=(t // tm,),
        out_shape=(jax.ShapeDtypeStruct((t, d), F32), jax.ShapeDtypeStruct((t, d), BF16),
                   jax.ShapeDtypeStruct((SUBLANES, LANES), F32), jax.ShapeDtypeStruct((batch, 1, d), F32),
                   jax.ShapeDtypeStruct((1, d), F32)),
        in_specs=[tile(D_FF), _resident((D_FF, d)), tile(d),
                  pl.BlockSpec((1, 1, 6 * d), lambda i: (i // per_seq, 0, 0)),
                  pl.BlockSpec((1, d), lambda i: (0, 0)), tile(d)],
        out_specs=(tile(d), tile(d), pl.BlockSpec((SUBLANES, LANES), lambda i: (0, 0)),
                   pl.BlockSpec((1, 1, d), lambda i: (i // per_seq, 0, 0)), pl.BlockSpec((1, d), lambda i: (0, 0))),
        compiler_params=_cparams(("arbitrary",), VMEM_BIG),
    )(a, w_down, x1, mod3, g_final, target)


def _down_weight_grad(a, dffn):
    t, dff = a.shape
    d = dffn.shape[1]
    tk = 2 * TOKEN_TILE
    n_k = t // tk

    def body(a_ref, df_ref, g_ref, gb_ref):
        k = pl.program_id(0)

        @pl.when(k == 0)
        def _():
            g_ref[...] = jnp.zeros_like(g_ref)

        g_ref[...] += _mm_tn(a_ref[...], df_ref[...])

        @pl.when(k == n_k - 1)
        def _():
            gb_ref[...] = g_ref[...].astype(BF16)

    whole = _resident((dff, d))
    return pl.pallas_call(
        body, name="down_weight_grad", grid=(n_k,),
        out_shape=(jax.ShapeDtypeStruct((dff, d), F32), jax.ShapeDtypeStruct((dff, d), BF16)),
        in_specs=[pl.BlockSpec((tk, dff), lambda k: (k, 0)), pl.BlockSpec((tk, d), lambda k: (k, 0))],
        out_specs=(whole, whole),
        compiler_params=_cparams(("arbitrary",), VMEM_BIG),
    )(a, dffn)


def _ffn_backward(dffn, w_down, u, conv_w, conv_b, batch, seq, rider=None):
    t, d = dffn.shape
    cw = FF_TILE
    rows = CONV_CHUNK

    def body(df_ref, wd_ref, u_ref, w_ref, b_ref, du_ref, gcw_ref, gcb_ref, da_scr, dgc_scr):
        b = pl.program_id(1)
        da_scr[...] = _mm_nt(df_ref[...], wd_ref[...])

        @pl.when(b == 0)
        def _():
            gcw_ref[...] = jnp.zeros_like(gcw_ref)
            gcb_ref[...] = jnp.zeros_like(gcb_ref)

        def fold(v):
            return jnp.sum(v.reshape(rows // SUBLANES, SUBLANES, cw), axis=0)

        def chunk(i, carry):
            s = pl.multiple_of(i * rows, rows)
            here = pl.ds(s, rows)
            gt, prev, nxt = _taps_chunk(lambda at, n: u_ref[1, pl.ds(at, n), :].astype(F32), s, rows, seq)
            val, da = u_ref[0, here, :].astype(F32), da_scr[here, :]
            gc = prev * w_ref[0:1, :] + gt * w_ref[1:2, :] + nxt * w_ref[2:3, :] + b_ref[...]
            sg = _sigmoid(gc)
            sl = gc * sg
            du_ref[0, here, :] = (da * sl).astype(BF16)
            dgc = (da * val) * (sg * (1.0 + gc * (1.0 - sg)))
            dgc_scr[here, :] = dgc
            cb, c0, c1, c2 = carry
            return cb + fold(dgc), c0 + fold(dgc * prev), c1 + fold(dgc * gt), c2 + fold(dgc * nxt)

        zero = jnp.zeros((SUBLANES, cw), F32)
        cb, c0, c1, c2 = lax.fori_loop(0, seq // rows, chunk, (zero, zero, zero, zero))
        gcb_ref[...] += jnp.sum(cb, axis=0, keepdims=True)
        gcw_ref[0:1, :] += jnp.sum(c0, axis=0, keepdims=True)
        gcw_ref[1:2, :] += jnp.sum(c1, axis=0, keepdims=True)
        gcw_ref[2:3, :] += jnp.sum(c2, axis=0, keepdims=True)

        def chunk2(i, carry):
            s = pl.multiple_of(i * rows, rows)
            dgc, dprev, dnxt = _taps_chunk(lambda at, n: dgc_scr[pl.ds(at, n), :], s, rows, seq)
            du_ref[1, pl.ds(s, rows), :] = (dnxt * w_ref[0:1, :] + dgc * w_ref[1:2, :]
                                            + dprev * w_ref[2:3, :]).astype(BF16)
            return carry

        lax.fori_loop(0, seq // rows, chunk2, 0)

    return _hosted(
        body, rider, name="ffn_backward", grid=(D_FF // cw, batch),
        out_shape=[jax.ShapeDtypeStruct((2, t, D_FF), BF16),
                   jax.ShapeDtypeStruct((3, D_FF), F32), jax.ShapeDtypeStruct((1, D_FF), F32)],
        in_specs=[pl.BlockSpec((seq, d), lambda j, b: (b, 0)), pl.BlockSpec((cw, d), lambda j, b: (j, 0)),
                  pl.BlockSpec((2, seq, cw), lambda j, b: (0, b, j)),
                  pl.BlockSpec((3, cw), lambda j, b: (0, j)), pl.BlockSpec((1, cw), lambda j, b: (0, j))],
        out_specs=[pl.BlockSpec((2, seq, cw), lambda j, b: (0, b, j)),
                   pl.BlockSpec((3, cw), lambda j, b: (0, j)), pl.BlockSpec((1, cw), lambda j, b: (0, j))],
        scratch_shapes=[pltpu.VMEM((seq, cw), F32), pltpu.VMEM((seq, cw), F32)],
        compiler_params=_cparams(("arbitrary", "arbitrary"), VMEM_BIG), args=[dffn, w_down, u, conv_w, conv_b])


def _up_backward(du, w_up, x1, mod3, g_ffn, dx2, mix, seq, rider=None):
    _, t, _ = du.shape
    d = x1.shape[1]
    tm = TOKEN_TILE
    per_seq = seq // tm
    batch = t // seq
    w_a, w_b = w_up
    half, wcol = w_a.shape[1], w_a.shape[2]

    def body(du_ref, wa_ref, wb_ref, x1_ref, mod_ref, g_ref, dx2_ref, mix_ref,
             dx1_ref, dmix_ref, dsh_ref, dsc_ref, dga_ref, dg_ref):
        i = pl.program_id(0)
        parts = []
        for w_ref in (wa_ref, wb_ref):
            acc = jnp.zeros((tm, half), F32)
            for j in range(N_SHARD):
                acc = acc + _mm_nt(du_ref[j // 2, :, (j % 2) * wcol:(j % 2 + 1) * wcol], w_ref[j])
            parts.append(acc)
        dh = jnp.concatenate(parts, axis=1)
        gate_a = mod_ref[0, :, 2 * d:3 * d]
        scale_f = mod_ref[0, :, 4 * d:5 * d]
        r, xn = _rms_stats(x1_ref[...])
        xg = xn * g_ref[...]
        dxg = dh * (1.0 + scale_f)
        dx1 = dx2_ref[...] + _rms_bwd(dxg * g_ref[...], xn, r)
        dx1_ref[...] = dx1
        dmix_ref[...] = (dx1 * gate_a).astype(BF16)

        @pl.when(i == 0)
        def _():
            dg_ref[...] = jnp.zeros_like(dg_ref)

        @pl.when(i % per_seq == 0)
        def _():
            dsh_ref[...] = jnp.zeros_like(dsh_ref)
            dsc_ref[...] = jnp.zeros_like(dsc_ref)
            dga_ref[...] = jnp.zeros_like(dga_ref)

        dg_ref[...] += jnp.sum(dxg * xn, axis=0, keepdims=True)
        dsh_ref[0] += jnp.sum(dh, axis=0, keepdims=True)
        dsc_ref[0] += jnp.sum(dh * xg, axis=0, keepdims=True)
        dga_ref[0] += jnp.sum(dx1 * mix_ref[...], axis=0, keepdims=True)

    tile = lambda w: pl.BlockSpec((tm, w), lambda i: (i, 0))
    per_b = pl.BlockSpec((1, 1, d), lambda i: (i // per_seq, 0, 0))
    small = jax.ShapeDtypeStruct((batch, 1, d), F32)
    return _hosted(
        body, rider, name="up_backward", grid=(t // tm,),
        out_shape=[jax.ShapeDtypeStruct((t, d), F32), jax.ShapeDtypeStruct((t, d), BF16), small, small, small,
                   jax.ShapeDtypeStruct((1, d), F32)],
        in_specs=[pl.BlockSpec((2, tm, D_FF), lambda i: (0, i, 0)),
                  _resident((N_SHARD, half, wcol)), _resident((N_SHARD, half, wcol)), tile(d),
                  pl.BlockSpec((1, 1, 6 * d), lambda i: (i // per_seq, 0, 0)),
                  pl.BlockSpec((1, d), lambda i: (0, 0)), tile(d), tile(d)],
        out_specs=[tile(d), tile(d), per_b, per_b, per_b, pl.BlockSpec((1, d), lambda i: (0, 0))],
        scratch_shapes=[], compiler_params=_cparams(("arbitrary",), VMEM_BIG),
        args=[du, w_a, w_b, x1, mod3, g_ffn, dx2, mix])


def _up_weight_grad(h2, du, rider=None):
    t, d = h2.shape
    tk = 2 * TOKEN_TILE
    wcol = D_FF // 2
    half = d // 2
    n_k = t // tk

    def body(h_ref, du_ref, ga_ref, gb_ref, ga16_ref, gb16_ref):
        k = pl.program_id(1)

        @pl.when(k == 0)
        def _():
            ga_ref[...] = jnp.zeros_like(ga_ref)
            gb_ref[...] = jnp.zeros_like(gb_ref)

        du = du_ref[0]
        ga_ref[0] += _mm_tn(h_ref[:, :half], du)
        gb_ref[0] += _mm_tn(h_ref[:, half:], du)

        @pl.when(k == n_k - 1)
        def _():
            ga16_ref[...] = ga_ref[...].astype(BF16)
            gb16_ref[...] = gb_ref[...].astype(BF16)

    g_spec = pl.BlockSpec((1, half, wcol), lambda j, k: (j, 0, 0))
    f32_out = jax.ShapeDtypeStruct((N_SHARD, half, wcol), F32)
    b16_out = jax.ShapeDtypeStruct((N_SHARD, half, wcol), BF16)
    return _hosted(
        body, rider, name="up_weight_grad", grid=(N_SHARD, n_k),
        out_shape=[f32_out, f32_out, b16_out, b16_out],
        in_specs=[pl.BlockSpec((tk, d), lambda j, k: (k, 0)),
                  pl.BlockSpec((1, tk, wcol), lambda j, k: (j // 2, k, j % 2))],
        out_specs=[g_spec, g_spec, g_spec, g_spec], scratch_shapes=[],
        compiler_params=_cparams(("arbitrary", "arbitrary"), VMEM_BIG), args=[h2, du])


def _out_backward(dmix, w_out, oab, oa, ob, g_na, g_sw):
    t, d = dmix.shape
    tm = 2 * TOKEN_TILE
    hw = NA_WIDTH

    def body(dm_ref, w_ref, oab_ref, oa_ref, ob_ref, gna_ref, gsw_ref,
             doa_ref, dob_ref, gw_ref, gwb_ref, dgna_ref, dgsw_ref):
        @pl.when(pl.program_id(0) == 0)
        def _():
            gw_ref[...] = jnp.zeros_like(gw_ref)
            dgna_ref[...] = jnp.zeros_like(dgna_ref)
            dgsw_ref[...] = jnp.zeros_like(dgsw_ref)

        dm = dm_ref[...]
        gw_ref[...] += _mm_tn(oab_ref[...], dm)

        @pl.when(pl.program_id(0) == t // tm - 1)
        def _():
            gwb_ref[...] = gw_ref[...].astype(BF16)

        do = _mm_nt(dm, w_ref[...])
        for raw_ref, g_ref, dst_ref, dg_ref, lo in ((oa_ref, gna_ref, doa_ref, dgna_ref, 0),
                                                     (ob_ref, gsw_ref, dob_ref, dgsw_ref, hw)):
            r, xn = _rms_stats(raw_ref[...])
            dpart = do[:, lo:lo + hw]
            dg_ref[...] += jnp.sum(dpart * xn, axis=0, keepdims=True)
            dst_ref[...] = _rms_bwd(dpart * g_ref[...], xn, r).astype(BF16)

    tile = lambda w: pl.BlockSpec((tm, w), lambda i: (i, 0))
    vec = lambda w: pl.BlockSpec((1, w), lambda i: (0, 0))
    return pl.pallas_call(
        body, name="out_backward", grid=(t // tm,),
        out_shape=(jax.ShapeDtypeStruct((t, hw), BF16), jax.ShapeDtypeStruct((t, hw), BF16),
                   jax.ShapeDtypeStruct((d, d), F32), jax.ShapeDtypeStruct((d, d), BF16),
                   jax.ShapeDtypeStruct((1, hw), F32), jax.ShapeDtypeStruct((1, hw), F32)),
        in_specs=[tile(d), pl.BlockSpec((d, d), lambda i: (0, 0)), tile(d), tile(hw), tile(hw), vec(hw), vec(hw)],
        out_specs=(tile(hw), tile(hw), pl.BlockSpec((d, d), lambda i: (0, 0)), pl.BlockSpec((d, d), lambda i: (0, 0)),
                   vec(hw), vec(hw)),
        compiler_params=_cparams(("arbitrary",), VMEM_BIG),
    )(dmix, w_out, oab, oa, ob, g_na, g_sw)


def _na_backward(proj, d_o, lse, tiles, batch, seq, rider=None):
    t = proj.shape[0]
    n_rows = seq // GRID_W
    n_pairs = NA_WIDTH // LANES
    win = NA_ROWS * GRID_W
    n_tiles = 2 * NA_ROWS - 2

    def body(q_ref, k_ref, v_ref, do_ref, lse_ref, tp_ref, dq_ref, dk_ref, dv_ref, dtp_ref, km, vm, dk_acc, dv_acc):
        @pl.when(pl.program_id(1) == 0)
        def _():
            dtp_ref[...] = jnp.zeros_like(dtp_ref)

        _na_prepare(k_ref, v_ref, km, vm)
        dk_acc[...] = jnp.zeros_like(dk_acc)
        dv_acc[...] = jnp.zeros_like(dv_acc)
        low = lax.broadcasted_iota(jnp.int32, (win, LANES), 1) < HEAD_DIM

        def scores(r):
            rs, off = _na_window(r, n_rows)
            rows = pl.ds(pl.multiple_of(r * GRID_W, GRID_W), GRID_W)
            wrows = pl.ds(pl.multiple_of(rs * GRID_W, GRID_W), win)
            q, do = q_ref[rows, :], do_ref[rows, :]
            k2 = _na_pair_window(km, wrows)
            s = _na_scores(q, k2, tp_ref, off)
            dp = _mm_nt(do, _na_pair_window(vm, wrows))
            return rows, wrows, off, q, do, k2, s, dp

        def finish(rows, wrows, off, q, do, k2, s, dp):
            p = _pair_probs_from_lse(s, lse_ref[rows, :])
            parts = []
            for h in range(2):
                ph, dph = p[:, h * win:(h + 1) * win], dp[:, h * win:(h + 1) * win]
                dsh = ph * (dph - jnp.sum(ph * dph, axis=-1, keepdims=True))
                for w in range(NA_ROWS // 2):
                    dtp_ref[h, 2 * w - off + (NA_ROWS - 1)] += dsh[:, w * LANES:(w + 1) * LANES]
                parts.append(dsh)
            dsb = (jnp.concatenate(parts, axis=1) * QK_SCALE).astype(BF16)
            dq_ref[rows, :] = _mm(dsb, k2).astype(BF16)
            dk_acc[wrows, :] += _pair_grad(dsb, q, low)
            dv_acc[wrows, :] += _pair_grad(p.astype(BF16), do, low)

        def row_group(i, carry):
            for state in [scores(NA_GROUP * i + j) for j in range(NA_GROUP)]:
                finish(*state)
            return carry

        lax.fori_loop(0, n_rows // NA_GROUP, row_group, 0)
        dk_ref[...] = dk_acc[...].astype(BF16)
        dv_ref[...] = dv_acc[...].astype(BF16)

    blk = lambda off: pl.BlockSpec((seq, LANES), lambda p, b: (b, off + p))
    out = jax.ShapeDtypeStruct((t, NA_WIDTH), BF16)
    return _hosted(
        body, rider, name="na_backward", grid=(n_pairs, batch),
        out_shape=[out, out, out, jax.ShapeDtypeStruct(tiles.shape, F32)],
        in_specs=[blk(0), blk(n_pairs), blk(2 * n_pairs), blk(0), blk(0),
                  pl.BlockSpec((2, n_tiles, GRID_W, LANES), lambda p, b: (p, 0, 0, 0))],
        out_specs=[blk(0), blk(0), blk(0), pl.BlockSpec((2, n_tiles, GRID_W, LANES), lambda p, b: (p, 0, 0, 0))],
        scratch_shapes=[pltpu.VMEM((2, seq, LANES), BF16), pltpu.VMEM((2, seq, LANES), BF16),
                        pltpu.VMEM((seq, LANES), F32), pltpu.VMEM((seq, LANES), F32)],
        compiler_params=_cparams(("arbitrary", "arbitrary")), args=[proj, proj, proj, d_o, lse, tiles])


def _na_bias_grad(dtiles, expand):
    n = dtiles.shape[0]

    def body(t_ref, e_ref, o_ref):
        flat = jnp.concatenate([t_ref[:, qq, :] for qq in range(GRID_W)], axis=1)
        o_ref[...] = lax.dot_general(flat, e_ref[...], (((1,), (1,)), ((), ())),
                                     precision=lax.Precision.HIGHEST, preferred_element_type=F32)

    return body, [dtiles, expand], jax.ShapeDtypeStruct((n, expand.shape[0]), F32)


def _sw_backward(proj, d_o, lse, sink, batch, seq, rider=None):
    t = proj.shape[0]
    n_pairs = SW_WIDTH // LANES
    q_blk = 3 * NA_WIDTH // LANES
    k_blk = q_blk + n_pairs
    n_blocks = seq // SW_BLOCK
    pad = seq + 2 * SW_BLOCK

    def body(sink_ref, q_ref, k_ref, v_ref, do_ref, lse_ref, dq_ref, dk_ref, dv_ref, dsk_ref,
             k_lo, k_hi, v_lo, v_hi, dk_loc, dv_loc, dk_tot, dv_tot):
        hp = pl.program_id(1)
        g = hp // 2

        @pl.when(hp % 2 == 0)
        def _():
            _sw_prepare(k_ref, g, k_lo, k_hi, seq)
            _sw_prepare(v_ref, g, v_lo, v_hi, seq)
            dk_loc[...] = jnp.zeros_like(dk_loc)
            dv_loc[...] = jnp.zeros_like(dv_loc)

        @pl.when(hp == 0)
        def _():
            dk_tot[...] = jnp.zeros_like(dk_tot)
            dv_tot[...] = jnp.zeros_like(dv_tot)

        band = 3 * SW_BLOCK
        low = lax.broadcasted_iota(jnp.int32, (band, LANES), 1) < HEAD_DIM

        sinks = (sink_ref[2 * hp], sink_ref[2 * hp + 1])

        def scores(n):
            rows = pl.ds(pl.multiple_of(n * SW_BLOCK, SW_BLOCK), SW_BLOCK)
            wrows = pl.ds(pl.multiple_of(n * SW_BLOCK, SW_BLOCK), band)
            qb, do = q_ref[rows, :], do_ref[rows, :]
            k2 = jnp.concatenate([k_lo[wrows, :], k_hi[wrows, :]], axis=0)
            v2 = jnp.concatenate([v_lo[wrows, :], v_hi[wrows, :]], axis=0)
            return n, rows, wrows, qb, do, k2, _mm_nt(qb, k2) * QK_SCALE, _mm_nt(do, v2)

        def finish(sink_acc, n, rows, wrows, qb, do, k2, s2, dp):
            p, ps = _sw_probs_from_lse(s2, _sw_mask(n, seq), sinks, lse_ref[rows, :])
            parts, new = [], []
            for i in range(2):
                ph, dph = p[:, i * band:(i + 1) * band], dp[:, i * band:(i + 1) * band]
                delta = jnp.sum(ph * dph, axis=-1, keepdims=True)
                parts.append(ph * (dph - delta))
                new.append(sink_acc[i] - ps[i] * delta)
            dsb = (jnp.concatenate(parts, axis=1) * QK_SCALE).astype(BF16)
            dq_ref[rows, :] = _mm(dsb, k2)
            dk_loc[wrows, :] += _pair_grad(dsb, qb, low)
            dv_loc[wrows, :] += _pair_grad(p.astype(BF16), do, low)
            return tuple(new)

        def block_group(i, carry):
            for state in [scores(SW_GROUP_BLOCKS * i + j) for j in range(SW_GROUP_BLOCKS)]:
                carry = finish(carry, *state)
            return carry

        zero = jnp.zeros((SW_BLOCK, 1), F32)
        s0, s1 = lax.fori_loop(0, n_blocks // SW_GROUP_BLOCKS, block_group, (zero, zero))
        row = lax.broadcasted_iota(jnp.int32, (SUBLANES, LANES), 0)
        dsk_ref[0, 0] = jnp.where(row == 0, jnp.sum(s0), jnp.where(row == 1, jnp.sum(s1), 0.0))

        @pl.when(hp % 2 == 1)
        def _():
            lane_s = lax.broadcasted_iota(jnp.int32, (seq, LANES), 1)
            mine_g = (lane_s // HEAD_DIM) == g
            for loc, tot in ((dk_loc, dk_tot), (dv_loc, dv_tot)):
                part = loc[SW_BLOCK:SW_BLOCK + seq, :]
                tot[...] += jnp.where(mine_g, part + pltpu.roll(part, HEAD_DIM, 1), 0.0)

        @pl.when(hp == n_pairs - 1)
        def _():
            dk_ref[...] = dk_tot[...]
            dv_ref[...] = dv_tot[...].astype(BF16)

    return _hosted(
        body, rider, name="sw_backward", grid=(batch, n_pairs),
        out_shape=[jax.ShapeDtypeStruct((t, SW_WIDTH), F32), jax.ShapeDtypeStruct((t, LANES), F32),
                   jax.ShapeDtypeStruct((t, LANES), BF16), jax.ShapeDtypeStruct((batch, n_pairs, SUBLANES, LANES), F32)],
        in_specs=[pl.BlockSpec(memory_space=pltpu.SMEM),
                  pl.BlockSpec((seq, LANES), lambda b, p: (b, q_blk + p)),
                  pl.BlockSpec((seq, LANES), lambda b, p: (b, k_blk)),
                  pl.BlockSpec((seq, LANES), lambda b, p: (b, k_blk + 1)),
                  pl.BlockSpec((seq, LANES), lambda b, p: (b, p)), pl.BlockSpec((seq, LANES), lambda b, p: (b, p))],
        out_specs=[pl.BlockSpec((seq, LANES), lambda b, p: (b, p)), pl.BlockSpec((seq, LANES), lambda b, p: (b, 0)),
                   pl.BlockSpec((seq, LANES), lambda b, p: (b, 0)),
                   pl.BlockSpec((1, 1, SUBLANES, LANES), lambda b, p: (b, p, 0, 0))],
        scratch_shapes=[pltpu.VMEM((pad, LANES), BF16)] * 4 + [pltpu.VMEM((pad, LANES), F32)] * 2
        + [pltpu.VMEM((seq, LANES), F32)] * 2,
        compiler_params=_cparams(("arbitrary", "arbitrary")), args=[sink, proj, proj, proj, d_o, lse])


def _in_backward(dqkv_a, dq_b, dk_b, dv_b, w_in_t, h1, x, mod3, g_attn, dx1, cos_t, sin_t, seq):
    t, d = x.shape
    tm = TOKEN_TILE
    per_seq = seq // tm
    batch = t // seq
    dqa, dka, dva = dqkv_a
    n_q = SW_WIDTH // LANES

    def body(dqa_ref, dka_ref, dva_ref, dqb_ref, dkb_ref, dvb_ref, w_ref, h_ref, x_ref, mod_ref, g_ref, dx1_ref,
             cos_ref, sin_ref, dx_ref, gw_ref, gwb_ref, dsh_ref, dsc_ref, dg_ref):
        i = pl.program_id(0)

        @pl.when(i == 0)
        def _():
            gw_ref[...] = jnp.zeros_like(gw_ref)
            dg_ref[...] = jnp.zeros_like(dg_ref)

        @pl.when(i % per_seq == 0)
        def _():
            dsh_ref[...] = jnp.zeros_like(dsh_ref)
            dsc_ref[...] = jnp.zeros_like(dsc_ref)

        dr = jnp.concatenate([dqb_ref[...], dkb_ref[...]], axis=1)
        cos = jnp.concatenate([cos_ref[...]] * (n_q + 1), axis=1)
        sin = jnp.concatenate([sin_ref[...]] * (n_q + 1), axis=1)
        dr = dr * cos + _rope_rot(dr * sin)
        dproj = jnp.concatenate([dqa_ref[...], dka_ref[...], dva_ref[...], dr.astype(BF16), dvb_ref[...]], axis=1)
        gw_ref[...] += _mm_tn(dproj, h_ref[...])

        @pl.when(i == t // tm - 1)
        def _():
            gwb_ref[...] = gw_ref[...].astype(BF16)

        dh = _mm(dproj, w_ref[...])
        scale = mod_ref[0, :, d:2 * d]
        r, xn = _rms_stats(x_ref[...])
        xg = xn * g_ref[...]
        dxg = dh * (1.0 + scale)
        dx_ref[...] = dx1_ref[...] + _rms_bwd(dxg * g_ref[...], xn, r)
        dg_ref[...] += jnp.sum(dxg * xn, axis=0, keepdims=True)
        dsh_ref[0] += jnp.sum(dh, axis=0, keepdims=True)
        dsc_ref[0] += jnp.sum(dh * xg, axis=0, keepdims=True)

    tile = lambda w: pl.BlockSpec((tm, w), lambda i: (i, 0))
    per_b = pl.BlockSpec((1, 1, d), lambda i: (i // per_seq, 0, 0))
    small = jax.ShapeDtypeStruct((batch, 1, d), F32)
    rope = pl.BlockSpec((tm, LANES), lambda i: (i % per_seq, 0))
    return pl.pallas_call(
        body, name="in_backward", grid=(t // tm,),
        out_shape=(jax.ShapeDtypeStruct((t, d), F32), jax.ShapeDtypeStruct((IN_WIDTH, d), F32),
                   jax.ShapeDtypeStruct((IN_WIDTH, d), BF16), small, small, jax.ShapeDtypeStruct((1, d), F32)),
        in_specs=[tile(NA_WIDTH), tile(NA_WIDTH), tile(NA_WIDTH), tile(SW_WIDTH), tile(LANES), tile(LANES),
                  _resident((IN_WIDTH, d)), tile(d), tile(d),
                  pl.BlockSpec((1, 1, 6 * d), lambda i: (i // per_seq, 0, 0)),
                  pl.BlockSpec((1, d), lambda i: (0, 0)), tile(d), rope, rope],
        out_specs=(tile(d), _resident((IN_WIDTH, d)), _resident((IN_WIDTH, d)),
                   per_b, per_b, pl.BlockSpec((1, d), lambda i: (0, 0))),
        compiler_params=_cparams(("arbitrary",), VMEM_BIG),
    )(dqa, dka, dva, dq_b, dk_b, dv_b, w_in_t, h1, x, mod3, g_attn, dx1, cos_t, sin_t)


def _row_tile(rows, cols, tile_bytes=1 << 20):
    target = max(SUBLANES, tile_bytes // (4 * cols))
    best = rows
    for cand in range(SUBLANES, rows + 1, SUBLANES):
        if rows % cand == 0 and cand <= target:
            best = cand
    return best if rows % SUBLANES == 0 else rows


def _sum_slots(results, name, rider=None, passenger=None):
    extra_body, extra_in, extra_out = passenger if passenger is not None else (None, [], None)
    n_extra = len(extra_in)
    parts = [group for groups in results for group in groups]
    result_of = [k for k, groups in enumerate(results) for _ in groups]
    n_parts = len(parts)
    cols = [results[k][0][1].shape[1] for k in result_of]
    tr = [_row_tile(min(own.shape[0] for _, own in results[k]), c) for k, c in zip(result_of, cols)]
    assert all(own.shape[0] % r == 0 and own.shape[1] == c for (_, own), r, c in zip(parts, tr, cols))
    tiles = [own.shape[0] // r for (_, own), r in zip(parts, tr)]
    first = [sum(tiles[:q]) for q in range(n_parts)]

    def body(*refs):
        o_refs = refs[2 * n_parts + n_extra:]
        step = pl.program_id(0)
        if passenger is not None:
            pl.when(step == 0)(lambda: extra_body(*refs[2 * n_parts:2 * n_parts + n_extra], o_refs[len(results)]))
        for q in range(n_parts):
            @pl.when((step >= first[q]) & (step < first[q] + tiles[q]))
            def _(q=q):
                p_ref, own_ref = refs[2 * q], refs[2 * q + 1]
                o_refs[result_of[q]][...] = (((own_ref[...] + p_ref[0].astype(F32)) + p_ref[1].astype(F32))
                                             + p_ref[2].astype(F32))

    def tile(start, count):
        return lambda i: jnp.clip(i - start, 0, count - 1)

    in_specs, args = [], []
    for q, (recv, own) in enumerate(parts):
        at = tile(first[q], tiles[q])
        in_specs.append(pl.BlockSpec((N_SHARD - 1, tr[q], cols[q]), lambda i, at=at: (0, at(i), 0)))
        in_specs.append(pl.BlockSpec((tr[q], cols[q]), lambda i, at=at: (at(i), 0)))
        args += [recv, own]
    out_shape, out_specs = [], []
    for k in range(len(results)):
        mine = [q for q in range(n_parts) if result_of[q] == k]
        count = sum(tiles[q] for q in mine)
        at = tile(first[mine[0]], count)
        out_shape.append(jax.ShapeDtypeStruct((count * tr[mine[0]], cols[mine[0]]), F32))
        out_specs.append(pl.BlockSpec((tr[mine[0]], cols[mine[0]]), lambda i, at=at: (at(i), 0)))
    whole = lambda a: pl.BlockSpec(a.shape, lambda i, nd=len(a.shape): (0,) * nd)
    in_specs += [whole(a) for a in extra_in]
    args += extra_in
    if passenger is not None:
        out_shape.append(extra_out)
        out_specs.append(whole(extra_out))
    return _hosted(body, rider, name=name, grid=(sum(tiles),), out_shape=out_shape, in_specs=in_specs,
                   out_specs=out_specs, scratch_shapes=[], compiler_params=_cparams(("arbitrary",), VMEM_BIG), args=args)


def _adamw_math(w, g, m, v):
    m2 = ADAM_B1 * m + (1.0 - ADAM_B1) * g
    v2 = ADAM_B2 * v + (1.0 - ADAM_B2) * (g * g)
    m_hat = m2 / (1.0 - ADAM_B1 ** ADAM_STEP)
    v_hat = v2 / (1.0 - ADAM_B2 ** ADAM_STEP)
    return -ADAM_LR * (m_hat / (jnp.sqrt(v_hat) + ADAM_EPS) + ADAM_WD * w), m2, v2


def _small_sums(partials, dmod, rider=None):
    moving = list(partials) + [dmod]
    n_mov = len(moving)

    def body(*refs):
        mov, refs = refs[:n_mov], refs[n_mov:]
        sums_out, refs = refs[:n_mov - 1], refs[n_mov - 1:]
        b_out, dmod_out, refs = refs[0], refs[1], refs[2:]
        everyone, (ssem, rsem) = refs[:n_mov], refs[n_mov:]
        x, y, c = _my_pos()
        me = 4 * x + 2 * y + c
        cps = []
        for a in range(n_mov):
            everyone[a][me] = mov[a][...]
            for k in range(1, N_DEV):
                peer = (_flip(x, (k >> 2) & 1), _flip(y, (k >> 1) & 1), _flip(c, k & 1))
                cps.append(pltpu.make_async_remote_copy(
                    src_ref=everyone[a].at[me], dst_ref=everyone[a].at[me], send_sem=ssem.at[a, k - 1],
                    recv_sem=rsem.at[a, k - 1], device_id=peer, device_id_type=MESH))
        for cp in cps:
            cp.start()
        for cp in cps:
            cp.wait_recv()

        def total(a):
            acc = everyone[a][0]
            for dev in range(1, N_DEV):
                acc = acc + everyone[a][dev]
            return acc

        for a in range(n_mov - 1):
            sums_out[a][...] = total(a)
        b_out[...] = jnp.sum(total(n_mov - 1), axis=0, keepdims=True)
        dmod_out[...] = everyone[n_mov - 1][...]
        for cp in cps:
            cp.wait_send()

    vm = pl.BlockSpec(memory_space=pltpu.VMEM)
    sds = jax.ShapeDtypeStruct
    out_shape = [sds(p.shape, F32) for p in partials]
    out_shape += [sds((1, dmod.shape[1]), F32), sds((N_DEV,) + dmod.shape, F32)]
    return _hosted(
        body, rider, name="small_sums", grid=(), out_shape=out_shape,
        in_specs=[vm] * n_mov, out_specs=[vm] * len(out_shape),
        scratch_shapes=[pltpu.VMEM((N_DEV,) + a.shape, F32) for a in moving]
        + [pltpu.SemaphoreType.DMA((n_mov, N_DEV - 1)), pltpu.SemaphoreType.DMA((n_mov, N_DEV - 1))],
        compiler_params=_cparams(vmem=VMEM_BIG), args=moving)


def _small_adamw(states, grads):
    n = len(states)

    def body(*refs):
        g_refs, wmv, res = refs[:n], refs[n:4 * n], refs[4 * n:]
        for j in range(n):
            g = g_refs[j][...]
            delta, m2, v2 = _adamw_math(wmv[3 * j][...], g, wmv[3 * j + 1][...], wmv[3 * j + 2][...])
            res[4 * j][...] = g
            res[4 * j + 1][...] = delta
            res[4 * j + 2][...] = m2
            res[4 * j + 3][...] = v2

    out_shape = []
    for w, _, _ in states:
        out_shape += [jax.ShapeDtypeStruct(w.shape, F32)] * 4
    outs = pl.pallas_call(body, name="small_adamw", out_shape=tuple(out_shape),
                          compiler_params=_cparams(vmem=VMEM_BIG))(*grads, *[a for st in states for a in st])
    return [outs[4 * j:4 * j + 4] for j in range(n)]


def _adamw(w, grads, m, v, name, product=None):
    rows, cols = w.shape
    tr = _row_tile(rows, cols, 2 << 20)
    if product is not None:
        assert not grads
        a, b = product
        grads = [a, b]
        g_specs = [pl.BlockSpec((tr, a.shape[1]), lambda i: (i, 0)), pl.BlockSpec(b.shape, lambda i: (0, 0))]
    ng = len(grads)

    def body(*refs):
        w_ref = refs[0]
        g_refs = refs[1:1 + ng]
        m_ref, v_ref = refs[1 + ng], refs[2 + ng]
        g_out, d_out, m_out, v_out = refs[3 + ng:]
        if product is not None:
            g = _mm(g_refs[0][...].astype(BF16), g_refs[1][...].astype(BF16))
        else:
            g = g_refs[0][...]
            for extra in g_refs[1:]:
                g = g + extra[...]
        g_out[...] = g
        d_out[...], m_out[...], v_out[...] = _adamw_math(w_ref[...], g, m_ref[...], v_ref[...])

    spec = pl.BlockSpec((tr, cols), lambda i: (i, 0))
    if product is None:
        g_specs = [spec] * ng
    out = jax.ShapeDtypeStruct((rows, cols), F32)
    return pl.pallas_call(
        body, name=name, grid=(rows // tr,),
        out_shape=(out, out, out, out),
        in_specs=[spec] + g_specs + [spec, spec], out_specs=(spec, spec, spec, spec),
        compiler_params=_cparams(("arbitrary",), VMEM_BIG),
    )(w, *grads, m, v)


def _rope_tables(seq):
    half = HEAD_DIM // 2
    inv = np.float32(ROPE_THETA) ** (-np.arange(half, dtype=np.float32) / np.float32(half))
    ang = (np.arange(seq, dtype=np.float32)[:, None] * inv[None, :]).astype(np.float64)
    cos, sin = np.cos(ang).astype(np.float32), np.sin(ang).astype(np.float32)
    cos_t = np.concatenate([cos, cos, cos, cos], axis=1)
    sin_t = np.concatenate([-sin, sin, -sin, sin], axis=1)
    return jnp.asarray(cos_t), jnp.asarray(sin_t)


def kernel(x, c, w_ada, b_ada, g_attn, w_in, na_rpb, sw_sink, g_na_out, g_sw_out, w_out, g_ffn, w_up, conv_w, conv_b, w_down, g_final, loss_target, m_w_ada, m_b_ada, m_g_attn, m_w_in, m_na_rpb, m_sw_sink, m_g_na_out, m_g_sw_out, m_w_out, m_g_ffn, m_w_up, m_conv_w, m_conv_b, m_w_down, m_g_final, v_w_ada, v_b_ada, v_g_attn, v_w_in, v_na_rpb, v_sw_sink, v_g_na_out, v_g_sw_out, v_w_out, v_g_ffn, v_w_up, v_conv_w, v_conv_b, v_w_down, v_g_final):
    batch, seq, d = x.shape
    t = batch * seq
    assert d == D_MODEL and seq % (NA_ROWS * GRID_W) == 0 and seq % TOKEN_TILE == 0 and batch <= SUBLANES
    shard = 2 * lax.axis_index("x") + lax.axis_index("y")
    xt = x.reshape(t, d)
    tgt = loss_target.reshape(t, d)

    c8 = jnp.pad(c, ((0, SUBLANES - batch), (0, 0)))
    w_in_t_s = jnp.transpose(w_in[0]).astype(BF16)
    n_heads = NA_WIDTH // HEAD_DIM
    n_tiles, n_dc = 2 * NA_ROWS - 2, 2 * NA_COLS - 1
    expand, neg_mask = _na_bias_pattern()
    rpb = na_rpb[0]
    rows2 = jnp.concatenate([rpb[:, :-1, :], rpb[:, 1:, :]], axis=2).reshape(n_heads * n_tiles, 2 * n_dc)
    rows2 = jnp.pad(rows2, ((0, 0), (0, GRID_W - 2 * n_dc)))
    (mod8, sc_all, tiles), (w_in_g,) = _ada_forward(
        c8, w_ada[0], b_ada, _Rider("gather", [w_in_t_s]), _na_bias_tiles(rows2, expand, neg_mask))
    tiles = tiles.reshape(n_heads, n_tiles, GRID_W, LANES)
    mod3 = mod8[:batch].reshape(batch, 1, 6 * d)
    w_in_t = w_in_g.reshape(IN_WIDTH, d)

    cos_t, sin_t = _rope_tables(seq)
    (h1, proj), _ = _in_proj(xt, mod3, g_attn, w_in_t, cos_t, sin_t, seq)
    sink = sw_sink[0]
    w_up_b16 = w_up[0].astype(BF16)
    (oa, lse_a), (w_up_a,) = _na_forward(proj, tiles, batch, seq, _Rider("gather", [w_up_b16[:d // 2]]))
    (ob, lse_b), (w_up_b, conv_w_g, w_out_g) = _sw_forward(
        proj, sink, batch, seq, _Rider("gather", [w_up_b16[d // 2:], conv_w[0], w_out[0].astype(BF16)]))
    w_up_f = (w_up_a, w_up_b)
    w_out_f = w_out_g.reshape(d, d)
    conv_w_f = jnp.transpose(conv_w_g, (1, 0, 2)).reshape(3, D_FF)
    oab, mix, x1, h2 = _out_proj(oa, ob, g_na_out, g_sw_out, w_out_f, xt, mod3, g_ffn, seq)
    (u,), _ = _up_proj(h2, w_up_f)
    (a,), (w_down_g,) = _conv_gate(u, conv_w_f, conv_b, batch, seq, _Rider("gather", [w_down[0].astype(BF16)]))
    w_down_f = w_down_g.reshape(D_FF, d)
    dx2, dffn, loss_part, dgate_f, dg_final = _down_and_loss(a, w_down_f, x1, mod3, g_final.reshape(1, d), tgt, seq)

    gw_down, gw_down_b = _down_weight_grad(a, dffn)
    blocks = lambda g, rows: g.reshape(N_SHARD, rows // N_SHARD, d)
    (du, gconv_w, gconv_b), (recv_down, own_down) = _ffn_backward(
        dffn, w_down_f, u, conv_w_f, conv_b, batch, seq,
        _Rider("scatter", [blocks(gw_down_b, D_FF)], [blocks(gw_down, D_FF)]))
    (gw_up_top, gw_up_bot, gw_up_top_b, gw_up_bot_b), _ = _up_weight_grad(h2, du)
    (dx1, dmix, dshift_f, dscale_f, dgate_a, dg_ffn), _ = _up_backward(du, w_up_f, x1, mod3, g_ffn, dx2, mix, seq)
    doa, dob, gw_out, gw_out_b, dg_na, dg_sw = _out_backward(dmix, w_out_f, oab, oa, ob, g_na_out, g_sw_out)
    (dqa, dka, dva, dtiles), (recv_out, recv_up_bot, own_out, own_up_bot) = _na_backward(
        proj, doa, lse_a, tiles, batch, seq,
        _Rider("scatter", [blocks(gw_out_b, d), gw_up_bot_b], [blocks(gw_out, d), gw_up_bot]))
    (dq_b, dk_b, dv_b, dsink_parts), (recv_up_top, own_up_top) = _sw_backward(
        proj, dob, lse_b, sink, batch, seq, _Rider("scatter", [gw_up_top_b], [gw_up_top]))
    gx, gw_in_t, gw_in_b, dshift_a, dscale_a, dg_attn = _in_backward(
        (dqa, dka, dva), dq_b, dk_b, dv_b, w_in_t, h1, xt, mod3, g_attn, dx1, cos_t, sin_t, seq)

    late, (recv_in, own_in) = _sum_slots(
        [[(recv_out, own_out)], [(recv_up_top, own_up_top), (recv_up_bot, own_up_bot)], [(recv_down, own_down)]],
        "sum_w_out_up_down", _Rider("scatter", [blocks(gw_in_b, IN_WIDTH)], [blocks(gw_in_t, IN_WIDTH)]),
        _na_bias_grad(dtiles.reshape(n_heads * n_tiles, GRID_W, LANES), expand))

    red = late.pop()[:, :2 * n_dc]
    red = red.reshape(n_heads, n_tiles, 2, n_dc)
    zero_row = jnp.zeros((n_heads, 1, n_dc), F32)
    g_rpb = (jnp.concatenate([red[:, :, 0, :], zero_row], axis=1)
             + jnp.concatenate([zero_row, red[:, :, 1, :]], axis=1))
    g_sink = jnp.sum(dsink_parts[:, :, :2, 0], axis=0).reshape(SW_WIDTH // HEAD_DIM)

    dmod = jnp.concatenate([dshift_a, dscale_a, dgate_a, dshift_f, dscale_f, dgate_f], axis=2).reshape(batch, 6 * d)
    rpb_shape = na_rpb.shape[1:]
    states = [(g_attn, m_g_attn, v_g_attn),
              (na_rpb.reshape(rpb_shape), m_na_rpb.reshape(rpb_shape), v_na_rpb.reshape(rpb_shape)),
              (sw_sink, m_sw_sink, v_sw_sink), (g_na_out, m_g_na_out, v_g_na_out), (g_sw_out, m_g_sw_out, v_g_sw_out),
              (g_ffn, m_g_ffn, v_g_ffn), (conv_b, m_conv_b, v_conv_b),
              (g_final.reshape(1, d), m_g_final.reshape(1, d), v_g_final.reshape(1, d))]
    partials = [dg_attn, g_rpb, g_sink.reshape(sw_sink.shape), dg_na, dg_sw, dg_ffn, gconv_b, dg_final,
                gconv_w, loss_part]
    mine = _sum_slots([[(recv_in, own_in)]], "sum_w_in")[0] + late
    small, theirs = _small_sums(partials, dmod, _Rider("swap", mine))
    g_conv_w_full, loss_sum, g_b_ada, dmod_all = small[len(states):]
    r_small = _small_adamw(states + [(b_ada, m_b_ada, v_b_ada)], small[:len(states)] + [g_b_ada])
    loss = loss_sum[0, 0]
    dmod_rows = jnp.pad(dmod_all, ((0, 0), (0, SUBLANES - batch), (0, 0))).reshape(N_DEV * SUBLANES, 6 * d)
    ncol = w_ada.shape[2]
    dmod_cols = lax.dynamic_slice(dmod_rows, (0, shard * ncol), (N_DEV * SUBLANES, ncol))
    cshard = conv_w.shape[2]
    g_conv_w = lax.dynamic_slice(g_conv_w_full, (0, shard * cshard), (3, cshard))

    def big(w, m, v, g_parts, name):
        shape = w.shape
        outs = _adamw(w[0], g_parts, m[0], v[0], name)
        return [o.reshape(shape) for o in outs]

    r_w_ada = [o.reshape(w_ada.shape) for o in
               _adamw(w_ada[0], [], m_w_ada[0], v_w_ada[0], "adamw_w_ada", product=(jnp.transpose(sc_all), dmod_cols))]
    r_w_in = [jnp.transpose(o).reshape(w_in.shape) for o in
              _adamw(jnp.transpose(w_in[0]), [mine[0], theirs[0]], jnp.transpose(m_w_in[0]), jnp.transpose(v_w_in[0]),
                     "adamw_w_in")]
    r_w_out = big(w_out, m_w_out, v_w_out, [mine[1], theirs[1]], "adamw_w_out")
    r_w_up = big(w_up, m_w_up, v_w_up, [mine[2], theirs[2]], "adamw_w_up")
    r_w_down = big(w_down, m_w_down, v_w_down, [mine[3], theirs[3]], "adamw_w_down")

    r_conv_w = big(conv_w, m_conv_w, v_conv_w, [g_conv_w], "adamw_conv_w")

    def pick(k):
        ga_, rpb_, sk_, gna_, gsw_, gf_, cb_, gfin_, b_ = [r[k] for r in r_small]
        return [r_w_ada[k], b_, ga_, r_w_in[k], rpb_.reshape(na_rpb.shape), sk_, gna_, gsw_, r_w_out[k], gf_,
                r_w_up[k], r_conv_w[k], cb_, r_w_down[k], gfin_.reshape(d)]

    return (loss, gx.reshape(batch, seq, d), *pick(0), *pick(1), *pick(2), *pick(3))
```
